```python
import math
import jax, jax.numpy as jnp
from jax import lax
import numpy as np

D_MODEL = 1024
BATCH = 8
SEQ = 8192
DEPTH = 4

HEAD_DIM = 64
MIX_WIDTH = D_MODEL
A_WIDTH = MIX_WIDTH // 4
A_HEADS = A_WIDTH // HEAD_DIM
CHUNK = 128
B_WIDTH = MIX_WIDTH // 2
B_Q_HEADS = B_WIDTH // HEAD_DIM
B_KV_HEADS = 2
B_GROUP = B_Q_HEADS // B_KV_HEADS
WINDOW = 128
ROPE_THETA = 10000.0
C_WIDTH = MIX_WIDTH // 4
C_GROUP = 16
C_GROUPS = C_WIDTH // C_GROUP
C_STATE = 64
DT_MIN = 0.001
DT_MAX = 0.1
IN_A = 2 * A_WIDTH
IN_Q = B_WIDTH
IN_KV = B_KV_HEADS * HEAD_DIM
IN_C = C_WIDTH
IN_COLS = IN_A + IN_Q + 2 * IN_KV + IN_C
D_FF = 4 * D_MODEL
PLE_DIM = 256
EPS = 1e-6

kernel_name = "hybrid_gmlp_swa_s5_trunk"


def rmsnorm(x, g):
    xf = x.astype(jnp.float32)
    y = xf * lax.rsqrt(jnp.mean(xf * xf, axis=-1, keepdims=True) + EPS)
    return (y * g.astype(jnp.float32)).astype(x.dtype)


def layernorm(x, g, b):
    xf = x.astype(jnp.float32)
    mu = jnp.mean(xf, axis=-1, keepdims=True)
    xc = xf - mu
    y = xc * lax.rsqrt(jnp.mean(xc * xc, axis=-1, keepdims=True) + EPS)
    return (y * g.astype(jnp.float32) + b.astype(jnp.float32)).astype(x.dtype)


def rope_tables(positions):
    inv = 1.0 / (ROPE_THETA ** (jnp.arange(0, HEAD_DIM, 2, dtype=jnp.float32) / HEAD_DIM))
    ang = positions.astype(jnp.float32)[..., None] * inv
    return jnp.cos(ang), jnp.sin(ang)


def apply_rope(x, cos, sin):
    xf = x.astype(jnp.float32)
    x1, x2 = jnp.split(xf, 2, axis=-1)
    c = cos[:, :, None, :]
    s = sin[:, :, None, :]
    return jnp.concatenate([x1 * c - x2 * s, x2 * c + x1 * s], axis=-1).astype(x.dtype)


def chunk_gmlp(z, ln_g, ln_b, w_s, b_s):
    bsz, L, _ = z.shape
    z = jax.nn.gelu(z).reshape(bsz, L // CHUNK, CHUNK, A_HEADS, 2 * HEAD_DIM)
    u, v = jnp.split(z, 2, axis=-1)
    v = layernorm(v, ln_g, ln_b)
    causal = jnp.tril(jnp.ones((CHUNK, CHUNK), dtype=bool))
    w = jnp.where(causal, w_s, 0.0).astype(v.dtype)
    sv = jnp.einsum('hts,bnshd->bnthd', w, v) + b_s.T[None, None, :, :, None].astype(v.dtype)
    return (u * sv).reshape(bsz, L, A_WIDTH)


def swa_sink_attention(q, k, v, sinks):
    bsz, L = q.shape[:2]
    nb = L // WINDOW
    qb = q.reshape(bsz, nb, WINDOW, B_KV_HEADS, B_GROUP, HEAD_DIM)

    def band(t):
        t = t.reshape(bsz, nb, WINDOW, B_KV_HEADS, HEAD_DIM)
        prev = jnp.pad(t[:, :-1], ((0, 0), (1, 0), (0, 0), (0, 0), (0, 0)))
        return jnp.concatenate([prev, t], axis=2)

    kb, vb = band(k), band(v)
    s = jnp.einsum('bnqhgd,bnkhd->bnhgqk', qb, kb,
                   preferred_element_type=jnp.float32) * (HEAD_DIM ** -0.5)
    qi = jnp.arange(WINDOW)[:, None] + WINDOW
    kj = jnp.arange(2 * WINDOW)[None, :]
    diff = qi - kj
    band_ok = (diff >= 0) & (diff < WINDOW)
    not_first = jnp.arange(nb)[:, None, None] > 0
    mask = band_ok[None] & (not_first | (kj >= WINDOW)[None])
    s = jnp.where(mask[None, :, None, None], s, -jnp.inf)
    sink = sinks.astype(jnp.float32).reshape(B_KV_HEADS, B_GROUP)[None, None, :, :, None, None]
    m = jnp.maximum(jnp.max(s, axis=-1, keepdims=True), sink)
    pr = jnp.exp(s - m)
    denom = jnp.sum(pr, axis=-1, keepdims=True) + jnp.exp(sink - m)
    o = jnp.einsum('bnhgqk,bnkhd->bnqhgd', (pr / denom).astype(v.dtype), vb)
    return o.reshape(bsz, L, B_WIDTH)


def s5_ssm(u, a_re, a_im, log_dt, b_re, b_im, c_re, c_im, d_skip, glu_w1, glu_w2):
    bsz, L, _ = u.shape
    uf = u.astype(jnp.float32).reshape(bsz, L, C_GROUPS, C_GROUP)
    lam = lax.complex(a_re.astype(jnp.float32), a_im.astype(jnp.float32))
    dt = jnp.exp(log_dt.astype(jnp.float32))[:, None]
    lam_bar = jnp.exp(lam * dt)
    bmat = lax.complex(b_re.astype(jnp.float32), b_im.astype(jnp.float32))
    b_bar = ((lam_bar - 1.0) / lam)[..., None] * bmat
    bu = jnp.einsum('gph,blgh->blgp', b_bar, uf.astype(jnp.complex64))
    a_seq = jnp.broadcast_to(lam_bar, bu.shape)

    def combine(e1, e2):
        a1, x1 = e1
        a2, x2 = e2
        return a1 * a2, a2 * x1 + x2

    _, states = lax.associative_scan(combine, (a_seq, bu), axis=1)
    cmat = lax.complex(c_re.astype(jnp.float32), c_im.astype(jnp.float32))
    y = jnp.real(jnp.einsum('ghp,blgp->blgh', cmat, states)) + d_skip.astype(jnp.float32) * uf
    y = jax.nn.gelu(y.reshape(bsz, L, C_WIDTH)).astype(u.dtype)
    return (y @ glu_w1) * jax.nn.sigmoid(y @ glu_w2)


def hybrid_layer(h, p_i, cos, sin, attn_norm_g, w_in, gmlp_ln_g, gmlp_ln_b, gmlp_ws, gmlp_bs,
                 q_norm_g, k_norm_g, sinks, ssm_a_re, ssm_a_im, ssm_log_dt, ssm_b_re, ssm_b_im,
                 ssm_c_re, ssm_c_im, ssm_d, glu_w1, glu_w2, mix_out_g, w_out,
                 mlp_norm_g, w_ff1, w_ff2, ple_norm_g, w_ple_gate, w_ple_proj):
    bsz, L, _ = h.shape
    xn = rmsnorm(h, attn_norm_g)
    z = xn @ w_in
    za, zq, zk, zv, zc = jnp.split(
        z, [IN_A, IN_A + IN_Q, IN_A + IN_Q + IN_KV, IN_A + IN_Q + 2 * IN_KV], axis=-1)
    ya = chunk_gmlp(za, gmlp_ln_g, gmlp_ln_b, gmlp_ws, gmlp_bs)
    q = zq.reshape(bsz, L, B_Q_HEADS, HEAD_DIM)
    k = zk.reshape(bsz, L, B_KV_HEADS, HEAD_DIM)
    v = zv.reshape(bsz, L, B_KV_HEADS, HEAD_DIM)
    q = apply_rope(rmsnorm(q, q_norm_g), cos, sin)
    k = apply_rope(rmsnorm(k, k_norm_g), cos, sin)
    yb = swa_sink_attention(q, k, v, sinks)
    yc = s5_ssm(zc, ssm_a_re, ssm_a_im, ssm_log_dt, ssm_b_re, ssm_b_im,
                ssm_c_re, ssm_c_im, ssm_d, glu_w1, glu_w2)
    y = jnp.concatenate([
        rmsnorm(ya, mix_out_g[:A_WIDTH]),
        rmsnorm(yb, mix_out_g[A_WIDTH:A_WIDTH + B_WIDTH]),
        rmsnorm(yc, mix_out_g[A_WIDTH + B_WIDTH:]),
    ], axis=-1)
    h = h + y @ w_out
    hn = rmsnorm(h, mlp_norm_g)
    h = h + jnp.square(jax.nn.relu(hn @ w_ff1)) @ w_ff2
    gate = jax.nn.sigmoid(rmsnorm(h, ple_norm_g) @ w_ple_gate)
    return h + gate * (p_i @ w_ple_proj)


def _fwd_setup_inputs(seed: int = 0) -> dict:
    key = jax.random.key(seed)
    ks = iter(jax.random.split(key, 40))
    f32 = jnp.float32

    def nrm(shape, scale):
        return jax.random.normal(next(ks), shape, f32) * scale

    def gain(shape):
        return 1.0 + nrm(shape, 0.02)

    x = nrm((BATCH, SEQ, D_MODEL), 1.0)
    p = nrm((DEPTH, BATCH, SEQ, PLE_DIM), 1.0)
    offsets = jax.random.randint(next(ks), (BATCH, 1), 0, 1024, dtype=jnp.int32)
    positions = offsets + jnp.arange(SEQ, dtype=jnp.int32)[None, :]

    n_idx = jnp.arange(C_STATE, dtype=f32)
    ssm_a_re = -0.5 + nrm((DEPTH, C_GROUPS, C_STATE), 0.01)
    ssm_a_im = math.pi * n_idx[None, None, :] + nrm((DEPTH, C_GROUPS, C_STATE), 0.01)
    ssm_log_dt = jax.random.uniform(next(ks), (DEPTH, C_GROUPS), f32,
                                    math.log(DT_MIN), math.log(DT_MAX))
    b_scale = (2.0 * C_GROUP) ** -0.5
    c_scale = (2.0 * C_STATE) ** -0.5

    return {
        "x": x,
        "p": p,
        "positions": positions,
        "attn_norm_g": gain((DEPTH, D_MODEL)),
        "w_in": nrm((DEPTH, D_MODEL, IN_COLS), D_MODEL ** -0.5),
        "gmlp_ln_g": gain((DEPTH, A_HEADS, HEAD_DIM)),
        "gmlp_ln_b": nrm((DEPTH, A_HEADS, HEAD_DIM), 0.02),
        "gmlp_ws": nrm((DEPTH, A_HEADS, CHUNK, CHUNK), 0.5 * CHUNK ** -0.5),
        "gmlp_bs": gain((DEPTH, A_HEADS, CHUNK)),
        "q_norm_g": gain((DEPTH, HEAD_DIM)),
        "k_norm_g": gain((DEPTH, HEAD_DIM)),
        "sinks": nrm((DEPTH, B_Q_HEADS), 0.5),
        "ssm_a_re": ssm_a_re,
        "ssm_a_im": ssm_a_im,
        "ssm_log_dt": ssm_log_dt,
        "ssm_b_re": nrm((DEPTH, C_GROUPS, C_STATE, C_GROUP), b_scale),
        "ssm_b_im": nrm((DEPTH, C_GROUPS, C_STATE, C_GROUP), b_scale),
        "ssm_c_re": nrm((DEPTH, C_GROUPS, C_GROUP, C_STATE), c_scale),
        "ssm_c_im": nrm((DEPTH, C_GROUPS, C_GROUP, C_STATE), c_scale),
        "ssm_d": nrm((DEPTH, C_GROUPS, C_GROUP), 0.5),
        "glu_w1": nrm((DEPTH, C_WIDTH, C_WIDTH), C_WIDTH ** -0.5),
        "glu_w2": nrm((DEPTH, C_WIDTH, C_WIDTH), C_WIDTH ** -0.5),
        "mix_out_g": gain((DEPTH, MIX_WIDTH)),
        "w_out": nrm((DEPTH, MIX_WIDTH, D_MODEL), MIX_WIDTH ** -0.5),
        "mlp_norm_g": gain((DEPTH, D_MODEL)),
        "w_ff1": nrm((DEPTH, D_MODEL, D_FF), D_MODEL ** -0.5),
        "w_ff2": nrm((DEPTH, D_FF, D_MODEL), D_FF ** -0.5),
        "ple_norm_g": gain((DEPTH, D_MODEL)),
        "w_ple_gate": nrm((DEPTH, D_MODEL, D_MODEL), D_MODEL ** -0.5),
        "w_ple_proj": nrm((DEPTH, PLE_DIM, D_MODEL), 0.5 * PLE_DIM ** -0.5),
    }


def _fwd_reference(x, p, positions, attn_norm_g, w_in, gmlp_ln_g, gmlp_ln_b, gmlp_ws, gmlp_bs,
              q_norm_g, k_norm_g, sinks, ssm_a_re, ssm_a_im, ssm_log_dt, ssm_b_re, ssm_b_im,
              ssm_c_re, ssm_c_im, ssm_d, glu_w1, glu_w2, mix_out_g, w_out,
              mlp_norm_g, w_ff1, w_ff2, ple_norm_g, w_ple_gate, w_ple_proj):
    cos, sin = rope_tables(positions)
    h = x
    for i in range(DEPTH):
        h = hybrid_layer(h, p[i], cos, sin, attn_norm_g[i], w_in[i], gmlp_ln_g[i], gmlp_ln_b[i],
                         gmlp_ws[i], gmlp_bs[i], q_norm_g[i], k_norm_g[i], sinks[i],
                         ssm_a_re[i], ssm_a_im[i], ssm_log_dt[i], ssm_b_re[i], ssm_b_im[i],
                         ssm_c_re[i], ssm_c_im[i], ssm_d[i], glu_w1[i], glu_w2[i],
                         mix_out_g[i], w_out[i], mlp_norm_g[i], w_ff1[i], w_ff2[i],
                         ple_norm_g[i], w_ple_gate[i], w_ple_proj[i])
    return h


import jax as _jax
import jax.numpy as _jnp

TWIN_FORMAT = 'train_step'
FWD_PARAMS = ['x', 'p', 'positions', 'attn_norm_g', 'w_in', 'gmlp_ln_g', 'gmlp_ln_b', 'gmlp_ws', 'gmlp_bs', 'q_norm_g', 'k_norm_g', 'sinks', 'ssm_a_re', 'ssm_a_im', 'ssm_log_dt', 'ssm_b_re', 'ssm_b_im', 'ssm_c_re', 'ssm_c_im', 'ssm_d', 'glu_w1', 'glu_w2', 'mix_out_g', 'w_out', 'mlp_norm_g', 'w_ff1', 'w_ff2', 'ple_norm_g', 'w_ple_gate', 'w_ple_proj']
TWIN_WEIGHTS = ['attn_norm_g', 'w_in', 'gmlp_ln_g', 'gmlp_ln_b', 'gmlp_ws', 'gmlp_bs', 'q_norm_g', 'k_norm_g', 'sinks', 'ssm_a_re', 'ssm_a_im', 'ssm_log_dt', 'ssm_b_re', 'ssm_b_im', 'ssm_c_re', 'ssm_c_im', 'ssm_d', 'glu_w1', 'glu_w2', 'mix_out_g', 'w_out', 'mlp_norm_g', 'w_ff1', 'w_ff2', 'ple_norm_g', 'w_ple_gate', 'w_ple_proj']
TWIN_DIFF_INPUT = 'x'
TWIN_INPUTS = ['x', 'p', 'positions', 'attn_norm_g', 'w_in', 'gmlp_ln_g', 'gmlp_ln_b', 'gmlp_ws', 'gmlp_bs', 'q_norm_g', 'k_norm_g', 'sinks', 'ssm_a_re', 'ssm_a_im', 'ssm_log_dt', 'ssm_b_re', 'ssm_b_im', 'ssm_c_re', 'ssm_c_im', 'ssm_d', 'glu_w1', 'glu_w2', 'mix_out_g', 'w_out', 'mlp_norm_g', 'w_ff1', 'w_ff2', 'ple_norm_g', 'w_ple_gate', 'w_ple_proj', 'loss_target', 'm_attn_norm_g', 'm_w_in', 'm_gmlp_ln_g', 'm_gmlp_ln_b', 'm_gmlp_ws', 'm_gmlp_bs', 'm_q_norm_g', 'm_k_norm_g', 'm_sinks', 'm_ssm_a_re', 'm_ssm_a_im', 'm_ssm_log_dt', 'm_ssm_b_re', 'm_ssm_b_im', 'm_ssm_c_re', 'm_ssm_c_im', 'm_ssm_d', 'm_glu_w1', 'm_glu_w2', 'm_mix_out_g', 'm_w_out', 'm_mlp_norm_g', 'm_w_ff1', 'm_w_ff2', 'm_ple_norm_g', 'm_w_ple_gate', 'm_w_ple_proj', 'v_attn_norm_g', 'v_w_in', 'v_gmlp_ln_g', 'v_gmlp_ln_b', 'v_gmlp_ws', 'v_gmlp_bs', 'v_q_norm_g', 'v_k_norm_g', 'v_sinks', 'v_ssm_a_re', 'v_ssm_a_im', 'v_ssm_log_dt', 'v_ssm_b_re', 'v_ssm_b_im', 'v_ssm_c_re', 'v_ssm_c_im', 'v_ssm_d', 'v_glu_w1', 'v_glu_w2', 'v_mix_out_g', 'v_w_out', 'v_mlp_norm_g', 'v_w_ff1', 'v_w_ff2', 'v_ple_norm_g', 'v_w_ple_gate', 'v_w_ple_proj']
TWIN_OUTPUTS = ['loss', 'grad_x', 'grad_attn_norm_g', 'grad_w_in', 'grad_gmlp_ln_g', 'grad_gmlp_ln_b', 'grad_gmlp_ws', 'grad_gmlp_bs', 'grad_q_norm_g', 'grad_k_norm_g', 'grad_sinks', 'grad_ssm_a_re', 'grad_ssm_a_im', 'grad_ssm_log_dt', 'grad_ssm_b_re', 'grad_ssm_b_im', 'grad_ssm_c_re', 'grad_ssm_c_im', 'grad_ssm_d', 'grad_glu_w1', 'grad_glu_w2', 'grad_mix_out_g', 'grad_w_out', 'grad_mlp_norm_g', 'grad_w_ff1', 'grad_w_ff2', 'grad_ple_norm_g', 'grad_w_ple_gate', 'grad_w_ple_proj', 'delta_attn_norm_g', 'delta_w_in', 'delta_gmlp_ln_g', 'delta_gmlp_ln_b', 'delta_gmlp_ws', 'delta_gmlp_bs', 'delta_q_norm_g', 'delta_k_norm_g', 'delta_sinks', 'delta_ssm_a_re', 'delta_ssm_a_im', 'delta_ssm_log_dt', 'delta_ssm_b_re', 'delta_ssm_b_im', 'delta_ssm_c_re', 'delta_ssm_c_im', 'delta_ssm_d', 'delta_glu_w1', 'delta_glu_w2', 'delta_mix_out_g', 'delta_w_out', 'delta_mlp_norm_g', 'delta_w_ff1', 'delta_w_ff2', 'delta_ple_norm_g', 'delta_w_ple_gate', 'delta_w_ple_proj', 'new_m_attn_norm_g', 'new_m_w_in', 'new_m_gmlp_ln_g', 'new_m_gmlp_ln_b', 'new_m_gmlp_ws', 'new_m_gmlp_bs', 'new_m_q_norm_g', 'new_m_k_norm_g', 'new_m_sinks', 'new_m_ssm_a_re', 'new_m_ssm_a_im', 'new_m_ssm_log_dt', 'new_m_ssm_b_re', 'new_m_ssm_b_im', 'new_m_ssm_c_re', 'new_m_ssm_c_im', 'new_m_ssm_d', 'new_m_glu_w1', 'new_m_glu_w2', 'new_m_mix_out_g', 'new_m_w_out', 'new_m_mlp_norm_g', 'new_m_w_ff1', 'new_m_w_ff2', 'new_m_ple_norm_g', 'new_m_w_ple_gate', 'new_m_w_ple_proj', 'new_v_attn_norm_g', 'new_v_w_in', 'new_v_gmlp_ln_g', 'new_v_gmlp_ln_b', 'new_v_gmlp_ws', 'new_v_gmlp_bs', 'new_v_q_norm_g', 'new_v_k_norm_g', 'new_v_sinks', 'new_v_ssm_a_re', 'new_v_ssm_a_im', 'new_v_ssm_log_dt', 'new_v_ssm_b_re', 'new_v_ssm_b_im', 'new_v_ssm_c_re', 'new_v_ssm_c_im', 'new_v_ssm_d', 'new_v_glu_w1', 'new_v_glu_w2', 'new_v_mix_out_g', 'new_v_w_out', 'new_v_mlp_norm_g', 'new_v_w_ff1', 'new_v_w_ff2', 'new_v_ple_norm_g', 'new_v_w_ple_gate', 'new_v_w_ple_proj']
TWIN_LEAF_KINDS = {'loss': 'loss', 'grad_x': 'grad_x', 'grad_attn_norm_g': 'grad_w', 'grad_w_in': 'grad_w', 'grad_gmlp_ln_g': 'grad_w', 'grad_gmlp_ln_b': 'grad_w', 'grad_gmlp_ws': 'grad_w', 'grad_gmlp_bs': 'grad_w', 'grad_q_norm_g': 'grad_w', 'grad_k_norm_g': 'grad_w', 'grad_sinks': 'grad_w', 'grad_ssm_a_re': 'grad_w', 'grad_ssm_a_im': 'grad_w', 'grad_ssm_log_dt': 'grad_w', 'grad_ssm_b_re': 'grad_w', 'grad_ssm_b_im': 'grad_w', 'grad_ssm_c_re': 'grad_w', 'grad_ssm_c_im': 'grad_w', 'grad_ssm_d': 'grad_w', 'grad_glu_w1': 'grad_w', 'grad_glu_w2': 'grad_w', 'grad_mix_out_g': 'grad_w', 'grad_w_out': 'grad_w', 'grad_mlp_norm_g': 'grad_w', 'grad_w_ff1': 'grad_w', 'grad_w_ff2': 'grad_w', 'grad_ple_norm_g': 'grad_w', 'grad_w_ple_gate': 'grad_w', 'grad_w_ple_proj': 'grad_w', 'delta_attn_norm_g': 'delta_w', 'delta_w_in': 'delta_w', 'delta_gmlp_ln_g': 'delta_w', 'delta_gmlp_ln_b': 'delta_w', 'delta_gmlp_ws': 'delta_w', 'delta_gmlp_bs': 'delta_w', 'delta_q_norm_g': 'delta_w', 'delta_k_norm_g': 'delta_w', 'delta_sinks': 'delta_w', 'delta_ssm_a_re': 'delta_w', 'delta_ssm_a_im': 'delta_w', 'delta_ssm_log_dt': 'delta_w', 'delta_ssm_b_re': 'delta_w', 'delta_ssm_b_im': 'delta_w', 'delta_ssm_c_re': 'delta_w', 'delta_ssm_c_im': 'delta_w', 'delta_ssm_d': 'delta_w', 'delta_glu_w1': 'delta_w', 'delta_glu_w2': 'delta_w', 'delta_mix_out_g': 'delta_w', 'delta_w_out': 'delta_w', 'delta_mlp_norm_g': 'delta_w', 'delta_w_ff1': 'delta_w', 'delta_w_ff2': 'delta_w', 'delta_ple_norm_g': 'delta_w', 'delta_w_ple_gate': 'delta_w', 'delta_w_ple_proj': 'delta_w', 'new_m_attn_norm_g': 'new_m', 'new_m_w_in': 'new_m', 'new_m_gmlp_ln_g': 'new_m', 'new_m_gmlp_ln_b': 'new_m', 'new_m_gmlp_ws': 'new_m', 'new_m_gmlp_bs': 'new_m', 'new_m_q_norm_g': 'new_m', 'new_m_k_norm_g': 'new_m', 'new_m_sinks': 'new_m', 'new_m_ssm_a_re': 'new_m', 'new_m_ssm_a_im': 'new_m', 'new_m_ssm_log_dt': 'new_m', 'new_m_ssm_b_re': 'new_m', 'new_m_ssm_b_im': 'new_m', 'new_m_ssm_c_re': 'new_m', 'new_m_ssm_c_im': 'new_m', 'new_m_ssm_d': 'new_m', 'new_m_glu_w1': 'new_m', 'new_m_glu_w2': 'new_m', 'new_m_mix_out_g': 'new_m', 'new_m_w_out': 'new_m', 'new_m_mlp_norm_g': 'new_m', 'new_m_w_ff1': 'new_m', 'new_m_w_ff2': 'new_m', 'new_m_ple_norm_g': 'new_m', 'new_m_w_ple_gate': 'new_m', 'new_m_w_ple_proj': 'new_m', 'new_v_attn_norm_g': 'new_v', 'new_v_w_in': 'new_v', 'new_v_gmlp_ln_g': 'new_v', 'new_v_gmlp_ln_b': 'new_v', 'new_v_gmlp_ws': 'new_v', 'new_v_gmlp_bs': 'new_v', 'new_v_q_norm_g': 'new_v', 'new_v_k_norm_g': 'new_v', 'new_v_sinks': 'new_v', 'new_v_ssm_a_re': 'new_v', 'new_v_ssm_a_im': 'new_v', 'new_v_ssm_log_dt': 'new_v', 'new_v_ssm_b_re': 'new_v', 'new_v_ssm_b_im': 'new_v', 'new_v_ssm_c_re': 'new_v', 'new_v_ssm_c_im': 'new_v', 'new_v_ssm_d': 'new_v', 'new_v_glu_w1': 'new_v', 'new_v_glu_w2': 'new_v', 'new_v_mix_out_g': 'new_v', 'new_v_w_out': 'new_v', 'new_v_mlp_norm_g': 'new_v', 'new_v_w_ff1': 'new_v', 'new_v_w_ff2': 'new_v', 'new_v_ple_norm_g': 'new_v', 'new_v_w_ple_gate': 'new_v', 'new_v_w_ple_proj': 'new_v'}


def _forward(args):
    return _fwd_reference(*[args[k] for k in FWD_PARAMS])


def _output_shape():
    def fwd():
        inp = _fwd_setup_inputs(0)
        return _fwd_reference(*[inp[k] for k in FWD_PARAMS])
    out = _jax.eval_shape(fwd)
    return out.shape, out.dtype

N_MICROBATCH = 1
ADAM_LR = 0.001
ADAM_B1 = 0.9
ADAM_B2 = 0.999
ADAM_EPS = 1e-08
ADAM_WD = 0.01
ADAM_STEP = 10
PER_EXAMPLE_BATCH_AXIS = {'x': 0, 'p': 1, 'positions': 0, 'loss_target': 0}
SHARED_INPUTS = []
_WEIGHT_DTYPES = {'attn_norm_g': _jnp.float32, 'w_in': _jnp.float32, 'gmlp_ln_g': _jnp.float32, 'gmlp_ln_b': _jnp.float32, 'gmlp_ws': _jnp.float32, 'gmlp_bs': _jnp.float32, 'q_norm_g': _jnp.float32, 'k_norm_g': _jnp.float32, 'sinks': _jnp.float32, 'ssm_a_re': _jnp.float32, 'ssm_a_im': _jnp.float32, 'ssm_log_dt': _jnp.float32, 'ssm_b_re': _jnp.float32, 'ssm_b_im': _jnp.float32, 'ssm_c_re': _jnp.float32, 'ssm_c_im': _jnp.float32, 'ssm_d': _jnp.float32, 'glu_w1': _jnp.float32, 'glu_w2': _jnp.float32, 'mix_out_g': _jnp.float32, 'w_out': _jnp.float32, 'mlp_norm_g': _jnp.float32, 'w_ff1': _jnp.float32, 'w_ff2': _jnp.float32, 'ple_norm_g': _jnp.float32, 'w_ple_gate': _jnp.float32, 'w_ple_proj': _jnp.float32}
MOMENT_SCALE = {'attn_norm_g': 8.190185e+01, 'w_in': 6.783269e+01, 'gmlp_ln_g': 1.154897e+00, 'gmlp_ln_b': 1.782336e+00, 'gmlp_ws': 9.864510e-01, 'gmlp_bs': 2.471554e+00, 'q_norm_g': 5.377151e+00, 'k_norm_g': 6.148571e+00, 'sinks': 3.370741e+00, 'ssm_a_re': 7.157962e+00, 'ssm_a_im': 8.805936e+00, 'ssm_log_dt': 1.404012e+02, 'ssm_b_re': 4.441577e+00, 'ssm_b_im': 3.658686e+00, 'ssm_c_re': 1.163780e+01, 'ssm_c_im': 1.450168e+01, 'ssm_d': 1.861159e+02, 'glu_w1': 8.519541e+01, 'glu_w2': 7.469099e+00, 'mix_out_g': 1.223176e+02, 'w_out': 9.573022e+01, 'mlp_norm_g': 2.044778e+02, 'w_ff1': 3.230491e+01, 'w_ff2': 9.914994e+01, 'ple_norm_g': 7.253847e-01, 'w_ple_gate': 5.644958e-01, 'w_ple_proj': 8.515460e-01}


def _to_microbatches(a, axis):
    t = _jnp.moveaxis(a, axis, 0)
    t = t.reshape((N_MICROBATCH, t.shape[0] // N_MICROBATCH) + t.shape[1:])
    return _jnp.moveaxis(t, 1, axis + 1)


def setup_inputs(seed: int = 0) -> dict:
    inp = _fwd_setup_inputs(seed)
    key = _jax.random.fold_in(_jax.random.key(seed), 7919)
    shape, _ = _output_shape()
    out = dict(inp)
    out["loss_target"] = _jax.random.normal(_jax.random.fold_in(key, 0), shape, _jnp.float32)
    for i, name in enumerate(TWIN_WEIGHTS):
        w = inp[name].astype(_jnp.float32)
        if MOMENT_SCALE is None:
            s = _jnp.sqrt(_jnp.mean(_jnp.square(w)) + 1e-30)
        else:
            s = MOMENT_SCALE[name]
        km, kv = _jax.random.split(_jax.random.fold_in(key, i + 1))
        out[name] = w
        out["m_" + name] = s * _jax.random.normal(km, w.shape, _jnp.float32)
        out["v_" + name] = (s * s) * _jax.random.uniform(kv, w.shape, _jnp.float32, 0.5, 1.5)
    if N_MICROBATCH > 1:
        for name, axis in PER_EXAMPLE_BATCH_AXIS.items():
            out[name] = _to_microbatches(out[name], axis)
    return {'x': out['x'], 'p': out['p'], 'positions': out['positions'], 'attn_norm_g': out['attn_norm_g'], 'w_in': out['w_in'], 'gmlp_ln_g': out['gmlp_ln_g'], 'gmlp_ln_b': out['gmlp_ln_b'], 'gmlp_ws': out['gmlp_ws'], 'gmlp_bs': out['gmlp_bs'], 'q_norm_g': out['q_norm_g'], 'k_norm_g': out['k_norm_g'], 'sinks': out['sinks'], 'ssm_a_re': out['ssm_a_re'], 'ssm_a_im': out['ssm_a_im'], 'ssm_log_dt': out['ssm_log_dt'], 'ssm_b_re': out['ssm_b_re'], 'ssm_b_im': out['ssm_b_im'], 'ssm_c_re': out['ssm_c_re'], 'ssm_c_im': out['ssm_c_im'], 'ssm_d': out['ssm_d'], 'glu_w1': out['glu_w1'], 'glu_w2': out['glu_w2'], 'mix_out_g': out['mix_out_g'], 'w_out': out['w_out'], 'mlp_norm_g': out['mlp_norm_g'], 'w_ff1': out['w_ff1'], 'w_ff2': out['w_ff2'], 'ple_norm_g': out['ple_norm_g'], 'w_ple_gate': out['w_ple_gate'], 'w_ple_proj': out['w_ple_proj'], 'loss_target': out['loss_target'], 'm_attn_norm_g': out['m_attn_norm_g'], 'm_w_in': out['m_w_in'], 'm_gmlp_ln_g': out['m_gmlp_ln_g'], 'm_gmlp_ln_b': out['m_gmlp_ln_b'], 'm_gmlp_ws': out['m_gmlp_ws'], 'm_gmlp_bs': out['m_gmlp_bs'], 'm_q_norm_g': out['m_q_norm_g'], 'm_k_norm_g': out['m_k_norm_g'], 'm_sinks': out['m_sinks'], 'm_ssm_a_re': out['m_ssm_a_re'], 'm_ssm_a_im': out['m_ssm_a_im'], 'm_ssm_log_dt': out['m_ssm_log_dt'], 'm_ssm_b_re': out['m_ssm_b_re'], 'm_ssm_b_im': out['m_ssm_b_im'], 'm_ssm_c_re': out['m_ssm_c_re'], 'm_ssm_c_im': out['m_ssm_c_im'], 'm_ssm_d': out['m_ssm_d'], 'm_glu_w1': out['m_glu_w1'], 'm_glu_w2': out['m_glu_w2'], 'm_mix_out_g': out['m_mix_out_g'], 'm_w_out': out['m_w_out'], 'm_mlp_norm_g': out['m_mlp_norm_g'], 'm_w_ff1': out['m_w_ff1'], 'm_w_ff2': out['m_w_ff2'], 'm_ple_norm_g': out['m_ple_norm_g'], 'm_w_ple_gate': out['m_w_ple_gate'], 'm_w_ple_proj': out['m_w_ple_proj'], 'v_attn_norm_g': out['v_attn_norm_g'], 'v_w_in': out['v_w_in'], 'v_gmlp_ln_g': out['v_gmlp_ln_g'], 'v_gmlp_ln_b': out['v_gmlp_ln_b'], 'v_gmlp_ws': out['v_gmlp_ws'], 'v_gmlp_bs': out['v_gmlp_bs'], 'v_q_norm_g': out['v_q_norm_g'], 'v_k_norm_g': out['v_k_norm_g'], 'v_sinks': out['v_sinks'], 'v_ssm_a_re': out['v_ssm_a_re'], 'v_ssm_a_im': out['v_ssm_a_im'], 'v_ssm_log_dt': out['v_ssm_log_dt'], 'v_ssm_b_re': out['v_ssm_b_re'], 'v_ssm_b_im': out['v_ssm_b_im'], 'v_ssm_c_re': out['v_ssm_c_re'], 'v_ssm_c_im': out['v_ssm_c_im'], 'v_ssm_d': out['v_ssm_d'], 'v_glu_w1': out['v_glu_w1'], 'v_glu_w2': out['v_glu_w2'], 'v_mix_out_g': out['v_mix_out_g'], 'v_w_out': out['v_w_out'], 'v_mlp_norm_g': out['v_mlp_norm_g'], 'v_w_ff1': out['v_w_ff1'], 'v_w_ff2': out['v_w_ff2'], 'v_ple_norm_g': out['v_ple_norm_g'], 'v_w_ple_gate': out['v_w_ple_gate'], 'v_w_ple_proj': out['v_w_ple_proj']}


def _loss(weights, diff, rest, loss_target):
    with _jax.named_scope("forward"):
        args = {**rest, TWIN_DIFF_INPUT: diff, **{k: w.astype(_WEIGHT_DTYPES[k]) for k, w in weights.items()}}
        y = _forward(args)
    with _jax.named_scope("loss_head"):
        err = _jnp.square(y.astype(_jnp.float32) - loss_target)
        return 0.5 * _jnp.sum(_jnp.mean(err, axis=-1)) if err.ndim else 0.5 * err


def _adamw(w, g, m, v):
    m = ADAM_B1 * m + (1.0 - ADAM_B1) * g
    v = ADAM_B2 * v + (1.0 - ADAM_B2) * _jnp.square(g)
    m_hat = m / (1.0 - ADAM_B1 ** ADAM_STEP)
    v_hat = v / (1.0 - ADAM_B2 ** ADAM_STEP)
    delta = -ADAM_LR * (m_hat / (_jnp.sqrt(v_hat) + ADAM_EPS) + ADAM_WD * w)
    return delta, m, v


def reference(x, p, positions, attn_norm_g, w_in, gmlp_ln_g, gmlp_ln_b, gmlp_ws, gmlp_bs, q_norm_g, k_norm_g, sinks, ssm_a_re, ssm_a_im, ssm_log_dt, ssm_b_re, ssm_b_im, ssm_c_re, ssm_c_im, ssm_d, glu_w1, glu_w2, mix_out_g, w_out, mlp_norm_g, w_ff1, w_ff2, ple_norm_g, w_ple_gate, w_ple_proj, loss_target, m_attn_norm_g, m_w_in, m_gmlp_ln_g, m_gmlp_ln_b, m_gmlp_ws, m_gmlp_bs, m_q_norm_g, m_k_norm_g, m_sinks, m_ssm_a_re, m_ssm_a_im, m_ssm_log_dt, m_ssm_b_re, m_ssm_b_im, m_ssm_c_re, m_ssm_c_im, m_ssm_d, m_glu_w1, m_glu_w2, m_mix_out_g, m_w_out, m_mlp_norm_g, m_w_ff1, m_w_ff2, m_ple_norm_g, m_w_ple_gate, m_w_ple_proj, v_attn_norm_g, v_w_in, v_gmlp_ln_g, v_gmlp_ln_b, v_gmlp_ws, v_gmlp_bs, v_q_norm_g, v_k_norm_g, v_sinks, v_ssm_a_re, v_ssm_a_im, v_ssm_log_dt, v_ssm_b_re, v_ssm_b_im, v_ssm_c_re, v_ssm_c_im, v_ssm_d, v_glu_w1, v_glu_w2, v_mix_out_g, v_w_out, v_mlp_norm_g, v_w_ff1, v_w_ff2, v_ple_norm_g, v_w_ple_gate, v_w_ple_proj):
    given = dict(x=x, p=p, positions=positions, attn_norm_g=attn_norm_g, w_in=w_in, gmlp_ln_g=gmlp_ln_g, gmlp_ln_b=gmlp_ln_b, gmlp_ws=gmlp_ws, gmlp_bs=gmlp_bs, q_norm_g=q_norm_g, k_norm_g=k_norm_g, sinks=sinks, ssm_a_re=ssm_a_re, ssm_a_im=ssm_a_im, ssm_log_dt=ssm_log_dt, ssm_b_re=ssm_b_re, ssm_b_im=ssm_b_im, ssm_c_re=ssm_c_re, ssm_c_im=ssm_c_im, ssm_d=ssm_d, glu_w1=glu_w1, glu_w2=glu_w2, mix_out_g=mix_out_g, w_out=w_out, mlp_norm_g=mlp_norm_g, w_ff1=w_ff1, w_ff2=w_ff2, ple_norm_g=ple_norm_g, w_ple_gate=w_ple_gate, w_ple_proj=w_ple_proj, loss_target=loss_target, m_attn_norm_g=m_attn_norm_g, m_w_in=m_w_in, m_gmlp_ln_g=m_gmlp_ln_g, m_gmlp_ln_b=m_gmlp_ln_b, m_gmlp_ws=m_gmlp_ws, m_gmlp_bs=m_gmlp_bs, m_q_norm_g=m_q_norm_g, m_k_norm_g=m_k_norm_g, m_sinks=m_sinks, m_ssm_a_re=m_ssm_a_re, m_ssm_a_im=m_ssm_a_im, m_ssm_log_dt=m_ssm_log_dt, m_ssm_b_re=m_ssm_b_re, m_ssm_b_im=m_ssm_b_im, m_ssm_c_re=m_ssm_c_re, m_ssm_c_im=m_ssm_c_im, m_ssm_d=m_ssm_d, m_glu_w1=m_glu_w1, m_glu_w2=m_glu_w2, m_mix_out_g=m_mix_out_g, m_w_out=m_w_out, m_mlp_norm_g=m_mlp_norm_g, m_w_ff1=m_w_ff1, m_w_ff2=m_w_ff2, m_ple_norm_g=m_ple_norm_g, m_w_ple_gate=m_w_ple_gate, m_w_ple_proj=m_w_ple_proj, v_attn_norm_g=v_attn_norm_g, v_w_in=v_w_in, v_gmlp_ln_g=v_gmlp_ln_g, v_gmlp_ln_b=v_gmlp_ln_b, v_gmlp_ws=v_gmlp_ws, v_gmlp_bs=v_gmlp_bs, v_q_norm_g=v_q_norm_g, v_k_norm_g=v_k_norm_g, v_sinks=v_sinks, v_ssm_a_re=v_ssm_a_re, v_ssm_a_im=v_ssm_a_im, v_ssm_log_dt=v_ssm_log_dt, v_ssm_b_re=v_ssm_b_re, v_ssm_b_im=v_ssm_b_im, v_ssm_c_re=v_ssm_c_re, v_ssm_c_im=v_ssm_c_im, v_ssm_d=v_ssm_d, v_glu_w1=v_glu_w1, v_glu_w2=v_glu_w2, v_mix_out_g=v_mix_out_g, v_w_out=v_w_out, v_mlp_norm_g=v_mlp_norm_g, v_w_ff1=v_w_ff1, v_w_ff2=v_w_ff2, v_ple_norm_g=v_ple_norm_g, v_w_ple_gate=v_w_ple_gate, v_w_ple_proj=v_w_ple_proj)
    weights = {n: given[n] for n in TWIN_WEIGHTS}
    shared = {n: given[n] for n in SHARED_INPUTS}
    per_example = {n: given[n] for n in ['x', 'p', 'positions']}
    grad_fn = _jax.value_and_grad(_loss, argnums=(0, 1))

    def one_microbatch(ex, loss_target):
        ex = dict(ex)
        diff = ex.pop(TWIN_DIFF_INPUT)
        return grad_fn(weights, diff, {**shared, **ex}, loss_target)

    if N_MICROBATCH == 1:
        loss, (grad_w, grad_x) = one_microbatch(per_example, given["loss_target"])
    else:
        def body(carry, xs):
            loss_sum, grad_sum = carry
            l_k, (gw_k, gx_k) = one_microbatch(xs[0], xs[1])
            with _jax.named_scope("update"):
                return (loss_sum + l_k, _jax.tree.map(_jnp.add, grad_sum, gw_k)), gx_k

        init = (_jnp.zeros((), _jnp.float32), _jax.tree.map(_jnp.zeros_like, weights))
        (loss, grad_w), grad_x = _jax.lax.scan(body, init, (per_example, given["loss_target"]))
    with _jax.named_scope("update"):
        delta_w, new_m, new_v = {}, {}, {}
        for n in TWIN_WEIGHTS:
            delta_w[n], new_m[n], new_v[n] = _adamw(weights[n], grad_w[n], given["m_" + n], given["v_" + n])
    return (loss, grad_x, *[grad_w[n] for n in TWIN_WEIGHTS], *[delta_w[n] for n in TWIN_WEIGHTS],
            *[new_m[n] for n in TWIN_WEIGHTS], *[new_v[n] for n in TWIN_WEIGHTS])
```

```python
import functools
import math

import jax
import jax.numpy as jnp
from jax import lax
from jax.experimental import pallas as pl
from jax.experimental.pallas import tpu as pltpu

F32 = jnp.float32
MXU = jnp.bfloat16
WIRE = jnp.bfloat16

D_MODEL = 1024
DEPTH = 4
HEAD_DIM = 64
A_WIDTH = 256
A_HEADS = 4
CHUNK = 128
B_WIDTH = 512
WINDOW = 128
C_WIDTH = 256
C_GROUP = 16
C_GROUPS = 16
C_STATE = 64
N_STATE = C_GROUPS * C_STATE
IN_COLS = 1536
D_FF = 4096
PLE_DIM = 256
EPS = 1e-6
ROPE_THETA = 10000.0
SCALE = HEAD_DIM ** -0.5
NEG = -1e30
N_DEV = 8

ADAM_LR = 0.001
ADAM_B1 = 0.9
ADAM_B2 = 0.999
ADAM_EPS = 1e-08
ADAM_WD = 0.01
ADAM_STEP = 10

V7X_VMEM_BYTES = 64 * 2 ** 20
VMEM_LIMIT = V7X_VMEM_BYTES - 8 * 2 ** 20
LANES = 128

MESH = pl.DeviceIdType.MESH


def _cp(*sem):
    return pltpu.CompilerParams(dimension_semantics=sem, vmem_limit_bytes=VMEM_LIMIT)


def _sds(shape, dtype=F32):
    return jax.ShapeDtypeStruct(shape, dtype)


def _mm(a, b):
    return jnp.dot(a.astype(MXU), b.astype(MXU), preferred_element_type=F32)


def _mm_nt(a, b):
    return lax.dot_general(a.astype(MXU), b.astype(MXU), (((1,), (1,)), ((), ())), preferred_element_type=F32)


def _mm_tn(a, b):
    return lax.dot_general(a.astype(MXU), b.astype(MXU), (((0,), (0,)), ((), ())), preferred_element_type=F32)


def _lane(shape):
    return lax.broadcasted_iota(jnp.int32, shape, len(shape) - 1)


def _row(shape):
    return lax.broadcasted_iota(jnp.int32, shape, 0)


_GELU_C = math.sqrt(2.0 / math.pi)


def _gelu(x):
    return 0.5 * x * (1.0 + jnp.tanh(_GELU_C * (x + 0.044715 * (x * x * x))))


def _gelu_grad(x):
    t = jnp.tanh(_GELU_C * (x + 0.044715 * (x * x * x)))
    return 0.5 * (1.0 + t) + 0.5 * x * (1.0 - t * t) * (_GELU_C * (1.0 + 3.0 * 0.044715 * (x * x)))


def _sigmoid(x):
    return 1.0 / (1.0 + jnp.exp(-x))


def _rms_stat(x):
    return lax.rsqrt(jnp.mean(x * x, axis=-1, keepdims=True) + EPS)


def _rms_bwd(x, r, g, dy):
    xh = x * r
    dxh = dy * g
    dx = r * (dxh - xh * jnp.mean(dxh * xh, axis=-1, keepdims=True))
    return dx, jnp.sum(dy * xh, axis=0, keepdims=True)


def _tril(w):
    return jnp.where(_row(w.shape) >= _lane(w.shape), w, 0.0)


def _swap64(x):
    return pltpu.roll(x, HEAD_DIM, 1)


def _group_sum64(x, lo):
    s_lo = jnp.sum(jnp.where(lo, x, 0.0), axis=-1, keepdims=True)
    s_hi = jnp.sum(jnp.where(lo, 0.0, x), axis=-1, keepdims=True)
    return jnp.where(lo, s_lo, s_hi)


def _partner(x):
    n = x.shape[-1]
    first = (_lane(x.shape) % HEAD_DIM) < HEAD_DIM // 2
    return jnp.where(first, pltpu.roll(x, n - HEAD_DIM // 2, 1), pltpu.roll(x, HEAD_DIM // 2, 1))


def _rope(y, cs, sn):
    return y * cs + _partner(y) * sn


def _rope_bwd(d, cs, sn):
    return d * cs + _partner(d * sn)


def _qk_norm_rope(x, g, cs, sn):
    lo = _lane(x.shape) < HEAD_DIM
    r = lax.rsqrt(_group_sum64(x * x, lo) * (1.0 / HEAD_DIM) + EPS)
    xh = x * r
    return _rope(xh * g, cs, sn), xh, r


def _qk_norm_rope_bwd(xh, r, g, cs, sn, d):
    lo = _lane(xh.shape) < HEAD_DIM
    dy = _rope_bwd(d, cs, sn)
    dxh = dy * g
    m = _group_sum64(dxh * xh, lo) * (1.0 / HEAD_DIM)
    return r * (dxh - xh * m), jnp.sum(dy * xh, axis=0, keepdims=True)


def _gmlp_head(blk, g, b):
    hi = _lane(blk.shape) >= HEAD_DIM
    mu = jnp.sum(jnp.where(hi, blk, 0.0), axis=-1, keepdims=True) * (1.0 / HEAD_DIM)
    xc = jnp.where(hi, blk - mu, 0.0)
    rstd = lax.rsqrt(jnp.sum(xc * xc, axis=-1, keepdims=True) * (1.0 / HEAD_DIM) + EPS)
    vhat = xc * rstd
    return vhat * g + b, vhat, rstd


def _rope_tables(pos_col, inv_row):
    T = pos_col.shape[0]
    tm = min(T, 1024)

    def body(p_ref, inv_ref, cs_ref, sn_ref):
        ang = p_ref[...].astype(F32) * inv_ref[...]
        s = jnp.sin(ang)
        cs_ref[...] = jnp.cos(ang)
        sn_ref[...] = jnp.where((_lane(ang.shape) % HEAD_DIM) < HEAD_DIM // 2, -s, s)

    blk = pl.BlockSpec((tm, LANES), lambda i: (i, 0))
    return pl.pallas_call(
        body, name="rope_tables", grid=(T // tm,),
        in_specs=[pl.BlockSpec((tm, 1), lambda i: (i, 0)), pl.BlockSpec((1, LANES), lambda i: (0, 0))],
        out_specs=[blk, blk], out_shape=[_sds((T, LANES))] * 2, compiler_params=_cp("parallel"))(pos_col, inv_row)


def _inproj_fwd(h, g, w):
    T = h.shape[0]
    tm = min(T, 512)

    def body(h_ref, g_ref, w_ref, z_ref):
        x = h_ref[...]
        z_ref[...] = _mm(x * _rms_stat(x) * g_ref[...], w_ref[...])

    return pl.pallas_call(
        body, name="inproj_fwd", grid=(T // tm,),
        in_specs=[pl.BlockSpec((tm, D_MODEL), lambda i: (i, 0)), pl.BlockSpec((1, D_MODEL), lambda i: (0, 0)),
                  pl.BlockSpec((D_MODEL, IN_COLS), lambda i: (0, 0))],
        out_specs=pl.BlockSpec((tm, IN_COLS), lambda i: (i, 0)), out_shape=_sds((T, IN_COLS)),
        compiler_params=_cp("parallel"))(h, g, w)


def _inproj_bwd(gres, h, g, w, dza, dzq, dzk, dzv, dzc):
    T = h.shape[0]
    tm = min(T, 512)

    def body(gr_ref, h_ref, g_ref, w_ref, a_ref, q_ref, k_ref, v_ref, c_ref, dh_ref, xn_ref, dz_ref, dg_ref):
        @pl.when(pl.program_id(0) == 0)
        def _():
            dg_ref[...] = jnp.zeros_like(dg_ref)

        x = h_ref[...]
        r = _rms_stat(x)
        gg = g_ref[...]
        dz = jnp.concatenate([a_ref[...], q_ref[...], k_ref[...], v_ref[...], c_ref[...]], axis=1)
        dxn = _mm_nt(dz, w_ref[...])
        dx, dg = _rms_bwd(x, r, gg, dxn)
        dh_ref[...] = gr_ref[...] + dx
        dg_ref[...] += dg
        xn_ref[...] = (x * r * gg).astype(MXU)
        dz_ref[...] = dz.astype(MXU)

    def rows(w_):
        return pl.BlockSpec((tm, w_), lambda i: (i, 0))

    row = pl.BlockSpec((1, D_MODEL), lambda i: (0, 0))
    return pl.pallas_call(
        body, name="inproj_bwd", grid=(T // tm,),
        in_specs=[rows(D_MODEL), rows(D_MODEL), row, pl.BlockSpec((D_MODEL, IN_COLS), lambda i: (0, 0)),
                  rows(512), rows(512), rows(128), rows(128), rows(256)],
        out_specs=[rows(D_MODEL), rows(D_MODEL), rows(IN_COLS), row],
        out_shape=[_sds((T, D_MODEL)), _sds((T, D_MODEL), MXU), _sds((T, IN_COLS), MXU), _sds((1, D_MODEL))],
        compiler_params=_cp("arbitrary"))(gres, h, g, w, dza, dzq, dzk, dzv, dzc)


def _gmlp_fwd(z, lng, lnb, ws, bsx):
    T = z.shape[0]
    tm = min(T, 512)
    nc = tm // CHUNK

    def body(z_ref, g_ref, b_ref, w_ref, bs_ref, ya_ref):
        zg = _gelu(z_ref[...])
        lo = _lane((tm, LANES)) < HEAD_DIM
        prods = []
        for hd in range(A_HEADS):
            sl = slice(hd * LANES, (hd + 1) * LANES)
            blk = zg[:, sl]
            vn, _, _ = _gmlp_head(blk, g_ref[:, sl], b_ref[:, sl])
            wm = _tril(w_ref[hd])
            sv = jnp.concatenate([_mm(wm, vn[c * CHUNK:(c + 1) * CHUNK]) + bs_ref[hd] for c in range(nc)], axis=0)
            prods.append(blk * _swap64(sv))
        ya_ref[:, 0:LANES] = jnp.where(lo, prods[0], _swap64(prods[1]))
        ya_ref[:, LANES:2 * LANES] = jnp.where(lo, prods[2], _swap64(prods[3]))

    row = pl.BlockSpec((1, 512), lambda i: (0, 0))
    mat = pl.BlockSpec((A_HEADS, CHUNK, CHUNK), lambda i: (0, 0, 0))
    return pl.pallas_call(
        body, name="gmlp_fwd", grid=(T // tm,),
        in_specs=[pl.BlockSpec((tm, 512), lambda i: (i, 0)), row, row, mat, mat],
        out_specs=pl.BlockSpec((tm, A_WIDTH), lambda i: (i, 0)), out_shape=_sds((T, A_WIDTH)),
        compiler_params=_cp("parallel"))(z, lng, lnb, ws, bsx)


def _gmlp_bwd(z, dya, lng, lnb, ws, bsx):
    T = z.shape[0]
    tm = min(T, 512)
    nc = tm // CHUNK

    def body(z_ref, dya_ref, g_ref, b_ref, w_ref, bs_ref, dza_ref, dw_ref, dbs_ref, dg_ref, db_ref):
        @pl.when(pl.program_id(0) == 0)
        def _():
            dw_ref[...] = jnp.zeros_like(dw_ref)
            dbs_ref[...] = jnp.zeros_like(dbs_ref)
            dg_ref[...] = jnp.zeros_like(dg_ref)
            db_ref[...] = jnp.zeros_like(db_ref)

        za = z_ref[...]
        zg = _gelu(za)
        gp = _gelu_grad(za)
        lo = _lane((tm, LANES)) < HEAD_DIM
        for hd in range(A_HEADS):
            sl = slice(hd * LANES, (hd + 1) * LANES)
            blk = zg[:, sl]
            g = g_ref[:, sl]
            vn, vhat, rstd = _gmlp_head(blk, g, b_ref[:, sl])
            wm = _tril(w_ref[hd])
            pair = dya_ref[:, (hd // 2) * LANES:(hd // 2 + 1) * LANES]
            dy = jnp.where(lo, pair if hd % 2 == 0 else _swap64(pair), 0.0)
            dsv = _swap64(dy * blk)
            svs, dvns = [], []
            dw = jnp.zeros((CHUNK, CHUNK), F32)
            dbs = jnp.zeros((CHUNK, 1), F32)
            for c in range(nc):
                cs = slice(c * CHUNK, (c + 1) * CHUNK)
                svs.append(_mm(wm, vn[cs]) + bs_ref[hd])
                dw = dw + _mm_nt(dsv[cs], vn[cs])
                dbs = dbs + jnp.sum(dsv[cs], axis=-1, keepdims=True)
                dvns.append(_mm_tn(wm, dsv[cs]))
            sv = jnp.concatenate(svs, axis=0)
            dvn = jnp.concatenate(dvns, axis=0)
            dw_ref[hd] += _tril(dw)
            dbs_ref[hd] += jnp.broadcast_to(dbs, (CHUNK, CHUNK))
            dg_ref[:, sl] += jnp.sum(dvn * vhat, axis=0, keepdims=True)
            db_ref[:, sl] += jnp.sum(dvn, axis=0, keepdims=True)
            du = dy * _swap64(sv)
            dvh = dvn * g
            m1 = jnp.sum(dvh, axis=-1, keepdims=True) * (1.0 / HEAD_DIM)
            m2 = jnp.sum(dvh * vhat, axis=-1, keepdims=True) * (1.0 / HEAD_DIM)
            dv = jnp.where(lo, 0.0, rstd * (dvh - m1 - vhat * m2))
            dza_ref[:, sl] = (du + dv) * gp[:, sl]

    row = pl.BlockSpec((1, 512), lambda i: (0, 0))
    mat = pl.BlockSpec((A_HEADS, CHUNK, CHUNK), lambda i: (0, 0, 0))
    return pl.pallas_call(
        body, name="gmlp_bwd", grid=(T // tm,),
        in_specs=[pl.BlockSpec((tm, 512), lambda i: (i, 0)), pl.BlockSpec((tm, A_WIDTH), lambda i: (i, 0)), row, row, mat, mat],
        out_specs=[pl.BlockSpec((tm, 512), lambda i: (i, 0)), mat, mat, row, row],
        out_shape=[_sds((T, 512)), _sds((A_HEADS, CHUNK, CHUNK)), _sds((A_HEADS, CHUNK, CHUNK)), _sds((1, 512)), _sds((1, 512))],
        compiler_params=_cp("arbitrary"))(z, dya, lng, lnb, ws, bsx)


def _attn_specs(T, tq, tile_of):
    nb = tq // WINDOW

    def prev(i):
        return jnp.maximum(tile_of(i) * nb - 1, 0)

    row = pl.BlockSpec((1, LANES), lambda i: (0, 0))
    return [
        pl.BlockSpec((tq, B_WIDTH), lambda i: (tile_of(i), 1)),
        pl.BlockSpec((tq, LANES), lambda i: (tile_of(i), 8)),
        pl.BlockSpec((tq, LANES), lambda i: (tile_of(i), 9)),
        pl.BlockSpec((WINDOW, LANES), lambda i: (prev(i), 8)),
        pl.BlockSpec((WINDOW, LANES), lambda i: (prev(i), 9)),
        pl.BlockSpec((tq, LANES), lambda i: (tile_of(i), 0)),
        pl.BlockSpec((tq, LANES), lambda i: (tile_of(i), 0)),
        pl.BlockSpec((WINDOW, LANES), lambda i: (prev(i), 0)),
        pl.BlockSpec((WINDOW, LANES), lambda i: (prev(i), 0)),
        row, row,
        pl.BlockSpec((8, LANES), lambda i: (0, 0)),
    ]


def _attn_mask(first):
    qi = lax.broadcasted_iota(jnp.int32, (WINDOW, 2 * WINDOW), 0)
    kj = lax.broadcasted_iota(jnp.int32, (WINDOW, 2 * WINDOW), 1)
    diff = qi + WINDOW - kj
    band = (diff >= 0) & (diff < WINDOW)
    return band & ((kj >= WINDOW) | jnp.logical_not(first))


def _dup_heads(x, lo):
    sw = _swap64(x)
    return jnp.where(lo, x, sw), jnp.where(lo, sw, x)


def _attn_probs(qm, kw, mask, sink):
    s = jnp.where(mask, _mm_nt(qm, kw) * SCALE, NEG)
    m = jnp.maximum(jnp.max(s, axis=-1, keepdims=True), sink)
    p = jnp.exp(s - m)
    es = jnp.exp(sink - m)
    den = jnp.sum(p, axis=-1, keepdims=True) + es
    return p / den, es / den


def _attn_fwd(z, cs, sn, qg, kg, sinks):
    T = z.shape[0]
    tq = min(T, 512)
    nb = tq // WINDOW

    def body(q_ref, k_ref, v_ref, kp_ref, vp_ref, cs_ref, sn_ref, csp_ref, snp_ref, qg_ref, kg_ref, sk_ref, o_ref):
        i = pl.program_id(0)
        csq, snq = cs_ref[...], sn_ref[...]
        cs_all = jnp.concatenate([csp_ref[...], csq], axis=0)
        sn_all = jnp.concatenate([snp_ref[...], snq], axis=0)
        k_all = jnp.concatenate([kp_ref[...], k_ref[...]], axis=0)
        v_all = jnp.concatenate([vp_ref[...], v_ref[...]], axis=0)
        kr, _, _ = _qk_norm_rope(k_all, kg_ref[...], cs_all, sn_all)
        lo_all = _lane(k_all.shape) < HEAD_DIM
        kd = _dup_heads(kr, lo_all)
        vd = _dup_heads(v_all, lo_all)
        lo = _lane((WINDOW, LANES)) < HEAD_DIM
        for pr in range(4):
            g = pr // 2
            qr, _, _ = _qk_norm_rope(q_ref[:, pr * LANES:(pr + 1) * LANES], qg_ref[...], csq, snq)
            outs = []
            for b in range(nb):
                mask = _attn_mask(i * nb + b == 0)
                qb = qr[b * WINDOW:(b + 1) * WINDOW]
                kw = kd[g][b * WINDOW:(b + 2) * WINDOW]
                vw = vd[g][b * WINDOW:(b + 2) * WINDOW]
                halves = []
                for hh in range(2):
                    sel = lo if hh == 0 else jnp.logical_not(lo)
                    a = 2 * pr + hh
                    pn, _ = _attn_probs(jnp.where(sel, qb, 0.0), kw, mask, sk_ref[a:a + 1, 0:1])
                    halves.append(_mm(pn, vw))
                outs.append(jnp.where(lo, halves[0], halves[1]))
            o_ref[:, pr * LANES:(pr + 1) * LANES] = jnp.concatenate(outs, axis=0)

    return pl.pallas_call(
        body, name="attn_fwd", grid=(T // tq,),
        in_specs=_attn_specs(T, tq, lambda i: i),
        out_specs=pl.BlockSpec((tq, B_WIDTH), lambda i: (i, 0)), out_shape=_sds((T, B_WIDTH)),
        compiler_params=_cp("parallel"))(z, z, z, z, z, cs, sn, cs, sn, qg, kg, sinks)


def _attn_bwd(z, cs, sn, qg, kg, sinks, o, do):
    T = z.shape[0]
    tq = min(T, 512)
    nb = tq // WINDOW
    nt = T // tq
    tk = tq + WINDOW

    def tile_of(i):
        return nt - 1 - i

    def body(q_ref, k_ref, v_ref, kp_ref, vp_ref, cs_ref, sn_ref, csp_ref, snp_ref, qg_ref, kg_ref, sk_ref, o_ref, do_ref,
             dq_ref, dk_ref, dv_ref, dqg_ref, dkg_ref, dsk_ref, acck, accv, ck, cv):
        i = pl.program_id(0)
        ti = nt - 1 - i

        @pl.when(i == 0)
        def _():
            dqg_ref[...] = jnp.zeros_like(dqg_ref)
            dkg_ref[...] = jnp.zeros_like(dkg_ref)
            dsk_ref[...] = jnp.zeros_like(dsk_ref)
            ck[...] = jnp.zeros_like(ck)
            cv[...] = jnp.zeros_like(cv)

        csq, snq = cs_ref[...], sn_ref[...]
        cs_all = jnp.concatenate([csp_ref[...], csq], axis=0)
        sn_all = jnp.concatenate([snp_ref[...], snq], axis=0)
        k_all = jnp.concatenate([kp_ref[...], k_ref[...]], axis=0)
        v_all = jnp.concatenate([vp_ref[...], v_ref[...]], axis=0)
        kr, kh, rk = _qk_norm_rope(k_all, kg_ref[...], cs_all, sn_all)
        lo_all = _lane(k_all.shape) < HEAD_DIM
        kd = _dup_heads(kr, lo_all)
        vd = _dup_heads(v_all, lo_all)
        lo = _lane((WINDOW, LANES)) < HEAD_DIM
        acck[...] = jnp.zeros_like(acck)
        accv[...] = jnp.zeros_like(accv)
        for pr in range(4):
            g = pr // 2
            psl = slice(pr * LANES, (pr + 1) * LANES)
            qr, qh, rq = _qk_norm_rope(q_ref[:, psl], qg_ref[...], csq, snq)
            dqs = []
            for b in range(nb):
                mask = _attn_mask(ti * nb + b == 0)
                bs = slice(b * WINDOW, (b + 1) * WINDOW)
                ws = slice(b * WINDOW, (b + 2) * WINDOW)
                qb = qr[bs]
                kw = kd[g][ws]
                vw = vd[g][ws]
                ob = o_ref[bs, psl]
                dob = do_ref[bs, psl]
                halves = []
                for hh in range(2):
                    sel = lo if hh == 0 else jnp.logical_not(lo)
                    a = 2 * pr + hh
                    qm = jnp.where(sel, qb, 0.0)
                    pn, psink = _attn_probs(qm, kw, mask, sk_ref[a:a + 1, 0:1])
                    dom = jnp.where(sel, dob, 0.0)
                    delta = jnp.sum(dom * ob, axis=-1, keepdims=True)
                    ds = pn * (_mm_nt(dom, vw) - delta)
                    dsk_ref[a:a + 1, :] += jnp.zeros((1, LANES), F32) - jnp.sum(psink * delta)
                    halves.append(_mm(ds, kw) * SCALE)
                    acck[g, ws, :] += _mm_tn(ds, qm) * SCALE
                    accv[g, ws, :] += _mm_tn(pn, dom)
                dqs.append(jnp.where(lo, halves[0], halves[1]))
            dx, dg = _qk_norm_rope_bwd(qh, rq, qg_ref[...], csq, snq, jnp.concatenate(dqs, axis=0))
            dq_ref[:, psl] = dx
            dqg_ref[...] += dg

        def fold(acc):
            f0 = acc[0] + _swap64(acc[0])
            f1 = acc[1] + _swap64(acc[1])
            return jnp.where(lo_all, f0, f1)

        dk_all = fold(acck)
        dv_all = fold(accv)
        pad = jnp.zeros((tq - WINDOW, LANES), F32)
        dk_own = dk_all[WINDOW:] + (jnp.concatenate([pad, ck[...]], axis=0) if nb > 1 else ck[...])
        dv_own = dv_all[WINDOW:] + (jnp.concatenate([pad, cv[...]], axis=0) if nb > 1 else cv[...])
        ck[...] = dk_all[:WINDOW]
        cv[...] = dv_all[:WINDOW]
        dxk, dgk = _qk_norm_rope_bwd(kh[WINDOW:], rk[WINDOW:], kg_ref[...], csq, snq, dk_own)
        dk_ref[...] = dxk
        dkg_ref[...] += dgk
        dv_ref[...] = dv_own

    row = pl.BlockSpec((1, LANES), lambda i: (0, 0))
    return pl.pallas_call(
        body, name="attn_bwd", grid=(nt,),
        in_specs=_attn_specs(T, tq, tile_of) + [pl.BlockSpec((tq, B_WIDTH), lambda i: (tile_of(i), 0))] * 2,
        out_specs=[pl.BlockSpec((tq, B_WIDTH), lambda i: (tile_of(i), 0)), pl.BlockSpec((tq, LANES), lambda i: (tile_of(i), 0)),
                   pl.BlockSpec((tq, LANES), lambda i: (tile_of(i), 0)), row, row, pl.BlockSpec((8, LANES), lambda i: (0, 0))],
        out_shape=[_sds((T, B_WIDTH)), _sds((T, LANES)), _sds((T, LANES)), _sds((1, LANES)), _sds((1, LANES)), _sds((8, LANES))],
        scratch_shapes=[pltpu.VMEM((2, tk, LANES), F32), pltpu.VMEM((2, tk, LANES), F32),
                        pltpu.VMEM((WINDOW, LANES), F32), pltpu.VMEM((WINDOW, LANES), F32)],
        compiler_params=_cp("arbitrary"))(z, z, z, z, z, cs, sn, cs, sn, qg, kg, sinks, o, do)


def _bbar_t(are, aim, ldt, btr, bti):
    lbr, lbi = _lam_bar(are, aim, ldt)
    den = are * are + aim * aim
    nr = lbr - 1.0
    cr = (nr * are + lbi * aim) / den
    ci = (lbi * are - nr * aim) / den
    return cr * btr - ci * bti, cr * bti + ci * btr


def _lam_bar(are, aim, ldt):
    dt = jnp.exp(ldt)
    er = jnp.exp(are * dt)
    return er * jnp.cos(aim * dt), er * jnp.sin(aim * dt)


def _block_diag(x):
    t = jnp.concatenate([x] * C_GROUPS, axis=1)
    return jnp.where(_row(t.shape) // C_GROUP == _lane(t.shape) // C_STATE, t, 0.0)


def _block_diag_fold(m):
    rg = _row((C_WIDTH, C_STATE)) // C_GROUP
    acc = jnp.zeros((C_WIDTH, C_STATE), F32)
    for g in range(C_GROUPS):
        acc = acc + jnp.where(rg == g, m[:, g * C_STATE:(g + 1) * C_STATE], 0.0)
    return acc


def _ssm_prep(are, aim, ldt, are_x, aim_x, ldt_x, btr, bti, cre, cim):
    def body(are_r, aim_r, ldt_r, arex_r, aimx_r, ldtx_r, btr_r, bti_r, cre_r, cim_r, bbd_ref, cbd_ref, lr_ref, li_ref):
        lr, li = _lam_bar(are_r[...], aim_r[...], ldt_r[...])
        lr_ref[...] = lr
        li_ref[...] = li
        br, bi = _bbar_t(arex_r[...], aimx_r[...], ldtx_r[...], btr_r[...], bti_r[...])
        bbd_ref[...] = jnp.concatenate([_block_diag(br), _block_diag(bi)], axis=1).astype(MXU)
        cbd_ref[...] = jnp.concatenate([_block_diag(cre_r[...]), -_block_diag(cim_r[...])], axis=1).astype(MXU)

    return pl.pallas_call(
        body, name="ssm_prep",
        out_shape=[_sds((C_WIDTH, 2 * N_STATE), MXU), _sds((C_WIDTH, 2 * N_STATE), MXU), _sds((C_GROUPS, C_STATE)), _sds((C_GROUPS, C_STATE))],
        compiler_params=pltpu.CompilerParams(vmem_limit_bytes=VMEM_LIMIT))(are, aim, ldt, are_x, aim_x, ldt_x, btr, bti, cre, cim)


def _ssm_param_bwd(are, aim, ldt, are_x, aim_x, ldt_x, btr, bti, dlr, dli, dbbd, dcr, dci):
    def body(are_r, aim_r, ldt_r, arex_r, aimx_r, ldtx_r, btr_r, bti_r, dlr_r, dli_r, dbbd_r, dcr_r, dci_r,
             dare_ref, daim_ref, dldt_ref, dbtr_ref, dbti_ref, dcre_ref, dcim_ref):
        _, vjp_l = jax.vjp(_lam_bar, are_r[...], aim_r[...], ldt_r[...])
        da1, di1, dl1 = vjp_l((dlr_r[...], dli_r[...]))
        dbr = _block_diag_fold(dbbd_r[:, 0:N_STATE])
        dbi = _block_diag_fold(dbbd_r[:, N_STATE:2 * N_STATE])
        _, vjp_b = jax.vjp(_bbar_t, arex_r[...], aimx_r[...], ldtx_r[...], btr_r[...], bti_r[...])
        da2, di2, dl2, dbtr, dbti = vjp_b((dbr, dbi))

        def gsum(x):
            return x.reshape(C_GROUPS, C_GROUP, C_STATE).sum(axis=1)

        dare_ref[...] = da1 + gsum(da2)
        daim_ref[...] = di1 + gsum(di2)
        dldt_ref[...] = jnp.broadcast_to(jnp.sum(dl1 + gsum(dl2), axis=-1, keepdims=True), (C_GROUPS, LANES))
        dbtr_ref[...] = dbtr
        dbti_ref[...] = dbti
        dcre_ref[...] = _block_diag_fold(dcr_r[...])
        dcim_ref[...] = -_block_diag_fold(dci_r[...])

    g = _sds((C_GROUPS, C_STATE))
    x = _sds((C_WIDTH, C_STATE))
    return pl.pallas_call(
        body, name="ssm_param_bwd", out_shape=[g, g, _sds((C_GROUPS, LANES)), x, x, x, x],
        compiler_params=pltpu.CompilerParams(vmem_limit_bytes=VMEM_LIMIT))(are, aim, ldt, are_x, aim_x, ldt_x, btr, bti, dlr, dli, dbbd, dcr, dci)


def _ssm_fwd(z, bbd, lr, li, cbd, dsk, w1, w2):
    T = z.shape[0]
    tt = min(T, 256)

    def body(u_ref, bbd_ref, lr_ref, li_ref, cbd_ref, d_ref, w1_ref, w2_ref, yc_ref, xr_ref, xi_ref, sr, si):
        @pl.when(pl.program_id(0) == 0)
        def _():
            sr[...] = jnp.zeros_like(sr)
            si[...] = jnp.zeros_like(si)

        u = u_ref[...]
        bu = _mm(u, bbd_ref[...])
        xr_ref[...] = bu[:, 0:N_STATE]
        xi_ref[...] = bu[:, N_STATE:2 * N_STATE]
        lam_r, lam_i = lr_ref[...], li_ref[...]

        def step(t, c):
            cr, ci = c
            nr = lam_r * cr - lam_i * ci + xr_ref[pl.ds(t, 1), :]
            ni = lam_r * ci + lam_i * cr + xi_ref[pl.ds(t, 1), :]
            xr_ref[pl.ds(t, 1), :] = nr
            xi_ref[pl.ds(t, 1), :] = ni
            return nr, ni

        cr, ci = lax.fori_loop(0, tt, step, (sr[...], si[...]), unroll=8)
        sr[...] = cr
        si[...] = ci
        x = jnp.concatenate([xr_ref[...], xi_ref[...]], axis=1)
        y2 = _gelu(_mm_nt(x, cbd_ref[...]) + d_ref[...] * u)
        yc_ref[...] = _mm(y2, w1_ref[...]) * _sigmoid(_mm(y2, w2_ref[...]))

    big = pl.BlockSpec((C_WIDTH, 2 * N_STATE), lambda i: (0, 0))
    srow = pl.BlockSpec((1, N_STATE), lambda i: (0, 0))
    wsp = pl.BlockSpec((C_WIDTH, C_WIDTH), lambda i: (0, 0))
    xs = pl.BlockSpec((tt, N_STATE), lambda i: (i, 0))
    return pl.pallas_call(
        body, name="ssm_fwd", grid=(T // tt,),
        in_specs=[pl.BlockSpec((tt, C_WIDTH), lambda i: (i, 5)), big, srow, srow, big, pl.BlockSpec((1, C_WIDTH), lambda i: (0, 0)), wsp, wsp],
        out_specs=[pl.BlockSpec((tt, C_WIDTH), lambda i: (i, 0)), xs, xs],
        out_shape=[_sds((T, C_WIDTH)), _sds((T, N_STATE)), _sds((T, N_STATE))],
        scratch_shapes=[pltpu.VMEM((1, N_STATE), F32)] * 2,
        compiler_params=_cp("arbitrary"))(z, bbd, lr, li, cbd, dsk, w1, w2)


def _ssm_bwd(dyc, z, xr, xi, bbd, lr, li, cbd, dsk, w1, w2):
    T = z.shape[0]
    tt = min(T, 256)
    nt = T // tt

    def tile_of(i):
        return nt - 1 - i

    def body(dyc_ref, u_ref, xr_ref, xi_ref, xpr_ref, xpi_ref, bbd_ref, lr_ref, li_ref, cbd_ref, d_ref, w1_ref, w2_ref,
             du_ref, y2_ref, da1_ref, da2_ref, dy_ref, arai_ref, dlr_ref, dli_ref, dd_ref, gr, gi, sr, si):
        i = pl.program_id(0)
        ti = nt - 1 - i

        @pl.when(i == 0)
        def _():
            sr[...] = jnp.zeros_like(sr)
            si[...] = jnp.zeros_like(si)
            dlr_ref[...] = jnp.zeros_like(dlr_ref)
            dli_ref[...] = jnp.zeros_like(dli_ref)
            dd_ref[...] = jnp.zeros_like(dd_ref)

        u = u_ref[...]
        xr_t, xi_t = xr_ref[...], xi_ref[...]
        y = _mm_nt(jnp.concatenate([xr_t, xi_t], axis=1), cbd_ref[...]) + d_ref[...] * u
        y2 = _gelu(y)
        a1 = _mm(y2, w1_ref[...])
        sg = _sigmoid(_mm(y2, w2_ref[...]))
        dyc_t = dyc_ref[...]
        da1 = dyc_t * sg
        da2 = dyc_t * a1 * sg * (1.0 - sg)
        dy = (_mm_nt(da1, w1_ref[...]) + _mm_nt(da2, w2_ref[...])) * _gelu_grad(y)
        gx = _mm(dy, cbd_ref[...])
        gr[...] = gx[:, 0:N_STATE]
        gi[...] = gx[:, N_STATE:2 * N_STATE]
        lam_r, lam_i = lr_ref[...], li_ref[...]

        def step(k, c):
            t = tt - 1 - k
            cr, ci = c
            nr = gr[pl.ds(t, 1), :] + lam_r * cr + lam_i * ci
            ni = gi[pl.ds(t, 1), :] - lam_i * cr + lam_r * ci
            gr[pl.ds(t, 1), :] = nr
            gi[pl.ds(t, 1), :] = ni
            return nr, ni

        cr, ci = lax.fori_loop(0, tt, step, (sr[...], si[...]), unroll=8)
        sr[...] = cr
        si[...] = ci
        ar, ai = gr[...], gi[...]
        first_row = _row(ar.shape) == 0
        live = jnp.where(ti > 0, 1.0, 0.0)
        xsr = jnp.where(first_row, xpr_ref[7:8, :] * live, pltpu.roll(xr_t, 1, 0))
        xsi = jnp.where(first_row, xpi_ref[7:8, :] * live, pltpu.roll(xi_t, 1, 0))
        dlr_ref[...] += jnp.sum(ar * xsr + ai * xsi, axis=0, keepdims=True)
        dli_ref[...] += jnp.sum(ai * xsr - ar * xsi, axis=0, keepdims=True)
        dd_ref[...] += jnp.sum(dy * u, axis=0, keepdims=True)
        arai = jnp.concatenate([ar, ai], axis=1)
        du_ref[...] = _mm_nt(arai, bbd_ref[...]) + d_ref[...] * dy
        y2_ref[...] = y2.astype(MXU)
        da1_ref[...] = da1.astype(MXU)
        da2_ref[...] = da2.astype(MXU)
        dy_ref[...] = dy.astype(MXU)
        arai_ref[...] = arai.astype(MXU)

    def prev(i):
        return jnp.maximum(tile_of(i) * (tt // 8) - 1, 0)

    big = pl.BlockSpec((C_WIDTH, 2 * N_STATE), lambda i: (0, 0))
    srow = pl.BlockSpec((1, N_STATE), lambda i: (0, 0))
    wsp = pl.BlockSpec((C_WIDTH, C_WIDTH), lambda i: (0, 0))
    xs = pl.BlockSpec((tt, N_STATE), lambda i: (tile_of(i), 0))
    xp = pl.BlockSpec((8, N_STATE), lambda i: (prev(i), 0))
    cw = pl.BlockSpec((tt, C_WIDTH), lambda i: (tile_of(i), 0))
    drow = pl.BlockSpec((1, C_WIDTH), lambda i: (0, 0))
    return pl.pallas_call(
        body, name="ssm_bwd", grid=(nt,),
        in_specs=[cw, pl.BlockSpec((tt, C_WIDTH), lambda i: (tile_of(i), 5)), xs, xs, xp, xp, big, srow, srow, big, drow, wsp, wsp],
        out_specs=[cw, cw, cw, cw, cw, pl.BlockSpec((tt, 2 * N_STATE), lambda i: (tile_of(i), 0)), srow, srow, drow],
        out_shape=[_sds((T, C_WIDTH))] + [_sds((T, C_WIDTH), MXU)] * 4 + [_sds((T, 2 * N_STATE), MXU), _sds((1, N_STATE)), _sds((1, N_STATE)), _sds((1, C_WIDTH))],
        scratch_shapes=[pltpu.VMEM((tt, N_STATE), F32)] * 2 + [pltpu.VMEM((1, N_STATE), F32)] * 2,
        compiler_params=_cp("arbitrary"))(dyc, z, xr, xi, xr, xi, bbd, lr, li, cbd, dsk, w1, w2)


_GROUPS = ((0, A_WIDTH), (A_WIDTH, A_WIDTH + B_WIDTH), (A_WIDTH + B_WIDTH, D_MODEL))


def _merge_fwd(h, ya, yb, yc, g, w):
    T = h.shape[0]
    tm = min(T, 512)

    def body(h_ref, a_ref, b_ref, c_ref, g_ref, w_ref, o_ref):
        yn = jnp.concatenate([y * _rms_stat(y) for y in (a_ref[...], b_ref[...], c_ref[...])], axis=1) * g_ref[...]
        o_ref[...] = h_ref[...] + _mm(yn, w_ref[...])

    def rows(w_):
        return pl.BlockSpec((tm, w_), lambda i: (i, 0))

    return pl.pallas_call(
        body, name="merge_fwd", grid=(T // tm,),
        in_specs=[rows(D_MODEL), rows(A_WIDTH), rows(B_WIDTH), rows(C_WIDTH), pl.BlockSpec((1, D_MODEL), lambda i: (0, 0)),
                  pl.BlockSpec((D_MODEL, D_MODEL), lambda i: (0, 0))],
        out_specs=rows(D_MODEL), out_shape=_sds((T, D_MODEL)), compiler_params=_cp("parallel"))(h, ya, yb, yc, g, w)


def _merge_bwd(gres, ya, yb, yc, g, w):
    T = gres.shape[0]
    tm = min(T, 512)

    def body(gr_ref, a_ref, b_ref, c_ref, g_ref, w_ref, da_ref, db_ref, dc_ref, yn_ref, dg_ref):
        @pl.when(pl.program_id(0) == 0)
        def _():
            dg_ref[...] = jnp.zeros_like(dg_ref)

        dyn = _mm_nt(gr_ref[...], w_ref[...])
        yns, dgs = [], []
        for (c0, c1), y_ref, d_ref in zip(_GROUPS, (a_ref, b_ref, c_ref), (da_ref, db_ref, dc_ref)):
            y = y_ref[...]
            r = _rms_stat(y)
            gg = g_ref[:, c0:c1]
            dx, dg = _rms_bwd(y, r, gg, dyn[:, c0:c1])
            d_ref[...] = dx
            dgs.append(dg)
            yns.append(y * r * gg)
        dg_ref[...] += jnp.concatenate(dgs, axis=1)
        yn_ref[...] = jnp.concatenate(yns, axis=1).astype(MXU)

    def rows(w_):
        return pl.BlockSpec((tm, w_), lambda i: (i, 0))

    row = pl.BlockSpec((1, D_MODEL), lambda i: (0, 0))
    return pl.pallas_call(
        body, name="merge_bwd", grid=(T // tm,),
        in_specs=[rows(D_MODEL), rows(A_WIDTH), rows(B_WIDTH), rows(C_WIDTH), row, pl.BlockSpec((D_MODEL, D_MODEL), lambda i: (0, 0))],
        out_specs=[rows(A_WIDTH), rows(B_WIDTH), rows(C_WIDTH), rows(D_MODEL), row],
        out_shape=[_sds((T, A_WIDTH)), _sds((T, B_WIDTH)), _sds((T, C_WIDTH)), _sds((T, D_MODEL), MXU), _sds((1, D_MODEL))],
        compiler_params=_cp("arbitrary"))(gres, ya, yb, yc, g, w)


FF_BLOCK = D_FF // N_DEV


def _mlp_fwd(h, g, w1, w2):
    T = h.shape[0]
    tm = min(T, 1024)

    def body(h_ref, g_ref, w1_ref, w2_ref, o_ref, hn):
        @pl.when(pl.program_id(1) == 0)
        def _():
            x = h_ref[...]
            hn[...] = (x * _rms_stat(x) * g_ref[...]).astype(MXU)
            o_ref[...] = x

        a = jnp.maximum(_mm(hn[...], w1_ref[...]), 0.0)
        o_ref[...] += _mm(a * a, w2_ref[...])

    return pl.pallas_call(
        body, name="mlp_fwd", grid=(T // tm, N_DEV),
        in_specs=[pl.BlockSpec((tm, D_MODEL), lambda i, j: (i, 0)), pl.BlockSpec((1, D_MODEL), lambda i, j: (0, 0)),
                  pl.BlockSpec((None, D_MODEL, FF_BLOCK), lambda i, j: (j, 0, 0)), pl.BlockSpec((FF_BLOCK, D_MODEL), lambda i, j: (j, 0))],
        out_specs=pl.BlockSpec((tm, D_MODEL), lambda i, j: (i, 0)), out_shape=_sds((T, D_MODEL)),
        scratch_shapes=[pltpu.VMEM((tm, D_MODEL), MXU)],
        compiler_params=_cp("parallel", "arbitrary"))(h, g, w1, w2)


def _mlp_bwd(gres, h, g, w1, w2):
    T = h.shape[0]
    tm = min(T, 1024)

    def body(gr_ref, h_ref, g_ref, w1_ref, w2_ref, dh_ref, hn_ref, da_ref, r_ref, dg_ref, acc):
        i, j = pl.program_id(0), pl.program_id(1)

        @pl.when((i == 0) & (j == 0))
        def _():
            dg_ref[...] = jnp.zeros_like(dg_ref)

        @pl.when(j == 0)
        def _():
            x = h_ref[...]
            hn_ref[...] = (x * _rms_stat(x) * g_ref[...]).astype(MXU)
            acc[...] = jnp.zeros_like(acc)

        a = jnp.maximum(_mm(hn_ref[...], w1_ref[...]), 0.0)
        da = _mm_nt(gr_ref[...], w2_ref[...]) * (2.0 * a)
        acc[...] += _mm_nt(da, w1_ref[...])
        da_ref[...] = da.astype(MXU)
        r_ref[...] = (a * a).astype(MXU)

        @pl.when(j == N_DEV - 1)
        def _():
            x = h_ref[...]
            dx, dg = _rms_bwd(x, _rms_stat(x), g_ref[...], acc[...])
            dh_ref[...] = gr_ref[...] + dx
            dg_ref[...] += dg

    rows = pl.BlockSpec((tm, D_MODEL), lambda i, j: (i, 0))
    row = pl.BlockSpec((1, D_MODEL), lambda i, j: (0, 0))
    ffb = pl.BlockSpec((tm, FF_BLOCK), lambda i, j: (i, j))
    return pl.pallas_call(
        body, name="mlp_bwd", grid=(T // tm, N_DEV),
        in_specs=[rows, rows, row, pl.BlockSpec((None, D_MODEL, FF_BLOCK), lambda i, j: (j, 0, 0)),
                  pl.BlockSpec((FF_BLOCK, D_MODEL), lambda i, j: (j, 0))],
        out_specs=[rows, rows, ffb, ffb, row],
        out_shape=[_sds((T, D_MODEL)), _sds((T, D_MODEL), MXU), _sds((T, D_FF), MXU), _sds((T, D_FF), MXU), _sds((1, D_MODEL))],
        scratch_shapes=[pltpu.VMEM((tm, D_MODEL), F32)],
        compiler_params=_cp("arbitrary", "arbitrary"))(gres, h, g, w1, w2)


def _ple_fwd(h, p, g, wg, wp):
    T = h.shape[0]
    tm = min(T, 512)

    def body(h_ref, p_ref, g_ref, wg_ref, wp_ref, o_ref):
        x = h_ref[...]
        gate = _sigmoid(_mm(x * _rms_stat(x) * g_ref[...], wg_ref[...]))
        o_ref[...] = x + gate * _mm(p_ref[...], wp_ref[...])

    rows = pl.BlockSpec((tm, D_MODEL), lambda i: (i, 0))
    return pl.pallas_call(
        body, name="ple_fwd", grid=(T // tm,),
        in_specs=[rows, pl.BlockSpec((tm, PLE_DIM), lambda i: (i, 0)), pl.BlockSpec((1, D_MODEL), lambda i: (0, 0)),
                  pl.BlockSpec((D_MODEL, D_MODEL), lambda i: (0, 0)), pl.BlockSpec((PLE_DIM, D_MODEL), lambda i: (0, 0))],
        out_specs=rows, out_shape=_sds((T, D_MODEL)), compiler_params=_cp("parallel"))(h, p, g, wg, wp)


def _ple_bwd(gres, h, p, g, wg, wp):
    T = h.shape[0]
    tm = min(T, 512)

    def body(gr_ref, h_ref, p_ref, g_ref, wg_ref, wp_ref, dh_ref, hn_ref, dgp_ref, de_ref, dg_ref):
        @pl.when(pl.program_id(0) == 0)
        def _():
            dg_ref[...] = jnp.zeros_like(dg_ref)

        x = h_ref[...]
        r = _rms_stat(x)
        gg = g_ref[...]
        hn = x * r * gg
        gate = _sigmoid(_mm(hn, wg_ref[...]))
        e = _mm(p_ref[...], wp_ref[...])
        gr = gr_ref[...]
        dgp = gr * e * gate * (1.0 - gate)
        dx, dg = _rms_bwd(x, r, gg, _mm_nt(dgp, wg_ref[...]))
        dh_ref[...] = gr + dx
        dg_ref[...] += dg
        hn_ref[...] = hn.astype(MXU)
        dgp_ref[...] = dgp.astype(MXU)
        de_ref[...] = (gr * gate).astype(MXU)

    rows = pl.BlockSpec((tm, D_MODEL), lambda i: (i, 0))
    row = pl.BlockSpec((1, D_MODEL), lambda i: (0, 0))
    return pl.pallas_call(
        body, name="ple_bwd", grid=(T // tm,),
        in_specs=[rows, rows, pl.BlockSpec((tm, PLE_DIM), lambda i: (i, 0)), row,
                  pl.BlockSpec((D_MODEL, D_MODEL), lambda i: (0, 0)), pl.BlockSpec((PLE_DIM, D_MODEL), lambda i: (0, 0))],
        out_specs=[rows, rows, rows, rows, row],
        out_shape=[_sds((T, D_MODEL))] + [_sds((T, D_MODEL), MXU)] * 3 + [_sds((1, D_MODEL))],
        compiler_params=_cp("arbitrary"))(gres, h, p, g, wg, wp)


def _loss_head(h, target):
    T = h.shape[0]
    tm = min(T, 1024)

    def body(h_ref, t_ref, dh_ref, l_ref):
        @pl.when(pl.program_id(0) == 0)
        def _():
            l_ref[...] = jnp.zeros_like(l_ref)

        e = h_ref[...] - t_ref[...]
        dh_ref[...] = e * (1.0 / D_MODEL)
        l_ref[...] += jnp.zeros_like(l_ref) + 0.5 * jnp.sum(jnp.mean(e * e, axis=-1, keepdims=True))

    rows = pl.BlockSpec((tm, D_MODEL), lambda i: (i, 0))
    return pl.pallas_call(
        body, name="loss_head", grid=(T // tm,), in_specs=[rows, rows],
        out_specs=[rows, pl.BlockSpec((8, LANES), lambda i: (0, 0))], out_shape=[_sds((T, D_MODEL)), _sds((8, LANES))],
        compiler_params=_cp("arbitrary"))(h, target)


def _tn(a, b, m, n, *, bm, bn, a_off=0, b_off=0, n_major=False, name="tn"):
    T = a.shape[0]
    tk = min(T, 1024)
    assert m % bm == 0 and n % bn == 0 and (not n_major or bm == m)

    def body(a_ref, b_ref, o_ref):
        @pl.when(pl.program_id(2) == 0)
        def _():
            o_ref[...] = jnp.zeros_like(o_ref)

        o_ref[...] += _mm_tn(a_ref[...], b_ref[...])

    if n_major:
        out_spec = pl.BlockSpec((None, bm, bn), lambda i, j, k: (j, 0, 0))
        out_shape = _sds((n // bn, m, bn))
    else:
        out_spec = pl.BlockSpec((bm, bn), lambda i, j, k: (i, j))
        out_shape = _sds((m, n))
    return pl.pallas_call(
        body, name=name, grid=(m // bm, n // bn, T // tk),
        in_specs=[pl.BlockSpec((tk, bm), lambda i, j, k: (k, a_off + i)), pl.BlockSpec((tk, bn), lambda i, j, k: (k, b_off + j))],
        out_specs=out_spec, out_shape=out_shape, compiler_params=_cp("parallel", "parallel", "arbitrary"))(a, b)


def _sum_slots(land):
    _, R, C = land.shape
    tr = R
    for cand in (512, 256, 128, 64, 32, 16, 8):
        if R % cand == 0:
            tr = cand
            break

    def body(l_ref, o_ref):
        acc = l_ref[0].astype(F32)
        for s in range(1, N_DEV):
            acc = acc + l_ref[s].astype(F32)
        o_ref[...] = acc

    return pl.pallas_call(
        body, name="sum_slots", grid=(R // tr,), in_specs=[pl.BlockSpec((N_DEV, tr, C), lambda i: (0, i, 0))],
        out_specs=pl.BlockSpec((tr, C), lambda i: (i, 0)), out_shape=_sds((R, C)), compiler_params=_cp("parallel"))(land)


def _adamw(w, g, m, v):
    R, C = w.shape
    tr = R
    for cand in (512, 256, 128, 64, 32, 16, 8):
        if R % cand == 0 and cand * C * 4 <= 2 ** 21:
            tr = cand
            break

    def body(w_ref, g_ref, m_ref, v_ref, d_ref, nm_ref, nv_ref):
        gg = g_ref[...]
        nm = ADAM_B1 * m_ref[...] + (1.0 - ADAM_B1) * gg
        nv = ADAM_B2 * v_ref[...] + (1.0 - ADAM_B2) * (gg * gg)
        m_hat = nm / (1.0 - ADAM_B1 ** ADAM_STEP)
        v_hat = nv / (1.0 - ADAM_B2 ** ADAM_STEP)
        d_ref[...] = -ADAM_LR * (m_hat / (jnp.sqrt(v_hat) + ADAM_EPS) + ADAM_WD * w_ref[...])
        nm_ref[...] = nm
        nv_ref[...] = nv

    blk = pl.BlockSpec((tr, C), lambda i: (i, 0))
    return pl.pallas_call(
        body, name="adamw", grid=(R // tr,), in_specs=[blk] * 4, out_specs=[blk] * 3, out_shape=[_sds((R, C))] * 3,
        compiler_params=_cp("parallel"))(w, g, m, v)


def _exchange(big, small, *, gather_big, name):
    big_shape = big.shape[-2:]

    def body(big_ref, small_ref, bl_ref, sl_ref, send_b, recv_b, send_s, recv_s, loc):
        x, y, c = lax.axis_index("x"), lax.axis_index("y"), lax.axis_index("c")
        me = 4 * x + 2 * y + c
        src_me = big_ref if gather_big else big_ref.at[me]
        own_b = pltpu.make_async_copy(src_me, bl_ref.at[me], loc.at[0])
        own_s = pltpu.make_async_copy(small_ref, sl_ref.at[me], loc.at[1])
        own_b.start()
        own_s.start()
        copies = []
        for k in range(1, N_DEV):
            px, py, pc = x ^ (k >> 2), y ^ ((k >> 1) & 1), c ^ (k & 1)
            peer = 4 * px + 2 * py + pc
            src = big_ref if gather_big else big_ref.at[peer]
            cb = pltpu.make_async_remote_copy(src_ref=src, dst_ref=bl_ref.at[me], send_sem=send_b.at[k], recv_sem=recv_b.at[k],
                                              device_id=(px, py, pc), device_id_type=MESH)
            cs = pltpu.make_async_remote_copy(src_ref=small_ref, dst_ref=sl_ref.at[me], send_sem=send_s.at[k], recv_sem=recv_s.at[k],
                                              device_id=(px, py, pc), device_id_type=MESH)
            cs.start()
            cb.start()
            rb = pltpu.make_async_remote_copy(src_ref=src, dst_ref=bl_ref.at[peer], send_sem=send_b.at[k], recv_sem=recv_b.at[k],
                                              device_id=(px, py, pc), device_id_type=MESH)
            rs = pltpu.make_async_remote_copy(src_ref=small_ref, dst_ref=sl_ref.at[peer], send_sem=send_s.at[k], recv_sem=recv_s.at[k],
                                              device_id=(px, py, pc), device_id_type=MESH)
            copies.append((cb, cs, rb, rs))
        for cb, cs, rb, rs in copies:
            rs.wait_recv()
            rb.wait_recv()
        for cb, cs, rb, rs in copies:
            cs.wait_send()
            cb.wait_send()
        own_b.wait()
        own_s.wait()

    anyspec = pl.BlockSpec(memory_space=pl.ANY)
    return pl.pallas_call(
        body, name=name, in_specs=[anyspec, anyspec], out_specs=[anyspec, anyspec],
        out_shape=[_sds((N_DEV,) + tuple(big_shape), big.dtype), _sds((N_DEV,) + tuple(small.shape), small.dtype)],
        scratch_shapes=[pltpu.SemaphoreType.DMA((N_DEV,)), pltpu.SemaphoreType.DMA((N_DEV,)), pltpu.SemaphoreType.DMA((N_DEV,)),
                        pltpu.SemaphoreType.DMA((N_DEV,)), pltpu.SemaphoreType.DMA((2,))],
        compiler_params=pltpu.CompilerParams(has_side_effects=True))(big, small)


SHARDED = ("w_in", "glu_w1", "glu_w2", "w_out", "w_ff1", "w_ff2", "w_ple_gate", "w_ple_proj")
SMALL = ("attn_norm_g", "gmlp_ln_g", "gmlp_ln_b", "gmlp_ws", "gmlp_bs", "q_norm_g", "k_norm_g", "sinks", "ssm_a_re", "ssm_a_im",
         "ssm_log_dt", "ssm_b_re", "ssm_b_im", "ssm_c_re", "ssm_c_im", "ssm_d", "mix_out_g", "mlp_norm_g", "ple_norm_g")
WEIGHTS = ("attn_norm_g", "w_in", "gmlp_ln_g", "gmlp_ln_b", "gmlp_ws", "gmlp_bs", "q_norm_g", "k_norm_g", "sinks", "ssm_a_re", "ssm_a_im",
           "ssm_log_dt", "ssm_b_re", "ssm_b_im", "ssm_c_re", "ssm_c_im", "ssm_d", "glu_w1", "glu_w2", "mix_out_g", "w_out", "mlp_norm_g",
           "w_ff1", "w_ff2", "ple_norm_g", "w_ple_gate", "w_ple_proj")
FLAT_COLS = 1024


def _pack(arrs, dtype):
    flat = jnp.concatenate([a.astype(dtype).reshape(-1) for a in arrs])
    pad = (-flat.shape[0]) % (8 * FLAT_COLS)
    if pad:
        flat = jnp.concatenate([flat, jnp.zeros((pad,), dtype)])
    return flat.reshape(-1, FLAT_COLS)


def _unpack(flat, shapes, lead=()):
    flat = flat.reshape(lead + (-1,))
    out, off = [], 0
    for s in shapes:
        n = math.prod(s)
        out.append(flat[..., off:off + n].reshape(lead + tuple(s)))
        off += n
    return out


def _col_major(w):
    rows, cols = w.shape
    return w.reshape(rows, N_DEV, cols // N_DEV).transpose(1, 0, 2)


def _from_col_major(s):
    n, rows, cs = s.shape
    return s.transpose(1, 0, 2).reshape(rows, n * cs)


def _layer_fwd(h, p_l, cs, sn, W, S, sp):
    z = _inproj_fwd(h, S["attn_norm_g"], W["w_in"])
    ya = _gmlp_fwd(z, S["lng"], S["lnb"], S["gmlp_ws"], S["bsx"])
    yb = _attn_fwd(z, cs, sn, S["qg"], S["kg"], S["sinks"])
    yc, xr, xi = _ssm_fwd(z, sp["bbd"], sp["lr"], sp["li"], sp["cbd"], S["ssm_d"], W["glu_w1"], W["glu_w2"])
    h1 = _merge_fwd(h, ya, yb, yc, S["mix_out_g"], W["w_out"])
    h2 = _mlp_fwd(h1, S["mlp_norm_g"], W["w_ff1"], W["w_ff2"])
    h3 = _ple_fwd(h2, p_l, S["ple_norm_g"], W["w_ple_gate"], W["w_ple_proj"])
    return h3, dict(h=h, z=z, ya=ya, yb=yb, yc=yc, xr=xr, xi=xi, h1=h1, h2=h2)


def _layer_bwd(g3, p_l, cs, sn, W, S, sp, A, raw):
    G = {}
    g2, hn3, dgp, de, G["ple_norm_g"] = _ple_bwd(g3, A["h2"], p_l, S["ple_norm_g"], W["w_ple_gate"], W["w_ple_proj"])
    G["w_ple_gate"] = _tn(hn3, dgp, D_MODEL, D_MODEL, bm=512, bn=512, name="tn_gate")
    G["w_ple_proj"] = _col_major(_tn(p_l, de, PLE_DIM, D_MODEL, bm=256, bn=512, name="tn_proj"))
    g1, hn, da, r, G["mlp_norm_g"] = _mlp_bwd(g2, A["h1"], S["mlp_norm_g"], W["w_ff1"], W["w_ff2"])
    G["w_ff1"] = _tn(hn, da, D_MODEL, D_FF, bm=D_MODEL, bn=FF_BLOCK, n_major=True, name="tn_ff1")
    G["w_ff2"] = _tn(r, g2, D_FF, D_MODEL, bm=512, bn=512, name="tn_ff2")
    dya, dyb, dyc, yn, G["mix_out_g"] = _merge_bwd(g1, A["ya"], A["yb"], A["yc"], S["mix_out_g"], W["w_out"])
    G["w_out"] = _tn(yn, g1, D_MODEL, D_MODEL, bm=512, bn=512, name="tn_out")
    dzc, y2, da1, da2, dy, arai, dlr, dli, dd = _ssm_bwd(dyc, A["z"], A["xr"], A["xi"], sp["bbd"], sp["lr"], sp["li"], sp["cbd"],
                                                        S["ssm_d"], W["glu_w1"], W["glu_w2"])
    G["glu_w1"] = _tn(y2, da1, C_WIDTH, C_WIDTH, bm=256, bn=256, name="tn_glu1")
    G["glu_w2"] = _tn(y2, da2, C_WIDTH, C_WIDTH, bm=256, bn=256, name="tn_glu2")
    dbbd = _tn(A["z"], arai, C_WIDTH, 2 * N_STATE, bm=256, bn=512, a_off=5, name="tn_bbd")
    dcr = _tn(dy, A["xr"], C_WIDTH, N_STATE, bm=256, bn=512, name="tn_cre")
    dci = _tn(dy, A["xi"], C_WIDTH, N_STATE, bm=256, bn=512, name="tn_cim")
    dare, daim, dldt, dbtr, dbti, dcre, dcim = _ssm_param_bwd(
        raw["are"], raw["aim"], raw["ldt"], raw["are_x"], raw["aim_x"], raw["ldt_x"], raw["btr"], raw["bti"],
        dlr.reshape(C_GROUPS, C_STATE), dli.reshape(C_GROUPS, C_STATE), dbbd, dcr, dci)
    G["ssm_a_re"], G["ssm_a_im"], G["ssm_log_dt"] = dare, daim, dldt[:, 0]
    G["ssm_b_re"] = dbtr.reshape(C_GROUPS, C_GROUP, C_STATE).transpose(0, 2, 1)
    G["ssm_b_im"] = dbti.reshape(C_GROUPS, C_GROUP, C_STATE).transpose(0, 2, 1)
    G["ssm_c_re"] = dcre.reshape(C_GROUPS, C_GROUP, C_STATE)
    G["ssm_c_im"] = dcim.reshape(C_GROUPS, C_GROUP, C_STATE)
    G["ssm_d"] = dd.reshape(C_GROUPS, C_GROUP)
    dzq, dzk, dzv, dqg, dkg, dsk = _attn_bwd(A["z"], cs, sn, S["qg"], S["kg"], S["sinks"], A["yb"], dyb)
    G["q_norm_g"] = dqg[0, :HEAD_DIM] + dqg[0, HEAD_DIM:]
    G["k_norm_g"] = dkg[0, :HEAD_DIM] + dkg[0, HEAD_DIM:]
    G["sinks"] = dsk[:, 0]
    dza, dws, dbs, dlng, dlnb = _gmlp_bwd(A["z"], dya, S["lng"], S["lnb"], S["gmlp_ws"], S["bsx"])
    G["gmlp_ws"] = dws
    G["gmlp_bs"] = dbs[:, :, 0]
    G["gmlp_ln_g"] = dlng.reshape(A_HEADS, 2, HEAD_DIM)[:, 1]
    G["gmlp_ln_b"] = dlnb.reshape(A_HEADS, 2, HEAD_DIM)[:, 1]
    g0, xn, dz, G["attn_norm_g"] = _inproj_bwd(g1, A["h"], S["attn_norm_g"], W["w_in"], dza, dzq, dzk, dzv, dzc)
    G["w_in"] = _col_major(_tn(xn, dz, D_MODEL, IN_COLS, bm=512, bn=512, name="tn_in"))
    return g0, G


def _small_layouts(P, l):
    def row(a):
        return a.reshape(1, -1)

    zeros = jnp.zeros((A_HEADS, HEAD_DIM), F32)
    S = dict(
        attn_norm_g=row(P["attn_norm_g"][l]), mix_out_g=row(P["mix_out_g"][l]), mlp_norm_g=row(P["mlp_norm_g"][l]),
        ple_norm_g=row(P["ple_norm_g"][l]),
        lng=jnp.stack([zeros, P["gmlp_ln_g"][l]], axis=1).reshape(1, 512),
        lnb=jnp.stack([zeros, P["gmlp_ln_b"][l]], axis=1).reshape(1, 512),
        gmlp_ws=P["gmlp_ws"][l],
        bsx=jnp.broadcast_to(P["gmlp_bs"][l][:, :, None], (A_HEADS, CHUNK, CHUNK)),
        qg=jnp.tile(P["q_norm_g"][l], 2).reshape(1, LANES), kg=jnp.tile(P["k_norm_g"][l], 2).reshape(1, LANES),
        sinks=jnp.broadcast_to(P["sinks"][l][:, None], (8, LANES)),
        ssm_d=row(P["ssm_d"][l]),
    )
    are, aim = P["ssm_a_re"][l], P["ssm_a_im"][l]
    ldt = jnp.broadcast_to(P["ssm_log_dt"][l][:, None], (C_GROUPS, C_STATE))
    raw = dict(
        are=are, aim=aim, ldt=ldt,
        are_x=jnp.repeat(are, C_GROUP, axis=0), aim_x=jnp.repeat(aim, C_GROUP, axis=0), ldt_x=jnp.repeat(ldt, C_GROUP, axis=0),
        btr=P["ssm_b_re"][l].transpose(0, 2, 1).reshape(C_WIDTH, C_STATE), bti=P["ssm_b_im"][l].transpose(0, 2, 1).reshape(C_WIDTH, C_STATE),
        cre=P["ssm_c_re"][l].reshape(C_WIDTH, C_STATE), cim=P["ssm_c_im"][l].reshape(C_WIDTH, C_STATE),
    )
    return S, raw


def _ssm_prep_layer(raw):
    bbd, cbd, lr, li = _ssm_prep(raw["are"], raw["aim"], raw["ldt"], raw["are_x"], raw["aim_x"], raw["ldt_x"],
                                 raw["btr"], raw["bti"], raw["cre"], raw["cim"])
    return dict(bbd=bbd, cbd=cbd, lr=lr.reshape(1, N_STATE), li=li.reshape(1, N_STATE))


def _local_step(x, p, positions, target, P, Wfull):
    inv = 1.0 / (ROPE_THETA ** (jnp.arange(0, HEAD_DIM, 2, dtype=F32) / HEAD_DIM))
    cs, sn = _rope_tables(positions.reshape(-1, 1), jnp.tile(inv, 4).reshape(1, LANES))
    h = x
    acts, smalls = [], []
    for l in range(DEPTH):
        S, raw = _small_layouts(P, l)
        sp = _ssm_prep_layer(raw)
        h, A = _layer_fwd(h, p[l], cs, sn, Wfull[l], S, sp)
        acts.append(A)
        smalls.append((S, raw, sp))
    g, lsum = _loss_head(h, target)
    grads = [None] * DEPTH
    for l in reversed(range(DEPTH)):
        S, raw, sp = smalls[l]
        g, grads[l] = _layer_bwd(g, p[l], cs, sn, Wfull[l], S, sp, acts[l], raw)
    return lsum[0, 0], g, grads


_SHARD_SHAPES = dict(w_in=(DEPTH, D_MODEL, IN_COLS // N_DEV), glu_w1=(DEPTH, C_WIDTH // N_DEV, C_WIDTH), glu_w2=(DEPTH, C_WIDTH // N_DEV, C_WIDTH),
                     w_out=(DEPTH, D_MODEL // N_DEV, D_MODEL), w_ff1=(DEPTH, D_MODEL, FF_BLOCK), w_ff2=(DEPTH, FF_BLOCK, D_MODEL),
                     w_ple_gate=(DEPTH, D_MODEL // N_DEV, D_MODEL), w_ple_proj=(DEPTH, PLE_DIM, D_MODEL // N_DEV))


def _assemble_weights(gathered):
    out = []
    for l in range(DEPTH):
        g = {k: v[:, l] for k, v in gathered.items()}
        out.append(dict(
            w_in=_from_col_major(g["w_in"]), glu_w1=g["glu_w1"].reshape(C_WIDTH, C_WIDTH), glu_w2=g["glu_w2"].reshape(C_WIDTH, C_WIDTH),
            w_out=g["w_out"].reshape(D_MODEL, D_MODEL), w_ff1=g["w_ff1"], w_ff2=g["w_ff2"].reshape(D_FF, D_MODEL),
            w_ple_gate=g["w_ple_gate"].reshape(D_MODEL, D_MODEL), w_ple_proj=_from_col_major(g["w_ple_proj"])))
    return out


def _grad_slots(grads):
    def stack(name):
        return jnp.stack([grads[l][name] for l in range(DEPTH)], axis=1)

    out = {}
    for name in ("w_in", "w_ff1", "w_ple_proj"):
        out[name] = stack(name)
    for name in ("glu_w1", "glu_w2", "w_out", "w_ff2", "w_ple_gate"):
        out[name] = jnp.stack([grads[l][name].reshape((N_DEV, -1) + grads[l][name].shape[1:]) for l in range(DEPTH)], axis=1)
    return out


def kernel(x, p, positions, attn_norm_g, w_in, gmlp_ln_g, gmlp_ln_b, gmlp_ws, gmlp_bs, q_norm_g, k_norm_g, sinks, ssm_a_re, ssm_a_im, ssm_log_dt, ssm_b_re, ssm_b_im, ssm_c_re, ssm_c_im, ssm_d, glu_w1, glu_w2, mix_out_g, w_out, mlp_norm_g, w_ff1, w_ff2, ple_norm_g, w_ple_gate, w_ple_proj, loss_target, m_attn_norm_g, m_w_in, m_gmlp_ln_g, m_gmlp_ln_b, m_gmlp_ws, m_gmlp_bs, m_q_norm_g, m_k_norm_g, m_sinks, m_ssm_a_re, m_ssm_a_im, m_ssm_log_dt, m_ssm_b_re, m_ssm_b_im, m_ssm_c_re, m_ssm_c_im, m_ssm_d, m_glu_w1, m_glu_w2, m_mix_out_g, m_w_out, m_mlp_norm_g, m_w_ff1, m_w_ff2, m_ple_norm_g, m_w_ple_gate, m_w_ple_proj, v_attn_norm_g, v_w_in, v_gmlp_ln_g, v_gmlp_ln_b, v_gmlp_ws, v_gmlp_bs, v_q_norm_g, v_k_norm_g, v_sinks, v_ssm_a_re, v_ssm_a_im, v_ssm_log_dt, v_ssm_b_re, v_ssm_b_im, v_ssm_c_re, v_ssm_c_im, v_ssm_d, v_glu_w1, v_glu_w2, v_mix_out_g, v_w_out, v_mlp_norm_g, v_w_ff1, v_w_ff2, v_ple_norm_g, v_w_ple_gate, v_w_ple_proj):
    env = dict(locals())
    P = {n: env[n] for n in WEIGHTS}
    M = {n: env["m_" + n] for n in WEIGHTS}
    V = {n: env["v_" + n] for n in WEIGHTS}
    return _step(x, p, positions, loss_target, P, M, V)


def _step(x, p, positions, loss_target, P, M, V):
    shard_shapes = [_SHARD_SHAPES[n] for n in SHARDED]
    small_shapes = [P[n].shape for n in SMALL]
    wflat = _pack([P[n] for n in SHARDED], WIRE)
    gathered, _ = _exchange(wflat, jnp.zeros((8, LANES), F32), gather_big=True, name="gather_weights")
    gathered = dict(zip(SHARDED, _unpack(gathered, shard_shapes, lead=(N_DEV,))))
    Wfull = _assemble_weights(gathered)
    lsum, gx, grads = _local_step(x[0], p[:, 0], positions[0], loss_target[0], P, Wfull)
    slots = _grad_slots(grads)
    gflat = jnp.stack([_pack([slots[n][e] for n in SHARDED], WIRE) for e in range(N_DEV)])
    sflat = _pack([jnp.stack([grads[l][n] for l in range(DEPTH)]) for n in SMALL], F32)
    land_b, land_s = _exchange(gflat, sflat, gather_big=False, name="exchange_grads")
    gb = dict(zip(SHARDED, _unpack(_sum_slots(land_b), shard_shapes)))
    gs = dict(zip(SMALL, _unpack(_sum_slots(land_s), small_shapes)))
    G = {**gb, **gs}
    delta, new_m, new_v = {}, {}, {}
    for n in SHARDED:
        shp = P[n].shape
        d2, m2, v2 = _adamw(*(a.reshape(-1, shp[-1]) for a in (P[n], G[n], M[n], V[n])))
        delta[n], new_m[n], new_v[n] = d2.reshape(shp), m2.reshape(shp), v2.reshape(shp)
    d2, m2, v2 = _adamw(_pack([P[n] for n in SMALL], F32), _pack([G[n] for n in SMALL], F32),
                        _pack([M[n] for n in SMALL], F32), _pack([V[n] for n in SMALL], F32))
    for dst, flat in ((delta, d2), (new_m, m2), (new_v, v2)):
        dst.update(zip(SMALL, _unpack(flat, small_shapes)))
    loss = lax.psum(lsum, ("x", "y", "c"))
    return (loss, gx[None], *[G[n] for n in WEIGHTS], *[delta[n] for n in WEIGHTS], *[new_m[n] for n in WEIGHTS], *[new_v[n] for n in WEIGHTS])
```

```python
import functools
import math

import jax
import jax.numpy as jnp
from jax import lax
from jax.experimental import pallas as pl
from jax.experimental.pallas import tpu as pltpu

F32 = jnp.float32
MXU = jnp.bfloat16
WIRE = jnp.bfloat16

D_MODEL = 1024
DEPTH = 4
HEAD_DIM = 64
A_WIDTH = 256
A_HEADS = 4
CHUNK = 128
B_WIDTH = 512
WINDOW = 128
C_WIDTH = 256
C_GROUP = 16
C_GROUPS = 16
C_STATE = 64
N_STATE = C_GROUPS * C_STATE
IN_COLS = 1536
D_FF = 4096
PLE_DIM = 256
EPS = 1e-6
ROPE_THETA = 10000.0
SCALE = HEAD_DIM ** -0.5
NEG = -1e30
N_DEV = 8

ADAM_LR = 0.001
ADAM_B1 = 0.9
ADAM_B2 = 0.999
ADAM_EPS = 1e-08
ADAM_WD = 0.01
ADAM_STEP = 10

V7X_VMEM_BYTES = 64 * 2 ** 20
VMEM_LIMIT = V7X_VMEM_BYTES - 8 * 2 ** 20
LANES = 128

MESH = pl.DeviceIdType.MESH


def _cp(*sem):
    return pltpu.CompilerParams(dimension_semantics=sem, vmem_limit_bytes=VMEM_LIMIT)


def _sds(shape, dtype=F32):
    return jax.ShapeDtypeStruct(shape, dtype)


def _mm(a, b):
    return jnp.dot(a.astype(MXU), b.astype(MXU), preferred_element_type=F32)


def _mm_nt(a, b):
    return lax.dot_general(a.astype(MXU), b.astype(MXU), (((1,), (1,)), ((), ())), preferred_element_type=F32)


def _mm_tn(a, b):
    return lax.dot_general(a.astype(MXU), b.astype(MXU), (((0,), (0,)), ((), ())), preferred_element_type=F32)


def _lane(shape):
    return lax.broadcasted_iota(jnp.int32, shape, len(shape) - 1)


def _row(shape):
    return lax.broadcasted_iota(jnp.int32, shape, 0)


_GELU_C = math.sqrt(2.0 / math.pi)


def _gelu(x):
    return 0.5 * x * (1.0 + jnp.tanh(_GELU_C * (x + 0.044715 * (x * x * x))))


def _gelu_grad(x):
    t = jnp.tanh(_GELU_C * (x + 0.044715 * (x * x * x)))
    return 0.5 * (1.0 + t) + 0.5 * x * (1.0 - t * t) * (_GELU_C * (1.0 + 3.0 * 0.044715 * (x * x)))


def _sigmoid(x):
    return 1.0 / (1.0 + jnp.exp(-x))


def _rms_stat(x):
    return lax.rsqrt(jnp.mean(x * x, axis=-1, keepdims=True) + EPS)


def _rms_bwd(x, r, g, dy):
    xh = x * r
    dxh = dy * g
    dx = r * (dxh - xh * jnp.mean(dxh * xh, axis=-1, keepdims=True))
    return dx, jnp.sum(dy * xh, axis=0, keepdims=True)


def _tril(w):
    return jnp.where(_row(w.shape) >= _lane(w.shape), w, 0.0)


def _swap64(x):
    return pltpu.roll(x, HEAD_DIM, 1)


def _group_sum64(x, lo):
    s_lo = jnp.sum(jnp.where(lo, x, 0.0), axis=-1, keepdims=True)
    s_hi = jnp.sum(jnp.where(lo, 0.0, x), axis=-1, keepdims=True)
    return jnp.where(lo, s_lo, s_hi)


def _partner(x):
    n = x.shape[-1]
    first = (_lane(x.shape) % HEAD_DIM) < HEAD_DIM // 2
    return jnp.where(first, pltpu.roll(x, n - HEAD_DIM // 2, 1), pltpu.roll(x, HEAD_DIM // 2, 1))


def _rope(y, cs, sn):
    return y * cs + _partner(y) * sn


def _rope_bwd(d, cs, sn):
    return d * cs + _partner(d * sn)


def _qk_norm_rope(x, g, cs, sn):
    lo = _lane(x.shape) < HEAD_DIM
    r = lax.rsqrt(_group_sum64(x * x, lo) * (1.0 / HEAD_DIM) + EPS)
    xh = x * r
    return _rope(xh * g, cs, sn), xh, r


def _qk_norm_rope_bwd(xh, r, g, cs, sn, d):
    lo = _lane(xh.shape) < HEAD_DIM
    dy = _rope_bwd(d, cs, sn)
    dxh = dy * g
    m = _group_sum64(dxh * xh, lo) * (1.0 / HEAD_DIM)
    return r * (dxh - xh * m), jnp.sum(dy * xh, axis=0, keepdims=True)


def _gmlp_head(blk, g, b):
    hi = _lane(blk.shape) >= HEAD_DIM
    mu = jnp.sum(jnp.where(hi, blk, 0.0), axis=-1, keepdims=True) * (1.0 / HEAD_DIM)
    xc = jnp.where(hi, blk - mu, 0.0)
    rstd = lax.rsqrt(jnp.sum(xc * xc, axis=-1, keepdims=True) * (1.0 / HEAD_DIM) + EPS)
    vhat = xc * rstd
    return vhat * g + b, vhat, rstd


def _rope_tables(pos_col, inv_row):
    T = pos_col.shape[0]
    tm = min(T, 1024)

    def body(p_ref, inv_ref, cs_ref, sn_ref):
        ang = p_ref[...].astype(F32) * inv_ref[...]
        s = jnp.sin(ang)
        cs_ref[...] = jnp.cos(ang)
        sn_ref[...] = jnp.where((_lane(ang.shape) % HEAD_DIM) < HEAD_DIM // 2, -s, s)

    blk = pl.BlockSpec((tm, LANES), lambda i: (i, 0))
    return pl.pallas_call(
        body, name="rope_tables", grid=(T // tm,),
        in_specs=[pl.BlockSpec((tm, 1), lambda i: (i, 0)), pl.BlockSpec((1, LANES), lambda i: (0, 0))],
        out_specs=[blk, blk], out_shape=[_sds((T, LANES))] * 2, compiler_params=_cp("parallel"))(pos_col, inv_row)


def _inproj_fwd(h, g, w):
    T = h.shape[0]
    tm = min(T, 512)

    def body(h_ref, g_ref, w_ref, z_ref):
        x = h_ref[...]
        z_ref[...] = _mm(x * _rms_stat(x) * g_ref[...], w_ref[...])

    return pl.pallas_call(
        body, name="inproj_fwd", grid=(T // tm,),
        in_specs=[pl.BlockSpec((tm, D_MODEL), lambda i: (i, 0)), pl.BlockSpec((1, D_MODEL), lambda i: (0, 0)),
                  pl.BlockSpec((D_MODEL, IN_COLS), lambda i: (0, 0))],
        out_specs=pl.BlockSpec((tm, IN_COLS), lambda i: (i, 0)), out_shape=_sds((T, IN_COLS)),
        compiler_params=_cp("parallel"))(h, g, w)


def _inproj_bwd(gres, h, g, w, dza, dzq, dzk, dzv, dzc):
    T = h.shape[0]
    tm = min(T, 512)

    def body(gr_ref, h_ref, g_ref, w_ref, a_ref, q_ref, k_ref, v_ref, c_ref, dh_ref, xn_ref, dz_ref, dg_ref):
        @pl.when(pl.program_id(0) == 0)
        def _():
            dg_ref[...] = jnp.zeros_like(dg_ref)

        x = h_ref[...]
        r = _rms_stat(x)
        gg = g_ref[...]
        dz = jnp.concatenate([a_ref[...], q_ref[...], k_ref[...], v_ref[...], c_ref[...]], axis=1)
        dxn = _mm_nt(dz, w_ref[...])
        dx, dg = _rms_bwd(x, r, gg, dxn)
        dh_ref[...] = gr_ref[...] + dx
        dg_ref[...] += dg
        xn_ref[...] = (x * r * gg).astype(MXU)
        dz_ref[...] = dz.astype(MXU)

    def rows(w_):
        return pl.BlockSpec((tm, w_), lambda i: (i, 0))

    row = pl.BlockSpec((1, D_MODEL), lambda i: (0, 0))
    return pl.pallas_call(
        body, name="inproj_bwd", grid=(T // tm,),
        in_specs=[rows(D_MODEL), rows(D_MODEL), row, pl.BlockSpec((D_MODEL, IN_COLS), lambda i: (0, 0)),
                  rows(512), rows(512), rows(128), rows(128), rows(256)],
        out_specs=[rows(D_MODEL), rows(D_MODEL), rows(IN_COLS), row],
        out_shape=[_sds((T, D_MODEL)), _sds((T, D_MODEL), MXU), _sds((T, IN_COLS), MXU), _sds((1, D_MODEL))],
        compiler_params=_cp("arbitrary"))(gres, h, g, w, dza, dzq, dzk, dzv, dzc)


def _gmlp_fwd(z, lng, lnb, ws, bsx):
    T = z.shape[0]
    tm = min(T, 512)
    nc = tm // CHUNK

    def body(z_ref, g_ref, b_ref, w_ref, bs_ref, ya_ref):
        zg = _gelu(z_ref[...])
        lo = _lane((tm, LANES)) < HEAD_DIM
        prods = []
        for hd in range(A_HEADS):
            sl = slice(hd * LANES, (hd + 1) * LANES)
            blk = zg[:, sl]
            vn, _, _ = _gmlp_head(blk, g_ref[:, sl], b_ref[:, sl])
            wm = _tril(w_ref[hd])
            sv = jnp.concatenate([_mm(wm, vn[c * CHUNK:(c + 1) * CHUNK]) + bs_ref[hd] for c in range(nc)], axis=0)
            prods.append(blk * _swap64(sv))
        ya_ref[:, 0:LANES] = jnp.where(lo, prods[0], _swap64(prods[1]))
        ya_ref[:, LANES:2 * LANES] = jnp.where(lo, prods[2], _swap64(prods[3]))

    row = pl.BlockSpec((1, 512), lambda i: (0, 0))
    mat = pl.BlockSpec((A_HEADS, CHUNK, CHUNK), lambda i: (0, 0, 0))
    return pl.pallas_call(
        body, name="gmlp_fwd", grid=(T // tm,),
        in_specs=[pl.BlockSpec((tm, 512), lambda i: (i, 0)), row, row, mat, mat],
        out_specs=pl.BlockSpec((tm, A_WIDTH), lambda i: (i, 0)), out_shape=_sds((T, A_WIDTH)),
        compiler_params=_cp("parallel"))(z, lng, lnb, ws, bsx)


def _gmlp_bwd(z, dya, lng, lnb, ws, bsx):
    T = z.shape[0]
    tm = min(T, 512)
    nc = tm // CHUNK

    def body(z_ref, dya_ref, g_ref, b_ref, w_ref, bs_ref, dza_ref, dw_ref, dbs_ref, dg_ref, db_ref):
        @pl.when(pl.program_id(0) == 0)
        def _():
            dw_ref[...] = jnp.zeros_like(dw_ref)
            dbs_ref[...] = jnp.zeros_like(dbs_ref)
            dg_ref[...] = jnp.zeros_like(dg_ref)
            db_ref[...] = jnp.zeros_like(db_ref)

        za = z_ref[...]
        zg = _gelu(za)
        gp = _gelu_grad(za)
        lo = _lane((tm, LANES)) < HEAD_DIM
        for hd in range(A_HEADS):
            sl = slice(hd * LANES, (hd + 1) * LANES)
            blk = zg[:, sl]
            g = g_ref[:, sl]
            vn, vhat, rstd = _gmlp_head(blk, g, b_ref[:, sl])
            wm = _tril(w_ref[hd])
            pair = dya_ref[:, (hd // 2) * LANES:(hd // 2 + 1) * LANES]
            dy = jnp.where(lo, pair if hd % 2 == 0 else _swap64(pair), 0.0)
            dsv = _swap64(dy * blk)
            svs, dvns = [], []
            dw = jnp.zeros((CHUNK, CHUNK), F32)
            dbs = jnp.zeros((CHUNK, 1), F32)
            for c in range(nc):
                cs = slice(c * CHUNK, (c + 1) * CHUNK)
                svs.append(_mm(wm, vn[cs]) + bs_ref[hd])
                dw = dw + _mm_nt(dsv[cs], vn[cs])
                dbs = dbs + jnp.sum(dsv[cs], axis=-1, keepdims=True)
                dvns.append(_mm_tn(wm, dsv[cs]))
            sv = jnp.concatenate(svs, axis=0)
            dvn = jnp.concatenate(dvns, axis=0)
            dw_ref[hd] += _tril(dw)
            dbs_ref[hd] += jnp.broadcast_to(dbs, (CHUNK, CHUNK))
            dg_ref[:, sl] += jnp.sum(dvn * vhat, axis=0, keepdims=True)
            db_ref[:, sl] += jnp.sum(dvn, axis=0, keepdims=True)
            du = dy * _swap64(sv)
            dvh = dvn * g
            m1 = jnp.sum(dvh, axis=-1, keepdims=True) * (1.0 / HEAD_DIM)
            m2 = jnp.sum(dvh * vhat, axis=-1, keepdims=True) * (1.0 / HEAD_DIM)
            dv = jnp.where(lo, 0.0, rstd * (dvh - m1 - vhat * m2))
            dza_ref[:, sl] = (du + dv) * gp[:, sl]

    row = pl.BlockSpec((1, 512), lambda i: (0, 0))
    mat = pl.BlockSpec((A_HEADS, CHUNK, CHUNK), lambda i: (0, 0, 0))
    return pl.pallas_call(
        body, name="gmlp_bwd", grid=(T // tm,),
        in_specs=[pl.BlockSpec((tm, 512), lambda i: (i, 0)), pl.BlockSpec((tm, A_WIDTH), lambda i: (i, 0)), row, row, mat, mat],
        out_specs=[pl.BlockSpec((tm, 512), lambda i: (i, 0)), mat, mat, row, row],
        out_shape=[_sds((T, 512)), _sds((A_HEADS, CHUNK, CHUNK)), _sds((A_HEADS, CHUNK, CHUNK)), _sds((1, 512)), _sds((1, 512))],
        compiler_params=_cp("arbitrary"))(z, dya, lng, lnb, ws, bsx)


def _attn_specs(T, tq, tile_of):
    nb = tq // WINDOW

    def prev(i):
        return jnp.maximum(tile_of(i) * nb - 1, 0)

    row = pl.BlockSpec((1, LANES), lambda i: (0, 0))
    return [
        pl.BlockSpec((tq, B_WIDTH), lambda i: (tile_of(i), 1)),
        pl.BlockSpec((tq, LANES), lambda i: (tile_of(i), 8)),
        pl.BlockSpec((tq, LANES), lambda i: (tile_of(i), 9)),
        pl.BlockSpec((WINDOW, LANES), lambda i: (prev(i), 8)),
        pl.BlockSpec((WINDOW, LANES), lambda i: (prev(i), 9)),
        pl.BlockSpec((tq, LANES), lambda i: (tile_of(i), 0)),
        pl.BlockSpec((tq, LANES), lambda i: (tile_of(i), 0)),
        pl.BlockSpec((WINDOW, LANES), lambda i: (prev(i), 0)),
        pl.BlockSpec((WINDOW, LANES), lambda i: (prev(i), 0)),
        row, row,
        pl.BlockSpec((8, LANES), lambda i: (0, 0)),
    ]


def _attn_mask(first):
    qi = lax.broadcasted_iota(jnp.int32, (WINDOW, 2 * WINDOW), 0)
    kj = lax.broadcasted_iota(jnp.int32, (WINDOW, 2 * WINDOW), 1)
    diff = qi + WINDOW - kj
    band = (diff >= 0) & (diff < WINDOW)
    return band & ((kj >= WINDOW) | jnp.logical_not(first))


def _dup_heads(x, lo):
    sw = _swap64(x)
    return jnp.where(lo, x, sw), jnp.where(lo, sw, x)


def _attn_probs(qm, kw, mask, sink):
    s = jnp.where(mask, _mm_nt(qm, kw) * SCALE, NEG)
    m = jnp.maximum(jnp.max(s, axis=-1, keepdims=True), sink)
    p = jnp.exp(s - m)
    es = jnp.exp(sink - m)
    den = jnp.sum(p, axis=-1, keepdims=True) + es
    return p / den, es / den


def _attn_fwd(z, cs, sn, qg, kg, sinks):
    T = z.shape[0]
    tq = min(T, 512)
    nb = tq // WINDOW

    def body(q_ref, k_ref, v_ref, kp_ref, vp_ref, cs_ref, sn_ref, csp_ref, snp_ref, qg_ref, kg_ref, sk_ref, o_ref):
        i = pl.program_id(0)
        csq, snq = cs_ref[...], sn_ref[...]
        cs_all = jnp.concatenate([csp_ref[...], csq], axis=0)
        sn_all = jnp.concatenate([snp_ref[...], snq], axis=0)
        k_all = jnp.concatenate([kp_ref[...], k_ref[...]], axis=0)
        v_all = jnp.concatenate([vp_ref[...], v_ref[...]], axis=0)
        kr, _, _ = _qk_norm_rope(k_all, kg_ref[...], cs_all, sn_all)
        lo_all = _lane(k_all.shape) < HEAD_DIM
        kd = _dup_heads(kr, lo_all)
        vd = _dup_heads(v_all, lo_all)
        lo = _lane((WINDOW, LANES)) < HEAD_DIM
        for pr in range(4):
            g = pr // 2
            qr, _, _ = _qk_norm_rope(q_ref[:, pr * LANES:(pr + 1) * LANES], qg_ref[...], csq, snq)
            outs = []
            for b in range(nb):
                mask = _attn_mask(i * nb + b == 0)
                qb = qr[b * WINDOW:(b + 1) * WINDOW]
                kw = kd[g][b * WINDOW:(b + 2) * WINDOW]
                vw = vd[g][b * WINDOW:(b + 2) * WINDOW]
                halves = []
                for hh in range(2):
                    sel = lo if hh == 0 else jnp.logical_not(lo)
                    a = 2 * pr + hh
                    pn, _ = _attn_probs(jnp.where(sel, qb, 0.0), kw, mask, sk_ref[a:a + 1, 0:1])
                    halves.append(_mm(pn, vw))
                outs.append(jnp.where(lo, halves[0], halves[1]))
            o_ref[:, pr * LANES:(pr + 1) * LANES] = jnp.concatenate(outs, axis=0)

    return pl.pallas_call(
        body, name="attn_fwd", grid=(T // tq,),
        in_specs=_attn_specs(T, tq, lambda i: i),
        out_specs=pl.BlockSpec((tq, B_WIDTH), lambda i: (i, 0)), out_shape=_sds((T, B_WIDTH)),
        compiler_params=_cp("parallel"))(z, z, z, z, z, cs, sn, cs, sn, qg, kg, sinks)


def _attn_bwd(z, cs, sn, qg, kg, sinks, o, do):
    T = z.shape[0]
    tq = min(T, 512)
    nb = tq // WINDOW
    nt = T // tq
    tk = tq + WINDOW

    def tile_of(i):
        return nt - 1 - i

    def body(q_ref, k_ref, v_ref, kp_ref, vp_ref, cs_ref, sn_ref, csp_ref, snp_ref, qg_ref, kg_ref, sk_ref, o_ref, do_ref,
             dq_ref, dk_ref, dv_ref, dqg_ref, dkg_ref, dsk_ref, acck, accv, ck, cv):
        i = pl.program_id(0)
        ti = nt - 1 - i

        @pl.when(i == 0)
        def _():
            dqg_ref[...] = jnp.zeros_like(dqg_ref)
            dkg_ref[...] = jnp.zeros_like(dkg_ref)
            dsk_ref[...] = jnp.zeros_like(dsk_ref)
            ck[...] = jnp.zeros_like(ck)
            cv[...] = jnp.zeros_like(cv)

        csq, snq = cs_ref[...], sn_ref[...]
        cs_all = jnp.concatenate([csp_ref[...], csq], axis=0)
        sn_all = jnp.concatenate([snp_ref[...], snq], axis=0)
        k_all = jnp.concatenate([kp_ref[...], k_ref[...]], axis=0)
        v_all = jnp.concatenate([vp_ref[...], v_ref[...]], axis=0)
        kr, kh, rk = _qk_norm_rope(k_all, kg_ref[...], cs_all, sn_all)
        lo_all = _lane(k_all.shape) < HEAD_DIM
        kd = _dup_heads(kr, lo_all)
        vd = _dup_heads(v_all, lo_all)
        lo = _lane((WINDOW, LANES)) < HEAD_DIM
        acck[...] = jnp.zeros_like(acck)
        accv[...] = jnp.zeros_like(accv)
        for pr in range(4):
            g = pr // 2
            psl = slice(pr * LANES, (pr + 1) * LANES)
            qr, qh, rq = _qk_norm_rope(q_ref[:, psl], qg_ref[...], csq, snq)
            dqs = []
            for b in range(nb):
                mask = _attn_mask(ti * nb + b == 0)
                bs = slice(b * WINDOW, (b + 1) * WINDOW)
                ws = slice(b * WINDOW, (b + 2) * WINDOW)
                qb = qr[bs]
                kw = kd[g][ws]
                vw = vd[g][ws]
                ob = o_ref[bs, psl]
                dob = do_ref[bs, psl]
                halves = []
                for hh in range(2):
                    sel = lo if hh == 0 else jnp.logical_not(lo)
                    a = 2 * pr + hh
                    qm = jnp.where(sel, qb, 0.0)
                    pn, psink = _attn_probs(qm, kw, mask, sk_ref[a:a + 1, 0:1])
                    dom = jnp.where(sel, dob, 0.0)
                    delta = jnp.sum(dom * ob, axis=-1, keepdims=True)
                    ds = pn * (_mm_nt(dom, vw) - delta)
                    dsk_ref[a:a + 1, :] += jnp.zeros((1, LANES), F32) - jnp.sum(psink * delta)
                    halves.append(_mm(ds, kw) * SCALE)
                    acck[g, ws, :] += _mm_tn(ds, qm) * SCALE
                    accv[g, ws, :] += _mm_tn(pn, dom)
                dqs.append(jnp.where(lo, halves[0], halves[1]))
            dx, dg = _qk_norm_rope_bwd(qh, rq, qg_ref[...], csq, snq, jnp.concatenate(dqs, axis=0))
            dq_ref[:, psl] = dx
            dqg_ref[...] += dg

        def fold(acc):
            f0 = acc[0] + _swap64(acc[0])
            f1 = acc[1] + _swap64(acc[1])
            return jnp.where(lo_all, f0, f1)

        dk_all = fold(acck)
        dv_all = fold(accv)
        pad = jnp.zeros((tq - WINDOW, LANES), F32)
        dk_own = dk_all[WINDOW:] + (jnp.concatenate([pad, ck[...]], axis=0) if nb > 1 else ck[...])
        dv_own = dv_all[WINDOW:] + (jnp.concatenate([pad, cv[...]], axis=0) if nb > 1 else cv[...])
        ck[...] = dk_all[:WINDOW]
        cv[...] = dv_all[:WINDOW]
        dxk, dgk = _qk_norm_rope_bwd(kh[WINDOW:], rk[WINDOW:], kg_ref[...], csq, snq, dk_own)
        dk_ref[...] = dxk
        dkg_ref[...] += dgk
        dv_ref[...] = dv_own

    row = pl.BlockSpec((1, LANES), lambda i: (0, 0))
    return pl.pallas_call(
        body, name="attn_bwd", grid=(nt,),
        in_specs=_attn_specs(T, tq, tile_of) + [pl.BlockSpec((tq, B_WIDTH), lambda i: (tile_of(i), 0))] * 2,
        out_specs=[pl.BlockSpec((tq, B_WIDTH), lambda i: (tile_of(i), 0)), pl.BlockSpec((tq, LANES), lambda i: (tile_of(i), 0)),
                   pl.BlockSpec((tq, LANES), lambda i: (tile_of(i), 0)), row, row, pl.BlockSpec((8, LANES), lambda i: (0, 0))],
        out_shape=[_sds((T, B_WIDTH)), _sds((T, LANES)), _sds((T, LANES)), _sds((1, LANES)), _sds((1, LANES)), _sds((8, LANES))],
        scratch_shapes=[pltpu.VMEM((2, tk, LANES), F32), pltpu.VMEM((2, tk, LANES), F32),
                        pltpu.VMEM((WINDOW, LANES), F32), pltpu.VMEM((WINDOW, LANES), F32)],
        compiler_params=_cp("arbitrary"))(z, z, z, z, z, cs, sn, cs, sn, qg, kg, sinks, o, do)


def _bbar_t(are, aim, ldt, btr, bti):
    lbr, lbi = _lam_bar(are, aim, ldt)
    den = are * are + aim * aim
    nr = lbr - 1.0
    cr = (nr * are + lbi * aim) / den
    ci = (lbi * are - nr * aim) / den
    return cr * btr - ci * bti, cr * bti + ci * btr


def _lam_bar(are, aim, ldt):
    dt = jnp.exp(ldt)
    er = jnp.exp(are * dt)
    return er * jnp.cos(aim * dt), er * jnp.sin(aim * dt)


def _block_diag(x):
    t = jnp.concatenate([x] * C_GROUPS, axis=1)
    return jnp.where(_row(t.shape) // C_GROUP == _lane(t.shape) // C_STATE, t, 0.0)


def _block_diag_fold(m):
    rg = _row((C_WIDTH, C_STATE)) // C_GROUP
    acc = jnp.zeros((C_WIDTH, C_STATE), F32)
    for g in range(C_GROUPS):
        acc = acc + jnp.where(rg == g, m[:, g * C_STATE:(g + 1) * C_STATE], 0.0)
    return acc


def _ssm_prep(are, aim, ldt, are_x, aim_x, ldt_x, btr, bti, cre, cim):
    def body(are_r, aim_r, ldt_r, arex_r, aimx_r, ldtx_r, btr_r, bti_r, cre_r, cim_r, bbd_ref, cbd_ref, lr_ref, li_ref):
        lr, li = _lam_bar(are_r[...], aim_r[...], ldt_r[...])
        lr_ref[...] = lr
        li_ref[...] = li
        br, bi = _bbar_t(arex_r[...], aimx_r[...], ldtx_r[...], btr_r[...], bti_r[...])
        bbd_ref[...] = jnp.concatenate([_block_diag(br), _block_diag(bi)], axis=1).astype(MXU)
        cbd_ref[...] = jnp.concatenate([_block_diag(cre_r[...]), -_block_diag(cim_r[...])], axis=1).astype(MXU)

    return pl.pallas_call(
        body, name="ssm_prep",
        out_shape=[_sds((C_WIDTH, 2 * N_STATE), MXU), _sds((C_WIDTH, 2 * N_STATE), MXU), _sds((C_GROUPS, C_STATE)), _sds((C_GROUPS, C_STATE))],
        compiler_params=pltpu.CompilerParams(vmem_limit_bytes=VMEM_LIMIT))(are, aim, ldt, are_x, aim_x, ldt_x, btr, bti, cre, cim)


def _ssm_param_bwd(are, aim, ldt, are_x, aim_x, ldt_x, btr, bti, dlr, dli, dbbd, dcr, dci):
    def body(are_r, aim_r, ldt_r, arex_r, aimx_r, ldtx_r, btr_r, bti_r, dlr_r, dli_r, dbbd_r, dcr_r, dci_r,
             dare_ref, daim_ref, dldt_ref, dbtr_ref, dbti_ref, dcre_ref, dcim_ref):
        _, vjp_l = jax.vjp(_lam_bar, are_r[...], aim_r[...], ldt_r[...])
        da1, di1, dl1 = vjp_l((dlr_r[...], dli_r[...]))
        dbr = _block_diag_fold(dbbd_r[:, 0:N_STATE])
        dbi = _block_diag_fold(dbbd_r[:, N_STATE:2 * N_STATE])
        _, vjp_b = jax.vjp(_bbar_t, arex_r[...], aimx_r[...], ldtx_r[...], btr_r[...], bti_r[...])
        da2, di2, dl2, dbtr, dbti = vjp_b((dbr, dbi))

        def gsum(x):
            return x.reshape(C_GROUPS, C_GROUP, C_STATE).sum(axis=1)

        dare_ref[...] = da1 + gsum(da2)
        daim_ref[...] = di1 + gsum(di2)
        dldt_ref[...] = jnp.broadcast_to(jnp.sum(dl1 + gsum(dl2), axis=-1, keepdims=True), (C_GROUPS, LANES))
        dbtr_ref[...] = dbtr
        dbti_ref[...] = dbti
        dcre_ref[...] = _block_diag_fold(dcr_r[...])
        dcim_ref[...] = -_block_diag_fold(dci_r[...])

    g = _sds((C_GROUPS, C_STATE))
    x = _sds((C_WIDTH, C_STATE))
    return pl.pallas_call(
        body, name="ssm_param_bwd", out_shape=[g, g, _sds((C_GROUPS, LANES)), x, x, x, x],
        compiler_params=pltpu.CompilerParams(vmem_limit_bytes=VMEM_LIMIT))(are, aim, ldt, are_x, aim_x, ldt_x, btr, bti, dlr, dli, dbbd, dcr, dci)


def _ssm_fwd(z, bbd, lr, li, cbd, dsk, w1, w2):
    T = z.shape[0]
    tt = min(T, 256)

    def body(u_ref, bbd_ref, lr_ref, li_ref, cbd_ref, d_ref, w1_ref, w2_ref, yc_ref, xr_ref, xi_ref, sr, si):
        @pl.when(pl.program_id(0) == 0)
        def _():
            sr[...] = jnp.zeros_like(sr)
            si[...] = jnp.zeros_like(si)

        u = u_ref[...]
        bu = _mm(u, bbd_ref[...])
        xr_ref[...] = bu[:, 0:N_STATE]
        xi_ref[...] = bu[:, N_STATE:2 * N_STATE]
        lam_r, lam_i = lr_ref[...], li_ref[...]

        def step(t, c):
            cr, ci = c
            nr = lam_r * cr - lam_i * ci + xr_ref[pl.ds(t, 1), :]
            ni = lam_r * ci + lam_i * cr + xi_ref[pl.ds(t, 1), :]
            xr_ref[pl.ds(t, 1), :] = nr
            xi_ref[pl.ds(t, 1), :] = ni
            return nr, ni

        cr, ci = lax.fori_loop(0, tt, step, (sr[...], si[...]), unroll=8)
        sr[...] = cr
        si[...] = ci
        x = jnp.concatenate([xr_ref[...], xi_ref[...]], axis=1)
        y2 = _gelu(_mm_nt(x, cbd_ref[...]) + d_ref[...] * u)
        yc_ref[...] = _mm(y2, w1_ref[...]) * _sigmoid(_mm(y2, w2_ref[...]))

    big = pl.BlockSpec((C_WIDTH, 2 * N_STATE), lambda i: (0, 0))
    srow = pl.BlockSpec((1, N_STATE), lambda i: (0, 0))
    wsp = pl.BlockSpec((C_WIDTH, C_WIDTH), lambda i: (0, 0))
    xs = pl.BlockSpec((tt, N_STATE), lambda i: (i, 0))
    return pl.pallas_call(
        body, name="ssm_fwd", grid=(T // tt,),
        in_specs=[pl.BlockSpec((tt, C_WIDTH), lambda i: (i, 5)), big, srow, srow, big, pl.BlockSpec((1, C_WIDTH), lambda i: (0, 0)), wsp, wsp],
        out_specs=[pl.BlockSpec((tt, C_WIDTH), lambda i: (i, 0)), xs, xs],
        out_shape=[_sds((T, C_WIDTH)), _sds((T, N_STATE)), _sds((T, N_STATE))],
        scratch_shapes=[pltpu.VMEM((1, N_STATE), F32)] * 2,
        compiler_params=_cp("arbitrary"))(z, bbd, lr, li, cbd, dsk, w1, w2)


def _ssm_bwd(dyc, z, xr, xi, bbd, lr, li, cbd, dsk, w1, w2):
    T = z.shape[0]
    tt = min(T, 256)
    nt = T // tt

    def tile_of(i):
        return nt - 1 - i

    def body(dyc_ref, u_ref, xr_ref, xi_ref, xpr_ref, xpi_ref, bbd_ref, lr_ref, li_ref, cbd_ref, d_ref, w1_ref, w2_ref,
             du_ref, y2_ref, da1_ref, da2_ref, dy_ref, arai_ref, dlr_ref, dli_ref, dd_ref, gr, gi, sr, si):
        i = pl.program_id(0)
        ti = nt - 1 - i

        @pl.when(i == 0)
        def _():
            sr[...] = jnp.zeros_like(sr)
            si[...] = jnp.zeros_like(si)
            dlr_ref[...] = jnp.zeros_like(dlr_ref)
            dli_ref[...] = jnp.zeros_like(dli_ref)
            dd_ref[...] = jnp.zeros_like(dd_ref)

        u = u_ref[...]
        xr_t, xi_t = xr_ref[...], xi_ref[...]
        y = _mm_nt(jnp.concatenate([xr_t, xi_t], axis=1), cbd_ref[...]) + d_ref[...] * u
        y2 = _gelu(y)
        a1 = _mm(y2, w1_ref[...])
        sg = _sigmoid(_mm(y2, w2_ref[...]))
        dyc_t = dyc_ref[...]
        da1 = dyc_t * sg
        da2 = dyc_t * a1 * sg * (1.0 - sg)
        dy = (_mm_nt(da1, w1_ref[...]) + _mm_nt(da2, w2_ref[...])) * _gelu_grad(y)
        gx = _mm(dy, cbd_ref[...])
        gr[...] = gx[:, 0:N_STATE]
        gi[...] = gx[:, N_STATE:2 * N_STATE]
        lam_r, lam_i = lr_ref[...], li_ref[...]

        def step(k, c):
            t = tt - 1 - k
            cr, ci = c
            nr = gr[pl.ds(t, 1), :] + lam_r * cr + lam_i * ci
            ni = gi[pl.ds(t, 1), :] - lam_i * cr + lam_r * ci
            gr[pl.ds(t, 1), :] = nr
            gi[pl.ds(t, 1), :] = ni
            return nr, ni

        cr, ci = lax.fori_loop(0, tt, step, (sr[...], si[...]), unroll=8)
        sr[...] = cr
        si[...] = ci
        ar, ai = gr[...], gi[...]
        first_row = _row(ar.shape) == 0
        live = jnp.where(ti > 0, 1.0, 0.0)
        xsr = jnp.where(first_row, xpr_ref[7:8, :] * live, pltpu.roll(xr_t, 1, 0))
        xsi = jnp.where(first_row, xpi_ref[7:8, :] * live, pltpu.roll(xi_t, 1, 0))
        dlr_ref[...] += jnp.sum(ar * xsr + ai * xsi, axis=0, keepdims=True)
        dli_ref[...] += jnp.sum(ai * xsr - ar * xsi, axis=0, keepdims=True)
        dd_ref[...] += jnp.sum(dy * u, axis=0, keepdims=True)
        arai = jnp.concatenate([ar, ai], axis=1)
        du_ref[...] = _mm_nt(arai, bbd_ref[...]) + d_ref[...] * dy
        y2_ref[...] = y2.astype(MXU)
        da1_ref[...] = da1.astype(MXU)
        da2_ref[...] = da2.astype(MXU)
        dy_ref[...] = dy.astype(MXU)
        arai_ref[...] = arai.astype(MXU)

    def prev(i):
        return jnp.maximum(tile_of(i) * (tt // 8) - 1, 0)

    big = pl.BlockSpec((C_WIDTH, 2 * N_STATE), lambda i: (0, 0))
    srow = pl.BlockSpec((1, N_STATE), lambda i: (0, 0))
    wsp = pl.BlockSpec((C_WIDTH, C_WIDTH), lambda i: (0, 0))
    xs = pl.BlockSpec((tt, N_STATE), lambda i: (tile_of(i), 0))
    xp = pl.BlockSpec((8, N_STATE), lambda i: (prev(i), 0))
    cw = pl.BlockSpec((tt, C_WIDTH), lambda i: (tile_of(i), 0))
    drow = pl.BlockSpec((1, C_WIDTH), lambda i: (0, 0))
    return pl.pallas_call(
        body, name="ssm_bwd", grid=(nt,),
        in_specs=[cw, pl.BlockSpec((tt, C_WIDTH), lambda i: (tile_of(i), 5)), xs, xs, xp, xp, big, srow, srow, big, drow, wsp, wsp],
        out_specs=[cw, cw, cw, cw, cw, pl.BlockSpec((tt, 2 * N_STATE), lambda i: (tile_of(i), 0)), srow, srow, drow],
        out_shape=[_sds((T, C_WIDTH))] + [_sds((T, C_WIDTH), MXU)] * 4 + [_sds((T, 2 * N_STATE), MXU), _sds((1, N_STATE)), _sds((1, N_STATE)), _sds((1, C_WIDTH))],
        scratch_shapes=[pltpu.VMEM((tt, N_STATE), F32)] * 2 + [pltpu.VMEM((1, N_STATE), F32)] * 2,
        compiler_params=_cp("arbitrary"))(dyc, z, xr, xi, xr, xi, bbd, lr, li, cbd, dsk, w1, w2)


_GROUPS = ((0, A_WIDTH), (A_WIDTH, A_WIDTH + B_WIDTH), (A_WIDTH + B_WIDTH, D_MODEL))


def _merge_fwd(h, ya, yb, yc, g, w):
    T = h.shape[0]
    tm = min(T, 512)

    def body(h_ref, a_ref, b_ref, c_ref, g_ref, w_ref, o_ref):
        yn = jnp.concatenate([y * _rms_stat(y) for y in (a_ref[...], b_ref[...], c_ref[...])], axis=1) * g_ref[...]
        o_ref[...] = h_ref[...] + _mm(yn, w_ref[...])

    def rows(w_):
        return pl.BlockSpec((tm, w_), lambda i: (i, 0))

    return pl.pallas_call(
        body, name="merge_fwd", grid=(T // tm,),
        in_specs=[rows(D_MODEL), rows(A_WIDTH), rows(B_WIDTH), rows(C_WIDTH), pl.BlockSpec((1, D_MODEL), lambda i: (0, 0)),
                  pl.BlockSpec((D_MODEL, D_MODEL), lambda i: (0, 0))],
        out_specs=rows(D_MODEL), out_shape=_sds((T, D_MODEL)), compiler_params=_cp("parallel"))(h, ya, yb, yc, g, w)


def _merge_bwd(gres, ya, yb, yc, g, w):
    T = gres.shape[0]
    tm = min(T, 512)

    def body(gr_ref, a_ref, b_ref, c_ref, g_ref, w_ref, da_ref, db_ref, dc_ref, yn_ref, dg_ref):
        @pl.when(pl.program_id(0) == 0)
        def _():
            dg_ref[...] = jnp.zeros_like(dg_ref)

        dyn = _mm_nt(gr_ref[...], w_ref[...])
        yns, dgs = [], []
        for (c0, c1), y_ref, d_ref in zip(_GROUPS, (a_ref, b_ref, c_ref), (da_ref, db_ref, dc_ref)):
            y = y_ref[...]
            r = _rms_stat(y)
            gg = g_ref[:, c0:c1]
            dx, dg = _rms_bwd(y, r, gg, dyn[:, c0:c1])
            d_ref[...] = dx
            dgs.append(dg)
            yns.append(y * r * gg)
        dg_ref[...] += jnp.concatenate(dgs, axis=1)
        yn_ref[...] = jnp.concatenate(yns, axis=1).astype(MXU)

    def rows(w_):
        return pl.BlockSpec((tm, w_), lambda i: (i, 0))

    row = pl.BlockSpec((1, D_MODEL), lambda i: (0, 0))
    return pl.pallas_call(
        body, name="merge_bwd", grid=(T // tm,),
        in_specs=[rows(D_MODEL), rows(A_WIDTH), rows(B_WIDTH), rows(C_WIDTH), row, pl.BlockSpec((D_MODEL, D_MODEL), lambda i: (0, 0))],
        out_specs=[rows(A_WIDTH), rows(B_WIDTH), rows(C_WIDTH), rows(D_MODEL), row],
        out_shape=[_sds((T, A_WIDTH)), _sds((T, B_WIDTH)), _sds((T, C_WIDTH)), _sds((T, D_MODEL), MXU), _sds((1, D_MODEL))],
        compiler_params=_cp("arbitrary"))(gres, ya, yb, yc, g, w)


FF_BLOCK = D_FF // N_DEV


def _mlp_fwd(h, g, w1, w2):
    T = h.shape[0]
    tm = min(T, 1024)

    def body(h_ref, g_ref, w1_ref, w2_ref, o_ref, hn):
        @pl.when(pl.program_id(1) == 0)
        def _():
            x = h_ref[...]
            hn[...] = (x * _rms_stat(x) * g_ref[...]).astype(MXU)
            o_ref[...] = x

        a = jnp.maximum(_mm(hn[...], w1_ref[...]), 0.0)
        o_ref[...] += _mm(a * a, w2_ref[...])

    return pl.pallas_call(
        body, name="mlp_fwd", grid=(T // tm, N_DEV),
        in_specs=[pl.BlockSpec((tm, D_MODEL), lambda i, j: (i, 0)), pl.BlockSpec((1, D_MODEL), lambda i, j: (0, 0)),
                  pl.BlockSpec((None, D_MODEL, FF_BLOCK), lambda i, j: (j, 0, 0)), pl.BlockSpec((FF_BLOCK, D_MODEL), lambda i, j: (j, 0))],
        out_specs=pl.BlockSpec((tm, D_MODEL), lambda i, j: (i, 0)), out_shape=_sds((T, D_MODEL)),
        scratch_shapes=[pltpu.VMEM((tm, D_MODEL), MXU)],
        compiler_params=_cp("parallel", "arbitrary"))(h, g, w1, w2)


def _mlp_bwd(gres, h, g, w1, w2):
    T = h.shape[0]
    tm = min(T, 1024)

    def body(gr_ref, h_ref, g_ref, w1_ref, w2_ref, dh_ref, hn_ref, da_ref, r_ref, dg_ref, acc):
        i, j = pl.program_id(0), pl.program_id(1)

        @pl.when((i == 0) & (j == 0))
        def _():
            dg_ref[...] = jnp.zeros_like(dg_ref)

        @pl.when(j == 0)
        def _():
            x = h_ref[...]
            hn_ref[...] = (x * _rms_stat(x) * g_ref[...]).astype(MXU)
            acc[...] = jnp.zeros_like(acc)

        a = jnp.maximum(_mm(hn_ref[...], w1_ref[...]), 0.0)
        da = _mm_nt(gr_ref[...], w2_ref[...]) * (2.0 * a)
        acc[...] += _mm_nt(da, w1_ref[...])
        da_ref[...] = da.astype(MXU)
        r_ref[...] = (a * a).astype(MXU)

        @pl.when(j == N_DEV - 1)
        def _():
            x = h_ref[...]
            dx, dg = _rms_bwd(x, _rms_stat(x), g_ref[...], acc[...])
            dh_ref[...] = gr_ref[...] + dx
            dg_ref[...] += dg

    rows = pl.BlockSpec((tm, D_MODEL), lambda i, j: (i, 0))
    row = pl.BlockSpec((1, D_MODEL), lambda i, j: (0, 0))
    ffb = pl.BlockSpec((tm, FF_BLOCK), lambda i, j: (i, j))
    return pl.pallas_call(
        body, name="mlp_bwd", grid=(T // tm, N_DEV),
        in_specs=[rows, rows, row, pl.BlockSpec((None, D_MODEL, FF_BLOCK), lambda i, j: (j, 0, 0)),
                  pl.BlockSpec((FF_BLOCK, D_MODEL), lambda i, j: (j, 0))],
        out_specs=[rows, rows, ffb, ffb, row],
        out_shape=[_sds((T, D_MODEL)), _sds((T, D_MODEL), MXU), _sds((T, D_FF), MXU), _sds((T, D_FF), MXU), _sds((1, D_MODEL))],
        scratch_shapes=[pltpu.VMEM((tm, D_MODEL), F32)],
        compiler_params=_cp("arbitrary", "arbitrary"))(gres, h, g, w1, w2)


def _ple_fwd(h, p, g, wg, wp):
    T = h.shape[0]
    tm = min(T, 512)

    def body(h_ref, p_ref, g_ref, wg_ref, wp_ref, o_ref):
        x = h_ref[...]
        gate = _sigmoid(_mm(x * _rms_stat(x) * g_ref[...], wg_ref[...]))
        o_ref[...] = x + gate * _mm(p_ref[...], wp_ref[...])

    rows = pl.BlockSpec((tm, D_MODEL), lambda i: (i, 0))
    return pl.pallas_call(
        body, name="ple_fwd", grid=(T // tm,),
        in_specs=[rows, pl.BlockSpec((tm, PLE_DIM), lambda i: (i, 0)), pl.BlockSpec((1, D_MODEL), lambda i: (0, 0)),
                  pl.BlockSpec((D_MODEL, D_MODEL), lambda i: (0, 0)), pl.BlockSpec((PLE_DIM, D_MODEL), lambda i: (0, 0))],
        out_specs=rows, out_shape=_sds((T, D_MODEL)), compiler_params=_cp("parallel"))(h, p, g, wg, wp)


def _ple_bwd(gres, h, p, g, wg, wp):
    T = h.shape[0]
    tm = min(T, 512)

    def body(gr_ref, h_ref, p_ref, g_ref, wg_ref, wp_ref, dh_ref, hn_ref, dgp_ref, de_ref, dg_ref):
        @pl.when(pl.program_id(0) == 0)
        def _():
            dg_ref[...] = jnp.zeros_like(dg_ref)

        x = h_ref[...]
        r = _rms_stat(x)
        gg = g_ref[...]
        hn = x * r * gg
        gate = _sigmoid(_mm(hn, wg_ref[...]))
        e = _mm(p_ref[...], wp_ref[...])
        gr = gr_ref[...]
        dgp = gr * e * gate * (1.0 - gate)
        dx, dg = _rms_bwd(x, r, gg, _mm_nt(dgp, wg_ref[...]))
        dh_ref[...] = gr + dx
        dg_ref[...] += dg
        hn_ref[...] = hn.astype(MXU)
        dgp_ref[...] = dgp.astype(MXU)
        de_ref[...] = (gr * gate).astype(MXU)

    rows = pl.BlockSpec((tm, D_MODEL), lambda i: (i, 0))
    row = pl.BlockSpec((1, D_MODEL), lambda i: (0, 0))
    return pl.pallas_call(
        body, name="ple_bwd", grid=(T // tm,),
        in_specs=[rows, rows, pl.BlockSpec((tm, PLE_DIM), lambda i: (i, 0)), row,
                  pl.BlockSpec((D_MODEL, D_MODEL), lambda i: (0, 0)), pl.BlockSpec((PLE_DIM, D_MODEL), lambda i: (0, 0))],
        out_specs=[rows, rows, rows, rows, row],
        out_shape=[_sds((T, D_MODEL))] + [_sds((T, D_MODEL), MXU)] * 3 + [_sds((1, D_MODEL))],
        compiler_params=_cp("arbitrary"))(gres, h, p, g, wg, wp)


def _loss_head(h, target):
    T = h.shape[0]
    tm = min(T, 1024)

    def body(h_ref, t_ref, dh_ref, l_ref):
        @pl.when(pl.program_id(0) == 0)
        def _():
            l_ref[...] = jnp.zeros_like(l_ref)

        e = h_ref[...] - t_ref[...]
        dh_ref[...] = e * (1.0 / D_MODEL)
        l_ref[...] += jnp.zeros_like(l_ref) + 0.5 * jnp.sum(jnp.mean(e * e, axis=-1, keepdims=True))

    rows = pl.BlockSpec((tm, D_MODEL), lambda i: (i, 0))
    return pl.pallas_call(
        body, name="loss_head", grid=(T // tm,), in_specs=[rows, rows],
        out_specs=[rows, pl.BlockSpec((8, LANES), lambda i: (0, 0))], out_shape=[_sds((T, D_MODEL)), _sds((8, LANES))],
        compiler_params=_cp("arbitrary"))(h, target)


def _tn(a, b, m, n, *, bm, bn, a_off=0, b_off=0, n_major=False, dtype=F32, name="tn"):
    T = a.shape[0]
    tk = min(T, 1024)
    nk = T // tk
    assert m % bm == 0 and n % bn == 0 and (not n_major or bm == m)

    def body(a_ref, b_ref, o_ref, acc):
        k = pl.program_id(2)

        @pl.when(k == 0)
        def _():
            acc[...] = jnp.zeros_like(acc)

        acc[...] += _mm_tn(a_ref[...], b_ref[...])

        @pl.when(k == nk - 1)
        def _():
            o_ref[...] = acc[...].astype(dtype)

    if n_major:
        out_spec = pl.BlockSpec((None, bm, bn), lambda i, j, k: (j, 0, 0))
        out_shape = _sds((n // bn, m, bn), dtype)
    else:
        out_spec = pl.BlockSpec((bm, bn), lambda i, j, k: (i, j))
        out_shape = _sds((m, n), dtype)
    return pl.pallas_call(
        body, name=name, grid=(m // bm, n // bn, nk),
        in_specs=[pl.BlockSpec((tk, bm), lambda i, j, k: (k, a_off + i)), pl.BlockSpec((tk, bn), lambda i, j, k: (k, b_off + j))],
        out_specs=out_spec, out_shape=out_shape, scratch_shapes=[pltpu.VMEM((bm, bn), F32)],
        compiler_params=_cp("parallel", "parallel", "arbitrary"))(a, b)


def _sum_adamw(land, w, m, v):
    R, C = w.shape
    tr = R
    for cand in (512, 256, 128, 64, 32, 16, 8):
        if R % cand == 0 and cand * C * 4 <= 2 ** 20:
            tr = cand
            break

    def body(l_ref, w_ref, m_ref, v_ref, g_ref, d_ref, nm_ref, nv_ref):
        gg = l_ref[0].astype(F32)
        for s in range(1, N_DEV):
            gg = gg + l_ref[s].astype(F32)
        g_ref[...] = gg
        nm = ADAM_B1 * m_ref[...] + (1.0 - ADAM_B1) * gg
        nv = ADAM_B2 * v_ref[...] + (1.0 - ADAM_B2) * (gg * gg)
        m_hat = nm / (1.0 - ADAM_B1 ** ADAM_STEP)
        v_hat = nv / (1.0 - ADAM_B2 ** ADAM_STEP)
        d_ref[...] = -ADAM_LR * (m_hat / (jnp.sqrt(v_hat) + ADAM_EPS) + ADAM_WD * w_ref[...])
        nm_ref[...] = nm
        nv_ref[...] = nv

    blk = pl.BlockSpec((tr, C), lambda i: (i, 0))
    return pl.pallas_call(
        body, name="sum_adamw", grid=(R // tr,), in_specs=[pl.BlockSpec((N_DEV, tr, C), lambda i: (0, i, 0))] + [blk] * 3,
        out_specs=[blk] * 4, out_shape=[_sds((R, C))] * 4, compiler_params=_cp("parallel"))(land, w, m, v)


def _all_to_all(pairs, name):
    n = len(pairs)

    def body(*refs):
        srcs, lands = refs[:n], refs[2 * n:3 * n]
        send, recv, loc = refs[3 * n:]
        x, y, c = lax.axis_index("x"), lax.axis_index("y"), lax.axis_index("c")
        me = 4 * x + 2 * y + c
        own = [pltpu.make_async_copy(pairs[t][2](srcs[t], me, me), pairs[t][3](lands[t], me), loc.at[t]) for t in range(n)]
        for cp in own:
            cp.start()
        sends, recvs = [], []
        for k in range(1, N_DEV):
            px, py, pc = x ^ (k >> 2), y ^ ((k >> 1) & 1), c ^ (k & 1)
            peer = 4 * px + 2 * py + pc
            for t in range(n):
                src = pairs[t][2](srcs[t], me, peer)
                cp = pltpu.make_async_remote_copy(src_ref=src, dst_ref=pairs[t][3](lands[t], me), send_sem=send.at[t, k],
                                                  recv_sem=recv.at[t, k], device_id=(px, py, pc), device_id_type=MESH)
                cp.start()
                sends.append(cp)
                recvs.append(pltpu.make_async_remote_copy(src_ref=src, dst_ref=pairs[t][3](lands[t], peer), send_sem=send.at[t, k],
                                                          recv_sem=recv.at[t, k], device_id=(px, py, pc), device_id_type=MESH))
        for cp in recvs:
            cp.wait_recv()
        for cp in sends:
            cp.wait_send()
        for cp in own:
            cp.wait()

    anyspec = pl.BlockSpec(memory_space=pl.ANY)
    lands = [pr[1] for pr in pairs]
    return pl.pallas_call(
        body, name=name, in_specs=[anyspec] * (2 * n), out_specs=[anyspec] * n,
        out_shape=[_sds(a.shape, a.dtype) for a in lands], input_output_aliases={n + t: t for t in range(n)},
        scratch_shapes=[pltpu.SemaphoreType.DMA((n, N_DEV)), pltpu.SemaphoreType.DMA((n, N_DEV)), pltpu.SemaphoreType.DMA((n,))],
        compiler_params=pltpu.CompilerParams(has_side_effects=True))(*[pr[0] for pr in pairs], *lands)


def _gather_layer(shards, l):
    pairs = [(s, lax.empty((N_DEV,) + s.shape[1:], s.dtype), lambda ref, me, peer: ref.at[l], lambda ref, sender: ref.at[sender])
             for s in shards]
    return _all_to_all(pairs, f"gather_weights_{l}")


def _scatter_layer(parts, lands, l, whole=()):
    pairs = [(s, ld, lambda ref, me, peer: ref.at[peer], lambda ref, sender: ref.at[sender, l]) for s, ld in zip(parts, lands)]
    pairs += [(s, ld, lambda ref, me, peer: ref, lambda ref, sender: ref.at[sender]) for s, ld in whole]
    return _all_to_all(pairs, f"exchange_grads_{l}")


SHARDED = ("w_in", "glu_w1", "glu_w2", "w_out", "w_ff1", "w_ff2", "w_ple_gate", "w_ple_proj")
SMALL = ("attn_norm_g", "gmlp_ln_g", "gmlp_ln_b", "gmlp_ws", "gmlp_bs", "q_norm_g", "k_norm_g", "sinks", "ssm_a_re", "ssm_a_im",
         "ssm_log_dt", "ssm_b_re", "ssm_b_im", "ssm_c_re", "ssm_c_im", "ssm_d", "mix_out_g", "mlp_norm_g", "ple_norm_g")
WEIGHTS = ("attn_norm_g", "w_in", "gmlp_ln_g", "gmlp_ln_b", "gmlp_ws", "gmlp_bs", "q_norm_g", "k_norm_g", "sinks", "ssm_a_re", "ssm_a_im",
           "ssm_log_dt", "ssm_b_re", "ssm_b_im", "ssm_c_re", "ssm_c_im", "ssm_d", "glu_w1", "glu_w2", "mix_out_g", "w_out", "mlp_norm_g",
           "w_ff1", "w_ff2", "ple_norm_g", "w_ple_gate", "w_ple_proj")
FLAT_COLS = 1024


def _pack(arrs, dtype):
    flat = jnp.concatenate([a.astype(dtype).reshape(-1) for a in arrs])
    pad = (-flat.shape[0]) % (8 * FLAT_COLS)
    if pad:
        flat = jnp.concatenate([flat, jnp.zeros((pad,), dtype)])
    return flat.reshape(-1, FLAT_COLS)


def _unpack(flat, shapes, lead=()):
    flat = flat.reshape(lead + (-1,))
    out, off = [], 0
    for s in shapes:
        n = math.prod(s)
        out.append(flat[..., off:off + n].reshape(lead + tuple(s)))
        off += n
    return out


def _col_major(w):
    rows, cols = w.shape
    return w.reshape(rows, N_DEV, cols // N_DEV).transpose(1, 0, 2)


def _from_col_major(s):
    n, rows, cs = s.shape
    return s.transpose(1, 0, 2).reshape(rows, n * cs)


def _layer_fwd(h, p_l, cs, sn, W, S, sp):
    z = _inproj_fwd(h, S["attn_norm_g"], W["w_in"])
    ya = _gmlp_fwd(z, S["lng"], S["lnb"], S["gmlp_ws"], S["bsx"])
    yb = _attn_fwd(z, cs, sn, S["qg"], S["kg"], S["sinks"])
    yc, xr, xi = _ssm_fwd(z, sp["bbd"], sp["lr"], sp["li"], sp["cbd"], S["ssm_d"], W["glu_w1"], W["glu_w2"])
    h1 = _merge_fwd(h, ya, yb, yc, S["mix_out_g"], W["w_out"])
    h2 = _mlp_fwd(h1, S["mlp_norm_g"], W["w_ff1"], W["w_ff2"])
    h3 = _ple_fwd(h2, p_l, S["ple_norm_g"], W["w_ple_gate"], W["w_ple_proj"])
    return h3, dict(h=h, z=z, ya=ya, yb=yb, yc=yc, xr=xr, xi=xi, h1=h1, h2=h2)


def _layer_bwd(g3, p_l, cs, sn, W, S, sp, A, raw):
    G = {}
    g2, hn3, dgp, de, G["ple_norm_g"] = _ple_bwd(g3, A["h2"], p_l, S["ple_norm_g"], W["w_ple_gate"], W["w_ple_proj"])
    G["w_ple_gate"] = _tn(hn3, dgp, D_MODEL, D_MODEL, bm=1024, bn=1024, dtype=WIRE, name="tn_gate").reshape(N_DEV, -1, D_MODEL)
    G["w_ple_proj"] = _tn(p_l, de, PLE_DIM, D_MODEL, bm=PLE_DIM, bn=D_MODEL // N_DEV, n_major=True, dtype=WIRE, name="tn_proj")
    g1, hn, da, r, G["mlp_norm_g"] = _mlp_bwd(g2, A["h1"], S["mlp_norm_g"], W["w_ff1"], W["w_ff2"])
    G["w_ff1"] = _tn(hn, da, D_MODEL, D_FF, bm=D_MODEL, bn=FF_BLOCK, n_major=True, dtype=WIRE, name="tn_ff1")
    G["w_ff2"] = _tn(r, g2, D_FF, D_MODEL, bm=1024, bn=1024, dtype=WIRE, name="tn_ff2").reshape(N_DEV, -1, D_MODEL)
    dya, dyb, dyc, yn, G["mix_out_g"] = _merge_bwd(g1, A["ya"], A["yb"], A["yc"], S["mix_out_g"], W["w_out"])
    G["w_out"] = _tn(yn, g1, D_MODEL, D_MODEL, bm=1024, bn=1024, dtype=WIRE, name="tn_out").reshape(N_DEV, -1, D_MODEL)
    dzc, y2, da1, da2, dy, arai, dlr, dli, dd = _ssm_bwd(dyc, A["z"], A["xr"], A["xi"], sp["bbd"], sp["lr"], sp["li"], sp["cbd"],
                                                        S["ssm_d"], W["glu_w1"], W["glu_w2"])
    G["glu_w1"] = _tn(y2, da1, C_WIDTH, C_WIDTH, bm=256, bn=256, dtype=WIRE, name="tn_glu1").reshape(N_DEV, -1, C_WIDTH)
    G["glu_w2"] = _tn(y2, da2, C_WIDTH, C_WIDTH, bm=256, bn=256, dtype=WIRE, name="tn_glu2").reshape(N_DEV, -1, C_WIDTH)
    dbbd = _tn(A["z"], arai, C_WIDTH, 2 * N_STATE, bm=256, bn=512, a_off=5, name="tn_bbd")
    dcr = _tn(dy, A["xr"], C_WIDTH, N_STATE, bm=256, bn=512, name="tn_cre")
    dci = _tn(dy, A["xi"], C_WIDTH, N_STATE, bm=256, bn=512, name="tn_cim")
    dare, daim, dldt, dbtr, dbti, dcre, dcim = _ssm_param_bwd(
        raw["are"], raw["aim"], raw["ldt"], raw["are_x"], raw["aim_x"], raw["ldt_x"], raw["btr"], raw["bti"],
        dlr.reshape(C_GROUPS, C_STATE), dli.reshape(C_GROUPS, C_STATE), dbbd, dcr, dci)
    G["ssm_a_re"], G["ssm_a_im"], G["ssm_log_dt"] = dare, daim, dldt[:, 0]
    G["ssm_b_re"] = dbtr.reshape(C_GROUPS, C_GROUP, C_STATE).transpose(0, 2, 1)
    G["ssm_b_im"] = dbti.reshape(C_GROUPS, C_GROUP, C_STATE).transpose(0, 2, 1)
    G["ssm_c_re"] = dcre.reshape(C_GROUPS, C_GROUP, C_STATE)
    G["ssm_c_im"] = dcim.reshape(C_GROUPS, C_GROUP, C_STATE)
    G["ssm_d"] = dd.reshape(C_GROUPS, C_GROUP)
    dzq, dzk, dzv, dqg, dkg, dsk = _attn_bwd(A["z"], cs, sn, S["qg"], S["kg"], S["sinks"], A["yb"], dyb)
    G["q_norm_g"] = dqg[0, :HEAD_DIM] + dqg[0, HEAD_DIM:]
    G["k_norm_g"] = dkg[0, :HEAD_DIM] + dkg[0, HEAD_DIM:]
    G["sinks"] = dsk[:, 0]
    dza, dws, dbs, dlng, dlnb = _gmlp_bwd(A["z"], dya, S["lng"], S["lnb"], S["gmlp_ws"], S["bsx"])
    G["gmlp_ws"] = dws
    G["gmlp_bs"] = dbs[:, :, 0]
    G["gmlp_ln_g"] = dlng.reshape(A_HEADS, 2, HEAD_DIM)[:, 1]
    G["gmlp_ln_b"] = dlnb.reshape(A_HEADS, 2, HEAD_DIM)[:, 1]
    g0, xn, dz, G["attn_norm_g"] = _inproj_bwd(g1, A["h"], S["attn_norm_g"], W["w_in"], dza, dzq, dzk, dzv, dzc)
    G["w_in"] = _col_major(_tn(xn, dz, D_MODEL, IN_COLS, bm=1024, bn=512, dtype=WIRE, name="tn_in"))
    return g0, G


def _small_layouts(P, l):
    def row(a):
        return a.reshape(1, -1)

    zeros = jnp.zeros((A_HEADS, HEAD_DIM), F32)
    S = dict(
        attn_norm_g=row(P["attn_norm_g"][l]), mix_out_g=row(P["mix_out_g"][l]), mlp_norm_g=row(P["mlp_norm_g"][l]),
        ple_norm_g=row(P["ple_norm_g"][l]),
        lng=jnp.stack([zeros, P["gmlp_ln_g"][l]], axis=1).reshape(1, 512),
        lnb=jnp.stack([zeros, P["gmlp_ln_b"][l]], axis=1).reshape(1, 512),
        gmlp_ws=P["gmlp_ws"][l],
        bsx=jnp.broadcast_to(P["gmlp_bs"][l][:, :, None], (A_HEADS, CHUNK, CHUNK)),
        qg=jnp.tile(P["q_norm_g"][l], 2).reshape(1, LANES), kg=jnp.tile(P["k_norm_g"][l], 2).reshape(1, LANES),
        sinks=jnp.broadcast_to(P["sinks"][l][:, None], (8, LANES)),
        ssm_d=row(P["ssm_d"][l]),
    )
    are, aim = P["ssm_a_re"][l], P["ssm_a_im"][l]
    ldt = jnp.broadcast_to(P["ssm_log_dt"][l][:, None], (C_GROUPS, C_STATE))
    raw = dict(
        are=are, aim=aim, ldt=ldt,
        are_x=jnp.repeat(are, C_GROUP, axis=0), aim_x=jnp.repeat(aim, C_GROUP, axis=0), ldt_x=jnp.repeat(ldt, C_GROUP, axis=0),
        btr=P["ssm_b_re"][l].transpose(0, 2, 1).reshape(C_WIDTH, C_STATE), bti=P["ssm_b_im"][l].transpose(0, 2, 1).reshape(C_WIDTH, C_STATE),
        cre=P["ssm_c_re"][l].reshape(C_WIDTH, C_STATE), cim=P["ssm_c_im"][l].reshape(C_WIDTH, C_STATE),
    )
    return S, raw


def _ssm_prep_layer(raw):
    bbd, cbd, lr, li = _ssm_prep(raw["are"], raw["aim"], raw["ldt"], raw["are_x"], raw["aim_x"], raw["ldt_x"],
                                 raw["btr"], raw["bti"], raw["cre"], raw["cim"])
    return dict(bbd=bbd, cbd=cbd, lr=lr.reshape(1, N_STATE), li=li.reshape(1, N_STATE))


def _local_step(x, p, positions, target, P, Wfull):
    inv = 1.0 / (ROPE_THETA ** (jnp.arange(0, HEAD_DIM, 2, dtype=F32) / HEAD_DIM))
    cs, sn = _rope_tables(positions.reshape(-1, 1), jnp.tile(inv, 4).reshape(1, LANES))
    h = x
    acts, smalls = [], []
    for l in range(DEPTH):
        S, raw = _small_layouts(P, l)
        sp = _ssm_prep_layer(raw)
        h, A = _layer_fwd(h, p[l], cs, sn, Wfull[l], S, sp)
        acts.append(A)
        smalls.append((S, raw, sp))
    g, lsum = _loss_head(h, target)
    grads = [None] * DEPTH
    for l in reversed(range(DEPTH)):
        S, raw, sp = smalls[l]
        g, grads[l] = _layer_bwd(g, p[l], cs, sn, Wfull[l], S, sp, acts[l], raw)
    return lsum[0, 0], g, grads


def _layer_weights(g):
    return dict(
        w_in=_from_col_major(g["w_in"]), glu_w1=g["glu_w1"].reshape(C_WIDTH, C_WIDTH), glu_w2=g["glu_w2"].reshape(C_WIDTH, C_WIDTH),
        w_out=g["w_out"].reshape(D_MODEL, D_MODEL), w_ff1=g["w_ff1"], w_ff2=g["w_ff2"].reshape(D_FF, D_MODEL),
        w_ple_gate=g["w_ple_gate"].reshape(D_MODEL, D_MODEL), w_ple_proj=_from_col_major(g["w_ple_proj"]))


def kernel(x, p, positions, attn_norm_g, w_in, gmlp_ln_g, gmlp_ln_b, gmlp_ws, gmlp_bs, q_norm_g, k_norm_g, sinks, ssm_a_re, ssm_a_im, ssm_log_dt, ssm_b_re, ssm_b_im, ssm_c_re, ssm_c_im, ssm_d, glu_w1, glu_w2, mix_out_g, w_out, mlp_norm_g, w_ff1, w_ff2, ple_norm_g, w_ple_gate, w_ple_proj, loss_target, m_attn_norm_g, m_w_in, m_gmlp_ln_g, m_gmlp_ln_b, m_gmlp_ws, m_gmlp_bs, m_q_norm_g, m_k_norm_g, m_sinks, m_ssm_a_re, m_ssm_a_im, m_ssm_log_dt, m_ssm_b_re, m_ssm_b_im, m_ssm_c_re, m_ssm_c_im, m_ssm_d, m_glu_w1, m_glu_w2, m_mix_out_g, m_w_out, m_mlp_norm_g, m_w_ff1, m_w_ff2, m_ple_norm_g, m_w_ple_gate, m_w_ple_proj, v_attn_norm_g, v_w_in, v_gmlp_ln_g, v_gmlp_ln_b, v_gmlp_ws, v_gmlp_bs, v_q_norm_g, v_k_norm_g, v_sinks, v_ssm_a_re, v_ssm_a_im, v_ssm_log_dt, v_ssm_b_re, v_ssm_b_im, v_ssm_c_re, v_ssm_c_im, v_ssm_d, v_glu_w1, v_glu_w2, v_mix_out_g, v_w_out, v_mlp_norm_g, v_w_ff1, v_w_ff2, v_ple_norm_g, v_w_ple_gate, v_w_ple_proj):
    env = dict(locals())
    P = {n: env[n] for n in WEIGHTS}
    M = {n: env["m_" + n] for n in WEIGHTS}
    V = {n: env["v_" + n] for n in WEIGHTS}
    return _step(x, p, positions, loss_target, P, M, V)


def _step(x, p, positions, loss_target, P, M, V):
    small_shapes = [P[n].shape for n in SMALL]
    shards = [P[n].astype(WIRE) for n in SHARDED]
    Wfull = [_layer_weights(dict(zip(SHARDED, _gather_layer(shards, l)))) for l in range(DEPTH)]
    lsum, gx, grads = _local_step(x[0], p[:, 0], positions[0], loss_target[0], P, Wfull)
    lands = [lax.empty((N_DEV,) + P[n].shape, WIRE) for n in SHARDED]
    sflat = _pack([jnp.stack([grads[l][n] for l in range(DEPTH)]) for n in SMALL], F32)
    for l in reversed(range(DEPTH)):
        whole = [(sflat, lax.empty((N_DEV,) + sflat.shape, F32))] if l == 0 else []
        out = _scatter_layer([grads[l][n] for n in SHARDED], lands, l, whole)
        lands = out[:len(SHARDED)]
    land_s = out[-1]
    G, delta, new_m, new_v = {}, {}, {}, {}
    for n, land in zip(SHARDED, lands):
        shp = P[n].shape
        res = _sum_adamw(land.reshape(N_DEV, -1, shp[-1]), *(a.reshape(-1, shp[-1]) for a in (P[n], M[n], V[n])))
        G[n], delta[n], new_m[n], new_v[n] = (a.reshape(shp) for a in res)
    res = _sum_adamw(land_s, _pack([P[n] for n in SMALL], F32), _pack([M[n] for n in SMALL], F32), _pack([V[n] for n in SMALL], F32))
    for dst, flat in zip((G, delta, new_m, new_v), res):
        dst.update(zip(SMALL, _unpack(flat, small_shapes)))
    loss = lax.psum(lsum, ("x", "y", "c"))
    return (loss, gx[None], *[G[n] for n in WEIGHTS], *[delta[n] for n in WEIGHTS], *[new_m[n] for n in WEIGHTS], *[new_v[n] for n in WEIGHTS])
```

```python
import functools
import math

import jax
import jax.numpy as jnp
from jax import lax
from jax.experimental import pallas as pl
from jax.experimental.pallas import tpu as pltpu

F32 = jnp.float32
MXU = jnp.bfloat16
WIRE = jnp.bfloat16

D_MODEL = 1024
DEPTH = 4
HEAD_DIM = 64
A_WIDTH = 256
A_HEADS = 4
CHUNK = 128
B_WIDTH = 512
WINDOW = 128
C_WIDTH = 256
C_GROUP = 16
C_GROUPS = 16
C_STATE = 64
N_STATE = C_GROUPS * C_STATE
IN_COLS = 1536
D_FF = 4096
PLE_DIM = 256
EPS = 1e-6
ROPE_THETA = 10000.0
SCALE = HEAD_DIM ** -0.5
NEG = -1e30
N_DEV = 8

ADAM_LR = 0.001
ADAM_B1 = 0.9
ADAM_B2 = 0.999
ADAM_EPS = 1e-08
ADAM_WD = 0.01
ADAM_STEP = 10

V7X_VMEM_BYTES = 64 * 2 ** 20
VMEM_LIMIT = V7X_VMEM_BYTES - 8 * 2 ** 20
LANES = 128

MESH = pl.DeviceIdType.MESH


def _cp(*sem):
    return pltpu.CompilerParams(dimension_semantics=sem, vmem_limit_bytes=VMEM_LIMIT)


def _sds(shape, dtype=F32):
    return jax.ShapeDtypeStruct(shape, dtype)


def _mm(a, b):
    return jnp.dot(a.astype(MXU), b.astype(MXU), preferred_element_type=F32)


def _mm_nt(a, b):
    return lax.dot_general(a.astype(MXU), b.astype(MXU), (((1,), (1,)), ((), ())), preferred_element_type=F32)


def _mm_tn(a, b):
    return lax.dot_general(a.astype(MXU), b.astype(MXU), (((0,), (0,)), ((), ())), preferred_element_type=F32)


def _lane(shape):
    return lax.broadcasted_iota(jnp.int32, shape, len(shape) - 1)


def _row(shape):
    return lax.broadcasted_iota(jnp.int32, shape, 0)


_GELU_C = math.sqrt(2.0 / math.pi)


def _gelu(x):
    return 0.5 * x * (1.0 + jnp.tanh(_GELU_C * (x + 0.044715 * (x * x * x))))


def _gelu_grad(x):
    t = jnp.tanh(_GELU_C * (x + 0.044715 * (x * x * x)))
    return 0.5 * (1.0 + t) + 0.5 * x * (1.0 - t * t) * (_GELU_C * (1.0 + 3.0 * 0.044715 * (x * x)))


def _sigmoid(x):
    return 1.0 / (1.0 + jnp.exp(-x))


def _rms_stat(x):
    return lax.rsqrt(jnp.mean(x * x, axis=-1, keepdims=True) + EPS)


def _rms_bwd(x, r, g, dy):
    xh = x * r
    dxh = dy * g
    dx = r * (dxh - xh * jnp.mean(dxh * xh, axis=-1, keepdims=True))
    return dx, jnp.sum(dy * xh, axis=0, keepdims=True)


def _tril(w):
    return jnp.where(_row(w.shape) >= _lane(w.shape), w, 0.0)


def _swap64(x):
    return pltpu.roll(x, HEAD_DIM, 1)


def _group_sum64(x, lo):
    s_lo = jnp.sum(jnp.where(lo, x, 0.0), axis=-1, keepdims=True)
    s_hi = jnp.sum(jnp.where(lo, 0.0, x), axis=-1, keepdims=True)
    return jnp.where(lo, s_lo, s_hi)


def _partner(x):
    n = x.shape[-1]
    first = (_lane(x.shape) % HEAD_DIM) < HEAD_DIM // 2
    return jnp.where(first, pltpu.roll(x, n - HEAD_DIM // 2, 1), pltpu.roll(x, HEAD_DIM // 2, 1))


def _rope(y, cs, sn):
    return y * cs + _partner(y) * sn


def _rope_bwd(d, cs, sn):
    return d * cs + _partner(d * sn)


def _qk_norm_rope(x, g, cs, sn):
    lo = _lane(x.shape) < HEAD_DIM
    r = lax.rsqrt(_group_sum64(x * x, lo) * (1.0 / HEAD_DIM) + EPS)
    xh = x * r
    return _rope(xh * g, cs, sn), xh, r


def _qk_norm_rope_bwd(xh, r, g, cs, sn, d):
    lo = _lane(xh.shape) < HEAD_DIM
    dy = _rope_bwd(d, cs, sn)
    dxh = dy * g
    m = _group_sum64(dxh * xh, lo) * (1.0 / HEAD_DIM)
    return r * (dxh - xh * m), jnp.sum(dy * xh, axis=0, keepdims=True)


def _gmlp_head(blk, g, b):
    hi = _lane(blk.shape) >= HEAD_DIM
    mu = jnp.sum(jnp.where(hi, blk, 0.0), axis=-1, keepdims=True) * (1.0 / HEAD_DIM)
    xc = jnp.where(hi, blk - mu, 0.0)
    rstd = lax.rsqrt(jnp.sum(xc * xc, axis=-1, keepdims=True) * (1.0 / HEAD_DIM) + EPS)
    vhat = xc * rstd
    return vhat * g + b, vhat, rstd


def _rope_tables(pos_col, inv_row):
    T = pos_col.shape[0]
    tm = min(T, 1024)

    def body(p_ref, inv_ref, cs_ref, sn_ref):
        ang = p_ref[...].astype(F32) * inv_ref[...]
        s = jnp.sin(ang)
        cs_ref[...] = jnp.cos(ang)
        sn_ref[...] = jnp.where((_lane(ang.shape) % HEAD_DIM) < HEAD_DIM // 2, -s, s)

    blk = pl.BlockSpec((tm, LANES), lambda i: (i, 0))
    return pl.pallas_call(
        body, name="rope_tables", grid=(T // tm,),
        in_specs=[pl.BlockSpec((tm, 1), lambda i: (i, 0)), pl.BlockSpec((1, LANES), lambda i: (0, 0))],
        out_specs=[blk, blk], out_shape=[_sds((T, LANES))] * 2, compiler_params=_cp("parallel"))(pos_col, inv_row)


def _inproj_fwd(h, g, w):
    T = h.shape[0]
    tm = min(T, 512)

    def body(h_ref, g_ref, w_ref, z_ref):
        x = h_ref[...]
        z_ref[...] = _mm(x * _rms_stat(x) * g_ref[...], w_ref[...])

    return pl.pallas_call(
        body, name="inproj_fwd", grid=(T // tm,),
        in_specs=[pl.BlockSpec((tm, D_MODEL), lambda i: (i, 0)), pl.BlockSpec((1, D_MODEL), lambda i: (0, 0)),
                  pl.BlockSpec((D_MODEL, IN_COLS), lambda i: (0, 0))],
        out_specs=pl.BlockSpec((tm, IN_COLS), lambda i: (i, 0)), out_shape=_sds((T, IN_COLS)),
        compiler_params=_cp("parallel"))(h, g, w)


def _inproj_bwd(gres, h, g, w, dza, dzq, dzk, dzv, dzc):
    T = h.shape[0]
    tm = min(T, 512)

    def body(gr_ref, h_ref, g_ref, w_ref, a_ref, q_ref, k_ref, v_ref, c_ref, dh_ref, xn_ref, dz_ref, dg_ref):
        @pl.when(pl.program_id(0) == 0)
        def _():
            dg_ref[...] = jnp.zeros_like(dg_ref)

        x = h_ref[...]
        r = _rms_stat(x)
        gg = g_ref[...]
        dz = jnp.concatenate([a_ref[...], q_ref[...], k_ref[...], v_ref[...], c_ref[...]], axis=1)
        dxn = _mm_nt(dz, w_ref[...])
        dx, dg = _rms_bwd(x, r, gg, dxn)
        dh_ref[...] = gr_ref[...] + dx
        dg_ref[...] += dg
        xn_ref[...] = (x * r * gg).astype(MXU)
        dz_ref[...] = dz.astype(MXU)

    def rows(w_):
        return pl.BlockSpec((tm, w_), lambda i: (i, 0))

    row = pl.BlockSpec((1, D_MODEL), lambda i: (0, 0))
    return pl.pallas_call(
        body, name="inproj_bwd", grid=(T // tm,),
        in_specs=[rows(D_MODEL), rows(D_MODEL), row, pl.BlockSpec((D_MODEL, IN_COLS), lambda i: (0, 0)),
                  rows(512), rows(512), rows(128), rows(128), rows(256)],
        out_specs=[rows(D_MODEL), rows(D_MODEL), rows(IN_COLS), row],
        out_shape=[_sds((T, D_MODEL)), _sds((T, D_MODEL), MXU), _sds((T, IN_COLS), MXU), _sds((1, D_MODEL))],
        compiler_params=_cp("arbitrary"))(gres, h, g, w, dza, dzq, dzk, dzv, dzc)


def _gmlp_fwd(z, lng, lnb, ws, bsx):
    T = z.shape[0]
    tm = min(T, 512)
    nc = tm // CHUNK

    def body(z_ref, g_ref, b_ref, w_ref, bs_ref, ya_ref):
        zg = _gelu(z_ref[...])
        lo = _lane((tm, LANES)) < HEAD_DIM
        prods = []
        for hd in range(A_HEADS):
            sl = slice(hd * LANES, (hd + 1) * LANES)
            blk = zg[:, sl]
            vn, _, _ = _gmlp_head(blk, g_ref[:, sl], b_ref[:, sl])
            wm = _tril(w_ref[hd])
            sv = jnp.concatenate([_mm(wm, vn[c * CHUNK:(c + 1) * CHUNK]) + bs_ref[hd] for c in range(nc)], axis=0)
            prods.append(blk * _swap64(sv))
        ya_ref[:, 0:LANES] = jnp.where(lo, prods[0], _swap64(prods[1]))
        ya_ref[:, LANES:2 * LANES] = jnp.where(lo, prods[2], _swap64(prods[3]))

    row = pl.BlockSpec((1, 512), lambda i: (0, 0))
    mat = pl.BlockSpec((A_HEADS, CHUNK, CHUNK), lambda i: (0, 0, 0))
    return pl.pallas_call(
        body, name="gmlp_fwd", grid=(T // tm,),
        in_specs=[pl.BlockSpec((tm, 512), lambda i: (i, 0)), row, row, mat, mat],
        out_specs=pl.BlockSpec((tm, A_WIDTH), lambda i: (i, 0)), out_shape=_sds((T, A_WIDTH)),
        compiler_params=_cp("parallel"))(z, lng, lnb, ws, bsx)


def _gmlp_bwd(z, dya, lng, lnb, ws, bsx):
    T = z.shape[0]
    tm = min(T, 512)
    nc = tm // CHUNK

    def body(z_ref, dya_ref, g_ref, b_ref, w_ref, bs_ref, dza_ref, dw_ref, dbs_ref, dg_ref, db_ref):
        @pl.when(pl.program_id(0) == 0)
        def _():
            dw_ref[...] = jnp.zeros_like(dw_ref)
            dbs_ref[...] = jnp.zeros_like(dbs_ref)
            dg_ref[...] = jnp.zeros_like(dg_ref)
            db_ref[...] = jnp.zeros_like(db_ref)

        za = z_ref[...]
        zg = _gelu(za)
        gp = _gelu_grad(za)
        lo = _lane((tm, LANES)) < HEAD_DIM
        for hd in range(A_HEADS):
            sl = slice(hd * LANES, (hd + 1) * LANES)
            blk = zg[:, sl]
            g = g_ref[:, sl]
            vn, vhat, rstd = _gmlp_head(blk, g, b_ref[:, sl])
            wm = _tril(w_ref[hd])
            pair = dya_ref[:, (hd // 2) * LANES:(hd // 2 + 1) * LANES]
            dy = jnp.where(lo, pair if hd % 2 == 0 else _swap64(pair), 0.0)
            dsv = _swap64(dy * blk)
            svs, dvns = [], []
            dw = jnp.zeros((CHUNK, CHUNK), F32)
            dbs = jnp.zeros((CHUNK, 1), F32)
            for c in range(nc):
                cs = slice(c * CHUNK, (c + 1) * CHUNK)
                svs.append(_mm(wm, vn[cs]) + bs_ref[hd])
                dw = dw + _mm_nt(dsv[cs], vn[cs])
                dbs = dbs + jnp.sum(dsv[cs], axis=-1, keepdims=True)
                dvns.append(_mm_tn(wm, dsv[cs]))
            sv = jnp.concatenate(svs, axis=0)
            dvn = jnp.concatenate(dvns, axis=0)
            dw_ref[hd] += _tril(dw)
            dbs_ref[hd] += jnp.broadcast_to(dbs, (CHUNK, CHUNK))
            dg_ref[:, sl] += jnp.sum(dvn * vhat, axis=0, keepdims=True)
            db_ref[:, sl] += jnp.sum(dvn, axis=0, keepdims=True)
            du = dy * _swap64(sv)
            dvh = dvn * g
            m1 = jnp.sum(dvh, axis=-1, keepdims=True) * (1.0 / HEAD_DIM)
            m2 = jnp.sum(dvh * vhat, axis=-1, keepdims=True) * (1.0 / HEAD_DIM)
            dv = jnp.where(lo, 0.0, rstd * (dvh - m1 - vhat * m2))
            dza_ref[:, sl] = (du + dv) * gp[:, sl]

    row = pl.BlockSpec((1, 512), lambda i: (0, 0))
    mat = pl.BlockSpec((A_HEADS, CHUNK, CHUNK), lambda i: (0, 0, 0))
    return pl.pallas_call(
        body, name="gmlp_bwd", grid=(T // tm,),
        in_specs=[pl.BlockSpec((tm, 512), lambda i: (i, 0)), pl.BlockSpec((tm, A_WIDTH), lambda i: (i, 0)), row, row, mat, mat],
        out_specs=[pl.BlockSpec((tm, 512), lambda i: (i, 0)), mat, mat, row, row],
        out_shape=[_sds((T, 512)), _sds((A_HEADS, CHUNK, CHUNK)), _sds((A_HEADS, CHUNK, CHUNK)), _sds((1, 512)), _sds((1, 512))],
        compiler_params=_cp("arbitrary"))(z, dya, lng, lnb, ws, bsx)


def _attn_specs(T, tq, tile_of):
    nb = tq // WINDOW

    def prev(i):
        return jnp.maximum(tile_of(i) * nb - 1, 0)

    row = pl.BlockSpec((1, LANES), lambda i: (0, 0))
    return [
        pl.BlockSpec((tq, B_WIDTH), lambda i: (tile_of(i), 1)),
        pl.BlockSpec((tq, LANES), lambda i: (tile_of(i), 8)),
        pl.BlockSpec((tq, LANES), lambda i: (tile_of(i), 9)),
        pl.BlockSpec((WINDOW, LANES), lambda i: (prev(i), 8)),
        pl.BlockSpec((WINDOW, LANES), lambda i: (prev(i), 9)),
        pl.BlockSpec((tq, LANES), lambda i: (tile_of(i), 0)),
        pl.BlockSpec((tq, LANES), lambda i: (tile_of(i), 0)),
        pl.BlockSpec((WINDOW, LANES), lambda i: (prev(i), 0)),
        pl.BlockSpec((WINDOW, LANES), lambda i: (prev(i), 0)),
        row, row,
        pl.BlockSpec((8, LANES), lambda i: (0, 0)),
    ]


def _attn_mask(first):
    qi = lax.broadcasted_iota(jnp.int32, (WINDOW, 2 * WINDOW), 0)
    kj = lax.broadcasted_iota(jnp.int32, (WINDOW, 2 * WINDOW), 1)
    diff = qi + WINDOW - kj
    band = (diff >= 0) & (diff < WINDOW)
    return band & ((kj >= WINDOW) | jnp.logical_not(first))


def _dup_heads(x, lo):
    sw = _swap64(x)
    return jnp.where(lo, x, sw), jnp.where(lo, sw, x)


def _attn_probs(qm, kw, mask, sink):
    s = jnp.where(mask, _mm_nt(qm, kw) * SCALE, NEG)
    m = jnp.maximum(jnp.max(s, axis=-1, keepdims=True), sink)
    p = jnp.exp(s - m)
    es = jnp.exp(sink - m)
    den = jnp.sum(p, axis=-1, keepdims=True) + es
    return p / den, es / den


def _attn_fwd(z, cs, sn, qg, kg, sinks):
    T = z.shape[0]
    tq = min(T, 512)
    nb = tq // WINDOW

    def body(q_ref, k_ref, v_ref, kp_ref, vp_ref, cs_ref, sn_ref, csp_ref, snp_ref, qg_ref, kg_ref, sk_ref, o_ref):
        i = pl.program_id(0)
        csq, snq = cs_ref[...], sn_ref[...]
        cs_all = jnp.concatenate([csp_ref[...], csq], axis=0)
        sn_all = jnp.concatenate([snp_ref[...], snq], axis=0)
        k_all = jnp.concatenate([kp_ref[...], k_ref[...]], axis=0)
        v_all = jnp.concatenate([vp_ref[...], v_ref[...]], axis=0)
        kr, _, _ = _qk_norm_rope(k_all, kg_ref[...], cs_all, sn_all)
        lo_all = _lane(k_all.shape) < HEAD_DIM
        kd = _dup_heads(kr, lo_all)
        vd = _dup_heads(v_all, lo_all)
        lo = _lane((WINDOW, LANES)) < HEAD_DIM
        for pr in range(4):
            g = pr // 2
            qr, _, _ = _qk_norm_rope(q_ref[:, pr * LANES:(pr + 1) * LANES], qg_ref[...], csq, snq)
            outs = []
            for b in range(nb):
                mask = _attn_mask(i * nb + b == 0)
                qb = qr[b * WINDOW:(b + 1) * WINDOW]
                kw = kd[g][b * WINDOW:(b + 2) * WINDOW]
                vw = vd[g][b * WINDOW:(b + 2) * WINDOW]
                halves = []
                for hh in range(2):
                    sel = lo if hh == 0 else jnp.logical_not(lo)
                    a = 2 * pr + hh
                    pn, _ = _attn_probs(jnp.where(sel, qb, 0.0), kw, mask, sk_ref[a:a + 1, 0:1])
                    halves.append(_mm(pn, vw))
                outs.append(jnp.where(lo, halves[0], halves[1]))
            o_ref[:, pr * LANES:(pr + 1) * LANES] = jnp.concatenate(outs, axis=0)

    return pl.pallas_call(
        body, name="attn_fwd", grid=(T // tq,),
        in_specs=_attn_specs(T, tq, lambda i: i),
        out_specs=pl.BlockSpec((tq, B_WIDTH), lambda i: (i, 0)), out_shape=_sds((T, B_WIDTH)),
        compiler_params=_cp("parallel"))(z, z, z, z, z, cs, sn, cs, sn, qg, kg, sinks)


def _attn_bwd(z, cs, sn, qg, kg, sinks, o, do):
    T = z.shape[0]
    tq = min(T, 512)
    nb = tq // WINDOW
    nt = T // tq
    tk = tq + WINDOW

    def tile_of(i):
        return nt - 1 - i

    def body(q_ref, k_ref, v_ref, kp_ref, vp_ref, cs_ref, sn_ref, csp_ref, snp_ref, qg_ref, kg_ref, sk_ref, o_ref, do_ref,
             dq_ref, dk_ref, dv_ref, dqg_ref, dkg_ref, dsk_ref, acck, accv, ck, cv):
        i = pl.program_id(0)
        ti = nt - 1 - i

        @pl.when(i == 0)
        def _():
            dqg_ref[...] = jnp.zeros_like(dqg_ref)
            dkg_ref[...] = jnp.zeros_like(dkg_ref)
            dsk_ref[...] = jnp.zeros_like(dsk_ref)
            ck[...] = jnp.zeros_like(ck)
            cv[...] = jnp.zeros_like(cv)

        csq, snq = cs_ref[...], sn_ref[...]
        cs_all = jnp.concatenate([csp_ref[...], csq], axis=0)
        sn_all = jnp.concatenate([snp_ref[...], snq], axis=0)
        k_all = jnp.concatenate([kp_ref[...], k_ref[...]], axis=0)
        v_all = jnp.concatenate([vp_ref[...], v_ref[...]], axis=0)
        kr, kh, rk = _qk_norm_rope(k_all, kg_ref[...], cs_all, sn_all)
        lo_all = _lane(k_all.shape) < HEAD_DIM
        kd = _dup_heads(kr, lo_all)
        vd = _dup_heads(v_all, lo_all)
        lo = _lane((WINDOW, LANES)) < HEAD_DIM
        acck[...] = jnp.zeros_like(acck)
        accv[...] = jnp.zeros_like(accv)
        for pr in range(4):
            g = pr // 2
            psl = slice(pr * LANES, (pr + 1) * LANES)
            qr, qh, rq = _qk_norm_rope(q_ref[:, psl], qg_ref[...], csq, snq)
            dqs = []
            for b in range(nb):
                mask = _attn_mask(ti * nb + b == 0)
                bs = slice(b * WINDOW, (b + 1) * WINDOW)
                ws = slice(b * WINDOW, (b + 2) * WINDOW)
                qb = qr[bs]
                kw = kd[g][ws]
                vw = vd[g][ws]
                ob = o_ref[bs, psl]
                dob = do_ref[bs, psl]
                halves = []
                for hh in range(2):
                    sel = lo if hh == 0 else jnp.logical_not(lo)
                    a = 2 * pr + hh
                    qm = jnp.where(sel, qb, 0.0)
                    pn, psink = _attn_probs(qm, kw, mask, sk_ref[a:a + 1, 0:1])
                    dom = jnp.where(sel, dob, 0.0)
                    delta = jnp.sum(dom * ob, axis=-1, keepdims=True)
                    ds = pn * (_mm_nt(dom, vw) - delta)
                    dsk_ref[a:a + 1, :] += jnp.zeros((1, LANES), F32) - jnp.sum(psink * delta)
                    halves.append(_mm(ds, kw) * SCALE)
                    acck[g, ws, :] += _mm_tn(ds, qm) * SCALE
                    accv[g, ws, :] += _mm_tn(pn, dom)
                dqs.append(jnp.where(lo, halves[0], halves[1]))
            dx, dg = _qk_norm_rope_bwd(qh, rq, qg_ref[...], csq, snq, jnp.concatenate(dqs, axis=0))
            dq_ref[:, psl] = dx
            dqg_ref[...] += dg

        def fold(acc):
            f0 = acc[0] + _swap64(acc[0])
            f1 = acc[1] + _swap64(acc[1])
            return jnp.where(lo_all, f0, f1)

        dk_all = fold(acck)
        dv_all = fold(accv)
        pad = jnp.zeros((tq - WINDOW, LANES), F32)
        dk_own = dk_all[WINDOW:] + (jnp.concatenate([pad, ck[...]], axis=0) if nb > 1 else ck[...])
        dv_own = dv_all[WINDOW:] + (jnp.concatenate([pad, cv[...]], axis=0) if nb > 1 else cv[...])
        ck[...] = dk_all[:WINDOW]
        cv[...] = dv_all[:WINDOW]
        dxk, dgk = _qk_norm_rope_bwd(kh[WINDOW:], rk[WINDOW:], kg_ref[...], csq, snq, dk_own)
        dk_ref[...] = dxk
        dkg_ref[...] += dgk
        dv_ref[...] = dv_own

    row = pl.BlockSpec((1, LANES), lambda i: (0, 0))
    return pl.pallas_call(
        body, name="attn_bwd", grid=(nt,),
        in_specs=_attn_specs(T, tq, tile_of) + [pl.BlockSpec((tq, B_WIDTH), lambda i: (tile_of(i), 0))] * 2,
        out_specs=[pl.BlockSpec((tq, B_WIDTH), lambda i: (tile_of(i), 0)), pl.BlockSpec((tq, LANES), lambda i: (tile_of(i), 0)),
                   pl.BlockSpec((tq, LANES), lambda i: (tile_of(i), 0)), row, row, pl.BlockSpec((8, LANES), lambda i: (0, 0))],
        out_shape=[_sds((T, B_WIDTH)), _sds((T, LANES)), _sds((T, LANES)), _sds((1, LANES)), _sds((1, LANES)), _sds((8, LANES))],
        scratch_shapes=[pltpu.VMEM((2, tk, LANES), F32), pltpu.VMEM((2, tk, LANES), F32),
                        pltpu.VMEM((WINDOW, LANES), F32), pltpu.VMEM((WINDOW, LANES), F32)],
        compiler_params=_cp("arbitrary"))(z, z, z, z, z, cs, sn, cs, sn, qg, kg, sinks, o, do)


def _bbar_t(are, aim, ldt, btr, bti):
    lbr, lbi = _lam_bar(are, aim, ldt)
    den = are * are + aim * aim
    nr = lbr - 1.0
    cr = (nr * are + lbi * aim) / den
    ci = (lbi * are - nr * aim) / den
    return cr * btr - ci * bti, cr * bti + ci * btr


def _lam_bar(are, aim, ldt):
    dt = jnp.exp(ldt)
    er = jnp.exp(are * dt)
    return er * jnp.cos(aim * dt), er * jnp.sin(aim * dt)


def _block_diag(x):
    t = jnp.concatenate([x] * C_GROUPS, axis=1)
    return jnp.where(_row(t.shape) // C_GROUP == _lane(t.shape) // C_STATE, t, 0.0)


def _block_diag_fold(m):
    rg = _row((C_WIDTH, C_STATE)) // C_GROUP
    acc = jnp.zeros((C_WIDTH, C_STATE), F32)
    for g in range(C_GROUPS):
        acc = acc + jnp.where(rg == g, m[:, g * C_STATE:(g + 1) * C_STATE], 0.0)
    return acc


def _ssm_prep(are, aim, ldt, are_x, aim_x, ldt_x, btr, bti, cre, cim):
    def body(are_r, aim_r, ldt_r, arex_r, aimx_r, ldtx_r, btr_r, bti_r, cre_r, cim_r, bbd_ref, cbd_ref, lr_ref, li_ref):
        lr, li = _lam_bar(are_r[...], aim_r[...], ldt_r[...])
        lr_ref[...] = lr
        li_ref[...] = li
        br, bi = _bbar_t(arex_r[...], aimx_r[...], ldtx_r[...], btr_r[...], bti_r[...])
        bbd_ref[...] = jnp.concatenate([_block_diag(br), _block_diag(bi)], axis=1).astype(MXU)
        cbd_ref[...] = jnp.concatenate([_block_diag(cre_r[...]), -_block_diag(cim_r[...])], axis=1).astype(MXU)

    return pl.pallas_call(
        body, name="ssm_prep",
        out_shape=[_sds((C_WIDTH, 2 * N_STATE), MXU), _sds((C_WIDTH, 2 * N_STATE), MXU), _sds((C_GROUPS, C_STATE)), _sds((C_GROUPS, C_STATE))],
        compiler_params=pltpu.CompilerParams(vmem_limit_bytes=VMEM_LIMIT))(are, aim, ldt, are_x, aim_x, ldt_x, btr, bti, cre, cim)


def _ssm_param_bwd(are, aim, ldt, are_x, aim_x, ldt_x, btr, bti, dlr, dli, dbbd, dcr, dci):
    def body(are_r, aim_r, ldt_r, arex_r, aimx_r, ldtx_r, btr_r, bti_r, dlr_r, dli_r, dbbd_r, dcr_r, dci_r,
             dare_ref, daim_ref, dldt_ref, dbtr_ref, dbti_ref, dcre_ref, dcim_ref):
        _, vjp_l = jax.vjp(_lam_bar, are_r[...], aim_r[...], ldt_r[...])
        da1, di1, dl1 = vjp_l((dlr_r[...], dli_r[...]))
        dbr = _block_diag_fold(dbbd_r[:, 0:N_STATE])
        dbi = _block_diag_fold(dbbd_r[:, N_STATE:2 * N_STATE])
        _, vjp_b = jax.vjp(_bbar_t, arex_r[...], aimx_r[...], ldtx_r[...], btr_r[...], bti_r[...])
        da2, di2, dl2, dbtr, dbti = vjp_b((dbr, dbi))

        def gsum(x):
            return x.reshape(C_GROUPS, C_GROUP, C_STATE).sum(axis=1)

        dare_ref[...] = da1 + gsum(da2)
        daim_ref[...] = di1 + gsum(di2)
        dldt_ref[...] = jnp.broadcast_to(jnp.sum(dl1 + gsum(dl2), axis=-1, keepdims=True), (C_GROUPS, LANES))
        dbtr_ref[...] = dbtr
        dbti_ref[...] = dbti
        dcre_ref[...] = _block_diag_fold(dcr_r[...])
        dcim_ref[...] = -_block_diag_fold(dci_r[...])

    g = _sds((C_GROUPS, C_STATE))
    x = _sds((C_WIDTH, C_STATE))
    return pl.pallas_call(
        body, name="ssm_param_bwd", out_shape=[g, g, _sds((C_GROUPS, LANES)), x, x, x, x],
        compiler_params=pltpu.CompilerParams(vmem_limit_bytes=VMEM_LIMIT))(are, aim, ldt, are_x, aim_x, ldt_x, btr, bti, dlr, dli, dbbd, dcr, dci)


def _ssm_fwd(z, bbd, lr, li, cbd, dsk, w1, w2):
    T = z.shape[0]
    tt = min(T, 256)

    def body(u_ref, bbd_ref, lr_ref, li_ref, cbd_ref, d_ref, w1_ref, w2_ref, yc_ref, xr_ref, xi_ref, sr, si):
        @pl.when(pl.program_id(0) == 0)
        def _():
            sr[...] = jnp.zeros_like(sr)
            si[...] = jnp.zeros_like(si)

        u = u_ref[...]
        bu = _mm(u, bbd_ref[...])
        xr_ref[...] = bu[:, 0:N_STATE]
        xi_ref[...] = bu[:, N_STATE:2 * N_STATE]
        lam_r, lam_i = lr_ref[...], li_ref[...]

        def step(t, c):
            cr, ci = c
            nr = lam_r * cr - lam_i * ci + xr_ref[pl.ds(t, 1), :]
            ni = lam_r * ci + lam_i * cr + xi_ref[pl.ds(t, 1), :]
            xr_ref[pl.ds(t, 1), :] = nr
            xi_ref[pl.ds(t, 1), :] = ni
            return nr, ni

        cr, ci = lax.fori_loop(0, tt, step, (sr[...], si[...]), unroll=8)
        sr[...] = cr
        si[...] = ci
        x = jnp.concatenate([xr_ref[...], xi_ref[...]], axis=1)
        y2 = _gelu(_mm_nt(x, cbd_ref[...]) + d_ref[...] * u)
        yc_ref[...] = _mm(y2, w1_ref[...]) * _sigmoid(_mm(y2, w2_ref[...]))

    big = pl.BlockSpec((C_WIDTH, 2 * N_STATE), lambda i: (0, 0))
    srow = pl.BlockSpec((1, N_STATE), lambda i: (0, 0))
    wsp = pl.BlockSpec((C_WIDTH, C_WIDTH), lambda i: (0, 0))
    xs = pl.BlockSpec((tt, N_STATE), lambda i: (i, 0))
    return pl.pallas_call(
        body, name="ssm_fwd", grid=(T // tt,),
        in_specs=[pl.BlockSpec((tt, C_WIDTH), lambda i: (i, 5)), big, srow, srow, big, pl.BlockSpec((1, C_WIDTH), lambda i: (0, 0)), wsp, wsp],
        out_specs=[pl.BlockSpec((tt, C_WIDTH), lambda i: (i, 0)), xs, xs],
        out_shape=[_sds((T, C_WIDTH)), _sds((T, N_STATE)), _sds((T, N_STATE))],
        scratch_shapes=[pltpu.VMEM((1, N_STATE), F32)] * 2,
        compiler_params=_cp("arbitrary"))(z, bbd, lr, li, cbd, dsk, w1, w2)


def _ssm_bwd(dyc, z, xr, xi, bbd, lr, li, cbd, dsk, w1, w2):
    T = z.shape[0]
    tt = min(T, 256)
    nt = T // tt

    def tile_of(i):
        return nt - 1 - i

    def body(dyc_ref, u_ref, xr_ref, xi_ref, xpr_ref, xpi_ref, bbd_ref, lr_ref, li_ref, cbd_ref, d_ref, w1_ref, w2_ref,
             du_ref, y2_ref, da1_ref, da2_ref, dy_ref, arai_ref, dlr_ref, dli_ref, dd_ref, gr, gi, sr, si):
        i = pl.program_id(0)
        ti = nt - 1 - i

        @pl.when(i == 0)
        def _():
            sr[...] = jnp.zeros_like(sr)
            si[...] = jnp.zeros_like(si)
            dlr_ref[...] = jnp.zeros_like(dlr_ref)
            dli_ref[...] = jnp.zeros_like(dli_ref)
            dd_ref[...] = jnp.zeros_like(dd_ref)

        u = u_ref[...]
        xr_t, xi_t = xr_ref[...], xi_ref[...]
        y = _mm_nt(jnp.concatenate([xr_t, xi_t], axis=1), cbd_ref[...]) + d_ref[...] * u
        y2 = _gelu(y)
        a1 = _mm(y2, w1_ref[...])
        sg = _sigmoid(_mm(y2, w2_ref[...]))
        dyc_t = dyc_ref[...]
        da1 = dyc_t * sg
        da2 = dyc_t * a1 * sg * (1.0 - sg)
        dy = (_mm_nt(da1, w1_ref[...]) + _mm_nt(da2, w2_ref[...])) * _gelu_grad(y)
        gx = _mm(dy, cbd_ref[...])
        gr[...] = gx[:, 0:N_STATE]
        gi[...] = gx[:, N_STATE:2 * N_STATE]
        lam_r, lam_i = lr_ref[...], li_ref[...]

        def step(k, c):
            t = tt - 1 - k
            cr, ci = c
            nr = gr[pl.ds(t, 1), :] + lam_r * cr + lam_i * ci
            ni = gi[pl.ds(t, 1), :] - lam_i * cr + lam_r * ci
            gr[pl.ds(t, 1), :] = nr
            gi[pl.ds(t, 1), :] = ni
            return nr, ni

        cr, ci = lax.fori_loop(0, tt, step, (sr[...], si[...]), unroll=8)
        sr[...] = cr
        si[...] = ci
        ar, ai = gr[...], gi[...]
        first_row = _row(ar.shape) == 0
        live = jnp.where(ti > 0, 1.0, 0.0)
        xsr = jnp.where(first_row, xpr_ref[7:8, :] * live, pltpu.roll(xr_t, 1, 0))
        xsi = jnp.where(first_row, xpi_ref[7:8, :] * live, pltpu.roll(xi_t, 1, 0))
        dlr_ref[...] += jnp.sum(ar * xsr + ai * xsi, axis=0, keepdims=True)
        dli_ref[...] += jnp.sum(ai * xsr - ar * xsi, axis=0, keepdims=True)
        dd_ref[...] += jnp.sum(dy * u, axis=0, keepdims=True)
        arai = jnp.concatenate([ar, ai], axis=1)
        du_ref[...] = _mm_nt(arai, bbd_ref[...]) + d_ref[...] * dy
        y2_ref[...] = y2.astype(MXU)
        da1_ref[...] = da1.astype(MXU)
        da2_ref[...] = da2.astype(MXU)
        dy_ref[...] = dy.astype(MXU)
        arai_ref[...] = arai.astype(MXU)

    def prev(i):
        return jnp.maximum(tile_of(i) * (tt // 8) - 1, 0)

    big = pl.BlockSpec((C_WIDTH, 2 * N_STATE), lambda i: (0, 0))
    srow = pl.BlockSpec((1, N_STATE), lambda i: (0, 0))
    wsp = pl.BlockSpec((C_WIDTH, C_WIDTH), lambda i: (0, 0))
    xs = pl.BlockSpec((tt, N_STATE), lambda i: (tile_of(i), 0))
    xp = pl.BlockSpec((8, N_STATE), lambda i: (prev(i), 0))
    cw = pl.BlockSpec((tt, C_WIDTH), lambda i: (tile_of(i), 0))
    drow = pl.BlockSpec((1, C_WIDTH), lambda i: (0, 0))
    return pl.pallas_call(
        body, name="ssm_bwd", grid=(nt,),
        in_specs=[cw, pl.BlockSpec((tt, C_WIDTH), lambda i: (tile_of(i), 5)), xs, xs, xp, xp, big, srow, srow, big, drow, wsp, wsp],
        out_specs=[cw, cw, cw, cw, cw, pl.BlockSpec((tt, 2 * N_STATE), lambda i: (tile_of(i), 0)), srow, srow, drow],
        out_shape=[_sds((T, C_WIDTH))] + [_sds((T, C_WIDTH), MXU)] * 4 + [_sds((T, 2 * N_STATE), MXU), _sds((1, N_STATE)), _sds((1, N_STATE)), _sds((1, C_WIDTH))],
        scratch_shapes=[pltpu.VMEM((tt, N_STATE), F32)] * 2 + [pltpu.VMEM((1, N_STATE), F32)] * 2,
        compiler_params=_cp("arbitrary"))(dyc, z, xr, xi, xr, xi, bbd, lr, li, cbd, dsk, w1, w2)


_GROUPS = ((0, A_WIDTH), (A_WIDTH, A_WIDTH + B_WIDTH), (A_WIDTH + B_WIDTH, D_MODEL))


def _merge_fwd(h, ya, yb, yc, g, w):
    T = h.shape[0]
    tm = min(T, 512)

    def body(h_ref, a_ref, b_ref, c_ref, g_ref, w_ref, o_ref):
        yn = jnp.concatenate([y * _rms_stat(y) for y in (a_ref[...], b_ref[...], c_ref[...])], axis=1) * g_ref[...]
        o_ref[...] = h_ref[...] + _mm(yn, w_ref[...])

    def rows(w_):
        return pl.BlockSpec((tm, w_), lambda i: (i, 0))

    return pl.pallas_call(
        body, name="merge_fwd", grid=(T // tm,),
        in_specs=[rows(D_MODEL), rows(A_WIDTH), rows(B_WIDTH), rows(C_WIDTH), pl.BlockSpec((1, D_MODEL), lambda i: (0, 0)),
                  pl.BlockSpec((D_MODEL, D_MODEL), lambda i: (0, 0))],
        out_specs=rows(D_MODEL), out_shape=_sds((T, D_MODEL)), compiler_params=_cp("parallel"))(h, ya, yb, yc, g, w)


def _merge_bwd(gres, ya, yb, yc, g, w):
    T = gres.shape[0]
    tm = min(T, 512)

    def body(gr_ref, a_ref, b_ref, c_ref, g_ref, w_ref, da_ref, db_ref, dc_ref, yn_ref, dg_ref):
        @pl.when(pl.program_id(0) == 0)
        def _():
            dg_ref[...] = jnp.zeros_like(dg_ref)

        dyn = _mm_nt(gr_ref[...], w_ref[...])
        yns, dgs = [], []
        for (c0, c1), y_ref, d_ref in zip(_GROUPS, (a_ref, b_ref, c_ref), (da_ref, db_ref, dc_ref)):
            y = y_ref[...]
            r = _rms_stat(y)
            gg = g_ref[:, c0:c1]
            dx, dg = _rms_bwd(y, r, gg, dyn[:, c0:c1])
            d_ref[...] = dx
            dgs.append(dg)
            yns.append(y * r * gg)
        dg_ref[...] += jnp.concatenate(dgs, axis=1)
        yn_ref[...] = jnp.concatenate(yns, axis=1).astype(MXU)

    def rows(w_):
        return pl.BlockSpec((tm, w_), lambda i: (i, 0))

    row = pl.BlockSpec((1, D_MODEL), lambda i: (0, 0))
    return pl.pallas_call(
        body, name="merge_bwd", grid=(T // tm,),
        in_specs=[rows(D_MODEL), rows(A_WIDTH), rows(B_WIDTH), rows(C_WIDTH), row, pl.BlockSpec((D_MODEL, D_MODEL), lambda i: (0, 0))],
        out_specs=[rows(A_WIDTH), rows(B_WIDTH), rows(C_WIDTH), rows(D_MODEL), row],
        out_shape=[_sds((T, A_WIDTH)), _sds((T, B_WIDTH)), _sds((T, C_WIDTH)), _sds((T, D_MODEL), MXU), _sds((1, D_MODEL))],
        compiler_params=_cp("arbitrary"))(gres, ya, yb, yc, g, w)


FF_BLOCK = D_FF // N_DEV


def _mlp_fwd(h, g, w1, w2):
    T = h.shape[0]
    tm = min(T, 1024)

    def body(h_ref, g_ref, w1_ref, w2_ref, o_ref, hn):
        @pl.when(pl.program_id(1) == 0)
        def _():
            x = h_ref[...]
            hn[...] = (x * _rms_stat(x) * g_ref[...]).astype(MXU)
            o_ref[...] = x

        a = jnp.maximum(_mm(hn[...], w1_ref[...]), 0.0)
        o_ref[...] += _mm(a * a, w2_ref[...])

    return pl.pallas_call(
        body, name="mlp_fwd", grid=(T // tm, N_DEV),
        in_specs=[pl.BlockSpec((tm, D_MODEL), lambda i, j: (i, 0)), pl.BlockSpec((1, D_MODEL), lambda i, j: (0, 0)),
                  pl.BlockSpec((None, D_MODEL, FF_BLOCK), lambda i, j: (j, 0, 0)), pl.BlockSpec((FF_BLOCK, D_MODEL), lambda i, j: (j, 0))],
        out_specs=pl.BlockSpec((tm, D_MODEL), lambda i, j: (i, 0)), out_shape=_sds((T, D_MODEL)),
        scratch_shapes=[pltpu.VMEM((tm, D_MODEL), MXU)],
        compiler_params=_cp("parallel", "arbitrary"))(h, g, w1, w2)


def _mlp_bwd(gres, h, g, w1, w2):
    T = h.shape[0]
    tm = min(T, 1024)

    def body(gr_ref, h_ref, g_ref, w1_ref, w2_ref, dh_ref, hn_ref, da_ref, r_ref, dg_ref, acc):
        i, j = pl.program_id(0), pl.program_id(1)

        @pl.when((i == 0) & (j == 0))
        def _():
            dg_ref[...] = jnp.zeros_like(dg_ref)

        @pl.when(j == 0)
        def _():
            x = h_ref[...]
            hn_ref[...] = (x * _rms_stat(x) * g_ref[...]).astype(MXU)
            acc[...] = jnp.zeros_like(acc)

        a = jnp.maximum(_mm(hn_ref[...], w1_ref[...]), 0.0)
        da = _mm_nt(gr_ref[...], w2_ref[...]) * (2.0 * a)
        acc[...] += _mm_nt(da, w1_ref[...])
        da_ref[...] = da.astype(MXU)
        r_ref[...] = (a * a).astype(MXU)

        @pl.when(j == N_DEV - 1)
        def _():
            x = h_ref[...]
            dx, dg = _rms_bwd(x, _rms_stat(x), g_ref[...], acc[...])
            dh_ref[...] = gr_ref[...] + dx
            dg_ref[...] += dg

    rows = pl.BlockSpec((tm, D_MODEL), lambda i, j: (i, 0))
    row = pl.BlockSpec((1, D_MODEL), lambda i, j: (0, 0))
    ffb = pl.BlockSpec((tm, FF_BLOCK), lambda i, j: (i, j))
    return pl.pallas_call(
        body, name="mlp_bwd", grid=(T // tm, N_DEV),
        in_specs=[rows, rows, row, pl.BlockSpec((None, D_MODEL, FF_BLOCK), lambda i, j: (j, 0, 0)),
                  pl.BlockSpec((FF_BLOCK, D_MODEL), lambda i, j: (j, 0))],
        out_specs=[rows, rows, ffb, ffb, row],
        out_shape=[_sds((T, D_MODEL)), _sds((T, D_MODEL), MXU), _sds((T, D_FF), MXU), _sds((T, D_FF), MXU), _sds((1, D_MODEL))],
        scratch_shapes=[pltpu.VMEM((tm, D_MODEL), F32)],
        compiler_params=_cp("arbitrary", "arbitrary"))(gres, h, g, w1, w2)


def _ple_fwd(h, p, g, wg, wp):
    T = h.shape[0]
    tm = min(T, 512)

    def body(h_ref, p_ref, g_ref, wg_ref, wp_ref, o_ref):
        x = h_ref[...]
        gate = _sigmoid(_mm(x * _rms_stat(x) * g_ref[...], wg_ref[...]))
        o_ref[...] = x + gate * _mm(p_ref[...], wp_ref[...])

    rows = pl.BlockSpec((tm, D_MODEL), lambda i: (i, 0))
    return pl.pallas_call(
        body, name="ple_fwd", grid=(T // tm,),
        in_specs=[rows, pl.BlockSpec((tm, PLE_DIM), lambda i: (i, 0)), pl.BlockSpec((1, D_MODEL), lambda i: (0, 0)),
                  pl.BlockSpec((D_MODEL, D_MODEL), lambda i: (0, 0)), pl.BlockSpec((PLE_DIM, D_MODEL), lambda i: (0, 0))],
        out_specs=rows, out_shape=_sds((T, D_MODEL)), compiler_params=_cp("parallel"))(h, p, g, wg, wp)


def _ple_bwd(gres, h, p, g, wg, wp):
    T = h.shape[0]
    tm = min(T, 512)

    def body(gr_ref, h_ref, p_ref, g_ref, wg_ref, wp_ref, dh_ref, hn_ref, dgp_ref, de_ref, dg_ref):
        @pl.when(pl.program_id(0) == 0)
        def _():
            dg_ref[...] = jnp.zeros_like(dg_ref)

        x = h_ref[...]
        r = _rms_stat(x)
        gg = g_ref[...]
        hn = x * r * gg
        gate = _sigmoid(_mm(hn, wg_ref[...]))
        e = _mm(p_ref[...], wp_ref[...])
        gr = gr_ref[...]
        dgp = gr * e * gate * (1.0 - gate)
        dx, dg = _rms_bwd(x, r, gg, _mm_nt(dgp, wg_ref[...]))
        dh_ref[...] = gr + dx
        dg_ref[...] += dg
        hn_ref[...] = hn.astype(MXU)
        dgp_ref[...] = dgp.astype(MXU)
        de_ref[...] = (gr * gate).astype(MXU)

    rows = pl.BlockSpec((tm, D_MODEL), lambda i: (i, 0))
    row = pl.BlockSpec((1, D_MODEL), lambda i: (0, 0))
    return pl.pallas_call(
        body, name="ple_bwd", grid=(T // tm,),
        in_specs=[rows, rows, pl.BlockSpec((tm, PLE_DIM), lambda i: (i, 0)), row,
                  pl.BlockSpec((D_MODEL, D_MODEL), lambda i: (0, 0)), pl.BlockSpec((PLE_DIM, D_MODEL), lambda i: (0, 0))],
        out_specs=[rows, rows, rows, rows, row],
        out_shape=[_sds((T, D_MODEL))] + [_sds((T, D_MODEL), MXU)] * 3 + [_sds((1, D_MODEL))],
        compiler_params=_cp("arbitrary"))(gres, h, p, g, wg, wp)


def _loss_head(h, target):
    T = h.shape[0]
    tm = min(T, 1024)

    def body(h_ref, t_ref, dh_ref, l_ref):
        @pl.when(pl.program_id(0) == 0)
        def _():
            l_ref[...] = jnp.zeros_like(l_ref)

        e = h_ref[...] - t_ref[...]
        dh_ref[...] = e * (1.0 / D_MODEL)
        l_ref[...] += jnp.zeros_like(l_ref) + 0.5 * jnp.sum(jnp.mean(e * e, axis=-1, keepdims=True))

    rows = pl.BlockSpec((tm, D_MODEL), lambda i: (i, 0))
    return pl.pallas_call(
        body, name="loss_head", grid=(T // tm,), in_specs=[rows, rows],
        out_specs=[rows, pl.BlockSpec((8, LANES), lambda i: (0, 0))], out_shape=[_sds((T, D_MODEL)), _sds((8, LANES))],
        compiler_params=_cp("arbitrary"))(h, target)


def _tn(a, b, m, n, *, bm, bn, a_off=0, b_off=0, n_major=False, split=None, dtype=F32, name="tn"):
    T = a.shape[0]
    tk = min(T, 1024)
    nk = T // tk
    assert m % bm == 0 and n % bn == 0 and (not n_major or bm == m) and (split is None or (bn == n and n % split == 0))

    def body(a_ref, b_ref, o_ref, acc):
        k = pl.program_id(2)

        @pl.when(k == 0)
        def _():
            acc[...] = jnp.zeros_like(acc)

        acc[...] += _mm_tn(a_ref[...], b_ref[...])

        @pl.when(k == nk - 1)
        def _():
            if split is None:
                o_ref[...] = acc[...].astype(dtype)
            else:
                for d in range(n // split):
                    o_ref[d] = acc[:, d * split:(d + 1) * split].astype(dtype)

    if split is not None:
        out_spec = pl.BlockSpec((n // split, bm, split), lambda i, j, k: (0, i, 0))
        out_shape = _sds((n // split, m, split), dtype)
    elif n_major:
        out_spec = pl.BlockSpec((None, bm, bn), lambda i, j, k: (j, 0, 0))
        out_shape = _sds((n // bn, m, bn), dtype)
    else:
        out_spec = pl.BlockSpec((bm, bn), lambda i, j, k: (i, j))
        out_shape = _sds((m, n), dtype)
    return pl.pallas_call(
        body, name=name, grid=(m // bm, n // bn, nk),
        in_specs=[pl.BlockSpec((tk, bm), lambda i, j, k: (k, a_off + i)), pl.BlockSpec((tk, bn), lambda i, j, k: (k, b_off + j))],
        out_specs=out_spec, out_shape=out_shape, scratch_shapes=[pltpu.VMEM((bm, bn), F32)],
        compiler_params=_cp("parallel", "parallel", "arbitrary"))(a, b)


def _sum_adamw(land, w, m, v):
    R, C = w.shape
    tr = R
    for cand in (512, 256, 128, 64, 32, 16, 8):
        if R % cand == 0 and cand * C * 4 <= 2 ** 20:
            tr = cand
            break

    def body(l_ref, w_ref, m_ref, v_ref, g_ref, d_ref, nm_ref, nv_ref):
        gg = l_ref[0].astype(F32)
        for s in range(1, N_DEV):
            gg = gg + l_ref[s].astype(F32)
        g_ref[...] = gg
        nm = ADAM_B1 * m_ref[...] + (1.0 - ADAM_B1) * gg
        nv = ADAM_B2 * v_ref[...] + (1.0 - ADAM_B2) * (gg * gg)
        m_hat = nm / (1.0 - ADAM_B1 ** ADAM_STEP)
        v_hat = nv / (1.0 - ADAM_B2 ** ADAM_STEP)
        d_ref[...] = -ADAM_LR * (m_hat / (jnp.sqrt(v_hat) + ADAM_EPS) + ADAM_WD * w_ref[...])
        nm_ref[...] = nm
        nv_ref[...] = nv

    blk = pl.BlockSpec((tr, C), lambda i: (i, 0))
    return pl.pallas_call(
        body, name="sum_adamw", grid=(R // tr,), in_specs=[pl.BlockSpec((N_DEV, tr, C), lambda i: (0, i, 0))] + [blk] * 3,
        out_specs=[blk] * 4, out_shape=[_sds((R, C))] * 4, compiler_params=_cp("parallel"))(land, w, m, v)


def _all_to_all(pairs, name):
    n = len(pairs)

    def body(*refs):
        srcs, lands = refs[:n], refs[2 * n:3 * n]
        send, recv, loc = refs[3 * n:]
        x, y, c = lax.axis_index("x"), lax.axis_index("y"), lax.axis_index("c")
        me = 4 * x + 2 * y + c
        own = [pltpu.make_async_copy(pairs[t][2](srcs[t], me, me), pairs[t][3](lands[t], me), loc.at[t]) for t in range(n)]
        for cp in own:
            cp.start()
        sends, recvs = [], []
        for k in range(1, N_DEV):
            px, py, pc = x ^ (k >> 2), y ^ ((k >> 1) & 1), c ^ (k & 1)
            peer = 4 * px + 2 * py + pc
            for t in range(n):
                src = pairs[t][2](srcs[t], me, peer)
                cp = pltpu.make_async_remote_copy(src_ref=src, dst_ref=pairs[t][3](lands[t], me), send_sem=send.at[t, k],
                                                  recv_sem=recv.at[t, k], device_id=(px, py, pc), device_id_type=MESH)
                cp.start()
                sends.append(cp)
                recvs.append(pltpu.make_async_remote_copy(src_ref=src, dst_ref=pairs[t][3](lands[t], peer), send_sem=send.at[t, k],
                                                          recv_sem=recv.at[t, k], device_id=(px, py, pc), device_id_type=MESH))
        for cp in recvs:
            cp.wait_recv()
        for cp in sends:
            cp.wait_send()
        for cp in own:
            cp.wait()

    anyspec = pl.BlockSpec(memory_space=pl.ANY)
    lands = [pr[1] for pr in pairs]
    return pl.pallas_call(
        body, name=name, in_specs=[anyspec] * (2 * n), out_specs=[anyspec] * n,
        out_shape=[_sds(a.shape, a.dtype) for a in lands], input_output_aliases={n + t: t for t in range(n)},
        scratch_shapes=[pltpu.SemaphoreType.DMA((n, N_DEV)), pltpu.SemaphoreType.DMA((n, N_DEV)), pltpu.SemaphoreType.DMA((n,))],
        compiler_params=pltpu.CompilerParams(has_side_effects=True))(*[pr[0] for pr in pairs], *lands)


def _gather_whole(src):
    land = lax.empty((N_DEV,) + src.shape, src.dtype)
    return _all_to_all([(src, land, lambda ref, me, peer: ref, lambda ref, sender: ref.at[sender])], "gather_small_grads")[0]


_HBM = pl.BlockSpec(memory_space=pltpu.HBM)
_SEM = pl.BlockSpec(memory_space=pltpu.SEMAPHORE)
_EFFECT = pltpu.SideEffectType.DATAFLOW_SIDE_EFFECTING


def _peers():
    x, y, c = lax.axis_index("x"), lax.axis_index("y"), lax.axis_index("c")
    out = []
    for k in range(1, N_DEV):
        px, py, pc = x ^ (k >> 2), y ^ ((k >> 1) & 1), c ^ (k & 1)
        out.append((k, (px, py, pc), 4 * px + 2 * py + pc))
    return 4 * x + 2 * y + c, out


def _split_start(srcs, lands, src_of, dst_of, after, name):
    n = len(srcs)

    def body(*refs):
        src, land = refs[:n], refs[n:2 * n]
        send, recv, token = refs[2 * n + 1], refs[2 * n + 2], refs[-1]
        me, peers = _peers()
        for k, dev, peer in peers:
            for t in range(n):
                pltpu.make_async_remote_copy(src_ref=src_of(src[t], peer), dst_ref=dst_of(land[t], me), send_sem=send.at[t * N_DEV + k],
                                             recv_sem=recv.at[t * N_DEV + k], device_id=dev, device_id_type=MESH).start()
        token[...] = jnp.zeros_like(token)

    bufs = list(srcs) + list(lands)
    outs = pl.pallas_call(
        body, name=name,
        out_shape=(pltpu.SemaphoreType.DMA((n * N_DEV,)), pltpu.SemaphoreType.DMA((n * N_DEV,)),
                   *[pltpu.HBM(a.shape, a.dtype) for a in bufs], _sds((8, LANES))),
        in_specs=[_HBM] * (2 * n) + [pl.BlockSpec(memory_space=pl.ANY)],
        out_specs=(_SEM, _SEM, *[_HBM] * (2 * n), pl.BlockSpec(memory_space=pltpu.VMEM)),
        input_output_aliases={i: 2 + i for i in range(2 * n)},
        compiler_params=pltpu.CompilerParams(has_side_effects=_EFFECT),
    )(*[pltpu.with_memory_space_constraint(a, pltpu.HBM) for a in bufs], after)
    return outs[0], outs[1], list(outs[2:2 + n]), list(outs[2 + n:2 + 2 * n]), outs[-1]


def _split_wait(send, recv, srcs, lands, src_of, dst_of, after, name):
    n = len(srcs)

    def body(*refs):
        src, land = refs[:n], refs[n:2 * n]
        send_r, recv_r = refs[2 * n], refs[2 * n + 1]
        _, peers = _peers()
        for k, dev, peer in peers:
            for t in range(n):
                cp = pltpu.make_async_remote_copy(src_ref=src_of(src[t], peer), dst_ref=dst_of(land[t], peer), send_sem=send_r.at[t * N_DEV + k],
                                                  recv_sem=recv_r.at[t * N_DEV + k], device_id=dev, device_id_type=MESH)
                cp.wait_send()
                cp.wait_recv()

    bufs = list(srcs) + list(lands)
    outs = pl.pallas_call(
        body, name=name, out_shape=tuple(pltpu.HBM(a.shape, a.dtype) for a in bufs),
        in_specs=[_HBM] * (2 * n) + [_SEM, _SEM, pl.BlockSpec(memory_space=pl.ANY)], out_specs=[_HBM] * (2 * n),
        input_output_aliases={i: i for i in range(2 * n)},
        compiler_params=pltpu.CompilerParams(has_side_effects=_EFFECT),
    )(*bufs, send, recv, after)
    return list(outs[n:])


SHARDED = ("w_in", "glu_w1", "glu_w2", "w_out", "w_ff1", "w_ff2", "w_ple_gate", "w_ple_proj")
SMALL = ("attn_norm_g", "gmlp_ln_g", "gmlp_ln_b", "gmlp_ws", "gmlp_bs", "q_norm_g", "k_norm_g", "sinks", "ssm_a_re", "ssm_a_im",
         "ssm_log_dt", "ssm_b_re", "ssm_b_im", "ssm_c_re", "ssm_c_im", "ssm_d", "mix_out_g", "mlp_norm_g", "ple_norm_g")
WEIGHTS = ("attn_norm_g", "w_in", "gmlp_ln_g", "gmlp_ln_b", "gmlp_ws", "gmlp_bs", "q_norm_g", "k_norm_g", "sinks", "ssm_a_re", "ssm_a_im",
           "ssm_log_dt", "ssm_b_re", "ssm_b_im", "ssm_c_re", "ssm_c_im", "ssm_d", "glu_w1", "glu_w2", "mix_out_g", "w_out", "mlp_norm_g",
           "w_ff1", "w_ff2", "ple_norm_g", "w_ple_gate", "w_ple_proj")
FLAT_COLS = 1024


def _pack(arrs, dtype):
    flat = jnp.concatenate([a.astype(dtype).reshape(-1) for a in arrs])
    pad = (-flat.shape[0]) % (8 * FLAT_COLS)
    if pad:
        flat = jnp.concatenate([flat, jnp.zeros((pad,), dtype)])
    return flat.reshape(-1, FLAT_COLS)


def _unpack(flat, shapes, lead=()):
    flat = flat.reshape(lead + (-1,))
    out, off = [], 0
    for s in shapes:
        n = math.prod(s)
        out.append(flat[..., off:off + n].reshape(lead + tuple(s)))
        off += n
    return out


def _col_major(w):
    rows, cols = w.shape
    return w.reshape(rows, N_DEV, cols // N_DEV).transpose(1, 0, 2)


def _from_col_major(s):
    n, rows, cs = s.shape
    return s.transpose(1, 0, 2).reshape(rows, n * cs)


def _layer_fwd(h, p_l, cs, sn, W, S, sp):
    z = _inproj_fwd(h, S["attn_norm_g"], W["w_in"])
    ya = _gmlp_fwd(z, S["lng"], S["lnb"], S["gmlp_ws"], S["bsx"])
    yb = _attn_fwd(z, cs, sn, S["qg"], S["kg"], S["sinks"])
    yc, xr, xi = _ssm_fwd(z, sp["bbd"], sp["lr"], sp["li"], sp["cbd"], S["ssm_d"], W["glu_w1"], W["glu_w2"])
    h1 = _merge_fwd(h, ya, yb, yc, S["mix_out_g"], W["w_out"])
    h2 = _mlp_fwd(h1, S["mlp_norm_g"], W["w_ff1"], W["w_ff2"])
    h3 = _ple_fwd(h2, p_l, S["ple_norm_g"], W["w_ple_gate"], W["w_ple_proj"])
    return h3, dict(h=h, z=z, ya=ya, yb=yb, yc=yc, xr=xr, xi=xi, h1=h1, h2=h2)


def _layer_bwd(g3, p_l, cs, sn, W, S, sp, A, raw):
    G = {}
    g2, hn3, dgp, de, G["ple_norm_g"] = _ple_bwd(g3, A["h2"], p_l, S["ple_norm_g"], W["w_ple_gate"], W["w_ple_proj"])
    G["w_ple_gate"] = _tn(hn3, dgp, D_MODEL, D_MODEL, bm=1024, bn=1024, dtype=WIRE, name="tn_gate").reshape(N_DEV, -1, D_MODEL)
    G["w_ple_proj"] = _tn(p_l, de, PLE_DIM, D_MODEL, bm=PLE_DIM, bn=D_MODEL, split=D_MODEL // N_DEV, dtype=WIRE, name="tn_proj")
    g1, hn, da, r, G["mlp_norm_g"] = _mlp_bwd(g2, A["h1"], S["mlp_norm_g"], W["w_ff1"], W["w_ff2"])
    G["w_ff1"] = _tn(hn, da, D_MODEL, D_FF, bm=D_MODEL, bn=FF_BLOCK, n_major=True, dtype=WIRE, name="tn_ff1")
    G["w_ff2"] = _tn(r, g2, D_FF, D_MODEL, bm=1024, bn=1024, dtype=WIRE, name="tn_ff2").reshape(N_DEV, -1, D_MODEL)
    dya, dyb, dyc, yn, G["mix_out_g"] = _merge_bwd(g1, A["ya"], A["yb"], A["yc"], S["mix_out_g"], W["w_out"])
    G["w_out"] = _tn(yn, g1, D_MODEL, D_MODEL, bm=1024, bn=1024, dtype=WIRE, name="tn_out").reshape(N_DEV, -1, D_MODEL)
    dzc, y2, da1, da2, dy, arai, dlr, dli, dd = _ssm_bwd(dyc, A["z"], A["xr"], A["xi"], sp["bbd"], sp["lr"], sp["li"], sp["cbd"],
                                                        S["ssm_d"], W["glu_w1"], W["glu_w2"])
    G["glu_w1"] = _tn(y2, da1, C_WIDTH, C_WIDTH, bm=256, bn=256, dtype=WIRE, name="tn_glu1").reshape(N_DEV, -1, C_WIDTH)
    G["glu_w2"] = _tn(y2, da2, C_WIDTH, C_WIDTH, bm=256, bn=256, dtype=WIRE, name="tn_glu2").reshape(N_DEV, -1, C_WIDTH)
    dbbd = _tn(A["z"], arai, C_WIDTH, 2 * N_STATE, bm=256, bn=512, a_off=5, name="tn_bbd")
    dcr = _tn(dy, A["xr"], C_WIDTH, N_STATE, bm=256, bn=512, name="tn_cre")
    dci = _tn(dy, A["xi"], C_WIDTH, N_STATE, bm=256, bn=512, name="tn_cim")
    dare, daim, dldt, dbtr, dbti, dcre, dcim = _ssm_param_bwd(
        raw["are"], raw["aim"], raw["ldt"], raw["are_x"], raw["aim_x"], raw["ldt_x"], raw["btr"], raw["bti"],
        dlr.reshape(C_GROUPS, C_STATE), dli.reshape(C_GROUPS, C_STATE), dbbd, dcr, dci)
    G["ssm_a_re"], G["ssm_a_im"], G["ssm_log_dt"] = dare, daim, dldt[:, 0]
    G["ssm_b_re"] = dbtr.reshape(C_GROUPS, C_GROUP, C_STATE).transpose(0, 2, 1)
    G["ssm_b_im"] = dbti.reshape(C_GROUPS, C_GROUP, C_STATE).transpose(0, 2, 1)
    G["ssm_c_re"] = dcre.reshape(C_GROUPS, C_GROUP, C_STATE)
    G["ssm_c_im"] = dcim.reshape(C_GROUPS, C_GROUP, C_STATE)
    G["ssm_d"] = dd.reshape(C_GROUPS, C_GROUP)
    dzq, dzk, dzv, dqg, dkg, dsk = _attn_bwd(A["z"], cs, sn, S["qg"], S["kg"], S["sinks"], A["yb"], dyb)
    G["q_norm_g"] = dqg[0, :HEAD_DIM] + dqg[0, HEAD_DIM:]
    G["k_norm_g"] = dkg[0, :HEAD_DIM] + dkg[0, HEAD_DIM:]
    G["sinks"] = dsk[:, 0]
    dza, dws, dbs, dlng, dlnb = _gmlp_bwd(A["z"], dya, S["lng"], S["lnb"], S["gmlp_ws"], S["bsx"])
    G["gmlp_ws"] = dws
    G["gmlp_bs"] = dbs[:, :, 0]
    G["gmlp_ln_g"] = dlng.reshape(A_HEADS, 2, HEAD_DIM)[:, 1]
    G["gmlp_ln_b"] = dlnb.reshape(A_HEADS, 2, HEAD_DIM)[:, 1]
    g0, xn, dz, G["attn_norm_g"] = _inproj_bwd(g1, A["h"], S["attn_norm_g"], W["w_in"], dza, dzq, dzk, dzv, dzc)
    G["w_in"] = _tn(xn, dz, D_MODEL, IN_COLS, bm=D_MODEL, bn=IN_COLS, split=IN_COLS // N_DEV, dtype=WIRE, name="tn_in")
    return g0, G


def _small_layouts(P, l):
    def row(a):
        return a.reshape(1, -1)

    zeros = jnp.zeros((A_HEADS, HEAD_DIM), F32)
    S = dict(
        attn_norm_g=row(P["attn_norm_g"][l]), mix_out_g=row(P["mix_out_g"][l]), mlp_norm_g=row(P["mlp_norm_g"][l]),
        ple_norm_g=row(P["ple_norm_g"][l]),
        lng=jnp.stack([zeros, P["gmlp_ln_g"][l]], axis=1).reshape(1, 512),
        lnb=jnp.stack([zeros, P["gmlp_ln_b"][l]], axis=1).reshape(1, 512),
        gmlp_ws=P["gmlp_ws"][l],
        bsx=jnp.broadcast_to(P["gmlp_bs"][l][:, :, None], (A_HEADS, CHUNK, CHUNK)),
        qg=jnp.tile(P["q_norm_g"][l], 2).reshape(1, LANES), kg=jnp.tile(P["k_norm_g"][l], 2).reshape(1, LANES),
        sinks=jnp.broadcast_to(P["sinks"][l][:, None], (8, LANES)),
        ssm_d=row(P["ssm_d"][l]),
    )
    are, aim = P["ssm_a_re"][l], P["ssm_a_im"][l]
    ldt = jnp.broadcast_to(P["ssm_log_dt"][l][:, None], (C_GROUPS, C_STATE))
    raw = dict(
        are=are, aim=aim, ldt=ldt,
        are_x=jnp.repeat(are, C_GROUP, axis=0), aim_x=jnp.repeat(aim, C_GROUP, axis=0), ldt_x=jnp.repeat(ldt, C_GROUP, axis=0),
        btr=P["ssm_b_re"][l].transpose(0, 2, 1).reshape(C_WIDTH, C_STATE), bti=P["ssm_b_im"][l].transpose(0, 2, 1).reshape(C_WIDTH, C_STATE),
        cre=P["ssm_c_re"][l].reshape(C_WIDTH, C_STATE), cim=P["ssm_c_im"][l].reshape(C_WIDTH, C_STATE),
    )
    return S, raw


def _ssm_prep_layer(raw):
    bbd, cbd, lr, li = _ssm_prep(raw["are"], raw["aim"], raw["ldt"], raw["are_x"], raw["aim_x"], raw["ldt_x"],
                                 raw["btr"], raw["bti"], raw["cre"], raw["cim"])
    return dict(bbd=bbd, cbd=cbd, lr=lr.reshape(1, N_STATE), li=li.reshape(1, N_STATE))


def _behind(row, token):
    return row if token is None else row + token[0:1, 0:1]


def _local_step(x, p, positions, target, P, weights_of, after_bwd):
    inv = 1.0 / (ROPE_THETA ** (jnp.arange(0, HEAD_DIM, 2, dtype=F32) / HEAD_DIM))
    cs, sn = _rope_tables(positions.reshape(-1, 1), jnp.tile(inv, 4).reshape(1, LANES))
    h = x
    acts, smalls, weights = [], [], []
    for l in range(DEPTH):
        W, token = weights_of(l, h)
        S, raw = _small_layouts(P, l)
        sp = _ssm_prep_layer(raw)
        h, A = _layer_fwd(h, p[l], cs, sn, W, {**S, "attn_norm_g": _behind(S["attn_norm_g"], token)}, sp)
        acts.append(A)
        smalls.append((S, raw, sp))
        weights.append(W)
    g, lsum = _loss_head(h, target)
    grads = [None] * DEPTH
    token = None
    for l in reversed(range(DEPTH)):
        S, raw, sp = smalls[l]
        g, grads[l] = _layer_bwd(g, p[l], cs, sn, weights[l], {**S, "ple_norm_g": _behind(S["ple_norm_g"], token)}, sp, acts[l], raw)
        token = after_bwd(l, g, grads[l])
    return lsum[0, 0], g, grads


def _layer_weights(g):
    return dict(
        w_in=_from_col_major(g["w_in"]), glu_w1=g["glu_w1"].reshape(C_WIDTH, C_WIDTH), glu_w2=g["glu_w2"].reshape(C_WIDTH, C_WIDTH),
        w_out=g["w_out"].reshape(D_MODEL, D_MODEL), w_ff1=g["w_ff1"], w_ff2=g["w_ff2"].reshape(D_FF, D_MODEL),
        w_ple_gate=g["w_ple_gate"].reshape(D_MODEL, D_MODEL), w_ple_proj=_from_col_major(g["w_ple_proj"]))


def kernel(x, p, positions, attn_norm_g, w_in, gmlp_ln_g, gmlp_ln_b, gmlp_ws, gmlp_bs, q_norm_g, k_norm_g, sinks, ssm_a_re, ssm_a_im, ssm_log_dt, ssm_b_re, ssm_b_im, ssm_c_re, ssm_c_im, ssm_d, glu_w1, glu_w2, mix_out_g, w_out, mlp_norm_g, w_ff1, w_ff2, ple_norm_g, w_ple_gate, w_ple_proj, loss_target, m_attn_norm_g, m_w_in, m_gmlp_ln_g, m_gmlp_ln_b, m_gmlp_ws, m_gmlp_bs, m_q_norm_g, m_k_norm_g, m_sinks, m_ssm_a_re, m_ssm_a_im, m_ssm_log_dt, m_ssm_b_re, m_ssm_b_im, m_ssm_c_re, m_ssm_c_im, m_ssm_d, m_glu_w1, m_glu_w2, m_mix_out_g, m_w_out, m_mlp_norm_g, m_w_ff1, m_w_ff2, m_ple_norm_g, m_w_ple_gate, m_w_ple_proj, v_attn_norm_g, v_w_in, v_gmlp_ln_g, v_gmlp_ln_b, v_gmlp_ws, v_gmlp_bs, v_q_norm_g, v_k_norm_g, v_sinks, v_ssm_a_re, v_ssm_a_im, v_ssm_log_dt, v_ssm_b_re, v_ssm_b_im, v_ssm_c_re, v_ssm_c_im, v_ssm_d, v_glu_w1, v_glu_w2, v_mix_out_g, v_w_out, v_mlp_norm_g, v_w_ff1, v_w_ff2, v_ple_norm_g, v_w_ple_gate, v_w_ple_proj):
    env = dict(locals())
    P = {n: env[n] for n in WEIGHTS}
    M = {n: env["m_" + n] for n in WEIGHTS}
    V = {n: env["v_" + n] for n in WEIGHTS}
    return _step(x, p, positions, loss_target, P, M, V)


def _step(x, p, positions, loss_target, P, M, V):
    small_shapes = [P[n].shape for n in SMALL]
    me = 4 * lax.axis_index("x") + 2 * lax.axis_index("y") + lax.axis_index("c")
    nothing = jnp.zeros((8, LANES), F32)

    def put(land, own, lead):
        return lax.dynamic_update_slice(land, own.reshape((1,) * len(lead) + own.shape), tuple(lead) + (0,) * own.ndim)

    def to_all(ref, peer):
        return ref

    def from_owner(ref, sender):
        return ref.at[sender]

    def gather_start(l, after):
        shards = [P[n][l].astype(WIRE) for n in SHARDED]
        lands = [lax.empty((N_DEV,) + s.shape, WIRE) for s in shards]
        return _split_start(shards, lands, to_all, from_owner, after, f"gather_start_{l}")

    flying = {0: gather_start(0, nothing)}

    def weights_of(l, h):
        send, recv, shards, lands, _ = flying.pop(l)
        lands = _split_wait(send, recv, shards, lands, to_all, from_owner, h, f"gather_wait_{l}")
        token = None
        if l + 1 < DEPTH:
            flying[l + 1] = gather_start(l + 1, lands[0])
            token = flying[l + 1][4]
        lands = [put(ld, P[n][l].astype(WIRE), (me,)) for n, ld in zip(SHARDED, lands)]
        return _layer_weights(dict(zip(SHARDED, lands))), token

    grad_lands = [lax.empty((N_DEV,) + P[n].shape, WIRE) for n in SHARDED]
    sent = {}

    def to_owner(ref, peer):
        return ref.at[peer]

    def land_own(l, parts):
        return [put(ld, lax.dynamic_index_in_dim(part, me, 0, keepdims=False), (me, l)) for part, ld in zip(parts, grad_lands)]

    def scatter_wait(l, after):
        send, recv, parts, lands = sent.pop(l)
        grad_lands[:] = _split_wait(send, recv, parts, lands, to_owner, lambda ref, sender: ref.at[sender, l], after,
                                    f"scatter_wait_{l}")
        grad_lands[:] = land_own(l, parts)

    def after_bwd(l, g, G):
        if l + 1 in sent:
            scatter_wait(l + 1, g)
        send, recv, parts, lands, token = _split_start([G[n] for n in SHARDED], grad_lands, to_owner,
                                                       lambda ref, sender: ref.at[sender, l], nothing, f"scatter_start_{l}")
        sent[l] = (send, recv, parts, lands)
        return token

    lsum, gx, grads = _local_step(x[0], p[:, 0], positions[0], loss_target[0], P, weights_of, after_bwd)
    sflat = _pack([jnp.stack([grads[l][n] for l in range(DEPTH)]) for n in SMALL], F32)
    land_s = _gather_whole(sflat)
    scatter_wait(0, land_s)
    lands = grad_lands
    G, delta, new_m, new_v = {}, {}, {}, {}
    for n, land in zip(SHARDED, lands):
        shp = P[n].shape
        res = _sum_adamw(land.reshape(N_DEV, -1, shp[-1]), *(a.reshape(-1, shp[-1]) for a in (P[n], M[n], V[n])))
        G[n], delta[n], new_m[n], new_v[n] = (a.reshape(shp) for a in res)
    res = _sum_adamw(land_s, _pack([P[n] for n in SMALL], F32), _pack([M[n] for n in SMALL], F32), _pack([V[n] for n in SMALL], F32))
    for dst, flat in zip((G, delta, new_m, new_v), res):
        dst.update(zip(SMALL, _unpack(flat, small_shapes)))
    loss = lax.psum(lsum, ("x", "y", "c"))
    return (loss, gx[None], *[G[n] for n in WEIGHTS], *[delta[n] for n in WEIGHTS], *[new_m[n] for n in WEIGHTS], *[new_v[n] for n in WEIGHTS])
```

```python
import functools
import math

import jax
import jax.numpy as jnp
from jax import lax
from jax.experimental import pallas as pl
from jax.experimental.pallas import tpu as pltpu

F32 = jnp.float32
MXU = jnp.bfloat16
WIRE = jnp.bfloat16

D_MODEL = 1024
DEPTH = 4
HEAD_DIM = 64
A_WIDTH = 256
A_HEADS = 4
CHUNK = 128
B_WIDTH = 512
WINDOW = 128
C_WIDTH = 256
C_GROUP = 16
C_GROUPS = 16
C_STATE = 64
N_STATE = C_GROUPS * C_STATE
IN_COLS = 1536
D_FF = 4096
PLE_DIM = 256
EPS = 1e-6
ROPE_THETA = 10000.0
SCALE = HEAD_DIM ** -0.5
NEG = -1e30
N_DEV = 8

ADAM_LR = 0.001
ADAM_B1 = 0.9
ADAM_B2 = 0.999
ADAM_EPS = 1e-08
ADAM_WD = 0.01
ADAM_STEP = 10

V7X_VMEM_BYTES = 64 * 2 ** 20
VMEM_LIMIT = V7X_VMEM_BYTES - 8 * 2 ** 20
LANES = 128

MESH = pl.DeviceIdType.MESH


def _cp(*sem):
    return pltpu.CompilerParams(dimension_semantics=sem, vmem_limit_bytes=VMEM_LIMIT)


def _sds(shape, dtype=F32):
    return jax.ShapeDtypeStruct(shape, dtype)


def _mm(a, b):
    return jnp.dot(a.astype(MXU), b.astype(MXU), preferred_element_type=F32)


def _mm_nt(a, b):
    return lax.dot_general(a.astype(MXU), b.astype(MXU), (((1,), (1,)), ((), ())), preferred_element_type=F32)


def _mm_tn(a, b):
    return lax.dot_general(a.astype(MXU), b.astype(MXU), (((0,), (0,)), ((), ())), preferred_element_type=F32)


def _lane(shape):
    return lax.broadcasted_iota(jnp.int32, shape, len(shape) - 1)


def _row(shape):
    return lax.broadcasted_iota(jnp.int32, shape, 0)


_GELU_C = math.sqrt(2.0 / math.pi)


def _gelu(x):
    return 0.5 * x * (1.0 + jnp.tanh(_GELU_C * (x + 0.044715 * (x * x * x))))


def _gelu_grad(x):
    t = jnp.tanh(_GELU_C * (x + 0.044715 * (x * x * x)))
    return 0.5 * (1.0 + t) + 0.5 * x * (1.0 - t * t) * (_GELU_C * (1.0 + 3.0 * 0.044715 * (x * x)))


def _sigmoid(x):
    return 1.0 / (1.0 + jnp.exp(-x))


def _rms_stat(x):
    return lax.rsqrt(jnp.mean(x * x, axis=-1, keepdims=True) + EPS)


def _rms_bwd(x, r, g, dy):
    xh = x * r
    dxh = dy * g
    dx = r * (dxh - xh * jnp.mean(dxh * xh, axis=-1, keepdims=True))
    return dx, jnp.sum(dy * xh, axis=0, keepdims=True)


def _tril(w):
    return jnp.where(_row(w.shape) >= _lane(w.shape), w, 0.0)


def _swap64(x):
    return pltpu.roll(x, HEAD_DIM, 1)


def _group_sum64(x, lo):
    s_lo = jnp.sum(jnp.where(lo, x, 0.0), axis=-1, keepdims=True)
    s_hi = jnp.sum(jnp.where(lo, 0.0, x), axis=-1, keepdims=True)
    return jnp.where(lo, s_lo, s_hi)


def _partner(x):
    n = x.shape[-1]
    first = (_lane(x.shape) % HEAD_DIM) < HEAD_DIM // 2
    return jnp.where(first, pltpu.roll(x, n - HEAD_DIM // 2, 1), pltpu.roll(x, HEAD_DIM // 2, 1))


def _rope(y, cs, sn):
    return y * cs + _partner(y) * sn


def _rope_bwd(d, cs, sn):
    return d * cs + _partner(d * sn)


def _qk_norm_rope(x, g, cs, sn):
    lo = _lane(x.shape) < HEAD_DIM
    r = lax.rsqrt(_group_sum64(x * x, lo) * (1.0 / HEAD_DIM) + EPS)
    xh = x * r
    return _rope(xh * g, cs, sn), xh, r


def _qk_norm_rope_bwd(xh, r, g, cs, sn, d):
    lo = _lane(xh.shape) < HEAD_DIM
    dy = _rope_bwd(d, cs, sn)
    dxh = dy * g
    m = _group_sum64(dxh * xh, lo) * (1.0 / HEAD_DIM)
    return r * (dxh - xh * m), jnp.sum(dy * xh, axis=0, keepdims=True)


def _gmlp_head(blk, g, b):
    hi = _lane(blk.shape) >= HEAD_DIM
    mu = jnp.sum(jnp.where(hi, blk, 0.0), axis=-1, keepdims=True) * (1.0 / HEAD_DIM)
    xc = jnp.where(hi, blk - mu, 0.0)
    rstd = lax.rsqrt(jnp.sum(xc * xc, axis=-1, keepdims=True) * (1.0 / HEAD_DIM) + EPS)
    vhat = xc * rstd
    return vhat * g + b, vhat, rstd


def _rope_tables(pos_col, inv_row):
    T = pos_col.shape[0]
    tm = min(T, 1024)

    def body(p_ref, inv_ref, cs_ref, sn_ref):
        ang = p_ref[...].astype(F32) * inv_ref[...]
        s = jnp.sin(ang)
        cs_ref[...] = jnp.cos(ang)
        sn_ref[...] = jnp.where((_lane(ang.shape) % HEAD_DIM) < HEAD_DIM // 2, -s, s)

    blk = pl.BlockSpec((tm, LANES), lambda i: (i, 0))
    return pl.pallas_call(
        body, name="rope_tables", grid=(T // tm,),
        in_specs=[pl.BlockSpec((tm, 1), lambda i: (i, 0)), pl.BlockSpec((1, LANES), lambda i: (0, 0))],
        out_specs=[blk, blk], out_shape=[_sds((T, LANES))] * 2, compiler_params=_cp("parallel"))(pos_col, inv_row)


def _inproj_fwd(h, g, w):
    T = h.shape[0]
    tm = min(T, 512)

    def body(h_ref, g_ref, w_ref, z_ref):
        x = h_ref[...]
        z_ref[...] = _mm(x * _rms_stat(x) * g_ref[...], w_ref[...])

    return pl.pallas_call(
        body, name="inproj_fwd", grid=(T // tm,),
        in_specs=[pl.BlockSpec((tm, D_MODEL), lambda i: (i, 0)), pl.BlockSpec((1, D_MODEL), lambda i: (0, 0)),
                  pl.BlockSpec((D_MODEL, IN_COLS), lambda i: (0, 0))],
        out_specs=pl.BlockSpec((tm, IN_COLS), lambda i: (i, 0)), out_shape=_sds((T, IN_COLS)),
        compiler_params=_cp("parallel"))(h, g, w)


def _inproj_bwd(gres, h, g, w, dza, dzq, dzk, dzv, dzc):
    T = h.shape[0]
    tm = min(T, 512)

    def body(gr_ref, h_ref, g_ref, w_ref, a_ref, q_ref, k_ref, v_ref, c_ref, dh_ref, xn_ref, dz_ref, dg_ref):
        @pl.when(pl.program_id(0) == 0)
        def _():
            dg_ref[...] = jnp.zeros_like(dg_ref)

        x = h_ref[...]
        r = _rms_stat(x)
        gg = g_ref[...]
        dz = jnp.concatenate([a_ref[...], q_ref[...], k_ref[...], v_ref[...], c_ref[...]], axis=1)
        dxn = _mm_nt(dz, w_ref[...])
        dx, dg = _rms_bwd(x, r, gg, dxn)
        dh_ref[...] = gr_ref[...] + dx
        dg_ref[...] += dg
        xn_ref[...] = (x * r * gg).astype(MXU)
        dz_ref[...] = dz.astype(MXU)

    def rows(w_):
        return pl.BlockSpec((tm, w_), lambda i: (i, 0))

    row = pl.BlockSpec((1, D_MODEL), lambda i: (0, 0))
    return pl.pallas_call(
        body, name="inproj_bwd", grid=(T // tm,),
        in_specs=[rows(D_MODEL), rows(D_MODEL), row, pl.BlockSpec((D_MODEL, IN_COLS), lambda i: (0, 0)),
                  rows(512), rows(512), rows(128), rows(128), rows(256)],
        out_specs=[rows(D_MODEL), rows(D_MODEL), rows(IN_COLS), row],
        out_shape=[_sds((T, D_MODEL)), _sds((T, D_MODEL), MXU), _sds((T, IN_COLS), MXU), _sds((1, D_MODEL))],
        compiler_params=_cp("arbitrary"))(gres, h, g, w, dza, dzq, dzk, dzv, dzc)


def _gmlp_fwd(z, lng, lnb, ws, bsx):
    T = z.shape[0]
    tm = min(T, 512)
    nc = tm // CHUNK

    def body(z_ref, g_ref, b_ref, w_ref, bs_ref, ya_ref):
        zg = _gelu(z_ref[...])
        lo = _lane((tm, LANES)) < HEAD_DIM
        prods = []
        for hd in range(A_HEADS):
            sl = slice(hd * LANES, (hd + 1) * LANES)
            blk = zg[:, sl]
            vn, _, _ = _gmlp_head(blk, g_ref[:, sl], b_ref[:, sl])
            wm = _tril(w_ref[hd])
            sv = jnp.concatenate([_mm(wm, vn[c * CHUNK:(c + 1) * CHUNK]) + bs_ref[hd] for c in range(nc)], axis=0)
            prods.append(blk * _swap64(sv))
        ya_ref[:, 0:LANES] = jnp.where(lo, prods[0], _swap64(prods[1]))
        ya_ref[:, LANES:2 * LANES] = jnp.where(lo, prods[2], _swap64(prods[3]))

    row = pl.BlockSpec((1, 512), lambda i: (0, 0))
    mat = pl.BlockSpec((A_HEADS, CHUNK, CHUNK), lambda i: (0, 0, 0))
    return pl.pallas_call(
        body, name="gmlp_fwd", grid=(T // tm,),
        in_specs=[pl.BlockSpec((tm, 512), lambda i: (i, 0)), row, row, mat, mat],
        out_specs=pl.BlockSpec((tm, A_WIDTH), lambda i: (i, 0)), out_shape=_sds((T, A_WIDTH)),
        compiler_params=_cp("parallel"))(z, lng, lnb, ws, bsx)


def _gmlp_bwd(z, dya, lng, lnb, ws, bsx):
    T = z.shape[0]
    tm = min(T, 512)
    nc = tm // CHUNK

    def body(z_ref, dya_ref, g_ref, b_ref, w_ref, bs_ref, dza_ref, dw_ref, dbs_ref, dg_ref, db_ref):
        @pl.when(pl.program_id(0) == 0)
        def _():
            dw_ref[...] = jnp.zeros_like(dw_ref)
            dbs_ref[...] = jnp.zeros_like(dbs_ref)
            dg_ref[...] = jnp.zeros_like(dg_ref)
            db_ref[...] = jnp.zeros_like(db_ref)

        za = z_ref[...]
        zg = _gelu(za)
        gp = _gelu_grad(za)
        lo = _lane((tm, LANES)) < HEAD_DIM
        for hd in range(A_HEADS):
            sl = slice(hd * LANES, (hd + 1) * LANES)
            blk = zg[:, sl]
            g = g_ref[:, sl]
            vn, vhat, rstd = _gmlp_head(blk, g, b_ref[:, sl])
            wm = _tril(w_ref[hd])
            pair = dya_ref[:, (hd // 2) * LANES:(hd // 2 + 1) * LANES]
            dy = jnp.where(lo, pair if hd % 2 == 0 else _swap64(pair), 0.0)
            dsv = _swap64(dy * blk)
            svs, dvns = [], []
            dw = jnp.zeros((CHUNK, CHUNK), F32)
            dbs = jnp.zeros((CHUNK, 1), F32)
            for c in range(nc):
                cs = slice(c * CHUNK, (c + 1) * CHUNK)
                svs.append(_mm(wm, vn[cs]) + bs_ref[hd])
                dw = dw + _mm_nt(dsv[cs], vn[cs])
                dbs = dbs + jnp.sum(dsv[cs], axis=-1, keepdims=True)
                dvns.append(_mm_tn(wm, dsv[cs]))
            sv = jnp.concatenate(svs, axis=0)
            dvn = jnp.concatenate(dvns, axis=0)
            dw_ref[hd] += _tril(dw)
            dbs_ref[hd] += jnp.broadcast_to(dbs, (CHUNK, CHUNK))
            dg_ref[:, sl] += jnp.sum(dvn * vhat, axis=0, keepdims=True)
            db_ref[:, sl] += jnp.sum(dvn, axis=0, keepdims=True)
            du = dy * _swap64(sv)
            dvh = dvn * g
            m1 = jnp.sum(dvh, axis=-1, keepdims=True) * (1.0 / HEAD_DIM)
            m2 = jnp.sum(dvh * vhat, axis=-1, keepdims=True) * (1.0 / HEAD_DIM)
            dv = jnp.where(lo, 0.0, rstd * (dvh - m1 - vhat * m2))
            dza_ref[:, sl] = (du + dv) * gp[:, sl]

    row = pl.BlockSpec((1, 512), lambda i: (0, 0))
    mat = pl.BlockSpec((A_HEADS, CHUNK, CHUNK), lambda i: (0, 0, 0))
    return pl.pallas_call(
        body, name="gmlp_bwd", grid=(T // tm,),
        in_specs=[pl.BlockSpec((tm, 512), lambda i: (i, 0)), pl.BlockSpec((tm, A_WIDTH), lambda i: (i, 0)), row, row, mat, mat],
        out_specs=[pl.BlockSpec((tm, 512), lambda i: (i, 0)), mat, mat, row, row],
        out_shape=[_sds((T, 512)), _sds((A_HEADS, CHUNK, CHUNK)), _sds((A_HEADS, CHUNK, CHUNK)), _sds((1, 512)), _sds((1, 512))],
        compiler_params=_cp("arbitrary"))(z, dya, lng, lnb, ws, bsx)


def _attn_specs(T, tq, tile_of):
    nb = tq // WINDOW

    def prev(i):
        return jnp.maximum(tile_of(i) * nb - 1, 0)

    row = pl.BlockSpec((1, LANES), lambda i: (0, 0))
    return [
        pl.BlockSpec((tq, B_WIDTH), lambda i: (tile_of(i), 1)),
        pl.BlockSpec((tq, LANES), lambda i: (tile_of(i), 8)),
        pl.BlockSpec((tq, LANES), lambda i: (tile_of(i), 9)),
        pl.BlockSpec((WINDOW, LANES), lambda i: (prev(i), 8)),
        pl.BlockSpec((WINDOW, LANES), lambda i: (prev(i), 9)),
        pl.BlockSpec((tq, LANES), lambda i: (tile_of(i), 0)),
        pl.BlockSpec((tq, LANES), lambda i: (tile_of(i), 0)),
        pl.BlockSpec((WINDOW, LANES), lambda i: (prev(i), 0)),
        pl.BlockSpec((WINDOW, LANES), lambda i: (prev(i), 0)),
        row, row,
        pl.BlockSpec((8, LANES), lambda i: (0, 0)),
    ]


def _attn_bias(first):
    qi = lax.broadcasted_iota(jnp.int32, (WINDOW, 2 * WINDOW), 0)
    kj = lax.broadcasted_iota(jnp.int32, (WINDOW, 2 * WINDOW), 1)
    diff = qi + WINDOW - kj
    ok = (diff >= 0) & (diff < WINDOW) & ((kj >= WINDOW) | jnp.logical_not(first))
    return jnp.where(ok, 0.0, NEG)


def _dup_heads(x, lo):
    sw = _swap64(x)
    return jnp.where(lo, x, sw), jnp.where(lo, sw, x)


HEADS_PER_KV = 4


def _stack_heads(x0, x1, lo):
    return jnp.concatenate([jnp.where(lo, x0, 0.0), jnp.where(lo, 0.0, x0), jnp.where(lo, x1, 0.0), jnp.where(lo, 0.0, x1)], axis=0)


def _unstack_heads(x4, lo):
    return (jnp.where(lo, x4[0:WINDOW], x4[WINDOW:2 * WINDOW]), jnp.where(lo, x4[2 * WINDOW:3 * WINDOW], x4[3 * WINDOW:4 * WINDOW]))


def _sink_column(sk_ref, g):
    return jnp.concatenate([jnp.broadcast_to(sk_ref[a:a + 1, 0:1], (WINDOW, 1)) for a in range(HEADS_PER_KV * g, HEADS_PER_KV * (g + 1))], axis=0)


def _attn_probs(q4, kw, bias, sink):
    s = _mm_nt(q4, kw)
    s = (s.reshape(HEADS_PER_KV, WINDOW, 2 * WINDOW) + bias[None]).reshape(HEADS_PER_KV * WINDOW, 2 * WINDOW)
    m = jnp.maximum(jnp.max(s, axis=-1, keepdims=True), sink)
    p = jnp.exp(s - m)
    es = jnp.exp(sink - m)
    inv = 1.0 / (jnp.sum(p, axis=-1, keepdims=True) + es)
    return p * inv, es * inv


def _attn_fwd(z, cs, sn, qg, kg, sinks):
    T = z.shape[0]
    tq = min(T, 512)
    nb = tq // WINDOW

    def body(q_ref, k_ref, v_ref, kp_ref, vp_ref, cs_ref, sn_ref, csp_ref, snp_ref, qg_ref, kg_ref, sk_ref, o_ref):
        i = pl.program_id(0)
        csq, snq = cs_ref[...], sn_ref[...]
        cs_all = jnp.concatenate([csp_ref[...], csq], axis=0)
        sn_all = jnp.concatenate([snp_ref[...], snq], axis=0)
        k_all = jnp.concatenate([kp_ref[...], k_ref[...]], axis=0)
        v_all = jnp.concatenate([vp_ref[...], v_ref[...]], axis=0)
        kr, _, _ = _qk_norm_rope(k_all, kg_ref[...], cs_all, sn_all)
        lo_all = _lane(k_all.shape) < HEAD_DIM
        kd = _dup_heads(kr, lo_all)
        vd = _dup_heads(v_all, lo_all)
        lo = _lane((WINDOW, LANES)) < HEAD_DIM
        qrs = [_qk_norm_rope(q_ref[:, pr * LANES:(pr + 1) * LANES], qg_ref[...], csq, snq)[0] * SCALE for pr in range(4)]
        biases = [_attn_bias(i * nb + b == 0) for b in range(nb)]
        for g in range(2):
            sink = _sink_column(sk_ref, g)
            for b in range(nb):
                bs = slice(b * WINDOW, (b + 1) * WINDOW)
                ws = slice(b * WINDOW, (b + 2) * WINDOW)
                pn, _ = _attn_probs(_stack_heads(qrs[2 * g][bs], qrs[2 * g + 1][bs], lo), kd[g][ws], biases[b], sink)
                o0, o1 = _unstack_heads(_mm(pn, vd[g][ws]), lo)
                o_ref[bs, 2 * g * LANES:(2 * g + 1) * LANES] = o0
                o_ref[bs, (2 * g + 1) * LANES:(2 * g + 2) * LANES] = o1

    return pl.pallas_call(
        body, name="attn_fwd", grid=(T // tq,),
        in_specs=_attn_specs(T, tq, lambda i: i),
        out_specs=pl.BlockSpec((tq, B_WIDTH), lambda i: (i, 0)), out_shape=_sds((T, B_WIDTH)),
        compiler_params=_cp("parallel"))(z, z, z, z, z, cs, sn, cs, sn, qg, kg, sinks)


def _attn_bwd(z, cs, sn, qg, kg, sinks, o, do):
    T = z.shape[0]
    tq = min(T, 512)
    nb = tq // WINDOW
    nt = T // tq
    tk = tq + WINDOW

    def tile_of(i):
        return nt - 1 - i

    def body(q_ref, k_ref, v_ref, kp_ref, vp_ref, cs_ref, sn_ref, csp_ref, snp_ref, qg_ref, kg_ref, sk_ref, o_ref, do_ref,
             dq_ref, dk_ref, dv_ref, dqg_ref, dkg_ref, dsk_ref, acck, accv, ck, cv):
        i = pl.program_id(0)
        ti = nt - 1 - i

        @pl.when(i == 0)
        def _():
            dqg_ref[...] = jnp.zeros_like(dqg_ref)
            dkg_ref[...] = jnp.zeros_like(dkg_ref)
            dsk_ref[...] = jnp.zeros_like(dsk_ref)
            ck[...] = jnp.zeros_like(ck)
            cv[...] = jnp.zeros_like(cv)

        csq, snq = cs_ref[...], sn_ref[...]
        cs_all = jnp.concatenate([csp_ref[...], csq], axis=0)
        sn_all = jnp.concatenate([snp_ref[...], snq], axis=0)
        k_all = jnp.concatenate([kp_ref[...], k_ref[...]], axis=0)
        v_all = jnp.concatenate([vp_ref[...], v_ref[...]], axis=0)
        kr, kh, rk = _qk_norm_rope(k_all, kg_ref[...], cs_all, sn_all)
        lo_all = _lane(k_all.shape) < HEAD_DIM
        kd = _dup_heads(kr, lo_all)
        vd = _dup_heads(v_all, lo_all)
        lo = _lane((WINDOW, LANES)) < HEAD_DIM
        acck[...] = jnp.zeros_like(acck)
        accv[...] = jnp.zeros_like(accv)
        prep = [_qk_norm_rope(q_ref[:, pr * LANES:(pr + 1) * LANES], qg_ref[...], csq, snq) for pr in range(4)]
        biases = [_attn_bias(ti * nb + b == 0) for b in range(nb)]
        dqs = [[None] * nb for _ in range(4)]
        for g in range(2):
            sink = _sink_column(sk_ref, g)
            dsink = jnp.zeros((HEADS_PER_KV * WINDOW, 1), F32)
            for b in range(nb):
                bs = slice(b * WINDOW, (b + 1) * WINDOW)
                ws = slice(b * WINDOW, (b + 2) * WINDOW)
                kw, vw = kd[g][ws], vd[g][ws]
                q4 = _stack_heads(prep[2 * g][0][bs] * SCALE, prep[2 * g + 1][0][bs] * SCALE, lo)
                pn, psink = _attn_probs(q4, kw, biases[b], sink)
                o0, o1 = o_ref[bs, 2 * g * LANES:(2 * g + 1) * LANES], o_ref[bs, (2 * g + 1) * LANES:(2 * g + 2) * LANES]
                do4 = _stack_heads(do_ref[bs, 2 * g * LANES:(2 * g + 1) * LANES], do_ref[bs, (2 * g + 1) * LANES:(2 * g + 2) * LANES], lo)
                delta = jnp.sum(do4 * jnp.concatenate([o0, o0, o1, o1], axis=0), axis=-1, keepdims=True)
                ds = pn * (_mm_nt(do4, vw) - delta)
                dsink = dsink - psink * delta
                dqs[2 * g][b], dqs[2 * g + 1][b] = _unstack_heads(_mm(ds, kw) * SCALE, lo)
                acck[g, ws, :] += _mm_tn(ds, q4)
                accv[g, ws, :] += _mm_tn(pn, do4)
            for hh in range(HEADS_PER_KV):
                a = HEADS_PER_KV * g + hh
                dsk_ref[a:a + 1, :] += jnp.zeros((1, LANES), F32) + jnp.sum(dsink[hh * WINDOW:(hh + 1) * WINDOW])
        for pr in range(4):
            _, qh, rq = prep[pr]
            dx, dg = _qk_norm_rope_bwd(qh, rq, qg_ref[...], csq, snq, jnp.concatenate(dqs[pr], axis=0))
            dq_ref[:, pr * LANES:(pr + 1) * LANES] = dx
            dqg_ref[...] += dg

        def fold(acc):
            f0 = acc[0] + _swap64(acc[0])
            f1 = acc[1] + _swap64(acc[1])
            return jnp.where(lo_all, f0, f1)

        dk_all = fold(acck)
        dv_all = fold(accv)
        pad = jnp.zeros((tq - WINDOW, LANES), F32)
        dk_own = dk_all[WINDOW:] + (jnp.concatenate([pad, ck[...]], axis=0) if nb > 1 else ck[...])
        dv_own = dv_all[WINDOW:] + (jnp.concatenate([pad, cv[...]], axis=0) if nb > 1 else cv[...])
        ck[...] = dk_all[:WINDOW]
        cv[...] = dv_all[:WINDOW]
        dxk, dgk = _qk_norm_rope_bwd(kh[WINDOW:], rk[WINDOW:], kg_ref[...], csq, snq, dk_own)
        dk_ref[...] = dxk
        dkg_ref[...] += dgk
        dv_ref[...] = dv_own

    row = pl.BlockSpec((1, LANES), lambda i: (0, 0))
    return pl.pallas_call(
        body, name="attn_bwd", grid=(nt,),
        in_specs=_attn_specs(T, tq, tile_of) + [pl.BlockSpec((tq, B_WIDTH), lambda i: (tile_of(i), 0))] * 2,
        out_specs=[pl.BlockSpec((tq, B_WIDTH), lambda i: (tile_of(i), 0)), pl.BlockSpec((tq, LANES), lambda i: (tile_of(i), 0)),
                   pl.BlockSpec((tq, LANES), lambda i: (tile_of(i), 0)), row, row, pl.BlockSpec((8, LANES), lambda i: (0, 0))],
        out_shape=[_sds((T, B_WIDTH)), _sds((T, LANES)), _sds((T, LANES)), _sds((1, LANES)), _sds((1, LANES)), _sds((8, LANES))],
        scratch_shapes=[pltpu.VMEM((2, tk, LANES), F32), pltpu.VMEM((2, tk, LANES), F32),
                        pltpu.VMEM((WINDOW, LANES), F32), pltpu.VMEM((WINDOW, LANES), F32)],
        compiler_params=_cp("arbitrary"))(z, z, z, z, z, cs, sn, cs, sn, qg, kg, sinks, o, do)


def _bbar_t(are, aim, ldt, btr, bti):
    lbr, lbi = _lam_bar(are, aim, ldt)
    den = are * are + aim * aim
    nr = lbr - 1.0
    cr = (nr * are + lbi * aim) / den
    ci = (lbi * are - nr * aim) / den
    return cr * btr - ci * bti, cr * bti + ci * btr


def _lam_bar(are, aim, ldt):
    dt = jnp.exp(ldt)
    er = jnp.exp(are * dt)
    return er * jnp.cos(aim * dt), er * jnp.sin(aim * dt)


def _block_diag(x):
    t = jnp.concatenate([x] * C_GROUPS, axis=1)
    return jnp.where(_row(t.shape) // C_GROUP == _lane(t.shape) // C_STATE, t, 0.0)


def _block_diag_fold(m):
    rg = _row((C_WIDTH, C_STATE)) // C_GROUP
    acc = jnp.zeros((C_WIDTH, C_STATE), F32)
    for g in range(C_GROUPS):
        acc = acc + jnp.where(rg == g, m[:, g * C_STATE:(g + 1) * C_STATE], 0.0)
    return acc


def _ssm_prep(are, aim, ldt, are_x, aim_x, ldt_x, btr, bti, cre, cim):
    def body(are_r, aim_r, ldt_r, arex_r, aimx_r, ldtx_r, btr_r, bti_r, cre_r, cim_r, bbd_ref, cbd_ref, lr_ref, li_ref):
        lr, li = _lam_bar(are_r[...], aim_r[...], ldt_r[...])
        lr_ref[...] = lr
        li_ref[...] = li
        br, bi = _bbar_t(arex_r[...], aimx_r[...], ldtx_r[...], btr_r[...], bti_r[...])
        bbd_ref[...] = jnp.concatenate([_block_diag(br), _block_diag(bi)], axis=1).astype(MXU)
        cbd_ref[...] = jnp.concatenate([_block_diag(cre_r[...]), -_block_diag(cim_r[...])], axis=1).astype(MXU)

    return pl.pallas_call(
        body, name="ssm_prep",
        out_shape=[_sds((C_WIDTH, 2 * N_STATE), MXU), _sds((C_WIDTH, 2 * N_STATE), MXU), _sds((C_GROUPS, C_STATE)), _sds((C_GROUPS, C_STATE))],
        compiler_params=pltpu.CompilerParams(vmem_limit_bytes=VMEM_LIMIT))(are, aim, ldt, are_x, aim_x, ldt_x, btr, bti, cre, cim)


def _ssm_param_bwd(are, aim, ldt, are_x, aim_x, ldt_x, btr, bti, dlr, dli, dbbd, dcr, dci):
    def body(are_r, aim_r, ldt_r, arex_r, aimx_r, ldtx_r, btr_r, bti_r, dlr_r, dli_r, dbbd_r, dcr_r, dci_r,
             dare_ref, daim_ref, dldt_ref, dbtr_ref, dbti_ref, dcre_ref, dcim_ref):
        _, vjp_l = jax.vjp(_lam_bar, are_r[...], aim_r[...], ldt_r[...])
        da1, di1, dl1 = vjp_l((dlr_r[...], dli_r[...]))
        dbr = _block_diag_fold(dbbd_r[:, 0:N_STATE])
        dbi = _block_diag_fold(dbbd_r[:, N_STATE:2 * N_STATE])
        _, vjp_b = jax.vjp(_bbar_t, arex_r[...], aimx_r[...], ldtx_r[...], btr_r[...], bti_r[...])
        da2, di2, dl2, dbtr, dbti = vjp_b((dbr, dbi))

        def gsum(x):
            return x.reshape(C_GROUPS, C_GROUP, C_STATE).sum(axis=1)

        dare_ref[...] = da1 + gsum(da2)
        daim_ref[...] = di1 + gsum(di2)
        dldt_ref[...] = jnp.broadcast_to(jnp.sum(dl1 + gsum(dl2), axis=-1, keepdims=True), (C_GROUPS, LANES))
        dbtr_ref[...] = dbtr
        dbti_ref[...] = dbti
        dcre_ref[...] = _block_diag_fold(dcr_r[...])
        dcim_ref[...] = -_block_diag_fold(dci_r[...])

    g = _sds((C_GROUPS, C_STATE))
    x = _sds((C_WIDTH, C_STATE))
    return pl.pallas_call(
        body, name="ssm_param_bwd", out_shape=[g, g, _sds((C_GROUPS, LANES)), x, x, x, x],
        compiler_params=pltpu.CompilerParams(vmem_limit_bytes=VMEM_LIMIT))(are, aim, ldt, are_x, aim_x, ldt_x, btr, bti, dlr, dli, dbbd, dcr, dci)


def _ssm_fwd(z, bbd, lr, li, cbd, dsk, w1, w2):
    T = z.shape[0]
    tt = min(T, 256)

    def body(u_ref, bbd_ref, lr_ref, li_ref, cbd_ref, d_ref, w1_ref, w2_ref, yc_ref, xr_ref, xi_ref, sr, si):
        @pl.when(pl.program_id(0) == 0)
        def _():
            sr[...] = jnp.zeros_like(sr)
            si[...] = jnp.zeros_like(si)

        u = u_ref[...]
        bu = _mm(u, bbd_ref[...])
        xr_ref[...] = bu[:, 0:N_STATE]
        xi_ref[...] = bu[:, N_STATE:2 * N_STATE]
        lam_r, lam_i = lr_ref[...], li_ref[...]

        def step(t, c):
            cr, ci = c
            nr = lam_r * cr - lam_i * ci + xr_ref[pl.ds(t, 1), :]
            ni = lam_r * ci + lam_i * cr + xi_ref[pl.ds(t, 1), :]
            xr_ref[pl.ds(t, 1), :] = nr
            xi_ref[pl.ds(t, 1), :] = ni
            return nr, ni

        cr, ci = lax.fori_loop(0, tt, step, (sr[...], si[...]), unroll=8)
        sr[...] = cr
        si[...] = ci
        x = jnp.concatenate([xr_ref[...], xi_ref[...]], axis=1)
        y2 = _gelu(_mm_nt(x, cbd_ref[...]) + d_ref[...] * u)
        yc_ref[...] = _mm(y2, w1_ref[...]) * _sigmoid(_mm(y2, w2_ref[...]))

    big = pl.BlockSpec((C_WIDTH, 2 * N_STATE), lambda i: (0, 0))
    srow = pl.BlockSpec((1, N_STATE), lambda i: (0, 0))
    wsp = pl.BlockSpec((C_WIDTH, C_WIDTH), lambda i: (0, 0))
    xs = pl.BlockSpec((tt, N_STATE), lambda i: (i, 0))
    return pl.pallas_call(
        body, name="ssm_fwd", grid=(T // tt,),
        in_specs=[pl.BlockSpec((tt, C_WIDTH), lambda i: (i, 5)), big, srow, srow, big, pl.BlockSpec((1, C_WIDTH), lambda i: (0, 0)), wsp, wsp],
        out_specs=[pl.BlockSpec((tt, C_WIDTH), lambda i: (i, 0)), xs, xs],
        out_shape=[_sds((T, C_WIDTH)), _sds((T, N_STATE)), _sds((T, N_STATE))],
        scratch_shapes=[pltpu.VMEM((1, N_STATE), F32)] * 2,
        compiler_params=_cp("arbitrary"))(z, bbd, lr, li, cbd, dsk, w1, w2)


def _ssm_bwd(dyc, z, xr, xi, bbd, lr, li, cbd, dsk, w1, w2):
    T = z.shape[0]
    tt = min(T, 256)
    nt = T // tt

    def tile_of(i):
        return nt - 1 - i

    def body(dyc_ref, u_ref, xr_ref, xi_ref, xpr_ref, xpi_ref, bbd_ref, lr_ref, li_ref, cbd_ref, d_ref, w1_ref, w2_ref,
             du_ref, y2_ref, da1_ref, da2_ref, dy_ref, arai_ref, dlr_ref, dli_ref, dd_ref, gr, gi, sr, si):
        i = pl.program_id(0)
        ti = nt - 1 - i

        @pl.when(i == 0)
        def _():
            sr[...] = jnp.zeros_like(sr)
            si[...] = jnp.zeros_like(si)
            dlr_ref[...] = jnp.zeros_like(dlr_ref)
            dli_ref[...] = jnp.zeros_like(dli_ref)
            dd_ref[...] = jnp.zeros_like(dd_ref)

        u = u_ref[...]
        xr_t, xi_t = xr_ref[...], xi_ref[...]
        y = _mm_nt(jnp.concatenate([xr_t, xi_t], axis=1), cbd_ref[...]) + d_ref[...] * u
        y2 = _gelu(y)
        a1 = _mm(y2, w1_ref[...])
        sg = _sigmoid(_mm(y2, w2_ref[...]))
        dyc_t = dyc_ref[...]
        da1 = dyc_t * sg
        da2 = dyc_t * a1 * sg * (1.0 - sg)
        dy = (_mm_nt(da1, w1_ref[...]) + _mm_nt(da2, w2_ref[...])) * _gelu_grad(y)
        gx = _mm(dy, cbd_ref[...])
        gr[...] = gx[:, 0:N_STATE]
        gi[...] = gx[:, N_STATE:2 * N_STATE]
        lam_r, lam_i = lr_ref[...], li_ref[...]

        def step(k, c):
            t = tt - 1 - k
            cr, ci = c
            nr = gr[pl.ds(t, 1), :] + lam_r * cr + lam_i * ci
            ni = gi[pl.ds(t, 1), :] - lam_i * cr + lam_r * ci
            gr[pl.ds(t, 1), :] = nr
            gi[pl.ds(t, 1), :] = ni
            return nr, ni

        cr, ci = lax.fori_loop(0, tt, step, (sr[...], si[...]), unroll=8)
        sr[...] = cr
        si[...] = ci
        ar, ai = gr[...], gi[...]
        first_row = _row(ar.shape) == 0
        live = jnp.where(ti > 0, 1.0, 0.0)
        xsr = jnp.where(first_row, xpr_ref[7:8, :] * live, pltpu.roll(xr_t, 1, 0))
        xsi = jnp.where(first_row, xpi_ref[7:8, :] * live, pltpu.roll(xi_t, 1, 0))
        dlr_ref[...] += jnp.sum(ar * xsr + ai * xsi, axis=0, keepdims=True)
        dli_ref[...] += jnp.sum(ai * xsr - ar * xsi, axis=0, keepdims=True)
        dd_ref[...] += jnp.sum(dy * u, axis=0, keepdims=True)
        arai = jnp.concatenate([ar, ai], axis=1)
        du_ref[...] = _mm_nt(arai, bbd_ref[...]) + d_ref[...] * dy
        y2_ref[...] = y2.astype(MXU)
        da1_ref[...] = da1.astype(MXU)
        da2_ref[...] = da2.astype(MXU)
        dy_ref[...] = dy.astype(MXU)
        arai_ref[...] = arai.astype(MXU)

    def prev(i):
        return jnp.maximum(tile_of(i) * (tt // 8) - 1, 0)

    big = pl.BlockSpec((C_WIDTH, 2 * N_STATE), lambda i: (0, 0))
    srow = pl.BlockSpec((1, N_STATE), lambda i: (0, 0))
    wsp = pl.BlockSpec((C_WIDTH, C_WIDTH), lambda i: (0, 0))
    xs = pl.BlockSpec((tt, N_STATE), lambda i: (tile_of(i), 0))
    xp = pl.BlockSpec((8, N_STATE), lambda i: (prev(i), 0))
    cw = pl.BlockSpec((tt, C_WIDTH), lambda i: (tile_of(i), 0))
    drow = pl.BlockSpec((1, C_WIDTH), lambda i: (0, 0))
    return pl.pallas_call(
        body, name="ssm_bwd", grid=(nt,),
        in_specs=[cw, pl.BlockSpec((tt, C_WIDTH), lambda i: (tile_of(i), 5)), xs, xs, xp, xp, big, srow, srow, big, drow, wsp, wsp],
        out_specs=[cw, cw, cw, cw, cw, pl.BlockSpec((tt, 2 * N_STATE), lambda i: (tile_of(i), 0)), srow, srow, drow],
        out_shape=[_sds((T, C_WIDTH))] + [_sds((T, C_WIDTH), MXU)] * 4 + [_sds((T, 2 * N_STATE), MXU), _sds((1, N_STATE)), _sds((1, N_STATE)), _sds((1, C_WIDTH))],
        scratch_shapes=[pltpu.VMEM((tt, N_STATE), F32)] * 2 + [pltpu.VMEM((1, N_STATE), F32)] * 2,
        compiler_params=_cp("arbitrary"))(dyc, z, xr, xi, xr, xi, bbd, lr, li, cbd, dsk, w1, w2)


_GROUPS = ((0, A_WIDTH), (A_WIDTH, A_WIDTH + B_WIDTH), (A_WIDTH + B_WIDTH, D_MODEL))


def _merge_fwd(h, ya, yb, yc, g, w):
    T = h.shape[0]
    tm = min(T, 512)

    def body(h_ref, a_ref, b_ref, c_ref, g_ref, w_ref, o_ref):
        yn = jnp.concatenate([y * _rms_stat(y) for y in (a_ref[...], b_ref[...], c_ref[...])], axis=1) * g_ref[...]
        o_ref[...] = h_ref[...] + _mm(yn, w_ref[...])

    def rows(w_):
        return pl.BlockSpec((tm, w_), lambda i: (i, 0))

    return pl.pallas_call(
        body, name="merge_fwd", grid=(T // tm,),
        in_specs=[rows(D_MODEL), rows(A_WIDTH), rows(B_WIDTH), rows(C_WIDTH), pl.BlockSpec((1, D_MODEL), lambda i: (0, 0)),
                  pl.BlockSpec((D_MODEL, D_MODEL), lambda i: (0, 0))],
        out_specs=rows(D_MODEL), out_shape=_sds((T, D_MODEL)), compiler_params=_cp("parallel"))(h, ya, yb, yc, g, w)


def _merge_bwd(gres, ya, yb, yc, g, w):
    T = gres.shape[0]
    tm = min(T, 512)

    def body(gr_ref, a_ref, b_ref, c_ref, g_ref, w_ref, da_ref, db_ref, dc_ref, yn_ref, dg_ref):
        @pl.when(pl.program_id(0) == 0)
        def _():
            dg_ref[...] = jnp.zeros_like(dg_ref)

        dyn = _mm_nt(gr_ref[...], w_ref[...])
        yns, dgs = [], []
        for (c0, c1), y_ref, d_ref in zip(_GROUPS, (a_ref, b_ref, c_ref), (da_ref, db_ref, dc_ref)):
            y = y_ref[...]
            r = _rms_stat(y)
            gg = g_ref[:, c0:c1]
            dx, dg = _rms_bwd(y, r, gg, dyn[:, c0:c1])
            d_ref[...] = dx
            dgs.append(dg)
            yns.append(y * r * gg)
        dg_ref[...] += jnp.concatenate(dgs, axis=1)
        yn_ref[...] = jnp.concatenate(yns, axis=1).astype(MXU)

    def rows(w_):
        return pl.BlockSpec((tm, w_), lambda i: (i, 0))

    row = pl.BlockSpec((1, D_MODEL), lambda i: (0, 0))
    return pl.pallas_call(
        body, name="merge_bwd", grid=(T // tm,),
        in_specs=[rows(D_MODEL), rows(A_WIDTH), rows(B_WIDTH), rows(C_WIDTH), row, pl.BlockSpec((D_MODEL, D_MODEL), lambda i: (0, 0))],
        out_specs=[rows(A_WIDTH), rows(B_WIDTH), rows(C_WIDTH), rows(D_MODEL), row],
        out_shape=[_sds((T, A_WIDTH)), _sds((T, B_WIDTH)), _sds((T, C_WIDTH)), _sds((T, D_MODEL), MXU), _sds((1, D_MODEL))],
        compiler_params=_cp("arbitrary"))(gres, ya, yb, yc, g, w)


FF_BLOCK = D_FF // N_DEV


def _mlp_fwd(h, g, w1, w2):
    T = h.shape[0]
    tm = min(T, 1024)

    def body(h_ref, g_ref, w1_ref, w2_ref, o_ref, hn):
        @pl.when(pl.program_id(1) == 0)
        def _():
            x = h_ref[...]
            hn[...] = (x * _rms_stat(x) * g_ref[...]).astype(MXU)
            o_ref[...] = x

        a = jnp.maximum(_mm(hn[...], w1_ref[...]), 0.0)
        o_ref[...] += _mm(a * a, w2_ref[...])

    return pl.pallas_call(
        body, name="mlp_fwd", grid=(T // tm, N_DEV),
        in_specs=[pl.BlockSpec((tm, D_MODEL), lambda i, j: (i, 0)), pl.BlockSpec((1, D_MODEL), lambda i, j: (0, 0)),
                  pl.BlockSpec((None, D_MODEL, FF_BLOCK), lambda i, j: (j, 0, 0)), pl.BlockSpec((FF_BLOCK, D_MODEL), lambda i, j: (j, 0))],
        out_specs=pl.BlockSpec((tm, D_MODEL), lambda i, j: (i, 0)), out_shape=_sds((T, D_MODEL)),
        scratch_shapes=[pltpu.VMEM((tm, D_MODEL), MXU)],
        compiler_params=_cp("parallel", "arbitrary"))(h, g, w1, w2)


def _mlp_bwd(gres, h, g, w1, w2):
    T = h.shape[0]
    tm = min(T, 1024)

    def body(gr_ref, h_ref, g_ref, w1_ref, w2_ref, dh_ref, hn_ref, da_ref, r_ref, dg_ref, acc):
        i, j = pl.program_id(0), pl.program_id(1)

        @pl.when((i == 0) & (j == 0))
        def _():
            dg_ref[...] = jnp.zeros_like(dg_ref)

        @pl.when(j == 0)
        def _():
            x = h_ref[...]
            hn_ref[...] = (x * _rms_stat(x) * g_ref[...]).astype(MXU)
            acc[...] = jnp.zeros_like(acc)

        a = jnp.maximum(_mm(hn_ref[...], w1_ref[...]), 0.0)
        da = _mm_nt(gr_ref[...], w2_ref[...]) * (2.0 * a)
        acc[...] += _mm_nt(da, w1_ref[...])
        da_ref[...] = da.astype(MXU)
        r_ref[...] = (a * a).astype(MXU)

        @pl.when(j == N_DEV - 1)
        def _():
            x = h_ref[...]
            dx, dg = _rms_bwd(x, _rms_stat(x), g_ref[...], acc[...])
            dh_ref[...] = gr_ref[...] + dx
            dg_ref[...] += dg

    rows = pl.BlockSpec((tm, D_MODEL), lambda i, j: (i, 0))
    row = pl.BlockSpec((1, D_MODEL), lambda i, j: (0, 0))
    ffb = pl.BlockSpec((tm, FF_BLOCK), lambda i, j: (i, j))
    return pl.pallas_call(
        body, name="mlp_bwd", grid=(T // tm, N_DEV),
        in_specs=[rows, rows, row, pl.BlockSpec((None, D_MODEL, FF_BLOCK), lambda i, j: (j, 0, 0)),
                  pl.BlockSpec((FF_BLOCK, D_MODEL), lambda i, j: (j, 0))],
        out_specs=[rows, rows, ffb, ffb, row],
        out_shape=[_sds((T, D_MODEL)), _sds((T, D_MODEL), MXU), _sds((T, D_FF), MXU), _sds((T, D_FF), MXU), _sds((1, D_MODEL))],
        scratch_shapes=[pltpu.VMEM((tm, D_MODEL), F32)],
        compiler_params=_cp("arbitrary", "arbitrary"))(gres, h, g, w1, w2)


def _ple_fwd(h, p, g, wg, wp):
    T = h.shape[0]
    tm = min(T, 512)

    def body(h_ref, p_ref, g_ref, wg_ref, wp_ref, o_ref):
        x = h_ref[...]
        gate = _sigmoid(_mm(x * _rms_stat(x) * g_ref[...], wg_ref[...]))
        o_ref[...] = x + gate * _mm(p_ref[...], wp_ref[...])

    rows = pl.BlockSpec((tm, D_MODEL), lambda i: (i, 0))
    return pl.pallas_call(
        body, name="ple_fwd", grid=(T // tm,),
        in_specs=[rows, pl.BlockSpec((tm, PLE_DIM), lambda i: (i, 0)), pl.BlockSpec((1, D_MODEL), lambda i: (0, 0)),
                  pl.BlockSpec((D_MODEL, D_MODEL), lambda i: (0, 0)), pl.BlockSpec((PLE_DIM, D_MODEL), lambda i: (0, 0))],
        out_specs=rows, out_shape=_sds((T, D_MODEL)), compiler_params=_cp("parallel"))(h, p, g, wg, wp)


def _ple_bwd(gres, h, p, g, wg, wp):
    T = h.shape[0]
    tm = min(T, 512)

    def body(gr_ref, h_ref, p_ref, g_ref, wg_ref, wp_ref, dh_ref, hn_ref, dgp_ref, de_ref, dg_ref):
        @pl.when(pl.program_id(0) == 0)
        def _():
            dg_ref[...] = jnp.zeros_like(dg_ref)

        x = h_ref[...]
        r = _rms_stat(x)
        gg = g_ref[...]
        hn = x * r * gg
        gate = _sigmoid(_mm(hn, wg_ref[...]))
        e = _mm(p_ref[...], wp_ref[...])
        gr = gr_ref[...]
        dgp = gr * e * gate * (1.0 - gate)
        dx, dg = _rms_bwd(x, r, gg, _mm_nt(dgp, wg_ref[...]))
        dh_ref[...] = gr + dx
        dg_ref[...] += dg
        hn_ref[...] = hn.astype(MXU)
        dgp_ref[...] = dgp.astype(MXU)
        de_ref[...] = (gr * gate).astype(MXU)

    rows = pl.BlockSpec((tm, D_MODEL), lambda i: (i, 0))
    row = pl.BlockSpec((1, D_MODEL), lambda i: (0, 0))
    return pl.pallas_call(
        body, name="ple_bwd", grid=(T // tm,),
        in_specs=[rows, rows, pl.BlockSpec((tm, PLE_DIM), lambda i: (i, 0)), row,
                  pl.BlockSpec((D_MODEL, D_MODEL), lambda i: (0, 0)), pl.BlockSpec((PLE_DIM, D_MODEL), lambda i: (0, 0))],
        out_specs=[rows, rows, rows, rows, row],
        out_shape=[_sds((T, D_MODEL))] + [_sds((T, D_MODEL), MXU)] * 3 + [_sds((1, D_MODEL))],
        compiler_params=_cp("arbitrary"))(gres, h, p, g, wg, wp)


def _loss_head(h, target):
    T = h.shape[0]
    tm = min(T, 1024)

    def body(h_ref, t_ref, dh_ref, l_ref):
        @pl.when(pl.program_id(0) == 0)
        def _():
            l_ref[...] = jnp.zeros_like(l_ref)

        e = h_ref[...] - t_ref[...]
        dh_ref[...] = e * (1.0 / D_MODEL)
        l_ref[...] += jnp.zeros_like(l_ref) + 0.5 * jnp.sum(jnp.mean(e * e, axis=-1, keepdims=True))

    rows = pl.BlockSpec((tm, D_MODEL), lambda i: (i, 0))
    return pl.pallas_call(
        body, name="loss_head", grid=(T // tm,), in_specs=[rows, rows],
        out_specs=[rows, pl.BlockSpec((8, LANES), lambda i: (0, 0))], out_shape=[_sds((T, D_MODEL)), _sds((8, LANES))],
        compiler_params=_cp("arbitrary"))(h, target)


def _tn(a, b, m, n, *, bm, bn, a_off=0, b_off=0, n_major=False, split=None, dtype=F32, name="tn"):
    T = a.shape[0]
    tk = min(T, 1024)
    nk = T // tk
    assert m % bm == 0 and n % bn == 0 and (not n_major or bm == m) and (split is None or (bn == n and n % split == 0))

    def body(a_ref, b_ref, o_ref, acc):
        k = pl.program_id(2)

        @pl.when(k == 0)
        def _():
            acc[...] = jnp.zeros_like(acc)

        acc[...] += _mm_tn(a_ref[...], b_ref[...])

        @pl.when(k == nk - 1)
        def _():
            if split is None:
                o_ref[...] = acc[...].astype(dtype)
            else:
                for d in range(n // split):
                    o_ref[d] = acc[:, d * split:(d + 1) * split].astype(dtype)

    if split is not None:
        out_spec = pl.BlockSpec((n // split, bm, split), lambda i, j, k: (0, i, 0))
        out_shape = _sds((n // split, m, split), dtype)
    elif n_major:
        out_spec = pl.BlockSpec((None, bm, bn), lambda i, j, k: (j, 0, 0))
        out_shape = _sds((n // bn, m, bn), dtype)
    else:
        out_spec = pl.BlockSpec((bm, bn), lambda i, j, k: (i, j))
        out_shape = _sds((m, n), dtype)
    return pl.pallas_call(
        body, name=name, grid=(m // bm, n // bn, nk),
        in_specs=[pl.BlockSpec((tk, bm), lambda i, j, k: (k, a_off + i)), pl.BlockSpec((tk, bn), lambda i, j, k: (k, b_off + j))],
        out_specs=out_spec, out_shape=out_shape, scratch_shapes=[pltpu.VMEM((bm, bn), F32)],
        compiler_params=_cp("parallel", "parallel", "arbitrary"))(a, b)


def _sum_adamw(land, w, m, v):
    R, C = w.shape
    tr = R
    for cand in (512, 256, 128, 64, 32, 16, 8):
        if R % cand == 0 and cand * C * 4 <= 2 ** 20:
            tr = cand
            break

    def body(l_ref, w_ref, m_ref, v_ref, g_ref, d_ref, nm_ref, nv_ref):
        gg = l_ref[0].astype(F32)
        for s in range(1, N_DEV):
            gg = gg + l_ref[s].astype(F32)
        g_ref[...] = gg
        nm = ADAM_B1 * m_ref[...] + (1.0 - ADAM_B1) * gg
        nv = ADAM_B2 * v_ref[...] + (1.0 - ADAM_B2) * (gg * gg)
        m_hat = nm / (1.0 - ADAM_B1 ** ADAM_STEP)
        v_hat = nv / (1.0 - ADAM_B2 ** ADAM_STEP)
        d_ref[...] = -ADAM_LR * (m_hat / (jnp.sqrt(v_hat) + ADAM_EPS) + ADAM_WD * w_ref[...])
        nm_ref[...] = nm
        nv_ref[...] = nv

    blk = pl.BlockSpec((tr, C), lambda i: (i, 0))
    return pl.pallas_call(
        body, name="sum_adamw", grid=(R // tr,), in_specs=[pl.BlockSpec((N_DEV, tr, C), lambda i: (0, i, 0))] + [blk] * 3,
        out_specs=[blk] * 4, out_shape=[_sds((R, C))] * 4, compiler_params=_cp("parallel"))(land, w, m, v)


def _all_to_all(pairs, name):
    n = len(pairs)

    def body(*refs):
        srcs, lands = refs[:n], refs[2 * n:3 * n]
        send, recv, loc = refs[3 * n:]
        x, y, c = lax.axis_index("x"), lax.axis_index("y"), lax.axis_index("c")
        me = 4 * x + 2 * y + c
        own = [pltpu.make_async_copy(pairs[t][2](srcs[t], me, me), pairs[t][3](lands[t], me), loc.at[t]) for t in range(n)]
        for cp in own:
            cp.start()
        sends, recvs = [], []
        for k in range(1, N_DEV):
            px, py, pc = x ^ (k >> 2), y ^ ((k >> 1) & 1), c ^ (k & 1)
            peer = 4 * px + 2 * py + pc
            for t in range(n):
                src = pairs[t][2](srcs[t], me, peer)
                cp = pltpu.make_async_remote_copy(src_ref=src, dst_ref=pairs[t][3](lands[t], me), send_sem=send.at[t, k],
                                                  recv_sem=recv.at[t, k], device_id=(px, py, pc), device_id_type=MESH)
                cp.start()
                sends.append(cp)
                recvs.append(pltpu.make_async_remote_copy(src_ref=src, dst_ref=pairs[t][3](lands[t], peer), send_sem=send.at[t, k],
                                                          recv_sem=recv.at[t, k], device_id=(px, py, pc), device_id_type=MESH))
        for cp in recvs:
            cp.wait_recv()
        for cp in sends:
            cp.wait_send()
        for cp in own:
            cp.wait()

    anyspec = pl.BlockSpec(memory_space=pl.ANY)
    lands = [pr[1] for pr in pairs]
    return pl.pallas_call(
        body, name=name, in_specs=[anyspec] * (2 * n), out_specs=[anyspec] * n,
        out_shape=[_sds(a.shape, a.dtype) for a in lands], input_output_aliases={n + t: t for t in range(n)},
        scratch_shapes=[pltpu.SemaphoreType.DMA((n, N_DEV)), pltpu.SemaphoreType.DMA((n, N_DEV)), pltpu.SemaphoreType.DMA((n,))],
        compiler_params=pltpu.CompilerParams(has_side_effects=True))(*[pr[0] for pr in pairs], *lands)


def _gather_whole(src):
    land = lax.empty((N_DEV,) + src.shape, src.dtype)
    return _all_to_all([(src, land, lambda ref, me, peer: ref, lambda ref, sender: ref.at[sender])], "gather_small_grads")[0]


_HBM = pl.BlockSpec(memory_space=pltpu.HBM)
_SEM = pl.BlockSpec(memory_space=pltpu.SEMAPHORE)
_EFFECT = pltpu.SideEffectType.DATAFLOW_SIDE_EFFECTING


def _peers():
    x, y, c = lax.axis_index("x"), lax.axis_index("y"), lax.axis_index("c")
    out = []
    for k in range(1, N_DEV):
        px, py, pc = x ^ (k >> 2), y ^ ((k >> 1) & 1), c ^ (k & 1)
        out.append((k, (px, py, pc), 4 * px + 2 * py + pc))
    return 4 * x + 2 * y + c, out


def _split_start(srcs, lands, src_of, dst_of, after, name):
    n = len(srcs)

    def body(*refs):
        src, land = refs[:n], refs[n:2 * n]
        send, recv, token = refs[2 * n + 1], refs[2 * n + 2], refs[-1]
        me, peers = _peers()
        for k, dev, peer in peers:
            for t in range(n):
                pltpu.make_async_remote_copy(src_ref=src_of(src[t], peer), dst_ref=dst_of(land[t], me), send_sem=send.at[t * N_DEV + k],
                                             recv_sem=recv.at[t * N_DEV + k], device_id=dev, device_id_type=MESH).start()
        token[...] = jnp.zeros_like(token)

    bufs = list(srcs) + list(lands)
    outs = pl.pallas_call(
        body, name=name,
        out_shape=(pltpu.SemaphoreType.DMA((n * N_DEV,)), pltpu.SemaphoreType.DMA((n * N_DEV,)),
                   *[pltpu.HBM(a.shape, a.dtype) for a in bufs], _sds((8, LANES))),
        in_specs=[_HBM] * (2 * n) + [pl.BlockSpec(memory_space=pl.ANY)],
        out_specs=(_SEM, _SEM, *[_HBM] * (2 * n), pl.BlockSpec(memory_space=pltpu.VMEM)),
        input_output_aliases={i: 2 + i for i in range(2 * n)},
        compiler_params=pltpu.CompilerParams(has_side_effects=_EFFECT),
    )(*[pltpu.with_memory_space_constraint(a, pltpu.HBM) for a in bufs], after)
    return outs[0], outs[1], list(outs[2:2 + n]), list(outs[2 + n:2 + 2 * n]), outs[-1]


def _split_wait(send, recv, srcs, lands, src_of, dst_of, after, name):
    n = len(srcs)

    def body(*refs):
        src, land = refs[:n], refs[n:2 * n]
        send_r, recv_r = refs[2 * n], refs[2 * n + 1]
        _, peers = _peers()
        for k, dev, peer in peers:
            for t in range(n):
                cp = pltpu.make_async_remote_copy(src_ref=src_of(src[t], peer), dst_ref=dst_of(land[t], peer), send_sem=send_r.at[t * N_DEV + k],
                                                  recv_sem=recv_r.at[t * N_DEV + k], device_id=dev, device_id_type=MESH)
                cp.wait_send()
                cp.wait_recv()

    bufs = list(srcs) + list(lands)
    outs = pl.pallas_call(
        body, name=name, out_shape=tuple(pltpu.HBM(a.shape, a.dtype) for a in bufs),
        in_specs=[_HBM] * (2 * n) + [_SEM, _SEM, pl.BlockSpec(memory_space=pl.ANY)], out_specs=[_HBM] * (2 * n),
        input_output_aliases={i: i for i in range(2 * n)},
        compiler_params=pltpu.CompilerParams(has_side_effects=_EFFECT),
    )(*bufs, send, recv, after)
    return list(outs[n:])


SHARDED = ("w_in", "glu_w1", "glu_w2", "w_out", "w_ff1", "w_ff2", "w_ple_gate", "w_ple_proj")
SMALL = ("attn_norm_g", "gmlp_ln_g", "gmlp_ln_b", "gmlp_ws", "gmlp_bs", "q_norm_g", "k_norm_g", "sinks", "ssm_a_re", "ssm_a_im",
         "ssm_log_dt", "ssm_b_re", "ssm_b_im", "ssm_c_re", "ssm_c_im", "ssm_d", "mix_out_g", "mlp_norm_g", "ple_norm_g")
WEIGHTS = ("attn_norm_g", "w_in", "gmlp_ln_g", "gmlp_ln_b", "gmlp_ws", "gmlp_bs", "q_norm_g", "k_norm_g", "sinks", "ssm_a_re", "ssm_a_im",
           "ssm_log_dt", "ssm_b_re", "ssm_b_im", "ssm_c_re", "ssm_c_im", "ssm_d", "glu_w1", "glu_w2", "mix_out_g", "w_out", "mlp_norm_g",
           "w_ff1", "w_ff2", "ple_norm_g", "w_ple_gate", "w_ple_proj")
FLAT_COLS = 1024


def _pack(arrs, dtype):
    flat = jnp.concatenate([a.astype(dtype).reshape(-1) for a in arrs])
    pad = (-flat.shape[0]) % (8 * FLAT_COLS)
    if pad:
        flat = jnp.concatenate([flat, jnp.zeros((pad,), dtype)])
    return flat.reshape(-1, FLAT_COLS)


def _unpack(flat, shapes, lead=()):
    flat = flat.reshape(lead + (-1,))
    out, off = [], 0
    for s in shapes:
        n = math.prod(s)
        out.append(flat[..., off:off + n].reshape(lead + tuple(s)))
        off += n
    return out


def _col_major(w):
    rows, cols = w.shape
    return w.reshape(rows, N_DEV, cols // N_DEV).transpose(1, 0, 2)


def _from_col_major(s):
    n, rows, cs = s.shape
    return s.transpose(1, 0, 2).reshape(rows, n * cs)


def _layer_fwd(h, p_l, cs, sn, W, S, sp):
    z = _inproj_fwd(h, S["attn_norm_g"], W["w_in"])
    ya = _gmlp_fwd(z, S["lng"], S["lnb"], S["gmlp_ws"], S["bsx"])
    yb = _attn_fwd(z, cs, sn, S["qg"], S["kg"], S["sinks"])
    yc, xr, xi = _ssm_fwd(z, sp["bbd"], sp["lr"], sp["li"], sp["cbd"], S["ssm_d"], W["glu_w1"], W["glu_w2"])
    h1 = _merge_fwd(h, ya, yb, yc, S["mix_out_g"], W["w_out"])
    h2 = _mlp_fwd(h1, S["mlp_norm_g"], W["w_ff1"], W["w_ff2"])
    h3 = _ple_fwd(h2, p_l, S["ple_norm_g"], W["w_ple_gate"], W["w_ple_proj"])
    return h3, dict(h=h, z=z, ya=ya, yb=yb, yc=yc, xr=xr, xi=xi, h1=h1, h2=h2)


def _layer_bwd(g3, p_l, cs, sn, W, S, sp, A, raw):
    G = {}
    g2, hn3, dgp, de, G["ple_norm_g"] = _ple_bwd(g3, A["h2"], p_l, S["ple_norm_g"], W["w_ple_gate"], W["w_ple_proj"])
    G["w_ple_gate"] = _tn(hn3, dgp, D_MODEL, D_MODEL, bm=1024, bn=1024, dtype=WIRE, name="tn_gate").reshape(N_DEV, -1, D_MODEL)
    G["w_ple_proj"] = _tn(p_l, de, PLE_DIM, D_MODEL, bm=PLE_DIM, bn=D_MODEL, split=D_MODEL // N_DEV, dtype=WIRE, name="tn_proj")
    g1, hn, da, r, G["mlp_norm_g"] = _mlp_bwd(g2, A["h1"], S["mlp_norm_g"], W["w_ff1"], W["w_ff2"])
    G["w_ff1"] = _tn(hn, da, D_MODEL, D_FF, bm=D_MODEL, bn=FF_BLOCK, n_major=True, dtype=WIRE, name="tn_ff1")
    G["w_ff2"] = _tn(r, g2, D_FF, D_MODEL, bm=1024, bn=1024, dtype=WIRE, name="tn_ff2").reshape(N_DEV, -1, D_MODEL)
    dya, dyb, dyc, yn, G["mix_out_g"] = _merge_bwd(g1, A["ya"], A["yb"], A["yc"], S["mix_out_g"], W["w_out"])
    G["w_out"] = _tn(yn, g1, D_MODEL, D_MODEL, bm=1024, bn=1024, dtype=WIRE, name="tn_out").reshape(N_DEV, -1, D_MODEL)
    dzc, y2, da1, da2, dy, arai, dlr, dli, dd = _ssm_bwd(dyc, A["z"], A["xr"], A["xi"], sp["bbd"], sp["lr"], sp["li"], sp["cbd"],
                                                        S["ssm_d"], W["glu_w1"], W["glu_w2"])
    G["glu_w1"] = _tn(y2, da1, C_WIDTH, C_WIDTH, bm=256, bn=256, dtype=WIRE, name="tn_glu1").reshape(N_DEV, -1, C_WIDTH)
    G["glu_w2"] = _tn(y2, da2, C_WIDTH, C_WIDTH, bm=256, bn=256, dtype=WIRE, name="tn_glu2").reshape(N_DEV, -1, C_WIDTH)
    dbbd = _tn(A["z"], arai, C_WIDTH, 2 * N_STATE, bm=256, bn=512, a_off=5, name="tn_bbd")
    dcr = _tn(dy, A["xr"], C_WIDTH, N_STATE, bm=256, bn=512, name="tn_cre")
    dci = _tn(dy, A["xi"], C_WIDTH, N_STATE, bm=256, bn=512, name="tn_cim")
    dare, daim, dldt, dbtr, dbti, dcre, dcim = _ssm_param_bwd(
        raw["are"], raw["aim"], raw["ldt"], raw["are_x"], raw["aim_x"], raw["ldt_x"], raw["btr"], raw["bti"],
        dlr.reshape(C_GROUPS, C_STATE), dli.reshape(C_GROUPS, C_STATE), dbbd, dcr, dci)
    G["ssm_a_re"], G["ssm_a_im"], G["ssm_log_dt"] = dare, daim, dldt[:, 0]
    G["ssm_b_re"] = dbtr.reshape(C_GROUPS, C_GROUP, C_STATE).transpose(0, 2, 1)
    G["ssm_b_im"] = dbti.reshape(C_GROUPS, C_GROUP, C_STATE).transpose(0, 2, 1)
    G["ssm_c_re"] = dcre.reshape(C_GROUPS, C_GROUP, C_STATE)
    G["ssm_c_im"] = dcim.reshape(C_GROUPS, C_GROUP, C_STATE)
    G["ssm_d"] = dd.reshape(C_GROUPS, C_GROUP)
    dzq, dzk, dzv, dqg, dkg, dsk = _attn_bwd(A["z"], cs, sn, S["qg"], S["kg"], S["sinks"], A["yb"], dyb)
    G["q_norm_g"] = dqg[0, :HEAD_DIM] + dqg[0, HEAD_DIM:]
    G["k_norm_g"] = dkg[0, :HEAD_DIM] + dkg[0, HEAD_DIM:]
    G["sinks"] = dsk[:, 0]
    dza, dws, dbs, dlng, dlnb = _gmlp_bwd(A["z"], dya, S["lng"], S["lnb"], S["gmlp_ws"], S["bsx"])
    G["gmlp_ws"] = dws
    G["gmlp_bs"] = dbs[:, :, 0]
    G["gmlp_ln_g"] = dlng.reshape(A_HEADS, 2, HEAD_DIM)[:, 1]
    G["gmlp_ln_b"] = dlnb.reshape(A_HEADS, 2, HEAD_DIM)[:, 1]
    g0, xn, dz, G["attn_norm_g"] = _inproj_bwd(g1, A["h"], S["attn_norm_g"], W["w_in"], dza, dzq, dzk, dzv, dzc)
    G["w_in"] = _tn(xn, dz, D_MODEL, IN_COLS, bm=D_MODEL, bn=IN_COLS, split=IN_COLS // N_DEV, dtype=WIRE, name="tn_in")
    return g0, G


def _small_layouts(P, l):
    def row(a):
        return a.reshape(1, -1)

    zeros = jnp.zeros((A_HEADS, HEAD_DIM), F32)
    S = dict(
        attn_norm_g=row(P["attn_norm_g"][l]), mix_out_g=row(P["mix_out_g"][l]), mlp_norm_g=row(P["mlp_norm_g"][l]),
        ple_norm_g=row(P["ple_norm_g"][l]),
        lng=jnp.stack([zeros, P["gmlp_ln_g"][l]], axis=1).reshape(1, 512),
        lnb=jnp.stack([zeros, P["gmlp_ln_b"][l]], axis=1).reshape(1, 512),
        gmlp_ws=P["gmlp_ws"][l],
        bsx=jnp.broadcast_to(P["gmlp_bs"][l][:, :, None], (A_HEADS, CHUNK, CHUNK)),
        qg=jnp.tile(P["q_norm_g"][l], 2).reshape(1, LANES), kg=jnp.tile(P["k_norm_g"][l], 2).reshape(1, LANES),
        sinks=jnp.broadcast_to(P["sinks"][l][:, None], (8, LANES)),
        ssm_d=row(P["ssm_d"][l]),
    )
    are, aim = P["ssm_a_re"][l], P["ssm_a_im"][l]
    ldt = jnp.broadcast_to(P["ssm_log_dt"][l][:, None], (C_GROUPS, C_STATE))
    raw = dict(
        are=are, aim=aim, ldt=ldt,
        are_x=jnp.repeat(are, C_GROUP, axis=0), aim_x=jnp.repeat(aim, C_GROUP, axis=0), ldt_x=jnp.repeat(ldt, C_GROUP, axis=0),
        btr=P["ssm_b_re"][l].transpose(0, 2, 1).reshape(C_WIDTH, C_STATE), bti=P["ssm_b_im"][l].transpose(0, 2, 1).reshape(C_WIDTH, C_STATE),
        cre=P["ssm_c_re"][l].reshape(C_WIDTH, C_STATE), cim=P["ssm_c_im"][l].reshape(C_WIDTH, C_STATE),
    )
    return S, raw


def _ssm_prep_layer(raw):
    bbd, cbd, lr, li = _ssm_prep(raw["are"], raw["aim"], raw["ldt"], raw["are_x"], raw["aim_x"], raw["ldt_x"],
                                 raw["btr"], raw["bti"], raw["cre"], raw["cim"])
    return dict(bbd=bbd, cbd=cbd, lr=lr.reshape(1, N_STATE), li=li.reshape(1, N_STATE))


def _behind(row, token):
    return row if token is None else row + token[0:1, 0:1]


def _local_step(x, p, positions, target, P, weights_of, after_bwd):
    inv = 1.0 / (ROPE_THETA ** (jnp.arange(0, HEAD_DIM, 2, dtype=F32) / HEAD_DIM))
    cs, sn = _rope_tables(positions.reshape(-1, 1), jnp.tile(inv, 4).reshape(1, LANES))
    h = x
    acts, smalls, weights = [], [], []
    for l in range(DEPTH):
        W, token = weights_of(l, h)
        S, raw = _small_layouts(P, l)
        sp = _ssm_prep_layer(raw)
        h, A = _layer_fwd(h, p[l], cs, sn, W, {**S, "attn_norm_g": _behind(S["attn_norm_g"], token)}, sp)
        acts.append(A)
        smalls.append((S, raw, sp))
        weights.append(W)
    g, lsum = _loss_head(h, target)
    grads = [None] * DEPTH
    token = None
    for l in reversed(range(DEPTH)):
        S, raw, sp = smalls[l]
        g, grads[l] = _layer_bwd(g, p[l], cs, sn, weights[l], {**S, "ple_norm_g": _behind(S["ple_norm_g"], token)}, sp, acts[l], raw)
        token = after_bwd(l, g, grads[l])
    return lsum[0, 0], g, grads


def _layer_weights(g):
    return dict(
        w_in=_from_col_major(g["w_in"]), glu_w1=g["glu_w1"].reshape(C_WIDTH, C_WIDTH), glu_w2=g["glu_w2"].reshape(C_WIDTH, C_WIDTH),
        w_out=g["w_out"].reshape(D_MODEL, D_MODEL), w_ff1=g["w_ff1"], w_ff2=g["w_ff2"].reshape(D_FF, D_MODEL),
        w_ple_gate=g["w_ple_gate"].reshape(D_MODEL, D_MODEL), w_ple_proj=_from_col_major(g["w_ple_proj"]))


def kernel(x, p, positions, attn_norm_g, w_in, gmlp_ln_g, gmlp_ln_b, gmlp_ws, gmlp_bs, q_norm_g, k_norm_g, sinks, ssm_a_re, ssm_a_im, ssm_log_dt, ssm_b_re, ssm_b_im, ssm_c_re, ssm_c_im, ssm_d, glu_w1, glu_w2, mix_out_g, w_out, mlp_norm_g, w_ff1, w_ff2, ple_norm_g, w_ple_gate, w_ple_proj, loss_target, m_attn_norm_g, m_w_in, m_gmlp_ln_g, m_gmlp_ln_b, m_gmlp_ws, m_gmlp_bs, m_q_norm_g, m_k_norm_g, m_sinks, m_ssm_a_re, m_ssm_a_im, m_ssm_log_dt, m_ssm_b_re, m_ssm_b_im, m_ssm_c_re, m_ssm_c_im, m_ssm_d, m_glu_w1, m_glu_w2, m_mix_out_g, m_w_out, m_mlp_norm_g, m_w_ff1, m_w_ff2, m_ple_norm_g, m_w_ple_gate, m_w_ple_proj, v_attn_norm_g, v_w_in, v_gmlp_ln_g, v_gmlp_ln_b, v_gmlp_ws, v_gmlp_bs, v_q_norm_g, v_k_norm_g, v_sinks, v_ssm_a_re, v_ssm_a_im, v_ssm_log_dt, v_ssm_b_re, v_ssm_b_im, v_ssm_c_re, v_ssm_c_im, v_ssm_d, v_glu_w1, v_glu_w2, v_mix_out_g, v_w_out, v_mlp_norm_g, v_w_ff1, v_w_ff2, v_ple_norm_g, v_w_ple_gate, v_w_ple_proj):
    env = dict(locals())
    P = {n: env[n] for n in WEIGHTS}
    M = {n: env["m_" + n] for n in WEIGHTS}
    V = {n: env["v_" + n] for n in WEIGHTS}
    return _step(x, p, positions, loss_target, P, M, V)


def _step(x, p, positions, loss_target, P, M, V):
    small_shapes = [P[n].shape for n in SMALL]
    me = 4 * lax.axis_index("x") + 2 * lax.axis_index("y") + lax.axis_index("c")
    nothing = jnp.zeros((8, LANES), F32)

    def put(land, own, lead):
        return lax.dynamic_update_slice(land, own.reshape((1,) * len(lead) + own.shape), tuple(lead) + (0,) * own.ndim)

    def to_all(ref, peer):
        return ref

    def from_owner(ref, sender):
        return ref.at[sender]

    def gather_start(l, after):
        shards = [P[n][l].astype(WIRE) for n in SHARDED]
        lands = [lax.empty((N_DEV,) + s.shape, WIRE) for s in shards]
        return _split_start(shards, lands, to_all, from_owner, after, f"gather_start_{l}")

    flying = {0: gather_start(0, nothing)}

    def weights_of(l, h):
        send, recv, shards, lands, _ = flying.pop(l)
        lands = _split_wait(send, recv, shards, lands, to_all, from_owner, h, f"gather_wait_{l}")
        token = None
        if l + 1 < DEPTH:
            flying[l + 1] = gather_start(l + 1, lands[0])
            token = flying[l + 1][4]
        lands = [put(ld, P[n][l].astype(WIRE), (me,)) for n, ld in zip(SHARDED, lands)]
        return _layer_weights(dict(zip(SHARDED, lands))), token

    grad_lands = [lax.empty((N_DEV,) + P[n].shape, WIRE) for n in SHARDED]
    sent = {}

    def to_owner(ref, peer):
        return ref.at[peer]

    def land_own(l, parts):
        return [put(ld, lax.dynamic_index_in_dim(part, me, 0, keepdims=False), (me, l)) for part, ld in zip(parts, grad_lands)]

    def scatter_wait(l, after):
        send, recv, parts, lands = sent.pop(l)
        grad_lands[:] = _split_wait(send, recv, parts, lands, to_owner, lambda ref, sender: ref.at[sender, l], after,
                                    f"scatter_wait_{l}")
        grad_lands[:] = land_own(l, parts)

    def after_bwd(l, g, G):
        if l + 1 in sent:
            scatter_wait(l + 1, g)
        send, recv, parts, lands, token = _split_start([G[n] for n in SHARDED], grad_lands, to_owner,
                                                       lambda ref, sender: ref.at[sender, l], nothing, f"scatter_start_{l}")
        sent[l] = (send, recv, parts, lands)
        return token

    lsum, gx, grads = _local_step(x[0], p[:, 0], positions[0], loss_target[0], P, weights_of, after_bwd)
    sflat = _pack([jnp.stack([grads[l][n] for l in range(DEPTH)]) for n in SMALL], F32)
    land_s = _gather_whole(sflat)
    scatter_wait(0, land_s)
    lands = grad_lands
    G, delta, new_m, new_v = {}, {}, {}, {}
    for n, land in zip(SHARDED, lands):
        shp = P[n].shape
        res = _sum_adamw(land.reshape(N_DEV, -1, shp[-1]), *(a.reshape(-1, shp[-1]) for a in (P[n], M[n], V[n])))
        G[n], delta[n], new_m[n], new_v[n] = (a.reshape(shp) for a in res)
    res = _sum_adamw(land_s, _pack([P[n] for n in SMALL], F32), _pack([M[n] for n in SMALL], F32), _pack([V[n] for n in SMALL], F32))
    for dst, flat in zip((G, delta, new_m, new_v), res):
        dst.update(zip(SMALL, _unpack(flat, small_shapes)))
    loss = lax.psum(lsum, ("x", "y", "c"))
    return (loss, gx[None], *[G[n] for n in WEIGHTS], *[delta[n] for n in WEIGHTS], *[new_m[n] for n in WEIGHTS], *[new_v[n] for n in WEIGHTS])
```

```python
import functools
import math

import jax
import jax.numpy as jnp
from jax import lax
from jax.experimental import pallas as pl
from jax.experimental.pallas import tpu as pltpu

F32 = jnp.float32
MXU = jnp.bfloat16
WIRE = jnp.bfloat16

D_MODEL = 1024
DEPTH = 4
HEAD_DIM = 64
A_WIDTH = 256
A_HEADS = 4
CHUNK = 128
B_WIDTH = 512
WINDOW = 128
C_WIDTH = 256
C_GROUP = 16
C_GROUPS = 16
C_STATE = 64
N_STATE = C_GROUPS * C_STATE
IN_COLS = 1536
D_FF = 4096
PLE_DIM = 256
EPS = 1e-6
ROPE_THETA = 10000.0
SCALE = HEAD_DIM ** -0.5
NEG = -1e30
N_DEV = 8

ADAM_LR = 0.001
ADAM_B1 = 0.9
ADAM_B2 = 0.999
ADAM_EPS = 1e-08
ADAM_WD = 0.01
ADAM_STEP = 10

V7X_VMEM_BYTES = 64 * 2 ** 20
VMEM_LIMIT = V7X_VMEM_BYTES - 8 * 2 ** 20
LANES = 128

MESH = pl.DeviceIdType.MESH


def _cp(*sem):
    return pltpu.CompilerParams(dimension_semantics=sem, vmem_limit_bytes=VMEM_LIMIT)


def _sds(shape, dtype=F32):
    return jax.ShapeDtypeStruct(shape, dtype)


def _mm(a, b):
    return jnp.dot(a.astype(MXU), b.astype(MXU), preferred_element_type=F32)


def _mm_nt(a, b):
    return lax.dot_general(a.astype(MXU), b.astype(MXU), (((1,), (1,)), ((), ())), preferred_element_type=F32)


def _mm_tn(a, b):
    return lax.dot_general(a.astype(MXU), b.astype(MXU), (((0,), (0,)), ((), ())), preferred_element_type=F32)


def _lane(shape):
    return lax.broadcasted_iota(jnp.int32, shape, len(shape) - 1)


def _row(shape):
    return lax.broadcasted_iota(jnp.int32, shape, 0)


_GELU_C = math.sqrt(2.0 / math.pi)


def _gelu(x):
    return 0.5 * x * (1.0 + jnp.tanh(_GELU_C * (x + 0.044715 * (x * x * x))))


def _gelu_grad(x):
    t = jnp.tanh(_GELU_C * (x + 0.044715 * (x * x * x)))
    return 0.5 * (1.0 + t) + 0.5 * x * (1.0 - t * t) * (_GELU_C * (1.0 + 3.0 * 0.044715 * (x * x)))


def _sigmoid(x):
    return 1.0 / (1.0 + jnp.exp(-x))


def _rms_stat(x):
    return lax.rsqrt(jnp.mean(x * x, axis=-1, keepdims=True) + EPS)


def _rms_bwd(x, r, g, dy):
    xh = x * r
    dxh = dy * g
    dx = r * (dxh - xh * jnp.mean(dxh * xh, axis=-1, keepdims=True))
    return dx, jnp.sum(dy * xh, axis=0, keepdims=True)


def _tril(w):
    return jnp.where(_row(w.shape) >= _lane(w.shape), w, 0.0)


def _swap64(x):
    return pltpu.roll(x, HEAD_DIM, 1)


def _group_sum64(x, lo):
    s_lo = jnp.sum(jnp.where(lo, x, 0.0), axis=-1, keepdims=True)
    s_hi = jnp.sum(jnp.where(lo, 0.0, x), axis=-1, keepdims=True)
    return jnp.where(lo, s_lo, s_hi)


def _partner(x):
    n = x.shape[-1]
    first = (_lane(x.shape) % HEAD_DIM) < HEAD_DIM // 2
    return jnp.where(first, pltpu.roll(x, n - HEAD_DIM // 2, 1), pltpu.roll(x, HEAD_DIM // 2, 1))


def _rope(y, cs, sn):
    return y * cs + _partner(y) * sn


def _rope_bwd(d, cs, sn):
    return d * cs + _partner(d * sn)


def _qk_norm_rope(x, g, cs, sn):
    lo = _lane(x.shape) < HEAD_DIM
    r = lax.rsqrt(_group_sum64(x * x, lo) * (1.0 / HEAD_DIM) + EPS)
    xh = x * r
    return _rope(xh * g, cs, sn), xh, r


def _qk_norm_rope_bwd(xh, r, g, cs, sn, d):
    lo = _lane(xh.shape) < HEAD_DIM
    dy = _rope_bwd(d, cs, sn)
    dxh = dy * g
    m = _group_sum64(dxh * xh, lo) * (1.0 / HEAD_DIM)
    return r * (dxh - xh * m), jnp.sum(dy * xh, axis=0, keepdims=True)


def _gmlp_head(blk, g, b):
    hi = _lane(blk.shape) >= HEAD_DIM
    mu = jnp.sum(jnp.where(hi, blk, 0.0), axis=-1, keepdims=True) * (1.0 / HEAD_DIM)
    xc = jnp.where(hi, blk - mu, 0.0)
    rstd = lax.rsqrt(jnp.sum(xc * xc, axis=-1, keepdims=True) * (1.0 / HEAD_DIM) + EPS)
    vhat = xc * rstd
    return vhat * g + b, vhat, rstd


def _rope_tables(pos_col, inv_row):
    T = pos_col.shape[0]
    tm = min(T, 1024)

    def body(p_ref, inv_ref, cs_ref, sn_ref):
        ang = p_ref[...].astype(F32) * inv_ref[...]
        s = jnp.sin(ang)
        cs_ref[...] = jnp.cos(ang)
        sn_ref[...] = jnp.where((_lane(ang.shape) % HEAD_DIM) < HEAD_DIM // 2, -s, s)

    blk = pl.BlockSpec((tm, LANES), lambda i: (i, 0))
    return pl.pallas_call(
        body, name="rope_tables", grid=(T // tm,),
        in_specs=[pl.BlockSpec((tm, 1), lambda i: (i, 0)), pl.BlockSpec((1, LANES), lambda i: (0, 0))],
        out_specs=[blk, blk], out_shape=[_sds((T, LANES))] * 2, compiler_params=_cp("parallel"))(pos_col, inv_row)


def _inproj_fwd(h, g, w):
    T = h.shape[0]
    tm = min(T, 512)

    def body(h_ref, g_ref, w_ref, z_ref):
        x = h_ref[...]
        z_ref[...] = _mm(x * _rms_stat(x) * g_ref[...], w_ref[...])

    return pl.pallas_call(
        body, name="inproj_fwd", grid=(T // tm,),
        in_specs=[pl.BlockSpec((tm, D_MODEL), lambda i: (i, 0)), pl.BlockSpec((1, D_MODEL), lambda i: (0, 0)),
                  pl.BlockSpec((D_MODEL, IN_COLS), lambda i: (0, 0))],
        out_specs=pl.BlockSpec((tm, IN_COLS), lambda i: (i, 0)), out_shape=_sds((T, IN_COLS)),
        compiler_params=_cp("parallel"))(h, g, w)


def _inproj_bwd(gres, h, g, w, dza, dzq, dzk, dzv, dzc):
    T = h.shape[0]
    tm = min(T, 512)

    def body(gr_ref, h_ref, g_ref, w_ref, a_ref, q_ref, k_ref, v_ref, c_ref, dh_ref, xn_ref, dz_ref, dg_ref):
        @pl.when(pl.program_id(0) == 0)
        def _():
            dg_ref[...] = jnp.zeros_like(dg_ref)

        x = h_ref[...]
        r = _rms_stat(x)
        gg = g_ref[...]
        dz = jnp.concatenate([a_ref[...], q_ref[...], k_ref[...], v_ref[...], c_ref[...]], axis=1)
        dxn = _mm_nt(dz, w_ref[...])
        dx, dg = _rms_bwd(x, r, gg, dxn)
        dh_ref[...] = gr_ref[...] + dx
        dg_ref[...] += dg
        xn_ref[...] = (x * r * gg).astype(MXU)
        dz_ref[...] = dz.astype(MXU)

    def rows(w_):
        return pl.BlockSpec((tm, w_), lambda i: (i, 0))

    row = pl.BlockSpec((1, D_MODEL), lambda i: (0, 0))
    return pl.pallas_call(
        body, name="inproj_bwd", grid=(T // tm,),
        in_specs=[rows(D_MODEL), rows(D_MODEL), row, pl.BlockSpec((D_MODEL, IN_COLS), lambda i: (0, 0)),
                  rows(512), rows(512), rows(128), rows(128), rows(256)],
        out_specs=[rows(D_MODEL), rows(D_MODEL), rows(IN_COLS), row],
        out_shape=[_sds((T, D_MODEL)), _sds((T, D_MODEL), MXU), _sds((T, IN_COLS), MXU), _sds((1, D_MODEL))],
        compiler_params=_cp("arbitrary"))(gres, h, g, w, dza, dzq, dzk, dzv, dzc)


def _gmlp_fwd(z, lng, lnb, ws, bsx):
    T = z.shape[0]
    tm = min(T, 512)
    nc = tm // CHUNK

    def body(z_ref, g_ref, b_ref, w_ref, bs_ref, ya_ref):
        zg = _gelu(z_ref[...])
        lo = _lane((tm, LANES)) < HEAD_DIM
        prods = []
        for hd in range(A_HEADS):
            sl = slice(hd * LANES, (hd + 1) * LANES)
            blk = zg[:, sl]
            vn, _, _ = _gmlp_head(blk, g_ref[:, sl], b_ref[:, sl])
            wm = _tril(w_ref[hd])
            sv = jnp.concatenate([_mm(wm, vn[c * CHUNK:(c + 1) * CHUNK]) + bs_ref[hd] for c in range(nc)], axis=0)
            prods.append(blk * _swap64(sv))
        ya_ref[:, 0:LANES] = jnp.where(lo, prods[0], _swap64(prods[1]))
        ya_ref[:, LANES:2 * LANES] = jnp.where(lo, prods[2], _swap64(prods[3]))

    row = pl.BlockSpec((1, 512), lambda i: (0, 0))
    mat = pl.BlockSpec((A_HEADS, CHUNK, CHUNK), lambda i: (0, 0, 0))
    return pl.pallas_call(
        body, name="gmlp_fwd", grid=(T // tm,),
        in_specs=[pl.BlockSpec((tm, 512), lambda i: (i, 0)), row, row, mat, mat],
        out_specs=pl.BlockSpec((tm, A_WIDTH), lambda i: (i, 0)), out_shape=_sds((T, A_WIDTH)),
        compiler_params=_cp("parallel"))(z, lng, lnb, ws, bsx)


def _gmlp_bwd(z, dya, lng, lnb, ws, bsx):
    T = z.shape[0]
    tm = min(T, 512)
    nc = tm // CHUNK

    def body(z_ref, dya_ref, g_ref, b_ref, w_ref, bs_ref, dza_ref, dw_ref, dbs_ref, dg_ref, db_ref):
        @pl.when(pl.program_id(0) == 0)
        def _():
            dw_ref[...] = jnp.zeros_like(dw_ref)
            dbs_ref[...] = jnp.zeros_like(dbs_ref)
            dg_ref[...] = jnp.zeros_like(dg_ref)
            db_ref[...] = jnp.zeros_like(db_ref)

        za = z_ref[...]
        zg = _gelu(za)
        gp = _gelu_grad(za)
        lo = _lane((tm, LANES)) < HEAD_DIM
        for hd in range(A_HEADS):
            sl = slice(hd * LANES, (hd + 1) * LANES)
            blk = zg[:, sl]
            g = g_ref[:, sl]
            vn, vhat, rstd = _gmlp_head(blk, g, b_ref[:, sl])
            wm = _tril(w_ref[hd])
            pair = dya_ref[:, (hd // 2) * LANES:(hd // 2 + 1) * LANES]
            dy = jnp.where(lo, pair if hd % 2 == 0 else _swap64(pair), 0.0)
            dsv = _swap64(dy * blk)
            svs, dvns = [], []
            dw = jnp.zeros((CHUNK, CHUNK), F32)
            dbs = jnp.zeros((CHUNK, 1), F32)
            for c in range(nc):
                cs = slice(c * CHUNK, (c + 1) * CHUNK)
                svs.append(_mm(wm, vn[cs]) + bs_ref[hd])
                dw = dw + _mm_nt(dsv[cs], vn[cs])
                dbs = dbs + jnp.sum(dsv[cs], axis=-1, keepdims=True)
                dvns.append(_mm_tn(wm, dsv[cs]))
            sv = jnp.concatenate(svs, axis=0)
            dvn = jnp.concatenate(dvns, axis=0)
            dw_ref[hd] += _tril(dw)
            dbs_ref[hd] += jnp.broadcast_to(dbs, (CHUNK, CHUNK))
            dg_ref[:, sl] += jnp.sum(dvn * vhat, axis=0, keepdims=True)
            db_ref[:, sl] += jnp.sum(dvn, axis=0, keepdims=True)
            du = dy * _swap64(sv)
            dvh = dvn * g
            m1 = jnp.sum(dvh, axis=-1, keepdims=True) * (1.0 / HEAD_DIM)
            m2 = jnp.sum(dvh * vhat, axis=-1, keepdims=True) * (1.0 / HEAD_DIM)
            dv = jnp.where(lo, 0.0, rstd * (dvh - m1 - vhat * m2))
            dza_ref[:, sl] = (du + dv) * gp[:, sl]

    row = pl.BlockSpec((1, 512), lambda i: (0, 0))
    mat = pl.BlockSpec((A_HEADS, CHUNK, CHUNK), lambda i: (0, 0, 0))
    return pl.pallas_call(
        body, name="gmlp_bwd", grid=(T // tm,),
        in_specs=[pl.BlockSpec((tm, 512), lambda i: (i, 0)), pl.BlockSpec((tm, A_WIDTH), lambda i: (i, 0)), row, row, mat, mat],
        out_specs=[pl.BlockSpec((tm, 512), lambda i: (i, 0)), mat, mat, row, row],
        out_shape=[_sds((T, 512)), _sds((A_HEADS, CHUNK, CHUNK)), _sds((A_HEADS, CHUNK, CHUNK)), _sds((1, 512)), _sds((1, 512))],
        compiler_params=_cp("arbitrary"))(z, dya, lng, lnb, ws, bsx)


def _attn_specs(T, tq, tile_of):
    nb = tq // WINDOW

    def prev(i):
        return jnp.maximum(tile_of(i) * nb - 1, 0)

    row = pl.BlockSpec((1, LANES), lambda i: (0, 0))
    return [
        pl.BlockSpec((tq, B_WIDTH), lambda i: (tile_of(i), 1)),
        pl.BlockSpec((tq, LANES), lambda i: (tile_of(i), 8)),
        pl.BlockSpec((tq, LANES), lambda i: (tile_of(i), 9)),
        pl.BlockSpec((WINDOW, LANES), lambda i: (prev(i), 8)),
        pl.BlockSpec((WINDOW, LANES), lambda i: (prev(i), 9)),
        pl.BlockSpec((tq, LANES), lambda i: (tile_of(i), 0)),
        pl.BlockSpec((tq, LANES), lambda i: (tile_of(i), 0)),
        pl.BlockSpec((WINDOW, LANES), lambda i: (prev(i), 0)),
        pl.BlockSpec((WINDOW, LANES), lambda i: (prev(i), 0)),
        row, row,
        pl.BlockSpec((8, LANES), lambda i: (0, 0)),
    ]


def _attn_bias(first):
    qi = lax.broadcasted_iota(jnp.int32, (WINDOW, 2 * WINDOW), 0)
    kj = lax.broadcasted_iota(jnp.int32, (WINDOW, 2 * WINDOW), 1)
    diff = qi + WINDOW - kj
    ok = (diff >= 0) & (diff < WINDOW) & ((kj >= WINDOW) | jnp.logical_not(first))
    return jnp.where(ok, 0.0, NEG)


def _dup_heads(x, lo):
    sw = _swap64(x)
    return jnp.where(lo, x, sw), jnp.where(lo, sw, x)


HEADS_PER_KV = 4


def _stack_heads(x0, x1, lo):
    return jnp.concatenate([jnp.where(lo, x0, 0.0), jnp.where(lo, 0.0, x0), jnp.where(lo, x1, 0.0), jnp.where(lo, 0.0, x1)], axis=0)


def _unstack_heads(x4, lo):
    return (jnp.where(lo, x4[0:WINDOW], x4[WINDOW:2 * WINDOW]), jnp.where(lo, x4[2 * WINDOW:3 * WINDOW], x4[3 * WINDOW:4 * WINDOW]))


def _sink_column(sk_ref, g):
    return jnp.concatenate([jnp.broadcast_to(sk_ref[a:a + 1, 0:1], (WINDOW, 1)) for a in range(HEADS_PER_KV * g, HEADS_PER_KV * (g + 1))], axis=0)


def _attn_probs(q4, kw, bias, sink):
    s = _mm_nt(q4, kw)
    s = (s.reshape(HEADS_PER_KV, WINDOW, 2 * WINDOW) + bias[None]).reshape(HEADS_PER_KV * WINDOW, 2 * WINDOW)
    m = jnp.maximum(jnp.max(s, axis=-1, keepdims=True), sink)
    p = jnp.exp(s - m)
    es = jnp.exp(sink - m)
    inv = 1.0 / (jnp.sum(p, axis=-1, keepdims=True) + es)
    return p * inv, es * inv


def _attn_fwd(z, cs, sn, qg, kg, sinks):
    T = z.shape[0]
    tq = min(T, 512)
    nb = tq // WINDOW

    def body(q_ref, k_ref, v_ref, kp_ref, vp_ref, cs_ref, sn_ref, csp_ref, snp_ref, qg_ref, kg_ref, sk_ref, o_ref):
        i = pl.program_id(0)
        csq, snq = cs_ref[...], sn_ref[...]
        cs_all = jnp.concatenate([csp_ref[...], csq], axis=0)
        sn_all = jnp.concatenate([snp_ref[...], snq], axis=0)
        k_all = jnp.concatenate([kp_ref[...], k_ref[...]], axis=0)
        v_all = jnp.concatenate([vp_ref[...], v_ref[...]], axis=0)
        kr, _, _ = _qk_norm_rope(k_all, kg_ref[...], cs_all, sn_all)
        lo_all = _lane(k_all.shape) < HEAD_DIM
        kd = _dup_heads(kr, lo_all)
        vd = _dup_heads(v_all, lo_all)
        lo = _lane((WINDOW, LANES)) < HEAD_DIM
        qrs = [_qk_norm_rope(q_ref[:, pr * LANES:(pr + 1) * LANES], qg_ref[...], csq, snq)[0] * SCALE for pr in range(4)]
        biases = [_attn_bias(i * nb + b == 0) for b in range(nb)]
        for g in range(2):
            sink = _sink_column(sk_ref, g)
            for b in range(nb):
                bs = slice(b * WINDOW, (b + 1) * WINDOW)
                ws = slice(b * WINDOW, (b + 2) * WINDOW)
                pn, _ = _attn_probs(_stack_heads(qrs[2 * g][bs], qrs[2 * g + 1][bs], lo), kd[g][ws], biases[b], sink)
                o0, o1 = _unstack_heads(_mm(pn, vd[g][ws]), lo)
                o_ref[bs, 2 * g * LANES:(2 * g + 1) * LANES] = o0
                o_ref[bs, (2 * g + 1) * LANES:(2 * g + 2) * LANES] = o1

    return pl.pallas_call(
        body, name="attn_fwd", grid=(T // tq,),
        in_specs=_attn_specs(T, tq, lambda i: i),
        out_specs=pl.BlockSpec((tq, B_WIDTH), lambda i: (i, 0)), out_shape=_sds((T, B_WIDTH)),
        compiler_params=_cp("parallel"))(z, z, z, z, z, cs, sn, cs, sn, qg, kg, sinks)


def _attn_bwd(z, cs, sn, qg, kg, sinks, o, do):
    T = z.shape[0]
    tq = min(T, 512)
    nb = tq // WINDOW
    nt = T // tq
    tk = tq + WINDOW

    def tile_of(i):
        return nt - 1 - i

    def body(q_ref, k_ref, v_ref, kp_ref, vp_ref, cs_ref, sn_ref, csp_ref, snp_ref, qg_ref, kg_ref, sk_ref, o_ref, do_ref,
             dq_ref, dk_ref, dv_ref, dqg_ref, dkg_ref, dsk_ref, acck, accv, ck, cv):
        i = pl.program_id(0)
        ti = nt - 1 - i

        @pl.when(i == 0)
        def _():
            dqg_ref[...] = jnp.zeros_like(dqg_ref)
            dkg_ref[...] = jnp.zeros_like(dkg_ref)
            dsk_ref[...] = jnp.zeros_like(dsk_ref)
            ck[...] = jnp.zeros_like(ck)
            cv[...] = jnp.zeros_like(cv)

        csq, snq = cs_ref[...], sn_ref[...]
        cs_all = jnp.concatenate([csp_ref[...], csq], axis=0)
        sn_all = jnp.concatenate([snp_ref[...], snq], axis=0)
        k_all = jnp.concatenate([kp_ref[...], k_ref[...]], axis=0)
        v_all = jnp.concatenate([vp_ref[...], v_ref[...]], axis=0)
        kr, kh, rk = _qk_norm_rope(k_all, kg_ref[...], cs_all, sn_all)
        lo_all = _lane(k_all.shape) < HEAD_DIM
        kd = _dup_heads(kr, lo_all)
        vd = _dup_heads(v_all, lo_all)
        lo = _lane((WINDOW, LANES)) < HEAD_DIM
        acck[...] = jnp.zeros_like(acck)
        accv[...] = jnp.zeros_like(accv)
        prep = [_qk_norm_rope(q_ref[:, pr * LANES:(pr + 1) * LANES], qg_ref[...], csq, snq) for pr in range(4)]
        biases = [_attn_bias(ti * nb + b == 0) for b in range(nb)]
        dqs = [[None] * nb for _ in range(4)]
        for g in range(2):
            sink = _sink_column(sk_ref, g)
            dsink = jnp.zeros((HEADS_PER_KV * WINDOW, 1), F32)
            for b in range(nb):
                bs = slice(b * WINDOW, (b + 1) * WINDOW)
                ws = slice(b * WINDOW, (b + 2) * WINDOW)
                kw, vw = kd[g][ws], vd[g][ws]
                q4 = _stack_heads(prep[2 * g][0][bs] * SCALE, prep[2 * g + 1][0][bs] * SCALE, lo)
                pn, psink = _attn_probs(q4, kw, biases[b], sink)
                o0, o1 = o_ref[bs, 2 * g * LANES:(2 * g + 1) * LANES], o_ref[bs, (2 * g + 1) * LANES:(2 * g + 2) * LANES]
                do4 = _stack_heads(do_ref[bs, 2 * g * LANES:(2 * g + 1) * LANES], do_ref[bs, (2 * g + 1) * LANES:(2 * g + 2) * LANES], lo)
                delta = jnp.sum(do4 * jnp.concatenate([o0, o0, o1, o1], axis=0), axis=-1, keepdims=True)
                ds = pn * (_mm_nt(do4, vw) - delta)
                dsink = dsink - psink * delta
                dqs[2 * g][b], dqs[2 * g + 1][b] = _unstack_heads(_mm(ds, kw) * SCALE, lo)
                acck[g, ws, :] += _mm_tn(ds, q4)
                accv[g, ws, :] += _mm_tn(pn, do4)
            for hh in range(HEADS_PER_KV):
                a = HEADS_PER_KV * g + hh
                dsk_ref[a:a + 1, :] += jnp.zeros((1, LANES), F32) + jnp.sum(dsink[hh * WINDOW:(hh + 1) * WINDOW])
        for pr in range(4):
            _, qh, rq = prep[pr]
            dx, dg = _qk_norm_rope_bwd(qh, rq, qg_ref[...], csq, snq, jnp.concatenate(dqs[pr], axis=0))
            dq_ref[:, pr * LANES:(pr + 1) * LANES] = dx
            dqg_ref[...] += dg

        def fold(acc):
            f0 = acc[0] + _swap64(acc[0])
            f1 = acc[1] + _swap64(acc[1])
            return jnp.where(lo_all, f0, f1)

        dk_all = fold(acck)
        dv_all = fold(accv)
        pad = jnp.zeros((tq - WINDOW, LANES), F32)
        dk_own = dk_all[WINDOW:] + (jnp.concatenate([pad, ck[...]], axis=0) if nb > 1 else ck[...])
        dv_own = dv_all[WINDOW:] + (jnp.concatenate([pad, cv[...]], axis=0) if nb > 1 else cv[...])
        ck[...] = dk_all[:WINDOW]
        cv[...] = dv_all[:WINDOW]
        dxk, dgk = _qk_norm_rope_bwd(kh[WINDOW:], rk[WINDOW:], kg_ref[...], csq, snq, dk_own)
        dk_ref[...] = dxk
        dkg_ref[...] += dgk
        dv_ref[...] = dv_own

    row = pl.BlockSpec((1, LANES), lambda i: (0, 0))
    return pl.pallas_call(
        body, name="attn_bwd", grid=(nt,),
        in_specs=_attn_specs(T, tq, tile_of) + [pl.BlockSpec((tq, B_WIDTH), lambda i: (tile_of(i), 0))] * 2,
        out_specs=[pl.BlockSpec((tq, B_WIDTH), lambda i: (tile_of(i), 0)), pl.BlockSpec((tq, LANES), lambda i: (tile_of(i), 0)),
                   pl.BlockSpec((tq, LANES), lambda i: (tile_of(i), 0)), row, row, pl.BlockSpec((8, LANES), lambda i: (0, 0))],
        out_shape=[_sds((T, B_WIDTH)), _sds((T, LANES)), _sds((T, LANES)), _sds((1, LANES)), _sds((1, LANES)), _sds((8, LANES))],
        scratch_shapes=[pltpu.VMEM((2, tk, LANES), F32), pltpu.VMEM((2, tk, LANES), F32),
                        pltpu.VMEM((WINDOW, LANES), F32), pltpu.VMEM((WINDOW, LANES), F32)],
        compiler_params=_cp("arbitrary"))(z, z, z, z, z, cs, sn, cs, sn, qg, kg, sinks, o, do)


def _bbar_t(are, aim, ldt, btr, bti):
    lbr, lbi = _lam_bar(are, aim, ldt)
    den = are * are + aim * aim
    nr = lbr - 1.0
    cr = (nr * are + lbi * aim) / den
    ci = (lbi * are - nr * aim) / den
    return cr * btr - ci * bti, cr * bti + ci * btr


def _lam_bar(are, aim, ldt):
    dt = jnp.exp(ldt)
    er = jnp.exp(are * dt)
    return er * jnp.cos(aim * dt), er * jnp.sin(aim * dt)


def _block_diag(x):
    t = jnp.concatenate([x] * C_GROUPS, axis=1)
    return jnp.where(_row(t.shape) // C_GROUP == _lane(t.shape) // C_STATE, t, 0.0)


def _block_diag_fold(m):
    rg = _row((C_WIDTH, C_STATE)) // C_GROUP
    acc = jnp.zeros((C_WIDTH, C_STATE), F32)
    for g in range(C_GROUPS):
        acc = acc + jnp.where(rg == g, m[:, g * C_STATE:(g + 1) * C_STATE], 0.0)
    return acc


def _ssm_prep(are, aim, ldt, are_x, aim_x, ldt_x, btr, bti, cre, cim):
    def body(are_r, aim_r, ldt_r, arex_r, aimx_r, ldtx_r, btr_r, bti_r, cre_r, cim_r, bbd_ref, cbd_ref, lr_ref, li_ref):
        lr, li = _lam_bar(are_r[...], aim_r[...], ldt_r[...])
        lr_ref[...] = lr
        li_ref[...] = li
        br, bi = _bbar_t(arex_r[...], aimx_r[...], ldtx_r[...], btr_r[...], bti_r[...])
        bbd_ref[...] = jnp.concatenate([_block_diag(br), _block_diag(bi)], axis=1).astype(MXU)
        cbd_ref[...] = jnp.concatenate([_block_diag(cre_r[...]), -_block_diag(cim_r[...])], axis=1).astype(MXU)

    return pl.pallas_call(
        body, name="ssm_prep",
        out_shape=[_sds((C_WIDTH, 2 * N_STATE), MXU), _sds((C_WIDTH, 2 * N_STATE), MXU), _sds((C_GROUPS, C_STATE)), _sds((C_GROUPS, C_STATE))],
        compiler_params=pltpu.CompilerParams(vmem_limit_bytes=VMEM_LIMIT))(are, aim, ldt, are_x, aim_x, ldt_x, btr, bti, cre, cim)


def _ssm_param_bwd(are, aim, ldt, are_x, aim_x, ldt_x, btr, bti, dlr, dli, dbbd, dcr, dci):
    def body(are_r, aim_r, ldt_r, arex_r, aimx_r, ldtx_r, btr_r, bti_r, dlr_r, dli_r, dbbd_r, dcr_r, dci_r,
             dare_ref, daim_ref, dldt_ref, dbtr_ref, dbti_ref, dcre_ref, dcim_ref):
        _, vjp_l = jax.vjp(_lam_bar, are_r[...], aim_r[...], ldt_r[...])
        da1, di1, dl1 = vjp_l((dlr_r[...], dli_r[...]))
        dbr = _block_diag_fold(dbbd_r[:, 0:N_STATE])
        dbi = _block_diag_fold(dbbd_r[:, N_STATE:2 * N_STATE])
        _, vjp_b = jax.vjp(_bbar_t, arex_r[...], aimx_r[...], ldtx_r[...], btr_r[...], bti_r[...])
        da2, di2, dl2, dbtr, dbti = vjp_b((dbr, dbi))

        def gsum(x):
            return x.reshape(C_GROUPS, C_GROUP, C_STATE).sum(axis=1)

        dare_ref[...] = da1 + gsum(da2)
        daim_ref[...] = di1 + gsum(di2)
        dldt_ref[...] = jnp.broadcast_to(jnp.sum(dl1 + gsum(dl2), axis=-1, keepdims=True), (C_GROUPS, LANES))
        dbtr_ref[...] = dbtr
        dbti_ref[...] = dbti
        dcre_ref[...] = _block_diag_fold(dcr_r[...])
        dcim_ref[...] = -_block_diag_fold(dci_r[...])

    g = _sds((C_GROUPS, C_STATE))
    x = _sds((C_WIDTH, C_STATE))
    return pl.pallas_call(
        body, name="ssm_param_bwd", out_shape=[g, g, _sds((C_GROUPS, LANES)), x, x, x, x],
        compiler_params=pltpu.CompilerParams(vmem_limit_bytes=VMEM_LIMIT))(are, aim, ldt, are_x, aim_x, ldt_x, btr, bti, dlr, dli, dbbd, dcr, dci)


def _ssm_fwd(z, bbd, lr, li, cbd, dsk, w1, w2):
    T = z.shape[0]
    tt = min(T, 256)

    def body(u_ref, bbd_ref, lr_ref, li_ref, cbd_ref, d_ref, w1_ref, w2_ref, yc_ref, xr_ref, xi_ref, sr, si):
        @pl.when(pl.program_id(0) == 0)
        def _():
            sr[...] = jnp.zeros_like(sr)
            si[...] = jnp.zeros_like(si)

        u = u_ref[...]
        bu = _mm(u, bbd_ref[...])
        xr_ref[...] = bu[:, 0:N_STATE]
        xi_ref[...] = bu[:, N_STATE:2 * N_STATE]
        lam_r, lam_i = lr_ref[...], li_ref[...]

        def step(t, c):
            cr, ci = c
            nr = lam_r * cr - lam_i * ci + xr_ref[pl.ds(t, 1), :]
            ni = lam_r * ci + lam_i * cr + xi_ref[pl.ds(t, 1), :]
            xr_ref[pl.ds(t, 1), :] = nr
            xi_ref[pl.ds(t, 1), :] = ni
            return nr, ni

        cr, ci = lax.fori_loop(0, tt, step, (sr[...], si[...]), unroll=8)
        sr[...] = cr
        si[...] = ci
        x = jnp.concatenate([xr_ref[...], xi_ref[...]], axis=1)
        y2 = _gelu(_mm_nt(x, cbd_ref[...]) + d_ref[...] * u)
        yc_ref[...] = _mm(y2, w1_ref[...]) * _sigmoid(_mm(y2, w2_ref[...]))

    big = pl.BlockSpec((C_WIDTH, 2 * N_STATE), lambda i: (0, 0))
    srow = pl.BlockSpec((1, N_STATE), lambda i: (0, 0))
    wsp = pl.BlockSpec((C_WIDTH, C_WIDTH), lambda i: (0, 0))
    xs = pl.BlockSpec((tt, N_STATE), lambda i: (i, 0))
    return pl.pallas_call(
        body, name="ssm_fwd", grid=(T // tt,),
        in_specs=[pl.BlockSpec((tt, C_WIDTH), lambda i: (i, 5)), big, srow, srow, big, pl.BlockSpec((1, C_WIDTH), lambda i: (0, 0)), wsp, wsp],
        out_specs=[pl.BlockSpec((tt, C_WIDTH), lambda i: (i, 0)), xs, xs],
        out_shape=[_sds((T, C_WIDTH)), _sds((T, N_STATE)), _sds((T, N_STATE))],
        scratch_shapes=[pltpu.VMEM((1, N_STATE), F32)] * 2,
        compiler_params=_cp("arbitrary"))(z, bbd, lr, li, cbd, dsk, w1, w2)


def _ssm_bwd(dyc, z, xr, xi, bbd, lr, li, cbd, dsk, w1, w2):
    T = z.shape[0]
    tt = min(T, 256)
    nt = T // tt

    def tile_of(i):
        return nt - 1 - i

    def body(dyc_ref, u_ref, xr_ref, xi_ref, xpr_ref, xpi_ref, bbd_ref, lr_ref, li_ref, cbd_ref, d_ref, w1_ref, w2_ref,
             du_ref, y2_ref, da1_ref, da2_ref, dy_ref, arai_ref, dlr_ref, dli_ref, dd_ref, gr, gi, sr, si):
        i = pl.program_id(0)
        ti = nt - 1 - i

        @pl.when(i == 0)
        def _():
            sr[...] = jnp.zeros_like(sr)
            si[...] = jnp.zeros_like(si)
            dlr_ref[...] = jnp.zeros_like(dlr_ref)
            dli_ref[...] = jnp.zeros_like(dli_ref)
            dd_ref[...] = jnp.zeros_like(dd_ref)

        u = u_ref[...]
        xr_t, xi_t = xr_ref[...], xi_ref[...]
        y = _mm_nt(jnp.concatenate([xr_t, xi_t], axis=1), cbd_ref[...]) + d_ref[...] * u
        y2 = _gelu(y)
        a1 = _mm(y2, w1_ref[...])
        sg = _sigmoid(_mm(y2, w2_ref[...]))
        dyc_t = dyc_ref[...]
        da1 = dyc_t * sg
        da2 = dyc_t * a1 * sg * (1.0 - sg)
        dy = (_mm_nt(da1, w1_ref[...]) + _mm_nt(da2, w2_ref[...])) * _gelu_grad(y)
        gx = _mm(dy, cbd_ref[...])
        gr[...] = gx[:, 0:N_STATE]
        gi[...] = gx[:, N_STATE:2 * N_STATE]
        lam_r, lam_i = lr_ref[...], li_ref[...]

        def step(k, c):
            t = tt - 1 - k
            cr, ci = c
            nr = gr[pl.ds(t, 1), :] + lam_r * cr + lam_i * ci
            ni = gi[pl.ds(t, 1), :] - lam_i * cr + lam_r * ci
            gr[pl.ds(t, 1), :] = nr
            gi[pl.ds(t, 1), :] = ni
            return nr, ni

        cr, ci = lax.fori_loop(0, tt, step, (sr[...], si[...]), unroll=8)
        sr[...] = cr
        si[...] = ci
        ar, ai = gr[...], gi[...]
        first_row = _row(ar.shape) == 0
        live = jnp.where(ti > 0, 1.0, 0.0)
        xsr = jnp.where(first_row, xpr_ref[7:8, :] * live, pltpu.roll(xr_t, 1, 0))
        xsi = jnp.where(first_row, xpi_ref[7:8, :] * live, pltpu.roll(xi_t, 1, 0))
        dlr_ref[...] += jnp.sum(ar * xsr + ai * xsi, axis=0, keepdims=True)
        dli_ref[...] += jnp.sum(ai * xsr - ar * xsi, axis=0, keepdims=True)
        dd_ref[...] += jnp.sum(dy * u, axis=0, keepdims=True)
        arai = jnp.concatenate([ar, ai], axis=1)
        du_ref[...] = _mm_nt(arai, bbd_ref[...]) + d_ref[...] * dy
        y2_ref[...] = y2.astype(MXU)
        da1_ref[...] = da1.astype(MXU)
        da2_ref[...] = da2.astype(MXU)
        dy_ref[...] = dy.astype(MXU)
        arai_ref[...] = arai.astype(MXU)

    def prev(i):
        return jnp.maximum(tile_of(i) * (tt // 8) - 1, 0)

    big = pl.BlockSpec((C_WIDTH, 2 * N_STATE), lambda i: (0, 0))
    srow = pl.BlockSpec((1, N_STATE), lambda i: (0, 0))
    wsp = pl.BlockSpec((C_WIDTH, C_WIDTH), lambda i: (0, 0))
    xs = pl.BlockSpec((tt, N_STATE), lambda i: (tile_of(i), 0))
    xp = pl.BlockSpec((8, N_STATE), lambda i: (prev(i), 0))
    cw = pl.BlockSpec((tt, C_WIDTH), lambda i: (tile_of(i), 0))
    drow = pl.BlockSpec((1, C_WIDTH), lambda i: (0, 0))
    return pl.pallas_call(
        body, name="ssm_bwd", grid=(nt,),
        in_specs=[cw, pl.BlockSpec((tt, C_WIDTH), lambda i: (tile_of(i), 5)), xs, xs, xp, xp, big, srow, srow, big, drow, wsp, wsp],
        out_specs=[cw, cw, cw, cw, cw, pl.BlockSpec((tt, 2 * N_STATE), lambda i: (tile_of(i), 0)), srow, srow, drow],
        out_shape=[_sds((T, C_WIDTH))] + [_sds((T, C_WIDTH), MXU)] * 4 + [_sds((T, 2 * N_STATE), MXU), _sds((1, N_STATE)), _sds((1, N_STATE)), _sds((1, C_WIDTH))],
        scratch_shapes=[pltpu.VMEM((tt, N_STATE), F32)] * 2 + [pltpu.VMEM((1, N_STATE), F32)] * 2,
        compiler_params=_cp("arbitrary"))(dyc, z, xr, xi, xr, xi, bbd, lr, li, cbd, dsk, w1, w2)


_GROUPS = ((0, A_WIDTH), (A_WIDTH, A_WIDTH + B_WIDTH), (A_WIDTH + B_WIDTH, D_MODEL))


def _merge_fwd(h, ya, yb, yc, g, w):
    T = h.shape[0]
    tm = min(T, 512)

    def body(h_ref, a_ref, b_ref, c_ref, g_ref, w_ref, o_ref):
        yn = jnp.concatenate([y * _rms_stat(y) for y in (a_ref[...], b_ref[...], c_ref[...])], axis=1) * g_ref[...]
        o_ref[...] = h_ref[...] + _mm(yn, w_ref[...])

    def rows(w_):
        return pl.BlockSpec((tm, w_), lambda i: (i, 0))

    return pl.pallas_call(
        body, name="merge_fwd", grid=(T // tm,),
        in_specs=[rows(D_MODEL), rows(A_WIDTH), rows(B_WIDTH), rows(C_WIDTH), pl.BlockSpec((1, D_MODEL), lambda i: (0, 0)),
                  pl.BlockSpec((D_MODEL, D_MODEL), lambda i: (0, 0))],
        out_specs=rows(D_MODEL), out_shape=_sds((T, D_MODEL)), compiler_params=_cp("parallel"))(h, ya, yb, yc, g, w)


def _merge_bwd(gres, ya, yb, yc, g, w):
    T = gres.shape[0]
    tm = min(T, 512)

    def body(gr_ref, a_ref, b_ref, c_ref, g_ref, w_ref, da_ref, db_ref, dc_ref, yn_ref, dg_ref):
        @pl.when(pl.program_id(0) == 0)
        def _():
            dg_ref[...] = jnp.zeros_like(dg_ref)

        dyn = _mm_nt(gr_ref[...], w_ref[...])
        yns, dgs = [], []
        for (c0, c1), y_ref, d_ref in zip(_GROUPS, (a_ref, b_ref, c_ref), (da_ref, db_ref, dc_ref)):
            y = y_ref[...]
            r = _rms_stat(y)
            gg = g_ref[:, c0:c1]
            dx, dg = _rms_bwd(y, r, gg, dyn[:, c0:c1])
            d_ref[...] = dx
            dgs.append(dg)
            yns.append(y * r * gg)
        dg_ref[...] += jnp.concatenate(dgs, axis=1)
        yn_ref[...] = jnp.concatenate(yns, axis=1).astype(MXU)

    def rows(w_):
        return pl.BlockSpec((tm, w_), lambda i: (i, 0))

    row = pl.BlockSpec((1, D_MODEL), lambda i: (0, 0))
    return pl.pallas_call(
        body, name="merge_bwd", grid=(T // tm,),
        in_specs=[rows(D_MODEL), rows(A_WIDTH), rows(B_WIDTH), rows(C_WIDTH), row, pl.BlockSpec((D_MODEL, D_MODEL), lambda i: (0, 0))],
        out_specs=[rows(A_WIDTH), rows(B_WIDTH), rows(C_WIDTH), rows(D_MODEL), row],
        out_shape=[_sds((T, A_WIDTH)), _sds((T, B_WIDTH)), _sds((T, C_WIDTH)), _sds((T, D_MODEL), MXU), _sds((1, D_MODEL))],
        compiler_params=_cp("arbitrary"))(gres, ya, yb, yc, g, w)


FF_BLOCK = D_FF // N_DEV


def _mlp_fwd(h, g, w1, w2):
    T = h.shape[0]
    tm = min(T, 1024)

    def body(h_ref, g_ref, w1_ref, w2_ref, o_ref, hn):
        @pl.when(pl.program_id(1) == 0)
        def _():
            x = h_ref[...]
            hn[...] = (x * _rms_stat(x) * g_ref[...]).astype(MXU)
            o_ref[...] = x

        a = jnp.maximum(_mm(hn[...], w1_ref[...]), 0.0)
        o_ref[...] += _mm(a * a, w2_ref[...])

    return pl.pallas_call(
        body, name="mlp_fwd", grid=(T // tm, N_DEV),
        in_specs=[pl.BlockSpec((tm, D_MODEL), lambda i, j: (i, 0)), pl.BlockSpec((1, D_MODEL), lambda i, j: (0, 0)),
                  pl.BlockSpec((None, D_MODEL, FF_BLOCK), lambda i, j: (j, 0, 0)), pl.BlockSpec((FF_BLOCK, D_MODEL), lambda i, j: (j, 0))],
        out_specs=pl.BlockSpec((tm, D_MODEL), lambda i, j: (i, 0)), out_shape=_sds((T, D_MODEL)),
        scratch_shapes=[pltpu.VMEM((tm, D_MODEL), MXU)],
        compiler_params=_cp("parallel", "arbitrary"))(h, g, w1, w2)


def _mlp_bwd(gres, h, g, w1, w2):
    T = h.shape[0]
    tm = min(T, 1024)

    def body(gr_ref, h_ref, g_ref, w1_ref, w2_ref, dh_ref, hn_ref, da_ref, r_ref, dg_ref, acc):
        i, j = pl.program_id(0), pl.program_id(1)

        @pl.when((i == 0) & (j == 0))
        def _():
            dg_ref[...] = jnp.zeros_like(dg_ref)

        @pl.when(j == 0)
        def _():
            x = h_ref[...]
            hn_ref[...] = (x * _rms_stat(x) * g_ref[...]).astype(MXU)
            acc[...] = jnp.zeros_like(acc)

        a = jnp.maximum(_mm(hn_ref[...], w1_ref[...]), 0.0)
        da = _mm_nt(gr_ref[...], w2_ref[...]) * (2.0 * a)
        acc[...] += _mm_nt(da, w1_ref[...])
        da_ref[...] = da.astype(MXU)
        r_ref[...] = (a * a).astype(MXU)

        @pl.when(j == N_DEV - 1)
        def _():
            x = h_ref[...]
            dx, dg = _rms_bwd(x, _rms_stat(x), g_ref[...], acc[...])
            dh_ref[...] = gr_ref[...] + dx
            dg_ref[...] += dg

    rows = pl.BlockSpec((tm, D_MODEL), lambda i, j: (i, 0))
    row = pl.BlockSpec((1, D_MODEL), lambda i, j: (0, 0))
    ffb = pl.BlockSpec((tm, FF_BLOCK), lambda i, j: (i, j))
    return pl.pallas_call(
        body, name="mlp_bwd", grid=(T // tm, N_DEV),
        in_specs=[rows, rows, row, pl.BlockSpec((None, D_MODEL, FF_BLOCK), lambda i, j: (j, 0, 0)),
                  pl.BlockSpec((FF_BLOCK, D_MODEL), lambda i, j: (j, 0))],
        out_specs=[rows, rows, ffb, ffb, row],
        out_shape=[_sds((T, D_MODEL)), _sds((T, D_MODEL), MXU), _sds((T, D_FF), MXU), _sds((T, D_FF), MXU), _sds((1, D_MODEL))],
        scratch_shapes=[pltpu.VMEM((tm, D_MODEL), F32)],
        compiler_params=_cp("arbitrary", "arbitrary"))(gres, h, g, w1, w2)


def _ple_fwd(h, p, l, g, wg, wp):
    T = h.shape[0]
    tm = min(T, 512)

    def body(h_ref, p_ref, g_ref, wg_ref, wp_ref, o_ref):
        x = h_ref[...]
        gate = _sigmoid(_mm(x * _rms_stat(x) * g_ref[...], wg_ref[...]))
        o_ref[...] = x + gate * _mm(p_ref[...], wp_ref[...])

    rows = pl.BlockSpec((tm, D_MODEL), lambda i: (i, 0))
    return pl.pallas_call(
        body, name="ple_fwd", grid=(T // tm,),
        in_specs=[rows, pl.BlockSpec((None, tm, PLE_DIM), lambda i: (l, i, 0)), pl.BlockSpec((1, D_MODEL), lambda i: (0, 0)),
                  pl.BlockSpec((D_MODEL, D_MODEL), lambda i: (0, 0)), pl.BlockSpec((PLE_DIM, D_MODEL), lambda i: (0, 0))],
        out_specs=rows, out_shape=_sds((T, D_MODEL)), compiler_params=_cp("parallel"))(h, p, g, wg, wp)


def _ple_bwd(gres, h, p, l, g, wg, wp):
    T = h.shape[0]
    tm = min(T, 512)

    def body(gr_ref, h_ref, p_ref, g_ref, wg_ref, wp_ref, dh_ref, hn_ref, dgp_ref, de_ref, dg_ref):
        @pl.when(pl.program_id(0) == 0)
        def _():
            dg_ref[...] = jnp.zeros_like(dg_ref)

        x = h_ref[...]
        r = _rms_stat(x)
        gg = g_ref[...]
        hn = x * r * gg
        gate = _sigmoid(_mm(hn, wg_ref[...]))
        e = _mm(p_ref[...], wp_ref[...])
        gr = gr_ref[...]
        dgp = gr * e * gate * (1.0 - gate)
        dx, dg = _rms_bwd(x, r, gg, _mm_nt(dgp, wg_ref[...]))
        dh_ref[...] = gr + dx
        dg_ref[...] += dg
        hn_ref[...] = hn.astype(MXU)
        dgp_ref[...] = dgp.astype(MXU)
        de_ref[...] = (gr * gate).astype(MXU)

    rows = pl.BlockSpec((tm, D_MODEL), lambda i: (i, 0))
    row = pl.BlockSpec((1, D_MODEL), lambda i: (0, 0))
    return pl.pallas_call(
        body, name="ple_bwd", grid=(T // tm,),
        in_specs=[rows, rows, pl.BlockSpec((None, tm, PLE_DIM), lambda i: (l, i, 0)), row,
                  pl.BlockSpec((D_MODEL, D_MODEL), lambda i: (0, 0)), pl.BlockSpec((PLE_DIM, D_MODEL), lambda i: (0, 0))],
        out_specs=[rows, rows, rows, rows, row],
        out_shape=[_sds((T, D_MODEL))] + [_sds((T, D_MODEL), MXU)] * 3 + [_sds((1, D_MODEL))],
        compiler_params=_cp("arbitrary"))(gres, h, p, g, wg, wp)


def _loss_head(h, target):
    T = h.shape[0]
    tm = min(T, 1024)

    def body(h_ref, t_ref, dh_ref, l_ref):
        @pl.when(pl.program_id(0) == 0)
        def _():
            l_ref[...] = jnp.zeros_like(l_ref)

        e = h_ref[...] - t_ref[...]
        dh_ref[...] = e * (1.0 / D_MODEL)
        l_ref[...] += jnp.zeros_like(l_ref) + 0.5 * jnp.sum(jnp.mean(e * e, axis=-1, keepdims=True))

    rows = pl.BlockSpec((tm, D_MODEL), lambda i: (i, 0))
    return pl.pallas_call(
        body, name="loss_head", grid=(T // tm,), in_specs=[rows, rows],
        out_specs=[rows, pl.BlockSpec((8, LANES), lambda i: (0, 0))], out_shape=[_sds((T, D_MODEL)), _sds((8, LANES))],
        compiler_params=_cp("arbitrary"))(h, target)


def _tn(a, b, m, n, *, bm, bn, a_off=0, b_off=0, a_lead=None, n_major=False, split=None, dtype=F32, name="tn"):
    T = a.shape[-2]
    tk = min(T, 1024)
    nk = T // tk
    if a_lead is None:
        a_spec = pl.BlockSpec((tk, bm), lambda i, j, k: (k, a_off + i))
    else:
        a_spec = pl.BlockSpec((None, tk, bm), lambda i, j, k: (a_lead, k, a_off + i))
    assert m % bm == 0 and n % bn == 0 and (not n_major or bm == m) and (split is None or (bn == n and n % split == 0))

    def body(a_ref, b_ref, o_ref, acc):
        k = pl.program_id(2)

        @pl.when(k == 0)
        def _():
            acc[...] = jnp.zeros_like(acc)

        acc[...] += _mm_tn(a_ref[...], b_ref[...])

        @pl.when(k == nk - 1)
        def _():
            if split is None:
                o_ref[...] = acc[...].astype(dtype)
            else:
                for d in range(n // split):
                    o_ref[d] = acc[:, d * split:(d + 1) * split].astype(dtype)

    if split is not None:
        out_spec = pl.BlockSpec((n // split, bm, split), lambda i, j, k: (0, i, 0))
        out_shape = _sds((n // split, m, split), dtype)
    elif n_major:
        out_spec = pl.BlockSpec((None, bm, bn), lambda i, j, k: (j, 0, 0))
        out_shape = _sds((n // bn, m, bn), dtype)
    else:
        out_spec = pl.BlockSpec((bm, bn), lambda i, j, k: (i, j))
        out_shape = _sds((m, n), dtype)
    return pl.pallas_call(
        body, name=name, grid=(m // bm, n // bn, nk),
        in_specs=[a_spec, pl.BlockSpec((tk, bn), lambda i, j, k: (k, b_off + j))],
        out_specs=out_spec, out_shape=out_shape, scratch_shapes=[pltpu.VMEM((bm, bn), F32)],
        compiler_params=_cp("parallel", "parallel", "arbitrary"))(a, b)


def _row_tile(R, C):
    for cand in (512, 256, 128, 64, 32, 16, 8):
        if R % cand == 0 and cand * C * 4 <= 2 ** 20:
            return cand
    return R


def _sum_slots(land):
    S, R, C = land.shape
    tr = _row_tile(R, C)

    def body(l_ref, o_ref):
        acc = l_ref[0].astype(F32)
        for s in range(1, S):
            acc = acc + l_ref[s].astype(F32)
        o_ref[...] = acc

    return pl.pallas_call(
        body, name="sum_slots", grid=(R // tr,), in_specs=[pl.BlockSpec((S, tr, C), lambda i: (0, i, 0))],
        out_specs=pl.BlockSpec((tr, C), lambda i: (i, 0)), out_shape=_sds((R, C)), compiler_params=_cp("parallel"))(land)


def _sum_adamw(land, w, m, v):
    R, C = w.shape
    S = land.shape[0]
    tr = _row_tile(R, C)

    def body(l_ref, w_ref, m_ref, v_ref, g_ref, d_ref, nm_ref, nv_ref):
        gg = l_ref[0].astype(F32)
        for s in range(1, S):
            gg = gg + l_ref[s].astype(F32)
        g_ref[...] = gg
        nm = ADAM_B1 * m_ref[...] + (1.0 - ADAM_B1) * gg
        nv = ADAM_B2 * v_ref[...] + (1.0 - ADAM_B2) * (gg * gg)
        m_hat = nm / (1.0 - ADAM_B1 ** ADAM_STEP)
        v_hat = nv / (1.0 - ADAM_B2 ** ADAM_STEP)
        d_ref[...] = -ADAM_LR * (m_hat / (jnp.sqrt(v_hat) + ADAM_EPS) + ADAM_WD * w_ref[...])
        nm_ref[...] = nm
        nv_ref[...] = nv

    blk = pl.BlockSpec((tr, C), lambda i: (i, 0))
    return pl.pallas_call(
        body, name="sum_adamw", grid=(R // tr,), in_specs=[pl.BlockSpec((S, tr, C), lambda i: (0, i, 0))] + [blk] * 3,
        out_specs=[blk] * 4, out_shape=[_sds((R, C))] * 4, compiler_params=_cp("parallel"))(land, w, m, v)


def _all_to_all(pairs, name):
    n = len(pairs)

    def body(*refs):
        srcs, lands = refs[:n], refs[2 * n:3 * n]
        send, recv, loc = refs[3 * n:]
        x, y, c = lax.axis_index("x"), lax.axis_index("y"), lax.axis_index("c")
        me = 4 * x + 2 * y + c
        own = [pltpu.make_async_copy(pairs[t][2](srcs[t], me, me), pairs[t][3](lands[t], me), loc.at[t]) for t in range(n)]
        for cp in own:
            cp.start()
        sends, recvs = [], []
        for k in range(1, N_DEV):
            px, py, pc = x ^ (k >> 2), y ^ ((k >> 1) & 1), c ^ (k & 1)
            peer = 4 * px + 2 * py + pc
            for t in range(n):
                src = pairs[t][2](srcs[t], me, peer)
                cp = pltpu.make_async_remote_copy(src_ref=src, dst_ref=pairs[t][3](lands[t], me), send_sem=send.at[t, k],
                                                  recv_sem=recv.at[t, k], device_id=(px, py, pc), device_id_type=MESH)
                cp.start()
                sends.append(cp)
                recvs.append(pltpu.make_async_remote_copy(src_ref=src, dst_ref=pairs[t][3](lands[t], peer), send_sem=send.at[t, k],
                                                          recv_sem=recv.at[t, k], device_id=(px, py, pc), device_id_type=MESH))
        for cp in recvs:
            cp.wait_recv()
        for cp in sends:
            cp.wait_send()
        for cp in own:
            cp.wait()

    anyspec = pl.BlockSpec(memory_space=pl.ANY)
    lands = [pr[1] for pr in pairs]
    return pl.pallas_call(
        body, name=name, in_specs=[anyspec] * (2 * n), out_specs=[anyspec] * n,
        out_shape=[_sds(a.shape, a.dtype) for a in lands], input_output_aliases={n + t: t for t in range(n)},
        scratch_shapes=[pltpu.SemaphoreType.DMA((n, N_DEV)), pltpu.SemaphoreType.DMA((n, N_DEV)), pltpu.SemaphoreType.DMA((n,))],
        compiler_params=pltpu.CompilerParams(has_side_effects=True))(*[pr[0] for pr in pairs], *lands)


def _gather_whole(src):
    land = lax.empty((N_DEV,) + src.shape, src.dtype)
    return _all_to_all([(src, land, lambda ref, me, peer: ref, lambda ref, sender: ref.at[sender])], "gather_small_grads")[0]


_HBM = pl.BlockSpec(memory_space=pltpu.HBM)
_SEM = pl.BlockSpec(memory_space=pltpu.SEMAPHORE)
_EFFECT = pltpu.SideEffectType.DATAFLOW_SIDE_EFFECTING


def _peers():
    x, y, c = lax.axis_index("x"), lax.axis_index("y"), lax.axis_index("c")
    out = []
    for k in range(1, N_DEV):
        px, py, pc = x ^ (k >> 2), y ^ ((k >> 1) & 1), c ^ (k & 1)
        out.append((k, (px, py, pc), 4 * px + 2 * py + pc))
    return 4 * x + 2 * y + c, out


def _route(mode, layer):
    if mode == "all":
        return (lambda ref, peer: ref), (lambda ref, sender: ref.at[sender])
    return (lambda ref, peer: ref.at[peer]), (lambda ref, sender: ref.at[sender, layer])


def _split_start(srcs, lands, modes, layer, after, name):
    n = len(srcs)
    routes = [_route(m, layer) for m in modes]

    def body(*refs):
        src, land = refs[:n], refs[n:2 * n]
        send, recv, token = refs[2 * n + 1], refs[2 * n + 2], refs[-1]
        me, peers = _peers()
        for k, dev, peer in peers:
            for t, (src_of, dst_of) in enumerate(routes):
                pltpu.make_async_remote_copy(src_ref=src_of(src[t], peer), dst_ref=dst_of(land[t], me), send_sem=send.at[t * N_DEV + k],
                                             recv_sem=recv.at[t * N_DEV + k], device_id=dev, device_id_type=MESH).start()
        token[...] = jnp.zeros_like(token)

    bufs = list(srcs) + list(lands)
    outs = pl.pallas_call(
        body, name=name,
        out_shape=(pltpu.SemaphoreType.DMA((n * N_DEV,)), pltpu.SemaphoreType.DMA((n * N_DEV,)),
                   *[pltpu.HBM(a.shape, a.dtype) for a in bufs], _sds((8, LANES))),
        in_specs=[_HBM] * (2 * n) + [pl.BlockSpec(memory_space=pl.ANY)],
        out_specs=(_SEM, _SEM, *[_HBM] * (2 * n), pl.BlockSpec(memory_space=pltpu.VMEM)),
        input_output_aliases={i: 2 + i for i in range(2 * n)},
        compiler_params=pltpu.CompilerParams(has_side_effects=_EFFECT),
    )(*[pltpu.with_memory_space_constraint(a, pltpu.HBM) for a in bufs], after)
    return outs[0], outs[1], list(outs[2:2 + n]), list(outs[2 + n:2 + 2 * n]), outs[-1]


def _split_wait(send, recv, srcs, lands, modes, layer, after, name):
    n = len(srcs)
    routes = [_route(m, layer) for m in modes]

    def body(*refs):
        src, land = refs[:n], refs[n:2 * n]
        send_r, recv_r = refs[2 * n], refs[2 * n + 1]
        _, peers = _peers()
        for k, dev, peer in peers:
            for t, (src_of, dst_of) in enumerate(routes):
                cp = pltpu.make_async_remote_copy(src_ref=src_of(src[t], peer), dst_ref=dst_of(land[t], peer), send_sem=send_r.at[t * N_DEV + k],
                                                  recv_sem=recv_r.at[t * N_DEV + k], device_id=dev, device_id_type=MESH)
                cp.wait_send()
                cp.wait_recv()

    bufs = list(srcs) + list(lands)
    outs = pl.pallas_call(
        body, name=name, out_shape=tuple(pltpu.HBM(a.shape, a.dtype) for a in bufs),
        in_specs=[_HBM] * (2 * n) + [_SEM, _SEM, pl.BlockSpec(memory_space=pl.ANY)], out_specs=[_HBM] * (2 * n),
        input_output_aliases={i: i for i in range(2 * n)},
        compiler_params=pltpu.CompilerParams(has_side_effects=_EFFECT),
    )(*bufs, send, recv, after)
    return list(outs[:n]), list(outs[n:])


SHARDED = ("w_in", "glu_w1", "glu_w2", "w_out", "w_ff1", "w_ff2", "w_ple_gate", "w_ple_proj")
SMALL = ("attn_norm_g", "gmlp_ln_g", "gmlp_ln_b", "gmlp_ws", "gmlp_bs", "q_norm_g", "k_norm_g", "sinks", "ssm_a_re", "ssm_a_im",
         "ssm_log_dt", "ssm_b_re", "ssm_b_im", "ssm_c_re", "ssm_c_im", "ssm_d", "mix_out_g", "mlp_norm_g", "ple_norm_g")
WEIGHTS = ("attn_norm_g", "w_in", "gmlp_ln_g", "gmlp_ln_b", "gmlp_ws", "gmlp_bs", "q_norm_g", "k_norm_g", "sinks", "ssm_a_re", "ssm_a_im",
           "ssm_log_dt", "ssm_b_re", "ssm_b_im", "ssm_c_re", "ssm_c_im", "ssm_d", "glu_w1", "glu_w2", "mix_out_g", "w_out", "mlp_norm_g",
           "w_ff1", "w_ff2", "ple_norm_g", "w_ple_gate", "w_ple_proj")
FLAT_COLS = 1024


PACK_TILE_ROWS = 8
PACK_ROWS_MULTIPLE = PACK_TILE_ROWS * N_DEV


def _packed_rows(shape):
    return -(-math.prod(shape) // (PACK_TILE_ROWS * FLAT_COLS)) * PACK_TILE_ROWS


def _pack(arrs, dtype):
    blocks = []
    for a in arrs:
        flat = a.astype(dtype).reshape(-1)
        pad = _packed_rows(a.shape) * FLAT_COLS - flat.shape[0]
        if pad:
            flat = jnp.concatenate([flat, jnp.zeros((pad,), dtype)])
        blocks.append(flat.reshape(-1, FLAT_COLS))
    rows = sum(b.shape[0] for b in blocks)
    if rows % PACK_ROWS_MULTIPLE:
        blocks.append(jnp.zeros((PACK_ROWS_MULTIPLE - rows % PACK_ROWS_MULTIPLE, FLAT_COLS), dtype))
    return jnp.concatenate(blocks, axis=0)


def _unpack(flat, shapes):
    out, r = [], 0
    for s in shapes:
        nr = _packed_rows(s)
        out.append(flat[r:r + nr].reshape(-1)[:math.prod(s)].reshape(s))
        r += nr
    return out


def _col_major(w):
    rows, cols = w.shape
    return w.reshape(rows, N_DEV, cols // N_DEV).transpose(1, 0, 2)


def _from_col_major(s):
    n, rows, cs = s.shape
    return s.transpose(1, 0, 2).reshape(rows, n * cs)


EARLY = ("w_in", "glu_w1", "glu_w2")
LATE = ("w_out", "w_ff1", "w_ff2", "w_ple_gate", "w_ple_proj")
GRADS_MID = ("w_ff1", "w_ff2", "w_ple_gate", "w_ple_proj")
GRADS_END = ("w_in", "glu_w1", "glu_w2", "w_out")


def _layer_fwd(h, p, l, cs, sn, W, late_weights, S, sp):
    z = _inproj_fwd(h, S["attn_norm_g"], W["w_in"])
    ya = _gmlp_fwd(z, S["lng"], S["lnb"], S["gmlp_ws"], S["bsx"])
    yb = _attn_fwd(z, cs, sn, S["qg"], S["kg"], S["sinks"])
    yc, xr, xi = _ssm_fwd(z, sp["bbd"], sp["lr"], sp["li"], sp["cbd"], S["ssm_d"], W["glu_w1"], W["glu_w2"])
    late, token = late_weights(yc)
    W = {**W, **late}
    h1 = _merge_fwd(h, ya, yb, yc, _behind(S["mix_out_g"], token), W["w_out"])
    h2 = _mlp_fwd(h1, S["mlp_norm_g"], W["w_ff1"], W["w_ff2"])
    h3 = _ple_fwd(h2, p, l, S["ple_norm_g"], W["w_ple_gate"], W["w_ple_proj"])
    return h3, dict(h=h, z=z, ya=ya, yb=yb, yc=yc, xr=xr, xi=xi, h1=h1, h2=h2), W


def _layer_bwd(g3, p, l, cs, sn, W, S, sp, A, raw, mid_bwd):
    G = {}
    g2, hn3, dgp, de, G["ple_norm_g"] = _ple_bwd(g3, A["h2"], p, l, S["ple_norm_g"], W["w_ple_gate"], W["w_ple_proj"])
    G["w_ple_gate"] = _tn(hn3, dgp, D_MODEL, D_MODEL, bm=1024, bn=1024, dtype=WIRE, name="tn_gate").reshape(N_DEV, -1, D_MODEL)
    G["w_ple_proj"] = _tn(p, de, PLE_DIM, D_MODEL, bm=PLE_DIM, bn=D_MODEL, a_lead=l, split=D_MODEL // N_DEV, dtype=WIRE, name="tn_proj")
    g1, hn, da, r, G["mlp_norm_g"] = _mlp_bwd(g2, A["h1"], S["mlp_norm_g"], W["w_ff1"], W["w_ff2"])
    G["w_ff1"] = _tn(hn, da, D_MODEL, D_FF, bm=D_MODEL, bn=FF_BLOCK, n_major=True, dtype=WIRE, name="tn_ff1")
    G["w_ff2"] = _tn(r, g2, D_FF, D_MODEL, bm=1024, bn=1024, dtype=WIRE, name="tn_ff2").reshape(N_DEV, -1, D_MODEL)
    token = mid_bwd(g1, G)
    dya, dyb, dyc, yn, G["mix_out_g"] = _merge_bwd(g1, A["ya"], A["yb"], A["yc"], _behind(S["mix_out_g"], token), W["w_out"])
    G["w_out"] = _tn(yn, g1, D_MODEL, D_MODEL, bm=1024, bn=1024, dtype=WIRE, name="tn_out").reshape(N_DEV, -1, D_MODEL)
    dzc, y2, da1, da2, dy, arai, dlr, dli, dd = _ssm_bwd(dyc, A["z"], A["xr"], A["xi"], sp["bbd"], sp["lr"], sp["li"], sp["cbd"],
                                                        S["ssm_d"], W["glu_w1"], W["glu_w2"])
    G["glu_w1"] = _tn(y2, da1, C_WIDTH, C_WIDTH, bm=256, bn=256, dtype=WIRE, name="tn_glu1").reshape(N_DEV, -1, C_WIDTH)
    G["glu_w2"] = _tn(y2, da2, C_WIDTH, C_WIDTH, bm=256, bn=256, dtype=WIRE, name="tn_glu2").reshape(N_DEV, -1, C_WIDTH)
    dbbd = _tn(A["z"], arai, C_WIDTH, 2 * N_STATE, bm=256, bn=512, a_off=5, name="tn_bbd")
    dcr = _tn(dy, A["xr"], C_WIDTH, N_STATE, bm=256, bn=512, name="tn_cre")
    dci = _tn(dy, A["xi"], C_WIDTH, N_STATE, bm=256, bn=512, name="tn_cim")
    dare, daim, dldt, dbtr, dbti, dcre, dcim = _ssm_param_bwd(
        raw["are"], raw["aim"], raw["ldt"], raw["are_x"], raw["aim_x"], raw["ldt_x"], raw["btr"], raw["bti"],
        dlr.reshape(C_GROUPS, C_STATE), dli.reshape(C_GROUPS, C_STATE), dbbd, dcr, dci)
    G["ssm_a_re"], G["ssm_a_im"], G["ssm_log_dt"] = dare, daim, dldt[:, 0]
    G["ssm_b_re"] = dbtr.reshape(C_GROUPS, C_GROUP, C_STATE).transpose(0, 2, 1)
    G["ssm_b_im"] = dbti.reshape(C_GROUPS, C_GROUP, C_STATE).transpose(0, 2, 1)
    G["ssm_c_re"] = dcre.reshape(C_GROUPS, C_GROUP, C_STATE)
    G["ssm_c_im"] = dcim.reshape(C_GROUPS, C_GROUP, C_STATE)
    G["ssm_d"] = dd.reshape(C_GROUPS, C_GROUP)
    dzq, dzk, dzv, dqg, dkg, dsk = _attn_bwd(A["z"], cs, sn, S["qg"], S["kg"], S["sinks"], A["yb"], dyb)
    G["q_norm_g"] = dqg[0, :HEAD_DIM] + dqg[0, HEAD_DIM:]
    G["k_norm_g"] = dkg[0, :HEAD_DIM] + dkg[0, HEAD_DIM:]
    G["sinks"] = dsk[:, 0]
    dza, dws, dbs, dlng, dlnb = _gmlp_bwd(A["z"], dya, S["lng"], S["lnb"], S["gmlp_ws"], S["bsx"])
    G["gmlp_ws"] = dws
    G["gmlp_bs"] = dbs[:, :, 0]
    G["gmlp_ln_g"] = dlng.reshape(A_HEADS, 2, HEAD_DIM)[:, 1]
    G["gmlp_ln_b"] = dlnb.reshape(A_HEADS, 2, HEAD_DIM)[:, 1]
    g0, xn, dz, G["attn_norm_g"] = _inproj_bwd(g1, A["h"], S["attn_norm_g"], W["w_in"], dza, dzq, dzk, dzv, dzc)
    G["w_in"] = _tn(xn, dz, D_MODEL, IN_COLS, bm=D_MODEL, bn=IN_COLS, split=IN_COLS // N_DEV, dtype=WIRE, name="tn_in")
    return g0, G


def _small_layouts(P, l):
    def row(a):
        return a.reshape(1, -1)

    zeros = jnp.zeros((A_HEADS, HEAD_DIM), F32)
    S = dict(
        attn_norm_g=row(P["attn_norm_g"][l]), mix_out_g=row(P["mix_out_g"][l]), mlp_norm_g=row(P["mlp_norm_g"][l]),
        ple_norm_g=row(P["ple_norm_g"][l]),
        lng=jnp.stack([zeros, P["gmlp_ln_g"][l]], axis=1).reshape(1, 512),
        lnb=jnp.stack([zeros, P["gmlp_ln_b"][l]], axis=1).reshape(1, 512),
        gmlp_ws=P["gmlp_ws"][l],
        bsx=jnp.broadcast_to(P["gmlp_bs"][l][:, :, None], (A_HEADS, CHUNK, CHUNK)),
        qg=jnp.tile(P["q_norm_g"][l], 2).reshape(1, LANES), kg=jnp.tile(P["k_norm_g"][l], 2).reshape(1, LANES),
        sinks=jnp.broadcast_to(P["sinks"][l][:, None], (8, LANES)),
        ssm_d=row(P["ssm_d"][l]),
    )
    are, aim = P["ssm_a_re"][l], P["ssm_a_im"][l]
    ldt = jnp.broadcast_to(P["ssm_log_dt"][l][:, None], (C_GROUPS, C_STATE))
    raw = dict(
        are=are, aim=aim, ldt=ldt,
        are_x=jnp.repeat(are, C_GROUP, axis=0), aim_x=jnp.repeat(aim, C_GROUP, axis=0), ldt_x=jnp.repeat(ldt, C_GROUP, axis=0),
        btr=P["ssm_b_re"][l].transpose(0, 2, 1).reshape(C_WIDTH, C_STATE), bti=P["ssm_b_im"][l].transpose(0, 2, 1).reshape(C_WIDTH, C_STATE),
        cre=P["ssm_c_re"][l].reshape(C_WIDTH, C_STATE), cim=P["ssm_c_im"][l].reshape(C_WIDTH, C_STATE),
    )
    return S, raw


def _ssm_prep_layer(raw):
    bbd, cbd, lr, li = _ssm_prep(raw["are"], raw["aim"], raw["ldt"], raw["are_x"], raw["aim_x"], raw["ldt_x"],
                                 raw["btr"], raw["bti"], raw["cre"], raw["cim"])
    return dict(bbd=bbd, cbd=cbd, lr=lr.reshape(1, N_STATE), li=li.reshape(1, N_STATE))


def _behind(row, token):
    return row if token is None else row + token[0:1, 0:1]


def _local_step(x, p, positions, target, P, weights_of, mid_bwd, after_bwd):
    inv = 1.0 / (ROPE_THETA ** (jnp.arange(0, HEAD_DIM, 2, dtype=F32) / HEAD_DIM))
    cs, sn = _rope_tables(positions.reshape(-1, 1), jnp.tile(inv, 4).reshape(1, LANES))
    h = x
    acts, smalls, weights = [], [], []
    for l in range(DEPTH):
        W, late_weights, token = weights_of(l, h)
        S, raw = _small_layouts(P, l)
        sp = _ssm_prep_layer(raw)
        h, A, W = _layer_fwd(h, p, l, cs, sn, W, late_weights, {**S, "attn_norm_g": _behind(S["attn_norm_g"], token)}, sp)
        acts.append(A)
        smalls.append((S, raw, sp))
        weights.append(W)
    g, lsum = _loss_head(h, target)
    grads = [None] * DEPTH
    token = None
    for l in reversed(range(DEPTH)):
        S, raw, sp = smalls[l]
        g, grads[l] = _layer_bwd(g, p, l, cs, sn, weights[l], {**S, "ple_norm_g": _behind(S["ple_norm_g"], token)}, sp, acts[l], raw,
                                 functools.partial(mid_bwd, l))
        token = after_bwd(l, g, grads[l])
    return lsum[0, 0], g, grads


def _layer_weights(g):
    layout = dict(
        w_in=_from_col_major, glu_w1=lambda a: a.reshape(C_WIDTH, C_WIDTH), glu_w2=lambda a: a.reshape(C_WIDTH, C_WIDTH),
        w_out=lambda a: a.reshape(D_MODEL, D_MODEL), w_ff1=lambda a: a, w_ff2=lambda a: a.reshape(D_FF, D_MODEL),
        w_ple_gate=lambda a: a.reshape(D_MODEL, D_MODEL), w_ple_proj=_from_col_major)
    return {n: layout[n](a) for n, a in g.items()}


def kernel(x, p, positions, attn_norm_g, w_in, gmlp_ln_g, gmlp_ln_b, gmlp_ws, gmlp_bs, q_norm_g, k_norm_g, sinks, ssm_a_re, ssm_a_im, ssm_log_dt, ssm_b_re, ssm_b_im, ssm_c_re, ssm_c_im, ssm_d, glu_w1, glu_w2, mix_out_g, w_out, mlp_norm_g, w_ff1, w_ff2, ple_norm_g, w_ple_gate, w_ple_proj, loss_target, m_attn_norm_g, m_w_in, m_gmlp_ln_g, m_gmlp_ln_b, m_gmlp_ws, m_gmlp_bs, m_q_norm_g, m_k_norm_g, m_sinks, m_ssm_a_re, m_ssm_a_im, m_ssm_log_dt, m_ssm_b_re, m_ssm_b_im, m_ssm_c_re, m_ssm_c_im, m_ssm_d, m_glu_w1, m_glu_w2, m_mix_out_g, m_w_out, m_mlp_norm_g, m_w_ff1, m_w_ff2, m_ple_norm_g, m_w_ple_gate, m_w_ple_proj, v_attn_norm_g, v_w_in, v_gmlp_ln_g, v_gmlp_ln_b, v_gmlp_ws, v_gmlp_bs, v_q_norm_g, v_k_norm_g, v_sinks, v_ssm_a_re, v_ssm_a_im, v_ssm_log_dt, v_ssm_b_re, v_ssm_b_im, v_ssm_c_re, v_ssm_c_im, v_ssm_d, v_glu_w1, v_glu_w2, v_mix_out_g, v_w_out, v_mlp_norm_g, v_w_ff1, v_w_ff2, v_ple_norm_g, v_w_ple_gate, v_w_ple_proj):
    env = dict(locals())
    P = {n: env[n] for n in WEIGHTS}
    M = {n: env["m_" + n] for n in WEIGHTS}
    V = {n: env["v_" + n] for n in WEIGHTS}
    return _step(x, p, positions, loss_target, P, M, V)


def _step(x, p, positions, loss_target, P, M, V):
    small_shapes = [P[n].shape for n in SMALL]
    me = 4 * lax.axis_index("x") + 2 * lax.axis_index("y") + lax.axis_index("c")
    nothing = jnp.zeros((8, LANES), F32)

    def put(land, own, lead):
        return lax.dynamic_update_slice(land, own.reshape((1,) * len(lead) + own.shape), tuple(lead) + (0,) * own.ndim)

    def gather_start(l, names, after, tag):
        shards = [P[n][l].astype(WIRE) for n in names]
        lands = [lax.empty((N_DEV,) + s.shape, WIRE) for s in shards]
        send, recv, shards, lands, token = _split_start(shards, lands, ["all"] * len(names), 0, after, f"gather_start_{l}{tag}")
        return dict(names=names, send=send, recv=recv, shards=shards, lands=lands, token=token, name=f"gather_wait_{l}{tag}")

    def gather_wait(f, after):
        shards, lands = _split_wait(f["send"], f["recv"], f["shards"], f["lands"], ["all"] * len(f["names"]), 0, after, f["name"])
        return dict(zip(f["names"], [put(ld, sh, (me,)) for sh, ld in zip(shards, lands)]))

    first = gather_start(0, EARLY, nothing, "a")
    flying = {0: (first, gather_start(0, LATE, first["token"], "b"))}

    def weights_of(l, h):
        fa, fb = flying.pop(l)
        got = gather_wait(fa, h)
        if fb is None:
            token = None
            if l + 1 < DEPTH:
                flying[l + 1] = (gather_start(l + 1, SHARDED, got["w_in"], ""), None)
                token = flying[l + 1][0]["token"]
            W = _layer_weights(got)
            return {n: W[n] for n in EARLY}, (lambda after: ({n: W[n] for n in LATE}, None)), token

        def late_weights(after):
            late = gather_wait(fb, after)
            flying[l + 1] = (gather_start(l + 1, SHARDED, late["w_out"], ""), None)
            return _layer_weights(late), flying[l + 1][0]["token"]

        return _layer_weights(got), late_weights, None

    grad_lands = {n: lax.empty((N_DEV,) + P[n].shape, WIRE) for n in SHARDED}
    sent = []

    def scatter_start(l, names, parts, lands, tag):
        send, recv, parts, lands, token = _split_start(parts, lands, ["own"] * len(parts), l, nothing, f"scatter_start_{l}{tag}")
        sent.append(dict(l=l, names=names, send=send, recv=recv, parts=parts, lands=lands, name=f"scatter_wait_{l}{tag}"))
        return token

    def scatter_wait(after):
        f = sent.pop(0)
        parts, lands = _split_wait(f["send"], f["recv"], f["parts"], f["lands"], ["own"] * len(f["parts"]), f["l"], after, f["name"])
        lands = [put(ld, lax.dynamic_index_in_dim(part, me, 0, keepdims=False), (me, f["l"])) for part, ld in zip(parts, lands)]
        return dict(zip(f["names"], lands))

    def mid_bwd(l, g1, G):
        if l > 0:
            return None
        grad_lands.update(scatter_wait(g1))
        return scatter_start(0, GRADS_MID, [G[n] for n in GRADS_MID], [grad_lands[n] for n in GRADS_MID], "a")

    small_land = []

    def after_bwd(l, g, G):
        if l > 0:
            if sent:
                grad_lands.update(scatter_wait(g))
            return scatter_start(l, SHARDED, [G[n] for n in SHARDED], [grad_lands[n] for n in SHARDED], "")
        sflat = _pack([jnp.stack([grads_of[k][n] for k in range(DEPTH)]) for n in SMALL], F32)
        sparts = sflat.reshape(N_DEV, -1, FLAT_COLS)
        small_land.append(sflat.shape)
        return scatter_start(0, GRADS_END + ("small",), [G[n] for n in GRADS_END] + [sparts],
                             [grad_lands[n] for n in GRADS_END] + [lax.empty((N_DEV, 1) + sparts.shape[1:], F32)], "b")

    grads_of = {}

    def after_bwd_recording(l, g, G):
        grads_of[l] = G
        return after_bwd(l, g, G)

    lsum, gx, grads = _local_step(x[0], p[:, 0], positions[0], loss_target[0], P, weights_of, mid_bwd, after_bwd_recording)
    grad_lands.update(scatter_wait(gx))
    last = scatter_wait(gx)
    small_parts = last.pop("small")
    grad_lands.update(last)
    G, delta, new_m, new_v = {}, {}, {}, {}
    for n in SHARDED:
        shp = P[n].shape
        res = _sum_adamw(grad_lands[n].reshape(N_DEV, -1, shp[-1]), *(a.reshape(-1, shp[-1]) for a in (P[n], M[n], V[n])))
        G[n], delta[n], new_m[n], new_v[n] = (a.reshape(shp) for a in res)
    small_sum = _gather_whole(_sum_slots(small_parts[:, 0])).reshape((1,) + small_land[0])
    res = _sum_adamw(small_sum, _pack([P[n] for n in SMALL], F32), _pack([M[n] for n in SMALL], F32), _pack([V[n] for n in SMALL], F32))
    for dst, flat in zip((G, delta, new_m, new_v), res):
        dst.update(zip(SMALL, _unpack(flat, small_shapes)))
    loss = lax.psum(lsum, ("x", "y", "c"))
    return (loss, gx[None], *[G[n] for n in WEIGHTS], *[delta[n] for n in WEIGHTS], *[new_m[n] for n in WEIGHTS], *[new_v[n] for n in WEIGHTS])
```

```python
import functools
import math

import jax
import jax.numpy as jnp
from jax import lax
from jax.experimental import pallas as pl
from jax.experimental.pallas import tpu as pltpu

F32 = jnp.float32
MXU = jnp.bfloat16
WIRE = jnp.bfloat16

D_MODEL = 1024
DEPTH = 4
HEAD_DIM = 64
A_WIDTH = 256
A_HEADS = 4
CHUNK = 128
B_WIDTH = 512
WINDOW = 128
C_WIDTH = 256
C_GROUP = 16
C_GROUPS = 16
C_STATE = 64
N_STATE = C_GROUPS * C_STATE
IN_COLS = 1536
D_FF = 4096
PLE_DIM = 256
EPS = 1e-6
ROPE_THETA = 10000.0
SCALE = HEAD_DIM ** -0.5
NEG = -1e30
N_DEV = 8

ADAM_LR = 0.001
ADAM_B1 = 0.9
ADAM_B2 = 0.999
ADAM_EPS = 1e-08
ADAM_WD = 0.01
ADAM_STEP = 10

V7X_VMEM_BYTES = 64 * 2 ** 20
VMEM_LIMIT = V7X_VMEM_BYTES - 8 * 2 ** 20
LANES = 128

MESH = pl.DeviceIdType.MESH


def _cp(*sem):
    return pltpu.CompilerParams(dimension_semantics=sem, vmem_limit_bytes=VMEM_LIMIT)


def _sds(shape, dtype=F32):
    return jax.ShapeDtypeStruct(shape, dtype)


def _mm(a, b):
    return jnp.dot(a.astype(MXU), b.astype(MXU), preferred_element_type=F32)


def _mm_nt(a, b):
    return lax.dot_general(a.astype(MXU), b.astype(MXU), (((1,), (1,)), ((), ())), preferred_element_type=F32)


def _mm_tn(a, b):
    return lax.dot_general(a.astype(MXU), b.astype(MXU), (((0,), (0,)), ((), ())), preferred_element_type=F32)


def _lane(shape):
    return lax.broadcasted_iota(jnp.int32, shape, len(shape) - 1)


def _row(shape):
    return lax.broadcasted_iota(jnp.int32, shape, 0)


_GELU_C = math.sqrt(2.0 / math.pi)


def _gelu(x):
    return 0.5 * x * (1.0 + jnp.tanh(_GELU_C * (x + 0.044715 * (x * x * x))))


def _gelu_grad(x):
    t = jnp.tanh(_GELU_C * (x + 0.044715 * (x * x * x)))
    return 0.5 * (1.0 + t) + 0.5 * x * (1.0 - t * t) * (_GELU_C * (1.0 + 3.0 * 0.044715 * (x * x)))


def _sigmoid(x):
    return 1.0 / (1.0 + jnp.exp(-x))


def _rms_stat(x):
    return lax.rsqrt(jnp.mean(x * x, axis=-1, keepdims=True) + EPS)


def _rms_bwd(x, r, g, dy):
    xh = x * r
    dxh = dy * g
    dx = r * (dxh - xh * jnp.mean(dxh * xh, axis=-1, keepdims=True))
    return dx, jnp.sum(dy * xh, axis=0, keepdims=True)


def _tril(w):
    return jnp.where(_row(w.shape) >= _lane(w.shape), w, 0.0)


def _swap64(x):
    return pltpu.roll(x, HEAD_DIM, 1)


def _group_sum64(x, lo):
    s_lo = jnp.sum(jnp.where(lo, x, 0.0), axis=-1, keepdims=True)
    s_hi = jnp.sum(jnp.where(lo, 0.0, x), axis=-1, keepdims=True)
    return jnp.where(lo, s_lo, s_hi)


def _partner(x):
    n = x.shape[-1]
    first = (_lane(x.shape) % HEAD_DIM) < HEAD_DIM // 2
    return jnp.where(first, pltpu.roll(x, n - HEAD_DIM // 2, 1), pltpu.roll(x, HEAD_DIM // 2, 1))


def _rope(y, cs, sn):
    return y * cs + _partner(y) * sn


def _rope_bwd(d, cs, sn):
    return d * cs + _partner(d * sn)


def _qk_norm_rope(x, g, cs, sn):
    lo = _lane(x.shape) < HEAD_DIM
    r = lax.rsqrt(_group_sum64(x * x, lo) * (1.0 / HEAD_DIM) + EPS)
    xh = x * r
    return _rope(xh * g, cs, sn), xh, r


def _qk_norm_rope_bwd(xh, r, g, cs, sn, d):
    lo = _lane(xh.shape) < HEAD_DIM
    dy = _rope_bwd(d, cs, sn)
    dxh = dy * g
    m = _group_sum64(dxh * xh, lo) * (1.0 / HEAD_DIM)
    return r * (dxh - xh * m), jnp.sum(dy * xh, axis=0, keepdims=True)


def _gmlp_head(blk, g, b):
    hi = _lane(blk.shape) >= HEAD_DIM
    mu = jnp.sum(jnp.where(hi, blk, 0.0), axis=-1, keepdims=True) * (1.0 / HEAD_DIM)
    xc = jnp.where(hi, blk - mu, 0.0)
    rstd = lax.rsqrt(jnp.sum(xc * xc, axis=-1, keepdims=True) * (1.0 / HEAD_DIM) + EPS)
    vhat = xc * rstd
    return vhat * g + b, vhat, rstd


def _rope_tables(pos_col, inv_row):
    T = pos_col.shape[0]
    tm = min(T, 1024)

    def body(p_ref, inv_ref, cs_ref, sn_ref):
        ang = p_ref[...].astype(F32) * inv_ref[...]
        s = jnp.sin(ang)
        cs_ref[...] = jnp.cos(ang)
        sn_ref[...] = jnp.where((_lane(ang.shape) % HEAD_DIM) < HEAD_DIM // 2, -s, s)

    blk = pl.BlockSpec((tm, LANES), lambda i: (i, 0))
    return pl.pallas_call(
        body, name="rope_tables", grid=(T // tm,),
        in_specs=[pl.BlockSpec((tm, 1), lambda i: (i, 0)), pl.BlockSpec((1, LANES), lambda i: (0, 0))],
        out_specs=[blk, blk], out_shape=[_sds((T, LANES))] * 2, compiler_params=_cp("parallel"))(pos_col, inv_row)


def _inproj_fwd(h, g, w):
    T = h.shape[0]
    tm = min(T, 512)

    def body(h_ref, g_ref, w_ref, z_ref):
        x = h_ref[...]
        z_ref[...] = _mm(x * _rms_stat(x) * g_ref[...], w_ref[...])

    return pl.pallas_call(
        body, name="inproj_fwd", grid=(T // tm,),
        in_specs=[pl.BlockSpec((tm, D_MODEL), lambda i: (i, 0)), pl.BlockSpec((1, D_MODEL), lambda i: (0, 0)),
                  pl.BlockSpec((D_MODEL, IN_COLS), lambda i: (0, 0))],
        out_specs=pl.BlockSpec((tm, IN_COLS), lambda i: (i, 0)), out_shape=_sds((T, IN_COLS)),
        compiler_params=_cp("parallel"))(h, g, w)


def _inproj_bwd(gres, h, g, w, dza, dzq, dzk, dzv, dzc):
    T = h.shape[0]
    tm = min(T, 512)

    def body(gr_ref, h_ref, g_ref, w_ref, a_ref, q_ref, k_ref, v_ref, c_ref, dh_ref, xn_ref, dz_ref, dg_ref):
        @pl.when(pl.program_id(0) == 0)
        def _():
            dg_ref[...] = jnp.zeros_like(dg_ref)

        x = h_ref[...]
        r = _rms_stat(x)
        gg = g_ref[...]
        dz = jnp.concatenate([a_ref[...], q_ref[...], k_ref[...], v_ref[...], c_ref[...]], axis=1)
        dxn = _mm_nt(dz, w_ref[...])
        dx, dg = _rms_bwd(x, r, gg, dxn)
        dh_ref[...] = gr_ref[...] + dx
        dg_ref[...] += dg
        xn_ref[...] = (x * r * gg).astype(MXU)
        dz_ref[...] = dz.astype(MXU)

    def rows(w_):
        return pl.BlockSpec((tm, w_), lambda i: (i, 0))

    row = pl.BlockSpec((1, D_MODEL), lambda i: (0, 0))
    return pl.pallas_call(
        body, name="inproj_bwd", grid=(T // tm,),
        in_specs=[rows(D_MODEL), rows(D_MODEL), row, pl.BlockSpec((D_MODEL, IN_COLS), lambda i: (0, 0)),
                  rows(512), rows(512), rows(128), rows(128), rows(256)],
        out_specs=[rows(D_MODEL), rows(D_MODEL), rows(IN_COLS), row],
        out_shape=[_sds((T, D_MODEL)), _sds((T, D_MODEL), MXU), _sds((T, IN_COLS), MXU), _sds((1, D_MODEL))],
        compiler_params=_cp("arbitrary"))(gres, h, g, w, dza, dzq, dzk, dzv, dzc)


def _gmlp_fwd(z, lng, lnb, ws, bsx):
    T = z.shape[0]
    tm = min(T, 512)
    nc = tm // CHUNK

    def body(z_ref, g_ref, b_ref, w_ref, bs_ref, ya_ref):
        zg = _gelu(z_ref[...])
        lo = _lane((tm, LANES)) < HEAD_DIM
        prods = []
        for hd in range(A_HEADS):
            sl = slice(hd * LANES, (hd + 1) * LANES)
            blk = zg[:, sl]
            vn, _, _ = _gmlp_head(blk, g_ref[:, sl], b_ref[:, sl])
            wm = _tril(w_ref[hd])
            sv = jnp.concatenate([_mm(wm, vn[c * CHUNK:(c + 1) * CHUNK]) + bs_ref[hd] for c in range(nc)], axis=0)
            prods.append(blk * _swap64(sv))
        ya_ref[:, 0:LANES] = jnp.where(lo, prods[0], _swap64(prods[1]))
        ya_ref[:, LANES:2 * LANES] = jnp.where(lo, prods[2], _swap64(prods[3]))

    row = pl.BlockSpec((1, 512), lambda i: (0, 0))
    mat = pl.BlockSpec((A_HEADS, CHUNK, CHUNK), lambda i: (0, 0, 0))
    return pl.pallas_call(
        body, name="gmlp_fwd", grid=(T // tm,),
        in_specs=[pl.BlockSpec((tm, 512), lambda i: (i, 0)), row, row, mat, mat],
        out_specs=pl.BlockSpec((tm, A_WIDTH), lambda i: (i, 0)), out_shape=_sds((T, A_WIDTH)),
        compiler_params=_cp("parallel"))(z, lng, lnb, ws, bsx)


def _gmlp_bwd(z, dya, lng, lnb, ws, bsx):
    T = z.shape[0]
    tm = min(T, 512)
    nc = tm // CHUNK

    def body(z_ref, dya_ref, g_ref, b_ref, w_ref, bs_ref, dza_ref, dw_ref, dbs_ref, dg_ref, db_ref):
        @pl.when(pl.program_id(0) == 0)
        def _():
            dw_ref[...] = jnp.zeros_like(dw_ref)
            dbs_ref[...] = jnp.zeros_like(dbs_ref)
            dg_ref[...] = jnp.zeros_like(dg_ref)
            db_ref[...] = jnp.zeros_like(db_ref)

        za = z_ref[...]
        zg = _gelu(za)
        gp = _gelu_grad(za)
        lo = _lane((tm, LANES)) < HEAD_DIM
        for hd in range(A_HEADS):
            sl = slice(hd * LANES, (hd + 1) * LANES)
            blk = zg[:, sl]
            g = g_ref[:, sl]
            vn, vhat, rstd = _gmlp_head(blk, g, b_ref[:, sl])
            wm = _tril(w_ref[hd])
            pair = dya_ref[:, (hd // 2) * LANES:(hd // 2 + 1) * LANES]
            dy = jnp.where(lo, pair if hd % 2 == 0 else _swap64(pair), 0.0)
            dsv = _swap64(dy * blk)
            svs, dvns = [], []
            dw = jnp.zeros((CHUNK, CHUNK), F32)
            dbs = jnp.zeros((CHUNK, 1), F32)
            for c in range(nc):
                cs = slice(c * CHUNK, (c + 1) * CHUNK)
                svs.append(_mm(wm, vn[cs]) + bs_ref[hd])
                dw = dw + _mm_nt(dsv[cs], vn[cs])
                dbs = dbs + jnp.sum(dsv[cs], axis=-1, keepdims=True)
                dvns.append(_mm_tn(wm, dsv[cs]))
            sv = jnp.concatenate(svs, axis=0)
            dvn = jnp.concatenate(dvns, axis=0)
            dw_ref[hd] += _tril(dw)
            dbs_ref[hd] += jnp.broadcast_to(dbs, (CHUNK, CHUNK))
            dg_ref[:, sl] += jnp.sum(dvn * vhat, axis=0, keepdims=True)
            db_ref[:, sl] += jnp.sum(dvn, axis=0, keepdims=True)
            du = dy * _swap64(sv)
            dvh = dvn * g
            m1 = jnp.sum(dvh, axis=-1, keepdims=True) * (1.0 / HEAD_DIM)
            m2 = jnp.sum(dvh * vhat, axis=-1, keepdims=True) * (1.0 / HEAD_DIM)
            dv = jnp.where(lo, 0.0, rstd * (dvh - m1 - vhat * m2))
            dza_ref[:, sl] = (du + dv) * gp[:, sl]

    row = pl.BlockSpec((1, 512), lambda i: (0, 0))
    mat = pl.BlockSpec((A_HEADS, CHUNK, CHUNK), lambda i: (0, 0, 0))
    return pl.pallas_call(
        body, name="gmlp_bwd", grid=(T // tm,),
        in_specs=[pl.BlockSpec((tm, 512), lambda i: (i, 0)), pl.BlockSpec((tm, A_WIDTH), lambda i: (i, 0)), row, row, mat, mat],
        out_specs=[pl.BlockSpec((tm, 512), lambda i: (i, 0)), mat, mat, row, row],
        out_shape=[_sds((T, 512)), _sds((A_HEADS, CHUNK, CHUNK)), _sds((A_HEADS, CHUNK, CHUNK)), _sds((1, 512)), _sds((1, 512))],
        compiler_params=_cp("arbitrary"))(z, dya, lng, lnb, ws, bsx)


def _attn_specs(T, tq, tile_of):
    nb = tq // WINDOW

    def prev(i):
        return jnp.maximum(tile_of(i) * nb - 1, 0)

    row = pl.BlockSpec((1, LANES), lambda i: (0, 0))
    return [
        pl.BlockSpec((tq, B_WIDTH), lambda i: (tile_of(i), 1)),
        pl.BlockSpec((tq, LANES), lambda i: (tile_of(i), 8)),
        pl.BlockSpec((tq, LANES), lambda i: (tile_of(i), 9)),
        pl.BlockSpec((WINDOW, LANES), lambda i: (prev(i), 8)),
        pl.BlockSpec((WINDOW, LANES), lambda i: (prev(i), 9)),
        pl.BlockSpec((tq, LANES), lambda i: (tile_of(i), 0)),
        pl.BlockSpec((tq, LANES), lambda i: (tile_of(i), 0)),
        pl.BlockSpec((WINDOW, LANES), lambda i: (prev(i), 0)),
        pl.BlockSpec((WINDOW, LANES), lambda i: (prev(i), 0)),
        row, row,
        pl.BlockSpec((8, LANES), lambda i: (0, 0)),
    ]


def _attn_bias(first):
    qi = lax.broadcasted_iota(jnp.int32, (WINDOW, 2 * WINDOW), 0)
    kj = lax.broadcasted_iota(jnp.int32, (WINDOW, 2 * WINDOW), 1)
    diff = qi + WINDOW - kj
    ok = (diff >= 0) & (diff < WINDOW) & ((kj >= WINDOW) | jnp.logical_not(first))
    return jnp.where(ok, 0.0, NEG)


def _dup_heads(x, lo):
    sw = _swap64(x)
    return jnp.where(lo, x, sw), jnp.where(lo, sw, x)


HEADS_PER_KV = 4


def _stack_heads(x0, x1, lo):
    return jnp.concatenate([jnp.where(lo, x0, 0.0), jnp.where(lo, 0.0, x0), jnp.where(lo, x1, 0.0), jnp.where(lo, 0.0, x1)], axis=0)


def _unstack_heads(x4, lo):
    return (jnp.where(lo, x4[0:WINDOW], x4[WINDOW:2 * WINDOW]), jnp.where(lo, x4[2 * WINDOW:3 * WINDOW], x4[3 * WINDOW:4 * WINDOW]))


def _sink_column(sk_ref, g):
    return jnp.concatenate([jnp.broadcast_to(sk_ref[a:a + 1, 0:1], (WINDOW, 1)) for a in range(HEADS_PER_KV * g, HEADS_PER_KV * (g + 1))], axis=0)


def _attn_probs(q4, kw, bias, sink):
    s = _mm_nt(q4, kw)
    s = (s.reshape(HEADS_PER_KV, WINDOW, 2 * WINDOW) + bias[None]).reshape(HEADS_PER_KV * WINDOW, 2 * WINDOW)
    m = jnp.maximum(jnp.max(s, axis=-1, keepdims=True), sink)
    p = jnp.exp(s - m)
    es = jnp.exp(sink - m)
    inv = 1.0 / (jnp.sum(p, axis=-1, keepdims=True) + es)
    return p * inv, es * inv


def _attn_fwd(z, cs, sn, qg, kg, sinks):
    T = z.shape[0]
    tq = min(T, 512)
    nb = tq // WINDOW

    def body(q_ref, k_ref, v_ref, kp_ref, vp_ref, cs_ref, sn_ref, csp_ref, snp_ref, qg_ref, kg_ref, sk_ref, o_ref):
        i = pl.program_id(0)
        csq, snq = cs_ref[...], sn_ref[...]
        cs_all = jnp.concatenate([csp_ref[...], csq], axis=0)
        sn_all = jnp.concatenate([snp_ref[...], snq], axis=0)
        k_all = jnp.concatenate([kp_ref[...], k_ref[...]], axis=0)
        v_all = jnp.concatenate([vp_ref[...], v_ref[...]], axis=0)
        kr, _, _ = _qk_norm_rope(k_all, kg_ref[...], cs_all, sn_all)
        lo_all = _lane(k_all.shape) < HEAD_DIM
        kd = _dup_heads(kr, lo_all)
        vd = _dup_heads(v_all, lo_all)
        lo = _lane((WINDOW, LANES)) < HEAD_DIM
        qrs = [_qk_norm_rope(q_ref[:, pr * LANES:(pr + 1) * LANES], qg_ref[...], csq, snq)[0] * SCALE for pr in range(4)]
        biases = [_attn_bias(i * nb + b == 0) for b in range(nb)]
        for g in range(2):
            sink = _sink_column(sk_ref, g)
            for b in range(nb):
                bs = slice(b * WINDOW, (b + 1) * WINDOW)
                ws = slice(b * WINDOW, (b + 2) * WINDOW)
                pn, _ = _attn_probs(_stack_heads(qrs[2 * g][bs], qrs[2 * g + 1][bs], lo), kd[g][ws], biases[b], sink)
                o0, o1 = _unstack_heads(_mm(pn, vd[g][ws]), lo)
                o_ref[bs, 2 * g * LANES:(2 * g + 1) * LANES] = o0
                o_ref[bs, (2 * g + 1) * LANES:(2 * g + 2) * LANES] = o1

    return pl.pallas_call(
        body, name="attn_fwd", grid=(T // tq,),
        in_specs=_attn_specs(T, tq, lambda i: i),
        out_specs=pl.BlockSpec((tq, B_WIDTH), lambda i: (i, 0)), out_shape=_sds((T, B_WIDTH)),
        compiler_params=_cp("parallel"))(z, z, z, z, z, cs, sn, cs, sn, qg, kg, sinks)


def _attn_bwd(z, cs, sn, qg, kg, sinks, o, do):
    T = z.shape[0]
    tq = min(T, 512)
    nb = tq // WINDOW
    nt = T // tq
    tk = tq + WINDOW

    def tile_of(i):
        return nt - 1 - i

    def body(q_ref, k_ref, v_ref, kp_ref, vp_ref, cs_ref, sn_ref, csp_ref, snp_ref, qg_ref, kg_ref, sk_ref, o_ref, do_ref,
             dq_ref, dk_ref, dv_ref, dqg_ref, dkg_ref, dsk_ref, acck, accv, ck, cv):
        i = pl.program_id(0)
        ti = nt - 1 - i

        @pl.when(i == 0)
        def _():
            dqg_ref[...] = jnp.zeros_like(dqg_ref)
            dkg_ref[...] = jnp.zeros_like(dkg_ref)
            dsk_ref[...] = jnp.zeros_like(dsk_ref)
            ck[...] = jnp.zeros_like(ck)
            cv[...] = jnp.zeros_like(cv)

        csq, snq = cs_ref[...], sn_ref[...]
        cs_all = jnp.concatenate([csp_ref[...], csq], axis=0)
        sn_all = jnp.concatenate([snp_ref[...], snq], axis=0)
        k_all = jnp.concatenate([kp_ref[...], k_ref[...]], axis=0)
        v_all = jnp.concatenate([vp_ref[...], v_ref[...]], axis=0)
        kr, kh, rk = _qk_norm_rope(k_all, kg_ref[...], cs_all, sn_all)
        lo_all = _lane(k_all.shape) < HEAD_DIM
        kd = _dup_heads(kr, lo_all)
        vd = _dup_heads(v_all, lo_all)
        lo = _lane((WINDOW, LANES)) < HEAD_DIM
        acck[...] = jnp.zeros_like(acck)
        accv[...] = jnp.zeros_like(accv)
        prep = [_qk_norm_rope(q_ref[:, pr * LANES:(pr + 1) * LANES], qg_ref[...], csq, snq) for pr in range(4)]
        biases = [_attn_bias(ti * nb + b == 0) for b in range(nb)]
        dqs = [[None] * nb for _ in range(4)]
        for g in range(2):
            sink = _sink_column(sk_ref, g)
            dsink = jnp.zeros((HEADS_PER_KV * WINDOW, 1), F32)
            for b in range(nb):
                bs = slice(b * WINDOW, (b + 1) * WINDOW)
                ws = slice(b * WINDOW, (b + 2) * WINDOW)
                kw, vw = kd[g][ws], vd[g][ws]
                q4 = _stack_heads(prep[2 * g][0][bs] * SCALE, prep[2 * g + 1][0][bs] * SCALE, lo)
                pn, psink = _attn_probs(q4, kw, biases[b], sink)
                o0, o1 = o_ref[bs, 2 * g * LANES:(2 * g + 1) * LANES], o_ref[bs, (2 * g + 1) * LANES:(2 * g + 2) * LANES]
                do4 = _stack_heads(do_ref[bs, 2 * g * LANES:(2 * g + 1) * LANES], do_ref[bs, (2 * g + 1) * LANES:(2 * g + 2) * LANES], lo)
                delta = jnp.sum(do4 * jnp.concatenate([o0, o0, o1, o1], axis=0), axis=-1, keepdims=True)
                ds = pn * (_mm_nt(do4, vw) - delta)
                dsink = dsink - psink * delta
                dqs[2 * g][b], dqs[2 * g + 1][b] = _unstack_heads(_mm(ds, kw) * SCALE, lo)
                acck[g, ws, :] += _mm_tn(ds, q4)
                accv[g, ws, :] += _mm_tn(pn, do4)
            for hh in range(HEADS_PER_KV):
                a = HEADS_PER_KV * g + hh
                dsk_ref[a:a + 1, :] += jnp.zeros((1, LANES), F32) + jnp.sum(dsink[hh * WINDOW:(hh + 1) * WINDOW])
        for pr in range(4):
            _, qh, rq = prep[pr]
            dx, dg = _qk_norm_rope_bwd(qh, rq, qg_ref[...], csq, snq, jnp.concatenate(dqs[pr], axis=0))
            dq_ref[:, pr * LANES:(pr + 1) * LANES] = dx
            dqg_ref[...] += dg

        def fold(acc):
            f0 = acc[0] + _swap64(acc[0])
            f1 = acc[1] + _swap64(acc[1])
            return jnp.where(lo_all, f0, f1)

        dk_all = fold(acck)
        dv_all = fold(accv)
        pad = jnp.zeros((tq - WINDOW, LANES), F32)
        dk_own = dk_all[WINDOW:] + (jnp.concatenate([pad, ck[...]], axis=0) if nb > 1 else ck[...])
        dv_own = dv_all[WINDOW:] + (jnp.concatenate([pad, cv[...]], axis=0) if nb > 1 else cv[...])
        ck[...] = dk_all[:WINDOW]
        cv[...] = dv_all[:WINDOW]
        dxk, dgk = _qk_norm_rope_bwd(kh[WINDOW:], rk[WINDOW:], kg_ref[...], csq, snq, dk_own)
        dk_ref[...] = dxk
        dkg_ref[...] += dgk
        dv_ref[...] = dv_own

    row = pl.BlockSpec((1, LANES), lambda i: (0, 0))
    return pl.pallas_call(
        body, name="attn_bwd", grid=(nt,),
        in_specs=_attn_specs(T, tq, tile_of) + [pl.BlockSpec((tq, B_WIDTH), lambda i: (tile_of(i), 0))] * 2,
        out_specs=[pl.BlockSpec((tq, B_WIDTH), lambda i: (tile_of(i), 0)), pl.BlockSpec((tq, LANES), lambda i: (tile_of(i), 0)),
                   pl.BlockSpec((tq, LANES), lambda i: (tile_of(i), 0)), row, row, pl.BlockSpec((8, LANES), lambda i: (0, 0))],
        out_shape=[_sds((T, B_WIDTH)), _sds((T, LANES)), _sds((T, LANES)), _sds((1, LANES)), _sds((1, LANES)), _sds((8, LANES))],
        scratch_shapes=[pltpu.VMEM((2, tk, LANES), F32), pltpu.VMEM((2, tk, LANES), F32),
                        pltpu.VMEM((WINDOW, LANES), F32), pltpu.VMEM((WINDOW, LANES), F32)],
        compiler_params=_cp("arbitrary"))(z, z, z, z, z, cs, sn, cs, sn, qg, kg, sinks, o, do)


def _bbar_t(are, aim, ldt, btr, bti):
    lbr, lbi = _lam_bar(are, aim, ldt)
    den = are * are + aim * aim
    nr = lbr - 1.0
    cr = (nr * are + lbi * aim) / den
    ci = (lbi * are - nr * aim) / den
    return cr * btr - ci * bti, cr * bti + ci * btr


def _lam_bar(are, aim, ldt):
    dt = jnp.exp(ldt)
    er = jnp.exp(are * dt)
    return er * jnp.cos(aim * dt), er * jnp.sin(aim * dt)


def _block_diag(x):
    t = jnp.concatenate([x] * C_GROUPS, axis=1)
    return jnp.where(_row(t.shape) // C_GROUP == _lane(t.shape) // C_STATE, t, 0.0)


def _block_diag_fold(m):
    rg = _row((C_WIDTH, C_STATE)) // C_GROUP
    acc = jnp.zeros((C_WIDTH, C_STATE), F32)
    for g in range(C_GROUPS):
        acc = acc + jnp.where(rg == g, m[:, g * C_STATE:(g + 1) * C_STATE], 0.0)
    return acc


def _ssm_prep(are, aim, ldt, are_x, aim_x, ldt_x, btr, bti, cre, cim):
    def body(are_r, aim_r, ldt_r, arex_r, aimx_r, ldtx_r, btr_r, bti_r, cre_r, cim_r, bbd_ref, cbd_ref, pwr_ref, pwi_ref):
        lr, li = _lam_bar(are_r[...], aim_r[...], ldt_r[...])
        cr, ci = lr, li
        for r in range(SCAN_SEG):
            pwr_ref[r:r + 1, :] = cr
            pwi_ref[r:r + 1, :] = ci
            cr, ci = cr * lr - ci * li, cr * li + ci * lr
        br, bi = _bbar_t(arex_r[...], aimx_r[...], ldtx_r[...], btr_r[...], bti_r[...])
        bbd_ref[...] = jnp.concatenate([_block_diag(br), _block_diag(bi)], axis=1).astype(MXU)
        cbd_ref[...] = jnp.concatenate([_block_diag(cre_r[...]), -_block_diag(cim_r[...])], axis=1).astype(MXU)

    return pl.pallas_call(
        body, name="ssm_prep",
        out_shape=[_sds((C_WIDTH, 2 * N_STATE), MXU), _sds((C_WIDTH, 2 * N_STATE), MXU), _sds((SCAN_SEG, N_STATE)), _sds((SCAN_SEG, N_STATE))],
        compiler_params=pltpu.CompilerParams(vmem_limit_bytes=VMEM_LIMIT))(are, aim, ldt, are_x, aim_x, ldt_x, btr, bti, cre, cim)


def _ssm_param_bwd(are, aim, ldt, are_x, aim_x, ldt_x, btr, bti, dlr, dli, dbbd, dcr, dci):
    def body(are_r, aim_r, ldt_r, arex_r, aimx_r, ldtx_r, btr_r, bti_r, dlr_r, dli_r, dbbd_r, dcr_r, dci_r,
             dare_ref, daim_ref, dldt_ref, dbtr_ref, dbti_ref, dcre_ref, dcim_ref):
        _, vjp_l = jax.vjp(_lam_bar, are_r[...], aim_r[...], ldt_r[...])
        da1, di1, dl1 = vjp_l((dlr_r[...], dli_r[...]))
        dbr = _block_diag_fold(dbbd_r[:, 0:N_STATE])
        dbi = _block_diag_fold(dbbd_r[:, N_STATE:2 * N_STATE])
        _, vjp_b = jax.vjp(_bbar_t, arex_r[...], aimx_r[...], ldtx_r[...], btr_r[...], bti_r[...])
        da2, di2, dl2, dbtr, dbti = vjp_b((dbr, dbi))

        def gsum(x):
            return x.reshape(C_GROUPS, C_GROUP, C_STATE).sum(axis=1)

        dare_ref[...] = da1 + gsum(da2)
        daim_ref[...] = di1 + gsum(di2)
        dldt_ref[...] = jnp.broadcast_to(jnp.sum(dl1 + gsum(dl2), axis=-1, keepdims=True), (C_GROUPS, LANES))
        dbtr_ref[...] = dbtr
        dbti_ref[...] = dbti
        dcre_ref[...] = _block_diag_fold(dcr_r[...])
        dcim_ref[...] = -_block_diag_fold(dci_r[...])

    g = _sds((C_GROUPS, C_STATE))
    x = _sds((C_WIDTH, C_STATE))
    return pl.pallas_call(
        body, name="ssm_param_bwd", out_shape=[g, g, _sds((C_GROUPS, LANES)), x, x, x, x],
        compiler_params=pltpu.CompilerParams(vmem_limit_bytes=VMEM_LIMIT))(are, aim, ldt, are_x, aim_x, ldt_x, btr, bti, dlr, dli, dbbd, dcr, dci)


SCAN_TILE = 256
SCAN_SEG = 8


def _scan_tables(pwr_ref, pwi_ref, conj, reverse):
    row = _row((SCAN_SEG, N_STATE))
    shifts = []
    for k in (1, 2, 4):
        keep = (row < SCAN_SEG - k) if reverse else (row >= k)
        ar = jnp.broadcast_to(pwr_ref[k - 1:k, :], (SCAN_SEG, N_STATE))
        ai = jnp.broadcast_to(pwi_ref[k - 1:k, :], (SCAN_SEG, N_STATE)) * conj
        shifts.append((SCAN_SEG - k if reverse else k, jnp.where(keep, ar, 0.0), jnp.where(keep, ai, 0.0)))
    if reverse:
        pr = jnp.concatenate([pwr_ref[SCAN_SEG - 1 - r:SCAN_SEG - r, :] for r in range(SCAN_SEG)], axis=0)
        pi = jnp.concatenate([pwi_ref[SCAN_SEG - 1 - r:SCAN_SEG - r, :] for r in range(SCAN_SEG)], axis=0) * conj
    else:
        pr, pi = pwr_ref[...], pwi_ref[...] * conj
    return shifts, (pr, pi)


def _tile_scan(xr_ref, xi_ref, pwr_ref, pwi_ref, sr, si, conj, reverse):
    shifts, (pr, pi) = _scan_tables(pwr_ref, pwi_ref, conj, reverse)
    groups = xr_ref.shape[0] // SCAN_SEG
    out_row = 0 if reverse else SCAN_SEG - 1

    def step(k, c):
        cr, ci = c
        g = groups - 1 - k if reverse else k
        rows = pl.ds(pl.multiple_of(g * SCAN_SEG, SCAN_SEG), SCAN_SEG)
        xr, xi = xr_ref[rows, :], xi_ref[rows, :]
        for amount, ar, ai in shifts:
            qr, qi = pltpu.roll(xr, amount, 0), pltpu.roll(xi, amount, 0)
            xr, xi = xr + ar * qr - ai * qi, xi + ar * qi + ai * qr
        xr, xi = xr + pr * cr - pi * ci, xi + pr * ci + pi * cr
        xr_ref[rows, :] = xr
        xi_ref[rows, :] = xi
        return xr[out_row:out_row + 1], xi[out_row:out_row + 1]

    cr, ci = lax.fori_loop(0, groups, step, (sr[...], si[...]), unroll=2)
    sr[...] = cr
    si[...] = ci


def _ssm_fwd(z, bbd, pwr, pwi, cbd, dsk, w1, w2):
    T = z.shape[0]
    tt = min(T, SCAN_TILE)

    def body(u_ref, bbd_ref, pwr_ref, pwi_ref, cbd_ref, d_ref, w1_ref, w2_ref, yc_ref, xr_ref, xi_ref, sr, si):
        @pl.when(pl.program_id(0) == 0)
        def _():
            sr[...] = jnp.zeros_like(sr)
            si[...] = jnp.zeros_like(si)

        u = u_ref[...]
        bu = _mm(u, bbd_ref[...])
        xr_ref[...] = bu[:, 0:N_STATE]
        xi_ref[...] = bu[:, N_STATE:2 * N_STATE]
        _tile_scan(xr_ref, xi_ref, pwr_ref, pwi_ref, sr, si, 1.0, False)
        x = jnp.concatenate([xr_ref[...], xi_ref[...]], axis=1)
        y2 = _gelu(_mm_nt(x, cbd_ref[...]) + d_ref[...] * u)
        yc_ref[...] = _mm(y2, w1_ref[...]) * _sigmoid(_mm(y2, w2_ref[...]))

    big = pl.BlockSpec((C_WIDTH, 2 * N_STATE), lambda i: (0, 0))
    tab = pl.BlockSpec((SCAN_SEG, N_STATE), lambda i: (0, 0))
    wsp = pl.BlockSpec((C_WIDTH, C_WIDTH), lambda i: (0, 0))
    xs = pl.BlockSpec((tt, N_STATE), lambda i: (i, 0))
    return pl.pallas_call(
        body, name="ssm_fwd", grid=(T // tt,),
        in_specs=[pl.BlockSpec((tt, C_WIDTH), lambda i: (i, 5)), big, tab, tab, big, pl.BlockSpec((1, C_WIDTH), lambda i: (0, 0)), wsp, wsp],
        out_specs=[pl.BlockSpec((tt, C_WIDTH), lambda i: (i, 0)), xs, xs],
        out_shape=[_sds((T, C_WIDTH)), _sds((T, N_STATE)), _sds((T, N_STATE))],
        scratch_shapes=[pltpu.VMEM((1, N_STATE), F32)] * 2,
        compiler_params=_cp("arbitrary"))(z, bbd, pwr, pwi, cbd, dsk, w1, w2)


def _ssm_bwd(dyc, z, xr, xi, bbd, pwr, pwi, cbd, dsk, w1, w2):
    T = z.shape[0]
    tt = min(T, SCAN_TILE)
    nt = T // tt

    def tile_of(i):
        return nt - 1 - i

    def body(dyc_ref, u_ref, xr_ref, xi_ref, xpr_ref, xpi_ref, bbd_ref, pwr_ref, pwi_ref, cbd_ref, d_ref, w1_ref, w2_ref,
             du_ref, y2_ref, da1_ref, da2_ref, dy_ref, arai_ref, dlr_ref, dli_ref, dd_ref, gr, gi, sr, si):
        i = pl.program_id(0)
        ti = nt - 1 - i

        @pl.when(i == 0)
        def _():
            sr[...] = jnp.zeros_like(sr)
            si[...] = jnp.zeros_like(si)
            dlr_ref[...] = jnp.zeros_like(dlr_ref)
            dli_ref[...] = jnp.zeros_like(dli_ref)
            dd_ref[...] = jnp.zeros_like(dd_ref)

        u = u_ref[...]
        xr_t, xi_t = xr_ref[...], xi_ref[...]
        y = _mm_nt(jnp.concatenate([xr_t, xi_t], axis=1), cbd_ref[...]) + d_ref[...] * u
        y2 = _gelu(y)
        a1 = _mm(y2, w1_ref[...])
        sg = _sigmoid(_mm(y2, w2_ref[...]))
        dyc_t = dyc_ref[...]
        da1 = dyc_t * sg
        da2 = dyc_t * a1 * sg * (1.0 - sg)
        dy = (_mm_nt(da1, w1_ref[...]) + _mm_nt(da2, w2_ref[...])) * _gelu_grad(y)
        gx = _mm(dy, cbd_ref[...])
        gr[...] = gx[:, 0:N_STATE]
        gi[...] = gx[:, N_STATE:2 * N_STATE]
        _tile_scan(gr, gi, pwr_ref, pwi_ref, sr, si, -1.0, True)
        ar, ai = gr[...], gi[...]
        first_row = _row(ar.shape) == 0
        live = jnp.where(ti > 0, 1.0, 0.0)
        xsr = jnp.where(first_row, xpr_ref[7:8, :] * live, pltpu.roll(xr_t, 1, 0))
        xsi = jnp.where(first_row, xpi_ref[7:8, :] * live, pltpu.roll(xi_t, 1, 0))
        dlr_ref[...] += jnp.sum(ar * xsr + ai * xsi, axis=0, keepdims=True)
        dli_ref[...] += jnp.sum(ai * xsr - ar * xsi, axis=0, keepdims=True)
        dd_ref[...] += jnp.sum(dy * u, axis=0, keepdims=True)
        arai = jnp.concatenate([ar, ai], axis=1)
        du_ref[...] = _mm_nt(arai, bbd_ref[...]) + d_ref[...] * dy
        y2_ref[...] = y2.astype(MXU)
        da1_ref[...] = da1.astype(MXU)
        da2_ref[...] = da2.astype(MXU)
        dy_ref[...] = dy.astype(MXU)
        arai_ref[...] = arai.astype(MXU)

    def prev(i):
        return jnp.maximum(tile_of(i) * (tt // 8) - 1, 0)

    big = pl.BlockSpec((C_WIDTH, 2 * N_STATE), lambda i: (0, 0))
    tab = pl.BlockSpec((SCAN_SEG, N_STATE), lambda i: (0, 0))
    srow = pl.BlockSpec((1, N_STATE), lambda i: (0, 0))
    wsp = pl.BlockSpec((C_WIDTH, C_WIDTH), lambda i: (0, 0))
    xs = pl.BlockSpec((tt, N_STATE), lambda i: (tile_of(i), 0))
    xp = pl.BlockSpec((8, N_STATE), lambda i: (prev(i), 0))
    cw = pl.BlockSpec((tt, C_WIDTH), lambda i: (tile_of(i), 0))
    drow = pl.BlockSpec((1, C_WIDTH), lambda i: (0, 0))
    return pl.pallas_call(
        body, name="ssm_bwd", grid=(nt,),
        in_specs=[cw, pl.BlockSpec((tt, C_WIDTH), lambda i: (tile_of(i), 5)), xs, xs, xp, xp, big, tab, tab, big, drow, wsp, wsp],
        out_specs=[cw, cw, cw, cw, cw, pl.BlockSpec((tt, 2 * N_STATE), lambda i: (tile_of(i), 0)), srow, srow, drow],
        out_shape=[_sds((T, C_WIDTH))] + [_sds((T, C_WIDTH), MXU)] * 4 + [_sds((T, 2 * N_STATE), MXU), _sds((1, N_STATE)), _sds((1, N_STATE)), _sds((1, C_WIDTH))],
        scratch_shapes=[pltpu.VMEM((tt, N_STATE), F32)] * 2 + [pltpu.VMEM((1, N_STATE), F32)] * 2,
        compiler_params=_cp("arbitrary"))(dyc, z, xr, xi, xr, xi, bbd, pwr, pwi, cbd, dsk, w1, w2)


_GROUPS = ((0, A_WIDTH), (A_WIDTH, A_WIDTH + B_WIDTH), (A_WIDTH + B_WIDTH, D_MODEL))


def _merge_fwd(h, ya, yb, yc, g, w):
    T = h.shape[0]
    tm = min(T, 512)

    def body(h_ref, a_ref, b_ref, c_ref, g_ref, w_ref, o_ref):
        yn = jnp.concatenate([y * _rms_stat(y) for y in (a_ref[...], b_ref[...], c_ref[...])], axis=1) * g_ref[...]
        o_ref[...] = h_ref[...] + _mm(yn, w_ref[...])

    def rows(w_):
        return pl.BlockSpec((tm, w_), lambda i: (i, 0))

    return pl.pallas_call(
        body, name="merge_fwd", grid=(T // tm,),
        in_specs=[rows(D_MODEL), rows(A_WIDTH), rows(B_WIDTH), rows(C_WIDTH), pl.BlockSpec((1, D_MODEL), lambda i: (0, 0)),
                  pl.BlockSpec((D_MODEL, D_MODEL), lambda i: (0, 0))],
        out_specs=rows(D_MODEL), out_shape=_sds((T, D_MODEL)), compiler_params=_cp("parallel"))(h, ya, yb, yc, g, w)


def _merge_bwd(gres, ya, yb, yc, g, w):
    T = gres.shape[0]
    tm = min(T, 512)

    def body(gr_ref, a_ref, b_ref, c_ref, g_ref, w_ref, da_ref, db_ref, dc_ref, yn_ref, dg_ref):
        @pl.when(pl.program_id(0) == 0)
        def _():
            dg_ref[...] = jnp.zeros_like(dg_ref)

        dyn = _mm_nt(gr_ref[...], w_ref[...])
        yns, dgs = [], []
        for (c0, c1), y_ref, d_ref in zip(_GROUPS, (a_ref, b_ref, c_ref), (da_ref, db_ref, dc_ref)):
            y = y_ref[...]
            r = _rms_stat(y)
            gg = g_ref[:, c0:c1]
            dx, dg = _rms_bwd(y, r, gg, dyn[:, c0:c1])
            d_ref[...] = dx
            dgs.append(dg)
            yns.append(y * r * gg)
        dg_ref[...] += jnp.concatenate(dgs, axis=1)
        yn_ref[...] = jnp.concatenate(yns, axis=1).astype(MXU)

    def rows(w_):
        return pl.BlockSpec((tm, w_), lambda i: (i, 0))

    row = pl.BlockSpec((1, D_MODEL), lambda i: (0, 0))
    return pl.pallas_call(
        body, name="merge_bwd", grid=(T // tm,),
        in_specs=[rows(D_MODEL), rows(A_WIDTH), rows(B_WIDTH), rows(C_WIDTH), row, pl.BlockSpec((D_MODEL, D_MODEL), lambda i: (0, 0))],
        out_specs=[rows(A_WIDTH), rows(B_WIDTH), rows(C_WIDTH), rows(D_MODEL), row],
        out_shape=[_sds((T, A_WIDTH)), _sds((T, B_WIDTH)), _sds((T, C_WIDTH)), _sds((T, D_MODEL), MXU), _sds((1, D_MODEL))],
        compiler_params=_cp("arbitrary"))(gres, ya, yb, yc, g, w)


FF_BLOCK = D_FF // N_DEV


def _mlp_fwd(h, g, w1, w2):
    T = h.shape[0]
    tm = min(T, 1024)

    def body(h_ref, g_ref, w1_ref, w2_ref, o_ref, hn):
        @pl.when(pl.program_id(1) == 0)
        def _():
            x = h_ref[...]
            hn[...] = (x * _rms_stat(x) * g_ref[...]).astype(MXU)
            o_ref[...] = x

        a = jnp.maximum(_mm(hn[...], w1_ref[...]), 0.0)
        o_ref[...] += _mm(a * a, w2_ref[...])

    return pl.pallas_call(
        body, name="mlp_fwd", grid=(T // tm, N_DEV),
        in_specs=[pl.BlockSpec((tm, D_MODEL), lambda i, j: (i, 0)), pl.BlockSpec((1, D_MODEL), lambda i, j: (0, 0)),
                  pl.BlockSpec((None, D_MODEL, FF_BLOCK), lambda i, j: (j, 0, 0)), pl.BlockSpec((FF_BLOCK, D_MODEL), lambda i, j: (j, 0))],
        out_specs=pl.BlockSpec((tm, D_MODEL), lambda i, j: (i, 0)), out_shape=_sds((T, D_MODEL)),
        scratch_shapes=[pltpu.VMEM((tm, D_MODEL), MXU)],
        compiler_params=_cp("parallel", "arbitrary"))(h, g, w1, w2)


def _mlp_bwd(gres, h, g, w1, w2):
    T = h.shape[0]
    tm = min(T, 1024)

    def body(gr_ref, h_ref, g_ref, w1_ref, w2_ref, dh_ref, hn_ref, da_ref, r_ref, dg_ref, acc):
        i, j = pl.program_id(0), pl.program_id(1)

        @pl.when((i == 0) & (j == 0))
        def _():
            dg_ref[...] = jnp.zeros_like(dg_ref)

        @pl.when(j == 0)
        def _():
            x = h_ref[...]
            hn_ref[...] = (x * _rms_stat(x) * g_ref[...]).astype(MXU)
            acc[...] = jnp.zeros_like(acc)

        a = jnp.maximum(_mm(hn_ref[...], w1_ref[...]), 0.0)
        da = _mm_nt(gr_ref[...], w2_ref[...]) * (2.0 * a)
        acc[...] += _mm_nt(da, w1_ref[...])
        da_ref[...] = da.astype(MXU)
        r_ref[...] = (a * a).astype(MXU)

        @pl.when(j == N_DEV - 1)
        def _():
            x = h_ref[...]
            dx, dg = _rms_bwd(x, _rms_stat(x), g_ref[...], acc[...])
            dh_ref[...] = gr_ref[...] + dx
            dg_ref[...] += dg

    rows = pl.BlockSpec((tm, D_MODEL), lambda i, j: (i, 0))
    row = pl.BlockSpec((1, D_MODEL), lambda i, j: (0, 0))
    ffb = pl.BlockSpec((tm, FF_BLOCK), lambda i, j: (i, j))
    return pl.pallas_call(
        body, name="mlp_bwd", grid=(T // tm, N_DEV),
        in_specs=[rows, rows, row, pl.BlockSpec((None, D_MODEL, FF_BLOCK), lambda i, j: (j, 0, 0)),
                  pl.BlockSpec((FF_BLOCK, D_MODEL), lambda i, j: (j, 0))],
        out_specs=[rows, rows, ffb, ffb, row],
        out_shape=[_sds((T, D_MODEL)), _sds((T, D_MODEL), MXU), _sds((T, D_FF), MXU), _sds((T, D_FF), MXU), _sds((1, D_MODEL))],
        scratch_shapes=[pltpu.VMEM((tm, D_MODEL), F32)],
        compiler_params=_cp("arbitrary", "arbitrary"))(gres, h, g, w1, w2)


def _ple_fwd(h, p, l, g, wg, wp):
    T = h.shape[0]
    tm = min(T, 512)

    def body(h_ref, p_ref, g_ref, wg_ref, wp_ref, o_ref):
        x = h_ref[...]
        gate = _sigmoid(_mm(x * _rms_stat(x) * g_ref[...], wg_ref[...]))
        o_ref[...] = x + gate * _mm(p_ref[...], wp_ref[...])

    rows = pl.BlockSpec((tm, D_MODEL), lambda i: (i, 0))
    return pl.pallas_call(
        body, name="ple_fwd", grid=(T // tm,),
        in_specs=[rows, pl.BlockSpec((None, tm, PLE_DIM), lambda i: (l, i, 0)), pl.BlockSpec((1, D_MODEL), lambda i: (0, 0)),
                  pl.BlockSpec((D_MODEL, D_MODEL), lambda i: (0, 0)), pl.BlockSpec((PLE_DIM, D_MODEL), lambda i: (0, 0))],
        out_specs=rows, out_shape=_sds((T, D_MODEL)), compiler_params=_cp("parallel"))(h, p, g, wg, wp)


def _ple_bwd(gres, h, p, l, g, wg, wp):
    T = h.shape[0]
    tm = min(T, 512)

    def body(gr_ref, h_ref, p_ref, g_ref, wg_ref, wp_ref, dh_ref, hn_ref, dgp_ref, de_ref, dg_ref):
        @pl.when(pl.program_id(0) == 0)
        def _():
            dg_ref[...] = jnp.zeros_like(dg_ref)

        x = h_ref[...]
        r = _rms_stat(x)
        gg = g_ref[...]
        hn = x * r * gg
        gate = _sigmoid(_mm(hn, wg_ref[...]))
        e = _mm(p_ref[...], wp_ref[...])
        gr = gr_ref[...]
        dgp = gr * e * gate * (1.0 - gate)
        dx, dg = _rms_bwd(x, r, gg, _mm_nt(dgp, wg_ref[...]))
        dh_ref[...] = gr + dx
        dg_ref[...] += dg
        hn_ref[...] = hn.astype(MXU)
        dgp_ref[...] = dgp.astype(MXU)
        de_ref[...] = (gr * gate).astype(MXU)

    rows = pl.BlockSpec((tm, D_MODEL), lambda i: (i, 0))
    row = pl.BlockSpec((1, D_MODEL), lambda i: (0, 0))
    return pl.pallas_call(
        body, name="ple_bwd", grid=(T // tm,),
        in_specs=[rows, rows, pl.BlockSpec((None, tm, PLE_DIM), lambda i: (l, i, 0)), row,
                  pl.BlockSpec((D_MODEL, D_MODEL), lambda i: (0, 0)), pl.BlockSpec((PLE_DIM, D_MODEL), lambda i: (0, 0))],
        out_specs=[rows, rows, rows, rows, row],
        out_shape=[_sds((T, D_MODEL))] + [_sds((T, D_MODEL), MXU)] * 3 + [_sds((1, D_MODEL))],
        compiler_params=_cp("arbitrary"))(gres, h, p, g, wg, wp)


def _loss_head(h, target):
    T = h.shape[0]
    tm = min(T, 1024)

    def body(h_ref, t_ref, dh_ref, l_ref):
        @pl.when(pl.program_id(0) == 0)
        def _():
            l_ref[...] = jnp.zeros_like(l_ref)

        e = h_ref[...] - t_ref[...]
        dh_ref[...] = e * (1.0 / D_MODEL)
        l_ref[...] += jnp.zeros_like(l_ref) + 0.5 * jnp.sum(jnp.mean(e * e, axis=-1, keepdims=True))

    rows = pl.BlockSpec((tm, D_MODEL), lambda i: (i, 0))
    return pl.pallas_call(
        body, name="loss_head", grid=(T // tm,), in_specs=[rows, rows],
        out_specs=[rows, pl.BlockSpec((8, LANES), lambda i: (0, 0))], out_shape=[_sds((T, D_MODEL)), _sds((8, LANES))],
        compiler_params=_cp("arbitrary"))(h, target)


TN_ROWS = 2048

def _tn(a, b, m, n, *, bm, bn, a_off=0, b_off=0, a_lead=None, b_blocked=False, n_major=False, split=None, dtype=F32, name="tn"):
    T = a.shape[-2]
    tk = min(T, TN_ROWS)
    nk = T // tk
    if b_blocked:
        b_spec = pl.BlockSpec((None, tk, bn), lambda i, j, k: (j, k, 0))
    else:
        b_spec = pl.BlockSpec((tk, bn), lambda i, j, k: (k, b_off + j))
    if a_lead is None:
        a_spec = pl.BlockSpec((tk, bm), lambda i, j, k: (k, a_off + i))
    else:
        a_spec = pl.BlockSpec((None, tk, bm), lambda i, j, k: (a_lead, k, a_off + i))
    assert m % bm == 0 and n % bn == 0 and (not n_major or bm == m) and (split is None or (bn == n and n % split == 0))

    def body(a_ref, b_ref, o_ref, acc):
        k = pl.program_id(2)

        @pl.when(k == 0)
        def _():
            acc[...] = jnp.zeros_like(acc)

        acc[...] += _mm_tn(a_ref[...], b_ref[...])

        @pl.when(k == nk - 1)
        def _():
            if split is None:
                o_ref[...] = acc[...].astype(dtype)
            else:
                for d in range(n // split):
                    o_ref[d] = acc[:, d * split:(d + 1) * split].astype(dtype)

    if split is not None:
        out_spec = pl.BlockSpec((n // split, bm, split), lambda i, j, k: (0, i, 0))
        out_shape = _sds((n // split, m, split), dtype)
    elif n_major:
        out_spec = pl.BlockSpec((None, bm, bn), lambda i, j, k: (j, 0, 0))
        out_shape = _sds((n // bn, m, bn), dtype)
    else:
        out_spec = pl.BlockSpec((bm, bn), lambda i, j, k: (i, j))
        out_shape = _sds((m, n), dtype)
    return pl.pallas_call(
        body, name=name, grid=(m // bm, n // bn, nk),
        in_specs=[a_spec, b_spec],
        out_specs=out_spec, out_shape=out_shape, scratch_shapes=[pltpu.VMEM((bm, bn), F32)],
        compiler_params=_cp("parallel", "parallel", "arbitrary"))(a, b)


def _row_tile(R, C):
    for cand in (512, 256, 128, 64, 32, 16, 8):
        if R % cand == 0 and cand * C * 4 <= 2 ** 20:
            return cand
    return R


def _sum_slots(land):
    S, R, C = land.shape
    tr = _row_tile(R, C)

    def body(l_ref, o_ref):
        acc = l_ref[0].astype(F32)
        for s in range(1, S):
            acc = acc + l_ref[s].astype(F32)
        o_ref[...] = acc

    return pl.pallas_call(
        body, name="sum_slots", grid=(R // tr,), in_specs=[pl.BlockSpec((S, tr, C), lambda i: (0, i, 0))],
        out_specs=pl.BlockSpec((tr, C), lambda i: (i, 0)), out_shape=_sds((R, C)), compiler_params=_cp("parallel"))(land)


def _sum_adamw(land, w, m, v):
    R, C = w.shape
    S = land.shape[0]
    tr = _row_tile(R, C)

    def body(l_ref, w_ref, m_ref, v_ref, g_ref, d_ref, nm_ref, nv_ref):
        gg = l_ref[0].astype(F32)
        for s in range(1, S):
            gg = gg + l_ref[s].astype(F32)
        g_ref[...] = gg
        nm = ADAM_B1 * m_ref[...] + (1.0 - ADAM_B1) * gg
        nv = ADAM_B2 * v_ref[...] + (1.0 - ADAM_B2) * (gg * gg)
        m_hat = nm / (1.0 - ADAM_B1 ** ADAM_STEP)
        v_hat = nv / (1.0 - ADAM_B2 ** ADAM_STEP)
        d_ref[...] = -ADAM_LR * (m_hat / (jnp.sqrt(v_hat) + ADAM_EPS) + ADAM_WD * w_ref[...])
        nm_ref[...] = nm
        nv_ref[...] = nv

    blk = pl.BlockSpec((tr, C), lambda i: (i, 0))
    return pl.pallas_call(
        body, name="sum_adamw", grid=(R // tr,), in_specs=[pl.BlockSpec((S, tr, C), lambda i: (0, i, 0))] + [blk] * 3,
        out_specs=[blk] * 4, out_shape=[_sds((R, C))] * 4, compiler_params=_cp("parallel"))(land, w, m, v)


def _all_to_all(pairs, name):
    n = len(pairs)

    def body(*refs):
        srcs, lands = refs[:n], refs[2 * n:3 * n]
        send, recv, loc = refs[3 * n:]
        x, y, c = lax.axis_index("x"), lax.axis_index("y"), lax.axis_index("c")
        me = 4 * x + 2 * y + c
        own = [pltpu.make_async_copy(pairs[t][2](srcs[t], me, me), pairs[t][3](lands[t], me), loc.at[t]) for t in range(n)]
        for cp in own:
            cp.start()
        sends, recvs = [], []
        for k in range(1, N_DEV):
            px, py, pc = x ^ (k >> 2), y ^ ((k >> 1) & 1), c ^ (k & 1)
            peer = 4 * px + 2 * py + pc
            for t in range(n):
                src = pairs[t][2](srcs[t], me, peer)
                cp = pltpu.make_async_remote_copy(src_ref=src, dst_ref=pairs[t][3](lands[t], me), send_sem=send.at[t, k],
                                                  recv_sem=recv.at[t, k], device_id=(px, py, pc), device_id_type=MESH)
                cp.start()
                sends.append(cp)
                recvs.append(pltpu.make_async_remote_copy(src_ref=src, dst_ref=pairs[t][3](lands[t], peer), send_sem=send.at[t, k],
                                                          recv_sem=recv.at[t, k], device_id=(px, py, pc), device_id_type=MESH))
        for cp in recvs:
            cp.wait_recv()
        for cp in sends:
            cp.wait_send()
        for cp in own:
            cp.wait()

    anyspec = pl.BlockSpec(memory_space=pl.ANY)
    lands = [pr[1] for pr in pairs]
    return pl.pallas_call(
        body, name=name, in_specs=[anyspec] * (2 * n), out_specs=[anyspec] * n,
        out_shape=[_sds(a.shape, a.dtype) for a in lands], input_output_aliases={n + t: t for t in range(n)},
        scratch_shapes=[pltpu.SemaphoreType.DMA((n, N_DEV)), pltpu.SemaphoreType.DMA((n, N_DEV)), pltpu.SemaphoreType.DMA((n,))],
        compiler_params=pltpu.CompilerParams(has_side_effects=True))(*[pr[0] for pr in pairs], *lands)


def _gather_whole(src):
    land = lax.empty((N_DEV,) + src.shape, src.dtype)
    return _all_to_all([(src, land, lambda ref, me, peer: ref, lambda ref, sender: ref.at[sender])], "gather_small_grads")[0]


_HBM = pl.BlockSpec(memory_space=pltpu.HBM)
_SEM = pl.BlockSpec(memory_space=pltpu.SEMAPHORE)
_EFFECT = pltpu.SideEffectType.DATAFLOW_SIDE_EFFECTING


def _peers():
    x, y, c = lax.axis_index("x"), lax.axis_index("y"), lax.axis_index("c")
    out = []
    for k in range(1, N_DEV):
        px, py, pc = x ^ (k >> 2), y ^ ((k >> 1) & 1), c ^ (k & 1)
        out.append((k, (px, py, pc), 4 * px + 2 * py + pc))
    return 4 * x + 2 * y + c, out


def _route(mode, layer):
    if mode == "all":
        return (lambda ref, peer: ref), (lambda ref, sender: ref.at[sender])
    return (lambda ref, peer: ref.at[peer]), (lambda ref, sender: ref.at[sender, layer])


def _split_start(srcs, lands, modes, layer, after, name):
    n = len(srcs)
    routes = [_route(m, layer) for m in modes]

    def body(*refs):
        src, land = refs[:n], refs[n:2 * n]
        send, recv, token = refs[2 * n + 1], refs[2 * n + 2], refs[-1]
        me, peers = _peers()
        for k, dev, peer in peers:
            for t, (src_of, dst_of) in enumerate(routes):
                pltpu.make_async_remote_copy(src_ref=src_of(src[t], peer), dst_ref=dst_of(land[t], me), send_sem=send.at[t * N_DEV + k],
                                             recv_sem=recv.at[t * N_DEV + k], device_id=dev, device_id_type=MESH).start()
        token[...] = jnp.zeros_like(token)

    bufs = list(srcs) + list(lands)
    outs = pl.pallas_call(
        body, name=name,
        out_shape=(pltpu.SemaphoreType.DMA((n * N_DEV,)), pltpu.SemaphoreType.DMA((n * N_DEV,)),
                   *[pltpu.HBM(a.shape, a.dtype) for a in bufs], _sds((8, LANES))),
        in_specs=[_HBM] * (2 * n) + [pl.BlockSpec(memory_space=pl.ANY)],
        out_specs=(_SEM, _SEM, *[_HBM] * (2 * n), pl.BlockSpec(memory_space=pltpu.VMEM)),
        input_output_aliases={i: 2 + i for i in range(2 * n)},
        compiler_params=pltpu.CompilerParams(has_side_effects=_EFFECT),
    )(*[pltpu.with_memory_space_constraint(a, pltpu.HBM) for a in bufs], after)
    return outs[0], outs[1], list(outs[2:2 + n]), list(outs[2 + n:2 + 2 * n]), outs[-1]


def _split_wait(send, recv, srcs, lands, modes, layer, after, name):
    n = len(srcs)
    routes = [_route(m, layer) for m in modes]

    def body(*refs):
        src, land = refs[:n], refs[n:2 * n]
        send_r, recv_r = refs[2 * n], refs[2 * n + 1]
        _, peers = _peers()
        for k, dev, peer in peers:
            for t, (src_of, dst_of) in enumerate(routes):
                cp = pltpu.make_async_remote_copy(src_ref=src_of(src[t], peer), dst_ref=dst_of(land[t], peer), send_sem=send_r.at[t * N_DEV + k],
                                                  recv_sem=recv_r.at[t * N_DEV + k], device_id=dev, device_id_type=MESH)
                cp.wait_send()
                cp.wait_recv()

    bufs = list(srcs) + list(lands)
    outs = pl.pallas_call(
        body, name=name, out_shape=tuple(pltpu.HBM(a.shape, a.dtype) for a in bufs),
        in_specs=[_HBM] * (2 * n) + [_SEM, _SEM, pl.BlockSpec(memory_space=pl.ANY)], out_specs=[_HBM] * (2 * n),
        input_output_aliases={i: i for i in range(2 * n)},
        compiler_params=pltpu.CompilerParams(has_side_effects=_EFFECT),
    )(*bufs, send, recv, after)
    return list(outs[:n]), list(outs[n:])


SHARDED = ("w_in", "glu_w1", "glu_w2", "w_out", "w_ff1", "w_ff2", "w_ple_gate", "w_ple_proj")
SMALL = ("attn_norm_g", "gmlp_ln_g", "gmlp_ln_b", "gmlp_ws", "gmlp_bs", "q_norm_g", "k_norm_g", "sinks", "ssm_a_re", "ssm_a_im",
         "ssm_log_dt", "ssm_b_re", "ssm_b_im", "ssm_c_re", "ssm_c_im", "ssm_d", "mix_out_g", "mlp_norm_g", "ple_norm_g")
WEIGHTS = ("attn_norm_g", "w_in", "gmlp_ln_g", "gmlp_ln_b", "gmlp_ws", "gmlp_bs", "q_norm_g", "k_norm_g", "sinks", "ssm_a_re", "ssm_a_im",
           "ssm_log_dt", "ssm_b_re", "ssm_b_im", "ssm_c_re", "ssm_c_im", "ssm_d", "glu_w1", "glu_w2", "mix_out_g", "w_out", "mlp_norm_g",
           "w_ff1", "w_ff2", "ple_norm_g", "w_ple_gate", "w_ple_proj")
FLAT_COLS = 1024


PACK_TILE_ROWS = 8
PACK_ROWS_MULTIPLE = PACK_TILE_ROWS * N_DEV


def _packed_rows(shape):
    return -(-math.prod(shape) // (PACK_TILE_ROWS * FLAT_COLS)) * PACK_TILE_ROWS


def _pack(arrs, dtype):
    blocks = []
    for a in arrs:
        flat = a.astype(dtype).reshape(-1)
        pad = _packed_rows(a.shape) * FLAT_COLS - flat.shape[0]
        if pad:
            flat = jnp.concatenate([flat, jnp.zeros((pad,), dtype)])
        blocks.append(flat.reshape(-1, FLAT_COLS))
    rows = sum(b.shape[0] for b in blocks)
    if rows % PACK_ROWS_MULTIPLE:
        blocks.append(jnp.zeros((PACK_ROWS_MULTIPLE - rows % PACK_ROWS_MULTIPLE, FLAT_COLS), dtype))
    return jnp.concatenate(blocks, axis=0)


def _unpack(flat, shapes):
    out, r = [], 0
    for s in shapes:
        nr = _packed_rows(s)
        out.append(flat[r:r + nr].reshape(-1)[:math.prod(s)].reshape(s))
        r += nr
    return out


def _col_major(w):
    rows, cols = w.shape
    return w.reshape(rows, N_DEV, cols // N_DEV).transpose(1, 0, 2)


def _from_col_major(s):
    n, rows, cs = s.shape
    return s.transpose(1, 0, 2).reshape(rows, n * cs)


EARLY = ("w_in", "glu_w1", "glu_w2")
LATE = ("w_out", "w_ff1", "w_ff2", "w_ple_gate", "w_ple_proj")
GRADS_MID = ("w_ff1", "w_ff2", "w_ple_gate", "w_ple_proj")
GRADS_END = ("w_in", "glu_w1", "glu_w2", "w_out")


def _layer_fwd(h, p, l, cs, sn, W, late_weights, S, sp):
    z = _inproj_fwd(h, S["attn_norm_g"], W["w_in"])
    ya = _gmlp_fwd(z, S["lng"], S["lnb"], S["gmlp_ws"], S["bsx"])
    yb = _attn_fwd(z, cs, sn, S["qg"], S["kg"], S["sinks"])
    yc, xr, xi = _ssm_fwd(z, sp["bbd"], sp["pwr"], sp["pwi"], sp["cbd"], S["ssm_d"], W["glu_w1"], W["glu_w2"])
    late, token = late_weights(yc)
    W = {**W, **late}
    h1 = _merge_fwd(h, ya, yb, yc, _behind(S["mix_out_g"], token), W["w_out"])
    h2 = _mlp_fwd(h1, S["mlp_norm_g"], W["w_ff1"], W["w_ff2"])
    h3 = _ple_fwd(h2, p, l, S["ple_norm_g"], W["w_ple_gate"], W["w_ple_proj"])
    return h3, dict(h=h, z=z, ya=ya, yb=yb, yc=yc, xr=xr, xi=xi, h1=h1, h2=h2), W


def _layer_bwd(g3, p, l, cs, sn, W, S, sp, A, raw, mid_bwd):
    G = {}
    g2, hn3, dgp, de, G["ple_norm_g"] = _ple_bwd(g3, A["h2"], p, l, S["ple_norm_g"], W["w_ple_gate"], W["w_ple_proj"])
    G["w_ple_gate"] = _tn(hn3, dgp, D_MODEL, D_MODEL, bm=1024, bn=1024, dtype=WIRE, name="tn_gate").reshape(N_DEV, -1, D_MODEL)
    G["w_ple_proj"] = _tn(p, de, PLE_DIM, D_MODEL, bm=PLE_DIM, bn=D_MODEL, a_lead=l, split=D_MODEL // N_DEV, dtype=WIRE, name="tn_proj")
    g1, hn, da, r, G["mlp_norm_g"] = _mlp_bwd(g2, A["h1"], S["mlp_norm_g"], W["w_ff1"], W["w_ff2"])
    G["w_ff1"] = _tn(hn, da, D_MODEL, D_FF, bm=D_MODEL, bn=FF_BLOCK, n_major=True, dtype=WIRE, name="tn_ff1")
    G["w_ff2"] = _tn(r, g2, D_FF, D_MODEL, bm=1024, bn=1024, dtype=WIRE, name="tn_ff2").reshape(N_DEV, -1, D_MODEL)
    token = mid_bwd(g1, G)
    dya, dyb, dyc, yn, G["mix_out_g"] = _merge_bwd(g1, A["ya"], A["yb"], A["yc"], _behind(S["mix_out_g"], token), W["w_out"])
    G["w_out"] = _tn(yn, g1, D_MODEL, D_MODEL, bm=1024, bn=1024, dtype=WIRE, name="tn_out").reshape(N_DEV, -1, D_MODEL)
    dzc, y2, da1, da2, dy, arai, dlr, dli, dd = _ssm_bwd(dyc, A["z"], A["xr"], A["xi"], sp["bbd"], sp["pwr"], sp["pwi"], sp["cbd"],
                                                        S["ssm_d"], W["glu_w1"], W["glu_w2"])
    G["glu_w1"] = _tn(y2, da1, C_WIDTH, C_WIDTH, bm=256, bn=256, dtype=WIRE, name="tn_glu1").reshape(N_DEV, -1, C_WIDTH)
    G["glu_w2"] = _tn(y2, da2, C_WIDTH, C_WIDTH, bm=256, bn=256, dtype=WIRE, name="tn_glu2").reshape(N_DEV, -1, C_WIDTH)
    dbbd = _tn(A["z"], arai, C_WIDTH, 2 * N_STATE, bm=256, bn=1024, a_off=5, name="tn_bbd")
    dcr = _tn(dy, A["xr"], C_WIDTH, N_STATE, bm=256, bn=1024, name="tn_cre")
    dci = _tn(dy, A["xi"], C_WIDTH, N_STATE, bm=256, bn=1024, name="tn_cim")
    dare, daim, dldt, dbtr, dbti, dcre, dcim = _ssm_param_bwd(
        raw["are"], raw["aim"], raw["ldt"], raw["are_x"], raw["aim_x"], raw["ldt_x"], raw["btr"], raw["bti"],
        dlr.reshape(C_GROUPS, C_STATE), dli.reshape(C_GROUPS, C_STATE), dbbd, dcr, dci)
    G["ssm_a_re"], G["ssm_a_im"], G["ssm_log_dt"] = dare, daim, dldt[:, 0]
    G["ssm_b_re"] = dbtr.reshape(C_GROUPS, C_GROUP, C_STATE).transpose(0, 2, 1)
    G["ssm_b_im"] = dbti.reshape(C_GROUPS, C_GROUP, C_STATE).transpose(0, 2, 1)
    G["ssm_c_re"] = dcre.reshape(C_GROUPS, C_GROUP, C_STATE)
    G["ssm_c_im"] = dcim.reshape(C_GROUPS, C_GROUP, C_STATE)
    G["ssm_d"] = dd.reshape(C_GROUPS, C_GROUP)
    dzq, dzk, dzv, dqg, dkg, dsk = _attn_bwd(A["z"], cs, sn, S["qg"], S["kg"], S["sinks"], A["yb"], dyb)
    G["q_norm_g"] = dqg[0, :HEAD_DIM] + dqg[0, HEAD_DIM:]
    G["k_norm_g"] = dkg[0, :HEAD_DIM] + dkg[0, HEAD_DIM:]
    G["sinks"] = dsk[:, 0]
    dza, dws, dbs, dlng, dlnb = _gmlp_bwd(A["z"], dya, S["lng"], S["lnb"], S["gmlp_ws"], S["bsx"])
    G["gmlp_ws"] = dws
    G["gmlp_bs"] = dbs[:, :, 0]
    G["gmlp_ln_g"] = dlng.reshape(A_HEADS, 2, HEAD_DIM)[:, 1]
    G["gmlp_ln_b"] = dlnb.reshape(A_HEADS, 2, HEAD_DIM)[:, 1]
    g0, xn, dz, G["attn_norm_g"] = _inproj_bwd(g1, A["h"], S["attn_norm_g"], W["w_in"], dza, dzq, dzk, dzv, dzc)
    G["w_in"] = _tn(xn, dz, D_MODEL, IN_COLS, bm=D_MODEL, bn=IN_COLS, split=IN_COLS // N_DEV, dtype=WIRE, name="tn_in")
    return g0, G


def _small_layouts(P, l):
    def row(a):
        return a.reshape(1, -1)

    zeros = jnp.zeros((A_HEADS, HEAD_DIM), F32)
    S = dict(
        attn_norm_g=row(P["attn_norm_g"][l]), mix_out_g=row(P["mix_out_g"][l]), mlp_norm_g=row(P["mlp_norm_g"][l]),
        ple_norm_g=row(P["ple_norm_g"][l]),
        lng=jnp.stack([zeros, P["gmlp_ln_g"][l]], axis=1).reshape(1, 512),
        lnb=jnp.stack([zeros, P["gmlp_ln_b"][l]], axis=1).reshape(1, 512),
        gmlp_ws=P["gmlp_ws"][l],
        bsx=jnp.broadcast_to(P["gmlp_bs"][l][:, :, None], (A_HEADS, CHUNK, CHUNK)),
        qg=jnp.tile(P["q_norm_g"][l], 2).reshape(1, LANES), kg=jnp.tile(P["k_norm_g"][l], 2).reshape(1, LANES),
        sinks=jnp.broadcast_to(P["sinks"][l][:, None], (8, LANES)),
        ssm_d=row(P["ssm_d"][l]),
    )
    are, aim = P["ssm_a_re"][l], P["ssm_a_im"][l]
    ldt = jnp.broadcast_to(P["ssm_log_dt"][l][:, None], (C_GROUPS, C_STATE))
    raw = dict(
        are=are, aim=aim, ldt=ldt,
        are_x=jnp.repeat(are, C_GROUP, axis=0), aim_x=jnp.repeat(aim, C_GROUP, axis=0), ldt_x=jnp.repeat(ldt, C_GROUP, axis=0),
        btr=P["ssm_b_re"][l].transpose(0, 2, 1).reshape(C_WIDTH, C_STATE), bti=P["ssm_b_im"][l].transpose(0, 2, 1).reshape(C_WIDTH, C_STATE),
        cre=P["ssm_c_re"][l].reshape(C_WIDTH, C_STATE), cim=P["ssm_c_im"][l].reshape(C_WIDTH, C_STATE),
    )
    return S, raw


def _ssm_prep_layer(raw):
    bbd, cbd, pwr, pwi = _ssm_prep(raw["are"].reshape(1, N_STATE), raw["aim"].reshape(1, N_STATE), raw["ldt"].reshape(1, N_STATE),
                                   raw["are_x"], raw["aim_x"], raw["ldt_x"], raw["btr"], raw["bti"], raw["cre"], raw["cim"])
    return dict(bbd=bbd, cbd=cbd, pwr=pwr, pwi=pwi)


def _behind(row, token):
    return row if token is None else row + token[0:1, 0:1]


def _local_step(x, p, positions, target, P, weights_of, mid_bwd, after_bwd):
    inv = 1.0 / (ROPE_THETA ** (jnp.arange(0, HEAD_DIM, 2, dtype=F32) / HEAD_DIM))
    cs, sn = _rope_tables(positions.reshape(-1, 1), jnp.tile(inv, 4).reshape(1, LANES))
    h = x
    acts, smalls, weights = [], [], []
    for l in range(DEPTH):
        W, late_weights, token = weights_of(l, h)
        S, raw = _small_layouts(P, l)
        sp = _ssm_prep_layer(raw)
        h, A, W = _layer_fwd(h, p, l, cs, sn, W, late_weights, {**S, "attn_norm_g": _behind(S["attn_norm_g"], token)}, sp)
        acts.append(A)
        smalls.append((S, raw, sp))
        weights.append(W)
    g, lsum = _loss_head(h, target)
    grads = [None] * DEPTH
    token = None
    for l in reversed(range(DEPTH)):
        S, raw, sp = smalls[l]
        g, grads[l] = _layer_bwd(g, p, l, cs, sn, weights[l], {**S, "ple_norm_g": _behind(S["ple_norm_g"], token)}, sp, acts[l], raw,
                                 functools.partial(mid_bwd, l))
        token = after_bwd(l, g, grads[l])
    return lsum[0, 0], g, grads


def _layer_weights(g):
    layout = dict(
        w_in=_from_col_major, glu_w1=lambda a: a.reshape(C_WIDTH, C_WIDTH), glu_w2=lambda a: a.reshape(C_WIDTH, C_WIDTH),
        w_out=lambda a: a.reshape(D_MODEL, D_MODEL), w_ff1=lambda a: a, w_ff2=lambda a: a.reshape(D_FF, D_MODEL),
        w_ple_gate=lambda a: a.reshape(D_MODEL, D_MODEL), w_ple_proj=_from_col_major)
    return {n: layout[n](a) for n, a in g.items()}


def kernel(x, p, positions, attn_norm_g, w_in, gmlp_ln_g, gmlp_ln_b, gmlp_ws, gmlp_bs, q_norm_g, k_norm_g, sinks, ssm_a_re, ssm_a_im, ssm_log_dt, ssm_b_re, ssm_b_im, ssm_c_re, ssm_c_im, ssm_d, glu_w1, glu_w2, mix_out_g, w_out, mlp_norm_g, w_ff1, w_ff2, ple_norm_g, w_ple_gate, w_ple_proj, loss_target, m_attn_norm_g, m_w_in, m_gmlp_ln_g, m_gmlp_ln_b, m_gmlp_ws, m_gmlp_bs, m_q_norm_g, m_k_norm_g, m_sinks, m_ssm_a_re, m_ssm_a_im, m_ssm_log_dt, m_ssm_b_re, m_ssm_b_im, m_ssm_c_re, m_ssm_c_im, m_ssm_d, m_glu_w1, m_glu_w2, m_mix_out_g, m_w_out, m_mlp_norm_g, m_w_ff1, m_w_ff2, m_ple_norm_g, m_w_ple_gate, m_w_ple_proj, v_attn_norm_g, v_w_in, v_gmlp_ln_g, v_gmlp_ln_b, v_gmlp_ws, v_gmlp_bs, v_q_norm_g, v_k_norm_g, v_sinks, v_ssm_a_re, v_ssm_a_im, v_ssm_log_dt, v_ssm_b_re, v_ssm_b_im, v_ssm_c_re, v_ssm_c_im, v_ssm_d, v_glu_w1, v_glu_w2, v_mix_out_g, v_w_out, v_mlp_norm_g, v_w_ff1, v_w_ff2, v_ple_norm_g, v_w_ple_gate, v_w_ple_proj):
    env = dict(locals())
    P = {n: env[n] for n in WEIGHTS}
    M = {n: env["m_" + n] for n in WEIGHTS}
    V = {n: env["v_" + n] for n in WEIGHTS}
    return _step(x, p, positions, loss_target, P, M, V)


def _step(x, p, positions, loss_target, P, M, V):
    small_shapes = [P[n].shape for n in SMALL]
    me = 4 * lax.axis_index("x") + 2 * lax.axis_index("y") + lax.axis_index("c")
    nothing = jnp.zeros((8, LANES), F32)

    def put(land, own, lead):
        return lax.dynamic_update_slice(land, own.reshape((1,) * len(lead) + own.shape), tuple(lead) + (0,) * own.ndim)

    def gather_start(l, names, after, tag):
        shards = [P[n][l].astype(WIRE) for n in names]
        lands = [lax.empty((N_DEV,) + s.shape, WIRE) for s in shards]
        send, recv, shards, lands, token = _split_start(shards, lands, ["all"] * len(names), 0, after, f"gather_start_{l}{tag}")
        return dict(names=names, send=send, recv=recv, shards=shards, lands=lands, token=token, name=f"gather_wait_{l}{tag}")

    def gather_wait(f, after):
        shards, lands = _split_wait(f["send"], f["recv"], f["shards"], f["lands"], ["all"] * len(f["names"]), 0, after, f["name"])
        return dict(zip(f["names"], [put(ld, sh, (me,)) for sh, ld in zip(shards, lands)]))

    first = gather_start(0, EARLY, nothing, "a")
    flying = {0: (first, gather_start(0, LATE, first["token"], "b"))}

    def weights_of(l, h):
        fa, fb = flying.pop(l)
        got = gather_wait(fa, h)
        if fb is None:
            token = None
            if l + 1 < DEPTH:
                flying[l + 1] = (gather_start(l + 1, SHARDED, got["w_in"], ""), None)
                token = flying[l + 1][0]["token"]
            W = _layer_weights(got)
            return {n: W[n] for n in EARLY}, (lambda after: ({n: W[n] for n in LATE}, None)), token

        def late_weights(after):
            late = gather_wait(fb, after)
            flying[l + 1] = (gather_start(l + 1, SHARDED, late["w_out"], ""), None)
            return _layer_weights(late), flying[l + 1][0]["token"]

        return _layer_weights(got), late_weights, None

    grad_lands = {n: lax.empty((N_DEV,) + P[n].shape, WIRE) for n in SHARDED}
    sent = []

    def scatter_start(l, names, parts, lands, tag):
        send, recv, parts, lands, token = _split_start(parts, lands, ["own"] * len(parts), l, nothing, f"scatter_start_{l}{tag}")
        sent.append(dict(l=l, names=names, send=send, recv=recv, parts=parts, lands=lands, name=f"scatter_wait_{l}{tag}"))
        return token

    def scatter_wait(after):
        f = sent.pop(0)
        parts, lands = _split_wait(f["send"], f["recv"], f["parts"], f["lands"], ["own"] * len(f["parts"]), f["l"], after, f["name"])
        lands = [put(ld, lax.dynamic_index_in_dim(part, me, 0, keepdims=False), (me, f["l"])) for part, ld in zip(parts, lands)]
        return dict(zip(f["names"], lands))

    def mid_bwd(l, g1, G):
        if l > 0:
            return None
        grad_lands.update(scatter_wait(g1))
        return scatter_start(0, GRADS_MID, [G[n] for n in GRADS_MID], [grad_lands[n] for n in GRADS_MID], "a")

    small_land = []

    def after_bwd(l, g, G):
        if l > 0:
            if sent:
                grad_lands.update(scatter_wait(g))
            return scatter_start(l, SHARDED, [G[n] for n in SHARDED], [grad_lands[n] for n in SHARDED], "")
        sflat = _pack([jnp.stack([grads_of[k][n] for k in range(DEPTH)]) for n in SMALL], F32)
        sparts = sflat.reshape(N_DEV, -1, FLAT_COLS)
        small_land.append(sflat.shape)
        return scatter_start(0, GRADS_END + ("small",), [G[n] for n in GRADS_END] + [sparts],
                             [grad_lands[n] for n in GRADS_END] + [lax.empty((N_DEV, 1) + sparts.shape[1:], F32)], "b")

    grads_of = {}

    def after_bwd_recording(l, g, G):
        grads_of[l] = G
        return after_bwd(l, g, G)

    lsum, gx, grads = _local_step(x[0], p[:, 0], positions[0], loss_target[0], P, weights_of, mid_bwd, after_bwd_recording)
    grad_lands.update(scatter_wait(gx))
    last = scatter_wait(gx)
    small_parts = last.pop("small")
    grad_lands.update(last)
    G, delta, new_m, new_v = {}, {}, {}, {}
    for n in SHARDED:
        shp = P[n].shape
        res = _sum_adamw(grad_lands[n].reshape(N_DEV, -1, shp[-1]), *(a.reshape(-1, shp[-1]) for a in (P[n], M[n], V[n])))
        G[n], delta[n], new_m[n], new_v[n] = (a.reshape(shp) for a in res)
    small_sum = _gather_whole(_sum_slots(small_parts[:, 0])).reshape((1,) + small_land[0])
    res = _sum_adamw(small_sum, _pack([P[n] for n in SMALL], F32), _pack([M[n] for n in SMALL], F32), _pack([V[n] for n in SMALL], F32))
    for dst, flat in zip((G, delta, new_m, new_v), res):
        dst.update(zip(SMALL, _unpack(flat, small_shapes)))
    loss = lax.psum(lsum, ("x", "y", "c"))
    return (loss, gx[None], *[G[n] for n in WEIGHTS], *[delta[n] for n in WEIGHTS], *[new_m[n] for n in WEIGHTS], *[new_v[n] for n in WEIGHTS])
```

```python
import functools
import math

import jax
import jax.numpy as jnp
from jax import lax
from jax.experimental import pallas as pl
from jax.experimental.pallas import tpu as pltpu

F32 = jnp.float32
MXU = jnp.bfloat16
WIRE = jnp.bfloat16

D_MODEL = 1024
DEPTH = 4
HEAD_DIM = 64
A_WIDTH = 256
A_HEADS = 4
CHUNK = 128
B_WIDTH = 512
WINDOW = 128
C_WIDTH = 256
C_GROUP = 16
C_GROUPS = 16
C_STATE = 64
N_STATE = C_GROUPS * C_STATE
IN_COLS = 1536
D_FF = 4096
PLE_DIM = 256
EPS = 1e-6
ROPE_THETA = 10000.0
SCALE = HEAD_DIM ** -0.5
NEG = -1e30
N_DEV = 8

ADAM_LR = 0.001
ADAM_B1 = 0.9
ADAM_B2 = 0.999
ADAM_EPS = 1e-08
ADAM_WD = 0.01
ADAM_STEP = 10

V7X_VMEM_BYTES = 64 * 2 ** 20
VMEM_LIMIT = V7X_VMEM_BYTES - 8 * 2 ** 20
LANES = 128

MESH = pl.DeviceIdType.MESH


def _cp(*sem):
    return pltpu.CompilerParams(dimension_semantics=sem, vmem_limit_bytes=VMEM_LIMIT)


def _sds(shape, dtype=F32):
    return jax.ShapeDtypeStruct(shape, dtype)


def _mm(a, b):
    return jnp.dot(a.astype(MXU), b.astype(MXU), preferred_element_type=F32)


def _mm_nt(a, b):
    return lax.dot_general(a.astype(MXU), b.astype(MXU), (((1,), (1,)), ((), ())), preferred_element_type=F32)


def _mm_tn(a, b):
    return lax.dot_general(a.astype(MXU), b.astype(MXU), (((0,), (0,)), ((), ())), preferred_element_type=F32)


def _lane(shape):
    return lax.broadcasted_iota(jnp.int32, shape, len(shape) - 1)


def _row(shape):
    return lax.broadcasted_iota(jnp.int32, shape, 0)


_GELU_C = math.sqrt(2.0 / math.pi)


def _gelu(x):
    return 0.5 * x * (1.0 + jnp.tanh(_GELU_C * (x + 0.044715 * (x * x * x))))


def _gelu_grad(x):
    t = jnp.tanh(_GELU_C * (x + 0.044715 * (x * x * x)))
    return 0.5 * (1.0 + t) + 0.5 * x * (1.0 - t * t) * (_GELU_C * (1.0 + 3.0 * 0.044715 * (x * x)))


def _sigmoid(x):
    return 1.0 / (1.0 + jnp.exp(-x))


def _rms_stat(x):
    return lax.rsqrt(jnp.mean(x * x, axis=-1, keepdims=True) + EPS)


def _rms_bwd(x, r, g, dy):
    xh = x * r
    dxh = dy * g
    dx = r * (dxh - xh * jnp.mean(dxh * xh, axis=-1, keepdims=True))
    return dx, jnp.sum(dy * xh, axis=0, keepdims=True)


def _tril(w):
    return jnp.where(_row(w.shape) >= _lane(w.shape), w, 0.0)


def _swap64(x):
    return pltpu.roll(x, HEAD_DIM, 1)


def _group_sum64(x, lo):
    s_lo = jnp.sum(jnp.where(lo, x, 0.0), axis=-1, keepdims=True)
    s_hi = jnp.sum(jnp.where(lo, 0.0, x), axis=-1, keepdims=True)
    return jnp.where(lo, s_lo, s_hi)


def _partner(x):
    n = x.shape[-1]
    first = (_lane(x.shape) % HEAD_DIM) < HEAD_DIM // 2
    return jnp.where(first, pltpu.roll(x, n - HEAD_DIM // 2, 1), pltpu.roll(x, HEAD_DIM // 2, 1))


def _rope(y, cs, sn):
    return y * cs + _partner(y) * sn


def _rope_bwd(d, cs, sn):
    return d * cs + _partner(d * sn)


def _qk_norm_rope(x, g, cs, sn):
    lo = _lane(x.shape) < HEAD_DIM
    r = lax.rsqrt(_group_sum64(x * x, lo) * (1.0 / HEAD_DIM) + EPS)
    xh = x * r
    return _rope(xh * g, cs, sn), xh, r


def _qk_norm_rope_bwd(xh, r, g, cs, sn, d):
    lo = _lane(xh.shape) < HEAD_DIM
    dy = _rope_bwd(d, cs, sn)
    dxh = dy * g
    m = _group_sum64(dxh * xh, lo) * (1.0 / HEAD_DIM)
    return r * (dxh - xh * m), jnp.sum(dy * xh, axis=0, keepdims=True)


def _gmlp_head(blk, g, b):
    hi = _lane(blk.shape) >= HEAD_DIM
    mu = jnp.sum(jnp.where(hi, blk, 0.0), axis=-1, keepdims=True) * (1.0 / HEAD_DIM)
    xc = jnp.where(hi, blk - mu, 0.0)
    rstd = lax.rsqrt(jnp.sum(xc * xc, axis=-1, keepdims=True) * (1.0 / HEAD_DIM) + EPS)
    vhat = xc * rstd
    return vhat * g + b, vhat, rstd


def _rope_tables(pos_col, inv_row):
    T = pos_col.shape[0]
    tm = min(T, 1024)

    def body(p_ref, inv_ref, cs_ref, sn_ref):
        ang = p_ref[...].astype(F32) * inv_ref[...]
        s = jnp.sin(ang)
        cs_ref[...] = jnp.cos(ang)
        sn_ref[...] = jnp.where((_lane(ang.shape) % HEAD_DIM) < HEAD_DIM // 2, -s, s)

    blk = pl.BlockSpec((tm, LANES), lambda i: (i, 0))
    return pl.pallas_call(
        body, name="rope_tables", grid=(T // tm,),
        in_specs=[pl.BlockSpec((tm, 1), lambda i: (i, 0)), pl.BlockSpec((1, LANES), lambda i: (0, 0))],
        out_specs=[blk, blk], out_shape=[_sds((T, LANES))] * 2, compiler_params=_cp("parallel"))(pos_col, inv_row)


def _inproj_fwd(h, g, w):
    T = h.shape[0]
    tm = min(T, 512)

    def body(h_ref, g_ref, w_ref, z_ref):
        x = h_ref[...]
        z_ref[...] = _mm(x * _rms_stat(x) * g_ref[...], w_ref[...])

    return pl.pallas_call(
        body, name="inproj_fwd", grid=(T // tm,),
        in_specs=[pl.BlockSpec((tm, D_MODEL), lambda i: (i, 0)), pl.BlockSpec((1, D_MODEL), lambda i: (0, 0)),
                  pl.BlockSpec((D_MODEL, IN_COLS), lambda i: (0, 0))],
        out_specs=pl.BlockSpec((tm, IN_COLS), lambda i: (i, 0)), out_shape=_sds((T, IN_COLS)),
        compiler_params=_cp("parallel"))(h, g, w)


def _inproj_bwd(gres, h, g, w, dza, dzq, dzk, dzv, dzc):
    T = h.shape[0]
    tm = min(T, 512)

    def body(gr_ref, h_ref, g_ref, w_ref, a_ref, q_ref, k_ref, v_ref, c_ref, dh_ref, xn_ref, dz_ref, dg_ref):
        @pl.when(pl.program_id(0) == 0)
        def _():
            dg_ref[...] = jnp.zeros_like(dg_ref)

        x = h_ref[...]
        r = _rms_stat(x)
        gg = g_ref[...]
        dz = jnp.concatenate([a_ref[...], q_ref[...], k_ref[...], v_ref[...], c_ref[...]], axis=1)
        dxn = _mm_nt(dz, w_ref[...])
        dx, dg = _rms_bwd(x, r, gg, dxn)
        dh_ref[...] = gr_ref[...] + dx
        dg_ref[...] += dg
        xn_ref[...] = (x * r * gg).astype(MXU)
        dz_ref[...] = dz.astype(MXU)

    def rows(w_):
        return pl.BlockSpec((tm, w_), lambda i: (i, 0))

    row = pl.BlockSpec((1, D_MODEL), lambda i: (0, 0))
    return pl.pallas_call(
        body, name="inproj_bwd", grid=(T // tm,),
        in_specs=[rows(D_MODEL), rows(D_MODEL), row, pl.BlockSpec((D_MODEL, IN_COLS), lambda i: (0, 0)),
                  rows(512), rows(512), rows(128), rows(128), rows(256)],
        out_specs=[rows(D_MODEL), rows(D_MODEL), rows(IN_COLS), row],
        out_shape=[_sds((T, D_MODEL)), _sds((T, D_MODEL), MXU), _sds((T, IN_COLS), MXU), _sds((1, D_MODEL))],
        compiler_params=_cp("arbitrary"))(gres, h, g, w, dza, dzq, dzk, dzv, dzc)


def _gmlp_fwd(z, lng, lnb, ws, bsx):
    T = z.shape[0]
    tm = min(T, 512)
    nc = tm // CHUNK

    def body(z_ref, g_ref, b_ref, w_ref, bs_ref, ya_ref):
        zg = _gelu(z_ref[...])
        lo = _lane((tm, LANES)) < HEAD_DIM
        prods = []
        for hd in range(A_HEADS):
            sl = slice(hd * LANES, (hd + 1) * LANES)
            blk = zg[:, sl]
            vn, _, _ = _gmlp_head(blk, g_ref[:, sl], b_ref[:, sl])
            wm = _tril(w_ref[hd])
            sv = jnp.concatenate([_mm(wm, vn[c * CHUNK:(c + 1) * CHUNK]) + bs_ref[hd] for c in range(nc)], axis=0)
            prods.append(blk * _swap64(sv))
        ya_ref[:, 0:LANES] = jnp.where(lo, prods[0], _swap64(prods[1]))
        ya_ref[:, LANES:2 * LANES] = jnp.where(lo, prods[2], _swap64(prods[3]))

    row = pl.BlockSpec((1, 512), lambda i: (0, 0))
    mat = pl.BlockSpec((A_HEADS, CHUNK, CHUNK), lambda i: (0, 0, 0))
    return pl.pallas_call(
        body, name="gmlp_fwd", grid=(T // tm,),
        in_specs=[pl.BlockSpec((tm, 512), lambda i: (i, 0)), row, row, mat, mat],
        out_specs=pl.BlockSpec((tm, A_WIDTH), lambda i: (i, 0)), out_shape=_sds((T, A_WIDTH)),
        compiler_params=_cp("parallel"))(z, lng, lnb, ws, bsx)


def _gmlp_bwd(z, dya, lng, lnb, ws, bsx):
    T = z.shape[0]
    tm = min(T, 512)
    nc = tm // CHUNK

    def body(z_ref, dya_ref, g_ref, b_ref, w_ref, bs_ref, dza_ref, dw_ref, dbs_ref, dg_ref, db_ref):
        @pl.when(pl.program_id(0) == 0)
        def _():
            dw_ref[...] = jnp.zeros_like(dw_ref)
            dbs_ref[...] = jnp.zeros_like(dbs_ref)
            dg_ref[...] = jnp.zeros_like(dg_ref)
            db_ref[...] = jnp.zeros_like(db_ref)

        za = z_ref[...]
        zg = _gelu(za)
        gp = _gelu_grad(za)
        lo = _lane((tm, LANES)) < HEAD_DIM
        for hd in range(A_HEADS):
            sl = slice(hd * LANES, (hd + 1) * LANES)
            blk = zg[:, sl]
            g = g_ref[:, sl]
            vn, vhat, rstd = _gmlp_head(blk, g, b_ref[:, sl])
            wm = _tril(w_ref[hd])
            pair = dya_ref[:, (hd // 2) * LANES:(hd // 2 + 1) * LANES]
            dy = jnp.where(lo, pair if hd % 2 == 0 else _swap64(pair), 0.0)
            dsv = _swap64(dy * blk)
            svs, dvns = [], []
            dw = jnp.zeros((CHUNK, CHUNK), F32)
            dbs = jnp.zeros((CHUNK, 1), F32)
            for c in range(nc):
                cs = slice(c * CHUNK, (c + 1) * CHUNK)
                svs.append(_mm(wm, vn[cs]) + bs_ref[hd])
                dw = dw + _mm_nt(dsv[cs], vn[cs])
                dbs = dbs + jnp.sum(dsv[cs], axis=-1, keepdims=True)
                dvns.append(_mm_tn(wm, dsv[cs]))
            sv = jnp.concatenate(svs, axis=0)
            dvn = jnp.concatenate(dvns, axis=0)
            dw_ref[hd] += _tril(dw)
            dbs_ref[hd] += jnp.broadcast_to(dbs, (CHUNK, CHUNK))
            dg_ref[:, sl] += jnp.sum(dvn * vhat, axis=0, keepdims=True)
            db_ref[:, sl] += jnp.sum(dvn, axis=0, keepdims=True)
            du = dy * _swap64(sv)
            dvh = dvn * g
            m1 = jnp.sum(dvh, axis=-1, keepdims=True) * (1.0 / HEAD_DIM)
            m2 = jnp.sum(dvh * vhat, axis=-1, keepdims=True) * (1.0 / HEAD_DIM)
            dv = jnp.where(lo, 0.0, rstd * (dvh - m1 - vhat * m2))
            dza_ref[:, sl] = (du + dv) * gp[:, sl]

    row = pl.BlockSpec((1, 512), lambda i: (0, 0))
    mat = pl.BlockSpec((A_HEADS, CHUNK, CHUNK), lambda i: (0, 0, 0))
    return pl.pallas_call(
        body, name="gmlp_bwd", grid=(T // tm,),
        in_specs=[pl.BlockSpec((tm, 512), lambda i: (i, 0)), pl.BlockSpec((tm, A_WIDTH), lambda i: (i, 0)), row, row, mat, mat],
        out_specs=[pl.BlockSpec((tm, 512), lambda i: (i, 0)), mat, mat, row, row],
        out_shape=[_sds((T, 512)), _sds((A_HEADS, CHUNK, CHUNK)), _sds((A_HEADS, CHUNK, CHUNK)), _sds((1, 512)), _sds((1, 512))],
        compiler_params=_cp("arbitrary"))(z, dya, lng, lnb, ws, bsx)


def _attn_specs(T, tq, tile_of):
    nb = tq // WINDOW

    def prev(i):
        return jnp.maximum(tile_of(i) * nb - 1, 0)

    row = pl.BlockSpec((1, LANES), lambda i: (0, 0))
    return [
        pl.BlockSpec((tq, B_WIDTH), lambda i: (tile_of(i), 1)),
        pl.BlockSpec((tq, LANES), lambda i: (tile_of(i), 8)),
        pl.BlockSpec((tq, LANES), lambda i: (tile_of(i), 9)),
        pl.BlockSpec((WINDOW, LANES), lambda i: (prev(i), 8)),
        pl.BlockSpec((WINDOW, LANES), lambda i: (prev(i), 9)),
        pl.BlockSpec((tq, LANES), lambda i: (tile_of(i), 0)),
        pl.BlockSpec((tq, LANES), lambda i: (tile_of(i), 0)),
        pl.BlockSpec((WINDOW, LANES), lambda i: (prev(i), 0)),
        pl.BlockSpec((WINDOW, LANES), lambda i: (prev(i), 0)),
        row, row,
        pl.BlockSpec((8, LANES), lambda i: (0, 0)),
    ]


def _attn_bias(first):
    qi = lax.broadcasted_iota(jnp.int32, (WINDOW, 2 * WINDOW), 0)
    kj = lax.broadcasted_iota(jnp.int32, (WINDOW, 2 * WINDOW), 1)
    diff = qi + WINDOW - kj
    ok = (diff >= 0) & (diff < WINDOW) & ((kj >= WINDOW) | jnp.logical_not(first))
    return jnp.where(ok, 0.0, NEG)


def _dup_heads(x, lo):
    sw = _swap64(x)
    return jnp.where(lo, x, sw), jnp.where(lo, sw, x)


HEADS_PER_KV = 4


def _stack_heads(x0, x1, lo):
    return jnp.concatenate([jnp.where(lo, x0, 0.0), jnp.where(lo, 0.0, x0), jnp.where(lo, x1, 0.0), jnp.where(lo, 0.0, x1)], axis=0)


def _unstack_heads(x4, lo):
    return (jnp.where(lo, x4[0:WINDOW], x4[WINDOW:2 * WINDOW]), jnp.where(lo, x4[2 * WINDOW:3 * WINDOW], x4[3 * WINDOW:4 * WINDOW]))


def _sink_column(sk_ref, g):
    return jnp.concatenate([jnp.broadcast_to(sk_ref[a:a + 1, 0:1], (WINDOW, 1)) for a in range(HEADS_PER_KV * g, HEADS_PER_KV * (g + 1))], axis=0)


def _attn_probs(q4, kw, bias, sink):
    s = _mm_nt(q4, kw)
    s = (s.reshape(HEADS_PER_KV, WINDOW, 2 * WINDOW) + bias[None]).reshape(HEADS_PER_KV * WINDOW, 2 * WINDOW)
    m = jnp.maximum(jnp.max(s, axis=-1, keepdims=True), sink)
    p = jnp.exp(s - m)
    es = jnp.exp(sink - m)
    inv = 1.0 / (jnp.sum(p, axis=-1, keepdims=True) + es)
    return p * inv, es * inv


def _attn_fwd(z, cs, sn, qg, kg, sinks):
    T = z.shape[0]
    tq = min(T, 512)
    nb = tq // WINDOW

    def body(q_ref, k_ref, v_ref, kp_ref, vp_ref, cs_ref, sn_ref, csp_ref, snp_ref, qg_ref, kg_ref, sk_ref, o_ref):
        i = pl.program_id(0)
        csq, snq = cs_ref[...], sn_ref[...]
        cs_all = jnp.concatenate([csp_ref[...], csq], axis=0)
        sn_all = jnp.concatenate([snp_ref[...], snq], axis=0)
        k_all = jnp.concatenate([kp_ref[...], k_ref[...]], axis=0)
        v_all = jnp.concatenate([vp_ref[...], v_ref[...]], axis=0)
        kr, _, _ = _qk_norm_rope(k_all, kg_ref[...], cs_all, sn_all)
        lo_all = _lane(k_all.shape) < HEAD_DIM
        kd = _dup_heads(kr, lo_all)
        vd = _dup_heads(v_all, lo_all)
        lo = _lane((WINDOW, LANES)) < HEAD_DIM
        qrs = [_qk_norm_rope(q_ref[:, pr * LANES:(pr + 1) * LANES], qg_ref[...], csq, snq)[0] * SCALE for pr in range(4)]
        biases = [_attn_bias(i * nb + b == 0) for b in range(nb)]
        for g in range(2):
            sink = _sink_column(sk_ref, g)
            for b in range(nb):
                bs = slice(b * WINDOW, (b + 1) * WINDOW)
                ws = slice(b * WINDOW, (b + 2) * WINDOW)
                pn, _ = _attn_probs(_stack_heads(qrs[2 * g][bs], qrs[2 * g + 1][bs], lo), kd[g][ws], biases[b], sink)
                o0, o1 = _unstack_heads(_mm(pn, vd[g][ws]), lo)
                o_ref[bs, 2 * g * LANES:(2 * g + 1) * LANES] = o0
                o_ref[bs, (2 * g + 1) * LANES:(2 * g + 2) * LANES] = o1

    return pl.pallas_call(
        body, name="attn_fwd", grid=(T // tq,),
        in_specs=_attn_specs(T, tq, lambda i: i),
        out_specs=pl.BlockSpec((tq, B_WIDTH), lambda i: (i, 0)), out_shape=_sds((T, B_WIDTH)),
        compiler_params=_cp("parallel"))(z, z, z, z, z, cs, sn, cs, sn, qg, kg, sinks)


def _attn_bwd(z, cs, sn, qg, kg, sinks, o, do):
    T = z.shape[0]
    tq = min(T, 512)
    nb = tq // WINDOW
    nt = T // tq
    tk = tq + WINDOW

    def tile_of(i):
        return nt - 1 - i

    def body(q_ref, k_ref, v_ref, kp_ref, vp_ref, cs_ref, sn_ref, csp_ref, snp_ref, qg_ref, kg_ref, sk_ref, o_ref, do_ref,
             dq_ref, dk_ref, dv_ref, dqg_ref, dkg_ref, dsk_ref, acck, accv, ck, cv):
        i = pl.program_id(0)
        ti = nt - 1 - i

        @pl.when(i == 0)
        def _():
            dqg_ref[...] = jnp.zeros_like(dqg_ref)
            dkg_ref[...] = jnp.zeros_like(dkg_ref)
            dsk_ref[...] = jnp.zeros_like(dsk_ref)
            ck[...] = jnp.zeros_like(ck)
            cv[...] = jnp.zeros_like(cv)

        csq, snq = cs_ref[...], sn_ref[...]
        cs_all = jnp.concatenate([csp_ref[...], csq], axis=0)
        sn_all = jnp.concatenate([snp_ref[...], snq], axis=0)
        k_all = jnp.concatenate([kp_ref[...], k_ref[...]], axis=0)
        v_all = jnp.concatenate([vp_ref[...], v_ref[...]], axis=0)
        kr, kh, rk = _qk_norm_rope(k_all, kg_ref[...], cs_all, sn_all)
        lo_all = _lane(k_all.shape) < HEAD_DIM
        kd = _dup_heads(kr, lo_all)
        vd = _dup_heads(v_all, lo_all)
        lo = _lane((WINDOW, LANES)) < HEAD_DIM
        acck[...] = jnp.zeros_like(acck)
        accv[...] = jnp.zeros_like(accv)
        prep = [_qk_norm_rope(q_ref[:, pr * LANES:(pr + 1) * LANES], qg_ref[...], csq, snq) for pr in range(4)]
        biases = [_attn_bias(ti * nb + b == 0) for b in range(nb)]
        dqs = [[None] * nb for _ in range(4)]
        for g in range(2):
            sink = _sink_column(sk_ref, g)
            dsink = jnp.zeros((HEADS_PER_KV * WINDOW, 1), F32)
            for b in range(nb):
                bs = slice(b * WINDOW, (b + 1) * WINDOW)
                ws = slice(b * WINDOW, (b + 2) * WINDOW)
                kw, vw = kd[g][ws], vd[g][ws]
                q4 = _stack_heads(prep[2 * g][0][bs] * SCALE, prep[2 * g + 1][0][bs] * SCALE, lo)
                pn, psink = _attn_probs(q4, kw, biases[b], sink)
                o0, o1 = o_ref[bs, 2 * g * LANES:(2 * g + 1) * LANES], o_ref[bs, (2 * g + 1) * LANES:(2 * g + 2) * LANES]
                do4 = _stack_heads(do_ref[bs, 2 * g * LANES:(2 * g + 1) * LANES], do_ref[bs, (2 * g + 1) * LANES:(2 * g + 2) * LANES], lo)
                delta = jnp.sum(do4 * jnp.concatenate([o0, o0, o1, o1], axis=0), axis=-1, keepdims=True)
                ds = pn * (_mm_nt(do4, vw) - delta)
                dsink = dsink - psink * delta
                dqs[2 * g][b], dqs[2 * g + 1][b] = _unstack_heads(_mm(ds, kw) * SCALE, lo)
                acck[g, ws, :] += _mm_tn(ds, q4)
                accv[g, ws, :] += _mm_tn(pn, do4)
            for hh in range(HEADS_PER_KV):
                a = HEADS_PER_KV * g + hh
                dsk_ref[a:a + 1, :] += jnp.zeros((1, LANES), F32) + jnp.sum(dsink[hh * WINDOW:(hh + 1) * WINDOW])
        for pr in range(4):
            _, qh, rq = prep[pr]
            dx, dg = _qk_norm_rope_bwd(qh, rq, qg_ref[...], csq, snq, jnp.concatenate(dqs[pr], axis=0))
            dq_ref[:, pr * LANES:(pr + 1) * LANES] = dx
            dqg_ref[...] += dg

        def fold(acc):
            f0 = acc[0] + _swap64(acc[0])
            f1 = acc[1] + _swap64(acc[1])
            return jnp.where(lo_all, f0, f1)

        dk_all = fold(acck)
        dv_all = fold(accv)
        pad = jnp.zeros((tq - WINDOW, LANES), F32)
        dk_own = dk_all[WINDOW:] + (jnp.concatenate([pad, ck[...]], axis=0) if nb > 1 else ck[...])
        dv_own = dv_all[WINDOW:] + (jnp.concatenate([pad, cv[...]], axis=0) if nb > 1 else cv[...])
        ck[...] = dk_all[:WINDOW]
        cv[...] = dv_all[:WINDOW]
        dxk, dgk = _qk_norm_rope_bwd(kh[WINDOW:], rk[WINDOW:], kg_ref[...], csq, snq, dk_own)
        dk_ref[...] = dxk
        dkg_ref[...] += dgk
        dv_ref[...] = dv_own

    row = pl.BlockSpec((1, LANES), lambda i: (0, 0))
    return pl.pallas_call(
        body, name="attn_bwd", grid=(nt,),
        in_specs=_attn_specs(T, tq, tile_of) + [pl.BlockSpec((tq, B_WIDTH), lambda i: (tile_of(i), 0))] * 2,
        out_specs=[pl.BlockSpec((tq, B_WIDTH), lambda i: (tile_of(i), 0)), pl.BlockSpec((tq, LANES), lambda i: (tile_of(i), 0)),
                   pl.BlockSpec((tq, LANES), lambda i: (tile_of(i), 0)), row, row, pl.BlockSpec((8, LANES), lambda i: (0, 0))],
        out_shape=[_sds((T, B_WIDTH)), _sds((T, LANES)), _sds((T, LANES)), _sds((1, LANES)), _sds((1, LANES)), _sds((8, LANES))],
        scratch_shapes=[pltpu.VMEM((2, tk, LANES), F32), pltpu.VMEM((2, tk, LANES), F32),
                        pltpu.VMEM((WINDOW, LANES), F32), pltpu.VMEM((WINDOW, LANES), F32)],
        compiler_params=_cp("arbitrary"))(z, z, z, z, z, cs, sn, cs, sn, qg, kg, sinks, o, do)


def _bbar_t(are, aim, ldt, btr, bti):
    lbr, lbi = _lam_bar(are, aim, ldt)
    den = are * are + aim * aim
    nr = lbr - 1.0
    cr = (nr * are + lbi * aim) / den
    ci = (lbi * are - nr * aim) / den
    return cr * btr - ci * bti, cr * bti + ci * btr


def _lam_bar(are, aim, ldt):
    dt = jnp.exp(ldt)
    er = jnp.exp(are * dt)
    return er * jnp.cos(aim * dt), er * jnp.sin(aim * dt)


def _block_diag(x):
    t = jnp.concatenate([x] * C_GROUPS, axis=1)
    return jnp.where(_row(t.shape) // C_GROUP == _lane(t.shape) // C_STATE, t, 0.0)


def _block_diag_fold(m):
    rg = _row((C_WIDTH, C_STATE)) // C_GROUP
    acc = jnp.zeros((C_WIDTH, C_STATE), F32)
    for g in range(C_GROUPS):
        acc = acc + jnp.where(rg == g, m[:, g * C_STATE:(g + 1) * C_STATE], 0.0)
    return acc


def _ssm_prep(are, aim, ldt, are_x, aim_x, ldt_x, btr, bti, cre, cim):
    def body(are_r, aim_r, ldt_r, arex_r, aimx_r, ldtx_r, btr_r, bti_r, cre_r, cim_r, bbd_ref, cbd_ref, pwr_ref, pwi_ref):
        lr, li = _lam_bar(are_r[...], aim_r[...], ldt_r[...])
        cr, ci = lr, li
        for r in range(SCAN_SEG):
            pwr_ref[r:r + 1, :] = cr
            pwi_ref[r:r + 1, :] = ci
            cr, ci = cr * lr - ci * li, cr * li + ci * lr
        br, bi = _bbar_t(arex_r[...], aimx_r[...], ldtx_r[...], btr_r[...], bti_r[...])
        bbd_ref[...] = jnp.concatenate([_block_diag(br), _block_diag(bi)], axis=1).astype(MXU)
        cbd_ref[...] = jnp.concatenate([_block_diag(cre_r[...]), -_block_diag(cim_r[...])], axis=1).astype(MXU)

    return pl.pallas_call(
        body, name="ssm_prep",
        out_shape=[_sds((C_WIDTH, 2 * N_STATE), MXU), _sds((C_WIDTH, 2 * N_STATE), MXU), _sds((SCAN_SEG, N_STATE)), _sds((SCAN_SEG, N_STATE))],
        compiler_params=pltpu.CompilerParams(vmem_limit_bytes=VMEM_LIMIT))(are, aim, ldt, are_x, aim_x, ldt_x, btr, bti, cre, cim)


def _ssm_param_bwd(are, aim, ldt, are_x, aim_x, ldt_x, btr, bti, dlr, dli, dbbd, dcr, dci):
    def body(are_r, aim_r, ldt_r, arex_r, aimx_r, ldtx_r, btr_r, bti_r, dlr_r, dli_r, dbbd_r, dcr_r, dci_r,
             dare_ref, daim_ref, dldt_ref, dbtr_ref, dbti_ref, dcre_ref, dcim_ref):
        _, vjp_l = jax.vjp(_lam_bar, are_r[...], aim_r[...], ldt_r[...])
        da1, di1, dl1 = vjp_l((dlr_r[...], dli_r[...]))
        dbr = _block_diag_fold(dbbd_r[:, 0:N_STATE])
        dbi = _block_diag_fold(dbbd_r[:, N_STATE:2 * N_STATE])
        _, vjp_b = jax.vjp(_bbar_t, arex_r[...], aimx_r[...], ldtx_r[...], btr_r[...], bti_r[...])
        da2, di2, dl2, dbtr, dbti = vjp_b((dbr, dbi))

        def gsum(x):
            return x.reshape(C_GROUPS, C_GROUP, C_STATE).sum(axis=1)

        dare_ref[...] = da1 + gsum(da2)
        daim_ref[...] = di1 + gsum(di2)
        dldt_ref[...] = jnp.broadcast_to(jnp.sum(dl1 + gsum(dl2), axis=-1, keepdims=True), (C_GROUPS, LANES))
        dbtr_ref[...] = dbtr
        dbti_ref[...] = dbti
        dcre_ref[...] = _block_diag_fold(dcr_r[...])
        dcim_ref[...] = -_block_diag_fold(dci_r[...])

    g = _sds((C_GROUPS, C_STATE))
    x = _sds((C_WIDTH, C_STATE))
    return pl.pallas_call(
        body, name="ssm_param_bwd", out_shape=[g, g, _sds((C_GROUPS, LANES)), x, x, x, x],
        compiler_params=pltpu.CompilerParams(vmem_limit_bytes=VMEM_LIMIT))(are, aim, ldt, are_x, aim_x, ldt_x, btr, bti, dlr, dli, dbbd, dcr, dci)


SCAN_TILE = 256
SCAN_SEG = 8


def _scan_tables(pwr_ref, pwi_ref, conj, reverse):
    row = _row((SCAN_SEG, N_STATE))
    shifts = []
    for k in (1, 2, 4):
        keep = (row < SCAN_SEG - k) if reverse else (row >= k)
        ar = jnp.broadcast_to(pwr_ref[k - 1:k, :], (SCAN_SEG, N_STATE))
        ai = jnp.broadcast_to(pwi_ref[k - 1:k, :], (SCAN_SEG, N_STATE)) * conj
        shifts.append((SCAN_SEG - k if reverse else k, jnp.where(keep, ar, 0.0), jnp.where(keep, ai, 0.0)))
    if reverse:
        pr = jnp.concatenate([pwr_ref[SCAN_SEG - 1 - r:SCAN_SEG - r, :] for r in range(SCAN_SEG)], axis=0)
        pi = jnp.concatenate([pwi_ref[SCAN_SEG - 1 - r:SCAN_SEG - r, :] for r in range(SCAN_SEG)], axis=0) * conj
    else:
        pr, pi = pwr_ref[...], pwi_ref[...] * conj
    return shifts, (pr, pi)


def _tile_scan(xr_ref, xi_ref, pwr_ref, pwi_ref, sr, si, conj, reverse):
    shifts, (pr, pi) = _scan_tables(pwr_ref, pwi_ref, conj, reverse)
    groups = xr_ref.shape[0] // SCAN_SEG
    out_row = 0 if reverse else SCAN_SEG - 1

    def step(k, c):
        cr, ci = c
        g = groups - 1 - k if reverse else k
        rows = pl.ds(pl.multiple_of(g * SCAN_SEG, SCAN_SEG), SCAN_SEG)
        xr, xi = xr_ref[rows, :], xi_ref[rows, :]
        for amount, ar, ai in shifts:
            qr, qi = pltpu.roll(xr, amount, 0), pltpu.roll(xi, amount, 0)
            xr, xi = xr + ar * qr - ai * qi, xi + ar * qi + ai * qr
        xr, xi = xr + pr * cr - pi * ci, xi + pr * ci + pi * cr
        xr_ref[rows, :] = xr
        xi_ref[rows, :] = xi
        return xr[out_row:out_row + 1], xi[out_row:out_row + 1]

    cr, ci = lax.fori_loop(0, groups, step, (sr[...], si[...]), unroll=2)
    sr[...] = cr
    si[...] = ci


def _ssm_fwd(z, bbd, pwr, pwi, cbd, dsk, w1, w2):
    T = z.shape[0]
    tt = min(T, SCAN_TILE)

    def body(u_ref, bbd_ref, pwr_ref, pwi_ref, cbd_ref, d_ref, w1_ref, w2_ref, yc_ref, xr_ref, xi_ref, sr, si):
        @pl.when(pl.program_id(0) == 0)
        def _():
            sr[...] = jnp.zeros_like(sr)
            si[...] = jnp.zeros_like(si)

        u = u_ref[...]
        bu = _mm(u, bbd_ref[...])
        xr_ref[...] = bu[:, 0:N_STATE]
        xi_ref[...] = bu[:, N_STATE:2 * N_STATE]
        _tile_scan(xr_ref, xi_ref, pwr_ref, pwi_ref, sr, si, 1.0, False)
        x = jnp.concatenate([xr_ref[...], xi_ref[...]], axis=1)
        y2 = _gelu(_mm_nt(x, cbd_ref[...]) + d_ref[...] * u)
        yc_ref[...] = _mm(y2, w1_ref[...]) * _sigmoid(_mm(y2, w2_ref[...]))

    big = pl.BlockSpec((C_WIDTH, 2 * N_STATE), lambda i: (0, 0))
    tab = pl.BlockSpec((SCAN_SEG, N_STATE), lambda i: (0, 0))
    wsp = pl.BlockSpec((C_WIDTH, C_WIDTH), lambda i: (0, 0))
    xs = pl.BlockSpec((tt, N_STATE), lambda i: (i, 0))
    return pl.pallas_call(
        body, name="ssm_fwd", grid=(T // tt,),
        in_specs=[pl.BlockSpec((tt, C_WIDTH), lambda i: (i, 5)), big, tab, tab, big, pl.BlockSpec((1, C_WIDTH), lambda i: (0, 0)), wsp, wsp],
        out_specs=[pl.BlockSpec((tt, C_WIDTH), lambda i: (i, 0)), xs, xs],
        out_shape=[_sds((T, C_WIDTH)), _sds((T, N_STATE)), _sds((T, N_STATE))],
        scratch_shapes=[pltpu.VMEM((1, N_STATE), F32)] * 2,
        compiler_params=_cp("arbitrary"))(z, bbd, pwr, pwi, cbd, dsk, w1, w2)


def _ssm_bwd(dyc, z, xr, xi, bbd, pwr, pwi, cbd, dsk, w1, w2):
    T = z.shape[0]
    tt = min(T, SCAN_TILE)
    nt = T // tt

    def tile_of(i):
        return nt - 1 - i

    def body(dyc_ref, u_ref, xr_ref, xi_ref, xpr_ref, xpi_ref, bbd_ref, pwr_ref, pwi_ref, cbd_ref, d_ref, w1_ref, w2_ref,
             du_ref, y2_ref, da1_ref, da2_ref, dy_ref, arai_ref, dlr_ref, dli_ref, dd_ref, gr, gi, sr, si):
        i = pl.program_id(0)
        ti = nt - 1 - i

        @pl.when(i == 0)
        def _():
            sr[...] = jnp.zeros_like(sr)
            si[...] = jnp.zeros_like(si)
            dlr_ref[...] = jnp.zeros_like(dlr_ref)
            dli_ref[...] = jnp.zeros_like(dli_ref)
            dd_ref[...] = jnp.zeros_like(dd_ref)

        u = u_ref[...]
        xr_t, xi_t = xr_ref[...], xi_ref[...]
        y = _mm_nt(jnp.concatenate([xr_t, xi_t], axis=1), cbd_ref[...]) + d_ref[...] * u
        y2 = _gelu(y)
        a1 = _mm(y2, w1_ref[...])
        sg = _sigmoid(_mm(y2, w2_ref[...]))
        dyc_t = dyc_ref[...]
        da1 = dyc_t * sg
        da2 = dyc_t * a1 * sg * (1.0 - sg)
        dy = (_mm_nt(da1, w1_ref[...]) + _mm_nt(da2, w2_ref[...])) * _gelu_grad(y)
        gx = _mm(dy, cbd_ref[...])
        gr[...] = gx[:, 0:N_STATE]
        gi[...] = gx[:, N_STATE:2 * N_STATE]
        _tile_scan(gr, gi, pwr_ref, pwi_ref, sr, si, -1.0, True)
        ar, ai = gr[...], gi[...]
        first_row = _row(ar.shape) == 0
        live = jnp.where(ti > 0, 1.0, 0.0)
        xsr = jnp.where(first_row, xpr_ref[7:8, :] * live, pltpu.roll(xr_t, 1, 0))
        xsi = jnp.where(first_row, xpi_ref[7:8, :] * live, pltpu.roll(xi_t, 1, 0))
        dlr_ref[...] += jnp.sum(ar * xsr + ai * xsi, axis=0, keepdims=True)
        dli_ref[...] += jnp.sum(ai * xsr - ar * xsi, axis=0, keepdims=True)
        dd_ref[...] += jnp.sum(dy * u, axis=0, keepdims=True)
        arai = jnp.concatenate([ar, ai], axis=1)
        du_ref[...] = _mm_nt(arai, bbd_ref[...]) + d_ref[...] * dy
        y2_ref[...] = y2.astype(MXU)
        da1_ref[...] = da1.astype(MXU)
        da2_ref[...] = da2.astype(MXU)
        dy_ref[...] = dy.astype(MXU)
        arai_ref[...] = arai.astype(MXU)

    def prev(i):
        return jnp.maximum(tile_of(i) * (tt // 8) - 1, 0)

    big = pl.BlockSpec((C_WIDTH, 2 * N_STATE), lambda i: (0, 0))
    tab = pl.BlockSpec((SCAN_SEG, N_STATE), lambda i: (0, 0))
    srow = pl.BlockSpec((1, N_STATE), lambda i: (0, 0))
    wsp = pl.BlockSpec((C_WIDTH, C_WIDTH), lambda i: (0, 0))
    xs = pl.BlockSpec((tt, N_STATE), lambda i: (tile_of(i), 0))
    xp = pl.BlockSpec((8, N_STATE), lambda i: (prev(i), 0))
    cw = pl.BlockSpec((tt, C_WIDTH), lambda i: (tile_of(i), 0))
    drow = pl.BlockSpec((1, C_WIDTH), lambda i: (0, 0))
    return pl.pallas_call(
        body, name="ssm_bwd", grid=(nt,),
        in_specs=[cw, pl.BlockSpec((tt, C_WIDTH), lambda i: (tile_of(i), 5)), xs, xs, xp, xp, big, tab, tab, big, drow, wsp, wsp],
        out_specs=[cw, cw, cw, cw, cw, pl.BlockSpec((tt, 2 * N_STATE), lambda i: (tile_of(i), 0)), srow, srow, drow],
        out_shape=[_sds((T, C_WIDTH))] + [_sds((T, C_WIDTH), MXU)] * 4 + [_sds((T, 2 * N_STATE), MXU), _sds((1, N_STATE)), _sds((1, N_STATE)), _sds((1, C_WIDTH))],
        scratch_shapes=[pltpu.VMEM((tt, N_STATE), F32)] * 2 + [pltpu.VMEM((1, N_STATE), F32)] * 2,
        compiler_params=_cp("arbitrary"))(dyc, z, xr, xi, xr, xi, bbd, pwr, pwi, cbd, dsk, w1, w2)


_GROUPS = ((0, A_WIDTH), (A_WIDTH, A_WIDTH + B_WIDTH), (A_WIDTH + B_WIDTH, D_MODEL))


def _merge_fwd(h, ya, yb, yc, g, w):
    T = h.shape[0]
    tm = min(T, 512)

    def body(h_ref, a_ref, b_ref, c_ref, g_ref, w_ref, o_ref):
        yn = jnp.concatenate([y * _rms_stat(y) for y in (a_ref[...], b_ref[...], c_ref[...])], axis=1) * g_ref[...]
        o_ref[...] = h_ref[...] + _mm(yn, w_ref[...])

    def rows(w_):
        return pl.BlockSpec((tm, w_), lambda i: (i, 0))

    return pl.pallas_call(
        body, name="merge_fwd", grid=(T // tm,),
        in_specs=[rows(D_MODEL), rows(A_WIDTH), rows(B_WIDTH), rows(C_WIDTH), pl.BlockSpec((1, D_MODEL), lambda i: (0, 0)),
                  pl.BlockSpec((D_MODEL, D_MODEL), lambda i: (0, 0))],
        out_specs=rows(D_MODEL), out_shape=_sds((T, D_MODEL)), compiler_params=_cp("parallel"))(h, ya, yb, yc, g, w)


def _merge_bwd(gres, ya, yb, yc, g, w):
    T = gres.shape[0]
    tm = min(T, 512)

    def body(gr_ref, a_ref, b_ref, c_ref, g_ref, w_ref, da_ref, db_ref, dc_ref, yn_ref, dg_ref):
        @pl.when(pl.program_id(0) == 0)
        def _():
            dg_ref[...] = jnp.zeros_like(dg_ref)

        dyn = _mm_nt(gr_ref[...], w_ref[...])
        yns, dgs = [], []
        for (c0, c1), y_ref, d_ref in zip(_GROUPS, (a_ref, b_ref, c_ref), (da_ref, db_ref, dc_ref)):
            y = y_ref[...]
            r = _rms_stat(y)
            gg = g_ref[:, c0:c1]
            dx, dg = _rms_bwd(y, r, gg, dyn[:, c0:c1])
            d_ref[...] = dx
            dgs.append(dg)
            yns.append(y * r * gg)
        dg_ref[...] += jnp.concatenate(dgs, axis=1)
        yn_ref[...] = jnp.concatenate(yns, axis=1).astype(MXU)

    def rows(w_):
        return pl.BlockSpec((tm, w_), lambda i: (i, 0))

    row = pl.BlockSpec((1, D_MODEL), lambda i: (0, 0))
    return pl.pallas_call(
        body, name="merge_bwd", grid=(T // tm,),
        in_specs=[rows(D_MODEL), rows(A_WIDTH), rows(B_WIDTH), rows(C_WIDTH), row, pl.BlockSpec((D_MODEL, D_MODEL), lambda i: (0, 0))],
        out_specs=[rows(A_WIDTH), rows(B_WIDTH), rows(C_WIDTH), rows(D_MODEL), row],
        out_shape=[_sds((T, A_WIDTH)), _sds((T, B_WIDTH)), _sds((T, C_WIDTH)), _sds((T, D_MODEL), MXU), _sds((1, D_MODEL))],
        compiler_params=_cp("arbitrary"))(gres, ya, yb, yc, g, w)


FF_BLOCK = D_FF // N_DEV


def _mlp_fwd(h, g, w1, w2):
    T = h.shape[0]
    tm = min(T, 1024)

    def body(h_ref, g_ref, w1_ref, w2_ref, o_ref, r_ref, hn):
        @pl.when(pl.program_id(1) == 0)
        def _():
            x = h_ref[...]
            hn[...] = (x * _rms_stat(x) * g_ref[...]).astype(MXU)
            o_ref[...] = x

        a = jnp.maximum(_mm(hn[...], w1_ref[...]), 0.0)
        r = (a * a).astype(MXU)
        r_ref[...] = r
        o_ref[...] += _mm(r, w2_ref[...])

    return pl.pallas_call(
        body, name="mlp_fwd", grid=(T // tm, N_DEV),
        in_specs=[pl.BlockSpec((tm, D_MODEL), lambda i, j: (i, 0)), pl.BlockSpec((1, D_MODEL), lambda i, j: (0, 0)),
                  pl.BlockSpec((None, D_MODEL, FF_BLOCK), lambda i, j: (j, 0, 0)), pl.BlockSpec((FF_BLOCK, D_MODEL), lambda i, j: (j, 0))],
        out_specs=[pl.BlockSpec((tm, D_MODEL), lambda i, j: (i, 0)), pl.BlockSpec((tm, FF_BLOCK), lambda i, j: (i, j))],
        out_shape=[_sds((T, D_MODEL)), _sds((T, D_FF), MXU)],
        scratch_shapes=[pltpu.VMEM((tm, D_MODEL), MXU)],
        compiler_params=_cp("parallel", "arbitrary"))(h, g, w1, w2)


def _mlp_bwd(gres, h, g, r, w1, w2):
    T = h.shape[0]
    tm = min(T, 1024)

    def body(gr_ref, h_ref, g_ref, r_ref, w1_ref, w2_ref, dh_ref, hn_ref, da_ref, dg_ref, acc):
        i, j = pl.program_id(0), pl.program_id(1)

        @pl.when((i == 0) & (j == 0))
        def _():
            dg_ref[...] = jnp.zeros_like(dg_ref)

        @pl.when(j == 0)
        def _():
            x = h_ref[...]
            hn_ref[...] = (x * _rms_stat(x) * g_ref[...]).astype(MXU)
            acc[...] = jnp.zeros_like(acc)

        da = _mm_nt(gr_ref[...], w2_ref[...]) * (2.0 * jnp.sqrt(r_ref[...].astype(F32)))
        acc[...] += _mm_nt(da, w1_ref[...])
        da_ref[...] = da.astype(MXU)

        @pl.when(j == N_DEV - 1)
        def _():
            x = h_ref[...]
            dx, dg = _rms_bwd(x, _rms_stat(x), g_ref[...], acc[...])
            dh_ref[...] = gr_ref[...] + dx
            dg_ref[...] += dg

    rows = pl.BlockSpec((tm, D_MODEL), lambda i, j: (i, 0))
    row = pl.BlockSpec((1, D_MODEL), lambda i, j: (0, 0))
    ffb = pl.BlockSpec((tm, FF_BLOCK), lambda i, j: (i, j))
    return pl.pallas_call(
        body, name="mlp_bwd", grid=(T // tm, N_DEV),
        in_specs=[rows, rows, row, ffb, pl.BlockSpec((None, D_MODEL, FF_BLOCK), lambda i, j: (j, 0, 0)),
                  pl.BlockSpec((FF_BLOCK, D_MODEL), lambda i, j: (j, 0))],
        out_specs=[rows, rows, ffb, row],
        out_shape=[_sds((T, D_MODEL)), _sds((T, D_MODEL), MXU), _sds((T, D_FF), MXU), _sds((1, D_MODEL))],
        scratch_shapes=[pltpu.VMEM((tm, D_MODEL), F32)],
        compiler_params=_cp("arbitrary", "arbitrary"))(gres, h, g, r, w1, w2)


def _ple_fwd(h, p, l, g, wg, wp):
    T = h.shape[0]
    tm = min(T, 512)

    def body(h_ref, p_ref, g_ref, wg_ref, wp_ref, o_ref):
        x = h_ref[...]
        gate = _sigmoid(_mm(x * _rms_stat(x) * g_ref[...], wg_ref[...]))
        o_ref[...] = x + gate * _mm(p_ref[...], wp_ref[...])

    rows = pl.BlockSpec((tm, D_MODEL), lambda i: (i, 0))
    return pl.pallas_call(
        body, name="ple_fwd", grid=(T // tm,),
        in_specs=[rows, pl.BlockSpec((None, tm, PLE_DIM), lambda i: (l, i, 0)), pl.BlockSpec((1, D_MODEL), lambda i: (0, 0)),
                  pl.BlockSpec((D_MODEL, D_MODEL), lambda i: (0, 0)), pl.BlockSpec((PLE_DIM, D_MODEL), lambda i: (0, 0))],
        out_specs=rows, out_shape=_sds((T, D_MODEL)), compiler_params=_cp("parallel"))(h, p, g, wg, wp)


def _ple_bwd(gres, h, p, l, g, wg, wp):
    T = h.shape[0]
    tm = min(T, 512)

    def body(gr_ref, h_ref, p_ref, g_ref, wg_ref, wp_ref, dh_ref, hn_ref, dgp_ref, de_ref, dg_ref):
        @pl.when(pl.program_id(0) == 0)
        def _():
            dg_ref[...] = jnp.zeros_like(dg_ref)

        x = h_ref[...]
        r = _rms_stat(x)
        gg = g_ref[...]
        hn = x * r * gg
        gate = _sigmoid(_mm(hn, wg_ref[...]))
        e = _mm(p_ref[...], wp_ref[...])
        gr = gr_ref[...]
        dgp = gr * e * gate * (1.0 - gate)
        dx, dg = _rms_bwd(x, r, gg, _mm_nt(dgp, wg_ref[...]))
        dh_ref[...] = gr + dx
        dg_ref[...] += dg
        hn_ref[...] = hn.astype(MXU)
        dgp_ref[...] = dgp.astype(MXU)
        de_ref[...] = (gr * gate).astype(MXU)

    rows = pl.BlockSpec((tm, D_MODEL), lambda i: (i, 0))
    row = pl.BlockSpec((1, D_MODEL), lambda i: (0, 0))
    return pl.pallas_call(
        body, name="ple_bwd", grid=(T // tm,),
        in_specs=[rows, rows, pl.BlockSpec((None, tm, PLE_DIM), lambda i: (l, i, 0)), row,
                  pl.BlockSpec((D_MODEL, D_MODEL), lambda i: (0, 0)), pl.BlockSpec((PLE_DIM, D_MODEL), lambda i: (0, 0))],
        out_specs=[rows, rows, rows, rows, row],
        out_shape=[_sds((T, D_MODEL))] + [_sds((T, D_MODEL), MXU)] * 3 + [_sds((1, D_MODEL))],
        compiler_params=_cp("arbitrary"))(gres, h, p, g, wg, wp)


def _loss_head(h, target):
    T = h.shape[0]
    tm = min(T, 1024)

    def body(h_ref, t_ref, dh_ref, l_ref):
        @pl.when(pl.program_id(0) == 0)
        def _():
            l_ref[...] = jnp.zeros_like(l_ref)

        e = h_ref[...] - t_ref[...]
        dh_ref[...] = e * (1.0 / D_MODEL)
        l_ref[...] += jnp.zeros_like(l_ref) + 0.5 * jnp.sum(jnp.mean(e * e, axis=-1, keepdims=True))

    rows = pl.BlockSpec((tm, D_MODEL), lambda i: (i, 0))
    return pl.pallas_call(
        body, name="loss_head", grid=(T // tm,), in_specs=[rows, rows],
        out_specs=[rows, pl.BlockSpec((8, LANES), lambda i: (0, 0))], out_shape=[_sds((T, D_MODEL)), _sds((8, LANES))],
        compiler_params=_cp("arbitrary"))(h, target)


TN_ROWS = 2048

def _tn(a, b, m, n, *, bm, bn, a_off=0, b_off=0, a_lead=None, b_blocked=False, n_major=False, split=None, dtype=F32, name="tn"):
    T = a.shape[-2]
    tk = min(T, TN_ROWS)
    nk = T // tk
    if b_blocked:
        b_spec = pl.BlockSpec((None, tk, bn), lambda i, j, k: (j, k, 0))
    else:
        b_spec = pl.BlockSpec((tk, bn), lambda i, j, k: (k, b_off + j))
    if a_lead is None:
        a_spec = pl.BlockSpec((tk, bm), lambda i, j, k: (k, a_off + i))
    else:
        a_spec = pl.BlockSpec((None, tk, bm), lambda i, j, k: (a_lead, k, a_off + i))
    assert m % bm == 0 and n % bn == 0 and (not n_major or bm == m) and (split is None or (bn == n and n % split == 0))

    def body(a_ref, b_ref, o_ref, acc):
        k = pl.program_id(2)

        @pl.when(k == 0)
        def _():
            acc[...] = jnp.zeros_like(acc)

        acc[...] += _mm_tn(a_ref[...], b_ref[...])

        @pl.when(k == nk - 1)
        def _():
            if split is None:
                o_ref[...] = acc[...].astype(dtype)
            else:
                for d in range(n // split):
                    o_ref[d] = acc[:, d * split:(d + 1) * split].astype(dtype)

    if split is not None:
        out_spec = pl.BlockSpec((n // split, bm, split), lambda i, j, k: (0, i, 0))
        out_shape = _sds((n // split, m, split), dtype)
    elif n_major:
        out_spec = pl.BlockSpec((None, bm, bn), lambda i, j, k: (j, 0, 0))
        out_shape = _sds((n // bn, m, bn), dtype)
    else:
        out_spec = pl.BlockSpec((bm, bn), lambda i, j, k: (i, j))
        out_shape = _sds((m, n), dtype)
    return pl.pallas_call(
        body, name=name, grid=(m // bm, n // bn, nk),
        in_specs=[a_spec, b_spec],
        out_specs=out_spec, out_shape=out_shape, scratch_shapes=[pltpu.VMEM((bm, bn), F32)],
        compiler_params=_cp("parallel", "parallel", "arbitrary"))(a, b)


def _row_tile(R, C):
    for cand in (512, 256, 128, 64, 32, 16, 8):
        if R % cand == 0 and cand * C * 4 <= 2 ** 20:
            return cand
    return R


def _sum_slots(land):
    S, R, C = land.shape
    tr = _row_tile(R, C)

    def body(l_ref, o_ref):
        acc = l_ref[0].astype(F32)
        for s in range(1, S):
            acc = acc + l_ref[s].astype(F32)
        o_ref[...] = acc

    return pl.pallas_call(
        body, name="sum_slots", grid=(R // tr,), in_specs=[pl.BlockSpec((S, tr, C), lambda i: (0, i, 0))],
        out_specs=pl.BlockSpec((tr, C), lambda i: (i, 0)), out_shape=_sds((R, C)), compiler_params=_cp("parallel"))(land)


def _sum_adamw(land, w, m, v):
    R, C = w.shape
    S = land.shape[0]
    tr = _row_tile(R, C)

    def body(l_ref, w_ref, m_ref, v_ref, g_ref, d_ref, nm_ref, nv_ref):
        gg = l_ref[0].astype(F32)
        for s in range(1, S):
            gg = gg + l_ref[s].astype(F32)
        g_ref[...] = gg
        nm = ADAM_B1 * m_ref[...] + (1.0 - ADAM_B1) * gg
        nv = ADAM_B2 * v_ref[...] + (1.0 - ADAM_B2) * (gg * gg)
        m_hat = nm / (1.0 - ADAM_B1 ** ADAM_STEP)
        v_hat = nv / (1.0 - ADAM_B2 ** ADAM_STEP)
        d_ref[...] = -ADAM_LR * (m_hat / (jnp.sqrt(v_hat) + ADAM_EPS) + ADAM_WD * w_ref[...])
        nm_ref[...] = nm
        nv_ref[...] = nv

    blk = pl.BlockSpec((tr, C), lambda i: (i, 0))
    return pl.pallas_call(
        body, name="sum_adamw", grid=(R // tr,), in_specs=[pl.BlockSpec((S, tr, C), lambda i: (0, i, 0))] + [blk] * 3,
        out_specs=[blk] * 4, out_shape=[_sds((R, C))] * 4, compiler_params=_cp("parallel"))(land, w, m, v)


def _all_to_all(pairs, name):
    n = len(pairs)

    def body(*refs):
        srcs, lands = refs[:n], refs[2 * n:3 * n]
        send, recv, loc = refs[3 * n:]
        x, y, c = lax.axis_index("x"), lax.axis_index("y"), lax.axis_index("c")
        me = 4 * x + 2 * y + c
        own = [pltpu.make_async_copy(pairs[t][2](srcs[t], me, me), pairs[t][3](lands[t], me), loc.at[t]) for t in range(n)]
        for cp in own:
            cp.start()
        sends, recvs = [], []
        for k in range(1, N_DEV):
            px, py, pc = x ^ (k >> 2), y ^ ((k >> 1) & 1), c ^ (k & 1)
            peer = 4 * px + 2 * py + pc
            for t in range(n):
                src = pairs[t][2](srcs[t], me, peer)
                cp = pltpu.make_async_remote_copy(src_ref=src, dst_ref=pairs[t][3](lands[t], me), send_sem=send.at[t, k],
                                                  recv_sem=recv.at[t, k], device_id=(px, py, pc), device_id_type=MESH)
                cp.start()
                sends.append(cp)
                recvs.append(pltpu.make_async_remote_copy(src_ref=src, dst_ref=pairs[t][3](lands[t], peer), send_sem=send.at[t, k],
                                                          recv_sem=recv.at[t, k], device_id=(px, py, pc), device_id_type=MESH))
        for cp in recvs:
            cp.wait_recv()
        for cp in sends:
            cp.wait_send()
        for cp in own:
            cp.wait()

    anyspec = pl.BlockSpec(memory_space=pl.ANY)
    lands = [pr[1] for pr in pairs]
    return pl.pallas_call(
        body, name=name, in_specs=[anyspec] * (2 * n), out_specs=[anyspec] * n,
        out_shape=[_sds(a.shape, a.dtype) for a in lands], input_output_aliases={n + t: t for t in range(n)},
        scratch_shapes=[pltpu.SemaphoreType.DMA((n, N_DEV)), pltpu.SemaphoreType.DMA((n, N_DEV)), pltpu.SemaphoreType.DMA((n,))],
        compiler_params=pltpu.CompilerParams(has_side_effects=True))(*[pr[0] for pr in pairs], *lands)


def _gather_whole(src):
    land = lax.empty((N_DEV,) + src.shape, src.dtype)
    return _all_to_all([(src, land, lambda ref, me, peer: ref, lambda ref, sender: ref.at[sender])], "gather_small_grads")[0]


_HBM = pl.BlockSpec(memory_space=pltpu.HBM)
_SEM = pl.BlockSpec(memory_space=pltpu.SEMAPHORE)
_EFFECT = pltpu.SideEffectType.DATAFLOW_SIDE_EFFECTING


def _peers():
    x, y, c = lax.axis_index("x"), lax.axis_index("y"), lax.axis_index("c")
    out = []
    for k in range(1, N_DEV):
        px, py, pc = x ^ (k >> 2), y ^ ((k >> 1) & 1), c ^ (k & 1)
        out.append((k, (px, py, pc), 4 * px + 2 * py + pc))
    return 4 * x + 2 * y + c, out


def _route(mode, layer):
    if mode == "all":
        return (lambda ref, peer: ref), (lambda ref, sender: ref.at[sender])
    return (lambda ref, peer: ref.at[peer]), (lambda ref, sender: ref.at[sender, layer])


def _split_start(srcs, lands, modes, layer, after, name):
    n = len(srcs)
    routes = [_route(m, layer) for m in modes]

    def body(*refs):
        src, land = refs[:n], refs[n:2 * n]
        send, recv, token = refs[2 * n + 1], refs[2 * n + 2], refs[-1]
        me, peers = _peers()
        for k, dev, peer in peers:
            for t, (src_of, dst_of) in enumerate(routes):
                pltpu.make_async_remote_copy(src_ref=src_of(src[t], peer), dst_ref=dst_of(land[t], me), send_sem=send.at[t * N_DEV + k],
                                             recv_sem=recv.at[t * N_DEV + k], device_id=dev, device_id_type=MESH).start()
        token[...] = jnp.zeros_like(token)

    bufs = list(srcs) + list(lands)
    outs = pl.pallas_call(
        body, name=name,
        out_shape=(pltpu.SemaphoreType.DMA((n * N_DEV,)), pltpu.SemaphoreType.DMA((n * N_DEV,)),
                   *[pltpu.HBM(a.shape, a.dtype) for a in bufs], _sds((8, LANES))),
        in_specs=[_HBM] * (2 * n) + [pl.BlockSpec(memory_space=pl.ANY)],
        out_specs=(_SEM, _SEM, *[_HBM] * (2 * n), pl.BlockSpec(memory_space=pltpu.VMEM)),
        input_output_aliases={i: 2 + i for i in range(2 * n)},
        compiler_params=pltpu.CompilerParams(has_side_effects=_EFFECT),
    )(*[pltpu.with_memory_space_constraint(a, pltpu.HBM) for a in bufs], after)
    return outs[0], outs[1], list(outs[2:2 + n]), list(outs[2 + n:2 + 2 * n]), outs[-1]


def _split_wait(send, recv, srcs, lands, modes, layer, after, name):
    n = len(srcs)
    routes = [_route(m, layer) for m in modes]

    def body(*refs):
        src, land = refs[:n], refs[n:2 * n]
        send_r, recv_r = refs[2 * n], refs[2 * n + 1]
        _, peers = _peers()
        for k, dev, peer in peers:
            for t, (src_of, dst_of) in enumerate(routes):
                cp = pltpu.make_async_remote_copy(src_ref=src_of(src[t], peer), dst_ref=dst_of(land[t], peer), send_sem=send_r.at[t * N_DEV + k],
                                                  recv_sem=recv_r.at[t * N_DEV + k], device_id=dev, device_id_type=MESH)
                cp.wait_send()
                cp.wait_recv()

    bufs = list(srcs) + list(lands)
    outs = pl.pallas_call(
        body, name=name, out_shape=tuple(pltpu.HBM(a.shape, a.dtype) for a in bufs),
        in_specs=[_HBM] * (2 * n) + [_SEM, _SEM, pl.BlockSpec(memory_space=pl.ANY)], out_specs=[_HBM] * (2 * n),
        input_output_aliases={i: i for i in range(2 * n)},
        compiler_params=pltpu.CompilerParams(has_side_effects=_EFFECT),
    )(*bufs, send, recv, after)
    return list(outs[:n]), list(outs[n:])


SHARDED = ("w_in", "glu_w1", "glu_w2", "w_out", "w_ff1", "w_ff2", "w_ple_gate", "w_ple_proj")
SMALL = ("attn_norm_g", "gmlp_ln_g", "gmlp_ln_b", "gmlp_ws", "gmlp_bs", "q_norm_g", "k_norm_g", "sinks", "ssm_a_re", "ssm_a_im",
         "ssm_log_dt", "ssm_b_re", "ssm_b_im", "ssm_c_re", "ssm_c_im", "ssm_d", "mix_out_g", "mlp_norm_g", "ple_norm_g")
WEIGHTS = ("attn_norm_g", "w_in", "gmlp_ln_g", "gmlp_ln_b", "gmlp_ws", "gmlp_bs", "q_norm_g", "k_norm_g", "sinks", "ssm_a_re", "ssm_a_im",
           "ssm_log_dt", "ssm_b_re", "ssm_b_im", "ssm_c_re", "ssm_c_im", "ssm_d", "glu_w1", "glu_w2", "mix_out_g", "w_out", "mlp_norm_g",
           "w_ff1", "w_ff2", "ple_norm_g", "w_ple_gate", "w_ple_proj")
FLAT_COLS = 1024


PACK_TILE_ROWS = 8
PACK_ROWS_MULTIPLE = PACK_TILE_ROWS * N_DEV


def _packed_rows(shape):
    return -(-math.prod(shape) // (PACK_TILE_ROWS * FLAT_COLS)) * PACK_TILE_ROWS


def _pack(arrs, dtype):
    blocks = []
    for a in arrs:
        flat = a.astype(dtype).reshape(-1)
        pad = _packed_rows(a.shape) * FLAT_COLS - flat.shape[0]
        if pad:
            flat = jnp.concatenate([flat, jnp.zeros((pad,), dtype)])
        blocks.append(flat.reshape(-1, FLAT_COLS))
    rows = sum(b.shape[0] for b in blocks)
    if rows % PACK_ROWS_MULTIPLE:
        blocks.append(jnp.zeros((PACK_ROWS_MULTIPLE - rows % PACK_ROWS_MULTIPLE, FLAT_COLS), dtype))
    return jnp.concatenate(blocks, axis=0)


def _unpack(flat, shapes):
    out, r = [], 0
    for s in shapes:
        nr = _packed_rows(s)
        out.append(flat[r:r + nr].reshape(-1)[:math.prod(s)].reshape(s))
        r += nr
    return out


def _col_major(w):
    rows, cols = w.shape
    return w.reshape(rows, N_DEV, cols // N_DEV).transpose(1, 0, 2)


def _from_col_major(s):
    n, rows, cs = s.shape
    return s.transpose(1, 0, 2).reshape(rows, n * cs)


EARLY = ("w_in", "glu_w1", "glu_w2")
LATE = ("w_out", "w_ff1", "w_ff2", "w_ple_gate", "w_ple_proj")
GRADS_MID = ("w_ff1", "w_ff2", "w_ple_gate", "w_ple_proj")
GRADS_END = ("w_in", "glu_w1", "glu_w2", "w_out")


def _layer_fwd(h, p, l, cs, sn, W, late_weights, S, sp):
    z = _inproj_fwd(h, S["attn_norm_g"], W["w_in"])
    ya = _gmlp_fwd(z, S["lng"], S["lnb"], S["gmlp_ws"], S["bsx"])
    yb = _attn_fwd(z, cs, sn, S["qg"], S["kg"], S["sinks"])
    yc, xr, xi = _ssm_fwd(z, sp["bbd"], sp["pwr"], sp["pwi"], sp["cbd"], S["ssm_d"], W["glu_w1"], W["glu_w2"])
    late, token = late_weights(yc)
    W = {**W, **late}
    h1 = _merge_fwd(h, ya, yb, yc, _behind(S["mix_out_g"], token), W["w_out"])
    h2, r = _mlp_fwd(h1, S["mlp_norm_g"], W["w_ff1"], W["w_ff2"])
    h3 = _ple_fwd(h2, p, l, S["ple_norm_g"], W["w_ple_gate"], W["w_ple_proj"])
    return h3, dict(h=h, z=z, ya=ya, yb=yb, yc=yc, xr=xr, xi=xi, h1=h1, r=r, h2=h2), W


def _layer_bwd(g3, p, l, cs, sn, W, S, sp, A, raw, mid_bwd):
    G = {}
    g2, hn3, dgp, de, G["ple_norm_g"] = _ple_bwd(g3, A["h2"], p, l, S["ple_norm_g"], W["w_ple_gate"], W["w_ple_proj"])
    G["w_ple_gate"] = _tn(hn3, dgp, D_MODEL, D_MODEL, bm=1024, bn=1024, dtype=WIRE, name="tn_gate").reshape(N_DEV, -1, D_MODEL)
    G["w_ple_proj"] = _tn(p, de, PLE_DIM, D_MODEL, bm=PLE_DIM, bn=D_MODEL, a_lead=l, split=D_MODEL // N_DEV, dtype=WIRE, name="tn_proj")
    g1, hn, da, G["mlp_norm_g"] = _mlp_bwd(g2, A["h1"], S["mlp_norm_g"], A["r"], W["w_ff1"], W["w_ff2"])
    G["w_ff1"] = _tn(hn, da, D_MODEL, D_FF, bm=D_MODEL, bn=FF_BLOCK, n_major=True, dtype=WIRE, name="tn_ff1")
    G["w_ff2"] = _tn(A["r"], g2, D_FF, D_MODEL, bm=1024, bn=1024, dtype=WIRE, name="tn_ff2").reshape(N_DEV, -1, D_MODEL)
    token = mid_bwd(g1, G)
    dya, dyb, dyc, yn, G["mix_out_g"] = _merge_bwd(g1, A["ya"], A["yb"], A["yc"], _behind(S["mix_out_g"], token), W["w_out"])
    G["w_out"] = _tn(yn, g1, D_MODEL, D_MODEL, bm=1024, bn=1024, dtype=WIRE, name="tn_out").reshape(N_DEV, -1, D_MODEL)
    dzc, y2, da1, da2, dy, arai, dlr, dli, dd = _ssm_bwd(dyc, A["z"], A["xr"], A["xi"], sp["bbd"], sp["pwr"], sp["pwi"], sp["cbd"],
                                                        S["ssm_d"], W["glu_w1"], W["glu_w2"])
    G["glu_w1"] = _tn(y2, da1, C_WIDTH, C_WIDTH, bm=256, bn=256, dtype=WIRE, name="tn_glu1").reshape(N_DEV, -1, C_WIDTH)
    G["glu_w2"] = _tn(y2, da2, C_WIDTH, C_WIDTH, bm=256, bn=256, dtype=WIRE, name="tn_glu2").reshape(N_DEV, -1, C_WIDTH)
    dbbd = _tn(A["z"], arai, C_WIDTH, 2 * N_STATE, bm=256, bn=1024, a_off=5, name="tn_bbd")
    dcr = _tn(dy, A["xr"], C_WIDTH, N_STATE, bm=256, bn=1024, name="tn_cre")
    dci = _tn(dy, A["xi"], C_WIDTH, N_STATE, bm=256, bn=1024, name="tn_cim")
    dare, daim, dldt, dbtr, dbti, dcre, dcim = _ssm_param_bwd(
        raw["are"], raw["aim"], raw["ldt"], raw["are_x"], raw["aim_x"], raw["ldt_x"], raw["btr"], raw["bti"],
        dlr.reshape(C_GROUPS, C_STATE), dli.reshape(C_GROUPS, C_STATE), dbbd, dcr, dci)
    G["ssm_a_re"], G["ssm_a_im"], G["ssm_log_dt"] = dare, daim, dldt[:, 0]
    G["ssm_b_re"] = dbtr.reshape(C_GROUPS, C_GROUP, C_STATE).transpose(0, 2, 1)
    G["ssm_b_im"] = dbti.reshape(C_GROUPS, C_GROUP, C_STATE).transpose(0, 2, 1)
    G["ssm_c_re"] = dcre.reshape(C_GROUPS, C_GROUP, C_STATE)
    G["ssm_c_im"] = dcim.reshape(C_GROUPS, C_GROUP, C_STATE)
    G["ssm_d"] = dd.reshape(C_GROUPS, C_GROUP)
    dzq, dzk, dzv, dqg, dkg, dsk = _attn_bwd(A["z"], cs, sn, S["qg"], S["kg"], S["sinks"], A["yb"], dyb)
    G["q_norm_g"] = dqg[0, :HEAD_DIM] + dqg[0, HEAD_DIM:]
    G["k_norm_g"] = dkg[0, :HEAD_DIM] + dkg[0, HEAD_DIM:]
    G["sinks"] = dsk[:, 0]
    dza, dws, dbs, dlng, dlnb = _gmlp_bwd(A["z"], dya, S["lng"], S["lnb"], S["gmlp_ws"], S["bsx"])
    G["gmlp_ws"] = dws
    G["gmlp_bs"] = dbs[:, :, 0]
    G["gmlp_ln_g"] = dlng.reshape(A_HEADS, 2, HEAD_DIM)[:, 1]
    G["gmlp_ln_b"] = dlnb.reshape(A_HEADS, 2, HEAD_DIM)[:, 1]
    g0, xn, dz, G["attn_norm_g"] = _inproj_bwd(g1, A["h"], S["attn_norm_g"], W["w_in"], dza, dzq, dzk, dzv, dzc)
    G["w_in"] = _tn(xn, dz, D_MODEL, IN_COLS, bm=D_MODEL, bn=IN_COLS, split=IN_COLS // N_DEV, dtype=WIRE, name="tn_in")
    return g0, G


def _small_layouts(P, l):
    def row(a):
        return a.reshape(1, -1)

    zeros = jnp.zeros((A_HEADS, HEAD_DIM), F32)
    S = dict(
        attn_norm_g=row(P["attn_norm_g"][l]), mix_out_g=row(P["mix_out_g"][l]), mlp_norm_g=row(P["mlp_norm_g"][l]),
        ple_norm_g=row(P["ple_norm_g"][l]),
        lng=jnp.stack([zeros, P["gmlp_ln_g"][l]], axis=1).reshape(1, 512),
        lnb=jnp.stack([zeros, P["gmlp_ln_b"][l]], axis=1).reshape(1, 512),
        gmlp_ws=P["gmlp_ws"][l],
        bsx=jnp.broadcast_to(P["gmlp_bs"][l][:, :, None], (A_HEADS, CHUNK, CHUNK)),
        qg=jnp.tile(P["q_norm_g"][l], 2).reshape(1, LANES), kg=jnp.tile(P["k_norm_g"][l], 2).reshape(1, LANES),
        sinks=jnp.broadcast_to(P["sinks"][l][:, None], (8, LANES)),
        ssm_d=row(P["ssm_d"][l]),
    )
    are, aim = P["ssm_a_re"][l], P["ssm_a_im"][l]
    ldt = jnp.broadcast_to(P["ssm_log_dt"][l][:, None], (C_GROUPS, C_STATE))
    raw = dict(
        are=are, aim=aim, ldt=ldt,
        are_x=jnp.repeat(are, C_GROUP, axis=0), aim_x=jnp.repeat(aim, C_GROUP, axis=0), ldt_x=jnp.repeat(ldt, C_GROUP, axis=0),
        btr=P["ssm_b_re"][l].transpose(0, 2, 1).reshape(C_WIDTH, C_STATE), bti=P["ssm_b_im"][l].transpose(0, 2, 1).reshape(C_WIDTH, C_STATE),
        cre=P["ssm_c_re"][l].reshape(C_WIDTH, C_STATE), cim=P["ssm_c_im"][l].reshape(C_WIDTH, C_STATE),
    )
    return S, raw


def _ssm_prep_layer(raw):
    bbd, cbd, pwr, pwi = _ssm_prep(raw["are"].reshape(1, N_STATE), raw["aim"].reshape(1, N_STATE), raw["ldt"].reshape(1, N_STATE),
                                   raw["are_x"], raw["aim_x"], raw["ldt_x"], raw["btr"], raw["bti"], raw["cre"], raw["cim"])
    return dict(bbd=bbd, cbd=cbd, pwr=pwr, pwi=pwi)


def _behind(row, token):
    return row if token is None else row + token[0:1, 0:1]


def _local_step(x, p, positions, target, P, weights_of, mid_bwd, after_bwd):
    inv = 1.0 / (ROPE_THETA ** (jnp.arange(0, HEAD_DIM, 2, dtype=F32) / HEAD_DIM))
    cs, sn = _rope_tables(positions.reshape(-1, 1), jnp.tile(inv, 4).reshape(1, LANES))
    h = x
    acts, smalls, weights = [], [], []
    for l in range(DEPTH):
        W, late_weights, token = weights_of(l, h)
        S, raw = _small_layouts(P, l)
        sp = _ssm_prep_layer(raw)
        h, A, W = _layer_fwd(h, p, l, cs, sn, W, late_weights, {**S, "attn_norm_g": _behind(S["attn_norm_g"], token)}, sp)
        acts.append(A)
        smalls.append((S, raw, sp))
        weights.append(W)
    g, lsum = _loss_head(h, target)
    grads = [None] * DEPTH
    token = None
    for l in reversed(range(DEPTH)):
        S, raw, sp = smalls[l]
        g, grads[l] = _layer_bwd(g, p, l, cs, sn, weights[l], {**S, "ple_norm_g": _behind(S["ple_norm_g"], token)}, sp, acts[l], raw,
                                 functools.partial(mid_bwd, l))
        token = after_bwd(l, g, grads[l])
    return lsum[0, 0], g, grads


def _layer_weights(g):
    layout = dict(
        w_in=_from_col_major, glu_w1=lambda a: a.reshape(C_WIDTH, C_WIDTH), glu_w2=lambda a: a.reshape(C_WIDTH, C_WIDTH),
        w_out=lambda a: a.reshape(D_MODEL, D_MODEL), w_ff1=lambda a: a, w_ff2=lambda a: a.reshape(D_FF, D_MODEL),
        w_ple_gate=lambda a: a.reshape(D_MODEL, D_MODEL), w_ple_proj=_from_col_major)
    return {n: layout[n](a) for n, a in g.items()}


def kernel(x, p, positions, attn_norm_g, w_in, gmlp_ln_g, gmlp_ln_b, gmlp_ws, gmlp_bs, q_norm_g, k_norm_g, sinks, ssm_a_re, ssm_a_im, ssm_log_dt, ssm_b_re, ssm_b_im, ssm_c_re, ssm_c_im, ssm_d, glu_w1, glu_w2, mix_out_g, w_out, mlp_norm_g, w_ff1, w_ff2, ple_norm_g, w_ple_gate, w_ple_proj, loss_target, m_attn_norm_g, m_w_in, m_gmlp_ln_g, m_gmlp_ln_b, m_gmlp_ws, m_gmlp_bs, m_q_norm_g, m_k_norm_g, m_sinks, m_ssm_a_re, m_ssm_a_im, m_ssm_log_dt, m_ssm_b_re, m_ssm_b_im, m_ssm_c_re, m_ssm_c_im, m_ssm_d, m_glu_w1, m_glu_w2, m_mix_out_g, m_w_out, m_mlp_norm_g, m_w_ff1, m_w_ff2, m_ple_norm_g, m_w_ple_gate, m_w_ple_proj, v_attn_norm_g, v_w_in, v_gmlp_ln_g, v_gmlp_ln_b, v_gmlp_ws, v_gmlp_bs, v_q_norm_g, v_k_norm_g, v_sinks, v_ssm_a_re, v_ssm_a_im, v_ssm_log_dt, v_ssm_b_re, v_ssm_b_im, v_ssm_c_re, v_ssm_c_im, v_ssm_d, v_glu_w1, v_glu_w2, v_mix_out_g, v_w_out, v_mlp_norm_g, v_w_ff1, v_w_ff2, v_ple_norm_g, v_w_ple_gate, v_w_ple_proj):
    env = dict(locals())
    P = {n: env[n] for n in WEIGHTS}
    M = {n: env["m_" + n] for n in WEIGHTS}
    V = {n: env["v_" + n] for n in WEIGHTS}
    return _step(x, p, positions, loss_target, P, M, V)


def _step(x, p, positions, loss_target, P, M, V):
    small_shapes = [P[n].shape for n in SMALL]
    me = 4 * lax.axis_index("x") + 2 * lax.axis_index("y") + lax.axis_index("c")
    nothing = jnp.zeros((8, LANES), F32)

    def put(land, own, lead):
        return lax.dynamic_update_slice(land, own.reshape((1,) * len(lead) + own.shape), tuple(lead) + (0,) * own.ndim)

    def gather_start(l, names, after, tag):
        shards = [P[n][l].astype(WIRE) for n in names]
        lands = [lax.empty((N_DEV,) + s.shape, WIRE) for s in shards]
        send, recv, shards, lands, token = _split_start(shards, lands, ["all"] * len(names), 0, after, f"gather_start_{l}{tag}")
        return dict(names=names, send=send, recv=recv, shards=shards, lands=lands, token=token, name=f"gather_wait_{l}{tag}")

    def gather_wait(f, after):
        shards, lands = _split_wait(f["send"], f["recv"], f["shards"], f["lands"], ["all"] * len(f["names"]), 0, after, f["name"])
        return dict(zip(f["names"], [put(ld, sh, (me,)) for sh, ld in zip(shards, lands)]))

    first = gather_start(0, EARLY, nothing, "a")
    flying = {0: (first, gather_start(0, LATE, first["token"], "b"))}

    def weights_of(l, h):
        fa, fb = flying.pop(l)
        got = gather_wait(fa, h)
        if fb is None:
            token = None
            if l + 1 < DEPTH:
                flying[l + 1] = (gather_start(l + 1, SHARDED, got["w_in"], ""), None)
                token = flying[l + 1][0]["token"]
            W = _layer_weights(got)
            return {n: W[n] for n in EARLY}, (lambda after: ({n: W[n] for n in LATE}, None)), token

        def late_weights(after):
            late = gather_wait(fb, after)
            flying[l + 1] = (gather_start(l + 1, SHARDED, late["w_out"], ""), None)
            return _layer_weights(late), flying[l + 1][0]["token"]

        return _layer_weights(got), late_weights, None

    grad_lands = {n: lax.empty((N_DEV,) + P[n].shape, WIRE) for n in SHARDED}
    sent = []

    def scatter_start(l, names, parts, lands, tag):
        send, recv, parts, lands, token = _split_start(parts, lands, ["own"] * len(parts), l, nothing, f"scatter_start_{l}{tag}")
        sent.append(dict(l=l, names=names, send=send, recv=recv, parts=parts, lands=lands, name=f"scatter_wait_{l}{tag}"))
        return token

    def scatter_wait(after):
        f = sent.pop(0)
        parts, lands = _split_wait(f["send"], f["recv"], f["parts"], f["lands"], ["own"] * len(f["parts"]), f["l"], after, f["name"])
        lands = [put(ld, lax.dynamic_index_in_dim(part, me, 0, keepdims=False), (me, f["l"])) for part, ld in zip(parts, lands)]
        return dict(zip(f["names"], lands))

    def mid_bwd(l, g1, G):
        if l > 0:
            return None
        grad_lands.update(scatter_wait(g1))
        return scatter_start(0, GRADS_MID, [G[n] for n in GRADS_MID], [grad_lands[n] for n in GRADS_MID], "a")

    small_land = []

    def after_bwd(l, g, G):
        if l > 0:
            if sent:
                grad_lands.update(scatter_wait(g))
            return scatter_start(l, SHARDED, [G[n] for n in SHARDED], [grad_lands[n] for n in SHARDED], "")
        sflat = _pack([jnp.stack([grads_of[k][n] for k in range(DEPTH)]) for n in SMALL], F32)
        sparts = sflat.reshape(N_DEV, -1, FLAT_COLS)
        small_land.append(sflat.shape)
        return scatter_start(0, GRADS_END + ("small",), [G[n] for n in GRADS_END] + [sparts],
                             [grad_lands[n] for n in GRADS_END] + [lax.empty((N_DEV, 1) + sparts.shape[1:], F32)], "b")

    grads_of = {}

    def after_bwd_recording(l, g, G):
        grads_of[l] = G
        return after_bwd(l, g, G)

    lsum, gx, grads = _local_step(x[0], p[:, 0], positions[0], loss_target[0], P, weights_of, mid_bwd, after_bwd_recording)
    grad_lands.update(scatter_wait(gx))
    last = scatter_wait(gx)
    small_parts = last.pop("small")
    grad_lands.update(last)
    G, delta, new_m, new_v = {}, {}, {}, {}
    for n in SHARDED:
        shp = P[n].shape
        res = _sum_adamw(grad_lands[n].reshape(N_DEV, -1, shp[-1]), *(a.reshape(-1, shp[-1]) for a in (P[n], M[n], V[n])))
        G[n], delta[n], new_m[n], new_v[n] = (a.reshape(shp) for a in res)
    small_sum = _gather_whole(_sum_slots(small_parts[:, 0])).reshape((1,) + small_land[0])
    res = _sum_adamw(small_sum, _pack([P[n] for n in SMALL], F32), _pack([M[n] for n in SMALL], F32), _pack([V[n] for n in SMALL], F32))
    for dst, flat in zip((G, delta, new_m, new_v), res):
        dst.update(zip(SMALL, _unpack(flat, small_shapes)))
    loss = lax.psum(lsum, ("x", "y", "c"))
    return (loss, gx[None], *[G[n] for n in WEIGHTS], *[delta[n] for n in WEIGHTS], *[new_m[n] for n in WEIGHTS], *[new_v[n] for n in WEIGHTS])
```

```python
import functools
import math

import jax
import jax.numpy as jnp
from jax import lax
from jax.experimental import pallas as pl
from jax.experimental.pallas import tpu as pltpu

F32 = jnp.float32
MXU = jnp.bfloat16
WIRE = jnp.bfloat16

D_MODEL = 1024
DEPTH = 4
HEAD_DIM = 64
A_WIDTH = 256
A_HEADS = 4
CHUNK = 128
B_WIDTH = 512
WINDOW = 128
C_WIDTH = 256
C_GROUP = 16
C_GROUPS = 16
C_STATE = 64
N_STATE = C_GROUPS * C_STATE
IN_COLS = 1536
D_FF = 4096
PLE_DIM = 256
EPS = 1e-6
ROPE_THETA = 10000.0
SCALE = HEAD_DIM ** -0.5
NEG = -1e30
N_DEV = 8

ADAM_LR = 0.001
ADAM_B1 = 0.9
ADAM_B2 = 0.999
ADAM_EPS = 1e-08
ADAM_WD = 0.01
ADAM_STEP = 10

V7X_VMEM_BYTES = 64 * 2 ** 20
VMEM_LIMIT = V7X_VMEM_BYTES - 8 * 2 ** 20
LANES = 128

MESH = pl.DeviceIdType.MESH


def _cp(*sem):
    return pltpu.CompilerParams(dimension_semantics=sem, vmem_limit_bytes=VMEM_LIMIT)


def _sds(shape, dtype=F32):
    return jax.ShapeDtypeStruct(shape, dtype)


def _mm(a, b):
    return jnp.dot(a.astype(MXU), b.astype(MXU), preferred_element_type=F32)


def _mm_nt(a, b):
    return lax.dot_general(a.astype(MXU), b.astype(MXU), (((1,), (1,)), ((), ())), preferred_element_type=F32)


def _mm_tn(a, b):
    return lax.dot_general(a.astype(MXU), b.astype(MXU), (((0,), (0,)), ((), ())), preferred_element_type=F32)


def _lane(shape):
    return lax.broadcasted_iota(jnp.int32, shape, len(shape) - 1)


def _row(shape):
    return lax.broadcasted_iota(jnp.int32, shape, 0)


_GELU_C = math.sqrt(2.0 / math.pi)


def _gelu(x):
    return 0.5 * x * (1.0 + jnp.tanh(_GELU_C * (x + 0.044715 * (x * x * x))))


def _gelu_grad(x):
    t = jnp.tanh(_GELU_C * (x + 0.044715 * (x * x * x)))
    return 0.5 * (1.0 + t) + 0.5 * x * (1.0 - t * t) * (_GELU_C * (1.0 + 3.0 * 0.044715 * (x * x)))


def _sigmoid(x):
    return 1.0 / (1.0 + jnp.exp(-x))


def _rms_stat(x):
    return lax.rsqrt(jnp.mean(x * x, axis=-1, keepdims=True) + EPS)


def _rms_bwd(x, r, g, dy):
    xh = x * r
    dxh = dy * g
    dx = r * (dxh - xh * jnp.mean(dxh * xh, axis=-1, keepdims=True))
    return dx, jnp.sum(dy * xh, axis=0, keepdims=True)


def _tril(w):
    return jnp.where(_row(w.shape) >= _lane(w.shape), w, 0.0)


def _swap64(x):
    return pltpu.roll(x, HEAD_DIM, 1)


def _group_sum64(x, lo):
    s_lo = jnp.sum(jnp.where(lo, x, 0.0), axis=-1, keepdims=True)
    s_hi = jnp.sum(jnp.where(lo, 0.0, x), axis=-1, keepdims=True)
    return jnp.where(lo, s_lo, s_hi)


def _partner(x):
    n = x.shape[-1]
    first = (_lane(x.shape) % HEAD_DIM) < HEAD_DIM // 2
    return jnp.where(first, pltpu.roll(x, n - HEAD_DIM // 2, 1), pltpu.roll(x, HEAD_DIM // 2, 1))


def _rope(y, cs, sn):
    return y * cs + _partner(y) * sn


def _rope_bwd(d, cs, sn):
    return d * cs + _partner(d * sn)


def _qk_norm_rope(x, g, cs, sn):
    lo = _lane(x.shape) < HEAD_DIM
    r = lax.rsqrt(_group_sum64(x * x, lo) * (1.0 / HEAD_DIM) + EPS)
    xh = x * r
    return _rope(xh * g, cs, sn), xh, r


def _qk_norm_rope_bwd(xh, r, g, cs, sn, d):
    lo = _lane(xh.shape) < HEAD_DIM
    dy = _rope_bwd(d, cs, sn)
    dxh = dy * g
    m = _group_sum64(dxh * xh, lo) * (1.0 / HEAD_DIM)
    return r * (dxh - xh * m), jnp.sum(dy * xh, axis=0, keepdims=True)


def _gmlp_head(blk, g, b):
    hi = _lane(blk.shape) >= HEAD_DIM
    mu = jnp.sum(jnp.where(hi, blk, 0.0), axis=-1, keepdims=True) * (1.0 / HEAD_DIM)
    xc = jnp.where(hi, blk - mu, 0.0)
    rstd = lax.rsqrt(jnp.sum(xc * xc, axis=-1, keepdims=True) * (1.0 / HEAD_DIM) + EPS)
    vhat = xc * rstd
    return vhat * g + b, vhat, rstd


def _rope_tables(pos_col, inv_row):
    T = pos_col.shape[0]
    tm = min(T, 1024)

    def body(p_ref, inv_ref, cs_ref, sn_ref):
        ang = p_ref[...].astype(F32) * inv_ref[...]
        s = jnp.sin(ang)
        cs_ref[...] = jnp.cos(ang)
        sn_ref[...] = jnp.where((_lane(ang.shape) % HEAD_DIM) < HEAD_DIM // 2, -s, s)

    blk = pl.BlockSpec((tm, LANES), lambda i: (i, 0))
    return pl.pallas_call(
        body, name="rope_tables", grid=(T // tm,),
        in_specs=[pl.BlockSpec((tm, 1), lambda i: (i, 0)), pl.BlockSpec((1, LANES), lambda i: (0, 0))],
        out_specs=[blk, blk], out_shape=[_sds((T, LANES))] * 2, compiler_params=_cp("parallel"))(pos_col, inv_row)


def _inproj_fwd(h, g, w):
    T = h.shape[0]
    tm = min(T, 512)

    def body(h_ref, g_ref, w_ref, z_ref):
        x = h_ref[...]
        z_ref[...] = _mm(x * _rms_stat(x) * g_ref[...], w_ref[...])

    return pl.pallas_call(
        body, name="inproj_fwd", grid=(T // tm,),
        in_specs=[pl.BlockSpec((tm, D_MODEL), lambda i: (i, 0)), pl.BlockSpec((1, D_MODEL), lambda i: (0, 0)),
                  pl.BlockSpec((D_MODEL, IN_COLS), lambda i: (0, 0))],
        out_specs=pl.BlockSpec((tm, IN_COLS), lambda i: (i, 0)), out_shape=_sds((T, IN_COLS)),
        compiler_params=_cp("parallel"))(h, g, w)


def _inproj_bwd(gres, h, g, w, dza, dzq, dzk, dzv, dzc):
    T = h.shape[0]
    tm = min(T, 512)

    def body(gr_ref, h_ref, g_ref, w_ref, a_ref, q_ref, k_ref, v_ref, c_ref, dh_ref, xn_ref, dz_ref, dg_ref):
        @pl.when(pl.program_id(0) == 0)
        def _():
            dg_ref[...] = jnp.zeros_like(dg_ref)

        x = h_ref[...]
        r = _rms_stat(x)
        gg = g_ref[...]
        dz = jnp.concatenate([a_ref[...], q_ref[...], k_ref[...], v_ref[...], c_ref[...]], axis=1)
        dxn = _mm_nt(dz, w_ref[...])
        dx, dg = _rms_bwd(x, r, gg, dxn)
        dh_ref[...] = gr_ref[...] + dx
        dg_ref[...] += dg
        xn_ref[...] = (x * r * gg).astype(MXU)
        dz_ref[...] = dz.astype(MXU)

    def rows(w_):
        return pl.BlockSpec((tm, w_), lambda i: (i, 0))

    row = pl.BlockSpec((1, D_MODEL), lambda i: (0, 0))
    return pl.pallas_call(
        body, name="inproj_bwd", grid=(T // tm,),
        in_specs=[rows(D_MODEL), rows(D_MODEL), row, pl.BlockSpec((D_MODEL, IN_COLS), lambda i: (0, 0)),
                  rows(512), rows(512), rows(128), rows(128), rows(256)],
        out_specs=[rows(D_MODEL), rows(D_MODEL), rows(IN_COLS), row],
        out_shape=[_sds((T, D_MODEL)), _sds((T, D_MODEL), MXU), _sds((T, IN_COLS), MXU), _sds((1, D_MODEL))],
        compiler_params=_cp("arbitrary"))(gres, h, g, w, dza, dzq, dzk, dzv, dzc)


def _gmlp_fwd(z, lng, lnb, ws, bsx):
    T = z.shape[0]
    tm = min(T, 512)
    nc = tm // CHUNK

    def body(z_ref, g_ref, b_ref, w_ref, bs_ref, ya_ref):
        zg = _gelu(z_ref[...])
        lo = _lane((tm, LANES)) < HEAD_DIM
        prods = []
        for hd in range(A_HEADS):
            sl = slice(hd * LANES, (hd + 1) * LANES)
            blk = zg[:, sl]
            vn, _, _ = _gmlp_head(blk, g_ref[:, sl], b_ref[:, sl])
            wm = _tril(w_ref[hd])
            sv = jnp.concatenate([_mm(wm, vn[c * CHUNK:(c + 1) * CHUNK]) + bs_ref[hd] for c in range(nc)], axis=0)
            prods.append(blk * _swap64(sv))
        ya_ref[:, 0:LANES] = jnp.where(lo, prods[0], _swap64(prods[1]))
        ya_ref[:, LANES:2 * LANES] = jnp.where(lo, prods[2], _swap64(prods[3]))

    row = pl.BlockSpec((1, 512), lambda i: (0, 0))
    mat = pl.BlockSpec((A_HEADS, CHUNK, CHUNK), lambda i: (0, 0, 0))
    return pl.pallas_call(
        body, name="gmlp_fwd", grid=(T // tm,),
        in_specs=[pl.BlockSpec((tm, 512), lambda i: (i, 0)), row, row, mat, mat],
        out_specs=pl.BlockSpec((tm, A_WIDTH), lambda i: (i, 0)), out_shape=_sds((T, A_WIDTH)),
        compiler_params=_cp("parallel"))(z, lng, lnb, ws, bsx)


def _gmlp_bwd(z, dya, lng, lnb, ws, bsx):
    T = z.shape[0]
    tm = min(T, 512)
    nc = tm // CHUNK

    def body(z_ref, dya_ref, g_ref, b_ref, w_ref, bs_ref, dza_ref, dw_ref, dbs_ref, dg_ref, db_ref):
        @pl.when(pl.program_id(0) == 0)
        def _():
            dw_ref[...] = jnp.zeros_like(dw_ref)
            dbs_ref[...] = jnp.zeros_like(dbs_ref)
            dg_ref[...] = jnp.zeros_like(dg_ref)
            db_ref[...] = jnp.zeros_like(db_ref)

        za = z_ref[...]
        zg = _gelu(za)
        gp = _gelu_grad(za)
        lo = _lane((tm, LANES)) < HEAD_DIM
        for hd in range(A_HEADS):
            sl = slice(hd * LANES, (hd + 1) * LANES)
            blk = zg[:, sl]
            g = g_ref[:, sl]
            vn, vhat, rstd = _gmlp_head(blk, g, b_ref[:, sl])
            wm = _tril(w_ref[hd])
            pair = dya_ref[:, (hd // 2) * LANES:(hd // 2 + 1) * LANES]
            dy = jnp.where(lo, pair if hd % 2 == 0 else _swap64(pair), 0.0)
            dsv = _swap64(dy * blk)
            svs, dvns = [], []
            dw = jnp.zeros((CHUNK, CHUNK), F32)
            dbs = jnp.zeros((CHUNK, 1), F32)
            for c in range(nc):
                cs = slice(c * CHUNK, (c + 1) * CHUNK)
                svs.append(_mm(wm, vn[cs]) + bs_ref[hd])
                dw = dw + _mm_nt(dsv[cs], vn[cs])
                dbs = dbs + jnp.sum(dsv[cs], axis=-1, keepdims=True)
                dvns.append(_mm_tn(wm, dsv[cs]))
            sv = jnp.concatenate(svs, axis=0)
            dvn = jnp.concatenate(dvns, axis=0)
            dw_ref[hd] += _tril(dw)
            dbs_ref[hd] += jnp.broadcast_to(dbs, (CHUNK, CHUNK))
            dg_ref[:, sl] += jnp.sum(dvn * vhat, axis=0, keepdims=True)
            db_ref[:, sl] += jnp.sum(dvn, axis=0, keepdims=True)
            du = dy * _swap64(sv)
            dvh = dvn * g
            m1 = jnp.sum(dvh, axis=-1, keepdims=True) * (1.0 / HEAD_DIM)
            m2 = jnp.sum(dvh * vhat, axis=-1, keepdims=True) * (1.0 / HEAD_DIM)
            dv = jnp.where(lo, 0.0, rstd * (dvh - m1 - vhat * m2))
            dza_ref[:, sl] = (du + dv) * gp[:, sl]

    row = pl.BlockSpec((1, 512), lambda i: (0, 0))
    mat = pl.BlockSpec((A_HEADS, CHUNK, CHUNK), lambda i: (0, 0, 0))
    return pl.pallas_call(
        body, name="gmlp_bwd", grid=(T // tm,),
        in_specs=[pl.BlockSpec((tm, 512), lambda i: (i, 0)), pl.BlockSpec((tm, A_WIDTH), lambda i: (i, 0)), row, row, mat, mat],
        out_specs=[pl.BlockSpec((tm, 512), lambda i: (i, 0)), mat, mat, row, row],
        out_shape=[_sds((T, 512)), _sds((A_HEADS, CHUNK, CHUNK)), _sds((A_HEADS, CHUNK, CHUNK)), _sds((1, 512)), _sds((1, 512))],
        compiler_params=_cp("arbitrary"))(z, dya, lng, lnb, ws, bsx)


def _attn_specs(T, tq, tile_of):
    nb = tq // WINDOW

    def prev(i):
        return jnp.maximum(tile_of(i) * nb - 1, 0)

    row = pl.BlockSpec((1, LANES), lambda i: (0, 0))
    return [
        pl.BlockSpec((tq, B_WIDTH), lambda i: (tile_of(i), 1)),
        pl.BlockSpec((tq, LANES), lambda i: (tile_of(i), 8)),
        pl.BlockSpec((tq, LANES), lambda i: (tile_of(i), 9)),
        pl.BlockSpec((WINDOW, LANES), lambda i: (prev(i), 8)),
        pl.BlockSpec((WINDOW, LANES), lambda i: (prev(i), 9)),
        pl.BlockSpec((tq, LANES), lambda i: (tile_of(i), 0)),
        pl.BlockSpec((tq, LANES), lambda i: (tile_of(i), 0)),
        pl.BlockSpec((WINDOW, LANES), lambda i: (prev(i), 0)),
        pl.BlockSpec((WINDOW, LANES), lambda i: (prev(i), 0)),
        row, row,
        pl.BlockSpec((8, LANES), lambda i: (0, 0)),
    ]


def _attn_bias(first):
    qi = lax.broadcasted_iota(jnp.int32, (WINDOW, 2 * WINDOW), 0)
    kj = lax.broadcasted_iota(jnp.int32, (WINDOW, 2 * WINDOW), 1)
    diff = qi + WINDOW - kj
    ok = (diff >= 0) & (diff < WINDOW) & ((kj >= WINDOW) | jnp.logical_not(first))
    return jnp.where(ok, 0.0, NEG)


def _dup_heads(x, lo):
    sw = _swap64(x)
    return jnp.where(lo, x, sw), jnp.where(lo, sw, x)


HEADS_PER_KV = 4


def _stack_heads(x0, x1, lo):
    return jnp.concatenate([jnp.where(lo, x0, 0.0), jnp.where(lo, 0.0, x0), jnp.where(lo, x1, 0.0), jnp.where(lo, 0.0, x1)], axis=0)


def _unstack_heads(x4, lo):
    return (jnp.where(lo, x4[0:WINDOW], x4[WINDOW:2 * WINDOW]), jnp.where(lo, x4[2 * WINDOW:3 * WINDOW], x4[3 * WINDOW:4 * WINDOW]))


def _sink_column(sk_ref, g):
    return jnp.concatenate([jnp.broadcast_to(sk_ref[a:a + 1, 0:1], (WINDOW, 1)) for a in range(HEADS_PER_KV * g, HEADS_PER_KV * (g + 1))], axis=0)


def _attn_probs(q4, kw, bias, sink):
    s = _mm_nt(q4, kw)
    s = (s.reshape(HEADS_PER_KV, WINDOW, 2 * WINDOW) + bias[None]).reshape(HEADS_PER_KV * WINDOW, 2 * WINDOW)
    m = jnp.maximum(jnp.max(s, axis=-1, keepdims=True), sink)
    p = jnp.exp(s - m)
    es = jnp.exp(sink - m)
    inv = 1.0 / (jnp.sum(p, axis=-1, keepdims=True) + es)
    return p * inv, es * inv


def _attn_fwd(z, cs, sn, qg, kg, sinks):
    T = z.shape[0]
    tq = min(T, 512)
    nb = tq // WINDOW

    def body(q_ref, k_ref, v_ref, kp_ref, vp_ref, cs_ref, sn_ref, csp_ref, snp_ref, qg_ref, kg_ref, sk_ref, o_ref):
        i = pl.program_id(0)
        csq, snq = cs_ref[...], sn_ref[...]
        cs_all = jnp.concatenate([csp_ref[...], csq], axis=0)
        sn_all = jnp.concatenate([snp_ref[...], snq], axis=0)
        k_all = jnp.concatenate([kp_ref[...], k_ref[...]], axis=0)
        v_all = jnp.concatenate([vp_ref[...], v_ref[...]], axis=0)
        kr, _, _ = _qk_norm_rope(k_all, kg_ref[...], cs_all, sn_all)
        lo_all = _lane(k_all.shape) < HEAD_DIM
        kd = _dup_heads(kr, lo_all)
        vd = _dup_heads(v_all, lo_all)
        lo = _lane((WINDOW, LANES)) < HEAD_DIM
        qrs = [_qk_norm_rope(q_ref[:, pr * LANES:(pr + 1) * LANES], qg_ref[...], csq, snq)[0] * SCALE for pr in range(4)]
        biases = [_attn_bias(i * nb + b == 0) for b in range(nb)]
        for g in range(2):
            sink = _sink_column(sk_ref, g)
            for b in range(nb):
                bs = slice(b * WINDOW, (b + 1) * WINDOW)
                ws = slice(b * WINDOW, (b + 2) * WINDOW)
                pn, _ = _attn_probs(_stack_heads(qrs[2 * g][bs], qrs[2 * g + 1][bs], lo), kd[g][ws], biases[b], sink)
                o0, o1 = _unstack_heads(_mm(pn, vd[g][ws]), lo)
                o_ref[bs, 2 * g * LANES:(2 * g + 1) * LANES] = o0
                o_ref[bs, (2 * g + 1) * LANES:(2 * g + 2) * LANES] = o1

    return pl.pallas_call(
        body, name="attn_fwd", grid=(T // tq,),
        in_specs=_attn_specs(T, tq, lambda i: i),
        out_specs=pl.BlockSpec((tq, B_WIDTH), lambda i: (i, 0)), out_shape=_sds((T, B_WIDTH)),
        compiler_params=_cp("parallel"))(z, z, z, z, z, cs, sn, cs, sn, qg, kg, sinks)


def _attn_bwd(z, cs, sn, qg, kg, sinks, o, do):
    T = z.shape[0]
    tq = min(T, 512)
    nb = tq // WINDOW
    nt = T // tq
    tk = tq + WINDOW

    def tile_of(i):
        return nt - 1 - i

    def body(q_ref, k_ref, v_ref, kp_ref, vp_ref, cs_ref, sn_ref, csp_ref, snp_ref, qg_ref, kg_ref, sk_ref, o_ref, do_ref,
             dq_ref, dk_ref, dv_ref, dqg_ref, dkg_ref, dsk_ref, acck, accv, ck, cv):
        i = pl.program_id(0)
        ti = nt - 1 - i

        @pl.when(i == 0)
        def _():
            dqg_ref[...] = jnp.zeros_like(dqg_ref)
            dkg_ref[...] = jnp.zeros_like(dkg_ref)
            dsk_ref[...] = jnp.zeros_like(dsk_ref)
            ck[...] = jnp.zeros_like(ck)
            cv[...] = jnp.zeros_like(cv)

        csq, snq = cs_ref[...], sn_ref[...]
        cs_all = jnp.concatenate([csp_ref[...], csq], axis=0)
        sn_all = jnp.concatenate([snp_ref[...], snq], axis=0)
        k_all = jnp.concatenate([kp_ref[...], k_ref[...]], axis=0)
        v_all = jnp.concatenate([vp_ref[...], v_ref[...]], axis=0)
        kr, kh, rk = _qk_norm_rope(k_all, kg_ref[...], cs_all, sn_all)
        lo_all = _lane(k_all.shape) < HEAD_DIM
        kd = _dup_heads(kr, lo_all)
        vd = _dup_heads(v_all, lo_all)
        lo = _lane((WINDOW, LANES)) < HEAD_DIM
        acck[...] = jnp.zeros_like(acck)
        accv[...] = jnp.zeros_like(accv)
        prep = [_qk_norm_rope(q_ref[:, pr * LANES:(pr + 1) * LANES], qg_ref[...], csq, snq) for pr in range(4)]
        biases = [_attn_bias(ti * nb + b == 0) for b in range(nb)]
        dqs = [[None] * nb for _ in range(4)]
        for g in range(2):
            sink = _sink_column(sk_ref, g)
            dsink = jnp.zeros((HEADS_PER_KV * WINDOW, 1), F32)
            for b in range(nb):
                bs = slice(b * WINDOW, (b + 1) * WINDOW)
                ws = slice(b * WINDOW, (b + 2) * WINDOW)
                kw, vw = kd[g][ws], vd[g][ws]
                q4 = _stack_heads(prep[2 * g][0][bs] * SCALE, prep[2 * g + 1][0][bs] * SCALE, lo)
                pn, psink = _attn_probs(q4, kw, biases[b], sink)
                o0, o1 = o_ref[bs, 2 * g * LANES:(2 * g + 1) * LANES], o_ref[bs, (2 * g + 1) * LANES:(2 * g + 2) * LANES]
                do4 = _stack_heads(do_ref[bs, 2 * g * LANES:(2 * g + 1) * LANES], do_ref[bs, (2 * g + 1) * LANES:(2 * g + 2) * LANES], lo)
                delta = jnp.sum(do4 * jnp.concatenate([o0, o0, o1, o1], axis=0), axis=-1, keepdims=True)
                ds = pn * (_mm_nt(do4, vw) - delta)
                dsink = dsink - psink * delta
                dqs[2 * g][b], dqs[2 * g + 1][b] = _unstack_heads(_mm(ds, kw) * SCALE, lo)
                acck[g, ws, :] += _mm_tn(ds, q4)
                accv[g, ws, :] += _mm_tn(pn, do4)
            for hh in range(HEADS_PER_KV):
                a = HEADS_PER_KV * g + hh
                dsk_ref[a:a + 1, :] += jnp.zeros((1, LANES), F32) + jnp.sum(dsink[hh * WINDOW:(hh + 1) * WINDOW])
        for pr in range(4):
            _, qh, rq = prep[pr]
            dx, dg = _qk_norm_rope_bwd(qh, rq, qg_ref[...], csq, snq, jnp.concatenate(dqs[pr], axis=0))
            dq_ref[:, pr * LANES:(pr + 1) * LANES] = dx
            dqg_ref[...] += dg

        def fold(acc):
            f0 = acc[0] + _swap64(acc[0])
            f1 = acc[1] + _swap64(acc[1])
            return jnp.where(lo_all, f0, f1)

        dk_all = fold(acck)
        dv_all = fold(accv)
        pad = jnp.zeros((tq - WINDOW, LANES), F32)
        dk_own = dk_all[WINDOW:] + (jnp.concatenate([pad, ck[...]], axis=0) if nb > 1 else ck[...])
        dv_own = dv_all[WINDOW:] + (jnp.concatenate([pad, cv[...]], axis=0) if nb > 1 else cv[...])
        ck[...] = dk_all[:WINDOW]
        cv[...] = dv_all[:WINDOW]
        dxk, dgk = _qk_norm_rope_bwd(kh[WINDOW:], rk[WINDOW:], kg_ref[...], csq, snq, dk_own)
        dk_ref[...] = dxk
        dkg_ref[...] += dgk
        dv_ref[...] = dv_own

    row = pl.BlockSpec((1, LANES), lambda i: (0, 0))
    return pl.pallas_call(
        body, name="attn_bwd", grid=(nt,),
        in_specs=_attn_specs(T, tq, tile_of) + [pl.BlockSpec((tq, B_WIDTH), lambda i: (tile_of(i), 0))] * 2,
        out_specs=[pl.BlockSpec((tq, B_WIDTH), lambda i: (tile_of(i), 0)), pl.BlockSpec((tq, LANES), lambda i: (tile_of(i), 0)),
                   pl.BlockSpec((tq, LANES), lambda i: (tile_of(i), 0)), row, row, pl.BlockSpec((8, LANES), lambda i: (0, 0))],
        out_shape=[_sds((T, B_WIDTH)), _sds((T, LANES)), _sds((T, LANES)), _sds((1, LANES)), _sds((1, LANES)), _sds((8, LANES))],
        scratch_shapes=[pltpu.VMEM((2, tk, LANES), F32), pltpu.VMEM((2, tk, LANES), F32),
                        pltpu.VMEM((WINDOW, LANES), F32), pltpu.VMEM((WINDOW, LANES), F32)],
        compiler_params=_cp("arbitrary"))(z, z, z, z, z, cs, sn, cs, sn, qg, kg, sinks, o, do)


def _bbar_t(are, aim, ldt, btr, bti):
    lbr, lbi = _lam_bar(are, aim, ldt)
    den = are * are + aim * aim
    nr = lbr - 1.0
    cr = (nr * are + lbi * aim) / den
    ci = (lbi * are - nr * aim) / den
    return cr * btr - ci * bti, cr * bti + ci * btr


def _lam_bar(are, aim, ldt):
    dt = jnp.exp(ldt)
    er = jnp.exp(are * dt)
    return er * jnp.cos(aim * dt), er * jnp.sin(aim * dt)


def _block_diag(x):
    t = jnp.concatenate([x] * C_GROUPS, axis=1)
    return jnp.where(_row(t.shape) // C_GROUP == _lane(t.shape) // C_STATE, t, 0.0)


def _block_diag_fold(m):
    rg = _row((C_WIDTH, C_STATE)) // C_GROUP
    acc = jnp.zeros((C_WIDTH, C_STATE), F32)
    for g in range(C_GROUPS):
        acc = acc + jnp.where(rg == g, m[:, g * C_STATE:(g + 1) * C_STATE], 0.0)
    return acc


def _ssm_prep(are, aim, ldt, are_x, aim_x, ldt_x, btr, bti, cre, cim):
    def body(are_r, aim_r, ldt_r, arex_r, aimx_r, ldtx_r, btr_r, bti_r, cre_r, cim_r, bbd_ref, cbd_ref, pwr_ref, pwi_ref):
        lr, li = _lam_bar(are_r[...], aim_r[...], ldt_r[...])
        cr, ci = lr, li
        for r in range(SCAN_SEG):
            pwr_ref[r:r + 1, :] = cr
            pwi_ref[r:r + 1, :] = ci
            cr, ci = cr * lr - ci * li, cr * li + ci * lr
        br, bi = _bbar_t(arex_r[...], aimx_r[...], ldtx_r[...], btr_r[...], bti_r[...])
        bbd_ref[...] = jnp.concatenate([_block_diag(br), _block_diag(bi)], axis=1).astype(MXU)
        cbd_ref[...] = jnp.concatenate([_block_diag(cre_r[...]), -_block_diag(cim_r[...])], axis=1).astype(MXU)

    return pl.pallas_call(
        body, name="ssm_prep",
        out_shape=[_sds((C_WIDTH, 2 * N_STATE), MXU), _sds((C_WIDTH, 2 * N_STATE), MXU), _sds((SCAN_SEG, N_STATE)), _sds((SCAN_SEG, N_STATE))],
        compiler_params=pltpu.CompilerParams(vmem_limit_bytes=VMEM_LIMIT))(are, aim, ldt, are_x, aim_x, ldt_x, btr, bti, cre, cim)


def _ssm_param_bwd(are, aim, ldt, are_x, aim_x, ldt_x, btr, bti, dlr, dli, dbbd, dcr, dci):
    def body(are_r, aim_r, ldt_r, arex_r, aimx_r, ldtx_r, btr_r, bti_r, dlr_r, dli_r, dbbd_r, dcr_r, dci_r,
             dare_ref, daim_ref, dldt_ref, dbtr_ref, dbti_ref, dcre_ref, dcim_ref):
        _, vjp_l = jax.vjp(_lam_bar, are_r[...], aim_r[...], ldt_r[...])
        da1, di1, dl1 = vjp_l((dlr_r[...], dli_r[...]))
        dbr = _block_diag_fold(dbbd_r[:, 0:N_STATE])
        dbi = _block_diag_fold(dbbd_r[:, N_STATE:2 * N_STATE])
        _, vjp_b = jax.vjp(_bbar_t, arex_r[...], aimx_r[...], ldtx_r[...], btr_r[...], bti_r[...])
        da2, di2, dl2, dbtr, dbti = vjp_b((dbr, dbi))

        def gsum(x):
            return x.reshape(C_GROUPS, C_GROUP, C_STATE).sum(axis=1)

        dare_ref[...] = da1 + gsum(da2)
        daim_ref[...] = di1 + gsum(di2)
        dldt_ref[...] = jnp.broadcast_to(jnp.sum(dl1 + gsum(dl2), axis=-1, keepdims=True), (C_GROUPS, LANES))
        dbtr_ref[...] = dbtr
        dbti_ref[...] = dbti
        dcre_ref[...] = _block_diag_fold(dcr_r[...])
        dcim_ref[...] = -_block_diag_fold(dci_r[...])

    g = _sds((C_GROUPS, C_STATE))
    x = _sds((C_WIDTH, C_STATE))
    return pl.pallas_call(
        body, name="ssm_param_bwd", out_shape=[g, g, _sds((C_GROUPS, LANES)), x, x, x, x],
        compiler_params=pltpu.CompilerParams(vmem_limit_bytes=VMEM_LIMIT))(are, aim, ldt, are_x, aim_x, ldt_x, btr, bti, dlr, dli, dbbd, dcr, dci)


SCAN_TILE = 256
SCAN_SEG = 8


def _scan_tables(pwr_ref, pwi_ref, conj, reverse):
    row = _row((SCAN_SEG, N_STATE))
    shifts = []
    for k in (1, 2, 4):
        keep = (row < SCAN_SEG - k) if reverse else (row >= k)
        ar = jnp.broadcast_to(pwr_ref[k - 1:k, :], (SCAN_SEG, N_STATE))
        ai = jnp.broadcast_to(pwi_ref[k - 1:k, :], (SCAN_SEG, N_STATE)) * conj
        shifts.append((SCAN_SEG - k if reverse else k, jnp.where(keep, ar, 0.0), jnp.where(keep, ai, 0.0)))
    if reverse:
        pr = jnp.concatenate([pwr_ref[SCAN_SEG - 1 - r:SCAN_SEG - r, :] for r in range(SCAN_SEG)], axis=0)
        pi = jnp.concatenate([pwi_ref[SCAN_SEG - 1 - r:SCAN_SEG - r, :] for r in range(SCAN_SEG)], axis=0) * conj
    else:
        pr, pi = pwr_ref[...], pwi_ref[...] * conj
    return shifts, (pr, pi)


def _tile_scan(xr_ref, xi_ref, pwr_ref, pwi_ref, sr, si, conj, reverse):
    shifts, (pr, pi) = _scan_tables(pwr_ref, pwi_ref, conj, reverse)
    groups = xr_ref.shape[0] // SCAN_SEG
    out_row = 0 if reverse else SCAN_SEG - 1

    def step(k, c):
        cr, ci = c
        g = groups - 1 - k if reverse else k
        rows = pl.ds(pl.multiple_of(g * SCAN_SEG, SCAN_SEG), SCAN_SEG)
        xr, xi = xr_ref[rows, :], xi_ref[rows, :]
        for amount, ar, ai in shifts:
            qr, qi = pltpu.roll(xr, amount, 0), pltpu.roll(xi, amount, 0)
            xr, xi = xr + ar * qr - ai * qi, xi + ar * qi + ai * qr
        xr, xi = xr + pr * cr - pi * ci, xi + pr * ci + pi * cr
        xr_ref[rows, :] = xr
        xi_ref[rows, :] = xi
        return xr[out_row:out_row + 1], xi[out_row:out_row + 1]

    cr, ci = lax.fori_loop(0, groups, step, (sr[...], si[...]), unroll=2)
    sr[...] = cr
    si[...] = ci


def _ssm_fwd(z, bbd, pwr, pwi, cbd, dsk, w1, w2):
    T = z.shape[0]
    tt = min(T, SCAN_TILE)

    def body(u_ref, bbd_ref, pwr_ref, pwi_ref, cbd_ref, d_ref, w1_ref, w2_ref, yc_ref, xr_ref, xi_ref, sr, si):
        @pl.when(pl.program_id(0) == 0)
        def _():
            sr[...] = jnp.zeros_like(sr)
            si[...] = jnp.zeros_like(si)

        u = u_ref[...]
        bu = _mm(u, bbd_ref[...])
        xr_ref[...] = bu[:, 0:N_STATE]
        xi_ref[...] = bu[:, N_STATE:2 * N_STATE]
        _tile_scan(xr_ref, xi_ref, pwr_ref, pwi_ref, sr, si, 1.0, False)
        x = jnp.concatenate([xr_ref[...], xi_ref[...]], axis=1)
        y2 = _gelu(_mm_nt(x, cbd_ref[...]) + d_ref[...] * u)
        yc_ref[...] = _mm(y2, w1_ref[...]) * _sigmoid(_mm(y2, w2_ref[...]))

    big = pl.BlockSpec((C_WIDTH, 2 * N_STATE), lambda i: (0, 0))
    tab = pl.BlockSpec((SCAN_SEG, N_STATE), lambda i: (0, 0))
    wsp = pl.BlockSpec((C_WIDTH, C_WIDTH), lambda i: (0, 0))
    xs = pl.BlockSpec((tt, N_STATE), lambda i: (i, 0))
    return pl.pallas_call(
        body, name="ssm_fwd", grid=(T // tt,),
        in_specs=[pl.BlockSpec((tt, C_WIDTH), lambda i: (i, 5)), big, tab, tab, big, pl.BlockSpec((1, C_WIDTH), lambda i: (0, 0)), wsp, wsp],
        out_specs=[pl.BlockSpec((tt, C_WIDTH), lambda i: (i, 0)), xs, xs],
        out_shape=[_sds((T, C_WIDTH)), _sds((T, N_STATE)), _sds((T, N_STATE))],
        scratch_shapes=[pltpu.VMEM((1, N_STATE), F32)] * 2,
        compiler_params=_cp("arbitrary"))(z, bbd, pwr, pwi, cbd, dsk, w1, w2)


def _ssm_bwd(dyc, z, xr, xi, bbd, pwr, pwi, cbd, dsk, w1, w2):
    T = z.shape[0]
    tt = min(T, SCAN_TILE)
    nt = T // tt

    def tile_of(i):
        return nt - 1 - i

    def body(dyc_ref, u_ref, xr_ref, xi_ref, xpr_ref, xpi_ref, bbd_ref, pwr_ref, pwi_ref, cbd_ref, d_ref, w1_ref, w2_ref,
             du_ref, y2_ref, da1_ref, da2_ref, dy_ref, arai_ref, dlr_ref, dli_ref, dd_ref, gr, gi, sr, si):
        i = pl.program_id(0)
        ti = nt - 1 - i

        @pl.when(i == 0)
        def _():
            sr[...] = jnp.zeros_like(sr)
            si[...] = jnp.zeros_like(si)
            dlr_ref[...] = jnp.zeros_like(dlr_ref)
            dli_ref[...] = jnp.zeros_like(dli_ref)
            dd_ref[...] = jnp.zeros_like(dd_ref)

        u = u_ref[...]
        xr_t, xi_t = xr_ref[...], xi_ref[...]
        y = _mm_nt(jnp.concatenate([xr_t, xi_t], axis=1), cbd_ref[...]) + d_ref[...] * u
        y2 = _gelu(y)
        a1 = _mm(y2, w1_ref[...])
        sg = _sigmoid(_mm(y2, w2_ref[...]))
        dyc_t = dyc_ref[...]
        da1 = dyc_t * sg
        da2 = dyc_t * a1 * sg * (1.0 - sg)
        dy = (_mm_nt(da1, w1_ref[...]) + _mm_nt(da2, w2_ref[...])) * _gelu_grad(y)
        gx = _mm(dy, cbd_ref[...])
        gr[...] = gx[:, 0:N_STATE]
        gi[...] = gx[:, N_STATE:2 * N_STATE]
        _tile_scan(gr, gi, pwr_ref, pwi_ref, sr, si, -1.0, True)
        ar, ai = gr[...], gi[...]
        first_row = _row(ar.shape) == 0
        live = jnp.where(ti > 0, 1.0, 0.0)
        xsr = jnp.where(first_row, xpr_ref[7:8, :] * live, pltpu.roll(xr_t, 1, 0))
        xsi = jnp.where(first_row, xpi_ref[7:8, :] * live, pltpu.roll(xi_t, 1, 0))
        dlr_ref[...] += jnp.sum(ar * xsr + ai * xsi, axis=0, keepdims=True)
        dli_ref[...] += jnp.sum(ai * xsr - ar * xsi, axis=0, keepdims=True)
        dd_ref[...] += jnp.sum(dy * u, axis=0, keepdims=True)
        arai = jnp.concatenate([ar, ai], axis=1)
        du_ref[...] = _mm_nt(arai, bbd_ref[...]) + d_ref[...] * dy
        y2_ref[...] = y2.astype(MXU)
        da1_ref[...] = da1.astype(MXU)
        da2_ref[...] = da2.astype(MXU)
        dy_ref[...] = dy.astype(MXU)
        arai_ref[...] = arai.astype(MXU)

    def prev(i):
        return jnp.maximum(tile_of(i) * (tt // 8) - 1, 0)

    big = pl.BlockSpec((C_WIDTH, 2 * N_STATE), lambda i: (0, 0))
    tab = pl.BlockSpec((SCAN_SEG, N_STATE), lambda i: (0, 0))
    srow = pl.BlockSpec((1, N_STATE), lambda i: (0, 0))
    wsp = pl.BlockSpec((C_WIDTH, C_WIDTH), lambda i: (0, 0))
    xs = pl.BlockSpec((tt, N_STATE), lambda i: (tile_of(i), 0))
    xp = pl.BlockSpec((8, N_STATE), lambda i: (prev(i), 0))
    cw = pl.BlockSpec((tt, C_WIDTH), lambda i: (tile_of(i), 0))
    drow = pl.BlockSpec((1, C_WIDTH), lambda i: (0, 0))
    return pl.pallas_call(
        body, name="ssm_bwd", grid=(nt,),
        in_specs=[cw, pl.BlockSpec((tt, C_WIDTH), lambda i: (tile_of(i), 5)), xs, xs, xp, xp, big, tab, tab, big, drow, wsp, wsp],
        out_specs=[cw, cw, cw, cw, cw, pl.BlockSpec((tt, 2 * N_STATE), lambda i: (tile_of(i), 0)), srow, srow, drow],
        out_shape=[_sds((T, C_WIDTH))] + [_sds((T, C_WIDTH), MXU)] * 4 + [_sds((T, 2 * N_STATE), MXU), _sds((1, N_STATE)), _sds((1, N_STATE)), _sds((1, C_WIDTH))],
        scratch_shapes=[pltpu.VMEM((tt, N_STATE), F32)] * 2 + [pltpu.VMEM((1, N_STATE), F32)] * 2,
        compiler_params=_cp("arbitrary"))(dyc, z, xr, xi, xr, xi, bbd, pwr, pwi, cbd, dsk, w1, w2)


_GROUPS = ((0, A_WIDTH), (A_WIDTH, A_WIDTH + B_WIDTH), (A_WIDTH + B_WIDTH, D_MODEL))


def _merge_fwd(h, ya, yb, yc, g, w):
    T = h.shape[0]
    tm = min(T, 512)

    def body(h_ref, a_ref, b_ref, c_ref, g_ref, w_ref, o_ref):
        yn = jnp.concatenate([y * _rms_stat(y) for y in (a_ref[...], b_ref[...], c_ref[...])], axis=1) * g_ref[...]
        o_ref[...] = h_ref[...] + _mm(yn, w_ref[...])

    def rows(w_):
        return pl.BlockSpec((tm, w_), lambda i: (i, 0))

    return pl.pallas_call(
        body, name="merge_fwd", grid=(T // tm,),
        in_specs=[rows(D_MODEL), rows(A_WIDTH), rows(B_WIDTH), rows(C_WIDTH), pl.BlockSpec((1, D_MODEL), lambda i: (0, 0)),
                  pl.BlockSpec((D_MODEL, D_MODEL), lambda i: (0, 0))],
        out_specs=rows(D_MODEL), out_shape=_sds((T, D_MODEL)), compiler_params=_cp("parallel"))(h, ya, yb, yc, g, w)


def _merge_bwd(gres, ya, yb, yc, g, w):
    T = gres.shape[0]
    tm = min(T, 512)

    def body(gr_ref, a_ref, b_ref, c_ref, g_ref, w_ref, da_ref, db_ref, dc_ref, yn_ref, dg_ref):
        @pl.when(pl.program_id(0) == 0)
        def _():
            dg_ref[...] = jnp.zeros_like(dg_ref)

        dyn = _mm_nt(gr_ref[...], w_ref[...])
        yns, dgs = [], []
        for (c0, c1), y_ref, d_ref in zip(_GROUPS, (a_ref, b_ref, c_ref), (da_ref, db_ref, dc_ref)):
            y = y_ref[...]
            r = _rms_stat(y)
            gg = g_ref[:, c0:c1]
            dx, dg = _rms_bwd(y, r, gg, dyn[:, c0:c1])
            d_ref[...] = dx
            dgs.append(dg)
            yns.append(y * r * gg)
        dg_ref[...] += jnp.concatenate(dgs, axis=1)
        yn_ref[...] = jnp.concatenate(yns, axis=1).astype(MXU)

    def rows(w_):
        return pl.BlockSpec((tm, w_), lambda i: (i, 0))

    row = pl.BlockSpec((1, D_MODEL), lambda i: (0, 0))
    return pl.pallas_call(
        body, name="merge_bwd", grid=(T // tm,),
        in_specs=[rows(D_MODEL), rows(A_WIDTH), rows(B_WIDTH), rows(C_WIDTH), row, pl.BlockSpec((D_MODEL, D_MODEL), lambda i: (0, 0))],
        out_specs=[rows(A_WIDTH), rows(B_WIDTH), rows(C_WIDTH), rows(D_MODEL), row],
        out_shape=[_sds((T, A_WIDTH)), _sds((T, B_WIDTH)), _sds((T, C_WIDTH)), _sds((T, D_MODEL), MXU), _sds((1, D_MODEL))],
        compiler_params=_cp("arbitrary"))(gres, ya, yb, yc, g, w)


FF_BLOCK = D_FF // N_DEV


def _load_weights(w1_hbm, w2_hbm, w1, w2, sem):
    @pl.when(pl.program_id(0) == 0)
    def _():
        c1 = pltpu.make_async_copy(w1_hbm, w1, sem.at[0])
        c2 = pltpu.make_async_copy(w2_hbm, w2, sem.at[1])
        c1.start()
        c2.start()
        c1.wait()
        c2.wait()


def _mlp_weight_scratch():
    return [pltpu.VMEM((D_MODEL, D_FF), MXU), pltpu.VMEM((D_FF, D_MODEL), MXU), pltpu.SemaphoreType.DMA((2,))]


def _mlp_fwd(h, g, w1, w2):
    T = h.shape[0]
    tm = min(T, 512)

    def body(h_ref, g_ref, w1_hbm, w2_hbm, o_ref, r_ref, w1_v, w2_v, sem):
        _load_weights(w1_hbm, w2_hbm, w1_v, w2_v, sem)
        x = h_ref[...]
        a = jnp.maximum(_mm(x * _rms_stat(x) * g_ref[...], w1_v[...]), 0.0)
        r = (a * a).astype(MXU)
        r_ref[...] = r
        o_ref[...] = x + _mm(r, w2_v[...])

    rows = pl.BlockSpec((tm, D_MODEL), lambda i: (i, 0))
    hbm = pl.BlockSpec(memory_space=pl.ANY)
    return pl.pallas_call(
        body, name="mlp_fwd", grid=(T // tm,),
        in_specs=[rows, pl.BlockSpec((1, D_MODEL), lambda i: (0, 0)), hbm, hbm],
        out_specs=[rows, pl.BlockSpec((tm, D_FF), lambda i: (i, 0))],
        out_shape=[_sds((T, D_MODEL)), _sds((T, D_FF), MXU)],
        scratch_shapes=_mlp_weight_scratch(), compiler_params=_cp("arbitrary"))(h, g, w1, w2)


def _mlp_bwd(gres, h, g, r, w1, w2):
    T = h.shape[0]
    tm = min(T, 256)

    def body(gr_ref, h_ref, g_ref, r_ref, w1_hbm, w2_hbm, dh_ref, hn_ref, da_ref, dg_ref, w1_v, w2_v, sem):
        _load_weights(w1_hbm, w2_hbm, w1_v, w2_v, sem)

        @pl.when(pl.program_id(0) == 0)
        def _():
            dg_ref[...] = jnp.zeros_like(dg_ref)

        gr = gr_ref[...]
        da = (_mm_nt(gr, w2_v[...]) * (2.0 * jnp.sqrt(r_ref[...].astype(F32)))).astype(MXU)
        da_ref[...] = da
        x = h_ref[...]
        rs = _rms_stat(x)
        gg = g_ref[...]
        dx, dg = _rms_bwd(x, rs, gg, _mm_nt(da, w1_v[...]))
        dh_ref[...] = gr + dx
        dg_ref[...] += dg
        hn_ref[...] = (x * rs * gg).astype(MXU)

    rows = pl.BlockSpec((tm, D_MODEL), lambda i: (i, 0))
    row = pl.BlockSpec((1, D_MODEL), lambda i: (0, 0))
    wide = pl.BlockSpec((tm, D_FF), lambda i: (i, 0))
    hbm = pl.BlockSpec(memory_space=pl.ANY)
    return pl.pallas_call(
        body, name="mlp_bwd", grid=(T // tm,),
        in_specs=[rows, rows, row, wide, hbm, hbm], out_specs=[rows, rows, wide, row],
        out_shape=[_sds((T, D_MODEL)), _sds((T, D_MODEL), MXU), _sds((T, D_FF), MXU), _sds((1, D_MODEL))],
        scratch_shapes=_mlp_weight_scratch(), compiler_params=_cp("arbitrary"))(gres, h, g, r, w1, w2)


def _ple_fwd(h, p, l, g, wg, wp):
    T = h.shape[0]
    tm = min(T, 512)

    def body(h_ref, p_ref, g_ref, wg_ref, wp_ref, o_ref):
        x = h_ref[...]
        gate = _sigmoid(_mm(x * _rms_stat(x) * g_ref[...], wg_ref[...]))
        o_ref[...] = x + gate * _mm(p_ref[...], wp_ref[...])

    rows = pl.BlockSpec((tm, D_MODEL), lambda i: (i, 0))
    return pl.pallas_call(
        body, name="ple_fwd", grid=(T // tm,),
        in_specs=[rows, pl.BlockSpec((None, tm, PLE_DIM), lambda i: (l, i, 0)), pl.BlockSpec((1, D_MODEL), lambda i: (0, 0)),
                  pl.BlockSpec((D_MODEL, D_MODEL), lambda i: (0, 0)), pl.BlockSpec((PLE_DIM, D_MODEL), lambda i: (0, 0))],
        out_specs=rows, out_shape=_sds((T, D_MODEL)), compiler_params=_cp("parallel"))(h, p, g, wg, wp)


def _ple_bwd(gres, h, p, l, g, wg, wp):
    T = h.shape[0]
    tm = min(T, 512)

    def body(gr_ref, h_ref, p_ref, g_ref, wg_ref, wp_ref, dh_ref, hn_ref, dgp_ref, de_ref, dg_ref):
        @pl.when(pl.program_id(0) == 0)
        def _():
            dg_ref[...] = jnp.zeros_like(dg_ref)

        x = h_ref[...]
        r = _rms_stat(x)
        gg = g_ref[...]
        hn = x * r * gg
        gate = _sigmoid(_mm(hn, wg_ref[...]))
        e = _mm(p_ref[...], wp_ref[...])
        gr = gr_ref[...]
        dgp = gr * e * gate * (1.0 - gate)
        dx, dg = _rms_bwd(x, r, gg, _mm_nt(dgp, wg_ref[...]))
        dh_ref[...] = gr + dx
        dg_ref[...] += dg
        hn_ref[...] = hn.astype(MXU)
        dgp_ref[...] = dgp.astype(MXU)
        de_ref[...] = (gr * gate).astype(MXU)

    rows = pl.BlockSpec((tm, D_MODEL), lambda i: (i, 0))
    row = pl.BlockSpec((1, D_MODEL), lambda i: (0, 0))
    return pl.pallas_call(
        body, name="ple_bwd", grid=(T // tm,),
        in_specs=[rows, rows, pl.BlockSpec((None, tm, PLE_DIM), lambda i: (l, i, 0)), row,
                  pl.BlockSpec((D_MODEL, D_MODEL), lambda i: (0, 0)), pl.BlockSpec((PLE_DIM, D_MODEL), lambda i: (0, 0))],
        out_specs=[rows, rows, rows, rows, row],
        out_shape=[_sds((T, D_MODEL))] + [_sds((T, D_MODEL), MXU)] * 3 + [_sds((1, D_MODEL))],
        compiler_params=_cp("arbitrary"))(gres, h, p, g, wg, wp)


def _loss_head(h, target):
    T = h.shape[0]
    tm = min(T, 1024)

    def body(h_ref, t_ref, dh_ref, l_ref):
        @pl.when(pl.program_id(0) == 0)
        def _():
            l_ref[...] = jnp.zeros_like(l_ref)

        e = h_ref[...] - t_ref[...]
        dh_ref[...] = e * (1.0 / D_MODEL)
        l_ref[...] += jnp.zeros_like(l_ref) + 0.5 * jnp.sum(jnp.mean(e * e, axis=-1, keepdims=True))

    rows = pl.BlockSpec((tm, D_MODEL), lambda i: (i, 0))
    return pl.pallas_call(
        body, name="loss_head", grid=(T // tm,), in_specs=[rows, rows],
        out_specs=[rows, pl.BlockSpec((8, LANES), lambda i: (0, 0))], out_shape=[_sds((T, D_MODEL)), _sds((8, LANES))],
        compiler_params=_cp("arbitrary"))(h, target)


TN_ROWS = 2048

def _tn(a, b, m, n, *, bm, bn, a_off=0, b_off=0, a_lead=None, b_blocked=False, n_major=False, split=None, dtype=F32, name="tn"):
    T = a.shape[-2]
    tk = min(T, TN_ROWS)
    nk = T // tk
    if b_blocked:
        b_spec = pl.BlockSpec((None, tk, bn), lambda i, j, k: (j, k, 0))
    else:
        b_spec = pl.BlockSpec((tk, bn), lambda i, j, k: (k, b_off + j))
    if a_lead is None:
        a_spec = pl.BlockSpec((tk, bm), lambda i, j, k: (k, a_off + i))
    else:
        a_spec = pl.BlockSpec((None, tk, bm), lambda i, j, k: (a_lead, k, a_off + i))
    assert m % bm == 0 and n % bn == 0 and (not n_major or bm == m) and (split is None or (bn == n and n % split == 0))

    def body(a_ref, b_ref, o_ref, acc):
        k = pl.program_id(2)

        @pl.when(k == 0)
        def _():
            acc[...] = jnp.zeros_like(acc)

        acc[...] += _mm_tn(a_ref[...], b_ref[...])

        @pl.when(k == nk - 1)
        def _():
            if split is None:
                o_ref[...] = acc[...].astype(dtype)
            else:
                for d in range(n // split):
                    o_ref[d] = acc[:, d * split:(d + 1) * split].astype(dtype)

    if split is not None:
        out_spec = pl.BlockSpec((n // split, bm, split), lambda i, j, k: (0, i, 0))
        out_shape = _sds((n // split, m, split), dtype)
    elif n_major:
        out_spec = pl.BlockSpec((None, bm, bn), lambda i, j, k: (j, 0, 0))
        out_shape = _sds((n // bn, m, bn), dtype)
    else:
        out_spec = pl.BlockSpec((bm, bn), lambda i, j, k: (i, j))
        out_shape = _sds((m, n), dtype)
    return pl.pallas_call(
        body, name=name, grid=(m // bm, n // bn, nk),
        in_specs=[a_spec, b_spec],
        out_specs=out_spec, out_shape=out_shape, scratch_shapes=[pltpu.VMEM((bm, bn), F32)],
        compiler_params=_cp("parallel", "parallel", "arbitrary"))(a, b)


def _row_tile(R, C):
    for cand in (512, 256, 128, 64, 32, 16, 8):
        if R % cand == 0 and cand * C * 4 <= 2 ** 20:
            return cand
    return R


def _sum_slots(land):
    S, R, C = land.shape
    tr = _row_tile(R, C)

    def body(l_ref, o_ref):
        acc = l_ref[0].astype(F32)
        for s in range(1, S):
            acc = acc + l_ref[s].astype(F32)
        o_ref[...] = acc

    return pl.pallas_call(
        body, name="sum_slots", grid=(R // tr,), in_specs=[pl.BlockSpec((S, tr, C), lambda i: (0, i, 0))],
        out_specs=pl.BlockSpec((tr, C), lambda i: (i, 0)), out_shape=_sds((R, C)), compiler_params=_cp("parallel"))(land)


def _sum_adamw(land, w, m, v):
    R, C = w.shape
    S = land.shape[0]
    tr = _row_tile(R, C)

    def body(l_ref, w_ref, m_ref, v_ref, g_ref, d_ref, nm_ref, nv_ref):
        gg = l_ref[0].astype(F32)
        for s in range(1, S):
            gg = gg + l_ref[s].astype(F32)
        g_ref[...] = gg
        nm = ADAM_B1 * m_ref[...] + (1.0 - ADAM_B1) * gg
        nv = ADAM_B2 * v_ref[...] + (1.0 - ADAM_B2) * (gg * gg)
        m_hat = nm / (1.0 - ADAM_B1 ** ADAM_STEP)
        v_hat = nv / (1.0 - ADAM_B2 ** ADAM_STEP)
        d_ref[...] = -ADAM_LR * (m_hat / (jnp.sqrt(v_hat) + ADAM_EPS) + ADAM_WD * w_ref[...])
        nm_ref[...] = nm
        nv_ref[...] = nv

    blk = pl.BlockSpec((tr, C), lambda i: (i, 0))
    return pl.pallas_call(
        body, name="sum_adamw", grid=(R // tr,), in_specs=[pl.BlockSpec((S, tr, C), lambda i: (0, i, 0))] + [blk] * 3,
        out_specs=[blk] * 4, out_shape=[_sds((R, C))] * 4, compiler_params=_cp("parallel"))(land, w, m, v)


def _all_to_all(pairs, name):
    n = len(pairs)

    def body(*refs):
        srcs, lands = refs[:n], refs[2 * n:3 * n]
        send, recv, loc = refs[3 * n:]
        x, y, c = lax.axis_index("x"), lax.axis_index("y"), lax.axis_index("c")
        me = 4 * x + 2 * y + c
        own = [pltpu.make_async_copy(pairs[t][2](srcs[t], me, me), pairs[t][3](lands[t], me), loc.at[t]) for t in range(n)]
        for cp in own:
            cp.start()
        sends, recvs = [], []
        for k in range(1, N_DEV):
            px, py, pc = x ^ (k >> 2), y ^ ((k >> 1) & 1), c ^ (k & 1)
            peer = 4 * px + 2 * py + pc
            for t in range(n):
                src = pairs[t][2](srcs[t], me, peer)
                cp = pltpu.make_async_remote_copy(src_ref=src, dst_ref=pairs[t][3](lands[t], me), send_sem=send.at[t, k],
                                                  recv_sem=recv.at[t, k], device_id=(px, py, pc), device_id_type=MESH)
                cp.start()
                sends.append(cp)
                recvs.append(pltpu.make_async_remote_copy(src_ref=src, dst_ref=pairs[t][3](lands[t], peer), send_sem=send.at[t, k],
                                                          recv_sem=recv.at[t, k], device_id=(px, py, pc), device_id_type=MESH))
        for cp in recvs:
            cp.wait_recv()
        for cp in sends:
            cp.wait_send()
        for cp in own:
            cp.wait()

    anyspec = pl.BlockSpec(memory_space=pl.ANY)
    lands = [pr[1] for pr in pairs]
    return pl.pallas_call(
        body, name=name, in_specs=[anyspec] * (2 * n), out_specs=[anyspec] * n,
        out_shape=[_sds(a.shape, a.dtype) for a in lands], input_output_aliases={n + t: t for t in range(n)},
        scratch_shapes=[pltpu.SemaphoreType.DMA((n, N_DEV)), pltpu.SemaphoreType.DMA((n, N_DEV)), pltpu.SemaphoreType.DMA((n,))],
        compiler_params=pltpu.CompilerParams(has_side_effects=True))(*[pr[0] for pr in pairs], *lands)


def _gather_whole(src):
    land = lax.empty((N_DEV,) + src.shape, src.dtype)
    return _all_to_all([(src, land, lambda ref, me, peer: ref, lambda ref, sender: ref.at[sender])], "gather_small_grads")[0]


_HBM = pl.BlockSpec(memory_space=pltpu.HBM)
_SEM = pl.BlockSpec(memory_space=pltpu.SEMAPHORE)
_EFFECT = pltpu.SideEffectType.DATAFLOW_SIDE_EFFECTING


def _peers():
    x, y, c = lax.axis_index("x"), lax.axis_index("y"), lax.axis_index("c")
    out = []
    for k in range(1, N_DEV):
        px, py, pc = x ^ (k >> 2), y ^ ((k >> 1) & 1), c ^ (k & 1)
        out.append((k, (px, py, pc), 4 * px + 2 * py + pc))
    return 4 * x + 2 * y + c, out


def _route(mode, layer):
    if mode == "all":
        return (lambda ref, peer: ref), (lambda ref, sender: ref.at[sender])
    return (lambda ref, peer: ref.at[peer]), (lambda ref, sender: ref.at[sender, layer])


def _split_start(srcs, lands, modes, layer, after, name):
    n = len(srcs)
    routes = [_route(m, layer) for m in modes]

    def body(*refs):
        src, land = refs[:n], refs[n:2 * n]
        send, recv, token = refs[2 * n + 1], refs[2 * n + 2], refs[-1]
        me, peers = _peers()
        for k, dev, peer in peers:
            for t, (src_of, dst_of) in enumerate(routes):
                pltpu.make_async_remote_copy(src_ref=src_of(src[t], peer), dst_ref=dst_of(land[t], me), send_sem=send.at[t * N_DEV + k],
                                             recv_sem=recv.at[t * N_DEV + k], device_id=dev, device_id_type=MESH).start()
        token[...] = jnp.zeros_like(token)

    bufs = list(srcs) + list(lands)
    outs = pl.pallas_call(
        body, name=name,
        out_shape=(pltpu.SemaphoreType.DMA((n * N_DEV,)), pltpu.SemaphoreType.DMA((n * N_DEV,)),
                   *[pltpu.HBM(a.shape, a.dtype) for a in bufs], _sds((8, LANES))),
        in_specs=[_HBM] * (2 * n) + [pl.BlockSpec(memory_space=pl.ANY)],
        out_specs=(_SEM, _SEM, *[_HBM] * (2 * n), pl.BlockSpec(memory_space=pltpu.VMEM)),
        input_output_aliases={i: 2 + i for i in range(2 * n)},
        compiler_params=pltpu.CompilerParams(has_side_effects=_EFFECT),
    )(*[pltpu.with_memory_space_constraint(a, pltpu.HBM) for a in bufs], after)
    return outs[0], outs[1], list(outs[2:2 + n]), list(outs[2 + n:2 + 2 * n]), outs[-1]


def _split_wait(send, recv, srcs, lands, modes, layer, after, name):
    n = len(srcs)
    routes = [_route(m, layer) for m in modes]

    def body(*refs):
        src, land = refs[:n], refs[n:2 * n]
        send_r, recv_r = refs[2 * n], refs[2 * n + 1]
        _, peers = _peers()
        for k, dev, peer in peers:
            for t, (src_of, dst_of) in enumerate(routes):
                cp = pltpu.make_async_remote_copy(src_ref=src_of(src[t], peer), dst_ref=dst_of(land[t], peer), send_sem=send_r.at[t * N_DEV + k],
                                                  recv_sem=recv_r.at[t * N_DEV + k], device_id=dev, device_id_type=MESH)
                cp.wait_send()
                cp.wait_recv()

    bufs = list(srcs) + list(lands)
    outs = pl.pallas_call(
        body, name=name, out_shape=tuple(pltpu.HBM(a.shape, a.dtype) for a in bufs),
        in_specs=[_HBM] * (2 * n) + [_SEM, _SEM, pl.BlockSpec(memory_space=pl.ANY)], out_specs=[_HBM] * (2 * n),
        input_output_aliases={i: i for i in range(2 * n)},
        compiler_params=pltpu.CompilerParams(has_side_effects=_EFFECT),
    )(*bufs, send, recv, after)
    return list(outs[:n]), list(outs[n:])


SHARDED = ("w_in", "glu_w1", "glu_w2", "w_out", "w_ff1", "w_ff2", "w_ple_gate", "w_ple_proj")
SMALL = ("attn_norm_g", "gmlp_ln_g", "gmlp_ln_b", "gmlp_ws", "gmlp_bs", "q_norm_g", "k_norm_g", "sinks", "ssm_a_re", "ssm_a_im",
         "ssm_log_dt", "ssm_b_re", "ssm_b_im", "ssm_c_re", "ssm_c_im", "ssm_d", "mix_out_g", "mlp_norm_g", "ple_norm_g")
WEIGHTS = ("attn_norm_g", "w_in", "gmlp_ln_g", "gmlp_ln_b", "gmlp_ws", "gmlp_bs", "q_norm_g", "k_norm_g", "sinks", "ssm_a_re", "ssm_a_im",
           "ssm_log_dt", "ssm_b_re", "ssm_b_im", "ssm_c_re", "ssm_c_im", "ssm_d", "glu_w1", "glu_w2", "mix_out_g", "w_out", "mlp_norm_g",
           "w_ff1", "w_ff2", "ple_norm_g", "w_ple_gate", "w_ple_proj")
FLAT_COLS = 1024


PACK_TILE_ROWS = 8
PACK_ROWS_MULTIPLE = PACK_TILE_ROWS * N_DEV


def _packed_rows(shape):
    return -(-math.prod(shape) // (PACK_TILE_ROWS * FLAT_COLS)) * PACK_TILE_ROWS


def _pack(arrs, dtype):
    blocks = []
    for a in arrs:
        flat = a.astype(dtype).reshape(-1)
        pad = _packed_rows(a.shape) * FLAT_COLS - flat.shape[0]
        if pad:
            flat = jnp.concatenate([flat, jnp.zeros((pad,), dtype)])
        blocks.append(flat.reshape(-1, FLAT_COLS))
    rows = sum(b.shape[0] for b in blocks)
    if rows % PACK_ROWS_MULTIPLE:
        blocks.append(jnp.zeros((PACK_ROWS_MULTIPLE - rows % PACK_ROWS_MULTIPLE, FLAT_COLS), dtype))
    return jnp.concatenate(blocks, axis=0)


def _unpack(flat, shapes):
    out, r = [], 0
    for s in shapes:
        nr = _packed_rows(s)
        out.append(flat[r:r + nr].reshape(-1)[:math.prod(s)].reshape(s))
        r += nr
    return out


def _col_major(w):
    rows, cols = w.shape
    return w.reshape(rows, N_DEV, cols // N_DEV).transpose(1, 0, 2)


def _from_col_major(s):
    n, rows, cs = s.shape
    return s.transpose(1, 0, 2).reshape(rows, n * cs)


EARLY = ("w_in", "glu_w1", "glu_w2")
LATE = ("w_out", "w_ff1", "w_ff2", "w_ple_gate", "w_ple_proj")
GRADS_MID = ("w_ff1", "w_ff2", "w_ple_gate", "w_ple_proj")
GRADS_END = ("w_in", "glu_w1", "glu_w2", "w_out")


def _layer_fwd(h, p, l, cs, sn, W, late_weights, S, sp):
    z = _inproj_fwd(h, S["attn_norm_g"], W["w_in"])
    ya = _gmlp_fwd(z, S["lng"], S["lnb"], S["gmlp_ws"], S["bsx"])
    yb = _attn_fwd(z, cs, sn, S["qg"], S["kg"], S["sinks"])
    yc, xr, xi = _ssm_fwd(z, sp["bbd"], sp["pwr"], sp["pwi"], sp["cbd"], S["ssm_d"], W["glu_w1"], W["glu_w2"])
    late, token = late_weights(yc)
    W = {**W, **late}
    h1 = _merge_fwd(h, ya, yb, yc, _behind(S["mix_out_g"], token), W["w_out"])
    h2, r = _mlp_fwd(h1, S["mlp_norm_g"], W["w_ff1"], W["w_ff2"])
    h3 = _ple_fwd(h2, p, l, S["ple_norm_g"], W["w_ple_gate"], W["w_ple_proj"])
    return h3, dict(h=h, z=z, ya=ya, yb=yb, yc=yc, xr=xr, xi=xi, h1=h1, r=r, h2=h2), W


def _layer_bwd(g3, p, l, cs, sn, W, S, sp, A, raw, mid_bwd):
    G = {}
    g2, hn3, dgp, de, G["ple_norm_g"] = _ple_bwd(g3, A["h2"], p, l, S["ple_norm_g"], W["w_ple_gate"], W["w_ple_proj"])
    G["w_ple_gate"] = _tn(hn3, dgp, D_MODEL, D_MODEL, bm=1024, bn=1024, dtype=WIRE, name="tn_gate").reshape(N_DEV, -1, D_MODEL)
    G["w_ple_proj"] = _tn(p, de, PLE_DIM, D_MODEL, bm=PLE_DIM, bn=D_MODEL, a_lead=l, split=D_MODEL // N_DEV, dtype=WIRE, name="tn_proj")
    g1, hn, da, G["mlp_norm_g"] = _mlp_bwd(g2, A["h1"], S["mlp_norm_g"], A["r"], W["w_ff1"], W["w_ff2"])
    G["w_ff1"] = _tn(hn, da, D_MODEL, D_FF, bm=D_MODEL, bn=FF_BLOCK, n_major=True, dtype=WIRE, name="tn_ff1")
    G["w_ff2"] = _tn(A["r"], g2, D_FF, D_MODEL, bm=1024, bn=1024, dtype=WIRE, name="tn_ff2").reshape(N_DEV, -1, D_MODEL)
    token = mid_bwd(g1, G)
    dya, dyb, dyc, yn, G["mix_out_g"] = _merge_bwd(g1, A["ya"], A["yb"], A["yc"], _behind(S["mix_out_g"], token), W["w_out"])
    G["w_out"] = _tn(yn, g1, D_MODEL, D_MODEL, bm=1024, bn=1024, dtype=WIRE, name="tn_out").reshape(N_DEV, -1, D_MODEL)
    dzc, y2, da1, da2, dy, arai, dlr, dli, dd = _ssm_bwd(dyc, A["z"], A["xr"], A["xi"], sp["bbd"], sp["pwr"], sp["pwi"], sp["cbd"],
                                                        S["ssm_d"], W["glu_w1"], W["glu_w2"])
    G["glu_w1"] = _tn(y2, da1, C_WIDTH, C_WIDTH, bm=256, bn=256, dtype=WIRE, name="tn_glu1").reshape(N_DEV, -1, C_WIDTH)
    G["glu_w2"] = _tn(y2, da2, C_WIDTH, C_WIDTH, bm=256, bn=256, dtype=WIRE, name="tn_glu2").reshape(N_DEV, -1, C_WIDTH)
    dbbd = _tn(A["z"], arai, C_WIDTH, 2 * N_STATE, bm=256, bn=1024, a_off=5, name="tn_bbd")
    dcr = _tn(dy, A["xr"], C_WIDTH, N_STATE, bm=256, bn=1024, name="tn_cre")
    dci = _tn(dy, A["xi"], C_WIDTH, N_STATE, bm=256, bn=1024, name="tn_cim")
    dare, daim, dldt, dbtr, dbti, dcre, dcim = _ssm_param_bwd(
        raw["are"], raw["aim"], raw["ldt"], raw["are_x"], raw["aim_x"], raw["ldt_x"], raw["btr"], raw["bti"],
        dlr.reshape(C_GROUPS, C_STATE), dli.reshape(C_GROUPS, C_STATE), dbbd, dcr, dci)
    G["ssm_a_re"], G["ssm_a_im"], G["ssm_log_dt"] = dare, daim, dldt[:, 0]
    G["ssm_b_re"] = dbtr.reshape(C_GROUPS, C_GROUP, C_STATE).transpose(0, 2, 1)
    G["ssm_b_im"] = dbti.reshape(C_GROUPS, C_GROUP, C_STATE).transpose(0, 2, 1)
    G["ssm_c_re"] = dcre.reshape(C_GROUPS, C_GROUP, C_STATE)
    G["ssm_c_im"] = dcim.reshape(C_GROUPS, C_GROUP, C_STATE)
    G["ssm_d"] = dd.reshape(C_GROUPS, C_GROUP)
    dzq, dzk, dzv, dqg, dkg, dsk = _attn_bwd(A["z"], cs, sn, S["qg"], S["kg"], S["sinks"], A["yb"], dyb)
    G["q_norm_g"] = dqg[0, :HEAD_DIM] + dqg[0, HEAD_DIM:]
    G["k_norm_g"] = dkg[0, :HEAD_DIM] + dkg[0, HEAD_DIM:]
    G["sinks"] = dsk[:, 0]
    dza, dws, dbs, dlng, dlnb = _gmlp_bwd(A["z"], dya, S["lng"], S["lnb"], S["gmlp_ws"], S["bsx"])
    G["gmlp_ws"] = dws
    G["gmlp_bs"] = dbs[:, :, 0]
    G["gmlp_ln_g"] = dlng.reshape(A_HEADS, 2, HEAD_DIM)[:, 1]
    G["gmlp_ln_b"] = dlnb.reshape(A_HEADS, 2, HEAD_DIM)[:, 1]
    g0, xn, dz, G["attn_norm_g"] = _inproj_bwd(g1, A["h"], S["attn_norm_g"], W["w_in"], dza, dzq, dzk, dzv, dzc)
    G["w_in"] = _tn(xn, dz, D_MODEL, IN_COLS, bm=D_MODEL, bn=IN_COLS, split=IN_COLS // N_DEV, dtype=WIRE, name="tn_in")
    return g0, G


def _small_layouts(P, l):
    def row(a):
        return a.reshape(1, -1)

    zeros = jnp.zeros((A_HEADS, HEAD_DIM), F32)
    S = dict(
        attn_norm_g=row(P["attn_norm_g"][l]), mix_out_g=row(P["mix_out_g"][l]), mlp_norm_g=row(P["mlp_norm_g"][l]),
        ple_norm_g=row(P["ple_norm_g"][l]),
        lng=jnp.stack([zeros, P["gmlp_ln_g"][l]], axis=1).reshape(1, 512),
        lnb=jnp.stack([zeros, P["gmlp_ln_b"][l]], axis=1).reshape(1, 512),
        gmlp_ws=P["gmlp_ws"][l],
        bsx=jnp.broadcast_to(P["gmlp_bs"][l][:, :, None], (A_HEADS, CHUNK, CHUNK)),
        qg=jnp.tile(P["q_norm_g"][l], 2).reshape(1, LANES), kg=jnp.tile(P["k_norm_g"][l], 2).reshape(1, LANES),
        sinks=jnp.broadcast_to(P["sinks"][l][:, None], (8, LANES)),
        ssm_d=row(P["ssm_d"][l]),
    )
    are, aim = P["ssm_a_re"][l], P["ssm_a_im"][l]
    ldt = jnp.broadcast_to(P["ssm_log_dt"][l][:, None], (C_GROUPS, C_STATE))
    raw = dict(
        are=are, aim=aim, ldt=ldt,
        are_x=jnp.repeat(are, C_GROUP, axis=0), aim_x=jnp.repeat(aim, C_GROUP, axis=0), ldt_x=jnp.repeat(ldt, C_GROUP, axis=0),
        btr=P["ssm_b_re"][l].transpose(0, 2, 1).reshape(C_WIDTH, C_STATE), bti=P["ssm_b_im"][l].transpose(0, 2, 1).reshape(C_WIDTH, C_STATE),
        cre=P["ssm_c_re"][l].reshape(C_WIDTH, C_STATE), cim=P["ssm_c_im"][l].reshape(C_WIDTH, C_STATE),
    )
    return S, raw


def _ssm_prep_layer(raw):
    bbd, cbd, pwr, pwi = _ssm_prep(raw["are"].reshape(1, N_STATE), raw["aim"].reshape(1, N_STATE), raw["ldt"].reshape(1, N_STATE),
                                   raw["are_x"], raw["aim_x"], raw["ldt_x"], raw["btr"], raw["bti"], raw["cre"], raw["cim"])
    return dict(bbd=bbd, cbd=cbd, pwr=pwr, pwi=pwi)


def _behind(row, token):
    return row if token is None else row + token[0:1, 0:1]


def _local_step(x, p, positions, target, P, weights_of, mid_bwd, after_bwd):
    inv = 1.0 / (ROPE_THETA ** (jnp.arange(0, HEAD_DIM, 2, dtype=F32) / HEAD_DIM))
    cs, sn = _rope_tables(positions.reshape(-1, 1), jnp.tile(inv, 4).reshape(1, LANES))
    h = x
    acts, smalls, weights = [], [], []
    for l in range(DEPTH):
        W, late_weights, token = weights_of(l, h)
        S, raw = _small_layouts(P, l)
        sp = _ssm_prep_layer(raw)
        h, A, W = _layer_fwd(h, p, l, cs, sn, W, late_weights, {**S, "attn_norm_g": _behind(S["attn_norm_g"], token)}, sp)
        acts.append(A)
        smalls.append((S, raw, sp))
        weights.append(W)
    g, lsum = _loss_head(h, target)
    grads = [None] * DEPTH
    token = None
    for l in reversed(range(DEPTH)):
        S, raw, sp = smalls[l]
        g, grads[l] = _layer_bwd(g, p, l, cs, sn, weights[l], {**S, "ple_norm_g": _behind(S["ple_norm_g"], token)}, sp, acts[l], raw,
                                 functools.partial(mid_bwd, l))
        token = after_bwd(l, g, grads[l])
    return lsum[0, 0], g, grads


def _layer_weights(g):
    layout = dict(
        w_in=_from_col_major, glu_w1=lambda a: a.reshape(C_WIDTH, C_WIDTH), glu_w2=lambda a: a.reshape(C_WIDTH, C_WIDTH),
        w_out=lambda a: a.reshape(D_MODEL, D_MODEL), w_ff1=_from_col_major, w_ff2=lambda a: a.reshape(D_FF, D_MODEL),
        w_ple_gate=lambda a: a.reshape(D_MODEL, D_MODEL), w_ple_proj=_from_col_major)
    return {n: layout[n](a) for n, a in g.items()}


def kernel(x, p, positions, attn_norm_g, w_in, gmlp_ln_g, gmlp_ln_b, gmlp_ws, gmlp_bs, q_norm_g, k_norm_g, sinks, ssm_a_re, ssm_a_im, ssm_log_dt, ssm_b_re, ssm_b_im, ssm_c_re, ssm_c_im, ssm_d, glu_w1, glu_w2, mix_out_g, w_out, mlp_norm_g, w_ff1, w_ff2, ple_norm_g, w_ple_gate, w_ple_proj, loss_target, m_attn_norm_g, m_w_in, m_gmlp_ln_g, m_gmlp_ln_b, m_gmlp_ws, m_gmlp_bs, m_q_norm_g, m_k_norm_g, m_sinks, m_ssm_a_re, m_ssm_a_im, m_ssm_log_dt, m_ssm_b_re, m_ssm_b_im, m_ssm_c_re, m_ssm_c_im, m_ssm_d, m_glu_w1, m_glu_w2, m_mix_out_g, m_w_out, m_mlp_norm_g, m_w_ff1, m_w_ff2, m_ple_norm_g, m_w_ple_gate, m_w_ple_proj, v_attn_norm_g, v_w_in, v_gmlp_ln_g, v_gmlp_ln_b, v_gmlp_ws, v_gmlp_bs, v_q_norm_g, v_k_norm_g, v_sinks, v_ssm_a_re, v_ssm_a_im, v_ssm_log_dt, v_ssm_b_re, v_ssm_b_im, v_ssm_c_re, v_ssm_c_im, v_ssm_d, v_glu_w1, v_glu_w2, v_mix_out_g, v_w_out, v_mlp_norm_g, v_w_ff1, v_w_ff2, v_ple_norm_g, v_w_ple_gate, v_w_ple_proj):
    env = dict(locals())
    P = {n: env[n] for n in WEIGHTS}
    M = {n: env["m_" + n] for n in WEIGHTS}
    V = {n: env["v_" + n] for n in WEIGHTS}
    return _step(x, p, positions, loss_target, P, M, V)


def _step(x, p, positions, loss_target, P, M, V):
    small_shapes = [P[n].shape for n in SMALL]
    me = 4 * lax.axis_index("x") + 2 * lax.axis_index("y") + lax.axis_index("c")
    nothing = jnp.zeros((8, LANES), F32)

    def put(land, own, lead):
        return lax.dynamic_update_slice(land, own.reshape((1,) * len(lead) + own.shape), tuple(lead) + (0,) * own.ndim)

    def gather_start(l, names, after, tag):
        shards = [P[n][l].astype(WIRE) for n in names]
        lands = [lax.empty((N_DEV,) + s.shape, WIRE) for s in shards]
        send, recv, shards, lands, token = _split_start(shards, lands, ["all"] * len(names), 0, after, f"gather_start_{l}{tag}")
        return dict(names=names, send=send, recv=recv, shards=shards, lands=lands, token=token, name=f"gather_wait_{l}{tag}")

    def gather_wait(f, after):
        shards, lands = _split_wait(f["send"], f["recv"], f["shards"], f["lands"], ["all"] * len(f["names"]), 0, after, f["name"])
        return dict(zip(f["names"], [put(ld, sh, (me,)) for sh, ld in zip(shards, lands)]))

    first = gather_start(0, EARLY, nothing, "a")
    flying = {0: (first, gather_start(0, LATE, first["token"], "b"))}

    def weights_of(l, h):
        fa, fb = flying.pop(l)
        got = gather_wait(fa, h)
        if fb is None:
            token = None
            if l + 1 < DEPTH:
                flying[l + 1] = (gather_start(l + 1, SHARDED, got["w_in"], ""), None)
                token = flying[l + 1][0]["token"]
            W = _layer_weights(got)
            return {n: W[n] for n in EARLY}, (lambda after: ({n: W[n] for n in LATE}, None)), token

        def late_weights(after):
            late = gather_wait(fb, after)
            flying[l + 1] = (gather_start(l + 1, SHARDED, late["w_out"], ""), None)
            return _layer_weights(late), flying[l + 1][0]["token"]

        return _layer_weights(got), late_weights, None

    grad_lands = {n: lax.empty((N_DEV,) + P[n].shape, WIRE) for n in SHARDED}
    sent = []

    def scatter_start(l, names, parts, lands, tag):
        send, recv, parts, lands, token = _split_start(parts, lands, ["own"] * len(parts), l, nothing, f"scatter_start_{l}{tag}")
        sent.append(dict(l=l, names=names, send=send, recv=recv, parts=parts, lands=lands, name=f"scatter_wait_{l}{tag}"))
        return token

    def scatter_wait(after):
        f = sent.pop(0)
        parts, lands = _split_wait(f["send"], f["recv"], f["parts"], f["lands"], ["own"] * len(f["parts"]), f["l"], after, f["name"])
        lands = [put(ld, lax.dynamic_index_in_dim(part, me, 0, keepdims=False), (me, f["l"])) for part, ld in zip(parts, lands)]
        return dict(zip(f["names"], lands))

    def mid_bwd(l, g1, G):
        if l > 0:
            return None
        grad_lands.update(scatter_wait(g1))
        return scatter_start(0, GRADS_MID, [G[n] for n in GRADS_MID], [grad_lands[n] for n in GRADS_MID], "a")

    small_land = []

    def after_bwd(l, g, G):
        if l > 0:
            if sent:
                grad_lands.update(scatter_wait(g))
            return scatter_start(l, SHARDED, [G[n] for n in SHARDED], [grad_lands[n] for n in SHARDED], "")
        sflat = _pack([jnp.stack([grads_of[k][n] for k in range(DEPTH)]) for n in SMALL], F32)
        sparts = sflat.reshape(N_DEV, -1, FLAT_COLS)
        small_land.append(sflat.shape)
        return scatter_start(0, GRADS_END + ("small",), [G[n] for n in GRADS_END] + [sparts],
                             [grad_lands[n] for n in GRADS_END] + [lax.empty((N_DEV, 1) + sparts.shape[1:], F32)], "b")

    grads_of = {}

    def after_bwd_recording(l, g, G):
        grads_of[l] = G
        return after_bwd(l, g, G)

    lsum, gx, grads = _local_step(x[0], p[:, 0], positions[0], loss_target[0], P, weights_of, mid_bwd, after_bwd_recording)
    grad_lands.update(scatter_wait(gx))
    last = scatter_wait(gx)
    small_parts = last.pop("small")
    grad_lands.update(last)
    G, delta, new_m, new_v = {}, {}, {}, {}
    for n in SHARDED:
        shp = P[n].shape
        res = _sum_adamw(grad_lands[n].reshape(N_DEV, -1, shp[-1]), *(a.reshape(-1, shp[-1]) for a in (P[n], M[n], V[n])))
        G[n], delta[n], new_m[n], new_v[n] = (a.reshape(shp) for a in res)
    small_sum = _gather_whole(_sum_slots(small_parts[:, 0])).reshape((1,) + small_land[0])
    res = _sum_adamw(small_sum, _pack([P[n] for n in SMALL], F32), _pack([M[n] for n in SMALL], F32), _pack([V[n] for n in SMALL], F32))
    for dst, flat in zip((G, delta, new_m, new_v), res):
        dst.update(zip(SMALL, _unpack(flat, small_shapes)))
    loss = lax.psum(lsum, ("x", "y", "c"))
    return (loss, gx[None], *[G[n] for n in WEIGHTS], *[delta[n] for n in WEIGHTS], *[new_m[n] for n in WEIGHTS], *[new_v[n] for n in WEIGHTS])
```

```python
import functools
import math

import jax
import jax.numpy as jnp
from jax import lax
from jax.experimental import pallas as pl
from jax.experimental.pallas import tpu as pltpu

F32 = jnp.float32
MXU = jnp.bfloat16
WIRE = jnp.bfloat16

D_MODEL = 1024
DEPTH = 4
HEAD_DIM = 64
A_WIDTH = 256
A_HEADS = 4
CHUNK = 128
B_WIDTH = 512
WINDOW = 128
C_WIDTH = 256
C_GROUP = 16
C_GROUPS = 16
C_STATE = 64
N_STATE = C_GROUPS * C_STATE
IN_COLS = 1536
D_FF = 4096
PLE_DIM = 256
EPS = 1e-6
ROPE_THETA = 10000.0
SCALE = HEAD_DIM ** -0.5
NEG = -1e30
N_DEV = 8

ADAM_LR = 0.001
ADAM_B1 = 0.9
ADAM_B2 = 0.999
ADAM_EPS = 1e-08
ADAM_WD = 0.01
ADAM_STEP = 10

V7X_VMEM_BYTES = 64 * 2 ** 20
VMEM_LIMIT = V7X_VMEM_BYTES - 8 * 2 ** 20
LANES = 128

MESH = pl.DeviceIdType.MESH


def _cp(*sem):
    return pltpu.CompilerParams(dimension_semantics=sem, vmem_limit_bytes=VMEM_LIMIT)


def _sds(shape, dtype=F32):
    return jax.ShapeDtypeStruct(shape, dtype)


def _mm(a, b):
    return jnp.dot(a.astype(MXU), b.astype(MXU), preferred_element_type=F32)


def _mm_nt(a, b):
    return lax.dot_general(a.astype(MXU), b.astype(MXU), (((1,), (1,)), ((), ())), preferred_element_type=F32)


def _mm_tn(a, b):
    return lax.dot_general(a.astype(MXU), b.astype(MXU), (((0,), (0,)), ((), ())), preferred_element_type=F32)


def _lane(shape):
    return lax.broadcasted_iota(jnp.int32, shape, len(shape) - 1)


def _row(shape):
    return lax.broadcasted_iota(jnp.int32, shape, 0)


_GELU_C = math.sqrt(2.0 / math.pi)


def _gelu(x):
    return 0.5 * x * (1.0 + jnp.tanh(_GELU_C * (x + 0.044715 * (x * x * x))))


def _gelu_grad(x):
    t = jnp.tanh(_GELU_C * (x + 0.044715 * (x * x * x)))
    return 0.5 * (1.0 + t) + 0.5 * x * (1.0 - t * t) * (_GELU_C * (1.0 + 3.0 * 0.044715 * (x * x)))


def _sigmoid(x):
    return 1.0 / (1.0 + jnp.exp(-x))


def _rms_stat(x):
    return lax.rsqrt(jnp.mean(x * x, axis=-1, keepdims=True) + EPS)


def _rms_bwd(x, r, g, dy):
    xh = x * r
    dxh = dy * g
    dx = r * (dxh - xh * jnp.mean(dxh * xh, axis=-1, keepdims=True))
    return dx, jnp.sum(dy * xh, axis=0, keepdims=True)


def _tril(w):
    return jnp.where(_row(w.shape) >= _lane(w.shape), w, 0.0)


def _swap64(x):
    return pltpu.roll(x, HEAD_DIM, 1)


def _group_sum64(x, lo):
    s_lo = jnp.sum(jnp.where(lo, x, 0.0), axis=-1, keepdims=True)
    s_hi = jnp.sum(jnp.where(lo, 0.0, x), axis=-1, keepdims=True)
    return jnp.where(lo, s_lo, s_hi)


def _partner(x):
    n = x.shape[-1]
    first = (_lane(x.shape) % HEAD_DIM) < HEAD_DIM // 2
    return jnp.where(first, pltpu.roll(x, n - HEAD_DIM // 2, 1), pltpu.roll(x, HEAD_DIM // 2, 1))


def _rope(y, cs, sn):
    return y * cs + _partner(y) * sn


def _rope_bwd(d, cs, sn):
    return d * cs + _partner(d * sn)


def _qk_norm_rope(x, g, cs, sn):
    lo = _lane(x.shape) < HEAD_DIM
    r = lax.rsqrt(_group_sum64(x * x, lo) * (1.0 / HEAD_DIM) + EPS)
    xh = x * r
    return _rope(xh * g, cs, sn), xh, r


def _qk_norm_rope_bwd(xh, r, g, cs, sn, d):
    lo = _lane(xh.shape) < HEAD_DIM
    dy = _rope_bwd(d, cs, sn)
    dxh = dy * g
    m = _group_sum64(dxh * xh, lo) * (1.0 / HEAD_DIM)
    return r * (dxh - xh * m), jnp.sum(dy * xh, axis=0, keepdims=True)


def _gmlp_head(blk, g, b):
    hi = _lane(blk.shape) >= HEAD_DIM
    mu = jnp.sum(jnp.where(hi, blk, 0.0), axis=-1, keepdims=True) * (1.0 / HEAD_DIM)
    xc = jnp.where(hi, blk - mu, 0.0)
    rstd = lax.rsqrt(jnp.sum(xc * xc, axis=-1, keepdims=True) * (1.0 / HEAD_DIM) + EPS)
    vhat = xc * rstd
    return vhat * g + b, vhat, rstd


def _rope_tables(pos_col, inv_row):
    T = pos_col.shape[0]
    tm = min(T, 1024)

    def body(p_ref, inv_ref, cs_ref, sn_ref):
        ang = p_ref[...].astype(F32) * inv_ref[...]
        s = jnp.sin(ang)
        cs_ref[...] = jnp.cos(ang)
        sn_ref[...] = jnp.where((_lane(ang.shape) % HEAD_DIM) < HEAD_DIM // 2, -s, s)

    blk = pl.BlockSpec((tm, LANES), lambda i: (i, 0))
    return pl.pallas_call(
        body, name="rope_tables", grid=(T // tm,),
        in_specs=[pl.BlockSpec((tm, 1), lambda i: (i, 0)), pl.BlockSpec((1, LANES), lambda i: (0, 0))],
        out_specs=[blk, blk], out_shape=[_sds((T, LANES))] * 2, compiler_params=_cp("parallel"))(pos_col, inv_row)


def _inproj_fwd(h, g, w):
    T = h.shape[0]
    tm = min(T, 512)

    def body(h_ref, g_ref, w_ref, z_ref):
        x = h_ref[...]
        z_ref[...] = _mm(x * _rms_stat(x) * g_ref[...], w_ref[...])

    return pl.pallas_call(
        body, name="inproj_fwd", grid=(T // tm,),
        in_specs=[pl.BlockSpec((tm, D_MODEL), lambda i: (i, 0)), pl.BlockSpec((1, D_MODEL), lambda i: (0, 0)),
                  pl.BlockSpec((D_MODEL, IN_COLS), lambda i: (0, 0))],
        out_specs=pl.BlockSpec((tm, IN_COLS), lambda i: (i, 0)), out_shape=_sds((T, IN_COLS)),
        compiler_params=_cp("parallel"))(h, g, w)


def _inproj_bwd(gres, h, g, w, dza, dzq, dzk, dzv, dzc):
    T = h.shape[0]
    tm = min(T, 512)

    def body(gr_ref, h_ref, g_ref, w_ref, a_ref, q_ref, k_ref, v_ref, c_ref, dh_ref, xn_ref, dz_ref, dg_ref):
        @pl.when(pl.program_id(0) == 0)
        def _():
            dg_ref[...] = jnp.zeros_like(dg_ref)

        x = h_ref[...]
        r = _rms_stat(x)
        gg = g_ref[...]
        dz = jnp.concatenate([a_ref[...], q_ref[...], k_ref[...], v_ref[...], c_ref[...]], axis=1)
        dxn = _mm_nt(dz, w_ref[...])
        dx, dg = _rms_bwd(x, r, gg, dxn)
        dh_ref[...] = gr_ref[...] + dx
        dg_ref[...] += dg
        xn_ref[...] = (x * r * gg).astype(MXU)
        dz_ref[...] = dz.astype(MXU)

    def rows(w_):
        return pl.BlockSpec((tm, w_), lambda i: (i, 0))

    row = pl.BlockSpec((1, D_MODEL), lambda i: (0, 0))
    return pl.pallas_call(
        body, name="inproj_bwd", grid=(T // tm,),
        in_specs=[rows(D_MODEL), rows(D_MODEL), row, pl.BlockSpec((D_MODEL, IN_COLS), lambda i: (0, 0)),
                  rows(512), rows(512), rows(128), rows(128), rows(256)],
        out_specs=[rows(D_MODEL), rows(D_MODEL), rows(IN_COLS), row],
        out_shape=[_sds((T, D_MODEL)), _sds((T, D_MODEL), MXU), _sds((T, IN_COLS), MXU), _sds((1, D_MODEL))],
        compiler_params=_cp("arbitrary"))(gres, h, g, w, dza, dzq, dzk, dzv, dzc)


def _gmlp_fwd(z, lng, lnb, ws, bsx):
    T = z.shape[0]
    tm = min(T, 512)
    nc = tm // CHUNK

    def body(z_ref, g_ref, b_ref, w_ref, bs_ref, ya_ref):
        zg = _gelu(z_ref[...])
        lo = _lane((tm, LANES)) < HEAD_DIM
        prods = []
        for hd in range(A_HEADS):
            sl = slice(hd * LANES, (hd + 1) * LANES)
            blk = zg[:, sl]
            vn, _, _ = _gmlp_head(blk, g_ref[:, sl], b_ref[:, sl])
            wm = _tril(w_ref[hd])
            sv = jnp.concatenate([_mm(wm, vn[c * CHUNK:(c + 1) * CHUNK]) + bs_ref[hd] for c in range(nc)], axis=0)
            prods.append(blk * _swap64(sv))
        ya_ref[:, 0:LANES] = jnp.where(lo, prods[0], _swap64(prods[1]))
        ya_ref[:, LANES:2 * LANES] = jnp.where(lo, prods[2], _swap64(prods[3]))

    row = pl.BlockSpec((1, 512), lambda i: (0, 0))
    mat = pl.BlockSpec((A_HEADS, CHUNK, CHUNK), lambda i: (0, 0, 0))
    return pl.pallas_call(
        body, name="gmlp_fwd", grid=(T // tm,),
        in_specs=[pl.BlockSpec((tm, 512), lambda i: (i, 0)), row, row, mat, mat],
        out_specs=pl.BlockSpec((tm, A_WIDTH), lambda i: (i, 0)), out_shape=_sds((T, A_WIDTH)),
        compiler_params=_cp("parallel"))(z, lng, lnb, ws, bsx)


def _gmlp_bwd(z, dya, lng, lnb, ws, bsx):
    T = z.shape[0]
    tm = min(T, 512)
    nc = tm // CHUNK

    def body(z_ref, dya_ref, g_ref, b_ref, w_ref, bs_ref, dza_ref, dw_ref, dbs_ref, dg_ref, db_ref):
        @pl.when(pl.program_id(0) == 0)
        def _():
            dw_ref[...] = jnp.zeros_like(dw_ref)
            dbs_ref[...] = jnp.zeros_like(dbs_ref)
            dg_ref[...] = jnp.zeros_like(dg_ref)
            db_ref[...] = jnp.zeros_like(db_ref)

        za = z_ref[...]
        zg = _gelu(za)
        gp = _gelu_grad(za)
        lo = _lane((tm, LANES)) < HEAD_DIM
        for hd in range(A_HEADS):
            sl = slice(hd * LANES, (hd + 1) * LANES)
            blk = zg[:, sl]
            g = g_ref[:, sl]
            vn, vhat, rstd = _gmlp_head(blk, g, b_ref[:, sl])
            wm = _tril(w_ref[hd])
            pair = dya_ref[:, (hd // 2) * LANES:(hd // 2 + 1) * LANES]
            dy = jnp.where(lo, pair if hd % 2 == 0 else _swap64(pair), 0.0)
            dsv = _swap64(dy * blk)
            svs, dvns = [], []
            dw = jnp.zeros((CHUNK, CHUNK), F32)
            dbs = jnp.zeros((CHUNK, 1), F32)
            for c in range(nc):
                cs = slice(c * CHUNK, (c + 1) * CHUNK)
                svs.append(_mm(wm, vn[cs]) + bs_ref[hd])
                dw = dw + _mm_nt(dsv[cs], vn[cs])
                dbs = dbs + jnp.sum(dsv[cs], axis=-1, keepdims=True)
                dvns.append(_mm_tn(wm, dsv[cs]))
            sv = jnp.concatenate(svs, axis=0)
            dvn = jnp.concatenate(dvns, axis=0)
            dw_ref[hd] += _tril(dw)
            dbs_ref[hd] += jnp.broadcast_to(dbs, (CHUNK, CHUNK))
            dg_ref[:, sl] += jnp.sum(dvn * vhat, axis=0, keepdims=True)
            db_ref[:, sl] += jnp.sum(dvn, axis=0, keepdims=True)
            du = dy * _swap64(sv)
            dvh = dvn * g
            m1 = jnp.sum(dvh, axis=-1, keepdims=True) * (1.0 / HEAD_DIM)
            m2 = jnp.sum(dvh * vhat, axis=-1, keepdims=True) * (1.0 / HEAD_DIM)
            dv = jnp.where(lo, 0.0, rstd * (dvh - m1 - vhat * m2))
            dza_ref[:, sl] = (du + dv) * gp[:, sl]

    row = pl.BlockSpec((1, 512), lambda i: (0, 0))
    mat = pl.BlockSpec((A_HEADS, CHUNK, CHUNK), lambda i: (0, 0, 0))
    return pl.pallas_call(
        body, name="gmlp_bwd", grid=(T // tm,),
        in_specs=[pl.BlockSpec((tm, 512), lambda i: (i, 0)), pl.BlockSpec((tm, A_WIDTH), lambda i: (i, 0)), row, row, mat, mat],
        out_specs=[pl.BlockSpec((tm, 512), lambda i: (i, 0)), mat, mat, row, row],
        out_shape=[_sds((T, 512)), _sds((A_HEADS, CHUNK, CHUNK)), _sds((A_HEADS, CHUNK, CHUNK)), _sds((1, 512)), _sds((1, 512))],
        compiler_params=_cp("arbitrary"))(z, dya, lng, lnb, ws, bsx)


def _attn_specs(T, tq, tile_of):
    nb = tq // WINDOW

    def prev(i):
        return jnp.maximum(tile_of(i) * nb - 1, 0)

    row = pl.BlockSpec((1, LANES), lambda i: (0, 0))
    return [
        pl.BlockSpec((tq, B_WIDTH), lambda i: (tile_of(i), 1)),
        pl.BlockSpec((tq, LANES), lambda i: (tile_of(i), 8)),
        pl.BlockSpec((tq, LANES), lambda i: (tile_of(i), 9)),
        pl.BlockSpec((WINDOW, LANES), lambda i: (prev(i), 8)),
        pl.BlockSpec((WINDOW, LANES), lambda i: (prev(i), 9)),
        pl.BlockSpec((tq, LANES), lambda i: (tile_of(i), 0)),
        pl.BlockSpec((tq, LANES), lambda i: (tile_of(i), 0)),
        pl.BlockSpec((WINDOW, LANES), lambda i: (prev(i), 0)),
        pl.BlockSpec((WINDOW, LANES), lambda i: (prev(i), 0)),
        row, row,
        pl.BlockSpec((8, LANES), lambda i: (0, 0)),
    ]


def _attn_bias(first):
    qi = lax.broadcasted_iota(jnp.int32, (WINDOW, 2 * WINDOW), 0)
    kj = lax.broadcasted_iota(jnp.int32, (WINDOW, 2 * WINDOW), 1)
    diff = qi + WINDOW - kj
    ok = (diff >= 0) & (diff < WINDOW) & ((kj >= WINDOW) | jnp.logical_not(first))
    return jnp.where(ok, 0.0, NEG)


def _dup_heads(x, lo):
    sw = _swap64(x)
    return jnp.where(lo, x, sw), jnp.where(lo, sw, x)


HEADS_PER_KV = 4


def _stack_heads(x0, x1, lo):
    return jnp.concatenate([jnp.where(lo, x0, 0.0), jnp.where(lo, 0.0, x0), jnp.where(lo, x1, 0.0), jnp.where(lo, 0.0, x1)], axis=0)


def _unstack_heads(x4, lo):
    return (jnp.where(lo, x4[0:WINDOW], x4[WINDOW:2 * WINDOW]), jnp.where(lo, x4[2 * WINDOW:3 * WINDOW], x4[3 * WINDOW:4 * WINDOW]))


def _sink_column(sk_ref, g):
    return jnp.concatenate([jnp.broadcast_to(sk_ref[a:a + 1, 0:1], (WINDOW, 1)) for a in range(HEADS_PER_KV * g, HEADS_PER_KV * (g + 1))], axis=0)


def _attn_probs(q4, kw, bias, sink):
    s = _mm_nt(q4, kw)
    s = (s.reshape(HEADS_PER_KV, WINDOW, 2 * WINDOW) + bias[None]).reshape(HEADS_PER_KV * WINDOW, 2 * WINDOW)
    m = jnp.maximum(jnp.max(s, axis=-1, keepdims=True), sink)
    p = jnp.exp(s - m)
    es = jnp.exp(sink - m)
    inv = 1.0 / (jnp.sum(p, axis=-1, keepdims=True) + es)
    return p * inv, es * inv


def _attn_fwd(z, cs, sn, qg, kg, sinks):
    T = z.shape[0]
    tq = min(T, 512)
    nb = tq // WINDOW

    def body(q_ref, k_ref, v_ref, kp_ref, vp_ref, cs_ref, sn_ref, csp_ref, snp_ref, qg_ref, kg_ref, sk_ref, o_ref):
        i = pl.program_id(0)
        csq, snq = cs_ref[...], sn_ref[...]
        cs_all = jnp.concatenate([csp_ref[...], csq], axis=0)
        sn_all = jnp.concatenate([snp_ref[...], snq], axis=0)
        k_all = jnp.concatenate([kp_ref[...], k_ref[...]], axis=0)
        v_all = jnp.concatenate([vp_ref[...], v_ref[...]], axis=0)
        kr, _, _ = _qk_norm_rope(k_all, kg_ref[...], cs_all, sn_all)
        lo_all = _lane(k_all.shape) < HEAD_DIM
        kd = _dup_heads(kr, lo_all)
        vd = _dup_heads(v_all, lo_all)
        lo = _lane((WINDOW, LANES)) < HEAD_DIM
        qrs = [_qk_norm_rope(q_ref[:, pr * LANES:(pr + 1) * LANES], qg_ref[...], csq, snq)[0] * SCALE for pr in range(4)]
        biases = [_attn_bias(i * nb + b == 0) for b in range(nb)]
        for g in range(2):
            sink = _sink_column(sk_ref, g)
            for b in range(nb):
                bs = slice(b * WINDOW, (b + 1) * WINDOW)
                ws = slice(b * WINDOW, (b + 2) * WINDOW)
                pn, _ = _attn_probs(_stack_heads(qrs[2 * g][bs], qrs[2 * g + 1][bs], lo), kd[g][ws], biases[b], sink)
                o0, o1 = _unstack_heads(_mm(pn, vd[g][ws]), lo)
                o_ref[bs, 2 * g * LANES:(2 * g + 1) * LANES] = o0
                o_ref[bs, (2 * g + 1) * LANES:(2 * g + 2) * LANES] = o1

    return pl.pallas_call(
        body, name="attn_fwd", grid=(T // tq,),
        in_specs=_attn_specs(T, tq, lambda i: i),
        out_specs=pl.BlockSpec((tq, B_WIDTH), lambda i: (i, 0)), out_shape=_sds((T, B_WIDTH)),
        compiler_params=_cp("parallel"))(z, z, z, z, z, cs, sn, cs, sn, qg, kg, sinks)


def _attn_bwd(z, cs, sn, qg, kg, sinks, o, do):
    T = z.shape[0]
    tq = min(T, 512)
    nb = tq // WINDOW
    nt = T // tq
    tk = tq + WINDOW

    def tile_of(i):
        return nt - 1 - i

    def body(q_ref, k_ref, v_ref, kp_ref, vp_ref, cs_ref, sn_ref, csp_ref, snp_ref, qg_ref, kg_ref, sk_ref, o_ref, do_ref,
             dq_ref, dk_ref, dv_ref, dqg_ref, dkg_ref, dsk_ref, acck, accv, ck, cv):
        i = pl.program_id(0)
        ti = nt - 1 - i

        @pl.when(i == 0)
        def _():
            dqg_ref[...] = jnp.zeros_like(dqg_ref)
            dkg_ref[...] = jnp.zeros_like(dkg_ref)
            dsk_ref[...] = jnp.zeros_like(dsk_ref)
            ck[...] = jnp.zeros_like(ck)
            cv[...] = jnp.zeros_like(cv)

        csq, snq = cs_ref[...], sn_ref[...]
        cs_all = jnp.concatenate([csp_ref[...], csq], axis=0)
        sn_all = jnp.concatenate([snp_ref[...], snq], axis=0)
        k_all = jnp.concatenate([kp_ref[...], k_ref[...]], axis=0)
        v_all = jnp.concatenate([vp_ref[...], v_ref[...]], axis=0)
        kr, kh, rk = _qk_norm_rope(k_all, kg_ref[...], cs_all, sn_all)
        lo_all = _lane(k_all.shape) < HEAD_DIM
        kd = _dup_heads(kr, lo_all)
        vd = _dup_heads(v_all, lo_all)
        lo = _lane((WINDOW, LANES)) < HEAD_DIM
        acck[...] = jnp.zeros_like(acck)
        accv[...] = jnp.zeros_like(accv)
        prep = [_qk_norm_rope(q_ref[:, pr * LANES:(pr + 1) * LANES], qg_ref[...], csq, snq) for pr in range(4)]
        biases = [_attn_bias(ti * nb + b == 0) for b in range(nb)]
        dqs = [[None] * nb for _ in range(4)]
        for g in range(2):
            sink = _sink_column(sk_ref, g)
            dsink = jnp.zeros((HEADS_PER_KV * WINDOW, 1), F32)
            for b in range(nb):
                bs = slice(b * WINDOW, (b + 1) * WINDOW)
                ws = slice(b * WINDOW, (b + 2) * WINDOW)
                kw, vw = kd[g][ws], vd[g][ws]
                q4 = _stack_heads(prep[2 * g][0][bs] * SCALE, prep[2 * g + 1][0][bs] * SCALE, lo)
                pn, psink = _attn_probs(q4, kw, biases[b], sink)
                o0, o1 = o_ref[bs, 2 * g * LANES:(2 * g + 1) * LANES], o_ref[bs, (2 * g + 1) * LANES:(2 * g + 2) * LANES]
                do4 = _stack_heads(do_ref[bs, 2 * g * LANES:(2 * g + 1) * LANES], do_ref[bs, (2 * g + 1) * LANES:(2 * g + 2) * LANES], lo)
                delta = jnp.sum(do4 * jnp.concatenate([o0, o0, o1, o1], axis=0), axis=-1, keepdims=True)
                ds = pn * (_mm_nt(do4, vw) - delta)
                dsink = dsink - psink * delta
                dqs[2 * g][b], dqs[2 * g + 1][b] = _unstack_heads(_mm(ds, kw) * SCALE, lo)
                acck[g, ws, :] += _mm_tn(ds, q4)
                accv[g, ws, :] += _mm_tn(pn, do4)
            for hh in range(HEADS_PER_KV):
                a = HEADS_PER_KV * g + hh
                dsk_ref[a:a + 1, :] += jnp.zeros((1, LANES), F32) + jnp.sum(dsink[hh * WINDOW:(hh + 1) * WINDOW])
        for pr in range(4):
            _, qh, rq = prep[pr]
            dx, dg = _qk_norm_rope_bwd(qh, rq, qg_ref[...], csq, snq, jnp.concatenate(dqs[pr], axis=0))
            dq_ref[:, pr * LANES:(pr + 1) * LANES] = dx
            dqg_ref[...] += dg

        def fold(acc):
            f0 = acc[0] + _swap64(acc[0])
            f1 = acc[1] + _swap64(acc[1])
            return jnp.where(lo_all, f0, f1)

        dk_all = fold(acck)
        dv_all = fold(accv)
        pad = jnp.zeros((tq - WINDOW, LANES), F32)
        dk_own = dk_all[WINDOW:] + (jnp.concatenate([pad, ck[...]], axis=0) if nb > 1 else ck[...])
        dv_own = dv_all[WINDOW:] + (jnp.concatenate([pad, cv[...]], axis=0) if nb > 1 else cv[...])
        ck[...] = dk_all[:WINDOW]
        cv[...] = dv_all[:WINDOW]
        dxk, dgk = _qk_norm_rope_bwd(kh[WINDOW:], rk[WINDOW:], kg_ref[...], csq, snq, dk_own)
        dk_ref[...] = dxk
        dkg_ref[...] += dgk
        dv_ref[...] = dv_own

    row = pl.BlockSpec((1, LANES), lambda i: (0, 0))
    return pl.pallas_call(
        body, name="attn_bwd", grid=(nt,),
        in_specs=_attn_specs(T, tq, tile_of) + [pl.BlockSpec((tq, B_WIDTH), lambda i: (tile_of(i), 0))] * 2,
        out_specs=[pl.BlockSpec((tq, B_WIDTH), lambda i: (tile_of(i), 0)), pl.BlockSpec((tq, LANES), lambda i: (tile_of(i), 0)),
                   pl.BlockSpec((tq, LANES), lambda i: (tile_of(i), 0)), row, row, pl.BlockSpec((8, LANES), lambda i: (0, 0))],
        out_shape=[_sds((T, B_WIDTH)), _sds((T, LANES)), _sds((T, LANES)), _sds((1, LANES)), _sds((1, LANES)), _sds((8, LANES))],
        scratch_shapes=[pltpu.VMEM((2, tk, LANES), F32), pltpu.VMEM((2, tk, LANES), F32),
                        pltpu.VMEM((WINDOW, LANES), F32), pltpu.VMEM((WINDOW, LANES), F32)],
        compiler_params=_cp("arbitrary"))(z, z, z, z, z, cs, sn, cs, sn, qg, kg, sinks, o, do)


def _bbar_t(are, aim, ldt, btr, bti):
    lbr, lbi = _lam_bar(are, aim, ldt)
    den = are * are + aim * aim
    nr = lbr - 1.0
    cr = (nr * are + lbi * aim) / den
    ci = (lbi * are - nr * aim) / den
    return cr * btr - ci * bti, cr * bti + ci * btr


def _lam_bar(are, aim, ldt):
    dt = jnp.exp(ldt)
    er = jnp.exp(are * dt)
    return er * jnp.cos(aim * dt), er * jnp.sin(aim * dt)


def _block_diag(x):
    t = jnp.concatenate([x] * C_GROUPS, axis=1)
    return jnp.where(_row(t.shape) // C_GROUP == _lane(t.shape) // C_STATE, t, 0.0)


def _block_diag_fold(m):
    rg = _row((C_WIDTH, C_STATE)) // C_GROUP
    acc = jnp.zeros((C_WIDTH, C_STATE), F32)
    for g in range(C_GROUPS):
        acc = acc + jnp.where(rg == g, m[:, g * C_STATE:(g + 1) * C_STATE], 0.0)
    return acc


def _ssm_prep(are, aim, ldt, are_x, aim_x, ldt_x, btr, bti, cre, cim):
    def body(are_r, aim_r, ldt_r, arex_r, aimx_r, ldtx_r, btr_r, bti_r, cre_r, cim_r, bbd_ref, cbd_ref, pwr_ref, pwi_ref):
        lr, li = _lam_bar(are_r[...], aim_r[...], ldt_r[...])
        cr, ci = lr, li
        for r in range(SCAN_SEG):
            pwr_ref[r:r + 1, :] = cr
            pwi_ref[r:r + 1, :] = ci
            cr, ci = cr * lr - ci * li, cr * li + ci * lr
        br, bi = _bbar_t(arex_r[...], aimx_r[...], ldtx_r[...], btr_r[...], bti_r[...])
        bbd_ref[...] = jnp.concatenate([_block_diag(br), _block_diag(bi)], axis=1).astype(MXU)
        cbd_ref[...] = jnp.concatenate([_block_diag(cre_r[...]), -_block_diag(cim_r[...])], axis=1).astype(MXU)

    return pl.pallas_call(
        body, name="ssm_prep",
        out_shape=[_sds((C_WIDTH, 2 * N_STATE), MXU), _sds((C_WIDTH, 2 * N_STATE), MXU), _sds((SCAN_SEG, N_STATE)), _sds((SCAN_SEG, N_STATE))],
        compiler_params=pltpu.CompilerParams(vmem_limit_bytes=VMEM_LIMIT))(are, aim, ldt, are_x, aim_x, ldt_x, btr, bti, cre, cim)


def _ssm_param_bwd(are, aim, ldt, are_x, aim_x, ldt_x, btr, bti, dlr, dli, dbbd, dcr, dci):
    def body(are_r, aim_r, ldt_r, arex_r, aimx_r, ldtx_r, btr_r, bti_r, dlr_r, dli_r, dbbd_r, dcr_r, dci_r,
             dare_ref, daim_ref, dldt_ref, dbtr_ref, dbti_ref, dcre_ref, dcim_ref):
        _, vjp_l = jax.vjp(_lam_bar, are_r[...], aim_r[...], ldt_r[...])
        da1, di1, dl1 = vjp_l((dlr_r[...], dli_r[...]))
        dbr = _block_diag_fold(dbbd_r[:, 0:N_STATE])
        dbi = _block_diag_fold(dbbd_r[:, N_STATE:2 * N_STATE])
        _, vjp_b = jax.vjp(_bbar_t, arex_r[...], aimx_r[...], ldtx_r[...], btr_r[...], bti_r[...])
        da2, di2, dl2, dbtr, dbti = vjp_b((dbr, dbi))

        def gsum(x):
            return x.reshape(C_GROUPS, C_GROUP, C_STATE).sum(axis=1)

        dare_ref[...] = da1 + gsum(da2)
        daim_ref[...] = di1 + gsum(di2)
        dldt_ref[...] = jnp.broadcast_to(jnp.sum(dl1 + gsum(dl2), axis=-1, keepdims=True), (C_GROUPS, LANES))
        dbtr_ref[...] = dbtr
        dbti_ref[...] = dbti
        dcre_ref[...] = _block_diag_fold(dcr_r[...])
        dcim_ref[...] = -_block_diag_fold(dci_r[...])

    g = _sds((C_GROUPS, C_STATE))
    x = _sds((C_WIDTH, C_STATE))
    return pl.pallas_call(
        body, name="ssm_param_bwd", out_shape=[g, g, _sds((C_GROUPS, LANES)), x, x, x, x],
        compiler_params=pltpu.CompilerParams(vmem_limit_bytes=VMEM_LIMIT))(are, aim, ldt, are_x, aim_x, ldt_x, btr, bti, dlr, dli, dbbd, dcr, dci)


SCAN_TILE = 256
SCAN_SEG = 8


def _scan_tables(pwr_ref, pwi_ref, conj, reverse):
    row = _row((SCAN_SEG, N_STATE))
    shifts = []
    for k in (1, 2, 4):
        keep = (row < SCAN_SEG - k) if reverse else (row >= k)
        ar = jnp.broadcast_to(pwr_ref[k - 1:k, :], (SCAN_SEG, N_STATE))
        ai = jnp.broadcast_to(pwi_ref[k - 1:k, :], (SCAN_SEG, N_STATE)) * conj
        shifts.append((SCAN_SEG - k if reverse else k, jnp.where(keep, ar, 0.0), jnp.where(keep, ai, 0.0)))
    if reverse:
        pr = jnp.concatenate([pwr_ref[SCAN_SEG - 1 - r:SCAN_SEG - r, :] for r in range(SCAN_SEG)], axis=0)
        pi = jnp.concatenate([pwi_ref[SCAN_SEG - 1 - r:SCAN_SEG - r, :] for r in range(SCAN_SEG)], axis=0) * conj
    else:
        pr, pi = pwr_ref[...], pwi_ref[...] * conj
    return shifts, (pr, pi)


def _tile_scan(xr_ref, xi_ref, pwr_ref, pwi_ref, sr, si, conj, reverse):
    shifts, (pr, pi) = _scan_tables(pwr_ref, pwi_ref, conj, reverse)
    groups = xr_ref.shape[0] // SCAN_SEG
    out_row = 0 if reverse else SCAN_SEG - 1

    def step(k, c):
        cr, ci = c
        g = groups - 1 - k if reverse else k
        rows = pl.ds(pl.multiple_of(g * SCAN_SEG, SCAN_SEG), SCAN_SEG)
        xr, xi = xr_ref[rows, :], xi_ref[rows, :]
        for amount, ar, ai in shifts:
            qr, qi = pltpu.roll(xr, amount, 0), pltpu.roll(xi, amount, 0)
            xr, xi = xr + ar * qr - ai * qi, xi + ar * qi + ai * qr
        xr, xi = xr + pr * cr - pi * ci, xi + pr * ci + pi * cr
        xr_ref[rows, :] = xr
        xi_ref[rows, :] = xi
        return xr[out_row:out_row + 1], xi[out_row:out_row + 1]

    cr, ci = lax.fori_loop(0, groups, step, (sr[...], si[...]), unroll=2)
    sr[...] = cr
    si[...] = ci


def _ssm_fwd(z, bbd, pwr, pwi, cbd, dsk, w1, w2):
    T = z.shape[0]
    tt = min(T, SCAN_TILE)

    def body(u_ref, bbd_ref, pwr_ref, pwi_ref, cbd_ref, d_ref, w1_ref, w2_ref, yc_ref, xr_ref, xi_ref, sr, si):
        @pl.when(pl.program_id(0) == 0)
        def _():
            sr[...] = jnp.zeros_like(sr)
            si[...] = jnp.zeros_like(si)

        u = u_ref[...]
        bu = _mm(u, bbd_ref[...])
        xr_ref[...] = bu[:, 0:N_STATE]
        xi_ref[...] = bu[:, N_STATE:2 * N_STATE]
        _tile_scan(xr_ref, xi_ref, pwr_ref, pwi_ref, sr, si, 1.0, False)
        x = jnp.concatenate([xr_ref[...], xi_ref[...]], axis=1)
        y2 = _gelu(_mm_nt(x, cbd_ref[...]) + d_ref[...] * u)
        yc_ref[...] = _mm(y2, w1_ref[...]) * _sigmoid(_mm(y2, w2_ref[...]))

    big = pl.BlockSpec((C_WIDTH, 2 * N_STATE), lambda i: (0, 0))
    tab = pl.BlockSpec((SCAN_SEG, N_STATE), lambda i: (0, 0))
    wsp = pl.BlockSpec((C_WIDTH, C_WIDTH), lambda i: (0, 0))
    xs = pl.BlockSpec((tt, N_STATE), lambda i: (i, 0))
    return pl.pallas_call(
        body, name="ssm_fwd", grid=(T // tt,),
        in_specs=[pl.BlockSpec((tt, C_WIDTH), lambda i: (i, 5)), big, tab, tab, big, pl.BlockSpec((1, C_WIDTH), lambda i: (0, 0)), wsp, wsp],
        out_specs=[pl.BlockSpec((tt, C_WIDTH), lambda i: (i, 0)), xs, xs],
        out_shape=[_sds((T, C_WIDTH)), _sds((T, N_STATE)), _sds((T, N_STATE))],
        scratch_shapes=[pltpu.VMEM((1, N_STATE), F32)] * 2,
        compiler_params=_cp("arbitrary"))(z, bbd, pwr, pwi, cbd, dsk, w1, w2)


def _ssm_bwd(dyc, z, xr, xi, bbd, pwr, pwi, cbd, dsk, w1, w2):
    T = z.shape[0]
    tt = min(T, SCAN_TILE)
    nt = T // tt

    def tile_of(i):
        return nt - 1 - i

    def body(dyc_ref, u_ref, xr_ref, xi_ref, xpr_ref, xpi_ref, bbd_ref, pwr_ref, pwi_ref, cbd_ref, d_ref, w1_ref, w2_ref,
             du_ref, y2_ref, da1_ref, da2_ref, dy_ref, arai_ref, dlr_ref, dli_ref, dd_ref, gr, gi, sr, si):
        i = pl.program_id(0)
        ti = nt - 1 - i

        @pl.when(i == 0)
        def _():
            sr[...] = jnp.zeros_like(sr)
            si[...] = jnp.zeros_like(si)
            dlr_ref[...] = jnp.zeros_like(dlr_ref)
            dli_ref[...] = jnp.zeros_like(dli_ref)
            dd_ref[...] = jnp.zeros_like(dd_ref)

        u = u_ref[...]
        xr_t, xi_t = xr_ref[...], xi_ref[...]
        y = _mm_nt(jnp.concatenate([xr_t, xi_t], axis=1), cbd_ref[...]) + d_ref[...] * u
        y2 = _gelu(y)
        a1 = _mm(y2, w1_ref[...])
        sg = _sigmoid(_mm(y2, w2_ref[...]))
        dyc_t = dyc_ref[...]
        da1 = dyc_t * sg
        da2 = dyc_t * a1 * sg * (1.0 - sg)
        dy = (_mm_nt(da1, w1_ref[...]) + _mm_nt(da2, w2_ref[...])) * _gelu_grad(y)
        gx = _mm(dy, cbd_ref[...])
        gr[...] = gx[:, 0:N_STATE]
        gi[...] = gx[:, N_STATE:2 * N_STATE]
        _tile_scan(gr, gi, pwr_ref, pwi_ref, sr, si, -1.0, True)
        ar, ai = gr[...], gi[...]
        first_row = _row(ar.shape) == 0
        live = jnp.where(ti > 0, 1.0, 0.0)
        xsr = jnp.where(first_row, xpr_ref[7:8, :] * live, pltpu.roll(xr_t, 1, 0))
        xsi = jnp.where(first_row, xpi_ref[7:8, :] * live, pltpu.roll(xi_t, 1, 0))
        dlr_ref[...] += jnp.sum(ar * xsr + ai * xsi, axis=0, keepdims=True)
        dli_ref[...] += jnp.sum(ai * xsr - ar * xsi, axis=0, keepdims=True)
        dd_ref[...] += jnp.sum(dy * u, axis=0, keepdims=True)
        arai = jnp.concatenate([ar, ai], axis=1)
        du_ref[...] = _mm_nt(arai, bbd_ref[...]) + d_ref[...] * dy
        y2_ref[...] = y2.astype(MXU)
        da1_ref[...] = da1.astype(MXU)
        da2_ref[...] = da2.astype(MXU)
        dy_ref[...] = dy.astype(MXU)
        arai_ref[...] = arai.astype(MXU)

    def prev(i):
        return jnp.maximum(tile_of(i) * (tt // 8) - 1, 0)

    big = pl.BlockSpec((C_WIDTH, 2 * N_STATE), lambda i: (0, 0))
    tab = pl.BlockSpec((SCAN_SEG, N_STATE), lambda i: (0, 0))
    srow = pl.BlockSpec((1, N_STATE), lambda i: (0, 0))
    wsp = pl.BlockSpec((C_WIDTH, C_WIDTH), lambda i: (0, 0))
    xs = pl.BlockSpec((tt, N_STATE), lambda i: (tile_of(i), 0))
    xp = pl.BlockSpec((8, N_STATE), lambda i: (prev(i), 0))
    cw = pl.BlockSpec((tt, C_WIDTH), lambda i: (tile_of(i), 0))
    drow = pl.BlockSpec((1, C_WIDTH), lambda i: (0, 0))
    return pl.pallas_call(
        body, name="ssm_bwd", grid=(nt,),
        in_specs=[cw, pl.BlockSpec((tt, C_WIDTH), lambda i: (tile_of(i), 5)), xs, xs, xp, xp, big, tab, tab, big, drow, wsp, wsp],
        out_specs=[cw, cw, cw, cw, cw, pl.BlockSpec((tt, 2 * N_STATE), lambda i: (tile_of(i), 0)), srow, srow, drow],
        out_shape=[_sds((T, C_WIDTH))] + [_sds((T, C_WIDTH), MXU)] * 4 + [_sds((T, 2 * N_STATE), MXU), _sds((1, N_STATE)), _sds((1, N_STATE)), _sds((1, C_WIDTH))],
        scratch_shapes=[pltpu.VMEM((tt, N_STATE), F32)] * 2 + [pltpu.VMEM((1, N_STATE), F32)] * 2,
        compiler_params=_cp("arbitrary"))(dyc, z, xr, xi, xr, xi, bbd, pwr, pwi, cbd, dsk, w1, w2)


_GROUPS = ((0, A_WIDTH), (A_WIDTH, A_WIDTH + B_WIDTH), (A_WIDTH + B_WIDTH, D_MODEL))


def _merge_fwd(h, ya, yb, yc, g, w):
    T = h.shape[0]
    tm = min(T, 512)

    def body(h_ref, a_ref, b_ref, c_ref, g_ref, w_ref, o_ref):
        yn = jnp.concatenate([y * _rms_stat(y) for y in (a_ref[...], b_ref[...], c_ref[...])], axis=1) * g_ref[...]
        o_ref[...] = h_ref[...] + _mm(yn, w_ref[...])

    def rows(w_):
        return pl.BlockSpec((tm, w_), lambda i: (i, 0))

    return pl.pallas_call(
        body, name="merge_fwd", grid=(T // tm,),
        in_specs=[rows(D_MODEL), rows(A_WIDTH), rows(B_WIDTH), rows(C_WIDTH), pl.BlockSpec((1, D_MODEL), lambda i: (0, 0)),
                  pl.BlockSpec((D_MODEL, D_MODEL), lambda i: (0, 0))],
        out_specs=rows(D_MODEL), out_shape=_sds((T, D_MODEL)), compiler_params=_cp("parallel"))(h, ya, yb, yc, g, w)


def _merge_bwd(gres, ya, yb, yc, g, w):
    T = gres.shape[0]
    tm = min(T, 512)

    def body(gr_ref, a_ref, b_ref, c_ref, g_ref, w_ref, da_ref, db_ref, dc_ref, yn_ref, dg_ref):
        @pl.when(pl.program_id(0) == 0)
        def _():
            dg_ref[...] = jnp.zeros_like(dg_ref)

        dyn = _mm_nt(gr_ref[...], w_ref[...])
        yns, dgs = [], []
        for (c0, c1), y_ref, d_ref in zip(_GROUPS, (a_ref, b_ref, c_ref), (da_ref, db_ref, dc_ref)):
            y = y_ref[...]
            r = _rms_stat(y)
            gg = g_ref[:, c0:c1]
            dx, dg = _rms_bwd(y, r, gg, dyn[:, c0:c1])
            d_ref[...] = dx
            dgs.append(dg)
            yns.append(y * r * gg)
        dg_ref[...] += jnp.concatenate(dgs, axis=1)
        yn_ref[...] = jnp.concatenate(yns, axis=1).astype(MXU)

    def rows(w_):
        return pl.BlockSpec((tm, w_), lambda i: (i, 0))

    row = pl.BlockSpec((1, D_MODEL), lambda i: (0, 0))
    return pl.pallas_call(
        body, name="merge_bwd", grid=(T // tm,),
        in_specs=[rows(D_MODEL), rows(A_WIDTH), rows(B_WIDTH), rows(C_WIDTH), row, pl.BlockSpec((D_MODEL, D_MODEL), lambda i: (0, 0))],
        out_specs=[rows(A_WIDTH), rows(B_WIDTH), rows(C_WIDTH), rows(D_MODEL), row],
        out_shape=[_sds((T, A_WIDTH)), _sds((T, B_WIDTH)), _sds((T, C_WIDTH)), _sds((T, D_MODEL), MXU), _sds((1, D_MODEL))],
        compiler_params=_cp("arbitrary"))(gres, ya, yb, yc, g, w)


FF_BLOCK = D_FF // N_DEV


def _load_weights(w1_hbm, w2_hbm, w1, w2, sem):
    @pl.when(pl.program_id(0) == 0)
    def _():
        c1 = pltpu.make_async_copy(w1_hbm, w1, sem.at[0])
        c2 = pltpu.make_async_copy(w2_hbm, w2, sem.at[1])
        c1.start()
        c2.start()
        c1.wait()
        c2.wait()


def _mlp_weight_scratch():
    return [pltpu.VMEM((D_MODEL, D_FF), MXU), pltpu.VMEM((D_FF, D_MODEL), MXU), pltpu.SemaphoreType.DMA((2,))]


def _mlp_fwd(h, g, w1, w2):
    T = h.shape[0]
    tm = min(T, 512)

    def body(h_ref, g_ref, w1_hbm, w2_hbm, o_ref, r_ref, w1_v, w2_v, sem):
        _load_weights(w1_hbm, w2_hbm, w1_v, w2_v, sem)
        x = h_ref[...]
        a = jnp.maximum(_mm(x * _rms_stat(x) * g_ref[...], w1_v[...]), 0.0)
        r = (a * a).astype(MXU)
        r_ref[...] = r
        o_ref[...] = x + _mm(r, w2_v[...])

    rows = pl.BlockSpec((tm, D_MODEL), lambda i: (i, 0))
    hbm = pl.BlockSpec(memory_space=pl.ANY)
    return pl.pallas_call(
        body, name="mlp_fwd", grid=(T // tm,),
        in_specs=[rows, pl.BlockSpec((1, D_MODEL), lambda i: (0, 0)), hbm, hbm],
        out_specs=[rows, pl.BlockSpec((tm, D_FF), lambda i: (i, 0))],
        out_shape=[_sds((T, D_MODEL)), _sds((T, D_FF), MXU)],
        scratch_shapes=_mlp_weight_scratch(), compiler_params=_cp("arbitrary"))(h, g, w1, w2)


def _mlp_bwd(gres, h, g, r, w1, w2):
    T = h.shape[0]
    tm = min(T, 256)

    def body(gr_ref, h_ref, g_ref, r_ref, w1_hbm, w2_hbm, dh_ref, hn_ref, da_ref, dg_ref, w1_v, w2_v, sem):
        _load_weights(w1_hbm, w2_hbm, w1_v, w2_v, sem)

        @pl.when(pl.program_id(0) == 0)
        def _():
            dg_ref[...] = jnp.zeros_like(dg_ref)

        gr = gr_ref[...]
        da = (_mm_nt(gr, w2_v[...]) * (2.0 * jnp.sqrt(r_ref[...].astype(F32)))).astype(MXU)
        da_ref[...] = da
        x = h_ref[...]
        rs = _rms_stat(x)
        gg = g_ref[...]
        dx, dg = _rms_bwd(x, rs, gg, _mm_nt(da, w1_v[...]))
        dh_ref[...] = gr + dx
        dg_ref[...] += dg
        hn_ref[...] = (x * rs * gg).astype(MXU)

    rows = pl.BlockSpec((tm, D_MODEL), lambda i: (i, 0))
    row = pl.BlockSpec((1, D_MODEL), lambda i: (0, 0))
    wide = pl.BlockSpec((tm, D_FF), lambda i: (i, 0))
    hbm = pl.BlockSpec(memory_space=pl.ANY)
    return pl.pallas_call(
        body, name="mlp_bwd", grid=(T // tm,),
        in_specs=[rows, rows, row, wide, hbm, hbm], out_specs=[rows, rows, wide, row],
        out_shape=[_sds((T, D_MODEL)), _sds((T, D_MODEL), MXU), _sds((T, D_FF), MXU), _sds((1, D_MODEL))],
        scratch_shapes=_mlp_weight_scratch(), compiler_params=_cp("arbitrary"))(gres, h, g, r, w1, w2)


def _ple_fwd(h, p, l, g, wg, wp):
    T = h.shape[0]
    tm = min(T, 512)

    def body(h_ref, p_ref, g_ref, wg_ref, wp_ref, o_ref):
        x = h_ref[...]
        gate = _sigmoid(_mm(x * _rms_stat(x) * g_ref[...], wg_ref[...]))
        o_ref[...] = x + gate * _mm(p_ref[...], wp_ref[...])

    rows = pl.BlockSpec((tm, D_MODEL), lambda i: (i, 0))
    return pl.pallas_call(
        body, name="ple_fwd", grid=(T // tm,),
        in_specs=[rows, pl.BlockSpec((None, tm, PLE_DIM), lambda i: (l, i, 0)), pl.BlockSpec((1, D_MODEL), lambda i: (0, 0)),
                  pl.BlockSpec((D_MODEL, D_MODEL), lambda i: (0, 0)), pl.BlockSpec((PLE_DIM, D_MODEL), lambda i: (0, 0))],
        out_specs=rows, out_shape=_sds((T, D_MODEL)), compiler_params=_cp("parallel"))(h, p, g, wg, wp)


def _ple_bwd(gres, h, p, l, g, wg, wp):
    T = h.shape[0]
    tm = min(T, 512)

    def body(gr_ref, h_ref, p_ref, g_ref, wg_ref, wp_ref, dh_ref, hn_ref, dgp_ref, de_ref, dg_ref):
        @pl.when(pl.program_id(0) == 0)
        def _():
            dg_ref[...] = jnp.zeros_like(dg_ref)

        x = h_ref[...]
        r = _rms_stat(x)
        gg = g_ref[...]
        hn = x * r * gg
        gate = _sigmoid(_mm(hn, wg_ref[...]))
        e = _mm(p_ref[...], wp_ref[...])
        gr = gr_ref[...]
        dgp = gr * e * gate * (1.0 - gate)
        dx, dg = _rms_bwd(x, r, gg, _mm_nt(dgp, wg_ref[...]))
        dh_ref[...] = gr + dx
        dg_ref[...] += dg
        hn_ref[...] = hn.astype(MXU)
        dgp_ref[...] = dgp.astype(MXU)
        de_ref[...] = (gr * gate).astype(MXU)

    rows = pl.BlockSpec((tm, D_MODEL), lambda i: (i, 0))
    row = pl.BlockSpec((1, D_MODEL), lambda i: (0, 0))
    return pl.pallas_call(
        body, name="ple_bwd", grid=(T // tm,),
        in_specs=[rows, rows, pl.BlockSpec((None, tm, PLE_DIM), lambda i: (l, i, 0)), row,
                  pl.BlockSpec((D_MODEL, D_MODEL), lambda i: (0, 0)), pl.BlockSpec((PLE_DIM, D_MODEL), lambda i: (0, 0))],
        out_specs=[rows, rows, rows, rows, row],
        out_shape=[_sds((T, D_MODEL))] + [_sds((T, D_MODEL), MXU)] * 3 + [_sds((1, D_MODEL))],
        compiler_params=_cp("arbitrary"))(gres, h, p, g, wg, wp)


def _loss_head(h, target):
    T = h.shape[0]
    tm = min(T, 1024)

    def body(h_ref, t_ref, dh_ref, l_ref):
        @pl.when(pl.program_id(0) == 0)
        def _():
            l_ref[...] = jnp.zeros_like(l_ref)

        e = h_ref[...] - t_ref[...]
        dh_ref[...] = e * (1.0 / D_MODEL)
        l_ref[...] += jnp.zeros_like(l_ref) + 0.5 * jnp.sum(jnp.mean(e * e, axis=-1, keepdims=True))

    rows = pl.BlockSpec((tm, D_MODEL), lambda i: (i, 0))
    return pl.pallas_call(
        body, name="loss_head", grid=(T // tm,), in_specs=[rows, rows],
        out_specs=[rows, pl.BlockSpec((8, LANES), lambda i: (0, 0))], out_shape=[_sds((T, D_MODEL)), _sds((8, LANES))],
        compiler_params=_cp("arbitrary"))(h, target)


TN_ROWS = 2048

def _tn(a, b, m, n, *, bm, bn, a_off=0, b_off=0, a_lead=None, b_blocked=False, n_major=False, split=None, dtype=F32, name="tn"):
    T = a.shape[-2]
    tk = min(T, TN_ROWS)
    nk = T // tk
    if b_blocked:
        b_spec = pl.BlockSpec((None, tk, bn), lambda i, j, k: (j, k, 0))
    else:
        b_spec = pl.BlockSpec((tk, bn), lambda i, j, k: (k, b_off + j))
    if a_lead is None:
        a_spec = pl.BlockSpec((tk, bm), lambda i, j, k: (k, a_off + i))
    else:
        a_spec = pl.BlockSpec((None, tk, bm), lambda i, j, k: (a_lead, k, a_off + i))
    assert m % bm == 0 and n % bn == 0 and (not n_major or bm == m) and (split is None or (bn == n and n % split == 0))

    def body(a_ref, b_ref, o_ref, acc):
        k = pl.program_id(2)

        @pl.when(k == 0)
        def _():
            acc[...] = jnp.zeros_like(acc)

        acc[...] += _mm_tn(a_ref[...], b_ref[...])

        @pl.when(k == nk - 1)
        def _():
            if split is None:
                o_ref[...] = acc[...].astype(dtype)
            else:
                for d in range(n // split):
                    o_ref[d] = acc[:, d * split:(d + 1) * split].astype(dtype)

    if split is not None:
        out_spec = pl.BlockSpec((n // split, bm, split), lambda i, j, k: (0, i, 0))
        out_shape = _sds((n // split, m, split), dtype)
    elif n_major:
        out_spec = pl.BlockSpec((None, bm, bn), lambda i, j, k: (j, 0, 0))
        out_shape = _sds((n // bn, m, bn), dtype)
    else:
        out_spec = pl.BlockSpec((bm, bn), lambda i, j, k: (i, j))
        out_shape = _sds((m, n), dtype)
    return pl.pallas_call(
        body, name=name, grid=(m // bm, n // bn, nk),
        in_specs=[a_spec, b_spec],
        out_specs=out_spec, out_shape=out_shape, scratch_shapes=[pltpu.VMEM((bm, bn), F32)],
        compiler_params=_cp("parallel", "parallel", "arbitrary"))(a, b)


def _row_tile(R, C):
    for cand in (512, 256, 128, 64, 32, 16, 8):
        if R % cand == 0 and cand * C * 4 <= 2 ** 20:
            return cand
    return R


def _sum_slots(land):
    S, R, C = land.shape
    tr = _row_tile(R, C)

    def body(l_ref, o_ref):
        acc = l_ref[0].astype(F32)
        for s in range(1, S):
            acc = acc + l_ref[s].astype(F32)
        o_ref[...] = acc

    return pl.pallas_call(
        body, name="sum_slots", grid=(R // tr,), in_specs=[pl.BlockSpec((S, tr, C), lambda i: (0, i, 0))],
        out_specs=pl.BlockSpec((tr, C), lambda i: (i, 0)), out_shape=_sds((R, C)), compiler_params=_cp("parallel"))(land)


def _sum_adamw(land, w, m, v):
    R, C = w.shape
    S = land.shape[0]
    tr = _row_tile(R, C)

    def body(l_ref, w_ref, m_ref, v_ref, g_ref, d_ref, nm_ref, nv_ref):
        gg = l_ref[0].astype(F32)
        for s in range(1, S):
            gg = gg + l_ref[s].astype(F32)
        g_ref[...] = gg
        nm = ADAM_B1 * m_ref[...] + (1.0 - ADAM_B1) * gg
        nv = ADAM_B2 * v_ref[...] + (1.0 - ADAM_B2) * (gg * gg)
        m_hat = nm / (1.0 - ADAM_B1 ** ADAM_STEP)
        v_hat = nv / (1.0 - ADAM_B2 ** ADAM_STEP)
        d_ref[...] = -ADAM_LR * (m_hat / (jnp.sqrt(v_hat) + ADAM_EPS) + ADAM_WD * w_ref[...])
        nm_ref[...] = nm
        nv_ref[...] = nv

    blk = pl.BlockSpec((tr, C), lambda i: (i, 0))
    return pl.pallas_call(
        body, name="sum_adamw", grid=(R // tr,), in_specs=[pl.BlockSpec((S, tr, C), lambda i: (0, i, 0))] + [blk] * 3,
        out_specs=[blk] * 4, out_shape=[_sds((R, C))] * 4, compiler_params=_cp("parallel"))(land, w, m, v)


def _all_to_all(pairs, name):
    n = len(pairs)

    def body(*refs):
        srcs, lands = refs[:n], refs[2 * n:3 * n]
        send, recv, loc = refs[3 * n:]
        x, y, c = lax.axis_index("x"), lax.axis_index("y"), lax.axis_index("c")
        me = 4 * x + 2 * y + c
        own = [pltpu.make_async_copy(pairs[t][2](srcs[t], me, me), pairs[t][3](lands[t], me), loc.at[t]) for t in range(n)]
        for cp in own:
            cp.start()
        sends, recvs = [], []
        for k in range(1, N_DEV):
            px, py, pc = x ^ (k >> 2), y ^ ((k >> 1) & 1), c ^ (k & 1)
            peer = 4 * px + 2 * py + pc
            for t in range(n):
                src = pairs[t][2](srcs[t], me, peer)
                cp = pltpu.make_async_remote_copy(src_ref=src, dst_ref=pairs[t][3](lands[t], me), send_sem=send.at[t, k],
                                                  recv_sem=recv.at[t, k], device_id=(px, py, pc), device_id_type=MESH)
                cp.start()
                sends.append(cp)
                recvs.append(pltpu.make_async_remote_copy(src_ref=src, dst_ref=pairs[t][3](lands[t], peer), send_sem=send.at[t, k],
                                                          recv_sem=recv.at[t, k], device_id=(px, py, pc), device_id_type=MESH))
        for cp in recvs:
            cp.wait_recv()
        for cp in sends:
            cp.wait_send()
        for cp in own:
            cp.wait()

    anyspec = pl.BlockSpec(memory_space=pl.ANY)
    lands = [pr[1] for pr in pairs]
    return pl.pallas_call(
        body, name=name, in_specs=[anyspec] * (2 * n), out_specs=[anyspec] * n,
        out_shape=[_sds(a.shape, a.dtype) for a in lands], input_output_aliases={n + t: t for t in range(n)},
        scratch_shapes=[pltpu.SemaphoreType.DMA((n, N_DEV)), pltpu.SemaphoreType.DMA((n, N_DEV)), pltpu.SemaphoreType.DMA((n,))],
        compiler_params=pltpu.CompilerParams(has_side_effects=True))(*[pr[0] for pr in pairs], *lands)


def _gather_whole(src):
    land = lax.empty((N_DEV,) + src.shape, src.dtype)
    return _all_to_all([(src, land, lambda ref, me, peer: ref, lambda ref, sender: ref.at[sender])], "gather_small_grads")[0]


_HBM = pl.BlockSpec(memory_space=pltpu.HBM)
_SEM = pl.BlockSpec(memory_space=pltpu.SEMAPHORE)
_EFFECT = pltpu.SideEffectType.DATAFLOW_SIDE_EFFECTING


def _peers():
    x, y, c = lax.axis_index("x"), lax.axis_index("y"), lax.axis_index("c")
    out = []
    for k in range(1, N_DEV):
        px, py, pc = x ^ (k >> 2), y ^ ((k >> 1) & 1), c ^ (k & 1)
        out.append((k, (px, py, pc), 4 * px + 2 * py + pc))
    return 4 * x + 2 * y + c, out


def _route(mode, layer):
    if mode == "all":
        return (lambda ref, peer: ref), (lambda ref, sender: ref.at[sender])
    return (lambda ref, peer: ref.at[peer]), (lambda ref, sender: ref.at[sender, layer])


def _split_start(srcs, lands, modes, layer, after, name):
    n = len(srcs)
    routes = [_route(m, layer) for m in modes]

    def body(*refs):
        src, land = refs[:n], refs[n:2 * n]
        send, recv, token = refs[2 * n + 1], refs[2 * n + 2], refs[-1]
        me, peers = _peers()
        for k, dev, peer in peers:
            for t, (src_of, dst_of) in enumerate(routes):
                pltpu.make_async_remote_copy(src_ref=src_of(src[t], peer), dst_ref=dst_of(land[t], me), send_sem=send.at[t * N_DEV + k],
                                             recv_sem=recv.at[t * N_DEV + k], device_id=dev, device_id_type=MESH).start()
        token[...] = jnp.zeros_like(token)

    bufs = list(srcs) + list(lands)
    outs = pl.pallas_call(
        body, name=name,
        out_shape=(pltpu.SemaphoreType.DMA((n * N_DEV,)), pltpu.SemaphoreType.DMA((n * N_DEV,)),
                   *[pltpu.HBM(a.shape, a.dtype) for a in bufs], _sds((8, LANES))),
        in_specs=[_HBM] * (2 * n) + [pl.BlockSpec(memory_space=pl.ANY)],
        out_specs=(_SEM, _SEM, *[_HBM] * (2 * n), pl.BlockSpec(memory_space=pltpu.VMEM)),
        input_output_aliases={i: 2 + i for i in range(2 * n)},
        compiler_params=pltpu.CompilerParams(has_side_effects=_EFFECT),
    )(*[pltpu.with_memory_space_constraint(a, pltpu.HBM) for a in bufs], after)
    return outs[0], outs[1], list(outs[2:2 + n]), list(outs[2 + n:2 + 2 * n]), outs[-1]


def _split_wait(send, recv, srcs, lands, modes, layer, after, name):
    n = len(srcs)
    routes = [_route(m, layer) for m in modes]

    def body(*refs):
        src, land = refs[:n], refs[n:2 * n]
        send_r, recv_r = refs[2 * n], refs[2 * n + 1]
        _, peers = _peers()
        for k, dev, peer in peers:
            for t, (src_of, dst_of) in enumerate(routes):
                cp = pltpu.make_async_remote_copy(src_ref=src_of(src[t], peer), dst_ref=dst_of(land[t], peer), send_sem=send_r.at[t * N_DEV + k],
                                                  recv_sem=recv_r.at[t * N_DEV + k], device_id=dev, device_id_type=MESH)
                cp.wait_send()
                cp.wait_recv()

    bufs = list(srcs) + list(lands)
    outs = pl.pallas_call(
        body, name=name, out_shape=tuple(pltpu.HBM(a.shape, a.dtype) for a in bufs),
        in_specs=[_HBM] * (2 * n) + [_SEM, _SEM, pl.BlockSpec(memory_space=pl.ANY)], out_specs=[_HBM] * (2 * n),
        input_output_aliases={i: i for i in range(2 * n)},
        compiler_params=pltpu.CompilerParams(has_side_effects=_EFFECT),
    )(*bufs, send, recv, after)
    return list(outs[:n]), list(outs[n:])


SHARDED = ("w_in", "glu_w1", "glu_w2", "w_out", "w_ff1", "w_ff2", "w_ple_gate", "w_ple_proj")
SMALL = ("attn_norm_g", "gmlp_ln_g", "gmlp_ln_b", "gmlp_ws", "gmlp_bs", "q_norm_g", "k_norm_g", "sinks", "ssm_a_re", "ssm_a_im",
         "ssm_log_dt", "ssm_b_re", "ssm_b_im", "ssm_c_re", "ssm_c_im", "ssm_d", "mix_out_g", "mlp_norm_g", "ple_norm_g")
WEIGHTS = ("attn_norm_g", "w_in", "gmlp_ln_g", "gmlp_ln_b", "gmlp_ws", "gmlp_bs", "q_norm_g", "k_norm_g", "sinks", "ssm_a_re", "ssm_a_im",
           "ssm_log_dt", "ssm_b_re", "ssm_b_im", "ssm_c_re", "ssm_c_im", "ssm_d", "glu_w1", "glu_w2", "mix_out_g", "w_out", "mlp_norm_g",
           "w_ff1", "w_ff2", "ple_norm_g", "w_ple_gate", "w_ple_proj")
FLAT_COLS = 1024


PACK_TILE_ROWS = 8
PACK_ROWS_MULTIPLE = PACK_TILE_ROWS * N_DEV


def _packed_rows(shape):
    return -(-math.prod(shape) // (PACK_TILE_ROWS * FLAT_COLS)) * PACK_TILE_ROWS


def _pack(arrs, dtype):
    blocks = []
    for a in arrs:
        flat = a.astype(dtype).reshape(-1)
        pad = _packed_rows(a.shape) * FLAT_COLS - flat.shape[0]
        if pad:
            flat = jnp.concatenate([flat, jnp.zeros((pad,), dtype)])
        blocks.append(flat.reshape(-1, FLAT_COLS))
    rows = sum(b.shape[0] for b in blocks)
    if rows % PACK_ROWS_MULTIPLE:
        blocks.append(jnp.zeros((PACK_ROWS_MULTIPLE - rows % PACK_ROWS_MULTIPLE, FLAT_COLS), dtype))
    return jnp.concatenate(blocks, axis=0)


def _unpack(flat, shapes):
    out, r = [], 0
    for s in shapes:
        nr = _packed_rows(s)
        out.append(flat[r:r + nr].reshape(-1)[:math.prod(s)].reshape(s))
        r += nr
    return out


def _col_major(w):
    rows, cols = w.shape
    return w.reshape(rows, N_DEV, cols // N_DEV).transpose(1, 0, 2)


def _from_col_major(s):
    n, rows, cs = s.shape
    return s.transpose(1, 0, 2).reshape(rows, n * cs)


EARLY = ("w_in", "glu_w1", "glu_w2")
LATE = ("w_out", "w_ff1", "w_ff2", "w_ple_gate", "w_ple_proj")
GRADS_MID = ("w_ff1", "w_ff2", "w_ple_gate", "w_ple_proj")
GRADS_END = ("w_in", "glu_w1", "glu_w2", "w_out")


def _layer_fwd(h, p, l, cs, sn, W, late_weights, S, sp):
    z = _inproj_fwd(h, S["attn_norm_g"], W["w_in"])
    ya = _gmlp_fwd(z, S["lng"], S["lnb"], S["gmlp_ws"], S["bsx"])
    yb = _attn_fwd(z, cs, sn, S["qg"], S["kg"], S["sinks"])
    yc, xr, xi = _ssm_fwd(z, sp["bbd"], sp["pwr"], sp["pwi"], sp["cbd"], S["ssm_d"], W["glu_w1"], W["glu_w2"])
    late, token = late_weights(yc)
    W = {**W, **late}
    h1 = _merge_fwd(h, ya, yb, yc, _behind(S["mix_out_g"], token), W["w_out"])
    h2, r = _mlp_fwd(h1, S["mlp_norm_g"], W["w_ff1"], W["w_ff2"])
    h3 = _ple_fwd(h2, p, l, S["ple_norm_g"], W["w_ple_gate"], W["w_ple_proj"])
    return h3, dict(h=h, z=z, ya=ya, yb=yb, yc=yc, xr=xr, xi=xi, h1=h1, r=r, h2=h2), W


def _layer_bwd(g3, p, l, cs, sn, W, S, sp, A, raw, mid_bwd):
    G = {}
    g2, hn3, dgp, de, G["ple_norm_g"] = _ple_bwd(g3, A["h2"], p, l, S["ple_norm_g"], W["w_ple_gate"], W["w_ple_proj"])
    G["w_ple_gate"] = _tn(hn3, dgp, D_MODEL, D_MODEL, bm=1024, bn=1024, dtype=WIRE, name="tn_gate").reshape(N_DEV, -1, D_MODEL)
    G["w_ple_proj"] = _tn(p, de, PLE_DIM, D_MODEL, bm=PLE_DIM, bn=D_MODEL, a_lead=l, split=D_MODEL // N_DEV, dtype=WIRE, name="tn_proj")
    g1, hn, da, G["mlp_norm_g"] = _mlp_bwd(g2, A["h1"], S["mlp_norm_g"], A["r"], W["w_ff1"], W["w_ff2"])
    G["w_ff1"] = _tn(hn, da, D_MODEL, D_FF, bm=D_MODEL, bn=FF_BLOCK, n_major=True, dtype=WIRE, name="tn_ff1")
    G["w_ff2"] = _tn(A["r"], g2, D_FF, D_MODEL, bm=1024, bn=1024, dtype=WIRE, name="tn_ff2").reshape(N_DEV, -1, D_MODEL)
    token = mid_bwd(g1, G)
    dya, dyb, dyc, yn, G["mix_out_g"] = _merge_bwd(g1, A["ya"], A["yb"], A["yc"], _behind(S["mix_out_g"], token), W["w_out"])
    G["w_out"] = _tn(yn, g1, D_MODEL, D_MODEL, bm=1024, bn=1024, dtype=WIRE, name="tn_out").reshape(N_DEV, -1, D_MODEL)
    dzc, y2, da1, da2, dy, arai, dlr, dli, dd = _ssm_bwd(dyc, A["z"], A["xr"], A["xi"], sp["bbd"], sp["pwr"], sp["pwi"], sp["cbd"],
                                                        S["ssm_d"], W["glu_w1"], W["glu_w2"])
    G["glu_w1"] = _tn(y2, da1, C_WIDTH, C_WIDTH, bm=256, bn=256, dtype=WIRE, name="tn_glu1").reshape(N_DEV, -1, C_WIDTH)
    G["glu_w2"] = _tn(y2, da2, C_WIDTH, C_WIDTH, bm=256, bn=256, dtype=WIRE, name="tn_glu2").reshape(N_DEV, -1, C_WIDTH)
    dbbd = _tn(A["z"], arai, C_WIDTH, 2 * N_STATE, bm=256, bn=1024, a_off=5, name="tn_bbd")
    dcr = _tn(dy, A["xr"], C_WIDTH, N_STATE, bm=256, bn=1024, name="tn_cre")
    dci = _tn(dy, A["xi"], C_WIDTH, N_STATE, bm=256, bn=1024, name="tn_cim")
    dare, daim, dldt, dbtr, dbti, dcre, dcim = _ssm_param_bwd(
        raw["are"], raw["aim"], raw["ldt"], raw["are_x"], raw["aim_x"], raw["ldt_x"], raw["btr"], raw["bti"],
        dlr.reshape(C_GROUPS, C_STATE), dli.reshape(C_GROUPS, C_STATE), dbbd, dcr, dci)
    G["ssm_a_re"], G["ssm_a_im"], G["ssm_log_dt"] = dare, daim, dldt[:, 0]
    G["ssm_b_re"] = dbtr.reshape(C_GROUPS, C_GROUP, C_STATE).transpose(0, 2, 1)
    G["ssm_b_im"] = dbti.reshape(C_GROUPS, C_GROUP, C_STATE).transpose(0, 2, 1)
    G["ssm_c_re"] = dcre.reshape(C_GROUPS, C_GROUP, C_STATE)
    G["ssm_c_im"] = dcim.reshape(C_GROUPS, C_GROUP, C_STATE)
    G["ssm_d"] = dd.reshape(C_GROUPS, C_GROUP)
    dzq, dzk, dzv, dqg, dkg, dsk = _attn_bwd(A["z"], cs, sn, S["qg"], S["kg"], S["sinks"], A["yb"], dyb)
    G["q_norm_g"] = dqg[0, :HEAD_DIM] + dqg[0, HEAD_DIM:]
    G["k_norm_g"] = dkg[0, :HEAD_DIM] + dkg[0, HEAD_DIM:]
    G["sinks"] = dsk[:, 0]
    dza, dws, dbs, dlng, dlnb = _gmlp_bwd(A["z"], dya, S["lng"], S["lnb"], S["gmlp_ws"], S["bsx"])
    G["gmlp_ws"] = dws
    G["gmlp_bs"] = dbs[:, :, 0]
    G["gmlp_ln_g"] = dlng.reshape(A_HEADS, 2, HEAD_DIM)[:, 1]
    G["gmlp_ln_b"] = dlnb.reshape(A_HEADS, 2, HEAD_DIM)[:, 1]
    g0, xn, dz, G["attn_norm_g"] = _inproj_bwd(g1, A["h"], S["attn_norm_g"], W["w_in"], dza, dzq, dzk, dzv, dzc)
    G["w_in"] = _tn(xn, dz, D_MODEL, IN_COLS, bm=D_MODEL, bn=IN_COLS, split=IN_COLS // N_DEV, dtype=WIRE, name="tn_in")
    return g0, G


def _small_layouts(P, l):
    def row(a):
        return a.reshape(1, -1)

    zeros = jnp.zeros((A_HEADS, HEAD_DIM), F32)
    S = dict(
        attn_norm_g=row(P["attn_norm_g"][l]), mix_out_g=row(P["mix_out_g"][l]), mlp_norm_g=row(P["mlp_norm_g"][l]),
        ple_norm_g=row(P["ple_norm_g"][l]),
        lng=jnp.stack([zeros, P["gmlp_ln_g"][l]], axis=1).reshape(1, 512),
        lnb=jnp.stack([zeros, P["gmlp_ln_b"][l]], axis=1).reshape(1, 512),
        gmlp_ws=P["gmlp_ws"][l],
        bsx=jnp.broadcast_to(P["gmlp_bs"][l][:, :, None], (A_HEADS, CHUNK, CHUNK)),
        qg=jnp.tile(P["q_norm_g"][l], 2).reshape(1, LANES), kg=jnp.tile(P["k_norm_g"][l], 2).reshape(1, LANES),
        sinks=jnp.broadcast_to(P["sinks"][l][:, None], (8, LANES)),
        ssm_d=row(P["ssm_d"][l]),
    )
    are, aim = P["ssm_a_re"][l], P["ssm_a_im"][l]
    ldt = jnp.broadcast_to(P["ssm_log_dt"][l][:, None], (C_GROUPS, C_STATE))
    raw = dict(
        are=are, aim=aim, ldt=ldt,
        are_x=jnp.repeat(are, C_GROUP, axis=0), aim_x=jnp.repeat(aim, C_GROUP, axis=0), ldt_x=jnp.repeat(ldt, C_GROUP, axis=0),
        btr=P["ssm_b_re"][l].transpose(0, 2, 1).reshape(C_WIDTH, C_STATE), bti=P["ssm_b_im"][l].transpose(0, 2, 1).reshape(C_WIDTH, C_STATE),
        cre=P["ssm_c_re"][l].reshape(C_WIDTH, C_STATE), cim=P["ssm_c_im"][l].reshape(C_WIDTH, C_STATE),
    )
    return S, raw


def _ssm_prep_layer(raw):
    bbd, cbd, pwr, pwi = _ssm_prep(raw["are"].reshape(1, N_STATE), raw["aim"].reshape(1, N_STATE), raw["ldt"].reshape(1, N_STATE),
                                   raw["are_x"], raw["aim_x"], raw["ldt_x"], raw["btr"], raw["bti"], raw["cre"], raw["cim"])
    return dict(bbd=bbd, cbd=cbd, pwr=pwr, pwi=pwi)


def _behind(row, token):
    return row if token is None else row + token[0:1, 0:1]


def _local_step(x, p, positions, target, P, weights_of, mid_bwd, after_bwd):
    inv = 1.0 / (ROPE_THETA ** (jnp.arange(0, HEAD_DIM, 2, dtype=F32) / HEAD_DIM))
    cs, sn = _rope_tables(positions.reshape(-1, 1), jnp.tile(inv, 4).reshape(1, LANES))
    h = x
    acts, smalls, weights = [], [], []
    for l in range(DEPTH):
        W, late_weights, token = weights_of(l, h)
        S, raw = _small_layouts(P, l)
        sp = _ssm_prep_layer(raw)
        h, A, W = _layer_fwd(h, p, l, cs, sn, W, late_weights, {**S, "attn_norm_g": _behind(S["attn_norm_g"], token)}, sp)
        acts.append(A)
        smalls.append((S, raw, sp))
        weights.append(W)
    g, lsum = _loss_head(h, target)
    grads = [None] * DEPTH
    token = None
    for l in reversed(range(DEPTH)):
        S, raw, sp = smalls[l]
        g, grads[l] = _layer_bwd(g, p, l, cs, sn, weights[l], {**S, "ple_norm_g": _behind(S["ple_norm_g"], token)}, sp, acts[l], raw,
                                 functools.partial(mid_bwd, l))
        token = after_bwd(l, g, grads[l])
    return lsum[0, 0], g, grads


def _layer_weights(g):
    layout = dict(
        w_in=_from_col_major, glu_w1=lambda a: a.reshape(C_WIDTH, C_WIDTH), glu_w2=lambda a: a.reshape(C_WIDTH, C_WIDTH),
        w_out=lambda a: a.reshape(D_MODEL, D_MODEL), w_ff1=_from_col_major, w_ff2=lambda a: a.reshape(D_FF, D_MODEL),
        w_ple_gate=lambda a: a.reshape(D_MODEL, D_MODEL), w_ple_proj=_from_col_major)
    return {n: layout[n](a) for n, a in g.items()}


def kernel(x, p, positions, attn_norm_g, w_in, gmlp_ln_g, gmlp_ln_b, gmlp_ws, gmlp_bs, q_norm_g, k_norm_g, sinks, ssm_a_re, ssm_a_im, ssm_log_dt, ssm_b_re, ssm_b_im, ssm_c_re, ssm_c_im, ssm_d, glu_w1, glu_w2, mix_out_g, w_out, mlp_norm_g, w_ff1, w_ff2, ple_norm_g, w_ple_gate, w_ple_proj, loss_target, m_attn_norm_g, m_w_in, m_gmlp_ln_g, m_gmlp_ln_b, m_gmlp_ws, m_gmlp_bs, m_q_norm_g, m_k_norm_g, m_sinks, m_ssm_a_re, m_ssm_a_im, m_ssm_log_dt, m_ssm_b_re, m_ssm_b_im, m_ssm_c_re, m_ssm_c_im, m_ssm_d, m_glu_w1, m_glu_w2, m_mix_out_g, m_w_out, m_mlp_norm_g, m_w_ff1, m_w_ff2, m_ple_norm_g, m_w_ple_gate, m_w_ple_proj, v_attn_norm_g, v_w_in, v_gmlp_ln_g, v_gmlp_ln_b, v_gmlp_ws, v_gmlp_bs, v_q_norm_g, v_k_norm_g, v_sinks, v_ssm_a_re, v_ssm_a_im, v_ssm_log_dt, v_ssm_b_re, v_ssm_b_im, v_ssm_c_re, v_ssm_c_im, v_ssm_d, v_glu_w1, v_glu_w2, v_mix_out_g, v_w_out, v_mlp_norm_g, v_w_ff1, v_w_ff2, v_ple_norm_g, v_w_ple_gate, v_w_ple_proj):
    env = dict(locals())
    P = {n: env[n] for n in WEIGHTS}
    M = {n: env["m_" + n] for n in WEIGHTS}
    V = {n: env["v_" + n] for n in WEIGHTS}
    return _step(x, p, positions, loss_target, P, M, V)


def _step(x, p, positions, loss_target, P, M, V):
    small_shapes = [P[n].shape for n in SMALL]
    me = 4 * lax.axis_index("x") + 2 * lax.axis_index("y") + lax.axis_index("c")
    nothing = jnp.zeros((8, LANES), F32)

    def put(land, own, lead):
        return lax.dynamic_update_slice(land, own.reshape((1,) * len(lead) + own.shape), tuple(lead) + (0,) * own.ndim)

    def gather_start(l, names, after, tag):
        shards = [P[n][l].astype(WIRE) for n in names]
        lands = [lax.empty((N_DEV,) + s.shape, WIRE) for s in shards]
        send, recv, shards, lands, token = _split_start(shards, lands, ["all"] * len(names), 0, after, f"gather_start_{l}{tag}")
        return dict(names=names, send=send, recv=recv, shards=shards, lands=lands, token=token, name=f"gather_wait_{l}{tag}")

    def gather_wait(f, after):
        shards, lands = _split_wait(f["send"], f["recv"], f["shards"], f["lands"], ["all"] * len(f["names"]), 0, after, f["name"])
        return dict(zip(f["names"], [put(ld, sh, (me,)) for sh, ld in zip(shards, lands)]))

    first = gather_start(0, EARLY, nothing, "a")
    flying = {0: (first, gather_start(0, LATE, first["token"], "b"))}

    def weights_of(l, h):
        fa, fb = flying.pop(l)
        got = gather_wait(fa, h)
        if fb is None:
            token = None
            if l + 1 < DEPTH:
                flying[l + 1] = (gather_start(l + 1, SHARDED, got["w_in"], ""), None)
                token = flying[l + 1][0]["token"]
            W = _layer_weights(got)
            return {n: W[n] for n in EARLY}, (lambda after: ({n: W[n] for n in LATE}, None)), token

        def late_weights(after):
            late = gather_wait(fb, after)
            flying[l + 1] = (gather_start(l + 1, SHARDED, late["w_out"], ""), None)
            return _layer_weights(late), flying[l + 1][0]["token"]

        return _layer_weights(got), late_weights, fb["token"]

    grad_lands = {n: lax.empty((N_DEV,) + P[n].shape, WIRE) for n in SHARDED}
    sent = []

    def scatter_start(l, names, parts, lands, tag):
        send, recv, parts, lands, token = _split_start(parts, lands, ["own"] * len(parts), l, nothing, f"scatter_start_{l}{tag}")
        sent.append(dict(l=l, names=names, send=send, recv=recv, parts=parts, lands=lands, name=f"scatter_wait_{l}{tag}"))
        return token

    def scatter_wait(after):
        f = sent.pop(0)
        parts, lands = _split_wait(f["send"], f["recv"], f["parts"], f["lands"], ["own"] * len(f["parts"]), f["l"], after, f["name"])
        lands = [put(ld, lax.dynamic_index_in_dim(part, me, 0, keepdims=False), (me, f["l"])) for part, ld in zip(parts, lands)]
        return dict(zip(f["names"], lands))

    def mid_bwd(l, g1, G):
        if l > 0:
            return None
        grad_lands.update(scatter_wait(g1))
        return scatter_start(0, GRADS_MID, [G[n] for n in GRADS_MID], [grad_lands[n] for n in GRADS_MID], "a")

    small_land = []

    def after_bwd(l, g, G):
        if l > 0:
            if sent:
                grad_lands.update(scatter_wait(g))
            return scatter_start(l, SHARDED, [G[n] for n in SHARDED], [grad_lands[n] for n in SHARDED], "")
        sflat = _pack([jnp.stack([grads_of[k][n] for k in range(DEPTH)]) for n in SMALL], F32)
        sparts = sflat.reshape(N_DEV, -1, FLAT_COLS)
        small_land.append(sflat.shape)
        return scatter_start(0, GRADS_END + ("small",), [G[n] for n in GRADS_END] + [sparts],
                             [grad_lands[n] for n in GRADS_END] + [lax.empty((N_DEV, 1) + sparts.shape[1:], F32)], "b")

    grads_of = {}

    def after_bwd_recording(l, g, G):
        grads_of[l] = G
        return after_bwd(l, g, G)

    lsum, gx, grads = _local_step(x[0], p[:, 0], positions[0], loss_target[0], P, weights_of, mid_bwd, after_bwd_recording)
    grad_lands.update(scatter_wait(gx))
    last = scatter_wait(gx)
    small_parts = last.pop("small")
    grad_lands.update(last)
    G, delta, new_m, new_v = {}, {}, {}, {}
    for n in SHARDED:
        shp = P[n].shape
        res = _sum_adamw(grad_lands[n].reshape(N_DEV, -1, shp[-1]), *(a.reshape(-1, shp[-1]) for a in (P[n], M[n], V[n])))
        G[n], delta[n], new_m[n], new_v[n] = (a.reshape(shp) for a in res)
    small_sum = _gather_whole(_sum_slots(small_parts[:, 0])).reshape((1,) + small_land[0])
    res = _sum_adamw(small_sum, _pack([P[n] for n in SMALL], F32), _pack([M[n] for n in SMALL], F32), _pack([V[n] for n in SMALL], F32))
    for dst, flat in zip((G, delta, new_m, new_v), res):
        dst.update(zip(SMALL, _unpack(flat, small_shapes)))
    loss = lax.psum(lsum, ("x", "y", "c"))
    return (loss, gx[None], *[G[n] for n in WEIGHTS], *[delta[n] for n in WEIGHTS], *[new_m[n] for n in WEIGHTS], *[new_v[n] for n in WEIGHTS])
```

```python
import functools
import math

import jax
import jax.numpy as jnp
from jax import lax
from jax.experimental import pallas as pl
from jax.experimental.pallas import tpu as pltpu

F32 = jnp.float32
MXU = jnp.bfloat16
WIRE = jnp.bfloat16

D_MODEL = 1024
DEPTH = 4
HEAD_DIM = 64
A_WIDTH = 256
A_HEADS = 4
CHUNK = 128
B_WIDTH = 512
WINDOW = 128
C_WIDTH = 256
C_GROUP = 16
C_GROUPS = 16
C_STATE = 64
N_STATE = C_GROUPS * C_STATE
IN_COLS = 1536
D_FF = 4096
PLE_DIM = 256
EPS = 1e-6
ROPE_THETA = 10000.0
SCALE = HEAD_DIM ** -0.5
NEG = -1e30
N_DEV = 8

ADAM_LR = 0.001
ADAM_B1 = 0.9
ADAM_B2 = 0.999
ADAM_EPS = 1e-08
ADAM_WD = 0.01
ADAM_STEP = 10

V7X_VMEM_BYTES = 64 * 2 ** 20
VMEM_LIMIT = V7X_VMEM_BYTES - 8 * 2 ** 20
LANES = 128

MESH = pl.DeviceIdType.MESH


def _cp(*sem):
    return pltpu.CompilerParams(dimension_semantics=sem, vmem_limit_bytes=VMEM_LIMIT)


def _sds(shape, dtype=F32):
    return jax.ShapeDtypeStruct(shape, dtype)


def _mm(a, b):
    return jnp.dot(a.astype(MXU), b.astype(MXU), preferred_element_type=F32)


def _mm_nt(a, b):
    return lax.dot_general(a.astype(MXU), b.astype(MXU), (((1,), (1,)), ((), ())), preferred_element_type=F32)


def _mm_tn(a, b):
    return lax.dot_general(a.astype(MXU), b.astype(MXU), (((0,), (0,)), ((), ())), preferred_element_type=F32)


def _lane(shape):
    return lax.broadcasted_iota(jnp.int32, shape, len(shape) - 1)


def _row(shape):
    return lax.broadcasted_iota(jnp.int32, shape, 0)


_GELU_C = math.sqrt(2.0 / math.pi)


def _gelu(x):
    return 0.5 * x * (1.0 + jnp.tanh(_GELU_C * (x + 0.044715 * (x * x * x))))


def _gelu_grad(x):
    t = jnp.tanh(_GELU_C * (x + 0.044715 * (x * x * x)))
    return 0.5 * (1.0 + t) + 0.5 * x * (1.0 - t * t) * (_GELU_C * (1.0 + 3.0 * 0.044715 * (x * x)))


def _sigmoid(x):
    return 1.0 / (1.0 + jnp.exp(-x))


def _rms_stat(x):
    return lax.rsqrt(jnp.mean(x * x, axis=-1, keepdims=True) + EPS)


def _rms_bwd(x, r, g, dy):
    xh = x * r
    dxh = dy * g
    dx = r * (dxh - xh * jnp.mean(dxh * xh, axis=-1, keepdims=True))
    return dx, jnp.sum(dy * xh, axis=0, keepdims=True)


def _tril(w):
    return jnp.where(_row(w.shape) >= _lane(w.shape), w, 0.0)


def _swap64(x):
    return pltpu.roll(x, HEAD_DIM, 1)


def _group_sum64(x, lo):
    s_lo = jnp.sum(jnp.where(lo, x, 0.0), axis=-1, keepdims=True)
    s_hi = jnp.sum(jnp.where(lo, 0.0, x), axis=-1, keepdims=True)
    return jnp.where(lo, s_lo, s_hi)


def _partner(x):
    n = x.shape[-1]
    first = (_lane(x.shape) % HEAD_DIM) < HEAD_DIM // 2
    return jnp.where(first, pltpu.roll(x, n - HEAD_DIM // 2, 1), pltpu.roll(x, HEAD_DIM // 2, 1))


def _rope(y, cs, sn):
    return y * cs + _partner(y) * sn


def _rope_bwd(d, cs, sn):
    return d * cs + _partner(d * sn)


def _qk_norm_rope(x, g, cs, sn):
    lo = _lane(x.shape) < HEAD_DIM
    r = lax.rsqrt(_group_sum64(x * x, lo) * (1.0 / HEAD_DIM) + EPS)
    xh = x * r
    return _rope(xh * g, cs, sn), xh, r


def _qk_norm_rope_bwd(xh, r, g, cs, sn, d):
    lo = _lane(xh.shape) < HEAD_DIM
    dy = _rope_bwd(d, cs, sn)
    dxh = dy * g
    m = _group_sum64(dxh * xh, lo) * (1.0 / HEAD_DIM)
    return r * (dxh - xh * m), jnp.sum(dy * xh, axis=0, keepdims=True)


def _gmlp_head(blk, g, b):
    hi = _lane(blk.shape) >= HEAD_DIM
    mu = jnp.sum(jnp.where(hi, blk, 0.0), axis=-1, keepdims=True) * (1.0 / HEAD_DIM)
    xc = jnp.where(hi, blk - mu, 0.0)
    rstd = lax.rsqrt(jnp.sum(xc * xc, axis=-1, keepdims=True) * (1.0 / HEAD_DIM) + EPS)
    vhat = xc * rstd
    return vhat * g + b, vhat, rstd


def _rope_tables(pos_col, inv_row):
    T = pos_col.shape[0]
    tm = min(T, 1024)

    def body(p_ref, inv_ref, cs_ref, sn_ref):
        ang = p_ref[...].astype(F32) * inv_ref[...]
        s = jnp.sin(ang)
        cs_ref[...] = jnp.cos(ang)
        sn_ref[...] = jnp.where((_lane(ang.shape) % HEAD_DIM) < HEAD_DIM // 2, -s, s)

    blk = pl.BlockSpec((tm, LANES), lambda i: (i, 0))
    return pl.pallas_call(
        body, name="rope_tables", grid=(T // tm,),
        in_specs=[pl.BlockSpec((tm, 1), lambda i: (i, 0)), pl.BlockSpec((1, LANES), lambda i: (0, 0))],
        out_specs=[blk, blk], out_shape=[_sds((T, LANES))] * 2, compiler_params=_cp("parallel"))(pos_col, inv_row)


def _inproj_fwd(h, g, w):
    T = h.shape[0]
    tm = min(T, 512)

    def body(h_ref, g_ref, w_ref, z_ref):
        x = h_ref[...]
        z_ref[...] = _mm(x * _rms_stat(x) * g_ref[...], w_ref[...])

    return pl.pallas_call(
        body, name="inproj_fwd", grid=(T // tm,),
        in_specs=[pl.BlockSpec((tm, D_MODEL), lambda i: (i, 0)), pl.BlockSpec((1, D_MODEL), lambda i: (0, 0)),
                  pl.BlockSpec((D_MODEL, IN_COLS), lambda i: (0, 0))],
        out_specs=pl.BlockSpec((tm, IN_COLS), lambda i: (i, 0)), out_shape=_sds((T, IN_COLS)),
        compiler_params=_cp("parallel"))(h, g, w)


def _inproj_bwd(gres, h, g, w, dza, dzq, dzk, dzv, dzc):
    T = h.shape[0]
    tm = min(T, 512)

    def body(gr_ref, h_ref, g_ref, w_ref, a_ref, q_ref, k_ref, v_ref, c_ref, dh_ref, xn_ref, dz_ref, dg_ref):
        @pl.when(pl.program_id(0) == 0)
        def _():
            dg_ref[...] = jnp.zeros_like(dg_ref)

        x = h_ref[...]
        r = _rms_stat(x)
        gg = g_ref[...]
        dz = jnp.concatenate([a_ref[...], q_ref[...], k_ref[...], v_ref[...], c_ref[...]], axis=1)
        dxn = _mm_nt(dz, w_ref[...])
        dx, dg = _rms_bwd(x, r, gg, dxn)
        dh_ref[...] = gr_ref[...] + dx
        dg_ref[...] += dg
        xn_ref[...] = (x * r * gg).astype(MXU)
        dz_ref[...] = dz.astype(MXU)

    def rows(w_):
        return pl.BlockSpec((tm, w_), lambda i: (i, 0))

    row = pl.BlockSpec((1, D_MODEL), lambda i: (0, 0))
    return pl.pallas_call(
        body, name="inproj_bwd", grid=(T // tm,),
        in_specs=[rows(D_MODEL), rows(D_MODEL), row, pl.BlockSpec((D_MODEL, IN_COLS), lambda i: (0, 0)),
                  rows(512), rows(512), rows(128), rows(128), rows(256)],
        out_specs=[rows(D_MODEL), rows(D_MODEL), rows(IN_COLS), row],
        out_shape=[_sds((T, D_MODEL)), _sds((T, D_MODEL), MXU), _sds((T, IN_COLS), MXU), _sds((1, D_MODEL))],
        compiler_params=_cp("arbitrary"))(gres, h, g, w, dza, dzq, dzk, dzv, dzc)


def _gmlp_fwd(z, lng, lnb, ws, bsx):
    T = z.shape[0]
    tm = min(T, 512)
    nc = tm // CHUNK

    def body(z_ref, g_ref, b_ref, w_ref, bs_ref, ya_ref):
        zg = _gelu(z_ref[...])
        lo = _lane((tm, LANES)) < HEAD_DIM
        prods = []
        for hd in range(A_HEADS):
            sl = slice(hd * LANES, (hd + 1) * LANES)
            blk = zg[:, sl]
            vn, _, _ = _gmlp_head(blk, g_ref[:, sl], b_ref[:, sl])
            wm = _tril(w_ref[hd])
            sv = jnp.concatenate([_mm(wm, vn[c * CHUNK:(c + 1) * CHUNK]) + bs_ref[hd] for c in range(nc)], axis=0)
            prods.append(blk * _swap64(sv))
        ya_ref[:, 0:LANES] = jnp.where(lo, prods[0], _swap64(prods[1]))
        ya_ref[:, LANES:2 * LANES] = jnp.where(lo, prods[2], _swap64(prods[3]))

    row = pl.BlockSpec((1, 512), lambda i: (0, 0))
    mat = pl.BlockSpec((A_HEADS, CHUNK, CHUNK), lambda i: (0, 0, 0))
    return pl.pallas_call(
        body, name="gmlp_fwd", grid=(T // tm,),
        in_specs=[pl.BlockSpec((tm, 512), lambda i: (i, 0)), row, row, mat, mat],
        out_specs=pl.BlockSpec((tm, A_WIDTH), lambda i: (i, 0)), out_shape=_sds((T, A_WIDTH)),
        compiler_params=_cp("parallel"))(z, lng, lnb, ws, bsx)


def _gmlp_bwd(z, dya, lng, lnb, ws, bsx):
    T = z.shape[0]
    tm = min(T, 512)
    nc = tm // CHUNK

    def body(z_ref, dya_ref, g_ref, b_ref, w_ref, bs_ref, dza_ref, dw_ref, dbs_ref, dg_ref, db_ref):
        @pl.when(pl.program_id(0) == 0)
        def _():
            dw_ref[...] = jnp.zeros_like(dw_ref)
            dbs_ref[...] = jnp.zeros_like(dbs_ref)
            dg_ref[...] = jnp.zeros_like(dg_ref)
            db_ref[...] = jnp.zeros_like(db_ref)

        za = z_ref[...]
        zg = _gelu(za)
        gp = _gelu_grad(za)
        lo = _lane((tm, LANES)) < HEAD_DIM
        for hd in range(A_HEADS):
            sl = slice(hd * LANES, (hd + 1) * LANES)
            blk = zg[:, sl]
            g = g_ref[:, sl]
            vn, vhat, rstd = _gmlp_head(blk, g, b_ref[:, sl])
            wm = _tril(w_ref[hd])
            pair = dya_ref[:, (hd // 2) * LANES:(hd // 2 + 1) * LANES]
            dy = jnp.where(lo, pair if hd % 2 == 0 else _swap64(pair), 0.0)
            dsv = _swap64(dy * blk)
            svs, dvns = [], []
            dw = jnp.zeros((CHUNK, CHUNK), F32)
            dbs = jnp.zeros((CHUNK, 1), F32)
            for c in range(nc):
                cs = slice(c * CHUNK, (c + 1) * CHUNK)
                svs.append(_mm(wm, vn[cs]) + bs_ref[hd])
                dw = dw + _mm_nt(dsv[cs], vn[cs])
                dbs = dbs + jnp.sum(dsv[cs], axis=-1, keepdims=True)
                dvns.append(_mm_tn(wm, dsv[cs]))
            sv = jnp.concatenate(svs, axis=0)
            dvn = jnp.concatenate(dvns, axis=0)
            dw_ref[hd] += _tril(dw)
            dbs_ref[hd] += jnp.broadcast_to(dbs, (CHUNK, CHUNK))
            dg_ref[:, sl] += jnp.sum(dvn * vhat, axis=0, keepdims=True)
            db_ref[:, sl] += jnp.sum(dvn, axis=0, keepdims=True)
            du = dy * _swap64(sv)
            dvh = dvn * g
            m1 = jnp.sum(dvh, axis=-1, keepdims=True) * (1.0 / HEAD_DIM)
            m2 = jnp.sum(dvh * vhat, axis=-1, keepdims=True) * (1.0 / HEAD_DIM)
            dv = jnp.where(lo, 0.0, rstd * (dvh - m1 - vhat * m2))
            dza_ref[:, sl] = (du + dv) * gp[:, sl]

    row = pl.BlockSpec((1, 512), lambda i: (0, 0))
    mat = pl.BlockSpec((A_HEADS, CHUNK, CHUNK), lambda i: (0, 0, 0))
    return pl.pallas_call(
        body, name="gmlp_bwd", grid=(T // tm,),
        in_specs=[pl.BlockSpec((tm, 512), lambda i: (i, 0)), pl.BlockSpec((tm, A_WIDTH), lambda i: (i, 0)), row, row, mat, mat],
        out_specs=[pl.BlockSpec((tm, 512), lambda i: (i, 0)), mat, mat, row, row],
        out_shape=[_sds((T, 512)), _sds((A_HEADS, CHUNK, CHUNK)), _sds((A_HEADS, CHUNK, CHUNK)), _sds((1, 512)), _sds((1, 512))],
        compiler_params=_cp("arbitrary"))(z, dya, lng, lnb, ws, bsx)


def _attn_specs(T, tq, tile_of):
    nb = tq // WINDOW

    def prev(i):
        return jnp.maximum(tile_of(i) * nb - 1, 0)

    row = pl.BlockSpec((1, LANES), lambda i: (0, 0))
    return [
        pl.BlockSpec((tq, B_WIDTH), lambda i: (tile_of(i), 1)),
        pl.BlockSpec((tq, LANES), lambda i: (tile_of(i), 8)),
        pl.BlockSpec((tq, LANES), lambda i: (tile_of(i), 9)),
        pl.BlockSpec((WINDOW, LANES), lambda i: (prev(i), 8)),
        pl.BlockSpec((WINDOW, LANES), lambda i: (prev(i), 9)),
        pl.BlockSpec((tq, LANES), lambda i: (tile_of(i), 0)),
        pl.BlockSpec((tq, LANES), lambda i: (tile_of(i), 0)),
        pl.BlockSpec((WINDOW, LANES), lambda i: (prev(i), 0)),
        pl.BlockSpec((WINDOW, LANES), lambda i: (prev(i), 0)),
        row, row,
        pl.BlockSpec((8, LANES), lambda i: (0, 0)),
    ]


def _attn_bias(first):
    qi = lax.broadcasted_iota(jnp.int32, (WINDOW, 2 * WINDOW), 0)
    kj = lax.broadcasted_iota(jnp.int32, (WINDOW, 2 * WINDOW), 1)
    diff = qi + WINDOW - kj
    ok = (diff >= 0) & (diff < WINDOW) & ((kj >= WINDOW) | jnp.logical_not(first))
    return jnp.where(ok, 0.0, NEG)


def _dup_heads(x, lo):
    sw = _swap64(x)
    return jnp.where(lo, x, sw), jnp.where(lo, sw, x)


HEADS_PER_KV = 4


def _stack_heads(x0, x1, lo):
    return jnp.concatenate([jnp.where(lo, x0, 0.0), jnp.where(lo, 0.0, x0), jnp.where(lo, x1, 0.0), jnp.where(lo, 0.0, x1)], axis=0)


def _unstack_heads(x4, lo):
    return (jnp.where(lo, x4[0:WINDOW], x4[WINDOW:2 * WINDOW]), jnp.where(lo, x4[2 * WINDOW:3 * WINDOW], x4[3 * WINDOW:4 * WINDOW]))


def _sink_column(sk_ref, g):
    return jnp.concatenate([jnp.broadcast_to(sk_ref[a:a + 1, 0:1], (WINDOW, 1)) for a in range(HEADS_PER_KV * g, HEADS_PER_KV * (g + 1))], axis=0)


def _attn_probs(q4, kw, bias, sink):
    s = _mm_nt(q4, kw)
    s = (s.reshape(HEADS_PER_KV, WINDOW, 2 * WINDOW) + bias[None]).reshape(HEADS_PER_KV * WINDOW, 2 * WINDOW)
    m = jnp.maximum(jnp.max(s, axis=-1, keepdims=True), sink)
    p = jnp.exp(s - m)
    es = jnp.exp(sink - m)
    inv = 1.0 / (jnp.sum(p, axis=-1, keepdims=True) + es)
    return p * inv, es * inv


def _attn_fwd(z, cs, sn, qg, kg, sinks):
    T = z.shape[0]
    tq = min(T, 512)
    nb = tq // WINDOW

    def body(q_ref, k_ref, v_ref, kp_ref, vp_ref, cs_ref, sn_ref, csp_ref, snp_ref, qg_ref, kg_ref, sk_ref, o_ref):
        i = pl.program_id(0)
        csq, snq = cs_ref[...], sn_ref[...]
        cs_all = jnp.concatenate([csp_ref[...], csq], axis=0)
        sn_all = jnp.concatenate([snp_ref[...], snq], axis=0)
        k_all = jnp.concatenate([kp_ref[...], k_ref[...]], axis=0)
        v_all = jnp.concatenate([vp_ref[...], v_ref[...]], axis=0)
        kr, _, _ = _qk_norm_rope(k_all, kg_ref[...], cs_all, sn_all)
        lo_all = _lane(k_all.shape) < HEAD_DIM
        kd = _dup_heads(kr, lo_all)
        vd = _dup_heads(v_all, lo_all)
        lo = _lane((WINDOW, LANES)) < HEAD_DIM
        qrs = [_qk_norm_rope(q_ref[:, pr * LANES:(pr + 1) * LANES], qg_ref[...], csq, snq)[0] * SCALE for pr in range(4)]
        biases = [_attn_bias(i * nb + b == 0) for b in range(nb)]
        for g in range(2):
            sink = _sink_column(sk_ref, g)
            for b in range(nb):
                bs = slice(b * WINDOW, (b + 1) * WINDOW)
                ws = slice(b * WINDOW, (b + 2) * WINDOW)
                pn, _ = _attn_probs(_stack_heads(qrs[2 * g][bs], qrs[2 * g + 1][bs], lo), kd[g][ws], biases[b], sink)
                o0, o1 = _unstack_heads(_mm(pn, vd[g][ws]), lo)
                o_ref[bs, 2 * g * LANES:(2 * g + 1) * LANES] = o0
                o_ref[bs, (2 * g + 1) * LANES:(2 * g + 2) * LANES] = o1

    return pl.pallas_call(
        body, name="attn_fwd", grid=(T // tq,),
        in_specs=_attn_specs(T, tq, lambda i: i),
        out_specs=pl.BlockSpec((tq, B_WIDTH), lambda i: (i, 0)), out_shape=_sds((T, B_WIDTH)),
        compiler_params=_cp("parallel"))(z, z, z, z, z, cs, sn, cs, sn, qg, kg, sinks)


def _attn_bwd(z, cs, sn, qg, kg, sinks, o, do):
    T = z.shape[0]
    tq = min(T, 512)
    nb = tq // WINDOW
    nt = T // tq
    tk = tq + WINDOW

    def tile_of(i):
        return nt - 1 - i

    def body(q_ref, k_ref, v_ref, kp_ref, vp_ref, cs_ref, sn_ref, csp_ref, snp_ref, qg_ref, kg_ref, sk_ref, o_ref, do_ref,
             dq_ref, dk_ref, dv_ref, dqg_ref, dkg_ref, dsk_ref, acck, accv, ck, cv):
        i = pl.program_id(0)
        ti = nt - 1 - i

        @pl.when(i == 0)
        def _():
            dqg_ref[...] = jnp.zeros_like(dqg_ref)
            dkg_ref[...] = jnp.zeros_like(dkg_ref)
            dsk_ref[...] = jnp.zeros_like(dsk_ref)
            ck[...] = jnp.zeros_like(ck)
            cv[...] = jnp.zeros_like(cv)

        csq, snq = cs_ref[...], sn_ref[...]
        cs_all = jnp.concatenate([csp_ref[...], csq], axis=0)
        sn_all = jnp.concatenate([snp_ref[...], snq], axis=0)
        k_all = jnp.concatenate([kp_ref[...], k_ref[...]], axis=0)
        v_all = jnp.concatenate([vp_ref[...], v_ref[...]], axis=0)
        kr, kh, rk = _qk_norm_rope(k_all, kg_ref[...], cs_all, sn_all)
        lo_all = _lane(k_all.shape) < HEAD_DIM
        kd = _dup_heads(kr, lo_all)
        vd = _dup_heads(v_all, lo_all)
        lo = _lane((WINDOW, LANES)) < HEAD_DIM
        acck[...] = jnp.zeros_like(acck)
        accv[...] = jnp.zeros_like(accv)
        prep = [_qk_norm_rope(q_ref[:, pr * LANES:(pr + 1) * LANES], qg_ref[...], csq, snq) for pr in range(4)]
        biases = [_attn_bias(ti * nb + b == 0) for b in range(nb)]
        dqs = [[None] * nb for _ in range(4)]
        for g in range(2):
            sink = _sink_column(sk_ref, g)
            dsink = jnp.zeros((HEADS_PER_KV * WINDOW, 1), F32)
            for b in range(nb):
                bs = slice(b * WINDOW, (b + 1) * WINDOW)
                ws = slice(b * WINDOW, (b + 2) * WINDOW)
                kw, vw = kd[g][ws], vd[g][ws]
                q4 = _stack_heads(prep[2 * g][0][bs] * SCALE, prep[2 * g + 1][0][bs] * SCALE, lo)
                pn, psink = _attn_probs(q4, kw, biases[b], sink)
                o0, o1 = o_ref[bs, 2 * g * LANES:(2 * g + 1) * LANES], o_ref[bs, (2 * g + 1) * LANES:(2 * g + 2) * LANES]
                do4 = _stack_heads(do_ref[bs, 2 * g * LANES:(2 * g + 1) * LANES], do_ref[bs, (2 * g + 1) * LANES:(2 * g + 2) * LANES], lo)
                delta = jnp.sum(do4 * jnp.concatenate([o0, o0, o1, o1], axis=0), axis=-1, keepdims=True)
                ds = pn * (_mm_nt(do4, vw) - delta)
                dsink = dsink - psink * delta
                dqs[2 * g][b], dqs[2 * g + 1][b] = _unstack_heads(_mm(ds, kw) * SCALE, lo)
                acck[g, ws, :] += _mm_tn(ds, q4)
                accv[g, ws, :] += _mm_tn(pn, do4)
            for hh in range(HEADS_PER_KV):
                a = HEADS_PER_KV * g + hh
                dsk_ref[a:a + 1, :] += jnp.zeros((1, LANES), F32) + jnp.sum(dsink[hh * WINDOW:(hh + 1) * WINDOW])
        for pr in range(4):
            _, qh, rq = prep[pr]
            dx, dg = _qk_norm_rope_bwd(qh, rq, qg_ref[...], csq, snq, jnp.concatenate(dqs[pr], axis=0))
            dq_ref[:, pr * LANES:(pr + 1) * LANES] = dx
            dqg_ref[...] += dg

        def fold(acc):
            f0 = acc[0] + _swap64(acc[0])
            f1 = acc[1] + _swap64(acc[1])
            return jnp.where(lo_all, f0, f1)

        dk_all = fold(acck)
        dv_all = fold(accv)
        pad = jnp.zeros((tq - WINDOW, LANES), F32)
        dk_own = dk_all[WINDOW:] + (jnp.concatenate([pad, ck[...]], axis=0) if nb > 1 else ck[...])
        dv_own = dv_all[WINDOW:] + (jnp.concatenate([pad, cv[...]], axis=0) if nb > 1 else cv[...])
        ck[...] = dk_all[:WINDOW]
        cv[...] = dv_all[:WINDOW]
        dxk, dgk = _qk_norm_rope_bwd(kh[WINDOW:], rk[WINDOW:], kg_ref[...], csq, snq, dk_own)
        dk_ref[...] = dxk
        dkg_ref[...] += dgk
        dv_ref[...] = dv_own

    row = pl.BlockSpec((1, LANES), lambda i: (0, 0))
    return pl.pallas_call(
        body, name="attn_bwd", grid=(nt,),
        in_specs=_attn_specs(T, tq, tile_of) + [pl.BlockSpec((tq, B_WIDTH), lambda i: (tile_of(i), 0))] * 2,
        out_specs=[pl.BlockSpec((tq, B_WIDTH), lambda i: (tile_of(i), 0)), pl.BlockSpec((tq, LANES), lambda i: (tile_of(i), 0)),
                   pl.BlockSpec((tq, LANES), lambda i: (tile_of(i), 0)), row, row, pl.BlockSpec((8, LANES), lambda i: (0, 0))],
        out_shape=[_sds((T, B_WIDTH)), _sds((T, LANES)), _sds((T, LANES)), _sds((1, LANES)), _sds((1, LANES)), _sds((8, LANES))],
        scratch_shapes=[pltpu.VMEM((2, tk, LANES), F32), pltpu.VMEM((2, tk, LANES), F32),
                        pltpu.VMEM((WINDOW, LANES), F32), pltpu.VMEM((WINDOW, LANES), F32)],
        compiler_params=_cp("arbitrary"))(z, z, z, z, z, cs, sn, cs, sn, qg, kg, sinks, o, do)


def _bbar_t(are, aim, ldt, btr, bti):
    lbr, lbi = _lam_bar(are, aim, ldt)
    den = are * are + aim * aim
    nr = lbr - 1.0
    cr = (nr * are + lbi * aim) / den
    ci = (lbi * are - nr * aim) / den
    return cr * btr - ci * bti, cr * bti + ci * btr


def _lam_bar(are, aim, ldt):
    dt = jnp.exp(ldt)
    er = jnp.exp(are * dt)
    return er * jnp.cos(aim * dt), er * jnp.sin(aim * dt)


def _block_diag(x):
    t = jnp.concatenate([x] * C_GROUPS, axis=1)
    return jnp.where(_row(t.shape) // C_GROUP == _lane(t.shape) // C_STATE, t, 0.0)


def _block_diag_fold(m):
    rg = _row((C_WIDTH, C_STATE)) // C_GROUP
    acc = jnp.zeros((C_WIDTH, C_STATE), F32)
    for g in range(C_GROUPS):
        acc = acc + jnp.where(rg == g, m[:, g * C_STATE:(g + 1) * C_STATE], 0.0)
    return acc


def _ssm_prep(are, aim, ldt, are_x, aim_x, ldt_x, btr, bti, cre, cim):
    def body(are_r, aim_r, ldt_r, arex_r, aimx_r, ldtx_r, btr_r, bti_r, cre_r, cim_r, bbd_ref, cbd_ref, pwr_ref, pwi_ref):
        lr, li = _lam_bar(are_r[...], aim_r[...], ldt_r[...])
        cr, ci = lr, li
        for r in range(SCAN_SEG):
            pwr_ref[r:r + 1, :] = cr
            pwi_ref[r:r + 1, :] = ci
            cr, ci = cr * lr - ci * li, cr * li + ci * lr
        br, bi = _bbar_t(arex_r[...], aimx_r[...], ldtx_r[...], btr_r[...], bti_r[...])
        bbd_ref[...] = jnp.concatenate([_block_diag(br), _block_diag(bi)], axis=1).astype(MXU)
        cbd_ref[...] = jnp.concatenate([_block_diag(cre_r[...]), -_block_diag(cim_r[...])], axis=1).astype(MXU)

    return pl.pallas_call(
        body, name="ssm_prep",
        out_shape=[_sds((C_WIDTH, 2 * N_STATE), MXU), _sds((C_WIDTH, 2 * N_STATE), MXU), _sds((SCAN_SEG, N_STATE)), _sds((SCAN_SEG, N_STATE))],
        compiler_params=pltpu.CompilerParams(vmem_limit_bytes=VMEM_LIMIT))(are, aim, ldt, are_x, aim_x, ldt_x, btr, bti, cre, cim)


def _ssm_param_bwd(are, aim, ldt, are_x, aim_x, ldt_x, btr, bti, dlr, dli, dbbd, dcr, dci):
    def body(are_r, aim_r, ldt_r, arex_r, aimx_r, ldtx_r, btr_r, bti_r, dlr_r, dli_r, dbbd_r, dcr_r, dci_r,
             dare_ref, daim_ref, dldt_ref, dbtr_ref, dbti_ref, dcre_ref, dcim_ref):
        _, vjp_l = jax.vjp(_lam_bar, are_r[...], aim_r[...], ldt_r[...])
        da1, di1, dl1 = vjp_l((dlr_r[...], dli_r[...]))
        dbr = _block_diag_fold(dbbd_r[:, 0:N_STATE])
        dbi = _block_diag_fold(dbbd_r[:, N_STATE:2 * N_STATE])
        _, vjp_b = jax.vjp(_bbar_t, arex_r[...], aimx_r[...], ldtx_r[...], btr_r[...], bti_r[...])
        da2, di2, dl2, dbtr, dbti = vjp_b((dbr, dbi))

        def gsum(x):
            return x.reshape(C_GROUPS, C_GROUP, C_STATE).sum(axis=1)

        dare_ref[...] = da1 + gsum(da2)
        daim_ref[...] = di1 + gsum(di2)
        dldt_ref[...] = jnp.broadcast_to(jnp.sum(dl1 + gsum(dl2), axis=-1, keepdims=True), (C_GROUPS, LANES))
        dbtr_ref[...] = dbtr
        dbti_ref[...] = dbti
        dcre_ref[...] = _block_diag_fold(dcr_r[...])
        dcim_ref[...] = -_block_diag_fold(dci_r[...])

    g = _sds((C_GROUPS, C_STATE))
    x = _sds((C_WIDTH, C_STATE))
    return pl.pallas_call(
        body, name="ssm_param_bwd", out_shape=[g, g, _sds((C_GROUPS, LANES)), x, x, x, x],
        compiler_params=pltpu.CompilerParams(vmem_limit_bytes=VMEM_LIMIT))(are, aim, ldt, are_x, aim_x, ldt_x, btr, bti, dlr, dli, dbbd, dcr, dci)


SCAN_TILE = 256
SCAN_SEG = 8


def _scan_tables(pwr_ref, pwi_ref, conj, reverse):
    row = _row((SCAN_SEG, N_STATE))
    shifts = []
    for k in (1, 2, 4):
        keep = (row < SCAN_SEG - k) if reverse else (row >= k)
        ar = jnp.broadcast_to(pwr_ref[k - 1:k, :], (SCAN_SEG, N_STATE))
        ai = jnp.broadcast_to(pwi_ref[k - 1:k, :], (SCAN_SEG, N_STATE)) * conj
        shifts.append((SCAN_SEG - k if reverse else k, jnp.where(keep, ar, 0.0), jnp.where(keep, ai, 0.0)))
    if reverse:
        pr = jnp.concatenate([pwr_ref[SCAN_SEG - 1 - r:SCAN_SEG - r, :] for r in range(SCAN_SEG)], axis=0)
        pi = jnp.concatenate([pwi_ref[SCAN_SEG - 1 - r:SCAN_SEG - r, :] for r in range(SCAN_SEG)], axis=0) * conj
    else:
        pr, pi = pwr_ref[...], pwi_ref[...] * conj
    return shifts, (pr, pi)


def _tile_scan(xr_ref, xi_ref, pwr_ref, pwi_ref, sr, si, conj, reverse):
    shifts, (pr, pi) = _scan_tables(pwr_ref, pwi_ref, conj, reverse)
    groups = xr_ref.shape[0] // SCAN_SEG
    out_row = 0 if reverse else SCAN_SEG - 1

    def step(k, c):
        cr, ci = c
        g = groups - 1 - k if reverse else k
        rows = pl.ds(pl.multiple_of(g * SCAN_SEG, SCAN_SEG), SCAN_SEG)
        xr, xi = xr_ref[rows, :], xi_ref[rows, :]
        for amount, ar, ai in shifts:
            qr, qi = pltpu.roll(xr, amount, 0), pltpu.roll(xi, amount, 0)
            xr, xi = xr + ar * qr - ai * qi, xi + ar * qi + ai * qr
        xr, xi = xr + pr * cr - pi * ci, xi + pr * ci + pi * cr
        xr_ref[rows, :] = xr
        xi_ref[rows, :] = xi
        return xr[out_row:out_row + 1], xi[out_row:out_row + 1]

    cr, ci = lax.fori_loop(0, groups, step, (sr[...], si[...]), unroll=2)
    sr[...] = cr
    si[...] = ci


def _ssm_fwd(z, bbd, pwr, pwi, cbd, dsk, w1, w2):
    T = z.shape[0]
    tt = min(T, SCAN_TILE)

    def body(u_ref, bbd_ref, pwr_ref, pwi_ref, cbd_ref, d_ref, w1_ref, w2_ref, yc_ref, xr_ref, xi_ref, sr, si):
        @pl.when(pl.program_id(0) == 0)
        def _():
            sr[...] = jnp.zeros_like(sr)
            si[...] = jnp.zeros_like(si)

        u = u_ref[...]
        bu = _mm(u, bbd_ref[...])
        xr_ref[...] = bu[:, 0:N_STATE]
        xi_ref[...] = bu[:, N_STATE:2 * N_STATE]
        _tile_scan(xr_ref, xi_ref, pwr_ref, pwi_ref, sr, si, 1.0, False)
        x = jnp.concatenate([xr_ref[...], xi_ref[...]], axis=1)
        y2 = _gelu(_mm_nt(x, cbd_ref[...]) + d_ref[...] * u)
        yc_ref[...] = _mm(y2, w1_ref[...]) * _sigmoid(_mm(y2, w2_ref[...]))

    big = pl.BlockSpec((C_WIDTH, 2 * N_STATE), lambda i: (0, 0))
    tab = pl.BlockSpec((SCAN_SEG, N_STATE), lambda i: (0, 0))
    wsp = pl.BlockSpec((C_WIDTH, C_WIDTH), lambda i: (0, 0))
    xs = pl.BlockSpec((tt, N_STATE), lambda i: (i, 0))
    return pl.pallas_call(
        body, name="ssm_fwd", grid=(T // tt,),
        in_specs=[pl.BlockSpec((tt, C_WIDTH), lambda i: (i, 5)), big, tab, tab, big, pl.BlockSpec((1, C_WIDTH), lambda i: (0, 0)), wsp, wsp],
        out_specs=[pl.BlockSpec((tt, C_WIDTH), lambda i: (i, 0)), xs, xs],
        out_shape=[_sds((T, C_WIDTH)), _sds((T, N_STATE)), _sds((T, N_STATE))],
        scratch_shapes=[pltpu.VMEM((1, N_STATE), F32)] * 2,
        compiler_params=_cp("arbitrary"))(z, bbd, pwr, pwi, cbd, dsk, w1, w2)


def _ssm_bwd(dyc, z, xr, xi, bbd, pwr, pwi, cbd, dsk, w1, w2):
    T = z.shape[0]
    tt = min(T, SCAN_TILE)
    nt = T // tt

    def tile_of(i):
        return nt - 1 - i

    def body(dyc_ref, u_ref, xr_ref, xi_ref, xpr_ref, xpi_ref, bbd_ref, pwr_ref, pwi_ref, cbd_ref, d_ref, w1_ref, w2_ref,
             du_ref, y2_ref, da1_ref, da2_ref, dy_ref, arai_ref, dlr_ref, dli_ref, dd_ref, gr, gi, sr, si):
        i = pl.program_id(0)
        ti = nt - 1 - i

        @pl.when(i == 0)
        def _():
            sr[...] = jnp.zeros_like(sr)
            si[...] = jnp.zeros_like(si)
            dlr_ref[...] = jnp.zeros_like(dlr_ref)
            dli_ref[...] = jnp.zeros_like(dli_ref)
            dd_ref[...] = jnp.zeros_like(dd_ref)

        u = u_ref[...]
        xr_t, xi_t = xr_ref[...], xi_ref[...]
        y = _mm_nt(jnp.concatenate([xr_t, xi_t], axis=1), cbd_ref[...]) + d_ref[...] * u
        y2 = _gelu(y)
        a1 = _mm(y2, w1_ref[...])
        sg = _sigmoid(_mm(y2, w2_ref[...]))
        dyc_t = dyc_ref[...]
        da1 = dyc_t * sg
        da2 = dyc_t * a1 * sg * (1.0 - sg)
        dy = (_mm_nt(da1, w1_ref[...]) + _mm_nt(da2, w2_ref[...])) * _gelu_grad(y)
        gx = _mm(dy, cbd_ref[...])
        gr[...] = gx[:, 0:N_STATE]
        gi[...] = gx[:, N_STATE:2 * N_STATE]
        _tile_scan(gr, gi, pwr_ref, pwi_ref, sr, si, -1.0, True)
        ar, ai = gr[...], gi[...]
        first_row = _row(ar.shape) == 0
        live = jnp.where(ti > 0, 1.0, 0.0)
        xsr = jnp.where(first_row, xpr_ref[7:8, :] * live, pltpu.roll(xr_t, 1, 0))
        xsi = jnp.where(first_row, xpi_ref[7:8, :] * live, pltpu.roll(xi_t, 1, 0))
        dlr_ref[...] += jnp.sum(ar * xsr + ai * xsi, axis=0, keepdims=True)
        dli_ref[...] += jnp.sum(ai * xsr - ar * xsi, axis=0, keepdims=True)
        dd_ref[...] += jnp.sum(dy * u, axis=0, keepdims=True)
        arai = jnp.concatenate([ar, ai], axis=1)
        du_ref[...] = _mm_nt(arai, bbd_ref[...]) + d_ref[...] * dy
        y2_ref[...] = y2.astype(MXU)
        da1_ref[...] = da1.astype(MXU)
        da2_ref[...] = da2.astype(MXU)
        dy_ref[...] = dy.astype(MXU)
        arai_ref[...] = arai.astype(MXU)

    def prev(i):
        return jnp.maximum(tile_of(i) * (tt // 8) - 1, 0)

    big = pl.BlockSpec((C_WIDTH, 2 * N_STATE), lambda i: (0, 0))
    tab = pl.BlockSpec((SCAN_SEG, N_STATE), lambda i: (0, 0))
    srow = pl.BlockSpec((1, N_STATE), lambda i: (0, 0))
    wsp = pl.BlockSpec((C_WIDTH, C_WIDTH), lambda i: (0, 0))
    xs = pl.BlockSpec((tt, N_STATE), lambda i: (tile_of(i), 0))
    xp = pl.BlockSpec((8, N_STATE), lambda i: (prev(i), 0))
    cw = pl.BlockSpec((tt, C_WIDTH), lambda i: (tile_of(i), 0))
    drow = pl.BlockSpec((1, C_WIDTH), lambda i: (0, 0))
    return pl.pallas_call(
        body, name="ssm_bwd", grid=(nt,),
        in_specs=[cw, pl.BlockSpec((tt, C_WIDTH), lambda i: (tile_of(i), 5)), xs, xs, xp, xp, big, tab, tab, big, drow, wsp, wsp],
        out_specs=[cw, cw, cw, cw, cw, pl.BlockSpec((tt, 2 * N_STATE), lambda i: (tile_of(i), 0)), srow, srow, drow],
        out_shape=[_sds((T, C_WIDTH))] + [_sds((T, C_WIDTH), MXU)] * 4 + [_sds((T, 2 * N_STATE), MXU), _sds((1, N_STATE)), _sds((1, N_STATE)), _sds((1, C_WIDTH))],
        scratch_shapes=[pltpu.VMEM((tt, N_STATE), F32)] * 2 + [pltpu.VMEM((1, N_STATE), F32)] * 2,
        compiler_params=_cp("arbitrary"))(dyc, z, xr, xi, xr, xi, bbd, pwr, pwi, cbd, dsk, w1, w2)


_GROUPS = ((0, A_WIDTH), (A_WIDTH, A_WIDTH + B_WIDTH), (A_WIDTH + B_WIDTH, D_MODEL))


def _merge_fwd(h, ya, yb, yc, g, w):
    T = h.shape[0]
    tm = min(T, 512)

    def body(h_ref, a_ref, b_ref, c_ref, g_ref, w_ref, o_ref):
        yn = jnp.concatenate([y * _rms_stat(y) for y in (a_ref[...], b_ref[...], c_ref[...])], axis=1) * g_ref[...]
        o_ref[...] = h_ref[...] + _mm(yn, w_ref[...])

    def rows(w_):
        return pl.BlockSpec((tm, w_), lambda i: (i, 0))

    return pl.pallas_call(
        body, name="merge_fwd", grid=(T // tm,),
        in_specs=[rows(D_MODEL), rows(A_WIDTH), rows(B_WIDTH), rows(C_WIDTH), pl.BlockSpec((1, D_MODEL), lambda i: (0, 0)),
                  pl.BlockSpec((D_MODEL, D_MODEL), lambda i: (0, 0))],
        out_specs=rows(D_MODEL), out_shape=_sds((T, D_MODEL)), compiler_params=_cp("parallel"))(h, ya, yb, yc, g, w)


def _merge_bwd(gres, ya, yb, yc, g, w):
    T = gres.shape[0]
    tm = min(T, 512)

    def body(gr_ref, a_ref, b_ref, c_ref, g_ref, w_ref, da_ref, db_ref, dc_ref, yn_ref, dg_ref):
        @pl.when(pl.program_id(0) == 0)
        def _():
            dg_ref[...] = jnp.zeros_like(dg_ref)

        dyn = _mm_nt(gr_ref[...], w_ref[...])
        yns, dgs = [], []
        for (c0, c1), y_ref, d_ref in zip(_GROUPS, (a_ref, b_ref, c_ref), (da_ref, db_ref, dc_ref)):
            y = y_ref[...]
            r = _rms_stat(y)
            gg = g_ref[:, c0:c1]
            dx, dg = _rms_bwd(y, r, gg, dyn[:, c0:c1])
            d_ref[...] = dx
            dgs.append(dg)
            yns.append(y * r * gg)
        dg_ref[...] += jnp.concatenate(dgs, axis=1)
        yn_ref[...] = jnp.concatenate(yns, axis=1).astype(MXU)

    def rows(w_):
        return pl.BlockSpec((tm, w_), lambda i: (i, 0))

    row = pl.BlockSpec((1, D_MODEL), lambda i: (0, 0))
    return pl.pallas_call(
        body, name="merge_bwd", grid=(T // tm,),
        in_specs=[rows(D_MODEL), rows(A_WIDTH), rows(B_WIDTH), rows(C_WIDTH), row, pl.BlockSpec((D_MODEL, D_MODEL), lambda i: (0, 0))],
        out_specs=[rows(A_WIDTH), rows(B_WIDTH), rows(C_WIDTH), rows(D_MODEL), row],
        out_shape=[_sds((T, A_WIDTH)), _sds((T, B_WIDTH)), _sds((T, C_WIDTH)), _sds((T, D_MODEL), MXU), _sds((1, D_MODEL))],
        compiler_params=_cp("arbitrary"))(gres, ya, yb, yc, g, w)


FF_BLOCK = D_FF // N_DEV


def _load_weights(w1_hbm, w2_hbm, w1, w2, sem):
    @pl.when(pl.program_id(0) == 0)
    def _():
        copies = [pltpu.make_async_copy(w1_hbm.at[j], w1.at[:, pl.ds(j * FF_BLOCK, FF_BLOCK)], sem.at[j]) for j in range(N_DEV)]
        copies.append(pltpu.make_async_copy(w2_hbm, w2, sem.at[N_DEV]))
        for cp in copies:
            cp.start()
        for cp in copies:
            cp.wait()


def _mlp_weight_scratch():
    return [pltpu.VMEM((D_MODEL, D_FF), MXU), pltpu.VMEM((D_FF, D_MODEL), MXU), pltpu.SemaphoreType.DMA((N_DEV + 1,))]


def _mlp_fwd(h, g, w1, w2):
    T = h.shape[0]
    tm = min(T, 512)

    def body(h_ref, g_ref, w1_hbm, w2_hbm, o_ref, r_ref, w1_v, w2_v, sem):
        _load_weights(w1_hbm, w2_hbm, w1_v, w2_v, sem)
        x = h_ref[...]
        a = jnp.maximum(_mm(x * _rms_stat(x) * g_ref[...], w1_v[...]), 0.0)
        r = (a * a).astype(MXU)
        r_ref[...] = r
        o_ref[...] = x + _mm(r, w2_v[...])

    rows = pl.BlockSpec((tm, D_MODEL), lambda i: (i, 0))
    hbm = pl.BlockSpec(memory_space=pl.ANY)
    return pl.pallas_call(
        body, name="mlp_fwd", grid=(T // tm,),
        in_specs=[rows, pl.BlockSpec((1, D_MODEL), lambda i: (0, 0)), hbm, hbm],
        out_specs=[rows, pl.BlockSpec((tm, D_FF), lambda i: (i, 0))],
        out_shape=[_sds((T, D_MODEL)), _sds((T, D_FF), MXU)],
        scratch_shapes=_mlp_weight_scratch(), compiler_params=_cp("arbitrary"))(h, g, w1, w2)


def _mlp_bwd(gres, h, g, r, w1, w2):
    T = h.shape[0]
    tm = min(T, 256)

    def body(gr_ref, h_ref, g_ref, r_ref, w1_hbm, w2_hbm, dh_ref, hn_ref, da_ref, dg_ref, w1_v, w2_v, sem):
        _load_weights(w1_hbm, w2_hbm, w1_v, w2_v, sem)

        @pl.when(pl.program_id(0) == 0)
        def _():
            dg_ref[...] = jnp.zeros_like(dg_ref)

        gr = gr_ref[...]
        da = (_mm_nt(gr, w2_v[...]) * (2.0 * jnp.sqrt(r_ref[...].astype(F32)))).astype(MXU)
        da_ref[...] = da
        x = h_ref[...]
        rs = _rms_stat(x)
        gg = g_ref[...]
        dx, dg = _rms_bwd(x, rs, gg, _mm_nt(da, w1_v[...]))
        dh_ref[...] = gr + dx
        dg_ref[...] += dg
        hn_ref[...] = (x * rs * gg).astype(MXU)

    rows = pl.BlockSpec((tm, D_MODEL), lambda i: (i, 0))
    row = pl.BlockSpec((1, D_MODEL), lambda i: (0, 0))
    wide = pl.BlockSpec((tm, D_FF), lambda i: (i, 0))
    hbm = pl.BlockSpec(memory_space=pl.ANY)
    return pl.pallas_call(
        body, name="mlp_bwd", grid=(T // tm,),
        in_specs=[rows, rows, row, wide, hbm, hbm], out_specs=[rows, rows, wide, row],
        out_shape=[_sds((T, D_MODEL)), _sds((T, D_MODEL), MXU), _sds((T, D_FF), MXU), _sds((1, D_MODEL))],
        scratch_shapes=_mlp_weight_scratch(), compiler_params=_cp("arbitrary"))(gres, h, g, r, w1, w2)


def _ple_fwd(h, p, l, g, wg, wp):
    T = h.shape[0]
    tm = min(T, 512)

    def body(h_ref, p_ref, g_ref, wg_ref, wp_ref, o_ref):
        x = h_ref[...]
        gate = _sigmoid(_mm(x * _rms_stat(x) * g_ref[...], wg_ref[...]))
        o_ref[...] = x + gate * _mm(p_ref[...], wp_ref[...])

    rows = pl.BlockSpec((tm, D_MODEL), lambda i: (i, 0))
    return pl.pallas_call(
        body, name="ple_fwd", grid=(T // tm,),
        in_specs=[rows, pl.BlockSpec((None, tm, PLE_DIM), lambda i: (l, i, 0)), pl.BlockSpec((1, D_MODEL), lambda i: (0, 0)),
                  pl.BlockSpec((D_MODEL, D_MODEL), lambda i: (0, 0)), pl.BlockSpec((PLE_DIM, D_MODEL), lambda i: (0, 0))],
        out_specs=rows, out_shape=_sds((T, D_MODEL)), compiler_params=_cp("parallel"))(h, p, g, wg, wp)


def _ple_bwd(gres, h, p, l, g, wg, wp):
    T = h.shape[0]
    tm = min(T, 512)

    def body(gr_ref, h_ref, p_ref, g_ref, wg_ref, wp_ref, dh_ref, hn_ref, dgp_ref, de_ref, dg_ref):
        @pl.when(pl.program_id(0) == 0)
        def _():
            dg_ref[...] = jnp.zeros_like(dg_ref)

        x = h_ref[...]
        r = _rms_stat(x)
        gg = g_ref[...]
        hn = x * r * gg
        gate = _sigmoid(_mm(hn, wg_ref[...]))
        e = _mm(p_ref[...], wp_ref[...])
        gr = gr_ref[...]
        dgp = gr * e * gate * (1.0 - gate)
        dx, dg = _rms_bwd(x, r, gg, _mm_nt(dgp, wg_ref[...]))
        dh_ref[...] = gr + dx
        dg_ref[...] += dg
        hn_ref[...] = hn.astype(MXU)
        dgp_ref[...] = dgp.astype(MXU)
        de_ref[...] = (gr * gate).astype(MXU)

    rows = pl.BlockSpec((tm, D_MODEL), lambda i: (i, 0))
    row = pl.BlockSpec((1, D_MODEL), lambda i: (0, 0))
    return pl.pallas_call(
        body, name="ple_bwd", grid=(T // tm,),
        in_specs=[rows, rows, pl.BlockSpec((None, tm, PLE_DIM), lambda i: (l, i, 0)), row,
                  pl.BlockSpec((D_MODEL, D_MODEL), lambda i: (0, 0)), pl.BlockSpec((PLE_DIM, D_MODEL), lambda i: (0, 0))],
        out_specs=[rows, rows, rows, rows, row],
        out_shape=[_sds((T, D_MODEL))] + [_sds((T, D_MODEL), MXU)] * 3 + [_sds((1, D_MODEL))],
        compiler_params=_cp("arbitrary"))(gres, h, p, g, wg, wp)


def _loss_head(h, target):
    T = h.shape[0]
    tm = min(T, 1024)

    def body(h_ref, t_ref, dh_ref, l_ref):
        @pl.when(pl.program_id(0) == 0)
        def _():
            l_ref[...] = jnp.zeros_like(l_ref)

        e = h_ref[...] - t_ref[...]
        dh_ref[...] = e * (1.0 / D_MODEL)
        l_ref[...] += jnp.zeros_like(l_ref) + 0.5 * jnp.sum(jnp.mean(e * e, axis=-1, keepdims=True))

    rows = pl.BlockSpec((tm, D_MODEL), lambda i: (i, 0))
    return pl.pallas_call(
        body, name="loss_head", grid=(T // tm,), in_specs=[rows, rows],
        out_specs=[rows, pl.BlockSpec((8, LANES), lambda i: (0, 0))], out_shape=[_sds((T, D_MODEL)), _sds((8, LANES))],
        compiler_params=_cp("arbitrary"))(h, target)


TN_ROWS = 2048

def _tn(a, b, m, n, *, bm, bn, a_off=0, b_off=0, a_lead=None, b_blocked=False, n_major=False, split=None, dtype=F32, name="tn"):
    T = a.shape[-2]
    tk = min(T, TN_ROWS)
    nk = T // tk
    if b_blocked:
        b_spec = pl.BlockSpec((None, tk, bn), lambda i, j, k: (j, k, 0))
    else:
        b_spec = pl.BlockSpec((tk, bn), lambda i, j, k: (k, b_off + j))
    if a_lead is None:
        a_spec = pl.BlockSpec((tk, bm), lambda i, j, k: (k, a_off + i))
    else:
        a_spec = pl.BlockSpec((None, tk, bm), lambda i, j, k: (a_lead, k, a_off + i))
    assert m % bm == 0 and n % bn == 0 and (not n_major or bm == m) and (split is None or (bn == n and n % split == 0))

    def body(a_ref, b_ref, o_ref, acc):
        k = pl.program_id(2)

        @pl.when(k == 0)
        def _():
            acc[...] = jnp.zeros_like(acc)

        acc[...] += _mm_tn(a_ref[...], b_ref[...])

        @pl.when(k == nk - 1)
        def _():
            if split is None:
                o_ref[...] = acc[...].astype(dtype)
            else:
                for d in range(n // split):
                    o_ref[d] = acc[:, d * split:(d + 1) * split].astype(dtype)

    if split is not None:
        out_spec = pl.BlockSpec((n // split, bm, split), lambda i, j, k: (0, i, 0))
        out_shape = _sds((n // split, m, split), dtype)
    elif n_major:
        out_spec = pl.BlockSpec((None, bm, bn), lambda i, j, k: (j, 0, 0))
        out_shape = _sds((n // bn, m, bn), dtype)
    else:
        out_spec = pl.BlockSpec((bm, bn), lambda i, j, k: (i, j))
        out_shape = _sds((m, n), dtype)
    return pl.pallas_call(
        body, name=name, grid=(m // bm, n // bn, nk),
        in_specs=[a_spec, b_spec],
        out_specs=out_spec, out_shape=out_shape, scratch_shapes=[pltpu.VMEM((bm, bn), F32)],
        compiler_params=_cp("parallel", "parallel", "arbitrary"))(a, b)


def _row_tile(R, C):
    for cand in (512, 256, 128, 64, 32, 16, 8):
        if R % cand == 0 and cand * C * 4 <= 2 ** 20:
            return cand
    return R


def _sum_slots(land):
    S, R, C = land.shape
    tr = _row_tile(R, C)

    def body(l_ref, o_ref):
        acc = l_ref[0].astype(F32)
        for s in range(1, S):
            acc = acc + l_ref[s].astype(F32)
        o_ref[...] = acc

    return pl.pallas_call(
        body, name="sum_slots", grid=(R // tr,), in_specs=[pl.BlockSpec((S, tr, C), lambda i: (0, i, 0))],
        out_specs=pl.BlockSpec((tr, C), lambda i: (i, 0)), out_shape=_sds((R, C)), compiler_params=_cp("parallel"))(land)


def _sum_adamw(land, w, m, v):
    R, C = w.shape
    S = land.shape[0]
    tr = _row_tile(R, C)

    def body(l_ref, w_ref, m_ref, v_ref, g_ref, d_ref, nm_ref, nv_ref):
        gg = l_ref[0].astype(F32)
        for s in range(1, S):
            gg = gg + l_ref[s].astype(F32)
        g_ref[...] = gg
        nm = ADAM_B1 * m_ref[...] + (1.0 - ADAM_B1) * gg
        nv = ADAM_B2 * v_ref[...] + (1.0 - ADAM_B2) * (gg * gg)
        m_hat = nm / (1.0 - ADAM_B1 ** ADAM_STEP)
        v_hat = nv / (1.0 - ADAM_B2 ** ADAM_STEP)
        d_ref[...] = -ADAM_LR * (m_hat / (jnp.sqrt(v_hat) + ADAM_EPS) + ADAM_WD * w_ref[...])
        nm_ref[...] = nm
        nv_ref[...] = nv

    blk = pl.BlockSpec((tr, C), lambda i: (i, 0))
    return pl.pallas_call(
        body, name="sum_adamw", grid=(R // tr,), in_specs=[pl.BlockSpec((S, tr, C), lambda i: (0, i, 0))] + [blk] * 3,
        out_specs=[blk] * 4, out_shape=[_sds((R, C))] * 4, compiler_params=_cp("parallel"))(land, w, m, v)


def _all_to_all(pairs, name):
    n = len(pairs)

    def body(*refs):
        srcs, lands = refs[:n], refs[2 * n:3 * n]
        send, recv, loc = refs[3 * n:]
        x, y, c = lax.axis_index("x"), lax.axis_index("y"), lax.axis_index("c")
        me = 4 * x + 2 * y + c
        own = [pltpu.make_async_copy(pairs[t][2](srcs[t], me, me), pairs[t][3](lands[t], me), loc.at[t]) for t in range(n)]
        for cp in own:
            cp.start()
        sends, recvs = [], []
        for k in range(1, N_DEV):
            px, py, pc = x ^ (k >> 2), y ^ ((k >> 1) & 1), c ^ (k & 1)
            peer = 4 * px + 2 * py + pc
            for t in range(n):
                src = pairs[t][2](srcs[t], me, peer)
                cp = pltpu.make_async_remote_copy(src_ref=src, dst_ref=pairs[t][3](lands[t], me), send_sem=send.at[t, k],
                                                  recv_sem=recv.at[t, k], device_id=(px, py, pc), device_id_type=MESH)
                cp.start()
                sends.append(cp)
                recvs.append(pltpu.make_async_remote_copy(src_ref=src, dst_ref=pairs[t][3](lands[t], peer), send_sem=send.at[t, k],
                                                          recv_sem=recv.at[t, k], device_id=(px, py, pc), device_id_type=MESH))
        for cp in recvs:
            cp.wait_recv()
        for cp in sends:
            cp.wait_send()
        for cp in own:
            cp.wait()

    anyspec = pl.BlockSpec(memory_space=pl.ANY)
    lands = [pr[1] for pr in pairs]
    return pl.pallas_call(
        body, name=name, in_specs=[anyspec] * (2 * n), out_specs=[anyspec] * n,
        out_shape=[_sds(a.shape, a.dtype) for a in lands], input_output_aliases={n + t: t for t in range(n)},
        scratch_shapes=[pltpu.SemaphoreType.DMA((n, N_DEV)), pltpu.SemaphoreType.DMA((n, N_DEV)), pltpu.SemaphoreType.DMA((n,))],
        compiler_params=pltpu.CompilerParams(has_side_effects=True))(*[pr[0] for pr in pairs], *lands)


def _gather_whole(src):
    land = lax.empty((N_DEV,) + src.shape, src.dtype)
    return _all_to_all([(src, land, lambda ref, me, peer: ref, lambda ref, sender: ref.at[sender])], "gather_small_grads")[0]


_HBM = pl.BlockSpec(memory_space=pltpu.HBM)
_SEM = pl.BlockSpec(memory_space=pltpu.SEMAPHORE)
_EFFECT = pltpu.SideEffectType.DATAFLOW_SIDE_EFFECTING


def _peers():
    x, y, c = lax.axis_index("x"), lax.axis_index("y"), lax.axis_index("c")
    out = []
    for k in range(1, N_DEV):
        px, py, pc = x ^ (k >> 2), y ^ ((k >> 1) & 1), c ^ (k & 1)
        out.append((k, (px, py, pc), 4 * px + 2 * py + pc))
    return 4 * x + 2 * y + c, out


def _route(mode, layer):
    if mode == "all":
        return (lambda ref, peer: ref), (lambda ref, sender: ref.at[sender])
    return (lambda ref, peer: ref.at[peer]), (lambda ref, sender: ref.at[sender, layer])


def _split_start(srcs, lands, modes, layer, after, name):
    n = len(srcs)
    routes = [_route(m, layer) for m in modes]

    def body(*refs):
        src, land = refs[:n], refs[n:2 * n]
        send, recv, token = refs[2 * n + 1], refs[2 * n + 2], refs[-1]
        me, peers = _peers()
        for k, dev, peer in peers:
            for t, (src_of, dst_of) in enumerate(routes):
                pltpu.make_async_remote_copy(src_ref=src_of(src[t], peer), dst_ref=dst_of(land[t], me), send_sem=send.at[t * N_DEV + k],
                                             recv_sem=recv.at[t * N_DEV + k], device_id=dev, device_id_type=MESH).start()
        token[...] = jnp.zeros_like(token)

    bufs = list(srcs) + list(lands)
    outs = pl.pallas_call(
        body, name=name,
        out_shape=(pltpu.SemaphoreType.DMA((n * N_DEV,)), pltpu.SemaphoreType.DMA((n * N_DEV,)),
                   *[pltpu.HBM(a.shape, a.dtype) for a in bufs], _sds((8, LANES))),
        in_specs=[_HBM] * (2 * n) + [pl.BlockSpec(memory_space=pl.ANY)],
        out_specs=(_SEM, _SEM, *[_HBM] * (2 * n), pl.BlockSpec(memory_space=pltpu.VMEM)),
        input_output_aliases={i: 2 + i for i in range(2 * n)},
        compiler_params=pltpu.CompilerParams(has_side_effects=_EFFECT),
    )(*[pltpu.with_memory_space_constraint(a, pltpu.HBM) for a in bufs], after)
    return outs[0], outs[1], list(outs[2:2 + n]), list(outs[2 + n:2 + 2 * n]), outs[-1]


def _split_wait(send, recv, srcs, lands, modes, layer, after, name):
    n = len(srcs)
    routes = [_route(m, layer) for m in modes]

    def body(*refs):
        src, land = refs[:n], refs[n:2 * n]
        send_r, recv_r = refs[2 * n], refs[2 * n + 1]
        _, peers = _peers()
        for k, dev, peer in peers:
            for t, (src_of, dst_of) in enumerate(routes):
                cp = pltpu.make_async_remote_copy(src_ref=src_of(src[t], peer), dst_ref=dst_of(land[t], peer), send_sem=send_r.at[t * N_DEV + k],
                                                  recv_sem=recv_r.at[t * N_DEV + k], device_id=dev, device_id_type=MESH)
                cp.wait_send()
                cp.wait_recv()

    bufs = list(srcs) + list(lands)
    outs = pl.pallas_call(
        body, name=name, out_shape=tuple(pltpu.HBM(a.shape, a.dtype) for a in bufs),
        in_specs=[_HBM] * (2 * n) + [_SEM, _SEM, pl.BlockSpec(memory_space=pl.ANY)], out_specs=[_HBM] * (2 * n),
        input_output_aliases={i: i for i in range(2 * n)},
        compiler_params=pltpu.CompilerParams(has_side_effects=_EFFECT),
    )(*bufs, send, recv, after)
    return list(outs[:n]), list(outs[n:])


SHARDED = ("w_in", "glu_w1", "glu_w2", "w_out", "w_ff1", "w_ff2", "w_ple_gate", "w_ple_proj")
SMALL = ("attn_norm_g", "gmlp_ln_g", "gmlp_ln_b", "gmlp_ws", "gmlp_bs", "q_norm_g", "k_norm_g", "sinks", "ssm_a_re", "ssm_a_im",
         "ssm_log_dt", "ssm_b_re", "ssm_b_im", "ssm_c_re", "ssm_c_im", "ssm_d", "mix_out_g", "mlp_norm_g", "ple_norm_g")
WEIGHTS = ("attn_norm_g", "w_in", "gmlp_ln_g", "gmlp_ln_b", "gmlp_ws", "gmlp_bs", "q_norm_g", "k_norm_g", "sinks", "ssm_a_re", "ssm_a_im",
           "ssm_log_dt", "ssm_b_re", "ssm_b_im", "ssm_c_re", "ssm_c_im", "ssm_d", "glu_w1", "glu_w2", "mix_out_g", "w_out", "mlp_norm_g",
           "w_ff1", "w_ff2", "ple_norm_g", "w_ple_gate", "w_ple_proj")
FLAT_COLS = 1024


PACK_TILE_ROWS = 8
PACK_ROWS_MULTIPLE = PACK_TILE_ROWS * N_DEV


def _packed_rows(shape):
    return -(-math.prod(shape) // (PACK_TILE_ROWS * FLAT_COLS)) * PACK_TILE_ROWS


def _pack(arrs, dtype):
    blocks = []
    for a in arrs:
        flat = a.astype(dtype).reshape(-1)
        pad = _packed_rows(a.shape) * FLAT_COLS - flat.shape[0]
        if pad:
            flat = jnp.concatenate([flat, jnp.zeros((pad,), dtype)])
        blocks.append(flat.reshape(-1, FLAT_COLS))
    rows = sum(b.shape[0] for b in blocks)
    if rows % PACK_ROWS_MULTIPLE:
        blocks.append(jnp.zeros((PACK_ROWS_MULTIPLE - rows % PACK_ROWS_MULTIPLE, FLAT_COLS), dtype))
    return jnp.concatenate(blocks, axis=0)


def _unpack(flat, shapes):
    out, r = [], 0
    for s in shapes:
        nr = _packed_rows(s)
        out.append(flat[r:r + nr].reshape(-1)[:math.prod(s)].reshape(s))
        r += nr
    return out


def _col_major(w):
    rows, cols = w.shape
    return w.reshape(rows, N_DEV, cols // N_DEV).transpose(1, 0, 2)


def _from_col_major(s):
    n, rows, cs = s.shape
    return s.transpose(1, 0, 2).reshape(rows, n * cs)


EARLY = ("w_in", "glu_w1", "glu_w2")
LATE = ("w_out", "w_ff1", "w_ff2", "w_ple_gate", "w_ple_proj")
GRADS_MID = ("w_ff1", "w_ff2", "w_ple_gate", "w_ple_proj")
GRADS_END = ("w_in", "glu_w1", "glu_w2", "w_out")


def _layer_fwd(h, p, l, cs, sn, W, late_weights, S, sp):
    z = _inproj_fwd(h, S["attn_norm_g"], W["w_in"])
    ya = _gmlp_fwd(z, S["lng"], S["lnb"], S["gmlp_ws"], S["bsx"])
    yb = _attn_fwd(z, cs, sn, S["qg"], S["kg"], S["sinks"])
    yc, xr, xi = _ssm_fwd(z, sp["bbd"], sp["pwr"], sp["pwi"], sp["cbd"], S["ssm_d"], W["glu_w1"], W["glu_w2"])
    late, token = late_weights(ya[0:8, 0:LANES] + yb[0:8, 0:LANES] + yc[0:8, 0:LANES])
    W = {**W, **late}
    h1 = _merge_fwd(h, ya, yb, yc, _behind(S["mix_out_g"], token), W["w_out"])
    h2, r = _mlp_fwd(h1, S["mlp_norm_g"], W["w_ff1"], W["w_ff2"])
    h3 = _ple_fwd(h2, p, l, S["ple_norm_g"], W["w_ple_gate"], W["w_ple_proj"])
    return h3, dict(h=h, z=z, ya=ya, yb=yb, yc=yc, xr=xr, xi=xi, h1=h1, r=r, h2=h2), W


def _layer_bwd(g3, p, l, cs, sn, W, S, sp, A, raw, mid_bwd):
    G = {}
    g2, hn3, dgp, de, G["ple_norm_g"] = _ple_bwd(g3, A["h2"], p, l, S["ple_norm_g"], W["w_ple_gate"], W["w_ple_proj"])
    G["w_ple_gate"] = _tn(hn3, dgp, D_MODEL, D_MODEL, bm=1024, bn=1024, dtype=WIRE, name="tn_gate").reshape(N_DEV, -1, D_MODEL)
    G["w_ple_proj"] = _tn(p, de, PLE_DIM, D_MODEL, bm=PLE_DIM, bn=D_MODEL, a_lead=l, split=D_MODEL // N_DEV, dtype=WIRE, name="tn_proj")
    g1, hn, da, G["mlp_norm_g"] = _mlp_bwd(g2, A["h1"], S["mlp_norm_g"], A["r"], W["w_ff1"], W["w_ff2"])
    G["w_ff1"] = _tn(hn, da, D_MODEL, D_FF, bm=D_MODEL, bn=FF_BLOCK, n_major=True, dtype=WIRE, name="tn_ff1")
    G["w_ff2"] = _tn(A["r"], g2, D_FF, D_MODEL, bm=1024, bn=1024, dtype=WIRE, name="tn_ff2").reshape(N_DEV, -1, D_MODEL)
    token = mid_bwd(g1, G)
    dya, dyb, dyc, yn, G["mix_out_g"] = _merge_bwd(g1, A["ya"], A["yb"], A["yc"], _behind(S["mix_out_g"], token), W["w_out"])
    G["w_out"] = _tn(yn, g1, D_MODEL, D_MODEL, bm=1024, bn=1024, dtype=WIRE, name="tn_out").reshape(N_DEV, -1, D_MODEL)
    dzc, y2, da1, da2, dy, arai, dlr, dli, dd = _ssm_bwd(dyc, A["z"], A["xr"], A["xi"], sp["bbd"], sp["pwr"], sp["pwi"], sp["cbd"],
                                                        S["ssm_d"], W["glu_w1"], W["glu_w2"])
    G["glu_w1"] = _tn(y2, da1, C_WIDTH, C_WIDTH, bm=256, bn=256, dtype=WIRE, name="tn_glu1").reshape(N_DEV, -1, C_WIDTH)
    G["glu_w2"] = _tn(y2, da2, C_WIDTH, C_WIDTH, bm=256, bn=256, dtype=WIRE, name="tn_glu2").reshape(N_DEV, -1, C_WIDTH)
    dbbd = _tn(A["z"], arai, C_WIDTH, 2 * N_STATE, bm=256, bn=1024, a_off=5, name="tn_bbd")
    dcr = _tn(dy, A["xr"], C_WIDTH, N_STATE, bm=256, bn=1024, name="tn_cre")
    dci = _tn(dy, A["xi"], C_WIDTH, N_STATE, bm=256, bn=1024, name="tn_cim")
    dare, daim, dldt, dbtr, dbti, dcre, dcim = _ssm_param_bwd(
        raw["are"], raw["aim"], raw["ldt"], raw["are_x"], raw["aim_x"], raw["ldt_x"], raw["btr"], raw["bti"],
        dlr.reshape(C_GROUPS, C_STATE), dli.reshape(C_GROUPS, C_STATE), dbbd, dcr, dci)
    G["ssm_a_re"], G["ssm_a_im"], G["ssm_log_dt"] = dare, daim, dldt[:, 0]
    G["ssm_b_re"] = dbtr.reshape(C_GROUPS, C_GROUP, C_STATE).transpose(0, 2, 1)
    G["ssm_b_im"] = dbti.reshape(C_GROUPS, C_GROUP, C_STATE).transpose(0, 2, 1)
    G["ssm_c_re"] = dcre.reshape(C_GROUPS, C_GROUP, C_STATE)
    G["ssm_c_im"] = dcim.reshape(C_GROUPS, C_GROUP, C_STATE)
    G["ssm_d"] = dd.reshape(C_GROUPS, C_GROUP)
    dzq, dzk, dzv, dqg, dkg, dsk = _attn_bwd(A["z"], cs, sn, S["qg"], S["kg"], S["sinks"], A["yb"], dyb)
    G["q_norm_g"] = dqg[0, :HEAD_DIM] + dqg[0, HEAD_DIM:]
    G["k_norm_g"] = dkg[0, :HEAD_DIM] + dkg[0, HEAD_DIM:]
    G["sinks"] = dsk[:, 0]
    dza, dws, dbs, dlng, dlnb = _gmlp_bwd(A["z"], dya, S["lng"], S["lnb"], S["gmlp_ws"], S["bsx"])
    G["gmlp_ws"] = dws
    G["gmlp_bs"] = dbs[:, :, 0]
    G["gmlp_ln_g"] = dlng.reshape(A_HEADS, 2, HEAD_DIM)[:, 1]
    G["gmlp_ln_b"] = dlnb.reshape(A_HEADS, 2, HEAD_DIM)[:, 1]
    g0, xn, dz, G["attn_norm_g"] = _inproj_bwd(g1, A["h"], S["attn_norm_g"], W["w_in"], dza, dzq, dzk, dzv, dzc)
    G["w_in"] = _tn(xn, dz, D_MODEL, IN_COLS, bm=D_MODEL, bn=IN_COLS, split=IN_COLS // N_DEV, dtype=WIRE, name="tn_in")
    return g0, G


def _small_layouts(P, l):
    def row(a):
        return a.reshape(1, -1)

    zeros = jnp.zeros((A_HEADS, HEAD_DIM), F32)
    S = dict(
        attn_norm_g=row(P["attn_norm_g"][l]), mix_out_g=row(P["mix_out_g"][l]), mlp_norm_g=row(P["mlp_norm_g"][l]),
        ple_norm_g=row(P["ple_norm_g"][l]),
        lng=jnp.stack([zeros, P["gmlp_ln_g"][l]], axis=1).reshape(1, 512),
        lnb=jnp.stack([zeros, P["gmlp_ln_b"][l]], axis=1).reshape(1, 512),
        gmlp_ws=P["gmlp_ws"][l],
        bsx=jnp.broadcast_to(P["gmlp_bs"][l][:, :, None], (A_HEADS, CHUNK, CHUNK)),
        qg=jnp.tile(P["q_norm_g"][l], 2).reshape(1, LANES), kg=jnp.tile(P["k_norm_g"][l], 2).reshape(1, LANES),
        sinks=jnp.broadcast_to(P["sinks"][l][:, None], (8, LANES)),
        ssm_d=row(P["ssm_d"][l]),
    )
    are, aim = P["ssm_a_re"][l], P["ssm_a_im"][l]
    ldt = jnp.broadcast_to(P["ssm_log_dt"][l][:, None], (C_GROUPS, C_STATE))
    raw = dict(
        are=are, aim=aim, ldt=ldt,
        are_x=jnp.repeat(are, C_GROUP, axis=0), aim_x=jnp.repeat(aim, C_GROUP, axis=0), ldt_x=jnp.repeat(ldt, C_GROUP, axis=0),
        btr=P["ssm_b_re"][l].transpose(0, 2, 1).reshape(C_WIDTH, C_STATE), bti=P["ssm_b_im"][l].transpose(0, 2, 1).reshape(C_WIDTH, C_STATE),
        cre=P["ssm_c_re"][l].reshape(C_WIDTH, C_STATE), cim=P["ssm_c_im"][l].reshape(C_WIDTH, C_STATE),
    )
    return S, raw


def _ssm_prep_layer(raw):
    bbd, cbd, pwr, pwi = _ssm_prep(raw["are"].reshape(1, N_STATE), raw["aim"].reshape(1, N_STATE), raw["ldt"].reshape(1, N_STATE),
                                   raw["are_x"], raw["aim_x"], raw["ldt_x"], raw["btr"], raw["bti"], raw["cre"], raw["cim"])
    return dict(bbd=bbd, cbd=cbd, pwr=pwr, pwi=pwi)


def _behind(row, token):
    return row if token is None else row + token[0:1, 0:1]


def _local_step(x, p, positions, target, P, weights_of, mid_bwd, after_bwd):
    inv = 1.0 / (ROPE_THETA ** (jnp.arange(0, HEAD_DIM, 2, dtype=F32) / HEAD_DIM))
    cs, sn = _rope_tables(positions.reshape(-1, 1), jnp.tile(inv, 4).reshape(1, LANES))
    h = x
    acts, smalls, weights = [], [], []
    for l in range(DEPTH):
        W, late_weights, token = weights_of(l, h)
        S, raw = _small_layouts(P, l)
        sp = _ssm_prep_layer(raw)
        h, A, W = _layer_fwd(h, p, l, cs, sn, W, late_weights, {**S, "attn_norm_g": _behind(S["attn_norm_g"], token)}, sp)
        acts.append(A)
        smalls.append((S, raw, sp))
        weights.append(W)
    g, lsum = _loss_head(h, target)
    grads = [None] * DEPTH
    token = None
    for l in reversed(range(DEPTH)):
        S, raw, sp = smalls[l]
        g, grads[l] = _layer_bwd(g, p, l, cs, sn, weights[l], {**S, "ple_norm_g": _behind(S["ple_norm_g"], token)}, sp, acts[l], raw,
                                 functools.partial(mid_bwd, l))
        token = after_bwd(l, g, grads[l])
    return lsum[0, 0], g, grads


def _layer_weights(g):
    layout = dict(
        w_in=_from_col_major, glu_w1=lambda a: a.reshape(C_WIDTH, C_WIDTH), glu_w2=lambda a: a.reshape(C_WIDTH, C_WIDTH),
        w_out=lambda a: a.reshape(D_MODEL, D_MODEL), w_ff1=lambda a: a, w_ff2=lambda a: a.reshape(D_FF, D_MODEL),
        w_ple_gate=lambda a: a.reshape(D_MODEL, D_MODEL), w_ple_proj=_from_col_major)
    return {n: layout[n](a) for n, a in g.items()}


def kernel(x, p, positions, attn_norm_g, w_in, gmlp_ln_g, gmlp_ln_b, gmlp_ws, gmlp_bs, q_norm_g, k_norm_g, sinks, ssm_a_re, ssm_a_im, ssm_log_dt, ssm_b_re, ssm_b_im, ssm_c_re, ssm_c_im, ssm_d, glu_w1, glu_w2, mix_out_g, w_out, mlp_norm_g, w_ff1, w_ff2, ple_norm_g, w_ple_gate, w_ple_proj, loss_target, m_attn_norm_g, m_w_in, m_gmlp_ln_g, m_gmlp_ln_b, m_gmlp_ws, m_gmlp_bs, m_q_norm_g, m_k_norm_g, m_sinks, m_ssm_a_re, m_ssm_a_im, m_ssm_log_dt, m_ssm_b_re, m_ssm_b_im, m_ssm_c_re, m_ssm_c_im, m_ssm_d, m_glu_w1, m_glu_w2, m_mix_out_g, m_w_out, m_mlp_norm_g, m_w_ff1, m_w_ff2, m_ple_norm_g, m_w_ple_gate, m_w_ple_proj, v_attn_norm_g, v_w_in, v_gmlp_ln_g, v_gmlp_ln_b, v_gmlp_ws, v_gmlp_bs, v_q_norm_g, v_k_norm_g, v_sinks, v_ssm_a_re, v_ssm_a_im, v_ssm_log_dt, v_ssm_b_re, v_ssm_b_im, v_ssm_c_re, v_ssm_c_im, v_ssm_d, v_glu_w1, v_glu_w2, v_mix_out_g, v_w_out, v_mlp_norm_g, v_w_ff1, v_w_ff2, v_ple_norm_g, v_w_ple_gate, v_w_ple_proj):
    env = dict(locals())
    P = {n: env[n] for n in WEIGHTS}
    M = {n: env["m_" + n] for n in WEIGHTS}
    V = {n: env["v_" + n] for n in WEIGHTS}
    return _step(x, p, positions, loss_target, P, M, V)


def _step(x, p, positions, loss_target, P, M, V):
    small_shapes = [P[n].shape for n in SMALL]
    me = 4 * lax.axis_index("x") + 2 * lax.axis_index("y") + lax.axis_index("c")
    nothing = jnp.zeros((8, LANES), F32)

    def put(land, own, lead):
        return lax.dynamic_update_slice(land, own.reshape((1,) * len(lead) + own.shape), tuple(lead) + (0,) * own.ndim)

    def gather_start(l, names, after, tag):
        shards = [P[n][l].astype(WIRE) for n in names]
        lands = [lax.empty((N_DEV,) + s.shape, WIRE) for s in shards]
        send, recv, shards, lands, token = _split_start(shards, lands, ["all"] * len(names), 0, after, f"gather_start_{l}{tag}")
        return dict(names=names, send=send, recv=recv, shards=shards, lands=lands, token=token, name=f"gather_wait_{l}{tag}")

    def gather_wait(f, after):
        shards, lands = _split_wait(f["send"], f["recv"], f["shards"], f["lands"], ["all"] * len(f["names"]), 0, after, f["name"])
        return dict(zip(f["names"], [put(ld, sh, (me,)) for sh, ld in zip(shards, lands)]))

    first = gather_start(0, EARLY, nothing, "a")
    flying = {0: (first, gather_start(0, LATE, first["token"], "b"))}

    def weights_of(l, h):
        fa, fb = flying.pop(l)
        got = gather_wait(fa, h)
        if fb is None:
            token = None
            if l + 1 < DEPTH:
                flying[l + 1] = (gather_start(l + 1, SHARDED, got["w_in"], ""), None)
                token = flying[l + 1][0]["token"]
            W = _layer_weights(got)
            return {n: W[n] for n in EARLY}, (lambda after: ({n: W[n] for n in LATE}, None)), token

        def late_weights(after):
            late = gather_wait(fb, after)
            flying[l + 1] = (gather_start(l + 1, SHARDED, late["w_out"], ""), None)
            return _layer_weights(late), flying[l + 1][0]["token"]

        return _layer_weights(got), late_weights, fb["token"]

    grad_lands = {n: lax.empty((N_DEV,) + P[n].shape, WIRE) for n in SHARDED}
    sent = []

    def scatter_start(l, names, parts, lands, tag):
        send, recv, parts, lands, token = _split_start(parts, lands, ["own"] * len(parts), l, nothing, f"scatter_start_{l}{tag}")
        sent.append(dict(l=l, names=names, send=send, recv=recv, parts=parts, lands=lands, name=f"scatter_wait_{l}{tag}"))
        return token

    def scatter_wait(after):
        f = sent.pop(0)
        parts, lands = _split_wait(f["send"], f["recv"], f["parts"], f["lands"], ["own"] * len(f["parts"]), f["l"], after, f["name"])
        lands = [put(ld, lax.dynamic_index_in_dim(part, me, 0, keepdims=False), (me, f["l"])) for part, ld in zip(parts, lands)]
        return dict(zip(f["names"], lands))

    def mid_bwd(l, g1, G):
        if l > 0:
            return None
        grad_lands.update(scatter_wait(g1))
        return scatter_start(0, GRADS_MID, [G[n] for n in GRADS_MID], [grad_lands[n] for n in GRADS_MID], "a")

    small_land = []

    def after_bwd(l, g, G):
        if l > 0:
            if sent:
                grad_lands.update(scatter_wait(g))
            return scatter_start(l, SHARDED, [G[n] for n in SHARDED], [grad_lands[n] for n in SHARDED], "")
        sflat = _pack([jnp.stack([grads_of[k][n] for k in range(DEPTH)]) for n in SMALL], F32)
        sparts = sflat.reshape(N_DEV, -1, FLAT_COLS)
        small_land.append(sflat.shape)
        return scatter_start(0, GRADS_END + ("small",), [G[n] for n in GRADS_END] + [sparts],
                             [grad_lands[n] for n in GRADS_END] + [lax.empty((N_DEV, 1) + sparts.shape[1:], F32)], "b")

    grads_of = {}

    def after_bwd_recording(l, g, G):
        grads_of[l] = G
        return after_bwd(l, g, G)

    lsum, gx, grads = _local_step(x[0], p[:, 0], positions[0], loss_target[0], P, weights_of, mid_bwd, after_bwd_recording)
    grad_lands.update(scatter_wait(gx))
    last = scatter_wait(gx)
    small_parts = last.pop("small")
    grad_lands.update(last)
    G, delta, new_m, new_v = {}, {}, {}, {}
    for n in SHARDED:
        shp = P[n].shape
        res = _sum_adamw(grad_lands[n].reshape(N_DEV, -1, shp[-1]), *(a.reshape(-1, shp[-1]) for a in (P[n], M[n], V[n])))
        G[n], delta[n], new_m[n], new_v[n] = (a.reshape(shp) for a in res)
    small_sum = _gather_whole(_sum_slots(small_parts[:, 0])).reshape((1,) + small_land[0])
    res = _sum_adamw(small_sum, _pack([P[n] for n in SMALL], F32), _pack([M[n] for n in SMALL], F32), _pack([V[n] for n in SMALL], F32))
    for dst, flat in zip((G, delta, new_m, new_v), res):
        dst.update(zip(SMALL, _unpack(flat, small_shapes)))
    loss = lax.psum(lsum, ("x", "y", "c"))
    return (loss, gx[None], *[G[n] for n in WEIGHTS], *[delta[n] for n in WEIGHTS], *[new_m[n] for n in WEIGHTS], *[new_v[n] for n in WEIGHTS])
```

```python
import functools
import math

import jax
import jax.numpy as jnp
from jax import lax
from jax.experimental import pallas as pl
from jax.experimental.pallas import tpu as pltpu

F32 = jnp.float32
MXU = jnp.bfloat16
WIRE = jnp.bfloat16

D_MODEL = 1024
DEPTH = 4
HEAD_DIM = 64
A_WIDTH = 256
A_HEADS = 4
CHUNK = 128
B_WIDTH = 512
WINDOW = 128
C_WIDTH = 256
C_GROUP = 16
C_GROUPS = 16
C_STATE = 64
N_STATE = C_GROUPS * C_STATE
IN_COLS = 1536
D_FF = 4096
PLE_DIM = 256
EPS = 1e-6
ROPE_THETA = 10000.0
SCALE = HEAD_DIM ** -0.5
NEG = -1e30
N_DEV = 8

ADAM_LR = 0.001
ADAM_B1 = 0.9
ADAM_B2 = 0.999
ADAM_EPS = 1e-08
ADAM_WD = 0.01
ADAM_STEP = 10

V7X_VMEM_BYTES = 64 * 2 ** 20
VMEM_LIMIT = V7X_VMEM_BYTES - 8 * 2 ** 20
LANES = 128

MESH = pl.DeviceIdType.MESH


def _cp(*sem):
    return pltpu.CompilerParams(dimension_semantics=sem, vmem_limit_bytes=VMEM_LIMIT)


def _sds(shape, dtype=F32):
    return jax.ShapeDtypeStruct(shape, dtype)


def _mm(a, b):
    return jnp.dot(a.astype(MXU), b.astype(MXU), preferred_element_type=F32)


def _mm_nt(a, b):
    return lax.dot_general(a.astype(MXU), b.astype(MXU), (((1,), (1,)), ((), ())), preferred_element_type=F32)


def _mm_tn(a, b):
    return lax.dot_general(a.astype(MXU), b.astype(MXU), (((0,), (0,)), ((), ())), preferred_element_type=F32)


def _lane(shape):
    return lax.broadcasted_iota(jnp.int32, shape, len(shape) - 1)


def _row(shape):
    return lax.broadcasted_iota(jnp.int32, shape, 0)


_GELU_C = math.sqrt(2.0 / math.pi)


def _gelu(x):
    return 0.5 * x * (1.0 + jnp.tanh(_GELU_C * (x + 0.044715 * (x * x * x))))


def _gelu_grad(x):
    t = jnp.tanh(_GELU_C * (x + 0.044715 * (x * x * x)))
    return 0.5 * (1.0 + t) + 0.5 * x * (1.0 - t * t) * (_GELU_C * (1.0 + 3.0 * 0.044715 * (x * x)))


def _sigmoid(x):
    return 1.0 / (1.0 + jnp.exp(-x))


def _rms_stat(x):
    return lax.rsqrt(jnp.mean(x * x, axis=-1, keepdims=True) + EPS)


def _rms_bwd(x, r, g, dy):
    xh = x * r
    dxh = dy * g
    dx = r * (dxh - xh * jnp.mean(dxh * xh, axis=-1, keepdims=True))
    return dx, jnp.sum(dy * xh, axis=0, keepdims=True)


def _tril(w):
    return jnp.where(_row(w.shape) >= _lane(w.shape), w, 0.0)


def _swap64(x):
    return pltpu.roll(x, HEAD_DIM, 1)


def _group_sum64(x, lo):
    s_lo = jnp.sum(jnp.where(lo, x, 0.0), axis=-1, keepdims=True)
    s_hi = jnp.sum(jnp.where(lo, 0.0, x), axis=-1, keepdims=True)
    return jnp.where(lo, s_lo, s_hi)


def _partner(x):
    n = x.shape[-1]
    first = (_lane(x.shape) % HEAD_DIM) < HEAD_DIM // 2
    return jnp.where(first, pltpu.roll(x, n - HEAD_DIM // 2, 1), pltpu.roll(x, HEAD_DIM // 2, 1))


def _rope(y, cs, sn):
    return y * cs + _partner(y) * sn


def _rope_bwd(d, cs, sn):
    return d * cs + _partner(d * sn)


def _qk_norm_rope(x, g, cs, sn):
    lo = _lane(x.shape) < HEAD_DIM
    r = lax.rsqrt(_group_sum64(x * x, lo) * (1.0 / HEAD_DIM) + EPS)
    xh = x * r
    return _rope(xh * g, cs, sn), xh, r


def _qk_norm_rope_bwd(xh, r, g, cs, sn, d):
    lo = _lane(xh.shape) < HEAD_DIM
    dy = _rope_bwd(d, cs, sn)
    dxh = dy * g
    m = _group_sum64(dxh * xh, lo) * (1.0 / HEAD_DIM)
    return r * (dxh - xh * m), jnp.sum(dy * xh, axis=0, keepdims=True)


def _gmlp_head(blk, g, b):
    hi = _lane(blk.shape) >= HEAD_DIM
    mu = jnp.sum(jnp.where(hi, blk, 0.0), axis=-1, keepdims=True) * (1.0 / HEAD_DIM)
    xc = jnp.where(hi, blk - mu, 0.0)
    rstd = lax.rsqrt(jnp.sum(xc * xc, axis=-1, keepdims=True) * (1.0 / HEAD_DIM) + EPS)
    vhat = xc * rstd
    return vhat * g + b, vhat, rstd


def _rope_tables(pos_col, inv_row):
    T = pos_col.shape[0]
    tm = min(T, 1024)

    def body(p_ref, inv_ref, cs_ref, sn_ref):
        ang = p_ref[...].astype(F32) * inv_ref[...]
        s = jnp.sin(ang)
        cs_ref[...] = jnp.cos(ang)
        sn_ref[...] = jnp.where((_lane(ang.shape) % HEAD_DIM) < HEAD_DIM // 2, -s, s)

    blk = pl.BlockSpec((tm, LANES), lambda i: (i, 0))
    return pl.pallas_call(
        body, name="rope_tables", grid=(T // tm,),
        in_specs=[pl.BlockSpec((tm, 1), lambda i: (i, 0)), pl.BlockSpec((1, LANES), lambda i: (0, 0))],
        out_specs=[blk, blk], out_shape=[_sds((T, LANES))] * 2, compiler_params=_cp("parallel"))(pos_col, inv_row)


def _inproj_fwd(h, g, w):
    T = h.shape[0]
    tm = min(T, 512)

    def body(h_ref, g_ref, w_ref, z_ref):
        x = h_ref[...]
        z_ref[...] = _mm(x * _rms_stat(x) * g_ref[...], w_ref[...])

    return pl.pallas_call(
        body, name="inproj_fwd", grid=(T // tm,),
        in_specs=[pl.BlockSpec((tm, D_MODEL), lambda i: (i, 0)), pl.BlockSpec((1, D_MODEL), lambda i: (0, 0)),
                  pl.BlockSpec((D_MODEL, IN_COLS), lambda i: (0, 0))],
        out_specs=pl.BlockSpec((tm, IN_COLS), lambda i: (i, 0)), out_shape=_sds((T, IN_COLS)),
        compiler_params=_cp("parallel"))(h, g, w)


def _inproj_bwd(gres, h, g, w, dza, dzq, dzk, dzv, dzc):
    T = h.shape[0]
    tm = min(T, 512)

    def body(gr_ref, h_ref, g_ref, w_ref, a_ref, q_ref, k_ref, v_ref, c_ref, dh_ref, xn_ref, dz_ref, dg_ref):
        @pl.when(pl.program_id(0) == 0)
        def _():
            dg_ref[...] = jnp.zeros_like(dg_ref)

        x = h_ref[...]
        r = _rms_stat(x)
        gg = g_ref[...]
        dz = jnp.concatenate([a_ref[...], q_ref[...], k_ref[...], v_ref[...], c_ref[...]], axis=1)
        dxn = _mm_nt(dz, w_ref[...])
        dx, dg = _rms_bwd(x, r, gg, dxn)
        dh_ref[...] = gr_ref[...] + dx
        dg_ref[...] += dg
        xn_ref[...] = (x * r * gg).astype(MXU)
        dz_ref[...] = dz.astype(MXU)

    def rows(w_):
        return pl.BlockSpec((tm, w_), lambda i: (i, 0))

    row = pl.BlockSpec((1, D_MODEL), lambda i: (0, 0))
    return pl.pallas_call(
        body, name="inproj_bwd", grid=(T // tm,),
        in_specs=[rows(D_MODEL), rows(D_MODEL), row, pl.BlockSpec((D_MODEL, IN_COLS), lambda i: (0, 0)),
                  rows(512), rows(512), rows(128), rows(128), rows(256)],
        out_specs=[rows(D_MODEL), rows(D_MODEL), rows(IN_COLS), row],
        out_shape=[_sds((T, D_MODEL)), _sds((T, D_MODEL), MXU), _sds((T, IN_COLS), MXU), _sds((1, D_MODEL))],
        compiler_params=_cp("arbitrary"))(gres, h, g, w, dza, dzq, dzk, dzv, dzc)


def _gmlp_fwd(z, lng, lnb, ws, bsx):
    T = z.shape[0]
    tm = min(T, 512)
    nc = tm // CHUNK

    def body(z_ref, g_ref, b_ref, w_ref, bs_ref, ya_ref):
        zg = _gelu(z_ref[...])
        lo = _lane((tm, LANES)) < HEAD_DIM
        prods = []
        for hd in range(A_HEADS):
            sl = slice(hd * LANES, (hd + 1) * LANES)
            blk = zg[:, sl]
            vn, _, _ = _gmlp_head(blk, g_ref[:, sl], b_ref[:, sl])
            wm = _tril(w_ref[hd])
            sv = jnp.concatenate([_mm(wm, vn[c * CHUNK:(c + 1) * CHUNK]) + bs_ref[hd] for c in range(nc)], axis=0)
            prods.append(blk * _swap64(sv))
        ya_ref[:, 0:LANES] = jnp.where(lo, prods[0], _swap64(prods[1]))
        ya_ref[:, LANES:2 * LANES] = jnp.where(lo, prods[2], _swap64(prods[3]))

    row = pl.BlockSpec((1, 512), lambda i: (0, 0))
    mat = pl.BlockSpec((A_HEADS, CHUNK, CHUNK), lambda i: (0, 0, 0))
    return pl.pallas_call(
        body, name="gmlp_fwd", grid=(T // tm,),
        in_specs=[pl.BlockSpec((tm, 512), lambda i: (i, 0)), row, row, mat, mat],
        out_specs=pl.BlockSpec((tm, A_WIDTH), lambda i: (i, 0)), out_shape=_sds((T, A_WIDTH)),
        compiler_params=_cp("parallel"))(z, lng, lnb, ws, bsx)


def _gmlp_bwd(z, dya, lng, lnb, ws, bsx):
    T = z.shape[0]
    tm = min(T, 512)
    nc = tm // CHUNK

    def body(z_ref, dya_ref, g_ref, b_ref, w_ref, bs_ref, dza_ref, dw_ref, dbs_ref, dg_ref, db_ref):
        @pl.when(pl.program_id(0) == 0)
        def _():
            dw_ref[...] = jnp.zeros_like(dw_ref)
            dbs_ref[...] = jnp.zeros_like(dbs_ref)
            dg_ref[...] = jnp.zeros_like(dg_ref)
            db_ref[...] = jnp.zeros_like(db_ref)

        za = z_ref[...]
        zg = _gelu(za)
        gp = _gelu_grad(za)
        lo = _lane((tm, LANES)) < HEAD_DIM
        for hd in range(A_HEADS):
            sl = slice(hd * LANES, (hd + 1) * LANES)
            blk = zg[:, sl]
            g = g_ref[:, sl]
            vn, vhat, rstd = _gmlp_head(blk, g, b_ref[:, sl])
            wm = _tril(w_ref[hd])
            pair = dya_ref[:, (hd // 2) * LANES:(hd // 2 + 1) * LANES]
            dy = jnp.where(lo, pair if hd % 2 == 0 else _swap64(pair), 0.0)
            dsv = _swap64(dy * blk)
            svs, dvns = [], []
            dw = jnp.zeros((CHUNK, CHUNK), F32)
            dbs = jnp.zeros((CHUNK, 1), F32)
            for c in range(nc):
                cs = slice(c * CHUNK, (c + 1) * CHUNK)
                svs.append(_mm(wm, vn[cs]) + bs_ref[hd])
                dw = dw + _mm_nt(dsv[cs], vn[cs])
                dbs = dbs + jnp.sum(dsv[cs], axis=-1, keepdims=True)
                dvns.append(_mm_tn(wm, dsv[cs]))
            sv = jnp.concatenate(svs, axis=0)
            dvn = jnp.concatenate(dvns, axis=0)
            dw_ref[hd] += _tril(dw)
            dbs_ref[hd] += jnp.broadcast_to(dbs, (CHUNK, CHUNK))
            dg_ref[:, sl] += jnp.sum(dvn * vhat, axis=0, keepdims=True)
            db_ref[:, sl] += jnp.sum(dvn, axis=0, keepdims=True)
            du = dy * _swap64(sv)
            dvh = dvn * g
            m1 = jnp.sum(dvh, axis=-1, keepdims=True) * (1.0 / HEAD_DIM)
            m2 = jnp.sum(dvh * vhat, axis=-1, keepdims=True) * (1.0 / HEAD_DIM)
            dv = jnp.where(lo, 0.0, rstd * (dvh - m1 - vhat * m2))
            dza_ref[:, sl] = (du + dv) * gp[:, sl]

    row = pl.BlockSpec((1, 512), lambda i: (0, 0))
    mat = pl.BlockSpec((A_HEADS, CHUNK, CHUNK), lambda i: (0, 0, 0))
    return pl.pallas_call(
        body, name="gmlp_bwd", grid=(T // tm,),
        in_specs=[pl.BlockSpec((tm, 512), lambda i: (i, 0)), pl.BlockSpec((tm, A_WIDTH), lambda i: (i, 0)), row, row, mat, mat],
        out_specs=[pl.BlockSpec((tm, 512), lambda i: (i, 0)), mat, mat, row, row],
        out_shape=[_sds((T, 512)), _sds((A_HEADS, CHUNK, CHUNK)), _sds((A_HEADS, CHUNK, CHUNK)), _sds((1, 512)), _sds((1, 512))],
        compiler_params=_cp("arbitrary"))(z, dya, lng, lnb, ws, bsx)


def _attn_specs(T, tq, tile_of):
    nb = tq // WINDOW

    def prev(i):
        return jnp.maximum(tile_of(i) * nb - 1, 0)

    row = pl.BlockSpec((1, LANES), lambda i: (0, 0))
    return [
        pl.BlockSpec((tq, B_WIDTH), lambda i: (tile_of(i), 1)),
        pl.BlockSpec((tq, LANES), lambda i: (tile_of(i), 8)),
        pl.BlockSpec((tq, LANES), lambda i: (tile_of(i), 9)),
        pl.BlockSpec((WINDOW, LANES), lambda i: (prev(i), 8)),
        pl.BlockSpec((WINDOW, LANES), lambda i: (prev(i), 9)),
        pl.BlockSpec((tq, LANES), lambda i: (tile_of(i), 0)),
        pl.BlockSpec((tq, LANES), lambda i: (tile_of(i), 0)),
        pl.BlockSpec((WINDOW, LANES), lambda i: (prev(i), 0)),
        pl.BlockSpec((WINDOW, LANES), lambda i: (prev(i), 0)),
        row, row,
        pl.BlockSpec((8, LANES), lambda i: (0, 0)),
    ]


def _attn_bias(first):
    qi = lax.broadcasted_iota(jnp.int32, (WINDOW, 2 * WINDOW), 0)
    kj = lax.broadcasted_iota(jnp.int32, (WINDOW, 2 * WINDOW), 1)
    diff = qi + WINDOW - kj
    ok = (diff >= 0) & (diff < WINDOW) & ((kj >= WINDOW) | jnp.logical_not(first))
    return jnp.where(ok, 0.0, NEG)


def _dup_heads(x, lo):
    sw = _swap64(x)
    return jnp.where(lo, x, sw), jnp.where(lo, sw, x)


HEADS_PER_KV = 4


def _stack_heads(x0, x1, lo):
    return jnp.concatenate([jnp.where(lo, x0, 0.0), jnp.where(lo, 0.0, x0), jnp.where(lo, x1, 0.0), jnp.where(lo, 0.0, x1)], axis=0)


def _unstack_heads(x4, lo):
    return (jnp.where(lo, x4[0:WINDOW], x4[WINDOW:2 * WINDOW]), jnp.where(lo, x4[2 * WINDOW:3 * WINDOW], x4[3 * WINDOW:4 * WINDOW]))


def _sink_column(sk_ref, g):
    return jnp.concatenate([jnp.broadcast_to(sk_ref[a:a + 1, 0:1], (WINDOW, 1)) for a in range(HEADS_PER_KV * g, HEADS_PER_KV * (g + 1))], axis=0)


def _attn_probs(q4, kw, bias, sink):
    s = _mm_nt(q4, kw)
    s = (s.reshape(HEADS_PER_KV, WINDOW, 2 * WINDOW) + bias[None]).reshape(HEADS_PER_KV * WINDOW, 2 * WINDOW)
    m = jnp.maximum(jnp.max(s, axis=-1, keepdims=True), sink)
    p = jnp.exp(s - m)
    es = jnp.exp(sink - m)
    inv = 1.0 / (jnp.sum(p, axis=-1, keepdims=True) + es)
    return p * inv, es * inv


def _attn_fwd(z, cs, sn, qg, kg, sinks):
    T = z.shape[0]
    tq = min(T, 512)
    nb = tq // WINDOW

    def body(q_ref, k_ref, v_ref, kp_ref, vp_ref, cs_ref, sn_ref, csp_ref, snp_ref, qg_ref, kg_ref, sk_ref, o_ref):
        i = pl.program_id(0)
        csq, snq = cs_ref[...], sn_ref[...]
        cs_all = jnp.concatenate([csp_ref[...], csq], axis=0)
        sn_all = jnp.concatenate([snp_ref[...], snq], axis=0)
        k_all = jnp.concatenate([kp_ref[...], k_ref[...]], axis=0)
        v_all = jnp.concatenate([vp_ref[...], v_ref[...]], axis=0)
        kr, _, _ = _qk_norm_rope(k_all, kg_ref[...], cs_all, sn_all)
        lo_all = _lane(k_all.shape) < HEAD_DIM
        kd = _dup_heads(kr, lo_all)
        vd = _dup_heads(v_all, lo_all)
        lo = _lane((WINDOW, LANES)) < HEAD_DIM
        qrs = [_qk_norm_rope(q_ref[:, pr * LANES:(pr + 1) * LANES], qg_ref[...], csq, snq)[0] * SCALE for pr in range(4)]
        biases = [_attn_bias(i * nb + b == 0) for b in range(nb)]
        for g in range(2):
            sink = _sink_column(sk_ref, g)
            for b in range(nb):
                bs = slice(b * WINDOW, (b + 1) * WINDOW)
                ws = slice(b * WINDOW, (b + 2) * WINDOW)
                pn, _ = _attn_probs(_stack_heads(qrs[2 * g][bs], qrs[2 * g + 1][bs], lo), kd[g][ws], biases[b], sink)
                o0, o1 = _unstack_heads(_mm(pn, vd[g][ws]), lo)
                o_ref[bs, 2 * g * LANES:(2 * g + 1) * LANES] = o0
                o_ref[bs, (2 * g + 1) * LANES:(2 * g + 2) * LANES] = o1

    return pl.pallas_call(
        body, name="attn_fwd", grid=(T // tq,),
        in_specs=_attn_specs(T, tq, lambda i: i),
        out_specs=pl.BlockSpec((tq, B_WIDTH), lambda i: (i, 0)), out_shape=_sds((T, B_WIDTH)),
        compiler_params=_cp("parallel"))(z, z, z, z, z, cs, sn, cs, sn, qg, kg, sinks)


def _attn_bwd(z, cs, sn, qg, kg, sinks, o, do):
    T = z.shape[0]
    tq = min(T, 512)
    nb = tq // WINDOW
    nt = T // tq
    tk = tq + WINDOW

    def tile_of(i):
        return nt - 1 - i

    def body(q_ref, k_ref, v_ref, kp_ref, vp_ref, cs_ref, sn_ref, csp_ref, snp_ref, qg_ref, kg_ref, sk_ref, o_ref, do_ref,
             dq_ref, dk_ref, dv_ref, dqg_ref, dkg_ref, dsk_ref, acck, accv, ck, cv):
        i = pl.program_id(0)
        ti = nt - 1 - i

        @pl.when(i == 0)
        def _():
            dqg_ref[...] = jnp.zeros_like(dqg_ref)
            dkg_ref[...] = jnp.zeros_like(dkg_ref)
            dsk_ref[...] = jnp.zeros_like(dsk_ref)
            ck[...] = jnp.zeros_like(ck)
            cv[...] = jnp.zeros_like(cv)

        csq, snq = cs_ref[...], sn_ref[...]
        cs_all = jnp.concatenate([csp_ref[...], csq], axis=0)
        sn_all = jnp.concatenate([snp_ref[...], snq], axis=0)
        k_all = jnp.concatenate([kp_ref[...], k_ref[...]], axis=0)
        v_all = jnp.concatenate([vp_ref[...], v_ref[...]], axis=0)
        kr, kh, rk = _qk_norm_rope(k_all, kg_ref[...], cs_all, sn_all)
        lo_all = _lane(k_all.shape) < HEAD_DIM
        kd = _dup_heads(kr, lo_all)
        vd = _dup_heads(v_all, lo_all)
        lo = _lane((WINDOW, LANES)) < HEAD_DIM
        acck[...] = jnp.zeros_like(acck)
        accv[...] = jnp.zeros_like(accv)
        prep = [_qk_norm_rope(q_ref[:, pr * LANES:(pr + 1) * LANES], qg_ref[...], csq, snq) for pr in range(4)]
        biases = [_attn_bias(ti * nb + b == 0) for b in range(nb)]
        dqs = [[None] * nb for _ in range(4)]
        for g in range(2):
            sink = _sink_column(sk_ref, g)
            dsink = jnp.zeros((HEADS_PER_KV * WINDOW, 1), F32)
            for b in range(nb):
                bs = slice(b * WINDOW, (b + 1) * WINDOW)
                ws = slice(b * WINDOW, (b + 2) * WINDOW)
                kw, vw = kd[g][ws], vd[g][ws]
                q4 = _stack_heads(prep[2 * g][0][bs] * SCALE, prep[2 * g + 1][0][bs] * SCALE, lo)
                pn, psink = _attn_probs(q4, kw, biases[b], sink)
                o0, o1 = o_ref[bs, 2 * g * LANES:(2 * g + 1) * LANES], o_ref[bs, (2 * g + 1) * LANES:(2 * g + 2) * LANES]
                do4 = _stack_heads(do_ref[bs, 2 * g * LANES:(2 * g + 1) * LANES], do_ref[bs, (2 * g + 1) * LANES:(2 * g + 2) * LANES], lo)
                delta = jnp.sum(do4 * jnp.concatenate([o0, o0, o1, o1], axis=0), axis=-1, keepdims=True)
                ds = pn * (_mm_nt(do4, vw) - delta)
                dsink = dsink - psink * delta
                dqs[2 * g][b], dqs[2 * g + 1][b] = _unstack_heads(_mm(ds, kw) * SCALE, lo)
                acck[g, ws, :] += _mm_tn(ds, q4)
                accv[g, ws, :] += _mm_tn(pn, do4)
            for hh in range(HEADS_PER_KV):
                a = HEADS_PER_KV * g + hh
                dsk_ref[a:a + 1, :] += jnp.zeros((1, LANES), F32) + jnp.sum(dsink[hh * WINDOW:(hh + 1) * WINDOW])
        for pr in range(4):
            _, qh, rq = prep[pr]
            dx, dg = _qk_norm_rope_bwd(qh, rq, qg_ref[...], csq, snq, jnp.concatenate(dqs[pr], axis=0))
            dq_ref[:, pr * LANES:(pr + 1) * LANES] = dx
            dqg_ref[...] += dg

        def fold(acc):
            f0 = acc[0] + _swap64(acc[0])
            f1 = acc[1] + _swap64(acc[1])
            return jnp.where(lo_all, f0, f1)

        dk_all = fold(acck)
        dv_all = fold(accv)
        pad = jnp.zeros((tq - WINDOW, LANES), F32)
        dk_own = dk_all[WINDOW:] + (jnp.concatenate([pad, ck[...]], axis=0) if nb > 1 else ck[...])
        dv_own = dv_all[WINDOW:] + (jnp.concatenate([pad, cv[...]], axis=0) if nb > 1 else cv[...])
        ck[...] = dk_all[:WINDOW]
        cv[...] = dv_all[:WINDOW]
        dxk, dgk = _qk_norm_rope_bwd(kh[WINDOW:], rk[WINDOW:], kg_ref[...], csq, snq, dk_own)
        dk_ref[...] = dxk
        dkg_ref[...] += dgk
        dv_ref[...] = dv_own

    row = pl.BlockSpec((1, LANES), lambda i: (0, 0))
    return pl.pallas_call(
        body, name="attn_bwd", grid=(nt,),
        in_specs=_attn_specs(T, tq, tile_of) + [pl.BlockSpec((tq, B_WIDTH), lambda i: (tile_of(i), 0))] * 2,
        out_specs=[pl.BlockSpec((tq, B_WIDTH), lambda i: (tile_of(i), 0)), pl.BlockSpec((tq, LANES), lambda i: (tile_of(i), 0)),
                   pl.BlockSpec((tq, LANES), lambda i: (tile_of(i), 0)), row, row, pl.BlockSpec((8, LANES), lambda i: (0, 0))],
        out_shape=[_sds((T, B_WIDTH)), _sds((T, LANES)), _sds((T, LANES)), _sds((1, LANES)), _sds((1, LANES)), _sds((8, LANES))],
        scratch_shapes=[pltpu.VMEM((2, tk, LANES), F32), pltpu.VMEM((2, tk, LANES), F32),
                        pltpu.VMEM((WINDOW, LANES), F32), pltpu.VMEM((WINDOW, LANES), F32)],
        compiler_params=_cp("arbitrary"))(z, z, z, z, z, cs, sn, cs, sn, qg, kg, sinks, o, do)


def _bbar_t(are, aim, ldt, btr, bti):
    lbr, lbi = _lam_bar(are, aim, ldt)
    den = are * are + aim * aim
    nr = lbr - 1.0
    cr = (nr * are + lbi * aim) / den
    ci = (lbi * are - nr * aim) / den
    return cr * btr - ci * bti, cr * bti + ci * btr


def _lam_bar(are, aim, ldt):
    dt = jnp.exp(ldt)
    er = jnp.exp(are * dt)
    return er * jnp.cos(aim * dt), er * jnp.sin(aim * dt)


def _block_diag(x):
    t = jnp.concatenate([x] * C_GROUPS, axis=1)
    return jnp.where(_row(t.shape) // C_GROUP == _lane(t.shape) // C_STATE, t, 0.0)


def _block_diag_fold(m):
    rg = _row((C_WIDTH, C_STATE)) // C_GROUP
    acc = jnp.zeros((C_WIDTH, C_STATE), F32)
    for g in range(C_GROUPS):
        acc = acc + jnp.where(rg == g, m[:, g * C_STATE:(g + 1) * C_STATE], 0.0)
    return acc


def _ssm_prep(are, aim, ldt, are_x, aim_x, ldt_x, btr, bti, cre, cim):
    def body(are_r, aim_r, ldt_r, arex_r, aimx_r, ldtx_r, btr_r, bti_r, cre_r, cim_r, bbd_ref, cbd_ref, pwr_ref, pwi_ref):
        lr, li = _lam_bar(are_r[...], aim_r[...], ldt_r[...])
        cr, ci = lr, li
        for r in range(SCAN_SEG):
            pwr_ref[r:r + 1, :] = cr
            pwi_ref[r:r + 1, :] = ci
            cr, ci = cr * lr - ci * li, cr * li + ci * lr
        br, bi = _bbar_t(arex_r[...], aimx_r[...], ldtx_r[...], btr_r[...], bti_r[...])
        bbd_ref[...] = jnp.concatenate([_block_diag(br), _block_diag(bi)], axis=1).astype(MXU)
        cbd_ref[...] = jnp.concatenate([_block_diag(cre_r[...]), -_block_diag(cim_r[...])], axis=1).astype(MXU)

    return pl.pallas_call(
        body, name="ssm_prep",
        out_shape=[_sds((C_WIDTH, 2 * N_STATE), MXU), _sds((C_WIDTH, 2 * N_STATE), MXU), _sds((SCAN_SEG, N_STATE)), _sds((SCAN_SEG, N_STATE))],
        compiler_params=pltpu.CompilerParams(vmem_limit_bytes=VMEM_LIMIT))(are, aim, ldt, are_x, aim_x, ldt_x, btr, bti, cre, cim)


def _ssm_param_bwd(are, aim, ldt, are_x, aim_x, ldt_x, btr, bti, dlr, dli, dbbd, dcr, dci):
    def body(are_r, aim_r, ldt_r, arex_r, aimx_r, ldtx_r, btr_r, bti_r, dlr_r, dli_r, dbbd_r, dcr_r, dci_r,
             dare_ref, daim_ref, dldt_ref, dbtr_ref, dbti_ref, dcre_ref, dcim_ref):
        _, vjp_l = jax.vjp(_lam_bar, are_r[...], aim_r[...], ldt_r[...])
        da1, di1, dl1 = vjp_l((dlr_r[...], dli_r[...]))
        dbr = _block_diag_fold(dbbd_r[:, 0:N_STATE])
        dbi = _block_diag_fold(dbbd_r[:, N_STATE:2 * N_STATE])
        _, vjp_b = jax.vjp(_bbar_t, arex_r[...], aimx_r[...], ldtx_r[...], btr_r[...], bti_r[...])
        da2, di2, dl2, dbtr, dbti = vjp_b((dbr, dbi))

        def gsum(x):
            return x.reshape(C_GROUPS, C_GROUP, C_STATE).sum(axis=1)

        dare_ref[...] = da1 + gsum(da2)
        daim_ref[...] = di1 + gsum(di2)
        dldt_ref[...] = jnp.broadcast_to(jnp.sum(dl1 + gsum(dl2), axis=-1, keepdims=True), (C_GROUPS, LANES))
        dbtr_ref[...] = dbtr
        dbti_ref[...] = dbti
        dcre_ref[...] = _block_diag_fold(dcr_r[...])
        dcim_ref[...] = -_block_diag_fold(dci_r[...])

    g = _sds((C_GROUPS, C_STATE))
    x = _sds((C_WIDTH, C_STATE))
    return pl.pallas_call(
        body, name="ssm_param_bwd", out_shape=[g, g, _sds((C_GROUPS, LANES)), x, x, x, x],
        compiler_params=pltpu.CompilerParams(vmem_limit_bytes=VMEM_LIMIT))(are, aim, ldt, are_x, aim_x, ldt_x, btr, bti, dlr, dli, dbbd, dcr, dci)


SCAN_TILE = 256
SCAN_SEG = 8


def _scan_tables(pwr_ref, pwi_ref, conj, reverse):
    row = _row((SCAN_SEG, N_STATE))
    shifts = []
    for k in (1, 2, 4):
        keep = (row < SCAN_SEG - k) if reverse else (row >= k)
        ar = jnp.broadcast_to(pwr_ref[k - 1:k, :], (SCAN_SEG, N_STATE))
        ai = jnp.broadcast_to(pwi_ref[k - 1:k, :], (SCAN_SEG, N_STATE)) * conj
        shifts.append((SCAN_SEG - k if reverse else k, jnp.where(keep, ar, 0.0), jnp.where(keep, ai, 0.0)))
    if reverse:
        pr = jnp.concatenate([pwr_ref[SCAN_SEG - 1 - r:SCAN_SEG - r, :] for r in range(SCAN_SEG)], axis=0)
        pi = jnp.concatenate([pwi_ref[SCAN_SEG - 1 - r:SCAN_SEG - r, :] for r in range(SCAN_SEG)], axis=0) * conj
    else:
        pr, pi = pwr_ref[...], pwi_ref[...] * conj
    return shifts, (pr, pi)


def _tile_scan(xr_ref, xi_ref, pwr_ref, pwi_ref, sr, si, conj, reverse):
    shifts, (pr, pi) = _scan_tables(pwr_ref, pwi_ref, conj, reverse)
    groups = xr_ref.shape[0] // SCAN_SEG
    out_row = 0 if reverse else SCAN_SEG - 1

    def step(k, c):
        cr, ci = c
        g = groups - 1 - k if reverse else k
        rows = pl.ds(pl.multiple_of(g * SCAN_SEG, SCAN_SEG), SCAN_SEG)
        xr, xi = xr_ref[rows, :], xi_ref[rows, :]
        for amount, ar, ai in shifts:
            qr, qi = pltpu.roll(xr, amount, 0), pltpu.roll(xi, amount, 0)
            xr, xi = xr + ar * qr - ai * qi, xi + ar * qi + ai * qr
        xr, xi = xr + pr * cr - pi * ci, xi + pr * ci + pi * cr
        xr_ref[rows, :] = xr
        xi_ref[rows, :] = xi
        return xr[out_row:out_row + 1], xi[out_row:out_row + 1]

    cr, ci = lax.fori_loop(0, groups, step, (sr[...], si[...]), unroll=2)
    sr[...] = cr
    si[...] = ci


def _ssm_fwd(z, bbd, pwr, pwi, cbd, dsk, w1, w2):
    T = z.shape[0]
    tt = min(T, SCAN_TILE)

    def body(u_ref, bbd_ref, pwr_ref, pwi_ref, cbd_ref, d_ref, w1_ref, w2_ref, yc_ref, xr_ref, xi_ref, sr, si):
        @pl.when(pl.program_id(0) == 0)
        def _():
            sr[...] = jnp.zeros_like(sr)
            si[...] = jnp.zeros_like(si)

        u = u_ref[...]
        bu = _mm(u, bbd_ref[...])
        xr_ref[...] = bu[:, 0:N_STATE]
        xi_ref[...] = bu[:, N_STATE:2 * N_STATE]
        _tile_scan(xr_ref, xi_ref, pwr_ref, pwi_ref, sr, si, 1.0, False)
        x = jnp.concatenate([xr_ref[...], xi_ref[...]], axis=1)
        y2 = _gelu(_mm_nt(x, cbd_ref[...]) + d_ref[...] * u)
        yc_ref[...] = _mm(y2, w1_ref[...]) * _sigmoid(_mm(y2, w2_ref[...]))

    big = pl.BlockSpec((C_WIDTH, 2 * N_STATE), lambda i: (0, 0))
    tab = pl.BlockSpec((SCAN_SEG, N_STATE), lambda i: (0, 0))
    wsp = pl.BlockSpec((C_WIDTH, C_WIDTH), lambda i: (0, 0))
    xs = pl.BlockSpec((tt, N_STATE), lambda i: (i, 0))
    return pl.pallas_call(
        body, name="ssm_fwd", grid=(T // tt,),
        in_specs=[pl.BlockSpec((tt, C_WIDTH), lambda i: (i, 5)), big, tab, tab, big, pl.BlockSpec((1, C_WIDTH), lambda i: (0, 0)), wsp, wsp],
        out_specs=[pl.BlockSpec((tt, C_WIDTH), lambda i: (i, 0)), xs, xs],
        out_shape=[_sds((T, C_WIDTH)), _sds((T, N_STATE)), _sds((T, N_STATE))],
        scratch_shapes=[pltpu.VMEM((1, N_STATE), F32)] * 2,
        compiler_params=_cp("arbitrary"))(z, bbd, pwr, pwi, cbd, dsk, w1, w2)


def _ssm_bwd(dyc, z, xr, xi, bbd, pwr, pwi, cbd, dsk, w1, w2):
    T = z.shape[0]
    tt = min(T, SCAN_TILE)
    nt = T // tt

    def tile_of(i):
        return nt - 1 - i

    def body(dyc_ref, u_ref, xr_ref, xi_ref, xpr_ref, xpi_ref, bbd_ref, pwr_ref, pwi_ref, cbd_ref, d_ref, w1_ref, w2_ref,
             du_ref, y2_ref, da1_ref, da2_ref, dy_ref, arai_ref, dlr_ref, dli_ref, dd_ref, gr, gi, sr, si):
        i = pl.program_id(0)
        ti = nt - 1 - i

        @pl.when(i == 0)
        def _():
            sr[...] = jnp.zeros_like(sr)
            si[...] = jnp.zeros_like(si)
            dlr_ref[...] = jnp.zeros_like(dlr_ref)
            dli_ref[...] = jnp.zeros_like(dli_ref)
            dd_ref[...] = jnp.zeros_like(dd_ref)

        u = u_ref[...]
        xr_t, xi_t = xr_ref[...], xi_ref[...]
        y = _mm_nt(jnp.concatenate([xr_t, xi_t], axis=1), cbd_ref[...]) + d_ref[...] * u
        y2 = _gelu(y)
        a1 = _mm(y2, w1_ref[...])
        sg = _sigmoid(_mm(y2, w2_ref[...]))
        dyc_t = dyc_ref[...]
        da1 = dyc_t * sg
        da2 = dyc_t * a1 * sg * (1.0 - sg)
        dy = (_mm_nt(da1, w1_ref[...]) + _mm_nt(da2, w2_ref[...])) * _gelu_grad(y)
        gx = _mm(dy, cbd_ref[...])
        gr[...] = gx[:, 0:N_STATE]
        gi[...] = gx[:, N_STATE:2 * N_STATE]
        _tile_scan(gr, gi, pwr_ref, pwi_ref, sr, si, -1.0, True)
        ar, ai = gr[...], gi[...]
        first_row = _row(ar.shape) == 0
        live = jnp.where(ti > 0, 1.0, 0.0)
        xsr = jnp.where(first_row, xpr_ref[7:8, :] * live, pltpu.roll(xr_t, 1, 0))
        xsi = jnp.where(first_row, xpi_ref[7:8, :] * live, pltpu.roll(xi_t, 1, 0))
        dlr_ref[...] += jnp.sum(ar * xsr + ai * xsi, axis=0, keepdims=True)
        dli_ref[...] += jnp.sum(ai * xsr - ar * xsi, axis=0, keepdims=True)
        dd_ref[...] += jnp.sum(dy * u, axis=0, keepdims=True)
        arai = jnp.concatenate([ar, ai], axis=1)
        du_ref[...] = _mm_nt(arai, bbd_ref[...]) + d_ref[...] * dy
        y2_ref[...] = y2.astype(MXU)
        da1_ref[...] = da1.astype(MXU)
        da2_ref[...] = da2.astype(MXU)
        dy_ref[...] = dy.astype(MXU)
        arai_ref[...] = arai.astype(MXU)

    def prev(i):
        return jnp.maximum(tile_of(i) * (tt // 8) - 1, 0)

    big = pl.BlockSpec((C_WIDTH, 2 * N_STATE), lambda i: (0, 0))
    tab = pl.BlockSpec((SCAN_SEG, N_STATE), lambda i: (0, 0))
    srow = pl.BlockSpec((1, N_STATE), lambda i: (0, 0))
    wsp = pl.BlockSpec((C_WIDTH, C_WIDTH), lambda i: (0, 0))
    xs = pl.BlockSpec((tt, N_STATE), lambda i: (tile_of(i), 0))
    xp = pl.BlockSpec((8, N_STATE), lambda i: (prev(i), 0))
    cw = pl.BlockSpec((tt, C_WIDTH), lambda i: (tile_of(i), 0))
    drow = pl.BlockSpec((1, C_WIDTH), lambda i: (0, 0))
    return pl.pallas_call(
        body, name="ssm_bwd", grid=(nt,),
        in_specs=[cw, pl.BlockSpec((tt, C_WIDTH), lambda i: (tile_of(i), 5)), xs, xs, xp, xp, big, tab, tab, big, drow, wsp, wsp],
        out_specs=[cw, cw, cw, cw, cw, pl.BlockSpec((tt, 2 * N_STATE), lambda i: (tile_of(i), 0)), srow, srow, drow],
        out_shape=[_sds((T, C_WIDTH))] + [_sds((T, C_WIDTH), MXU)] * 4 + [_sds((T, 2 * N_STATE), MXU), _sds((1, N_STATE)), _sds((1, N_STATE)), _sds((1, C_WIDTH))],
        scratch_shapes=[pltpu.VMEM((tt, N_STATE), F32)] * 2 + [pltpu.VMEM((1, N_STATE), F32)] * 2,
        compiler_params=_cp("arbitrary"))(dyc, z, xr, xi, xr, xi, bbd, pwr, pwi, cbd, dsk, w1, w2)


_GROUPS = ((0, A_WIDTH), (A_WIDTH, A_WIDTH + B_WIDTH), (A_WIDTH + B_WIDTH, D_MODEL))


def _merge_fwd(h, ya, yb, yc, g, w):
    T = h.shape[0]
    tm = min(T, 512)

    def body(h_ref, a_ref, b_ref, c_ref, g_ref, w_ref, o_ref):
        yn = jnp.concatenate([y * _rms_stat(y) for y in (a_ref[...], b_ref[...], c_ref[...])], axis=1) * g_ref[...]
        o_ref[...] = h_ref[...] + _mm(yn, w_ref[...])

    def rows(w_):
        return pl.BlockSpec((tm, w_), lambda i: (i, 0))

    return pl.pallas_call(
        body, name="merge_fwd", grid=(T // tm,),
        in_specs=[rows(D_MODEL), rows(A_WIDTH), rows(B_WIDTH), rows(C_WIDTH), pl.BlockSpec((1, D_MODEL), lambda i: (0, 0)),
                  pl.BlockSpec((D_MODEL, D_MODEL), lambda i: (0, 0))],
        out_specs=rows(D_MODEL), out_shape=_sds((T, D_MODEL)), compiler_params=_cp("parallel"))(h, ya, yb, yc, g, w)


def _merge_bwd(gres, ya, yb, yc, g, w):
    T = gres.shape[0]
    tm = min(T, 512)

    def body(gr_ref, a_ref, b_ref, c_ref, g_ref, w_ref, da_ref, db_ref, dc_ref, yn_ref, dg_ref):
        @pl.when(pl.program_id(0) == 0)
        def _():
            dg_ref[...] = jnp.zeros_like(dg_ref)

        dyn = _mm_nt(gr_ref[...], w_ref[...])
        yns, dgs = [], []
        for (c0, c1), y_ref, d_ref in zip(_GROUPS, (a_ref, b_ref, c_ref), (da_ref, db_ref, dc_ref)):
            y = y_ref[...]
            r = _rms_stat(y)
            gg = g_ref[:, c0:c1]
            dx, dg = _rms_bwd(y, r, gg, dyn[:, c0:c1])
            d_ref[...] = dx
            dgs.append(dg)
            yns.append(y * r * gg)
        dg_ref[...] += jnp.concatenate(dgs, axis=1)
        yn_ref[...] = jnp.concatenate(yns, axis=1).astype(MXU)

    def rows(w_):
        return pl.BlockSpec((tm, w_), lambda i: (i, 0))

    row = pl.BlockSpec((1, D_MODEL), lambda i: (0, 0))
    return pl.pallas_call(
        body, name="merge_bwd", grid=(T // tm,),
        in_specs=[rows(D_MODEL), rows(A_WIDTH), rows(B_WIDTH), rows(C_WIDTH), row, pl.BlockSpec((D_MODEL, D_MODEL), lambda i: (0, 0))],
        out_specs=[rows(A_WIDTH), rows(B_WIDTH), rows(C_WIDTH), rows(D_MODEL), row],
        out_shape=[_sds((T, A_WIDTH)), _sds((T, B_WIDTH)), _sds((T, C_WIDTH)), _sds((T, D_MODEL), MXU), _sds((1, D_MODEL))],
        compiler_params=_cp("arbitrary"))(gres, ya, yb, yc, g, w)


FF_BLOCK = D_FF // N_DEV


def _load_weights(w1_hbm, w2_hbm, w1, w2, sem):
    @pl.when(pl.program_id(0) == 0)
    def _():
        copies = [pltpu.make_async_copy(w1_hbm.at[j], w1.at[:, pl.ds(j * FF_BLOCK, FF_BLOCK)], sem.at[j]) for j in range(N_DEV)]
        copies.append(pltpu.make_async_copy(w2_hbm, w2, sem.at[N_DEV]))
        for cp in copies:
            cp.start()
        for cp in copies:
            cp.wait()


def _mlp_weight_scratch():
    return [pltpu.VMEM((D_MODEL, D_FF), MXU), pltpu.VMEM((D_FF, D_MODEL), MXU), pltpu.SemaphoreType.DMA((N_DEV + 1,))]


def _mlp_fwd(h, g, w1, w2):
    T = h.shape[0]
    tm = min(T, 512)

    def body(h_ref, g_ref, w1_hbm, w2_hbm, o_ref, r_ref, w1_v, w2_v, sem):
        _load_weights(w1_hbm, w2_hbm, w1_v, w2_v, sem)
        x = h_ref[...]
        a = jnp.maximum(_mm(x * _rms_stat(x) * g_ref[...], w1_v[...]), 0.0)
        r = (a * a).astype(MXU)
        r_ref[...] = r
        o_ref[...] = x + _mm(r, w2_v[...])

    rows = pl.BlockSpec((tm, D_MODEL), lambda i: (i, 0))
    hbm = pl.BlockSpec(memory_space=pl.ANY)
    return pl.pallas_call(
        body, name="mlp_fwd", grid=(T // tm,),
        in_specs=[rows, pl.BlockSpec((1, D_MODEL), lambda i: (0, 0)), hbm, hbm],
        out_specs=[rows, pl.BlockSpec((tm, D_FF), lambda i: (i, 0))],
        out_shape=[_sds((T, D_MODEL)), _sds((T, D_FF), MXU)],
        scratch_shapes=_mlp_weight_scratch(), compiler_params=_cp("arbitrary"))(h, g, w1, w2)


def _mlp_bwd(gres, h, g, r, w1, w2):
    T = h.shape[0]
    tm = min(T, 256)

    def body(gr_ref, h_ref, g_ref, r_ref, w1_hbm, w2_hbm, dh_ref, hn_ref, da_ref, dg_ref, w1_v, w2_v, sem):
        _load_weights(w1_hbm, w2_hbm, w1_v, w2_v, sem)

        @pl.when(pl.program_id(0) == 0)
        def _():
            dg_ref[...] = jnp.zeros_like(dg_ref)

        gr = gr_ref[...]
        da = (_mm_nt(gr, w2_v[...]) * (2.0 * jnp.sqrt(r_ref[...].astype(F32)))).astype(MXU)
        da_ref[...] = da
        x = h_ref[...]
        rs = _rms_stat(x)
        gg = g_ref[...]
        dx, dg = _rms_bwd(x, rs, gg, _mm_nt(da, w1_v[...]))
        dh_ref[...] = gr + dx
        dg_ref[...] += dg
        hn_ref[...] = (x * rs * gg).astype(MXU)

    rows = pl.BlockSpec((tm, D_MODEL), lambda i: (i, 0))
    row = pl.BlockSpec((1, D_MODEL), lambda i: (0, 0))
    wide = pl.BlockSpec((tm, D_FF), lambda i: (i, 0))
    hbm = pl.BlockSpec(memory_space=pl.ANY)
    return pl.pallas_call(
        body, name="mlp_bwd", grid=(T // tm,),
        in_specs=[rows, rows, row, wide, hbm, hbm], out_specs=[rows, rows, wide, row],
        out_shape=[_sds((T, D_MODEL)), _sds((T, D_MODEL), MXU), _sds((T, D_FF), MXU), _sds((1, D_MODEL))],
        scratch_shapes=_mlp_weight_scratch(), compiler_params=_cp("arbitrary"))(gres, h, g, r, w1, w2)


def _ple_fwd(h, p, l, g, wg, wp):
    T = h.shape[0]
    tm = min(T, 512)

    def body(h_ref, p_ref, g_ref, wg_ref, wp_ref, o_ref):
        x = h_ref[...]
        gate = _sigmoid(_mm(x * _rms_stat(x) * g_ref[...], wg_ref[...]))
        o_ref[...] = x + gate * _mm(p_ref[...], wp_ref[...])

    rows = pl.BlockSpec((tm, D_MODEL), lambda i: (i, 0))
    return pl.pallas_call(
        body, name="ple_fwd", grid=(T // tm,),
        in_specs=[rows, pl.BlockSpec((None, tm, PLE_DIM), lambda i: (l, i, 0)), pl.BlockSpec((1, D_MODEL), lambda i: (0, 0)),
                  pl.BlockSpec((D_MODEL, D_MODEL), lambda i: (0, 0)), pl.BlockSpec((PLE_DIM, D_MODEL), lambda i: (0, 0))],
        out_specs=rows, out_shape=_sds((T, D_MODEL)), compiler_params=_cp("parallel"))(h, p, g, wg, wp)


def _ple_bwd(gres, h, p, l, g, wg, wp):
    T = h.shape[0]
    tm = min(T, 512)

    def body(gr_ref, h_ref, p_ref, g_ref, wg_ref, wp_ref, dh_ref, hn_ref, dgp_ref, de_ref, dg_ref):
        @pl.when(pl.program_id(0) == 0)
        def _():
            dg_ref[...] = jnp.zeros_like(dg_ref)

        x = h_ref[...]
        r = _rms_stat(x)
        gg = g_ref[...]
        hn = x * r * gg
        gate = _sigmoid(_mm(hn, wg_ref[...]))
        e = _mm(p_ref[...], wp_ref[...])
        gr = gr_ref[...]
        dgp = gr * e * gate * (1.0 - gate)
        dx, dg = _rms_bwd(x, r, gg, _mm_nt(dgp, wg_ref[...]))
        dh_ref[...] = gr + dx
        dg_ref[...] += dg
        hn_ref[...] = hn.astype(MXU)
        dgp_ref[...] = dgp.astype(MXU)
        de_ref[...] = (gr * gate).astype(MXU)

    rows = pl.BlockSpec((tm, D_MODEL), lambda i: (i, 0))
    row = pl.BlockSpec((1, D_MODEL), lambda i: (0, 0))
    return pl.pallas_call(
        body, name="ple_bwd", grid=(T // tm,),
        in_specs=[rows, rows, pl.BlockSpec((None, tm, PLE_DIM), lambda i: (l, i, 0)), row,
                  pl.BlockSpec((D_MODEL, D_MODEL), lambda i: (0, 0)), pl.BlockSpec((PLE_DIM, D_MODEL), lambda i: (0, 0))],
        out_specs=[rows, rows, rows, rows, row],
        out_shape=[_sds((T, D_MODEL))] + [_sds((T, D_MODEL), MXU)] * 3 + [_sds((1, D_MODEL))],
        compiler_params=_cp("arbitrary"))(gres, h, p, g, wg, wp)


def _loss_head(h, target):
    T = h.shape[0]
    tm = min(T, 1024)

    def body(h_ref, t_ref, dh_ref, l_ref):
        @pl.when(pl.program_id(0) == 0)
        def _():
            l_ref[...] = jnp.zeros_like(l_ref)

        e = h_ref[...] - t_ref[...]
        dh_ref[...] = e * (1.0 / D_MODEL)
        l_ref[...] += jnp.zeros_like(l_ref) + 0.5 * jnp.sum(jnp.mean(e * e, axis=-1, keepdims=True))

    rows = pl.BlockSpec((tm, D_MODEL), lambda i: (i, 0))
    return pl.pallas_call(
        body, name="loss_head", grid=(T // tm,), in_specs=[rows, rows],
        out_specs=[rows, pl.BlockSpec((8, LANES), lambda i: (0, 0))], out_shape=[_sds((T, D_MODEL)), _sds((8, LANES))],
        compiler_params=_cp("arbitrary"))(h, target)


TN_ROWS = 2048

def _tn(a, b, m, n, *, bm, bn, a_off=0, b_off=0, a_lead=None, n_major=False, split=None, dtype=F32, name="tn"):
    T = a.shape[-2]
    tk = min(T, TN_ROWS)
    nk = T // tk
    b_spec = pl.BlockSpec((tk, bn), lambda i, j, k: (k, b_off + j))
    if a_lead is None:
        a_spec = pl.BlockSpec((tk, bm), lambda i, j, k: (k, a_off + i))
    else:
        a_spec = pl.BlockSpec((None, tk, bm), lambda i, j, k: (a_lead, k, a_off + i))
    assert m % bm == 0 and n % bn == 0 and (not n_major or bm == m) and (split is None or (bn == n and n % split == 0))

    def body(a_ref, b_ref, o_ref, acc):
        k = pl.program_id(2)

        @pl.when(k == 0)
        def _():
            acc[...] = jnp.zeros_like(acc)

        acc[...] += _mm_tn(a_ref[...], b_ref[...])

        @pl.when(k == nk - 1)
        def _():
            if split is None:
                o_ref[...] = acc[...].astype(dtype)
            else:
                for d in range(n // split):
                    o_ref[d] = acc[:, d * split:(d + 1) * split].astype(dtype)

    if split is not None:
        out_spec = pl.BlockSpec((n // split, bm, split), lambda i, j, k: (0, i, 0))
        out_shape = _sds((n // split, m, split), dtype)
    elif n_major:
        out_spec = pl.BlockSpec((None, bm, bn), lambda i, j, k: (j, 0, 0))
        out_shape = _sds((n // bn, m, bn), dtype)
    else:
        out_spec = pl.BlockSpec((bm, bn), lambda i, j, k: (i, j))
        out_shape = _sds((m, n), dtype)
    return pl.pallas_call(
        body, name=name, grid=(m // bm, n // bn, nk),
        in_specs=[a_spec, b_spec],
        out_specs=out_spec, out_shape=out_shape, scratch_shapes=[pltpu.VMEM((bm, bn), F32)],
        compiler_params=_cp("parallel", "parallel", "arbitrary"))(a, b)


def _row_tile(R, C):
    for cand in (512, 256, 128, 64, 32, 16, 8):
        if R % cand == 0 and cand * C * 4 <= 2 ** 20:
            return cand
    return R


def _sum_slots(land):
    S, R, C = land.shape
    tr = _row_tile(R, C)

    def body(l_ref, o_ref):
        acc = l_ref[0].astype(F32)
        for s in range(1, S):
            acc = acc + l_ref[s].astype(F32)
        o_ref[...] = acc

    return pl.pallas_call(
        body, name="sum_slots", grid=(R // tr,), in_specs=[pl.BlockSpec((S, tr, C), lambda i: (0, i, 0))],
        out_specs=pl.BlockSpec((tr, C), lambda i: (i, 0)), out_shape=_sds((R, C)), compiler_params=_cp("parallel"))(land)


def _sum_adamw(land, w, m, v):
    R, C = w.shape
    S = land.shape[0]
    tr = _row_tile(R, C)

    def body(l_ref, w_ref, m_ref, v_ref, g_ref, d_ref, nm_ref, nv_ref):
        gg = l_ref[0].astype(F32)
        for s in range(1, S):
            gg = gg + l_ref[s].astype(F32)
        g_ref[...] = gg
        nm = ADAM_B1 * m_ref[...] + (1.0 - ADAM_B1) * gg
        nv = ADAM_B2 * v_ref[...] + (1.0 - ADAM_B2) * (gg * gg)
        m_hat = nm / (1.0 - ADAM_B1 ** ADAM_STEP)
        v_hat = nv / (1.0 - ADAM_B2 ** ADAM_STEP)
        d_ref[...] = -ADAM_LR * (m_hat / (jnp.sqrt(v_hat) + ADAM_EPS) + ADAM_WD * w_ref[...])
        nm_ref[...] = nm
        nv_ref[...] = nv

    blk = pl.BlockSpec((tr, C), lambda i: (i, 0))
    return pl.pallas_call(
        body, name="sum_adamw", grid=(R // tr,), in_specs=[pl.BlockSpec((S, tr, C), lambda i: (0, i, 0))] + [blk] * 3,
        out_specs=[blk] * 4, out_shape=[_sds((R, C))] * 4, compiler_params=_cp("parallel"))(land, w, m, v)


def _all_to_all(pairs, name):
    n = len(pairs)

    def body(*refs):
        srcs, lands = refs[:n], refs[2 * n:3 * n]
        send, recv, loc = refs[3 * n:]
        x, y, c = lax.axis_index("x"), lax.axis_index("y"), lax.axis_index("c")
        me = 4 * x + 2 * y + c
        own = [pltpu.make_async_copy(pairs[t][2](srcs[t], me, me), pairs[t][3](lands[t], me), loc.at[t]) for t in range(n)]
        for cp in own:
            cp.start()
        sends, recvs = [], []
        for k in range(1, N_DEV):
            px, py, pc = x ^ (k >> 2), y ^ ((k >> 1) & 1), c ^ (k & 1)
            peer = 4 * px + 2 * py + pc
            for t in range(n):
                src = pairs[t][2](srcs[t], me, peer)
                cp = pltpu.make_async_remote_copy(src_ref=src, dst_ref=pairs[t][3](lands[t], me), send_sem=send.at[t, k],
                                                  recv_sem=recv.at[t, k], device_id=(px, py, pc), device_id_type=MESH)
                cp.start()
                sends.append(cp)
                recvs.append(pltpu.make_async_remote_copy(src_ref=src, dst_ref=pairs[t][3](lands[t], peer), send_sem=send.at[t, k],
                                                          recv_sem=recv.at[t, k], device_id=(px, py, pc), device_id_type=MESH))
        for cp in recvs:
            cp.wait_recv()
        for cp in sends:
            cp.wait_send()
        for cp in own:
            cp.wait()

    anyspec = pl.BlockSpec(memory_space=pl.ANY)
    lands = [pr[1] for pr in pairs]
    return pl.pallas_call(
        body, name=name, in_specs=[anyspec] * (2 * n), out_specs=[anyspec] * n,
        out_shape=[_sds(a.shape, a.dtype) for a in lands], input_output_aliases={n + t: t for t in range(n)},
        scratch_shapes=[pltpu.SemaphoreType.DMA((n, N_DEV)), pltpu.SemaphoreType.DMA((n, N_DEV)), pltpu.SemaphoreType.DMA((n,))],
        compiler_params=pltpu.CompilerParams(has_side_effects=True))(*[pr[0] for pr in pairs], *lands)


def _gather_whole(src):
    land = lax.empty((N_DEV,) + src.shape, src.dtype)
    return _all_to_all([(src, land, lambda ref, me, peer: ref, lambda ref, sender: ref.at[sender])], "gather_small_grads")[0]


_HBM = pl.BlockSpec(memory_space=pltpu.HBM)
_SEM = pl.BlockSpec(memory_space=pltpu.SEMAPHORE)
_EFFECT = pltpu.SideEffectType.DATAFLOW_SIDE_EFFECTING


def _peers():
    x, y, c = lax.axis_index("x"), lax.axis_index("y"), lax.axis_index("c")
    out = []
    for k in range(1, N_DEV):
        px, py, pc = x ^ (k >> 2), y ^ ((k >> 1) & 1), c ^ (k & 1)
        out.append((k, (px, py, pc), 4 * px + 2 * py + pc))
    return 4 * x + 2 * y + c, out


def _route(mode, layer):
    if mode == "all":
        return (lambda ref, peer: ref), (lambda ref, sender: ref.at[sender])
    return (lambda ref, peer: ref.at[peer]), (lambda ref, sender: ref.at[sender, layer])


def _split_start(srcs, lands, modes, layer, after, name):
    n = len(srcs)
    routes = [_route(m, layer) for m in modes]

    def body(*refs):
        src, land = refs[:n], refs[n:2 * n]
        send, recv, token = refs[2 * n + 1], refs[2 * n + 2], refs[-1]
        me, peers = _peers()
        for k, dev, peer in peers:
            for t, (src_of, dst_of) in enumerate(routes):
                pltpu.make_async_remote_copy(src_ref=src_of(src[t], peer), dst_ref=dst_of(land[t], me), send_sem=send.at[t * N_DEV + k],
                                             recv_sem=recv.at[t * N_DEV + k], device_id=dev, device_id_type=MESH).start()
        token[...] = jnp.zeros_like(token)

    bufs = list(srcs) + list(lands)
    outs = pl.pallas_call(
        body, name=name,
        out_shape=(pltpu.SemaphoreType.DMA((n * N_DEV,)), pltpu.SemaphoreType.DMA((n * N_DEV,)),
                   *[pltpu.HBM(a.shape, a.dtype) for a in bufs], _sds((8, LANES))),
        in_specs=[_HBM] * (2 * n) + [pl.BlockSpec(memory_space=pl.ANY)],
        out_specs=(_SEM, _SEM, *[_HBM] * (2 * n), pl.BlockSpec(memory_space=pltpu.VMEM)),
        input_output_aliases={i: 2 + i for i in range(2 * n)},
        compiler_params=pltpu.CompilerParams(has_side_effects=_EFFECT),
    )(*[pltpu.with_memory_space_constraint(a, pltpu.HBM) for a in bufs], after)
    return outs[0], outs[1], list(outs[2:2 + n]), list(outs[2 + n:2 + 2 * n]), outs[-1]


def _split_wait(send, recv, srcs, lands, modes, layer, after, name):
    n = len(srcs)
    routes = [_route(m, layer) for m in modes]

    def body(*refs):
        src, land = refs[:n], refs[n:2 * n]
        send_r, recv_r = refs[2 * n], refs[2 * n + 1]
        _, peers = _peers()
        for k, dev, peer in peers:
            for t, (src_of, dst_of) in enumerate(routes):
                cp = pltpu.make_async_remote_copy(src_ref=src_of(src[t], peer), dst_ref=dst_of(land[t], peer), send_sem=send_r.at[t * N_DEV + k],
                                                  recv_sem=recv_r.at[t * N_DEV + k], device_id=dev, device_id_type=MESH)
                cp.wait_send()
                cp.wait_recv()

    bufs = list(srcs) + list(lands)
    outs = pl.pallas_call(
        body, name=name, out_shape=tuple(pltpu.HBM(a.shape, a.dtype) for a in bufs),
        in_specs=[_HBM] * (2 * n) + [_SEM, _SEM, pl.BlockSpec(memory_space=pl.ANY)], out_specs=[_HBM] * (2 * n),
        input_output_aliases={i: i for i in range(2 * n)},
        compiler_params=pltpu.CompilerParams(has_side_effects=_EFFECT),
    )(*bufs, send, recv, after)
    return list(outs[:n]), list(outs[n:])


SHARDED = ("w_in", "glu_w1", "glu_w2", "w_out", "w_ff1", "w_ff2", "w_ple_gate", "w_ple_proj")
SMALL = ("attn_norm_g", "gmlp_ln_g", "gmlp_ln_b", "gmlp_ws", "gmlp_bs", "q_norm_g", "k_norm_g", "sinks", "ssm_a_re", "ssm_a_im",
         "ssm_log_dt", "ssm_b_re", "ssm_b_im", "ssm_c_re", "ssm_c_im", "ssm_d", "mix_out_g", "mlp_norm_g", "ple_norm_g")
WEIGHTS = ("attn_norm_g", "w_in", "gmlp_ln_g", "gmlp_ln_b", "gmlp_ws", "gmlp_bs", "q_norm_g", "k_norm_g", "sinks", "ssm_a_re", "ssm_a_im",
           "ssm_log_dt", "ssm_b_re", "ssm_b_im", "ssm_c_re", "ssm_c_im", "ssm_d", "glu_w1", "glu_w2", "mix_out_g", "w_out", "mlp_norm_g",
           "w_ff1", "w_ff2", "ple_norm_g", "w_ple_gate", "w_ple_proj")
FLAT_COLS = 1024


PACK_TILE_ROWS = 8
PACK_ROWS_MULTIPLE = PACK_TILE_ROWS * N_DEV


def _packed_rows(shape):
    return -(-math.prod(shape) // (PACK_TILE_ROWS * FLAT_COLS)) * PACK_TILE_ROWS


def _pack(arrs, dtype):
    blocks = []
    for a in arrs:
        flat = a.astype(dtype).reshape(-1)
        pad = _packed_rows(a.shape) * FLAT_COLS - flat.shape[0]
        if pad:
            flat = jnp.concatenate([flat, jnp.zeros((pad,), dtype)])
        blocks.append(flat.reshape(-1, FLAT_COLS))
    rows = sum(b.shape[0] for b in blocks)
    if rows % PACK_ROWS_MULTIPLE:
        blocks.append(jnp.zeros((PACK_ROWS_MULTIPLE - rows % PACK_ROWS_MULTIPLE, FLAT_COLS), dtype))
    return jnp.concatenate(blocks, axis=0)


def _unpack(flat, shapes):
    out, r = [], 0
    for s in shapes:
        nr = _packed_rows(s)
        out.append(flat[r:r + nr].reshape(-1)[:math.prod(s)].reshape(s))
        r += nr
    return out


def _from_col_major(s):
    n, rows, cs = s.shape
    return s.transpose(1, 0, 2).reshape(rows, n * cs)


EARLY = ("w_in", "glu_w1", "glu_w2")
LATE = ("w_out", "w_ff1", "w_ff2", "w_ple_gate", "w_ple_proj")
GRADS_MID = ("w_ff1", "w_ff2", "w_ple_gate", "w_ple_proj")
GRADS_END = ("w_in", "glu_w1", "glu_w2", "w_out")


def _layer_fwd(h, p, l, cs, sn, W, late_weights, S, sp):
    z = _inproj_fwd(h, S["attn_norm_g"], W["w_in"])
    ya = _gmlp_fwd(z, S["lng"], S["lnb"], S["gmlp_ws"], S["bsx"])
    yb = _attn_fwd(z, cs, sn, S["qg"], S["kg"], S["sinks"])
    yc, xr, xi = _ssm_fwd(z, sp["bbd"], sp["pwr"], sp["pwi"], sp["cbd"], S["ssm_d"], W["glu_w1"], W["glu_w2"])
    late, token = late_weights(ya[0:8, 0:LANES] + yb[0:8, 0:LANES] + yc[0:8, 0:LANES])
    W = {**W, **late}
    h1 = _merge_fwd(h, ya, yb, yc, _behind(S["mix_out_g"], token), W["w_out"])
    h2, r = _mlp_fwd(h1, S["mlp_norm_g"], W["w_ff1"], W["w_ff2"])
    h3 = _ple_fwd(h2, p, l, S["ple_norm_g"], W["w_ple_gate"], W["w_ple_proj"])
    return h3, dict(h=h, z=z, ya=ya, yb=yb, yc=yc, xr=xr, xi=xi, h1=h1, r=r, h2=h2), W


def _layer_bwd(g3, p, l, cs, sn, W, S, sp, A, raw, mid_bwd):
    G = {}
    g2, hn3, dgp, de, G["ple_norm_g"] = _ple_bwd(g3, A["h2"], p, l, S["ple_norm_g"], W["w_ple_gate"], W["w_ple_proj"])
    G["w_ple_gate"] = _tn(hn3, dgp, D_MODEL, D_MODEL, bm=1024, bn=1024, dtype=WIRE, name="tn_gate").reshape(N_DEV, -1, D_MODEL)
    G["w_ple_proj"] = _tn(p, de, PLE_DIM, D_MODEL, bm=PLE_DIM, bn=D_MODEL, a_lead=l, split=D_MODEL // N_DEV, dtype=WIRE, name="tn_proj")
    g1, hn, da, G["mlp_norm_g"] = _mlp_bwd(g2, A["h1"], S["mlp_norm_g"], A["r"], W["w_ff1"], W["w_ff2"])
    G["w_ff1"] = _tn(hn, da, D_MODEL, D_FF, bm=D_MODEL, bn=FF_BLOCK, n_major=True, dtype=WIRE, name="tn_ff1")
    G["w_ff2"] = _tn(A["r"], g2, D_FF, D_MODEL, bm=1024, bn=1024, dtype=WIRE, name="tn_ff2").reshape(N_DEV, -1, D_MODEL)
    token = mid_bwd(g1, G)
    dya, dyb, dyc, yn, G["mix_out_g"] = _merge_bwd(g1, A["ya"], A["yb"], A["yc"], _behind(S["mix_out_g"], token), W["w_out"])
    G["w_out"] = _tn(yn, g1, D_MODEL, D_MODEL, bm=1024, bn=1024, dtype=WIRE, name="tn_out").reshape(N_DEV, -1, D_MODEL)
    dzc, y2, da1, da2, dy, arai, dlr, dli, dd = _ssm_bwd(dyc, A["z"], A["xr"], A["xi"], sp["bbd"], sp["pwr"], sp["pwi"], sp["cbd"],
                                                        S["ssm_d"], W["glu_w1"], W["glu_w2"])
    G["glu_w1"] = _tn(y2, da1, C_WIDTH, C_WIDTH, bm=256, bn=256, dtype=WIRE, name="tn_glu1").reshape(N_DEV, -1, C_WIDTH)
    G["glu_w2"] = _tn(y2, da2, C_WIDTH, C_WIDTH, bm=256, bn=256, dtype=WIRE, name="tn_glu2").reshape(N_DEV, -1, C_WIDTH)
    dbbd = _tn(A["z"], arai, C_WIDTH, 2 * N_STATE, bm=256, bn=1024, a_off=5, name="tn_bbd")
    dcr = _tn(dy, A["xr"], C_WIDTH, N_STATE, bm=256, bn=1024, name="tn_cre")
    dci = _tn(dy, A["xi"], C_WIDTH, N_STATE, bm=256, bn=1024, name="tn_cim")
    dare, daim, dldt, dbtr, dbti, dcre, dcim = _ssm_param_bwd(
        raw["are"], raw["aim"], raw["ldt"], raw["are_x"], raw["aim_x"], raw["ldt_x"], raw["btr"], raw["bti"],
        dlr.reshape(C_GROUPS, C_STATE), dli.reshape(C_GROUPS, C_STATE), dbbd, dcr, dci)
    G["ssm_a_re"], G["ssm_a_im"], G["ssm_log_dt"] = dare, daim, dldt[:, 0]
    G["ssm_b_re"] = dbtr.reshape(C_GROUPS, C_GROUP, C_STATE).transpose(0, 2, 1)
    G["ssm_b_im"] = dbti.reshape(C_GROUPS, C_GROUP, C_STATE).transpose(0, 2, 1)
    G["ssm_c_re"] = dcre.reshape(C_GROUPS, C_GROUP, C_STATE)
    G["ssm_c_im"] = dcim.reshape(C_GROUPS, C_GROUP, C_STATE)
    G["ssm_d"] = dd.reshape(C_GROUPS, C_GROUP)
    dzq, dzk, dzv, dqg, dkg, dsk = _attn_bwd(A["z"], cs, sn, S["qg"], S["kg"], S["sinks"], A["yb"], dyb)
    G["q_norm_g"] = dqg[0, :HEAD_DIM] + dqg[0, HEAD_DIM:]
    G["k_norm_g"] = dkg[0, :HEAD_DIM] + dkg[0, HEAD_DIM:]
    G["sinks"] = dsk[:, 0]
    dza, dws, dbs, dlng, dlnb = _gmlp_bwd(A["z"], dya, S["lng"], S["lnb"], S["gmlp_ws"], S["bsx"])
    G["gmlp_ws"] = dws
    G["gmlp_bs"] = dbs[:, :, 0]
    G["gmlp_ln_g"] = dlng.reshape(A_HEADS, 2, HEAD_DIM)[:, 1]
    G["gmlp_ln_b"] = dlnb.reshape(A_HEADS, 2, HEAD_DIM)[:, 1]
    g0, xn, dz, G["attn_norm_g"] = _inproj_bwd(g1, A["h"], S["attn_norm_g"], W["w_in"], dza, dzq, dzk, dzv, dzc)
    G["w_in"] = _tn(xn, dz, D_MODEL, IN_COLS, bm=D_MODEL, bn=IN_COLS, split=IN_COLS // N_DEV, dtype=WIRE, name="tn_in")
    return g0, G


def _small_layouts(P, l):
    def row(a):
        return a.reshape(1, -1)

    zeros = jnp.zeros((A_HEADS, HEAD_DIM), F32)
    S = dict(
        attn_norm_g=row(P["attn_norm_g"][l]), mix_out_g=row(P["mix_out_g"][l]), mlp_norm_g=row(P["mlp_norm_g"][l]),
        ple_norm_g=row(P["ple_norm_g"][l]),
        lng=jnp.stack([zeros, P["gmlp_ln_g"][l]], axis=1).reshape(1, 512),
        lnb=jnp.stack([zeros, P["gmlp_ln_b"][l]], axis=1).reshape(1, 512),
        gmlp_ws=P["gmlp_ws"][l],
        bsx=jnp.broadcast_to(P["gmlp_bs"][l][:, :, None], (A_HEADS, CHUNK, CHUNK)),
        qg=jnp.tile(P["q_norm_g"][l], 2).reshape(1, LANES), kg=jnp.tile(P["k_norm_g"][l], 2).reshape(1, LANES),
        sinks=jnp.broadcast_to(P["sinks"][l][:, None], (8, LANES)),
        ssm_d=row(P["ssm_d"][l]),
    )
    are, aim = P["ssm_a_re"][l], P["ssm_a_im"][l]
    ldt = jnp.broadcast_to(P["ssm_log_dt"][l][:, None], (C_GROUPS, C_STATE))
    raw = dict(
        are=are, aim=aim, ldt=ldt,
        are_x=jnp.repeat(are, C_GROUP, axis=0), aim_x=jnp.repeat(aim, C_GROUP, axis=0), ldt_x=jnp.repeat(ldt, C_GROUP, axis=0),
        btr=P["ssm_b_re"][l].transpose(0, 2, 1).reshape(C_WIDTH, C_STATE), bti=P["ssm_b_im"][l].transpose(0, 2, 1).reshape(C_WIDTH, C_STATE),
        cre=P["ssm_c_re"][l].reshape(C_WIDTH, C_STATE), cim=P["ssm_c_im"][l].reshape(C_WIDTH, C_STATE),
    )
    return S, raw


def _ssm_prep_layer(raw):
    bbd, cbd, pwr, pwi = _ssm_prep(raw["are"].reshape(1, N_STATE), raw["aim"].reshape(1, N_STATE), raw["ldt"].reshape(1, N_STATE),
                                   raw["are_x"], raw["aim_x"], raw["ldt_x"], raw["btr"], raw["bti"], raw["cre"], raw["cim"])
    return dict(bbd=bbd, cbd=cbd, pwr=pwr, pwi=pwi)


def _behind(row, token):
    return row if token is None else row + token[0:1, 0:1]


def _local_step(x, p, positions, target, P, weights_of, mid_bwd, after_bwd):
    inv = 1.0 / (ROPE_THETA ** (jnp.arange(0, HEAD_DIM, 2, dtype=F32) / HEAD_DIM))
    cs, sn = _rope_tables(positions.reshape(-1, 1), jnp.tile(inv, 4).reshape(1, LANES))
    h = x
    acts, smalls, weights = [], [], []
    for l in range(DEPTH):
        W, late_weights, token = weights_of(l, h)
        S, raw = _small_layouts(P, l)
        sp = _ssm_prep_layer(raw)
        h, A, W = _layer_fwd(h, p, l, cs, sn, W, late_weights, {**S, "attn_norm_g": _behind(S["attn_norm_g"], token)}, sp)
        acts.append(A)
        smalls.append((S, raw, sp))
        weights.append(W)
    g, lsum = _loss_head(h, target)
    grads = [None] * DEPTH
    token = None
    for l in reversed(range(DEPTH)):
        S, raw, sp = smalls[l]
        g, grads[l] = _layer_bwd(g, p, l, cs, sn, weights[l], {**S, "ple_norm_g": _behind(S["ple_norm_g"], token)}, sp, acts[l], raw,
                                 functools.partial(mid_bwd, l))
        token = after_bwd(l, g, grads[l])
    return lsum[0, 0], g, grads


def _layer_weights(g):
    layout = dict(
        w_in=_from_col_major, glu_w1=lambda a: a.reshape(C_WIDTH, C_WIDTH), glu_w2=lambda a: a.reshape(C_WIDTH, C_WIDTH),
        w_out=lambda a: a.reshape(D_MODEL, D_MODEL), w_ff1=lambda a: a, w_ff2=lambda a: a.reshape(D_FF, D_MODEL),
        w_ple_gate=lambda a: a.reshape(D_MODEL, D_MODEL), w_ple_proj=_from_col_major)
    return {n: layout[n](a) for n, a in g.items()}


def kernel(x, p, positions, attn_norm_g, w_in, gmlp_ln_g, gmlp_ln_b, gmlp_ws, gmlp_bs, q_norm_g, k_norm_g, sinks, ssm_a_re, ssm_a_im, ssm_log_dt, ssm_b_re, ssm_b_im, ssm_c_re, ssm_c_im, ssm_d, glu_w1, glu_w2, mix_out_g, w_out, mlp_norm_g, w_ff1, w_ff2, ple_norm_g, w_ple_gate, w_ple_proj, loss_target, m_attn_norm_g, m_w_in, m_gmlp_ln_g, m_gmlp_ln_b, m_gmlp_ws, m_gmlp_bs, m_q_norm_g, m_k_norm_g, m_sinks, m_ssm_a_re, m_ssm_a_im, m_ssm_log_dt, m_ssm_b_re, m_ssm_b_im, m_ssm_c_re, m_ssm_c_im, m_ssm_d, m_glu_w1, m_glu_w2, m_mix_out_g, m_w_out, m_mlp_norm_g, m_w_ff1, m_w_ff2, m_ple_norm_g, m_w_ple_gate, m_w_ple_proj, v_attn_norm_g, v_w_in, v_gmlp_ln_g, v_gmlp_ln_b, v_gmlp_ws, v_gmlp_bs, v_q_norm_g, v_k_norm_g, v_sinks, v_ssm_a_re, v_ssm_a_im, v_ssm_log_dt, v_ssm_b_re, v_ssm_b_im, v_ssm_c_re, v_ssm_c_im, v_ssm_d, v_glu_w1, v_glu_w2, v_mix_out_g, v_w_out, v_mlp_norm_g, v_w_ff1, v_w_ff2, v_ple_norm_g, v_w_ple_gate, v_w_ple_proj):
    env = dict(locals())
    P = {n: env[n] for n in WEIGHTS}
    M = {n: env["m_" + n] for n in WEIGHTS}
    V = {n: env["v_" + n] for n in WEIGHTS}
    return _step(x, p, positions, loss_target, P, M, V)


def _step(x, p, positions, loss_target, P, M, V):
    small_shapes = [P[n].shape for n in SMALL]
    me = 4 * lax.axis_index("x") + 2 * lax.axis_index("y") + lax.axis_index("c")
    nothing = jnp.zeros((8, LANES), F32)

    def put(land, own, lead):
        return lax.dynamic_update_slice(land, own.reshape((1,) * len(lead) + own.shape), tuple(lead) + (0,) * own.ndim)

    def gather_start(l, names, after, tag):
        shards = [P[n][l].astype(WIRE) for n in names]
        lands = [lax.empty((N_DEV,) + s.shape, WIRE) for s in shards]
        send, recv, shards, lands, token = _split_start(shards, lands, ["all"] * len(names), 0, after, f"gather_start_{l}{tag}")
        return dict(names=names, send=send, recv=recv, shards=shards, lands=lands, token=token, name=f"gather_wait_{l}{tag}")

    def gather_wait(f, after):
        shards, lands = _split_wait(f["send"], f["recv"], f["shards"], f["lands"], ["all"] * len(f["names"]), 0, after, f["name"])
        return dict(zip(f["names"], [put(ld, sh, (me,)) for sh, ld in zip(shards, lands)]))

    first = gather_start(0, EARLY, nothing, "a")
    flying = {0: (first, gather_start(0, LATE, first["token"], "b"))}

    def weights_of(l, h):
        fa, fb = flying.pop(l)
        got = gather_wait(fa, h)
        if fb is None:
            token = None
            if l + 1 < DEPTH:
                flying[l + 1] = (gather_start(l + 1, SHARDED, got["w_in"], ""), None)
                token = flying[l + 1][0]["token"]
            W = _layer_weights(got)
            return {n: W[n] for n in EARLY}, (lambda after: ({n: W[n] for n in LATE}, None)), token

        def late_weights(after):
            late = gather_wait(fb, after)
            flying[l + 1] = (gather_start(l + 1, SHARDED, late["w_out"], ""), None)
            return _layer_weights(late), flying[l + 1][0]["token"]

        return _layer_weights(got), late_weights, fb["token"]

    grad_lands = {n: lax.empty((N_DEV,) + P[n].shape, WIRE) for n in SHARDED}
    sent = []

    def scatter_start(l, names, parts, lands, tag):
        send, recv, parts, lands, token = _split_start(parts, lands, ["own"] * len(parts), l, nothing, f"scatter_start_{l}{tag}")
        sent.append(dict(l=l, names=names, send=send, recv=recv, parts=parts, lands=lands, name=f"scatter_wait_{l}{tag}"))
        return token

    def scatter_wait(after):
        f = sent.pop(0)
        parts, lands = _split_wait(f["send"], f["recv"], f["parts"], f["lands"], ["own"] * len(f["parts"]), f["l"], after, f["name"])
        lands = [put(ld, lax.dynamic_index_in_dim(part, me, 0, keepdims=False), (me, f["l"])) for part, ld in zip(parts, lands)]
        return dict(zip(f["names"], lands))

    def mid_bwd(l, g1, G):
        if l > 0:
            return None
        grad_lands.update(scatter_wait(g1))
        return scatter_start(0, GRADS_MID, [G[n] for n in GRADS_MID], [grad_lands[n] for n in GRADS_MID], "a")

    small_land = []

    def after_bwd(l, g, G):
        if l > 0:
            if sent:
                grad_lands.update(scatter_wait(g))
            return scatter_start(l, SHARDED, [G[n] for n in SHARDED], [grad_lands[n] for n in SHARDED], "")
        sflat = _pack([jnp.stack([grads_of[k][n] for k in range(DEPTH)]) for n in SMALL], F32)
        sparts = sflat.reshape(N_DEV, -1, FLAT_COLS)
        small_land.append(sflat.shape)
        return scatter_start(0, GRADS_END + ("small",), [G[n] for n in GRADS_END] + [sparts],
                             [grad_lands[n] for n in GRADS_END] + [lax.empty((N_DEV, 1) + sparts.shape[1:], F32)], "b")

    grads_of = {}

    def after_bwd_recording(l, g, G):
        grads_of[l] = G
        grads_of["token"] = after_bwd(l, g, G)
        return grads_of["token"]

    lsum, gx, grads = _local_step(x[0], p[:, 0], positions[0], loss_target[0], P, weights_of, mid_bwd, after_bwd_recording)
    G, delta, new_m, new_v = {}, {}, {}, {}

    def update(names):
        for n in names:
            shp = P[n].shape
            res = _sum_adamw(grad_lands[n].reshape(N_DEV, -1, shp[-1]), *(a.reshape(-1, shp[-1]) for a in (P[n], M[n], V[n])))
            G[n], delta[n], new_m[n], new_v[n] = (a.reshape(shp) for a in res)

    grad_lands.update(scatter_wait(grads_of["token"]))
    update(GRADS_MID)
    last = scatter_wait(new_v[GRADS_MID[-1]])
    small_parts = last.pop("small")
    grad_lands.update(last)
    update(GRADS_END)
    small_sum = _gather_whole(_sum_slots(small_parts[:, 0])).reshape((1,) + small_land[0])
    res = _sum_adamw(small_sum, _pack([P[n] for n in SMALL], F32), _pack([M[n] for n in SMALL], F32), _pack([V[n] for n in SMALL], F32))
    for dst, flat in zip((G, delta, new_m, new_v), res):
        dst.update(zip(SMALL, _unpack(flat, small_shapes)))
    loss = lax.psum(lsum, ("x", "y", "c"))
    return (loss, gx[None], *[G[n] for n in WEIGHTS], *[delta[n] for n in WEIGHTS], *[new_m[n] for n in WEIGHTS], *[new_v[n] for n in WEIGHTS])
```

```python
import functools
import math

import jax
import jax.numpy as jnp
from jax import lax
from jax.experimental import pallas as pl
from jax.experimental.pallas import tpu as pltpu

F32 = jnp.float32
MXU = jnp.bfloat16
WIRE = jnp.bfloat16

D_MODEL = 1024
DEPTH = 4
HEAD_DIM = 64
A_WIDTH = 256
A_HEADS = 4
CHUNK = 128
B_WIDTH = 512
WINDOW = 128
C_WIDTH = 256
C_GROUP = 16
C_GROUPS = 16
C_STATE = 64
N_STATE = C_GROUPS * C_STATE
IN_COLS = 1536
D_FF = 4096
PLE_DIM = 256
EPS = 1e-6
ROPE_THETA = 10000.0
SCALE = HEAD_DIM ** -0.5
NEG = -1e30
N_DEV = 8

ADAM_LR = 0.001
ADAM_B1 = 0.9
ADAM_B2 = 0.999
ADAM_EPS = 1e-08
ADAM_WD = 0.01
ADAM_STEP = 10

V7X_VMEM_BYTES = 64 * 2 ** 20
VMEM_LIMIT = V7X_VMEM_BYTES - 8 * 2 ** 20
LANES = 128

MESH = pl.DeviceIdType.MESH


def _cp(*sem):
    return pltpu.CompilerParams(dimension_semantics=sem, vmem_limit_bytes=VMEM_LIMIT)


def _sds(shape, dtype=F32):
    return jax.ShapeDtypeStruct(shape, dtype)


def _mm(a, b):
    return jnp.dot(a.astype(MXU), b.astype(MXU), preferred_element_type=F32)


def _mm_nt(a, b):
    return lax.dot_general(a.astype(MXU), b.astype(MXU), (((1,), (1,)), ((), ())), preferred_element_type=F32)


def _mm_tn(a, b):
    return lax.dot_general(a.astype(MXU), b.astype(MXU), (((0,), (0,)), ((), ())), preferred_element_type=F32)


def _lane(shape):
    return lax.broadcasted_iota(jnp.int32, shape, len(shape) - 1)


def _row(shape):
    return lax.broadcasted_iota(jnp.int32, shape, 0)


_GELU_C = math.sqrt(2.0 / math.pi)


def _gelu(x):
    return 0.5 * x * (1.0 + jnp.tanh(_GELU_C * (x + 0.044715 * (x * x * x))))


def _gelu_grad(x):
    t = jnp.tanh(_GELU_C * (x + 0.044715 * (x * x * x)))
    return 0.5 * (1.0 + t) + 0.5 * x * (1.0 - t * t) * (_GELU_C * (1.0 + 3.0 * 0.044715 * (x * x)))


def _sigmoid(x):
    return 1.0 / (1.0 + jnp.exp(-x))


def _rms_stat(x):
    return lax.rsqrt(jnp.mean(x * x, axis=-1, keepdims=True) + EPS)


def _rms_bwd(x, r, g, dy):
    xh = x * r
    dxh = dy * g
    dx = r * (dxh - xh * jnp.mean(dxh * xh, axis=-1, keepdims=True))
    return dx, jnp.sum(dy * xh, axis=0, keepdims=True)


def _tril(w):
    return jnp.where(_row(w.shape) >= _lane(w.shape), w, 0.0)


def _swap64(x):
    return pltpu.roll(x, HEAD_DIM, 1)


def _group_sum64(x, lo):
    s_lo = jnp.sum(jnp.where(lo, x, 0.0), axis=-1, keepdims=True)
    s_hi = jnp.sum(jnp.where(lo, 0.0, x), axis=-1, keepdims=True)
    return jnp.where(lo, s_lo, s_hi)


def _partner(x):
    n = x.shape[-1]
    first = (_lane(x.shape) % HEAD_DIM) < HEAD_DIM // 2
    return jnp.where(first, pltpu.roll(x, n - HEAD_DIM // 2, 1), pltpu.roll(x, HEAD_DIM // 2, 1))


def _rope(y, cs, sn):
    return y * cs + _partner(y) * sn


def _rope_bwd(d, cs, sn):
    return d * cs + _partner(d * sn)


def _qk_norm_rope(x, g, cs, sn):
    lo = _lane(x.shape) < HEAD_DIM
    r = lax.rsqrt(_group_sum64(x * x, lo) * (1.0 / HEAD_DIM) + EPS)
    xh = x * r
    return _rope(xh * g, cs, sn), xh, r


def _qk_norm_rope_bwd(xh, r, g, cs, sn, d):
    lo = _lane(xh.shape) < HEAD_DIM
    dy = _rope_bwd(d, cs, sn)
    dxh = dy * g
    m = _group_sum64(dxh * xh, lo) * (1.0 / HEAD_DIM)
    return r * (dxh - xh * m), jnp.sum(dy * xh, axis=0, keepdims=True)


def _gmlp_head(blk, g, b):
    hi = _lane(blk.shape) >= HEAD_DIM
    mu = jnp.sum(jnp.where(hi, blk, 0.0), axis=-1, keepdims=True) * (1.0 / HEAD_DIM)
    xc = jnp.where(hi, blk - mu, 0.0)
    rstd = lax.rsqrt(jnp.sum(xc * xc, axis=-1, keepdims=True) * (1.0 / HEAD_DIM) + EPS)
    vhat = xc * rstd
    return vhat * g + b, vhat, rstd


def _rope_tables(pos_col, inv_row):
    T = pos_col.shape[0]
    tm = min(T, 1024)

    def body(p_ref, inv_ref, cs_ref, sn_ref):
        ang = p_ref[...].astype(F32) * inv_ref[...]
        s = jnp.sin(ang)
        cs_ref[...] = jnp.cos(ang)
        sn_ref[...] = jnp.where((_lane(ang.shape) % HEAD_DIM) < HEAD_DIM // 2, -s, s)

    blk = pl.BlockSpec((tm, LANES), lambda i: (i, 0))
    return pl.pallas_call(
        body, name="rope_tables", grid=(T // tm,),
        in_specs=[pl.BlockSpec((tm, 1), lambda i: (i, 0)), pl.BlockSpec((1, LANES), lambda i: (0, 0))],
        out_specs=[blk, blk], out_shape=[_sds((T, LANES))] * 2, compiler_params=_cp("parallel"))(pos_col, inv_row)


def _inproj_fwd(h, g, w):
    T = h.shape[0]
    tm = min(T, 512)

    def body(h_ref, g_ref, w_ref, z_ref):
        x = h_ref[...]
        z_ref[...] = _mm(x * _rms_stat(x) * g_ref[...], w_ref[...])

    return pl.pallas_call(
        body, name="inproj_fwd", grid=(T // tm,),
        in_specs=[pl.BlockSpec((tm, D_MODEL), lambda i: (i, 0)), pl.BlockSpec((1, D_MODEL), lambda i: (0, 0)),
                  pl.BlockSpec((D_MODEL, IN_COLS), lambda i: (0, 0))],
        out_specs=pl.BlockSpec((tm, IN_COLS), lambda i: (i, 0)), out_shape=_sds((T, IN_COLS)),
        compiler_params=_cp("parallel"))(h, g, w)


def _inproj_bwd(gres, h, g, w, dza, dzq, dzk, dzv, dzc):
    T = h.shape[0]
    tm = min(T, 512)

    def body(gr_ref, h_ref, g_ref, w_ref, a_ref, q_ref, k_ref, v_ref, c_ref, dh_ref, xn_ref, dz_ref, dg_ref):
        @pl.when(pl.program_id(0) == 0)
        def _():
            dg_ref[...] = jnp.zeros_like(dg_ref)

        x = h_ref[...]
        r = _rms_stat(x)
        gg = g_ref[...]
        dz = jnp.concatenate([a_ref[...], q_ref[...], k_ref[...], v_ref[...], c_ref[...]], axis=1)
        dxn = _mm_nt(dz, w_ref[...])
        dx, dg = _rms_bwd(x, r, gg, dxn)
        dh_ref[...] = gr_ref[...] + dx
        dg_ref[...] += dg
        xn_ref[...] = (x * r * gg).astype(MXU)
        dz_ref[...] = dz.astype(MXU)

    def rows(w_):
        return pl.BlockSpec((tm, w_), lambda i: (i, 0))

    row = pl.BlockSpec((1, D_MODEL), lambda i: (0, 0))
    return pl.pallas_call(
        body, name="inproj_bwd", grid=(T // tm,),
        in_specs=[rows(D_MODEL), rows(D_MODEL), row, pl.BlockSpec((D_MODEL, IN_COLS), lambda i: (0, 0)),
                  rows(512), rows(512), rows(128), rows(128), rows(256)],
        out_specs=[rows(D_MODEL), rows(D_MODEL), rows(IN_COLS), row],
        out_shape=[_sds((T, D_MODEL)), _sds((T, D_MODEL), MXU), _sds((T, IN_COLS), MXU), _sds((1, D_MODEL))],
        compiler_params=_cp("arbitrary"))(gres, h, g, w, dza, dzq, dzk, dzv, dzc)


def _gmlp_fwd(z, lng, lnb, ws, bsx):
    T = z.shape[0]
    tm = min(T, 512)
    nc = tm // CHUNK

    def body(z_ref, g_ref, b_ref, w_ref, bs_ref, ya_ref):
        zg = _gelu(z_ref[...])
        lo = _lane((tm, LANES)) < HEAD_DIM
        prods = []
        for hd in range(A_HEADS):
            sl = slice(hd * LANES, (hd + 1) * LANES)
            blk = zg[:, sl]
            vn, _, _ = _gmlp_head(blk, g_ref[:, sl], b_ref[:, sl])
            wm = _tril(w_ref[hd])
            sv = jnp.concatenate([_mm(wm, vn[c * CHUNK:(c + 1) * CHUNK]) + bs_ref[hd] for c in range(nc)], axis=0)
            prods.append(blk * _swap64(sv))
        ya_ref[:, 0:LANES] = jnp.where(lo, prods[0], _swap64(prods[1]))
        ya_ref[:, LANES:2 * LANES] = jnp.where(lo, prods[2], _swap64(prods[3]))

    row = pl.BlockSpec((1, 512), lambda i: (0, 0))
    mat = pl.BlockSpec((A_HEADS, CHUNK, CHUNK), lambda i: (0, 0, 0))
    return pl.pallas_call(
        body, name="gmlp_fwd", grid=(T // tm,),
        in_specs=[pl.BlockSpec((tm, 512), lambda i: (i, 0)), row, row, mat, mat],
        out_specs=pl.BlockSpec((tm, A_WIDTH), lambda i: (i, 0)), out_shape=_sds((T, A_WIDTH)),
        compiler_params=_cp("parallel"))(z, lng, lnb, ws, bsx)


def _gmlp_bwd(z, dya, lng, lnb, ws, bsx):
    T = z.shape[0]
    tm = min(T, 512)
    nc = tm // CHUNK

    def body(z_ref, dya_ref, g_ref, b_ref, w_ref, bs_ref, dza_ref, dw_ref, dbs_ref, dg_ref, db_ref):
        @pl.when(pl.program_id(0) == 0)
        def _():
            dw_ref[...] = jnp.zeros_like(dw_ref)
            dbs_ref[...] = jnp.zeros_like(dbs_ref)
            dg_ref[...] = jnp.zeros_like(dg_ref)
            db_ref[...] = jnp.zeros_like(db_ref)

        za = z_ref[...]
        zg = _gelu(za)
        gp = _gelu_grad(za)
        lo = _lane((tm, LANES)) < HEAD_DIM
        for hd in range(A_HEADS):
            sl = slice(hd * LANES, (hd + 1) * LANES)
            blk = zg[:, sl]
            g = g_ref[:, sl]
            vn, vhat, rstd = _gmlp_head(blk, g, b_ref[:, sl])
            wm = _tril(w_ref[hd])
            pair = dya_ref[:, (hd // 2) * LANES:(hd // 2 + 1) * LANES]
            dy = jnp.where(lo, pair if hd % 2 == 0 else _swap64(pair), 0.0)
            dsv = _swap64(dy * blk)
            svs, dvns = [], []
            dw = jnp.zeros((CHUNK, CHUNK), F32)
            dbs = jnp.zeros((CHUNK, 1), F32)
            for c in range(nc):
                cs = slice(c * CHUNK, (c + 1) * CHUNK)
                svs.append(_mm(wm, vn[cs]) + bs_ref[hd])
                dw = dw + _mm_nt(dsv[cs], vn[cs])
                dbs = dbs + jnp.sum(dsv[cs], axis=-1, keepdims=True)
                dvns.append(_mm_tn(wm, dsv[cs]))
            sv = jnp.concatenate(svs, axis=0)
            dvn = jnp.concatenate(dvns, axis=0)
            dw_ref[hd] += _tril(dw)
            dbs_ref[hd] += jnp.broadcast_to(dbs, (CHUNK, CHUNK))
            dg_ref[:, sl] += jnp.sum(dvn * vhat, axis=0, keepdims=True)
            db_ref[:, sl] += jnp.sum(dvn, axis=0, keepdims=True)
            du = dy * _swap64(sv)
            dvh = dvn * g
            m1 = jnp.sum(dvh, axis=-1, keepdims=True) * (1.0 / HEAD_DIM)
            m2 = jnp.sum(dvh * vhat, axis=-1, keepdims=True) * (1.0 / HEAD_DIM)
            dv = jnp.where(lo, 0.0, rstd * (dvh - m1 - vhat * m2))
            dza_ref[:, sl] = (du + dv) * gp[:, sl]

    row = pl.BlockSpec((1, 512), lambda i: (0, 0))
    mat = pl.BlockSpec((A_HEADS, CHUNK, CHUNK), lambda i: (0, 0, 0))
    return pl.pallas_call(
        body, name="gmlp_bwd", grid=(T // tm,),
        in_specs=[pl.BlockSpec((tm, 512), lambda i: (i, 0)), pl.BlockSpec((tm, A_WIDTH), lambda i: (i, 0)), row, row, mat, mat],
        out_specs=[pl.BlockSpec((tm, 512), lambda i: (i, 0)), mat, mat, row, row],
        out_shape=[_sds((T, 512)), _sds((A_HEADS, CHUNK, CHUNK)), _sds((A_HEADS, CHUNK, CHUNK)), _sds((1, 512)), _sds((1, 512))],
        compiler_params=_cp("arbitrary"))(z, dya, lng, lnb, ws, bsx)


def _attn_specs(T, tq, tile_of):
    nb = tq // WINDOW

    def prev(i):
        return jnp.maximum(tile_of(i) * nb - 1, 0)

    row = pl.BlockSpec((1, LANES), lambda i: (0, 0))
    return [
        pl.BlockSpec((tq, B_WIDTH), lambda i: (tile_of(i), 1)),
        pl.BlockSpec((tq, LANES), lambda i: (tile_of(i), 8)),
        pl.BlockSpec((tq, LANES), lambda i: (tile_of(i), 9)),
        pl.BlockSpec((WINDOW, LANES), lambda i: (prev(i), 8)),
        pl.BlockSpec((WINDOW, LANES), lambda i: (prev(i), 9)),
        pl.BlockSpec((tq, LANES), lambda i: (tile_of(i), 0)),
        pl.BlockSpec((tq, LANES), lambda i: (tile_of(i), 0)),
        pl.BlockSpec((WINDOW, LANES), lambda i: (prev(i), 0)),
        pl.BlockSpec((WINDOW, LANES), lambda i: (prev(i), 0)),
        row, row,
        pl.BlockSpec((8, LANES), lambda i: (0, 0)),
    ]


def _attn_bias(first):
    qi = lax.broadcasted_iota(jnp.int32, (WINDOW, 2 * WINDOW), 0)
    kj = lax.broadcasted_iota(jnp.int32, (WINDOW, 2 * WINDOW), 1)
    diff = qi + WINDOW - kj
    ok = (diff >= 0) & (diff < WINDOW) & ((kj >= WINDOW) | jnp.logical_not(first))
    return jnp.where(ok, 0.0, NEG)


def _dup_heads(x, lo):
    sw = _swap64(x)
    return jnp.where(lo, x, sw), jnp.where(lo, sw, x)


HEADS_PER_KV = 4


def _stack_heads(x0, x1, lo):
    return jnp.concatenate([jnp.where(lo, x0, 0.0), jnp.where(lo, 0.0, x0), jnp.where(lo, x1, 0.0), jnp.where(lo, 0.0, x1)], axis=0)


def _unstack_heads(x4, lo):
    return (jnp.where(lo, x4[0:WINDOW], x4[WINDOW:2 * WINDOW]), jnp.where(lo, x4[2 * WINDOW:3 * WINDOW], x4[3 * WINDOW:4 * WINDOW]))


def _sink_column(sk_ref, g):
    return jnp.concatenate([jnp.broadcast_to(sk_ref[a:a + 1, 0:1], (WINDOW, 1)) for a in range(HEADS_PER_KV * g, HEADS_PER_KV * (g + 1))], axis=0)


def _attn_probs(q4, kw, bias, sink):
    s = _mm_nt(q4, kw)
    s = (s.reshape(HEADS_PER_KV, WINDOW, 2 * WINDOW) + bias[None]).reshape(HEADS_PER_KV * WINDOW, 2 * WINDOW)
    m = jnp.maximum(jnp.max(s, axis=-1, keepdims=True), sink)
    p = jnp.exp(s - m)
    es = jnp.exp(sink - m)
    inv = 1.0 / (jnp.sum(p, axis=-1, keepdims=True) + es)
    return p * inv, es * inv


def _attn_fwd(z, cs, sn, qg, kg, sinks):
    T = z.shape[0]
    tq = min(T, 512)
    nb = tq // WINDOW

    def body(q_ref, k_ref, v_ref, kp_ref, vp_ref, cs_ref, sn_ref, csp_ref, snp_ref, qg_ref, kg_ref, sk_ref, o_ref):
        i = pl.program_id(0)
        csq, snq = cs_ref[...], sn_ref[...]
        cs_all = jnp.concatenate([csp_ref[...], csq], axis=0)
        sn_all = jnp.concatenate([snp_ref[...], snq], axis=0)
        k_all = jnp.concatenate([kp_ref[...], k_ref[...]], axis=0)
        v_all = jnp.concatenate([vp_ref[...], v_ref[...]], axis=0)
        kr, _, _ = _qk_norm_rope(k_all, kg_ref[...], cs_all, sn_all)
        lo_all = _lane(k_all.shape) < HEAD_DIM
        kd = _dup_heads(kr, lo_all)
        vd = _dup_heads(v_all, lo_all)
        lo = _lane((WINDOW, LANES)) < HEAD_DIM
        qrs = [_qk_norm_rope(q_ref[:, pr * LANES:(pr + 1) * LANES], qg_ref[...], csq, snq)[0] * SCALE for pr in range(4)]
        biases = [_attn_bias(i * nb + b == 0) for b in range(nb)]
        for g in range(2):
            sink = _sink_column(sk_ref, g)
            for b in range(nb):
                bs = slice(b * WINDOW, (b + 1) * WINDOW)
                ws = slice(b * WINDOW, (b + 2) * WINDOW)
                pn, _ = _attn_probs(_stack_heads(qrs[2 * g][bs], qrs[2 * g + 1][bs], lo), kd[g][ws], biases[b], sink)
                o0, o1 = _unstack_heads(_mm(pn, vd[g][ws]), lo)
                o_ref[bs, 2 * g * LANES:(2 * g + 1) * LANES] = o0
                o_ref[bs, (2 * g + 1) * LANES:(2 * g + 2) * LANES] = o1

    return pl.pallas_call(
        body, name="attn_fwd", grid=(T // tq,),
        in_specs=_attn_specs(T, tq, lambda i: i),
        out_specs=pl.BlockSpec((tq, B_WIDTH), lambda i: (i, 0)), out_shape=_sds((T, B_WIDTH)),
        compiler_params=_cp("parallel"))(z, z, z, z, z, cs, sn, cs, sn, qg, kg, sinks)


def _attn_bwd(z, cs, sn, qg, kg, sinks, o, do):
    T = z.shape[0]
    tq = min(T, 512)
    nb = tq // WINDOW
    nt = T // tq
    tk = tq + WINDOW

    def tile_of(i):
        return nt - 1 - i

    def body(q_ref, k_ref, v_ref, kp_ref, vp_ref, cs_ref, sn_ref, csp_ref, snp_ref, qg_ref, kg_ref, sk_ref, o_ref, do_ref,
             dq_ref, dk_ref, dv_ref, dqg_ref, dkg_ref, dsk_ref, acck, accv, ck, cv):
        i = pl.program_id(0)
        ti = nt - 1 - i

        @pl.when(i == 0)
        def _():
            dqg_ref[...] = jnp.zeros_like(dqg_ref)
            dkg_ref[...] = jnp.zeros_like(dkg_ref)
            dsk_ref[...] = jnp.zeros_like(dsk_ref)
            ck[...] = jnp.zeros_like(ck)
            cv[...] = jnp.zeros_like(cv)

        csq, snq = cs_ref[...], sn_ref[...]
        cs_all = jnp.concatenate([csp_ref[...], csq], axis=0)
        sn_all = jnp.concatenate([snp_ref[...], snq], axis=0)
        k_all = jnp.concatenate([kp_ref[...], k_ref[...]], axis=0)
        v_all = jnp.concatenate([vp_ref[...], v_ref[...]], axis=0)
        kr, kh, rk = _qk_norm_rope(k_all, kg_ref[...], cs_all, sn_all)
        lo_all = _lane(k_all.shape) < HEAD_DIM
        kd = _dup_heads(kr, lo_all)
        vd = _dup_heads(v_all, lo_all)
        lo = _lane((WINDOW, LANES)) < HEAD_DIM
        acck[...] = jnp.zeros_like(acck)
        accv[...] = jnp.zeros_like(accv)
        prep = [_qk_norm_rope(q_ref[:, pr * LANES:(pr + 1) * LANES], qg_ref[...], csq, snq) for pr in range(4)]
        biases = [_attn_bias(ti * nb + b == 0) for b in range(nb)]
        dqs = [[None] * nb for _ in range(4)]
        for g in range(2):
            sink = _sink_column(sk_ref, g)
            dsink = jnp.zeros((HEADS_PER_KV * WINDOW, 1), F32)
            for b in range(nb):
                bs = slice(b * WINDOW, (b + 1) * WINDOW)
                ws = slice(b * WINDOW, (b + 2) * WINDOW)
                kw, vw = kd[g][ws], vd[g][ws]
                q4 = _stack_heads(prep[2 * g][0][bs] * SCALE, prep[2 * g + 1][0][bs] * SCALE, lo)
                pn, psink = _attn_probs(q4, kw, biases[b], sink)
                o0, o1 = o_ref[bs, 2 * g * LANES:(2 * g + 1) * LANES], o_ref[bs, (2 * g + 1) * LANES:(2 * g + 2) * LANES]
                do4 = _stack_heads(do_ref[bs, 2 * g * LANES:(2 * g + 1) * LANES], do_ref[bs, (2 * g + 1) * LANES:(2 * g + 2) * LANES], lo)
                delta = jnp.sum(do4 * jnp.concatenate([o0, o0, o1, o1], axis=0), axis=-1, keepdims=True)
                ds = pn * (_mm_nt(do4, vw) - delta)
                dsink = dsink - psink * delta
                dqs[2 * g][b], dqs[2 * g + 1][b] = _unstack_heads(_mm(ds, kw) * SCALE, lo)
                acck[g, ws, :] += _mm_tn(ds, q4)
                accv[g, ws, :] += _mm_tn(pn, do4)
            for hh in range(HEADS_PER_KV):
                a = HEADS_PER_KV * g + hh
                dsk_ref[a:a + 1, :] += jnp.zeros((1, LANES), F32) + jnp.sum(dsink[hh * WINDOW:(hh + 1) * WINDOW])
        for pr in range(4):
            _, qh, rq = prep[pr]
            dx, dg = _qk_norm_rope_bwd(qh, rq, qg_ref[...], csq, snq, jnp.concatenate(dqs[pr], axis=0))
            dq_ref[:, pr * LANES:(pr + 1) * LANES] = dx
            dqg_ref[...] += dg

        def fold(acc):
            f0 = acc[0] + _swap64(acc[0])
            f1 = acc[1] + _swap64(acc[1])
            return jnp.where(lo_all, f0, f1)

        dk_all = fold(acck)
        dv_all = fold(accv)
        pad = jnp.zeros((tq - WINDOW, LANES), F32)
        dk_own = dk_all[WINDOW:] + (jnp.concatenate([pad, ck[...]], axis=0) if nb > 1 else ck[...])
        dv_own = dv_all[WINDOW:] + (jnp.concatenate([pad, cv[...]], axis=0) if nb > 1 else cv[...])
        ck[...] = dk_all[:WINDOW]
        cv[...] = dv_all[:WINDOW]
        dxk, dgk = _qk_norm_rope_bwd(kh[WINDOW:], rk[WINDOW:], kg_ref[...], csq, snq, dk_own)
        dk_ref[...] = dxk
        dkg_ref[...] += dgk
        dv_ref[...] = dv_own

    row = pl.BlockSpec((1, LANES), lambda i: (0, 0))
    return pl.pallas_call(
        body, name="attn_bwd", grid=(nt,),
        in_specs=_attn_specs(T, tq, tile_of) + [pl.BlockSpec((tq, B_WIDTH), lambda i: (tile_of(i), 0))] * 2,
        out_specs=[pl.BlockSpec((tq, B_WIDTH), lambda i: (tile_of(i), 0)), pl.BlockSpec((tq, LANES), lambda i: (tile_of(i), 0)),
                   pl.BlockSpec((tq, LANES), lambda i: (tile_of(i), 0)), row, row, pl.BlockSpec((8, LANES), lambda i: (0, 0))],
        out_shape=[_sds((T, B_WIDTH)), _sds((T, LANES)), _sds((T, LANES)), _sds((1, LANES)), _sds((1, LANES)), _sds((8, LANES))],
        scratch_shapes=[pltpu.VMEM((2, tk, LANES), F32), pltpu.VMEM((2, tk, LANES), F32),
                        pltpu.VMEM((WINDOW, LANES), F32), pltpu.VMEM((WINDOW, LANES), F32)],
        compiler_params=_cp("arbitrary"))(z, z, z, z, z, cs, sn, cs, sn, qg, kg, sinks, o, do)


def _bbar_t(are, aim, ldt, btr, bti):
    lbr, lbi = _lam_bar(are, aim, ldt)
    den = are * are + aim * aim
    nr = lbr - 1.0
    cr = (nr * are + lbi * aim) / den
    ci = (lbi * are - nr * aim) / den
    return cr * btr - ci * bti, cr * bti + ci * btr


def _lam_bar(are, aim, ldt):
    dt = jnp.exp(ldt)
    er = jnp.exp(are * dt)
    return er * jnp.cos(aim * dt), er * jnp.sin(aim * dt)


def _block_diag(x):
    t = jnp.concatenate([x] * C_GROUPS, axis=1)
    return jnp.where(_row(t.shape) // C_GROUP == _lane(t.shape) // C_STATE, t, 0.0)


def _block_diag_fold(m):
    rg = _row((C_WIDTH, C_STATE)) // C_GROUP
    acc = jnp.zeros((C_WIDTH, C_STATE), F32)
    for g in range(C_GROUPS):
        acc = acc + jnp.where(rg == g, m[:, g * C_STATE:(g + 1) * C_STATE], 0.0)
    return acc


def _ssm_prep(are, aim, ldt, are_x, aim_x, ldt_x, btr, bti, cre, cim):
    def body(are_r, aim_r, ldt_r, arex_r, aimx_r, ldtx_r, btr_r, bti_r, cre_r, cim_r, bbd_ref, cbd_ref, pwr_ref, pwi_ref):
        lr, li = _lam_bar(are_r[...], aim_r[...], ldt_r[...])
        cr, ci = lr, li
        for r in range(SCAN_SEG):
            pwr_ref[r:r + 1, :] = cr
            pwi_ref[r:r + 1, :] = ci
            cr, ci = cr * lr - ci * li, cr * li + ci * lr
        br, bi = _bbar_t(arex_r[...], aimx_r[...], ldtx_r[...], btr_r[...], bti_r[...])
        bbd_ref[...] = jnp.concatenate([_block_diag(br), _block_diag(bi)], axis=1).astype(MXU)
        cbd_ref[...] = jnp.concatenate([_block_diag(cre_r[...]), -_block_diag(cim_r[...])], axis=1).astype(MXU)

    return pl.pallas_call(
        body, name="ssm_prep",
        out_shape=[_sds((C_WIDTH, 2 * N_STATE), MXU), _sds((C_WIDTH, 2 * N_STATE), MXU), _sds((SCAN_SEG, N_STATE)), _sds((SCAN_SEG, N_STATE))],
        compiler_params=pltpu.CompilerParams(vmem_limit_bytes=VMEM_LIMIT))(are, aim, ldt, are_x, aim_x, ldt_x, btr, bti, cre, cim)


def _ssm_param_bwd(are, aim, ldt, are_x, aim_x, ldt_x, btr, bti, dlr, dli, dbbd, dcr, dci):
    def body(are_r, aim_r, ldt_r, arex_r, aimx_r, ldtx_r, btr_r, bti_r, dlr_r, dli_r, dbbd_r, dcr_r, dci_r,
             dare_ref, daim_ref, dldt_ref, dbtr_ref, dbti_ref, dcre_ref, dcim_ref):
        _, vjp_l = jax.vjp(_lam_bar, are_r[...], aim_r[...], ldt_r[...])
        da1, di1, dl1 = vjp_l((dlr_r[...], dli_r[...]))
        dbr = _block_diag_fold(dbbd_r[:, 0:N_STATE])
        dbi = _block_diag_fold(dbbd_r[:, N_STATE:2 * N_STATE])
        _, vjp_b = jax.vjp(_bbar_t, arex_r[...], aimx_r[...], ldtx_r[...], btr_r[...], bti_r[...])
        da2, di2, dl2, dbtr, dbti = vjp_b((dbr, dbi))

        def gsum(x):
            return x.reshape(C_GROUPS, C_GROUP, C_STATE).sum(axis=1)

        dare_ref[...] = da1 + gsum(da2)
        daim_ref[...] = di1 + gsum(di2)
        dldt_ref[...] = jnp.broadcast_to(jnp.sum(dl1 + gsum(dl2), axis=-1, keepdims=True), (C_GROUPS, LANES))
        dbtr_ref[...] = dbtr
        dbti_ref[...] = dbti
        dcre_ref[...] = _block_diag_fold(dcr_r[...])
        dcim_ref[...] = -_block_diag_fold(dci_r[...])

    g = _sds((C_GROUPS, C_STATE))
    x = _sds((C_WIDTH, C_STATE))
    return pl.pallas_call(
        body, name="ssm_param_bwd", out_shape=[g, g, _sds((C_GROUPS, LANES)), x, x, x, x],
        compiler_params=pltpu.CompilerParams(vmem_limit_bytes=VMEM_LIMIT))(are, aim, ldt, are_x, aim_x, ldt_x, btr, bti, dlr, dli, dbbd, dcr, dci)


SCAN_TILE = 256
SCAN_SEG = 8


def _scan_tables(pwr_ref, pwi_ref, conj, reverse):
    row = _row((SCAN_SEG, N_STATE))
    shifts = []
    for k in (1, 2, 4):
        keep = (row < SCAN_SEG - k) if reverse else (row >= k)
        ar = jnp.broadcast_to(pwr_ref[k - 1:k, :], (SCAN_SEG, N_STATE))
        ai = jnp.broadcast_to(pwi_ref[k - 1:k, :], (SCAN_SEG, N_STATE)) * conj
        shifts.append((SCAN_SEG - k if reverse else k, jnp.where(keep, ar, 0.0), jnp.where(keep, ai, 0.0)))
    if reverse:
        pr = jnp.concatenate([pwr_ref[SCAN_SEG - 1 - r:SCAN_SEG - r, :] for r in range(SCAN_SEG)], axis=0)
        pi = jnp.concatenate([pwi_ref[SCAN_SEG - 1 - r:SCAN_SEG - r, :] for r in range(SCAN_SEG)], axis=0) * conj
    else:
        pr, pi = pwr_ref[...], pwi_ref[...] * conj
    return shifts, (pr, pi)


def _tile_scan(xr_ref, xi_ref, pwr_ref, pwi_ref, sr, si, conj, reverse):
    shifts, (pr, pi) = _scan_tables(pwr_ref, pwi_ref, conj, reverse)
    groups = xr_ref.shape[0] // SCAN_SEG
    out_row = 0 if reverse else SCAN_SEG - 1

    def step(k, c):
        cr, ci = c
        g = groups - 1 - k if reverse else k
        rows = pl.ds(pl.multiple_of(g * SCAN_SEG, SCAN_SEG), SCAN_SEG)
        xr, xi = xr_ref[rows, :], xi_ref[rows, :]
        for amount, ar, ai in shifts:
            qr, qi = pltpu.roll(xr, amount, 0), pltpu.roll(xi, amount, 0)
            xr, xi = xr + ar * qr - ai * qi, xi + ar * qi + ai * qr
        xr, xi = xr + pr * cr - pi * ci, xi + pr * ci + pi * cr
        xr_ref[rows, :] = xr
        xi_ref[rows, :] = xi
        return xr[out_row:out_row + 1], xi[out_row:out_row + 1]

    cr, ci = lax.fori_loop(0, groups, step, (sr[...], si[...]), unroll=2)
    sr[...] = cr
    si[...] = ci


def _ssm_fwd(z, bbd, pwr, pwi, cbd, dsk, w1, w2):
    T = z.shape[0]
    tt = min(T, SCAN_TILE)

    def body(u_ref, bbd_ref, pwr_ref, pwi_ref, cbd_ref, d_ref, w1_ref, w2_ref, yc_ref, xr_ref, xi_ref, sr, si):
        @pl.when(pl.program_id(0) == 0)
        def _():
            sr[...] = jnp.zeros_like(sr)
            si[...] = jnp.zeros_like(si)

        u = u_ref[...]
        bu = _mm(u, bbd_ref[...])
        xr_ref[...] = bu[:, 0:N_STATE]
        xi_ref[...] = bu[:, N_STATE:2 * N_STATE]
        _tile_scan(xr_ref, xi_ref, pwr_ref, pwi_ref, sr, si, 1.0, False)
        x = jnp.concatenate([xr_ref[...], xi_ref[...]], axis=1)
        y2 = _gelu(_mm_nt(x, cbd_ref[...]) + d_ref[...] * u)
        yc_ref[...] = _mm(y2, w1_ref[...]) * _sigmoid(_mm(y2, w2_ref[...]))

    big = pl.BlockSpec((C_WIDTH, 2 * N_STATE), lambda i: (0, 0))
    tab = pl.BlockSpec((SCAN_SEG, N_STATE), lambda i: (0, 0))
    wsp = pl.BlockSpec((C_WIDTH, C_WIDTH), lambda i: (0, 0))
    xs = pl.BlockSpec((tt, N_STATE), lambda i: (i, 0))
    return pl.pallas_call(
        body, name="ssm_fwd", grid=(T // tt,),
        in_specs=[pl.BlockSpec((tt, C_WIDTH), lambda i: (i, 5)), big, tab, tab, big, pl.BlockSpec((1, C_WIDTH), lambda i: (0, 0)), wsp, wsp],
        out_specs=[pl.BlockSpec((tt, C_WIDTH), lambda i: (i, 0)), xs, xs],
        out_shape=[_sds((T, C_WIDTH)), _sds((T, N_STATE)), _sds((T, N_STATE))],
        scratch_shapes=[pltpu.VMEM((1, N_STATE), F32)] * 2,
        compiler_params=_cp("arbitrary"))(z, bbd, pwr, pwi, cbd, dsk, w1, w2)


def _ssm_bwd(dyc, z, xr, xi, bbd, pwr, pwi, cbd, dsk, w1, w2):
    T = z.shape[0]
    tt = min(T, SCAN_TILE)
    nt = T // tt

    def tile_of(i):
        return nt - 1 - i

    def body(dyc_ref, u_ref, xr_ref, xi_ref, xpr_ref, xpi_ref, bbd_ref, pwr_ref, pwi_ref, cbd_ref, d_ref, w1_ref, w2_ref,
             du_ref, y2_ref, da1_ref, da2_ref, dy_ref, arai_ref, dlr_ref, dli_ref, dd_ref, gr, gi, sr, si):
        i = pl.program_id(0)
        ti = nt - 1 - i

        @pl.when(i == 0)
        def _():
            sr[...] = jnp.zeros_like(sr)
            si[...] = jnp.zeros_like(si)
            dlr_ref[...] = jnp.zeros_like(dlr_ref)
            dli_ref[...] = jnp.zeros_like(dli_ref)
            dd_ref[...] = jnp.zeros_like(dd_ref)

        u = u_ref[...]
        xr_t, xi_t = xr_ref[...], xi_ref[...]
        y = _mm_nt(jnp.concatenate([xr_t, xi_t], axis=1), cbd_ref[...]) + d_ref[...] * u
        y2 = _gelu(y)
        a1 = _mm(y2, w1_ref[...])
        sg = _sigmoid(_mm(y2, w2_ref[...]))
        dyc_t = dyc_ref[...]
        da1 = dyc_t * sg
        da2 = dyc_t * a1 * sg * (1.0 - sg)
        dy = (_mm_nt(da1, w1_ref[...]) + _mm_nt(da2, w2_ref[...])) * _gelu_grad(y)
        gx = _mm(dy, cbd_ref[...])
        gr[...] = gx[:, 0:N_STATE]
        gi[...] = gx[:, N_STATE:2 * N_STATE]
        _tile_scan(gr, gi, pwr_ref, pwi_ref, sr, si, -1.0, True)
        ar, ai = gr[...], gi[...]
        first_row = _row(ar.shape) == 0
        live = jnp.where(ti > 0, 1.0, 0.0)
        xsr = jnp.where(first_row, xpr_ref[7:8, :] * live, pltpu.roll(xr_t, 1, 0))
        xsi = jnp.where(first_row, xpi_ref[7:8, :] * live, pltpu.roll(xi_t, 1, 0))
        dlr_ref[...] += jnp.sum(ar * xsr + ai * xsi, axis=0, keepdims=True)
        dli_ref[...] += jnp.sum(ai * xsr - ar * xsi, axis=0, keepdims=True)
        dd_ref[...] += jnp.sum(dy * u, axis=0, keepdims=True)
        arai = jnp.concatenate([ar, ai], axis=1)
        du_ref[...] = _mm_nt(arai, bbd_ref[...]) + d_ref[...] * dy
        y2_ref[...] = y2.astype(MXU)
        da1_ref[...] = da1.astype(MXU)
        da2_ref[...] = da2.astype(MXU)
        dy_ref[...] = dy.astype(MXU)
        arai_ref[...] = arai.astype(MXU)

    def prev(i):
        return jnp.maximum(tile_of(i) * (tt // 8) - 1, 0)

    big = pl.BlockSpec((C_WIDTH, 2 * N_STATE), lambda i: (0, 0))
    tab = pl.BlockSpec((SCAN_SEG, N_STATE), lambda i: (0, 0))
    srow = pl.BlockSpec((1, N_STATE), lambda i: (0, 0))
    wsp = pl.BlockSpec((C_WIDTH, C_WIDTH), lambda i: (0, 0))
    xs = pl.BlockSpec((tt, N_STATE), lambda i: (tile_of(i), 0))
    xp = pl.BlockSpec((8, N_STATE), lambda i: (prev(i), 0))
    cw = pl.BlockSpec((tt, C_WIDTH), lambda i: (tile_of(i), 0))
    drow = pl.BlockSpec((1, C_WIDTH), lambda i: (0, 0))
    return pl.pallas_call(
        body, name="ssm_bwd", grid=(nt,),
        in_specs=[cw, pl.BlockSpec((tt, C_WIDTH), lambda i: (tile_of(i), 5)), xs, xs, xp, xp, big, tab, tab, big, drow, wsp, wsp],
        out_specs=[cw, cw, cw, cw, cw, pl.BlockSpec((tt, 2 * N_STATE), lambda i: (tile_of(i), 0)), srow, srow, drow],
        out_shape=[_sds((T, C_WIDTH))] + [_sds((T, C_WIDTH), MXU)] * 4 + [_sds((T, 2 * N_STATE), MXU), _sds((1, N_STATE)), _sds((1, N_STATE)), _sds((1, C_WIDTH))],
        scratch_shapes=[pltpu.VMEM((tt, N_STATE), F32)] * 2 + [pltpu.VMEM((1, N_STATE), F32)] * 2,
        compiler_params=_cp("arbitrary"))(dyc, z, xr, xi, xr, xi, bbd, pwr, pwi, cbd, dsk, w1, w2)


_GROUPS = ((0, A_WIDTH), (A_WIDTH, A_WIDTH + B_WIDTH), (A_WIDTH + B_WIDTH, D_MODEL))


def _merge_fwd(h, ya, yb, yc, g, w):
    T = h.shape[0]
    tm = min(T, 512)

    def body(h_ref, a_ref, b_ref, c_ref, g_ref, w_ref, o_ref):
        yn = jnp.concatenate([y * _rms_stat(y) for y in (a_ref[...], b_ref[...], c_ref[...])], axis=1) * g_ref[...]
        o_ref[...] = h_ref[...] + _mm(yn, w_ref[...])

    def rows(w_):
        return pl.BlockSpec((tm, w_), lambda i: (i, 0))

    return pl.pallas_call(
        body, name="merge_fwd", grid=(T // tm,),
        in_specs=[rows(D_MODEL), rows(A_WIDTH), rows(B_WIDTH), rows(C_WIDTH), pl.BlockSpec((1, D_MODEL), lambda i: (0, 0)),
                  pl.BlockSpec((D_MODEL, D_MODEL), lambda i: (0, 0))],
        out_specs=rows(D_MODEL), out_shape=_sds((T, D_MODEL)), compiler_params=_cp("parallel"))(h, ya, yb, yc, g, w)


def _merge_bwd(gres, ya, yb, yc, g, w):
    T = gres.shape[0]
    tm = min(T, 512)

    def body(gr_ref, a_ref, b_ref, c_ref, g_ref, w_ref, da_ref, db_ref, dc_ref, yn_ref, dg_ref):
        @pl.when(pl.program_id(0) == 0)
        def _():
            dg_ref[...] = jnp.zeros_like(dg_ref)

        dyn = _mm_nt(gr_ref[...], w_ref[...])
        yns, dgs = [], []
        for (c0, c1), y_ref, d_ref in zip(_GROUPS, (a_ref, b_ref, c_ref), (da_ref, db_ref, dc_ref)):
            y = y_ref[...]
            r = _rms_stat(y)
            gg = g_ref[:, c0:c1]
            dx, dg = _rms_bwd(y, r, gg, dyn[:, c0:c1])
            d_ref[...] = dx
            dgs.append(dg)
            yns.append(y * r * gg)
        dg_ref[...] += jnp.concatenate(dgs, axis=1)
        yn_ref[...] = jnp.concatenate(yns, axis=1).astype(MXU)

    def rows(w_):
        return pl.BlockSpec((tm, w_), lambda i: (i, 0))

    row = pl.BlockSpec((1, D_MODEL), lambda i: (0, 0))
    return pl.pallas_call(
        body, name="merge_bwd", grid=(T // tm,),
        in_specs=[rows(D_MODEL), rows(A_WIDTH), rows(B_WIDTH), rows(C_WIDTH), row, pl.BlockSpec((D_MODEL, D_MODEL), lambda i: (0, 0))],
        out_specs=[rows(A_WIDTH), rows(B_WIDTH), rows(C_WIDTH), rows(D_MODEL), row],
        out_shape=[_sds((T, A_WIDTH)), _sds((T, B_WIDTH)), _sds((T, C_WIDTH)), _sds((T, D_MODEL), MXU), _sds((1, D_MODEL))],
        compiler_params=_cp("arbitrary"))(gres, ya, yb, yc, g, w)


FF_BLOCK = D_FF // N_DEV


def _load_weights(w1_hbm, w2_hbm, w1, w2, sem):
    @pl.when(pl.program_id(0) == 0)
    def _():
        copies = [pltpu.make_async_copy(w1_hbm.at[j], w1.at[:, pl.ds(j * FF_BLOCK, FF_BLOCK)], sem.at[j]) for j in range(N_DEV)]
        copies.append(pltpu.make_async_copy(w2_hbm, w2, sem.at[N_DEV]))
        for cp in copies:
            cp.start()
        for cp in copies:
            cp.wait()


def _mlp_weight_scratch():
    return [pltpu.VMEM((D_MODEL, D_FF), MXU), pltpu.VMEM((D_FF, D_MODEL), MXU), pltpu.SemaphoreType.DMA((N_DEV + 1,))]


def _mlp_fwd(h, g, w1, w2):
    T = h.shape[0]
    tm = min(T, 512)

    def body(h_ref, g_ref, w1_hbm, w2_hbm, o_ref, r_ref, w1_v, w2_v, sem):
        _load_weights(w1_hbm, w2_hbm, w1_v, w2_v, sem)
        x = h_ref[...]
        a = jnp.maximum(_mm(x * _rms_stat(x) * g_ref[...], w1_v[...]), 0.0)
        r = (a * a).astype(MXU)
        r_ref[...] = r
        o_ref[...] = x + _mm(r, w2_v[...])

    rows = pl.BlockSpec((tm, D_MODEL), lambda i: (i, 0))
    hbm = pl.BlockSpec(memory_space=pl.ANY)
    return pl.pallas_call(
        body, name="mlp_fwd", grid=(T // tm,),
        in_specs=[rows, pl.BlockSpec((1, D_MODEL), lambda i: (0, 0)), hbm, hbm],
        out_specs=[rows, pl.BlockSpec((tm, D_FF), lambda i: (i, 0))],
        out_shape=[_sds((T, D_MODEL)), _sds((T, D_FF), MXU)],
        scratch_shapes=_mlp_weight_scratch(), compiler_params=_cp("arbitrary"))(h, g, w1, w2)


def _mlp_bwd(gres, h, g, r, w1, w2):
    T = h.shape[0]
    tm = min(T, 256)

    def body(gr_ref, h_ref, g_ref, r_ref, w1_hbm, w2_hbm, dh_ref, hn_ref, da_ref, dg_ref, w1_v, w2_v, sem):
        _load_weights(w1_hbm, w2_hbm, w1_v, w2_v, sem)

        @pl.when(pl.program_id(0) == 0)
        def _():
            dg_ref[...] = jnp.zeros_like(dg_ref)

        gr = gr_ref[...]
        da = (_mm_nt(gr, w2_v[...]) * (2.0 * jnp.sqrt(r_ref[...].astype(F32)))).astype(MXU)
        da_ref[...] = da
        x = h_ref[...]
        rs = _rms_stat(x)
        gg = g_ref[...]
        dx, dg = _rms_bwd(x, rs, gg, _mm_nt(da, w1_v[...]))
        dh_ref[...] = gr + dx
        dg_ref[...] += dg
        hn_ref[...] = (x * rs * gg).astype(MXU)

    rows = pl.BlockSpec((tm, D_MODEL), lambda i: (i, 0))
    row = pl.BlockSpec((1, D_MODEL), lambda i: (0, 0))
    wide = pl.BlockSpec((tm, D_FF), lambda i: (i, 0))
    hbm = pl.BlockSpec(memory_space=pl.ANY)
    return pl.pallas_call(
        body, name="mlp_bwd", grid=(T // tm,),
        in_specs=[rows, rows, row, wide, hbm, hbm], out_specs=[rows, rows, wide, row],
        out_shape=[_sds((T, D_MODEL)), _sds((T, D_MODEL), MXU), _sds((T, D_FF), MXU), _sds((1, D_MODEL))],
        scratch_shapes=_mlp_weight_scratch(), compiler_params=_cp("arbitrary"))(gres, h, g, r, w1, w2)


def _ple_fwd(h, p, l, g, wg, wp):
    T = h.shape[0]
    tm = min(T, 512)

    def body(h_ref, p_ref, g_ref, wg_ref, wp_ref, o_ref):
        x = h_ref[...]
        gate = _sigmoid(_mm(x * _rms_stat(x) * g_ref[...], wg_ref[...]))
        o_ref[...] = x + gate * _mm(p_ref[...], wp_ref[...])

    rows = pl.BlockSpec((tm, D_MODEL), lambda i: (i, 0))
    return pl.pallas_call(
        body, name="ple_fwd", grid=(T // tm,),
        in_specs=[rows, pl.BlockSpec((None, tm, PLE_DIM), lambda i: (l, i, 0)), pl.BlockSpec((1, D_MODEL), lambda i: (0, 0)),
                  pl.BlockSpec((D_MODEL, D_MODEL), lambda i: (0, 0)), pl.BlockSpec((PLE_DIM, D_MODEL), lambda i: (0, 0))],
        out_specs=rows, out_shape=_sds((T, D_MODEL)), compiler_params=_cp("parallel"))(h, p, g, wg, wp)


def _ple_bwd(gres, h, p, l, g, wg, wp):
    T = h.shape[0]
    tm = min(T, 512)

    def body(gr_ref, h_ref, p_ref, g_ref, wg_ref, wp_ref, dh_ref, hn_ref, dgp_ref, de_ref, dg_ref):
        @pl.when(pl.program_id(0) == 0)
        def _():
            dg_ref[...] = jnp.zeros_like(dg_ref)

        x = h_ref[...]
        r = _rms_stat(x)
        gg = g_ref[...]
        hn = x * r * gg
        gate = _sigmoid(_mm(hn, wg_ref[...]))
        e = _mm(p_ref[...], wp_ref[...])
        gr = gr_ref[...]
        dgp = gr * e * gate * (1.0 - gate)
        dx, dg = _rms_bwd(x, r, gg, _mm_nt(dgp, wg_ref[...]))
        dh_ref[...] = gr + dx
        dg_ref[...] += dg
        hn_ref[...] = hn.astype(MXU)
        dgp_ref[...] = dgp.astype(MXU)
        de_ref[...] = (gr * gate).astype(MXU)

    rows = pl.BlockSpec((tm, D_MODEL), lambda i: (i, 0))
    row = pl.BlockSpec((1, D_MODEL), lambda i: (0, 0))
    return pl.pallas_call(
        body, name="ple_bwd", grid=(T // tm,),
        in_specs=[rows, rows, pl.BlockSpec((None, tm, PLE_DIM), lambda i: (l, i, 0)), row,
                  pl.BlockSpec((D_MODEL, D_MODEL), lambda i: (0, 0)), pl.BlockSpec((PLE_DIM, D_MODEL), lambda i: (0, 0))],
        out_specs=[rows, rows, rows, rows, row],
        out_shape=[_sds((T, D_MODEL))] + [_sds((T, D_MODEL), MXU)] * 3 + [_sds((1, D_MODEL))],
        compiler_params=_cp("arbitrary"))(gres, h, p, g, wg, wp)


def _loss_head(h, target):
    T = h.shape[0]
    tm = min(T, 1024)

    def body(h_ref, t_ref, dh_ref, l_ref):
        @pl.when(pl.program_id(0) == 0)
        def _():
            l_ref[...] = jnp.zeros_like(l_ref)

        e = h_ref[...] - t_ref[...]
        dh_ref[...] = e * (1.0 / D_MODEL)
        l_ref[...] += jnp.zeros_like(l_ref) + 0.5 * jnp.sum(jnp.mean(e * e, axis=-1, keepdims=True))

    rows = pl.BlockSpec((tm, D_MODEL), lambda i: (i, 0))
    return pl.pallas_call(
        body, name="loss_head", grid=(T // tm,), in_specs=[rows, rows],
        out_specs=[rows, pl.BlockSpec((8, LANES), lambda i: (0, 0))], out_shape=[_sds((T, D_MODEL)), _sds((8, LANES))],
        compiler_params=_cp("arbitrary"))(h, target)


TN_ROWS = 2048

def _tn(a, b, m, n, *, bm, bn, a_off=0, b_off=0, a_lead=None, n_major=False, split=None, dtype=F32, name="tn"):
    T = a.shape[-2]
    tk = min(T, TN_ROWS)
    nk = T // tk
    b_spec = pl.BlockSpec((tk, bn), lambda i, j, k: (k, b_off + j))
    if a_lead is None:
        a_spec = pl.BlockSpec((tk, bm), lambda i, j, k: (k, a_off + i))
    else:
        a_spec = pl.BlockSpec((None, tk, bm), lambda i, j, k: (a_lead, k, a_off + i))
    assert m % bm == 0 and n % bn == 0 and (not n_major or bm == m) and (split is None or (bn == n and n % split == 0))

    def body(a_ref, b_ref, o_ref, acc):
        k = pl.program_id(2)

        @pl.when(k == 0)
        def _():
            acc[...] = jnp.zeros_like(acc)

        acc[...] += _mm_tn(a_ref[...], b_ref[...])

        @pl.when(k == nk - 1)
        def _():
            if split is None:
                o_ref[...] = acc[...].astype(dtype)
            else:
                for d in range(n // split):
                    o_ref[d] = acc[:, d * split:(d + 1) * split].astype(dtype)

    if split is not None:
        out_spec = pl.BlockSpec((n // split, bm, split), lambda i, j, k: (0, i, 0))
        out_shape = _sds((n // split, m, split), dtype)
    elif n_major:
        out_spec = pl.BlockSpec((None, bm, bn), lambda i, j, k: (j, 0, 0))
        out_shape = _sds((n // bn, m, bn), dtype)
    else:
        out_spec = pl.BlockSpec((bm, bn), lambda i, j, k: (i, j))
        out_shape = _sds((m, n), dtype)
    return pl.pallas_call(
        body, name=name, grid=(m // bm, n // bn, nk),
        in_specs=[a_spec, b_spec],
        out_specs=out_spec, out_shape=out_shape, scratch_shapes=[pltpu.VMEM((bm, bn), F32)],
        compiler_params=_cp("parallel", "parallel", "arbitrary"))(a, b)


def _row_tile(R, C):
    for cand in (512, 256, 128, 64, 32, 16, 8):
        if R % cand == 0 and cand * C * 4 <= 2 ** 20:
            return cand
    return R


def _sum_slots(land):
    S, R, C = land.shape
    tr = _row_tile(R, C)

    def body(l_ref, o_ref):
        acc = l_ref[0].astype(F32)
        for s in range(1, S):
            acc = acc + l_ref[s].astype(F32)
        o_ref[...] = acc

    return pl.pallas_call(
        body, name="sum_slots", grid=(R // tr,), in_specs=[pl.BlockSpec((S, tr, C), lambda i: (0, i, 0))],
        out_specs=pl.BlockSpec((tr, C), lambda i: (i, 0)), out_shape=_sds((R, C)), compiler_params=_cp("parallel"))(land)


def _sum_adamw(land, w, m, v):
    R, C = w.shape
    S = land.shape[0]
    tr = _row_tile(R, C)

    def body(l_ref, w_ref, m_ref, v_ref, g_ref, d_ref, nm_ref, nv_ref):
        gg = l_ref[0].astype(F32)
        for s in range(1, S):
            gg = gg + l_ref[s].astype(F32)
        g_ref[...] = gg
        nm = ADAM_B1 * m_ref[...] + (1.0 - ADAM_B1) * gg
        nv = ADAM_B2 * v_ref[...] + (1.0 - ADAM_B2) * (gg * gg)
        m_hat = nm / (1.0 - ADAM_B1 ** ADAM_STEP)
        v_hat = nv / (1.0 - ADAM_B2 ** ADAM_STEP)
        d_ref[...] = -ADAM_LR * (m_hat / (jnp.sqrt(v_hat) + ADAM_EPS) + ADAM_WD * w_ref[...])
        nm_ref[...] = nm
        nv_ref[...] = nv

    blk = pl.BlockSpec((tr, C), lambda i: (i, 0))
    return pl.pallas_call(
        body, name="sum_adamw", grid=(R // tr,), in_specs=[pl.BlockSpec((S, tr, C), lambda i: (0, i, 0))] + [blk] * 3,
        out_specs=[blk] * 4, out_shape=[_sds((R, C))] * 4, compiler_params=_cp("parallel"))(land, w, m, v)


def _all_to_all(pairs, name):
    n = len(pairs)

    def body(*refs):
        srcs, lands = refs[:n], refs[2 * n:3 * n]
        send, recv, loc = refs[3 * n:]
        x, y, c = lax.axis_index("x"), lax.axis_index("y"), lax.axis_index("c")
        me = 4 * x + 2 * y + c
        own = [pltpu.make_async_copy(pairs[t][2](srcs[t], me, me), pairs[t][3](lands[t], me), loc.at[t]) for t in range(n)]
        for cp in own:
            cp.start()
        sends, recvs = [], []
        for k in range(1, N_DEV):
            px, py, pc = x ^ (k >> 2), y ^ ((k >> 1) & 1), c ^ (k & 1)
            peer = 4 * px + 2 * py + pc
            for t in range(n):
                src = pairs[t][2](srcs[t], me, peer)
                cp = pltpu.make_async_remote_copy(src_ref=src, dst_ref=pairs[t][3](lands[t], me), send_sem=send.at[t, k],
                                                  recv_sem=recv.at[t, k], device_id=(px, py, pc), device_id_type=MESH)
                cp.start()
                sends.append(cp)
                recvs.append(pltpu.make_async_remote_copy(src_ref=src, dst_ref=pairs[t][3](lands[t], peer), send_sem=send.at[t, k],
                                                          recv_sem=recv.at[t, k], device_id=(px, py, pc), device_id_type=MESH))
        for cp in recvs:
            cp.wait_recv()
        for cp in sends:
            cp.wait_send()
        for cp in own:
            cp.wait()

    anyspec = pl.BlockSpec(memory_space=pl.ANY)
    lands = [pr[1] for pr in pairs]
    return pl.pallas_call(
        body, name=name, in_specs=[anyspec] * (2 * n), out_specs=[anyspec] * n,
        out_shape=[_sds(a.shape, a.dtype) for a in lands], input_output_aliases={n + t: t for t in range(n)},
        scratch_shapes=[pltpu.SemaphoreType.DMA((n, N_DEV)), pltpu.SemaphoreType.DMA((n, N_DEV)), pltpu.SemaphoreType.DMA((n,))],
        compiler_params=pltpu.CompilerParams(has_side_effects=True))(*[pr[0] for pr in pairs], *lands)


def _gather_whole(src):
    land = lax.empty((N_DEV,) + src.shape, src.dtype)
    return _all_to_all([(src, land, lambda ref, me, peer: ref, lambda ref, sender: ref.at[sender])], "gather_small_grads")[0]


_HBM = pl.BlockSpec(memory_space=pltpu.HBM)
_SEM = pl.BlockSpec(memory_space=pltpu.SEMAPHORE)
_EFFECT = pltpu.SideEffectType.DATAFLOW_SIDE_EFFECTING


def _peers():
    x, y, c = lax.axis_index("x"), lax.axis_index("y"), lax.axis_index("c")
    out = []
    for k in range(1, N_DEV):
        px, py, pc = x ^ (k >> 2), y ^ ((k >> 1) & 1), c ^ (k & 1)
        out.append((k, (px, py, pc), 4 * px + 2 * py + pc))
    return 4 * x + 2 * y + c, out


def _route(mode, layer):
    if mode == "all":
        return (lambda ref, peer: ref), (lambda ref, sender: ref.at[sender])
    return (lambda ref, peer: ref.at[peer]), (lambda ref, sender: ref.at[sender, layer])


def _split_start(srcs, lands, modes, layer, after, name):
    n = len(srcs)
    routes = [_route(m, layer) for m in modes]

    def body(*refs):
        src, land = refs[:n], refs[n:2 * n]
        send, recv, token = refs[2 * n + 1], refs[2 * n + 2], refs[-1]
        me, peers = _peers()
        for k, dev, peer in peers:
            for t, (src_of, dst_of) in enumerate(routes):
                pltpu.make_async_remote_copy(src_ref=src_of(src[t], peer), dst_ref=dst_of(land[t], me), send_sem=send.at[t * N_DEV + k],
                                             recv_sem=recv.at[t * N_DEV + k], device_id=dev, device_id_type=MESH).start()
        token[...] = jnp.zeros_like(token)

    bufs = list(srcs) + list(lands)
    outs = pl.pallas_call(
        body, name=name,
        out_shape=(pltpu.SemaphoreType.DMA((n * N_DEV,)), pltpu.SemaphoreType.DMA((n * N_DEV,)),
                   *[pltpu.HBM(a.shape, a.dtype) for a in bufs], _sds((8, LANES))),
        in_specs=[_HBM] * (2 * n) + [pl.BlockSpec(memory_space=pl.ANY)],
        out_specs=(_SEM, _SEM, *[_HBM] * (2 * n), pl.BlockSpec(memory_space=pltpu.VMEM)),
        input_output_aliases={i: 2 + i for i in range(2 * n)},
        compiler_params=pltpu.CompilerParams(has_side_effects=_EFFECT),
    )(*[pltpu.with_memory_space_constraint(a, pltpu.HBM) for a in bufs], after)
    return outs[0], outs[1], list(outs[2:2 + n]), list(outs[2 + n:2 + 2 * n]), outs[-1]


def _split_wait(send, recv, srcs, lands, modes, layer, after, name):
    n = len(srcs)
    routes = [_route(m, layer) for m in modes]

    def body(*refs):
        src, land = refs[:n], refs[n:2 * n]
        send_r, recv_r = refs[2 * n], refs[2 * n + 1]
        _, peers = _peers()
        for k, dev, peer in peers:
            for t, (src_of, dst_of) in enumerate(routes):
                cp = pltpu.make_async_remote_copy(src_ref=src_of(src[t], peer), dst_ref=dst_of(land[t], peer), send_sem=send_r.at[t * N_DEV + k],
                                                  recv_sem=recv_r.at[t * N_DEV + k], device_id=dev, device_id_type=MESH)
                cp.wait_send()
                cp.wait_recv()

    bufs = list(srcs) + list(lands)
    outs = pl.pallas_call(
        body, name=name, out_shape=tuple(pltpu.HBM(a.shape, a.dtype) for a in bufs),
        in_specs=[_HBM] * (2 * n) + [_SEM, _SEM, pl.BlockSpec(memory_space=pl.ANY)], out_specs=[_HBM] * (2 * n),
        input_output_aliases={i: i for i in range(2 * n)},
        compiler_params=pltpu.CompilerParams(has_side_effects=_EFFECT),
    )(*bufs, send, recv, after)
    return list(outs[:n]), list(outs[n:])


SHARDED = ("w_in", "glu_w1", "glu_w2", "w_out", "w_ff1", "w_ff2", "w_ple_gate", "w_ple_proj")
SMALL = ("attn_norm_g", "gmlp_ln_g", "gmlp_ln_b", "gmlp_ws", "gmlp_bs", "q_norm_g", "k_norm_g", "sinks", "ssm_a_re", "ssm_a_im",
         "ssm_log_dt", "ssm_b_re", "ssm_b_im", "ssm_c_re", "ssm_c_im", "ssm_d", "mix_out_g", "mlp_norm_g", "ple_norm_g")
WEIGHTS = ("attn_norm_g", "w_in", "gmlp_ln_g", "gmlp_ln_b", "gmlp_ws", "gmlp_bs", "q_norm_g", "k_norm_g", "sinks", "ssm_a_re", "ssm_a_im",
           "ssm_log_dt", "ssm_b_re", "ssm_b_im", "ssm_c_re", "ssm_c_im", "ssm_d", "glu_w1", "glu_w2", "mix_out_g", "w_out", "mlp_norm_g",
           "w_ff1", "w_ff2", "ple_norm_g", "w_ple_gate", "w_ple_proj")
FLAT_COLS = 1024


PACK_TILE_ROWS = 8
PACK_ROWS_MULTIPLE = PACK_TILE_ROWS * N_DEV


def _packed_rows(shape):
    return -(-math.prod(shape) // (PACK_TILE_ROWS * FLAT_COLS)) * PACK_TILE_ROWS


def _pack(arrs, dtype):
    blocks = []
    for a in arrs:
        flat = a.astype(dtype).reshape(-1)
        pad = _packed_rows(a.shape) * FLAT_COLS - flat.shape[0]
        if pad:
            flat = jnp.concatenate([flat, jnp.zeros((pad,), dtype)])
        blocks.append(flat.reshape(-1, FLAT_COLS))
    rows = sum(b.shape[0] for b in blocks)
    if rows % PACK_ROWS_MULTIPLE:
        blocks.append(jnp.zeros((PACK_ROWS_MULTIPLE - rows % PACK_ROWS_MULTIPLE, FLAT_COLS), dtype))
    return jnp.concatenate(blocks, axis=0)


def _unpack(flat, shapes):
    out, r = [], 0
    for s in shapes:
        nr = _packed_rows(s)
        out.append(flat[r:r + nr].reshape(-1)[:math.prod(s)].reshape(s))
        r += nr
    return out


def _from_col_major(s):
    n, rows, cs = s.shape
    return s.transpose(1, 0, 2).reshape(rows, n * cs)


EARLY = ("w_in", "glu_w1", "glu_w2")
LATE = ("w_out", "w_ff1", "w_ff2", "w_ple_gate", "w_ple_proj")
GRADS_MID = ("w_ff1", "w_ff2", "w_ple_gate", "w_ple_proj")
GRADS_END = ("w_in", "glu_w1", "glu_w2", "w_out")


def _layer_fwd(h, p, l, cs, sn, W, late_weights, S, sp):
    z = _inproj_fwd(h, S["attn_norm_g"], W["w_in"])
    ya = _gmlp_fwd(z, S["lng"], S["lnb"], S["gmlp_ws"], S["bsx"])
    yb = _attn_fwd(z, cs, sn, S["qg"], S["kg"], S["sinks"])
    yc, xr, xi = _ssm_fwd(z, sp["bbd"], sp["pwr"], sp["pwi"], sp["cbd"], S["ssm_d"], W["glu_w1"], W["glu_w2"])
    late, token = late_weights(ya[0:8, 0:LANES] + yb[0:8, 0:LANES] + yc[0:8, 0:LANES])
    W = {**W, **late}
    h1 = _merge_fwd(h, ya, yb, yc, _behind(S["mix_out_g"], token), W["w_out"])
    h2, r = _mlp_fwd(h1, S["mlp_norm_g"], W["w_ff1"], W["w_ff2"])
    h3 = _ple_fwd(h2, p, l, S["ple_norm_g"], W["w_ple_gate"], W["w_ple_proj"])
    return h3, dict(h=h, z=z, ya=ya, yb=yb, yc=yc, xr=xr, xi=xi, h1=h1, r=r, h2=h2), W


def _layer_bwd(g3, p, l, cs, sn, W, S, sp, A, raw, mid_bwd):
    G = {}
    g2, hn3, dgp, de, G["ple_norm_g"] = _ple_bwd(g3, A["h2"], p, l, S["ple_norm_g"], W["w_ple_gate"], W["w_ple_proj"])
    G["w_ple_gate"] = _tn(hn3, dgp, D_MODEL, D_MODEL, bm=1024, bn=1024, dtype=WIRE, name="tn_gate").reshape(N_DEV, -1, D_MODEL)
    G["w_ple_proj"] = _tn(p, de, PLE_DIM, D_MODEL, bm=PLE_DIM, bn=D_MODEL, a_lead=l, split=D_MODEL // N_DEV, dtype=WIRE, name="tn_proj")
    g1, hn, da, G["mlp_norm_g"] = _mlp_bwd(g2, A["h1"], S["mlp_norm_g"], A["r"], W["w_ff1"], W["w_ff2"])
    G["w_ff1"] = _tn(hn, da, D_MODEL, D_FF, bm=D_MODEL, bn=FF_BLOCK, n_major=True, dtype=WIRE, name="tn_ff1")
    G["w_ff2"] = _tn(A["r"], g2, D_FF, D_MODEL, bm=1024, bn=1024, dtype=WIRE, name="tn_ff2").reshape(N_DEV, -1, D_MODEL)
    token = mid_bwd(g1, G)
    dya, dyb, dyc, yn, G["mix_out_g"] = _merge_bwd(g1, A["ya"], A["yb"], A["yc"], _behind(S["mix_out_g"], token), W["w_out"])
    G["w_out"] = _tn(yn, g1, D_MODEL, D_MODEL, bm=1024, bn=1024, dtype=WIRE, name="tn_out").reshape(N_DEV, -1, D_MODEL)
    dzc, y2, da1, da2, dy, arai, dlr, dli, dd = _ssm_bwd(dyc, A["z"], A["xr"], A["xi"], sp["bbd"], sp["pwr"], sp["pwi"], sp["cbd"],
                                                        S["ssm_d"], W["glu_w1"], W["glu_w2"])
    G["glu_w1"] = _tn(y2, da1, C_WIDTH, C_WIDTH, bm=256, bn=256, dtype=WIRE, name="tn_glu1").reshape(N_DEV, -1, C_WIDTH)
    G["glu_w2"] = _tn(y2, da2, C_WIDTH, C_WIDTH, bm=256, bn=256, dtype=WIRE, name="tn_glu2").reshape(N_DEV, -1, C_WIDTH)
    dbbd = _tn(A["z"], arai, C_WIDTH, 2 * N_STATE, bm=256, bn=1024, a_off=5, name="tn_bbd")
    dcr = _tn(dy, A["xr"], C_WIDTH, N_STATE, bm=256, bn=1024, name="tn_cre")
    dci = _tn(dy, A["xi"], C_WIDTH, N_STATE, bm=256, bn=1024, name="tn_cim")
    dare, daim, dldt, dbtr, dbti, dcre, dcim = _ssm_param_bwd(
        raw["are"], raw["aim"], raw["ldt"], raw["are_x"], raw["aim_x"], raw["ldt_x"], raw["btr"], raw["bti"],
        dlr.reshape(C_GROUPS, C_STATE), dli.reshape(C_GROUPS, C_STATE), dbbd, dcr, dci)
    G["ssm_a_re"], G["ssm_a_im"], G["ssm_log_dt"] = dare, daim, dldt[:, 0]
    G["ssm_b_re"] = dbtr.reshape(C_GROUPS, C_GROUP, C_STATE).transpose(0, 2, 1)
    G["ssm_b_im"] = dbti.reshape(C_GROUPS, C_GROUP, C_STATE).transpose(0, 2, 1)
    G["ssm_c_re"] = dcre.reshape(C_GROUPS, C_GROUP, C_STATE)
    G["ssm_c_im"] = dcim.reshape(C_GROUPS, C_GROUP, C_STATE)
    G["ssm_d"] = dd.reshape(C_GROUPS, C_GROUP)
    dzq, dzk, dzv, dqg, dkg, dsk = _attn_bwd(A["z"], cs, sn, S["qg"], S["kg"], S["sinks"], A["yb"], dyb)
    G["q_norm_g"] = dqg[0, :HEAD_DIM] + dqg[0, HEAD_DIM:]
    G["k_norm_g"] = dkg[0, :HEAD_DIM] + dkg[0, HEAD_DIM:]
    G["sinks"] = dsk[:, 0]
    dza, dws, dbs, dlng, dlnb = _gmlp_bwd(A["z"], dya, S["lng"], S["lnb"], S["gmlp_ws"], S["bsx"])
    G["gmlp_ws"] = dws
    G["gmlp_bs"] = dbs[:, :, 0]
    G["gmlp_ln_g"] = dlng.reshape(A_HEADS, 2, HEAD_DIM)[:, 1]
    G["gmlp_ln_b"] = dlnb.reshape(A_HEADS, 2, HEAD_DIM)[:, 1]
    g0, xn, dz, G["attn_norm_g"] = _inproj_bwd(g1, A["h"], S["attn_norm_g"], W["w_in"], dza, dzq, dzk, dzv, dzc)
    G["w_in"] = _tn(xn, dz, D_MODEL, IN_COLS, bm=D_MODEL, bn=IN_COLS, split=IN_COLS // N_DEV, dtype=WIRE, name="tn_in")
    return g0, G


def _small_layouts(P, l):
    def row(a):
        return a.reshape(1, -1)

    zeros = jnp.zeros((A_HEADS, HEAD_DIM), F32)
    S = dict(
        attn_norm_g=row(P["attn_norm_g"][l]), mix_out_g=row(P["mix_out_g"][l]), mlp_norm_g=row(P["mlp_norm_g"][l]),
        ple_norm_g=row(P["ple_norm_g"][l]),
        lng=jnp.stack([zeros, P["gmlp_ln_g"][l]], axis=1).reshape(1, 512),
        lnb=jnp.stack([zeros, P["gmlp_ln_b"][l]], axis=1).reshape(1, 512),
        gmlp_ws=P["gmlp_ws"][l],
        bsx=jnp.broadcast_to(P["gmlp_bs"][l][:, :, None], (A_HEADS, CHUNK, CHUNK)),
        qg=jnp.tile(P["q_norm_g"][l], 2).reshape(1, LANES), kg=jnp.tile(P["k_norm_g"][l], 2).reshape(1, LANES),
        sinks=jnp.broadcast_to(P["sinks"][l][:, None], (8, LANES)),
        ssm_d=row(P["ssm_d"][l]),
    )
    are, aim = P["ssm_a_re"][l], P["ssm_a_im"][l]
    ldt = jnp.broadcast_to(P["ssm_log_dt"][l][:, None], (C_GROUPS, C_STATE))
    raw = dict(
        are=are, aim=aim, ldt=ldt,
        are_x=jnp.repeat(are, C_GROUP, axis=0), aim_x=jnp.repeat(aim, C_GROUP, axis=0), ldt_x=jnp.repeat(ldt, C_GROUP, axis=0),
        btr=P["ssm_b_re"][l].transpose(0, 2, 1).reshape(C_WIDTH, C_STATE), bti=P["ssm_b_im"][l].transpose(0, 2, 1).reshape(C_WIDTH, C_STATE),
        cre=P["ssm_c_re"][l].reshape(C_WIDTH, C_STATE), cim=P["ssm_c_im"][l].reshape(C_WIDTH, C_STATE),
    )
    return S, raw


def _ssm_prep_layer(raw):
    bbd, cbd, pwr, pwi = _ssm_prep(raw["are"].reshape(1, N_STATE), raw["aim"].reshape(1, N_STATE), raw["ldt"].reshape(1, N_STATE),
                                   raw["are_x"], raw["aim_x"], raw["ldt_x"], raw["btr"], raw["bti"], raw["cre"], raw["cim"])
    return dict(bbd=bbd, cbd=cbd, pwr=pwr, pwi=pwi)


def _behind(row, token):
    return row if token is None else row + token[0:1, 0:1]


def _local_step(x, p, positions, target, P, weights_of, mid_bwd, after_bwd):
    inv = 1.0 / (ROPE_THETA ** (jnp.arange(0, HEAD_DIM, 2, dtype=F32) / HEAD_DIM))
    cs, sn = _rope_tables(positions.reshape(-1, 1), jnp.tile(inv, 4).reshape(1, LANES))
    h = x
    acts, smalls, weights = [], [], []
    for l in range(DEPTH):
        W, late_weights, token = weights_of(l, h)
        S, raw = _small_layouts(P, l)
        sp = _ssm_prep_layer(raw)
        h, A, W = _layer_fwd(h, p, l, cs, sn, W, late_weights, {**S, "attn_norm_g": _behind(S["attn_norm_g"], token)}, sp)
        acts.append(A)
        smalls.append((S, raw, sp))
        weights.append(W)
    g, lsum = _loss_head(h, target)
    grads = [None] * DEPTH
    token = None
    for l in reversed(range(DEPTH)):
        S, raw, sp = smalls[l]
        g, grads[l] = _layer_bwd(g, p, l, cs, sn, weights[l], {**S, "ple_norm_g": _behind(S["ple_norm_g"], token)}, sp, acts[l], raw,
                                 functools.partial(mid_bwd, l))
        token = after_bwd(l, g, grads[l])
    return lsum[0, 0], g, grads


def _layer_weights(g):
    layout = dict(
        w_in=_from_col_major, glu_w1=lambda a: a.reshape(C_WIDTH, C_WIDTH), glu_w2=lambda a: a.reshape(C_WIDTH, C_WIDTH),
        w_out=lambda a: a.reshape(D_MODEL, D_MODEL), w_ff1=lambda a: a, w_ff2=lambda a: a.reshape(D_FF, D_MODEL),
        w_ple_gate=lambda a: a.reshape(D_MODEL, D_MODEL), w_ple_proj=_from_col_major)
    return {n: layout[n](a) for n, a in g.items()}


def kernel(x, p, positions, attn_norm_g, w_in, gmlp_ln_g, gmlp_ln_b, gmlp_ws, gmlp_bs, q_norm_g, k_norm_g, sinks, ssm_a_re, ssm_a_im, ssm_log_dt, ssm_b_re, ssm_b_im, ssm_c_re, ssm_c_im, ssm_d, glu_w1, glu_w2, mix_out_g, w_out, mlp_norm_g, w_ff1, w_ff2, ple_norm_g, w_ple_gate, w_ple_proj, loss_target, m_attn_norm_g, m_w_in, m_gmlp_ln_g, m_gmlp_ln_b, m_gmlp_ws, m_gmlp_bs, m_q_norm_g, m_k_norm_g, m_sinks, m_ssm_a_re, m_ssm_a_im, m_ssm_log_dt, m_ssm_b_re, m_ssm_b_im, m_ssm_c_re, m_ssm_c_im, m_ssm_d, m_glu_w1, m_glu_w2, m_mix_out_g, m_w_out, m_mlp_norm_g, m_w_ff1, m_w_ff2, m_ple_norm_g, m_w_ple_gate, m_w_ple_proj, v_attn_norm_g, v_w_in, v_gmlp_ln_g, v_gmlp_ln_b, v_gmlp_ws, v_gmlp_bs, v_q_norm_g, v_k_norm_g, v_sinks, v_ssm_a_re, v_ssm_a_im, v_ssm_log_dt, v_ssm_b_re, v_ssm_b_im, v_ssm_c_re, v_ssm_c_im, v_ssm_d, v_glu_w1, v_glu_w2, v_mix_out_g, v_w_out, v_mlp_norm_g, v_w_ff1, v_w_ff2, v_ple_norm_g, v_w_ple_gate, v_w_ple_proj):
    env = dict(locals())
    P = {n: env[n] for n in WEIGHTS}
    M = {n: env["m_" + n] for n in WEIGHTS}
    V = {n: env["v_" + n] for n in WEIGHTS}
    return _step(x, p, positions, loss_target, P, M, V)


def _step(x, p, positions, loss_target, P, M, V):
    small_shapes = [P[n].shape for n in SMALL]
    me = 4 * lax.axis_index("x") + 2 * lax.axis_index("y") + lax.axis_index("c")
    nothing = jnp.zeros((8, LANES), F32)

    def put(land, own, lead):
        return lax.dynamic_update_slice(land, own.reshape((1,) * len(lead) + own.shape), tuple(lead) + (0,) * own.ndim)

    def gather_start(l, names, after, tag):
        shards = [P[n][l].astype(WIRE) for n in names]
        lands = [lax.empty((N_DEV,) + s.shape, WIRE) for s in shards]
        send, recv, shards, lands, token = _split_start(shards, lands, ["all"] * len(names), 0, after, f"gather_start_{l}{tag}")
        return dict(names=names, send=send, recv=recv, shards=shards, lands=lands, token=token, name=f"gather_wait_{l}{tag}")

    def gather_wait(f, after):
        shards, lands = _split_wait(f["send"], f["recv"], f["shards"], f["lands"], ["all"] * len(f["names"]), 0, after, f["name"])
        return dict(zip(f["names"], [put(ld, sh, (me,)) for sh, ld in zip(shards, lands)]))

    first = gather_start(0, EARLY, nothing, "a")
    flying = {0: (first, gather_start(0, LATE, first["token"], "b"))}

    def weights_of(l, h):
        fa, fb = flying.pop(l)
        got = gather_wait(fa, h)
        if fb is None:
            token = None
            if l + 1 < DEPTH:
                flying[l + 1] = (gather_start(l + 1, SHARDED, got["w_in"], ""), None)
                token = flying[l + 1][0]["token"]
            W = _layer_weights(got)
            return {n: W[n] for n in EARLY}, (lambda after: ({n: W[n] for n in LATE}, None)), token

        def late_weights(after):
            late = gather_wait(fb, after)
            flying[l + 1] = (gather_start(l + 1, SHARDED, late["w_out"], ""), None)
            return _layer_weights(late), flying[l + 1][0]["token"]

        return _layer_weights(got), late_weights, fb["token"]

    grad_lands = {n: lax.empty((N_DEV,) + P[n].shape, WIRE) for n in SHARDED}
    sent = []

    def scatter_start(l, names, parts, lands, tag):
        send, recv, parts, lands, token = _split_start(parts, lands, ["own"] * len(parts), l, nothing, f"scatter_start_{l}{tag}")
        sent.append(dict(l=l, names=names, send=send, recv=recv, parts=parts, lands=lands, name=f"scatter_wait_{l}{tag}"))
        return token

    def scatter_wait(after):
        f = sent.pop(0)
        parts, lands = _split_wait(f["send"], f["recv"], f["parts"], f["lands"], ["own"] * len(f["parts"]), f["l"], after, f["name"])
        lands = [put(ld, lax.dynamic_index_in_dim(part, me, 0, keepdims=False), (me, f["l"])) for part, ld in zip(parts, lands)]
        return dict(zip(f["names"], lands))

    def mid_bwd(l, g1, G):
        if l > 0:
            return None
        grad_lands.update(scatter_wait(g1))
        return scatter_start(0, GRADS_MID, [G[n] for n in GRADS_MID], [grad_lands[n] for n in GRADS_MID], "a")

    small_land = []

    def after_bwd(l, g, G):
        if l > 0:
            if sent:
                grad_lands.update(scatter_wait(g))
            return scatter_start(l, SHARDED, [G[n] for n in SHARDED], [grad_lands[n] for n in SHARDED], "")
        sflat = _pack([jnp.stack([grads_of[k][n] for k in range(DEPTH)]) for n in SMALL], F32)
        sparts = sflat.reshape(N_DEV, -1, FLAT_COLS)
        small_land.append(sflat.shape)
        return scatter_start(0, GRADS_END + ("small",), [G[n] for n in GRADS_END] + [sparts],
                             [grad_lands[n] for n in GRADS_END] + [lax.empty((N_DEV, 1) + sparts.shape[1:], F32)], "b")

    grads_of = {}

    def after_bwd_recording(l, g, G):
        grads_of[l] = G
        grads_of["token"] = after_bwd(l, g, G)
        return grads_of["token"]

    lsum, gx, grads = _local_step(x[0], p[:, 0], positions[0], loss_target[0], P, weights_of, mid_bwd, after_bwd_recording)
    G, delta, new_m, new_v = {}, {}, {}, {}

    def update(names):
        for n in names:
            shp = P[n].shape
            res = _sum_adamw(grad_lands[n].reshape(N_DEV, -1, shp[-1]), *(a.reshape(-1, shp[-1]) for a in (P[n], M[n], V[n])))
            G[n], delta[n], new_m[n], new_v[n] = (a.reshape(shp) for a in res)

    grad_lands.update(scatter_wait(grads_of["token"]))
    update(GRADS_MID)
    last = scatter_wait(sum(new_v[n].reshape(-1, LANES)[:8] for n in GRADS_MID))
    small_parts = last.pop("small")
    grad_lands.update(last)
    update(GRADS_END)
    small_sum = _gather_whole(_sum_slots(small_parts[:, 0])).reshape((1,) + small_land[0])
    res = _sum_adamw(small_sum, _pack([P[n] for n in SMALL], F32), _pack([M[n] for n in SMALL], F32), _pack([V[n] for n in SMALL], F32))
    for dst, flat in zip((G, delta, new_m, new_v), res):
        dst.update(zip(SMALL, _unpack(flat, small_shapes)))
    loss = lax.psum(lsum, ("x", "y", "c"))
    return (loss, gx[None], *[G[n] for n in WEIGHTS], *[delta[n] for n in WEIGHTS], *[new_m[n] for n in WEIGHTS], *[new_v[n] for n in WEIGHTS])
```

```python
import functools
import math

import jax
import jax.numpy as jnp
from jax import lax
from jax.experimental import pallas as pl
from jax.experimental.pallas import tpu as pltpu

F32 = jnp.float32
MXU = jnp.bfloat16
WIRE = jnp.bfloat16

D_MODEL = 1024
DEPTH = 4
HEAD_DIM = 64
A_WIDTH = 256
A_HEADS = 4
CHUNK = 128
B_WIDTH = 512
WINDOW = 128
C_WIDTH = 256
C_GROUP = 16
C_GROUPS = 16
C_STATE = 64
N_STATE = C_GROUPS * C_STATE
IN_COLS = 1536
D_FF = 4096
PLE_DIM = 256
EPS = 1e-6
ROPE_THETA = 10000.0
SCALE = HEAD_DIM ** -0.5
NEG = -1e30
N_DEV = 8

ADAM_LR = 0.001
ADAM_B1 = 0.9
ADAM_B2 = 0.999
ADAM_EPS = 1e-08
ADAM_WD = 0.01
ADAM_STEP = 10

V7X_VMEM_BYTES = 64 * 2 ** 20
VMEM_LIMIT = V7X_VMEM_BYTES - 8 * 2 ** 20
LANES = 128

MESH = pl.DeviceIdType.MESH


def _cp(*sem):
    return pltpu.CompilerParams(dimension_semantics=sem, vmem_limit_bytes=VMEM_LIMIT)


def _sds(shape, dtype=F32):
    return jax.ShapeDtypeStruct(shape, dtype)


def _mm(a, b):
    return jnp.dot(a.astype(MXU), b.astype(MXU), preferred_element_type=F32)


def _mm_nt(a, b):
    return lax.dot_general(a.astype(MXU), b.astype(MXU), (((1,), (1,)), ((), ())), preferred_element_type=F32)


def _mm_tn(a, b):
    return lax.dot_general(a.astype(MXU), b.astype(MXU), (((0,), (0,)), ((), ())), preferred_element_type=F32)


def _lane(shape):
    return lax.broadcasted_iota(jnp.int32, shape, len(shape) - 1)


def _row(shape):
    return lax.broadcasted_iota(jnp.int32, shape, 0)


_GELU_C = math.sqrt(2.0 / math.pi)


def _gelu(x):
    return 0.5 * x * (1.0 + jnp.tanh(_GELU_C * (x + 0.044715 * (x * x * x))))


def _gelu_grad(x):
    t = jnp.tanh(_GELU_C * (x + 0.044715 * (x * x * x)))
    return 0.5 * (1.0 + t) + 0.5 * x * (1.0 - t * t) * (_GELU_C * (1.0 + 3.0 * 0.044715 * (x * x)))


def _sigmoid(x):
    return 1.0 / (1.0 + jnp.exp(-x))


def _rms_stat(x):
    return lax.rsqrt(jnp.mean(x * x, axis=-1, keepdims=True) + EPS)


def _rms_bwd(x, r, g, dy):
    xh = x * r
    dxh = dy * g
    dx = r * (dxh - xh * jnp.mean(dxh * xh, axis=-1, keepdims=True))
    return dx, jnp.sum(dy * xh, axis=0, keepdims=True)


def _tril(w):
    return jnp.where(_row(w.shape) >= _lane(w.shape), w, 0.0)


def _swap64(x):
    return pltpu.roll(x, HEAD_DIM, 1)


def _group_sum64(x, lo):
    s_lo = jnp.sum(jnp.where(lo, x, 0.0), axis=-1, keepdims=True)
    s_hi = jnp.sum(jnp.where(lo, 0.0, x), axis=-1, keepdims=True)
    return jnp.where(lo, s_lo, s_hi)


def _partner(x):
    n = x.shape[-1]
    first = (_lane(x.shape) % HEAD_DIM) < HEAD_DIM // 2
    return jnp.where(first, pltpu.roll(x, n - HEAD_DIM // 2, 1), pltpu.roll(x, HEAD_DIM // 2, 1))


def _rope(y, cs, sn):
    return y * cs + _partner(y) * sn


def _rope_bwd(d, cs, sn):
    return d * cs + _partner(d * sn)


def _qk_norm_rope(x, g, cs, sn):
    lo = _lane(x.shape) < HEAD_DIM
    r = lax.rsqrt(_group_sum64(x * x, lo) * (1.0 / HEAD_DIM) + EPS)
    xh = x * r
    return _rope(xh * g, cs, sn), xh, r


def _qk_norm_rope_bwd(xh, r, g, cs, sn, d):
    lo = _lane(xh.shape) < HEAD_DIM
    dy = _rope_bwd(d, cs, sn)
    dxh = dy * g
    m = _group_sum64(dxh * xh, lo) * (1.0 / HEAD_DIM)
    return r * (dxh - xh * m), jnp.sum(dy * xh, axis=0, keepdims=True)


def _gmlp_head(blk, g, b):
    hi = _lane(blk.shape) >= HEAD_DIM
    mu = jnp.sum(jnp.where(hi, blk, 0.0), axis=-1, keepdims=True) * (1.0 / HEAD_DIM)
    xc = jnp.where(hi, blk - mu, 0.0)
    rstd = lax.rsqrt(jnp.sum(xc * xc, axis=-1, keepdims=True) * (1.0 / HEAD_DIM) + EPS)
    vhat = xc * rstd
    return vhat * g + b, vhat, rstd


def _rope_tables(pos_col, inv_row):
    T = pos_col.shape[0]
    tm = min(T, 1024)

    def body(p_ref, inv_ref, cs_ref, sn_ref):
        ang = p_ref[...].astype(F32) * inv_ref[...]
        s = jnp.sin(ang)
        cs_ref[...] = jnp.cos(ang)
        sn_ref[...] = jnp.where((_lane(ang.shape) % HEAD_DIM) < HEAD_DIM // 2, -s, s)

    blk = pl.BlockSpec((tm, LANES), lambda i: (i, 0))
    return pl.pallas_call(
        body, name="rope_tables", grid=(T // tm,),
        in_specs=[pl.BlockSpec((tm, 1), lambda i: (i, 0)), pl.BlockSpec((1, LANES), lambda i: (0, 0))],
        out_specs=[blk, blk], out_shape=[_sds((T, LANES))] * 2, compiler_params=_cp("parallel"))(pos_col, inv_row)


def _inproj_fwd(h, g, w):
    T = h.shape[0]
    tm = min(T, 512)

    def body(h_ref, g_ref, w_ref, z_ref):
        x = h_ref[...]
        z_ref[...] = _mm(x * _rms_stat(x) * g_ref[...], w_ref[...])

    return pl.pallas_call(
        body, name="inproj_fwd", grid=(T // tm,),
        in_specs=[pl.BlockSpec((tm, D_MODEL), lambda i: (i, 0)), pl.BlockSpec((1, D_MODEL), lambda i: (0, 0)),
                  pl.BlockSpec((D_MODEL, IN_COLS), lambda i: (0, 0))],
        out_specs=pl.BlockSpec((tm, IN_COLS), lambda i: (i, 0)), out_shape=_sds((T, IN_COLS)),
        compiler_params=_cp("parallel"))(h, g, w)


def _inproj_bwd(gres, h, g, w, dza, dzq, dzk, dzv, dzc):
    T = h.shape[0]
    tm = min(T, 512)

    def body(gr_ref, h_ref, g_ref, w_ref, a_ref, q_ref, k_ref, v_ref, c_ref, dh_ref, xn_ref, dz_ref, dg_ref):
        @pl.when(pl.program_id(0) == 0)
        def _():
            dg_ref[...] = jnp.zeros_like(dg_ref)

        x = h_ref[...]
        r = _rms_stat(x)
        gg = g_ref[...]
        dz = jnp.concatenate([a_ref[...], q_ref[...], k_ref[...], v_ref[...], c_ref[...]], axis=1)
        dxn = _mm_nt(dz, w_ref[...])
        dx, dg = _rms_bwd(x, r, gg, dxn)
        dh_ref[...] = gr_ref[...] + dx
        dg_ref[...] += dg
        xn_ref[...] = (x * r * gg).astype(MXU)
        dz_ref[...] = dz.astype(MXU)

    def rows(w_):
        return pl.BlockSpec((tm, w_), lambda i: (i, 0))

    row = pl.BlockSpec((1, D_MODEL), lambda i: (0, 0))
    return pl.pallas_call(
        body, name="inproj_bwd", grid=(T // tm,),
        in_specs=[rows(D_MODEL), rows(D_MODEL), row, pl.BlockSpec((D_MODEL, IN_COLS), lambda i: (0, 0)),
                  rows(512), rows(512), rows(128), rows(128), rows(256)],
        out_specs=[rows(D_MODEL), rows(D_MODEL), rows(IN_COLS), row],
        out_shape=[_sds((T, D_MODEL)), _sds((T, D_MODEL), MXU), _sds((T, IN_COLS), MXU), _sds((1, D_MODEL))],
        compiler_params=_cp("arbitrary"))(gres, h, g, w, dza, dzq, dzk, dzv, dzc)


def _gmlp_fwd(z, lng, lnb, ws, bsx):
    T = z.shape[0]
    tm = min(T, 512)
    nc = tm // CHUNK

    def body(z_ref, g_ref, b_ref, w_ref, bs_ref, ya_ref):
        zg = _gelu(z_ref[...])
        lo = _lane((tm, LANES)) < HEAD_DIM
        prods = []
        for hd in range(A_HEADS):
            sl = slice(hd * LANES, (hd + 1) * LANES)
            blk = zg[:, sl]
            vn, _, _ = _gmlp_head(blk, g_ref[:, sl], b_ref[:, sl])
            wm = _tril(w_ref[hd])
            sv = jnp.concatenate([_mm(wm, vn[c * CHUNK:(c + 1) * CHUNK]) + bs_ref[hd] for c in range(nc)], axis=0)
            prods.append(blk * _swap64(sv))
        ya_ref[:, 0:LANES] = jnp.where(lo, prods[0], _swap64(prods[1]))
        ya_ref[:, LANES:2 * LANES] = jnp.where(lo, prods[2], _swap64(prods[3]))

    row = pl.BlockSpec((1, 512), lambda i: (0, 0))
    mat = pl.BlockSpec((A_HEADS, CHUNK, CHUNK), lambda i: (0, 0, 0))
    return pl.pallas_call(
        body, name="gmlp_fwd", grid=(T // tm,),
        in_specs=[pl.BlockSpec((tm, 512), lambda i: (i, 0)), row, row, mat, mat],
        out_specs=pl.BlockSpec((tm, A_WIDTH), lambda i: (i, 0)), out_shape=_sds((T, A_WIDTH)),
        compiler_params=_cp("parallel"))(z, lng, lnb, ws, bsx)


def _gmlp_bwd(z, dya, lng, lnb, ws, bsx):
    T = z.shape[0]
    tm = min(T, 512)
    nc = tm // CHUNK

    def body(z_ref, dya_ref, g_ref, b_ref, w_ref, bs_ref, dza_ref, dw_ref, dbs_ref, dg_ref, db_ref):
        @pl.when(pl.program_id(0) == 0)
        def _():
            dw_ref[...] = jnp.zeros_like(dw_ref)
            dbs_ref[...] = jnp.zeros_like(dbs_ref)
            dg_ref[...] = jnp.zeros_like(dg_ref)
            db_ref[...] = jnp.zeros_like(db_ref)

        za = z_ref[...]
        zg = _gelu(za)
        gp = _gelu_grad(za)
        lo = _lane((tm, LANES)) < HEAD_DIM
        for hd in range(A_HEADS):
            sl = slice(hd * LANES, (hd + 1) * LANES)
            blk = zg[:, sl]
            g = g_ref[:, sl]
            vn, vhat, rstd = _gmlp_head(blk, g, b_ref[:, sl])
            wm = _tril(w_ref[hd])
            pair = dya_ref[:, (hd // 2) * LANES:(hd // 2 + 1) * LANES]
            dy = jnp.where(lo, pair if hd % 2 == 0 else _swap64(pair), 0.0)
            dsv = _swap64(dy * blk)
            svs, dvns = [], []
            dw = jnp.zeros((CHUNK, CHUNK), F32)
            dbs = jnp.zeros((CHUNK, 1), F32)
            for c in range(nc):
                cs = slice(c * CHUNK, (c + 1) * CHUNK)
                svs.append(_mm(wm, vn[cs]) + bs_ref[hd])
                dw = dw + _mm_nt(dsv[cs], vn[cs])
                dbs = dbs + jnp.sum(dsv[cs], axis=-1, keepdims=True)
                dvns.append(_mm_tn(wm, dsv[cs]))
            sv = jnp.concatenate(svs, axis=0)
            dvn = jnp.concatenate(dvns, axis=0)
            dw_ref[hd] += _tril(dw)
            dbs_ref[hd] += jnp.broadcast_to(dbs, (CHUNK, CHUNK))
            dg_ref[:, sl] += jnp.sum(dvn * vhat, axis=0, keepdims=True)
            db_ref[:, sl] += jnp.sum(dvn, axis=0, keepdims=True)
            du = dy * _swap64(sv)
            dvh = dvn * g
            m1 = jnp.sum(dvh, axis=-1, keepdims=True) * (1.0 / HEAD_DIM)
            m2 = jnp.sum(dvh * vhat, axis=-1, keepdims=True) * (1.0 / HEAD_DIM)
            dv = jnp.where(lo, 0.0, rstd * (dvh - m1 - vhat * m2))
            dza_ref[:, sl] = (du + dv) * gp[:, sl]

    row = pl.BlockSpec((1, 512), lambda i: (0, 0))
    mat = pl.BlockSpec((A_HEADS, CHUNK, CHUNK), lambda i: (0, 0, 0))
    return pl.pallas_call(
        body, name="gmlp_bwd", grid=(T // tm,),
        in_specs=[pl.BlockSpec((tm, 512), lambda i: (i, 0)), pl.BlockSpec((tm, A_WIDTH), lambda i: (i, 0)), row, row, mat, mat],
        out_specs=[pl.BlockSpec((tm, 512), lambda i: (i, 0)), mat, mat, row, row],
        out_shape=[_sds((T, 512)), _sds((A_HEADS, CHUNK, CHUNK)), _sds((A_HEADS, CHUNK, CHUNK)), _sds((1, 512)), _sds((1, 512))],
        compiler_params=_cp("arbitrary"))(z, dya, lng, lnb, ws, bsx)


def _attn_specs(T, tq, tile_of):
    nb = tq // WINDOW

    def prev(i):
        return jnp.maximum(tile_of(i) * nb - 1, 0)

    row = pl.BlockSpec((1, LANES), lambda i: (0, 0))
    return [
        pl.BlockSpec((tq, B_WIDTH), lambda i: (tile_of(i), 1)),
        pl.BlockSpec((tq, LANES), lambda i: (tile_of(i), 8)),
        pl.BlockSpec((tq, LANES), lambda i: (tile_of(i), 9)),
        pl.BlockSpec((WINDOW, LANES), lambda i: (prev(i), 8)),
        pl.BlockSpec((WINDOW, LANES), lambda i: (prev(i), 9)),
        pl.BlockSpec((tq, LANES), lambda i: (tile_of(i), 0)),
        pl.BlockSpec((tq, LANES), lambda i: (tile_of(i), 0)),
        pl.BlockSpec((WINDOW, LANES), lambda i: (prev(i), 0)),
        pl.BlockSpec((WINDOW, LANES), lambda i: (prev(i), 0)),
        row, row,
        pl.BlockSpec((8, LANES), lambda i: (0, 0)),
    ]


def _attn_bias(first):
    qi = lax.broadcasted_iota(jnp.int32, (WINDOW, 2 * WINDOW), 0)
    kj = lax.broadcasted_iota(jnp.int32, (WINDOW, 2 * WINDOW), 1)
    diff = qi + WINDOW - kj
    ok = (diff >= 0) & (diff < WINDOW) & ((kj >= WINDOW) | jnp.logical_not(first))
    return jnp.where(ok, 0.0, NEG)


def _dup_heads(x, lo):
    sw = _swap64(x)
    return jnp.where(lo, x, sw), jnp.where(lo, sw, x)


HEADS_PER_KV = 4


def _stack_heads(x0, x1, lo):
    return jnp.concatenate([jnp.where(lo, x0, 0.0), jnp.where(lo, 0.0, x0), jnp.where(lo, x1, 0.0), jnp.where(lo, 0.0, x1)], axis=0)


def _unstack_heads(x4, lo):
    return (jnp.where(lo, x4[0:WINDOW], x4[WINDOW:2 * WINDOW]), jnp.where(lo, x4[2 * WINDOW:3 * WINDOW], x4[3 * WINDOW:4 * WINDOW]))


def _sink_column(sk_ref, g):
    return jnp.concatenate([jnp.broadcast_to(sk_ref[a:a + 1, 0:1], (WINDOW, 1)) for a in range(HEADS_PER_KV * g, HEADS_PER_KV * (g + 1))], axis=0)


def _attn_probs(q4, kw, bias, sink):
    s = _mm_nt(q4, kw)
    s = (s.reshape(HEADS_PER_KV, WINDOW, 2 * WINDOW) + bias[None]).reshape(HEADS_PER_KV * WINDOW, 2 * WINDOW)
    m = jnp.maximum(jnp.max(s, axis=-1, keepdims=True), sink)
    p = jnp.exp(s - m)
    es = jnp.exp(sink - m)
    inv = 1.0 / (jnp.sum(p, axis=-1, keepdims=True) + es)
    return p * inv, es * inv


def _attn_fwd(z, cs, sn, qg, kg, sinks):
    T = z.shape[0]
    tq = min(T, 512)
    nb = tq // WINDOW

    def body(q_ref, k_ref, v_ref, kp_ref, vp_ref, cs_ref, sn_ref, csp_ref, snp_ref, qg_ref, kg_ref, sk_ref, o_ref):
        i = pl.program_id(0)
        csq, snq = cs_ref[...], sn_ref[...]
        cs_all = jnp.concatenate([csp_ref[...], csq], axis=0)
        sn_all = jnp.concatenate([snp_ref[...], snq], axis=0)
        k_all = jnp.concatenate([kp_ref[...], k_ref[...]], axis=0)
        v_all = jnp.concatenate([vp_ref[...], v_ref[...]], axis=0)
        kr, _, _ = _qk_norm_rope(k_all, kg_ref[...], cs_all, sn_all)
        lo_all = _lane(k_all.shape) < HEAD_DIM
        kd = _dup_heads(kr, lo_all)
        vd = _dup_heads(v_all, lo_all)
        lo = _lane((WINDOW, LANES)) < HEAD_DIM
        qrs = [_qk_norm_rope(q_ref[:, pr * LANES:(pr + 1) * LANES], qg_ref[...], csq, snq)[0] * SCALE for pr in range(4)]
        biases = [_attn_bias(i * nb + b == 0) for b in range(nb)]
        for g in range(2):
            sink = _sink_column(sk_ref, g)
            for b in range(nb):
                bs = slice(b * WINDOW, (b + 1) * WINDOW)
                ws = slice(b * WINDOW, (b + 2) * WINDOW)
                pn, _ = _attn_probs(_stack_heads(qrs[2 * g][bs], qrs[2 * g + 1][bs], lo), kd[g][ws], biases[b], sink)
                o0, o1 = _unstack_heads(_mm(pn, vd[g][ws]), lo)
                o_ref[bs, 2 * g * LANES:(2 * g + 1) * LANES] = o0
                o_ref[bs, (2 * g + 1) * LANES:(2 * g + 2) * LANES] = o1

    return pl.pallas_call(
        body, name="attn_fwd", grid=(T // tq,),
        in_specs=_attn_specs(T, tq, lambda i: i),
        out_specs=pl.BlockSpec((tq, B_WIDTH), lambda i: (i, 0)), out_shape=_sds((T, B_WIDTH)),
        compiler_params=_cp("parallel"))(z, z, z, z, z, cs, sn, cs, sn, qg, kg, sinks)


def _attn_bwd(z, cs, sn, qg, kg, sinks, o, do):
    T = z.shape[0]
    tq = min(T, 512)
    nb = tq // WINDOW
    nt = T // tq
    tk = tq + WINDOW

    def tile_of(i):
        return nt - 1 - i

    def body(q_ref, k_ref, v_ref, kp_ref, vp_ref, cs_ref, sn_ref, csp_ref, snp_ref, qg_ref, kg_ref, sk_ref, o_ref, do_ref,
             dq_ref, dk_ref, dv_ref, dqg_ref, dkg_ref, dsk_ref, acck, accv, ck, cv):
        i = pl.program_id(0)
        ti = nt - 1 - i

        @pl.when(i == 0)
        def _():
            dqg_ref[...] = jnp.zeros_like(dqg_ref)
            dkg_ref[...] = jnp.zeros_like(dkg_ref)
            dsk_ref[...] = jnp.zeros_like(dsk_ref)
            ck[...] = jnp.zeros_like(ck)
            cv[...] = jnp.zeros_like(cv)

        csq, snq = cs_ref[...], sn_ref[...]
        cs_all = jnp.concatenate([csp_ref[...], csq], axis=0)
        sn_all = jnp.concatenate([snp_ref[...], snq], axis=0)
        k_all = jnp.concatenate([kp_ref[...], k_ref[...]], axis=0)
        v_all = jnp.concatenate([vp_ref[...], v_ref[...]], axis=0)
        kr, kh, rk = _qk_norm_rope(k_all, kg_ref[...], cs_all, sn_all)
        lo_all = _lane(k_all.shape) < HEAD_DIM
        kd = _dup_heads(kr, lo_all)
        vd = _dup_heads(v_all, lo_all)
        lo = _lane((WINDOW, LANES)) < HEAD_DIM
        acck[...] = jnp.zeros_like(acck)
        accv[...] = jnp.zeros_like(accv)
        prep = [_qk_norm_rope(q_ref[:, pr * LANES:(pr + 1) * LANES], qg_ref[...], csq, snq) for pr in range(4)]
        biases = [_attn_bias(ti * nb + b == 0) for b in range(nb)]
        dqs = [[None] * nb for _ in range(4)]
        for g in range(2):
            sink = _sink_column(sk_ref, g)
            dsink = jnp.zeros((HEADS_PER_KV * WINDOW, 1), F32)
            for b in range(nb):
                bs = slice(b * WINDOW, (b + 1) * WINDOW)
                ws = slice(b * WINDOW, (b + 2) * WINDOW)
                kw, vw = kd[g][ws], vd[g][ws]
                q4 = _stack_heads(prep[2 * g][0][bs] * SCALE, prep[2 * g + 1][0][bs] * SCALE, lo)
                pn, psink = _attn_probs(q4, kw, biases[b], sink)
                o0, o1 = o_ref[bs, 2 * g * LANES:(2 * g + 1) * LANES], o_ref[bs, (2 * g + 1) * LANES:(2 * g + 2) * LANES]
                do4 = _stack_heads(do_ref[bs, 2 * g * LANES:(2 * g + 1) * LANES], do_ref[bs, (2 * g + 1) * LANES:(2 * g + 2) * LANES], lo)
                delta = jnp.sum(do4 * jnp.concatenate([o0, o0, o1, o1], axis=0), axis=-1, keepdims=True)
                ds = pn * (_mm_nt(do4, vw) - delta)
                dsink = dsink - psink * delta
                dqs[2 * g][b], dqs[2 * g + 1][b] = _unstack_heads(_mm(ds, kw) * SCALE, lo)
                acck[g, ws, :] += _mm_tn(ds, q4)
                accv[g, ws, :] += _mm_tn(pn, do4)
            for hh in range(HEADS_PER_KV):
                a = HEADS_PER_KV * g + hh
                dsk_ref[a:a + 1, :] += jnp.zeros((1, LANES), F32) + jnp.sum(dsink[hh * WINDOW:(hh + 1) * WINDOW])
        for pr in range(4):
            _, qh, rq = prep[pr]
            dx, dg = _qk_norm_rope_bwd(qh, rq, qg_ref[...], csq, snq, jnp.concatenate(dqs[pr], axis=0))
            dq_ref[:, pr * LANES:(pr + 1) * LANES] = dx
            dqg_ref[...] += dg

        def fold(acc):
            f0 = acc[0] + _swap64(acc[0])
            f1 = acc[1] + _swap64(acc[1])
            return jnp.where(lo_all, f0, f1)

        dk_all = fold(acck)
        dv_all = fold(accv)
        pad = jnp.zeros((tq - WINDOW, LANES), F32)
        dk_own = dk_all[WINDOW:] + (jnp.concatenate([pad, ck[...]], axis=0) if nb > 1 else ck[...])
        dv_own = dv_all[WINDOW:] + (jnp.concatenate([pad, cv[...]], axis=0) if nb > 1 else cv[...])
        ck[...] = dk_all[:WINDOW]
        cv[...] = dv_all[:WINDOW]
        dxk, dgk = _qk_norm_rope_bwd(kh[WINDOW:], rk[WINDOW:], kg_ref[...], csq, snq, dk_own)
        dk_ref[...] = dxk
        dkg_ref[...] += dgk
        dv_ref[...] = dv_own

    row = pl.BlockSpec((1, LANES), lambda i: (0, 0))
    return pl.pallas_call(
        body, name="attn_bwd", grid=(nt,),
        in_specs=_attn_specs(T, tq, tile_of) + [pl.BlockSpec((tq, B_WIDTH), lambda i: (tile_of(i), 0))] * 2,
        out_specs=[pl.BlockSpec((tq, B_WIDTH), lambda i: (tile_of(i), 0)), pl.BlockSpec((tq, LANES), lambda i: (tile_of(i), 0)),
                   pl.BlockSpec((tq, LANES), lambda i: (tile_of(i), 0)), row, row, pl.BlockSpec((8, LANES), lambda i: (0, 0))],
        out_shape=[_sds((T, B_WIDTH)), _sds((T, LANES)), _sds((T, LANES)), _sds((1, LANES)), _sds((1, LANES)), _sds((8, LANES))],
        scratch_shapes=[pltpu.VMEM((2, tk, LANES), F32), pltpu.VMEM((2, tk, LANES), F32),
                        pltpu.VMEM((WINDOW, LANES), F32), pltpu.VMEM((WINDOW, LANES), F32)],
        compiler_params=_cp("arbitrary"))(z, z, z, z, z, cs, sn, cs, sn, qg, kg, sinks, o, do)


def _bbar_t(are, aim, ldt, btr, bti):
    lbr, lbi = _lam_bar(are, aim, ldt)
    den = are * are + aim * aim
    nr = lbr - 1.0
    cr = (nr * are + lbi * aim) / den
    ci = (lbi * are - nr * aim) / den
    return cr * btr - ci * bti, cr * bti + ci * btr


def _lam_bar(are, aim, ldt):
    dt = jnp.exp(ldt)
    er = jnp.exp(are * dt)
    return er * jnp.cos(aim * dt), er * jnp.sin(aim * dt)


def _block_diag(x):
    t = jnp.concatenate([x] * C_GROUPS, axis=1)
    return jnp.where(_row(t.shape) // C_GROUP == _lane(t.shape) // C_STATE, t, 0.0)


def _block_diag_fold(m):
    rg = _row((C_WIDTH, C_STATE)) // C_GROUP
    acc = jnp.zeros((C_WIDTH, C_STATE), F32)
    for g in range(C_GROUPS):
        acc = acc + jnp.where(rg == g, m[:, g * C_STATE:(g + 1) * C_STATE], 0.0)
    return acc


def _ssm_prep(are, aim, ldt, are_x, aim_x, ldt_x, btr, bti, cre, cim):
    def body(are_r, aim_r, ldt_r, arex_r, aimx_r, ldtx_r, btr_r, bti_r, cre_r, cim_r, bbd_ref, cbd_ref, pwr_ref, pwi_ref):
        lr, li = _lam_bar(are_r[...], aim_r[...], ldt_r[...])
        cr, ci = lr, li
        for r in range(SCAN_SEG):
            pwr_ref[r:r + 1, :] = cr
            pwi_ref[r:r + 1, :] = ci
            cr, ci = cr * lr - ci * li, cr * li + ci * lr
        br, bi = _bbar_t(arex_r[...], aimx_r[...], ldtx_r[...], btr_r[...], bti_r[...])
        bbd_ref[...] = jnp.concatenate([_block_diag(br), _block_diag(bi)], axis=1).astype(MXU)
        cbd_ref[...] = jnp.concatenate([_block_diag(cre_r[...]), -_block_diag(cim_r[...])], axis=1).astype(MXU)

    return pl.pallas_call(
        body, name="ssm_prep",
        out_shape=[_sds((C_WIDTH, 2 * N_STATE), MXU), _sds((C_WIDTH, 2 * N_STATE), MXU), _sds((SCAN_SEG, N_STATE)), _sds((SCAN_SEG, N_STATE))],
        compiler_params=pltpu.CompilerParams(vmem_limit_bytes=VMEM_LIMIT))(are, aim, ldt, are_x, aim_x, ldt_x, btr, bti, cre, cim)


def _ssm_param_bwd(are, aim, ldt, are_x, aim_x, ldt_x, btr, bti, dlr, dli, dbbd, dcr, dci):
    def body(are_r, aim_r, ldt_r, arex_r, aimx_r, ldtx_r, btr_r, bti_r, dlr_r, dli_r, dbbd_r, dcr_r, dci_r,
             dare_ref, daim_ref, dldt_ref, dbtr_ref, dbti_ref, dcre_ref, dcim_ref):
        _, vjp_l = jax.vjp(_lam_bar, are_r[...], aim_r[...], ldt_r[...])
        da1, di1, dl1 = vjp_l((dlr_r[...], dli_r[...]))
        dbr = _block_diag_fold(dbbd_r[:, 0:N_STATE])
        dbi = _block_diag_fold(dbbd_r[:, N_STATE:2 * N_STATE])
        _, vjp_b = jax.vjp(_bbar_t, arex_r[...], aimx_r[...], ldtx_r[...], btr_r[...], bti_r[...])
        da2, di2, dl2, dbtr, dbti = vjp_b((dbr, dbi))

        def gsum(x):
            return x.reshape(C_GROUPS, C_GROUP, C_STATE).sum(axis=1)

        dare_ref[...] = da1 + gsum(da2)
        daim_ref[...] = di1 + gsum(di2)
        dldt_ref[...] = jnp.broadcast_to(jnp.sum(dl1 + gsum(dl2), axis=-1, keepdims=True), (C_GROUPS, LANES))
        dbtr_ref[...] = dbtr
        dbti_ref[...] = dbti
        dcre_ref[...] = _block_diag_fold(dcr_r[...])
        dcim_ref[...] = -_block_diag_fold(dci_r[...])

    g = _sds((C_GROUPS, C_STATE))
    x = _sds((C_WIDTH, C_STATE))
    return pl.pallas_call(
        body, name="ssm_param_bwd", out_shape=[g, g, _sds((C_GROUPS, LANES)), x, x, x, x],
        compiler_params=pltpu.CompilerParams(vmem_limit_bytes=VMEM_LIMIT))(are, aim, ldt, are_x, aim_x, ldt_x, btr, bti, dlr, dli, dbbd, dcr, dci)


SCAN_TILE = 512
SCAN_SEG = 8


def _scan_tables(pwr_ref, pwi_ref, conj, reverse):
    row = _row((SCAN_SEG, N_STATE))
    shifts = []
    for k in (1, 2, 4):
        keep = (row < SCAN_SEG - k) if reverse else (row >= k)
        ar = jnp.broadcast_to(pwr_ref[k - 1:k, :], (SCAN_SEG, N_STATE))
        ai = jnp.broadcast_to(pwi_ref[k - 1:k, :], (SCAN_SEG, N_STATE)) * conj
        shifts.append((SCAN_SEG - k if reverse else k, jnp.where(keep, ar, 0.0), jnp.where(keep, ai, 0.0)))
    if reverse:
        pr = jnp.concatenate([pwr_ref[SCAN_SEG - 1 - r:SCAN_SEG - r, :] for r in range(SCAN_SEG)], axis=0)
        pi = jnp.concatenate([pwi_ref[SCAN_SEG - 1 - r:SCAN_SEG - r, :] for r in range(SCAN_SEG)], axis=0) * conj
    else:
        pr, pi = pwr_ref[...], pwi_ref[...] * conj
    return shifts, (pr, pi)


def _tile_scan(xr_ref, xi_ref, pwr_ref, pwi_ref, sr, si, conj, reverse):
    shifts, (pr, pi) = _scan_tables(pwr_ref, pwi_ref, conj, reverse)
    groups = xr_ref.shape[0] // SCAN_SEG
    out_row = 0 if reverse else SCAN_SEG - 1

    def step(k, c):
        cr, ci = c
        g = groups - 1 - k if reverse else k
        rows = pl.ds(pl.multiple_of(g * SCAN_SEG, SCAN_SEG), SCAN_SEG)
        xr, xi = xr_ref[rows, :], xi_ref[rows, :]
        for amount, ar, ai in shifts:
            qr, qi = pltpu.roll(xr, amount, 0), pltpu.roll(xi, amount, 0)
            xr, xi = xr + ar * qr - ai * qi, xi + ar * qi + ai * qr
        xr, xi = xr + pr * cr - pi * ci, xi + pr * ci + pi * cr
        xr_ref[rows, :] = xr
        xi_ref[rows, :] = xi
        return xr[out_row:out_row + 1], xi[out_row:out_row + 1]

    cr, ci = lax.fori_loop(0, groups, step, (sr[...], si[...]), unroll=2)
    sr[...] = cr
    si[...] = ci


def _ssm_fwd(z, bbd, pwr, pwi, cbd, dsk, w1, w2):
    T = z.shape[0]
    tt = min(T, SCAN_TILE)

    def body(u_ref, bbd_ref, pwr_ref, pwi_ref, cbd_ref, d_ref, w1_ref, w2_ref, yc_ref, xr_ref, xi_ref, sr, si):
        @pl.when(pl.program_id(0) == 0)
        def _():
            sr[...] = jnp.zeros_like(sr)
            si[...] = jnp.zeros_like(si)

        u = u_ref[...]
        bu = _mm(u, bbd_ref[...])
        xr_ref[...] = bu[:, 0:N_STATE]
        xi_ref[...] = bu[:, N_STATE:2 * N_STATE]
        _tile_scan(xr_ref, xi_ref, pwr_ref, pwi_ref, sr, si, 1.0, False)
        x = jnp.concatenate([xr_ref[...], xi_ref[...]], axis=1)
        y2 = _gelu(_mm_nt(x, cbd_ref[...]) + d_ref[...] * u)
        yc_ref[...] = _mm(y2, w1_ref[...]) * _sigmoid(_mm(y2, w2_ref[...]))

    big = pl.BlockSpec((C_WIDTH, 2 * N_STATE), lambda i: (0, 0))
    tab = pl.BlockSpec((SCAN_SEG, N_STATE), lambda i: (0, 0))
    wsp = pl.BlockSpec((C_WIDTH, C_WIDTH), lambda i: (0, 0))
    xs = pl.BlockSpec((tt, N_STATE), lambda i: (i, 0))
    return pl.pallas_call(
        body, name="ssm_fwd", grid=(T // tt,),
        in_specs=[pl.BlockSpec((tt, C_WIDTH), lambda i: (i, 5)), big, tab, tab, big, pl.BlockSpec((1, C_WIDTH), lambda i: (0, 0)), wsp, wsp],
        out_specs=[pl.BlockSpec((tt, C_WIDTH), lambda i: (i, 0)), xs, xs],
        out_shape=[_sds((T, C_WIDTH)), _sds((T, N_STATE)), _sds((T, N_STATE))],
        scratch_shapes=[pltpu.VMEM((1, N_STATE), F32)] * 2,
        compiler_params=_cp("arbitrary"))(z, bbd, pwr, pwi, cbd, dsk, w1, w2)


def _ssm_bwd(dyc, z, xr, xi, bbd, pwr, pwi, cbd, dsk, w1, w2):
    T = z.shape[0]
    tt = min(T, SCAN_TILE)
    nt = T // tt

    def tile_of(i):
        return nt - 1 - i

    def body(dyc_ref, u_ref, xr_ref, xi_ref, xpr_ref, xpi_ref, bbd_ref, pwr_ref, pwi_ref, cbd_ref, d_ref, w1_ref, w2_ref,
             du_ref, y2_ref, da1_ref, da2_ref, dy_ref, arai_ref, dlr_ref, dli_ref, dd_ref, gr, gi, sr, si):
        i = pl.program_id(0)
        ti = nt - 1 - i

        @pl.when(i == 0)
        def _():
            sr[...] = jnp.zeros_like(sr)
            si[...] = jnp.zeros_like(si)
            dlr_ref[...] = jnp.zeros_like(dlr_ref)
            dli_ref[...] = jnp.zeros_like(dli_ref)
            dd_ref[...] = jnp.zeros_like(dd_ref)

        u = u_ref[...]
        xr_t, xi_t = xr_ref[...], xi_ref[...]
        y = _mm_nt(jnp.concatenate([xr_t, xi_t], axis=1), cbd_ref[...]) + d_ref[...] * u
        y2 = _gelu(y)
        a1 = _mm(y2, w1_ref[...])
        sg = _sigmoid(_mm(y2, w2_ref[...]))
        dyc_t = dyc_ref[...]
        da1 = dyc_t * sg
        da2 = dyc_t * a1 * sg * (1.0 - sg)
        dy = (_mm_nt(da1, w1_ref[...]) + _mm_nt(da2, w2_ref[...])) * _gelu_grad(y)
        gx = _mm(dy, cbd_ref[...])
        gr[...] = gx[:, 0:N_STATE]
        gi[...] = gx[:, N_STATE:2 * N_STATE]
        _tile_scan(gr, gi, pwr_ref, pwi_ref, sr, si, -1.0, True)
        ar, ai = gr[...], gi[...]
        first_row = _row(ar.shape) == 0
        live = jnp.where(ti > 0, 1.0, 0.0)
        xsr = jnp.where(first_row, xpr_ref[7:8, :] * live, pltpu.roll(xr_t, 1, 0))
        xsi = jnp.where(first_row, xpi_ref[7:8, :] * live, pltpu.roll(xi_t, 1, 0))
        dlr_ref[...] += jnp.sum(ar * xsr + ai * xsi, axis=0, keepdims=True)
        dli_ref[...] += jnp.sum(ai * xsr - ar * xsi, axis=0, keepdims=True)
        dd_ref[...] += jnp.sum(dy * u, axis=0, keepdims=True)
        arai = jnp.concatenate([ar, ai], axis=1)
        du_ref[...] = _mm_nt(arai, bbd_ref[...]) + d_ref[...] * dy
        y2_ref[...] = y2.astype(MXU)
        da1_ref[...] = da1.astype(MXU)
        da2_ref[...] = da2.astype(MXU)
        dy_ref[...] = dy.astype(MXU)
        arai_ref[...] = arai.astype(MXU)

    def prev(i):
        return jnp.maximum(tile_of(i) * (tt // 8) - 1, 0)

    big = pl.BlockSpec((C_WIDTH, 2 * N_STATE), lambda i: (0, 0))
    tab = pl.BlockSpec((SCAN_SEG, N_STATE), lambda i: (0, 0))
    srow = pl.BlockSpec((1, N_STATE), lambda i: (0, 0))
    wsp = pl.BlockSpec((C_WIDTH, C_WIDTH), lambda i: (0, 0))
    xs = pl.BlockSpec((tt, N_STATE), lambda i: (tile_of(i), 0))
    xp = pl.BlockSpec((8, N_STATE), lambda i: (prev(i), 0))
    cw = pl.BlockSpec((tt, C_WIDTH), lambda i: (tile_of(i), 0))
    drow = pl.BlockSpec((1, C_WIDTH), lambda i: (0, 0))
    return pl.pallas_call(
        body, name="ssm_bwd", grid=(nt,),
        in_specs=[cw, pl.BlockSpec((tt, C_WIDTH), lambda i: (tile_of(i), 5)), xs, xs, xp, xp, big, tab, tab, big, drow, wsp, wsp],
        out_specs=[cw, cw, cw, cw, cw, pl.BlockSpec((tt, 2 * N_STATE), lambda i: (tile_of(i), 0)), srow, srow, drow],
        out_shape=[_sds((T, C_WIDTH))] + [_sds((T, C_WIDTH), MXU)] * 4 + [_sds((T, 2 * N_STATE), MXU), _sds((1, N_STATE)), _sds((1, N_STATE)), _sds((1, C_WIDTH))],
        scratch_shapes=[pltpu.VMEM((tt, N_STATE), F32)] * 2 + [pltpu.VMEM((1, N_STATE), F32)] * 2,
        compiler_params=_cp("arbitrary"))(dyc, z, xr, xi, xr, xi, bbd, pwr, pwi, cbd, dsk, w1, w2)


_GROUPS = ((0, A_WIDTH), (A_WIDTH, A_WIDTH + B_WIDTH), (A_WIDTH + B_WIDTH, D_MODEL))


def _merge_fwd(h, ya, yb, yc, g, w):
    T = h.shape[0]
    tm = min(T, 512)

    def body(h_ref, a_ref, b_ref, c_ref, g_ref, w_ref, o_ref):
        yn = jnp.concatenate([y * _rms_stat(y) for y in (a_ref[...], b_ref[...], c_ref[...])], axis=1) * g_ref[...]
        o_ref[...] = h_ref[...] + _mm(yn, w_ref[...])

    def rows(w_):
        return pl.BlockSpec((tm, w_), lambda i: (i, 0))

    return pl.pallas_call(
        body, name="merge_fwd", grid=(T // tm,),
        in_specs=[rows(D_MODEL), rows(A_WIDTH), rows(B_WIDTH), rows(C_WIDTH), pl.BlockSpec((1, D_MODEL), lambda i: (0, 0)),
                  pl.BlockSpec((D_MODEL, D_MODEL), lambda i: (0, 0))],
        out_specs=rows(D_MODEL), out_shape=_sds((T, D_MODEL)), compiler_params=_cp("parallel"))(h, ya, yb, yc, g, w)


def _merge_bwd(gres, ya, yb, yc, g, w):
    T = gres.shape[0]
    tm = min(T, 512)

    def body(gr_ref, a_ref, b_ref, c_ref, g_ref, w_ref, da_ref, db_ref, dc_ref, yn_ref, dg_ref):
        @pl.when(pl.program_id(0) == 0)
        def _():
            dg_ref[...] = jnp.zeros_like(dg_ref)

        dyn = _mm_nt(gr_ref[...], w_ref[...])
        yns, dgs = [], []
        for (c0, c1), y_ref, d_ref in zip(_GROUPS, (a_ref, b_ref, c_ref), (da_ref, db_ref, dc_ref)):
            y = y_ref[...]
            r = _rms_stat(y)
            gg = g_ref[:, c0:c1]
            dx, dg = _rms_bwd(y, r, gg, dyn[:, c0:c1])
            d_ref[...] = dx
            dgs.append(dg)
            yns.append(y * r * gg)
        dg_ref[...] += jnp.concatenate(dgs, axis=1)
        yn_ref[...] = jnp.concatenate(yns, axis=1).astype(MXU)

    def rows(w_):
        return pl.BlockSpec((tm, w_), lambda i: (i, 0))

    row = pl.BlockSpec((1, D_MODEL), lambda i: (0, 0))
    return pl.pallas_call(
        body, name="merge_bwd", grid=(T // tm,),
        in_specs=[rows(D_MODEL), rows(A_WIDTH), rows(B_WIDTH), rows(C_WIDTH), row, pl.BlockSpec((D_MODEL, D_MODEL), lambda i: (0, 0))],
        out_specs=[rows(A_WIDTH), rows(B_WIDTH), rows(C_WIDTH), rows(D_MODEL), row],
        out_shape=[_sds((T, A_WIDTH)), _sds((T, B_WIDTH)), _sds((T, C_WIDTH)), _sds((T, D_MODEL), MXU), _sds((1, D_MODEL))],
        compiler_params=_cp("arbitrary"))(gres, ya, yb, yc, g, w)


FF_BLOCK = D_FF // N_DEV


def _load_weights(w1_hbm, w2_hbm, w1, w2, sem):
    @pl.when(pl.program_id(0) == 0)
    def _():
        copies = [pltpu.make_async_copy(w1_hbm.at[j], w1.at[:, pl.ds(j * FF_BLOCK, FF_BLOCK)], sem.at[j]) for j in range(N_DEV)]
        copies.append(pltpu.make_async_copy(w2_hbm, w2, sem.at[N_DEV]))
        for cp in copies:
            cp.start()
        for cp in copies:
            cp.wait()


def _mlp_weight_scratch():
    return [pltpu.VMEM((D_MODEL, D_FF), MXU), pltpu.VMEM((D_FF, D_MODEL), MXU), pltpu.SemaphoreType.DMA((N_DEV + 1,))]


def _mlp_fwd(h, g, w1, w2):
    T = h.shape[0]
    tm = min(T, 512)

    def body(h_ref, g_ref, w1_hbm, w2_hbm, o_ref, r_ref, w1_v, w2_v, sem):
        _load_weights(w1_hbm, w2_hbm, w1_v, w2_v, sem)
        x = h_ref[...]
        a = jnp.maximum(_mm(x * _rms_stat(x) * g_ref[...], w1_v[...]), 0.0)
        r = (a * a).astype(MXU)
        r_ref[...] = r
        o_ref[...] = x + _mm(r, w2_v[...])

    rows = pl.BlockSpec((tm, D_MODEL), lambda i: (i, 0))
    hbm = pl.BlockSpec(memory_space=pl.ANY)
    return pl.pallas_call(
        body, name="mlp_fwd", grid=(T // tm,),
        in_specs=[rows, pl.BlockSpec((1, D_MODEL), lambda i: (0, 0)), hbm, hbm],
        out_specs=[rows, pl.BlockSpec((tm, D_FF), lambda i: (i, 0))],
        out_shape=[_sds((T, D_MODEL)), _sds((T, D_FF), MXU)],
        scratch_shapes=_mlp_weight_scratch(), compiler_params=_cp("arbitrary"))(h, g, w1, w2)


def _mlp_bwd(gres, h, g, r, w1, w2):
    T = h.shape[0]
    tm = min(T, 256)

    def body(gr_ref, h_ref, g_ref, r_ref, w1_hbm, w2_hbm, dh_ref, hn_ref, da_ref, dg_ref, w1_v, w2_v, sem):
        _load_weights(w1_hbm, w2_hbm, w1_v, w2_v, sem)

        @pl.when(pl.program_id(0) == 0)
        def _():
            dg_ref[...] = jnp.zeros_like(dg_ref)

        gr = gr_ref[...]
        da = (_mm_nt(gr, w2_v[...]) * (2.0 * jnp.sqrt(r_ref[...].astype(F32)))).astype(MXU)
        da_ref[...] = da
        x = h_ref[...]
        rs = _rms_stat(x)
        gg = g_ref[...]
        dx, dg = _rms_bwd(x, rs, gg, _mm_nt(da, w1_v[...]))
        dh_ref[...] = gr + dx
        dg_ref[...] += dg
        hn_ref[...] = (x * rs * gg).astype(MXU)

    rows = pl.BlockSpec((tm, D_MODEL), lambda i: (i, 0))
    row = pl.BlockSpec((1, D_MODEL), lambda i: (0, 0))
    wide = pl.BlockSpec((tm, D_FF), lambda i: (i, 0))
    hbm = pl.BlockSpec(memory_space=pl.ANY)
    return pl.pallas_call(
        body, name="mlp_bwd", grid=(T // tm,),
        in_specs=[rows, rows, row, wide, hbm, hbm], out_specs=[rows, rows, wide, row],
        out_shape=[_sds((T, D_MODEL)), _sds((T, D_MODEL), MXU), _sds((T, D_FF), MXU), _sds((1, D_MODEL))],
        scratch_shapes=_mlp_weight_scratch(), compiler_params=_cp("arbitrary"))(gres, h, g, r, w1, w2)


def _ple_fwd(h, p, l, g, wg, wp):
    T = h.shape[0]
    tm = min(T, 512)

    def body(h_ref, p_ref, g_ref, wg_ref, wp_ref, o_ref):
        x = h_ref[...]
        gate = _sigmoid(_mm(x * _rms_stat(x) * g_ref[...], wg_ref[...]))
        o_ref[...] = x + gate * _mm(p_ref[...], wp_ref[...])

    rows = pl.BlockSpec((tm, D_MODEL), lambda i: (i, 0))
    return pl.pallas_call(
        body, name="ple_fwd", grid=(T // tm,),
        in_specs=[rows, pl.BlockSpec((None, tm, PLE_DIM), lambda i: (l, i, 0)), pl.BlockSpec((1, D_MODEL), lambda i: (0, 0)),
                  pl.BlockSpec((D_MODEL, D_MODEL), lambda i: (0, 0)), pl.BlockSpec((PLE_DIM, D_MODEL), lambda i: (0, 0))],
        out_specs=rows, out_shape=_sds((T, D_MODEL)), compiler_params=_cp("parallel"))(h, p, g, wg, wp)


def _ple_bwd(gres, h, p, l, g, wg, wp):
    T = h.shape[0]
    tm = min(T, 512)

    def body(gr_ref, h_ref, p_ref, g_ref, wg_ref, wp_ref, dh_ref, hn_ref, dgp_ref, de_ref, dg_ref):
        @pl.when(pl.program_id(0) == 0)
        def _():
            dg_ref[...] = jnp.zeros_like(dg_ref)

        x = h_ref[...]
        r = _rms_stat(x)
        gg = g_ref[...]
        hn = x * r * gg
        gate = _sigmoid(_mm(hn, wg_ref[...]))
        e = _mm(p_ref[...], wp_ref[...])
        gr = gr_ref[...]
        dgp = gr * e * gate * (1.0 - gate)
        dx, dg = _rms_bwd(x, r, gg, _mm_nt(dgp, wg_ref[...]))
        dh_ref[...] = gr + dx
        dg_ref[...] += dg
        hn_ref[...] = hn.astype(MXU)
        dgp_ref[...] = dgp.astype(MXU)
        de_ref[...] = (gr * gate).astype(MXU)

    rows = pl.BlockSpec((tm, D_MODEL), lambda i: (i, 0))
    row = pl.BlockSpec((1, D_MODEL), lambda i: (0, 0))
    return pl.pallas_call(
        body, name="ple_bwd", grid=(T // tm,),
        in_specs=[rows, rows, pl.BlockSpec((None, tm, PLE_DIM), lambda i: (l, i, 0)), row,
                  pl.BlockSpec((D_MODEL, D_MODEL), lambda i: (0, 0)), pl.BlockSpec((PLE_DIM, D_MODEL), lambda i: (0, 0))],
        out_specs=[rows, rows, rows, rows, row],
        out_shape=[_sds((T, D_MODEL))] + [_sds((T, D_MODEL), MXU)] * 3 + [_sds((1, D_MODEL))],
        compiler_params=_cp("arbitrary"))(gres, h, p, g, wg, wp)


def _loss_head(h, target):
    T = h.shape[0]
    tm = min(T, 1024)

    def body(h_ref, t_ref, dh_ref, l_ref):
        @pl.when(pl.program_id(0) == 0)
        def _():
            l_ref[...] = jnp.zeros_like(l_ref)

        e = h_ref[...] - t_ref[...]
        dh_ref[...] = e * (1.0 / D_MODEL)
        l_ref[...] += jnp.zeros_like(l_ref) + 0.5 * jnp.sum(jnp.mean(e * e, axis=-1, keepdims=True))

    rows = pl.BlockSpec((tm, D_MODEL), lambda i: (i, 0))
    return pl.pallas_call(
        body, name="loss_head", grid=(T // tm,), in_specs=[rows, rows],
        out_specs=[rows, pl.BlockSpec((8, LANES), lambda i: (0, 0))], out_shape=[_sds((T, D_MODEL)), _sds((8, LANES))],
        compiler_params=_cp("arbitrary"))(h, target)


TN_ROWS = 2048

def _tn(a, b, m, n, *, bm, bn, a_off=0, b_off=0, a_lead=None, n_major=False, split=None, dtype=F32, name="tn"):
    T = a.shape[-2]
    tk = min(T, TN_ROWS)
    nk = T // tk
    b_spec = pl.BlockSpec((tk, bn), lambda i, j, k: (k, b_off + j))
    if a_lead is None:
        a_spec = pl.BlockSpec((tk, bm), lambda i, j, k: (k, a_off + i))
    else:
        a_spec = pl.BlockSpec((None, tk, bm), lambda i, j, k: (a_lead, k, a_off + i))
    assert m % bm == 0 and n % bn == 0 and (not n_major or bm == m) and (split is None or (bn == n and n % split == 0))

    def body(a_ref, b_ref, o_ref, acc):
        k = pl.program_id(2)

        @pl.when(k == 0)
        def _():
            acc[...] = jnp.zeros_like(acc)

        acc[...] += _mm_tn(a_ref[...], b_ref[...])

        @pl.when(k == nk - 1)
        def _():
            if split is None:
                o_ref[...] = acc[...].astype(dtype)
            else:
                for d in range(n // split):
                    o_ref[d] = acc[:, d * split:(d + 1) * split].astype(dtype)

    if split is not None:
        out_spec = pl.BlockSpec((n // split, bm, split), lambda i, j, k: (0, i, 0))
        out_shape = _sds((n // split, m, split), dtype)
    elif n_major:
        out_spec = pl.BlockSpec((None, bm, bn), lambda i, j, k: (j, 0, 0))
        out_shape = _sds((n // bn, m, bn), dtype)
    else:
        out_spec = pl.BlockSpec((bm, bn), lambda i, j, k: (i, j))
        out_shape = _sds((m, n), dtype)
    return pl.pallas_call(
        body, name=name, grid=(m // bm, n // bn, nk),
        in_specs=[a_spec, b_spec],
        out_specs=out_spec, out_shape=out_shape, scratch_shapes=[pltpu.VMEM((bm, bn), F32)],
        compiler_params=_cp("parallel", "parallel", "arbitrary"))(a, b)


def _row_tile(R, C):
    for cand in (512, 256, 128, 64, 32, 16, 8):
        if R % cand == 0 and cand * C * 4 <= 2 ** 20:
            return cand
    return R


def _sum_slots(land):
    S, R, C = land.shape
    tr = _row_tile(R, C)

    def body(l_ref, o_ref):
        acc = l_ref[0].astype(F32)
        for s in range(1, S):
            acc = acc + l_ref[s].astype(F32)
        o_ref[...] = acc

    return pl.pallas_call(
        body, name="sum_slots", grid=(R // tr,), in_specs=[pl.BlockSpec((S, tr, C), lambda i: (0, i, 0))],
        out_specs=pl.BlockSpec((tr, C), lambda i: (i, 0)), out_shape=_sds((R, C)), compiler_params=_cp("parallel"))(land)


def _sum_adamw(land, w, m, v):
    R, C = w.shape
    S = land.shape[0]
    tr = _row_tile(R, C)

    def body(l_ref, w_ref, m_ref, v_ref, g_ref, d_ref, nm_ref, nv_ref):
        gg = l_ref[0].astype(F32)
        for s in range(1, S):
            gg = gg + l_ref[s].astype(F32)
        g_ref[...] = gg
        nm = ADAM_B1 * m_ref[...] + (1.0 - ADAM_B1) * gg
        nv = ADAM_B2 * v_ref[...] + (1.0 - ADAM_B2) * (gg * gg)
        m_hat = nm / (1.0 - ADAM_B1 ** ADAM_STEP)
        v_hat = nv / (1.0 - ADAM_B2 ** ADAM_STEP)
        d_ref[...] = -ADAM_LR * (m_hat / (jnp.sqrt(v_hat) + ADAM_EPS) + ADAM_WD * w_ref[...])
        nm_ref[...] = nm
        nv_ref[...] = nv

    blk = pl.BlockSpec((tr, C), lambda i: (i, 0))
    return pl.pallas_call(
        body, name="sum_adamw", grid=(R // tr,), in_specs=[pl.BlockSpec((S, tr, C), lambda i: (0, i, 0))] + [blk] * 3,
        out_specs=[blk] * 4, out_shape=[_sds((R, C))] * 4, compiler_params=_cp("parallel"))(land, w, m, v)


def _all_to_all(pairs, name):
    n = len(pairs)

    def body(*refs):
        srcs, lands = refs[:n], refs[2 * n:3 * n]
        send, recv, loc = refs[3 * n:]
        x, y, c = lax.axis_index("x"), lax.axis_index("y"), lax.axis_index("c")
        me = 4 * x + 2 * y + c
        own = [pltpu.make_async_copy(pairs[t][2](srcs[t], me, me), pairs[t][3](lands[t], me), loc.at[t]) for t in range(n)]
        for cp in own:
            cp.start()
        sends, recvs = [], []
        for k in range(1, N_DEV):
            px, py, pc = x ^ (k >> 2), y ^ ((k >> 1) & 1), c ^ (k & 1)
            peer = 4 * px + 2 * py + pc
            for t in range(n):
                src = pairs[t][2](srcs[t], me, peer)
                cp = pltpu.make_async_remote_copy(src_ref=src, dst_ref=pairs[t][3](lands[t], me), send_sem=send.at[t, k],
                                                  recv_sem=recv.at[t, k], device_id=(px, py, pc), device_id_type=MESH)
                cp.start()
                sends.append(cp)
                recvs.append(pltpu.make_async_remote_copy(src_ref=src, dst_ref=pairs[t][3](lands[t], peer), send_sem=send.at[t, k],
                                                          recv_sem=recv.at[t, k], device_id=(px, py, pc), device_id_type=MESH))
        for cp in recvs:
            cp.wait_recv()
        for cp in sends:
            cp.wait_send()
        for cp in own:
            cp.wait()

    anyspec = pl.BlockSpec(memory_space=pl.ANY)
    lands = [pr[1] for pr in pairs]
    return pl.pallas_call(
        body, name=name, in_specs=[anyspec] * (2 * n), out_specs=[anyspec] * n,
        out_shape=[_sds(a.shape, a.dtype) for a in lands], input_output_aliases={n + t: t for t in range(n)},
        scratch_shapes=[pltpu.SemaphoreType.DMA((n, N_DEV)), pltpu.SemaphoreType.DMA((n, N_DEV)), pltpu.SemaphoreType.DMA((n,))],
        compiler_params=pltpu.CompilerParams(has_side_effects=True))(*[pr[0] for pr in pairs], *lands)


def _gather_whole(src):
    land = lax.empty((N_DEV,) + src.shape, src.dtype)
    return _all_to_all([(src, land, lambda ref, me, peer: ref, lambda ref, sender: ref.at[sender])], "gather_small_grads")[0]


_HBM = pl.BlockSpec(memory_space=pltpu.HBM)
_SEM = pl.BlockSpec(memory_space=pltpu.SEMAPHORE)
_EFFECT = pltpu.SideEffectType.DATAFLOW_SIDE_EFFECTING


def _peers():
    x, y, c = lax.axis_index("x"), lax.axis_index("y"), lax.axis_index("c")
    out = []
    for k in range(1, N_DEV):
        px, py, pc = x ^ (k >> 2), y ^ ((k >> 1) & 1), c ^ (k & 1)
        out.append((k, (px, py, pc), 4 * px + 2 * py + pc))
    return 4 * x + 2 * y + c, out


def _route(mode, layer):
    if mode == "all":
        return (lambda ref, peer: ref), (lambda ref, sender: ref.at[sender])
    return (lambda ref, peer: ref.at[peer]), (lambda ref, sender: ref.at[sender, layer])


def _split_start(srcs, lands, modes, layer, after, name):
    n = len(srcs)
    routes = [_route(m, layer) for m in modes]

    def body(*refs):
        src, land = refs[:n], refs[n:2 * n]
        send, recv, token = refs[2 * n + 1], refs[2 * n + 2], refs[-1]
        me, peers = _peers()
        for k, dev, peer in peers:
            for t, (src_of, dst_of) in enumerate(routes):
                pltpu.make_async_remote_copy(src_ref=src_of(src[t], peer), dst_ref=dst_of(land[t], me), send_sem=send.at[t * N_DEV + k],
                                             recv_sem=recv.at[t * N_DEV + k], device_id=dev, device_id_type=MESH).start()
        token[...] = jnp.zeros_like(token)

    bufs = list(srcs) + list(lands)
    outs = pl.pallas_call(
        body, name=name,
        out_shape=(pltpu.SemaphoreType.DMA((n * N_DEV,)), pltpu.SemaphoreType.DMA((n * N_DEV,)),
                   *[pltpu.HBM(a.shape, a.dtype) for a in bufs], _sds((8, LANES))),
        in_specs=[_HBM] * (2 * n) + [pl.BlockSpec(memory_space=pl.ANY)],
        out_specs=(_SEM, _SEM, *[_HBM] * (2 * n), pl.BlockSpec(memory_space=pltpu.VMEM)),
        input_output_aliases={i: 2 + i for i in range(2 * n)},
        compiler_params=pltpu.CompilerParams(has_side_effects=_EFFECT),
    )(*[pltpu.with_memory_space_constraint(a, pltpu.HBM) for a in bufs], after)
    return outs[0], outs[1], list(outs[2:2 + n]), list(outs[2 + n:2 + 2 * n]), outs[-1]


def _split_wait(send, recv, srcs, lands, modes, layer, after, name):
    n = len(srcs)
    routes = [_route(m, layer) for m in modes]

    def body(*refs):
        src, land = refs[:n], refs[n:2 * n]
        send_r, recv_r = refs[2 * n], refs[2 * n + 1]
        _, peers = _peers()
        for k, dev, peer in peers:
            for t, (src_of, dst_of) in enumerate(routes):
                cp = pltpu.make_async_remote_copy(src_ref=src_of(src[t], peer), dst_ref=dst_of(land[t], peer), send_sem=send_r.at[t * N_DEV + k],
                                                  recv_sem=recv_r.at[t * N_DEV + k], device_id=dev, device_id_type=MESH)
                cp.wait_send()
                cp.wait_recv()

    bufs = list(srcs) + list(lands)
    outs = pl.pallas_call(
        body, name=name, out_shape=tuple(pltpu.HBM(a.shape, a.dtype) for a in bufs),
        in_specs=[_HBM] * (2 * n) + [_SEM, _SEM, pl.BlockSpec(memory_space=pl.ANY)], out_specs=[_HBM] * (2 * n),
        input_output_aliases={i: i for i in range(2 * n)},
        compiler_params=pltpu.CompilerParams(has_side_effects=_EFFECT),
    )(*bufs, send, recv, after)
    return list(outs[:n]), list(outs[n:])


SHARDED = ("w_in", "glu_w1", "glu_w2", "w_out", "w_ff1", "w_ff2", "w_ple_gate", "w_ple_proj")
SMALL = ("attn_norm_g", "gmlp_ln_g", "gmlp_ln_b", "gmlp_ws", "gmlp_bs", "q_norm_g", "k_norm_g", "sinks", "ssm_a_re", "ssm_a_im",
         "ssm_log_dt", "ssm_b_re", "ssm_b_im", "ssm_c_re", "ssm_c_im", "ssm_d", "mix_out_g", "mlp_norm_g", "ple_norm_g")
WEIGHTS = ("attn_norm_g", "w_in", "gmlp_ln_g", "gmlp_ln_b", "gmlp_ws", "gmlp_bs", "q_norm_g", "k_norm_g", "sinks", "ssm_a_re", "ssm_a_im",
           "ssm_log_dt", "ssm_b_re", "ssm_b_im", "ssm_c_re", "ssm_c_im", "ssm_d", "glu_w1", "glu_w2", "mix_out_g", "w_out", "mlp_norm_g",
           "w_ff1", "w_ff2", "ple_norm_g", "w_ple_gate", "w_ple_proj")
FLAT_COLS = 1024


PACK_TILE_ROWS = 8
PACK_ROWS_MULTIPLE = PACK_TILE_ROWS * N_DEV


def _packed_rows(shape):
    return -(-math.prod(shape) // (PACK_TILE_ROWS * FLAT_COLS)) * PACK_TILE_ROWS


def _pack(arrs, dtype):
    blocks = []
    for a in arrs:
        flat = a.astype(dtype).reshape(-1)
        pad = _packed_rows(a.shape) * FLAT_COLS - flat.shape[0]
        if pad:
            flat = jnp.concatenate([flat, jnp.zeros((pad,), dtype)])
        blocks.append(flat.reshape(-1, FLAT_COLS))
    rows = sum(b.shape[0] for b in blocks)
    if rows % PACK_ROWS_MULTIPLE:
        blocks.append(jnp.zeros((PACK_ROWS_MULTIPLE - rows % PACK_ROWS_MULTIPLE, FLAT_COLS), dtype))
    return jnp.concatenate(blocks, axis=0)


def _unpack(flat, shapes):
    out, r = [], 0
    for s in shapes:
        nr = _packed_rows(s)
        out.append(flat[r:r + nr].reshape(-1)[:math.prod(s)].reshape(s))
        r += nr
    return out


def _from_col_major(s):
    n, rows, cs = s.shape
    return s.transpose(1, 0, 2).reshape(rows, n * cs)


EARLY = ("w_in", "glu_w1", "glu_w2")
LATE = ("w_out", "w_ff1", "w_ff2", "w_ple_gate", "w_ple_proj")
GRADS_MID = ("w_ff1", "w_ff2", "w_ple_gate", "w_ple_proj")
GRADS_END = ("w_in", "glu_w1", "glu_w2", "w_out")


def _layer_fwd(h, p, l, cs, sn, W, late_weights, S, sp):
    z = _inproj_fwd(h, S["attn_norm_g"], W["w_in"])
    ya = _gmlp_fwd(z, S["lng"], S["lnb"], S["gmlp_ws"], S["bsx"])
    yb = _attn_fwd(z, cs, sn, S["qg"], S["kg"], S["sinks"])
    yc, xr, xi = _ssm_fwd(z, sp["bbd"], sp["pwr"], sp["pwi"], sp["cbd"], S["ssm_d"], W["glu_w1"], W["glu_w2"])
    late, token = late_weights(ya[0:8, 0:LANES] + yb[0:8, 0:LANES] + yc[0:8, 0:LANES])
    W = {**W, **late}
    h1 = _merge_fwd(h, ya, yb, yc, _behind(S["mix_out_g"], token), W["w_out"])
    h2, r = _mlp_fwd(h1, S["mlp_norm_g"], W["w_ff1"], W["w_ff2"])
    h3 = _ple_fwd(h2, p, l, S["ple_norm_g"], W["w_ple_gate"], W["w_ple_proj"])
    return h3, dict(h=h, z=z, ya=ya, yb=yb, yc=yc, xr=xr, xi=xi, h1=h1, r=r, h2=h2), W


def _layer_bwd(g3, p, l, cs, sn, W, S, sp, A, raw, mid_bwd):
    G = {}
    g2, hn3, dgp, de, G["ple_norm_g"] = _ple_bwd(g3, A["h2"], p, l, S["ple_norm_g"], W["w_ple_gate"], W["w_ple_proj"])
    G["w_ple_gate"] = _tn(hn3, dgp, D_MODEL, D_MODEL, bm=1024, bn=1024, dtype=WIRE, name="tn_gate").reshape(N_DEV, -1, D_MODEL)
    G["w_ple_proj"] = _tn(p, de, PLE_DIM, D_MODEL, bm=PLE_DIM, bn=D_MODEL, a_lead=l, split=D_MODEL // N_DEV, dtype=WIRE, name="tn_proj")
    g1, hn, da, G["mlp_norm_g"] = _mlp_bwd(g2, A["h1"], S["mlp_norm_g"], A["r"], W["w_ff1"], W["w_ff2"])
    G["w_ff1"] = _tn(hn, da, D_MODEL, D_FF, bm=D_MODEL, bn=FF_BLOCK, n_major=True, dtype=WIRE, name="tn_ff1")
    G["w_ff2"] = _tn(A["r"], g2, D_FF, D_MODEL, bm=1024, bn=1024, dtype=WIRE, name="tn_ff2").reshape(N_DEV, -1, D_MODEL)
    token = mid_bwd(g1, G)
    dya, dyb, dyc, yn, G["mix_out_g"] = _merge_bwd(g1, A["ya"], A["yb"], A["yc"], _behind(S["mix_out_g"], token), W["w_out"])
    G["w_out"] = _tn(yn, g1, D_MODEL, D_MODEL, bm=1024, bn=1024, dtype=WIRE, name="tn_out").reshape(N_DEV, -1, D_MODEL)
    dzc, y2, da1, da2, dy, arai, dlr, dli, dd = _ssm_bwd(dyc, A["z"], A["xr"], A["xi"], sp["bbd"], sp["pwr"], sp["pwi"], sp["cbd"],
                                                        S["ssm_d"], W["glu_w1"], W["glu_w2"])
    G["glu_w1"] = _tn(y2, da1, C_WIDTH, C_WIDTH, bm=256, bn=256, dtype=WIRE, name="tn_glu1").reshape(N_DEV, -1, C_WIDTH)
    G["glu_w2"] = _tn(y2, da2, C_WIDTH, C_WIDTH, bm=256, bn=256, dtype=WIRE, name="tn_glu2").reshape(N_DEV, -1, C_WIDTH)
    dbbd = _tn(A["z"], arai, C_WIDTH, 2 * N_STATE, bm=256, bn=1024, a_off=5, name="tn_bbd")
    dcr = _tn(dy, A["xr"], C_WIDTH, N_STATE, bm=256, bn=1024, name="tn_cre")
    dci = _tn(dy, A["xi"], C_WIDTH, N_STATE, bm=256, bn=1024, name="tn_cim")
    dare, daim, dldt, dbtr, dbti, dcre, dcim = _ssm_param_bwd(
        raw["are"], raw["aim"], raw["ldt"], raw["are_x"], raw["aim_x"], raw["ldt_x"], raw["btr"], raw["bti"],
        dlr.reshape(C_GROUPS, C_STATE), dli.reshape(C_GROUPS, C_STATE), dbbd, dcr, dci)
    G["ssm_a_re"], G["ssm_a_im"], G["ssm_log_dt"] = dare, daim, dldt[:, 0]
    G["ssm_b_re"] = dbtr.reshape(C_GROUPS, C_GROUP, C_STATE).transpose(0, 2, 1)
    G["ssm_b_im"] = dbti.reshape(C_GROUPS, C_GROUP, C_STATE).transpose(0, 2, 1)
    G["ssm_c_re"] = dcre.reshape(C_GROUPS, C_GROUP, C_STATE)
    G["ssm_c_im"] = dcim.reshape(C_GROUPS, C_GROUP, C_STATE)
    G["ssm_d"] = dd.reshape(C_GROUPS, C_GROUP)
    dzq, dzk, dzv, dqg, dkg, dsk = _attn_bwd(A["z"], cs, sn, S["qg"], S["kg"], S["sinks"], A["yb"], dyb)
    G["q_norm_g"] = dqg[0, :HEAD_DIM] + dqg[0, HEAD_DIM:]
    G["k_norm_g"] = dkg[0, :HEAD_DIM] + dkg[0, HEAD_DIM:]
    G["sinks"] = dsk[:, 0]
    dza, dws, dbs, dlng, dlnb = _gmlp_bwd(A["z"], dya, S["lng"], S["lnb"], S["gmlp_ws"], S["bsx"])
    G["gmlp_ws"] = dws
    G["gmlp_bs"] = dbs[:, :, 0]
    G["gmlp_ln_g"] = dlng.reshape(A_HEADS, 2, HEAD_DIM)[:, 1]
    G["gmlp_ln_b"] = dlnb.reshape(A_HEADS, 2, HEAD_DIM)[:, 1]
    g0, xn, dz, G["attn_norm_g"] = _inproj_bwd(g1, A["h"], S["attn_norm_g"], W["w_in"], dza, dzq, dzk, dzv, dzc)
    G["w_in"] = _tn(xn, dz, D_MODEL, IN_COLS, bm=D_MODEL, bn=IN_COLS, split=IN_COLS // N_DEV, dtype=WIRE, name="tn_in")
    return g0, G


def _small_layouts(P, l):
    def row(a):
        return a.reshape(1, -1)

    zeros = jnp.zeros((A_HEADS, HEAD_DIM), F32)
    S = dict(
        attn_norm_g=row(P["attn_norm_g"][l]), mix_out_g=row(P["mix_out_g"][l]), mlp_norm_g=row(P["mlp_norm_g"][l]),
        ple_norm_g=row(P["ple_norm_g"][l]),
        lng=jnp.stack([zeros, P["gmlp_ln_g"][l]], axis=1).reshape(1, 512),
        lnb=jnp.stack([zeros, P["gmlp_ln_b"][l]], axis=1).reshape(1, 512),
        gmlp_ws=P["gmlp_ws"][l],
        bsx=jnp.broadcast_to(P["gmlp_bs"][l][:, :, None], (A_HEADS, CHUNK, CHUNK)),
        qg=jnp.tile(P["q_norm_g"][l], 2).reshape(1, LANES), kg=jnp.tile(P["k_norm_g"][l], 2).reshape(1, LANES),
        sinks=jnp.broadcast_to(P["sinks"][l][:, None], (8, LANES)),
        ssm_d=row(P["ssm_d"][l]),
    )
    are, aim = P["ssm_a_re"][l], P["ssm_a_im"][l]
    ldt = jnp.broadcast_to(P["ssm_log_dt"][l][:, None], (C_GROUPS, C_STATE))
    raw = dict(
        are=are, aim=aim, ldt=ldt,
        are_x=jnp.repeat(are, C_GROUP, axis=0), aim_x=jnp.repeat(aim, C_GROUP, axis=0), ldt_x=jnp.repeat(ldt, C_GROUP, axis=0),
        btr=P["ssm_b_re"][l].transpose(0, 2, 1).reshape(C_WIDTH, C_STATE), bti=P["ssm_b_im"][l].transpose(0, 2, 1).reshape(C_WIDTH, C_STATE),
        cre=P["ssm_c_re"][l].reshape(C_WIDTH, C_STATE), cim=P["ssm_c_im"][l].reshape(C_WIDTH, C_STATE),
    )
    return S, raw


def _ssm_prep_layer(raw):
    bbd, cbd, pwr, pwi = _ssm_prep(raw["are"].reshape(1, N_STATE), raw["aim"].reshape(1, N_STATE), raw["ldt"].reshape(1, N_STATE),
                                   raw["are_x"], raw["aim_x"], raw["ldt_x"], raw["btr"], raw["bti"], raw["cre"], raw["cim"])
    return dict(bbd=bbd, cbd=cbd, pwr=pwr, pwi=pwi)


def _behind(row, token):
    return row if token is None else row + token[0:1, 0:1]


def _local_step(x, p, positions, target, P, weights_of, mid_bwd, after_bwd):
    inv = 1.0 / (ROPE_THETA ** (jnp.arange(0, HEAD_DIM, 2, dtype=F32) / HEAD_DIM))
    cs, sn = _rope_tables(positions.reshape(-1, 1), jnp.tile(inv, 4).reshape(1, LANES))
    h = x
    acts, smalls, weights = [], [], []
    for l in range(DEPTH):
        W, late_weights, token = weights_of(l, h)
        S, raw = _small_layouts(P, l)
        sp = _ssm_prep_layer(raw)
        h, A, W = _layer_fwd(h, p, l, cs, sn, W, late_weights, {**S, "attn_norm_g": _behind(S["attn_norm_g"], token)}, sp)
        acts.append(A)
        smalls.append((S, raw, sp))
        weights.append(W)
    g, lsum = _loss_head(h, target)
    grads = [None] * DEPTH
    token = None
    for l in reversed(range(DEPTH)):
        S, raw, sp = smalls[l]
        g, grads[l] = _layer_bwd(g, p, l, cs, sn, weights[l], {**S, "ple_norm_g": _behind(S["ple_norm_g"], token)}, sp, acts[l], raw,
                                 functools.partial(mid_bwd, l))
        token = after_bwd(l, g, grads[l])
    return lsum[0, 0], g, grads


def _layer_weights(g):
    layout = dict(
        w_in=_from_col_major, glu_w1=lambda a: a.reshape(C_WIDTH, C_WIDTH), glu_w2=lambda a: a.reshape(C_WIDTH, C_WIDTH),
        w_out=lambda a: a.reshape(D_MODEL, D_MODEL), w_ff1=lambda a: a, w_ff2=lambda a: a.reshape(D_FF, D_MODEL),
        w_ple_gate=lambda a: a.reshape(D_MODEL, D_MODEL), w_ple_proj=_from_col_major)
    return {n: layout[n](a) for n, a in g.items()}


def kernel(x, p, positions, attn_norm_g, w_in, gmlp_ln_g, gmlp_ln_b, gmlp_ws, gmlp_bs, q_norm_g, k_norm_g, sinks, ssm_a_re, ssm_a_im, ssm_log_dt, ssm_b_re, ssm_b_im, ssm_c_re, ssm_c_im, ssm_d, glu_w1, glu_w2, mix_out_g, w_out, mlp_norm_g, w_ff1, w_ff2, ple_norm_g, w_ple_gate, w_ple_proj, loss_target, m_attn_norm_g, m_w_in, m_gmlp_ln_g, m_gmlp_ln_b, m_gmlp_ws, m_gmlp_bs, m_q_norm_g, m_k_norm_g, m_sinks, m_ssm_a_re, m_ssm_a_im, m_ssm_log_dt, m_ssm_b_re, m_ssm_b_im, m_ssm_c_re, m_ssm_c_im, m_ssm_d, m_glu_w1, m_glu_w2, m_mix_out_g, m_w_out, m_mlp_norm_g, m_w_ff1, m_w_ff2, m_ple_norm_g, m_w_ple_gate, m_w_ple_proj, v_attn_norm_g, v_w_in, v_gmlp_ln_g, v_gmlp_ln_b, v_gmlp_ws, v_gmlp_bs, v_q_norm_g, v_k_norm_g, v_sinks, v_ssm_a_re, v_ssm_a_im, v_ssm_log_dt, v_ssm_b_re, v_ssm_b_im, v_ssm_c_re, v_ssm_c_im, v_ssm_d, v_glu_w1, v_glu_w2, v_mix_out_g, v_w_out, v_mlp_norm_g, v_w_ff1, v_w_ff2, v_ple_norm_g, v_w_ple_gate, v_w_ple_proj):
    env = dict(locals())
    P = {n: env[n] for n in WEIGHTS}
    M = {n: env["m_" + n] for n in WEIGHTS}
    V = {n: env["v_" + n] for n in WEIGHTS}
    return _step(x, p, positions, loss_target, P, M, V)


def _step(x, p, positions, loss_target, P, M, V):
    small_shapes = [P[n].shape for n in SMALL]
    me = 4 * lax.axis_index("x") + 2 * lax.axis_index("y") + lax.axis_index("c")
    nothing = jnp.zeros((8, LANES), F32)

    def put(land, own, lead):
        return lax.dynamic_update_slice(land, own.reshape((1,) * len(lead) + own.shape), tuple(lead) + (0,) * own.ndim)

    def gather_start(l, names, after, tag):
        shards = [P[n][l].astype(WIRE) for n in names]
        lands = [lax.empty((N_DEV,) + s.shape, WIRE) for s in shards]
        send, recv, shards, lands, token = _split_start(shards, lands, ["all"] * len(names), 0, after, f"gather_start_{l}{tag}")
        return dict(names=names, send=send, recv=recv, shards=shards, lands=lands, token=token, name=f"gather_wait_{l}{tag}")

    def gather_wait(f, after):
        shards, lands = _split_wait(f["send"], f["recv"], f["shards"], f["lands"], ["all"] * len(f["names"]), 0, after, f["name"])
        return dict(zip(f["names"], [put(ld, sh, (me,)) for sh, ld in zip(shards, lands)]))

    first = gather_start(0, EARLY, nothing, "a")
    flying = {0: (first, gather_start(0, LATE, first["token"], "b"))}

    def weights_of(l, h):
        fa, fb = flying.pop(l)
        got = gather_wait(fa, h)
        if fb is None:
            token = None
            if l + 1 < DEPTH:
                flying[l + 1] = (gather_start(l + 1, SHARDED, got["w_in"], ""), None)
                token = flying[l + 1][0]["token"]
            W = _layer_weights(got)
            return {n: W[n] for n in EARLY}, (lambda after: ({n: W[n] for n in LATE}, None)), token

        def late_weights(after):
            late = gather_wait(fb, after)
            flying[l + 1] = (gather_start(l + 1, SHARDED, late["w_out"], ""), None)
            return _layer_weights(late), flying[l + 1][0]["token"]

        return _layer_weights(got), late_weights, fb["token"]

    grad_lands = {n: lax.empty((N_DEV,) + P[n].shape, WIRE) for n in SHARDED}
    sent = []

    def scatter_start(l, names, parts, lands, tag):
        send, recv, parts, lands, token = _split_start(parts, lands, ["own"] * len(parts), l, nothing, f"scatter_start_{l}{tag}")
        sent.append(dict(l=l, names=names, send=send, recv=recv, parts=parts, lands=lands, name=f"scatter_wait_{l}{tag}"))
        return token

    def scatter_wait(after):
        f = sent.pop(0)
        parts, lands = _split_wait(f["send"], f["recv"], f["parts"], f["lands"], ["own"] * len(f["parts"]), f["l"], after, f["name"])
        lands = [put(ld, lax.dynamic_index_in_dim(part, me, 0, keepdims=False), (me, f["l"])) for part, ld in zip(parts, lands)]
        return dict(zip(f["names"], lands))

    def mid_bwd(l, g1, G):
        if l > 0:
            return None
        grad_lands.update(scatter_wait(g1))
        return scatter_start(0, GRADS_MID, [G[n] for n in GRADS_MID], [grad_lands[n] for n in GRADS_MID], "a")

    small_land = []

    def after_bwd(l, g, G):
        if l > 0:
            if sent:
                grad_lands.update(scatter_wait(g))
            return scatter_start(l, SHARDED, [G[n] for n in SHARDED], [grad_lands[n] for n in SHARDED], "")
        sflat = _pack([jnp.stack([grads_of[k][n] for k in range(DEPTH)]) for n in SMALL], F32)
        sparts = sflat.reshape(N_DEV, -1, FLAT_COLS)
        small_land.append(sflat.shape)
        return scatter_start(0, GRADS_END + ("small",), [G[n] for n in GRADS_END] + [sparts],
                             [grad_lands[n] for n in GRADS_END] + [lax.empty((N_DEV, 1) + sparts.shape[1:], F32)], "b")

    grads_of = {}

    def after_bwd_recording(l, g, G):
        grads_of[l] = G
        grads_of["token"] = after_bwd(l, g, G)
        return grads_of["token"]

    lsum, gx, grads = _local_step(x[0], p[:, 0], positions[0], loss_target[0], P, weights_of, mid_bwd, after_bwd_recording)
    G, delta, new_m, new_v = {}, {}, {}, {}

    def update(names):
        for n in names:
            shp = P[n].shape
            res = _sum_adamw(grad_lands[n].reshape(N_DEV, -1, shp[-1]), *(a.reshape(-1, shp[-1]) for a in (P[n], M[n], V[n])))
            G[n], delta[n], new_m[n], new_v[n] = (a.reshape(shp) for a in res)

    grad_lands.update(scatter_wait(grads_of["token"]))
    update(GRADS_MID)
    last = scatter_wait(sum(new_v[n].reshape(-1, LANES)[:8] for n in GRADS_MID))
    small_parts = last.pop("small")
    grad_lands.update(last)
    update(GRADS_END)
    small_sum = _gather_whole(_sum_slots(small_parts[:, 0])).reshape((1,) + small_land[0])
    res = _sum_adamw(small_sum, _pack([P[n] for n in SMALL], F32), _pack([M[n] for n in SMALL], F32), _pack([V[n] for n in SMALL], F32))
    for dst, flat in zip((G, delta, new_m, new_v), res):
        dst.update(zip(SMALL, _unpack(flat, small_shapes)))
    loss = lax.psum(lsum, ("x", "y", "c"))
    return (loss, gx[None], *[G[n] for n in WEIGHTS], *[delta[n] for n in WEIGHTS], *[new_m[n] for n in WEIGHTS], *[new_v[n] for n in WEIGHTS])
```

```python
import functools
import math

import jax
import jax.numpy as jnp
from jax import lax
from jax.experimental import pallas as pl
from jax.experimental.pallas import tpu as pltpu

F32 = jnp.float32
MXU = jnp.bfloat16
WIRE = jnp.bfloat16

D_MODEL = 1024
DEPTH = 4
HEAD_DIM = 64
A_WIDTH = 256
A_HEADS = 4
CHUNK = 128
B_WIDTH = 512
WINDOW = 128
C_WIDTH = 256
C_GROUP = 16
C_GROUPS = 16
C_STATE = 64
N_STATE = C_GROUPS * C_STATE
IN_COLS = 1536
D_FF = 4096
PLE_DIM = 256
EPS = 1e-6
ROPE_THETA = 10000.0
SCALE = HEAD_DIM ** -0.5
NEG = -1e30
N_DEV = 8

ADAM_LR = 0.001
ADAM_B1 = 0.9
ADAM_B2 = 0.999
ADAM_EPS = 1e-08
ADAM_WD = 0.01
ADAM_STEP = 10

V7X_VMEM_BYTES = 64 * 2 ** 20
VMEM_LIMIT = V7X_VMEM_BYTES - 8 * 2 ** 20
LANES = 128

MESH = pl.DeviceIdType.MESH


def _cp(*sem):
    return pltpu.CompilerParams(dimension_semantics=sem, vmem_limit_bytes=VMEM_LIMIT)


def _sds(shape, dtype=F32):
    return jax.ShapeDtypeStruct(shape, dtype)


def _mm(a, b):
    return jnp.dot(a.astype(MXU), b.astype(MXU), preferred_element_type=F32)


def _mm_nt(a, b):
    return lax.dot_general(a.astype(MXU), b.astype(MXU), (((1,), (1,)), ((), ())), preferred_element_type=F32)


def _mm_tn(a, b):
    return lax.dot_general(a.astype(MXU), b.astype(MXU), (((0,), (0,)), ((), ())), preferred_element_type=F32)


def _lane(shape):
    return lax.broadcasted_iota(jnp.int32, shape, len(shape) - 1)


def _row(shape):
    return lax.broadcasted_iota(jnp.int32, shape, 0)


_GELU_C = math.sqrt(2.0 / math.pi)


def _gelu(x):
    return 0.5 * x * (1.0 + jnp.tanh(_GELU_C * (x + 0.044715 * (x * x * x))))


def _gelu_grad(x):
    t = jnp.tanh(_GELU_C * (x + 0.044715 * (x * x * x)))
    return 0.5 * (1.0 + t) + 0.5 * x * (1.0 - t * t) * (_GELU_C * (1.0 + 3.0 * 0.044715 * (x * x)))


def _sigmoid(x):
    return 1.0 / (1.0 + jnp.exp(-x))


def _rms_stat(x):
    return lax.rsqrt(jnp.mean(x * x, axis=-1, keepdims=True) + EPS)


def _rms_bwd(x, r, g, dy):
    xh = x * r
    dxh = dy * g
    dx = r * (dxh - xh * jnp.mean(dxh * xh, axis=-1, keepdims=True))
    return dx, jnp.sum(dy * xh, axis=0, keepdims=True)


def _tril(w):
    return jnp.where(_row(w.shape) >= _lane(w.shape), w, 0.0)


def _swap64(x):
    return pltpu.roll(x, HEAD_DIM, 1)


def _group_sum64(x, lo):
    s_lo = jnp.sum(jnp.where(lo, x, 0.0), axis=-1, keepdims=True)
    s_hi = jnp.sum(jnp.where(lo, 0.0, x), axis=-1, keepdims=True)
    return jnp.where(lo, s_lo, s_hi)


def _partner(x):
    n = x.shape[-1]
    first = (_lane(x.shape) % HEAD_DIM) < HEAD_DIM // 2
    return jnp.where(first, pltpu.roll(x, n - HEAD_DIM // 2, 1), pltpu.roll(x, HEAD_DIM // 2, 1))


def _rope(y, cs, sn):
    return y * cs + _partner(y) * sn


def _rope_bwd(d, cs, sn):
    return d * cs + _partner(d * sn)


def _qk_norm_rope(x, g, cs, sn):
    lo = _lane(x.shape) < HEAD_DIM
    r = lax.rsqrt(_group_sum64(x * x, lo) * (1.0 / HEAD_DIM) + EPS)
    xh = x * r
    return _rope(xh * g, cs, sn), xh, r


def _qk_norm_rope_bwd(xh, r, g, cs, sn, d):
    lo = _lane(xh.shape) < HEAD_DIM
    dy = _rope_bwd(d, cs, sn)
    dxh = dy * g
    m = _group_sum64(dxh * xh, lo) * (1.0 / HEAD_DIM)
    return r * (dxh - xh * m), jnp.sum(dy * xh, axis=0, keepdims=True)


def _gmlp_head(blk, g, b):
    hi = _lane(blk.shape) >= HEAD_DIM
    mu = jnp.sum(jnp.where(hi, blk, 0.0), axis=-1, keepdims=True) * (1.0 / HEAD_DIM)
    xc = jnp.where(hi, blk - mu, 0.0)
    rstd = lax.rsqrt(jnp.sum(xc * xc, axis=-1, keepdims=True) * (1.0 / HEAD_DIM) + EPS)
    vhat = xc * rstd
    return vhat * g + b, vhat, rstd


def _rope_tables(pos_col, inv_row):
    T = pos_col.shape[0]
    tm = min(T, 1024)

    def body(p_ref, inv_ref, cs_ref, sn_ref):
        ang = p_ref[...].astype(F32) * inv_ref[...]
        s = jnp.sin(ang)
        cs_ref[...] = jnp.cos(ang)
        sn_ref[...] = jnp.where((_lane(ang.shape) % HEAD_DIM) < HEAD_DIM // 2, -s, s)

    blk = pl.BlockSpec((tm, LANES), lambda i: (i, 0))
    return pl.pallas_call(
        body, name="rope_tables", grid=(T // tm,),
        in_specs=[pl.BlockSpec((tm, 1), lambda i: (i, 0)), pl.BlockSpec((1, LANES), lambda i: (0, 0))],
        out_specs=[blk, blk], out_shape=[_sds((T, LANES))] * 2, compiler_params=_cp("parallel"))(pos_col, inv_row)


def _inproj_fwd(h, g, w):
    T = h.shape[0]
    tm = min(T, 512)

    def body(h_ref, g_ref, w_ref, z_ref):
        x = h_ref[...]
        z_ref[...] = _mm(x * _rms_stat(x) * g_ref[...], w_ref[...])

    return pl.pallas_call(
        body, name="inproj_fwd", grid=(T // tm,),
        in_specs=[pl.BlockSpec((tm, D_MODEL), lambda i: (i, 0)), pl.BlockSpec((1, D_MODEL), lambda i: (0, 0)),
                  pl.BlockSpec((D_MODEL, IN_COLS), lambda i: (0, 0))],
        out_specs=pl.BlockSpec((tm, IN_COLS), lambda i: (i, 0)), out_shape=_sds((T, IN_COLS)),
        compiler_params=_cp("parallel"))(h, g, w)


def _inproj_bwd(gres, h, g, w, dza, dzq, dzk, dzv, dzc):
    T = h.shape[0]
    tm = min(T, 512)

    def body(gr_ref, h_ref, g_ref, w_ref, a_ref, q_ref, k_ref, v_ref, c_ref, dh_ref, xn_ref, dz_ref, dg_ref):
        @pl.when(pl.program_id(0) == 0)
        def _():
            dg_ref[...] = jnp.zeros_like(dg_ref)

        x = h_ref[...]
        r = _rms_stat(x)
        gg = g_ref[...]
        dz = jnp.concatenate([a_ref[...], q_ref[...], k_ref[...], v_ref[...], c_ref[...]], axis=1)
        dxn = _mm_nt(dz, w_ref[...])
        dx, dg = _rms_bwd(x, r, gg, dxn)
        dh_ref[...] = gr_ref[...] + dx
        dg_ref[...] += dg
        xn_ref[...] = (x * r * gg).astype(MXU)
        dz_ref[...] = dz.astype(MXU)

    def rows(w_):
        return pl.BlockSpec((tm, w_), lambda i: (i, 0))

    row = pl.BlockSpec((1, D_MODEL), lambda i: (0, 0))
    return pl.pallas_call(
        body, name="inproj_bwd", grid=(T // tm,),
        in_specs=[rows(D_MODEL), rows(D_MODEL), row, pl.BlockSpec((D_MODEL, IN_COLS), lambda i: (0, 0)),
                  rows(512), rows(512), rows(128), rows(128), rows(256)],
        out_specs=[rows(D_MODEL), rows(D_MODEL), rows(IN_COLS), row],
        out_shape=[_sds((T, D_MODEL)), _sds((T, D_MODEL), MXU), _sds((T, IN_COLS), MXU), _sds((1, D_MODEL))],
        compiler_params=_cp("arbitrary"))(gres, h, g, w, dza, dzq, dzk, dzv, dzc)


def _gmlp_fwd(z, lng, lnb, ws, bsx):
    T = z.shape[0]
    tm = min(T, 512)
    nc = tm // CHUNK

    def body(z_ref, g_ref, b_ref, w_ref, bs_ref, ya_ref):
        zg = _gelu(z_ref[...])
        lo = _lane((tm, LANES)) < HEAD_DIM
        prods = []
        for hd in range(A_HEADS):
            sl = slice(hd * LANES, (hd + 1) * LANES)
            blk = zg[:, sl]
            vn, _, _ = _gmlp_head(blk, g_ref[:, sl], b_ref[:, sl])
            wm = _tril(w_ref[hd])
            sv = jnp.concatenate([_mm(wm, vn[c * CHUNK:(c + 1) * CHUNK]) + bs_ref[hd] for c in range(nc)], axis=0)
            prods.append(blk * _swap64(sv))
        ya_ref[:, 0:LANES] = jnp.where(lo, prods[0], _swap64(prods[1]))
        ya_ref[:, LANES:2 * LANES] = jnp.where(lo, prods[2], _swap64(prods[3]))

    row = pl.BlockSpec((1, 512), lambda i: (0, 0))
    mat = pl.BlockSpec((A_HEADS, CHUNK, CHUNK), lambda i: (0, 0, 0))
    return pl.pallas_call(
        body, name="gmlp_fwd", grid=(T // tm,),
        in_specs=[pl.BlockSpec((tm, 512), lambda i: (i, 0)), row, row, mat, mat],
        out_specs=pl.BlockSpec((tm, A_WIDTH), lambda i: (i, 0)), out_shape=_sds((T, A_WIDTH)),
        compiler_params=_cp("parallel"))(z, lng, lnb, ws, bsx)


def _gmlp_bwd(z, dya, lng, lnb, ws, bsx):
    T = z.shape[0]
    tm = min(T, 512)
    nc = tm // CHUNK

    def body(z_ref, dya_ref, g_ref, b_ref, w_ref, bs_ref, dza_ref, dw_ref, dbs_ref, dg_ref, db_ref):
        @pl.when(pl.program_id(0) == 0)
        def _():
            dw_ref[...] = jnp.zeros_like(dw_ref)
            dbs_ref[...] = jnp.zeros_like(dbs_ref)
            dg_ref[...] = jnp.zeros_like(dg_ref)
            db_ref[...] = jnp.zeros_like(db_ref)

        za = z_ref[...]
        zg = _gelu(za)
        gp = _gelu_grad(za)
        lo = _lane((tm, LANES)) < HEAD_DIM
        for hd in range(A_HEADS):
            sl = slice(hd * LANES, (hd + 1) * LANES)
            blk = zg[:, sl]
            g = g_ref[:, sl]
            vn, vhat, rstd = _gmlp_head(blk, g, b_ref[:, sl])
            wm = _tril(w_ref[hd])
            pair = dya_ref[:, (hd // 2) * LANES:(hd // 2 + 1) * LANES]
            dy = jnp.where(lo, pair if hd % 2 == 0 else _swap64(pair), 0.0)
            dsv = _swap64(dy * blk)
            svs, dvns = [], []
            dw = jnp.zeros((CHUNK, CHUNK), F32)
            dbs = jnp.zeros((CHUNK, 1), F32)
            for c in range(nc):
                cs = slice(c * CHUNK, (c + 1) * CHUNK)
                svs.append(_mm(wm, vn[cs]) + bs_ref[hd])
                dw = dw + _mm_nt(dsv[cs], vn[cs])
                dbs = dbs + jnp.sum(dsv[cs], axis=-1, keepdims=True)
                dvns.append(_mm_tn(wm, dsv[cs]))
            sv = jnp.concatenate(svs, axis=0)
            dvn = jnp.concatenate(dvns, axis=0)
            dw_ref[hd] += _tril(dw)
            dbs_ref[hd] += jnp.broadcast_to(dbs, (CHUNK, CHUNK))
            dg_ref[:, sl] += jnp.sum(dvn * vhat, axis=0, keepdims=True)
            db_ref[:, sl] += jnp.sum(dvn, axis=0, keepdims=True)
            du = dy * _swap64(sv)
            dvh = dvn * g
            m1 = jnp.sum(dvh, axis=-1, keepdims=True) * (1.0 / HEAD_DIM)
            m2 = jnp.sum(dvh * vhat, axis=-1, keepdims=True) * (1.0 / HEAD_DIM)
            dv = jnp.where(lo, 0.0, rstd * (dvh - m1 - vhat * m2))
            dza_ref[:, sl] = (du + dv) * gp[:, sl]

    row = pl.BlockSpec((1, 512), lambda i: (0, 0))
    mat = pl.BlockSpec((A_HEADS, CHUNK, CHUNK), lambda i: (0, 0, 0))
    return pl.pallas_call(
        body, name="gmlp_bwd", grid=(T // tm,),
        in_specs=[pl.BlockSpec((tm, 512), lambda i: (i, 0)), pl.BlockSpec((tm, A_WIDTH), lambda i: (i, 0)), row, row, mat, mat],
        out_specs=[pl.BlockSpec((tm, 512), lambda i: (i, 0)), mat, mat, row, row],
        out_shape=[_sds((T, 512)), _sds((A_HEADS, CHUNK, CHUNK)), _sds((A_HEADS, CHUNK, CHUNK)), _sds((1, 512)), _sds((1, 512))],
        compiler_params=_cp("arbitrary"))(z, dya, lng, lnb, ws, bsx)


def _attn_specs(T, tq, tile_of):
    nb = tq // WINDOW

    def prev(i):
        return jnp.maximum(tile_of(i) * nb - 1, 0)

    row = pl.BlockSpec((1, LANES), lambda i: (0, 0))
    return [
        pl.BlockSpec((tq, B_WIDTH), lambda i: (tile_of(i), 1)),
        pl.BlockSpec((tq, LANES), lambda i: (tile_of(i), 8)),
        pl.BlockSpec((tq, LANES), lambda i: (tile_of(i), 9)),
        pl.BlockSpec((WINDOW, LANES), lambda i: (prev(i), 8)),
        pl.BlockSpec((WINDOW, LANES), lambda i: (prev(i), 9)),
        pl.BlockSpec((tq, LANES), lambda i: (tile_of(i), 0)),
        pl.BlockSpec((tq, LANES), lambda i: (tile_of(i), 0)),
        pl.BlockSpec((WINDOW, LANES), lambda i: (prev(i), 0)),
        pl.BlockSpec((WINDOW, LANES), lambda i: (prev(i), 0)),
        row, row,
        pl.BlockSpec((8, LANES), lambda i: (0, 0)),
    ]


def _attn_bias(first):
    qi = lax.broadcasted_iota(jnp.int32, (WINDOW, 2 * WINDOW), 0)
    kj = lax.broadcasted_iota(jnp.int32, (WINDOW, 2 * WINDOW), 1)
    diff = qi + WINDOW - kj
    ok = (diff >= 0) & (diff < WINDOW) & ((kj >= WINDOW) | jnp.logical_not(first))
    return jnp.where(ok, 0.0, NEG)


def _dup_heads(x, lo):
    sw = _swap64(x)
    return jnp.where(lo, x, sw), jnp.where(lo, sw, x)


HEADS_PER_KV = 4


def _stack_heads(x0, x1, lo):
    return jnp.concatenate([jnp.where(lo, x0, 0.0), jnp.where(lo, 0.0, x0), jnp.where(lo, x1, 0.0), jnp.where(lo, 0.0, x1)], axis=0)


def _unstack_heads(x4, lo):
    return (jnp.where(lo, x4[0:WINDOW], x4[WINDOW:2 * WINDOW]), jnp.where(lo, x4[2 * WINDOW:3 * WINDOW], x4[3 * WINDOW:4 * WINDOW]))


def _sink_column(sk_ref, g):
    return jnp.concatenate([jnp.broadcast_to(sk_ref[a:a + 1, 0:1], (WINDOW, 1)) for a in range(HEADS_PER_KV * g, HEADS_PER_KV * (g + 1))], axis=0)


def _attn_probs(q4, kw, bias, sink):
    s = _mm_nt(q4, kw)
    s = (s.reshape(HEADS_PER_KV, WINDOW, 2 * WINDOW) + bias[None]).reshape(HEADS_PER_KV * WINDOW, 2 * WINDOW)
    m = jnp.maximum(jnp.max(s, axis=-1, keepdims=True), sink)
    p = jnp.exp(s - m)
    es = jnp.exp(sink - m)
    inv = 1.0 / (jnp.sum(p, axis=-1, keepdims=True) + es)
    return p * inv, es * inv


def _attn_fwd(z, cs, sn, qg, kg, sinks):
    T = z.shape[0]
    tq = min(T, 512)
    nb = tq // WINDOW

    def body(q_ref, k_ref, v_ref, kp_ref, vp_ref, cs_ref, sn_ref, csp_ref, snp_ref, qg_ref, kg_ref, sk_ref, o_ref):
        i = pl.program_id(0)
        csq, snq = cs_ref[...], sn_ref[...]
        cs_all = jnp.concatenate([csp_ref[...], csq], axis=0)
        sn_all = jnp.concatenate([snp_ref[...], snq], axis=0)
        k_all = jnp.concatenate([kp_ref[...], k_ref[...]], axis=0)
        v_all = jnp.concatenate([vp_ref[...], v_ref[...]], axis=0)
        kr, _, _ = _qk_norm_rope(k_all, kg_ref[...], cs_all, sn_all)
        lo_all = _lane(k_all.shape) < HEAD_DIM
        kd = _dup_heads(kr, lo_all)
        vd = _dup_heads(v_all, lo_all)
        lo = _lane((WINDOW, LANES)) < HEAD_DIM
        qrs = [_qk_norm_rope(q_ref[:, pr * LANES:(pr + 1) * LANES], qg_ref[...], csq, snq)[0] * SCALE for pr in range(4)]
        biases = [_attn_bias(i * nb + b == 0) for b in range(nb)]
        for g in range(2):
            sink = _sink_column(sk_ref, g)
            for b in range(nb):
                bs = slice(b * WINDOW, (b + 1) * WINDOW)
                ws = slice(b * WINDOW, (b + 2) * WINDOW)
                pn, _ = _attn_probs(_stack_heads(qrs[2 * g][bs], qrs[2 * g + 1][bs], lo), kd[g][ws], biases[b], sink)
                o0, o1 = _unstack_heads(_mm(pn, vd[g][ws]), lo)
                o_ref[bs, 2 * g * LANES:(2 * g + 1) * LANES] = o0
                o_ref[bs, (2 * g + 1) * LANES:(2 * g + 2) * LANES] = o1

    return pl.pallas_call(
        body, name="attn_fwd", grid=(T // tq,),
        in_specs=_attn_specs(T, tq, lambda i: i),
        out_specs=pl.BlockSpec((tq, B_WIDTH), lambda i: (i, 0)), out_shape=_sds((T, B_WIDTH)),
        compiler_params=_cp("parallel"))(z, z, z, z, z, cs, sn, cs, sn, qg, kg, sinks)


def _attn_bwd(z, cs, sn, qg, kg, sinks, o, do):
    T = z.shape[0]
    tq = min(T, 512)
    nb = tq // WINDOW
    nt = T // tq
    tk = tq + WINDOW

    def tile_of(i):
        return nt - 1 - i

    def body(q_ref, k_ref, v_ref, kp_ref, vp_ref, cs_ref, sn_ref, csp_ref, snp_ref, qg_ref, kg_ref, sk_ref, o_ref, do_ref,
             dq_ref, dk_ref, dv_ref, dqg_ref, dkg_ref, dsk_ref, acck, accv, ck, cv):
        i = pl.program_id(0)
        ti = nt - 1 - i

        @pl.when(i == 0)
        def _():
            dqg_ref[...] = jnp.zeros_like(dqg_ref)
            dkg_ref[...] = jnp.zeros_like(dkg_ref)
            dsk_ref[...] = jnp.zeros_like(dsk_ref)
            ck[...] = jnp.zeros_like(ck)
            cv[...] = jnp.zeros_like(cv)

        csq, snq = cs_ref[...], sn_ref[...]
        cs_all = jnp.concatenate([csp_ref[...], csq], axis=0)
        sn_all = jnp.concatenate([snp_ref[...], snq], axis=0)
        k_all = jnp.concatenate([kp_ref[...], k_ref[...]], axis=0)
        v_all = jnp.concatenate([vp_ref[...], v_ref[...]], axis=0)
        kr, kh, rk = _qk_norm_rope(k_all, kg_ref[...], cs_all, sn_all)
        lo_all = _lane(k_all.shape) < HEAD_DIM
        kd = _dup_heads(kr, lo_all)
        vd = _dup_heads(v_all, lo_all)
        lo = _lane((WINDOW, LANES)) < HEAD_DIM
        acck[...] = jnp.zeros_like(acck)
        accv[...] = jnp.zeros_like(accv)
        prep = [_qk_norm_rope(q_ref[:, pr * LANES:(pr + 1) * LANES], qg_ref[...], csq, snq) for pr in range(4)]
        biases = [_attn_bias(ti * nb + b == 0) for b in range(nb)]
        dqs = [[None] * nb for _ in range(4)]
        for g in range(2):
            sink = _sink_column(sk_ref, g)
            dsink = jnp.zeros((HEADS_PER_KV * WINDOW, 1), F32)
            for b in range(nb):
                bs = slice(b * WINDOW, (b + 1) * WINDOW)
                ws = slice(b * WINDOW, (b + 2) * WINDOW)
                kw, vw = kd[g][ws], vd[g][ws]
                q4 = _stack_heads(prep[2 * g][0][bs] * SCALE, prep[2 * g + 1][0][bs] * SCALE, lo)
                pn, psink = _attn_probs(q4, kw, biases[b], sink)
                o0, o1 = o_ref[bs, 2 * g * LANES:(2 * g + 1) * LANES], o_ref[bs, (2 * g + 1) * LANES:(2 * g + 2) * LANES]
                do4 = _stack_heads(do_ref[bs, 2 * g * LANES:(2 * g + 1) * LANES], do_ref[bs, (2 * g + 1) * LANES:(2 * g + 2) * LANES], lo)
                delta = jnp.sum(do4 * jnp.concatenate([o0, o0, o1, o1], axis=0), axis=-1, keepdims=True)
                ds = pn * (_mm_nt(do4, vw) - delta)
                dsink = dsink - psink * delta
                dqs[2 * g][b], dqs[2 * g + 1][b] = _unstack_heads(_mm(ds, kw) * SCALE, lo)
                acck[g, ws, :] += _mm_tn(ds, q4)
                accv[g, ws, :] += _mm_tn(pn, do4)
            for hh in range(HEADS_PER_KV):
                a = HEADS_PER_KV * g + hh
                dsk_ref[a:a + 1, :] += jnp.zeros((1, LANES), F32) + jnp.sum(dsink[hh * WINDOW:(hh + 1) * WINDOW])
        for pr in range(4):
            _, qh, rq = prep[pr]
            dx, dg = _qk_norm_rope_bwd(qh, rq, qg_ref[...], csq, snq, jnp.concatenate(dqs[pr], axis=0))
            dq_ref[:, pr * LANES:(pr + 1) * LANES] = dx
            dqg_ref[...] += dg

        def fold(acc):
            f0 = acc[0] + _swap64(acc[0])
            f1 = acc[1] + _swap64(acc[1])
            return jnp.where(lo_all, f0, f1)

        dk_all = fold(acck)
        dv_all = fold(accv)
        pad = jnp.zeros((tq - WINDOW, LANES), F32)
        dk_own = dk_all[WINDOW:] + (jnp.concatenate([pad, ck[...]], axis=0) if nb > 1 else ck[...])
        dv_own = dv_all[WINDOW:] + (jnp.concatenate([pad, cv[...]], axis=0) if nb > 1 else cv[...])
        ck[...] = dk_all[:WINDOW]
        cv[...] = dv_all[:WINDOW]
        dxk, dgk = _qk_norm_rope_bwd(kh[WINDOW:], rk[WINDOW:], kg_ref[...], csq, snq, dk_own)
        dk_ref[...] = dxk
        dkg_ref[...] += dgk
        dv_ref[...] = dv_own

    row = pl.BlockSpec((1, LANES), lambda i: (0, 0))
    return pl.pallas_call(
        body, name="attn_bwd", grid=(nt,),
        in_specs=_attn_specs(T, tq, tile_of) + [pl.BlockSpec((tq, B_WIDTH), lambda i: (tile_of(i), 0))] * 2,
        out_specs=[pl.BlockSpec((tq, B_WIDTH), lambda i: (tile_of(i), 0)), pl.BlockSpec((tq, LANES), lambda i: (tile_of(i), 0)),
                   pl.BlockSpec((tq, LANES), lambda i: (tile_of(i), 0)), row, row, pl.BlockSpec((8, LANES), lambda i: (0, 0))],
        out_shape=[_sds((T, B_WIDTH)), _sds((T, LANES)), _sds((T, LANES)), _sds((1, LANES)), _sds((1, LANES)), _sds((8, LANES))],
        scratch_shapes=[pltpu.VMEM((2, tk, LANES), F32), pltpu.VMEM((2, tk, LANES), F32),
                        pltpu.VMEM((WINDOW, LANES), F32), pltpu.VMEM((WINDOW, LANES), F32)],
        compiler_params=_cp("arbitrary"))(z, z, z, z, z, cs, sn, cs, sn, qg, kg, sinks, o, do)


def _bbar_t(are, aim, ldt, btr, bti):
    lbr, lbi = _lam_bar(are, aim, ldt)
    den = are * are + aim * aim
    nr = lbr - 1.0
    cr = (nr * are + lbi * aim) / den
    ci = (lbi * are - nr * aim) / den
    return cr * btr - ci * bti, cr * bti + ci * btr


def _lam_bar(are, aim, ldt):
    dt = jnp.exp(ldt)
    er = jnp.exp(are * dt)
    return er * jnp.cos(aim * dt), er * jnp.sin(aim * dt)


def _block_diag(x):
    t = jnp.concatenate([x] * C_GROUPS, axis=1)
    return jnp.where(_row(t.shape) // C_GROUP == _lane(t.shape) // C_STATE, t, 0.0)


def _block_diag_fold(m):
    rg = _row((C_WIDTH, C_STATE)) // C_GROUP
    acc = jnp.zeros((C_WIDTH, C_STATE), F32)
    for g in range(C_GROUPS):
        acc = acc + jnp.where(rg == g, m[:, g * C_STATE:(g + 1) * C_STATE], 0.0)
    return acc


def _ssm_prep(are, aim, ldt, are_x, aim_x, ldt_x, btr, bti, cre, cim):
    def body(are_r, aim_r, ldt_r, arex_r, aimx_r, ldtx_r, btr_r, bti_r, cre_r, cim_r, bbd_ref, cbd_ref, pwr_ref, pwi_ref):
        lr, li = _lam_bar(are_r[...], aim_r[...], ldt_r[...])
        cr, ci = lr, li
        for r in range(SCAN_SEG):
            pwr_ref[r:r + 1, :] = cr
            pwi_ref[r:r + 1, :] = ci
            cr, ci = cr * lr - ci * li, cr * li + ci * lr
        br, bi = _bbar_t(arex_r[...], aimx_r[...], ldtx_r[...], btr_r[...], bti_r[...])
        bbd_ref[...] = jnp.concatenate([_block_diag(br), _block_diag(bi)], axis=1).astype(MXU)
        cbd_ref[...] = jnp.concatenate([_block_diag(cre_r[...]), -_block_diag(cim_r[...])], axis=1).astype(MXU)

    return pl.pallas_call(
        body, name="ssm_prep",
        out_shape=[_sds((C_WIDTH, 2 * N_STATE), MXU), _sds((C_WIDTH, 2 * N_STATE), MXU), _sds((SCAN_SEG, N_STATE)), _sds((SCAN_SEG, N_STATE))],
        compiler_params=pltpu.CompilerParams(vmem_limit_bytes=VMEM_LIMIT))(are, aim, ldt, are_x, aim_x, ldt_x, btr, bti, cre, cim)


def _ssm_param_bwd(are, aim, ldt, are_x, aim_x, ldt_x, btr, bti, dlr, dli, dbbd, dcr, dci):
    def body(are_r, aim_r, ldt_r, arex_r, aimx_r, ldtx_r, btr_r, bti_r, dlr_r, dli_r, dbbd_r, dcr_r, dci_r,
             dare_ref, daim_ref, dldt_ref, dbtr_ref, dbti_ref, dcre_ref, dcim_ref):
        _, vjp_l = jax.vjp(_lam_bar, are_r[...], aim_r[...], ldt_r[...])
        da1, di1, dl1 = vjp_l((dlr_r[...], dli_r[...]))
        dbr = _block_diag_fold(dbbd_r[:, 0:N_STATE])
        dbi = _block_diag_fold(dbbd_r[:, N_STATE:2 * N_STATE])
        _, vjp_b = jax.vjp(_bbar_t, arex_r[...], aimx_r[...], ldtx_r[...], btr_r[...], bti_r[...])
        da2, di2, dl2, dbtr, dbti = vjp_b((dbr, dbi))

        def gsum(x):
            return x.reshape(C_GROUPS, C_GROUP, C_STATE).sum(axis=1)

        dare_ref[...] = da1 + gsum(da2)
        daim_ref[...] = di1 + gsum(di2)
        dldt_ref[...] = jnp.broadcast_to(jnp.sum(dl1 + gsum(dl2), axis=-1, keepdims=True), (C_GROUPS, LANES))
        dbtr_ref[...] = dbtr
        dbti_ref[...] = dbti
        dcre_ref[...] = _block_diag_fold(dcr_r[...])
        dcim_ref[...] = -_block_diag_fold(dci_r[...])

    g = _sds((C_GROUPS, C_STATE))
    x = _sds((C_WIDTH, C_STATE))
    return pl.pallas_call(
        body, name="ssm_param_bwd", out_shape=[g, g, _sds((C_GROUPS, LANES)), x, x, x, x],
        compiler_params=pltpu.CompilerParams(vmem_limit_bytes=VMEM_LIMIT))(are, aim, ldt, are_x, aim_x, ldt_x, btr, bti, dlr, dli, dbbd, dcr, dci)


SCAN_TILE = 512
SCAN_SEG = 8


def _scan_tables(pwr_ref, pwi_ref, conj, reverse):
    row = _row((SCAN_SEG, N_STATE))
    shifts = []
    for k in (1, 2, 4):
        keep = (row < SCAN_SEG - k) if reverse else (row >= k)
        ar = jnp.broadcast_to(pwr_ref[k - 1:k, :], (SCAN_SEG, N_STATE))
        ai = jnp.broadcast_to(pwi_ref[k - 1:k, :], (SCAN_SEG, N_STATE)) * conj
        shifts.append((SCAN_SEG - k if reverse else k, jnp.where(keep, ar, 0.0), jnp.where(keep, ai, 0.0)))
    if reverse:
        pr = jnp.concatenate([pwr_ref[SCAN_SEG - 1 - r:SCAN_SEG - r, :] for r in range(SCAN_SEG)], axis=0)
        pi = jnp.concatenate([pwi_ref[SCAN_SEG - 1 - r:SCAN_SEG - r, :] for r in range(SCAN_SEG)], axis=0) * conj
    else:
        pr, pi = pwr_ref[...], pwi_ref[...] * conj
    return shifts, (pr, pi)


def _tile_scan(xr_ref, xi_ref, pwr_ref, pwi_ref, sr, si, conj, reverse):
    shifts, (pr, pi) = _scan_tables(pwr_ref, pwi_ref, conj, reverse)
    groups = xr_ref.shape[0] // SCAN_SEG
    out_row = 0 if reverse else SCAN_SEG - 1

    def step(k, c):
        cr, ci = c
        g = groups - 1 - k if reverse else k
        rows = pl.ds(pl.multiple_of(g * SCAN_SEG, SCAN_SEG), SCAN_SEG)
        xr, xi = xr_ref[rows, :], xi_ref[rows, :]
        for amount, ar, ai in shifts:
            qr, qi = pltpu.roll(xr, amount, 0), pltpu.roll(xi, amount, 0)
            xr, xi = xr + ar * qr - ai * qi, xi + ar * qi + ai * qr
        xr, xi = xr + pr * cr - pi * ci, xi + pr * ci + pi * cr
        xr_ref[rows, :] = xr
        xi_ref[rows, :] = xi
        return xr[out_row:out_row + 1], xi[out_row:out_row + 1]

    cr, ci = lax.fori_loop(0, groups, step, (sr[...], si[...]), unroll=2)
    sr[...] = cr
    si[...] = ci


def _ssm_fwd(z, bbd, pwr, pwi, cbd, dsk, w1, w2):
    T = z.shape[0]
    tt = min(T, 2 * SCAN_TILE)

    def body(u_ref, bbd_ref, pwr_ref, pwi_ref, cbd_ref, d_ref, w1_ref, w2_ref, yc_ref, xr_ref, xi_ref, sr, si):
        @pl.when(pl.program_id(0) == 0)
        def _():
            sr[...] = jnp.zeros_like(sr)
            si[...] = jnp.zeros_like(si)

        u = u_ref[...]
        bu = _mm(u, bbd_ref[...])
        xr_ref[...] = bu[:, 0:N_STATE]
        xi_ref[...] = bu[:, N_STATE:2 * N_STATE]
        _tile_scan(xr_ref, xi_ref, pwr_ref, pwi_ref, sr, si, 1.0, False)
        x = jnp.concatenate([xr_ref[...], xi_ref[...]], axis=1)
        y2 = _gelu(_mm_nt(x, cbd_ref[...]) + d_ref[...] * u)
        yc_ref[...] = _mm(y2, w1_ref[...]) * _sigmoid(_mm(y2, w2_ref[...]))

    big = pl.BlockSpec((C_WIDTH, 2 * N_STATE), lambda i: (0, 0))
    tab = pl.BlockSpec((SCAN_SEG, N_STATE), lambda i: (0, 0))
    wsp = pl.BlockSpec((C_WIDTH, C_WIDTH), lambda i: (0, 0))
    xs = pl.BlockSpec((tt, N_STATE), lambda i: (i, 0))
    return pl.pallas_call(
        body, name="ssm_fwd", grid=(T // tt,),
        in_specs=[pl.BlockSpec((tt, C_WIDTH), lambda i: (i, 5)), big, tab, tab, big, pl.BlockSpec((1, C_WIDTH), lambda i: (0, 0)), wsp, wsp],
        out_specs=[pl.BlockSpec((tt, C_WIDTH), lambda i: (i, 0)), xs, xs],
        out_shape=[_sds((T, C_WIDTH)), _sds((T, N_STATE)), _sds((T, N_STATE))],
        scratch_shapes=[pltpu.VMEM((1, N_STATE), F32)] * 2,
        compiler_params=_cp("arbitrary"))(z, bbd, pwr, pwi, cbd, dsk, w1, w2)


def _ssm_bwd(dyc, z, xr, xi, bbd, pwr, pwi, cbd, dsk, w1, w2):
    T = z.shape[0]
    tt = min(T, SCAN_TILE)
    nt = T // tt

    def tile_of(i):
        return nt - 1 - i

    def body(dyc_ref, u_ref, xr_ref, xi_ref, xpr_ref, xpi_ref, bbd_ref, pwr_ref, pwi_ref, cbd_ref, d_ref, w1_ref, w2_ref,
             du_ref, y2_ref, da1_ref, da2_ref, dy_ref, arai_ref, dlr_ref, dli_ref, dd_ref, gr, gi, sr, si):
        i = pl.program_id(0)
        ti = nt - 1 - i

        @pl.when(i == 0)
        def _():
            sr[...] = jnp.zeros_like(sr)
            si[...] = jnp.zeros_like(si)
            dlr_ref[...] = jnp.zeros_like(dlr_ref)
            dli_ref[...] = jnp.zeros_like(dli_ref)
            dd_ref[...] = jnp.zeros_like(dd_ref)

        u = u_ref[...]
        xr_t, xi_t = xr_ref[...], xi_ref[...]
        y = _mm_nt(jnp.concatenate([xr_t, xi_t], axis=1), cbd_ref[...]) + d_ref[...] * u
        y2 = _gelu(y)
        a1 = _mm(y2, w1_ref[...])
        sg = _sigmoid(_mm(y2, w2_ref[...]))
        dyc_t = dyc_ref[...]
        da1 = dyc_t * sg
        da2 = dyc_t * a1 * sg * (1.0 - sg)
        dy = (_mm_nt(da1, w1_ref[...]) + _mm_nt(da2, w2_ref[...])) * _gelu_grad(y)
        gx = _mm(dy, cbd_ref[...])
        gr[...] = gx[:, 0:N_STATE]
        gi[...] = gx[:, N_STATE:2 * N_STATE]
        _tile_scan(gr, gi, pwr_ref, pwi_ref, sr, si, -1.0, True)
        ar, ai = gr[...], gi[...]
        first_row = _row(ar.shape) == 0
        live = jnp.where(ti > 0, 1.0, 0.0)
        xsr = jnp.where(first_row, xpr_ref[7:8, :] * live, pltpu.roll(xr_t, 1, 0))
        xsi = jnp.where(first_row, xpi_ref[7:8, :] * live, pltpu.roll(xi_t, 1, 0))
        dlr_ref[...] += jnp.sum(ar * xsr + ai * xsi, axis=0, keepdims=True)
        dli_ref[...] += jnp.sum(ai * xsr - ar * xsi, axis=0, keepdims=True)
        dd_ref[...] += jnp.sum(dy * u, axis=0, keepdims=True)
        arai = jnp.concatenate([ar, ai], axis=1)
        du_ref[...] = _mm_nt(arai, bbd_ref[...]) + d_ref[...] * dy
        y2_ref[...] = y2.astype(MXU)
        da1_ref[...] = da1.astype(MXU)
        da2_ref[...] = da2.astype(MXU)
        dy_ref[...] = dy.astype(MXU)
        arai_ref[...] = arai.astype(MXU)

    def prev(i):
        return jnp.maximum(tile_of(i) * (tt // 8) - 1, 0)

    big = pl.BlockSpec((C_WIDTH, 2 * N_STATE), lambda i: (0, 0))
    tab = pl.BlockSpec((SCAN_SEG, N_STATE), lambda i: (0, 0))
    srow = pl.BlockSpec((1, N_STATE), lambda i: (0, 0))
    wsp = pl.BlockSpec((C_WIDTH, C_WIDTH), lambda i: (0, 0))
    xs = pl.BlockSpec((tt, N_STATE), lambda i: (tile_of(i), 0))
    xp = pl.BlockSpec((8, N_STATE), lambda i: (prev(i), 0))
    cw = pl.BlockSpec((tt, C_WIDTH), lambda i: (tile_of(i), 0))
    drow = pl.BlockSpec((1, C_WIDTH), lambda i: (0, 0))
    return pl.pallas_call(
        body, name="ssm_bwd", grid=(nt,),
        in_specs=[cw, pl.BlockSpec((tt, C_WIDTH), lambda i: (tile_of(i), 5)), xs, xs, xp, xp, big, tab, tab, big, drow, wsp, wsp],
        out_specs=[cw, cw, cw, cw, cw, pl.BlockSpec((tt, 2 * N_STATE), lambda i: (tile_of(i), 0)), srow, srow, drow],
        out_shape=[_sds((T, C_WIDTH))] + [_sds((T, C_WIDTH), MXU)] * 4 + [_sds((T, 2 * N_STATE), MXU), _sds((1, N_STATE)), _sds((1, N_STATE)), _sds((1, C_WIDTH))],
        scratch_shapes=[pltpu.VMEM((tt, N_STATE), F32)] * 2 + [pltpu.VMEM((1, N_STATE), F32)] * 2,
        compiler_params=_cp("arbitrary"))(dyc, z, xr, xi, xr, xi, bbd, pwr, pwi, cbd, dsk, w1, w2)


_GROUPS = ((0, A_WIDTH), (A_WIDTH, A_WIDTH + B_WIDTH), (A_WIDTH + B_WIDTH, D_MODEL))


def _merge_fwd(h, ya, yb, yc, g, w):
    T = h.shape[0]
    tm = min(T, 512)

    def body(h_ref, a_ref, b_ref, c_ref, g_ref, w_ref, o_ref):
        yn = jnp.concatenate([y * _rms_stat(y) for y in (a_ref[...], b_ref[...], c_ref[...])], axis=1) * g_ref[...]
        o_ref[...] = h_ref[...] + _mm(yn, w_ref[...])

    def rows(w_):
        return pl.BlockSpec((tm, w_), lambda i: (i, 0))

    return pl.pallas_call(
        body, name="merge_fwd", grid=(T // tm,),
        in_specs=[rows(D_MODEL), rows(A_WIDTH), rows(B_WIDTH), rows(C_WIDTH), pl.BlockSpec((1, D_MODEL), lambda i: (0, 0)),
                  pl.BlockSpec((D_MODEL, D_MODEL), lambda i: (0, 0))],
        out_specs=rows(D_MODEL), out_shape=_sds((T, D_MODEL)), compiler_params=_cp("parallel"))(h, ya, yb, yc, g, w)


def _merge_bwd(gres, ya, yb, yc, g, w):
    T = gres.shape[0]
    tm = min(T, 512)

    def body(gr_ref, a_ref, b_ref, c_ref, g_ref, w_ref, da_ref, db_ref, dc_ref, yn_ref, dg_ref):
        @pl.when(pl.program_id(0) == 0)
        def _():
            dg_ref[...] = jnp.zeros_like(dg_ref)

        dyn = _mm_nt(gr_ref[...], w_ref[...])
        yns, dgs = [], []
        for (c0, c1), y_ref, d_ref in zip(_GROUPS, (a_ref, b_ref, c_ref), (da_ref, db_ref, dc_ref)):
            y = y_ref[...]
            r = _rms_stat(y)
            gg = g_ref[:, c0:c1]
            dx, dg = _rms_bwd(y, r, gg, dyn[:, c0:c1])
            d_ref[...] = dx
            dgs.append(dg)
            yns.append(y * r * gg)
        dg_ref[...] += jnp.concatenate(dgs, axis=1)
        yn_ref[...] = jnp.concatenate(yns, axis=1).astype(MXU)

    def rows(w_):
        return pl.BlockSpec((tm, w_), lambda i: (i, 0))

    row = pl.BlockSpec((1, D_MODEL), lambda i: (0, 0))
    return pl.pallas_call(
        body, name="merge_bwd", grid=(T // tm,),
        in_specs=[rows(D_MODEL), rows(A_WIDTH), rows(B_WIDTH), rows(C_WIDTH), row, pl.BlockSpec((D_MODEL, D_MODEL), lambda i: (0, 0))],
        out_specs=[rows(A_WIDTH), rows(B_WIDTH), rows(C_WIDTH), rows(D_MODEL), row],
        out_shape=[_sds((T, A_WIDTH)), _sds((T, B_WIDTH)), _sds((T, C_WIDTH)), _sds((T, D_MODEL), MXU), _sds((1, D_MODEL))],
        compiler_params=_cp("arbitrary"))(gres, ya, yb, yc, g, w)


FF_BLOCK = D_FF // N_DEV


def _load_weights(w1_hbm, w2_hbm, w1, w2, sem):
    @pl.when(pl.program_id(0) == 0)
    def _():
        copies = [pltpu.make_async_copy(w1_hbm.at[j], w1.at[:, pl.ds(j * FF_BLOCK, FF_BLOCK)], sem.at[j]) for j in range(N_DEV)]
        copies.append(pltpu.make_async_copy(w2_hbm, w2, sem.at[N_DEV]))
        for cp in copies:
            cp.start()
        for cp in copies:
            cp.wait()


def _mlp_weight_scratch():
    return [pltpu.VMEM((D_MODEL, D_FF), MXU), pltpu.VMEM((D_FF, D_MODEL), MXU), pltpu.SemaphoreType.DMA((N_DEV + 1,))]


def _mlp_fwd(h, g, w1, w2):
    T = h.shape[0]
    tm = min(T, 512)

    def body(h_ref, g_ref, w1_hbm, w2_hbm, o_ref, r_ref, w1_v, w2_v, sem):
        _load_weights(w1_hbm, w2_hbm, w1_v, w2_v, sem)
        x = h_ref[...]
        a = jnp.maximum(_mm(x * _rms_stat(x) * g_ref[...], w1_v[...]), 0.0)
        r = (a * a).astype(MXU)
        r_ref[...] = r
        o_ref[...] = x + _mm(r, w2_v[...])

    rows = pl.BlockSpec((tm, D_MODEL), lambda i: (i, 0))
    hbm = pl.BlockSpec(memory_space=pl.ANY)
    return pl.pallas_call(
        body, name="mlp_fwd", grid=(T // tm,),
        in_specs=[rows, pl.BlockSpec((1, D_MODEL), lambda i: (0, 0)), hbm, hbm],
        out_specs=[rows, pl.BlockSpec((tm, D_FF), lambda i: (i, 0))],
        out_shape=[_sds((T, D_MODEL)), _sds((T, D_FF), MXU)],
        scratch_shapes=_mlp_weight_scratch(), compiler_params=_cp("arbitrary"))(h, g, w1, w2)


def _mlp_bwd(gres, h, g, r, w1, w2):
    T = h.shape[0]
    tm = min(T, 256)

    def body(gr_ref, h_ref, g_ref, r_ref, w1_hbm, w2_hbm, dh_ref, hn_ref, da_ref, dg_ref, w1_v, w2_v, sem):
        _load_weights(w1_hbm, w2_hbm, w1_v, w2_v, sem)

        @pl.when(pl.program_id(0) == 0)
        def _():
            dg_ref[...] = jnp.zeros_like(dg_ref)

        gr = gr_ref[...]
        da = (_mm_nt(gr, w2_v[...]) * (2.0 * jnp.sqrt(r_ref[...].astype(F32)))).astype(MXU)
        da_ref[...] = da
        x = h_ref[...]
        rs = _rms_stat(x)
        gg = g_ref[...]
        dx, dg = _rms_bwd(x, rs, gg, _mm_nt(da, w1_v[...]))
        dh_ref[...] = gr + dx
        dg_ref[...] += dg
        hn_ref[...] = (x * rs * gg).astype(MXU)

    rows = pl.BlockSpec((tm, D_MODEL), lambda i: (i, 0))
    row = pl.BlockSpec((1, D_MODEL), lambda i: (0, 0))
    wide = pl.BlockSpec((tm, D_FF), lambda i: (i, 0))
    hbm = pl.BlockSpec(memory_space=pl.ANY)
    return pl.pallas_call(
        body, name="mlp_bwd", grid=(T // tm,),
        in_specs=[rows, rows, row, wide, hbm, hbm], out_specs=[rows, rows, wide, row],
        out_shape=[_sds((T, D_MODEL)), _sds((T, D_MODEL), MXU), _sds((T, D_FF), MXU), _sds((1, D_MODEL))],
        scratch_shapes=_mlp_weight_scratch(), compiler_params=_cp("arbitrary"))(gres, h, g, r, w1, w2)


def _ple_fwd(h, p, l, g, wg, wp):
    T = h.shape[0]
    tm = min(T, 512)

    def body(h_ref, p_ref, g_ref, wg_ref, wp_ref, o_ref):
        x = h_ref[...]
        gate = _sigmoid(_mm(x * _rms_stat(x) * g_ref[...], wg_ref[...]))
        o_ref[...] = x + gate * _mm(p_ref[...], wp_ref[...])

    rows = pl.BlockSpec((tm, D_MODEL), lambda i: (i, 0))
    return pl.pallas_call(
        body, name="ple_fwd", grid=(T // tm,),
        in_specs=[rows, pl.BlockSpec((None, tm, PLE_DIM), lambda i: (l, i, 0)), pl.BlockSpec((1, D_MODEL), lambda i: (0, 0)),
                  pl.BlockSpec((D_MODEL, D_MODEL), lambda i: (0, 0)), pl.BlockSpec((PLE_DIM, D_MODEL), lambda i: (0, 0))],
        out_specs=rows, out_shape=_sds((T, D_MODEL)), compiler_params=_cp("parallel"))(h, p, g, wg, wp)


def _ple_bwd(gres, h, p, l, g, wg, wp):
    T = h.shape[0]
    tm = min(T, 512)

    def body(gr_ref, h_ref, p_ref, g_ref, wg_ref, wp_ref, dh_ref, hn_ref, dgp_ref, de_ref, dg_ref):
        @pl.when(pl.program_id(0) == 0)
        def _():
            dg_ref[...] = jnp.zeros_like(dg_ref)

        x = h_ref[...]
        r = _rms_stat(x)
        gg = g_ref[...]
        hn = x * r * gg
        gate = _sigmoid(_mm(hn, wg_ref[...]))
        e = _mm(p_ref[...], wp_ref[...])
        gr = gr_ref[...]
        dgp = gr * e * gate * (1.0 - gate)
        dx, dg = _rms_bwd(x, r, gg, _mm_nt(dgp, wg_ref[...]))
        dh_ref[...] = gr + dx
        dg_ref[...] += dg
        hn_ref[...] = hn.astype(MXU)
        dgp_ref[...] = dgp.astype(MXU)
        de_ref[...] = (gr * gate).astype(MXU)

    rows = pl.BlockSpec((tm, D_MODEL), lambda i: (i, 0))
    row = pl.BlockSpec((1, D_MODEL), lambda i: (0, 0))
    return pl.pallas_call(
        body, name="ple_bwd", grid=(T // tm,),
        in_specs=[rows, rows, pl.BlockSpec((None, tm, PLE_DIM), lambda i: (l, i, 0)), row,
                  pl.BlockSpec((D_MODEL, D_MODEL), lambda i: (0, 0)), pl.BlockSpec((PLE_DIM, D_MODEL), lambda i: (0, 0))],
        out_specs=[rows, rows, rows, rows, row],
        out_shape=[_sds((T, D_MODEL))] + [_sds((T, D_MODEL), MXU)] * 3 + [_sds((1, D_MODEL))],
        compiler_params=_cp("arbitrary"))(gres, h, p, g, wg, wp)


def _loss_head(h, target):
    T = h.shape[0]
    tm = min(T, 1024)

    def body(h_ref, t_ref, dh_ref, l_ref):
        @pl.when(pl.program_id(0) == 0)
        def _():
            l_ref[...] = jnp.zeros_like(l_ref)

        e = h_ref[...] - t_ref[...]
        dh_ref[...] = e * (1.0 / D_MODEL)
        l_ref[...] += jnp.zeros_like(l_ref) + 0.5 * jnp.sum(jnp.mean(e * e, axis=-1, keepdims=True))

    rows = pl.BlockSpec((tm, D_MODEL), lambda i: (i, 0))
    return pl.pallas_call(
        body, name="loss_head", grid=(T // tm,), in_specs=[rows, rows],
        out_specs=[rows, pl.BlockSpec((8, LANES), lambda i: (0, 0))], out_shape=[_sds((T, D_MODEL)), _sds((8, LANES))],
        compiler_params=_cp("arbitrary"))(h, target)


TN_ROWS = 2048

def _tn(a, b, m, n, *, bm, bn, a_off=0, b_off=0, a_lead=None, n_major=False, split=None, dtype=F32, name="tn"):
    T = a.shape[-2]
    tk = min(T, TN_ROWS)
    nk = T // tk
    b_spec = pl.BlockSpec((tk, bn), lambda i, j, k: (k, b_off + j))
    if a_lead is None:
        a_spec = pl.BlockSpec((tk, bm), lambda i, j, k: (k, a_off + i))
    else:
        a_spec = pl.BlockSpec((None, tk, bm), lambda i, j, k: (a_lead, k, a_off + i))
    assert m % bm == 0 and n % bn == 0 and (not n_major or bm == m) and (split is None or (bn == n and n % split == 0))

    def body(a_ref, b_ref, o_ref, acc):
        k = pl.program_id(2)

        @pl.when(k == 0)
        def _():
            acc[...] = jnp.zeros_like(acc)

        acc[...] += _mm_tn(a_ref[...], b_ref[...])

        @pl.when(k == nk - 1)
        def _():
            if split is None:
                o_ref[...] = acc[...].astype(dtype)
            else:
                for d in range(n // split):
                    o_ref[d] = acc[:, d * split:(d + 1) * split].astype(dtype)

    if split is not None:
        out_spec = pl.BlockSpec((n // split, bm, split), lambda i, j, k: (0, i, 0))
        out_shape = _sds((n // split, m, split), dtype)
    elif n_major:
        out_spec = pl.BlockSpec((None, bm, bn), lambda i, j, k: (j, 0, 0))
        out_shape = _sds((n // bn, m, bn), dtype)
    else:
        out_spec = pl.BlockSpec((bm, bn), lambda i, j, k: (i, j))
        out_shape = _sds((m, n), dtype)
    return pl.pallas_call(
        body, name=name, grid=(m // bm, n // bn, nk),
        in_specs=[a_spec, b_spec],
        out_specs=out_spec, out_shape=out_shape, scratch_shapes=[pltpu.VMEM((bm, bn), F32)],
        compiler_params=_cp("parallel", "parallel", "arbitrary"))(a, b)


def _row_tile(R, C):
    for cand in (512, 256, 128, 64, 32, 16, 8):
        if R % cand == 0 and cand * C * 4 <= 2 ** 20:
            return cand
    return R


def _sum_slots(land):
    S, R, C = land.shape
    tr = _row_tile(R, C)

    def body(l_ref, o_ref):
        acc = l_ref[0].astype(F32)
        for s in range(1, S):
            acc = acc + l_ref[s].astype(F32)
        o_ref[...] = acc

    return pl.pallas_call(
        body, name="sum_slots", grid=(R // tr,), in_specs=[pl.BlockSpec((S, tr, C), lambda i: (0, i, 0))],
        out_specs=pl.BlockSpec((tr, C), lambda i: (i, 0)), out_shape=_sds((R, C)), compiler_params=_cp("parallel"))(land)


def _sum_adamw(land, w, m, v):
    R, C = w.shape
    S = land.shape[0]
    tr = _row_tile(R, C)

    def body(l_ref, w_ref, m_ref, v_ref, g_ref, d_ref, nm_ref, nv_ref):
        gg = l_ref[0].astype(F32)
        for s in range(1, S):
            gg = gg + l_ref[s].astype(F32)
        g_ref[...] = gg
        nm = ADAM_B1 * m_ref[...] + (1.0 - ADAM_B1) * gg
        nv = ADAM_B2 * v_ref[...] + (1.0 - ADAM_B2) * (gg * gg)
        m_hat = nm / (1.0 - ADAM_B1 ** ADAM_STEP)
        v_hat = nv / (1.0 - ADAM_B2 ** ADAM_STEP)
        d_ref[...] = -ADAM_LR * (m_hat / (jnp.sqrt(v_hat) + ADAM_EPS) + ADAM_WD * w_ref[...])
        nm_ref[...] = nm
        nv_ref[...] = nv

    blk = pl.BlockSpec((tr, C), lambda i: (i, 0))
    return pl.pallas_call(
        body, name="sum_adamw", grid=(R // tr,), in_specs=[pl.BlockSpec((S, tr, C), lambda i: (0, i, 0))] + [blk] * 3,
        out_specs=[blk] * 4, out_shape=[_sds((R, C))] * 4, compiler_params=_cp("parallel"))(land, w, m, v)


def _all_to_all(pairs, name):
    n = len(pairs)

    def body(*refs):
        srcs, lands = refs[:n], refs[2 * n:3 * n]
        send, recv, loc = refs[3 * n:]
        x, y, c = lax.axis_index("x"), lax.axis_index("y"), lax.axis_index("c")
        me = 4 * x + 2 * y + c
        own = [pltpu.make_async_copy(pairs[t][2](srcs[t], me, me), pairs[t][3](lands[t], me), loc.at[t]) for t in range(n)]
        for cp in own:
            cp.start()
        sends, recvs = [], []
        for k in range(1, N_DEV):
            px, py, pc = x ^ (k >> 2), y ^ ((k >> 1) & 1), c ^ (k & 1)
            peer = 4 * px + 2 * py + pc
            for t in range(n):
                src = pairs[t][2](srcs[t], me, peer)
                cp = pltpu.make_async_remote_copy(src_ref=src, dst_ref=pairs[t][3](lands[t], me), send_sem=send.at[t, k],
                                                  recv_sem=recv.at[t, k], device_id=(px, py, pc), device_id_type=MESH)
                cp.start()
                sends.append(cp)
                recvs.append(pltpu.make_async_remote_copy(src_ref=src, dst_ref=pairs[t][3](lands[t], peer), send_sem=send.at[t, k],
                                                          recv_sem=recv.at[t, k], device_id=(px, py, pc), device_id_type=MESH))
        for cp in recvs:
            cp.wait_recv()
        for cp in sends:
            cp.wait_send()
        for cp in own:
            cp.wait()

    anyspec = pl.BlockSpec(memory_space=pl.ANY)
    lands = [pr[1] for pr in pairs]
    return pl.pallas_call(
        body, name=name, in_specs=[anyspec] * (2 * n), out_specs=[anyspec] * n,
        out_shape=[_sds(a.shape, a.dtype) for a in lands], input_output_aliases={n + t: t for t in range(n)},
        scratch_shapes=[pltpu.SemaphoreType.DMA((n, N_DEV)), pltpu.SemaphoreType.DMA((n, N_DEV)), pltpu.SemaphoreType.DMA((n,))],
        compiler_params=pltpu.CompilerParams(has_side_effects=True))(*[pr[0] for pr in pairs], *lands)


def _gather_whole(src):
    land = lax.empty((N_DEV,) + src.shape, src.dtype)
    return _all_to_all([(src, land, lambda ref, me, peer: ref, lambda ref, sender: ref.at[sender])], "gather_small_grads")[0]


_HBM = pl.BlockSpec(memory_space=pltpu.HBM)
_SEM = pl.BlockSpec(memory_space=pltpu.SEMAPHORE)
_EFFECT = pltpu.SideEffectType.DATAFLOW_SIDE_EFFECTING


def _peers():
    x, y, c = lax.axis_index("x"), lax.axis_index("y"), lax.axis_index("c")
    out = []
    for k in range(1, N_DEV):
        px, py, pc = x ^ (k >> 2), y ^ ((k >> 1) & 1), c ^ (k & 1)
        out.append((k, (px, py, pc), 4 * px + 2 * py + pc))
    return 4 * x + 2 * y + c, out


def _route(mode, layer):
    if mode == "all":
        return (lambda ref, peer: ref), (lambda ref, sender: ref.at[sender])
    return (lambda ref, peer: ref.at[peer]), (lambda ref, sender: ref.at[sender, layer])


def _split_start(srcs, lands, modes, layer, after, name):
    n = len(srcs)
    routes = [_route(m, layer) for m in modes]

    def body(*refs):
        src, land = refs[:n], refs[n:2 * n]
        send, recv, token = refs[2 * n + 1], refs[2 * n + 2], refs[-1]
        me, peers = _peers()
        for k, dev, peer in peers:
            for t, (src_of, dst_of) in enumerate(routes):
                pltpu.make_async_remote_copy(src_ref=src_of(src[t], peer), dst_ref=dst_of(land[t], me), send_sem=send.at[t * N_DEV + k],
                                             recv_sem=recv.at[t * N_DEV + k], device_id=dev, device_id_type=MESH).start()
        token[...] = jnp.zeros_like(token)

    bufs = list(srcs) + list(lands)
    outs = pl.pallas_call(
        body, name=name,
        out_shape=(pltpu.SemaphoreType.DMA((n * N_DEV,)), pltpu.SemaphoreType.DMA((n * N_DEV,)),
                   *[pltpu.HBM(a.shape, a.dtype) for a in bufs], _sds((8, LANES))),
        in_specs=[_HBM] * (2 * n) + [pl.BlockSpec(memory_space=pl.ANY)],
        out_specs=(_SEM, _SEM, *[_HBM] * (2 * n), pl.BlockSpec(memory_space=pltpu.VMEM)),
        input_output_aliases={i: 2 + i for i in range(2 * n)},
        compiler_params=pltpu.CompilerParams(has_side_effects=_EFFECT),
    )(*[pltpu.with_memory_space_constraint(a, pltpu.HBM) for a in bufs], after)
    return outs[0], outs[1], list(outs[2:2 + n]), list(outs[2 + n:2 + 2 * n]), outs[-1]


def _split_wait(send, recv, srcs, lands, modes, layer, after, name):
    n = len(srcs)
    routes = [_route(m, layer) for m in modes]

    def body(*refs):
        src, land = refs[:n], refs[n:2 * n]
        send_r, recv_r = refs[2 * n], refs[2 * n + 1]
        _, peers = _peers()
        for k, dev, peer in peers:
            for t, (src_of, dst_of) in enumerate(routes):
                cp = pltpu.make_async_remote_copy(src_ref=src_of(src[t], peer), dst_ref=dst_of(land[t], peer), send_sem=send_r.at[t * N_DEV + k],
                                                  recv_sem=recv_r.at[t * N_DEV + k], device_id=dev, device_id_type=MESH)
                cp.wait_send()
                cp.wait_recv()

    bufs = list(srcs) + list(lands)
    outs = pl.pallas_call(
        body, name=name, out_shape=tuple(pltpu.HBM(a.shape, a.dtype) for a in bufs),
        in_specs=[_HBM] * (2 * n) + [_SEM, _SEM, pl.BlockSpec(memory_space=pl.ANY)], out_specs=[_HBM] * (2 * n),
        input_output_aliases={i: i for i in range(2 * n)},
        compiler_params=pltpu.CompilerParams(has_side_effects=_EFFECT),
    )(*bufs, send, recv, after)
    return list(outs[:n]), list(outs[n:])


SHARDED = ("w_in", "glu_w1", "glu_w2", "w_out", "w_ff1", "w_ff2", "w_ple_gate", "w_ple_proj")
SMALL = ("attn_norm_g", "gmlp_ln_g", "gmlp_ln_b", "gmlp_ws", "gmlp_bs", "q_norm_g", "k_norm_g", "sinks", "ssm_a_re", "ssm_a_im",
         "ssm_log_dt", "ssm_b_re", "ssm_b_im", "ssm_c_re", "ssm_c_im", "ssm_d", "mix_out_g", "mlp_norm_g", "ple_norm_g")
WEIGHTS = ("attn_norm_g", "w_in", "gmlp_ln_g", "gmlp_ln_b", "gmlp_ws", "gmlp_bs", "q_norm_g", "k_norm_g", "sinks", "ssm_a_re", "ssm_a_im",
           "ssm_log_dt", "ssm_b_re", "ssm_b_im", "ssm_c_re", "ssm_c_im", "ssm_d", "glu_w1", "glu_w2", "mix_out_g", "w_out", "mlp_norm_g",
           "w_ff1", "w_ff2", "ple_norm_g", "w_ple_gate", "w_ple_proj")
FLAT_COLS = 1024


PACK_TILE_ROWS = 8
PACK_ROWS_MULTIPLE = PACK_TILE_ROWS * N_DEV


def _packed_rows(shape):
    return -(-math.prod(shape) // (PACK_TILE_ROWS * FLAT_COLS)) * PACK_TILE_ROWS


def _pack(arrs, dtype):
    blocks = []
    for a in arrs:
        flat = a.astype(dtype).reshape(-1)
        pad = _packed_rows(a.shape) * FLAT_COLS - flat.shape[0]
        if pad:
            flat = jnp.concatenate([flat, jnp.zeros((pad,), dtype)])
        blocks.append(flat.reshape(-1, FLAT_COLS))
    rows = sum(b.shape[0] for b in blocks)
    if rows % PACK_ROWS_MULTIPLE:
        blocks.append(jnp.zeros((PACK_ROWS_MULTIPLE - rows % PACK_ROWS_MULTIPLE, FLAT_COLS), dtype))
    return jnp.concatenate(blocks, axis=0)


def _unpack(flat, shapes):
    out, r = [], 0
    for s in shapes:
        nr = _packed_rows(s)
        out.append(flat[r:r + nr].reshape(-1)[:math.prod(s)].reshape(s))
        r += nr
    return out


def _from_col_major(s):
    n, rows, cs = s.shape
    return s.transpose(1, 0, 2).reshape(rows, n * cs)


EARLY = ("w_in", "glu_w1", "glu_w2")
LATE = ("w_out", "w_ff1", "w_ff2", "w_ple_gate", "w_ple_proj")
GRADS_MID = ("w_ff1", "w_ff2", "w_ple_gate", "w_ple_proj")
GRADS_END = ("w_in", "glu_w1", "glu_w2", "w_out")


def _layer_fwd(h, p, l, cs, sn, W, late_weights, S, sp):
    z = _inproj_fwd(h, S["attn_norm_g"], W["w_in"])
    ya = _gmlp_fwd(z, S["lng"], S["lnb"], S["gmlp_ws"], S["bsx"])
    yb = _attn_fwd(z, cs, sn, S["qg"], S["kg"], S["sinks"])
    yc, xr, xi = _ssm_fwd(z, sp["bbd"], sp["pwr"], sp["pwi"], sp["cbd"], S["ssm_d"], W["glu_w1"], W["glu_w2"])
    late, token = late_weights(ya[0:8, 0:LANES] + yb[0:8, 0:LANES] + yc[0:8, 0:LANES])
    W = {**W, **late}
    h1 = _merge_fwd(h, ya, yb, yc, _behind(S["mix_out_g"], token), W["w_out"])
    h2, r = _mlp_fwd(h1, S["mlp_norm_g"], W["w_ff1"], W["w_ff2"])
    h3 = _ple_fwd(h2, p, l, S["ple_norm_g"], W["w_ple_gate"], W["w_ple_proj"])
    return h3, dict(h=h, z=z, ya=ya, yb=yb, yc=yc, xr=xr, xi=xi, h1=h1, r=r, h2=h2), W


def _layer_bwd(g3, p, l, cs, sn, W, S, sp, A, raw, mid_bwd):
    G = {}
    g2, hn3, dgp, de, G["ple_norm_g"] = _ple_bwd(g3, A["h2"], p, l, S["ple_norm_g"], W["w_ple_gate"], W["w_ple_proj"])
    G["w_ple_gate"] = _tn(hn3, dgp, D_MODEL, D_MODEL, bm=1024, bn=1024, dtype=WIRE, name="tn_gate").reshape(N_DEV, -1, D_MODEL)
    G["w_ple_proj"] = _tn(p, de, PLE_DIM, D_MODEL, bm=PLE_DIM, bn=D_MODEL, a_lead=l, split=D_MODEL // N_DEV, dtype=WIRE, name="tn_proj")
    g1, hn, da, G["mlp_norm_g"] = _mlp_bwd(g2, A["h1"], S["mlp_norm_g"], A["r"], W["w_ff1"], W["w_ff2"])
    G["w_ff1"] = _tn(hn, da, D_MODEL, D_FF, bm=D_MODEL, bn=FF_BLOCK, n_major=True, dtype=WIRE, name="tn_ff1")
    G["w_ff2"] = _tn(A["r"], g2, D_FF, D_MODEL, bm=1024, bn=1024, dtype=WIRE, name="tn_ff2").reshape(N_DEV, -1, D_MODEL)
    token = mid_bwd(g1, G)
    dya, dyb, dyc, yn, G["mix_out_g"] = _merge_bwd(g1, A["ya"], A["yb"], A["yc"], _behind(S["mix_out_g"], token), W["w_out"])
    G["w_out"] = _tn(yn, g1, D_MODEL, D_MODEL, bm=1024, bn=1024, dtype=WIRE, name="tn_out").reshape(N_DEV, -1, D_MODEL)
    dzc, y2, da1, da2, dy, arai, dlr, dli, dd = _ssm_bwd(dyc, A["z"], A["xr"], A["xi"], sp["bbd"], sp["pwr"], sp["pwi"], sp["cbd"],
                                                        S["ssm_d"], W["glu_w1"], W["glu_w2"])
    G["glu_w1"] = _tn(y2, da1, C_WIDTH, C_WIDTH, bm=256, bn=256, dtype=WIRE, name="tn_glu1").reshape(N_DEV, -1, C_WIDTH)
    G["glu_w2"] = _tn(y2, da2, C_WIDTH, C_WIDTH, bm=256, bn=256, dtype=WIRE, name="tn_glu2").reshape(N_DEV, -1, C_WIDTH)
    dbbd = _tn(A["z"], arai, C_WIDTH, 2 * N_STATE, bm=256, bn=1024, a_off=5, name="tn_bbd")
    dcr = _tn(dy, A["xr"], C_WIDTH, N_STATE, bm=256, bn=1024, name="tn_cre")
    dci = _tn(dy, A["xi"], C_WIDTH, N_STATE, bm=256, bn=1024, name="tn_cim")
    dare, daim, dldt, dbtr, dbti, dcre, dcim = _ssm_param_bwd(
        raw["are"], raw["aim"], raw["ldt"], raw["are_x"], raw["aim_x"], raw["ldt_x"], raw["btr"], raw["bti"],
        dlr.reshape(C_GROUPS, C_STATE), dli.reshape(C_GROUPS, C_STATE), dbbd, dcr, dci)
    G["ssm_a_re"], G["ssm_a_im"], G["ssm_log_dt"] = dare, daim, dldt[:, 0]
    G["ssm_b_re"] = dbtr.reshape(C_GROUPS, C_GROUP, C_STATE).transpose(0, 2, 1)
    G["ssm_b_im"] = dbti.reshape(C_GROUPS, C_GROUP, C_STATE).transpose(0, 2, 1)
    G["ssm_c_re"] = dcre.reshape(C_GROUPS, C_GROUP, C_STATE)
    G["ssm_c_im"] = dcim.reshape(C_GROUPS, C_GROUP, C_STATE)
    G["ssm_d"] = dd.reshape(C_GROUPS, C_GROUP)
    dzq, dzk, dzv, dqg, dkg, dsk = _attn_bwd(A["z"], cs, sn, S["qg"], S["kg"], S["sinks"], A["yb"], dyb)
    G["q_norm_g"] = dqg[0, :HEAD_DIM] + dqg[0, HEAD_DIM:]
    G["k_norm_g"] = dkg[0, :HEAD_DIM] + dkg[0, HEAD_DIM:]
    G["sinks"] = dsk[:, 0]
    dza, dws, dbs, dlng, dlnb = _gmlp_bwd(A["z"], dya, S["lng"], S["lnb"], S["gmlp_ws"], S["bsx"])
    G["gmlp_ws"] = dws
    G["gmlp_bs"] = dbs[:, :, 0]
    G["gmlp_ln_g"] = dlng.reshape(A_HEADS, 2, HEAD_DIM)[:, 1]
    G["gmlp_ln_b"] = dlnb.reshape(A_HEADS, 2, HEAD_DIM)[:, 1]
    g0, xn, dz, G["attn_norm_g"] = _inproj_bwd(g1, A["h"], S["attn_norm_g"], W["w_in"], dza, dzq, dzk, dzv, dzc)
    G["w_in"] = _tn(xn, dz, D_MODEL, IN_COLS, bm=D_MODEL, bn=IN_COLS, split=IN_COLS // N_DEV, dtype=WIRE, name="tn_in")
    return g0, G


def _small_layouts(P, l):
    def row(a):
        return a.reshape(1, -1)

    zeros = jnp.zeros((A_HEADS, HEAD_DIM), F32)
    S = dict(
        attn_norm_g=row(P["attn_norm_g"][l]), mix_out_g=row(P["mix_out_g"][l]), mlp_norm_g=row(P["mlp_norm_g"][l]),
        ple_norm_g=row(P["ple_norm_g"][l]),
        lng=jnp.stack([zeros, P["gmlp_ln_g"][l]], axis=1).reshape(1, 512),
        lnb=jnp.stack([zeros, P["gmlp_ln_b"][l]], axis=1).reshape(1, 512),
        gmlp_ws=P["gmlp_ws"][l],
        bsx=jnp.broadcast_to(P["gmlp_bs"][l][:, :, None], (A_HEADS, CHUNK, CHUNK)),
        qg=jnp.tile(P["q_norm_g"][l], 2).reshape(1, LANES), kg=jnp.tile(P["k_norm_g"][l], 2).reshape(1, LANES),
        sinks=jnp.broadcast_to(P["sinks"][l][:, None], (8, LANES)),
        ssm_d=row(P["ssm_d"][l]),
    )
    are, aim = P["ssm_a_re"][l], P["ssm_a_im"][l]
    ldt = jnp.broadcast_to(P["ssm_log_dt"][l][:, None], (C_GROUPS, C_STATE))
    raw = dict(
        are=are, aim=aim, ldt=ldt,
        are_x=jnp.repeat(are, C_GROUP, axis=0), aim_x=jnp.repeat(aim, C_GROUP, axis=0), ldt_x=jnp.repeat(ldt, C_GROUP, axis=0),
        btr=P["ssm_b_re"][l].transpose(0, 2, 1).reshape(C_WIDTH, C_STATE), bti=P["ssm_b_im"][l].transpose(0, 2, 1).reshape(C_WIDTH, C_STATE),
        cre=P["ssm_c_re"][l].reshape(C_WIDTH, C_STATE), cim=P["ssm_c_im"][l].reshape(C_WIDTH, C_STATE),
    )
    return S, raw


def _ssm_prep_layer(raw):
    bbd, cbd, pwr, pwi = _ssm_prep(raw["are"].reshape(1, N_STATE), raw["aim"].reshape(1, N_STATE), raw["ldt"].reshape(1, N_STATE),
                                   raw["are_x"], raw["aim_x"], raw["ldt_x"], raw["btr"], raw["bti"], raw["cre"], raw["cim"])
    return dict(bbd=bbd, cbd=cbd, pwr=pwr, pwi=pwi)


def _behind(row, token):
    return row if token is None else row + token[0:1, 0:1]


def _local_step(x, p, positions, target, P, weights_of, mid_bwd, after_bwd):
    inv = 1.0 / (ROPE_THETA ** (jnp.arange(0, HEAD_DIM, 2, dtype=F32) / HEAD_DIM))
    cs, sn = _rope_tables(positions.reshape(-1, 1), jnp.tile(inv, 4).reshape(1, LANES))
    h = x
    acts, smalls, weights = [], [], []
    for l in range(DEPTH):
        W, late_weights, token = weights_of(l, h)
        S, raw = _small_layouts(P, l)
        sp = _ssm_prep_layer(raw)
        h, A, W = _layer_fwd(h, p, l, cs, sn, W, late_weights, {**S, "attn_norm_g": _behind(S["attn_norm_g"], token)}, sp)
        acts.append(A)
        smalls.append((S, raw, sp))
        weights.append(W)
    g, lsum = _loss_head(h, target)
    grads = [None] * DEPTH
    token = None
    for l in reversed(range(DEPTH)):
        S, raw, sp = smalls[l]
        g, grads[l] = _layer_bwd(g, p, l, cs, sn, weights[l], {**S, "ple_norm_g": _behind(S["ple_norm_g"], token)}, sp, acts[l], raw,
                                 functools.partial(mid_bwd, l))
        token = after_bwd(l, g, grads[l])
    return lsum[0, 0], g, grads


def _layer_weights(g):
    layout = dict(
        w_in=_from_col_major, glu_w1=lambda a: a.reshape(C_WIDTH, C_WIDTH), glu_w2=lambda a: a.reshape(C_WIDTH, C_WIDTH),
        w_out=lambda a: a.reshape(D_MODEL, D_MODEL), w_ff1=lambda a: a, w_ff2=lambda a: a.reshape(D_FF, D_MODEL),
        w_ple_gate=lambda a: a.reshape(D_MODEL, D_MODEL), w_ple_proj=_from_col_major)
    return {n: layout[n](a) for n, a in g.items()}


def kernel(x, p, positions, attn_norm_g, w_in, gmlp_ln_g, gmlp_ln_b, gmlp_ws, gmlp_bs, q_norm_g, k_norm_g, sinks, ssm_a_re, ssm_a_im, ssm_log_dt, ssm_b_re, ssm_b_im, ssm_c_re, ssm_c_im, ssm_d, glu_w1, glu_w2, mix_out_g, w_out, mlp_norm_g, w_ff1, w_ff2, ple_norm_g, w_ple_gate, w_ple_proj, loss_target, m_attn_norm_g, m_w_in, m_gmlp_ln_g, m_gmlp_ln_b, m_gmlp_ws, m_gmlp_bs, m_q_norm_g, m_k_norm_g, m_sinks, m_ssm_a_re, m_ssm_a_im, m_ssm_log_dt, m_ssm_b_re, m_ssm_b_im, m_ssm_c_re, m_ssm_c_im, m_ssm_d, m_glu_w1, m_glu_w2, m_mix_out_g, m_w_out, m_mlp_norm_g, m_w_ff1, m_w_ff2, m_ple_norm_g, m_w_ple_gate, m_w_ple_proj, v_attn_norm_g, v_w_in, v_gmlp_ln_g, v_gmlp_ln_b, v_gmlp_ws, v_gmlp_bs, v_q_norm_g, v_k_norm_g, v_sinks, v_ssm_a_re, v_ssm_a_im, v_ssm_log_dt, v_ssm_b_re, v_ssm_b_im, v_ssm_c_re, v_ssm_c_im, v_ssm_d, v_glu_w1, v_glu_w2, v_mix_out_g, v_w_out, v_mlp_norm_g, v_w_ff1, v_w_ff2, v_ple_norm_g, v_w_ple_gate, v_w_ple_proj):
    env = dict(locals())
    P = {n: env[n] for n in WEIGHTS}
    M = {n: env["m_" + n] for n in WEIGHTS}
    V = {n: env["v_" + n] for n in WEIGHTS}
    return _step(x, p, positions, loss_target, P, M, V)


def _step(x, p, positions, loss_target, P, M, V):
    small_shapes = [P[n].shape for n in SMALL]
    me = 4 * lax.axis_index("x") + 2 * lax.axis_index("y") + lax.axis_index("c")
    nothing = jnp.zeros((8, LANES), F32)

    def put(land, own, lead):
        return lax.dynamic_update_slice(land, own.reshape((1,) * len(lead) + own.shape), tuple(lead) + (0,) * own.ndim)

    def gather_start(l, names, after, tag):
        shards = [P[n][l].astype(WIRE) for n in names]
        lands = [lax.empty((N_DEV,) + s.shape, WIRE) for s in shards]
        send, recv, shards, lands, token = _split_start(shards, lands, ["all"] * len(names), 0, after, f"gather_start_{l}{tag}")
        return dict(names=names, send=send, recv=recv, shards=shards, lands=lands, token=token, name=f"gather_wait_{l}{tag}")

    def gather_wait(f, after):
        shards, lands = _split_wait(f["send"], f["recv"], f["shards"], f["lands"], ["all"] * len(f["names"]), 0, after, f["name"])
        return dict(zip(f["names"], [put(ld, sh, (me,)) for sh, ld in zip(shards, lands)]))

    first = gather_start(0, EARLY, nothing, "a")
    flying = {0: (first, gather_start(0, LATE, first["token"], "b"))}

    def weights_of(l, h):
        fa, fb = flying.pop(l)
        got = gather_wait(fa, h)
        if fb is None:
            token = None
            if l + 1 < DEPTH:
                flying[l + 1] = (gather_start(l + 1, SHARDED, got["w_in"], ""), None)
                token = flying[l + 1][0]["token"]
            W = _layer_weights(got)
            return {n: W[n] for n in EARLY}, (lambda after: ({n: W[n] for n in LATE}, None)), token

        def late_weights(after):
            late = gather_wait(fb, after)
            flying[l + 1] = (gather_start(l + 1, SHARDED, late["w_out"], ""), None)
            return _layer_weights(late), flying[l + 1][0]["token"]

        return _layer_weights(got), late_weights, fb["token"]

    grad_lands = {n: lax.empty((N_DEV,) + P[n].shape, WIRE) for n in SHARDED}
    sent = []

    def scatter_start(l, names, parts, lands, tag):
        send, recv, parts, lands, token = _split_start(parts, lands, ["own"] * len(parts), l, nothing, f"scatter_start_{l}{tag}")
        sent.append(dict(l=l, names=names, send=send, recv=recv, parts=parts, lands=lands, name=f"scatter_wait_{l}{tag}"))
        return token

    def scatter_wait(after):
        f = sent.pop(0)
        parts, lands = _split_wait(f["send"], f["recv"], f["parts"], f["lands"], ["own"] * len(f["parts"]), f["l"], after, f["name"])
        lands = [put(ld, lax.dynamic_index_in_dim(part, me, 0, keepdims=False), (me, f["l"])) for part, ld in zip(parts, lands)]
        return dict(zip(f["names"], lands))

    def mid_bwd(l, g1, G):
        if l > 0:
            return None
        grad_lands.update(scatter_wait(g1))
        return scatter_start(0, GRADS_MID, [G[n] for n in GRADS_MID], [grad_lands[n] for n in GRADS_MID], "a")

    small_land = []

    def after_bwd(l, g, G):
        if l > 0:
            if sent:
                grad_lands.update(scatter_wait(g))
            return scatter_start(l, SHARDED, [G[n] for n in SHARDED], [grad_lands[n] for n in SHARDED], "")
        sflat = _pack([jnp.stack([grads_of[k][n] for k in range(DEPTH)]) for n in SMALL], F32)
        sparts = sflat.reshape(N_DEV, -1, FLAT_COLS)
        small_land.append(sflat.shape)
        return scatter_start(0, GRADS_END + ("small",), [G[n] for n in GRADS_END] + [sparts],
                             [grad_lands[n] for n in GRADS_END] + [lax.empty((N_DEV, 1) + sparts.shape[1:], F32)], "b")

    grads_of = {}

    def after_bwd_recording(l, g, G):
        grads_of[l] = G
        grads_of["token"] = after_bwd(l, g, G)
        return grads_of["token"]

    lsum, gx, grads = _local_step(x[0], p[:, 0], positions[0], loss_target[0], P, weights_of, mid_bwd, after_bwd_recording)
    G, delta, new_m, new_v = {}, {}, {}, {}

    def update(names):
        for n in names:
            shp = P[n].shape
            res = _sum_adamw(grad_lands[n].reshape(N_DEV, -1, shp[-1]), *(a.reshape(-1, shp[-1]) for a in (P[n], M[n], V[n])))
            G[n], delta[n], new_m[n], new_v[n] = (a.reshape(shp) for a in res)

    grad_lands.update(scatter_wait(grads_of["token"]))
    update(GRADS_MID)
    last = scatter_wait(sum(new_v[n].reshape(-1, LANES)[:8] for n in GRADS_MID))
    small_parts = last.pop("small")
    grad_lands.update(last)
    update(GRADS_END)
    small_sum = _gather_whole(_sum_slots(small_parts[:, 0])).reshape((1,) + small_land[0])
    res = _sum_adamw(small_sum, _pack([P[n] for n in SMALL], F32), _pack([M[n] for n in SMALL], F32), _pack([V[n] for n in SMALL], F32))
    for dst, flat in zip((G, delta, new_m, new_v), res):
        dst.update(zip(SMALL, _unpack(flat, small_shapes)))
    loss = lax.psum(lsum, ("x", "y", "c"))
    return (loss, gx[None], *[G[n] for n in WEIGHTS], *[delta[n] for n in WEIGHTS], *[new_m[n] for n in WEIGHTS], *[new_v[n] for n in WEIGHTS])
```

```python
import functools
import math

import jax
import jax.numpy as jnp
from jax import lax
from jax.experimental import pallas as pl
from jax.experimental.pallas import tpu as pltpu

F32 = jnp.float32
MXU = jnp.bfloat16
WIRE = jnp.bfloat16

D_MODEL = 1024
DEPTH = 4
HEAD_DIM = 64
A_WIDTH = 256
A_HEADS = 4
CHUNK = 128
B_WIDTH = 512
WINDOW = 128
C_WIDTH = 256
C_GROUP = 16
C_GROUPS = 16
C_STATE = 64
N_STATE = C_GROUPS * C_STATE
IN_COLS = 1536
D_FF = 4096
PLE_DIM = 256
EPS = 1e-6
ROPE_THETA = 10000.0
SCALE = HEAD_DIM ** -0.5
NEG = -1e30
N_DEV = 8

ADAM_LR = 0.001
ADAM_B1 = 0.9
ADAM_B2 = 0.999
ADAM_EPS = 1e-08
ADAM_WD = 0.01
ADAM_STEP = 10

V7X_VMEM_BYTES = 64 * 2 ** 20
VMEM_LIMIT = V7X_VMEM_BYTES - 8 * 2 ** 20
LANES = 128

MESH = pl.DeviceIdType.MESH


def _cp(*sem):
    return pltpu.CompilerParams(dimension_semantics=sem, vmem_limit_bytes=VMEM_LIMIT)


def _sds(shape, dtype=F32):
    return jax.ShapeDtypeStruct(shape, dtype)


def _mm(a, b):
    return jnp.dot(a.astype(MXU), b.astype(MXU), preferred_element_type=F32)


def _mm_nt(a, b):
    return lax.dot_general(a.astype(MXU), b.astype(MXU), (((1,), (1,)), ((), ())), preferred_element_type=F32)


def _mm_tn(a, b):
    return lax.dot_general(a.astype(MXU), b.astype(MXU), (((0,), (0,)), ((), ())), preferred_element_type=F32)


def _lane(shape):
    return lax.broadcasted_iota(jnp.int32, shape, len(shape) - 1)


def _row(shape):
    return lax.broadcasted_iota(jnp.int32, shape, 0)


_GELU_C = math.sqrt(2.0 / math.pi)


def _gelu(x):
    return 0.5 * x * (1.0 + jnp.tanh(_GELU_C * (x + 0.044715 * (x * x * x))))


def _gelu_grad(x):
    t = jnp.tanh(_GELU_C * (x + 0.044715 * (x * x * x)))
    return 0.5 * (1.0 + t) + 0.5 * x * (1.0 - t * t) * (_GELU_C * (1.0 + 3.0 * 0.044715 * (x * x)))


def _sigmoid(x):
    return 1.0 / (1.0 + jnp.exp(-x))


def _rms_stat(x):
    return lax.rsqrt(jnp.mean(x * x, axis=-1, keepdims=True) + EPS)


def _rms_bwd(x, r, g, dy):
    xh = x * r
    dxh = dy * g
    dx = r * (dxh - xh * jnp.mean(dxh * xh, axis=-1, keepdims=True))
    return dx, jnp.sum(dy * xh, axis=0, keepdims=True)


def _tril(w):
    return jnp.where(_row(w.shape) >= _lane(w.shape), w, 0.0)


def _swap64(x):
    return pltpu.roll(x, HEAD_DIM, 1)


def _group_sum64(x, lo):
    s_lo = jnp.sum(jnp.where(lo, x, 0.0), axis=-1, keepdims=True)
    s_hi = jnp.sum(jnp.where(lo, 0.0, x), axis=-1, keepdims=True)
    return jnp.where(lo, s_lo, s_hi)


def _partner(x):
    n = x.shape[-1]
    first = (_lane(x.shape) % HEAD_DIM) < HEAD_DIM // 2
    return jnp.where(first, pltpu.roll(x, n - HEAD_DIM // 2, 1), pltpu.roll(x, HEAD_DIM // 2, 1))


def _rope(y, cs, sn):
    return y * cs + _partner(y) * sn


def _rope_bwd(d, cs, sn):
    return d * cs + _partner(d * sn)


def _qk_norm_rope(x, g, cs, sn):
    lo = _lane(x.shape) < HEAD_DIM
    r = lax.rsqrt(_group_sum64(x * x, lo) * (1.0 / HEAD_DIM) + EPS)
    xh = x * r
    return _rope(xh * g, cs, sn), xh, r


def _qk_norm_rope_bwd(xh, r, g, cs, sn, d):
    lo = _lane(xh.shape) < HEAD_DIM
    dy = _rope_bwd(d, cs, sn)
    dxh = dy * g
    m = _group_sum64(dxh * xh, lo) * (1.0 / HEAD_DIM)
    return r * (dxh - xh * m), jnp.sum(dy * xh, axis=0, keepdims=True)


def _gmlp_head(blk, g, b):
    hi = _lane(blk.shape) >= HEAD_DIM
    mu = jnp.sum(jnp.where(hi, blk, 0.0), axis=-1, keepdims=True) * (1.0 / HEAD_DIM)
    xc = jnp.where(hi, blk - mu, 0.0)
    rstd = lax.rsqrt(jnp.sum(xc * xc, axis=-1, keepdims=True) * (1.0 / HEAD_DIM) + EPS)
    vhat = xc * rstd
    return vhat * g + b, vhat, rstd


def _rope_tables(pos_col, inv_row):
    T = pos_col.shape[0]
    tm = min(T, 1024)

    def body(p_ref, inv_ref, cs_ref, sn_ref):
        ang = p_ref[...].astype(F32) * inv_ref[...]
        s = jnp.sin(ang)
        cs_ref[...] = jnp.cos(ang)
        sn_ref[...] = jnp.where((_lane(ang.shape) % HEAD_DIM) < HEAD_DIM // 2, -s, s)

    blk = pl.BlockSpec((tm, LANES), lambda i: (i, 0))
    return pl.pallas_call(
        body, name="rope_tables", grid=(T // tm,),
        in_specs=[pl.BlockSpec((tm, 1), lambda i: (i, 0)), pl.BlockSpec((1, LANES), lambda i: (0, 0))],
        out_specs=[blk, blk], out_shape=[_sds((T, LANES))] * 2, compiler_params=_cp("parallel"))(pos_col, inv_row)


def _inproj_fwd(h, g, w):
    T = h.shape[0]
    tm = min(T, 512)

    def body(h_ref, g_ref, w_ref, z_ref):
        x = h_ref[...]
        z_ref[...] = _mm(x * _rms_stat(x) * g_ref[...], w_ref[...])

    return pl.pallas_call(
        body, name="inproj_fwd", grid=(T // tm,),
        in_specs=[pl.BlockSpec((tm, D_MODEL), lambda i: (i, 0)), pl.BlockSpec((1, D_MODEL), lambda i: (0, 0)),
                  pl.BlockSpec((D_MODEL, IN_COLS), lambda i: (0, 0))],
        out_specs=pl.BlockSpec((tm, IN_COLS), lambda i: (i, 0)), out_shape=_sds((T, IN_COLS)),
        compiler_params=_cp("parallel"))(h, g, w)


def _inproj_bwd(gres, h, g, w, dza, dzq, dzk, dzv, dzc):
    T = h.shape[0]
    tm = min(T, 512)

    def body(gr_ref, h_ref, g_ref, w_ref, a_ref, q_ref, k_ref, v_ref, c_ref, dh_ref, xn_ref, dz_ref, dg_ref):
        @pl.when(pl.program_id(0) == 0)
        def _():
            dg_ref[...] = jnp.zeros_like(dg_ref)

        x = h_ref[...]
        r = _rms_stat(x)
        gg = g_ref[...]
        dz = jnp.concatenate([a_ref[...], q_ref[...], k_ref[...], v_ref[...], c_ref[...]], axis=1)
        dxn = _mm_nt(dz, w_ref[...])
        dx, dg = _rms_bwd(x, r, gg, dxn)
        dh_ref[...] = gr_ref[...] + dx
        dg_ref[...] += dg
        xn_ref[...] = (x * r * gg).astype(MXU)
        dz_ref[...] = dz.astype(MXU)

    def rows(w_):
        return pl.BlockSpec((tm, w_), lambda i: (i, 0))

    row = pl.BlockSpec((1, D_MODEL), lambda i: (0, 0))
    return pl.pallas_call(
        body, name="inproj_bwd", grid=(T // tm,),
        in_specs=[rows(D_MODEL), rows(D_MODEL), row, pl.BlockSpec((D_MODEL, IN_COLS), lambda i: (0, 0)),
                  rows(512), rows(512), rows(128), rows(128), rows(256)],
        out_specs=[rows(D_MODEL), rows(D_MODEL), rows(IN_COLS), row],
        out_shape=[_sds((T, D_MODEL)), _sds((T, D_MODEL), MXU), _sds((T, IN_COLS), MXU), _sds((1, D_MODEL))],
        compiler_params=_cp("arbitrary"))(gres, h, g, w, dza, dzq, dzk, dzv, dzc)


def _gmlp_fwd(z, lng, lnb, ws, bsx):
    T = z.shape[0]
    tm = min(T, 512)
    nc = tm // CHUNK

    def body(z_ref, g_ref, b_ref, w_ref, bs_ref, ya_ref):
        zg = _gelu(z_ref[...])
        lo = _lane((tm, LANES)) < HEAD_DIM
        prods = []
        for hd in range(A_HEADS):
            sl = slice(hd * LANES, (hd + 1) * LANES)
            blk = zg[:, sl]
            vn, _, _ = _gmlp_head(blk, g_ref[:, sl], b_ref[:, sl])
            wm = _tril(w_ref[hd])
            sv = jnp.concatenate([_mm(wm, vn[c * CHUNK:(c + 1) * CHUNK]) + bs_ref[hd] for c in range(nc)], axis=0)
            prods.append(blk * _swap64(sv))
        ya_ref[:, 0:LANES] = jnp.where(lo, prods[0], _swap64(prods[1]))
        ya_ref[:, LANES:2 * LANES] = jnp.where(lo, prods[2], _swap64(prods[3]))

    row = pl.BlockSpec((1, 512), lambda i: (0, 0))
    mat = pl.BlockSpec((A_HEADS, CHUNK, CHUNK), lambda i: (0, 0, 0))
    return pl.pallas_call(
        body, name="gmlp_fwd", grid=(T // tm,),
        in_specs=[pl.BlockSpec((tm, 512), lambda i: (i, 0)), row, row, mat, mat],
        out_specs=pl.BlockSpec((tm, A_WIDTH), lambda i: (i, 0)), out_shape=_sds((T, A_WIDTH)),
        compiler_params=_cp("parallel"))(z, lng, lnb, ws, bsx)


def _gmlp_bwd(z, dya, lng, lnb, ws, bsx):
    T = z.shape[0]
    tm = min(T, 512)
    nc = tm // CHUNK

    def body(z_ref, dya_ref, g_ref, b_ref, w_ref, bs_ref, dza_ref, dw_ref, dbs_ref, dg_ref, db_ref):
        @pl.when(pl.program_id(0) == 0)
        def _():
            dw_ref[...] = jnp.zeros_like(dw_ref)
            dbs_ref[...] = jnp.zeros_like(dbs_ref)
            dg_ref[...] = jnp.zeros_like(dg_ref)
            db_ref[...] = jnp.zeros_like(db_ref)

        za = z_ref[...]
        zg = _gelu(za)
        gp = _gelu_grad(za)
        lo = _lane((tm, LANES)) < HEAD_DIM
        for hd in range(A_HEADS):
            sl = slice(hd * LANES, (hd + 1) * LANES)
            blk = zg[:, sl]
            g = g_ref[:, sl]
            vn, vhat, rstd = _gmlp_head(blk, g, b_ref[:, sl])
            wm = _tril(w_ref[hd])
            pair = dya_ref[:, (hd // 2) * LANES:(hd // 2 + 1) * LANES]
            dy = jnp.where(lo, pair if hd % 2 == 0 else _swap64(pair), 0.0)
            dsv = _swap64(dy * blk)
            svs, dvns = [], []
            dw = jnp.zeros((CHUNK, CHUNK), F32)
            dbs = jnp.zeros((CHUNK, 1), F32)
            for c in range(nc):
                cs = slice(c * CHUNK, (c + 1) * CHUNK)
                svs.append(_mm(wm, vn[cs]) + bs_ref[hd])
                dw = dw + _mm_nt(dsv[cs], vn[cs])
                dbs = dbs + jnp.sum(dsv[cs], axis=-1, keepdims=True)
                dvns.append(_mm_tn(wm, dsv[cs]))
            sv = jnp.concatenate(svs, axis=0)
            dvn = jnp.concatenate(dvns, axis=0)
            dw_ref[hd] += _tril(dw)
            dbs_ref[hd] += jnp.broadcast_to(dbs, (CHUNK, CHUNK))
            dg_ref[:, sl] += jnp.sum(dvn * vhat, axis=0, keepdims=True)
            db_ref[:, sl] += jnp.sum(dvn, axis=0, keepdims=True)
            du = dy * _swap64(sv)
            dvh = dvn * g
            m1 = jnp.sum(dvh, axis=-1, keepdims=True) * (1.0 / HEAD_DIM)
            m2 = jnp.sum(dvh * vhat, axis=-1, keepdims=True) * (1.0 / HEAD_DIM)
            dv = jnp.where(lo, 0.0, rstd * (dvh - m1 - vhat * m2))
            dza_ref[:, sl] = (du + dv) * gp[:, sl]

    row = pl.BlockSpec((1, 512), lambda i: (0, 0))
    mat = pl.BlockSpec((A_HEADS, CHUNK, CHUNK), lambda i: (0, 0, 0))
    return pl.pallas_call(
        body, name="gmlp_bwd", grid=(T // tm,),
        in_specs=[pl.BlockSpec((tm, 512), lambda i: (i, 0)), pl.BlockSpec((tm, A_WIDTH), lambda i: (i, 0)), row, row, mat, mat],
        out_specs=[pl.BlockSpec((tm, 512), lambda i: (i, 0)), mat, mat, row, row],
        out_shape=[_sds((T, 512)), _sds((A_HEADS, CHUNK, CHUNK)), _sds((A_HEADS, CHUNK, CHUNK)), _sds((1, 512)), _sds((1, 512))],
        compiler_params=_cp("arbitrary"))(z, dya, lng, lnb, ws, bsx)


def _attn_specs(T, tq, tile_of):
    nb = tq // WINDOW

    def prev(i):
        return jnp.maximum(tile_of(i) * nb - 1, 0)

    row = pl.BlockSpec((1, LANES), lambda i: (0, 0))
    return [
        pl.BlockSpec((tq, B_WIDTH), lambda i: (tile_of(i), 1)),
        pl.BlockSpec((tq, LANES), lambda i: (tile_of(i), 8)),
        pl.BlockSpec((tq, LANES), lambda i: (tile_of(i), 9)),
        pl.BlockSpec((WINDOW, LANES), lambda i: (prev(i), 8)),
        pl.BlockSpec((WINDOW, LANES), lambda i: (prev(i), 9)),
        pl.BlockSpec((tq, LANES), lambda i: (tile_of(i), 0)),
        pl.BlockSpec((tq, LANES), lambda i: (tile_of(i), 0)),
        pl.BlockSpec((WINDOW, LANES), lambda i: (prev(i), 0)),
        pl.BlockSpec((WINDOW, LANES), lambda i: (prev(i), 0)),
        row, row,
        pl.BlockSpec((8, LANES), lambda i: (0, 0)),
    ]


def _attn_bias(first):
    qi = lax.broadcasted_iota(jnp.int32, (WINDOW, 2 * WINDOW), 0)
    kj = lax.broadcasted_iota(jnp.int32, (WINDOW, 2 * WINDOW), 1)
    diff = qi + WINDOW - kj
    ok = (diff >= 0) & (diff < WINDOW) & ((kj >= WINDOW) | jnp.logical_not(first))
    return jnp.where(ok, 0.0, NEG)


def _dup_heads(x, lo):
    sw = _swap64(x)
    return jnp.where(lo, x, sw), jnp.where(lo, sw, x)


HEADS_PER_KV = 4


def _stack_heads(x0, x1, lo):
    return jnp.concatenate([jnp.where(lo, x0, 0.0), jnp.where(lo, 0.0, x0), jnp.where(lo, x1, 0.0), jnp.where(lo, 0.0, x1)], axis=0)


def _unstack_heads(x4, lo):
    return (jnp.where(lo, x4[0:WINDOW], x4[WINDOW:2 * WINDOW]), jnp.where(lo, x4[2 * WINDOW:3 * WINDOW], x4[3 * WINDOW:4 * WINDOW]))


def _sink_column(sk_ref, g):
    return jnp.concatenate([jnp.broadcast_to(sk_ref[a:a + 1, 0:1], (WINDOW, 1)) for a in range(HEADS_PER_KV * g, HEADS_PER_KV * (g + 1))], axis=0)


def _attn_probs(q4, kw, bias, sink):
    s = _mm_nt(q4, kw)
    s = (s.reshape(HEADS_PER_KV, WINDOW, 2 * WINDOW) + bias[None]).reshape(HEADS_PER_KV * WINDOW, 2 * WINDOW)
    m = jnp.maximum(jnp.max(s, axis=-1, keepdims=True), sink)
    p = jnp.exp(s - m)
    es = jnp.exp(sink - m)
    inv = 1.0 / (jnp.sum(p, axis=-1, keepdims=True) + es)
    return p * inv, es * inv


def _attn_fwd(z, cs, sn, qg, kg, sinks):
    T = z.shape[0]
    tq = min(T, 512)
    nb = tq // WINDOW

    def body(q_ref, k_ref, v_ref, kp_ref, vp_ref, cs_ref, sn_ref, csp_ref, snp_ref, qg_ref, kg_ref, sk_ref, o_ref):
        i = pl.program_id(0)
        csq, snq = cs_ref[...], sn_ref[...]
        cs_all = jnp.concatenate([csp_ref[...], csq], axis=0)
        sn_all = jnp.concatenate([snp_ref[...], snq], axis=0)
        k_all = jnp.concatenate([kp_ref[...], k_ref[...]], axis=0)
        v_all = jnp.concatenate([vp_ref[...], v_ref[...]], axis=0)
        kr, _, _ = _qk_norm_rope(k_all, kg_ref[...], cs_all, sn_all)
        lo_all = _lane(k_all.shape) < HEAD_DIM
        kd = _dup_heads(kr, lo_all)
        vd = _dup_heads(v_all, lo_all)
        lo = _lane((WINDOW, LANES)) < HEAD_DIM
        qrs = [_qk_norm_rope(q_ref[:, pr * LANES:(pr + 1) * LANES], qg_ref[...], csq, snq)[0] * SCALE for pr in range(4)]
        biases = [_attn_bias(i * nb + b == 0) for b in range(nb)]
        for g in range(2):
            sink = _sink_column(sk_ref, g)
            for b in range(nb):
                bs = slice(b * WINDOW, (b + 1) * WINDOW)
                ws = slice(b * WINDOW, (b + 2) * WINDOW)
                pn, _ = _attn_probs(_stack_heads(qrs[2 * g][bs], qrs[2 * g + 1][bs], lo), kd[g][ws], biases[b], sink)
                o0, o1 = _unstack_heads(_mm(pn, vd[g][ws]), lo)
                o_ref[bs, 2 * g * LANES:(2 * g + 1) * LANES] = o0
                o_ref[bs, (2 * g + 1) * LANES:(2 * g + 2) * LANES] = o1

    return pl.pallas_call(
        body, name="attn_fwd", grid=(T // tq,),
        in_specs=_attn_specs(T, tq, lambda i: i),
        out_specs=pl.BlockSpec((tq, B_WIDTH), lambda i: (i, 0)), out_shape=_sds((T, B_WIDTH)),
        compiler_params=_cp("parallel"))(z, z, z, z, z, cs, sn, cs, sn, qg, kg, sinks)


def _attn_bwd(z, cs, sn, qg, kg, sinks, o, do):
    T = z.shape[0]
    tq = min(T, 512)
    nb = tq // WINDOW
    nt = T // tq
    tk = tq + WINDOW

    def tile_of(i):
        return nt - 1 - i

    def body(q_ref, k_ref, v_ref, kp_ref, vp_ref, cs_ref, sn_ref, csp_ref, snp_ref, qg_ref, kg_ref, sk_ref, o_ref, do_ref,
             dq_ref, dk_ref, dv_ref, dqg_ref, dkg_ref, dsk_ref, acck, accv, ck, cv):
        i = pl.program_id(0)
        ti = nt - 1 - i

        @pl.when(i == 0)
        def _():
            dqg_ref[...] = jnp.zeros_like(dqg_ref)
            dkg_ref[...] = jnp.zeros_like(dkg_ref)
            dsk_ref[...] = jnp.zeros_like(dsk_ref)
            ck[...] = jnp.zeros_like(ck)
            cv[...] = jnp.zeros_like(cv)

        csq, snq = cs_ref[...], sn_ref[...]
        cs_all = jnp.concatenate([csp_ref[...], csq], axis=0)
        sn_all = jnp.concatenate([snp_ref[...], snq], axis=0)
        k_all = jnp.concatenate([kp_ref[...], k_ref[...]], axis=0)
        v_all = jnp.concatenate([vp_ref[...], v_ref[...]], axis=0)
        kr, kh, rk = _qk_norm_rope(k_all, kg_ref[...], cs_all, sn_all)
        lo_all = _lane(k_all.shape) < HEAD_DIM
        kd = _dup_heads(kr, lo_all)
        vd = _dup_heads(v_all, lo_all)
        lo = _lane((WINDOW, LANES)) < HEAD_DIM
        acck[...] = jnp.zeros_like(acck)
        accv[...] = jnp.zeros_like(accv)
        prep = [_qk_norm_rope(q_ref[:, pr * LANES:(pr + 1) * LANES], qg_ref[...], csq, snq) for pr in range(4)]
        biases = [_attn_bias(ti * nb + b == 0) for b in range(nb)]
        dqs = [[None] * nb for _ in range(4)]
        for g in range(2):
            sink = _sink_column(sk_ref, g)
            dsink = jnp.zeros((HEADS_PER_KV * WINDOW, 1), F32)
            for b in range(nb):
                bs = slice(b * WINDOW, (b + 1) * WINDOW)
                ws = slice(b * WINDOW, (b + 2) * WINDOW)
                kw, vw = kd[g][ws], vd[g][ws]
                q4 = _stack_heads(prep[2 * g][0][bs] * SCALE, prep[2 * g + 1][0][bs] * SCALE, lo)
                pn, psink = _attn_probs(q4, kw, biases[b], sink)
                o0, o1 = o_ref[bs, 2 * g * LANES:(2 * g + 1) * LANES], o_ref[bs, (2 * g + 1) * LANES:(2 * g + 2) * LANES]
                do4 = _stack_heads(do_ref[bs, 2 * g * LANES:(2 * g + 1) * LANES], do_ref[bs, (2 * g + 1) * LANES:(2 * g + 2) * LANES], lo)
                delta = jnp.sum(do4 * jnp.concatenate([o0, o0, o1, o1], axis=0), axis=-1, keepdims=True)
                ds = pn * (_mm_nt(do4, vw) - delta)
                dsink = dsink - psink * delta
                dqs[2 * g][b], dqs[2 * g + 1][b] = _unstack_heads(_mm(ds, kw) * SCALE, lo)
                acck[g, ws, :] += _mm_tn(ds, q4)
                accv[g, ws, :] += _mm_tn(pn, do4)
            for hh in range(HEADS_PER_KV):
                a = HEADS_PER_KV * g + hh
                dsk_ref[a:a + 1, :] += jnp.zeros((1, LANES), F32) + jnp.sum(dsink[hh * WINDOW:(hh + 1) * WINDOW])
        for pr in range(4):
            _, qh, rq = prep[pr]
            dx, dg = _qk_norm_rope_bwd(qh, rq, qg_ref[...], csq, snq, jnp.concatenate(dqs[pr], axis=0))
            dq_ref[:, pr * LANES:(pr + 1) * LANES] = dx
            dqg_ref[...] += dg

        def fold(acc):
            f0 = acc[0] + _swap64(acc[0])
            f1 = acc[1] + _swap64(acc[1])
            return jnp.where(lo_all, f0, f1)

        dk_all = fold(acck)
        dv_all = fold(accv)
        pad = jnp.zeros((tq - WINDOW, LANES), F32)
        dk_own = dk_all[WINDOW:] + (jnp.concatenate([pad, ck[...]], axis=0) if nb > 1 else ck[...])
        dv_own = dv_all[WINDOW:] + (jnp.concatenate([pad, cv[...]], axis=0) if nb > 1 else cv[...])
        ck[...] = dk_all[:WINDOW]
        cv[...] = dv_all[:WINDOW]
        dxk, dgk = _qk_norm_rope_bwd(kh[WINDOW:], rk[WINDOW:], kg_ref[...], csq, snq, dk_own)
        dk_ref[...] = dxk
        dkg_ref[...] += dgk
        dv_ref[...] = dv_own

    row = pl.BlockSpec((1, LANES), lambda i: (0, 0))
    return pl.pallas_call(
        body, name="attn_bwd", grid=(nt,),
        in_specs=_attn_specs(T, tq, tile_of) + [pl.BlockSpec((tq, B_WIDTH), lambda i: (tile_of(i), 0))] * 2,
        out_specs=[pl.BlockSpec((tq, B_WIDTH), lambda i: (tile_of(i), 0)), pl.BlockSpec((tq, LANES), lambda i: (tile_of(i), 0)),
                   pl.BlockSpec((tq, LANES), lambda i: (tile_of(i), 0)), row, row, pl.BlockSpec((8, LANES), lambda i: (0, 0))],
        out_shape=[_sds((T, B_WIDTH)), _sds((T, LANES)), _sds((T, LANES)), _sds((1, LANES)), _sds((1, LANES)), _sds((8, LANES))],
        scratch_shapes=[pltpu.VMEM((2, tk, LANES), F32), pltpu.VMEM((2, tk, LANES), F32),
                        pltpu.VMEM((WINDOW, LANES), F32), pltpu.VMEM((WINDOW, LANES), F32)],
        compiler_params=_cp("arbitrary"))(z, z, z, z, z, cs, sn, cs, sn, qg, kg, sinks, o, do)


def _bbar_t(are, aim, ldt, btr, bti):
    lbr, lbi = _lam_bar(are, aim, ldt)
    den = are * are + aim * aim
    nr = lbr - 1.0
    cr = (nr * are + lbi * aim) / den
    ci = (lbi * are - nr * aim) / den
    return cr * btr - ci * bti, cr * bti + ci * btr


def _lam_bar(are, aim, ldt):
    dt = jnp.exp(ldt)
    er = jnp.exp(are * dt)
    return er * jnp.cos(aim * dt), er * jnp.sin(aim * dt)


def _block_diag(x):
    t = jnp.concatenate([x] * C_GROUPS, axis=1)
    return jnp.where(_row(t.shape) // C_GROUP == _lane(t.shape) // C_STATE, t, 0.0)


def _block_diag_fold(m):
    rg = _row((C_WIDTH, C_STATE)) // C_GROUP
    acc = jnp.zeros((C_WIDTH, C_STATE), F32)
    for g in range(C_GROUPS):
        acc = acc + jnp.where(rg == g, m[:, g * C_STATE:(g + 1) * C_STATE], 0.0)
    return acc


def _ssm_prep(are, aim, ldt, are_x, aim_x, ldt_x, btr, bti, cre, cim):
    def body(are_r, aim_r, ldt_r, arex_r, aimx_r, ldtx_r, btr_r, bti_r, cre_r, cim_r, bbd_ref, cbd_ref, pwr_ref, pwi_ref):
        lr, li = _lam_bar(are_r[...], aim_r[...], ldt_r[...])
        cr, ci = lr, li
        for r in range(SCAN_SEG):
            pwr_ref[r:r + 1, :] = cr
            pwi_ref[r:r + 1, :] = ci
            cr, ci = cr * lr - ci * li, cr * li + ci * lr
        br, bi = _bbar_t(arex_r[...], aimx_r[...], ldtx_r[...], btr_r[...], bti_r[...])
        bbd_ref[...] = jnp.concatenate([_block_diag(br), _block_diag(bi)], axis=1).astype(MXU)
        cbd_ref[...] = jnp.concatenate([_block_diag(cre_r[...]), -_block_diag(cim_r[...])], axis=1).astype(MXU)

    return pl.pallas_call(
        body, name="ssm_prep",
        out_shape=[_sds((C_WIDTH, 2 * N_STATE), MXU), _sds((C_WIDTH, 2 * N_STATE), MXU), _sds((SCAN_SEG, N_STATE)), _sds((SCAN_SEG, N_STATE))],
        compiler_params=pltpu.CompilerParams(vmem_limit_bytes=VMEM_LIMIT))(are, aim, ldt, are_x, aim_x, ldt_x, btr, bti, cre, cim)


def _ssm_param_bwd(are, aim, ldt, are_x, aim_x, ldt_x, btr, bti, dlr, dli, dbbd, dcr, dci):
    def body(are_r, aim_r, ldt_r, arex_r, aimx_r, ldtx_r, btr_r, bti_r, dlr_r, dli_r, dbbd_r, dcr_r, dci_r,
             dare_ref, daim_ref, dldt_ref, dbtr_ref, dbti_ref, dcre_ref, dcim_ref):
        _, vjp_l = jax.vjp(_lam_bar, are_r[...], aim_r[...], ldt_r[...])
        da1, di1, dl1 = vjp_l((dlr_r[...], dli_r[...]))
        dbr = _block_diag_fold(dbbd_r[:, 0:N_STATE])
        dbi = _block_diag_fold(dbbd_r[:, N_STATE:2 * N_STATE])
        _, vjp_b = jax.vjp(_bbar_t, arex_r[...], aimx_r[...], ldtx_r[...], btr_r[...], bti_r[...])
        da2, di2, dl2, dbtr, dbti = vjp_b((dbr, dbi))

        def gsum(x):
            return x.reshape(C_GROUPS, C_GROUP, C_STATE).sum(axis=1)

        dare_ref[...] = da1 + gsum(da2)
        daim_ref[...] = di1 + gsum(di2)
        dldt_ref[...] = jnp.broadcast_to(jnp.sum(dl1 + gsum(dl2), axis=-1, keepdims=True), (C_GROUPS, LANES))
        dbtr_ref[...] = dbtr
        dbti_ref[...] = dbti
        dcre_ref[...] = _block_diag_fold(dcr_r[...])
        dcim_ref[...] = -_block_diag_fold(dci_r[...])

    g = _sds((C_GROUPS, C_STATE))
    x = _sds((C_WIDTH, C_STATE))
    return pl.pallas_call(
        body, name="ssm_param_bwd", out_shape=[g, g, _sds((C_GROUPS, LANES)), x, x, x, x],
        compiler_params=pltpu.CompilerParams(vmem_limit_bytes=VMEM_LIMIT))(are, aim, ldt, are_x, aim_x, ldt_x, btr, bti, dlr, dli, dbbd, dcr, dci)


SCAN_TILE = 512
SCAN_SEG = 8


def _scan_tables(pwr_ref, pwi_ref, conj, reverse):
    row = _row((SCAN_SEG, N_STATE))
    shifts = []
    for k in (1, 2, 4):
        keep = (row < SCAN_SEG - k) if reverse else (row >= k)
        ar = jnp.broadcast_to(pwr_ref[k - 1:k, :], (SCAN_SEG, N_STATE))
        ai = jnp.broadcast_to(pwi_ref[k - 1:k, :], (SCAN_SEG, N_STATE)) * conj
        shifts.append((SCAN_SEG - k if reverse else k, jnp.where(keep, ar, 0.0), jnp.where(keep, ai, 0.0)))
    if reverse:
        pr = jnp.concatenate([pwr_ref[SCAN_SEG - 1 - r:SCAN_SEG - r, :] for r in range(SCAN_SEG)], axis=0)
        pi = jnp.concatenate([pwi_ref[SCAN_SEG - 1 - r:SCAN_SEG - r, :] for r in range(SCAN_SEG)], axis=0) * conj
    else:
        pr, pi = pwr_ref[...], pwi_ref[...] * conj
    return shifts, (pr, pi)


def _tile_scan(xr_ref, xi_ref, pwr_ref, pwi_ref, sr, si, conj, reverse):
    shifts, (pr, pi) = _scan_tables(pwr_ref, pwi_ref, conj, reverse)
    groups = xr_ref.shape[0] // SCAN_SEG
    out_row = 0 if reverse else SCAN_SEG - 1

    def step(k, c):
        cr, ci = c
        g = groups - 1 - k if reverse else k
        rows = pl.ds(pl.multiple_of(g * SCAN_SEG, SCAN_SEG), SCAN_SEG)
        xr, xi = xr_ref[rows, :], xi_ref[rows, :]
        for amount, ar, ai in shifts:
            qr, qi = pltpu.roll(xr, amount, 0), pltpu.roll(xi, amount, 0)
            xr, xi = xr + ar * qr - ai * qi, xi + ar * qi + ai * qr
        xr, xi = xr + pr * cr - pi * ci, xi + pr * ci + pi * cr
        xr_ref[rows, :] = xr
        xi_ref[rows, :] = xi
        return xr[out_row:out_row + 1], xi[out_row:out_row + 1]

    cr, ci = lax.fori_loop(0, groups, step, (sr[...], si[...]), unroll=2)
    sr[...] = cr
    si[...] = ci


def _ssm_fwd(z, bbd, pwr, pwi, cbd, dsk, w1, w2):
    T = z.shape[0]
    tt = min(T, 2 * SCAN_TILE)

    def body(u_ref, bbd_ref, pwr_ref, pwi_ref, cbd_ref, d_ref, w1_ref, w2_ref, yc_ref, xr_ref, xi_ref, sr, si):
        @pl.when(pl.program_id(0) == 0)
        def _():
            sr[...] = jnp.zeros_like(sr)
            si[...] = jnp.zeros_like(si)

        u = u_ref[...]
        bu = _mm(u, bbd_ref[...])
        xr_ref[...] = bu[:, 0:N_STATE]
        xi_ref[...] = bu[:, N_STATE:2 * N_STATE]
        _tile_scan(xr_ref, xi_ref, pwr_ref, pwi_ref, sr, si, 1.0, False)
        x = jnp.concatenate([xr_ref[...], xi_ref[...]], axis=1)
        y2 = _gelu(_mm_nt(x, cbd_ref[...]) + d_ref[...] * u)
        yc_ref[...] = _mm(y2, w1_ref[...]) * _sigmoid(_mm(y2, w2_ref[...]))

    big = pl.BlockSpec((C_WIDTH, 2 * N_STATE), lambda i: (0, 0))
    tab = pl.BlockSpec((SCAN_SEG, N_STATE), lambda i: (0, 0))
    wsp = pl.BlockSpec((C_WIDTH, C_WIDTH), lambda i: (0, 0))
    xs = pl.BlockSpec((tt, N_STATE), lambda i: (i, 0))
    return pl.pallas_call(
        body, name="ssm_fwd", grid=(T // tt,),
        in_specs=[pl.BlockSpec((tt, C_WIDTH), lambda i: (i, 5)), big, tab, tab, big, pl.BlockSpec((1, C_WIDTH), lambda i: (0, 0)), wsp, wsp],
        out_specs=[pl.BlockSpec((tt, C_WIDTH), lambda i: (i, 0)), xs, xs],
        out_shape=[_sds((T, C_WIDTH)), _sds((T, N_STATE)), _sds((T, N_STATE))],
        scratch_shapes=[pltpu.VMEM((1, N_STATE), F32)] * 2,
        compiler_params=_cp("arbitrary"))(z, bbd, pwr, pwi, cbd, dsk, w1, w2)


def _ssm_bwd(dyc, z, xr, xi, bbd, pwr, pwi, cbd, dsk, w1, w2):
    T = z.shape[0]
    tt = min(T, SCAN_TILE)
    nt = T // tt

    def tile_of(i):
        return nt - 1 - i

    def body(dyc_ref, u_ref, xr_ref, xi_ref, xpr_ref, xpi_ref, bbd_ref, pwr_ref, pwi_ref, cbd_ref, d_ref, w1_ref, w2_ref,
             du_ref, dw1_ref, dw2_ref, dcr_ref, dci_ref, dbbd_ref, dlr_ref, dli_ref, dd_ref, gr, gi, sr, si):
        i = pl.program_id(0)
        ti = nt - 1 - i

        @pl.when(i == 0)
        def _():
            sr[...] = jnp.zeros_like(sr)
            si[...] = jnp.zeros_like(si)
            for acc in (dw1_ref, dw2_ref, dcr_ref, dci_ref, dbbd_ref, dlr_ref, dli_ref, dd_ref):
                acc[...] = jnp.zeros_like(acc)

        u = u_ref[...]
        xr_t, xi_t = xr_ref[...], xi_ref[...]
        y = _mm_nt(jnp.concatenate([xr_t, xi_t], axis=1), cbd_ref[...]) + d_ref[...] * u
        y2 = _gelu(y)
        a1 = _mm(y2, w1_ref[...])
        sg = _sigmoid(_mm(y2, w2_ref[...]))
        dyc_t = dyc_ref[...]
        da1 = dyc_t * sg
        da2 = dyc_t * a1 * sg * (1.0 - sg)
        dy = (_mm_nt(da1, w1_ref[...]) + _mm_nt(da2, w2_ref[...])) * _gelu_grad(y)
        gx = _mm(dy, cbd_ref[...])
        gr[...] = gx[:, 0:N_STATE]
        gi[...] = gx[:, N_STATE:2 * N_STATE]
        _tile_scan(gr, gi, pwr_ref, pwi_ref, sr, si, -1.0, True)
        ar, ai = gr[...], gi[...]
        first_row = _row(ar.shape) == 0
        live = jnp.where(ti > 0, 1.0, 0.0)
        xsr = jnp.where(first_row, xpr_ref[7:8, :] * live, pltpu.roll(xr_t, 1, 0))
        xsi = jnp.where(first_row, xpi_ref[7:8, :] * live, pltpu.roll(xi_t, 1, 0))
        dlr_ref[...] += jnp.sum(ar * xsr + ai * xsi, axis=0, keepdims=True)
        dli_ref[...] += jnp.sum(ai * xsr - ar * xsi, axis=0, keepdims=True)
        dd_ref[...] += jnp.sum(dy * u, axis=0, keepdims=True)
        arai = jnp.concatenate([ar, ai], axis=1)
        du_ref[...] = _mm_nt(arai, bbd_ref[...]) + d_ref[...] * dy
        dw1_ref[...] += _mm_tn(y2, da1)
        dw2_ref[...] += _mm_tn(y2, da2)
        dcr_ref[...] += _mm_tn(dy, xr_t)
        dci_ref[...] += _mm_tn(dy, xi_t)
        dbbd_ref[...] += _mm_tn(u, arai)

    def prev(i):
        return jnp.maximum(tile_of(i) * (tt // 8) - 1, 0)

    big = pl.BlockSpec((C_WIDTH, 2 * N_STATE), lambda i: (0, 0))
    tab = pl.BlockSpec((SCAN_SEG, N_STATE), lambda i: (0, 0))
    srow = pl.BlockSpec((1, N_STATE), lambda i: (0, 0))
    wsp = pl.BlockSpec((C_WIDTH, C_WIDTH), lambda i: (0, 0))
    xs = pl.BlockSpec((tt, N_STATE), lambda i: (tile_of(i), 0))
    xp = pl.BlockSpec((8, N_STATE), lambda i: (prev(i), 0))
    cw = pl.BlockSpec((tt, C_WIDTH), lambda i: (tile_of(i), 0))
    drow = pl.BlockSpec((1, C_WIDTH), lambda i: (0, 0))
    return pl.pallas_call(
        body, name="ssm_bwd", grid=(nt,),
        in_specs=[cw, pl.BlockSpec((tt, C_WIDTH), lambda i: (tile_of(i), 5)), xs, xs, xp, xp, big, tab, tab, big, drow, wsp, wsp],
        out_specs=[cw, wsp, wsp, pl.BlockSpec((C_WIDTH, N_STATE), lambda i: (0, 0)), pl.BlockSpec((C_WIDTH, N_STATE), lambda i: (0, 0)), big,
                   srow, srow, drow],
        out_shape=[_sds((T, C_WIDTH)), _sds((C_WIDTH, C_WIDTH)), _sds((C_WIDTH, C_WIDTH)), _sds((C_WIDTH, N_STATE)), _sds((C_WIDTH, N_STATE)),
                   _sds((C_WIDTH, 2 * N_STATE)), _sds((1, N_STATE)), _sds((1, N_STATE)), _sds((1, C_WIDTH))],
        scratch_shapes=[pltpu.VMEM((tt, N_STATE), F32)] * 2 + [pltpu.VMEM((1, N_STATE), F32)] * 2,
        compiler_params=_cp("arbitrary"))(dyc, z, xr, xi, xr, xi, bbd, pwr, pwi, cbd, dsk, w1, w2)


_GROUPS = ((0, A_WIDTH), (A_WIDTH, A_WIDTH + B_WIDTH), (A_WIDTH + B_WIDTH, D_MODEL))


def _merge_fwd(h, ya, yb, yc, g, w):
    T = h.shape[0]
    tm = min(T, 512)

    def body(h_ref, a_ref, b_ref, c_ref, g_ref, w_ref, o_ref):
        yn = jnp.concatenate([y * _rms_stat(y) for y in (a_ref[...], b_ref[...], c_ref[...])], axis=1) * g_ref[...]
        o_ref[...] = h_ref[...] + _mm(yn, w_ref[...])

    def rows(w_):
        return pl.BlockSpec((tm, w_), lambda i: (i, 0))

    return pl.pallas_call(
        body, name="merge_fwd", grid=(T // tm,),
        in_specs=[rows(D_MODEL), rows(A_WIDTH), rows(B_WIDTH), rows(C_WIDTH), pl.BlockSpec((1, D_MODEL), lambda i: (0, 0)),
                  pl.BlockSpec((D_MODEL, D_MODEL), lambda i: (0, 0))],
        out_specs=rows(D_MODEL), out_shape=_sds((T, D_MODEL)), compiler_params=_cp("parallel"))(h, ya, yb, yc, g, w)


def _merge_bwd(gres, ya, yb, yc, g, w):
    T = gres.shape[0]
    tm = min(T, 512)

    def body(gr_ref, a_ref, b_ref, c_ref, g_ref, w_ref, da_ref, db_ref, dc_ref, yn_ref, dg_ref):
        @pl.when(pl.program_id(0) == 0)
        def _():
            dg_ref[...] = jnp.zeros_like(dg_ref)

        dyn = _mm_nt(gr_ref[...], w_ref[...])
        yns, dgs = [], []
        for (c0, c1), y_ref, d_ref in zip(_GROUPS, (a_ref, b_ref, c_ref), (da_ref, db_ref, dc_ref)):
            y = y_ref[...]
            r = _rms_stat(y)
            gg = g_ref[:, c0:c1]
            dx, dg = _rms_bwd(y, r, gg, dyn[:, c0:c1])
            d_ref[...] = dx
            dgs.append(dg)
            yns.append(y * r * gg)
        dg_ref[...] += jnp.concatenate(dgs, axis=1)
        yn_ref[...] = jnp.concatenate(yns, axis=1).astype(MXU)

    def rows(w_):
        return pl.BlockSpec((tm, w_), lambda i: (i, 0))

    row = pl.BlockSpec((1, D_MODEL), lambda i: (0, 0))
    return pl.pallas_call(
        body, name="merge_bwd", grid=(T // tm,),
        in_specs=[rows(D_MODEL), rows(A_WIDTH), rows(B_WIDTH), rows(C_WIDTH), row, pl.BlockSpec((D_MODEL, D_MODEL), lambda i: (0, 0))],
        out_specs=[rows(A_WIDTH), rows(B_WIDTH), rows(C_WIDTH), rows(D_MODEL), row],
        out_shape=[_sds((T, A_WIDTH)), _sds((T, B_WIDTH)), _sds((T, C_WIDTH)), _sds((T, D_MODEL), MXU), _sds((1, D_MODEL))],
        compiler_params=_cp("arbitrary"))(gres, ya, yb, yc, g, w)


FF_BLOCK = D_FF // N_DEV


def _load_weights(w1_hbm, w2_hbm, w1, w2, sem):
    @pl.when(pl.program_id(0) == 0)
    def _():
        copies = [pltpu.make_async_copy(w1_hbm.at[j], w1.at[:, pl.ds(j * FF_BLOCK, FF_BLOCK)], sem.at[j]) for j in range(N_DEV)]
        copies.append(pltpu.make_async_copy(w2_hbm, w2, sem.at[N_DEV]))
        for cp in copies:
            cp.start()
        for cp in copies:
            cp.wait()


def _mlp_weight_scratch():
    return [pltpu.VMEM((D_MODEL, D_FF), MXU), pltpu.VMEM((D_FF, D_MODEL), MXU), pltpu.SemaphoreType.DMA((N_DEV + 1,))]


def _mlp_fwd(h, g, w1, w2):
    T = h.shape[0]
    tm = min(T, 512)

    def body(h_ref, g_ref, w1_hbm, w2_hbm, o_ref, r_ref, w1_v, w2_v, sem):
        _load_weights(w1_hbm, w2_hbm, w1_v, w2_v, sem)
        x = h_ref[...]
        a = jnp.maximum(_mm(x * _rms_stat(x) * g_ref[...], w1_v[...]), 0.0)
        r = (a * a).astype(MXU)
        r_ref[...] = r
        o_ref[...] = x + _mm(r, w2_v[...])

    rows = pl.BlockSpec((tm, D_MODEL), lambda i: (i, 0))
    hbm = pl.BlockSpec(memory_space=pl.ANY)
    return pl.pallas_call(
        body, name="mlp_fwd", grid=(T // tm,),
        in_specs=[rows, pl.BlockSpec((1, D_MODEL), lambda i: (0, 0)), hbm, hbm],
        out_specs=[rows, pl.BlockSpec((tm, D_FF), lambda i: (i, 0))],
        out_shape=[_sds((T, D_MODEL)), _sds((T, D_FF), MXU)],
        scratch_shapes=_mlp_weight_scratch(), compiler_params=_cp("arbitrary"))(h, g, w1, w2)


def _mlp_bwd(gres, h, g, r, w1, w2):
    T = h.shape[0]
    tm = min(T, 256)

    def body(gr_ref, h_ref, g_ref, r_ref, w1_hbm, w2_hbm, dh_ref, hn_ref, da_ref, dg_ref, w1_v, w2_v, sem):
        _load_weights(w1_hbm, w2_hbm, w1_v, w2_v, sem)

        @pl.when(pl.program_id(0) == 0)
        def _():
            dg_ref[...] = jnp.zeros_like(dg_ref)

        gr = gr_ref[...]
        da = (_mm_nt(gr, w2_v[...]) * (2.0 * jnp.sqrt(r_ref[...].astype(F32)))).astype(MXU)
        da_ref[...] = da
        x = h_ref[...]
        rs = _rms_stat(x)
        gg = g_ref[...]
        dx, dg = _rms_bwd(x, rs, gg, _mm_nt(da, w1_v[...]))
        dh_ref[...] = gr + dx
        dg_ref[...] += dg
        hn_ref[...] = (x * rs * gg).astype(MXU)

    rows = pl.BlockSpec((tm, D_MODEL), lambda i: (i, 0))
    row = pl.BlockSpec((1, D_MODEL), lambda i: (0, 0))
    wide = pl.BlockSpec((tm, D_FF), lambda i: (i, 0))
    hbm = pl.BlockSpec(memory_space=pl.ANY)
    return pl.pallas_call(
        body, name="mlp_bwd", grid=(T // tm,),
        in_specs=[rows, rows, row, wide, hbm, hbm], out_specs=[rows, rows, wide, row],
        out_shape=[_sds((T, D_MODEL)), _sds((T, D_MODEL), MXU), _sds((T, D_FF), MXU), _sds((1, D_MODEL))],
        scratch_shapes=_mlp_weight_scratch(), compiler_params=_cp("arbitrary"))(gres, h, g, r, w1, w2)


def _ple_fwd(h, p, l, g, wg, wp):
    T = h.shape[0]
    tm = min(T, 512)

    def body(h_ref, p_ref, g_ref, wg_ref, wp_ref, o_ref):
        x = h_ref[...]
        gate = _sigmoid(_mm(x * _rms_stat(x) * g_ref[...], wg_ref[...]))
        o_ref[...] = x + gate * _mm(p_ref[...], wp_ref[...])

    rows = pl.BlockSpec((tm, D_MODEL), lambda i: (i, 0))
    return pl.pallas_call(
        body, name="ple_fwd", grid=(T // tm,),
        in_specs=[rows, pl.BlockSpec((None, tm, PLE_DIM), lambda i: (l, i, 0)), pl.BlockSpec((1, D_MODEL), lambda i: (0, 0)),
                  pl.BlockSpec((D_MODEL, D_MODEL), lambda i: (0, 0)), pl.BlockSpec((PLE_DIM, D_MODEL), lambda i: (0, 0))],
        out_specs=rows, out_shape=_sds((T, D_MODEL)), compiler_params=_cp("parallel"))(h, p, g, wg, wp)


def _ple_bwd(gres, h, p, l, g, wg, wp):
    T = h.shape[0]
    tm = min(T, 512)

    def body(gr_ref, h_ref, p_ref, g_ref, wg_ref, wp_ref, dh_ref, hn_ref, dgp_ref, de_ref, dg_ref):
        @pl.when(pl.program_id(0) == 0)
        def _():
            dg_ref[...] = jnp.zeros_like(dg_ref)

        x = h_ref[...]
        r = _rms_stat(x)
        gg = g_ref[...]
        hn = x * r * gg
        gate = _sigmoid(_mm(hn, wg_ref[...]))
        e = _mm(p_ref[...], wp_ref[...])
        gr = gr_ref[...]
        dgp = gr * e * gate * (1.0 - gate)
        dx, dg = _rms_bwd(x, r, gg, _mm_nt(dgp, wg_ref[...]))
        dh_ref[...] = gr + dx
        dg_ref[...] += dg
        hn_ref[...] = hn.astype(MXU)
        dgp_ref[...] = dgp.astype(MXU)
        de_ref[...] = (gr * gate).astype(MXU)

    rows = pl.BlockSpec((tm, D_MODEL), lambda i: (i, 0))
    row = pl.BlockSpec((1, D_MODEL), lambda i: (0, 0))
    return pl.pallas_call(
        body, name="ple_bwd", grid=(T // tm,),
        in_specs=[rows, rows, pl.BlockSpec((None, tm, PLE_DIM), lambda i: (l, i, 0)), row,
                  pl.BlockSpec((D_MODEL, D_MODEL), lambda i: (0, 0)), pl.BlockSpec((PLE_DIM, D_MODEL), lambda i: (0, 0))],
        out_specs=[rows, rows, rows, rows, row],
        out_shape=[_sds((T, D_MODEL))] + [_sds((T, D_MODEL), MXU)] * 3 + [_sds((1, D_MODEL))],
        compiler_params=_cp("arbitrary"))(gres, h, p, g, wg, wp)


def _loss_head(h, target):
    T = h.shape[0]
    tm = min(T, 1024)

    def body(h_ref, t_ref, dh_ref, l_ref):
        @pl.when(pl.program_id(0) == 0)
        def _():
            l_ref[...] = jnp.zeros_like(l_ref)

        e = h_ref[...] - t_ref[...]
        dh_ref[...] = e * (1.0 / D_MODEL)
        l_ref[...] += jnp.zeros_like(l_ref) + 0.5 * jnp.sum(jnp.mean(e * e, axis=-1, keepdims=True))

    rows = pl.BlockSpec((tm, D_MODEL), lambda i: (i, 0))
    return pl.pallas_call(
        body, name="loss_head", grid=(T // tm,), in_specs=[rows, rows],
        out_specs=[rows, pl.BlockSpec((8, LANES), lambda i: (0, 0))], out_shape=[_sds((T, D_MODEL)), _sds((8, LANES))],
        compiler_params=_cp("arbitrary"))(h, target)


TN_ROWS = 2048

def _tn(a, b, m, n, *, bm, bn, a_off=0, b_off=0, a_lead=None, n_major=False, split=None, dtype=F32, name="tn"):
    T = a.shape[-2]
    tk = min(T, TN_ROWS)
    nk = T // tk
    b_spec = pl.BlockSpec((tk, bn), lambda i, j, k: (k, b_off + j))
    if a_lead is None:
        a_spec = pl.BlockSpec((tk, bm), lambda i, j, k: (k, a_off + i))
    else:
        a_spec = pl.BlockSpec((None, tk, bm), lambda i, j, k: (a_lead, k, a_off + i))
    assert m % bm == 0 and n % bn == 0 and (not n_major or bm == m) and (split is None or (bn == n and n % split == 0))

    def body(a_ref, b_ref, o_ref, acc):
        k = pl.program_id(2)

        @pl.when(k == 0)
        def _():
            acc[...] = jnp.zeros_like(acc)

        acc[...] += _mm_tn(a_ref[...], b_ref[...])

        @pl.when(k == nk - 1)
        def _():
            if split is None:
                o_ref[...] = acc[...].astype(dtype)
            else:
                for d in range(n // split):
                    o_ref[d] = acc[:, d * split:(d + 1) * split].astype(dtype)

    if split is not None:
        out_spec = pl.BlockSpec((n // split, bm, split), lambda i, j, k: (0, i, 0))
        out_shape = _sds((n // split, m, split), dtype)
    elif n_major:
        out_spec = pl.BlockSpec((None, bm, bn), lambda i, j, k: (j, 0, 0))
        out_shape = _sds((n // bn, m, bn), dtype)
    else:
        out_spec = pl.BlockSpec((bm, bn), lambda i, j, k: (i, j))
        out_shape = _sds((m, n), dtype)
    return pl.pallas_call(
        body, name=name, grid=(m // bm, n // bn, nk),
        in_specs=[a_spec, b_spec],
        out_specs=out_spec, out_shape=out_shape, scratch_shapes=[pltpu.VMEM((bm, bn), F32)],
        compiler_params=_cp("parallel", "parallel", "arbitrary"))(a, b)


def _row_tile(R, C):
    for cand in (512, 256, 128, 64, 32, 16, 8):
        if R % cand == 0 and cand * C * 4 <= 2 ** 20:
            return cand
    return R


def _sum_slots(land):
    S, R, C = land.shape
    tr = _row_tile(R, C)

    def body(l_ref, o_ref):
        acc = l_ref[0].astype(F32)
        for s in range(1, S):
            acc = acc + l_ref[s].astype(F32)
        o_ref[...] = acc

    return pl.pallas_call(
        body, name="sum_slots", grid=(R // tr,), in_specs=[pl.BlockSpec((S, tr, C), lambda i: (0, i, 0))],
        out_specs=pl.BlockSpec((tr, C), lambda i: (i, 0)), out_shape=_sds((R, C)), compiler_params=_cp("parallel"))(land)


def _sum_adamw(land, w, m, v):
    R, C = w.shape
    S = land.shape[0]
    tr = _row_tile(R, C)

    def body(l_ref, w_ref, m_ref, v_ref, g_ref, d_ref, nm_ref, nv_ref):
        gg = l_ref[0].astype(F32)
        for s in range(1, S):
            gg = gg + l_ref[s].astype(F32)
        g_ref[...] = gg
        nm = ADAM_B1 * m_ref[...] + (1.0 - ADAM_B1) * gg
        nv = ADAM_B2 * v_ref[...] + (1.0 - ADAM_B2) * (gg * gg)
        m_hat = nm / (1.0 - ADAM_B1 ** ADAM_STEP)
        v_hat = nv / (1.0 - ADAM_B2 ** ADAM_STEP)
        d_ref[...] = -ADAM_LR * (m_hat / (jnp.sqrt(v_hat) + ADAM_EPS) + ADAM_WD * w_ref[...])
        nm_ref[...] = nm
        nv_ref[...] = nv

    blk = pl.BlockSpec((tr, C), lambda i: (i, 0))
    return pl.pallas_call(
        body, name="sum_adamw", grid=(R // tr,), in_specs=[pl.BlockSpec((S, tr, C), lambda i: (0, i, 0))] + [blk] * 3,
        out_specs=[blk] * 4, out_shape=[_sds((R, C))] * 4, compiler_params=_cp("parallel"))(land, w, m, v)


def _all_to_all(pairs, name):
    n = len(pairs)

    def body(*refs):
        srcs, lands = refs[:n], refs[2 * n:3 * n]
        send, recv, loc = refs[3 * n:]
        x, y, c = lax.axis_index("x"), lax.axis_index("y"), lax.axis_index("c")
        me = 4 * x + 2 * y + c
        own = [pltpu.make_async_copy(pairs[t][2](srcs[t], me, me), pairs[t][3](lands[t], me), loc.at[t]) for t in range(n)]
        for cp in own:
            cp.start()
        sends, recvs = [], []
        for k in range(1, N_DEV):
            px, py, pc = x ^ (k >> 2), y ^ ((k >> 1) & 1), c ^ (k & 1)
            peer = 4 * px + 2 * py + pc
            for t in range(n):
                src = pairs[t][2](srcs[t], me, peer)
                cp = pltpu.make_async_remote_copy(src_ref=src, dst_ref=pairs[t][3](lands[t], me), send_sem=send.at[t, k],
                                                  recv_sem=recv.at[t, k], device_id=(px, py, pc), device_id_type=MESH)
                cp.start()
                sends.append(cp)
                recvs.append(pltpu.make_async_remote_copy(src_ref=src, dst_ref=pairs[t][3](lands[t], peer), send_sem=send.at[t, k],
                                                          recv_sem=recv.at[t, k], device_id=(px, py, pc), device_id_type=MESH))
        for cp in recvs:
            cp.wait_recv()
        for cp in sends:
            cp.wait_send()
        for cp in own:
            cp.wait()

    anyspec = pl.BlockSpec(memory_space=pl.ANY)
    lands = [pr[1] for pr in pairs]
    return pl.pallas_call(
        body, name=name, in_specs=[anyspec] * (2 * n), out_specs=[anyspec] * n,
        out_shape=[_sds(a.shape, a.dtype) for a in lands], input_output_aliases={n + t: t for t in range(n)},
        scratch_shapes=[pltpu.SemaphoreType.DMA((n, N_DEV)), pltpu.SemaphoreType.DMA((n, N_DEV)), pltpu.SemaphoreType.DMA((n,))],
        compiler_params=pltpu.CompilerParams(has_side_effects=True))(*[pr[0] for pr in pairs], *lands)


def _gather_whole(src):
    land = lax.empty((N_DEV,) + src.shape, src.dtype)
    return _all_to_all([(src, land, lambda ref, me, peer: ref, lambda ref, sender: ref.at[sender])], "gather_small_grads")[0]


_HBM = pl.BlockSpec(memory_space=pltpu.HBM)
_SEM = pl.BlockSpec(memory_space=pltpu.SEMAPHORE)
_EFFECT = pltpu.SideEffectType.DATAFLOW_SIDE_EFFECTING


def _peers():
    x, y, c = lax.axis_index("x"), lax.axis_index("y"), lax.axis_index("c")
    out = []
    for k in range(1, N_DEV):
        px, py, pc = x ^ (k >> 2), y ^ ((k >> 1) & 1), c ^ (k & 1)
        out.append((k, (px, py, pc), 4 * px + 2 * py + pc))
    return 4 * x + 2 * y + c, out


def _route(mode, layer):
    if mode == "all":
        return (lambda ref, peer: ref), (lambda ref, sender: ref.at[sender])
    return (lambda ref, peer: ref.at[peer]), (lambda ref, sender: ref.at[sender, layer])


def _split_start(srcs, lands, modes, layer, after, name):
    n = len(srcs)
    routes = [_route(m, layer) for m in modes]

    def body(*refs):
        src, land = refs[:n], refs[n:2 * n]
        send, recv, token = refs[2 * n + 1], refs[2 * n + 2], refs[-1]
        me, peers = _peers()
        for k, dev, peer in peers:
            for t, (src_of, dst_of) in enumerate(routes):
                pltpu.make_async_remote_copy(src_ref=src_of(src[t], peer), dst_ref=dst_of(land[t], me), send_sem=send.at[t * N_DEV + k],
                                             recv_sem=recv.at[t * N_DEV + k], device_id=dev, device_id_type=MESH).start()
        token[...] = jnp.zeros_like(token)

    bufs = list(srcs) + list(lands)
    outs = pl.pallas_call(
        body, name=name,
        out_shape=(pltpu.SemaphoreType.DMA((n * N_DEV,)), pltpu.SemaphoreType.DMA((n * N_DEV,)),
                   *[pltpu.HBM(a.shape, a.dtype) for a in bufs], _sds((8, LANES))),
        in_specs=[_HBM] * (2 * n) + [pl.BlockSpec(memory_space=pl.ANY)],
        out_specs=(_SEM, _SEM, *[_HBM] * (2 * n), pl.BlockSpec(memory_space=pltpu.VMEM)),
        input_output_aliases={i: 2 + i for i in range(2 * n)},
        compiler_params=pltpu.CompilerParams(has_side_effects=_EFFECT),
    )(*[pltpu.with_memory_space_constraint(a, pltpu.HBM) for a in bufs], after)
    return outs[0], outs[1], list(outs[2:2 + n]), list(outs[2 + n:2 + 2 * n]), outs[-1]


def _split_wait(send, recv, srcs, lands, modes, layer, after, name):
    n = len(srcs)
    routes = [_route(m, layer) for m in modes]

    def body(*refs):
        src, land = refs[:n], refs[n:2 * n]
        send_r, recv_r = refs[2 * n], refs[2 * n + 1]
        _, peers = _peers()
        for k, dev, peer in peers:
            for t, (src_of, dst_of) in enumerate(routes):
                cp = pltpu.make_async_remote_copy(src_ref=src_of(src[t], peer), dst_ref=dst_of(land[t], peer), send_sem=send_r.at[t * N_DEV + k],
                                                  recv_sem=recv_r.at[t * N_DEV + k], device_id=dev, device_id_type=MESH)
                cp.wait_send()
                cp.wait_recv()

    bufs = list(srcs) + list(lands)
    outs = pl.pallas_call(
        body, name=name, out_shape=tuple(pltpu.HBM(a.shape, a.dtype) for a in bufs),
        in_specs=[_HBM] * (2 * n) + [_SEM, _SEM, pl.BlockSpec(memory_space=pl.ANY)], out_specs=[_HBM] * (2 * n),
        input_output_aliases={i: i for i in range(2 * n)},
        compiler_params=pltpu.CompilerParams(has_side_effects=_EFFECT),
    )(*bufs, send, recv, after)
    return list(outs[:n]), list(outs[n:])


SHARDED = ("w_in", "glu_w1", "glu_w2", "w_out", "w_ff1", "w_ff2", "w_ple_gate", "w_ple_proj")
SMALL = ("attn_norm_g", "gmlp_ln_g", "gmlp_ln_b", "gmlp_ws", "gmlp_bs", "q_norm_g", "k_norm_g", "sinks", "ssm_a_re", "ssm_a_im",
         "ssm_log_dt", "ssm_b_re", "ssm_b_im", "ssm_c_re", "ssm_c_im", "ssm_d", "mix_out_g", "mlp_norm_g", "ple_norm_g")
WEIGHTS = ("attn_norm_g", "w_in", "gmlp_ln_g", "gmlp_ln_b", "gmlp_ws", "gmlp_bs", "q_norm_g", "k_norm_g", "sinks", "ssm_a_re", "ssm_a_im",
           "ssm_log_dt", "ssm_b_re", "ssm_b_im", "ssm_c_re", "ssm_c_im", "ssm_d", "glu_w1", "glu_w2", "mix_out_g", "w_out", "mlp_norm_g",
           "w_ff1", "w_ff2", "ple_norm_g", "w_ple_gate", "w_ple_proj")
FLAT_COLS = 1024


PACK_TILE_ROWS = 8
PACK_ROWS_MULTIPLE = PACK_TILE_ROWS * N_DEV


def _packed_rows(shape):
    return -(-math.prod(shape) // (PACK_TILE_ROWS * FLAT_COLS)) * PACK_TILE_ROWS


def _pack(arrs, dtype):
    blocks = []
    for a in arrs:
        flat = a.astype(dtype).reshape(-1)
        pad = _packed_rows(a.shape) * FLAT_COLS - flat.shape[0]
        if pad:
            flat = jnp.concatenate([flat, jnp.zeros((pad,), dtype)])
        blocks.append(flat.reshape(-1, FLAT_COLS))
    rows = sum(b.shape[0] for b in blocks)
    if rows % PACK_ROWS_MULTIPLE:
        blocks.append(jnp.zeros((PACK_ROWS_MULTIPLE - rows % PACK_ROWS_MULTIPLE, FLAT_COLS), dtype))
    return jnp.concatenate(blocks, axis=0)


def _unpack(flat, shapes):
    out, r = [], 0
    for s in shapes:
        nr = _packed_rows(s)
        out.append(flat[r:r + nr].reshape(-1)[:math.prod(s)].reshape(s))
        r += nr
    return out


def _from_col_major(s):
    n, rows, cs = s.shape
    return s.transpose(1, 0, 2).reshape(rows, n * cs)


EARLY = ("w_in", "glu_w1", "glu_w2")
LATE = ("w_out", "w_ff1", "w_ff2", "w_ple_gate", "w_ple_proj")
GRADS_MID = ("w_ff1", "w_ff2", "w_ple_gate", "w_ple_proj")
GRADS_END = ("w_in", "glu_w1", "glu_w2", "w_out")


def _layer_fwd(h, p, l, cs, sn, W, late_weights, S, sp):
    z = _inproj_fwd(h, S["attn_norm_g"], W["w_in"])
    ya = _gmlp_fwd(z, S["lng"], S["lnb"], S["gmlp_ws"], S["bsx"])
    yb = _attn_fwd(z, cs, sn, S["qg"], S["kg"], S["sinks"])
    yc, xr, xi = _ssm_fwd(z, sp["bbd"], sp["pwr"], sp["pwi"], sp["cbd"], S["ssm_d"], W["glu_w1"], W["glu_w2"])
    late, token = late_weights(ya[0:8, 0:LANES] + yb[0:8, 0:LANES] + yc[0:8, 0:LANES])
    W = {**W, **late}
    h1 = _merge_fwd(h, ya, yb, yc, _behind(S["mix_out_g"], token), W["w_out"])
    h2, r = _mlp_fwd(h1, S["mlp_norm_g"], W["w_ff1"], W["w_ff2"])
    h3 = _ple_fwd(h2, p, l, S["ple_norm_g"], W["w_ple_gate"], W["w_ple_proj"])
    return h3, dict(h=h, z=z, ya=ya, yb=yb, yc=yc, xr=xr, xi=xi, h1=h1, r=r, h2=h2), W


def _layer_bwd(g3, p, l, cs, sn, W, S, sp, A, raw, mid_bwd):
    G = {}
    g2, hn3, dgp, de, G["ple_norm_g"] = _ple_bwd(g3, A["h2"], p, l, S["ple_norm_g"], W["w_ple_gate"], W["w_ple_proj"])
    G["w_ple_gate"] = _tn(hn3, dgp, D_MODEL, D_MODEL, bm=1024, bn=1024, dtype=WIRE, name="tn_gate").reshape(N_DEV, -1, D_MODEL)
    G["w_ple_proj"] = _tn(p, de, PLE_DIM, D_MODEL, bm=PLE_DIM, bn=D_MODEL, a_lead=l, split=D_MODEL // N_DEV, dtype=WIRE, name="tn_proj")
    g1, hn, da, G["mlp_norm_g"] = _mlp_bwd(g2, A["h1"], S["mlp_norm_g"], A["r"], W["w_ff1"], W["w_ff2"])
    G["w_ff1"] = _tn(hn, da, D_MODEL, D_FF, bm=D_MODEL, bn=FF_BLOCK, n_major=True, dtype=WIRE, name="tn_ff1")
    G["w_ff2"] = _tn(A["r"], g2, D_FF, D_MODEL, bm=1024, bn=1024, dtype=WIRE, name="tn_ff2").reshape(N_DEV, -1, D_MODEL)
    token = mid_bwd(g1, G)
    dya, dyb, dyc, yn, G["mix_out_g"] = _merge_bwd(g1, A["ya"], A["yb"], A["yc"], _behind(S["mix_out_g"], token), W["w_out"])
    G["w_out"] = _tn(yn, g1, D_MODEL, D_MODEL, bm=1024, bn=1024, dtype=WIRE, name="tn_out").reshape(N_DEV, -1, D_MODEL)
    dzc, dw1, dw2, dcr, dci, dbbd, dlr, dli, dd = _ssm_bwd(dyc, A["z"], A["xr"], A["xi"], sp["bbd"], sp["pwr"], sp["pwi"], sp["cbd"],
                                                          S["ssm_d"], W["glu_w1"], W["glu_w2"])
    G["glu_w1"] = dw1.astype(WIRE).reshape(N_DEV, -1, C_WIDTH)
    G["glu_w2"] = dw2.astype(WIRE).reshape(N_DEV, -1, C_WIDTH)
    dare, daim, dldt, dbtr, dbti, dcre, dcim = _ssm_param_bwd(
        raw["are"], raw["aim"], raw["ldt"], raw["are_x"], raw["aim_x"], raw["ldt_x"], raw["btr"], raw["bti"],
        dlr.reshape(C_GROUPS, C_STATE), dli.reshape(C_GROUPS, C_STATE), dbbd, dcr, dci)
    G["ssm_a_re"], G["ssm_a_im"], G["ssm_log_dt"] = dare, daim, dldt[:, 0]
    G["ssm_b_re"] = dbtr.reshape(C_GROUPS, C_GROUP, C_STATE).transpose(0, 2, 1)
    G["ssm_b_im"] = dbti.reshape(C_GROUPS, C_GROUP, C_STATE).transpose(0, 2, 1)
    G["ssm_c_re"] = dcre.reshape(C_GROUPS, C_GROUP, C_STATE)
    G["ssm_c_im"] = dcim.reshape(C_GROUPS, C_GROUP, C_STATE)
    G["ssm_d"] = dd.reshape(C_GROUPS, C_GROUP)
    dzq, dzk, dzv, dqg, dkg, dsk = _attn_bwd(A["z"], cs, sn, S["qg"], S["kg"], S["sinks"], A["yb"], dyb)
    G["q_norm_g"] = dqg[0, :HEAD_DIM] + dqg[0, HEAD_DIM:]
    G["k_norm_g"] = dkg[0, :HEAD_DIM] + dkg[0, HEAD_DIM:]
    G["sinks"] = dsk[:, 0]
    dza, dws, dbs, dlng, dlnb = _gmlp_bwd(A["z"], dya, S["lng"], S["lnb"], S["gmlp_ws"], S["bsx"])
    G["gmlp_ws"] = dws
    G["gmlp_bs"] = dbs[:, :, 0]
    G["gmlp_ln_g"] = dlng.reshape(A_HEADS, 2, HEAD_DIM)[:, 1]
    G["gmlp_ln_b"] = dlnb.reshape(A_HEADS, 2, HEAD_DIM)[:, 1]
    g0, xn, dz, G["attn_norm_g"] = _inproj_bwd(g1, A["h"], S["attn_norm_g"], W["w_in"], dza, dzq, dzk, dzv, dzc)
    G["w_in"] = _tn(xn, dz, D_MODEL, IN_COLS, bm=D_MODEL, bn=IN_COLS, split=IN_COLS // N_DEV, dtype=WIRE, name="tn_in")
    return g0, G


def _small_layouts(P, l):
    def row(a):
        return a.reshape(1, -1)

    zeros = jnp.zeros((A_HEADS, HEAD_DIM), F32)
    S = dict(
        attn_norm_g=row(P["attn_norm_g"][l]), mix_out_g=row(P["mix_out_g"][l]), mlp_norm_g=row(P["mlp_norm_g"][l]),
        ple_norm_g=row(P["ple_norm_g"][l]),
        lng=jnp.stack([zeros, P["gmlp_ln_g"][l]], axis=1).reshape(1, 512),
        lnb=jnp.stack([zeros, P["gmlp_ln_b"][l]], axis=1).reshape(1, 512),
        gmlp_ws=P["gmlp_ws"][l],
        bsx=jnp.broadcast_to(P["gmlp_bs"][l][:, :, None], (A_HEADS, CHUNK, CHUNK)),
        qg=jnp.tile(P["q_norm_g"][l], 2).reshape(1, LANES), kg=jnp.tile(P["k_norm_g"][l], 2).reshape(1, LANES),
        sinks=jnp.broadcast_to(P["sinks"][l][:, None], (8, LANES)),
        ssm_d=row(P["ssm_d"][l]),
    )
    are, aim = P["ssm_a_re"][l], P["ssm_a_im"][l]
    ldt = jnp.broadcast_to(P["ssm_log_dt"][l][:, None], (C_GROUPS, C_STATE))
    raw = dict(
        are=are, aim=aim, ldt=ldt,
        are_x=jnp.repeat(are, C_GROUP, axis=0), aim_x=jnp.repeat(aim, C_GROUP, axis=0), ldt_x=jnp.repeat(ldt, C_GROUP, axis=0),
        btr=P["ssm_b_re"][l].transpose(0, 2, 1).reshape(C_WIDTH, C_STATE), bti=P["ssm_b_im"][l].transpose(0, 2, 1).reshape(C_WIDTH, C_STATE),
        cre=P["ssm_c_re"][l].reshape(C_WIDTH, C_STATE), cim=P["ssm_c_im"][l].reshape(C_WIDTH, C_STATE),
    )
    return S, raw


def _ssm_prep_layer(raw):
    bbd, cbd, pwr, pwi = _ssm_prep(raw["are"].reshape(1, N_STATE), raw["aim"].reshape(1, N_STATE), raw["ldt"].reshape(1, N_STATE),
                                   raw["are_x"], raw["aim_x"], raw["ldt_x"], raw["btr"], raw["bti"], raw["cre"], raw["cim"])
    return dict(bbd=bbd, cbd=cbd, pwr=pwr, pwi=pwi)


def _behind(row, token):
    return row if token is None else row + token[0:1, 0:1]


def _local_step(x, p, positions, target, P, weights_of, mid_bwd, after_bwd):
    inv = 1.0 / (ROPE_THETA ** (jnp.arange(0, HEAD_DIM, 2, dtype=F32) / HEAD_DIM))
    cs, sn = _rope_tables(positions.reshape(-1, 1), jnp.tile(inv, 4).reshape(1, LANES))
    h = x
    acts, smalls, weights = [], [], []
    for l in range(DEPTH):
        W, late_weights, token = weights_of(l, h)
        S, raw = _small_layouts(P, l)
        sp = _ssm_prep_layer(raw)
        h, A, W = _layer_fwd(h, p, l, cs, sn, W, late_weights, {**S, "attn_norm_g": _behind(S["attn_norm_g"], token)}, sp)
        acts.append(A)
        smalls.append((S, raw, sp))
        weights.append(W)
    g, lsum = _loss_head(h, target)
    grads = [None] * DEPTH
    token = None
    for l in reversed(range(DEPTH)):
        S, raw, sp = smalls[l]
        g, grads[l] = _layer_bwd(g, p, l, cs, sn, weights[l], {**S, "ple_norm_g": _behind(S["ple_norm_g"], token)}, sp, acts[l], raw,
                                 functools.partial(mid_bwd, l))
        token = after_bwd(l, g, grads[l])
    return lsum[0, 0], g, grads


def _layer_weights(g):
    layout = dict(
        w_in=_from_col_major, glu_w1=lambda a: a.reshape(C_WIDTH, C_WIDTH), glu_w2=lambda a: a.reshape(C_WIDTH, C_WIDTH),
        w_out=lambda a: a.reshape(D_MODEL, D_MODEL), w_ff1=lambda a: a, w_ff2=lambda a: a.reshape(D_FF, D_MODEL),
        w_ple_gate=lambda a: a.reshape(D_MODEL, D_MODEL), w_ple_proj=_from_col_major)
    return {n: layout[n](a) for n, a in g.items()}


def kernel(x, p, positions, attn_norm_g, w_in, gmlp_ln_g, gmlp_ln_b, gmlp_ws, gmlp_bs, q_norm_g, k_norm_g, sinks, ssm_a_re, ssm_a_im, ssm_log_dt, ssm_b_re, ssm_b_im, ssm_c_re, ssm_c_im, ssm_d, glu_w1, glu_w2, mix_out_g, w_out, mlp_norm_g, w_ff1, w_ff2, ple_norm_g, w_ple_gate, w_ple_proj, loss_target, m_attn_norm_g, m_w_in, m_gmlp_ln_g, m_gmlp_ln_b, m_gmlp_ws, m_gmlp_bs, m_q_norm_g, m_k_norm_g, m_sinks, m_ssm_a_re, m_ssm_a_im, m_ssm_log_dt, m_ssm_b_re, m_ssm_b_im, m_ssm_c_re, m_ssm_c_im, m_ssm_d, m_glu_w1, m_glu_w2, m_mix_out_g, m_w_out, m_mlp_norm_g, m_w_ff1, m_w_ff2, m_ple_norm_g, m_w_ple_gate, m_w_ple_proj, v_attn_norm_g, v_w_in, v_gmlp_ln_g, v_gmlp_ln_b, v_gmlp_ws, v_gmlp_bs, v_q_norm_g, v_k_norm_g, v_sinks, v_ssm_a_re, v_ssm_a_im, v_ssm_log_dt, v_ssm_b_re, v_ssm_b_im, v_ssm_c_re, v_ssm_c_im, v_ssm_d, v_glu_w1, v_glu_w2, v_mix_out_g, v_w_out, v_mlp_norm_g, v_w_ff1, v_w_ff2, v_ple_norm_g, v_w_ple_gate, v_w_ple_proj):
    env = dict(locals())
    P = {n: env[n] for n in WEIGHTS}
    M = {n: env["m_" + n] for n in WEIGHTS}
    V = {n: env["v_" + n] for n in WEIGHTS}
    return _step(x, p, positions, loss_target, P, M, V)


def _step(x, p, positions, loss_target, P, M, V):
    small_shapes = [P[n].shape for n in SMALL]
    me = 4 * lax.axis_index("x") + 2 * lax.axis_index("y") + lax.axis_index("c")
    nothing = jnp.zeros((8, LANES), F32)

    def put(land, own, lead):
        return lax.dynamic_update_slice(land, own.reshape((1,) * len(lead) + own.shape), tuple(lead) + (0,) * own.ndim)

    def gather_start(l, names, after, tag):
        shards = [P[n][l].astype(WIRE) for n in names]
        lands = [lax.empty((N_DEV,) + s.shape, WIRE) for s in shards]
        send, recv, shards, lands, token = _split_start(shards, lands, ["all"] * len(names), 0, after, f"gather_start_{l}{tag}")
        return dict(names=names, send=send, recv=recv, shards=shards, lands=lands, token=token, name=f"gather_wait_{l}{tag}")

    def gather_wait(f, after):
        shards, lands = _split_wait(f["send"], f["recv"], f["shards"], f["lands"], ["all"] * len(f["names"]), 0, after, f["name"])
        return dict(zip(f["names"], [put(ld, sh, (me,)) for sh, ld in zip(shards, lands)]))

    first = gather_start(0, EARLY, nothing, "a")
    flying = {0: (first, gather_start(0, LATE, first["token"], "b"))}

    def weights_of(l, h):
        fa, fb = flying.pop(l)
        got = gather_wait(fa, h)
        if fb is None:
            token = None
            if l + 1 < DEPTH:
                flying[l + 1] = (gather_start(l + 1, SHARDED, got["w_in"], ""), None)
                token = flying[l + 1][0]["token"]
            W = _layer_weights(got)
            return {n: W[n] for n in EARLY}, (lambda after: ({n: W[n] for n in LATE}, None)), token

        def late_weights(after):
            late = gather_wait(fb, after)
            flying[l + 1] = (gather_start(l + 1, SHARDED, late["w_out"], ""), None)
            return _layer_weights(late), flying[l + 1][0]["token"]

        return _layer_weights(got), late_weights, fb["token"]

    grad_lands = {n: lax.empty((N_DEV,) + P[n].shape, WIRE) for n in SHARDED}
    sent = []

    def scatter_start(l, names, parts, lands, tag):
        send, recv, parts, lands, token = _split_start(parts, lands, ["own"] * len(parts), l, nothing, f"scatter_start_{l}{tag}")
        sent.append(dict(l=l, names=names, send=send, recv=recv, parts=parts, lands=lands, name=f"scatter_wait_{l}{tag}"))
        return token

    def scatter_wait(after):
        f = sent.pop(0)
        parts, lands = _split_wait(f["send"], f["recv"], f["parts"], f["lands"], ["own"] * len(f["parts"]), f["l"], after, f["name"])
        lands = [put(ld, lax.dynamic_index_in_dim(part, me, 0, keepdims=False), (me, f["l"])) for part, ld in zip(parts, lands)]
        return dict(zip(f["names"], lands))

    def mid_bwd(l, g1, G):
        if l > 0:
            return None
        grad_lands.update(scatter_wait(g1))
        return scatter_start(0, GRADS_MID, [G[n] for n in GRADS_MID], [grad_lands[n] for n in GRADS_MID], "a")

    small_land = []

    def after_bwd(l, g, G):
        if l > 0:
            if sent:
                grad_lands.update(scatter_wait(g))
            return scatter_start(l, SHARDED, [G[n] for n in SHARDED], [grad_lands[n] for n in SHARDED], "")
        sflat = _pack([jnp.stack([grads_of[k][n] for k in range(DEPTH)]) for n in SMALL], F32)
        sparts = sflat.reshape(N_DEV, -1, FLAT_COLS)
        small_land.append(sflat.shape)
        return scatter_start(0, GRADS_END + ("small",), [G[n] for n in GRADS_END] + [sparts],
                             [grad_lands[n] for n in GRADS_END] + [lax.empty((N_DEV, 1) + sparts.shape[1:], F32)], "b")

    grads_of = {}

    def after_bwd_recording(l, g, G):
        grads_of[l] = G
        grads_of["token"] = after_bwd(l, g, G)
        return grads_of["token"]

    lsum, gx, grads = _local_step(x[0], p[:, 0], positions[0], loss_target[0], P, weights_of, mid_bwd, after_bwd_recording)
    G, delta, new_m, new_v = {}, {}, {}, {}

    def update(names):
        for n in names:
            shp = P[n].shape
            res = _sum_adamw(grad_lands[n].reshape(N_DEV, -1, shp[-1]), *(a.reshape(-1, shp[-1]) for a in (P[n], M[n], V[n])))
            G[n], delta[n], new_m[n], new_v[n] = (a.reshape(shp) for a in res)

    grad_lands.update(scatter_wait(grads_of["token"]))
    update(GRADS_MID)
    last = scatter_wait(sum(new_v[n].reshape(-1, LANES)[:8] for n in GRADS_MID))
    small_parts = last.pop("small")
    grad_lands.update(last)
    update(GRADS_END)
    small_sum = _gather_whole(_sum_slots(small_parts[:, 0])).reshape((1,) + small_land[0])
    res = _sum_adamw(small_sum, _pack([P[n] for n in SMALL], F32), _pack([M[n] for n in SMALL], F32), _pack([V[n] for n in SMALL], F32))
    for dst, flat in zip((G, delta, new_m, new_v), res):
        dst.update(zip(SMALL, _unpack(flat, small_shapes)))
    loss = lax.psum(lsum, ("x", "y", "c"))
    return (loss, gx[None], *[G[n] for n in WEIGHTS], *[delta[n] for n in WEIGHTS], *[new_m[n] for n in WEIGHTS], *[new_v[n] for n in WEIGHTS])
```

```python
import functools
import math

import jax
import jax.numpy as jnp
from jax import lax
from jax.experimental import pallas as pl
from jax.experimental.pallas import tpu as pltpu

F32 = jnp.float32
MXU = jnp.bfloat16
WIRE = jnp.bfloat16

D_MODEL = 1024
DEPTH = 4
HEAD_DIM = 64
A_WIDTH = 256
A_HEADS = 4
CHUNK = 128
B_WIDTH = 512
WINDOW = 128
C_WIDTH = 256
C_GROUP = 16
C_GROUPS = 16
C_STATE = 64
N_STATE = C_GROUPS * C_STATE
IN_COLS = 1536
D_FF = 4096
PLE_DIM = 256
EPS = 1e-6
ROPE_THETA = 10000.0
SCALE = HEAD_DIM ** -0.5
NEG = -1e30
N_DEV = 8

ADAM_LR = 0.001
ADAM_B1 = 0.9
ADAM_B2 = 0.999
ADAM_EPS = 1e-08
ADAM_WD = 0.01
ADAM_STEP = 10

V7X_VMEM_BYTES = 64 * 2 ** 20
VMEM_LIMIT = V7X_VMEM_BYTES - 8 * 2 ** 20
LANES = 128

MESH = pl.DeviceIdType.MESH


def _cp(*sem):
    return pltpu.CompilerParams(dimension_semantics=sem, vmem_limit_bytes=VMEM_LIMIT)


def _sds(shape, dtype=F32):
    return jax.ShapeDtypeStruct(shape, dtype)


def _mm(a, b):
    return jnp.dot(a.astype(MXU), b.astype(MXU), preferred_element_type=F32)


def _mm_nt(a, b):
    return lax.dot_general(a.astype(MXU), b.astype(MXU), (((1,), (1,)), ((), ())), preferred_element_type=F32)


def _mm_tn(a, b):
    return lax.dot_general(a.astype(MXU), b.astype(MXU), (((0,), (0,)), ((), ())), preferred_element_type=F32)


def _lane(shape):
    return lax.broadcasted_iota(jnp.int32, shape, len(shape) - 1)


def _row(shape):
    return lax.broadcasted_iota(jnp.int32, shape, 0)


_GELU_C = math.sqrt(2.0 / math.pi)


def _gelu(x):
    return 0.5 * x * (1.0 + jnp.tanh(_GELU_C * (x + 0.044715 * (x * x * x))))


def _gelu_grad(x):
    t = jnp.tanh(_GELU_C * (x + 0.044715 * (x * x * x)))
    return 0.5 * (1.0 + t) + 0.5 * x * (1.0 - t * t) * (_GELU_C * (1.0 + 3.0 * 0.044715 * (x * x)))


def _sigmoid(x):
    return 1.0 / (1.0 + jnp.exp(-x))


def _rms_stat(x):
    return lax.rsqrt(jnp.mean(x * x, axis=-1, keepdims=True) + EPS)


def _rms_bwd(x, r, g, dy):
    xh = x * r
    dxh = dy * g
    dx = r * (dxh - xh * jnp.mean(dxh * xh, axis=-1, keepdims=True))
    return dx, jnp.sum(dy * xh, axis=0, keepdims=True)


def _tril(w):
    return jnp.where(_row(w.shape) >= _lane(w.shape), w, 0.0)


def _swap64(x):
    return pltpu.roll(x, HEAD_DIM, 1)


def _group_sum64(x, lo):
    s_lo = jnp.sum(jnp.where(lo, x, 0.0), axis=-1, keepdims=True)
    s_hi = jnp.sum(jnp.where(lo, 0.0, x), axis=-1, keepdims=True)
    return jnp.where(lo, s_lo, s_hi)


def _partner(x):
    n = x.shape[-1]
    first = (_lane(x.shape) % HEAD_DIM) < HEAD_DIM // 2
    return jnp.where(first, pltpu.roll(x, n - HEAD_DIM // 2, 1), pltpu.roll(x, HEAD_DIM // 2, 1))


def _rope(y, cs, sn):
    return y * cs + _partner(y) * sn


def _rope_bwd(d, cs, sn):
    return d * cs + _partner(d * sn)


def _qk_norm_rope(x, g, cs, sn):
    lo = _lane(x.shape) < HEAD_DIM
    r = lax.rsqrt(_group_sum64(x * x, lo) * (1.0 / HEAD_DIM) + EPS)
    xh = x * r
    return _rope(xh * g, cs, sn), xh, r


def _qk_norm_rope_bwd(xh, r, g, cs, sn, d):
    lo = _lane(xh.shape) < HEAD_DIM
    dy = _rope_bwd(d, cs, sn)
    dxh = dy * g
    m = _group_sum64(dxh * xh, lo) * (1.0 / HEAD_DIM)
    return r * (dxh - xh * m), jnp.sum(dy * xh, axis=0, keepdims=True)


def _gmlp_head(blk, g, b):
    hi = _lane(blk.shape) >= HEAD_DIM
    mu = jnp.sum(jnp.where(hi, blk, 0.0), axis=-1, keepdims=True) * (1.0 / HEAD_DIM)
    xc = jnp.where(hi, blk - mu, 0.0)
    rstd = lax.rsqrt(jnp.sum(xc * xc, axis=-1, keepdims=True) * (1.0 / HEAD_DIM) + EPS)
    vhat = xc * rstd
    return vhat * g + b, vhat, rstd


def _rope_tables(pos_col, inv_row):
    T = pos_col.shape[0]
    tm = min(T, 1024)

    def body(p_ref, inv_ref, cs_ref, sn_ref):
        ang = p_ref[...].astype(F32) * inv_ref[...]
        s = jnp.sin(ang)
        cs_ref[...] = jnp.cos(ang)
        sn_ref[...] = jnp.where((_lane(ang.shape) % HEAD_DIM) < HEAD_DIM // 2, -s, s)

    blk = pl.BlockSpec((tm, LANES), lambda i: (i, 0))
    return pl.pallas_call(
        body, name="rope_tables", grid=(T // tm,),
        in_specs=[pl.BlockSpec((tm, 1), lambda i: (i, 0)), pl.BlockSpec((1, LANES), lambda i: (0, 0))],
        out_specs=[blk, blk], out_shape=[_sds((T, LANES))] * 2, compiler_params=_cp("parallel"))(pos_col, inv_row)


def _inproj_fwd(h, g, w):
    T = h.shape[0]
    tm = min(T, 512)

    def body(h_ref, g_ref, w_ref, z_ref):
        x = h_ref[...]
        z_ref[...] = _mm(x * _rms_stat(x) * g_ref[...], w_ref[...])

    return pl.pallas_call(
        body, name="inproj_fwd", grid=(T // tm,),
        in_specs=[pl.BlockSpec((tm, D_MODEL), lambda i: (i, 0)), pl.BlockSpec((1, D_MODEL), lambda i: (0, 0)),
                  pl.BlockSpec((D_MODEL, IN_COLS), lambda i: (0, 0))],
        out_specs=pl.BlockSpec((tm, IN_COLS), lambda i: (i, 0)), out_shape=_sds((T, IN_COLS)),
        compiler_params=_cp("parallel"))(h, g, w)


def _inproj_bwd(gres, h, g, w, dza, dzq, dzk, dzv, dzc):
    T = h.shape[0]
    tm = min(T, 512)
    nt = T // tm
    shard = IN_COLS // N_DEV

    def body(gr_ref, h_ref, g_ref, w_ref, a_ref, q_ref, k_ref, v_ref, c_ref, dh_ref, dw_ref, dg_ref, acc):
        i = pl.program_id(0)

        @pl.when(i == 0)
        def _():
            dg_ref[...] = jnp.zeros_like(dg_ref)
            acc[...] = jnp.zeros_like(acc)

        x = h_ref[...]
        r = _rms_stat(x)
        gg = g_ref[...]
        dz = jnp.concatenate([a_ref[...], q_ref[...], k_ref[...], v_ref[...], c_ref[...]], axis=1)
        dxn = _mm_nt(dz, w_ref[...])
        dx, dg = _rms_bwd(x, r, gg, dxn)
        dh_ref[...] = gr_ref[...] + dx
        dg_ref[...] += dg
        acc[...] += _mm_tn(x * r * gg, dz)

        @pl.when(i == nt - 1)
        def _():
            for d in range(N_DEV):
                dw_ref[d] = acc[:, d * shard:(d + 1) * shard].astype(WIRE)

    def rows(w_):
        return pl.BlockSpec((tm, w_), lambda i: (i, 0))

    row = pl.BlockSpec((1, D_MODEL), lambda i: (0, 0))
    return pl.pallas_call(
        body, name="inproj_bwd", grid=(nt,),
        in_specs=[rows(D_MODEL), rows(D_MODEL), row, pl.BlockSpec((D_MODEL, IN_COLS), lambda i: (0, 0)),
                  rows(512), rows(512), rows(128), rows(128), rows(256)],
        out_specs=[rows(D_MODEL), pl.BlockSpec((N_DEV, D_MODEL, shard), lambda i: (0, 0, 0)), row],
        out_shape=[_sds((T, D_MODEL)), _sds((N_DEV, D_MODEL, shard), WIRE), _sds((1, D_MODEL))],
        scratch_shapes=[pltpu.VMEM((D_MODEL, IN_COLS), F32)],
        compiler_params=_cp("arbitrary"))(gres, h, g, w, dza, dzq, dzk, dzv, dzc)


def _gmlp_fwd(z, lng, lnb, ws, bsx):
    T = z.shape[0]
    tm = min(T, 512)
    nc = tm // CHUNK

    def body(z_ref, g_ref, b_ref, w_ref, bs_ref, ya_ref):
        zg = _gelu(z_ref[...])
        lo = _lane((tm, LANES)) < HEAD_DIM
        prods = []
        for hd in range(A_HEADS):
            sl = slice(hd * LANES, (hd + 1) * LANES)
            blk = zg[:, sl]
            vn, _, _ = _gmlp_head(blk, g_ref[:, sl], b_ref[:, sl])
            wm = _tril(w_ref[hd])
            sv = jnp.concatenate([_mm(wm, vn[c * CHUNK:(c + 1) * CHUNK]) + bs_ref[hd] for c in range(nc)], axis=0)
            prods.append(blk * _swap64(sv))
        ya_ref[:, 0:LANES] = jnp.where(lo, prods[0], _swap64(prods[1]))
        ya_ref[:, LANES:2 * LANES] = jnp.where(lo, prods[2], _swap64(prods[3]))

    row = pl.BlockSpec((1, 512), lambda i: (0, 0))
    mat = pl.BlockSpec((A_HEADS, CHUNK, CHUNK), lambda i: (0, 0, 0))
    return pl.pallas_call(
        body, name="gmlp_fwd", grid=(T // tm,),
        in_specs=[pl.BlockSpec((tm, 512), lambda i: (i, 0)), row, row, mat, mat],
        out_specs=pl.BlockSpec((tm, A_WIDTH), lambda i: (i, 0)), out_shape=_sds((T, A_WIDTH)),
        compiler_params=_cp("parallel"))(z, lng, lnb, ws, bsx)


def _gmlp_bwd(z, dya, lng, lnb, ws, bsx):
    T = z.shape[0]
    tm = min(T, 512)
    nc = tm // CHUNK

    def body(z_ref, dya_ref, g_ref, b_ref, w_ref, bs_ref, dza_ref, dw_ref, dbs_ref, dg_ref, db_ref):
        @pl.when(pl.program_id(0) == 0)
        def _():
            dw_ref[...] = jnp.zeros_like(dw_ref)
            dbs_ref[...] = jnp.zeros_like(dbs_ref)
            dg_ref[...] = jnp.zeros_like(dg_ref)
            db_ref[...] = jnp.zeros_like(db_ref)

        za = z_ref[...]
        zg = _gelu(za)
        gp = _gelu_grad(za)
        lo = _lane((tm, LANES)) < HEAD_DIM
        for hd in range(A_HEADS):
            sl = slice(hd * LANES, (hd + 1) * LANES)
            blk = zg[:, sl]
            g = g_ref[:, sl]
            vn, vhat, rstd = _gmlp_head(blk, g, b_ref[:, sl])
            wm = _tril(w_ref[hd])
            pair = dya_ref[:, (hd // 2) * LANES:(hd // 2 + 1) * LANES]
            dy = jnp.where(lo, pair if hd % 2 == 0 else _swap64(pair), 0.0)
            dsv = _swap64(dy * blk)
            svs, dvns = [], []
            dw = jnp.zeros((CHUNK, CHUNK), F32)
            dbs = jnp.zeros((CHUNK, 1), F32)
            for c in range(nc):
                cs = slice(c * CHUNK, (c + 1) * CHUNK)
                svs.append(_mm(wm, vn[cs]) + bs_ref[hd])
                dw = dw + _mm_nt(dsv[cs], vn[cs])
                dbs = dbs + jnp.sum(dsv[cs], axis=-1, keepdims=True)
                dvns.append(_mm_tn(wm, dsv[cs]))
            sv = jnp.concatenate(svs, axis=0)
            dvn = jnp.concatenate(dvns, axis=0)
            dw_ref[hd] += _tril(dw)
            dbs_ref[hd] += jnp.broadcast_to(dbs, (CHUNK, CHUNK))
            dg_ref[:, sl] += jnp.sum(dvn * vhat, axis=0, keepdims=True)
            db_ref[:, sl] += jnp.sum(dvn, axis=0, keepdims=True)
            du = dy * _swap64(sv)
            dvh = dvn * g
            m1 = jnp.sum(dvh, axis=-1, keepdims=True) * (1.0 / HEAD_DIM)
            m2 = jnp.sum(dvh * vhat, axis=-1, keepdims=True) * (1.0 / HEAD_DIM)
            dv = jnp.where(lo, 0.0, rstd * (dvh - m1 - vhat * m2))
            dza_ref[:, sl] = (du + dv) * gp[:, sl]

    row = pl.BlockSpec((1, 512), lambda i: (0, 0))
    mat = pl.BlockSpec((A_HEADS, CHUNK, CHUNK), lambda i: (0, 0, 0))
    return pl.pallas_call(
        body, name="gmlp_bwd", grid=(T // tm,),
        in_specs=[pl.BlockSpec((tm, 512), lambda i: (i, 0)), pl.BlockSpec((tm, A_WIDTH), lambda i: (i, 0)), row, row, mat, mat],
        out_specs=[pl.BlockSpec((tm, 512), lambda i: (i, 0)), mat, mat, row, row],
        out_shape=[_sds((T, 512)), _sds((A_HEADS, CHUNK, CHUNK)), _sds((A_HEADS, CHUNK, CHUNK)), _sds((1, 512)), _sds((1, 512))],
        compiler_params=_cp("arbitrary"))(z, dya, lng, lnb, ws, bsx)


def _attn_specs(T, tq, tile_of):
    nb = tq // WINDOW

    def prev(i):
        return jnp.maximum(tile_of(i) * nb - 1, 0)

    row = pl.BlockSpec((1, LANES), lambda i: (0, 0))
    return [
        pl.BlockSpec((tq, B_WIDTH), lambda i: (tile_of(i), 1)),
        pl.BlockSpec((tq, LANES), lambda i: (tile_of(i), 8)),
        pl.BlockSpec((tq, LANES), lambda i: (tile_of(i), 9)),
        pl.BlockSpec((WINDOW, LANES), lambda i: (prev(i), 8)),
        pl.BlockSpec((WINDOW, LANES), lambda i: (prev(i), 9)),
        pl.BlockSpec((tq, LANES), lambda i: (tile_of(i), 0)),
        pl.BlockSpec((tq, LANES), lambda i: (tile_of(i), 0)),
        pl.BlockSpec((WINDOW, LANES), lambda i: (prev(i), 0)),
        pl.BlockSpec((WINDOW, LANES), lambda i: (prev(i), 0)),
        row, row,
        pl.BlockSpec((8, LANES), lambda i: (0, 0)),
    ]


def _attn_bias(first):
    qi = lax.broadcasted_iota(jnp.int32, (WINDOW, 2 * WINDOW), 0)
    kj = lax.broadcasted_iota(jnp.int32, (WINDOW, 2 * WINDOW), 1)
    diff = qi + WINDOW - kj
    ok = (diff >= 0) & (diff < WINDOW) & ((kj >= WINDOW) | jnp.logical_not(first))
    return jnp.where(ok, 0.0, NEG)


def _dup_heads(x, lo):
    sw = _swap64(x)
    return jnp.where(lo, x, sw), jnp.where(lo, sw, x)


HEADS_PER_KV = 4


def _stack_heads(x0, x1, lo):
    return jnp.concatenate([jnp.where(lo, x0, 0.0), jnp.where(lo, 0.0, x0), jnp.where(lo, x1, 0.0), jnp.where(lo, 0.0, x1)], axis=0)


def _unstack_heads(x4, lo):
    return (jnp.where(lo, x4[0:WINDOW], x4[WINDOW:2 * WINDOW]), jnp.where(lo, x4[2 * WINDOW:3 * WINDOW], x4[3 * WINDOW:4 * WINDOW]))


def _sink_column(sk_ref, g):
    return jnp.concatenate([jnp.broadcast_to(sk_ref[a:a + 1, 0:1], (WINDOW, 1)) for a in range(HEADS_PER_KV * g, HEADS_PER_KV * (g + 1))], axis=0)


def _attn_probs(q4, kw, bias, sink):
    s = _mm_nt(q4, kw)
    s = (s.reshape(HEADS_PER_KV, WINDOW, 2 * WINDOW) + bias[None]).reshape(HEADS_PER_KV * WINDOW, 2 * WINDOW)
    m = jnp.maximum(jnp.max(s, axis=-1, keepdims=True), sink)
    p = jnp.exp(s - m)
    es = jnp.exp(sink - m)
    inv = 1.0 / (jnp.sum(p, axis=-1, keepdims=True) + es)
    return p * inv, es * inv


def _attn_fwd(z, cs, sn, qg, kg, sinks):
    T = z.shape[0]
    tq = min(T, 512)
    nb = tq // WINDOW

    def body(q_ref, k_ref, v_ref, kp_ref, vp_ref, cs_ref, sn_ref, csp_ref, snp_ref, qg_ref, kg_ref, sk_ref, o_ref):
        i = pl.program_id(0)
        csq, snq = cs_ref[...], sn_ref[...]
        cs_all = jnp.concatenate([csp_ref[...], csq], axis=0)
        sn_all = jnp.concatenate([snp_ref[...], snq], axis=0)
        k_all = jnp.concatenate([kp_ref[...], k_ref[...]], axis=0)
        v_all = jnp.concatenate([vp_ref[...], v_ref[...]], axis=0)
        kr, _, _ = _qk_norm_rope(k_all, kg_ref[...], cs_all, sn_all)
        lo_all = _lane(k_all.shape) < HEAD_DIM
        kd = _dup_heads(kr, lo_all)
        vd = _dup_heads(v_all, lo_all)
        lo = _lane((WINDOW, LANES)) < HEAD_DIM
        qrs = [_qk_norm_rope(q_ref[:, pr * LANES:(pr + 1) * LANES], qg_ref[...], csq, snq)[0] * SCALE for pr in range(4)]
        biases = [_attn_bias(i * nb + b == 0) for b in range(nb)]
        for g in range(2):
            sink = _sink_column(sk_ref, g)
            for b in range(nb):
                bs = slice(b * WINDOW, (b + 1) * WINDOW)
                ws = slice(b * WINDOW, (b + 2) * WINDOW)
                pn, _ = _attn_probs(_stack_heads(qrs[2 * g][bs], qrs[2 * g + 1][bs], lo), kd[g][ws], biases[b], sink)
                o0, o1 = _unstack_heads(_mm(pn, vd[g][ws]), lo)
                o_ref[bs, 2 * g * LANES:(2 * g + 1) * LANES] = o0
                o_ref[bs, (2 * g + 1) * LANES:(2 * g + 2) * LANES] = o1

    return pl.pallas_call(
        body, name="attn_fwd", grid=(T // tq,),
        in_specs=_attn_specs(T, tq, lambda i: i),
        out_specs=pl.BlockSpec((tq, B_WIDTH), lambda i: (i, 0)), out_shape=_sds((T, B_WIDTH)),
        compiler_params=_cp("parallel"))(z, z, z, z, z, cs, sn, cs, sn, qg, kg, sinks)


def _attn_bwd(z, cs, sn, qg, kg, sinks, o, do):
    T = z.shape[0]
    tq = min(T, 512)
    nb = tq // WINDOW
    nt = T // tq
    tk = tq + WINDOW

    def tile_of(i):
        return nt - 1 - i

    def body(q_ref, k_ref, v_ref, kp_ref, vp_ref, cs_ref, sn_ref, csp_ref, snp_ref, qg_ref, kg_ref, sk_ref, o_ref, do_ref,
             dq_ref, dk_ref, dv_ref, dqg_ref, dkg_ref, dsk_ref, acck, accv, ck, cv):
        i = pl.program_id(0)
        ti = nt - 1 - i

        @pl.when(i == 0)
        def _():
            dqg_ref[...] = jnp.zeros_like(dqg_ref)
            dkg_ref[...] = jnp.zeros_like(dkg_ref)
            dsk_ref[...] = jnp.zeros_like(dsk_ref)
            ck[...] = jnp.zeros_like(ck)
            cv[...] = jnp.zeros_like(cv)

        csq, snq = cs_ref[...], sn_ref[...]
        cs_all = jnp.concatenate([csp_ref[...], csq], axis=0)
        sn_all = jnp.concatenate([snp_ref[...], snq], axis=0)
        k_all = jnp.concatenate([kp_ref[...], k_ref[...]], axis=0)
        v_all = jnp.concatenate([vp_ref[...], v_ref[...]], axis=0)
        kr, kh, rk = _qk_norm_rope(k_all, kg_ref[...], cs_all, sn_all)
        lo_all = _lane(k_all.shape) < HEAD_DIM
        kd = _dup_heads(kr, lo_all)
        vd = _dup_heads(v_all, lo_all)
        lo = _lane((WINDOW, LANES)) < HEAD_DIM
        acck[...] = jnp.zeros_like(acck)
        accv[...] = jnp.zeros_like(accv)
        prep = [_qk_norm_rope(q_ref[:, pr * LANES:(pr + 1) * LANES], qg_ref[...], csq, snq) for pr in range(4)]
        biases = [_attn_bias(ti * nb + b == 0) for b in range(nb)]
        dqs = [[None] * nb for _ in range(4)]
        for g in range(2):
            sink = _sink_column(sk_ref, g)
            dsink = jnp.zeros((HEADS_PER_KV * WINDOW, 1), F32)
            for b in range(nb):
                bs = slice(b * WINDOW, (b + 1) * WINDOW)
                ws = slice(b * WINDOW, (b + 2) * WINDOW)
                kw, vw = kd[g][ws], vd[g][ws]
                q4 = _stack_heads(prep[2 * g][0][bs] * SCALE, prep[2 * g + 1][0][bs] * SCALE, lo)
                pn, psink = _attn_probs(q4, kw, biases[b], sink)
                o0, o1 = o_ref[bs, 2 * g * LANES:(2 * g + 1) * LANES], o_ref[bs, (2 * g + 1) * LANES:(2 * g + 2) * LANES]
                do4 = _stack_heads(do_ref[bs, 2 * g * LANES:(2 * g + 1) * LANES], do_ref[bs, (2 * g + 1) * LANES:(2 * g + 2) * LANES], lo)
                delta = jnp.sum(do4 * jnp.concatenate([o0, o0, o1, o1], axis=0), axis=-1, keepdims=True)
                ds = pn * (_mm_nt(do4, vw) - delta)
                dsink = dsink - psink * delta
                dqs[2 * g][b], dqs[2 * g + 1][b] = _unstack_heads(_mm(ds, kw) * SCALE, lo)
                acck[g, ws, :] += _mm_tn(ds, q4)
                accv[g, ws, :] += _mm_tn(pn, do4)
            for hh in range(HEADS_PER_KV):
                a = HEADS_PER_KV * g + hh
                dsk_ref[a:a + 1, :] += jnp.zeros((1, LANES), F32) + jnp.sum(dsink[hh * WINDOW:(hh + 1) * WINDOW])
        for pr in range(4):
            _, qh, rq = prep[pr]
            dx, dg = _qk_norm_rope_bwd(qh, rq, qg_ref[...], csq, snq, jnp.concatenate(dqs[pr], axis=0))
            dq_ref[:, pr * LANES:(pr + 1) * LANES] = dx
            dqg_ref[...] += dg

        def fold(acc):
            f0 = acc[0] + _swap64(acc[0])
            f1 = acc[1] + _swap64(acc[1])
            return jnp.where(lo_all, f0, f1)

        dk_all = fold(acck)
        dv_all = fold(accv)
        pad = jnp.zeros((tq - WINDOW, LANES), F32)
        dk_own = dk_all[WINDOW:] + (jnp.concatenate([pad, ck[...]], axis=0) if nb > 1 else ck[...])
        dv_own = dv_all[WINDOW:] + (jnp.concatenate([pad, cv[...]], axis=0) if nb > 1 else cv[...])
        ck[...] = dk_all[:WINDOW]
        cv[...] = dv_all[:WINDOW]
        dxk, dgk = _qk_norm_rope_bwd(kh[WINDOW:], rk[WINDOW:], kg_ref[...], csq, snq, dk_own)
        dk_ref[...] = dxk
        dkg_ref[...] += dgk
        dv_ref[...] = dv_own

    row = pl.BlockSpec((1, LANES), lambda i: (0, 0))
    return pl.pallas_call(
        body, name="attn_bwd", grid=(nt,),
        in_specs=_attn_specs(T, tq, tile_of) + [pl.BlockSpec((tq, B_WIDTH), lambda i: (tile_of(i), 0))] * 2,
        out_specs=[pl.BlockSpec((tq, B_WIDTH), lambda i: (tile_of(i), 0)), pl.BlockSpec((tq, LANES), lambda i: (tile_of(i), 0)),
                   pl.BlockSpec((tq, LANES), lambda i: (tile_of(i), 0)), row, row, pl.BlockSpec((8, LANES), lambda i: (0, 0))],
        out_shape=[_sds((T, B_WIDTH)), _sds((T, LANES)), _sds((T, LANES)), _sds((1, LANES)), _sds((1, LANES)), _sds((8, LANES))],
        scratch_shapes=[pltpu.VMEM((2, tk, LANES), F32), pltpu.VMEM((2, tk, LANES), F32),
                        pltpu.VMEM((WINDOW, LANES), F32), pltpu.VMEM((WINDOW, LANES), F32)],
        compiler_params=_cp("arbitrary"))(z, z, z, z, z, cs, sn, cs, sn, qg, kg, sinks, o, do)


def _bbar_t(are, aim, ldt, btr, bti):
    lbr, lbi = _lam_bar(are, aim, ldt)
    den = are * are + aim * aim
    nr = lbr - 1.0
    cr = (nr * are + lbi * aim) / den
    ci = (lbi * are - nr * aim) / den
    return cr * btr - ci * bti, cr * bti + ci * btr


def _lam_bar(are, aim, ldt):
    dt = jnp.exp(ldt)
    er = jnp.exp(are * dt)
    return er * jnp.cos(aim * dt), er * jnp.sin(aim * dt)


def _block_diag(x):
    t = jnp.concatenate([x] * C_GROUPS, axis=1)
    return jnp.where(_row(t.shape) // C_GROUP == _lane(t.shape) // C_STATE, t, 0.0)


def _block_diag_fold(m):
    rg = _row((C_WIDTH, C_STATE)) // C_GROUP
    acc = jnp.zeros((C_WIDTH, C_STATE), F32)
    for g in range(C_GROUPS):
        acc = acc + jnp.where(rg == g, m[:, g * C_STATE:(g + 1) * C_STATE], 0.0)
    return acc


def _ssm_prep(are, aim, ldt, are_x, aim_x, ldt_x, btr, bti, cre, cim):
    def body(are_r, aim_r, ldt_r, arex_r, aimx_r, ldtx_r, btr_r, bti_r, cre_r, cim_r, bbd_ref, cbd_ref, pwr_ref, pwi_ref):
        lr, li = _lam_bar(are_r[...], aim_r[...], ldt_r[...])
        cr, ci = lr, li
        for r in range(SCAN_SEG):
            pwr_ref[r:r + 1, :] = cr
            pwi_ref[r:r + 1, :] = ci
            cr, ci = cr * lr - ci * li, cr * li + ci * lr
        br, bi = _bbar_t(arex_r[...], aimx_r[...], ldtx_r[...], btr_r[...], bti_r[...])
        bbd_ref[...] = jnp.concatenate([_block_diag(br), _block_diag(bi)], axis=1).astype(MXU)
        cbd_ref[...] = jnp.concatenate([_block_diag(cre_r[...]), -_block_diag(cim_r[...])], axis=1).astype(MXU)

    return pl.pallas_call(
        body, name="ssm_prep",
        out_shape=[_sds((C_WIDTH, 2 * N_STATE), MXU), _sds((C_WIDTH, 2 * N_STATE), MXU), _sds((SCAN_SEG, N_STATE)), _sds((SCAN_SEG, N_STATE))],
        compiler_params=pltpu.CompilerParams(vmem_limit_bytes=VMEM_LIMIT))(are, aim, ldt, are_x, aim_x, ldt_x, btr, bti, cre, cim)


def _ssm_param_bwd(are, aim, ldt, are_x, aim_x, ldt_x, btr, bti, dlr, dli, dbbd, dcr, dci):
    def body(are_r, aim_r, ldt_r, arex_r, aimx_r, ldtx_r, btr_r, bti_r, dlr_r, dli_r, dbbd_r, dcr_r, dci_r,
             dare_ref, daim_ref, dldt_ref, dbtr_ref, dbti_ref, dcre_ref, dcim_ref):
        _, vjp_l = jax.vjp(_lam_bar, are_r[...], aim_r[...], ldt_r[...])
        da1, di1, dl1 = vjp_l((dlr_r[...], dli_r[...]))
        dbr = _block_diag_fold(dbbd_r[:, 0:N_STATE])
        dbi = _block_diag_fold(dbbd_r[:, N_STATE:2 * N_STATE])
        _, vjp_b = jax.vjp(_bbar_t, arex_r[...], aimx_r[...], ldtx_r[...], btr_r[...], bti_r[...])
        da2, di2, dl2, dbtr, dbti = vjp_b((dbr, dbi))

        def gsum(x):
            return x.reshape(C_GROUPS, C_GROUP, C_STATE).sum(axis=1)

        dare_ref[...] = da1 + gsum(da2)
        daim_ref[...] = di1 + gsum(di2)
        dldt_ref[...] = jnp.broadcast_to(jnp.sum(dl1 + gsum(dl2), axis=-1, keepdims=True), (C_GROUPS, LANES))
        dbtr_ref[...] = dbtr
        dbti_ref[...] = dbti
        dcre_ref[...] = _block_diag_fold(dcr_r[...])
        dcim_ref[...] = -_block_diag_fold(dci_r[...])

    g = _sds((C_GROUPS, C_STATE))
    x = _sds((C_WIDTH, C_STATE))
    return pl.pallas_call(
        body, name="ssm_param_bwd", out_shape=[g, g, _sds((C_GROUPS, LANES)), x, x, x, x],
        compiler_params=pltpu.CompilerParams(vmem_limit_bytes=VMEM_LIMIT))(are, aim, ldt, are_x, aim_x, ldt_x, btr, bti, dlr, dli, dbbd, dcr, dci)


SCAN_TILE = 512
SCAN_SEG = 8


def _scan_tables(pwr_ref, pwi_ref, conj, reverse):
    row = _row((SCAN_SEG, N_STATE))
    shifts = []
    for k in (1, 2, 4):
        keep = (row < SCAN_SEG - k) if reverse else (row >= k)
        ar = jnp.broadcast_to(pwr_ref[k - 1:k, :], (SCAN_SEG, N_STATE))
        ai = jnp.broadcast_to(pwi_ref[k - 1:k, :], (SCAN_SEG, N_STATE)) * conj
        shifts.append((SCAN_SEG - k if reverse else k, jnp.where(keep, ar, 0.0), jnp.where(keep, ai, 0.0)))
    if reverse:
        pr = jnp.concatenate([pwr_ref[SCAN_SEG - 1 - r:SCAN_SEG - r, :] for r in range(SCAN_SEG)], axis=0)
        pi = jnp.concatenate([pwi_ref[SCAN_SEG - 1 - r:SCAN_SEG - r, :] for r in range(SCAN_SEG)], axis=0) * conj
    else:
        pr, pi = pwr_ref[...], pwi_ref[...] * conj
    return shifts, (pr, pi)


def _tile_scan(xr_ref, xi_ref, pwr_ref, pwi_ref, sr, si, conj, reverse):
    shifts, (pr, pi) = _scan_tables(pwr_ref, pwi_ref, conj, reverse)
    groups = xr_ref.shape[0] // SCAN_SEG
    out_row = 0 if reverse else SCAN_SEG - 1

    def step(k, c):
        cr, ci = c
        g = groups - 1 - k if reverse else k
        rows = pl.ds(pl.multiple_of(g * SCAN_SEG, SCAN_SEG), SCAN_SEG)
        xr, xi = xr_ref[rows, :], xi_ref[rows, :]
        for amount, ar, ai in shifts:
            qr, qi = pltpu.roll(xr, amount, 0), pltpu.roll(xi, amount, 0)
            xr, xi = xr + ar * qr - ai * qi, xi + ar * qi + ai * qr
        xr, xi = xr + pr * cr - pi * ci, xi + pr * ci + pi * cr
        xr_ref[rows, :] = xr
        xi_ref[rows, :] = xi
        return xr[out_row:out_row + 1], xi[out_row:out_row + 1]

    cr, ci = lax.fori_loop(0, groups, step, (sr[...], si[...]), unroll=2)
    sr[...] = cr
    si[...] = ci


def _ssm_fwd(z, bbd, pwr, pwi, cbd, dsk, w1, w2):
    T = z.shape[0]
    tt = min(T, 2 * SCAN_TILE)

    def body(u_ref, bbd_ref, pwr_ref, pwi_ref, cbd_ref, d_ref, w1_ref, w2_ref, yc_ref, xr_ref, xi_ref, sr, si):
        @pl.when(pl.program_id(0) == 0)
        def _():
            sr[...] = jnp.zeros_like(sr)
            si[...] = jnp.zeros_like(si)

        u = u_ref[...]
        bu = _mm(u, bbd_ref[...])
        xr_ref[...] = bu[:, 0:N_STATE]
        xi_ref[...] = bu[:, N_STATE:2 * N_STATE]
        _tile_scan(xr_ref, xi_ref, pwr_ref, pwi_ref, sr, si, 1.0, False)
        x = jnp.concatenate([xr_ref[...], xi_ref[...]], axis=1)
        y2 = _gelu(_mm_nt(x, cbd_ref[...]) + d_ref[...] * u)
        yc_ref[...] = _mm(y2, w1_ref[...]) * _sigmoid(_mm(y2, w2_ref[...]))

    big = pl.BlockSpec((C_WIDTH, 2 * N_STATE), lambda i: (0, 0))
    tab = pl.BlockSpec((SCAN_SEG, N_STATE), lambda i: (0, 0))
    wsp = pl.BlockSpec((C_WIDTH, C_WIDTH), lambda i: (0, 0))
    xs = pl.BlockSpec((tt, N_STATE), lambda i: (i, 0))
    return pl.pallas_call(
        body, name="ssm_fwd", grid=(T // tt,),
        in_specs=[pl.BlockSpec((tt, C_WIDTH), lambda i: (i, 5)), big, tab, tab, big, pl.BlockSpec((1, C_WIDTH), lambda i: (0, 0)), wsp, wsp],
        out_specs=[pl.BlockSpec((tt, C_WIDTH), lambda i: (i, 0)), xs, xs],
        out_shape=[_sds((T, C_WIDTH)), _sds((T, N_STATE)), _sds((T, N_STATE))],
        scratch_shapes=[pltpu.VMEM((1, N_STATE), F32)] * 2,
        compiler_params=_cp("arbitrary"))(z, bbd, pwr, pwi, cbd, dsk, w1, w2)


def _ssm_bwd(dyc, z, xr, xi, bbd, pwr, pwi, cbd, dsk, w1, w2):
    T = z.shape[0]
    tt = min(T, SCAN_TILE)
    nt = T // tt

    def tile_of(i):
        return nt - 1 - i

    def body(dyc_ref, u_ref, xr_ref, xi_ref, xpr_ref, xpi_ref, bbd_ref, pwr_ref, pwi_ref, cbd_ref, d_ref, w1_ref, w2_ref,
             du_ref, dw1_ref, dw2_ref, dcr_ref, dci_ref, dbbd_ref, dlr_ref, dli_ref, dd_ref, gr, gi, sr, si):
        i = pl.program_id(0)
        ti = nt - 1 - i

        @pl.when(i == 0)
        def _():
            sr[...] = jnp.zeros_like(sr)
            si[...] = jnp.zeros_like(si)
            for acc in (dw1_ref, dw2_ref, dcr_ref, dci_ref, dbbd_ref, dlr_ref, dli_ref, dd_ref):
                acc[...] = jnp.zeros_like(acc)

        u = u_ref[...]
        xr_t, xi_t = xr_ref[...], xi_ref[...]
        y = _mm_nt(jnp.concatenate([xr_t, xi_t], axis=1), cbd_ref[...]) + d_ref[...] * u
        y2 = _gelu(y)
        a1 = _mm(y2, w1_ref[...])
        sg = _sigmoid(_mm(y2, w2_ref[...]))
        dyc_t = dyc_ref[...]
        da1 = dyc_t * sg
        da2 = dyc_t * a1 * sg * (1.0 - sg)
        dy = (_mm_nt(da1, w1_ref[...]) + _mm_nt(da2, w2_ref[...])) * _gelu_grad(y)
        gx = _mm(dy, cbd_ref[...])
        gr[...] = gx[:, 0:N_STATE]
        gi[...] = gx[:, N_STATE:2 * N_STATE]
        _tile_scan(gr, gi, pwr_ref, pwi_ref, sr, si, -1.0, True)
        ar, ai = gr[...], gi[...]
        first_row = _row(ar.shape) == 0
        live = jnp.where(ti > 0, 1.0, 0.0)
        xsr = jnp.where(first_row, xpr_ref[7:8, :] * live, pltpu.roll(xr_t, 1, 0))
        xsi = jnp.where(first_row, xpi_ref[7:8, :] * live, pltpu.roll(xi_t, 1, 0))
        dlr_ref[...] += jnp.sum(ar * xsr + ai * xsi, axis=0, keepdims=True)
        dli_ref[...] += jnp.sum(ai * xsr - ar * xsi, axis=0, keepdims=True)
        dd_ref[...] += jnp.sum(dy * u, axis=0, keepdims=True)
        arai = jnp.concatenate([ar, ai], axis=1)
        du_ref[...] = _mm_nt(arai, bbd_ref[...]) + d_ref[...] * dy
        dw1_ref[...] += _mm_tn(y2, da1)
        dw2_ref[...] += _mm_tn(y2, da2)
        dcr_ref[...] += _mm_tn(dy, xr_t)
        dci_ref[...] += _mm_tn(dy, xi_t)
        dbbd_ref[...] += _mm_tn(u, arai)

    def prev(i):
        return jnp.maximum(tile_of(i) * (tt // 8) - 1, 0)

    big = pl.BlockSpec((C_WIDTH, 2 * N_STATE), lambda i: (0, 0))
    tab = pl.BlockSpec((SCAN_SEG, N_STATE), lambda i: (0, 0))
    srow = pl.BlockSpec((1, N_STATE), lambda i: (0, 0))
    wsp = pl.BlockSpec((C_WIDTH, C_WIDTH), lambda i: (0, 0))
    xs = pl.BlockSpec((tt, N_STATE), lambda i: (tile_of(i), 0))
    xp = pl.BlockSpec((8, N_STATE), lambda i: (prev(i), 0))
    cw = pl.BlockSpec((tt, C_WIDTH), lambda i: (tile_of(i), 0))
    drow = pl.BlockSpec((1, C_WIDTH), lambda i: (0, 0))
    return pl.pallas_call(
        body, name="ssm_bwd", grid=(nt,),
        in_specs=[cw, pl.BlockSpec((tt, C_WIDTH), lambda i: (tile_of(i), 5)), xs, xs, xp, xp, big, tab, tab, big, drow, wsp, wsp],
        out_specs=[cw, wsp, wsp, pl.BlockSpec((C_WIDTH, N_STATE), lambda i: (0, 0)), pl.BlockSpec((C_WIDTH, N_STATE), lambda i: (0, 0)), big,
                   srow, srow, drow],
        out_shape=[_sds((T, C_WIDTH)), _sds((C_WIDTH, C_WIDTH)), _sds((C_WIDTH, C_WIDTH)), _sds((C_WIDTH, N_STATE)), _sds((C_WIDTH, N_STATE)),
                   _sds((C_WIDTH, 2 * N_STATE)), _sds((1, N_STATE)), _sds((1, N_STATE)), _sds((1, C_WIDTH))],
        scratch_shapes=[pltpu.VMEM((tt, N_STATE), F32)] * 2 + [pltpu.VMEM((1, N_STATE), F32)] * 2,
        compiler_params=_cp("arbitrary"))(dyc, z, xr, xi, xr, xi, bbd, pwr, pwi, cbd, dsk, w1, w2)


_GROUPS = ((0, A_WIDTH), (A_WIDTH, A_WIDTH + B_WIDTH), (A_WIDTH + B_WIDTH, D_MODEL))


def _merge_fwd(h, ya, yb, yc, g, w):
    T = h.shape[0]
    tm = min(T, 512)

    def body(h_ref, a_ref, b_ref, c_ref, g_ref, w_ref, o_ref):
        yn = jnp.concatenate([y * _rms_stat(y) for y in (a_ref[...], b_ref[...], c_ref[...])], axis=1) * g_ref[...]
        o_ref[...] = h_ref[...] + _mm(yn, w_ref[...])

    def rows(w_):
        return pl.BlockSpec((tm, w_), lambda i: (i, 0))

    return pl.pallas_call(
        body, name="merge_fwd", grid=(T // tm,),
        in_specs=[rows(D_MODEL), rows(A_WIDTH), rows(B_WIDTH), rows(C_WIDTH), pl.BlockSpec((1, D_MODEL), lambda i: (0, 0)),
                  pl.BlockSpec((D_MODEL, D_MODEL), lambda i: (0, 0))],
        out_specs=rows(D_MODEL), out_shape=_sds((T, D_MODEL)), compiler_params=_cp("parallel"))(h, ya, yb, yc, g, w)


def _merge_bwd(gres, ya, yb, yc, g, w):
    T = gres.shape[0]
    tm = min(T, 512)
    nt = T // tm

    def body(gr_ref, a_ref, b_ref, c_ref, g_ref, w_ref, da_ref, db_ref, dc_ref, dw_ref, dg_ref, acc):
        i = pl.program_id(0)

        @pl.when(i == 0)
        def _():
            dg_ref[...] = jnp.zeros_like(dg_ref)
            acc[...] = jnp.zeros_like(acc)

        gr = gr_ref[...]
        dyn = _mm_nt(gr, w_ref[...])
        yns, dgs = [], []
        for (c0, c1), y_ref, d_ref in zip(_GROUPS, (a_ref, b_ref, c_ref), (da_ref, db_ref, dc_ref)):
            y = y_ref[...]
            r = _rms_stat(y)
            gg = g_ref[:, c0:c1]
            dx, dg = _rms_bwd(y, r, gg, dyn[:, c0:c1])
            d_ref[...] = dx
            dgs.append(dg)
            yns.append(y * r * gg)
        dg_ref[...] += jnp.concatenate(dgs, axis=1)
        acc[...] += _mm_tn(jnp.concatenate(yns, axis=1), gr)

        @pl.when(i == nt - 1)
        def _():
            dw_ref[...] = acc[...].astype(WIRE)

    def rows(w_):
        return pl.BlockSpec((tm, w_), lambda i: (i, 0))

    row = pl.BlockSpec((1, D_MODEL), lambda i: (0, 0))
    full = pl.BlockSpec((D_MODEL, D_MODEL), lambda i: (0, 0))
    return pl.pallas_call(
        body, name="merge_bwd", grid=(nt,),
        in_specs=[rows(D_MODEL), rows(A_WIDTH), rows(B_WIDTH), rows(C_WIDTH), row, full],
        out_specs=[rows(A_WIDTH), rows(B_WIDTH), rows(C_WIDTH), full, row],
        out_shape=[_sds((T, A_WIDTH)), _sds((T, B_WIDTH)), _sds((T, C_WIDTH)), _sds((D_MODEL, D_MODEL), WIRE), _sds((1, D_MODEL))],
        scratch_shapes=[pltpu.VMEM((D_MODEL, D_MODEL), F32)],
        compiler_params=_cp("arbitrary"))(gres, ya, yb, yc, g, w)


FF_BLOCK = D_FF // N_DEV


def _load_weights(w1_hbm, w2_hbm, w1, w2, sem):
    @pl.when(pl.program_id(0) == 0)
    def _():
        copies = [pltpu.make_async_copy(w1_hbm.at[j], w1.at[:, pl.ds(j * FF_BLOCK, FF_BLOCK)], sem.at[j]) for j in range(N_DEV)]
        copies.append(pltpu.make_async_copy(w2_hbm, w2, sem.at[N_DEV]))
        for cp in copies:
            cp.start()
        for cp in copies:
            cp.wait()


def _mlp_weight_scratch():
    return [pltpu.VMEM((D_MODEL, D_FF), MXU), pltpu.VMEM((D_FF, D_MODEL), MXU), pltpu.SemaphoreType.DMA((N_DEV + 1,))]


def _mlp_fwd(h, g, w1, w2):
    T = h.shape[0]
    tm = min(T, 512)

    def body(h_ref, g_ref, w1_hbm, w2_hbm, o_ref, r_ref, w1_v, w2_v, sem):
        _load_weights(w1_hbm, w2_hbm, w1_v, w2_v, sem)
        x = h_ref[...]
        a = jnp.maximum(_mm(x * _rms_stat(x) * g_ref[...], w1_v[...]), 0.0)
        r = (a * a).astype(MXU)
        r_ref[...] = r
        o_ref[...] = x + _mm(r, w2_v[...])

    rows = pl.BlockSpec((tm, D_MODEL), lambda i: (i, 0))
    hbm = pl.BlockSpec(memory_space=pl.ANY)
    return pl.pallas_call(
        body, name="mlp_fwd", grid=(T // tm,),
        in_specs=[rows, pl.BlockSpec((1, D_MODEL), lambda i: (0, 0)), hbm, hbm],
        out_specs=[rows, pl.BlockSpec((tm, D_FF), lambda i: (i, 0))],
        out_shape=[_sds((T, D_MODEL)), _sds((T, D_FF), MXU)],
        scratch_shapes=_mlp_weight_scratch(), compiler_params=_cp("arbitrary"))(h, g, w1, w2)


def _mlp_bwd(gres, h, g, r, w1, w2):
    T = h.shape[0]
    tm = min(T, 256)

    def body(gr_ref, h_ref, g_ref, r_ref, w1_hbm, w2_hbm, dh_ref, hn_ref, da_ref, dg_ref, w1_v, w2_v, sem):
        _load_weights(w1_hbm, w2_hbm, w1_v, w2_v, sem)

        @pl.when(pl.program_id(0) == 0)
        def _():
            dg_ref[...] = jnp.zeros_like(dg_ref)

        gr = gr_ref[...]
        da = (_mm_nt(gr, w2_v[...]) * (2.0 * jnp.sqrt(r_ref[...].astype(F32)))).astype(MXU)
        da_ref[...] = da
        x = h_ref[...]
        rs = _rms_stat(x)
        gg = g_ref[...]
        dx, dg = _rms_bwd(x, rs, gg, _mm_nt(da, w1_v[...]))
        dh_ref[...] = gr + dx
        dg_ref[...] += dg
        hn_ref[...] = (x * rs * gg).astype(MXU)

    rows = pl.BlockSpec((tm, D_MODEL), lambda i: (i, 0))
    row = pl.BlockSpec((1, D_MODEL), lambda i: (0, 0))
    wide = pl.BlockSpec((tm, D_FF), lambda i: (i, 0))
    hbm = pl.BlockSpec(memory_space=pl.ANY)
    return pl.pallas_call(
        body, name="mlp_bwd", grid=(T // tm,),
        in_specs=[rows, rows, row, wide, hbm, hbm], out_specs=[rows, rows, wide, row],
        out_shape=[_sds((T, D_MODEL)), _sds((T, D_MODEL), MXU), _sds((T, D_FF), MXU), _sds((1, D_MODEL))],
        scratch_shapes=_mlp_weight_scratch(), compiler_params=_cp("arbitrary"))(gres, h, g, r, w1, w2)


def _ple_fwd(h, p, l, g, wg, wp):
    T = h.shape[0]
    tm = min(T, 512)

    def body(h_ref, p_ref, g_ref, wg_ref, wp_ref, o_ref):
        x = h_ref[...]
        gate = _sigmoid(_mm(x * _rms_stat(x) * g_ref[...], wg_ref[...]))
        o_ref[...] = x + gate * _mm(p_ref[...], wp_ref[...])

    rows = pl.BlockSpec((tm, D_MODEL), lambda i: (i, 0))
    return pl.pallas_call(
        body, name="ple_fwd", grid=(T // tm,),
        in_specs=[rows, pl.BlockSpec((None, tm, PLE_DIM), lambda i: (l, i, 0)), pl.BlockSpec((1, D_MODEL), lambda i: (0, 0)),
                  pl.BlockSpec((D_MODEL, D_MODEL), lambda i: (0, 0)), pl.BlockSpec((PLE_DIM, D_MODEL), lambda i: (0, 0))],
        out_specs=rows, out_shape=_sds((T, D_MODEL)), compiler_params=_cp("parallel"))(h, p, g, wg, wp)


def _ple_bwd(gres, h, p, l, g, wg, wp):
    T = h.shape[0]
    tm = min(T, 512)
    nt = T // tm
    shard = D_MODEL // N_DEV

    def body(gr_ref, h_ref, p_ref, g_ref, wg_ref, wp_ref, dh_ref, dwg_ref, dwp_ref, dg_ref, accg, accp):
        i = pl.program_id(0)

        @pl.when(i == 0)
        def _():
            dg_ref[...] = jnp.zeros_like(dg_ref)
            accg[...] = jnp.zeros_like(accg)
            accp[...] = jnp.zeros_like(accp)

        x = h_ref[...]
        r = _rms_stat(x)
        gg = g_ref[...]
        hn = x * r * gg
        gate = _sigmoid(_mm(hn, wg_ref[...]))
        pe = p_ref[...]
        e = _mm(pe, wp_ref[...])
        gr = gr_ref[...]
        dgp = gr * e * gate * (1.0 - gate)
        dx, dg = _rms_bwd(x, r, gg, _mm_nt(dgp, wg_ref[...]))
        dh_ref[...] = gr + dx
        dg_ref[...] += dg
        accg[...] += _mm_tn(hn, dgp)
        accp[...] += _mm_tn(pe, gr * gate)

        @pl.when(i == nt - 1)
        def _():
            dwg_ref[...] = accg[...].astype(WIRE)
            for d in range(N_DEV):
                dwp_ref[d] = accp[:, d * shard:(d + 1) * shard].astype(WIRE)

    rows = pl.BlockSpec((tm, D_MODEL), lambda i: (i, 0))
    row = pl.BlockSpec((1, D_MODEL), lambda i: (0, 0))
    full = pl.BlockSpec((D_MODEL, D_MODEL), lambda i: (0, 0))
    return pl.pallas_call(
        body, name="ple_bwd", grid=(nt,),
        in_specs=[rows, rows, pl.BlockSpec((None, tm, PLE_DIM), lambda i: (l, i, 0)), row, full, pl.BlockSpec((PLE_DIM, D_MODEL), lambda i: (0, 0))],
        out_specs=[rows, full, pl.BlockSpec((N_DEV, PLE_DIM, shard), lambda i: (0, 0, 0)), row],
        out_shape=[_sds((T, D_MODEL)), _sds((D_MODEL, D_MODEL), WIRE), _sds((N_DEV, PLE_DIM, shard), WIRE), _sds((1, D_MODEL))],
        scratch_shapes=[pltpu.VMEM((D_MODEL, D_MODEL), F32), pltpu.VMEM((PLE_DIM, D_MODEL), F32)],
        compiler_params=_cp("arbitrary"))(gres, h, p, g, wg, wp)


def _loss_head(h, target):
    T = h.shape[0]
    tm = min(T, 1024)

    def body(h_ref, t_ref, dh_ref, l_ref):
        @pl.when(pl.program_id(0) == 0)
        def _():
            l_ref[...] = jnp.zeros_like(l_ref)

        e = h_ref[...] - t_ref[...]
        dh_ref[...] = e * (1.0 / D_MODEL)
        l_ref[...] += jnp.zeros_like(l_ref) + 0.5 * jnp.sum(jnp.mean(e * e, axis=-1, keepdims=True))

    rows = pl.BlockSpec((tm, D_MODEL), lambda i: (i, 0))
    return pl.pallas_call(
        body, name="loss_head", grid=(T // tm,), in_specs=[rows, rows],
        out_specs=[rows, pl.BlockSpec((8, LANES), lambda i: (0, 0))], out_shape=[_sds((T, D_MODEL)), _sds((8, LANES))],
        compiler_params=_cp("arbitrary"))(h, target)


TN_ROWS = 2048

def _tn(a, b, m, n, *, bm, bn, n_major=False, dtype=F32, name="tn"):
    T = a.shape[0]
    tk = min(T, TN_ROWS)
    nk = T // tk
    assert m % bm == 0 and n % bn == 0 and (not n_major or bm == m)

    def body(a_ref, b_ref, o_ref, acc):
        k = pl.program_id(2)

        @pl.when(k == 0)
        def _():
            acc[...] = jnp.zeros_like(acc)

        acc[...] += _mm_tn(a_ref[...], b_ref[...])

        @pl.when(k == nk - 1)
        def _():
            o_ref[...] = acc[...].astype(dtype)

    if n_major:
        out_spec = pl.BlockSpec((None, bm, bn), lambda i, j, k: (j, 0, 0))
        out_shape = _sds((n // bn, m, bn), dtype)
    else:
        out_spec = pl.BlockSpec((bm, bn), lambda i, j, k: (i, j))
        out_shape = _sds((m, n), dtype)
    return pl.pallas_call(
        body, name=name, grid=(m // bm, n // bn, nk),
        in_specs=[pl.BlockSpec((tk, bm), lambda i, j, k: (k, i)), pl.BlockSpec((tk, bn), lambda i, j, k: (k, j))],
        out_specs=out_spec, out_shape=out_shape, scratch_shapes=[pltpu.VMEM((bm, bn), F32)],
        compiler_params=_cp("parallel", "parallel", "arbitrary"))(a, b)


def _row_tile(R, C):
    for cand in (512, 256, 128, 64, 32, 16, 8):
        if R % cand == 0 and cand * C * 4 <= 2 ** 20:
            return cand
    return R


def _sum_slots(land):
    S, R, C = land.shape
    tr = _row_tile(R, C)

    def body(l_ref, o_ref):
        acc = l_ref[0].astype(F32)
        for s in range(1, S):
            acc = acc + l_ref[s].astype(F32)
        o_ref[...] = acc

    return pl.pallas_call(
        body, name="sum_slots", grid=(R // tr,), in_specs=[pl.BlockSpec((S, tr, C), lambda i: (0, i, 0))],
        out_specs=pl.BlockSpec((tr, C), lambda i: (i, 0)), out_shape=_sds((R, C)), compiler_params=_cp("parallel"))(land)


def _sum_adamw(land, w, m, v):
    R, C = w.shape
    S = land.shape[0]
    tr = _row_tile(R, C)

    def body(l_ref, w_ref, m_ref, v_ref, g_ref, d_ref, nm_ref, nv_ref):
        gg = l_ref[0].astype(F32)
        for s in range(1, S):
            gg = gg + l_ref[s].astype(F32)
        g_ref[...] = gg
        nm = ADAM_B1 * m_ref[...] + (1.0 - ADAM_B1) * gg
        nv = ADAM_B2 * v_ref[...] + (1.0 - ADAM_B2) * (gg * gg)
        m_hat = nm / (1.0 - ADAM_B1 ** ADAM_STEP)
        v_hat = nv / (1.0 - ADAM_B2 ** ADAM_STEP)
        d_ref[...] = -ADAM_LR * (m_hat / (jnp.sqrt(v_hat) + ADAM_EPS) + ADAM_WD * w_ref[...])
        nm_ref[...] = nm
        nv_ref[...] = nv

    blk = pl.BlockSpec((tr, C), lambda i: (i, 0))
    return pl.pallas_call(
        body, name="sum_adamw", grid=(R // tr,), in_specs=[pl.BlockSpec((S, tr, C), lambda i: (0, i, 0))] + [blk] * 3,
        out_specs=[blk] * 4, out_shape=[_sds((R, C))] * 4, compiler_params=_cp("parallel"))(land, w, m, v)


def _all_to_all(pairs, name):
    n = len(pairs)

    def body(*refs):
        srcs, lands = refs[:n], refs[2 * n:3 * n]
        send, recv, loc = refs[3 * n:]
        x, y, c = lax.axis_index("x"), lax.axis_index("y"), lax.axis_index("c")
        me = 4 * x + 2 * y + c
        own = [pltpu.make_async_copy(pairs[t][2](srcs[t], me, me), pairs[t][3](lands[t], me), loc.at[t]) for t in range(n)]
        for cp in own:
            cp.start()
        sends, recvs = [], []
        for k in range(1, N_DEV):
            px, py, pc = x ^ (k >> 2), y ^ ((k >> 1) & 1), c ^ (k & 1)
            peer = 4 * px + 2 * py + pc
            for t in range(n):
                src = pairs[t][2](srcs[t], me, peer)
                cp = pltpu.make_async_remote_copy(src_ref=src, dst_ref=pairs[t][3](lands[t], me), send_sem=send.at[t, k],
                                                  recv_sem=recv.at[t, k], device_id=(px, py, pc), device_id_type=MESH)
                cp.start()
                sends.append(cp)
                recvs.append(pltpu.make_async_remote_copy(src_ref=src, dst_ref=pairs[t][3](lands[t], peer), send_sem=send.at[t, k],
                                                          recv_sem=recv.at[t, k], device_id=(px, py, pc), device_id_type=MESH))
        for cp in recvs:
            cp.wait_recv()
        for cp in sends:
            cp.wait_send()
        for cp in own:
            cp.wait()

    anyspec = pl.BlockSpec(memory_space=pl.ANY)
    lands = [pr[1] for pr in pairs]
    return pl.pallas_call(
        body, name=name, in_specs=[anyspec] * (2 * n), out_specs=[anyspec] * n,
        out_shape=[_sds(a.shape, a.dtype) for a in lands], input_output_aliases={n + t: t for t in range(n)},
        scratch_shapes=[pltpu.SemaphoreType.DMA((n, N_DEV)), pltpu.SemaphoreType.DMA((n, N_DEV)), pltpu.SemaphoreType.DMA((n,))],
        compiler_params=pltpu.CompilerParams(has_side_effects=True))(*[pr[0] for pr in pairs], *lands)


def _gather_whole(src):
    land = lax.empty((N_DEV,) + src.shape, src.dtype)
    return _all_to_all([(src, land, lambda ref, me, peer: ref, lambda ref, sender: ref.at[sender])], "gather_small_grads")[0]


_HBM = pl.BlockSpec(memory_space=pltpu.HBM)
_SEM = pl.BlockSpec(memory_space=pltpu.SEMAPHORE)
_EFFECT = pltpu.SideEffectType.DATAFLOW_SIDE_EFFECTING


def _peers():
    x, y, c = lax.axis_index("x"), lax.axis_index("y"), lax.axis_index("c")
    out = []
    for k in range(1, N_DEV):
        px, py, pc = x ^ (k >> 2), y ^ ((k >> 1) & 1), c ^ (k & 1)
        out.append((k, (px, py, pc), 4 * px + 2 * py + pc))
    return 4 * x + 2 * y + c, out


def _route(mode, layer):
    if mode == "all":
        return (lambda ref, peer: ref), (lambda ref, sender: ref.at[sender])
    return (lambda ref, peer: ref.at[peer]), (lambda ref, sender: ref.at[sender, layer])


def _split_start(srcs, lands, modes, layer, after, name):
    n = len(srcs)
    routes = [_route(m, layer) for m in modes]

    def body(*refs):
        src, land = refs[:n], refs[n:2 * n]
        send, recv, token = refs[2 * n + 1], refs[2 * n + 2], refs[-1]
        me, peers = _peers()
        for k, dev, peer in peers:
            for t, (src_of, dst_of) in enumerate(routes):
                pltpu.make_async_remote_copy(src_ref=src_of(src[t], peer), dst_ref=dst_of(land[t], me), send_sem=send.at[t * N_DEV + k],
                                             recv_sem=recv.at[t * N_DEV + k], device_id=dev, device_id_type=MESH).start()
        token[...] = jnp.zeros_like(token)

    bufs = list(srcs) + list(lands)
    outs = pl.pallas_call(
        body, name=name,
        out_shape=(pltpu.SemaphoreType.DMA((n * N_DEV,)), pltpu.SemaphoreType.DMA((n * N_DEV,)),
                   *[pltpu.HBM(a.shape, a.dtype) for a in bufs], _sds((8, LANES))),
        in_specs=[_HBM] * (2 * n) + [pl.BlockSpec(memory_space=pl.ANY)],
        out_specs=(_SEM, _SEM, *[_HBM] * (2 * n), pl.BlockSpec(memory_space=pltpu.VMEM)),
        input_output_aliases={i: 2 + i for i in range(2 * n)},
        compiler_params=pltpu.CompilerParams(has_side_effects=_EFFECT),
    )(*[pltpu.with_memory_space_constraint(a, pltpu.HBM) for a in bufs], after)
    return outs[0], outs[1], list(outs[2:2 + n]), list(outs[2 + n:2 + 2 * n]), outs[-1]


def _split_wait(send, recv, srcs, lands, modes, layer, after, name):
    n = len(srcs)
    routes = [_route(m, layer) for m in modes]

    def body(*refs):
        src, land = refs[:n], refs[n:2 * n]
        send_r, recv_r = refs[2 * n], refs[2 * n + 1]
        _, peers = _peers()
        for k, dev, peer in peers:
            for t, (src_of, dst_of) in enumerate(routes):
                cp = pltpu.make_async_remote_copy(src_ref=src_of(src[t], peer), dst_ref=dst_of(land[t], peer), send_sem=send_r.at[t * N_DEV + k],
                                                  recv_sem=recv_r.at[t * N_DEV + k], device_id=dev, device_id_type=MESH)
                cp.wait_send()
                cp.wait_recv()

    bufs = list(srcs) + list(lands)
    outs = pl.pallas_call(
        body, name=name, out_shape=tuple(pltpu.HBM(a.shape, a.dtype) for a in bufs),
        in_specs=[_HBM] * (2 * n) + [_SEM, _SEM, pl.BlockSpec(memory_space=pl.ANY)], out_specs=[_HBM] * (2 * n),
        input_output_aliases={i: i for i in range(2 * n)},
        compiler_params=pltpu.CompilerParams(has_side_effects=_EFFECT),
    )(*bufs, send, recv, after)
    return list(outs[:n]), list(outs[n:])


SHARDED = ("w_in", "glu_w1", "glu_w2", "w_out", "w_ff1", "w_ff2", "w_ple_gate", "w_ple_proj")
SMALL = ("attn_norm_g", "gmlp_ln_g", "gmlp_ln_b", "gmlp_ws", "gmlp_bs", "q_norm_g", "k_norm_g", "sinks", "ssm_a_re", "ssm_a_im",
         "ssm_log_dt", "ssm_b_re", "ssm_b_im", "ssm_c_re", "ssm_c_im", "ssm_d", "mix_out_g", "mlp_norm_g", "ple_norm_g")
WEIGHTS = ("attn_norm_g", "w_in", "gmlp_ln_g", "gmlp_ln_b", "gmlp_ws", "gmlp_bs", "q_norm_g", "k_norm_g", "sinks", "ssm_a_re", "ssm_a_im",
           "ssm_log_dt", "ssm_b_re", "ssm_b_im", "ssm_c_re", "ssm_c_im", "ssm_d", "glu_w1", "glu_w2", "mix_out_g", "w_out", "mlp_norm_g",
           "w_ff1", "w_ff2", "ple_norm_g", "w_ple_gate", "w_ple_proj")
FLAT_COLS = 1024


PACK_TILE_ROWS = 8
PACK_ROWS_MULTIPLE = PACK_TILE_ROWS * N_DEV


def _packed_rows(shape):
    return -(-math.prod(shape) // (PACK_TILE_ROWS * FLAT_COLS)) * PACK_TILE_ROWS


def _pack(arrs, dtype):
    blocks = []
    for a in arrs:
        flat = a.astype(dtype).reshape(-1)
        pad = _packed_rows(a.shape) * FLAT_COLS - flat.shape[0]
        if pad:
            flat = jnp.concatenate([flat, jnp.zeros((pad,), dtype)])
        blocks.append(flat.reshape(-1, FLAT_COLS))
    rows = sum(b.shape[0] for b in blocks)
    if rows % PACK_ROWS_MULTIPLE:
        blocks.append(jnp.zeros((PACK_ROWS_MULTIPLE - rows % PACK_ROWS_MULTIPLE, FLAT_COLS), dtype))
    return jnp.concatenate(blocks, axis=0)


def _unpack(flat, shapes):
    out, r = [], 0
    for s in shapes:
        nr = _packed_rows(s)
        out.append(flat[r:r + nr].reshape(-1)[:math.prod(s)].reshape(s))
        r += nr
    return out


def _from_col_major(s):
    n, rows, cs = s.shape
    return s.transpose(1, 0, 2).reshape(rows, n * cs)


EARLY = ("w_in", "glu_w1", "glu_w2")
LATE = ("w_out", "w_ff1", "w_ff2", "w_ple_gate", "w_ple_proj")
GRADS_MID = ("w_ff1", "w_ff2", "w_ple_gate", "w_ple_proj")
GRADS_END = ("w_in", "glu_w1", "glu_w2", "w_out")


def _layer_fwd(h, p, l, cs, sn, W, late_weights, S, sp):
    z = _inproj_fwd(h, S["attn_norm_g"], W["w_in"])
    ya = _gmlp_fwd(z, S["lng"], S["lnb"], S["gmlp_ws"], S["bsx"])
    yb = _attn_fwd(z, cs, sn, S["qg"], S["kg"], S["sinks"])
    yc, xr, xi = _ssm_fwd(z, sp["bbd"], sp["pwr"], sp["pwi"], sp["cbd"], S["ssm_d"], W["glu_w1"], W["glu_w2"])
    late, token = late_weights(ya[0:8, 0:LANES] + yb[0:8, 0:LANES] + yc[0:8, 0:LANES])
    W = {**W, **late}
    h1 = _merge_fwd(h, ya, yb, yc, _behind(S["mix_out_g"], token), W["w_out"])
    h2, r = _mlp_fwd(h1, S["mlp_norm_g"], W["w_ff1"], W["w_ff2"])
    h3 = _ple_fwd(h2, p, l, S["ple_norm_g"], W["w_ple_gate"], W["w_ple_proj"])
    return h3, dict(h=h, z=z, ya=ya, yb=yb, yc=yc, xr=xr, xi=xi, h1=h1, r=r, h2=h2), W


def _layer_bwd(g3, p, l, cs, sn, W, S, sp, A, raw, mid_bwd):
    G = {}
    g2, dwg, G["w_ple_proj"], G["ple_norm_g"] = _ple_bwd(g3, A["h2"], p, l, S["ple_norm_g"], W["w_ple_gate"], W["w_ple_proj"])
    G["w_ple_gate"] = dwg.reshape(N_DEV, -1, D_MODEL)
    g1, hn, da, G["mlp_norm_g"] = _mlp_bwd(g2, A["h1"], S["mlp_norm_g"], A["r"], W["w_ff1"], W["w_ff2"])
    G["w_ff1"] = _tn(hn, da, D_MODEL, D_FF, bm=D_MODEL, bn=FF_BLOCK, n_major=True, dtype=WIRE, name="tn_ff1")
    G["w_ff2"] = _tn(A["r"], g2, D_FF, D_MODEL, bm=1024, bn=1024, dtype=WIRE, name="tn_ff2").reshape(N_DEV, -1, D_MODEL)
    token = mid_bwd(g1, G)
    dya, dyb, dyc, dwo, G["mix_out_g"] = _merge_bwd(g1, A["ya"], A["yb"], A["yc"], _behind(S["mix_out_g"], token), W["w_out"])
    G["w_out"] = dwo.reshape(N_DEV, -1, D_MODEL)
    dzc, dw1, dw2, dcr, dci, dbbd, dlr, dli, dd = _ssm_bwd(dyc, A["z"], A["xr"], A["xi"], sp["bbd"], sp["pwr"], sp["pwi"], sp["cbd"],
                                                          S["ssm_d"], W["glu_w1"], W["glu_w2"])
    G["glu_w1"] = dw1.astype(WIRE).reshape(N_DEV, -1, C_WIDTH)
    G["glu_w2"] = dw2.astype(WIRE).reshape(N_DEV, -1, C_WIDTH)
    dare, daim, dldt, dbtr, dbti, dcre, dcim = _ssm_param_bwd(
        raw["are"], raw["aim"], raw["ldt"], raw["are_x"], raw["aim_x"], raw["ldt_x"], raw["btr"], raw["bti"],
        dlr.reshape(C_GROUPS, C_STATE), dli.reshape(C_GROUPS, C_STATE), dbbd, dcr, dci)
    G["ssm_a_re"], G["ssm_a_im"], G["ssm_log_dt"] = dare, daim, dldt[:, 0]
    G["ssm_b_re"] = dbtr.reshape(C_GROUPS, C_GROUP, C_STATE).transpose(0, 2, 1)
    G["ssm_b_im"] = dbti.reshape(C_GROUPS, C_GROUP, C_STATE).transpose(0, 2, 1)
    G["ssm_c_re"] = dcre.reshape(C_GROUPS, C_GROUP, C_STATE)
    G["ssm_c_im"] = dcim.reshape(C_GROUPS, C_GROUP, C_STATE)
    G["ssm_d"] = dd.reshape(C_GROUPS, C_GROUP)
    dzq, dzk, dzv, dqg, dkg, dsk = _attn_bwd(A["z"], cs, sn, S["qg"], S["kg"], S["sinks"], A["yb"], dyb)
    G["q_norm_g"] = dqg[0, :HEAD_DIM] + dqg[0, HEAD_DIM:]
    G["k_norm_g"] = dkg[0, :HEAD_DIM] + dkg[0, HEAD_DIM:]
    G["sinks"] = dsk[:, 0]
    dza, dws, dbs, dlng, dlnb = _gmlp_bwd(A["z"], dya, S["lng"], S["lnb"], S["gmlp_ws"], S["bsx"])
    G["gmlp_ws"] = dws
    G["gmlp_bs"] = dbs[:, :, 0]
    G["gmlp_ln_g"] = dlng.reshape(A_HEADS, 2, HEAD_DIM)[:, 1]
    G["gmlp_ln_b"] = dlnb.reshape(A_HEADS, 2, HEAD_DIM)[:, 1]
    g0, G["w_in"], G["attn_norm_g"] = _inproj_bwd(g1, A["h"], S["attn_norm_g"], W["w_in"], dza, dzq, dzk, dzv, dzc)
    return g0, G


def _small_layouts(P, l):
    def row(a):
        return a.reshape(1, -1)

    zeros = jnp.zeros((A_HEADS, HEAD_DIM), F32)
    S = dict(
        attn_norm_g=row(P["attn_norm_g"][l]), mix_out_g=row(P["mix_out_g"][l]), mlp_norm_g=row(P["mlp_norm_g"][l]),
        ple_norm_g=row(P["ple_norm_g"][l]),
        lng=jnp.stack([zeros, P["gmlp_ln_g"][l]], axis=1).reshape(1, 512),
        lnb=jnp.stack([zeros, P["gmlp_ln_b"][l]], axis=1).reshape(1, 512),
        gmlp_ws=P["gmlp_ws"][l],
        bsx=jnp.broadcast_to(P["gmlp_bs"][l][:, :, None], (A_HEADS, CHUNK, CHUNK)),
        qg=jnp.tile(P["q_norm_g"][l], 2).reshape(1, LANES), kg=jnp.tile(P["k_norm_g"][l], 2).reshape(1, LANES),
        sinks=jnp.broadcast_to(P["sinks"][l][:, None], (8, LANES)),
        ssm_d=row(P["ssm_d"][l]),
    )
    are, aim = P["ssm_a_re"][l], P["ssm_a_im"][l]
    ldt = jnp.broadcast_to(P["ssm_log_dt"][l][:, None], (C_GROUPS, C_STATE))
    raw = dict(
        are=are, aim=aim, ldt=ldt,
        are_x=jnp.repeat(are, C_GROUP, axis=0), aim_x=jnp.repeat(aim, C_GROUP, axis=0), ldt_x=jnp.repeat(ldt, C_GROUP, axis=0),
        btr=P["ssm_b_re"][l].transpose(0, 2, 1).reshape(C_WIDTH, C_STATE), bti=P["ssm_b_im"][l].transpose(0, 2, 1).reshape(C_WIDTH, C_STATE),
        cre=P["ssm_c_re"][l].reshape(C_WIDTH, C_STATE), cim=P["ssm_c_im"][l].reshape(C_WIDTH, C_STATE),
    )
    return S, raw


def _ssm_prep_layer(raw):
    bbd, cbd, pwr, pwi = _ssm_prep(raw["are"].reshape(1, N_STATE), raw["aim"].reshape(1, N_STATE), raw["ldt"].reshape(1, N_STATE),
                                   raw["are_x"], raw["aim_x"], raw["ldt_x"], raw["btr"], raw["bti"], raw["cre"], raw["cim"])
    return dict(bbd=bbd, cbd=cbd, pwr=pwr, pwi=pwi)


def _behind(row, token):
    return row if token is None else row + token[0:1, 0:1]


def _local_step(x, p, positions, target, P, weights_of, mid_bwd, after_bwd):
    inv = 1.0 / (ROPE_THETA ** (jnp.arange(0, HEAD_DIM, 2, dtype=F32) / HEAD_DIM))
    cs, sn = _rope_tables(positions.reshape(-1, 1), jnp.tile(inv, 4).reshape(1, LANES))
    h = x
    acts, smalls, weights = [], [], []
    for l in range(DEPTH):
        W, late_weights, token = weights_of(l, h)
        S, raw = _small_layouts(P, l)
        sp = _ssm_prep_layer(raw)
        h, A, W = _layer_fwd(h, p, l, cs, sn, W, late_weights, {**S, "attn_norm_g": _behind(S["attn_norm_g"], token)}, sp)
        acts.append(A)
        smalls.append((S, raw, sp))
        weights.append(W)
    g, lsum = _loss_head(h, target)
    grads = [None] * DEPTH
    token = None
    for l in reversed(range(DEPTH)):
        S, raw, sp = smalls[l]
        g, grads[l] = _layer_bwd(g, p, l, cs, sn, weights[l], {**S, "ple_norm_g": _behind(S["ple_norm_g"], token)}, sp, acts[l], raw,
                                 functools.partial(mid_bwd, l))
        token = after_bwd(l, g, grads[l])
    return lsum[0, 0], g, grads


def _layer_weights(g):
    layout = dict(
        w_in=_from_col_major, glu_w1=lambda a: a.reshape(C_WIDTH, C_WIDTH), glu_w2=lambda a: a.reshape(C_WIDTH, C_WIDTH),
        w_out=lambda a: a.reshape(D_MODEL, D_MODEL), w_ff1=lambda a: a, w_ff2=lambda a: a.reshape(D_FF, D_MODEL),
        w_ple_gate=lambda a: a.reshape(D_MODEL, D_MODEL), w_ple_proj=_from_col_major)
    return {n: layout[n](a) for n, a in g.items()}


def kernel(x, p, positions, attn_norm_g, w_in, gmlp_ln_g, gmlp_ln_b, gmlp_ws, gmlp_bs, q_norm_g, k_norm_g, sinks, ssm_a_re, ssm_a_im, ssm_log_dt, ssm_b_re, ssm_b_im, ssm_c_re, ssm_c_im, ssm_d, glu_w1, glu_w2, mix_out_g, w_out, mlp_norm_g, w_ff1, w_ff2, ple_norm_g, w_ple_gate, w_ple_proj, loss_target, m_attn_norm_g, m_w_in, m_gmlp_ln_g, m_gmlp_ln_b, m_gmlp_ws, m_gmlp_bs, m_q_norm_g, m_k_norm_g, m_sinks, m_ssm_a_re, m_ssm_a_im, m_ssm_log_dt, m_ssm_b_re, m_ssm_b_im, m_ssm_c_re, m_ssm_c_im, m_ssm_d, m_glu_w1, m_glu_w2, m_mix_out_g, m_w_out, m_mlp_norm_g, m_w_ff1, m_w_ff2, m_ple_norm_g, m_w_ple_gate, m_w_ple_proj, v_attn_norm_g, v_w_in, v_gmlp_ln_g, v_gmlp_ln_b, v_gmlp_ws, v_gmlp_bs, v_q_norm_g, v_k_norm_g, v_sinks, v_ssm_a_re, v_ssm_a_im, v_ssm_log_dt, v_ssm_b_re, v_ssm_b_im, v_ssm_c_re, v_ssm_c_im, v_ssm_d, v_glu_w1, v_glu_w2, v_mix_out_g, v_w_out, v_mlp_norm_g, v_w_ff1, v_w_ff2, v_ple_norm_g, v_w_ple_gate, v_w_ple_proj):
    env = dict(locals())
    P = {n: env[n] for n in WEIGHTS}
    M = {n: env["m_" + n] for n in WEIGHTS}
    V = {n: env["v_" + n] for n in WEIGHTS}
    return _step(x, p, positions, loss_target, P, M, V)


def _step(x, p, positions, loss_target, P, M, V):
    small_shapes = [P[n].shape for n in SMALL]
    me = 4 * lax.axis_index("x") + 2 * lax.axis_index("y") + lax.axis_index("c")
    nothing = jnp.zeros((8, LANES), F32)

    def put(land, own, lead):
        return lax.dynamic_update_slice(land, own.reshape((1,) * len(lead) + own.shape), tuple(lead) + (0,) * own.ndim)

    def gather_start(l, names, after, tag):
        shards = [P[n][l].astype(WIRE) for n in names]
        lands = [lax.empty((N_DEV,) + s.shape, WIRE) for s in shards]
        send, recv, shards, lands, token = _split_start(shards, lands, ["all"] * len(names), 0, after, f"gather_start_{l}{tag}")
        return dict(names=names, send=send, recv=recv, shards=shards, lands=lands, token=token, name=f"gather_wait_{l}{tag}")

    def gather_wait(f, after):
        shards, lands = _split_wait(f["send"], f["recv"], f["shards"], f["lands"], ["all"] * len(f["names"]), 0, after, f["name"])
        return dict(zip(f["names"], [put(ld, sh, (me,)) for sh, ld in zip(shards, lands)]))

    first = gather_start(0, EARLY, nothing, "a")
    flying = {0: (first, gather_start(0, LATE, first["token"], "b"))}

    def weights_of(l, h):
        fa, fb = flying.pop(l)
        got = gather_wait(fa, h)
        if fb is None:
            token = None
            if l + 1 < DEPTH:
                flying[l + 1] = (gather_start(l + 1, SHARDED, got["w_in"], ""), None)
                token = flying[l + 1][0]["token"]
            W = _layer_weights(got)
            return {n: W[n] for n in EARLY}, (lambda after: ({n: W[n] for n in LATE}, None)), token

        def late_weights(after):
            late = gather_wait(fb, after)
            flying[l + 1] = (gather_start(l + 1, SHARDED, late["w_out"], ""), None)
            return _layer_weights(late), flying[l + 1][0]["token"]

        return _layer_weights(got), late_weights, fb["token"]

    grad_lands = {n: lax.empty((N_DEV,) + P[n].shape, WIRE) for n in SHARDED}
    sent = []

    def scatter_start(l, names, parts, lands, tag):
        send, recv, parts, lands, token = _split_start(parts, lands, ["own"] * len(parts), l, nothing, f"scatter_start_{l}{tag}")
        sent.append(dict(l=l, names=names, send=send, recv=recv, parts=parts, lands=lands, name=f"scatter_wait_{l}{tag}"))
        return token

    def scatter_wait(after):
        f = sent.pop(0)
        parts, lands = _split_wait(f["send"], f["recv"], f["parts"], f["lands"], ["own"] * len(f["parts"]), f["l"], after, f["name"])
        lands = [put(ld, lax.dynamic_index_in_dim(part, me, 0, keepdims=False), (me, f["l"])) for part, ld in zip(parts, lands)]
        return dict(zip(f["names"], lands))

    def mid_bwd(l, g1, G):
        if l > 0:
            return None
        grad_lands.update(scatter_wait(g1))
        return scatter_start(0, GRADS_MID, [G[n] for n in GRADS_MID], [grad_lands[n] for n in GRADS_MID], "a")

    small_land = []

    def after_bwd(l, g, G):
        if l > 0:
            if sent:
                grad_lands.update(scatter_wait(g))
            return scatter_start(l, SHARDED, [G[n] for n in SHARDED], [grad_lands[n] for n in SHARDED], "")
        sflat = _pack([jnp.stack([grads_of[k][n] for k in range(DEPTH)]) for n in SMALL], F32)
        sparts = sflat.reshape(N_DEV, -1, FLAT_COLS)
        small_land.append(sflat.shape)
        return scatter_start(0, GRADS_END + ("small",), [G[n] for n in GRADS_END] + [sparts],
                             [grad_lands[n] for n in GRADS_END] + [lax.empty((N_DEV, 1) + sparts.shape[1:], F32)], "b")

    grads_of = {}

    def after_bwd_recording(l, g, G):
        grads_of[l] = G
        grads_of["token"] = after_bwd(l, g, G)
        return grads_of["token"]

    lsum, gx, grads = _local_step(x[0], p[:, 0], positions[0], loss_target[0], P, weights_of, mid_bwd, after_bwd_recording)
    G, delta, new_m, new_v = {}, {}, {}, {}

    def update(names):
        for n in names:
            shp = P[n].shape
            res = _sum_adamw(grad_lands[n].reshape(N_DEV, -1, shp[-1]), *(a.reshape(-1, shp[-1]) for a in (P[n], M[n], V[n])))
            G[n], delta[n], new_m[n], new_v[n] = (a.reshape(shp) for a in res)

    grad_lands.update(scatter_wait(grads_of["token"]))
    update(GRADS_MID)
    last = scatter_wait(sum(new_v[n].reshape(-1, LANES)[:8] for n in GRADS_MID))
    small_parts = last.pop("small")
    grad_lands.update(last)
    update(GRADS_END)
    small_sum = _gather_whole(_sum_slots(small_parts[:, 0])).reshape((1,) + small_land[0])
    res = _sum_adamw(small_sum, _pack([P[n] for n in SMALL], F32), _pack([M[n] for n in SMALL], F32), _pack([V[n] for n in SMALL], F32))
    for dst, flat in zip((G, delta, new_m, new_v), res):
        dst.update(zip(SMALL, _unpack(flat, small_shapes)))
    loss = lax.psum(lsum, ("x", "y", "c"))
    return (loss, gx[None], *[G[n] for n in WEIGHTS], *[delta[n] for n in WEIGHTS], *[new_m[n] for n in WEIGHTS], *[new_v[n] for n in WEIGHTS])
```

```python
import functools
import math

import jax
import jax.numpy as jnp
from jax import lax
from jax.experimental import pallas as pl
from jax.experimental.pallas import tpu as pltpu

F32 = jnp.float32
MXU = jnp.bfloat16
WIRE = jnp.bfloat16

D_MODEL = 1024
DEPTH = 4
HEAD_DIM = 64
A_WIDTH = 256
A_HEADS = 4
CHUNK = 128
B_WIDTH = 512
WINDOW = 128
C_WIDTH = 256
C_GROUP = 16
C_GROUPS = 16
C_STATE = 64
N_STATE = C_GROUPS * C_STATE
IN_A = 2 * A_WIDTH
KV_WIDTH = 2 * HEAD_DIM
IN_COLS = IN_A + B_WIDTH + 2 * KV_WIDTH + C_WIDTH
Q_BLOCK = IN_A // B_WIDTH
K_BLOCK = (IN_A + B_WIDTH) // KV_WIDTH
V_BLOCK = K_BLOCK + 1
C_BLOCK = (IN_A + B_WIDTH + 2 * KV_WIDTH) // C_WIDTH
TOKEN_TILE = 512
D_FF = 4096
PLE_DIM = 256
EPS = 1e-6
ROPE_THETA = 10000.0
SCALE = HEAD_DIM ** -0.5
NEG = -1e30
N_DEV = 8

ADAM_LR = 0.001
ADAM_B1 = 0.9
ADAM_B2 = 0.999
ADAM_EPS = 1e-08
ADAM_WD = 0.01
ADAM_STEP = 10

V7X_VMEM_BYTES = 64 * 2 ** 20
VMEM_LIMIT = V7X_VMEM_BYTES - 8 * 2 ** 20
LANES = 128

MESH = pl.DeviceIdType.MESH


def _cp(*sem):
    return pltpu.CompilerParams(dimension_semantics=sem, vmem_limit_bytes=VMEM_LIMIT)


def _sds(shape, dtype=F32):
    return jax.ShapeDtypeStruct(shape, dtype)


def _mm(a, b):
    return jnp.dot(a.astype(MXU), b.astype(MXU), preferred_element_type=F32)


def _mm_nt(a, b):
    return lax.dot_general(a.astype(MXU), b.astype(MXU), (((1,), (1,)), ((), ())), preferred_element_type=F32)


def _mm_tn(a, b):
    return lax.dot_general(a.astype(MXU), b.astype(MXU), (((0,), (0,)), ((), ())), preferred_element_type=F32)


def _lane(shape):
    return lax.broadcasted_iota(jnp.int32, shape, len(shape) - 1)


def _row(shape):
    return lax.broadcasted_iota(jnp.int32, shape, 0)


_GELU_C = math.sqrt(2.0 / math.pi)


def _gelu(x):
    return 0.5 * x * (1.0 + jnp.tanh(_GELU_C * (x + 0.044715 * (x * x * x))))


def _gelu_grad(x):
    t = jnp.tanh(_GELU_C * (x + 0.044715 * (x * x * x)))
    return 0.5 * (1.0 + t) + 0.5 * x * (1.0 - t * t) * (_GELU_C * (1.0 + 3.0 * 0.044715 * (x * x)))


def _sigmoid(x):
    return 1.0 / (1.0 + jnp.exp(-x))


def _rms_stat(x):
    return lax.rsqrt(jnp.mean(x * x, axis=-1, keepdims=True) + EPS)


def _rms_bwd(x, r, g, dy):
    xh = x * r
    dxh = dy * g
    dx = r * (dxh - xh * jnp.mean(dxh * xh, axis=-1, keepdims=True))
    return dx, jnp.sum(dy * xh, axis=0, keepdims=True)


def _tril(w):
    return jnp.where(_row(w.shape) >= _lane(w.shape), w, 0.0)


def _swap64(x):
    return pltpu.roll(x, HEAD_DIM, 1)


def _group_sum64(x, lo):
    s_lo = jnp.sum(jnp.where(lo, x, 0.0), axis=-1, keepdims=True)
    s_hi = jnp.sum(jnp.where(lo, 0.0, x), axis=-1, keepdims=True)
    return jnp.where(lo, s_lo, s_hi)


def _partner(x):
    n = x.shape[-1]
    first = (_lane(x.shape) % HEAD_DIM) < HEAD_DIM // 2
    return jnp.where(first, pltpu.roll(x, n - HEAD_DIM // 2, 1), pltpu.roll(x, HEAD_DIM // 2, 1))


def _rope(y, cs, sn):
    return y * cs + _partner(y) * sn


def _rope_bwd(d, cs, sn):
    return d * cs + _partner(d * sn)


def _qk_norm_rope(x, g, cs, sn):
    lo = _lane(x.shape) < HEAD_DIM
    r = lax.rsqrt(_group_sum64(x * x, lo) * (1.0 / HEAD_DIM) + EPS)
    xh = x * r
    return _rope(xh * g, cs, sn), xh, r


def _qk_norm_rope_bwd(xh, r, g, cs, sn, d):
    lo = _lane(xh.shape) < HEAD_DIM
    dy = _rope_bwd(d, cs, sn)
    dxh = dy * g
    m = _group_sum64(dxh * xh, lo) * (1.0 / HEAD_DIM)
    return r * (dxh - xh * m), jnp.sum(dy * xh, axis=0, keepdims=True)


def _gmlp_head(blk, g, b):
    hi = _lane(blk.shape) >= HEAD_DIM
    mu = jnp.sum(jnp.where(hi, blk, 0.0), axis=-1, keepdims=True) * (1.0 / HEAD_DIM)
    xc = jnp.where(hi, blk - mu, 0.0)
    rstd = lax.rsqrt(jnp.sum(xc * xc, axis=-1, keepdims=True) * (1.0 / HEAD_DIM) + EPS)
    vhat = xc * rstd
    return vhat * g + b, vhat, rstd


def _rope_tables(pos_col, inv_row):
    T = pos_col.shape[0]
    tm = min(T, 1024)

    def body(p_ref, inv_ref, cs_ref, sn_ref):
        ang = p_ref[...].astype(F32) * inv_ref[...]
        s = jnp.sin(ang)
        cs_ref[...] = jnp.cos(ang)
        sn_ref[...] = jnp.where((_lane(ang.shape) % HEAD_DIM) < HEAD_DIM // 2, -s, s)

    blk = pl.BlockSpec((tm, LANES), lambda i: (i, 0))
    return pl.pallas_call(
        body, name="rope_tables", grid=(T // tm,),
        in_specs=[pl.BlockSpec((tm, 1), lambda i: (i, 0)), pl.BlockSpec((1, LANES), lambda i: (0, 0))],
        out_specs=[blk, blk], out_shape=[_sds((T, LANES))] * 2, compiler_params=_cp("parallel"))(pos_col, inv_row)


def _inproj_fwd(h, g, w):
    T = h.shape[0]
    tm = min(T, TOKEN_TILE)

    def body(h_ref, g_ref, w_ref, z_ref):
        x = h_ref[...]
        z_ref[...] = _mm(x * _rms_stat(x) * g_ref[...], w_ref[...])

    return pl.pallas_call(
        body, name="inproj_fwd", grid=(T // tm,),
        in_specs=[pl.BlockSpec((tm, D_MODEL), lambda i: (i, 0)), pl.BlockSpec((1, D_MODEL), lambda i: (0, 0)),
                  pl.BlockSpec((D_MODEL, IN_COLS), lambda i: (0, 0))],
        out_specs=pl.BlockSpec((tm, IN_COLS), lambda i: (i, 0)), out_shape=_sds((T, IN_COLS)),
        compiler_params=_cp("parallel"))(h, g, w)


def _inproj_bwd(gres, h, g, w, dza, dzq, dzk, dzv, dzc):
    T = h.shape[0]
    tm = min(T, TOKEN_TILE)
    nt = T // tm
    shard = IN_COLS // N_DEV

    def body(gr_ref, h_ref, g_ref, w_ref, a_ref, q_ref, k_ref, v_ref, c_ref, dh_ref, dw_ref, dg_ref, acc):
        i = pl.program_id(0)

        @pl.when(i == 0)
        def _():
            dg_ref[...] = jnp.zeros_like(dg_ref)
            acc[...] = jnp.zeros_like(acc)

        x = h_ref[...]
        r = _rms_stat(x)
        gg = g_ref[...]
        dz = jnp.concatenate([a_ref[...], q_ref[...], k_ref[...], v_ref[...], c_ref[...]], axis=1)
        dxn = _mm_nt(dz, w_ref[...])
        dx, dg = _rms_bwd(x, r, gg, dxn)
        dh_ref[...] = gr_ref[...] + dx
        dg_ref[...] += dg
        acc[...] += _mm_tn(x * r * gg, dz)

        @pl.when(i == nt - 1)
        def _():
            for d in range(N_DEV):
                dw_ref[d] = acc[:, d * shard:(d + 1) * shard].astype(WIRE)

    def rows(w_):
        return pl.BlockSpec((tm, w_), lambda i: (i, 0))

    row = pl.BlockSpec((1, D_MODEL), lambda i: (0, 0))
    return pl.pallas_call(
        body, name="inproj_bwd", grid=(nt,),
        in_specs=[rows(D_MODEL), rows(D_MODEL), row, pl.BlockSpec((D_MODEL, IN_COLS), lambda i: (0, 0)),
                  rows(IN_A), rows(B_WIDTH), rows(KV_WIDTH), rows(KV_WIDTH), rows(C_WIDTH)],
        out_specs=[rows(D_MODEL), pl.BlockSpec((N_DEV, D_MODEL, shard), lambda i: (0, 0, 0)), row],
        out_shape=[_sds((T, D_MODEL)), _sds((N_DEV, D_MODEL, shard), WIRE), _sds((1, D_MODEL))],
        scratch_shapes=[pltpu.VMEM((D_MODEL, IN_COLS), F32)],
        compiler_params=_cp("arbitrary"))(gres, h, g, w, dza, dzq, dzk, dzv, dzc)


def _gmlp_fwd(z, lng, lnb, ws, bsx):
    T = z.shape[0]
    tm = min(T, TOKEN_TILE)
    nc = tm // CHUNK

    def body(z_ref, g_ref, b_ref, w_ref, bs_ref, ya_ref):
        zg = _gelu(z_ref[...])
        lo = _lane((tm, LANES)) < HEAD_DIM
        prods = []
        for hd in range(A_HEADS):
            sl = slice(hd * LANES, (hd + 1) * LANES)
            blk = zg[:, sl]
            vn, _, _ = _gmlp_head(blk, g_ref[:, sl], b_ref[:, sl])
            wm = _tril(w_ref[hd])
            sv = jnp.concatenate([_mm(wm, vn[c * CHUNK:(c + 1) * CHUNK]) + bs_ref[hd] for c in range(nc)], axis=0)
            prods.append(blk * _swap64(sv))
        ya_ref[:, 0:LANES] = jnp.where(lo, prods[0], _swap64(prods[1]))
        ya_ref[:, LANES:2 * LANES] = jnp.where(lo, prods[2], _swap64(prods[3]))

    row = pl.BlockSpec((1, IN_A), lambda i: (0, 0))
    mat = pl.BlockSpec((A_HEADS, CHUNK, CHUNK), lambda i: (0, 0, 0))
    return pl.pallas_call(
        body, name="gmlp_fwd", grid=(T // tm,),
        in_specs=[pl.BlockSpec((tm, IN_A), lambda i: (i, 0)), row, row, mat, mat],
        out_specs=pl.BlockSpec((tm, A_WIDTH), lambda i: (i, 0)), out_shape=_sds((T, A_WIDTH)),
        compiler_params=_cp("parallel"))(z, lng, lnb, ws, bsx)


def _gmlp_bwd(z, dya, lng, lnb, ws, bsx):
    T = z.shape[0]
    tm = min(T, TOKEN_TILE)
    nc = tm // CHUNK

    def body(z_ref, dya_ref, g_ref, b_ref, w_ref, bs_ref, dza_ref, dw_ref, dbs_ref, dg_ref, db_ref):
        @pl.when(pl.program_id(0) == 0)
        def _():
            dw_ref[...] = jnp.zeros_like(dw_ref)
            dbs_ref[...] = jnp.zeros_like(dbs_ref)
            dg_ref[...] = jnp.zeros_like(dg_ref)
            db_ref[...] = jnp.zeros_like(db_ref)

        za = z_ref[...]
        zg = _gelu(za)
        gp = _gelu_grad(za)
        lo = _lane((tm, LANES)) < HEAD_DIM
        for hd in range(A_HEADS):
            sl = slice(hd * LANES, (hd + 1) * LANES)
            blk = zg[:, sl]
            g = g_ref[:, sl]
            vn, vhat, rstd = _gmlp_head(blk, g, b_ref[:, sl])
            wm = _tril(w_ref[hd])
            pair = dya_ref[:, (hd // 2) * LANES:(hd // 2 + 1) * LANES]
            dy = jnp.where(lo, pair if hd % 2 == 0 else _swap64(pair), 0.0)
            dsv = _swap64(dy * blk)
            svs, dvns = [], []
            dw = jnp.zeros((CHUNK, CHUNK), F32)
            dbs = jnp.zeros((CHUNK, 1), F32)
            for c in range(nc):
                cs = slice(c * CHUNK, (c + 1) * CHUNK)
                svs.append(_mm(wm, vn[cs]) + bs_ref[hd])
                dw = dw + _mm_nt(dsv[cs], vn[cs])
                dbs = dbs + jnp.sum(dsv[cs], axis=-1, keepdims=True)
                dvns.append(_mm_tn(wm, dsv[cs]))
            sv = jnp.concatenate(svs, axis=0)
            dvn = jnp.concatenate(dvns, axis=0)
            dw_ref[hd] += _tril(dw)
            dbs_ref[hd] += jnp.broadcast_to(dbs, (CHUNK, CHUNK))
            dg_ref[:, sl] += jnp.sum(dvn * vhat, axis=0, keepdims=True)
            db_ref[:, sl] += jnp.sum(dvn, axis=0, keepdims=True)
            du = dy * _swap64(sv)
            dvh = dvn * g
            m1 = jnp.sum(dvh, axis=-1, keepdims=True) * (1.0 / HEAD_DIM)
            m2 = jnp.sum(dvh * vhat, axis=-1, keepdims=True) * (1.0 / HEAD_DIM)
            dv = jnp.where(lo, 0.0, rstd * (dvh - m1 - vhat * m2))
            dza_ref[:, sl] = (du + dv) * gp[:, sl]

    row = pl.BlockSpec((1, IN_A), lambda i: (0, 0))
    mat = pl.BlockSpec((A_HEADS, CHUNK, CHUNK), lambda i: (0, 0, 0))
    return pl.pallas_call(
        body, name="gmlp_bwd", grid=(T // tm,),
        in_specs=[pl.BlockSpec((tm, IN_A), lambda i: (i, 0)), pl.BlockSpec((tm, A_WIDTH), lambda i: (i, 0)), row, row, mat, mat],
        out_specs=[pl.BlockSpec((tm, IN_A), lambda i: (i, 0)), mat, mat, row, row],
        out_shape=[_sds((T, IN_A)), _sds((A_HEADS, CHUNK, CHUNK)), _sds((A_HEADS, CHUNK, CHUNK)), _sds((1, IN_A)), _sds((1, IN_A))],
        compiler_params=_cp("arbitrary"))(z, dya, lng, lnb, ws, bsx)


def _attn_specs(T, tq, tile_of):
    nb = tq // WINDOW

    def prev(i):
        return jnp.maximum(tile_of(i) * nb - 1, 0)

    row = pl.BlockSpec((1, LANES), lambda i: (0, 0))
    return [
        pl.BlockSpec((tq, B_WIDTH), lambda i: (tile_of(i), Q_BLOCK)),
        pl.BlockSpec((tq, LANES), lambda i: (tile_of(i), K_BLOCK)),
        pl.BlockSpec((tq, LANES), lambda i: (tile_of(i), V_BLOCK)),
        pl.BlockSpec((WINDOW, LANES), lambda i: (prev(i), K_BLOCK)),
        pl.BlockSpec((WINDOW, LANES), lambda i: (prev(i), V_BLOCK)),
        pl.BlockSpec((tq, LANES), lambda i: (tile_of(i), 0)),
        pl.BlockSpec((tq, LANES), lambda i: (tile_of(i), 0)),
        pl.BlockSpec((WINDOW, LANES), lambda i: (prev(i), 0)),
        pl.BlockSpec((WINDOW, LANES), lambda i: (prev(i), 0)),
        row, row,
        pl.BlockSpec((8, LANES), lambda i: (0, 0)),
    ]


def _attn_bias(first):
    qi = lax.broadcasted_iota(jnp.int32, (WINDOW, 2 * WINDOW), 0)
    kj = lax.broadcasted_iota(jnp.int32, (WINDOW, 2 * WINDOW), 1)
    diff = qi + WINDOW - kj
    ok = (diff >= 0) & (diff < WINDOW) & ((kj >= WINDOW) | jnp.logical_not(first))
    return jnp.where(ok, 0.0, NEG)


def _dup_heads(x, lo):
    sw = _swap64(x)
    return jnp.where(lo, x, sw), jnp.where(lo, sw, x)


HEADS_PER_KV = 4


def _stack_heads(x0, x1, lo):
    return jnp.concatenate([jnp.where(lo, x0, 0.0), jnp.where(lo, 0.0, x0), jnp.where(lo, x1, 0.0), jnp.where(lo, 0.0, x1)], axis=0)


def _unstack_heads(x4, lo):
    return (jnp.where(lo, x4[0:WINDOW], x4[WINDOW:2 * WINDOW]), jnp.where(lo, x4[2 * WINDOW:3 * WINDOW], x4[3 * WINDOW:4 * WINDOW]))


def _sink_column(sk_ref, g):
    return jnp.concatenate([jnp.broadcast_to(sk_ref[a:a + 1, 0:1], (WINDOW, 1)) for a in range(HEADS_PER_KV * g, HEADS_PER_KV * (g + 1))], axis=0)


def _attn_probs(q4, kw, bias, sink):
    s = _mm_nt(q4, kw)
    s = (s.reshape(HEADS_PER_KV, WINDOW, 2 * WINDOW) + bias[None]).reshape(HEADS_PER_KV * WINDOW, 2 * WINDOW)
    m = jnp.maximum(jnp.max(s, axis=-1, keepdims=True), sink)
    p = jnp.exp(s - m)
    es = jnp.exp(sink - m)
    inv = 1.0 / (jnp.sum(p, axis=-1, keepdims=True) + es)
    return p * inv, es * inv


def _attn_fwd(z, cs, sn, qg, kg, sinks):
    T = z.shape[0]
    tq = min(T, 2 * TOKEN_TILE)
    nb = tq // WINDOW

    def body(q_ref, k_ref, v_ref, kp_ref, vp_ref, cs_ref, sn_ref, csp_ref, snp_ref, qg_ref, kg_ref, sk_ref, o_ref):
        i = pl.program_id(0)
        csq, snq = cs_ref[...], sn_ref[...]
        cs_all = jnp.concatenate([csp_ref[...], csq], axis=0)
        sn_all = jnp.concatenate([snp_ref[...], snq], axis=0)
        k_all = jnp.concatenate([kp_ref[...], k_ref[...]], axis=0)
        v_all = jnp.concatenate([vp_ref[...], v_ref[...]], axis=0)
        kr, _, _ = _qk_norm_rope(k_all, kg_ref[...], cs_all, sn_all)
        lo_all = _lane(k_all.shape) < HEAD_DIM
        kd = _dup_heads(kr, lo_all)
        vd = _dup_heads(v_all, lo_all)
        lo = _lane((WINDOW, LANES)) < HEAD_DIM
        qrs = [_qk_norm_rope(q_ref[:, pr * LANES:(pr + 1) * LANES], qg_ref[...], csq, snq)[0] * SCALE for pr in range(4)]
        biases = [_attn_bias(i * nb + b == 0) for b in range(nb)]
        for g in range(2):
            sink = _sink_column(sk_ref, g)
            for b in range(nb):
                bs = slice(b * WINDOW, (b + 1) * WINDOW)
                ws = slice(b * WINDOW, (b + 2) * WINDOW)
                pn, _ = _attn_probs(_stack_heads(qrs[2 * g][bs], qrs[2 * g + 1][bs], lo), kd[g][ws], biases[b], sink)
                o0, o1 = _unstack_heads(_mm(pn, vd[g][ws]), lo)
                o_ref[bs, 2 * g * LANES:(2 * g + 1) * LANES] = o0
                o_ref[bs, (2 * g + 1) * LANES:(2 * g + 2) * LANES] = o1

    return pl.pallas_call(
        body, name="attn_fwd", grid=(T // tq,),
        in_specs=_attn_specs(T, tq, lambda i: i),
        out_specs=pl.BlockSpec((tq, B_WIDTH), lambda i: (i, 0)), out_shape=_sds((T, B_WIDTH)),
        compiler_params=_cp("parallel"))(z, z, z, z, z, cs, sn, cs, sn, qg, kg, sinks)


def _attn_bwd(z, cs, sn, qg, kg, sinks, o, do):
    T = z.shape[0]
    tq = min(T, 2 * TOKEN_TILE)
    nb = tq // WINDOW
    nt = T // tq
    tk = tq + WINDOW

    def tile_of(i):
        return nt - 1 - i

    def body(q_ref, k_ref, v_ref, kp_ref, vp_ref, cs_ref, sn_ref, csp_ref, snp_ref, qg_ref, kg_ref, sk_ref, o_ref, do_ref,
             dq_ref, dk_ref, dv_ref, dqg_ref, dkg_ref, dsk_ref, acck, accv, ck, cv):
        i = pl.program_id(0)
        ti = nt - 1 - i

        @pl.when(i == 0)
        def _():
            dqg_ref[...] = jnp.zeros_like(dqg_ref)
            dkg_ref[...] = jnp.zeros_like(dkg_ref)
            dsk_ref[...] = jnp.zeros_like(dsk_ref)
            ck[...] = jnp.zeros_like(ck)
            cv[...] = jnp.zeros_like(cv)

        csq, snq = cs_ref[...], sn_ref[...]
        cs_all = jnp.concatenate([csp_ref[...], csq], axis=0)
        sn_all = jnp.concatenate([snp_ref[...], snq], axis=0)
        k_all = jnp.concatenate([kp_ref[...], k_ref[...]], axis=0)
        v_all = jnp.concatenate([vp_ref[...], v_ref[...]], axis=0)
        kr, kh, rk = _qk_norm_rope(k_all, kg_ref[...], cs_all, sn_all)
        lo_all = _lane(k_all.shape) < HEAD_DIM
        kd = _dup_heads(kr, lo_all)
        vd = _dup_heads(v_all, lo_all)
        lo = _lane((WINDOW, LANES)) < HEAD_DIM
        acck[...] = jnp.zeros_like(acck)
        accv[...] = jnp.zeros_like(accv)
        prep = [_qk_norm_rope(q_ref[:, pr * LANES:(pr + 1) * LANES], qg_ref[...], csq, snq) for pr in range(4)]
        biases = [_attn_bias(ti * nb + b == 0) for b in range(nb)]
        dqs = [[None] * nb for _ in range(4)]
        for g in range(2):
            sink = _sink_column(sk_ref, g)
            dsink = jnp.zeros((HEADS_PER_KV * WINDOW, 1), F32)
            for b in range(nb):
                bs = slice(b * WINDOW, (b + 1) * WINDOW)
                ws = slice(b * WINDOW, (b + 2) * WINDOW)
                kw, vw = kd[g][ws], vd[g][ws]
                q4 = _stack_heads(prep[2 * g][0][bs] * SCALE, prep[2 * g + 1][0][bs] * SCALE, lo)
                pn, psink = _attn_probs(q4, kw, biases[b], sink)
                o0, o1 = o_ref[bs, 2 * g * LANES:(2 * g + 1) * LANES], o_ref[bs, (2 * g + 1) * LANES:(2 * g + 2) * LANES]
                do4 = _stack_heads(do_ref[bs, 2 * g * LANES:(2 * g + 1) * LANES], do_ref[bs, (2 * g + 1) * LANES:(2 * g + 2) * LANES], lo)
                delta = jnp.sum(do4 * jnp.concatenate([o0, o0, o1, o1], axis=0), axis=-1, keepdims=True)
                ds = pn * (_mm_nt(do4, vw) - delta)
                dsink = dsink - psink * delta
                dqs[2 * g][b], dqs[2 * g + 1][b] = _unstack_heads(_mm(ds, kw) * SCALE, lo)
                acck[g, ws, :] += _mm_tn(ds, q4)
                accv[g, ws, :] += _mm_tn(pn, do4)
            for hh in range(HEADS_PER_KV):
                a = HEADS_PER_KV * g + hh
                dsk_ref[a:a + 1, :] += jnp.zeros((1, LANES), F32) + jnp.sum(dsink[hh * WINDOW:(hh + 1) * WINDOW])
        for pr in range(4):
            _, qh, rq = prep[pr]
            dx, dg = _qk_norm_rope_bwd(qh, rq, qg_ref[...], csq, snq, jnp.concatenate(dqs[pr], axis=0))
            dq_ref[:, pr * LANES:(pr + 1) * LANES] = dx
            dqg_ref[...] += dg

        def fold(acc):
            f0 = acc[0] + _swap64(acc[0])
            f1 = acc[1] + _swap64(acc[1])
            return jnp.where(lo_all, f0, f1)

        dk_all = fold(acck)
        dv_all = fold(accv)
        pad = jnp.zeros((tq - WINDOW, LANES), F32)
        dk_own = dk_all[WINDOW:] + (jnp.concatenate([pad, ck[...]], axis=0) if nb > 1 else ck[...])
        dv_own = dv_all[WINDOW:] + (jnp.concatenate([pad, cv[...]], axis=0) if nb > 1 else cv[...])
        ck[...] = dk_all[:WINDOW]
        cv[...] = dv_all[:WINDOW]
        dxk, dgk = _qk_norm_rope_bwd(kh[WINDOW:], rk[WINDOW:], kg_ref[...], csq, snq, dk_own)
        dk_ref[...] = dxk
        dkg_ref[...] += dgk
        dv_ref[...] = dv_own

    row = pl.BlockSpec((1, LANES), lambda i: (0, 0))
    return pl.pallas_call(
        body, name="attn_bwd", grid=(nt,),
        in_specs=_attn_specs(T, tq, tile_of) + [pl.BlockSpec((tq, B_WIDTH), lambda i: (tile_of(i), 0))] * 2,
        out_specs=[pl.BlockSpec((tq, B_WIDTH), lambda i: (tile_of(i), 0)), pl.BlockSpec((tq, LANES), lambda i: (tile_of(i), 0)),
                   pl.BlockSpec((tq, LANES), lambda i: (tile_of(i), 0)), row, row, pl.BlockSpec((8, LANES), lambda i: (0, 0))],
        out_shape=[_sds((T, B_WIDTH)), _sds((T, LANES)), _sds((T, LANES)), _sds((1, LANES)), _sds((1, LANES)), _sds((8, LANES))],
        scratch_shapes=[pltpu.VMEM((2, tk, LANES), F32), pltpu.VMEM((2, tk, LANES), F32),
                        pltpu.VMEM((WINDOW, LANES), F32), pltpu.VMEM((WINDOW, LANES), F32)],
        compiler_params=_cp("arbitrary"))(z, z, z, z, z, cs, sn, cs, sn, qg, kg, sinks, o, do)


def _bbar_t(are, aim, ldt, btr, bti):
    lbr, lbi = _lam_bar(are, aim, ldt)
    den = are * are + aim * aim
    nr = lbr - 1.0
    cr = (nr * are + lbi * aim) / den
    ci = (lbi * are - nr * aim) / den
    return cr * btr - ci * bti, cr * bti + ci * btr


def _lam_bar(are, aim, ldt):
    dt = jnp.exp(ldt)
    er = jnp.exp(are * dt)
    return er * jnp.cos(aim * dt), er * jnp.sin(aim * dt)


def _block_diag(x):
    t = jnp.concatenate([x] * C_GROUPS, axis=1)
    return jnp.where(_row(t.shape) // C_GROUP == _lane(t.shape) // C_STATE, t, 0.0)


def _block_diag_fold(m):
    rg = _row((C_WIDTH, C_STATE)) // C_GROUP
    acc = jnp.zeros((C_WIDTH, C_STATE), F32)
    for g in range(C_GROUPS):
        acc = acc + jnp.where(rg == g, m[:, g * C_STATE:(g + 1) * C_STATE], 0.0)
    return acc


def _ssm_prep(are, aim, ldt, are_x, aim_x, ldt_x, btr, bti, cre, cim):
    def body(are_r, aim_r, ldt_r, arex_r, aimx_r, ldtx_r, btr_r, bti_r, cre_r, cim_r, bbd_ref, cbd_ref, pwr_ref, pwi_ref):
        lr, li = _lam_bar(are_r[...], aim_r[...], ldt_r[...])
        cr, ci = lr, li
        for r in range(SCAN_SEG):
            pwr_ref[r:r + 1, :] = cr
            pwi_ref[r:r + 1, :] = ci
            cr, ci = cr * lr - ci * li, cr * li + ci * lr
        br, bi = _bbar_t(arex_r[...], aimx_r[...], ldtx_r[...], btr_r[...], bti_r[...])
        bbd_ref[...] = jnp.concatenate([_block_diag(br), _block_diag(bi)], axis=1).astype(MXU)
        cbd_ref[...] = jnp.concatenate([_block_diag(cre_r[...]), -_block_diag(cim_r[...])], axis=1).astype(MXU)

    return pl.pallas_call(
        body, name="ssm_prep",
        out_shape=[_sds((C_WIDTH, 2 * N_STATE), MXU), _sds((C_WIDTH, 2 * N_STATE), MXU), _sds((SCAN_SEG, N_STATE)), _sds((SCAN_SEG, N_STATE))],
        compiler_params=pltpu.CompilerParams(vmem_limit_bytes=VMEM_LIMIT))(are, aim, ldt, are_x, aim_x, ldt_x, btr, bti, cre, cim)


def _ssm_param_bwd(are, aim, ldt, are_x, aim_x, ldt_x, btr, bti, dlr, dli, dbbd, dcr, dci):
    def body(are_r, aim_r, ldt_r, arex_r, aimx_r, ldtx_r, btr_r, bti_r, dlr_r, dli_r, dbbd_r, dcr_r, dci_r,
             dare_ref, daim_ref, dldt_ref, dbtr_ref, dbti_ref, dcre_ref, dcim_ref):
        _, vjp_l = jax.vjp(_lam_bar, are_r[...], aim_r[...], ldt_r[...])
        da1, di1, dl1 = vjp_l((dlr_r[...], dli_r[...]))
        dbr = _block_diag_fold(dbbd_r[:, 0:N_STATE])
        dbi = _block_diag_fold(dbbd_r[:, N_STATE:2 * N_STATE])
        _, vjp_b = jax.vjp(_bbar_t, arex_r[...], aimx_r[...], ldtx_r[...], btr_r[...], bti_r[...])
        da2, di2, dl2, dbtr, dbti = vjp_b((dbr, dbi))

        def gsum(x):
            return x.reshape(C_GROUPS, C_GROUP, C_STATE).sum(axis=1)

        dare_ref[...] = da1 + gsum(da2)
        daim_ref[...] = di1 + gsum(di2)
        dldt_ref[...] = jnp.broadcast_to(jnp.sum(dl1 + gsum(dl2), axis=-1, keepdims=True), (C_GROUPS, LANES))
        dbtr_ref[...] = dbtr
        dbti_ref[...] = dbti
        dcre_ref[...] = _block_diag_fold(dcr_r[...])
        dcim_ref[...] = -_block_diag_fold(dci_r[...])

    g = _sds((C_GROUPS, C_STATE))
    x = _sds((C_WIDTH, C_STATE))
    return pl.pallas_call(
        body, name="ssm_param_bwd", out_shape=[g, g, _sds((C_GROUPS, LANES)), x, x, x, x],
        compiler_params=pltpu.CompilerParams(vmem_limit_bytes=VMEM_LIMIT))(are, aim, ldt, are_x, aim_x, ldt_x, btr, bti, dlr, dli, dbbd, dcr, dci)


SCAN_TILE = 512
SCAN_SEG = 8


def _scan_tables(pwr_ref, pwi_ref, conj, reverse):
    row = _row((SCAN_SEG, N_STATE))
    shifts = []
    for k in (1, 2, 4):
        keep = (row < SCAN_SEG - k) if reverse else (row >= k)
        ar = jnp.broadcast_to(pwr_ref[k - 1:k, :], (SCAN_SEG, N_STATE))
        ai = jnp.broadcast_to(pwi_ref[k - 1:k, :], (SCAN_SEG, N_STATE)) * conj
        shifts.append((SCAN_SEG - k if reverse else k, jnp.where(keep, ar, 0.0), jnp.where(keep, ai, 0.0)))
    if reverse:
        pr = jnp.concatenate([pwr_ref[SCAN_SEG - 1 - r:SCAN_SEG - r, :] for r in range(SCAN_SEG)], axis=0)
        pi = jnp.concatenate([pwi_ref[SCAN_SEG - 1 - r:SCAN_SEG - r, :] for r in range(SCAN_SEG)], axis=0) * conj
    else:
        pr, pi = pwr_ref[...], pwi_ref[...] * conj
    return shifts, (pr, pi)


def _tile_scan(xr_ref, xi_ref, pwr_ref, pwi_ref, sr, si, conj, reverse):
    shifts, (pr, pi) = _scan_tables(pwr_ref, pwi_ref, conj, reverse)
    groups = xr_ref.shape[0] // SCAN_SEG
    out_row = 0 if reverse else SCAN_SEG - 1

    def step(k, c):
        cr, ci = c
        g = groups - 1 - k if reverse else k
        rows = pl.ds(pl.multiple_of(g * SCAN_SEG, SCAN_SEG), SCAN_SEG)
        xr, xi = xr_ref[rows, :], xi_ref[rows, :]
        for amount, ar, ai in shifts:
            qr, qi = pltpu.roll(xr, amount, 0), pltpu.roll(xi, amount, 0)
            xr, xi = xr + ar * qr - ai * qi, xi + ar * qi + ai * qr
        xr, xi = xr + pr * cr - pi * ci, xi + pr * ci + pi * cr
        xr_ref[rows, :] = xr
        xi_ref[rows, :] = xi
        return xr[out_row:out_row + 1], xi[out_row:out_row + 1]

    cr, ci = lax.fori_loop(0, groups, step, (sr[...], si[...]), unroll=2)
    sr[...] = cr
    si[...] = ci


def _ssm_fwd(z, bbd, pwr, pwi, cbd, dsk, w1, w2):
    T = z.shape[0]
    tt = min(T, 2 * SCAN_TILE)

    def body(u_ref, bbd_ref, pwr_ref, pwi_ref, cbd_ref, d_ref, w1_ref, w2_ref, yc_ref, xr_ref, xi_ref, sr, si):
        @pl.when(pl.program_id(0) == 0)
        def _():
            sr[...] = jnp.zeros_like(sr)
            si[...] = jnp.zeros_like(si)

        u = u_ref[...]
        bu = _mm(u, bbd_ref[...])
        xr_ref[...] = bu[:, 0:N_STATE]
        xi_ref[...] = bu[:, N_STATE:2 * N_STATE]
        _tile_scan(xr_ref, xi_ref, pwr_ref, pwi_ref, sr, si, 1.0, False)
        x = jnp.concatenate([xr_ref[...], xi_ref[...]], axis=1)
        y2 = _gelu(_mm_nt(x, cbd_ref[...]) + d_ref[...] * u)
        yc_ref[...] = _mm(y2, w1_ref[...]) * _sigmoid(_mm(y2, w2_ref[...]))

    big = pl.BlockSpec((C_WIDTH, 2 * N_STATE), lambda i: (0, 0))
    tab = pl.BlockSpec((SCAN_SEG, N_STATE), lambda i: (0, 0))
    wsp = pl.BlockSpec((C_WIDTH, C_WIDTH), lambda i: (0, 0))
    xs = pl.BlockSpec((tt, N_STATE), lambda i: (i, 0))
    return pl.pallas_call(
        body, name="ssm_fwd", grid=(T // tt,),
        in_specs=[pl.BlockSpec((tt, C_WIDTH), lambda i: (i, C_BLOCK)), big, tab, tab, big, pl.BlockSpec((1, C_WIDTH), lambda i: (0, 0)), wsp, wsp],
        out_specs=[pl.BlockSpec((tt, C_WIDTH), lambda i: (i, 0)), xs, xs],
        out_shape=[_sds((T, C_WIDTH)), _sds((T, N_STATE)), _sds((T, N_STATE))],
        scratch_shapes=[pltpu.VMEM((1, N_STATE), F32)] * 2,
        compiler_params=_cp("arbitrary"))(z, bbd, pwr, pwi, cbd, dsk, w1, w2)


def _ssm_bwd(dyc, z, xr, xi, bbd, pwr, pwi, cbd, dsk, w1, w2):
    T = z.shape[0]
    tt = min(T, SCAN_TILE)
    nt = T // tt

    def tile_of(i):
        return nt - 1 - i

    def body(dyc_ref, u_ref, xr_ref, xi_ref, xpr_ref, xpi_ref, bbd_ref, pwr_ref, pwi_ref, cbd_ref, d_ref, w1_ref, w2_ref,
             du_ref, dw1_ref, dw2_ref, dcr_ref, dci_ref, dbbd_ref, dlr_ref, dli_ref, dd_ref, gr, gi, sr, si):
        i = pl.program_id(0)
        ti = nt - 1 - i

        @pl.when(i == 0)
        def _():
            sr[...] = jnp.zeros_like(sr)
            si[...] = jnp.zeros_like(si)
            for acc in (dw1_ref, dw2_ref, dcr_ref, dci_ref, dbbd_ref, dlr_ref, dli_ref, dd_ref):
                acc[...] = jnp.zeros_like(acc)

        u = u_ref[...]
        xr_t, xi_t = xr_ref[...], xi_ref[...]
        y = _mm_nt(jnp.concatenate([xr_t, xi_t], axis=1), cbd_ref[...]) + d_ref[...] * u
        y2 = _gelu(y)
        a1 = _mm(y2, w1_ref[...])
        sg = _sigmoid(_mm(y2, w2_ref[...]))
        dyc_t = dyc_ref[...]
        da1 = dyc_t * sg
        da2 = dyc_t * a1 * sg * (1.0 - sg)
        dy = (_mm_nt(da1, w1_ref[...]) + _mm_nt(da2, w2_ref[...])) * _gelu_grad(y)
        gx = _mm(dy, cbd_ref[...])
        gr[...] = gx[:, 0:N_STATE]
        gi[...] = gx[:, N_STATE:2 * N_STATE]
        _tile_scan(gr, gi, pwr_ref, pwi_ref, sr, si, -1.0, True)
        ar, ai = gr[...], gi[...]
        first_row = _row(ar.shape) == 0
        live = jnp.where(ti > 0, 1.0, 0.0)
        xsr = jnp.where(first_row, xpr_ref[7:8, :] * live, pltpu.roll(xr_t, 1, 0))
        xsi = jnp.where(first_row, xpi_ref[7:8, :] * live, pltpu.roll(xi_t, 1, 0))
        dlr_ref[...] += jnp.sum(ar * xsr + ai * xsi, axis=0, keepdims=True)
        dli_ref[...] += jnp.sum(ai * xsr - ar * xsi, axis=0, keepdims=True)
        dd_ref[...] += jnp.sum(dy * u, axis=0, keepdims=True)
        arai = jnp.concatenate([ar, ai], axis=1)
        du_ref[...] = _mm_nt(arai, bbd_ref[...]) + d_ref[...] * dy
        dw1_ref[...] += _mm_tn(y2, da1)
        dw2_ref[...] += _mm_tn(y2, da2)
        dcr_ref[...] += _mm_tn(dy, xr_t)
        dci_ref[...] += _mm_tn(dy, xi_t)
        dbbd_ref[...] += _mm_tn(u, arai)

    def prev(i):
        return jnp.maximum(tile_of(i) * (tt // 8) - 1, 0)

    big = pl.BlockSpec((C_WIDTH, 2 * N_STATE), lambda i: (0, 0))
    tab = pl.BlockSpec((SCAN_SEG, N_STATE), lambda i: (0, 0))
    srow = pl.BlockSpec((1, N_STATE), lambda i: (0, 0))
    wsp = pl.BlockSpec((C_WIDTH, C_WIDTH), lambda i: (0, 0))
    xs = pl.BlockSpec((tt, N_STATE), lambda i: (tile_of(i), 0))
    xp = pl.BlockSpec((8, N_STATE), lambda i: (prev(i), 0))
    cw = pl.BlockSpec((tt, C_WIDTH), lambda i: (tile_of(i), 0))
    drow = pl.BlockSpec((1, C_WIDTH), lambda i: (0, 0))
    return pl.pallas_call(
        body, name="ssm_bwd", grid=(nt,),
        in_specs=[cw, pl.BlockSpec((tt, C_WIDTH), lambda i: (tile_of(i), C_BLOCK)), xs, xs, xp, xp, big, tab, tab, big, drow, wsp, wsp],
        out_specs=[cw, wsp, wsp, pl.BlockSpec((C_WIDTH, N_STATE), lambda i: (0, 0)), pl.BlockSpec((C_WIDTH, N_STATE), lambda i: (0, 0)), big,
                   srow, srow, drow],
        out_shape=[_sds((T, C_WIDTH)), _sds((C_WIDTH, C_WIDTH)), _sds((C_WIDTH, C_WIDTH)), _sds((C_WIDTH, N_STATE)), _sds((C_WIDTH, N_STATE)),
                   _sds((C_WIDTH, 2 * N_STATE)), _sds((1, N_STATE)), _sds((1, N_STATE)), _sds((1, C_WIDTH))],
        scratch_shapes=[pltpu.VMEM((tt, N_STATE), F32)] * 2 + [pltpu.VMEM((1, N_STATE), F32)] * 2,
        compiler_params=_cp("arbitrary"))(dyc, z, xr, xi, xr, xi, bbd, pwr, pwi, cbd, dsk, w1, w2)


_GROUPS = ((0, A_WIDTH), (A_WIDTH, A_WIDTH + B_WIDTH), (A_WIDTH + B_WIDTH, D_MODEL))


def _merge_fwd(h, ya, yb, yc, g, w):
    T = h.shape[0]
    tm = min(T, TOKEN_TILE)

    def body(h_ref, a_ref, b_ref, c_ref, g_ref, w_ref, o_ref):
        yn = jnp.concatenate([y * _rms_stat(y) for y in (a_ref[...], b_ref[...], c_ref[...])], axis=1) * g_ref[...]
        o_ref[...] = h_ref[...] + _mm(yn, w_ref[...])

    def rows(w_):
        return pl.BlockSpec((tm, w_), lambda i: (i, 0))

    return pl.pallas_call(
        body, name="merge_fwd", grid=(T // tm,),
        in_specs=[rows(D_MODEL), rows(A_WIDTH), rows(B_WIDTH), rows(C_WIDTH), pl.BlockSpec((1, D_MODEL), lambda i: (0, 0)),
                  pl.BlockSpec((D_MODEL, D_MODEL), lambda i: (0, 0))],
        out_specs=rows(D_MODEL), out_shape=_sds((T, D_MODEL)), compiler_params=_cp("parallel"))(h, ya, yb, yc, g, w)


def _merge_bwd(gres, ya, yb, yc, g, w):
    T = gres.shape[0]
    tm = min(T, TOKEN_TILE)
    nt = T // tm

    def body(gr_ref, a_ref, b_ref, c_ref, g_ref, w_ref, da_ref, db_ref, dc_ref, dw_ref, dg_ref, acc):
        i = pl.program_id(0)

        @pl.when(i == 0)
        def _():
            dg_ref[...] = jnp.zeros_like(dg_ref)
            acc[...] = jnp.zeros_like(acc)

        gr = gr_ref[...]
        dyn = _mm_nt(gr, w_ref[...])
        yns, dgs = [], []
        for (c0, c1), y_ref, d_ref in zip(_GROUPS, (a_ref, b_ref, c_ref), (da_ref, db_ref, dc_ref)):
            y = y_ref[...]
            r = _rms_stat(y)
            gg = g_ref[:, c0:c1]
            dx, dg = _rms_bwd(y, r, gg, dyn[:, c0:c1])
            d_ref[...] = dx
            dgs.append(dg)
            yns.append(y * r * gg)
        dg_ref[...] += jnp.concatenate(dgs, axis=1)
        acc[...] += _mm_tn(jnp.concatenate(yns, axis=1), gr)

        @pl.when(i == nt - 1)
        def _():
            dw_ref[...] = acc[...].astype(WIRE)

    def rows(w_):
        return pl.BlockSpec((tm, w_), lambda i: (i, 0))

    row = pl.BlockSpec((1, D_MODEL), lambda i: (0, 0))
    full = pl.BlockSpec((D_MODEL, D_MODEL), lambda i: (0, 0))
    return pl.pallas_call(
        body, name="merge_bwd", grid=(nt,),
        in_specs=[rows(D_MODEL), rows(A_WIDTH), rows(B_WIDTH), rows(C_WIDTH), row, full],
        out_specs=[rows(A_WIDTH), rows(B_WIDTH), rows(C_WIDTH), full, row],
        out_shape=[_sds((T, A_WIDTH)), _sds((T, B_WIDTH)), _sds((T, C_WIDTH)), _sds((D_MODEL, D_MODEL), WIRE), _sds((1, D_MODEL))],
        scratch_shapes=[pltpu.VMEM((D_MODEL, D_MODEL), F32)],
        compiler_params=_cp("arbitrary"))(gres, ya, yb, yc, g, w)


FF_BLOCK = D_FF // N_DEV


def _load_weights(w1_hbm, w2_hbm, w1, w2, sem):
    @pl.when(pl.program_id(0) == 0)
    def _():
        copies = [pltpu.make_async_copy(w1_hbm.at[j], w1.at[:, pl.ds(j * FF_BLOCK, FF_BLOCK)], sem.at[j]) for j in range(N_DEV)]
        copies.append(pltpu.make_async_copy(w2_hbm, w2, sem.at[N_DEV]))
        for cp in copies:
            cp.start()
        for cp in copies:
            cp.wait()


def _mlp_weight_scratch():
    return [pltpu.VMEM((D_MODEL, D_FF), MXU), pltpu.VMEM((D_FF, D_MODEL), MXU), pltpu.SemaphoreType.DMA((N_DEV + 1,))]


def _mlp_fwd(h, g, w1, w2):
    T = h.shape[0]
    tm = min(T, TOKEN_TILE)

    def body(h_ref, g_ref, w1_hbm, w2_hbm, o_ref, r_ref, w1_v, w2_v, sem):
        _load_weights(w1_hbm, w2_hbm, w1_v, w2_v, sem)
        x = h_ref[...]
        a = jnp.maximum(_mm(x * _rms_stat(x) * g_ref[...], w1_v[...]), 0.0)
        r = (a * a).astype(MXU)
        r_ref[...] = r
        o_ref[...] = x + _mm(r, w2_v[...])

    rows = pl.BlockSpec((tm, D_MODEL), lambda i: (i, 0))
    hbm = pl.BlockSpec(memory_space=pl.ANY)
    return pl.pallas_call(
        body, name="mlp_fwd", grid=(T // tm,),
        in_specs=[rows, pl.BlockSpec((1, D_MODEL), lambda i: (0, 0)), hbm, hbm],
        out_specs=[rows, pl.BlockSpec((tm, D_FF), lambda i: (i, 0))],
        out_shape=[_sds((T, D_MODEL)), _sds((T, D_FF), MXU)],
        scratch_shapes=_mlp_weight_scratch(), compiler_params=_cp("arbitrary"))(h, g, w1, w2)


def _mlp_bwd(gres, h, g, r, w1, w2):
    T = h.shape[0]
    tm = min(T, 256)

    def body(gr_ref, h_ref, g_ref, r_ref, w1_hbm, w2_hbm, dh_ref, hn_ref, da_ref, dg_ref, w1_v, w2_v, sem):
        _load_weights(w1_hbm, w2_hbm, w1_v, w2_v, sem)

        @pl.when(pl.program_id(0) == 0)
        def _():
            dg_ref[...] = jnp.zeros_like(dg_ref)

        gr = gr_ref[...]
        da = (_mm_nt(gr, w2_v[...]) * (2.0 * jnp.sqrt(r_ref[...].astype(F32)))).astype(MXU)
        da_ref[...] = da
        x = h_ref[...]
        rs = _rms_stat(x)
        gg = g_ref[...]
        dx, dg = _rms_bwd(x, rs, gg, _mm_nt(da, w1_v[...]))
        dh_ref[...] = gr + dx
        dg_ref[...] += dg
        hn_ref[...] = (x * rs * gg).astype(MXU)

    rows = pl.BlockSpec((tm, D_MODEL), lambda i: (i, 0))
    row = pl.BlockSpec((1, D_MODEL), lambda i: (0, 0))
    wide = pl.BlockSpec((tm, D_FF), lambda i: (i, 0))
    hbm = pl.BlockSpec(memory_space=pl.ANY)
    return pl.pallas_call(
        body, name="mlp_bwd", grid=(T // tm,),
        in_specs=[rows, rows, row, wide, hbm, hbm], out_specs=[rows, rows, wide, row],
        out_shape=[_sds((T, D_MODEL)), _sds((T, D_MODEL), MXU), _sds((T, D_FF), MXU), _sds((1, D_MODEL))],
        scratch_shapes=_mlp_weight_scratch(), compiler_params=_cp("arbitrary"))(gres, h, g, r, w1, w2)


def _ple_fwd(h, p, l, g, wg, wp):
    T = h.shape[0]
    tm = min(T, TOKEN_TILE)

    def body(h_ref, p_ref, g_ref, wg_ref, wp_ref, o_ref):
        x = h_ref[...]
        gate = _sigmoid(_mm(x * _rms_stat(x) * g_ref[...], wg_ref[...]))
        o_ref[...] = x + gate * _mm(p_ref[...], wp_ref[...])

    rows = pl.BlockSpec((tm, D_MODEL), lambda i: (i, 0))
    return pl.pallas_call(
        body, name="ple_fwd", grid=(T // tm,),
        in_specs=[rows, pl.BlockSpec((None, tm, PLE_DIM), lambda i: (l, i, 0)), pl.BlockSpec((1, D_MODEL), lambda i: (0, 0)),
                  pl.BlockSpec((D_MODEL, D_MODEL), lambda i: (0, 0)), pl.BlockSpec((PLE_DIM, D_MODEL), lambda i: (0, 0))],
        out_specs=rows, out_shape=_sds((T, D_MODEL)), compiler_params=_cp("parallel"))(h, p, g, wg, wp)


def _ple_bwd(gres, h, p, l, g, wg, wp):
    T = h.shape[0]
    tm = min(T, TOKEN_TILE)
    nt = T // tm
    shard = D_MODEL // N_DEV

    def body(gr_ref, h_ref, p_ref, g_ref, wg_ref, wp_ref, dh_ref, dwg_ref, dwp_ref, dg_ref, accg, accp):
        i = pl.program_id(0)

        @pl.when(i == 0)
        def _():
            dg_ref[...] = jnp.zeros_like(dg_ref)
            accg[...] = jnp.zeros_like(accg)
            accp[...] = jnp.zeros_like(accp)

        x = h_ref[...]
        r = _rms_stat(x)
        gg = g_ref[...]
        hn = x * r * gg
        gate = _sigmoid(_mm(hn, wg_ref[...]))
        pe = p_ref[...]
        e = _mm(pe, wp_ref[...])
        gr = gr_ref[...]
        dgp = gr * e * gate * (1.0 - gate)
        dx, dg = _rms_bwd(x, r, gg, _mm_nt(dgp, wg_ref[...]))
        dh_ref[...] = gr + dx
        dg_ref[...] += dg
        accg[...] += _mm_tn(hn, dgp)
        accp[...] += _mm_tn(pe, gr * gate)

        @pl.when(i == nt - 1)
        def _():
            dwg_ref[...] = accg[...].astype(WIRE)
            for d in range(N_DEV):
                dwp_ref[d] = accp[:, d * shard:(d + 1) * shard].astype(WIRE)

    rows = pl.BlockSpec((tm, D_MODEL), lambda i: (i, 0))
    row = pl.BlockSpec((1, D_MODEL), lambda i: (0, 0))
    full = pl.BlockSpec((D_MODEL, D_MODEL), lambda i: (0, 0))
    return pl.pallas_call(
        body, name="ple_bwd", grid=(nt,),
        in_specs=[rows, rows, pl.BlockSpec((None, tm, PLE_DIM), lambda i: (l, i, 0)), row, full, pl.BlockSpec((PLE_DIM, D_MODEL), lambda i: (0, 0))],
        out_specs=[rows, full, pl.BlockSpec((N_DEV, PLE_DIM, shard), lambda i: (0, 0, 0)), row],
        out_shape=[_sds((T, D_MODEL)), _sds((D_MODEL, D_MODEL), WIRE), _sds((N_DEV, PLE_DIM, shard), WIRE), _sds((1, D_MODEL))],
        scratch_shapes=[pltpu.VMEM((D_MODEL, D_MODEL), F32), pltpu.VMEM((PLE_DIM, D_MODEL), F32)],
        compiler_params=_cp("arbitrary"))(gres, h, p, g, wg, wp)


def _loss_head(h, target):
    T = h.shape[0]
    tm = min(T, 1024)

    def body(h_ref, t_ref, dh_ref, l_ref):
        @pl.when(pl.program_id(0) == 0)
        def _():
            l_ref[...] = jnp.zeros_like(l_ref)

        e = h_ref[...] - t_ref[...]
        dh_ref[...] = e * (1.0 / D_MODEL)
        l_ref[...] += jnp.zeros_like(l_ref) + 0.5 * jnp.sum(jnp.mean(e * e, axis=-1, keepdims=True))

    rows = pl.BlockSpec((tm, D_MODEL), lambda i: (i, 0))
    return pl.pallas_call(
        body, name="loss_head", grid=(T // tm,), in_specs=[rows, rows],
        out_specs=[rows, pl.BlockSpec((8, LANES), lambda i: (0, 0))], out_shape=[_sds((T, D_MODEL)), _sds((8, LANES))],
        compiler_params=_cp("arbitrary"))(h, target)


TN_ROWS = 2048

def _tn(a, b, m, n, *, bm, bn, n_major=False, dtype=F32, name="tn"):
    T = a.shape[0]
    tk = min(T, TN_ROWS)
    nk = T // tk
    assert m % bm == 0 and n % bn == 0 and (not n_major or bm == m)

    def body(a_ref, b_ref, o_ref, acc):
        k = pl.program_id(2)

        @pl.when(k == 0)
        def _():
            acc[...] = jnp.zeros_like(acc)

        acc[...] += _mm_tn(a_ref[...], b_ref[...])

        @pl.when(k == nk - 1)
        def _():
            o_ref[...] = acc[...].astype(dtype)

    if n_major:
        out_spec = pl.BlockSpec((None, bm, bn), lambda i, j, k: (j, 0, 0))
        out_shape = _sds((n // bn, m, bn), dtype)
    else:
        out_spec = pl.BlockSpec((bm, bn), lambda i, j, k: (i, j))
        out_shape = _sds((m, n), dtype)
    return pl.pallas_call(
        body, name=name, grid=(m // bm, n // bn, nk),
        in_specs=[pl.BlockSpec((tk, bm), lambda i, j, k: (k, i)), pl.BlockSpec((tk, bn), lambda i, j, k: (k, j))],
        out_specs=out_spec, out_shape=out_shape, scratch_shapes=[pltpu.VMEM((bm, bn), F32)],
        compiler_params=_cp("parallel", "parallel", "arbitrary"))(a, b)


def _row_tile(R, C):
    for cand in (512, 256, 128, 64, 32, 16, 8):
        if R % cand == 0 and cand * C * 4 <= 2 ** 20:
            return cand
    return R


def _sum_slots(land):
    S, R, C = land.shape
    tr = _row_tile(R, C)

    def body(l_ref, o_ref):
        acc = l_ref[0].astype(F32)
        for s in range(1, S):
            acc = acc + l_ref[s].astype(F32)
        o_ref[...] = acc

    return pl.pallas_call(
        body, name="sum_slots", grid=(R // tr,), in_specs=[pl.BlockSpec((S, tr, C), lambda i: (0, i, 0))],
        out_specs=pl.BlockSpec((tr, C), lambda i: (i, 0)), out_shape=_sds((R, C)), compiler_params=_cp("parallel"))(land)


def _sum_adamw(land, w, m, v):
    R, C = w.shape
    S = land.shape[0]
    tr = _row_tile(R, C)

    def body(l_ref, w_ref, m_ref, v_ref, g_ref, d_ref, nm_ref, nv_ref):
        gg = l_ref[0].astype(F32)
        for s in range(1, S):
            gg = gg + l_ref[s].astype(F32)
        g_ref[...] = gg
        nm = ADAM_B1 * m_ref[...] + (1.0 - ADAM_B1) * gg
        nv = ADAM_B2 * v_ref[...] + (1.0 - ADAM_B2) * (gg * gg)
        m_hat = nm / (1.0 - ADAM_B1 ** ADAM_STEP)
        v_hat = nv / (1.0 - ADAM_B2 ** ADAM_STEP)
        d_ref[...] = -ADAM_LR * (m_hat / (jnp.sqrt(v_hat) + ADAM_EPS) + ADAM_WD * w_ref[...])
        nm_ref[...] = nm
        nv_ref[...] = nv

    blk = pl.BlockSpec((tr, C), lambda i: (i, 0))
    return pl.pallas_call(
        body, name="sum_adamw", grid=(R // tr,), in_specs=[pl.BlockSpec((S, tr, C), lambda i: (0, i, 0))] + [blk] * 3,
        out_specs=[blk] * 4, out_shape=[_sds((R, C))] * 4, compiler_params=_cp("parallel"))(land, w, m, v)


def _all_to_all(pairs, name):
    n = len(pairs)

    def body(*refs):
        srcs, lands = refs[:n], refs[2 * n:3 * n]
        send, recv, loc = refs[3 * n:]
        x, y, c = lax.axis_index("x"), lax.axis_index("y"), lax.axis_index("c")
        me = 4 * x + 2 * y + c
        own = [pltpu.make_async_copy(pairs[t][2](srcs[t], me, me), pairs[t][3](lands[t], me), loc.at[t]) for t in range(n)]
        for cp in own:
            cp.start()
        sends, recvs = [], []
        for k in range(1, N_DEV):
            px, py, pc = x ^ (k >> 2), y ^ ((k >> 1) & 1), c ^ (k & 1)
            peer = 4 * px + 2 * py + pc
            for t in range(n):
                src = pairs[t][2](srcs[t], me, peer)
                cp = pltpu.make_async_remote_copy(src_ref=src, dst_ref=pairs[t][3](lands[t], me), send_sem=send.at[t, k],
                                                  recv_sem=recv.at[t, k], device_id=(px, py, pc), device_id_type=MESH)
                cp.start()
                sends.append(cp)
                recvs.append(pltpu.make_async_remote_copy(src_ref=src, dst_ref=pairs[t][3](lands[t], peer), send_sem=send.at[t, k],
                                                          recv_sem=recv.at[t, k], device_id=(px, py, pc), device_id_type=MESH))
        for cp in recvs:
            cp.wait_recv()
        for cp in sends:
            cp.wait_send()
        for cp in own:
            cp.wait()

    anyspec = pl.BlockSpec(memory_space=pl.ANY)
    lands = [pr[1] for pr in pairs]
    return pl.pallas_call(
        body, name=name, in_specs=[anyspec] * (2 * n), out_specs=[anyspec] * n,
        out_shape=[_sds(a.shape, a.dtype) for a in lands], input_output_aliases={n + t: t for t in range(n)},
        scratch_shapes=[pltpu.SemaphoreType.DMA((n, N_DEV)), pltpu.SemaphoreType.DMA((n, N_DEV)), pltpu.SemaphoreType.DMA((n,))],
        compiler_params=pltpu.CompilerParams(has_side_effects=True))(*[pr[0] for pr in pairs], *lands)


def _gather_whole(src):
    land = lax.empty((N_DEV,) + src.shape, src.dtype)
    return _all_to_all([(src, land, lambda ref, me, peer: ref, lambda ref, sender: ref.at[sender])], "gather_small_grads")[0]


_HBM = pl.BlockSpec(memory_space=pltpu.HBM)
_SEM = pl.BlockSpec(memory_space=pltpu.SEMAPHORE)
_EFFECT = pltpu.SideEffectType.DATAFLOW_SIDE_EFFECTING


def _peers():
    x, y, c = lax.axis_index("x"), lax.axis_index("y"), lax.axis_index("c")
    out = []
    for k in range(1, N_DEV):
        px, py, pc = x ^ (k >> 2), y ^ ((k >> 1) & 1), c ^ (k & 1)
        out.append((k, (px, py, pc), 4 * px + 2 * py + pc))
    return 4 * x + 2 * y + c, out


def _route(mode, layer):
    if mode == "all":
        return (lambda ref, peer: ref), (lambda ref, sender: ref.at[sender])
    return (lambda ref, peer: ref.at[peer]), (lambda ref, sender: ref.at[sender, layer])


def _split_start(srcs, lands, modes, layer, after, name):
    n = len(srcs)
    routes = [_route(m, layer) for m in modes]

    def body(*refs):
        src, land = refs[:n], refs[n:2 * n]
        send, recv, token = refs[2 * n + 1], refs[2 * n + 2], refs[-1]
        me, peers = _peers()
        for k, dev, peer in peers:
            for t, (src_of, dst_of) in enumerate(routes):
                pltpu.make_async_remote_copy(src_ref=src_of(src[t], peer), dst_ref=dst_of(land[t], me), send_sem=send.at[t * N_DEV + k],
                                             recv_sem=recv.at[t * N_DEV + k], device_id=dev, device_id_type=MESH).start()
        token[...] = jnp.zeros_like(token)

    bufs = list(srcs) + list(lands)
    outs = pl.pallas_call(
        body, name=name,
        out_shape=(pltpu.SemaphoreType.DMA((n * N_DEV,)), pltpu.SemaphoreType.DMA((n * N_DEV,)),
                   *[pltpu.HBM(a.shape, a.dtype) for a in bufs], _sds((8, LANES))),
        in_specs=[_HBM] * (2 * n) + [pl.BlockSpec(memory_space=pl.ANY)],
        out_specs=(_SEM, _SEM, *[_HBM] * (2 * n), pl.BlockSpec(memory_space=pltpu.VMEM)),
        input_output_aliases={i: 2 + i for i in range(2 * n)},
        compiler_params=pltpu.CompilerParams(has_side_effects=_EFFECT),
    )(*[pltpu.with_memory_space_constraint(a, pltpu.HBM) for a in bufs], after)
    return outs[0], outs[1], list(outs[2:2 + n]), list(outs[2 + n:2 + 2 * n]), outs[-1]


def _split_wait(send, recv, srcs, lands, modes, layer, after, name):
    n = len(srcs)
    routes = [_route(m, layer) for m in modes]

    def body(*refs):
        src, land = refs[:n], refs[n:2 * n]
        send_r, recv_r = refs[2 * n], refs[2 * n + 1]
        _, peers = _peers()
        for k, dev, peer in peers:
            for t, (src_of, dst_of) in enumerate(routes):
                cp = pltpu.make_async_remote_copy(src_ref=src_of(src[t], peer), dst_ref=dst_of(land[t], peer), send_sem=send_r.at[t * N_DEV + k],
                                                  recv_sem=recv_r.at[t * N_DEV + k], device_id=dev, device_id_type=MESH)
                cp.wait_send()
                cp.wait_recv()

    bufs = list(srcs) + list(lands)
    outs = pl.pallas_call(
        body, name=name, out_shape=tuple(pltpu.HBM(a.shape, a.dtype) for a in bufs),
        in_specs=[_HBM] * (2 * n) + [_SEM, _SEM, pl.BlockSpec(memory_space=pl.ANY)], out_specs=[_HBM] * (2 * n),
        input_output_aliases={i: i for i in range(2 * n)},
        compiler_params=pltpu.CompilerParams(has_side_effects=_EFFECT),
    )(*bufs, send, recv, after)
    return list(outs[:n]), list(outs[n:])


SHARDED = ("w_in", "glu_w1", "glu_w2", "w_out", "w_ff1", "w_ff2", "w_ple_gate", "w_ple_proj")
SMALL = ("attn_norm_g", "gmlp_ln_g", "gmlp_ln_b", "gmlp_ws", "gmlp_bs", "q_norm_g", "k_norm_g", "sinks", "ssm_a_re", "ssm_a_im",
         "ssm_log_dt", "ssm_b_re", "ssm_b_im", "ssm_c_re", "ssm_c_im", "ssm_d", "mix_out_g", "mlp_norm_g", "ple_norm_g")
WEIGHTS = ("attn_norm_g", "w_in", "gmlp_ln_g", "gmlp_ln_b", "gmlp_ws", "gmlp_bs", "q_norm_g", "k_norm_g", "sinks", "ssm_a_re", "ssm_a_im",
           "ssm_log_dt", "ssm_b_re", "ssm_b_im", "ssm_c_re", "ssm_c_im", "ssm_d", "glu_w1", "glu_w2", "mix_out_g", "w_out", "mlp_norm_g",
           "w_ff1", "w_ff2", "ple_norm_g", "w_ple_gate", "w_ple_proj")
FLAT_COLS = 1024


PACK_TILE_ROWS = 8
PACK_ROWS_MULTIPLE = PACK_TILE_ROWS * N_DEV


def _packed_rows(shape):
    return -(-math.prod(shape) // (PACK_TILE_ROWS * FLAT_COLS)) * PACK_TILE_ROWS


def _pack(arrs, dtype):
    blocks = []
    for a in arrs:
        flat = a.astype(dtype).reshape(-1)
        pad = _packed_rows(a.shape) * FLAT_COLS - flat.shape[0]
        if pad:
            flat = jnp.concatenate([flat, jnp.zeros((pad,), dtype)])
        blocks.append(flat.reshape(-1, FLAT_COLS))
    rows = sum(b.shape[0] for b in blocks)
    if rows % PACK_ROWS_MULTIPLE:
        blocks.append(jnp.zeros((PACK_ROWS_MULTIPLE - rows % PACK_ROWS_MULTIPLE, FLAT_COLS), dtype))
    return jnp.concatenate(blocks, axis=0)


def _unpack(flat, shapes):
    out, r = [], 0
    for s in shapes:
        nr = _packed_rows(s)
        out.append(flat[r:r + nr].reshape(-1)[:math.prod(s)].reshape(s))
        r += nr
    return out


def _from_col_major(s):
    n, rows, cs = s.shape
    return s.transpose(1, 0, 2).reshape(rows, n * cs)


EARLY = ("w_in", "glu_w1", "glu_w2")
LATE = ("w_out", "w_ff1", "w_ff2", "w_ple_gate", "w_ple_proj")
GRADS_MID = ("w_ff1", "w_ff2", "w_ple_gate", "w_ple_proj")
GRADS_END = ("w_in", "glu_w1", "glu_w2", "w_out")


def _layer_fwd(h, p, l, cs, sn, W, late_weights, S, sp):
    z = _inproj_fwd(h, S["attn_norm_g"], W["w_in"])
    ya = _gmlp_fwd(z, S["lng"], S["lnb"], S["gmlp_ws"], S["bsx"])
    yb = _attn_fwd(z, cs, sn, S["qg"], S["kg"], S["sinks"])
    yc, xr, xi = _ssm_fwd(z, sp["bbd"], sp["pwr"], sp["pwi"], sp["cbd"], S["ssm_d"], W["glu_w1"], W["glu_w2"])
    late, token = late_weights(ya[0:8, 0:LANES] + yb[0:8, 0:LANES] + yc[0:8, 0:LANES])
    W = {**W, **late}
    h1 = _merge_fwd(h, ya, yb, yc, _behind(S["mix_out_g"], token), W["w_out"])
    h2, r = _mlp_fwd(h1, S["mlp_norm_g"], W["w_ff1"], W["w_ff2"])
    h3 = _ple_fwd(h2, p, l, S["ple_norm_g"], W["w_ple_gate"], W["w_ple_proj"])
    return h3, dict(h=h, z=z, ya=ya, yb=yb, yc=yc, xr=xr, xi=xi, h1=h1, r=r, h2=h2), W


def _layer_bwd(g3, p, l, cs, sn, W, S, sp, A, raw, mid_bwd):
    G = {}
    g2, dwg, G["w_ple_proj"], G["ple_norm_g"] = _ple_bwd(g3, A["h2"], p, l, S["ple_norm_g"], W["w_ple_gate"], W["w_ple_proj"])
    G["w_ple_gate"] = dwg.reshape(N_DEV, -1, D_MODEL)
    g1, hn, da, G["mlp_norm_g"] = _mlp_bwd(g2, A["h1"], S["mlp_norm_g"], A["r"], W["w_ff1"], W["w_ff2"])
    G["w_ff1"] = _tn(hn, da, D_MODEL, D_FF, bm=D_MODEL, bn=FF_BLOCK, n_major=True, dtype=WIRE, name="tn_ff1")
    G["w_ff2"] = _tn(A["r"], g2, D_FF, D_MODEL, bm=1024, bn=1024, dtype=WIRE, name="tn_ff2").reshape(N_DEV, -1, D_MODEL)
    token = mid_bwd(g1, G)
    dya, dyb, dyc, dwo, G["mix_out_g"] = _merge_bwd(g1, A["ya"], A["yb"], A["yc"], _behind(S["mix_out_g"], token), W["w_out"])
    G["w_out"] = dwo.reshape(N_DEV, -1, D_MODEL)
    dzc, dw1, dw2, dcr, dci, dbbd, dlr, dli, dd = _ssm_bwd(dyc, A["z"], A["xr"], A["xi"], sp["bbd"], sp["pwr"], sp["pwi"], sp["cbd"],
                                                          S["ssm_d"], W["glu_w1"], W["glu_w2"])
    G["glu_w1"] = dw1.astype(WIRE).reshape(N_DEV, -1, C_WIDTH)
    G["glu_w2"] = dw2.astype(WIRE).reshape(N_DEV, -1, C_WIDTH)
    dare, daim, dldt, dbtr, dbti, dcre, dcim = _ssm_param_bwd(
        raw["are"], raw["aim"], raw["ldt"], raw["are_x"], raw["aim_x"], raw["ldt_x"], raw["btr"], raw["bti"],
        dlr.reshape(C_GROUPS, C_STATE), dli.reshape(C_GROUPS, C_STATE), dbbd, dcr, dci)
    G["ssm_a_re"], G["ssm_a_im"], G["ssm_log_dt"] = dare, daim, dldt[:, 0]
    G["ssm_b_re"] = dbtr.reshape(C_GROUPS, C_GROUP, C_STATE).transpose(0, 2, 1)
    G["ssm_b_im"] = dbti.reshape(C_GROUPS, C_GROUP, C_STATE).transpose(0, 2, 1)
    G["ssm_c_re"] = dcre.reshape(C_GROUPS, C_GROUP, C_STATE)
    G["ssm_c_im"] = dcim.reshape(C_GROUPS, C_GROUP, C_STATE)
    G["ssm_d"] = dd.reshape(C_GROUPS, C_GROUP)
    dzq, dzk, dzv, dqg, dkg, dsk = _attn_bwd(A["z"], cs, sn, S["qg"], S["kg"], S["sinks"], A["yb"], dyb)
    G["q_norm_g"] = dqg[0, :HEAD_DIM] + dqg[0, HEAD_DIM:]
    G["k_norm_g"] = dkg[0, :HEAD_DIM] + dkg[0, HEAD_DIM:]
    G["sinks"] = dsk[:, 0]
    dza, dws, dbs, dlng, dlnb = _gmlp_bwd(A["z"], dya, S["lng"], S["lnb"], S["gmlp_ws"], S["bsx"])
    G["gmlp_ws"] = dws
    G["gmlp_bs"] = dbs[:, :, 0]
    G["gmlp_ln_g"] = dlng.reshape(A_HEADS, 2, HEAD_DIM)[:, 1]
    G["gmlp_ln_b"] = dlnb.reshape(A_HEADS, 2, HEAD_DIM)[:, 1]
    g0, G["w_in"], G["attn_norm_g"] = _inproj_bwd(g1, A["h"], S["attn_norm_g"], W["w_in"], dza, dzq, dzk, dzv, dzc)
    return g0, G


def _small_layouts(P, l):
    def row(a):
        return a.reshape(1, -1)

    zeros = jnp.zeros((A_HEADS, HEAD_DIM), F32)
    S = dict(
        attn_norm_g=row(P["attn_norm_g"][l]), mix_out_g=row(P["mix_out_g"][l]), mlp_norm_g=row(P["mlp_norm_g"][l]),
        ple_norm_g=row(P["ple_norm_g"][l]),
        lng=jnp.stack([zeros, P["gmlp_ln_g"][l]], axis=1).reshape(1, IN_A),
        lnb=jnp.stack([zeros, P["gmlp_ln_b"][l]], axis=1).reshape(1, IN_A),
        gmlp_ws=P["gmlp_ws"][l],
        bsx=jnp.broadcast_to(P["gmlp_bs"][l][:, :, None], (A_HEADS, CHUNK, CHUNK)),
        qg=jnp.tile(P["q_norm_g"][l], 2).reshape(1, LANES), kg=jnp.tile(P["k_norm_g"][l], 2).reshape(1, LANES),
        sinks=jnp.broadcast_to(P["sinks"][l][:, None], (8, LANES)),
        ssm_d=row(P["ssm_d"][l]),
    )
    are, aim = P["ssm_a_re"][l], P["ssm_a_im"][l]
    ldt = jnp.broadcast_to(P["ssm_log_dt"][l][:, None], (C_GROUPS, C_STATE))
    raw = dict(
        are=are, aim=aim, ldt=ldt,
        are_x=jnp.repeat(are, C_GROUP, axis=0), aim_x=jnp.repeat(aim, C_GROUP, axis=0), ldt_x=jnp.repeat(ldt, C_GROUP, axis=0),
        btr=P["ssm_b_re"][l].transpose(0, 2, 1).reshape(C_WIDTH, C_STATE), bti=P["ssm_b_im"][l].transpose(0, 2, 1).reshape(C_WIDTH, C_STATE),
        cre=P["ssm_c_re"][l].reshape(C_WIDTH, C_STATE), cim=P["ssm_c_im"][l].reshape(C_WIDTH, C_STATE),
    )
    return S, raw


def _ssm_prep_layer(raw):
    bbd, cbd, pwr, pwi = _ssm_prep(raw["are"].reshape(1, N_STATE), raw["aim"].reshape(1, N_STATE), raw["ldt"].reshape(1, N_STATE),
                                   raw["are_x"], raw["aim_x"], raw["ldt_x"], raw["btr"], raw["bti"], raw["cre"], raw["cim"])
    return dict(bbd=bbd, cbd=cbd, pwr=pwr, pwi=pwi)


def _behind(row, token):
    return row if token is None else row + token[0:1, 0:1]


def _local_step(x, p, positions, target, P, weights_of, mid_bwd, after_bwd):
    inv = 1.0 / (ROPE_THETA ** (jnp.arange(0, HEAD_DIM, 2, dtype=F32) / HEAD_DIM))
    cs, sn = _rope_tables(positions.reshape(-1, 1), jnp.tile(inv, 4).reshape(1, LANES))
    h = x
    acts, smalls, weights = [], [], []
    for l in range(DEPTH):
        W, late_weights, token = weights_of(l, h)
        S, raw = _small_layouts(P, l)
        sp = _ssm_prep_layer(raw)
        h, A, W = _layer_fwd(h, p, l, cs, sn, W, late_weights, {**S, "attn_norm_g": _behind(S["attn_norm_g"], token)}, sp)
        acts.append(A)
        smalls.append((S, raw, sp))
        weights.append(W)
    g, lsum = _loss_head(h, target)
    grads = [None] * DEPTH
    token = None
    for l in reversed(range(DEPTH)):
        S, raw, sp = smalls[l]
        g, grads[l] = _layer_bwd(g, p, l, cs, sn, weights[l], {**S, "ple_norm_g": _behind(S["ple_norm_g"], token)}, sp, acts[l], raw,
                                 functools.partial(mid_bwd, l))
        token = after_bwd(l, g, grads[l])
    return lsum[0, 0], g, grads


def _layer_weights(g):
    layout = dict(
        w_in=_from_col_major, glu_w1=lambda a: a.reshape(C_WIDTH, C_WIDTH), glu_w2=lambda a: a.reshape(C_WIDTH, C_WIDTH),
        w_out=lambda a: a.reshape(D_MODEL, D_MODEL), w_ff1=lambda a: a, w_ff2=lambda a: a.reshape(D_FF, D_MODEL),
        w_ple_gate=lambda a: a.reshape(D_MODEL, D_MODEL), w_ple_proj=_from_col_major)
    return {n: layout[n](a) for n, a in g.items()}


def kernel(x, p, positions, attn_norm_g, w_in, gmlp_ln_g, gmlp_ln_b, gmlp_ws, gmlp_bs, q_norm_g, k_norm_g, sinks, ssm_a_re, ssm_a_im, ssm_log_dt, ssm_b_re, ssm_b_im, ssm_c_re, ssm_c_im, ssm_d, glu_w1, glu_w2, mix_out_g, w_out, mlp_norm_g, w_ff1, w_ff2, ple_norm_g, w_ple_gate, w_ple_proj, loss_target, m_attn_norm_g, m_w_in, m_gmlp_ln_g, m_gmlp_ln_b, m_gmlp_ws, m_gmlp_bs, m_q_norm_g, m_k_norm_g, m_sinks, m_ssm_a_re, m_ssm_a_im, m_ssm_log_dt, m_ssm_b_re, m_ssm_b_im, m_ssm_c_re, m_ssm_c_im, m_ssm_d, m_glu_w1, m_glu_w2, m_mix_out_g, m_w_out, m_mlp_norm_g, m_w_ff1, m_w_ff2, m_ple_norm_g, m_w_ple_gate, m_w_ple_proj, v_attn_norm_g, v_w_in, v_gmlp_ln_g, v_gmlp_ln_b, v_gmlp_ws, v_gmlp_bs, v_q_norm_g, v_k_norm_g, v_sinks, v_ssm_a_re, v_ssm_a_im, v_ssm_log_dt, v_ssm_b_re, v_ssm_b_im, v_ssm_c_re, v_ssm_c_im, v_ssm_d, v_glu_w1, v_glu_w2, v_mix_out_g, v_w_out, v_mlp_norm_g, v_w_ff1, v_w_ff2, v_ple_norm_g, v_w_ple_gate, v_w_ple_proj):
    env = dict(locals())
    P = {n: env[n] for n in WEIGHTS}
    M = {n: env["m_" + n] for n in WEIGHTS}
    V = {n: env["v_" + n] for n in WEIGHTS}
    return _step(x, p, positions, loss_target, P, M, V)


def _step(x, p, positions, loss_target, P, M, V):
    small_shapes = [P[n].shape for n in SMALL]
    me = 4 * lax.axis_index("x") + 2 * lax.axis_index("y") + lax.axis_index("c")
    nothing = jnp.zeros((8, LANES), F32)

    def put(land, own, lead):
        return lax.dynamic_update_slice(land, own.reshape((1,) * len(lead) + own.shape), tuple(lead) + (0,) * own.ndim)

    def gather_start(l, names, after, tag):
        shards = [P[n][l].astype(WIRE) for n in names]
        lands = [lax.empty((N_DEV,) + s.shape, WIRE) for s in shards]
        send, recv, shards, lands, token = _split_start(shards, lands, ["all"] * len(names), 0, after, f"gather_start_{l}{tag}")
        return dict(names=names, send=send, recv=recv, shards=shards, lands=lands, token=token, name=f"gather_wait_{l}{tag}")

    def gather_wait(f, after):
        shards, lands = _split_wait(f["send"], f["recv"], f["shards"], f["lands"], ["all"] * len(f["names"]), 0, after, f["name"])
        return dict(zip(f["names"], [put(ld, sh, (me,)) for sh, ld in zip(shards, lands)]))

    first = gather_start(0, EARLY, nothing, "a")
    flying = {0: (first, gather_start(0, LATE, first["token"], "b"))}

    def weights_of(l, h):
        fa, fb = flying.pop(l)
        got = gather_wait(fa, h)
        if fb is None:
            token = None
            if l + 1 < DEPTH:
                flying[l + 1] = (gather_start(l + 1, SHARDED, got["w_in"], ""), None)
                token = flying[l + 1][0]["token"]
            W = _layer_weights(got)
            return {n: W[n] for n in EARLY}, (lambda after: ({n: W[n] for n in LATE}, None)), token

        def late_weights(after):
            late = gather_wait(fb, after)
            flying[l + 1] = (gather_start(l + 1, SHARDED, late["w_out"], ""), None)
            return _layer_weights(late), flying[l + 1][0]["token"]

        return _layer_weights(got), late_weights, fb["token"]

    grad_lands = {n: lax.empty((N_DEV,) + P[n].shape, WIRE) for n in SHARDED}
    sent = []

    def scatter_start(l, names, parts, lands, tag):
        send, recv, parts, lands, token = _split_start(parts, lands, ["own"] * len(parts), l, nothing, f"scatter_start_{l}{tag}")
        sent.append(dict(l=l, names=names, send=send, recv=recv, parts=parts, lands=lands, name=f"scatter_wait_{l}{tag}"))
        return token

    def scatter_wait(after):
        f = sent.pop(0)
        parts, lands = _split_wait(f["send"], f["recv"], f["parts"], f["lands"], ["own"] * len(f["parts"]), f["l"], after, f["name"])
        lands = [put(ld, lax.dynamic_index_in_dim(part, me, 0, keepdims=False), (me, f["l"])) for part, ld in zip(parts, lands)]
        return dict(zip(f["names"], lands))

    def mid_bwd(l, g1, G):
        if l > 0:
            return None
        grad_lands.update(scatter_wait(g1))
        return scatter_start(0, GRADS_MID, [G[n] for n in GRADS_MID], [grad_lands[n] for n in GRADS_MID], "a")

    small_land = []

    def after_bwd(l, g, G):
        if l > 0:
            if sent:
                grad_lands.update(scatter_wait(g))
            return scatter_start(l, SHARDED, [G[n] for n in SHARDED], [grad_lands[n] for n in SHARDED], "")
        sflat = _pack([jnp.stack([grads_of[k][n] for k in range(DEPTH)]) for n in SMALL], F32)
        sparts = sflat.reshape(N_DEV, -1, FLAT_COLS)
        small_land.append(sflat.shape)
        return scatter_start(0, GRADS_END + ("small",), [G[n] for n in GRADS_END] + [sparts],
                             [grad_lands[n] for n in GRADS_END] + [lax.empty((N_DEV, 1) + sparts.shape[1:], F32)], "b")

    grads_of = {}

    def after_bwd_recording(l, g, G):
        grads_of[l] = G
        grads_of["token"] = after_bwd(l, g, G)
        return grads_of["token"]

    lsum, gx, grads = _local_step(x[0], p[:, 0], positions[0], loss_target[0], P, weights_of, mid_bwd, after_bwd_recording)
    G, delta, new_m, new_v = {}, {}, {}, {}

    def update(names):
        for n in names:
            shp = P[n].shape
            res = _sum_adamw(grad_lands[n].reshape(N_DEV, -1, shp[-1]), *(a.reshape(-1, shp[-1]) for a in (P[n], M[n], V[n])))
            G[n], delta[n], new_m[n], new_v[n] = (a.reshape(shp) for a in res)

    grad_lands.update(scatter_wait(grads_of["token"]))
    update(GRADS_MID)
    last = scatter_wait(sum(new_v[n].reshape(-1, LANES)[:8] for n in GRADS_MID))
    small_parts = last.pop("small")
    grad_lands.update(last)
    update(GRADS_END)
    small_sum = _gather_whole(_sum_slots(small_parts[:, 0])).reshape((1,) + small_land[0])
    res = _sum_adamw(small_sum, _pack([P[n] for n in SMALL], F32), _pack([M[n] for n in SMALL], F32), _pack([V[n] for n in SMALL], F32))
    for dst, flat in zip((G, delta, new_m, new_v), res):
        dst.update(zip(SMALL, _unpack(flat, small_shapes)))
    loss = lax.psum(lsum, ("x", "y", "c"))
    return (loss, gx[None], *[G[n] for n in WEIGHTS], *[delta[n] for n in WEIGHTS], *[new_m[n] for n in WEIGHTS], *[new_v[n] for n in WEIGHTS])
```

```python
import functools
import math

import jax
import jax.numpy as jnp
from jax import lax
from jax.experimental import pallas as pl
from jax.experimental.pallas import tpu as pltpu

F32 = jnp.float32
MXU = jnp.bfloat16
WIRE = jnp.bfloat16

D_MODEL = 1024
DEPTH = 4
HEAD_DIM = 64
A_WIDTH = 256
A_HEADS = 4
CHUNK = 128
B_WIDTH = 512
WINDOW = 128
C_WIDTH = 256
C_GROUP = 16
C_GROUPS = 16
C_STATE = 64
N_STATE = C_GROUPS * C_STATE
IN_A = 2 * A_WIDTH
KV_WIDTH = 2 * HEAD_DIM
IN_COLS = IN_A + B_WIDTH + 2 * KV_WIDTH + C_WIDTH
Q_BLOCK = IN_A // B_WIDTH
K_BLOCK = (IN_A + B_WIDTH) // KV_WIDTH
V_BLOCK = K_BLOCK + 1
C_BLOCK = (IN_A + B_WIDTH + 2 * KV_WIDTH) // C_WIDTH
TOKEN_TILE = 512
D_FF = 4096
PLE_DIM = 256
EPS = 1e-6
ROPE_THETA = 10000.0
SCALE = HEAD_DIM ** -0.5
NEG = -1e30
N_DEV = 8

ADAM_LR = 0.001
ADAM_B1 = 0.9
ADAM_B2 = 0.999
ADAM_EPS = 1e-08
ADAM_WD = 0.01
ADAM_STEP = 10

V7X_VMEM_BYTES = 64 * 2 ** 20
VMEM_LIMIT = V7X_VMEM_BYTES - 8 * 2 ** 20
LANES = 128

MESH = pl.DeviceIdType.MESH


def _cp(*sem):
    return pltpu.CompilerParams(dimension_semantics=sem, vmem_limit_bytes=VMEM_LIMIT)


def _sds(shape, dtype=F32):
    return jax.ShapeDtypeStruct(shape, dtype)


def _mm(a, b):
    return jnp.dot(a.astype(MXU), b.astype(MXU), preferred_element_type=F32)


def _mm_nt(a, b):
    return lax.dot_general(a.astype(MXU), b.astype(MXU), (((1,), (1,)), ((), ())), preferred_element_type=F32)


def _mm_tn(a, b):
    return lax.dot_general(a.astype(MXU), b.astype(MXU), (((0,), (0,)), ((), ())), preferred_element_type=F32)


def _lane(shape):
    return lax.broadcasted_iota(jnp.int32, shape, len(shape) - 1)


def _row(shape):
    return lax.broadcasted_iota(jnp.int32, shape, 0)


_GELU_C = math.sqrt(2.0 / math.pi)


def _gelu(x):
    return 0.5 * x * (1.0 + jnp.tanh(_GELU_C * (x + 0.044715 * (x * x * x))))


def _gelu_grad(x):
    t = jnp.tanh(_GELU_C * (x + 0.044715 * (x * x * x)))
    return 0.5 * (1.0 + t) + 0.5 * x * (1.0 - t * t) * (_GELU_C * (1.0 + 3.0 * 0.044715 * (x * x)))


def _sigmoid(x):
    return 1.0 / (1.0 + jnp.exp(-x))


def _rms_stat(x):
    return lax.rsqrt(jnp.mean(x * x, axis=-1, keepdims=True) + EPS)


def _rms_bwd(x, r, g, dy):
    xh = x * r
    dxh = dy * g
    dx = r * (dxh - xh * jnp.mean(dxh * xh, axis=-1, keepdims=True))
    return dx, jnp.sum(dy * xh, axis=0, keepdims=True)


def _tril(w):
    return jnp.where(_row(w.shape) >= _lane(w.shape), w, 0.0)


def _swap64(x):
    return pltpu.roll(x, HEAD_DIM, 1)


def _group_sum64(x, lo):
    s_lo = jnp.sum(jnp.where(lo, x, 0.0), axis=-1, keepdims=True)
    s_hi = jnp.sum(jnp.where(lo, 0.0, x), axis=-1, keepdims=True)
    return jnp.where(lo, s_lo, s_hi)


def _partner(x):
    n = x.shape[-1]
    first = (_lane(x.shape) % HEAD_DIM) < HEAD_DIM // 2
    return jnp.where(first, pltpu.roll(x, n - HEAD_DIM // 2, 1), pltpu.roll(x, HEAD_DIM // 2, 1))


def _rope(y, cs, sn):
    return y * cs + _partner(y) * sn


def _rope_bwd(d, cs, sn):
    return d * cs + _partner(d * sn)


def _qk_norm_rope(x, g, cs, sn):
    lo = _lane(x.shape) < HEAD_DIM
    r = lax.rsqrt(_group_sum64(x * x, lo) * (1.0 / HEAD_DIM) + EPS)
    xh = x * r
    return _rope(xh * g, cs, sn), xh, r


def _qk_norm_rope_bwd(xh, r, g, cs, sn, d):
    lo = _lane(xh.shape) < HEAD_DIM
    dy = _rope_bwd(d, cs, sn)
    dxh = dy * g
    m = _group_sum64(dxh * xh, lo) * (1.0 / HEAD_DIM)
    return r * (dxh - xh * m), jnp.sum(dy * xh, axis=0, keepdims=True)


def _gmlp_head(blk, g, b):
    hi = _lane(blk.shape) >= HEAD_DIM
    mu = jnp.sum(jnp.where(hi, blk, 0.0), axis=-1, keepdims=True) * (1.0 / HEAD_DIM)
    xc = jnp.where(hi, blk - mu, 0.0)
    rstd = lax.rsqrt(jnp.sum(xc * xc, axis=-1, keepdims=True) * (1.0 / HEAD_DIM) + EPS)
    vhat = xc * rstd
    return vhat * g + b, vhat, rstd


def _rope_tables(pos_col, inv_row):
    T = pos_col.shape[0]
    tm = min(T, 1024)

    def body(p_ref, inv_ref, cs_ref, sn_ref):
        ang = p_ref[...].astype(F32) * inv_ref[...]
        s = jnp.sin(ang)
        cs_ref[...] = jnp.cos(ang)
        sn_ref[...] = jnp.where((_lane(ang.shape) % HEAD_DIM) < HEAD_DIM // 2, -s, s)

    blk = pl.BlockSpec((tm, LANES), lambda i: (i, 0))
    return pl.pallas_call(
        body, name="rope_tables", grid=(T // tm,),
        in_specs=[pl.BlockSpec((tm, 1), lambda i: (i, 0)), pl.BlockSpec((1, LANES), lambda i: (0, 0))],
        out_specs=[blk, blk], out_shape=[_sds((T, LANES))] * 2, compiler_params=_cp("parallel"))(pos_col, inv_row)


def _inproj_fwd(h, g, w):
    T = h.shape[0]
    tm = min(T, TOKEN_TILE)

    def body(h_ref, g_ref, w_ref, z_ref):
        x = h_ref[...]
        z_ref[...] = _mm(x * _rms_stat(x) * g_ref[...], w_ref[...])

    return pl.pallas_call(
        body, name="inproj_fwd", grid=(T // tm,),
        in_specs=[pl.BlockSpec((tm, D_MODEL), lambda i: (i, 0)), pl.BlockSpec((1, D_MODEL), lambda i: (0, 0)),
                  pl.BlockSpec((D_MODEL, IN_COLS), lambda i: (0, 0))],
        out_specs=pl.BlockSpec((tm, IN_COLS), lambda i: (i, 0)), out_shape=_sds((T, IN_COLS)),
        compiler_params=_cp("parallel"))(h, g, w)


def _inproj_bwd(gres, h, g, w, dza, dzq, dzk, dzv, dzc):
    T = h.shape[0]
    tm = min(T, TOKEN_TILE)
    nt = T // tm
    shard = IN_COLS // N_DEV

    def body(gr_ref, h_ref, g_ref, w_ref, a_ref, q_ref, k_ref, v_ref, c_ref, dh_ref, dw_ref, dg_ref, acc):
        i = pl.program_id(0)

        @pl.when(i == 0)
        def _():
            dg_ref[...] = jnp.zeros_like(dg_ref)
            acc[...] = jnp.zeros_like(acc)

        x = h_ref[...]
        r = _rms_stat(x)
        gg = g_ref[...]
        dz = jnp.concatenate([a_ref[...], q_ref[...], k_ref[...], v_ref[...], c_ref[...]], axis=1)
        dxn = _mm_nt(dz, w_ref[...])
        dx, dg = _rms_bwd(x, r, gg, dxn)
        dh_ref[...] = gr_ref[...] + dx
        dg_ref[...] += dg
        acc[...] += _mm_tn(x * r * gg, dz)

        @pl.when(i == nt - 1)
        def _():
            for d in range(N_DEV):
                dw_ref[d] = acc[:, d * shard:(d + 1) * shard].astype(WIRE)

    def rows(w_):
        return pl.BlockSpec((tm, w_), lambda i: (i, 0))

    row = pl.BlockSpec((1, D_MODEL), lambda i: (0, 0))
    return pl.pallas_call(
        body, name="inproj_bwd", grid=(nt,),
        in_specs=[rows(D_MODEL), rows(D_MODEL), row, pl.BlockSpec((D_MODEL, IN_COLS), lambda i: (0, 0)),
                  rows(IN_A), rows(B_WIDTH), rows(KV_WIDTH), rows(KV_WIDTH), rows(C_WIDTH)],
        out_specs=[rows(D_MODEL), pl.BlockSpec((N_DEV, D_MODEL, shard), lambda i: (0, 0, 0)), row],
        out_shape=[_sds((T, D_MODEL)), _sds((N_DEV, D_MODEL, shard), WIRE), _sds((1, D_MODEL))],
        scratch_shapes=[pltpu.VMEM((D_MODEL, IN_COLS), F32)],
        compiler_params=_cp("arbitrary"))(gres, h, g, w, dza, dzq, dzk, dzv, dzc)


def _gmlp_fwd(z, lng, lnb, ws, bsx):
    T = z.shape[0]
    tm = min(T, TOKEN_TILE)
    nc = tm // CHUNK

    def body(z_ref, g_ref, b_ref, w_ref, bs_ref, ya_ref):
        zg = _gelu(z_ref[...])
        lo = _lane((tm, LANES)) < HEAD_DIM
        prods = []
        for hd in range(A_HEADS):
            sl = slice(hd * LANES, (hd + 1) * LANES)
            blk = zg[:, sl]
            vn, _, _ = _gmlp_head(blk, g_ref[:, sl], b_ref[:, sl])
            wm = _tril(w_ref[hd])
            sv = jnp.concatenate([_mm(wm, vn[c * CHUNK:(c + 1) * CHUNK]) + bs_ref[hd] for c in range(nc)], axis=0)
            prods.append(blk * _swap64(sv))
        ya_ref[:, 0:LANES] = jnp.where(lo, prods[0], _swap64(prods[1]))
        ya_ref[:, LANES:2 * LANES] = jnp.where(lo, prods[2], _swap64(prods[3]))

    row = pl.BlockSpec((1, IN_A), lambda i: (0, 0))
    mat = pl.BlockSpec((A_HEADS, CHUNK, CHUNK), lambda i: (0, 0, 0))
    return pl.pallas_call(
        body, name="gmlp_fwd", grid=(T // tm,),
        in_specs=[pl.BlockSpec((tm, IN_A), lambda i: (i, 0)), row, row, mat, mat],
        out_specs=pl.BlockSpec((tm, A_WIDTH), lambda i: (i, 0)), out_shape=_sds((T, A_WIDTH)),
        compiler_params=_cp("parallel"))(z, lng, lnb, ws, bsx)


def _gmlp_bwd(z, dya, lng, lnb, ws, bsx):
    T = z.shape[0]
    tm = min(T, TOKEN_TILE)
    nc = tm // CHUNK

    def body(z_ref, dya_ref, g_ref, b_ref, w_ref, bs_ref, dza_ref, dw_ref, dbs_ref, dg_ref, db_ref):
        @pl.when(pl.program_id(0) == 0)
        def _():
            dw_ref[...] = jnp.zeros_like(dw_ref)
            dbs_ref[...] = jnp.zeros_like(dbs_ref)
            dg_ref[...] = jnp.zeros_like(dg_ref)
            db_ref[...] = jnp.zeros_like(db_ref)

        za = z_ref[...]
        zg = _gelu(za)
        gp = _gelu_grad(za)
        lo = _lane((tm, LANES)) < HEAD_DIM
        for hd in range(A_HEADS):
            sl = slice(hd * LANES, (hd + 1) * LANES)
            blk = zg[:, sl]
            g = g_ref[:, sl]
            vn, vhat, rstd = _gmlp_head(blk, g, b_ref[:, sl])
            wm = _tril(w_ref[hd])
            pair = dya_ref[:, (hd // 2) * LANES:(hd // 2 + 1) * LANES]
            dy = jnp.where(lo, pair if hd % 2 == 0 else _swap64(pair), 0.0)
            dsv = _swap64(dy * blk)
            svs, dvns = [], []
            dw = jnp.zeros((CHUNK, CHUNK), F32)
            dbs = jnp.zeros((CHUNK, 1), F32)
            for c in range(nc):
                cs = slice(c * CHUNK, (c + 1) * CHUNK)
                svs.append(_mm(wm, vn[cs]) + bs_ref[hd])
                dw = dw + _mm_nt(dsv[cs], vn[cs])
                dbs = dbs + jnp.sum(dsv[cs], axis=-1, keepdims=True)
                dvns.append(_mm_tn(wm, dsv[cs]))
            sv = jnp.concatenate(svs, axis=0)
            dvn = jnp.concatenate(dvns, axis=0)
            dw_ref[hd] += _tril(dw)
            dbs_ref[hd] += jnp.broadcast_to(dbs, (CHUNK, CHUNK))
            dg_ref[:, sl] += jnp.sum(dvn * vhat, axis=0, keepdims=True)
            db_ref[:, sl] += jnp.sum(dvn, axis=0, keepdims=True)
            du = dy * _swap64(sv)
            dvh = dvn * g
            m1 = jnp.sum(dvh, axis=-1, keepdims=True) * (1.0 / HEAD_DIM)
            m2 = jnp.sum(dvh * vhat, axis=-1, keepdims=True) * (1.0 / HEAD_DIM)
            dv = jnp.where(lo, 0.0, rstd * (dvh - m1 - vhat * m2))
            dza_ref[:, sl] = (du + dv) * gp[:, sl]

    row = pl.BlockSpec((1, IN_A), lambda i: (0, 0))
    mat = pl.BlockSpec((A_HEADS, CHUNK, CHUNK), lambda i: (0, 0, 0))
    return pl.pallas_call(
        body, name="gmlp_bwd", grid=(T // tm,),
        in_specs=[pl.BlockSpec((tm, IN_A), lambda i: (i, 0)), pl.BlockSpec((tm, A_WIDTH), lambda i: (i, 0)), row, row, mat, mat],
        out_specs=[pl.BlockSpec((tm, IN_A), lambda i: (i, 0)), mat, mat, row, row],
        out_shape=[_sds((T, IN_A)), _sds((A_HEADS, CHUNK, CHUNK)), _sds((A_HEADS, CHUNK, CHUNK)), _sds((1, IN_A)), _sds((1, IN_A))],
        compiler_params=_cp("arbitrary"))(z, dya, lng, lnb, ws, bsx)


def _attn_specs(T, tq, tile_of):
    nb = tq // WINDOW

    def prev(i):
        return jnp.maximum(tile_of(i) * nb - 1, 0)

    row = pl.BlockSpec((1, LANES), lambda i: (0, 0))
    return [
        pl.BlockSpec((tq, B_WIDTH), lambda i: (tile_of(i), Q_BLOCK)),
        pl.BlockSpec((tq, LANES), lambda i: (tile_of(i), K_BLOCK)),
        pl.BlockSpec((tq, LANES), lambda i: (tile_of(i), V_BLOCK)),
        pl.BlockSpec((WINDOW, LANES), lambda i: (prev(i), K_BLOCK)),
        pl.BlockSpec((WINDOW, LANES), lambda i: (prev(i), V_BLOCK)),
        pl.BlockSpec((tq, LANES), lambda i: (tile_of(i), 0)),
        pl.BlockSpec((tq, LANES), lambda i: (tile_of(i), 0)),
        pl.BlockSpec((WINDOW, LANES), lambda i: (prev(i), 0)),
        pl.BlockSpec((WINDOW, LANES), lambda i: (prev(i), 0)),
        row, row,
        pl.BlockSpec((8, LANES), lambda i: (0, 0)),
    ]


def _attn_bias(first):
    qi = lax.broadcasted_iota(jnp.int32, (WINDOW, 2 * WINDOW), 0)
    kj = lax.broadcasted_iota(jnp.int32, (WINDOW, 2 * WINDOW), 1)
    diff = qi + WINDOW - kj
    ok = (diff >= 0) & (diff < WINDOW) & ((kj >= WINDOW) | jnp.logical_not(first))
    return jnp.where(ok, 0.0, NEG)


def _dup_heads(x, lo):
    sw = _swap64(x)
    return jnp.where(lo, x, sw), jnp.where(lo, sw, x)


HEADS_PER_KV = 4


def _stack_heads(x0, x1, lo):
    return jnp.concatenate([jnp.where(lo, x0, 0.0), jnp.where(lo, 0.0, x0), jnp.where(lo, x1, 0.0), jnp.where(lo, 0.0, x1)], axis=0)


def _unstack_heads(x4, lo):
    return (jnp.where(lo, x4[0:WINDOW], x4[WINDOW:2 * WINDOW]), jnp.where(lo, x4[2 * WINDOW:3 * WINDOW], x4[3 * WINDOW:4 * WINDOW]))


def _sink_column(sk_ref, g):
    return jnp.concatenate([jnp.broadcast_to(sk_ref[a:a + 1, 0:1], (WINDOW, 1)) for a in range(HEADS_PER_KV * g, HEADS_PER_KV * (g + 1))], axis=0)


def _attn_probs(q4, kw, bias, sink):
    s = _mm_nt(q4, kw)
    s = (s.reshape(HEADS_PER_KV, WINDOW, 2 * WINDOW) + bias[None]).reshape(HEADS_PER_KV * WINDOW, 2 * WINDOW)
    m = jnp.maximum(jnp.max(s, axis=-1, keepdims=True), sink)
    p = jnp.exp(s - m)
    es = jnp.exp(sink - m)
    inv = 1.0 / (jnp.sum(p, axis=-1, keepdims=True) + es)
    return p * inv, es * inv


def _attn_fwd(z, cs, sn, qg, kg, sinks):
    T = z.shape[0]
    tq = min(T, 2 * TOKEN_TILE)
    nb = tq // WINDOW

    def body(q_ref, k_ref, v_ref, kp_ref, vp_ref, cs_ref, sn_ref, csp_ref, snp_ref, qg_ref, kg_ref, sk_ref, o_ref):
        i = pl.program_id(0)
        csq, snq = cs_ref[...], sn_ref[...]
        cs_all = jnp.concatenate([csp_ref[...], csq], axis=0)
        sn_all = jnp.concatenate([snp_ref[...], snq], axis=0)
        k_all = jnp.concatenate([kp_ref[...], k_ref[...]], axis=0)
        v_all = jnp.concatenate([vp_ref[...], v_ref[...]], axis=0)
        kr, _, _ = _qk_norm_rope(k_all, kg_ref[...], cs_all, sn_all)
        lo_all = _lane(k_all.shape) < HEAD_DIM
        kd = _dup_heads(kr, lo_all)
        vd = _dup_heads(v_all, lo_all)
        lo = _lane((WINDOW, LANES)) < HEAD_DIM
        qrs = [_qk_norm_rope(q_ref[:, pr * LANES:(pr + 1) * LANES], qg_ref[...], csq, snq)[0] * SCALE for pr in range(4)]
        biases = [_attn_bias(i * nb + b == 0) for b in range(nb)]
        for g in range(2):
            sink = _sink_column(sk_ref, g)
            for b in range(nb):
                bs = slice(b * WINDOW, (b + 1) * WINDOW)
                ws = slice(b * WINDOW, (b + 2) * WINDOW)
                pn, _ = _attn_probs(_stack_heads(qrs[2 * g][bs], qrs[2 * g + 1][bs], lo), kd[g][ws], biases[b], sink)
                o0, o1 = _unstack_heads(_mm(pn, vd[g][ws]), lo)
                o_ref[bs, 2 * g * LANES:(2 * g + 1) * LANES] = o0
                o_ref[bs, (2 * g + 1) * LANES:(2 * g + 2) * LANES] = o1

    return pl.pallas_call(
        body, name="attn_fwd", grid=(T // tq,),
        in_specs=_attn_specs(T, tq, lambda i: i),
        out_specs=pl.BlockSpec((tq, B_WIDTH), lambda i: (i, 0)), out_shape=_sds((T, B_WIDTH)),
        compiler_params=_cp("parallel"))(z, z, z, z, z, cs, sn, cs, sn, qg, kg, sinks)


def _attn_bwd(z, cs, sn, qg, kg, sinks, o, do):
    T = z.shape[0]
    tq = min(T, 2 * TOKEN_TILE)
    nb = tq // WINDOW
    nt = T // tq
    tk = tq + WINDOW

    def tile_of(i):
        return nt - 1 - i

    def body(q_ref, k_ref, v_ref, kp_ref, vp_ref, cs_ref, sn_ref, csp_ref, snp_ref, qg_ref, kg_ref, sk_ref, o_ref, do_ref,
             dq_ref, dk_ref, dv_ref, dqg_ref, dkg_ref, dsk_ref, acck, accv, ck, cv):
        i = pl.program_id(0)
        ti = nt - 1 - i

        @pl.when(i == 0)
        def _():
            dqg_ref[...] = jnp.zeros_like(dqg_ref)
            dkg_ref[...] = jnp.zeros_like(dkg_ref)
            dsk_ref[...] = jnp.zeros_like(dsk_ref)
            ck[...] = jnp.zeros_like(ck)
            cv[...] = jnp.zeros_like(cv)

        csq, snq = cs_ref[...], sn_ref[...]
        cs_all = jnp.concatenate([csp_ref[...], csq], axis=0)
        sn_all = jnp.concatenate([snp_ref[...], snq], axis=0)
        k_all = jnp.concatenate([kp_ref[...], k_ref[...]], axis=0)
        v_all = jnp.concatenate([vp_ref[...], v_ref[...]], axis=0)
        kr, kh, rk = _qk_norm_rope(k_all, kg_ref[...], cs_all, sn_all)
        lo_all = _lane(k_all.shape) < HEAD_DIM
        kd = _dup_heads(kr, lo_all)
        vd = _dup_heads(v_all, lo_all)
        lo = _lane((WINDOW, LANES)) < HEAD_DIM
        acck[...] = jnp.zeros_like(acck)
        accv[...] = jnp.zeros_like(accv)
        prep = [_qk_norm_rope(q_ref[:, pr * LANES:(pr + 1) * LANES], qg_ref[...], csq, snq) for pr in range(4)]
        biases = [_attn_bias(ti * nb + b == 0) for b in range(nb)]
        dqs = [[None] * nb for _ in range(4)]
        for g in range(2):
            sink = _sink_column(sk_ref, g)
            dsink = jnp.zeros((HEADS_PER_KV * WINDOW, 1), F32)
            for b in range(nb):
                bs = slice(b * WINDOW, (b + 1) * WINDOW)
                ws = slice(b * WINDOW, (b + 2) * WINDOW)
                kw, vw = kd[g][ws], vd[g][ws]
                q4 = _stack_heads(prep[2 * g][0][bs] * SCALE, prep[2 * g + 1][0][bs] * SCALE, lo)
                pn, psink = _attn_probs(q4, kw, biases[b], sink)
                o0, o1 = o_ref[bs, 2 * g * LANES:(2 * g + 1) * LANES], o_ref[bs, (2 * g + 1) * LANES:(2 * g + 2) * LANES]
                do4 = _stack_heads(do_ref[bs, 2 * g * LANES:(2 * g + 1) * LANES], do_ref[bs, (2 * g + 1) * LANES:(2 * g + 2) * LANES], lo)
                delta = jnp.sum(do4 * jnp.concatenate([o0, o0, o1, o1], axis=0), axis=-1, keepdims=True)
                ds = pn * (_mm_nt(do4, vw) - delta)
                dsink = dsink - psink * delta
                dqs[2 * g][b], dqs[2 * g + 1][b] = _unstack_heads(_mm(ds, kw) * SCALE, lo)
                acck[g, ws, :] += _mm_tn(ds, q4)
                accv[g, ws, :] += _mm_tn(pn, do4)
            for hh in range(HEADS_PER_KV):
                a = HEADS_PER_KV * g + hh
                dsk_ref[a:a + 1, :] += jnp.zeros((1, LANES), F32) + jnp.sum(dsink[hh * WINDOW:(hh + 1) * WINDOW])
        for pr in range(4):
            _, qh, rq = prep[pr]
            dx, dg = _qk_norm_rope_bwd(qh, rq, qg_ref[...], csq, snq, jnp.concatenate(dqs[pr], axis=0))
            dq_ref[:, pr * LANES:(pr + 1) * LANES] = dx
            dqg_ref[...] += dg

        def fold(acc):
            f0 = acc[0] + _swap64(acc[0])
            f1 = acc[1] + _swap64(acc[1])
            return jnp.where(lo_all, f0, f1)

        dk_all = fold(acck)
        dv_all = fold(accv)
        pad = jnp.zeros((tq - WINDOW, LANES), F32)
        dk_own = dk_all[WINDOW:] + (jnp.concatenate([pad, ck[...]], axis=0) if nb > 1 else ck[...])
        dv_own = dv_all[WINDOW:] + (jnp.concatenate([pad, cv[...]], axis=0) if nb > 1 else cv[...])
        ck[...] = dk_all[:WINDOW]
        cv[...] = dv_all[:WINDOW]
        dxk, dgk = _qk_norm_rope_bwd(kh[WINDOW:], rk[WINDOW:], kg_ref[...], csq, snq, dk_own)
        dk_ref[...] = dxk
        dkg_ref[...] += dgk
        dv_ref[...] = dv_own

    row = pl.BlockSpec((1, LANES), lambda i: (0, 0))
    return pl.pallas_call(
        body, name="attn_bwd", grid=(nt,),
        in_specs=_attn_specs(T, tq, tile_of) + [pl.BlockSpec((tq, B_WIDTH), lambda i: (tile_of(i), 0))] * 2,
        out_specs=[pl.BlockSpec((tq, B_WIDTH), lambda i: (tile_of(i), 0)), pl.BlockSpec((tq, LANES), lambda i: (tile_of(i), 0)),
                   pl.BlockSpec((tq, LANES), lambda i: (tile_of(i), 0)), row, row, pl.BlockSpec((8, LANES), lambda i: (0, 0))],
        out_shape=[_sds((T, B_WIDTH)), _sds((T, LANES)), _sds((T, LANES)), _sds((1, LANES)), _sds((1, LANES)), _sds((8, LANES))],
        scratch_shapes=[pltpu.VMEM((2, tk, LANES), F32), pltpu.VMEM((2, tk, LANES), F32),
                        pltpu.VMEM((WINDOW, LANES), F32), pltpu.VMEM((WINDOW, LANES), F32)],
        compiler_params=_cp("arbitrary"))(z, z, z, z, z, cs, sn, cs, sn, qg, kg, sinks, o, do)


def _bbar_t(are, aim, ldt, btr, bti):
    lbr, lbi = _lam_bar(are, aim, ldt)
    den = are * are + aim * aim
    nr = lbr - 1.0
    cr = (nr * are + lbi * aim) / den
    ci = (lbi * are - nr * aim) / den
    return cr * btr - ci * bti, cr * bti + ci * btr


def _lam_bar(are, aim, ldt):
    dt = jnp.exp(ldt)
    er = jnp.exp(are * dt)
    return er * jnp.cos(aim * dt), er * jnp.sin(aim * dt)


def _block_diag(x):
    t = jnp.concatenate([x] * C_GROUPS, axis=1)
    return jnp.where(_row(t.shape) // C_GROUP == _lane(t.shape) // C_STATE, t, 0.0)


def _block_diag_fold(m):
    rg = _row((C_WIDTH, C_STATE)) // C_GROUP
    acc = jnp.zeros((C_WIDTH, C_STATE), F32)
    for g in range(C_GROUPS):
        acc = acc + jnp.where(rg == g, m[:, g * C_STATE:(g + 1) * C_STATE], 0.0)
    return acc


def _ssm_prep(are, aim, ldt, are_x, aim_x, ldt_x, btr, bti, cre, cim):
    def body(are_r, aim_r, ldt_r, arex_r, aimx_r, ldtx_r, btr_r, bti_r, cre_r, cim_r, bbd_ref, cbd_ref, pwr_ref, pwi_ref):
        lr, li = _lam_bar(are_r[...], aim_r[...], ldt_r[...])
        cr, ci = lr, li
        for r in range(SCAN_SEG):
            pwr_ref[r:r + 1, :] = cr
            pwi_ref[r:r + 1, :] = ci
            cr, ci = cr * lr - ci * li, cr * li + ci * lr
        br, bi = _bbar_t(arex_r[...], aimx_r[...], ldtx_r[...], btr_r[...], bti_r[...])
        bbd_ref[...] = jnp.concatenate([_block_diag(br), _block_diag(bi)], axis=1).astype(MXU)
        cbd_ref[...] = jnp.concatenate([_block_diag(cre_r[...]), -_block_diag(cim_r[...])], axis=1).astype(MXU)

    return pl.pallas_call(
        body, name="ssm_prep",
        out_shape=[_sds((C_WIDTH, 2 * N_STATE), MXU), _sds((C_WIDTH, 2 * N_STATE), MXU), _sds((SCAN_SEG, N_STATE)), _sds((SCAN_SEG, N_STATE))],
        compiler_params=pltpu.CompilerParams(vmem_limit_bytes=VMEM_LIMIT))(are, aim, ldt, are_x, aim_x, ldt_x, btr, bti, cre, cim)


def _ssm_param_bwd(are, aim, ldt, are_x, aim_x, ldt_x, btr, bti, dlr, dli, dbbd, dcr, dci):
    def body(are_r, aim_r, ldt_r, arex_r, aimx_r, ldtx_r, btr_r, bti_r, dlr_r, dli_r, dbbd_r, dcr_r, dci_r,
             dare_ref, daim_ref, dldt_ref, dbtr_ref, dbti_ref, dcre_ref, dcim_ref):
        _, vjp_l = jax.vjp(_lam_bar, are_r[...], aim_r[...], ldt_r[...])
        da1, di1, dl1 = vjp_l((dlr_r[...], dli_r[...]))
        dbr = _block_diag_fold(dbbd_r[:, 0:N_STATE])
        dbi = _block_diag_fold(dbbd_r[:, N_STATE:2 * N_STATE])
        _, vjp_b = jax.vjp(_bbar_t, arex_r[...], aimx_r[...], ldtx_r[...], btr_r[...], bti_r[...])
        da2, di2, dl2, dbtr, dbti = vjp_b((dbr, dbi))

        def gsum(x):
            return x.reshape(C_GROUPS, C_GROUP, C_STATE).sum(axis=1)

        dare_ref[...] = da1 + gsum(da2)
        daim_ref[...] = di1 + gsum(di2)
        dldt_ref[...] = jnp.broadcast_to(jnp.sum(dl1 + gsum(dl2), axis=-1, keepdims=True), (C_GROUPS, LANES))
        dbtr_ref[...] = dbtr
        dbti_ref[...] = dbti
        dcre_ref[...] = _block_diag_fold(dcr_r[...])
        dcim_ref[...] = -_block_diag_fold(dci_r[...])

    g = _sds((C_GROUPS, C_STATE))
    x = _sds((C_WIDTH, C_STATE))
    return pl.pallas_call(
        body, name="ssm_param_bwd", out_shape=[g, g, _sds((C_GROUPS, LANES)), x, x, x, x],
        compiler_params=pltpu.CompilerParams(vmem_limit_bytes=VMEM_LIMIT))(are, aim, ldt, are_x, aim_x, ldt_x, btr, bti, dlr, dli, dbbd, dcr, dci)


SCAN_TILE = 512
SCAN_SEG = 8


def _scan_tables(pwr_ref, pwi_ref, conj, reverse):
    row = _row((SCAN_SEG, N_STATE))
    shifts = []
    for k in (1, 2, 4):
        keep = (row < SCAN_SEG - k) if reverse else (row >= k)
        ar = jnp.broadcast_to(pwr_ref[k - 1:k, :], (SCAN_SEG, N_STATE))
        ai = jnp.broadcast_to(pwi_ref[k - 1:k, :], (SCAN_SEG, N_STATE)) * conj
        shifts.append((SCAN_SEG - k if reverse else k, jnp.where(keep, ar, 0.0), jnp.where(keep, ai, 0.0)))
    if reverse:
        pr = jnp.concatenate([pwr_ref[SCAN_SEG - 1 - r:SCAN_SEG - r, :] for r in range(SCAN_SEG)], axis=0)
        pi = jnp.concatenate([pwi_ref[SCAN_SEG - 1 - r:SCAN_SEG - r, :] for r in range(SCAN_SEG)], axis=0) * conj
    else:
        pr, pi = pwr_ref[...], pwi_ref[...] * conj
    return shifts, (pr, pi)


def _tile_scan(xr_ref, xi_ref, pwr_ref, pwi_ref, sr, si, conj, reverse):
    shifts, (pr, pi) = _scan_tables(pwr_ref, pwi_ref, conj, reverse)
    groups = xr_ref.shape[0] // SCAN_SEG
    out_row = 0 if reverse else SCAN_SEG - 1

    def step(k, c):
        cr, ci = c
        g = groups - 1 - k if reverse else k
        rows = pl.ds(pl.multiple_of(g * SCAN_SEG, SCAN_SEG), SCAN_SEG)
        xr, xi = xr_ref[rows, :], xi_ref[rows, :]
        for amount, ar, ai in shifts:
            qr, qi = pltpu.roll(xr, amount, 0), pltpu.roll(xi, amount, 0)
            xr, xi = xr + ar * qr - ai * qi, xi + ar * qi + ai * qr
        xr, xi = xr + pr * cr - pi * ci, xi + pr * ci + pi * cr
        xr_ref[rows, :] = xr
        xi_ref[rows, :] = xi
        return xr[out_row:out_row + 1], xi[out_row:out_row + 1]

    cr, ci = lax.fori_loop(0, groups, step, (sr[...], si[...]), unroll=2)
    sr[...] = cr
    si[...] = ci


def _ssm_fwd(z, bbd, pwr, pwi, cbd, dsk, w1, w2):
    T = z.shape[0]
    tt = min(T, 2 * SCAN_TILE)

    def body(u_ref, bbd_ref, pwr_ref, pwi_ref, cbd_ref, d_ref, w1_ref, w2_ref, yc_ref, xr_ref, xi_ref, sr, si):
        @pl.when(pl.program_id(0) == 0)
        def _():
            sr[...] = jnp.zeros_like(sr)
            si[...] = jnp.zeros_like(si)

        u = u_ref[...]
        bu = _mm(u, bbd_ref[...])
        xr_ref[...] = bu[:, 0:N_STATE]
        xi_ref[...] = bu[:, N_STATE:2 * N_STATE]
        _tile_scan(xr_ref, xi_ref, pwr_ref, pwi_ref, sr, si, 1.0, False)
        x = jnp.concatenate([xr_ref[...], xi_ref[...]], axis=1)
        y2 = _gelu(_mm_nt(x, cbd_ref[...]) + d_ref[...] * u)
        yc_ref[...] = _mm(y2, w1_ref[...]) * _sigmoid(_mm(y2, w2_ref[...]))

    big = pl.BlockSpec((C_WIDTH, 2 * N_STATE), lambda i: (0, 0))
    tab = pl.BlockSpec((SCAN_SEG, N_STATE), lambda i: (0, 0))
    wsp = pl.BlockSpec((C_WIDTH, C_WIDTH), lambda i: (0, 0))
    xs = pl.BlockSpec((tt, N_STATE), lambda i: (i, 0))
    return pl.pallas_call(
        body, name="ssm_fwd", grid=(T // tt,),
        in_specs=[pl.BlockSpec((tt, C_WIDTH), lambda i: (i, C_BLOCK)), big, tab, tab, big, pl.BlockSpec((1, C_WIDTH), lambda i: (0, 0)), wsp, wsp],
        out_specs=[pl.BlockSpec((tt, C_WIDTH), lambda i: (i, 0)), xs, xs],
        out_shape=[_sds((T, C_WIDTH)), _sds((T, N_STATE)), _sds((T, N_STATE))],
        scratch_shapes=[pltpu.VMEM((1, N_STATE), F32)] * 2,
        compiler_params=_cp("arbitrary"))(z, bbd, pwr, pwi, cbd, dsk, w1, w2)


def _ssm_bwd(dyc, z, xr, xi, bbd, pwr, pwi, cbd, dsk, w1, w2):
    T = z.shape[0]
    tt = min(T, SCAN_TILE)
    nt = T // tt

    def tile_of(i):
        return nt - 1 - i

    def body(dyc_ref, u_ref, xr_ref, xi_ref, xpr_ref, xpi_ref, bbd_ref, pwr_ref, pwi_ref, cbd_ref, d_ref, w1_ref, w2_ref,
             du_ref, dw1_ref, dw2_ref, dcr_ref, dci_ref, dbbd_ref, dlr_ref, dli_ref, dd_ref, gr, gi, sr, si):
        i = pl.program_id(0)
        ti = nt - 1 - i

        @pl.when(i == 0)
        def _():
            sr[...] = jnp.zeros_like(sr)
            si[...] = jnp.zeros_like(si)
            for acc in (dw1_ref, dw2_ref, dcr_ref, dci_ref, dbbd_ref, dlr_ref, dli_ref, dd_ref):
                acc[...] = jnp.zeros_like(acc)

        u = u_ref[...]
        xr_t, xi_t = xr_ref[...], xi_ref[...]
        y = _mm_nt(jnp.concatenate([xr_t, xi_t], axis=1), cbd_ref[...]) + d_ref[...] * u
        y2 = _gelu(y)
        a1 = _mm(y2, w1_ref[...])
        sg = _sigmoid(_mm(y2, w2_ref[...]))
        dyc_t = dyc_ref[...]
        da1 = dyc_t * sg
        da2 = dyc_t * a1 * sg * (1.0 - sg)
        dy = (_mm_nt(da1, w1_ref[...]) + _mm_nt(da2, w2_ref[...])) * _gelu_grad(y)
        gx = _mm(dy, cbd_ref[...])
        gr[...] = gx[:, 0:N_STATE]
        gi[...] = gx[:, N_STATE:2 * N_STATE]
        _tile_scan(gr, gi, pwr_ref, pwi_ref, sr, si, -1.0, True)
        ar, ai = gr[...], gi[...]
        first_row = _row(ar.shape) == 0
        live = jnp.where(ti > 0, 1.0, 0.0)
        xsr = jnp.where(first_row, xpr_ref[7:8, :] * live, pltpu.roll(xr_t, 1, 0))
        xsi = jnp.where(first_row, xpi_ref[7:8, :] * live, pltpu.roll(xi_t, 1, 0))
        dlr_ref[...] += jnp.sum(ar * xsr + ai * xsi, axis=0, keepdims=True)
        dli_ref[...] += jnp.sum(ai * xsr - ar * xsi, axis=0, keepdims=True)
        dd_ref[...] += jnp.sum(dy * u, axis=0, keepdims=True)
        arai = jnp.concatenate([ar, ai], axis=1)
        du_ref[...] = _mm_nt(arai, bbd_ref[...]) + d_ref[...] * dy
        dw1_ref[...] += _mm_tn(y2, da1)
        dw2_ref[...] += _mm_tn(y2, da2)
        dcr_ref[...] += _mm_tn(dy, xr_t)
        dci_ref[...] += _mm_tn(dy, xi_t)
        dbbd_ref[...] += _mm_tn(u, arai)

    def prev(i):
        return jnp.maximum(tile_of(i) * (tt // 8) - 1, 0)

    big = pl.BlockSpec((C_WIDTH, 2 * N_STATE), lambda i: (0, 0))
    tab = pl.BlockSpec((SCAN_SEG, N_STATE), lambda i: (0, 0))
    srow = pl.BlockSpec((1, N_STATE), lambda i: (0, 0))
    wsp = pl.BlockSpec((C_WIDTH, C_WIDTH), lambda i: (0, 0))
    xs = pl.BlockSpec((tt, N_STATE), lambda i: (tile_of(i), 0))
    xp = pl.BlockSpec((8, N_STATE), lambda i: (prev(i), 0))
    cw = pl.BlockSpec((tt, C_WIDTH), lambda i: (tile_of(i), 0))
    drow = pl.BlockSpec((1, C_WIDTH), lambda i: (0, 0))
    return pl.pallas_call(
        body, name="ssm_bwd", grid=(nt,),
        in_specs=[cw, pl.BlockSpec((tt, C_WIDTH), lambda i: (tile_of(i), C_BLOCK)), xs, xs, xp, xp, big, tab, tab, big, drow, wsp, wsp],
        out_specs=[cw, wsp, wsp, pl.BlockSpec((C_WIDTH, N_STATE), lambda i: (0, 0)), pl.BlockSpec((C_WIDTH, N_STATE), lambda i: (0, 0)), big,
                   srow, srow, drow],
        out_shape=[_sds((T, C_WIDTH)), _sds((C_WIDTH, C_WIDTH)), _sds((C_WIDTH, C_WIDTH)), _sds((C_WIDTH, N_STATE)), _sds((C_WIDTH, N_STATE)),
                   _sds((C_WIDTH, 2 * N_STATE)), _sds((1, N_STATE)), _sds((1, N_STATE)), _sds((1, C_WIDTH))],
        scratch_shapes=[pltpu.VMEM((tt, N_STATE), F32)] * 2 + [pltpu.VMEM((1, N_STATE), F32)] * 2,
        compiler_params=_cp("arbitrary"))(dyc, z, xr, xi, xr, xi, bbd, pwr, pwi, cbd, dsk, w1, w2)


_GROUPS = ((0, A_WIDTH), (A_WIDTH, A_WIDTH + B_WIDTH), (A_WIDTH + B_WIDTH, D_MODEL))


def _merge_fwd(h, ya, yb, yc, g, w):
    T = h.shape[0]
    tm = min(T, TOKEN_TILE)

    def body(h_ref, a_ref, b_ref, c_ref, g_ref, w_ref, o_ref):
        yn = jnp.concatenate([y * _rms_stat(y) for y in (a_ref[...], b_ref[...], c_ref[...])], axis=1) * g_ref[...]
        o_ref[...] = h_ref[...] + _mm(yn, w_ref[...])

    def rows(w_):
        return pl.BlockSpec((tm, w_), lambda i: (i, 0))

    return pl.pallas_call(
        body, name="merge_fwd", grid=(T // tm,),
        in_specs=[rows(D_MODEL), rows(A_WIDTH), rows(B_WIDTH), rows(C_WIDTH), pl.BlockSpec((1, D_MODEL), lambda i: (0, 0)),
                  pl.BlockSpec((D_MODEL, D_MODEL), lambda i: (0, 0))],
        out_specs=rows(D_MODEL), out_shape=_sds((T, D_MODEL)), compiler_params=_cp("parallel"))(h, ya, yb, yc, g, w)


def _merge_bwd(gres, ya, yb, yc, g, w):
    T = gres.shape[0]
    tm = min(T, TOKEN_TILE)
    nt = T // tm

    def body(gr_ref, a_ref, b_ref, c_ref, g_ref, w_ref, da_ref, db_ref, dc_ref, dw_ref, dg_ref, acc):
        i = pl.program_id(0)

        @pl.when(i == 0)
        def _():
            dg_ref[...] = jnp.zeros_like(dg_ref)
            acc[...] = jnp.zeros_like(acc)

        gr = gr_ref[...]
        dyn = _mm_nt(gr, w_ref[...])
        yns, dgs = [], []
        for (c0, c1), y_ref, d_ref in zip(_GROUPS, (a_ref, b_ref, c_ref), (da_ref, db_ref, dc_ref)):
            y = y_ref[...]
            r = _rms_stat(y)
            gg = g_ref[:, c0:c1]
            dx, dg = _rms_bwd(y, r, gg, dyn[:, c0:c1])
            d_ref[...] = dx
            dgs.append(dg)
            yns.append(y * r * gg)
        dg_ref[...] += jnp.concatenate(dgs, axis=1)
        acc[...] += _mm_tn(jnp.concatenate(yns, axis=1), gr)

        @pl.when(i == nt - 1)
        def _():
            dw_ref[...] = acc[...].astype(WIRE)

    def rows(w_):
        return pl.BlockSpec((tm, w_), lambda i: (i, 0))

    row = pl.BlockSpec((1, D_MODEL), lambda i: (0, 0))
    full = pl.BlockSpec((D_MODEL, D_MODEL), lambda i: (0, 0))
    return pl.pallas_call(
        body, name="merge_bwd", grid=(nt,),
        in_specs=[rows(D_MODEL), rows(A_WIDTH), rows(B_WIDTH), rows(C_WIDTH), row, full],
        out_specs=[rows(A_WIDTH), rows(B_WIDTH), rows(C_WIDTH), full, row],
        out_shape=[_sds((T, A_WIDTH)), _sds((T, B_WIDTH)), _sds((T, C_WIDTH)), _sds((D_MODEL, D_MODEL), WIRE), _sds((1, D_MODEL))],
        scratch_shapes=[pltpu.VMEM((D_MODEL, D_MODEL), F32)],
        compiler_params=_cp("arbitrary"))(gres, ya, yb, yc, g, w)


FF_BLOCK = D_FF // N_DEV


def _load_weights(w1_hbm, w2_hbm, w1, w2, sem):
    @pl.when(pl.program_id(0) == 0)
    def _():
        copies = [pltpu.make_async_copy(w1_hbm.at[j], w1.at[:, pl.ds(j * FF_BLOCK, FF_BLOCK)], sem.at[j]) for j in range(N_DEV)]
        copies.append(pltpu.make_async_copy(w2_hbm, w2, sem.at[N_DEV]))
        for cp in copies:
            cp.start()
        for cp in copies:
            cp.wait()


def _mlp_weight_scratch():
    return [pltpu.VMEM((D_MODEL, D_FF), MXU), pltpu.VMEM((D_FF, D_MODEL), MXU), pltpu.SemaphoreType.DMA((N_DEV + 1,))]


def _mlp_fwd(h, g, w1, w2):
    T = h.shape[0]
    tm = min(T, TOKEN_TILE)

    def body(h_ref, g_ref, w1_hbm, w2_hbm, o_ref, r_ref, w1_v, w2_v, sem):
        _load_weights(w1_hbm, w2_hbm, w1_v, w2_v, sem)
        x = h_ref[...]
        a = jnp.maximum(_mm(x * _rms_stat(x) * g_ref[...], w1_v[...]), 0.0)
        r = (a * a).astype(MXU)
        r_ref[...] = r
        o_ref[...] = x + _mm(r, w2_v[...])

    rows = pl.BlockSpec((tm, D_MODEL), lambda i: (i, 0))
    hbm = pl.BlockSpec(memory_space=pl.ANY)
    return pl.pallas_call(
        body, name="mlp_fwd", grid=(T // tm,),
        in_specs=[rows, pl.BlockSpec((1, D_MODEL), lambda i: (0, 0)), hbm, hbm],
        out_specs=[rows, pl.BlockSpec((tm, D_FF), lambda i: (i, 0))],
        out_shape=[_sds((T, D_MODEL)), _sds((T, D_FF), MXU)],
        scratch_shapes=_mlp_weight_scratch(), compiler_params=_cp("arbitrary"))(h, g, w1, w2)


def _mlp_bwd(gres, h, g, r, w1, w2):
    T = h.shape[0]
    tm = min(T, 256)

    def body(gr_ref, h_ref, g_ref, r_ref, w1_hbm, w2_hbm, dh_ref, hn_ref, da_ref, dg_ref, w1_v, w2_v, sem):
        _load_weights(w1_hbm, w2_hbm, w1_v, w2_v, sem)

        @pl.when(pl.program_id(0) == 0)
        def _():
            dg_ref[...] = jnp.zeros_like(dg_ref)

        gr = gr_ref[...]
        da = (_mm_nt(gr, w2_v[...]) * (2.0 * jnp.sqrt(r_ref[...].astype(F32)))).astype(MXU)
        da_ref[...] = da
        x = h_ref[...]
        rs = _rms_stat(x)
        gg = g_ref[...]
        dx, dg = _rms_bwd(x, rs, gg, _mm_nt(da, w1_v[...]))
        dh_ref[...] = gr + dx
        dg_ref[...] += dg
        hn_ref[...] = (x * rs * gg).astype(MXU)

    rows = pl.BlockSpec((tm, D_MODEL), lambda i: (i, 0))
    row = pl.BlockSpec((1, D_MODEL), lambda i: (0, 0))
    wide = pl.BlockSpec((tm, D_FF), lambda i: (i, 0))
    hbm = pl.BlockSpec(memory_space=pl.ANY)
    return pl.pallas_call(
        body, name="mlp_bwd", grid=(T // tm,),
        in_specs=[rows, rows, row, wide, hbm, hbm], out_specs=[rows, rows, wide, row],
        out_shape=[_sds((T, D_MODEL)), _sds((T, D_MODEL), MXU), _sds((T, D_FF), MXU), _sds((1, D_MODEL))],
        scratch_shapes=_mlp_weight_scratch(), compiler_params=_cp("arbitrary"))(gres, h, g, r, w1, w2)


def _ple_fwd(h, p, l, g, wg, wp):
    T = h.shape[0]
    tm = min(T, TOKEN_TILE)

    def body(h_ref, p_ref, g_ref, wg_ref, wp_ref, o_ref, gp_ref, e_ref):
        x = h_ref[...]
        gp = _mm(x * _rms_stat(x) * g_ref[...], wg_ref[...])
        e = _mm(p_ref[...], wp_ref[...])
        o_ref[...] = x + _sigmoid(gp) * e
        gp_ref[...] = gp.astype(MXU)
        e_ref[...] = e.astype(MXU)

    rows = pl.BlockSpec((tm, D_MODEL), lambda i: (i, 0))
    return pl.pallas_call(
        body, name="ple_fwd", grid=(T // tm,),
        in_specs=[rows, pl.BlockSpec((None, tm, PLE_DIM), lambda i: (l, i, 0)), pl.BlockSpec((1, D_MODEL), lambda i: (0, 0)),
                  pl.BlockSpec((D_MODEL, D_MODEL), lambda i: (0, 0)), pl.BlockSpec((PLE_DIM, D_MODEL), lambda i: (0, 0))],
        out_specs=[rows, rows, rows], out_shape=[_sds((T, D_MODEL)), _sds((T, D_MODEL), MXU), _sds((T, D_MODEL), MXU)],
        compiler_params=_cp("parallel"))(h, p, g, wg, wp)


def _ple_bwd(gres, h, p, l, g, gp, e, wg):
    T = h.shape[0]
    tm = min(T, TOKEN_TILE)
    nt = T // tm
    shard = D_MODEL // N_DEV

    def body(gr_ref, h_ref, p_ref, g_ref, gp_ref, e_ref, wg_ref, dh_ref, dwg_ref, dwp_ref, dg_ref, accg, accp):
        i = pl.program_id(0)

        @pl.when(i == 0)
        def _():
            dg_ref[...] = jnp.zeros_like(dg_ref)
            accg[...] = jnp.zeros_like(accg)
            accp[...] = jnp.zeros_like(accp)

        x = h_ref[...]
        r = _rms_stat(x)
        gg = g_ref[...]
        hn = x * r * gg
        gate = _sigmoid(gp_ref[...].astype(F32))
        pe = p_ref[...]
        e = e_ref[...].astype(F32)
        gr = gr_ref[...]
        dgp = gr * e * gate * (1.0 - gate)
        dx, dg = _rms_bwd(x, r, gg, _mm_nt(dgp, wg_ref[...]))
        dh_ref[...] = gr + dx
        dg_ref[...] += dg
        accg[...] += _mm_tn(hn, dgp)
        accp[...] += _mm_tn(pe, gr * gate)

        @pl.when(i == nt - 1)
        def _():
            dwg_ref[...] = accg[...].astype(WIRE)
            for d in range(N_DEV):
                dwp_ref[d] = accp[:, d * shard:(d + 1) * shard].astype(WIRE)

    rows = pl.BlockSpec((tm, D_MODEL), lambda i: (i, 0))
    row = pl.BlockSpec((1, D_MODEL), lambda i: (0, 0))
    full = pl.BlockSpec((D_MODEL, D_MODEL), lambda i: (0, 0))
    return pl.pallas_call(
        body, name="ple_bwd", grid=(nt,),
        in_specs=[rows, rows, pl.BlockSpec((None, tm, PLE_DIM), lambda i: (l, i, 0)), row, rows, rows, full],
        out_specs=[rows, full, pl.BlockSpec((N_DEV, PLE_DIM, shard), lambda i: (0, 0, 0)), row],
        out_shape=[_sds((T, D_MODEL)), _sds((D_MODEL, D_MODEL), WIRE), _sds((N_DEV, PLE_DIM, shard), WIRE), _sds((1, D_MODEL))],
        scratch_shapes=[pltpu.VMEM((D_MODEL, D_MODEL), F32), pltpu.VMEM((PLE_DIM, D_MODEL), F32)],
        compiler_params=_cp("arbitrary"))(gres, h, p, g, gp, e, wg)


def _loss_head(h, target):
    T = h.shape[0]
    tm = min(T, 1024)

    def body(h_ref, t_ref, dh_ref, l_ref):
        @pl.when(pl.program_id(0) == 0)
        def _():
            l_ref[...] = jnp.zeros_like(l_ref)

        e = h_ref[...] - t_ref[...]
        dh_ref[...] = e * (1.0 / D_MODEL)
        l_ref[...] += jnp.zeros_like(l_ref) + 0.5 * jnp.sum(jnp.mean(e * e, axis=-1, keepdims=True))

    rows = pl.BlockSpec((tm, D_MODEL), lambda i: (i, 0))
    return pl.pallas_call(
        body, name="loss_head", grid=(T // tm,), in_specs=[rows, rows],
        out_specs=[rows, pl.BlockSpec((8, LANES), lambda i: (0, 0))], out_shape=[_sds((T, D_MODEL)), _sds((8, LANES))],
        compiler_params=_cp("arbitrary"))(h, target)


TN_ROWS = 2048

def _tn(a, b, m, n, *, bm, bn, n_major=False, dtype=F32, name="tn"):
    T = a.shape[0]
    tk = min(T, TN_ROWS)
    nk = T // tk
    assert m % bm == 0 and n % bn == 0 and (not n_major or bm == m)

    def body(a_ref, b_ref, o_ref, acc):
        k = pl.program_id(2)

        @pl.when(k == 0)
        def _():
            acc[...] = jnp.zeros_like(acc)

        acc[...] += _mm_tn(a_ref[...], b_ref[...])

        @pl.when(k == nk - 1)
        def _():
            o_ref[...] = acc[...].astype(dtype)

    if n_major:
        out_spec = pl.BlockSpec((None, bm, bn), lambda i, j, k: (j, 0, 0))
        out_shape = _sds((n // bn, m, bn), dtype)
    else:
        out_spec = pl.BlockSpec((bm, bn), lambda i, j, k: (i, j))
        out_shape = _sds((m, n), dtype)
    return pl.pallas_call(
        body, name=name, grid=(m // bm, n // bn, nk),
        in_specs=[pl.BlockSpec((tk, bm), lambda i, j, k: (k, i)), pl.BlockSpec((tk, bn), lambda i, j, k: (k, j))],
        out_specs=out_spec, out_shape=out_shape, scratch_shapes=[pltpu.VMEM((bm, bn), F32)],
        compiler_params=_cp("parallel", "parallel", "arbitrary"))(a, b)


def _row_tile(R, C):
    for cand in (512, 256, 128, 64, 32, 16, 8):
        if R % cand == 0 and cand * C * 4 <= 2 ** 20:
            return cand
    return R


def _sum_slots(land):
    S, R, C = land.shape
    tr = _row_tile(R, C)

    def body(l_ref, o_ref):
        acc = l_ref[0].astype(F32)
        for s in range(1, S):
            acc = acc + l_ref[s].astype(F32)
        o_ref[...] = acc

    return pl.pallas_call(
        body, name="sum_slots", grid=(R // tr,), in_specs=[pl.BlockSpec((S, tr, C), lambda i: (0, i, 0))],
        out_specs=pl.BlockSpec((tr, C), lambda i: (i, 0)), out_shape=_sds((R, C)), compiler_params=_cp("parallel"))(land)


def _sum_adamw(land, w, m, v):
    R, C = w.shape
    S = land.shape[0]
    tr = _row_tile(R, C)

    def body(l_ref, w_ref, m_ref, v_ref, g_ref, d_ref, nm_ref, nv_ref):
        gg = l_ref[0].astype(F32)
        for s in range(1, S):
            gg = gg + l_ref[s].astype(F32)
        g_ref[...] = gg
        nm = ADAM_B1 * m_ref[...] + (1.0 - ADAM_B1) * gg
        nv = ADAM_B2 * v_ref[...] + (1.0 - ADAM_B2) * (gg * gg)
        m_hat = nm / (1.0 - ADAM_B1 ** ADAM_STEP)
        v_hat = nv / (1.0 - ADAM_B2 ** ADAM_STEP)
        d_ref[...] = -ADAM_LR * (m_hat / (jnp.sqrt(v_hat) + ADAM_EPS) + ADAM_WD * w_ref[...])
        nm_ref[...] = nm
        nv_ref[...] = nv

    blk = pl.BlockSpec((tr, C), lambda i: (i, 0))
    return pl.pallas_call(
        body, name="sum_adamw", grid=(R // tr,), in_specs=[pl.BlockSpec((S, tr, C), lambda i: (0, i, 0))] + [blk] * 3,
        out_specs=[blk] * 4, out_shape=[_sds((R, C))] * 4, compiler_params=_cp("parallel"))(land, w, m, v)


def _all_to_all(pairs, name):
    n = len(pairs)

    def body(*refs):
        srcs, lands = refs[:n], refs[2 * n:3 * n]
        send, recv, loc = refs[3 * n:]
        x, y, c = lax.axis_index("x"), lax.axis_index("y"), lax.axis_index("c")
        me = 4 * x + 2 * y + c
        own = [pltpu.make_async_copy(pairs[t][2](srcs[t], me, me), pairs[t][3](lands[t], me), loc.at[t]) for t in range(n)]
        for cp in own:
            cp.start()
        sends, recvs = [], []
        for k in range(1, N_DEV):
            px, py, pc = x ^ (k >> 2), y ^ ((k >> 1) & 1), c ^ (k & 1)
            peer = 4 * px + 2 * py + pc
            for t in range(n):
                src = pairs[t][2](srcs[t], me, peer)
                cp = pltpu.make_async_remote_copy(src_ref=src, dst_ref=pairs[t][3](lands[t], me), send_sem=send.at[t, k],
                                                  recv_sem=recv.at[t, k], device_id=(px, py, pc), device_id_type=MESH)
                cp.start()
                sends.append(cp)
                recvs.append(pltpu.make_async_remote_copy(src_ref=src, dst_ref=pairs[t][3](lands[t], peer), send_sem=send.at[t, k],
                                                          recv_sem=recv.at[t, k], device_id=(px, py, pc), device_id_type=MESH))
        for cp in recvs:
            cp.wait_recv()
        for cp in sends:
            cp.wait_send()
        for cp in own:
            cp.wait()

    anyspec = pl.BlockSpec(memory_space=pl.ANY)
    lands = [pr[1] for pr in pairs]
    return pl.pallas_call(
        body, name=name, in_specs=[anyspec] * (2 * n), out_specs=[anyspec] * n,
        out_shape=[_sds(a.shape, a.dtype) for a in lands], input_output_aliases={n + t: t for t in range(n)},
        scratch_shapes=[pltpu.SemaphoreType.DMA((n, N_DEV)), pltpu.SemaphoreType.DMA((n, N_DEV)), pltpu.SemaphoreType.DMA((n,))],
        compiler_params=pltpu.CompilerParams(has_side_effects=True))(*[pr[0] for pr in pairs], *lands)


def _gather_whole(src):
    land = lax.empty((N_DEV,) + src.shape, src.dtype)
    return _all_to_all([(src, land, lambda ref, me, peer: ref, lambda ref, sender: ref.at[sender])], "gather_small_grads")[0]


_HBM = pl.BlockSpec(memory_space=pltpu.HBM)
_SEM = pl.BlockSpec(memory_space=pltpu.SEMAPHORE)
_EFFECT = pltpu.SideEffectType.DATAFLOW_SIDE_EFFECTING


def _peers():
    x, y, c = lax.axis_index("x"), lax.axis_index("y"), lax.axis_index("c")
    out = []
    for k in range(1, N_DEV):
        px, py, pc = x ^ (k >> 2), y ^ ((k >> 1) & 1), c ^ (k & 1)
        out.append((k, (px, py, pc), 4 * px + 2 * py + pc))
    return 4 * x + 2 * y + c, out


def _route(mode, layer):
    if mode == "all":
        return (lambda ref, peer: ref), (lambda ref, sender: ref.at[sender])
    return (lambda ref, peer: ref.at[peer]), (lambda ref, sender: ref.at[sender, layer])


def _split_start(srcs, lands, modes, layer, after, name):
    n = len(srcs)
    routes = [_route(m, layer) for m in modes]

    def body(*refs):
        src, land = refs[:n], refs[n:2 * n]
        send, recv, token = refs[2 * n + 1], refs[2 * n + 2], refs[-1]
        me, peers = _peers()
        for k, dev, peer in peers:
            for t, (src_of, dst_of) in enumerate(routes):
                pltpu.make_async_remote_copy(src_ref=src_of(src[t], peer), dst_ref=dst_of(land[t], me), send_sem=send.at[t * N_DEV + k],
                                             recv_sem=recv.at[t * N_DEV + k], device_id=dev, device_id_type=MESH).start()
        token[...] = jnp.zeros_like(token)

    bufs = list(srcs) + list(lands)
    outs = pl.pallas_call(
        body, name=name,
        out_shape=(pltpu.SemaphoreType.DMA((n * N_DEV,)), pltpu.SemaphoreType.DMA((n * N_DEV,)),
                   *[pltpu.HBM(a.shape, a.dtype) for a in bufs], _sds((8, LANES))),
        in_specs=[_HBM] * (2 * n) + [pl.BlockSpec(memory_space=pl.ANY)],
        out_specs=(_SEM, _SEM, *[_HBM] * (2 * n), pl.BlockSpec(memory_space=pltpu.VMEM)),
        input_output_aliases={i: 2 + i for i in range(2 * n)},
        compiler_params=pltpu.CompilerParams(has_side_effects=_EFFECT),
    )(*[pltpu.with_memory_space_constraint(a, pltpu.HBM) for a in bufs], after)
    return outs[0], outs[1], list(outs[2:2 + n]), list(outs[2 + n:2 + 2 * n]), outs[-1]


def _split_wait(send, recv, srcs, lands, modes, layer, after, name):
    n = len(srcs)
    routes = [_route(m, layer) for m in modes]

    def body(*refs):
        src, land = refs[:n], refs[n:2 * n]
        send_r, recv_r = refs[2 * n], refs[2 * n + 1]
        _, peers = _peers()
        for k, dev, peer in peers:
            for t, (src_of, dst_of) in enumerate(routes):
                cp = pltpu.make_async_remote_copy(src_ref=src_of(src[t], peer), dst_ref=dst_of(land[t], peer), send_sem=send_r.at[t * N_DEV + k],
                                                  recv_sem=recv_r.at[t * N_DEV + k], device_id=dev, device_id_type=MESH)
                cp.wait_send()
                cp.wait_recv()

    bufs = list(srcs) + list(lands)
    outs = pl.pallas_call(
        body, name=name, out_shape=tuple(pltpu.HBM(a.shape, a.dtype) for a in bufs),
        in_specs=[_HBM] * (2 * n) + [_SEM, _SEM, pl.BlockSpec(memory_space=pl.ANY)], out_specs=[_HBM] * (2 * n),
        input_output_aliases={i: i for i in range(2 * n)},
        compiler_params=pltpu.CompilerParams(has_side_effects=_EFFECT),
    )(*bufs, send, recv, after)
    return list(outs[:n]), list(outs[n:])


SHARDED = ("w_in", "glu_w1", "glu_w2", "w_out", "w_ff1", "w_ff2", "w_ple_gate", "w_ple_proj")
SMALL = ("attn_norm_g", "gmlp_ln_g", "gmlp_ln_b", "gmlp_ws", "gmlp_bs", "q_norm_g", "k_norm_g", "sinks", "ssm_a_re", "ssm_a_im",
         "ssm_log_dt", "ssm_b_re", "ssm_b_im", "ssm_c_re", "ssm_c_im", "ssm_d", "mix_out_g", "mlp_norm_g", "ple_norm_g")
WEIGHTS = ("attn_norm_g", "w_in", "gmlp_ln_g", "gmlp_ln_b", "gmlp_ws", "gmlp_bs", "q_norm_g", "k_norm_g", "sinks", "ssm_a_re", "ssm_a_im",
           "ssm_log_dt", "ssm_b_re", "ssm_b_im", "ssm_c_re", "ssm_c_im", "ssm_d", "glu_w1", "glu_w2", "mix_out_g", "w_out", "mlp_norm_g",
           "w_ff1", "w_ff2", "ple_norm_g", "w_ple_gate", "w_ple_proj")
FLAT_COLS = 1024


PACK_TILE_ROWS = 8
PACK_ROWS_MULTIPLE = PACK_TILE_ROWS * N_DEV


def _packed_rows(shape):
    return -(-math.prod(shape) // (PACK_TILE_ROWS * FLAT_COLS)) * PACK_TILE_ROWS


def _pack(arrs, dtype):
    blocks = []
    for a in arrs:
        flat = a.astype(dtype).reshape(-1)
        pad = _packed_rows(a.shape) * FLAT_COLS - flat.shape[0]
        if pad:
            flat = jnp.concatenate([flat, jnp.zeros((pad,), dtype)])
        blocks.append(flat.reshape(-1, FLAT_COLS))
    rows = sum(b.shape[0] for b in blocks)
    if rows % PACK_ROWS_MULTIPLE:
        blocks.append(jnp.zeros((PACK_ROWS_MULTIPLE - rows % PACK_ROWS_MULTIPLE, FLAT_COLS), dtype))
    return jnp.concatenate(blocks, axis=0)


def _unpack(flat, shapes):
    out, r = [], 0
    for s in shapes:
        nr = _packed_rows(s)
        out.append(flat[r:r + nr].reshape(-1)[:math.prod(s)].reshape(s))
        r += nr
    return out


def _from_col_major(s):
    n, rows, cs = s.shape
    return s.transpose(1, 0, 2).reshape(rows, n * cs)


EARLY = ("w_in", "glu_w1", "glu_w2")
LATE = ("w_out", "w_ff1", "w_ff2", "w_ple_gate", "w_ple_proj")
GRADS_MID = ("w_ff1", "w_ff2", "w_ple_gate", "w_ple_proj")
GRADS_END = ("w_in", "glu_w1", "glu_w2", "w_out")


def _layer_fwd(h, p, l, cs, sn, W, late_weights, S, sp):
    z = _inproj_fwd(h, S["attn_norm_g"], W["w_in"])
    ya = _gmlp_fwd(z, S["lng"], S["lnb"], S["gmlp_ws"], S["bsx"])
    yb = _attn_fwd(z, cs, sn, S["qg"], S["kg"], S["sinks"])
    yc, xr, xi = _ssm_fwd(z, sp["bbd"], sp["pwr"], sp["pwi"], sp["cbd"], S["ssm_d"], W["glu_w1"], W["glu_w2"])
    late, token = late_weights(ya[0:8, 0:LANES] + yb[0:8, 0:LANES] + yc[0:8, 0:LANES])
    W = {**W, **late}
    h1 = _merge_fwd(h, ya, yb, yc, _behind(S["mix_out_g"], token), W["w_out"])
    h2, r = _mlp_fwd(h1, S["mlp_norm_g"], W["w_ff1"], W["w_ff2"])
    h3, gp, e = _ple_fwd(h2, p, l, S["ple_norm_g"], W["w_ple_gate"], W["w_ple_proj"])
    return h3, dict(h=h, z=z, ya=ya, yb=yb, yc=yc, xr=xr, xi=xi, h1=h1, r=r, h2=h2, gp=gp, e=e), W


def _layer_bwd(g3, p, l, cs, sn, W, S, sp, A, raw, mid_bwd):
    G = {}
    g2, dwg, G["w_ple_proj"], G["ple_norm_g"] = _ple_bwd(g3, A["h2"], p, l, S["ple_norm_g"], A["gp"], A["e"], W["w_ple_gate"])
    G["w_ple_gate"] = dwg.reshape(N_DEV, -1, D_MODEL)
    g1, hn, da, G["mlp_norm_g"] = _mlp_bwd(g2, A["h1"], S["mlp_norm_g"], A["r"], W["w_ff1"], W["w_ff2"])
    G["w_ff1"] = _tn(hn, da, D_MODEL, D_FF, bm=D_MODEL, bn=FF_BLOCK, n_major=True, dtype=WIRE, name="tn_ff1")
    G["w_ff2"] = _tn(A["r"], g2, D_FF, D_MODEL, bm=D_MODEL, bn=D_MODEL, dtype=WIRE, name="tn_ff2").reshape(N_DEV, -1, D_MODEL)
    token = mid_bwd(g1, G)
    dya, dyb, dyc, dwo, G["mix_out_g"] = _merge_bwd(g1, A["ya"], A["yb"], A["yc"], _behind(S["mix_out_g"], token), W["w_out"])
    G["w_out"] = dwo.reshape(N_DEV, -1, D_MODEL)
    dzc, dw1, dw2, dcr, dci, dbbd, dlr, dli, dd = _ssm_bwd(dyc, A["z"], A["xr"], A["xi"], sp["bbd"], sp["pwr"], sp["pwi"], sp["cbd"],
                                                          S["ssm_d"], W["glu_w1"], W["glu_w2"])
    G["glu_w1"] = dw1.astype(WIRE).reshape(N_DEV, -1, C_WIDTH)
    G["glu_w2"] = dw2.astype(WIRE).reshape(N_DEV, -1, C_WIDTH)
    dare, daim, dldt, dbtr, dbti, dcre, dcim = _ssm_param_bwd(
        raw["are"], raw["aim"], raw["ldt"], raw["are_x"], raw["aim_x"], raw["ldt_x"], raw["btr"], raw["bti"],
        dlr.reshape(C_GROUPS, C_STATE), dli.reshape(C_GROUPS, C_STATE), dbbd, dcr, dci)
    G["ssm_a_re"], G["ssm_a_im"], G["ssm_log_dt"] = dare, daim, dldt[:, 0]
    G["ssm_b_re"] = dbtr.reshape(C_GROUPS, C_GROUP, C_STATE).transpose(0, 2, 1)
    G["ssm_b_im"] = dbti.reshape(C_GROUPS, C_GROUP, C_STATE).transpose(0, 2, 1)
    G["ssm_c_re"] = dcre.reshape(C_GROUPS, C_GROUP, C_STATE)
    G["ssm_c_im"] = dcim.reshape(C_GROUPS, C_GROUP, C_STATE)
    G["ssm_d"] = dd.reshape(C_GROUPS, C_GROUP)
    dzq, dzk, dzv, dqg, dkg, dsk = _attn_bwd(A["z"], cs, sn, S["qg"], S["kg"], S["sinks"], A["yb"], dyb)
    G["q_norm_g"] = dqg[0, :HEAD_DIM] + dqg[0, HEAD_DIM:]
    G["k_norm_g"] = dkg[0, :HEAD_DIM] + dkg[0, HEAD_DIM:]
    G["sinks"] = dsk[:, 0]
    dza, dws, dbs, dlng, dlnb = _gmlp_bwd(A["z"], dya, S["lng"], S["lnb"], S["gmlp_ws"], S["bsx"])
    G["gmlp_ws"] = dws
    G["gmlp_bs"] = dbs[:, :, 0]
    G["gmlp_ln_g"] = dlng.reshape(A_HEADS, 2, HEAD_DIM)[:, 1]
    G["gmlp_ln_b"] = dlnb.reshape(A_HEADS, 2, HEAD_DIM)[:, 1]
    g0, G["w_in"], G["attn_norm_g"] = _inproj_bwd(g1, A["h"], S["attn_norm_g"], W["w_in"], dza, dzq, dzk, dzv, dzc)
    return g0, G


def _small_layouts(P, l):
    def row(a):
        return a.reshape(1, -1)

    zeros = jnp.zeros((A_HEADS, HEAD_DIM), F32)
    S = dict(
        attn_norm_g=row(P["attn_norm_g"][l]), mix_out_g=row(P["mix_out_g"][l]), mlp_norm_g=row(P["mlp_norm_g"][l]),
        ple_norm_g=row(P["ple_norm_g"][l]),
        lng=jnp.stack([zeros, P["gmlp_ln_g"][l]], axis=1).reshape(1, IN_A),
        lnb=jnp.stack([zeros, P["gmlp_ln_b"][l]], axis=1).reshape(1, IN_A),
        gmlp_ws=P["gmlp_ws"][l],
        bsx=jnp.broadcast_to(P["gmlp_bs"][l][:, :, None], (A_HEADS, CHUNK, CHUNK)),
        qg=jnp.tile(P["q_norm_g"][l], 2).reshape(1, LANES), kg=jnp.tile(P["k_norm_g"][l], 2).reshape(1, LANES),
        sinks=jnp.broadcast_to(P["sinks"][l][:, None], (8, LANES)),
        ssm_d=row(P["ssm_d"][l]),
    )
    are, aim = P["ssm_a_re"][l], P["ssm_a_im"][l]
    ldt = jnp.broadcast_to(P["ssm_log_dt"][l][:, None], (C_GROUPS, C_STATE))
    raw = dict(
        are=are, aim=aim, ldt=ldt,
        are_x=jnp.repeat(are, C_GROUP, axis=0), aim_x=jnp.repeat(aim, C_GROUP, axis=0), ldt_x=jnp.repeat(ldt, C_GROUP, axis=0),
        btr=P["ssm_b_re"][l].transpose(0, 2, 1).reshape(C_WIDTH, C_STATE), bti=P["ssm_b_im"][l].transpose(0, 2, 1).reshape(C_WIDTH, C_STATE),
        cre=P["ssm_c_re"][l].reshape(C_WIDTH, C_STATE), cim=P["ssm_c_im"][l].reshape(C_WIDTH, C_STATE),
    )
    return S, raw


def _ssm_prep_layer(raw):
    bbd, cbd, pwr, pwi = _ssm_prep(raw["are"].reshape(1, N_STATE), raw["aim"].reshape(1, N_STATE), raw["ldt"].reshape(1, N_STATE),
                                   raw["are_x"], raw["aim_x"], raw["ldt_x"], raw["btr"], raw["bti"], raw["cre"], raw["cim"])
    return dict(bbd=bbd, cbd=cbd, pwr=pwr, pwi=pwi)


def _behind(row, token):
    return row if token is None else row + token[0:1, 0:1]


def _local_step(x, p, positions, target, P, weights_of, mid_bwd, after_bwd):
    inv = 1.0 / (ROPE_THETA ** (jnp.arange(0, HEAD_DIM, 2, dtype=F32) / HEAD_DIM))
    cs, sn = _rope_tables(positions.reshape(-1, 1), jnp.tile(inv, 4).reshape(1, LANES))
    h = x
    acts, smalls, weights = [], [], []
    for l in range(DEPTH):
        W, late_weights, token = weights_of(l, h)
        S, raw = _small_layouts(P, l)
        sp = _ssm_prep_layer(raw)
        h, A, W = _layer_fwd(h, p, l, cs, sn, W, late_weights, {**S, "attn_norm_g": _behind(S["attn_norm_g"], token)}, sp)
        acts.append(A)
        smalls.append((S, raw, sp))
        weights.append(W)
    g, lsum = _loss_head(h, target)
    grads = [None] * DEPTH
    token = None
    for l in reversed(range(DEPTH)):
        S, raw, sp = smalls[l]
        g, grads[l] = _layer_bwd(g, p, l, cs, sn, weights[l], {**S, "ple_norm_g": _behind(S["ple_norm_g"], token)}, sp, acts[l], raw,
                                 functools.partial(mid_bwd, l))
        token = after_bwd(l, g, grads[l])
    return lsum[0, 0], g, grads


def _layer_weights(g):
    layout = dict(
        w_in=_from_col_major, glu_w1=lambda a: a.reshape(C_WIDTH, C_WIDTH), glu_w2=lambda a: a.reshape(C_WIDTH, C_WIDTH),
        w_out=lambda a: a.reshape(D_MODEL, D_MODEL), w_ff1=lambda a: a, w_ff2=lambda a: a.reshape(D_FF, D_MODEL),
        w_ple_gate=lambda a: a.reshape(D_MODEL, D_MODEL), w_ple_proj=_from_col_major)
    return {n: layout[n](a) for n, a in g.items()}


def kernel(x, p, positions, attn_norm_g, w_in, gmlp_ln_g, gmlp_ln_b, gmlp_ws, gmlp_bs, q_norm_g, k_norm_g, sinks, ssm_a_re, ssm_a_im, ssm_log_dt, ssm_b_re, ssm_b_im, ssm_c_re, ssm_c_im, ssm_d, glu_w1, glu_w2, mix_out_g, w_out, mlp_norm_g, w_ff1, w_ff2, ple_norm_g, w_ple_gate, w_ple_proj, loss_target, m_attn_norm_g, m_w_in, m_gmlp_ln_g, m_gmlp_ln_b, m_gmlp_ws, m_gmlp_bs, m_q_norm_g, m_k_norm_g, m_sinks, m_ssm_a_re, m_ssm_a_im, m_ssm_log_dt, m_ssm_b_re, m_ssm_b_im, m_ssm_c_re, m_ssm_c_im, m_ssm_d, m_glu_w1, m_glu_w2, m_mix_out_g, m_w_out, m_mlp_norm_g, m_w_ff1, m_w_ff2, m_ple_norm_g, m_w_ple_gate, m_w_ple_proj, v_attn_norm_g, v_w_in, v_gmlp_ln_g, v_gmlp_ln_b, v_gmlp_ws, v_gmlp_bs, v_q_norm_g, v_k_norm_g, v_sinks, v_ssm_a_re, v_ssm_a_im, v_ssm_log_dt, v_ssm_b_re, v_ssm_b_im, v_ssm_c_re, v_ssm_c_im, v_ssm_d, v_glu_w1, v_glu_w2, v_mix_out_g, v_w_out, v_mlp_norm_g, v_w_ff1, v_w_ff2, v_ple_norm_g, v_w_ple_gate, v_w_ple_proj):
    env = dict(locals())
    P = {n: env[n] for n in WEIGHTS}
    M = {n: env["m_" + n] for n in WEIGHTS}
    V = {n: env["v_" + n] for n in WEIGHTS}
    return _step(x, p, positions, loss_target, P, M, V)


def _step(x, p, positions, loss_target, P, M, V):
    small_shapes = [P[n].shape for n in SMALL]
    me = 4 * lax.axis_index("x") + 2 * lax.axis_index("y") + lax.axis_index("c")
    nothing = jnp.zeros((8, LANES), F32)

    def put(land, own, lead):
        return lax.dynamic_update_slice(land, own.reshape((1,) * len(lead) + own.shape), tuple(lead) + (0,) * own.ndim)

    def gather_start(l, names, after, tag):
        shards = [P[n][l].astype(WIRE) for n in names]
        lands = [lax.empty((N_DEV,) + s.shape, WIRE) for s in shards]
        send, recv, shards, lands, token = _split_start(shards, lands, ["all"] * len(names), 0, after, f"gather_start_{l}{tag}")
        return dict(names=names, send=send, recv=recv, shards=shards, lands=lands, token=token, name=f"gather_wait_{l}{tag}")

    def gather_wait(f, after):
        shards, lands = _split_wait(f["send"], f["recv"], f["shards"], f["lands"], ["all"] * len(f["names"]), 0, after, f["name"])
        return dict(zip(f["names"], [put(ld, sh, (me,)) for sh, ld in zip(shards, lands)]))

    first = gather_start(0, EARLY, nothing, "a")
    flying = {0: (first, gather_start(0, LATE, first["token"], "b"))}

    def weights_of(l, h):
        fa, fb = flying.pop(l)
        got = gather_wait(fa, h)
        if fb is None:
            token = None
            if l + 1 < DEPTH:
                flying[l + 1] = (gather_start(l + 1, SHARDED, got["w_in"], ""), None)
                token = flying[l + 1][0]["token"]
            W = _layer_weights(got)
            return {n: W[n] for n in EARLY}, (lambda after: ({n: W[n] for n in LATE}, None)), token

        def late_weights(after):
            late = gather_wait(fb, after)
            flying[l + 1] = (gather_start(l + 1, SHARDED, late["w_out"], ""), None)
            return _layer_weights(late), flying[l + 1][0]["token"]

        return _layer_weights(got), late_weights, fb["token"]

    grad_lands = {n: lax.empty((N_DEV,) + P[n].shape, WIRE) for n in SHARDED}
    sent = []

    def scatter_start(l, names, parts, lands, tag):
        send, recv, parts, lands, token = _split_start(parts, lands, ["own"] * len(parts), l, nothing, f"scatter_start_{l}{tag}")
        sent.append(dict(l=l, names=names, send=send, recv=recv, parts=parts, lands=lands, name=f"scatter_wait_{l}{tag}"))
        return token

    def scatter_wait(after):
        f = sent.pop(0)
        parts, lands = _split_wait(f["send"], f["recv"], f["parts"], f["lands"], ["own"] * len(f["parts"]), f["l"], after, f["name"])
        lands = [put(ld, lax.dynamic_index_in_dim(part, me, 0, keepdims=False), (me, f["l"])) for part, ld in zip(parts, lands)]
        return dict(zip(f["names"], lands))

    def mid_bwd(l, g1, G):
        if l > 0:
            return None
        grad_lands.update(scatter_wait(g1))
        return scatter_start(0, GRADS_MID, [G[n] for n in GRADS_MID], [grad_lands[n] for n in GRADS_MID], "a")

    small_land = []

    def after_bwd(l, g, G):
        if l > 0:
            if sent:
                grad_lands.update(scatter_wait(g))
            return scatter_start(l, SHARDED, [G[n] for n in SHARDED], [grad_lands[n] for n in SHARDED], "")
        sflat = _pack([jnp.stack([grads_of[k][n] for k in range(DEPTH)]) for n in SMALL], F32)
        sparts = sflat.reshape(N_DEV, -1, FLAT_COLS)
        small_land.append(sflat.shape)
        return scatter_start(0, GRADS_END + ("small",), [G[n] for n in GRADS_END] + [sparts],
                             [grad_lands[n] for n in GRADS_END] + [lax.empty((N_DEV, 1) + sparts.shape[1:], F32)], "b")

    grads_of = {}

    def after_bwd_recording(l, g, G):
        grads_of[l] = G
        grads_of["token"] = after_bwd(l, g, G)
        return grads_of["token"]

    lsum, gx, grads = _local_step(x[0], p[:, 0], positions[0], loss_target[0], P, weights_of, mid_bwd, after_bwd_recording)
    G, delta, new_m, new_v = {}, {}, {}, {}

    def update(names):
        for n in names:
            shp = P[n].shape
            res = _sum_adamw(grad_lands[n].reshape(N_DEV, -1, shp[-1]), *(a.reshape(-1, shp[-1]) for a in (P[n], M[n], V[n])))
            G[n], delta[n], new_m[n], new_v[n] = (a.reshape(shp) for a in res)

    grad_lands.update(scatter_wait(grads_of["token"]))
    update(GRADS_MID)
    last = scatter_wait(sum(new_v[n].reshape(-1, LANES)[:8] for n in GRADS_MID))
    small_parts = last.pop("small")
    grad_lands.update(last)
    update(GRADS_END)
    small_sum = _gather_whole(_sum_slots(small_parts[:, 0])).reshape((1,) + small_land[0])
    res = _sum_adamw(small_sum, _pack([P[n] for n in SMALL], F32), _pack([M[n] for n in SMALL], F32), _pack([V[n] for n in SMALL], F32))
    for dst, flat in zip((G, delta, new_m, new_v), res):
        dst.update(zip(SMALL, _unpack(flat, small_shapes)))
    loss = lax.psum(lsum, ("x", "y", "c"))
    return (loss, gx[None], *[G[n] for n in WEIGHTS], *[delta[n] for n in WEIGHTS], *[new_m[n] for n in WEIGHTS], *[new_v[n] for n in WEIGHTS])
```

```python
import functools
import math

import jax
import jax.numpy as jnp
from jax import lax
from jax.experimental import pallas as pl
from jax.experimental.pallas import tpu as pltpu

F32 = jnp.float32
MXU = jnp.bfloat16
WIRE = jnp.bfloat16

D_MODEL = 1024
DEPTH = 4
HEAD_DIM = 64
A_WIDTH = 256
A_HEADS = 4
CHUNK = 128
B_WIDTH = 512
WINDOW = 128
C_WIDTH = 256
C_GROUP = 16
C_GROUPS = 16
C_STATE = 64
N_STATE = C_GROUPS * C_STATE
IN_A = 2 * A_WIDTH
KV_WIDTH = 2 * HEAD_DIM
IN_COLS = IN_A + B_WIDTH + 2 * KV_WIDTH + C_WIDTH
Q_BLOCK = IN_A // B_WIDTH
K_BLOCK = (IN_A + B_WIDTH) // KV_WIDTH
V_BLOCK = K_BLOCK + 1
C_BLOCK = (IN_A + B_WIDTH + 2 * KV_WIDTH) // C_WIDTH
TOKEN_TILE = 512
D_FF = 4096
PLE_DIM = 256
EPS = 1e-6
ROPE_THETA = 10000.0
SCALE = HEAD_DIM ** -0.5
NEG = -1e30
N_DEV = 8

ADAM_LR = 0.001
ADAM_B1 = 0.9
ADAM_B2 = 0.999
ADAM_EPS = 1e-08
ADAM_WD = 0.01
ADAM_STEP = 10

V7X_VMEM_BYTES = 64 * 2 ** 20
VMEM_LIMIT = V7X_VMEM_BYTES - 8 * 2 ** 20
LANES = 128

MESH = pl.DeviceIdType.MESH


def _cp(*sem):
    return pltpu.CompilerParams(dimension_semantics=sem, vmem_limit_bytes=VMEM_LIMIT)


def _sds(shape, dtype=F32):
    return jax.ShapeDtypeStruct(shape, dtype)


def _mm(a, b):
    return jnp.dot(a.astype(MXU), b.astype(MXU), preferred_element_type=F32)


def _mm_nt(a, b):
    return lax.dot_general(a.astype(MXU), b.astype(MXU), (((1,), (1,)), ((), ())), preferred_element_type=F32)


def _mm_tn(a, b):
    return lax.dot_general(a.astype(MXU), b.astype(MXU), (((0,), (0,)), ((), ())), preferred_element_type=F32)


def _lane(shape):
    return lax.broadcasted_iota(jnp.int32, shape, len(shape) - 1)


def _row(shape):
    return lax.broadcasted_iota(jnp.int32, shape, 0)


_GELU_C = math.sqrt(2.0 / math.pi)


def _gelu(x):
    return 0.5 * x * (1.0 + jnp.tanh(_GELU_C * (x + 0.044715 * (x * x * x))))


def _gelu_grad(x):
    t = jnp.tanh(_GELU_C * (x + 0.044715 * (x * x * x)))
    return 0.5 * (1.0 + t) + 0.5 * x * (1.0 - t * t) * (_GELU_C * (1.0 + 3.0 * 0.044715 * (x * x)))


def _sigmoid(x):
    return 1.0 / (1.0 + jnp.exp(-x))


def _rms_stat(x):
    return lax.rsqrt(jnp.mean(x * x, axis=-1, keepdims=True) + EPS)


def _rms_bwd(x, r, g, dy):
    xh = x * r
    dxh = dy * g
    dx = r * (dxh - xh * jnp.mean(dxh * xh, axis=-1, keepdims=True))
    return dx, jnp.sum(dy * xh, axis=0, keepdims=True)


def _tril(w):
    return jnp.where(_row(w.shape) >= _lane(w.shape), w, 0.0)


def _swap64(x):
    return pltpu.roll(x, HEAD_DIM, 1)


def _group_sum64(x, lo):
    s_lo = jnp.sum(jnp.where(lo, x, 0.0), axis=-1, keepdims=True)
    s_hi = jnp.sum(jnp.where(lo, 0.0, x), axis=-1, keepdims=True)
    return jnp.where(lo, s_lo, s_hi)


def _partner(x):
    n = x.shape[-1]
    first = (_lane(x.shape) % HEAD_DIM) < HEAD_DIM // 2
    return jnp.where(first, pltpu.roll(x, n - HEAD_DIM // 2, 1), pltpu.roll(x, HEAD_DIM // 2, 1))


def _rope(y, cs, sn):
    return y * cs + _partner(y) * sn


def _rope_bwd(d, cs, sn):
    return d * cs + _partner(d * sn)


def _qk_norm_rope(x, g, cs, sn):
    lo = _lane(x.shape) < HEAD_DIM
    r = lax.rsqrt(_group_sum64(x * x, lo) * (1.0 / HEAD_DIM) + EPS)
    xh = x * r
    return _rope(xh * g, cs, sn), xh, r


def _qk_norm_rope_bwd(xh, r, g, cs, sn, d):
    lo = _lane(xh.shape) < HEAD_DIM
    dy = _rope_bwd(d, cs, sn)
    dxh = dy * g
    m = _group_sum64(dxh * xh, lo) * (1.0 / HEAD_DIM)
    return r * (dxh - xh * m), jnp.sum(dy * xh, axis=0, keepdims=True)


def _gmlp_head(blk, g, b):
    hi = _lane(blk.shape) >= HEAD_DIM
    mu = jnp.sum(jnp.where(hi, blk, 0.0), axis=-1, keepdims=True) * (1.0 / HEAD_DIM)
    xc = jnp.where(hi, blk - mu, 0.0)
    rstd = lax.rsqrt(jnp.sum(xc * xc, axis=-1, keepdims=True) * (1.0 / HEAD_DIM) + EPS)
    vhat = xc * rstd
    return vhat * g + b, vhat, rstd


def _rope_tables(pos_col, inv_row):
    T = pos_col.shape[0]
    tm = min(T, 1024)

    def body(p_ref, inv_ref, cs_ref, sn_ref):
        ang = p_ref[...].astype(F32) * inv_ref[...]
        s = jnp.sin(ang)
        cs_ref[...] = jnp.cos(ang)
        sn_ref[...] = jnp.where((_lane(ang.shape) % HEAD_DIM) < HEAD_DIM // 2, -s, s)

    blk = pl.BlockSpec((tm, LANES), lambda i: (i, 0))
    return pl.pallas_call(
        body, name="rope_tables", grid=(T // tm,),
        in_specs=[pl.BlockSpec((tm, 1), lambda i: (i, 0)), pl.BlockSpec((1, LANES), lambda i: (0, 0))],
        out_specs=[blk, blk], out_shape=[_sds((T, LANES))] * 2, compiler_params=_cp("parallel"))(pos_col, inv_row)


def _inproj_fwd(h, g, w):
    T = h.shape[0]
    tm = min(T, TOKEN_TILE)

    def body(h_ref, g_ref, w_ref, z_ref):
        x = h_ref[...]
        z_ref[...] = _mm(x * _rms_stat(x) * g_ref[...], w_ref[...])

    return pl.pallas_call(
        body, name="inproj_fwd", grid=(T // tm,),
        in_specs=[pl.BlockSpec((tm, D_MODEL), lambda i: (i, 0)), pl.BlockSpec((1, D_MODEL), lambda i: (0, 0)),
                  pl.BlockSpec((D_MODEL, IN_COLS), lambda i: (0, 0))],
        out_specs=pl.BlockSpec((tm, IN_COLS), lambda i: (i, 0)), out_shape=_sds((T, IN_COLS)),
        compiler_params=_cp("parallel"))(h, g, w)


def _inproj_bwd(gres, h, g, w, dza, dzq, dzk, dzv, dzc):
    T = h.shape[0]
    tm = min(T, TOKEN_TILE)
    nt = T // tm
    shard = IN_COLS // N_DEV

    def body(gr_ref, h_ref, g_ref, w_ref, a_ref, q_ref, k_ref, v_ref, c_ref, dh_ref, dw_ref, dg_ref, acc):
        i = pl.program_id(0)

        @pl.when(i == 0)
        def _():
            dg_ref[...] = jnp.zeros_like(dg_ref)
            acc[...] = jnp.zeros_like(acc)

        x = h_ref[...]
        r = _rms_stat(x)
        gg = g_ref[...]
        dz = jnp.concatenate([a_ref[...], q_ref[...], k_ref[...], v_ref[...], c_ref[...]], axis=1)
        dxn = _mm_nt(dz, w_ref[...])
        dx, dg = _rms_bwd(x, r, gg, dxn)
        dh_ref[...] = gr_ref[...] + dx
        dg_ref[...] += dg
        acc[...] += _mm_tn(x * r * gg, dz)

        @pl.when(i == nt - 1)
        def _():
            for d in range(N_DEV):
                dw_ref[d] = acc[:, d * shard:(d + 1) * shard].astype(WIRE)

    def rows(w_):
        return pl.BlockSpec((tm, w_), lambda i: (i, 0))

    row = pl.BlockSpec((1, D_MODEL), lambda i: (0, 0))
    return pl.pallas_call(
        body, name="inproj_bwd", grid=(nt,),
        in_specs=[rows(D_MODEL), rows(D_MODEL), row, pl.BlockSpec((D_MODEL, IN_COLS), lambda i: (0, 0)),
                  rows(IN_A), rows(B_WIDTH), rows(KV_WIDTH), rows(KV_WIDTH), rows(C_WIDTH)],
        out_specs=[rows(D_MODEL), pl.BlockSpec((N_DEV, D_MODEL, shard), lambda i: (0, 0, 0)), row],
        out_shape=[_sds((T, D_MODEL)), _sds((N_DEV, D_MODEL, shard), WIRE), _sds((1, D_MODEL))],
        scratch_shapes=[pltpu.VMEM((D_MODEL, IN_COLS), F32)],
        compiler_params=_cp("arbitrary"))(gres, h, g, w, dza, dzq, dzk, dzv, dzc)


def _gmlp_fwd(z, lng, lnb, ws, bsx):
    T = z.shape[0]
    tm = min(T, TOKEN_TILE)
    nc = tm // CHUNK

    def body(z_ref, g_ref, b_ref, w_ref, bs_ref, ya_ref):
        zg = _gelu(z_ref[...])
        lo = _lane((tm, LANES)) < HEAD_DIM
        prods = []
        for hd in range(A_HEADS):
            sl = slice(hd * LANES, (hd + 1) * LANES)
            blk = zg[:, sl]
            vn, _, _ = _gmlp_head(blk, g_ref[:, sl], b_ref[:, sl])
            wm = _tril(w_ref[hd])
            sv = jnp.concatenate([_mm(wm, vn[c * CHUNK:(c + 1) * CHUNK]) + bs_ref[hd] for c in range(nc)], axis=0)
            prods.append(blk * _swap64(sv))
        ya_ref[:, 0:LANES] = jnp.where(lo, prods[0], _swap64(prods[1]))
        ya_ref[:, LANES:2 * LANES] = jnp.where(lo, prods[2], _swap64(prods[3]))

    row = pl.BlockSpec((1, IN_A), lambda i: (0, 0))
    mat = pl.BlockSpec((A_HEADS, CHUNK, CHUNK), lambda i: (0, 0, 0))
    return pl.pallas_call(
        body, name="gmlp_fwd", grid=(T // tm,),
        in_specs=[pl.BlockSpec((tm, IN_A), lambda i: (i, 0)), row, row, mat, mat],
        out_specs=pl.BlockSpec((tm, A_WIDTH), lambda i: (i, 0)), out_shape=_sds((T, A_WIDTH)),
        compiler_params=_cp("parallel"))(z, lng, lnb, ws, bsx)


def _gmlp_bwd(z, dya, lng, lnb, ws, bsx):
    T = z.shape[0]
    tm = min(T, TOKEN_TILE)
    nc = tm // CHUNK

    def body(z_ref, dya_ref, g_ref, b_ref, w_ref, bs_ref, dza_ref, dw_ref, dbs_ref, dg_ref, db_ref):
        @pl.when(pl.program_id(0) == 0)
        def _():
            dw_ref[...] = jnp.zeros_like(dw_ref)
            dbs_ref[...] = jnp.zeros_like(dbs_ref)
            dg_ref[...] = jnp.zeros_like(dg_ref)
            db_ref[...] = jnp.zeros_like(db_ref)

        za = z_ref[...]
        zg = _gelu(za)
        gp = _gelu_grad(za)
        lo = _lane((tm, LANES)) < HEAD_DIM
        for hd in range(A_HEADS):
            sl = slice(hd * LANES, (hd + 1) * LANES)
            blk = zg[:, sl]
            g = g_ref[:, sl]
            vn, vhat, rstd = _gmlp_head(blk, g, b_ref[:, sl])
            wm = _tril(w_ref[hd])
            pair = dya_ref[:, (hd // 2) * LANES:(hd // 2 + 1) * LANES]
            dy = jnp.where(lo, pair if hd % 2 == 0 else _swap64(pair), 0.0)
            dsv = _swap64(dy * blk)
            svs, dvns = [], []
            dw = jnp.zeros((CHUNK, CHUNK), F32)
            dbs = jnp.zeros((CHUNK, 1), F32)
            for c in range(nc):
                cs = slice(c * CHUNK, (c + 1) * CHUNK)
                svs.append(_mm(wm, vn[cs]) + bs_ref[hd])
                dw = dw + _mm_nt(dsv[cs], vn[cs])
                dbs = dbs + jnp.sum(dsv[cs], axis=-1, keepdims=True)
                dvns.append(_mm_tn(wm, dsv[cs]))
            sv = jnp.concatenate(svs, axis=0)
            dvn = jnp.concatenate(dvns, axis=0)
            dw_ref[hd] += _tril(dw)
            dbs_ref[hd] += jnp.broadcast_to(dbs, (CHUNK, CHUNK))
            dg_ref[:, sl] += jnp.sum(dvn * vhat, axis=0, keepdims=True)
            db_ref[:, sl] += jnp.sum(dvn, axis=0, keepdims=True)
            du = dy * _swap64(sv)
            dvh = dvn * g
            m1 = jnp.sum(dvh, axis=-1, keepdims=True) * (1.0 / HEAD_DIM)
            m2 = jnp.sum(dvh * vhat, axis=-1, keepdims=True) * (1.0 / HEAD_DIM)
            dv = jnp.where(lo, 0.0, rstd * (dvh - m1 - vhat * m2))
            dza_ref[:, sl] = (du + dv) * gp[:, sl]

    row = pl.BlockSpec((1, IN_A), lambda i: (0, 0))
    mat = pl.BlockSpec((A_HEADS, CHUNK, CHUNK), lambda i: (0, 0, 0))
    return pl.pallas_call(
        body, name="gmlp_bwd", grid=(T // tm,),
        in_specs=[pl.BlockSpec((tm, IN_A), lambda i: (i, 0)), pl.BlockSpec((tm, A_WIDTH), lambda i: (i, 0)), row, row, mat, mat],
        out_specs=[pl.BlockSpec((tm, IN_A), lambda i: (i, 0)), mat, mat, row, row],
        out_shape=[_sds((T, IN_A)), _sds((A_HEADS, CHUNK, CHUNK)), _sds((A_HEADS, CHUNK, CHUNK)), _sds((1, IN_A)), _sds((1, IN_A))],
        compiler_params=_cp("arbitrary"))(z, dya, lng, lnb, ws, bsx)


def _attn_specs(T, tq, tile_of):
    nb = tq // WINDOW

    def prev(i):
        return jnp.maximum(tile_of(i) * nb - 1, 0)

    row = pl.BlockSpec((1, LANES), lambda i: (0, 0))
    return [
        pl.BlockSpec((tq, B_WIDTH), lambda i: (tile_of(i), Q_BLOCK)),
        pl.BlockSpec((tq, LANES), lambda i: (tile_of(i), K_BLOCK)),
        pl.BlockSpec((tq, LANES), lambda i: (tile_of(i), V_BLOCK)),
        pl.BlockSpec((WINDOW, LANES), lambda i: (prev(i), K_BLOCK)),
        pl.BlockSpec((WINDOW, LANES), lambda i: (prev(i), V_BLOCK)),
        pl.BlockSpec((tq, LANES), lambda i: (tile_of(i), 0)),
        pl.BlockSpec((tq, LANES), lambda i: (tile_of(i), 0)),
        pl.BlockSpec((WINDOW, LANES), lambda i: (prev(i), 0)),
        pl.BlockSpec((WINDOW, LANES), lambda i: (prev(i), 0)),
        row, row,
        pl.BlockSpec((8, LANES), lambda i: (0, 0)),
    ]


def _attn_bias(first):
    qi = lax.broadcasted_iota(jnp.int32, (WINDOW, 2 * WINDOW), 0)
    kj = lax.broadcasted_iota(jnp.int32, (WINDOW, 2 * WINDOW), 1)
    diff = qi + WINDOW - kj
    ok = (diff >= 0) & (diff < WINDOW) & ((kj >= WINDOW) | jnp.logical_not(first))
    return jnp.where(ok, 0.0, NEG)


def _dup_heads(x, lo):
    sw = _swap64(x)
    return jnp.where(lo, x, sw), jnp.where(lo, sw, x)


HEADS_PER_KV = 4


def _stack_heads(x0, x1, lo):
    return jnp.concatenate([jnp.where(lo, x0, 0.0), jnp.where(lo, 0.0, x0), jnp.where(lo, x1, 0.0), jnp.where(lo, 0.0, x1)], axis=0)


def _unstack_heads(x4, lo):
    return (jnp.where(lo, x4[0:WINDOW], x4[WINDOW:2 * WINDOW]), jnp.where(lo, x4[2 * WINDOW:3 * WINDOW], x4[3 * WINDOW:4 * WINDOW]))


def _sink_column(sk_ref, g):
    return jnp.concatenate([jnp.broadcast_to(sk_ref[a:a + 1, 0:1], (WINDOW, 1)) for a in range(HEADS_PER_KV * g, HEADS_PER_KV * (g + 1))], axis=0)


def _attn_probs(q4, kw, bias, sink):
    s = _mm_nt(q4, kw)
    s = (s.reshape(HEADS_PER_KV, WINDOW, 2 * WINDOW) + bias[None]).reshape(HEADS_PER_KV * WINDOW, 2 * WINDOW)
    m = jnp.maximum(jnp.max(s, axis=-1, keepdims=True), sink)
    p = jnp.exp(s - m)
    es = jnp.exp(sink - m)
    inv = 1.0 / (jnp.sum(p, axis=-1, keepdims=True) + es)
    return p * inv, es * inv


def _attn_fwd(z, cs, sn, qg, kg, sinks):
    T = z.shape[0]
    tq = min(T, 2 * TOKEN_TILE)
    nb = tq // WINDOW

    def body(q_ref, k_ref, v_ref, kp_ref, vp_ref, cs_ref, sn_ref, csp_ref, snp_ref, qg_ref, kg_ref, sk_ref, o_ref):
        i = pl.program_id(0)
        csq, snq = cs_ref[...], sn_ref[...]
        cs_all = jnp.concatenate([csp_ref[...], csq], axis=0)
        sn_all = jnp.concatenate([snp_ref[...], snq], axis=0)
        k_all = jnp.concatenate([kp_ref[...], k_ref[...]], axis=0)
        v_all = jnp.concatenate([vp_ref[...], v_ref[...]], axis=0)
        kr, _, _ = _qk_norm_rope(k_all, kg_ref[...], cs_all, sn_all)
        lo_all = _lane(k_all.shape) < HEAD_DIM
        kd = _dup_heads(kr, lo_all)
        vd = _dup_heads(v_all, lo_all)
        lo = _lane((WINDOW, LANES)) < HEAD_DIM
        qrs = [_qk_norm_rope(q_ref[:, pr * LANES:(pr + 1) * LANES], qg_ref[...], csq, snq)[0] * SCALE for pr in range(4)]
        biases = [_attn_bias(i * nb + b == 0) for b in range(nb)]
        for g in range(2):
            sink = _sink_column(sk_ref, g)
            for b in range(nb):
                bs = slice(b * WINDOW, (b + 1) * WINDOW)
                ws = slice(b * WINDOW, (b + 2) * WINDOW)
                pn, _ = _attn_probs(_stack_heads(qrs[2 * g][bs], qrs[2 * g + 1][bs], lo), kd[g][ws], biases[b], sink)
                o0, o1 = _unstack_heads(_mm(pn, vd[g][ws]), lo)
                o_ref[bs, 2 * g * LANES:(2 * g + 1) * LANES] = o0
                o_ref[bs, (2 * g + 1) * LANES:(2 * g + 2) * LANES] = o1

    return pl.pallas_call(
        body, name="attn_fwd", grid=(T // tq,),
        in_specs=_attn_specs(T, tq, lambda i: i),
        out_specs=pl.BlockSpec((tq, B_WIDTH), lambda i: (i, 0)), out_shape=_sds((T, B_WIDTH)),
        compiler_params=_cp("parallel"))(z, z, z, z, z, cs, sn, cs, sn, qg, kg, sinks)


def _attn_bwd(z, cs, sn, qg, kg, sinks, o, do):
    T = z.shape[0]
    tq = min(T, 2 * TOKEN_TILE)
    nb = tq // WINDOW
    nt = T // tq
    tk = tq + WINDOW

    def tile_of(i):
        return nt - 1 - i

    def body(q_ref, k_ref, v_ref, kp_ref, vp_ref, cs_ref, sn_ref, csp_ref, snp_ref, qg_ref, kg_ref, sk_ref, o_ref, do_ref,
             dq_ref, dk_ref, dv_ref, dqg_ref, dkg_ref, dsk_ref, acck, accv, ck, cv):
        i = pl.program_id(0)
        ti = nt - 1 - i

        @pl.when(i == 0)
        def _():
            dqg_ref[...] = jnp.zeros_like(dqg_ref)
            dkg_ref[...] = jnp.zeros_like(dkg_ref)
            dsk_ref[...] = jnp.zeros_like(dsk_ref)
            ck[...] = jnp.zeros_like(ck)
            cv[...] = jnp.zeros_like(cv)

        csq, snq = cs_ref[...], sn_ref[...]
        cs_all = jnp.concatenate([csp_ref[...], csq], axis=0)
        sn_all = jnp.concatenate([snp_ref[...], snq], axis=0)
        k_all = jnp.concatenate([kp_ref[...], k_ref[...]], axis=0)
        v_all = jnp.concatenate([vp_ref[...], v_ref[...]], axis=0)
        kr, kh, rk = _qk_norm_rope(k_all, kg_ref[...], cs_all, sn_all)
        lo_all = _lane(k_all.shape) < HEAD_DIM
        kd = _dup_heads(kr, lo_all)
        vd = _dup_heads(v_all, lo_all)
        lo = _lane((WINDOW, LANES)) < HEAD_DIM
        acck[...] = jnp.zeros_like(acck)
        accv[...] = jnp.zeros_like(accv)
        prep = [_qk_norm_rope(q_ref[:, pr * LANES:(pr + 1) * LANES], qg_ref[...], csq, snq) for pr in range(4)]
        biases = [_attn_bias(ti * nb + b == 0) for b in range(nb)]
        dqs = [[None] * nb for _ in range(4)]
        for g in range(2):
            sink = _sink_column(sk_ref, g)
            dsink = jnp.zeros((HEADS_PER_KV * WINDOW, 1), F32)
            for b in range(nb):
                bs = slice(b * WINDOW, (b + 1) * WINDOW)
                ws = slice(b * WINDOW, (b + 2) * WINDOW)
                kw, vw = kd[g][ws], vd[g][ws]
                q4 = _stack_heads(prep[2 * g][0][bs] * SCALE, prep[2 * g + 1][0][bs] * SCALE, lo)
                pn, psink = _attn_probs(q4, kw, biases[b], sink)
                o0, o1 = o_ref[bs, 2 * g * LANES:(2 * g + 1) * LANES], o_ref[bs, (2 * g + 1) * LANES:(2 * g + 2) * LANES]
                do4 = _stack_heads(do_ref[bs, 2 * g * LANES:(2 * g + 1) * LANES], do_ref[bs, (2 * g + 1) * LANES:(2 * g + 2) * LANES], lo)
                delta = jnp.sum(do4 * jnp.concatenate([o0, o0, o1, o1], axis=0), axis=-1, keepdims=True)
                ds = pn * (_mm_nt(do4, vw) - delta)
                dsink = dsink - psink * delta
                dqs[2 * g][b], dqs[2 * g + 1][b] = _unstack_heads(_mm(ds, kw) * SCALE, lo)
                acck[g, ws, :] += _mm_tn(ds, q4)
                accv[g, ws, :] += _mm_tn(pn, do4)
            for hh in range(HEADS_PER_KV):
                a = HEADS_PER_KV * g + hh
                dsk_ref[a:a + 1, :] += jnp.zeros((1, LANES), F32) + jnp.sum(dsink[hh * WINDOW:(hh + 1) * WINDOW])
        for pr in range(4):
            _, qh, rq = prep[pr]
            dx, dg = _qk_norm_rope_bwd(qh, rq, qg_ref[...], csq, snq, jnp.concatenate(dqs[pr], axis=0))
            dq_ref[:, pr * LANES:(pr + 1) * LANES] = dx
            dqg_ref[...] += dg

        def fold(acc):
            f0 = acc[0] + _swap64(acc[0])
            f1 = acc[1] + _swap64(acc[1])
            return jnp.where(lo_all, f0, f1)

        dk_all = fold(acck)
        dv_all = fold(accv)
        pad = jnp.zeros((tq - WINDOW, LANES), F32)
        dk_own = dk_all[WINDOW:] + (jnp.concatenate([pad, ck[...]], axis=0) if nb > 1 else ck[...])
        dv_own = dv_all[WINDOW:] + (jnp.concatenate([pad, cv[...]], axis=0) if nb > 1 else cv[...])
        ck[...] = dk_all[:WINDOW]
        cv[...] = dv_all[:WINDOW]
        dxk, dgk = _qk_norm_rope_bwd(kh[WINDOW:], rk[WINDOW:], kg_ref[...], csq, snq, dk_own)
        dk_ref[...] = dxk
        dkg_ref[...] += dgk
        dv_ref[...] = dv_own

    row = pl.BlockSpec((1, LANES), lambda i: (0, 0))
    return pl.pallas_call(
        body, name="attn_bwd", grid=(nt,),
        in_specs=_attn_specs(T, tq, tile_of) + [pl.BlockSpec((tq, B_WIDTH), lambda i: (tile_of(i), 0))] * 2,
        out_specs=[pl.BlockSpec((tq, B_WIDTH), lambda i: (tile_of(i), 0)), pl.BlockSpec((tq, LANES), lambda i: (tile_of(i), 0)),
                   pl.BlockSpec((tq, LANES), lambda i: (tile_of(i), 0)), row, row, pl.BlockSpec((8, LANES), lambda i: (0, 0))],
        out_shape=[_sds((T, B_WIDTH)), _sds((T, LANES)), _sds((T, LANES)), _sds((1, LANES)), _sds((1, LANES)), _sds((8, LANES))],
        scratch_shapes=[pltpu.VMEM((2, tk, LANES), F32), pltpu.VMEM((2, tk, LANES), F32),
                        pltpu.VMEM((WINDOW, LANES), F32), pltpu.VMEM((WINDOW, LANES), F32)],
        compiler_params=_cp("arbitrary"))(z, z, z, z, z, cs, sn, cs, sn, qg, kg, sinks, o, do)


def _bbar_t(are, aim, ldt, btr, bti):
    lbr, lbi = _lam_bar(are, aim, ldt)
    den = are * are + aim * aim
    nr = lbr - 1.0
    cr = (nr * are + lbi * aim) / den
    ci = (lbi * are - nr * aim) / den
    return cr * btr - ci * bti, cr * bti + ci * btr


def _lam_bar(are, aim, ldt):
    dt = jnp.exp(ldt)
    er = jnp.exp(are * dt)
    return er * jnp.cos(aim * dt), er * jnp.sin(aim * dt)


def _block_diag(x):
    t = jnp.concatenate([x] * C_GROUPS, axis=1)
    return jnp.where(_row(t.shape) // C_GROUP == _lane(t.shape) // C_STATE, t, 0.0)


def _block_diag_fold(m):
    rg = _row((C_WIDTH, C_STATE)) // C_GROUP
    acc = jnp.zeros((C_WIDTH, C_STATE), F32)
    for g in range(C_GROUPS):
        acc = acc + jnp.where(rg == g, m[:, g * C_STATE:(g + 1) * C_STATE], 0.0)
    return acc


def _ssm_prep(are, aim, ldt, are_x, aim_x, ldt_x, btr, bti, cre, cim):
    def body(are_r, aim_r, ldt_r, arex_r, aimx_r, ldtx_r, btr_r, bti_r, cre_r, cim_r, bbd_ref, cbd_ref, pwr_ref, pwi_ref):
        lr, li = _lam_bar(are_r[...], aim_r[...], ldt_r[...])
        cr, ci = lr, li
        for r in range(SCAN_SEG):
            pwr_ref[r:r + 1, :] = cr
            pwi_ref[r:r + 1, :] = ci
            cr, ci = cr * lr - ci * li, cr * li + ci * lr
        br, bi = _bbar_t(arex_r[...], aimx_r[...], ldtx_r[...], btr_r[...], bti_r[...])
        bbd_ref[...] = jnp.concatenate([_block_diag(br), _block_diag(bi)], axis=1).astype(MXU)
        cbd_ref[...] = jnp.concatenate([_block_diag(cre_r[...]), -_block_diag(cim_r[...])], axis=1).astype(MXU)

    return pl.pallas_call(
        body, name="ssm_prep",
        out_shape=[_sds((C_WIDTH, 2 * N_STATE), MXU), _sds((C_WIDTH, 2 * N_STATE), MXU), _sds((SCAN_SEG, N_STATE)), _sds((SCAN_SEG, N_STATE))],
        compiler_params=pltpu.CompilerParams(vmem_limit_bytes=VMEM_LIMIT))(are, aim, ldt, are_x, aim_x, ldt_x, btr, bti, cre, cim)


def _ssm_param_bwd(are, aim, ldt, are_x, aim_x, ldt_x, btr, bti, dlr, dli, dbbd, dcr, dci):
    def body(are_r, aim_r, ldt_r, arex_r, aimx_r, ldtx_r, btr_r, bti_r, dlr_r, dli_r, dbbd_r, dcr_r, dci_r,
             dare_ref, daim_ref, dldt_ref, dbtr_ref, dbti_ref, dcre_ref, dcim_ref):
        _, vjp_l = jax.vjp(_lam_bar, are_r[...], aim_r[...], ldt_r[...])
        da1, di1, dl1 = vjp_l((dlr_r[...], dli_r[...]))
        dbr = _block_diag_fold(dbbd_r[:, 0:N_STATE])
        dbi = _block_diag_fold(dbbd_r[:, N_STATE:2 * N_STATE])
        _, vjp_b = jax.vjp(_bbar_t, arex_r[...], aimx_r[...], ldtx_r[...], btr_r[...], bti_r[...])
        da2, di2, dl2, dbtr, dbti = vjp_b((dbr, dbi))

        def gsum(x):
            return x.reshape(C_GROUPS, C_GROUP, C_STATE).sum(axis=1)

        dare_ref[...] = da1 + gsum(da2)
        daim_ref[...] = di1 + gsum(di2)
        dldt_ref[...] = jnp.broadcast_to(jnp.sum(dl1 + gsum(dl2), axis=-1, keepdims=True), (C_GROUPS, LANES))
        dbtr_ref[...] = dbtr
        dbti_ref[...] = dbti
        dcre_ref[...] = _block_diag_fold(dcr_r[...])
        dcim_ref[...] = -_block_diag_fold(dci_r[...])

    g = _sds((C_GROUPS, C_STATE))
    x = _sds((C_WIDTH, C_STATE))
    return pl.pallas_call(
        body, name="ssm_param_bwd", out_shape=[g, g, _sds((C_GROUPS, LANES)), x, x, x, x],
        compiler_params=pltpu.CompilerParams(vmem_limit_bytes=VMEM_LIMIT))(are, aim, ldt, are_x, aim_x, ldt_x, btr, bti, dlr, dli, dbbd, dcr, dci)


SCAN_TILE = 512
SCAN_SEG = 8


def _scan_tables(pwr_ref, pwi_ref, conj, reverse):
    row = _row((SCAN_SEG, N_STATE))
    shifts = []
    for k in (1, 2, 4):
        keep = (row < SCAN_SEG - k) if reverse else (row >= k)
        ar = jnp.broadcast_to(pwr_ref[k - 1:k, :], (SCAN_SEG, N_STATE))
        ai = jnp.broadcast_to(pwi_ref[k - 1:k, :], (SCAN_SEG, N_STATE)) * conj
        shifts.append((SCAN_SEG - k if reverse else k, jnp.where(keep, ar, 0.0), jnp.where(keep, ai, 0.0)))
    if reverse:
        pr = jnp.concatenate([pwr_ref[SCAN_SEG - 1 - r:SCAN_SEG - r, :] for r in range(SCAN_SEG)], axis=0)
        pi = jnp.concatenate([pwi_ref[SCAN_SEG - 1 - r:SCAN_SEG - r, :] for r in range(SCAN_SEG)], axis=0) * conj
    else:
        pr, pi = pwr_ref[...], pwi_ref[...] * conj
    return shifts, (pr, pi)


def _tile_scan(xr_ref, xi_ref, pwr_ref, pwi_ref, sr, si, conj, reverse):
    shifts, (pr, pi) = _scan_tables(pwr_ref, pwi_ref, conj, reverse)
    groups = xr_ref.shape[0] // SCAN_SEG
    out_row = 0 if reverse else SCAN_SEG - 1

    def step(k, c):
        cr, ci = c
        g = groups - 1 - k if reverse else k
        rows = pl.ds(pl.multiple_of(g * SCAN_SEG, SCAN_SEG), SCAN_SEG)
        xr, xi = xr_ref[rows, :], xi_ref[rows, :]
        for amount, ar, ai in shifts:
            qr, qi = pltpu.roll(xr, amount, 0), pltpu.roll(xi, amount, 0)
            xr, xi = xr + ar * qr - ai * qi, xi + ar * qi + ai * qr
        xr, xi = xr + pr * cr - pi * ci, xi + pr * ci + pi * cr
        xr_ref[rows, :] = xr
        xi_ref[rows, :] = xi
        return xr[out_row:out_row + 1], xi[out_row:out_row + 1]

    cr, ci = lax.fori_loop(0, groups, step, (sr[...], si[...]), unroll=2)
    sr[...] = cr
    si[...] = ci


def _ssm_fwd(z, bbd, pwr, pwi, cbd, dsk, w1, w2):
    T = z.shape[0]
    tt = min(T, 2 * SCAN_TILE)

    def body(u_ref, bbd_ref, pwr_ref, pwi_ref, cbd_ref, d_ref, w1_ref, w2_ref, yc_ref, y_ref, xr_ref, xi_ref, sr, si):
        @pl.when(pl.program_id(0) == 0)
        def _():
            sr[...] = jnp.zeros_like(sr)
            si[...] = jnp.zeros_like(si)

        u = u_ref[...]
        bu = _mm(u, bbd_ref[...])
        xr_ref[...] = bu[:, 0:N_STATE]
        xi_ref[...] = bu[:, N_STATE:2 * N_STATE]
        _tile_scan(xr_ref, xi_ref, pwr_ref, pwi_ref, sr, si, 1.0, False)
        x = jnp.concatenate([xr_ref[...], xi_ref[...]], axis=1)
        y = _mm_nt(x, cbd_ref[...]) + d_ref[...] * u
        y_ref[...] = y
        y2 = _gelu(y)
        yc_ref[...] = _mm(y2, w1_ref[...]) * _sigmoid(_mm(y2, w2_ref[...]))

    big = pl.BlockSpec((C_WIDTH, 2 * N_STATE), lambda i: (0, 0))
    tab = pl.BlockSpec((SCAN_SEG, N_STATE), lambda i: (0, 0))
    wsp = pl.BlockSpec((C_WIDTH, C_WIDTH), lambda i: (0, 0))
    xs = pl.BlockSpec((tt, N_STATE), lambda i: (i, 0))
    return pl.pallas_call(
        body, name="ssm_fwd", grid=(T // tt,),
        in_specs=[pl.BlockSpec((tt, C_WIDTH), lambda i: (i, C_BLOCK)), big, tab, tab, big, pl.BlockSpec((1, C_WIDTH), lambda i: (0, 0)), wsp, wsp],
        out_specs=[pl.BlockSpec((tt, C_WIDTH), lambda i: (i, 0)), pl.BlockSpec((tt, C_WIDTH), lambda i: (i, 0)), xs, xs],
        out_shape=[_sds((T, C_WIDTH)), _sds((T, C_WIDTH)), _sds((T, N_STATE)), _sds((T, N_STATE))],
        scratch_shapes=[pltpu.VMEM((1, N_STATE), F32)] * 2,
        compiler_params=_cp("arbitrary"))(z, bbd, pwr, pwi, cbd, dsk, w1, w2)


def _ssm_bwd(dyc, z, y, xr, xi, bbd, pwr, pwi, cbd, dsk, w1, w2):
    T = z.shape[0]
    tt = min(T, SCAN_TILE)
    nt = T // tt

    def tile_of(i):
        return nt - 1 - i

    def body(dyc_ref, u_ref, y_ref, xr_ref, xi_ref, xpr_ref, xpi_ref, bbd_ref, pwr_ref, pwi_ref, cbd_ref, d_ref, w1_ref, w2_ref,
             du_ref, dw1_ref, dw2_ref, dcr_ref, dci_ref, dbbd_ref, dlr_ref, dli_ref, dd_ref, gr, gi, sr, si):
        i = pl.program_id(0)
        ti = nt - 1 - i

        @pl.when(i == 0)
        def _():
            sr[...] = jnp.zeros_like(sr)
            si[...] = jnp.zeros_like(si)
            for acc in (dw1_ref, dw2_ref, dcr_ref, dci_ref, dbbd_ref, dlr_ref, dli_ref, dd_ref):
                acc[...] = jnp.zeros_like(acc)

        u = u_ref[...]
        xr_t, xi_t = xr_ref[...], xi_ref[...]
        y = y_ref[...]
        y2 = _gelu(y)
        a1 = _mm(y2, w1_ref[...])
        sg = _sigmoid(_mm(y2, w2_ref[...]))
        dyc_t = dyc_ref[...]
        da1 = dyc_t * sg
        da2 = dyc_t * a1 * sg * (1.0 - sg)
        dy = (_mm_nt(da1, w1_ref[...]) + _mm_nt(da2, w2_ref[...])) * _gelu_grad(y)
        gx = _mm(dy, cbd_ref[...])
        gr[...] = gx[:, 0:N_STATE]
        gi[...] = gx[:, N_STATE:2 * N_STATE]
        _tile_scan(gr, gi, pwr_ref, pwi_ref, sr, si, -1.0, True)
        ar, ai = gr[...], gi[...]
        first_row = _row(ar.shape) == 0
        live = jnp.where(ti > 0, 1.0, 0.0)
        xsr = jnp.where(first_row, xpr_ref[7:8, :] * live, pltpu.roll(xr_t, 1, 0))
        xsi = jnp.where(first_row, xpi_ref[7:8, :] * live, pltpu.roll(xi_t, 1, 0))
        dlr_ref[...] += jnp.sum(ar * xsr + ai * xsi, axis=0, keepdims=True)
        dli_ref[...] += jnp.sum(ai * xsr - ar * xsi, axis=0, keepdims=True)
        dd_ref[...] += jnp.sum(dy * u, axis=0, keepdims=True)
        arai = jnp.concatenate([ar, ai], axis=1)
        du_ref[...] = _mm_nt(arai, bbd_ref[...]) + d_ref[...] * dy
        dw1_ref[...] += _mm_tn(y2, da1)
        dw2_ref[...] += _mm_tn(y2, da2)
        dcr_ref[...] += _mm_tn(dy, xr_t)
        dci_ref[...] += _mm_tn(dy, xi_t)
        dbbd_ref[...] += _mm_tn(u, arai)

    def prev(i):
        return jnp.maximum(tile_of(i) * (tt // 8) - 1, 0)

    big = pl.BlockSpec((C_WIDTH, 2 * N_STATE), lambda i: (0, 0))
    tab = pl.BlockSpec((SCAN_SEG, N_STATE), lambda i: (0, 0))
    srow = pl.BlockSpec((1, N_STATE), lambda i: (0, 0))
    wsp = pl.BlockSpec((C_WIDTH, C_WIDTH), lambda i: (0, 0))
    xs = pl.BlockSpec((tt, N_STATE), lambda i: (tile_of(i), 0))
    xp = pl.BlockSpec((8, N_STATE), lambda i: (prev(i), 0))
    cw = pl.BlockSpec((tt, C_WIDTH), lambda i: (tile_of(i), 0))
    drow = pl.BlockSpec((1, C_WIDTH), lambda i: (0, 0))
    return pl.pallas_call(
        body, name="ssm_bwd", grid=(nt,),
        in_specs=[cw, pl.BlockSpec((tt, C_WIDTH), lambda i: (tile_of(i), C_BLOCK)), cw, xs, xs, xp, xp, big, tab, tab, big, drow, wsp, wsp],
        out_specs=[cw, wsp, wsp, pl.BlockSpec((C_WIDTH, N_STATE), lambda i: (0, 0)), pl.BlockSpec((C_WIDTH, N_STATE), lambda i: (0, 0)), big,
                   srow, srow, drow],
        out_shape=[_sds((T, C_WIDTH)), _sds((C_WIDTH, C_WIDTH)), _sds((C_WIDTH, C_WIDTH)), _sds((C_WIDTH, N_STATE)), _sds((C_WIDTH, N_STATE)),
                   _sds((C_WIDTH, 2 * N_STATE)), _sds((1, N_STATE)), _sds((1, N_STATE)), _sds((1, C_WIDTH))],
        scratch_shapes=[pltpu.VMEM((tt, N_STATE), F32)] * 2 + [pltpu.VMEM((1, N_STATE), F32)] * 2,
        compiler_params=_cp("arbitrary"))(dyc, z, y, xr, xi, xr, xi, bbd, pwr, pwi, cbd, dsk, w1, w2)


_GROUPS = ((0, A_WIDTH), (A_WIDTH, A_WIDTH + B_WIDTH), (A_WIDTH + B_WIDTH, D_MODEL))


def _merge_fwd(h, ya, yb, yc, g, w):
    T = h.shape[0]
    tm = min(T, TOKEN_TILE)

    def body(h_ref, a_ref, b_ref, c_ref, g_ref, w_ref, o_ref):
        yn = jnp.concatenate([y * _rms_stat(y) for y in (a_ref[...], b_ref[...], c_ref[...])], axis=1) * g_ref[...]
        o_ref[...] = h_ref[...] + _mm(yn, w_ref[...])

    def rows(w_):
        return pl.BlockSpec((tm, w_), lambda i: (i, 0))

    return pl.pallas_call(
        body, name="merge_fwd", grid=(T // tm,),
        in_specs=[rows(D_MODEL), rows(A_WIDTH), rows(B_WIDTH), rows(C_WIDTH), pl.BlockSpec((1, D_MODEL), lambda i: (0, 0)),
                  pl.BlockSpec((D_MODEL, D_MODEL), lambda i: (0, 0))],
        out_specs=rows(D_MODEL), out_shape=_sds((T, D_MODEL)), compiler_params=_cp("parallel"))(h, ya, yb, yc, g, w)


def _merge_bwd(gres, ya, yb, yc, g, w):
    T = gres.shape[0]
    tm = min(T, TOKEN_TILE)
    nt = T // tm

    def body(gr_ref, a_ref, b_ref, c_ref, g_ref, w_ref, da_ref, db_ref, dc_ref, dw_ref, dg_ref, acc):
        i = pl.program_id(0)

        @pl.when(i == 0)
        def _():
            dg_ref[...] = jnp.zeros_like(dg_ref)
            acc[...] = jnp.zeros_like(acc)

        gr = gr_ref[...]
        dyn = _mm_nt(gr, w_ref[...])
        yns, dgs = [], []
        for (c0, c1), y_ref, d_ref in zip(_GROUPS, (a_ref, b_ref, c_ref), (da_ref, db_ref, dc_ref)):
            y = y_ref[...]
            r = _rms_stat(y)
            gg = g_ref[:, c0:c1]
            dx, dg = _rms_bwd(y, r, gg, dyn[:, c0:c1])
            d_ref[...] = dx
            dgs.append(dg)
            yns.append(y * r * gg)
        dg_ref[...] += jnp.concatenate(dgs, axis=1)
        acc[...] += _mm_tn(jnp.concatenate(yns, axis=1), gr)

        @pl.when(i == nt - 1)
        def _():
            dw_ref[...] = acc[...].astype(WIRE)

    def rows(w_):
        return pl.BlockSpec((tm, w_), lambda i: (i, 0))

    row = pl.BlockSpec((1, D_MODEL), lambda i: (0, 0))
    full = pl.BlockSpec((D_MODEL, D_MODEL), lambda i: (0, 0))
    return pl.pallas_call(
        body, name="merge_bwd", grid=(nt,),
        in_specs=[rows(D_MODEL), rows(A_WIDTH), rows(B_WIDTH), rows(C_WIDTH), row, full],
        out_specs=[rows(A_WIDTH), rows(B_WIDTH), rows(C_WIDTH), full, row],
        out_shape=[_sds((T, A_WIDTH)), _sds((T, B_WIDTH)), _sds((T, C_WIDTH)), _sds((D_MODEL, D_MODEL), WIRE), _sds((1, D_MODEL))],
        scratch_shapes=[pltpu.VMEM((D_MODEL, D_MODEL), F32)],
        compiler_params=_cp("arbitrary"))(gres, ya, yb, yc, g, w)


FF_BLOCK = D_FF // N_DEV


def _load_weights(w1_hbm, w2_hbm, w1, w2, sem):
    @pl.when(pl.program_id(0) == 0)
    def _():
        copies = [pltpu.make_async_copy(w1_hbm.at[j], w1.at[:, pl.ds(j * FF_BLOCK, FF_BLOCK)], sem.at[j]) for j in range(N_DEV)]
        copies.append(pltpu.make_async_copy(w2_hbm, w2, sem.at[N_DEV]))
        for cp in copies:
            cp.start()
        for cp in copies:
            cp.wait()


def _mlp_weight_scratch():
    return [pltpu.VMEM((D_MODEL, D_FF), MXU), pltpu.VMEM((D_FF, D_MODEL), MXU), pltpu.SemaphoreType.DMA((N_DEV + 1,))]


def _mlp_fwd(h, g, w1, w2):
    T = h.shape[0]
    tm = min(T, TOKEN_TILE)

    def body(h_ref, g_ref, w1_hbm, w2_hbm, o_ref, r_ref, w1_v, w2_v, sem):
        _load_weights(w1_hbm, w2_hbm, w1_v, w2_v, sem)
        x = h_ref[...]
        a = jnp.maximum(_mm(x * _rms_stat(x) * g_ref[...], w1_v[...]), 0.0)
        r = (a * a).astype(MXU)
        r_ref[...] = r
        o_ref[...] = x + _mm(r, w2_v[...])

    rows = pl.BlockSpec((tm, D_MODEL), lambda i: (i, 0))
    hbm = pl.BlockSpec(memory_space=pl.ANY)
    return pl.pallas_call(
        body, name="mlp_fwd", grid=(T // tm,),
        in_specs=[rows, pl.BlockSpec((1, D_MODEL), lambda i: (0, 0)), hbm, hbm],
        out_specs=[rows, pl.BlockSpec((tm, D_FF), lambda i: (i, 0))],
        out_shape=[_sds((T, D_MODEL)), _sds((T, D_FF), MXU)],
        scratch_shapes=_mlp_weight_scratch(), compiler_params=_cp("arbitrary"))(h, g, w1, w2)


def _mlp_bwd(gres, h, g, r, w1, w2):
    T = h.shape[0]
    tm = min(T, 256)

    def body(gr_ref, h_ref, g_ref, r_ref, w1_hbm, w2_hbm, dh_ref, hn_ref, da_ref, dg_ref, w1_v, w2_v, sem):
        _load_weights(w1_hbm, w2_hbm, w1_v, w2_v, sem)

        @pl.when(pl.program_id(0) == 0)
        def _():
            dg_ref[...] = jnp.zeros_like(dg_ref)

        gr = gr_ref[...]
        da = (_mm_nt(gr, w2_v[...]) * (2.0 * jnp.sqrt(r_ref[...].astype(F32)))).astype(MXU)
        da_ref[...] = da
        x = h_ref[...]
        rs = _rms_stat(x)
        gg = g_ref[...]
        dx, dg = _rms_bwd(x, rs, gg, _mm_nt(da, w1_v[...]))
        dh_ref[...] = gr + dx
        dg_ref[...] += dg
        hn_ref[...] = (x * rs * gg).astype(MXU)

    rows = pl.BlockSpec((tm, D_MODEL), lambda i: (i, 0))
    row = pl.BlockSpec((1, D_MODEL), lambda i: (0, 0))
    wide = pl.BlockSpec((tm, D_FF), lambda i: (i, 0))
    hbm = pl.BlockSpec(memory_space=pl.ANY)
    return pl.pallas_call(
        body, name="mlp_bwd", grid=(T // tm,),
        in_specs=[rows, rows, row, wide, hbm, hbm], out_specs=[rows, rows, wide, row],
        out_shape=[_sds((T, D_MODEL)), _sds((T, D_MODEL), MXU), _sds((T, D_FF), MXU), _sds((1, D_MODEL))],
        scratch_shapes=_mlp_weight_scratch(), compiler_params=_cp("arbitrary"))(gres, h, g, r, w1, w2)


def _ple_fwd(h, p, l, g, wg, wp):
    T = h.shape[0]
    tm = min(T, TOKEN_TILE)

    def body(h_ref, p_ref, g_ref, wg_ref, wp_ref, o_ref, gp_ref, e_ref):
        x = h_ref[...]
        gp = _mm(x * _rms_stat(x) * g_ref[...], wg_ref[...])
        e = _mm(p_ref[...], wp_ref[...])
        o_ref[...] = x + _sigmoid(gp) * e
        gp_ref[...] = gp.astype(MXU)
        e_ref[...] = e.astype(MXU)

    rows = pl.BlockSpec((tm, D_MODEL), lambda i: (i, 0))
    return pl.pallas_call(
        body, name="ple_fwd", grid=(T // tm,),
        in_specs=[rows, pl.BlockSpec((None, tm, PLE_DIM), lambda i: (l, i, 0)), pl.BlockSpec((1, D_MODEL), lambda i: (0, 0)),
                  pl.BlockSpec((D_MODEL, D_MODEL), lambda i: (0, 0)), pl.BlockSpec((PLE_DIM, D_MODEL), lambda i: (0, 0))],
        out_specs=[rows, rows, rows], out_shape=[_sds((T, D_MODEL)), _sds((T, D_MODEL), MXU), _sds((T, D_MODEL), MXU)],
        compiler_params=_cp("parallel"))(h, p, g, wg, wp)


def _ple_bwd(gres, h, p, l, g, gp, e, wg):
    T = h.shape[0]
    tm = min(T, TOKEN_TILE)
    nt = T // tm
    shard = D_MODEL // N_DEV

    def body(gr_ref, h_ref, p_ref, g_ref, gp_ref, e_ref, wg_ref, dh_ref, dwg_ref, dwp_ref, dg_ref, accg, accp):
        i = pl.program_id(0)

        @pl.when(i == 0)
        def _():
            dg_ref[...] = jnp.zeros_like(dg_ref)
            accg[...] = jnp.zeros_like(accg)
            accp[...] = jnp.zeros_like(accp)

        x = h_ref[...]
        r = _rms_stat(x)
        gg = g_ref[...]
        hn = x * r * gg
        gate = _sigmoid(gp_ref[...].astype(F32))
        pe = p_ref[...]
        e = e_ref[...].astype(F32)
        gr = gr_ref[...]
        dgp = gr * e * gate * (1.0 - gate)
        dx, dg = _rms_bwd(x, r, gg, _mm_nt(dgp, wg_ref[...]))
        dh_ref[...] = gr + dx
        dg_ref[...] += dg
        accg[...] += _mm_tn(hn, dgp)
        accp[...] += _mm_tn(pe, gr * gate)

        @pl.when(i == nt - 1)
        def _():
            dwg_ref[...] = accg[...].astype(WIRE)
            for d in range(N_DEV):
                dwp_ref[d] = accp[:, d * shard:(d + 1) * shard].astype(WIRE)

    rows = pl.BlockSpec((tm, D_MODEL), lambda i: (i, 0))
    row = pl.BlockSpec((1, D_MODEL), lambda i: (0, 0))
    full = pl.BlockSpec((D_MODEL, D_MODEL), lambda i: (0, 0))
    return pl.pallas_call(
        body, name="ple_bwd", grid=(nt,),
        in_specs=[rows, rows, pl.BlockSpec((None, tm, PLE_DIM), lambda i: (l, i, 0)), row, rows, rows, full],
        out_specs=[rows, full, pl.BlockSpec((N_DEV, PLE_DIM, shard), lambda i: (0, 0, 0)), row],
        out_shape=[_sds((T, D_MODEL)), _sds((D_MODEL, D_MODEL), WIRE), _sds((N_DEV, PLE_DIM, shard), WIRE), _sds((1, D_MODEL))],
        scratch_shapes=[pltpu.VMEM((D_MODEL, D_MODEL), F32), pltpu.VMEM((PLE_DIM, D_MODEL), F32)],
        compiler_params=_cp("arbitrary"))(gres, h, p, g, gp, e, wg)


def _loss_head(h, target):
    T = h.shape[0]
    tm = min(T, 1024)

    def body(h_ref, t_ref, dh_ref, l_ref):
        @pl.when(pl.program_id(0) == 0)
        def _():
            l_ref[...] = jnp.zeros_like(l_ref)

        e = h_ref[...] - t_ref[...]
        dh_ref[...] = e * (1.0 / D_MODEL)
        l_ref[...] += jnp.zeros_like(l_ref) + 0.5 * jnp.sum(jnp.mean(e * e, axis=-1, keepdims=True))

    rows = pl.BlockSpec((tm, D_MODEL), lambda i: (i, 0))
    return pl.pallas_call(
        body, name="loss_head", grid=(T // tm,), in_specs=[rows, rows],
        out_specs=[rows, pl.BlockSpec((8, LANES), lambda i: (0, 0))], out_shape=[_sds((T, D_MODEL)), _sds((8, LANES))],
        compiler_params=_cp("arbitrary"))(h, target)


TN_ROWS = 2048

def _tn(a, b, m, n, *, bm, bn, n_major=False, dtype=F32, name="tn"):
    T = a.shape[0]
    tk = min(T, TN_ROWS)
    nk = T // tk
    assert m % bm == 0 and n % bn == 0 and (not n_major or bm == m)

    def body(a_ref, b_ref, o_ref, acc):
        k = pl.program_id(2)

        @pl.when(k == 0)
        def _():
            acc[...] = jnp.zeros_like(acc)

        acc[...] += _mm_tn(a_ref[...], b_ref[...])

        @pl.when(k == nk - 1)
        def _():
            o_ref[...] = acc[...].astype(dtype)

    if n_major:
        out_spec = pl.BlockSpec((None, bm, bn), lambda i, j, k: (j, 0, 0))
        out_shape = _sds((n // bn, m, bn), dtype)
    else:
        out_spec = pl.BlockSpec((bm, bn), lambda i, j, k: (i, j))
        out_shape = _sds((m, n), dtype)
    return pl.pallas_call(
        body, name=name, grid=(m // bm, n // bn, nk),
        in_specs=[pl.BlockSpec((tk, bm), lambda i, j, k: (k, i)), pl.BlockSpec((tk, bn), lambda i, j, k: (k, j))],
        out_specs=out_spec, out_shape=out_shape, scratch_shapes=[pltpu.VMEM((bm, bn), F32)],
        compiler_params=_cp("parallel", "parallel", "arbitrary"))(a, b)


def _row_tile(R, C):
    for cand in (512, 256, 128, 64, 32, 16, 8):
        if R % cand == 0 and cand * C * 4 <= 2 ** 20:
            return cand
    return R


def _sum_slots(land):
    S, R, C = land.shape
    tr = _row_tile(R, C)

    def body(l_ref, o_ref):
        acc = l_ref[0].astype(F32)
        for s in range(1, S):
            acc = acc + l_ref[s].astype(F32)
        o_ref[...] = acc

    return pl.pallas_call(
        body, name="sum_slots", grid=(R // tr,), in_specs=[pl.BlockSpec((S, tr, C), lambda i: (0, i, 0))],
        out_specs=pl.BlockSpec((tr, C), lambda i: (i, 0)), out_shape=_sds((R, C)), compiler_params=_cp("parallel"))(land)


def _sum_adamw(land, w, m, v):
    R, C = w.shape
    S = land.shape[0]
    tr = _row_tile(R, C)

    def body(l_ref, w_ref, m_ref, v_ref, g_ref, d_ref, nm_ref, nv_ref):
        gg = l_ref[0].astype(F32)
        for s in range(1, S):
            gg = gg + l_ref[s].astype(F32)
        g_ref[...] = gg
        nm = ADAM_B1 * m_ref[...] + (1.0 - ADAM_B1) * gg
        nv = ADAM_B2 * v_ref[...] + (1.0 - ADAM_B2) * (gg * gg)
        m_hat = nm / (1.0 - ADAM_B1 ** ADAM_STEP)
        v_hat = nv / (1.0 - ADAM_B2 ** ADAM_STEP)
        d_ref[...] = -ADAM_LR * (m_hat / (jnp.sqrt(v_hat) + ADAM_EPS) + ADAM_WD * w_ref[...])
        nm_ref[...] = nm
        nv_ref[...] = nv

    blk = pl.BlockSpec((tr, C), lambda i: (i, 0))
    return pl.pallas_call(
        body, name="sum_adamw", grid=(R // tr,), in_specs=[pl.BlockSpec((S, tr, C), lambda i: (0, i, 0))] + [blk] * 3,
        out_specs=[blk] * 4, out_shape=[_sds((R, C))] * 4, compiler_params=_cp("parallel"))(land, w, m, v)


def _all_to_all(pairs, name):
    n = len(pairs)

    def body(*refs):
        srcs, lands = refs[:n], refs[2 * n:3 * n]
        send, recv, loc = refs[3 * n:]
        x, y, c = lax.axis_index("x"), lax.axis_index("y"), lax.axis_index("c")
        me = 4 * x + 2 * y + c
        own = [pltpu.make_async_copy(pairs[t][2](srcs[t], me, me), pairs[t][3](lands[t], me), loc.at[t]) for t in range(n)]
        for cp in own:
            cp.start()
        sends, recvs = [], []
        for k in range(1, N_DEV):
            px, py, pc = x ^ (k >> 2), y ^ ((k >> 1) & 1), c ^ (k & 1)
            peer = 4 * px + 2 * py + pc
            for t in range(n):
                src = pairs[t][2](srcs[t], me, peer)
                cp = pltpu.make_async_remote_copy(src_ref=src, dst_ref=pairs[t][3](lands[t], me), send_sem=send.at[t, k],
                                                  recv_sem=recv.at[t, k], device_id=(px, py, pc), device_id_type=MESH)
                cp.start()
                sends.append(cp)
                recvs.append(pltpu.make_async_remote_copy(src_ref=src, dst_ref=pairs[t][3](lands[t], peer), send_sem=send.at[t, k],
                                                          recv_sem=recv.at[t, k], device_id=(px, py, pc), device_id_type=MESH))
        for cp in recvs:
            cp.wait_recv()
        for cp in sends:
            cp.wait_send()
        for cp in own:
            cp.wait()

    anyspec = pl.BlockSpec(memory_space=pl.ANY)
    lands = [pr[1] for pr in pairs]
    return pl.pallas_call(
        body, name=name, in_specs=[anyspec] * (2 * n), out_specs=[anyspec] * n,
        out_shape=[_sds(a.shape, a.dtype) for a in lands], input_output_aliases={n + t: t for t in range(n)},
        scratch_shapes=[pltpu.SemaphoreType.DMA((n, N_DEV)), pltpu.SemaphoreType.DMA((n, N_DEV)), pltpu.SemaphoreType.DMA((n,))],
        compiler_params=pltpu.CompilerParams(has_side_effects=True))(*[pr[0] for pr in pairs], *lands)


def _gather_whole(src):
    land = lax.empty((N_DEV,) + src.shape, src.dtype)
    return _all_to_all([(src, land, lambda ref, me, peer: ref, lambda ref, sender: ref.at[sender])], "gather_small_grads")[0]


_HBM = pl.BlockSpec(memory_space=pltpu.HBM)
_SEM = pl.BlockSpec(memory_space=pltpu.SEMAPHORE)
_EFFECT = pltpu.SideEffectType.DATAFLOW_SIDE_EFFECTING


def _peers():
    x, y, c = lax.axis_index("x"), lax.axis_index("y"), lax.axis_index("c")
    out = []
    for k in range(1, N_DEV):
        px, py, pc = x ^ (k >> 2), y ^ ((k >> 1) & 1), c ^ (k & 1)
        out.append((k, (px, py, pc), 4 * px + 2 * py + pc))
    return 4 * x + 2 * y + c, out


def _route(mode, layer):
    if mode == "all":
        return (lambda ref, peer: ref), (lambda ref, sender: ref.at[sender])
    return (lambda ref, peer: ref.at[peer]), (lambda ref, sender: ref.at[sender, layer])


def _split_start(srcs, lands, modes, layer, after, name):
    n = len(srcs)
    routes = [_route(m, layer) for m in modes]

    def body(*refs):
        src, land = refs[:n], refs[n:2 * n]
        send, recv, token = refs[2 * n + 1], refs[2 * n + 2], refs[-1]
        me, peers = _peers()
        for k, dev, peer in peers:
            for t, (src_of, dst_of) in enumerate(routes):
                pltpu.make_async_remote_copy(src_ref=src_of(src[t], peer), dst_ref=dst_of(land[t], me), send_sem=send.at[t * N_DEV + k],
                                             recv_sem=recv.at[t * N_DEV + k], device_id=dev, device_id_type=MESH).start()
        token[...] = jnp.zeros_like(token)

    bufs = list(srcs) + list(lands)
    outs = pl.pallas_call(
        body, name=name,
        out_shape=(pltpu.SemaphoreType.DMA((n * N_DEV,)), pltpu.SemaphoreType.DMA((n * N_DEV,)),
                   *[pltpu.HBM(a.shape, a.dtype) for a in bufs], _sds((8, LANES))),
        in_specs=[_HBM] * (2 * n) + [pl.BlockSpec(memory_space=pl.ANY)],
        out_specs=(_SEM, _SEM, *[_HBM] * (2 * n), pl.BlockSpec(memory_space=pltpu.VMEM)),
        input_output_aliases={i: 2 + i for i in range(2 * n)},
        compiler_params=pltpu.CompilerParams(has_side_effects=_EFFECT),
    )(*[pltpu.with_memory_space_constraint(a, pltpu.HBM) for a in bufs], after)
    return outs[0], outs[1], list(outs[2:2 + n]), list(outs[2 + n:2 + 2 * n]), outs[-1]


def _split_wait(send, recv, srcs, lands, modes, layer, after, name):
    n = len(srcs)
    routes = [_route(m, layer) for m in modes]

    def body(*refs):
        src, land = refs[:n], refs[n:2 * n]
        send_r, recv_r = refs[2 * n], refs[2 * n + 1]
        _, peers = _peers()
        for k, dev, peer in peers:
            for t, (src_of, dst_of) in enumerate(routes):
                cp = pltpu.make_async_remote_copy(src_ref=src_of(src[t], peer), dst_ref=dst_of(land[t], peer), send_sem=send_r.at[t * N_DEV + k],
                                                  recv_sem=recv_r.at[t * N_DEV + k], device_id=dev, device_id_type=MESH)
                cp.wait_send()
                cp.wait_recv()

    bufs = list(srcs) + list(lands)
    outs = pl.pallas_call(
        body, name=name, out_shape=tuple(pltpu.HBM(a.shape, a.dtype) for a in bufs),
        in_specs=[_HBM] * (2 * n) + [_SEM, _SEM, pl.BlockSpec(memory_space=pl.ANY)], out_specs=[_HBM] * (2 * n),
        input_output_aliases={i: i for i in range(2 * n)},
        compiler_params=pltpu.CompilerParams(has_side_effects=_EFFECT),
    )(*bufs, send, recv, after)
    return list(outs[:n]), list(outs[n:])


SHARDED = ("w_in", "glu_w1", "glu_w2", "w_out", "w_ff1", "w_ff2", "w_ple_gate", "w_ple_proj")
SMALL = ("attn_norm_g", "gmlp_ln_g", "gmlp_ln_b", "gmlp_ws", "gmlp_bs", "q_norm_g", "k_norm_g", "sinks", "ssm_a_re", "ssm_a_im",
         "ssm_log_dt", "ssm_b_re", "ssm_b_im", "ssm_c_re", "ssm_c_im", "ssm_d", "mix_out_g", "mlp_norm_g", "ple_norm_g")
WEIGHTS = ("attn_norm_g", "w_in", "gmlp_ln_g", "gmlp_ln_b", "gmlp_ws", "gmlp_bs", "q_norm_g", "k_norm_g", "sinks", "ssm_a_re", "ssm_a_im",
           "ssm_log_dt", "ssm_b_re", "ssm_b_im", "ssm_c_re", "ssm_c_im", "ssm_d", "glu_w1", "glu_w2", "mix_out_g", "w_out", "mlp_norm_g",
           "w_ff1", "w_ff2", "ple_norm_g", "w_ple_gate", "w_ple_proj")
FLAT_COLS = 1024


PACK_TILE_ROWS = 8
PACK_ROWS_MULTIPLE = PACK_TILE_ROWS * N_DEV


def _packed_rows(shape):
    return -(-math.prod(shape) // (PACK_TILE_ROWS * FLAT_COLS)) * PACK_TILE_ROWS


def _pack(arrs, dtype):
    blocks = []
    for a in arrs:
        flat = a.astype(dtype).reshape(-1)
        pad = _packed_rows(a.shape) * FLAT_COLS - flat.shape[0]
        if pad:
            flat = jnp.concatenate([flat, jnp.zeros((pad,), dtype)])
        blocks.append(flat.reshape(-1, FLAT_COLS))
    rows = sum(b.shape[0] for b in blocks)
    if rows % PACK_ROWS_MULTIPLE:
        blocks.append(jnp.zeros((PACK_ROWS_MULTIPLE - rows % PACK_ROWS_MULTIPLE, FLAT_COLS), dtype))
    return jnp.concatenate(blocks, axis=0)


def _unpack(flat, shapes):
    out, r = [], 0
    for s in shapes:
        nr = _packed_rows(s)
        out.append(flat[r:r + nr].reshape(-1)[:math.prod(s)].reshape(s))
        r += nr
    return out


def _from_col_major(s):
    n, rows, cs = s.shape
    return s.transpose(1, 0, 2).reshape(rows, n * cs)


EARLY = ("w_in", "glu_w1", "glu_w2")
LATE = ("w_out", "w_ff1", "w_ff2", "w_ple_gate", "w_ple_proj")
GRADS_MID = ("w_ff1", "w_ff2", "w_ple_gate", "w_ple_proj")
GRADS_END = ("w_in", "glu_w1", "glu_w2", "w_out")


def _layer_fwd(h, p, l, cs, sn, W, late_weights, S, sp):
    z = _inproj_fwd(h, S["attn_norm_g"], W["w_in"])
    ya = _gmlp_fwd(z, S["lng"], S["lnb"], S["gmlp_ws"], S["bsx"])
    yb = _attn_fwd(z, cs, sn, S["qg"], S["kg"], S["sinks"])
    yc, y, xr, xi = _ssm_fwd(z, sp["bbd"], sp["pwr"], sp["pwi"], sp["cbd"], S["ssm_d"], W["glu_w1"], W["glu_w2"])
    late, token = late_weights(ya[0:8, 0:LANES] + yb[0:8, 0:LANES] + yc[0:8, 0:LANES])
    W = {**W, **late}
    h1 = _merge_fwd(h, ya, yb, yc, _behind(S["mix_out_g"], token), W["w_out"])
    h2, r = _mlp_fwd(h1, S["mlp_norm_g"], W["w_ff1"], W["w_ff2"])
    h3, gp, e = _ple_fwd(h2, p, l, S["ple_norm_g"], W["w_ple_gate"], W["w_ple_proj"])
    return h3, dict(h=h, z=z, ya=ya, yb=yb, yc=yc, y=y, xr=xr, xi=xi, h1=h1, r=r, h2=h2, gp=gp, e=e), W


def _layer_bwd(g3, p, l, cs, sn, W, S, sp, A, raw, mid_bwd):
    G = {}
    g2, dwg, G["w_ple_proj"], G["ple_norm_g"] = _ple_bwd(g3, A["h2"], p, l, S["ple_norm_g"], A["gp"], A["e"], W["w_ple_gate"])
    G["w_ple_gate"] = dwg.reshape(N_DEV, -1, D_MODEL)
    g1, hn, da, G["mlp_norm_g"] = _mlp_bwd(g2, A["h1"], S["mlp_norm_g"], A["r"], W["w_ff1"], W["w_ff2"])
    G["w_ff1"] = _tn(hn, da, D_MODEL, D_FF, bm=D_MODEL, bn=FF_BLOCK, n_major=True, dtype=WIRE, name="tn_ff1")
    G["w_ff2"] = _tn(A["r"], g2, D_FF, D_MODEL, bm=D_MODEL, bn=D_MODEL, dtype=WIRE, name="tn_ff2").reshape(N_DEV, -1, D_MODEL)
    token = mid_bwd(g1, G)
    dya, dyb, dyc, dwo, G["mix_out_g"] = _merge_bwd(g1, A["ya"], A["yb"], A["yc"], _behind(S["mix_out_g"], token), W["w_out"])
    G["w_out"] = dwo.reshape(N_DEV, -1, D_MODEL)
    dzc, dw1, dw2, dcr, dci, dbbd, dlr, dli, dd = _ssm_bwd(dyc, A["z"], A["y"], A["xr"], A["xi"], sp["bbd"], sp["pwr"], sp["pwi"], sp["cbd"],
                                                          S["ssm_d"], W["glu_w1"], W["glu_w2"])
    G["glu_w1"] = dw1.astype(WIRE).reshape(N_DEV, -1, C_WIDTH)
    G["glu_w2"] = dw2.astype(WIRE).reshape(N_DEV, -1, C_WIDTH)
    dare, daim, dldt, dbtr, dbti, dcre, dcim = _ssm_param_bwd(
        raw["are"], raw["aim"], raw["ldt"], raw["are_x"], raw["aim_x"], raw["ldt_x"], raw["btr"], raw["bti"],
        dlr.reshape(C_GROUPS, C_STATE), dli.reshape(C_GROUPS, C_STATE), dbbd, dcr, dci)
    G["ssm_a_re"], G["ssm_a_im"], G["ssm_log_dt"] = dare, daim, dldt[:, 0]
    G["ssm_b_re"] = dbtr.reshape(C_GROUPS, C_GROUP, C_STATE).transpose(0, 2, 1)
    G["ssm_b_im"] = dbti.reshape(C_GROUPS, C_GROUP, C_STATE).transpose(0, 2, 1)
    G["ssm_c_re"] = dcre.reshape(C_GROUPS, C_GROUP, C_STATE)
    G["ssm_c_im"] = dcim.reshape(C_GROUPS, C_GROUP, C_STATE)
    G["ssm_d"] = dd.reshape(C_GROUPS, C_GROUP)
    dzq, dzk, dzv, dqg, dkg, dsk = _attn_bwd(A["z"], cs, sn, S["qg"], S["kg"], S["sinks"], A["yb"], dyb)
    G["q_norm_g"] = dqg[0, :HEAD_DIM] + dqg[0, HEAD_DIM:]
    G["k_norm_g"] = dkg[0, :HEAD_DIM] + dkg[0, HEAD_DIM:]
    G["sinks"] = dsk[:, 0]
    dza, dws, dbs, dlng, dlnb = _gmlp_bwd(A["z"], dya, S["lng"], S["lnb"], S["gmlp_ws"], S["bsx"])
    G["gmlp_ws"] = dws
    G["gmlp_bs"] = dbs[:, :, 0]
    G["gmlp_ln_g"] = dlng.reshape(A_HEADS, 2, HEAD_DIM)[:, 1]
    G["gmlp_ln_b"] = dlnb.reshape(A_HEADS, 2, HEAD_DIM)[:, 1]
    g0, G["w_in"], G["attn_norm_g"] = _inproj_bwd(g1, A["h"], S["attn_norm_g"], W["w_in"], dza, dzq, dzk, dzv, dzc)
    return g0, G


def _small_layouts(P, l):
    def row(a):
        return a.reshape(1, -1)

    zeros = jnp.zeros((A_HEADS, HEAD_DIM), F32)
    S = dict(
        attn_norm_g=row(P["attn_norm_g"][l]), mix_out_g=row(P["mix_out_g"][l]), mlp_norm_g=row(P["mlp_norm_g"][l]),
        ple_norm_g=row(P["ple_norm_g"][l]),
        lng=jnp.stack([zeros, P["gmlp_ln_g"][l]], axis=1).reshape(1, IN_A),
        lnb=jnp.stack([zeros, P["gmlp_ln_b"][l]], axis=1).reshape(1, IN_A),
        gmlp_ws=P["gmlp_ws"][l],
        bsx=jnp.broadcast_to(P["gmlp_bs"][l][:, :, None], (A_HEADS, CHUNK, CHUNK)),
        qg=jnp.tile(P["q_norm_g"][l], 2).reshape(1, LANES), kg=jnp.tile(P["k_norm_g"][l], 2).reshape(1, LANES),
        sinks=jnp.broadcast_to(P["sinks"][l][:, None], (8, LANES)),
        ssm_d=row(P["ssm_d"][l]),
    )
    are, aim = P["ssm_a_re"][l], P["ssm_a_im"][l]
    ldt = jnp.broadcast_to(P["ssm_log_dt"][l][:, None], (C_GROUPS, C_STATE))
    raw = dict(
        are=are, aim=aim, ldt=ldt,
        are_x=jnp.repeat(are, C_GROUP, axis=0), aim_x=jnp.repeat(aim, C_GROUP, axis=0), ldt_x=jnp.repeat(ldt, C_GROUP, axis=0),
        btr=P["ssm_b_re"][l].transpose(0, 2, 1).reshape(C_WIDTH, C_STATE), bti=P["ssm_b_im"][l].transpose(0, 2, 1).reshape(C_WIDTH, C_STATE),
        cre=P["ssm_c_re"][l].reshape(C_WIDTH, C_STATE), cim=P["ssm_c_im"][l].reshape(C_WIDTH, C_STATE),
    )
    return S, raw


def _ssm_prep_layer(raw):
    bbd, cbd, pwr, pwi = _ssm_prep(raw["are"].reshape(1, N_STATE), raw["aim"].reshape(1, N_STATE), raw["ldt"].reshape(1, N_STATE),
                                   raw["are_x"], raw["aim_x"], raw["ldt_x"], raw["btr"], raw["bti"], raw["cre"], raw["cim"])
    return dict(bbd=bbd, cbd=cbd, pwr=pwr, pwi=pwi)


def _behind(row, token):
    return row if token is None else row + token[0:1, 0:1]


def _local_step(x, p, positions, target, P, weights_of, mid_bwd, after_bwd):
    inv = 1.0 / (ROPE_THETA ** (jnp.arange(0, HEAD_DIM, 2, dtype=F32) / HEAD_DIM))
    cs, sn = _rope_tables(positions.reshape(-1, 1), jnp.tile(inv, 4).reshape(1, LANES))
    h = x
    acts, smalls, weights = [], [], []
    for l in range(DEPTH):
        W, late_weights, token = weights_of(l, h)
        S, raw = _small_layouts(P, l)
        sp = _ssm_prep_layer(raw)
        h, A, W = _layer_fwd(h, p, l, cs, sn, W, late_weights, {**S, "attn_norm_g": _behind(S["attn_norm_g"], token)}, sp)
        acts.append(A)
        smalls.append((S, raw, sp))
        weights.append(W)
    g, lsum = _loss_head(h, target)
    grads = [None] * DEPTH
    token = None
    for l in reversed(range(DEPTH)):
        S, raw, sp = smalls[l]
        g, grads[l] = _layer_bwd(g, p, l, cs, sn, weights[l], {**S, "ple_norm_g": _behind(S["ple_norm_g"], token)}, sp, acts[l], raw,
                                 functools.partial(mid_bwd, l))
        token = after_bwd(l, g, grads[l])
    return lsum[0, 0], g, grads


def _layer_weights(g):
    layout = dict(
        w_in=_from_col_major, glu_w1=lambda a: a.reshape(C_WIDTH, C_WIDTH), glu_w2=lambda a: a.reshape(C_WIDTH, C_WIDTH),
        w_out=lambda a: a.reshape(D_MODEL, D_MODEL), w_ff1=lambda a: a, w_ff2=lambda a: a.reshape(D_FF, D_MODEL),
        w_ple_gate=lambda a: a.reshape(D_MODEL, D_MODEL), w_ple_proj=_from_col_major)
    return {n: layout[n](a) for n, a in g.items()}


def kernel(x, p, positions, attn_norm_g, w_in, gmlp_ln_g, gmlp_ln_b, gmlp_ws, gmlp_bs, q_norm_g, k_norm_g, sinks, ssm_a_re, ssm_a_im, ssm_log_dt, ssm_b_re, ssm_b_im, ssm_c_re, ssm_c_im, ssm_d, glu_w1, glu_w2, mix_out_g, w_out, mlp_norm_g, w_ff1, w_ff2, ple_norm_g, w_ple_gate, w_ple_proj, loss_target, m_attn_norm_g, m_w_in, m_gmlp_ln_g, m_gmlp_ln_b, m_gmlp_ws, m_gmlp_bs, m_q_norm_g, m_k_norm_g, m_sinks, m_ssm_a_re, m_ssm_a_im, m_ssm_log_dt, m_ssm_b_re, m_ssm_b_im, m_ssm_c_re, m_ssm_c_im, m_ssm_d, m_glu_w1, m_glu_w2, m_mix_out_g, m_w_out, m_mlp_norm_g, m_w_ff1, m_w_ff2, m_ple_norm_g, m_w_ple_gate, m_w_ple_proj, v_attn_norm_g, v_w_in, v_gmlp_ln_g, v_gmlp_ln_b, v_gmlp_ws, v_gmlp_bs, v_q_norm_g, v_k_norm_g, v_sinks, v_ssm_a_re, v_ssm_a_im, v_ssm_log_dt, v_ssm_b_re, v_ssm_b_im, v_ssm_c_re, v_ssm_c_im, v_ssm_d, v_glu_w1, v_glu_w2, v_mix_out_g, v_w_out, v_mlp_norm_g, v_w_ff1, v_w_ff2, v_ple_norm_g, v_w_ple_gate, v_w_ple_proj):
    env = dict(locals())
    P = {n: env[n] for n in WEIGHTS}
    M = {n: env["m_" + n] for n in WEIGHTS}
    V = {n: env["v_" + n] for n in WEIGHTS}
    return _step(x, p, positions, loss_target, P, M, V)


def _step(x, p, positions, loss_target, P, M, V):
    small_shapes = [P[n].shape for n in SMALL]
    me = 4 * lax.axis_index("x") + 2 * lax.axis_index("y") + lax.axis_index("c")
    nothing = jnp.zeros((8, LANES), F32)

    def put(land, own, lead):
        return lax.dynamic_update_slice(land, own.reshape((1,) * len(lead) + own.shape), tuple(lead) + (0,) * own.ndim)

    def gather_start(l, names, after, tag):
        shards = [P[n][l].astype(WIRE) for n in names]
        lands = [lax.empty((N_DEV,) + s.shape, WIRE) for s in shards]
        send, recv, shards, lands, token = _split_start(shards, lands, ["all"] * len(names), 0, after, f"gather_start_{l}{tag}")
        return dict(names=names, send=send, recv=recv, shards=shards, lands=lands, token=token, name=f"gather_wait_{l}{tag}")

    def gather_wait(f, after):
        shards, lands = _split_wait(f["send"], f["recv"], f["shards"], f["lands"], ["all"] * len(f["names"]), 0, after, f["name"])
        return dict(zip(f["names"], [put(ld, sh, (me,)) for sh, ld in zip(shards, lands)]))

    first = gather_start(0, EARLY, nothing, "a")
    flying = {0: (first, gather_start(0, LATE, first["token"], "b"))}

    def weights_of(l, h):
        fa, fb = flying.pop(l)
        got = gather_wait(fa, h)
        if fb is None:
            token = None
            if l + 1 < DEPTH:
                flying[l + 1] = (gather_start(l + 1, SHARDED, got["w_in"], ""), None)
                token = flying[l + 1][0]["token"]
            W = _layer_weights(got)
            return {n: W[n] for n in EARLY}, (lambda after: ({n: W[n] for n in LATE}, None)), token

        def late_weights(after):
            late = gather_wait(fb, after)
            flying[l + 1] = (gather_start(l + 1, SHARDED, late["w_out"], ""), None)
            return _layer_weights(late), flying[l + 1][0]["token"]

        return _layer_weights(got), late_weights, fb["token"]

    grad_lands = {n: lax.empty((N_DEV,) + P[n].shape, WIRE) for n in SHARDED}
    sent = []

    def scatter_start(l, names, parts, lands, tag):
        send, recv, parts, lands, token = _split_start(parts, lands, ["own"] * len(parts), l, nothing, f"scatter_start_{l}{tag}")
        sent.append(dict(l=l, names=names, send=send, recv=recv, parts=parts, lands=lands, name=f"scatter_wait_{l}{tag}"))
        return token

    def scatter_wait(after):
        f = sent.pop(0)
        parts, lands = _split_wait(f["send"], f["recv"], f["parts"], f["lands"], ["own"] * len(f["parts"]), f["l"], after, f["name"])
        lands = [put(ld, lax.dynamic_index_in_dim(part, me, 0, keepdims=False), (me, f["l"])) for part, ld in zip(parts, lands)]
        return dict(zip(f["names"], lands))

    def mid_bwd(l, g1, G):
        if l > 0:
            return None
        grad_lands.update(scatter_wait(g1))
        return scatter_start(0, GRADS_MID, [G[n] for n in GRADS_MID], [grad_lands[n] for n in GRADS_MID], "a")

    small_land = []

    def after_bwd(l, g, G):
        if l > 0:
            if sent:
                grad_lands.update(scatter_wait(g))
            return scatter_start(l, SHARDED, [G[n] for n in SHARDED], [grad_lands[n] for n in SHARDED], "")
        sflat = _pack([jnp.stack([grads_of[k][n] for k in range(DEPTH)]) for n in SMALL], F32)
        sparts = sflat.reshape(N_DEV, -1, FLAT_COLS)
        small_land.append(sflat.shape)
        return scatter_start(0, GRADS_END + ("small",), [G[n] for n in GRADS_END] + [sparts],
                             [grad_lands[n] for n in GRADS_END] + [lax.empty((N_DEV, 1) + sparts.shape[1:], F32)], "b")

    grads_of = {}

    def after_bwd_recording(l, g, G):
        grads_of[l] = G
        grads_of["token"] = after_bwd(l, g, G)
        return grads_of["token"]

    lsum, gx, grads = _local_step(x[0], p[:, 0], positions[0], loss_target[0], P, weights_of, mid_bwd, after_bwd_recording)
    G, delta, new_m, new_v = {}, {}, {}, {}

    def update(names):
        for n in names:
            shp = P[n].shape
            res = _sum_adamw(grad_lands[n].reshape(N_DEV, -1, shp[-1]), *(a.reshape(-1, shp[-1]) for a in (P[n], M[n], V[n])))
            G[n], delta[n], new_m[n], new_v[n] = (a.reshape(shp) for a in res)

    grad_lands.update(scatter_wait(grads_of["token"]))
    update(GRADS_MID)
    last = scatter_wait(sum(new_v[n].reshape(-1, LANES)[:8] for n in GRADS_MID))
    small_parts = last.pop("small")
    grad_lands.update(last)
    update(GRADS_END)
    small_sum = _gather_whole(_sum_slots(small_parts[:, 0])).reshape((1,) + small_land[0])
    res = _sum_adamw(small_sum, _pack([P[n] for n in SMALL], F32), _pack([M[n] for n in SMALL], F32), _pack([V[n] for n in SMALL], F32))
    for dst, flat in zip((G, delta, new_m, new_v), res):
        dst.update(zip(SMALL, _unpack(flat, small_shapes)))
    loss = lax.psum(lsum, ("x", "y", "c"))
    return (loss, gx[None], *[G[n] for n in WEIGHTS], *[delta[n] for n in WEIGHTS], *[new_m[n] for n in WEIGHTS], *[new_v[n] for n in WEIGHTS])
```

```python
import functools
import math

import jax
import jax.numpy as jnp
from jax import lax
from jax.experimental import pallas as pl
from jax.experimental.pallas import tpu as pltpu

F32 = jnp.float32
MXU = jnp.bfloat16
WIRE = jnp.bfloat16

D_MODEL = 1024
DEPTH = 4
HEAD_DIM = 64
A_WIDTH = 256
A_HEADS = 4
CHUNK = 128
B_WIDTH = 512
WINDOW = 128
C_WIDTH = 256
C_GROUP = 16
C_GROUPS = 16
C_STATE = 64
N_STATE = C_GROUPS * C_STATE
IN_A = 2 * A_WIDTH
KV_WIDTH = 2 * HEAD_DIM
IN_COLS = IN_A + B_WIDTH + 2 * KV_WIDTH + C_WIDTH
Q_BLOCK = IN_A // B_WIDTH
K_BLOCK = (IN_A + B_WIDTH) // KV_WIDTH
V_BLOCK = K_BLOCK + 1
C_BLOCK = (IN_A + B_WIDTH + 2 * KV_WIDTH) // C_WIDTH
TOKEN_TILE = 512
D_FF = 4096
PLE_DIM = 256
EPS = 1e-6
ROPE_THETA = 10000.0
SCALE = HEAD_DIM ** -0.5
NEG = -1e30
N_DEV = 8

ADAM_LR = 0.001
ADAM_B1 = 0.9
ADAM_B2 = 0.999
ADAM_EPS = 1e-08
ADAM_WD = 0.01
ADAM_STEP = 10

V7X_VMEM_BYTES = 64 * 2 ** 20
VMEM_LIMIT = V7X_VMEM_BYTES - 8 * 2 ** 20
LANES = 128

MESH = pl.DeviceIdType.MESH


def _cp(*sem):
    return pltpu.CompilerParams(dimension_semantics=sem, vmem_limit_bytes=VMEM_LIMIT)


def _sds(shape, dtype=F32):
    return jax.ShapeDtypeStruct(shape, dtype)


def _mm(a, b):
    return jnp.dot(a.astype(MXU), b.astype(MXU), preferred_element_type=F32)


def _mm_nt(a, b):
    return lax.dot_general(a.astype(MXU), b.astype(MXU), (((1,), (1,)), ((), ())), preferred_element_type=F32)


def _mm_tn(a, b):
    return lax.dot_general(a.astype(MXU), b.astype(MXU), (((0,), (0,)), ((), ())), preferred_element_type=F32)


def _lane(shape):
    return lax.broadcasted_iota(jnp.int32, shape, len(shape) - 1)


def _row(shape):
    return lax.broadcasted_iota(jnp.int32, shape, 0)


_GELU_C = math.sqrt(2.0 / math.pi)


def _gelu(x):
    return 0.5 * x * (1.0 + jnp.tanh(_GELU_C * (x + 0.044715 * (x * x * x))))


def _gelu_grad(x):
    t = jnp.tanh(_GELU_C * (x + 0.044715 * (x * x * x)))
    return 0.5 * (1.0 + t) + 0.5 * x * (1.0 - t * t) * (_GELU_C * (1.0 + 3.0 * 0.044715 * (x * x)))


def _sigmoid(x):
    return 1.0 / (1.0 + jnp.exp(-x))


def _rms_stat(x):
    return lax.rsqrt(jnp.mean(x * x, axis=-1, keepdims=True) + EPS)


def _rms_bwd(x, r, g, dy):
    xh = x * r
    dxh = dy * g
    dx = r * (dxh - xh * jnp.mean(dxh * xh, axis=-1, keepdims=True))
    return dx, jnp.sum(dy * xh, axis=0, keepdims=True)


def _tril(w):
    return jnp.where(_row(w.shape) >= _lane(w.shape), w, 0.0)


def _swap64(x):
    return pltpu.roll(x, HEAD_DIM, 1)


def _group_sum64(x, lo):
    s_lo = jnp.sum(jnp.where(lo, x, 0.0), axis=-1, keepdims=True)
    s_hi = jnp.sum(jnp.where(lo, 0.0, x), axis=-1, keepdims=True)
    return jnp.where(lo, s_lo, s_hi)


def _partner(x):
    n = x.shape[-1]
    first = (_lane(x.shape) % HEAD_DIM) < HEAD_DIM // 2
    return jnp.where(first, pltpu.roll(x, n - HEAD_DIM // 2, 1), pltpu.roll(x, HEAD_DIM // 2, 1))


def _rope(y, cs, sn):
    return y * cs + _partner(y) * sn


def _rope_bwd(d, cs, sn):
    return d * cs + _partner(d * sn)


def _qk_norm_rope(x, g, cs, sn):
    lo = _lane(x.shape) < HEAD_DIM
    r = lax.rsqrt(_group_sum64(x * x, lo) * (1.0 / HEAD_DIM) + EPS)
    xh = x * r
    return _rope(xh * g, cs, sn), xh, r


def _qk_norm_rope_bwd(xh, r, g, cs, sn, d):
    lo = _lane(xh.shape) < HEAD_DIM
    dy = _rope_bwd(d, cs, sn)
    dxh = dy * g
    m = _group_sum64(dxh * xh, lo) * (1.0 / HEAD_DIM)
    return r * (dxh - xh * m), jnp.sum(dy * xh, axis=0, keepdims=True)


def _gmlp_head(blk, g, b):
    hi = _lane(blk.shape) >= HEAD_DIM
    mu = jnp.sum(jnp.where(hi, blk, 0.0), axis=-1, keepdims=True) * (1.0 / HEAD_DIM)
    xc = jnp.where(hi, blk - mu, 0.0)
    rstd = lax.rsqrt(jnp.sum(xc * xc, axis=-1, keepdims=True) * (1.0 / HEAD_DIM) + EPS)
    vhat = xc * rstd
    return vhat * g + b, vhat, rstd


def _rope_tables(pos_col, inv_row):
    T = pos_col.shape[0]
    tm = min(T, 1024)

    def body(p_ref, inv_ref, cs_ref, sn_ref):
        ang = p_ref[...].astype(F32) * inv_ref[...]
        s = jnp.sin(ang)
        cs_ref[...] = jnp.cos(ang)
        sn_ref[...] = jnp.where((_lane(ang.shape) % HEAD_DIM) < HEAD_DIM // 2, -s, s)

    blk = pl.BlockSpec((tm, LANES), lambda i: (i, 0))
    return pl.pallas_call(
        body, name="rope_tables", grid=(T // tm,),
        in_specs=[pl.BlockSpec((tm, 1), lambda i: (i, 0)), pl.BlockSpec((1, LANES), lambda i: (0, 0))],
        out_specs=[blk, blk], out_shape=[_sds((T, LANES))] * 2, compiler_params=_cp("parallel"))(pos_col, inv_row)


def _inproj_fwd(h, g, w):
    T = h.shape[0]
    tm = min(T, TOKEN_TILE)

    def body(h_ref, g_ref, w_ref, z_ref):
        x = h_ref[...]
        z_ref[...] = _mm(x * _rms_stat(x) * g_ref[...], w_ref[...])

    return pl.pallas_call(
        body, name="inproj_fwd", grid=(T // tm,),
        in_specs=[pl.BlockSpec((tm, D_MODEL), lambda i: (i, 0)), pl.BlockSpec((1, D_MODEL), lambda i: (0, 0)),
                  pl.BlockSpec((D_MODEL, IN_COLS), lambda i: (0, 0))],
        out_specs=pl.BlockSpec((tm, IN_COLS), lambda i: (i, 0)), out_shape=_sds((T, IN_COLS)),
        compiler_params=_cp("parallel"))(h, g, w)


def _inproj_bwd(gres, h, g, w, dza, dzq, dzk, dzv, dzc):
    T = h.shape[0]
    tm = min(T, TOKEN_TILE)
    nt = T // tm
    shard = IN_COLS // N_DEV

    def body(gr_ref, h_ref, g_ref, w_ref, a_ref, q_ref, k_ref, v_ref, c_ref, dh_ref, dw_ref, dg_ref, acc):
        i = pl.program_id(0)

        @pl.when(i == 0)
        def _():
            dg_ref[...] = jnp.zeros_like(dg_ref)
            acc[...] = jnp.zeros_like(acc)

        x = h_ref[...]
        r = _rms_stat(x)
        gg = g_ref[...]
        dz = jnp.concatenate([a_ref[...], q_ref[...], k_ref[...], v_ref[...], c_ref[...]], axis=1)
        dxn = _mm_nt(dz, w_ref[...])
        dx, dg = _rms_bwd(x, r, gg, dxn)
        dh_ref[...] = gr_ref[...] + dx
        dg_ref[...] += dg
        acc[...] += _mm_tn(x * r * gg, dz)

        @pl.when(i == nt - 1)
        def _():
            for d in range(N_DEV):
                dw_ref[d] = acc[:, d * shard:(d + 1) * shard].astype(WIRE)

    def rows(w_):
        return pl.BlockSpec((tm, w_), lambda i: (i, 0))

    row = pl.BlockSpec((1, D_MODEL), lambda i: (0, 0))
    return pl.pallas_call(
        body, name="inproj_bwd", grid=(nt,),
        in_specs=[rows(D_MODEL), rows(D_MODEL), row, pl.BlockSpec((D_MODEL, IN_COLS), lambda i: (0, 0)),
                  rows(IN_A), rows(B_WIDTH), rows(KV_WIDTH), rows(KV_WIDTH), rows(C_WIDTH)],
        out_specs=[rows(D_MODEL), pl.BlockSpec((N_DEV, D_MODEL, shard), lambda i: (0, 0, 0)), row],
        out_shape=[_sds((T, D_MODEL)), _sds((N_DEV, D_MODEL, shard), WIRE), _sds((1, D_MODEL))],
        scratch_shapes=[pltpu.VMEM((D_MODEL, IN_COLS), F32)],
        compiler_params=_cp("arbitrary"))(gres, h, g, w, dza, dzq, dzk, dzv, dzc)


def _gmlp_fwd(z, lng, lnb, ws, bsx):
    T = z.shape[0]
    tm = min(T, TOKEN_TILE)
    nc = tm // CHUNK

    def body(z_ref, g_ref, b_ref, w_ref, bs_ref, ya_ref):
        zg = _gelu(z_ref[...])
        lo = _lane((tm, LANES)) < HEAD_DIM
        prods = []
        for hd in range(A_HEADS):
            sl = slice(hd * LANES, (hd + 1) * LANES)
            blk = zg[:, sl]
            vn, _, _ = _gmlp_head(blk, g_ref[:, sl], b_ref[:, sl])
            wm = _tril(w_ref[hd])
            sv = jnp.concatenate([_mm(wm, vn[c * CHUNK:(c + 1) * CHUNK]) + bs_ref[hd] for c in range(nc)], axis=0)
            prods.append(blk * _swap64(sv))
        ya_ref[:, 0:LANES] = jnp.where(lo, prods[0], _swap64(prods[1]))
        ya_ref[:, LANES:2 * LANES] = jnp.where(lo, prods[2], _swap64(prods[3]))

    row = pl.BlockSpec((1, IN_A), lambda i: (0, 0))
    mat = pl.BlockSpec((A_HEADS, CHUNK, CHUNK), lambda i: (0, 0, 0))
    return pl.pallas_call(
        body, name="gmlp_fwd", grid=(T // tm,),
        in_specs=[pl.BlockSpec((tm, IN_A), lambda i: (i, 0)), row, row, mat, mat],
        out_specs=pl.BlockSpec((tm, A_WIDTH), lambda i: (i, 0)), out_shape=_sds((T, A_WIDTH)),
        compiler_params=_cp("parallel"))(z, lng, lnb, ws, bsx)


def _gmlp_bwd(z, dya, lng, lnb, ws, bsx):
    T = z.shape[0]
    tm = min(T, TOKEN_TILE)
    nc = tm // CHUNK

    def body(z_ref, dya_ref, g_ref, b_ref, w_ref, bs_ref, dza_ref, dw_ref, dbs_ref, dg_ref, db_ref):
        @pl.when(pl.program_id(0) == 0)
        def _():
            dw_ref[...] = jnp.zeros_like(dw_ref)
            dbs_ref[...] = jnp.zeros_like(dbs_ref)
            dg_ref[...] = jnp.zeros_like(dg_ref)
            db_ref[...] = jnp.zeros_like(db_ref)

        za = z_ref[...]
        zg = _gelu(za)
        gp = _gelu_grad(za)
        lo = _lane((tm, LANES)) < HEAD_DIM
        for hd in range(A_HEADS):
            sl = slice(hd * LANES, (hd + 1) * LANES)
            blk = zg[:, sl]
            g = g_ref[:, sl]
            vn, vhat, rstd = _gmlp_head(blk, g, b_ref[:, sl])
            wm = _tril(w_ref[hd])
            pair = dya_ref[:, (hd // 2) * LANES:(hd // 2 + 1) * LANES]
            dy = jnp.where(lo, pair if hd % 2 == 0 else _swap64(pair), 0.0)
            dsv = _swap64(dy * blk)
            svs, dvns = [], []
            dw = jnp.zeros((CHUNK, CHUNK), F32)
            dbs = jnp.zeros((CHUNK, 1), F32)
            for c in range(nc):
                cs = slice(c * CHUNK, (c + 1) * CHUNK)
                svs.append(_mm(wm, vn[cs]) + bs_ref[hd])
                dw = dw + _mm_nt(dsv[cs], vn[cs])
                dbs = dbs + jnp.sum(dsv[cs], axis=-1, keepdims=True)
                dvns.append(_mm_tn(wm, dsv[cs]))
            sv = jnp.concatenate(svs, axis=0)
            dvn = jnp.concatenate(dvns, axis=0)
            dw_ref[hd] += _tril(dw)
            dbs_ref[hd] += jnp.broadcast_to(dbs, (CHUNK, CHUNK))
            dg_ref[:, sl] += jnp.sum(dvn * vhat, axis=0, keepdims=True)
            db_ref[:, sl] += jnp.sum(dvn, axis=0, keepdims=True)
            du = dy * _swap64(sv)
            dvh = dvn * g
            m1 = jnp.sum(dvh, axis=-1, keepdims=True) * (1.0 / HEAD_DIM)
            m2 = jnp.sum(dvh * vhat, axis=-1, keepdims=True) * (1.0 / HEAD_DIM)
            dv = jnp.where(lo, 0.0, rstd * (dvh - m1 - vhat * m2))
            dza_ref[:, sl] = (du + dv) * gp[:, sl]

    row = pl.BlockSpec((1, IN_A), lambda i: (0, 0))
    mat = pl.BlockSpec((A_HEADS, CHUNK, CHUNK), lambda i: (0, 0, 0))
    return pl.pallas_call(
        body, name="gmlp_bwd", grid=(T // tm,),
        in_specs=[pl.BlockSpec((tm, IN_A), lambda i: (i, 0)), pl.BlockSpec((tm, A_WIDTH), lambda i: (i, 0)), row, row, mat, mat],
        out_specs=[pl.BlockSpec((tm, IN_A), lambda i: (i, 0)), mat, mat, row, row],
        out_shape=[_sds((T, IN_A)), _sds((A_HEADS, CHUNK, CHUNK)), _sds((A_HEADS, CHUNK, CHUNK)), _sds((1, IN_A)), _sds((1, IN_A))],
        compiler_params=_cp("arbitrary"))(z, dya, lng, lnb, ws, bsx)


def _attn_specs(T, tq, tile_of):
    nb = tq // WINDOW

    def prev(i):
        return jnp.maximum(tile_of(i) * nb - 1, 0)

    row = pl.BlockSpec((1, LANES), lambda i: (0, 0))
    return [
        pl.BlockSpec((tq, B_WIDTH), lambda i: (tile_of(i), Q_BLOCK)),
        pl.BlockSpec((tq, LANES), lambda i: (tile_of(i), K_BLOCK)),
        pl.BlockSpec((tq, LANES), lambda i: (tile_of(i), V_BLOCK)),
        pl.BlockSpec((WINDOW, LANES), lambda i: (prev(i), K_BLOCK)),
        pl.BlockSpec((WINDOW, LANES), lambda i: (prev(i), V_BLOCK)),
        pl.BlockSpec((tq, LANES), lambda i: (tile_of(i), 0)),
        pl.BlockSpec((tq, LANES), lambda i: (tile_of(i), 0)),
        pl.BlockSpec((WINDOW, LANES), lambda i: (prev(i), 0)),
        pl.BlockSpec((WINDOW, LANES), lambda i: (prev(i), 0)),
        row, row,
        pl.BlockSpec((8, LANES), lambda i: (0, 0)),
    ]


def _attn_bias(first):
    qi = lax.broadcasted_iota(jnp.int32, (WINDOW, 2 * WINDOW), 0)
    kj = lax.broadcasted_iota(jnp.int32, (WINDOW, 2 * WINDOW), 1)
    diff = qi + WINDOW - kj
    ok = (diff >= 0) & (diff < WINDOW) & ((kj >= WINDOW) | jnp.logical_not(first))
    return jnp.where(ok, 0.0, NEG)


def _dup_heads(x, lo):
    sw = _swap64(x)
    return jnp.where(lo, x, sw), jnp.where(lo, sw, x)


HEADS_PER_KV = 4


def _stack_heads(x0, x1, lo):
    return jnp.concatenate([jnp.where(lo, x0, 0.0), jnp.where(lo, 0.0, x0), jnp.where(lo, x1, 0.0), jnp.where(lo, 0.0, x1)], axis=0)


def _unstack_heads(x4, lo):
    return (jnp.where(lo, x4[0:WINDOW], x4[WINDOW:2 * WINDOW]), jnp.where(lo, x4[2 * WINDOW:3 * WINDOW], x4[3 * WINDOW:4 * WINDOW]))


def _sink_column(sk_ref, g):
    return jnp.concatenate([jnp.broadcast_to(sk_ref[a:a + 1, 0:1], (WINDOW, 1)) for a in range(HEADS_PER_KV * g, HEADS_PER_KV * (g + 1))], axis=0)


def _attn_probs(q4, kw, bias, sink):
    s = _mm_nt(q4, kw)
    s = (s.reshape(HEADS_PER_KV, WINDOW, 2 * WINDOW) + bias[None]).reshape(HEADS_PER_KV * WINDOW, 2 * WINDOW)
    m = jnp.maximum(jnp.max(s, axis=-1, keepdims=True), sink)
    p = jnp.exp(s - m)
    es = jnp.exp(sink - m)
    inv = 1.0 / (jnp.sum(p, axis=-1, keepdims=True) + es)
    return p * inv, es * inv


def _attn_fwd(z, cs, sn, qg, kg, sinks):
    T = z.shape[0]
    tq = min(T, 2 * TOKEN_TILE)
    nb = tq // WINDOW

    def body(q_ref, k_ref, v_ref, kp_ref, vp_ref, cs_ref, sn_ref, csp_ref, snp_ref, qg_ref, kg_ref, sk_ref, o_ref):
        i = pl.program_id(0)
        csq, snq = cs_ref[...], sn_ref[...]
        cs_all = jnp.concatenate([csp_ref[...], csq], axis=0)
        sn_all = jnp.concatenate([snp_ref[...], snq], axis=0)
        k_all = jnp.concatenate([kp_ref[...], k_ref[...]], axis=0)
        v_all = jnp.concatenate([vp_ref[...], v_ref[...]], axis=0)
        kr, _, _ = _qk_norm_rope(k_all, kg_ref[...], cs_all, sn_all)
        lo_all = _lane(k_all.shape) < HEAD_DIM
        kd = _dup_heads(kr, lo_all)
        vd = _dup_heads(v_all, lo_all)
        lo = _lane((WINDOW, LANES)) < HEAD_DIM
        qrs = [_qk_norm_rope(q_ref[:, pr * LANES:(pr + 1) * LANES], qg_ref[...], csq, snq)[0] * SCALE for pr in range(4)]
        biases = [_attn_bias(i * nb + b == 0) for b in range(nb)]
        for g in range(2):
            sink = _sink_column(sk_ref, g)
            for b in range(nb):
                bs = slice(b * WINDOW, (b + 1) * WINDOW)
                ws = slice(b * WINDOW, (b + 2) * WINDOW)
                pn, _ = _attn_probs(_stack_heads(qrs[2 * g][bs], qrs[2 * g + 1][bs], lo), kd[g][ws], biases[b], sink)
                o0, o1 = _unstack_heads(_mm(pn, vd[g][ws]), lo)
                o_ref[bs, 2 * g * LANES:(2 * g + 1) * LANES] = o0
                o_ref[bs, (2 * g + 1) * LANES:(2 * g + 2) * LANES] = o1

    return pl.pallas_call(
        body, name="attn_fwd", grid=(T // tq,),
        in_specs=_attn_specs(T, tq, lambda i: i),
        out_specs=pl.BlockSpec((tq, B_WIDTH), lambda i: (i, 0)), out_shape=_sds((T, B_WIDTH)),
        compiler_params=_cp("parallel"))(z, z, z, z, z, cs, sn, cs, sn, qg, kg, sinks)


def _attn_bwd(z, cs, sn, qg, kg, sinks, o, do):
    T = z.shape[0]
    tq = min(T, 2 * TOKEN_TILE)
    nb = tq // WINDOW
    nt = T // tq
    tk = tq + WINDOW

    def tile_of(i):
        return nt - 1 - i

    def body(q_ref, k_ref, v_ref, kp_ref, vp_ref, cs_ref, sn_ref, csp_ref, snp_ref, qg_ref, kg_ref, sk_ref, o_ref, do_ref,
             dq_ref, dk_ref, dv_ref, dqg_ref, dkg_ref, dsk_ref, acck, accv, ck, cv):
        i = pl.program_id(0)
        ti = nt - 1 - i

        @pl.when(i == 0)
        def _():
            dqg_ref[...] = jnp.zeros_like(dqg_ref)
            dkg_ref[...] = jnp.zeros_like(dkg_ref)
            dsk_ref[...] = jnp.zeros_like(dsk_ref)
            ck[...] = jnp.zeros_like(ck)
            cv[...] = jnp.zeros_like(cv)

        csq, snq = cs_ref[...], sn_ref[...]
        cs_all = jnp.concatenate([csp_ref[...], csq], axis=0)
        sn_all = jnp.concatenate([snp_ref[...], snq], axis=0)
        k_all = jnp.concatenate([kp_ref[...], k_ref[...]], axis=0)
        v_all = jnp.concatenate([vp_ref[...], v_ref[...]], axis=0)
        kr, kh, rk = _qk_norm_rope(k_all, kg_ref[...], cs_all, sn_all)
        lo_all = _lane(k_all.shape) < HEAD_DIM
        kd = _dup_heads(kr, lo_all)
        vd = _dup_heads(v_all, lo_all)
        lo = _lane((WINDOW, LANES)) < HEAD_DIM
        acck[...] = jnp.zeros_like(acck)
        accv[...] = jnp.zeros_like(accv)
        prep = [_qk_norm_rope(q_ref[:, pr * LANES:(pr + 1) * LANES], qg_ref[...], csq, snq) for pr in range(4)]
        biases = [_attn_bias(ti * nb + b == 0) for b in range(nb)]
        dqs = [[None] * nb for _ in range(4)]
        for g in range(2):
            sink = _sink_column(sk_ref, g)
            dsink = jnp.zeros((HEADS_PER_KV * WINDOW, 1), F32)
            for b in range(nb):
                bs = slice(b * WINDOW, (b + 1) * WINDOW)
                ws = slice(b * WINDOW, (b + 2) * WINDOW)
                kw, vw = kd[g][ws], vd[g][ws]
                q4 = _stack_heads(prep[2 * g][0][bs] * SCALE, prep[2 * g + 1][0][bs] * SCALE, lo)
                pn, psink = _attn_probs(q4, kw, biases[b], sink)
                o0, o1 = o_ref[bs, 2 * g * LANES:(2 * g + 1) * LANES], o_ref[bs, (2 * g + 1) * LANES:(2 * g + 2) * LANES]
                do4 = _stack_heads(do_ref[bs, 2 * g * LANES:(2 * g + 1) * LANES], do_ref[bs, (2 * g + 1) * LANES:(2 * g + 2) * LANES], lo)
                delta = jnp.sum(do4 * jnp.concatenate([o0, o0, o1, o1], axis=0), axis=-1, keepdims=True)
                ds = pn * (_mm_nt(do4, vw) - delta)
                dsink = dsink - psink * delta
                dqs[2 * g][b], dqs[2 * g + 1][b] = _unstack_heads(_mm(ds, kw) * SCALE, lo)
                acck[g, ws, :] += _mm_tn(ds, q4)
                accv[g, ws, :] += _mm_tn(pn, do4)
            for hh in range(HEADS_PER_KV):
                a = HEADS_PER_KV * g + hh
                dsk_ref[a:a + 1, :] += jnp.zeros((1, LANES), F32) + jnp.sum(dsink[hh * WINDOW:(hh + 1) * WINDOW])
        for pr in range(4):
            _, qh, rq = prep[pr]
            dx, dg = _qk_norm_rope_bwd(qh, rq, qg_ref[...], csq, snq, jnp.concatenate(dqs[pr], axis=0))
            dq_ref[:, pr * LANES:(pr + 1) * LANES] = dx
            dqg_ref[...] += dg

        def fold(acc):
            f0 = acc[0] + _swap64(acc[0])
            f1 = acc[1] + _swap64(acc[1])
            return jnp.where(lo_all, f0, f1)

        dk_all = fold(acck)
        dv_all = fold(accv)
        pad = jnp.zeros((tq - WINDOW, LANES), F32)
        dk_own = dk_all[WINDOW:] + (jnp.concatenate([pad, ck[...]], axis=0) if nb > 1 else ck[...])
        dv_own = dv_all[WINDOW:] + (jnp.concatenate([pad, cv[...]], axis=0) if nb > 1 else cv[...])
        ck[...] = dk_all[:WINDOW]
        cv[...] = dv_all[:WINDOW]
        dxk, dgk = _qk_norm_rope_bwd(kh[WINDOW:], rk[WINDOW:], kg_ref[...], csq, snq, dk_own)
        dk_ref[...] = dxk
        dkg_ref[...] += dgk
        dv_ref[...] = dv_own

    row = pl.BlockSpec((1, LANES), lambda i: (0, 0))
    return pl.pallas_call(
        body, name="attn_bwd", grid=(nt,),
        in_specs=_attn_specs(T, tq, tile_of) + [pl.BlockSpec((tq, B_WIDTH), lambda i: (tile_of(i), 0))] * 2,
        out_specs=[pl.BlockSpec((tq, B_WIDTH), lambda i: (tile_of(i), 0)), pl.BlockSpec((tq, LANES), lambda i: (tile_of(i), 0)),
                   pl.BlockSpec((tq, LANES), lambda i: (tile_of(i), 0)), row, row, pl.BlockSpec((8, LANES), lambda i: (0, 0))],
        out_shape=[_sds((T, B_WIDTH)), _sds((T, LANES)), _sds((T, LANES)), _sds((1, LANES)), _sds((1, LANES)), _sds((8, LANES))],
        scratch_shapes=[pltpu.VMEM((2, tk, LANES), F32), pltpu.VMEM((2, tk, LANES), F32),
                        pltpu.VMEM((WINDOW, LANES), F32), pltpu.VMEM((WINDOW, LANES), F32)],
        compiler_params=_cp("arbitrary"))(z, z, z, z, z, cs, sn, cs, sn, qg, kg, sinks, o, do)


def _bbar_t(are, aim, ldt, btr, bti):
    lbr, lbi = _lam_bar(are, aim, ldt)
    den = are * are + aim * aim
    nr = lbr - 1.0
    cr = (nr * are + lbi * aim) / den
    ci = (lbi * are - nr * aim) / den
    return cr * btr - ci * bti, cr * bti + ci * btr


def _lam_bar(are, aim, ldt):
    dt = jnp.exp(ldt)
    er = jnp.exp(are * dt)
    return er * jnp.cos(aim * dt), er * jnp.sin(aim * dt)


def _block_diag(x):
    t = jnp.concatenate([x] * C_GROUPS, axis=1)
    return jnp.where(_row(t.shape) // C_GROUP == _lane(t.shape) // C_STATE, t, 0.0)


def _block_diag_fold(m):
    rg = _row((C_WIDTH, C_STATE)) // C_GROUP
    acc = jnp.zeros((C_WIDTH, C_STATE), F32)
    for g in range(C_GROUPS):
        acc = acc + jnp.where(rg == g, m[:, g * C_STATE:(g + 1) * C_STATE], 0.0)
    return acc


def _ssm_prep(are, aim, ldt, are_x, aim_x, ldt_x, btr, bti, cre, cim):
    def body(are_r, aim_r, ldt_r, arex_r, aimx_r, ldtx_r, btr_r, bti_r, cre_r, cim_r, bbd_ref, cbd_ref, pwr_ref, pwi_ref):
        lr, li = _lam_bar(are_r[...], aim_r[...], ldt_r[...])
        cr, ci = lr, li
        for r in range(SCAN_SEG):
            pwr_ref[r:r + 1, :] = cr
            pwi_ref[r:r + 1, :] = ci
            cr, ci = cr * lr - ci * li, cr * li + ci * lr
        br, bi = _bbar_t(arex_r[...], aimx_r[...], ldtx_r[...], btr_r[...], bti_r[...])
        bbd_ref[...] = jnp.concatenate([_block_diag(br), _block_diag(bi)], axis=1).astype(MXU)
        cbd_ref[...] = jnp.concatenate([_block_diag(cre_r[...]), -_block_diag(cim_r[...])], axis=1).astype(MXU)

    return pl.pallas_call(
        body, name="ssm_prep",
        out_shape=[_sds((C_WIDTH, 2 * N_STATE), MXU), _sds((C_WIDTH, 2 * N_STATE), MXU), _sds((SCAN_SEG, N_STATE)), _sds((SCAN_SEG, N_STATE))],
        compiler_params=pltpu.CompilerParams(vmem_limit_bytes=VMEM_LIMIT))(are, aim, ldt, are_x, aim_x, ldt_x, btr, bti, cre, cim)


def _ssm_param_bwd(are, aim, ldt, are_x, aim_x, ldt_x, btr, bti, dlr, dli, dbbd, dcr, dci):
    def body(are_r, aim_r, ldt_r, arex_r, aimx_r, ldtx_r, btr_r, bti_r, dlr_r, dli_r, dbbd_r, dcr_r, dci_r,
             dare_ref, daim_ref, dldt_ref, dbtr_ref, dbti_ref, dcre_ref, dcim_ref):
        _, vjp_l = jax.vjp(_lam_bar, are_r[...], aim_r[...], ldt_r[...])
        da1, di1, dl1 = vjp_l((dlr_r[...], dli_r[...]))
        dbr = _block_diag_fold(dbbd_r[:, 0:N_STATE])
        dbi = _block_diag_fold(dbbd_r[:, N_STATE:2 * N_STATE])
        _, vjp_b = jax.vjp(_bbar_t, arex_r[...], aimx_r[...], ldtx_r[...], btr_r[...], bti_r[...])
        da2, di2, dl2, dbtr, dbti = vjp_b((dbr, dbi))

        def gsum(x):
            return x.reshape(C_GROUPS, C_GROUP, C_STATE).sum(axis=1)

        dare_ref[...] = da1 + gsum(da2)
        daim_ref[...] = di1 + gsum(di2)
        dldt_ref[...] = jnp.broadcast_to(jnp.sum(dl1 + gsum(dl2), axis=-1, keepdims=True), (C_GROUPS, LANES))
        dbtr_ref[...] = dbtr
        dbti_ref[...] = dbti
        dcre_ref[...] = _block_diag_fold(dcr_r[...])
        dcim_ref[...] = -_block_diag_fold(dci_r[...])

    g = _sds((C_GROUPS, C_STATE))
    x = _sds((C_WIDTH, C_STATE))
    return pl.pallas_call(
        body, name="ssm_param_bwd", out_shape=[g, g, _sds((C_GROUPS, LANES)), x, x, x, x],
        compiler_params=pltpu.CompilerParams(vmem_limit_bytes=VMEM_LIMIT))(are, aim, ldt, are_x, aim_x, ldt_x, btr, bti, dlr, dli, dbbd, dcr, dci)


SCAN_TILE = 512
SCAN_SEG = 8


def _scan_tables(pwr_ref, pwi_ref, conj, reverse):
    row = _row((SCAN_SEG, N_STATE))
    shifts = []
    for k in (1, 2, 4):
        keep = (row < SCAN_SEG - k) if reverse else (row >= k)
        ar = jnp.broadcast_to(pwr_ref[k - 1:k, :], (SCAN_SEG, N_STATE))
        ai = jnp.broadcast_to(pwi_ref[k - 1:k, :], (SCAN_SEG, N_STATE)) * conj
        shifts.append((SCAN_SEG - k if reverse else k, jnp.where(keep, ar, 0.0), jnp.where(keep, ai, 0.0)))
    if reverse:
        pr = jnp.concatenate([pwr_ref[SCAN_SEG - 1 - r:SCAN_SEG - r, :] for r in range(SCAN_SEG)], axis=0)
        pi = jnp.concatenate([pwi_ref[SCAN_SEG - 1 - r:SCAN_SEG - r, :] for r in range(SCAN_SEG)], axis=0) * conj
    else:
        pr, pi = pwr_ref[...], pwi_ref[...] * conj
    return shifts, (pr, pi)


def _tile_scan(xr_ref, xi_ref, pwr_ref, pwi_ref, sr, si, conj, reverse):
    shifts, (pr, pi) = _scan_tables(pwr_ref, pwi_ref, conj, reverse)
    groups = xr_ref.shape[0] // SCAN_SEG
    out_row = 0 if reverse else SCAN_SEG - 1

    def step(k, c):
        cr, ci = c
        g = groups - 1 - k if reverse else k
        rows = pl.ds(pl.multiple_of(g * SCAN_SEG, SCAN_SEG), SCAN_SEG)
        xr, xi = xr_ref[rows, :], xi_ref[rows, :]
        for amount, ar, ai in shifts:
            qr, qi = pltpu.roll(xr, amount, 0), pltpu.roll(xi, amount, 0)
            xr, xi = xr + ar * qr - ai * qi, xi + ar * qi + ai * qr
        xr, xi = xr + pr * cr - pi * ci, xi + pr * ci + pi * cr
        xr_ref[rows, :] = xr
        xi_ref[rows, :] = xi
        return xr[out_row:out_row + 1], xi[out_row:out_row + 1]

    cr, ci = lax.fori_loop(0, groups, step, (sr[...], si[...]), unroll=2)
    sr[...] = cr
    si[...] = ci


def _ssm_fwd(z, bbd, pwr, pwi, cbd, dsk, w1, w2):
    T = z.shape[0]
    tt = min(T, 2 * SCAN_TILE)

    def body(u_ref, bbd_ref, pwr_ref, pwi_ref, cbd_ref, d_ref, w1_ref, w2_ref, yc_ref, y_ref, xr_ref, xi_ref, sr, si):
        @pl.when(pl.program_id(0) == 0)
        def _():
            sr[...] = jnp.zeros_like(sr)
            si[...] = jnp.zeros_like(si)

        u = u_ref[...]
        bu = _mm(u, bbd_ref[...])
        xr_ref[...] = bu[:, 0:N_STATE]
        xi_ref[...] = bu[:, N_STATE:2 * N_STATE]
        _tile_scan(xr_ref, xi_ref, pwr_ref, pwi_ref, sr, si, 1.0, False)
        x = jnp.concatenate([xr_ref[...], xi_ref[...]], axis=1)
        y = _mm_nt(x, cbd_ref[...]) + d_ref[...] * u
        y_ref[...] = y
        y2 = _gelu(y)
        yc_ref[...] = _mm(y2, w1_ref[...]) * _sigmoid(_mm(y2, w2_ref[...]))

    big = pl.BlockSpec((C_WIDTH, 2 * N_STATE), lambda i: (0, 0))
    tab = pl.BlockSpec((SCAN_SEG, N_STATE), lambda i: (0, 0))
    wsp = pl.BlockSpec((C_WIDTH, C_WIDTH), lambda i: (0, 0))
    xs = pl.BlockSpec((tt, N_STATE), lambda i: (i, 0))
    return pl.pallas_call(
        body, name="ssm_fwd", grid=(T // tt,),
        in_specs=[pl.BlockSpec((tt, C_WIDTH), lambda i: (i, C_BLOCK)), big, tab, tab, big, pl.BlockSpec((1, C_WIDTH), lambda i: (0, 0)), wsp, wsp],
        out_specs=[pl.BlockSpec((tt, C_WIDTH), lambda i: (i, 0)), pl.BlockSpec((tt, C_WIDTH), lambda i: (i, 0)), xs, xs],
        out_shape=[_sds((T, C_WIDTH)), _sds((T, C_WIDTH)), _sds((T, N_STATE)), _sds((T, N_STATE))],
        scratch_shapes=[pltpu.VMEM((1, N_STATE), F32)] * 2,
        compiler_params=_cp("arbitrary"))(z, bbd, pwr, pwi, cbd, dsk, w1, w2)


def _ssm_bwd(dyc, z, y, xr, xi, bbd, pwr, pwi, cbd, dsk, w1, w2):
    T = z.shape[0]
    tt = min(T, SCAN_TILE)
    nt = T // tt

    def tile_of(i):
        return nt - 1 - i

    def body(dyc_ref, u_ref, y_ref, xr_ref, xi_ref, xpr_ref, xpi_ref, bbd_ref, pwr_ref, pwi_ref, cbd_ref, d_ref, w1_ref, w2_ref,
             du_ref, dw1_ref, dw2_ref, dcr_ref, dci_ref, dbbd_ref, dlr_ref, dli_ref, dd_ref, gr, gi, sr, si):
        i = pl.program_id(0)
        ti = nt - 1 - i

        @pl.when(i == 0)
        def _():
            sr[...] = jnp.zeros_like(sr)
            si[...] = jnp.zeros_like(si)
            for acc in (dw1_ref, dw2_ref, dcr_ref, dci_ref, dbbd_ref, dlr_ref, dli_ref, dd_ref):
                acc[...] = jnp.zeros_like(acc)

        u = u_ref[...]
        xr_t, xi_t = xr_ref[...], xi_ref[...]
        y = y_ref[...]
        y2 = _gelu(y)
        a1 = _mm(y2, w1_ref[...])
        sg = _sigmoid(_mm(y2, w2_ref[...]))
        dyc_t = dyc_ref[...]
        da1 = dyc_t * sg
        da2 = dyc_t * a1 * sg * (1.0 - sg)
        dy = (_mm_nt(da1, w1_ref[...]) + _mm_nt(da2, w2_ref[...])) * _gelu_grad(y)
        gx = _mm(dy, cbd_ref[...])
        gr[...] = gx[:, 0:N_STATE]
        gi[...] = gx[:, N_STATE:2 * N_STATE]
        _tile_scan(gr, gi, pwr_ref, pwi_ref, sr, si, -1.0, True)
        ar, ai = gr[...], gi[...]
        first_row = _row(ar.shape) == 0
        live = jnp.where(ti > 0, 1.0, 0.0)
        xsr = jnp.where(first_row, xpr_ref[7:8, :] * live, pltpu.roll(xr_t, 1, 0))
        xsi = jnp.where(first_row, xpi_ref[7:8, :] * live, pltpu.roll(xi_t, 1, 0))
        dlr_ref[...] += jnp.sum(ar * xsr + ai * xsi, axis=0, keepdims=True)
        dli_ref[...] += jnp.sum(ai * xsr - ar * xsi, axis=0, keepdims=True)
        dd_ref[...] += jnp.sum(dy * u, axis=0, keepdims=True)
        arai = jnp.concatenate([ar, ai], axis=1)
        du_ref[...] = _mm_nt(arai, bbd_ref[...]) + d_ref[...] * dy
        dw1_ref[...] += _mm_tn(y2, da1)
        dw2_ref[...] += _mm_tn(y2, da2)
        dcr_ref[...] += _mm_tn(dy, xr_t)
        dci_ref[...] += _mm_tn(dy, xi_t)
        dbbd_ref[...] += _mm_tn(u, arai)

    def prev(i):
        return jnp.maximum(tile_of(i) * (tt // 8) - 1, 0)

    big = pl.BlockSpec((C_WIDTH, 2 * N_STATE), lambda i: (0, 0))
    tab = pl.BlockSpec((SCAN_SEG, N_STATE), lambda i: (0, 0))
    srow = pl.BlockSpec((1, N_STATE), lambda i: (0, 0))
    wsp = pl.BlockSpec((C_WIDTH, C_WIDTH), lambda i: (0, 0))
    xs = pl.BlockSpec((tt, N_STATE), lambda i: (tile_of(i), 0))
    xp = pl.BlockSpec((8, N_STATE), lambda i: (prev(i), 0))
    cw = pl.BlockSpec((tt, C_WIDTH), lambda i: (tile_of(i), 0))
    drow = pl.BlockSpec((1, C_WIDTH), lambda i: (0, 0))
    return pl.pallas_call(
        body, name="ssm_bwd", grid=(nt,),
        in_specs=[cw, pl.BlockSpec((tt, C_WIDTH), lambda i: (tile_of(i), C_BLOCK)), cw, xs, xs, xp, xp, big, tab, tab, big, drow, wsp, wsp],
        out_specs=[cw, wsp, wsp, pl.BlockSpec((C_WIDTH, N_STATE), lambda i: (0, 0)), pl.BlockSpec((C_WIDTH, N_STATE), lambda i: (0, 0)), big,
                   srow, srow, drow],
        out_shape=[_sds((T, C_WIDTH)), _sds((C_WIDTH, C_WIDTH)), _sds((C_WIDTH, C_WIDTH)), _sds((C_WIDTH, N_STATE)), _sds((C_WIDTH, N_STATE)),
                   _sds((C_WIDTH, 2 * N_STATE)), _sds((1, N_STATE)), _sds((1, N_STATE)), _sds((1, C_WIDTH))],
        scratch_shapes=[pltpu.VMEM((tt, N_STATE), F32)] * 2 + [pltpu.VMEM((1, N_STATE), F32)] * 2,
        compiler_params=_cp("arbitrary"))(dyc, z, y, xr, xi, xr, xi, bbd, pwr, pwi, cbd, dsk, w1, w2)


_GROUPS = ((0, A_WIDTH), (A_WIDTH, A_WIDTH + B_WIDTH), (A_WIDTH + B_WIDTH, D_MODEL))


def _merge_fwd(h, ya, yb, yc, g, w):
    T = h.shape[0]
    tm = min(T, TOKEN_TILE)

    def body(h_ref, a_ref, b_ref, c_ref, g_ref, w_ref, o_ref):
        yn = jnp.concatenate([y * _rms_stat(y) for y in (a_ref[...], b_ref[...], c_ref[...])], axis=1) * g_ref[...]
        o_ref[...] = h_ref[...] + _mm(yn, w_ref[...])

    def rows(w_):
        return pl.BlockSpec((tm, w_), lambda i: (i, 0))

    return pl.pallas_call(
        body, name="merge_fwd", grid=(T // tm,),
        in_specs=[rows(D_MODEL), rows(A_WIDTH), rows(B_WIDTH), rows(C_WIDTH), pl.BlockSpec((1, D_MODEL), lambda i: (0, 0)),
                  pl.BlockSpec((D_MODEL, D_MODEL), lambda i: (0, 0))],
        out_specs=rows(D_MODEL), out_shape=_sds((T, D_MODEL)), compiler_params=_cp("parallel"))(h, ya, yb, yc, g, w)


def _merge_bwd(gres, ya, yb, yc, g, w):
    T = gres.shape[0]
    tm = min(T, TOKEN_TILE)
    nt = T // tm

    def body(gr_ref, a_ref, b_ref, c_ref, g_ref, w_ref, da_ref, db_ref, dc_ref, dw_ref, dg_ref, acc):
        i = pl.program_id(0)

        @pl.when(i == 0)
        def _():
            dg_ref[...] = jnp.zeros_like(dg_ref)
            acc[...] = jnp.zeros_like(acc)

        gr = gr_ref[...]
        dyn = _mm_nt(gr, w_ref[...])
        yns, dgs = [], []
        for (c0, c1), y_ref, d_ref in zip(_GROUPS, (a_ref, b_ref, c_ref), (da_ref, db_ref, dc_ref)):
            y = y_ref[...]
            r = _rms_stat(y)
            gg = g_ref[:, c0:c1]
            dx, dg = _rms_bwd(y, r, gg, dyn[:, c0:c1])
            d_ref[...] = dx
            dgs.append(dg)
            yns.append(y * r * gg)
        dg_ref[...] += jnp.concatenate(dgs, axis=1)
        acc[...] += _mm_tn(jnp.concatenate(yns, axis=1), gr)

        @pl.when(i == nt - 1)
        def _():
            dw_ref[...] = acc[...].astype(WIRE)

    def rows(w_):
        return pl.BlockSpec((tm, w_), lambda i: (i, 0))

    row = pl.BlockSpec((1, D_MODEL), lambda i: (0, 0))
    full = pl.BlockSpec((D_MODEL, D_MODEL), lambda i: (0, 0))
    return pl.pallas_call(
        body, name="merge_bwd", grid=(nt,),
        in_specs=[rows(D_MODEL), rows(A_WIDTH), rows(B_WIDTH), rows(C_WIDTH), row, full],
        out_specs=[rows(A_WIDTH), rows(B_WIDTH), rows(C_WIDTH), full, row],
        out_shape=[_sds((T, A_WIDTH)), _sds((T, B_WIDTH)), _sds((T, C_WIDTH)), _sds((D_MODEL, D_MODEL), WIRE), _sds((1, D_MODEL))],
        scratch_shapes=[pltpu.VMEM((D_MODEL, D_MODEL), F32)],
        compiler_params=_cp("arbitrary"))(gres, ya, yb, yc, g, w)


FF_BLOCK = D_FF // N_DEV


def _load_weights(w1_hbm, w2_hbm, w1, w2, sem):
    @pl.when(pl.program_id(0) == 0)
    def _():
        copies = [pltpu.make_async_copy(w1_hbm.at[j], w1.at[:, pl.ds(j * FF_BLOCK, FF_BLOCK)], sem.at[j]) for j in range(N_DEV)]
        copies.append(pltpu.make_async_copy(w2_hbm, w2, sem.at[N_DEV]))
        for cp in copies:
            cp.start()
        for cp in copies:
            cp.wait()


def _mlp_weight_scratch():
    return [pltpu.VMEM((D_MODEL, D_FF), MXU), pltpu.VMEM((D_FF, D_MODEL), MXU), pltpu.SemaphoreType.DMA((N_DEV + 1,))]


def _mlp_fwd(h, g, w1, w2):
    T = h.shape[0]
    tm = min(T, TOKEN_TILE)

    def body(h_ref, g_ref, w1_hbm, w2_hbm, o_ref, r_ref, w1_v, w2_v, sem):
        _load_weights(w1_hbm, w2_hbm, w1_v, w2_v, sem)
        x = h_ref[...]
        a = jnp.maximum(_mm(x * _rms_stat(x) * g_ref[...], w1_v[...]), 0.0)
        r = (a * a).astype(MXU)
        r_ref[...] = r
        o_ref[...] = x + _mm(r, w2_v[...])

    rows = pl.BlockSpec((tm, D_MODEL), lambda i: (i, 0))
    hbm = pl.BlockSpec(memory_space=pl.ANY)
    return pl.pallas_call(
        body, name="mlp_fwd", grid=(T // tm,),
        in_specs=[rows, pl.BlockSpec((1, D_MODEL), lambda i: (0, 0)), hbm, hbm],
        out_specs=[rows, pl.BlockSpec((tm, D_FF), lambda i: (i, 0))],
        out_shape=[_sds((T, D_MODEL)), _sds((T, D_FF), MXU)],
        scratch_shapes=_mlp_weight_scratch(), compiler_params=_cp("arbitrary"))(h, g, w1, w2)


def _mlp_bwd(gres, h, g, r, w1, w2):
    T = h.shape[0]
    tm = min(T, 256)

    def body(gr_ref, h_ref, g_ref, r_ref, w1_hbm, w2_hbm, dh_ref, hn_ref, da_ref, dg_ref, w1_v, w2_v, sem):
        _load_weights(w1_hbm, w2_hbm, w1_v, w2_v, sem)

        @pl.when(pl.program_id(0) == 0)
        def _():
            dg_ref[...] = jnp.zeros_like(dg_ref)

        gr = gr_ref[...]
        da = (_mm_nt(gr, w2_v[...]) * (2.0 * jnp.sqrt(r_ref[...].astype(F32)))).astype(MXU)
        da_ref[...] = da
        x = h_ref[...]
        rs = _rms_stat(x)
        gg = g_ref[...]
        dx, dg = _rms_bwd(x, rs, gg, _mm_nt(da, w1_v[...]))
        dh_ref[...] = gr + dx
        dg_ref[...] += dg
        hn_ref[...] = (x * rs * gg).astype(MXU)

    rows = pl.BlockSpec((tm, D_MODEL), lambda i: (i, 0))
    row = pl.BlockSpec((1, D_MODEL), lambda i: (0, 0))
    wide = pl.BlockSpec((tm, D_FF), lambda i: (i, 0))
    hbm = pl.BlockSpec(memory_space=pl.ANY)
    return pl.pallas_call(
        body, name="mlp_bwd", grid=(T // tm,),
        in_specs=[rows, rows, row, wide, hbm, hbm], out_specs=[rows, rows, wide, row],
        out_shape=[_sds((T, D_MODEL)), _sds((T, D_MODEL), MXU), _sds((T, D_FF), MXU), _sds((1, D_MODEL))],
        scratch_shapes=_mlp_weight_scratch(), compiler_params=_cp("arbitrary"))(gres, h, g, r, w1, w2)


def _ple_fwd(h, p, l, g, wg, wp):
    T = h.shape[0]
    tm = min(T, TOKEN_TILE)

    def body(h_ref, p_ref, g_ref, wg_ref, wp_ref, o_ref, gp_ref, e_ref):
        x = h_ref[...]
        gp = _mm(x * _rms_stat(x) * g_ref[...], wg_ref[...])
        e = _mm(p_ref[...], wp_ref[...])
        o_ref[...] = x + _sigmoid(gp) * e
        gp_ref[...] = gp.astype(MXU)
        e_ref[...] = e.astype(MXU)

    rows = pl.BlockSpec((tm, D_MODEL), lambda i: (i, 0))
    return pl.pallas_call(
        body, name="ple_fwd", grid=(T // tm,),
        in_specs=[rows, pl.BlockSpec((None, tm, PLE_DIM), lambda i: (l, i, 0)), pl.BlockSpec((1, D_MODEL), lambda i: (0, 0)),
                  pl.BlockSpec((D_MODEL, D_MODEL), lambda i: (0, 0)), pl.BlockSpec((PLE_DIM, D_MODEL), lambda i: (0, 0))],
        out_specs=[rows, rows, rows], out_shape=[_sds((T, D_MODEL)), _sds((T, D_MODEL), MXU), _sds((T, D_MODEL), MXU)],
        compiler_params=_cp("parallel"))(h, p, g, wg, wp)


def _ple_bwd(gres, h, p, l, g, gp, e, wg):
    T = h.shape[0]
    tm = min(T, TOKEN_TILE)
    nt = T // tm
    shard = D_MODEL // N_DEV

    def body(gr_ref, h_ref, p_ref, g_ref, gp_ref, e_ref, wg_ref, dh_ref, dwg_ref, dwp_ref, dg_ref, accg, accp):
        i = pl.program_id(0)

        @pl.when(i == 0)
        def _():
            dg_ref[...] = jnp.zeros_like(dg_ref)
            accg[...] = jnp.zeros_like(accg)
            accp[...] = jnp.zeros_like(accp)

        x = h_ref[...]
        r = _rms_stat(x)
        gg = g_ref[...]
        hn = x * r * gg
        gate = _sigmoid(gp_ref[...].astype(F32))
        pe = p_ref[...]
        e = e_ref[...].astype(F32)
        gr = gr_ref[...]
        dgp = gr * e * gate * (1.0 - gate)
        dx, dg = _rms_bwd(x, r, gg, _mm_nt(dgp, wg_ref[...]))
        dh_ref[...] = gr + dx
        dg_ref[...] += dg
        accg[...] += _mm_tn(hn, dgp)
        accp[...] += _mm_tn(pe, gr * gate)

        @pl.when(i == nt - 1)
        def _():
            dwg_ref[...] = accg[...].astype(WIRE)
            for d in range(N_DEV):
                dwp_ref[d] = accp[:, d * shard:(d + 1) * shard].astype(WIRE)

    rows = pl.BlockSpec((tm, D_MODEL), lambda i: (i, 0))
    row = pl.BlockSpec((1, D_MODEL), lambda i: (0, 0))
    full = pl.BlockSpec((D_MODEL, D_MODEL), lambda i: (0, 0))
    return pl.pallas_call(
        body, name="ple_bwd", grid=(nt,),
        in_specs=[rows, rows, pl.BlockSpec((None, tm, PLE_DIM), lambda i: (l, i, 0)), row, rows, rows, full],
        out_specs=[rows, full, pl.BlockSpec((N_DEV, PLE_DIM, shard), lambda i: (0, 0, 0)), row],
        out_shape=[_sds((T, D_MODEL)), _sds((D_MODEL, D_MODEL), WIRE), _sds((N_DEV, PLE_DIM, shard), WIRE), _sds((1, D_MODEL))],
        scratch_shapes=[pltpu.VMEM((D_MODEL, D_MODEL), F32), pltpu.VMEM((PLE_DIM, D_MODEL), F32)],
        compiler_params=_cp("arbitrary"))(gres, h, p, g, gp, e, wg)


def _loss_head(h, target):
    T = h.shape[0]
    tm = min(T, 1024)

    def body(h_ref, t_ref, dh_ref, l_ref):
        @pl.when(pl.program_id(0) == 0)
        def _():
            l_ref[...] = jnp.zeros_like(l_ref)

        e = h_ref[...] - t_ref[...]
        dh_ref[...] = e * (1.0 / D_MODEL)
        l_ref[...] += jnp.zeros_like(l_ref) + 0.5 * jnp.sum(jnp.mean(e * e, axis=-1, keepdims=True))

    rows = pl.BlockSpec((tm, D_MODEL), lambda i: (i, 0))
    return pl.pallas_call(
        body, name="loss_head", grid=(T // tm,), in_specs=[rows, rows],
        out_specs=[rows, pl.BlockSpec((8, LANES), lambda i: (0, 0))], out_shape=[_sds((T, D_MODEL)), _sds((8, LANES))],
        compiler_params=_cp("arbitrary"))(h, target)


TN_ROWS = 2048

def _tn(a, b, m, n, *, bm, bn, n_major=False, dtype=F32, name="tn"):
    T = a.shape[0]
    tk = min(T, TN_ROWS)
    nk = T // tk
    assert m % bm == 0 and n % bn == 0 and (not n_major or bm == m)

    def body(a_ref, b_ref, o_ref, acc):
        k = pl.program_id(2)

        @pl.when(k == 0)
        def _():
            acc[...] = jnp.zeros_like(acc)

        acc[...] += _mm_tn(a_ref[...], b_ref[...])

        @pl.when(k == nk - 1)
        def _():
            o_ref[...] = acc[...].astype(dtype)

    if n_major:
        out_spec = pl.BlockSpec((None, bm, bn), lambda i, j, k: (j, 0, 0))
        out_shape = _sds((n // bn, m, bn), dtype)
    else:
        out_spec = pl.BlockSpec((bm, bn), lambda i, j, k: (i, j))
        out_shape = _sds((m, n), dtype)
    return pl.pallas_call(
        body, name=name, grid=(m // bm, n // bn, nk),
        in_specs=[pl.BlockSpec((tk, bm), lambda i, j, k: (k, i)), pl.BlockSpec((tk, bn), lambda i, j, k: (k, j))],
        out_specs=out_spec, out_shape=out_shape, scratch_shapes=[pltpu.VMEM((bm, bn), F32)],
        compiler_params=_cp("parallel", "parallel", "arbitrary"))(a, b)


def _row_tile(R, C):
    for cand in (512, 256, 128, 64, 32, 16, 8):
        if R % cand == 0 and cand * C * 4 <= 2 ** 20:
            return cand
    return R


def _sum_slots(land):
    S, R, C = land.shape
    tr = _row_tile(R, C)

    def body(l_ref, o_ref):
        acc = l_ref[0].astype(F32)
        for s in range(1, S):
            acc = acc + l_ref[s].astype(F32)
        o_ref[...] = acc

    return pl.pallas_call(
        body, name="sum_slots", grid=(R // tr,), in_specs=[pl.BlockSpec((S, tr, C), lambda i: (0, i, 0))],
        out_specs=pl.BlockSpec((tr, C), lambda i: (i, 0)), out_shape=_sds((R, C)), compiler_params=_cp("parallel"))(land)


def _sum_adamw(land, w, m, v):
    R, C = w.shape
    S = land.shape[0]
    tr = _row_tile(R, C)

    def body(l_ref, w_ref, m_ref, v_ref, g_ref, d_ref, nm_ref, nv_ref):
        gg = l_ref[0].astype(F32)
        for s in range(1, S):
            gg = gg + l_ref[s].astype(F32)
        g_ref[...] = gg
        nm = ADAM_B1 * m_ref[...] + (1.0 - ADAM_B1) * gg
        nv = ADAM_B2 * v_ref[...] + (1.0 - ADAM_B2) * (gg * gg)
        m_hat = nm / (1.0 - ADAM_B1 ** ADAM_STEP)
        v_hat = nv / (1.0 - ADAM_B2 ** ADAM_STEP)
        d_ref[...] = -ADAM_LR * (m_hat / (jnp.sqrt(v_hat) + ADAM_EPS) + ADAM_WD * w_ref[...])
        nm_ref[...] = nm
        nv_ref[...] = nv

    blk = pl.BlockSpec((tr, C), lambda i: (i, 0))
    return pl.pallas_call(
        body, name="sum_adamw", grid=(R // tr,), in_specs=[pl.BlockSpec((S, tr, C), lambda i: (0, i, 0))] + [blk] * 3,
        out_specs=[blk] * 4, out_shape=[_sds((R, C))] * 4, compiler_params=_cp("parallel"))(land, w, m, v)


_HBM = pl.BlockSpec(memory_space=pltpu.HBM)
_SEM = pl.BlockSpec(memory_space=pltpu.SEMAPHORE)
_EFFECT = pltpu.SideEffectType.DATAFLOW_SIDE_EFFECTING


def _peers():
    x, y, c = lax.axis_index("x"), lax.axis_index("y"), lax.axis_index("c")
    out = []
    for k in range(1, N_DEV):
        px, py, pc = x ^ (k >> 2), y ^ ((k >> 1) & 1), c ^ (k & 1)
        out.append((k, (px, py, pc), 4 * px + 2 * py + pc))
    return 4 * x + 2 * y + c, out


def _route(mode, layer):
    if mode == "all":
        return (lambda ref, peer: ref), (lambda ref, sender: ref.at[sender])
    return (lambda ref, peer: ref.at[peer]), (lambda ref, sender: ref.at[sender, layer])


def _split_start(srcs, lands, modes, layer, after, name):
    n = len(srcs)
    routes = [_route(m, layer) for m in modes]

    def body(*refs):
        src, land = refs[:n], refs[n:2 * n]
        send, recv, token = refs[2 * n + 1], refs[2 * n + 2], refs[-1]
        me, peers = _peers()
        for k, dev, peer in peers:
            for t, (src_of, dst_of) in enumerate(routes):
                pltpu.make_async_remote_copy(src_ref=src_of(src[t], peer), dst_ref=dst_of(land[t], me), send_sem=send.at[t * N_DEV + k],
                                             recv_sem=recv.at[t * N_DEV + k], device_id=dev, device_id_type=MESH).start()
        token[...] = jnp.zeros_like(token)

    bufs = list(srcs) + list(lands)
    outs = pl.pallas_call(
        body, name=name,
        out_shape=(pltpu.SemaphoreType.DMA((n * N_DEV,)), pltpu.SemaphoreType.DMA((n * N_DEV,)),
                   *[pltpu.HBM(a.shape, a.dtype) for a in bufs], _sds((8, LANES))),
        in_specs=[_HBM] * (2 * n) + [pl.BlockSpec(memory_space=pl.ANY)],
        out_specs=(_SEM, _SEM, *[_HBM] * (2 * n), pl.BlockSpec(memory_space=pltpu.VMEM)),
        input_output_aliases={i: 2 + i for i in range(2 * n)},
        compiler_params=pltpu.CompilerParams(has_side_effects=_EFFECT),
    )(*[pltpu.with_memory_space_constraint(a, pltpu.HBM) for a in bufs], after)
    return outs[0], outs[1], list(outs[2:2 + n]), list(outs[2 + n:2 + 2 * n]), outs[-1]


def _split_wait(send, recv, srcs, lands, modes, layer, after, name):
    n = len(srcs)
    routes = [_route(m, layer) for m in modes]

    def body(*refs):
        src, land = refs[:n], refs[n:2 * n]
        send_r, recv_r = refs[2 * n], refs[2 * n + 1]
        _, peers = _peers()
        for k, dev, peer in peers:
            for t, (src_of, dst_of) in enumerate(routes):
                cp = pltpu.make_async_remote_copy(src_ref=src_of(src[t], peer), dst_ref=dst_of(land[t], peer), send_sem=send_r.at[t * N_DEV + k],
                                                  recv_sem=recv_r.at[t * N_DEV + k], device_id=dev, device_id_type=MESH)
                cp.wait_send()
                cp.wait_recv()

    bufs = list(srcs) + list(lands)
    outs = pl.pallas_call(
        body, name=name, out_shape=tuple(pltpu.HBM(a.shape, a.dtype) for a in bufs),
        in_specs=[_HBM] * (2 * n) + [_SEM, _SEM, pl.BlockSpec(memory_space=pl.ANY)], out_specs=[_HBM] * (2 * n),
        input_output_aliases={i: i for i in range(2 * n)},
        compiler_params=pltpu.CompilerParams(has_side_effects=_EFFECT),
    )(*bufs, send, recv, after)
    return list(outs[:n]), list(outs[n:])


SHARDED = ("w_in", "glu_w1", "glu_w2", "w_out", "w_ff1", "w_ff2", "w_ple_gate", "w_ple_proj")
SMALL = ("attn_norm_g", "gmlp_ln_g", "gmlp_ln_b", "gmlp_ws", "gmlp_bs", "q_norm_g", "k_norm_g", "sinks", "ssm_a_re", "ssm_a_im",
         "ssm_log_dt", "ssm_b_re", "ssm_b_im", "ssm_c_re", "ssm_c_im", "ssm_d", "mix_out_g", "mlp_norm_g", "ple_norm_g")
WEIGHTS = ("attn_norm_g", "w_in", "gmlp_ln_g", "gmlp_ln_b", "gmlp_ws", "gmlp_bs", "q_norm_g", "k_norm_g", "sinks", "ssm_a_re", "ssm_a_im",
           "ssm_log_dt", "ssm_b_re", "ssm_b_im", "ssm_c_re", "ssm_c_im", "ssm_d", "glu_w1", "glu_w2", "mix_out_g", "w_out", "mlp_norm_g",
           "w_ff1", "w_ff2", "ple_norm_g", "w_ple_gate", "w_ple_proj")
FLAT_COLS = 1024


PACK_TILE_ROWS = 8
PACK_ROWS_MULTIPLE = PACK_TILE_ROWS * N_DEV


def _packed_rows(shape):
    return -(-math.prod(shape) // (PACK_TILE_ROWS * FLAT_COLS)) * PACK_TILE_ROWS


def _pack(arrs, dtype):
    blocks = []
    for a in arrs:
        flat = a.astype(dtype).reshape(-1)
        pad = _packed_rows(a.shape) * FLAT_COLS - flat.shape[0]
        if pad:
            flat = jnp.concatenate([flat, jnp.zeros((pad,), dtype)])
        blocks.append(flat.reshape(-1, FLAT_COLS))
    rows = sum(b.shape[0] for b in blocks)
    if rows % PACK_ROWS_MULTIPLE:
        blocks.append(jnp.zeros((PACK_ROWS_MULTIPLE - rows % PACK_ROWS_MULTIPLE, FLAT_COLS), dtype))
    return jnp.concatenate(blocks, axis=0)


def _unpack(flat, shapes):
    out, r = [], 0
    for s in shapes:
        nr = _packed_rows(s)
        out.append(flat[r:r + nr].reshape(-1)[:math.prod(s)].reshape(s))
        r += nr
    return out


def _from_col_major(s):
    n, rows, cs = s.shape
    return s.transpose(1, 0, 2).reshape(rows, n * cs)


EARLY = ("w_in", "glu_w1", "glu_w2")
LATE = ("w_out", "w_ff1", "w_ff2", "w_ple_gate", "w_ple_proj")
GRADS_MID = ("w_ff1", "w_ff2", "w_ple_gate", "w_ple_proj")
GRADS_END = ("w_in", "glu_w1", "glu_w2", "w_out")


def _layer_fwd(h, p, l, cs, sn, W, late_weights, S, sp):
    z = _inproj_fwd(h, S["attn_norm_g"], W["w_in"])
    ya = _gmlp_fwd(z, S["lng"], S["lnb"], S["gmlp_ws"], S["bsx"])
    yb = _attn_fwd(z, cs, sn, S["qg"], S["kg"], S["sinks"])
    yc, y, xr, xi = _ssm_fwd(z, sp["bbd"], sp["pwr"], sp["pwi"], sp["cbd"], S["ssm_d"], W["glu_w1"], W["glu_w2"])
    late, token = late_weights(ya[0:8, 0:LANES] + yb[0:8, 0:LANES] + yc[0:8, 0:LANES])
    W = {**W, **late}
    h1 = _merge_fwd(h, ya, yb, yc, _behind(S["mix_out_g"], token), W["w_out"])
    h2, r = _mlp_fwd(h1, S["mlp_norm_g"], W["w_ff1"], W["w_ff2"])
    h3, gp, e = _ple_fwd(h2, p, l, S["ple_norm_g"], W["w_ple_gate"], W["w_ple_proj"])
    return h3, dict(h=h, z=z, ya=ya, yb=yb, yc=yc, y=y, xr=xr, xi=xi, h1=h1, r=r, h2=h2, gp=gp, e=e), W


def _layer_bwd(g3, p, l, cs, sn, W, S, sp, A, raw, mid_bwd):
    G = {}
    g2, dwg, G["w_ple_proj"], G["ple_norm_g"] = _ple_bwd(g3, A["h2"], p, l, S["ple_norm_g"], A["gp"], A["e"], W["w_ple_gate"])
    G["w_ple_gate"] = dwg.reshape(N_DEV, -1, D_MODEL)
    g1, hn, da, G["mlp_norm_g"] = _mlp_bwd(g2, A["h1"], S["mlp_norm_g"], A["r"], W["w_ff1"], W["w_ff2"])
    G["w_ff1"] = _tn(hn, da, D_MODEL, D_FF, bm=D_MODEL, bn=FF_BLOCK, n_major=True, dtype=WIRE, name="tn_ff1")
    G["w_ff2"] = _tn(A["r"], g2, D_FF, D_MODEL, bm=D_MODEL, bn=D_MODEL, dtype=WIRE, name="tn_ff2").reshape(N_DEV, -1, D_MODEL)
    token = mid_bwd(g1, G)
    dya, dyb, dyc, dwo, G["mix_out_g"] = _merge_bwd(g1, A["ya"], A["yb"], A["yc"], _behind(S["mix_out_g"], token), W["w_out"])
    G["w_out"] = dwo.reshape(N_DEV, -1, D_MODEL)
    dzc, dw1, dw2, dcr, dci, dbbd, dlr, dli, dd = _ssm_bwd(dyc, A["z"], A["y"], A["xr"], A["xi"], sp["bbd"], sp["pwr"], sp["pwi"], sp["cbd"],
                                                          S["ssm_d"], W["glu_w1"], W["glu_w2"])
    G["glu_w1"] = dw1.astype(WIRE).reshape(N_DEV, -1, C_WIDTH)
    G["glu_w2"] = dw2.astype(WIRE).reshape(N_DEV, -1, C_WIDTH)
    dare, daim, dldt, dbtr, dbti, dcre, dcim = _ssm_param_bwd(
        raw["are"], raw["aim"], raw["ldt"], raw["are_x"], raw["aim_x"], raw["ldt_x"], raw["btr"], raw["bti"],
        dlr.reshape(C_GROUPS, C_STATE), dli.reshape(C_GROUPS, C_STATE), dbbd, dcr, dci)
    G["ssm_a_re"], G["ssm_a_im"], G["ssm_log_dt"] = dare, daim, dldt[:, 0]
    G["ssm_b_re"] = dbtr.reshape(C_GROUPS, C_GROUP, C_STATE).transpose(0, 2, 1)
    G["ssm_b_im"] = dbti.reshape(C_GROUPS, C_GROUP, C_STATE).transpose(0, 2, 1)
    G["ssm_c_re"] = dcre.reshape(C_GROUPS, C_GROUP, C_STATE)
    G["ssm_c_im"] = dcim.reshape(C_GROUPS, C_GROUP, C_STATE)
    G["ssm_d"] = dd.reshape(C_GROUPS, C_GROUP)
    dzq, dzk, dzv, dqg, dkg, dsk = _attn_bwd(A["z"], cs, sn, S["qg"], S["kg"], S["sinks"], A["yb"], dyb)
    G["q_norm_g"] = dqg[0, :HEAD_DIM] + dqg[0, HEAD_DIM:]
    G["k_norm_g"] = dkg[0, :HEAD_DIM] + dkg[0, HEAD_DIM:]
    G["sinks"] = dsk[:, 0]
    dza, dws, dbs, dlng, dlnb = _gmlp_bwd(A["z"], dya, S["lng"], S["lnb"], S["gmlp_ws"], S["bsx"])
    G["gmlp_ws"] = dws
    G["gmlp_bs"] = dbs[:, :, 0]
    G["gmlp_ln_g"] = dlng.reshape(A_HEADS, 2, HEAD_DIM)[:, 1]
    G["gmlp_ln_b"] = dlnb.reshape(A_HEADS, 2, HEAD_DIM)[:, 1]
    g0, G["w_in"], G["attn_norm_g"] = _inproj_bwd(g1, A["h"], S["attn_norm_g"], W["w_in"], dza, dzq, dzk, dzv, dzc)
    return g0, G


def _small_layouts(P, l):
    def row(a):
        return a.reshape(1, -1)

    zeros = jnp.zeros((A_HEADS, HEAD_DIM), F32)
    S = dict(
        attn_norm_g=row(P["attn_norm_g"][l]), mix_out_g=row(P["mix_out_g"][l]), mlp_norm_g=row(P["mlp_norm_g"][l]),
        ple_norm_g=row(P["ple_norm_g"][l]),
        lng=jnp.stack([zeros, P["gmlp_ln_g"][l]], axis=1).reshape(1, IN_A),
        lnb=jnp.stack([zeros, P["gmlp_ln_b"][l]], axis=1).reshape(1, IN_A),
        gmlp_ws=P["gmlp_ws"][l],
        bsx=jnp.broadcast_to(P["gmlp_bs"][l][:, :, None], (A_HEADS, CHUNK, CHUNK)),
        qg=jnp.tile(P["q_norm_g"][l], 2).reshape(1, LANES), kg=jnp.tile(P["k_norm_g"][l], 2).reshape(1, LANES),
        sinks=jnp.broadcast_to(P["sinks"][l][:, None], (8, LANES)),
        ssm_d=row(P["ssm_d"][l]),
    )
    are, aim = P["ssm_a_re"][l], P["ssm_a_im"][l]
    ldt = jnp.broadcast_to(P["ssm_log_dt"][l][:, None], (C_GROUPS, C_STATE))
    raw = dict(
        are=are, aim=aim, ldt=ldt,
        are_x=jnp.repeat(are, C_GROUP, axis=0), aim_x=jnp.repeat(aim, C_GROUP, axis=0), ldt_x=jnp.repeat(ldt, C_GROUP, axis=0),
        btr=P["ssm_b_re"][l].transpose(0, 2, 1).reshape(C_WIDTH, C_STATE), bti=P["ssm_b_im"][l].transpose(0, 2, 1).reshape(C_WIDTH, C_STATE),
        cre=P["ssm_c_re"][l].reshape(C_WIDTH, C_STATE), cim=P["ssm_c_im"][l].reshape(C_WIDTH, C_STATE),
    )
    return S, raw


def _ssm_prep_layer(raw):
    bbd, cbd, pwr, pwi = _ssm_prep(raw["are"].reshape(1, N_STATE), raw["aim"].reshape(1, N_STATE), raw["ldt"].reshape(1, N_STATE),
                                   raw["are_x"], raw["aim_x"], raw["ldt_x"], raw["btr"], raw["bti"], raw["cre"], raw["cim"])
    return dict(bbd=bbd, cbd=cbd, pwr=pwr, pwi=pwi)


def _behind(row, token):
    return row if token is None else row + token[0:1, 0:1]


def _local_step(x, p, positions, target, P, weights_of, mid_bwd, after_bwd):
    inv = 1.0 / (ROPE_THETA ** (jnp.arange(0, HEAD_DIM, 2, dtype=F32) / HEAD_DIM))
    cs, sn = _rope_tables(positions.reshape(-1, 1), jnp.tile(inv, 4).reshape(1, LANES))
    h = x
    acts, smalls, weights = [], [], []
    for l in range(DEPTH):
        W, late_weights, token = weights_of(l, h)
        S, raw = _small_layouts(P, l)
        sp = _ssm_prep_layer(raw)
        h, A, W = _layer_fwd(h, p, l, cs, sn, W, late_weights, {**S, "attn_norm_g": _behind(S["attn_norm_g"], token)}, sp)
        acts.append(A)
        smalls.append((S, raw, sp))
        weights.append(W)
    g, lsum = _loss_head(h, target)
    grads = [None] * DEPTH
    token = None
    for l in reversed(range(DEPTH)):
        S, raw, sp = smalls[l]
        g, grads[l] = _layer_bwd(g, p, l, cs, sn, weights[l], {**S, "ple_norm_g": _behind(S["ple_norm_g"], token)}, sp, acts[l], raw,
                                 functools.partial(mid_bwd, l))
        token = after_bwd(l, g, grads[l])
    return lsum[0, 0], g, grads


def _layer_weights(g):
    layout = dict(
        w_in=_from_col_major, glu_w1=lambda a: a.reshape(C_WIDTH, C_WIDTH), glu_w2=lambda a: a.reshape(C_WIDTH, C_WIDTH),
        w_out=lambda a: a.reshape(D_MODEL, D_MODEL), w_ff1=lambda a: a, w_ff2=lambda a: a.reshape(D_FF, D_MODEL),
        w_ple_gate=lambda a: a.reshape(D_MODEL, D_MODEL), w_ple_proj=_from_col_major)
    return {n: layout[n](a) for n, a in g.items()}


def kernel(x, p, positions, attn_norm_g, w_in, gmlp_ln_g, gmlp_ln_b, gmlp_ws, gmlp_bs, q_norm_g, k_norm_g, sinks, ssm_a_re, ssm_a_im, ssm_log_dt, ssm_b_re, ssm_b_im, ssm_c_re, ssm_c_im, ssm_d, glu_w1, glu_w2, mix_out_g, w_out, mlp_norm_g, w_ff1, w_ff2, ple_norm_g, w_ple_gate, w_ple_proj, loss_target, m_attn_norm_g, m_w_in, m_gmlp_ln_g, m_gmlp_ln_b, m_gmlp_ws, m_gmlp_bs, m_q_norm_g, m_k_norm_g, m_sinks, m_ssm_a_re, m_ssm_a_im, m_ssm_log_dt, m_ssm_b_re, m_ssm_b_im, m_ssm_c_re, m_ssm_c_im, m_ssm_d, m_glu_w1, m_glu_w2, m_mix_out_g, m_w_out, m_mlp_norm_g, m_w_ff1, m_w_ff2, m_ple_norm_g, m_w_ple_gate, m_w_ple_proj, v_attn_norm_g, v_w_in, v_gmlp_ln_g, v_gmlp_ln_b, v_gmlp_ws, v_gmlp_bs, v_q_norm_g, v_k_norm_g, v_sinks, v_ssm_a_re, v_ssm_a_im, v_ssm_log_dt, v_ssm_b_re, v_ssm_b_im, v_ssm_c_re, v_ssm_c_im, v_ssm_d, v_glu_w1, v_glu_w2, v_mix_out_g, v_w_out, v_mlp_norm_g, v_w_ff1, v_w_ff2, v_ple_norm_g, v_w_ple_gate, v_w_ple_proj):
    env = dict(locals())
    P = {n: env[n] for n in WEIGHTS}
    M = {n: env["m_" + n] for n in WEIGHTS}
    V = {n: env["v_" + n] for n in WEIGHTS}
    return _step(x, p, positions, loss_target, P, M, V)


def _step(x, p, positions, loss_target, P, M, V):
    small_shapes = [P[n].shape for n in SMALL]
    me = 4 * lax.axis_index("x") + 2 * lax.axis_index("y") + lax.axis_index("c")
    nothing = jnp.zeros((8, LANES), F32)

    def put(land, own, lead):
        return lax.dynamic_update_slice(land, own.reshape((1,) * len(lead) + own.shape), tuple(lead) + (0,) * own.ndim)

    def gather_start(l, names, after, tag):
        shards = [P[n][l].astype(WIRE) for n in names]
        lands = [lax.empty((N_DEV,) + s.shape, WIRE) for s in shards]
        send, recv, shards, lands, token = _split_start(shards, lands, ["all"] * len(names), 0, after, f"gather_start_{l}{tag}")
        return dict(names=names, send=send, recv=recv, shards=shards, lands=lands, token=token, name=f"gather_wait_{l}{tag}")

    def gather_wait(f, after):
        shards, lands = _split_wait(f["send"], f["recv"], f["shards"], f["lands"], ["all"] * len(f["names"]), 0, after, f["name"])
        return dict(zip(f["names"], [put(ld, sh, (me,)) for sh, ld in zip(shards, lands)]))

    first = gather_start(0, EARLY, nothing, "a")
    flying = {0: (first, gather_start(0, LATE, first["token"], "b"))}

    def weights_of(l, h):
        fa, fb = flying.pop(l)
        got = gather_wait(fa, h)
        if fb is None:
            token = None
            if l + 1 < DEPTH:
                flying[l + 1] = (gather_start(l + 1, SHARDED, got["w_in"], ""), None)
                token = flying[l + 1][0]["token"]
            W = _layer_weights(got)
            return {n: W[n] for n in EARLY}, (lambda after: ({n: W[n] for n in LATE}, None)), token

        def late_weights(after):
            late = gather_wait(fb, after)
            flying[l + 1] = (gather_start(l + 1, SHARDED, late["w_out"], ""), None)
            return _layer_weights(late), flying[l + 1][0]["token"]

        return _layer_weights(got), late_weights, fb["token"]

    grad_lands = {n: lax.empty((N_DEV,) + P[n].shape, WIRE) for n in SHARDED}
    sent = []

    def scatter_start(l, names, parts, lands, tag):
        send, recv, parts, lands, token = _split_start(parts, lands, ["own"] * len(parts), l, nothing, f"scatter_start_{l}{tag}")
        sent.append(dict(l=l, names=names, send=send, recv=recv, parts=parts, lands=lands, name=f"scatter_wait_{l}{tag}"))
        return token

    def scatter_wait(after):
        f = sent.pop(0)
        parts, lands = _split_wait(f["send"], f["recv"], f["parts"], f["lands"], ["own"] * len(f["parts"]), f["l"], after, f["name"])
        lands = [put(ld, lax.dynamic_index_in_dim(part, me, 0, keepdims=False), (me, f["l"])) for part, ld in zip(parts, lands)]
        return dict(zip(f["names"], lands))

    def mid_bwd(l, g1, G):
        if l > 0:
            return None
        grad_lands.update(scatter_wait(g1))
        return scatter_start(0, GRADS_MID, [G[n] for n in GRADS_MID], [grad_lands[n] for n in GRADS_MID], "a")

    small_land = []

    def after_bwd(l, g, G):
        if l > 0:
            if sent:
                grad_lands.update(scatter_wait(g))
            return scatter_start(l, SHARDED, [G[n] for n in SHARDED], [grad_lands[n] for n in SHARDED], "")
        sflat = _pack([jnp.stack([grads_of[k][n] for k in range(DEPTH)]) for n in SMALL], F32)
        sparts = sflat.reshape(N_DEV, -1, FLAT_COLS)
        small_land.append(sflat.shape)
        return scatter_start(0, GRADS_END + ("small",), [G[n] for n in GRADS_END] + [sparts],
                             [grad_lands[n] for n in GRADS_END] + [lax.empty((N_DEV, 1) + sparts.shape[1:], F32)], "b")

    grads_of = {}

    def after_bwd_recording(l, g, G):
        grads_of[l] = G
        grads_of["token"] = after_bwd(l, g, G)
        return grads_of["token"]

    lsum, gx, grads = _local_step(x[0], p[:, 0], positions[0], loss_target[0], P, weights_of, mid_bwd, after_bwd_recording)
    G, delta, new_m, new_v = {}, {}, {}, {}

    def update(names):
        for n in names:
            shp = P[n].shape
            res = _sum_adamw(grad_lands[n].reshape(N_DEV, -1, shp[-1]), *(a.reshape(-1, shp[-1]) for a in (P[n], M[n], V[n])))
            G[n], delta[n], new_m[n], new_v[n] = (a.reshape(shp) for a in res)

    grad_lands.update(scatter_wait(grads_of["token"]))
    update(GRADS_MID)
    last = scatter_wait(sum(new_v[n].reshape(-1, LANES)[:8] for n in GRADS_MID))
    small_parts = last.pop("small")
    grad_lands.update(last)
    mine = _sum_slots(small_parts[:, 0])
    send, recv, srcs, lands, _ = _split_start([mine], [lax.empty((N_DEV,) + mine.shape, F32)], ["all"], 0, nothing, "gather_small_start")
    update(GRADS_END)
    srcs, lands = _split_wait(send, recv, srcs, lands, ["all"], 0, sum(new_v[n].reshape(-1, LANES)[:8] for n in GRADS_END), "gather_small_wait")
    small_sum = put(lands[0], srcs[0], (me,)).reshape((1,) + small_land[0])
    res = _sum_adamw(small_sum, _pack([P[n] for n in SMALL], F32), _pack([M[n] for n in SMALL], F32), _pack([V[n] for n in SMALL], F32))
    for dst, flat in zip((G, delta, new_m, new_v), res):
        dst.update(zip(SMALL, _unpack(flat, small_shapes)))
    loss = lax.psum(lsum, ("x", "y", "c"))
    return (loss, gx[None], *[G[n] for n in WEIGHTS], *[delta[n] for n in WEIGHTS], *[new_m[n] for n in WEIGHTS], *[new_v[n] for n in WEIGHTS])
```

```python
import functools
import math

import jax
import jax.numpy as jnp
from jax import lax
from jax.experimental import pallas as pl
from jax.experimental.pallas import tpu as pltpu

F32 = jnp.float32
MXU = jnp.bfloat16
WIRE = jnp.bfloat16

D_MODEL = 1024
DEPTH = 4
HEAD_DIM = 64
A_WIDTH = 256
A_HEADS = 4
CHUNK = 128
B_WIDTH = 512
WINDOW = 128
C_WIDTH = 256
C_GROUP = 16
C_GROUPS = 16
C_STATE = 64
N_STATE = C_GROUPS * C_STATE
IN_A = 2 * A_WIDTH
KV_WIDTH = 2 * HEAD_DIM
IN_COLS = IN_A + B_WIDTH + 2 * KV_WIDTH + C_WIDTH
Q_BLOCK = IN_A // B_WIDTH
K_BLOCK = (IN_A + B_WIDTH) // KV_WIDTH
V_BLOCK = K_BLOCK + 1
C_BLOCK = (IN_A + B_WIDTH + 2 * KV_WIDTH) // C_WIDTH
TOKEN_TILE = 512
D_FF = 4096
PLE_DIM = 256
EPS = 1e-6
ROPE_THETA = 10000.0
SCALE = HEAD_DIM ** -0.5
NEG = -1e30
N_DEV = 8

ADAM_LR = 0.001
ADAM_B1 = 0.9
ADAM_B2 = 0.999
ADAM_EPS = 1e-08
ADAM_WD = 0.01
ADAM_STEP = 10

V7X_VMEM_BYTES = 64 * 2 ** 20
VMEM_LIMIT = V7X_VMEM_BYTES - 8 * 2 ** 20
LANES = 128

MESH = pl.DeviceIdType.MESH


def _cp(*sem):
    return pltpu.CompilerParams(dimension_semantics=sem, vmem_limit_bytes=VMEM_LIMIT)


def _sds(shape, dtype=F32):
    return jax.ShapeDtypeStruct(shape, dtype)


def _mm(a, b):
    return jnp.dot(a.astype(MXU), b.astype(MXU), preferred_element_type=F32)


def _mm_nt(a, b):
    return lax.dot_general(a.astype(MXU), b.astype(MXU), (((1,), (1,)), ((), ())), preferred_element_type=F32)


def _mm_tn(a, b):
    return lax.dot_general(a.astype(MXU), b.astype(MXU), (((0,), (0,)), ((), ())), preferred_element_type=F32)


def _lane(shape):
    return lax.broadcasted_iota(jnp.int32, shape, len(shape) - 1)


def _row(shape):
    return lax.broadcasted_iota(jnp.int32, shape, 0)


_GELU_C = math.sqrt(2.0 / math.pi)


def _gelu(x):
    return 0.5 * x * (1.0 + jnp.tanh(_GELU_C * (x + 0.044715 * (x * x * x))))


def _gelu_grad(x):
    t = jnp.tanh(_GELU_C * (x + 0.044715 * (x * x * x)))
    return 0.5 * (1.0 + t) + 0.5 * x * (1.0 - t * t) * (_GELU_C * (1.0 + 3.0 * 0.044715 * (x * x)))


def _sigmoid(x):
    return 1.0 / (1.0 + jnp.exp(-x))


def _rms_stat(x):
    return lax.rsqrt(jnp.mean(x * x, axis=-1, keepdims=True) + EPS)


def _rms_bwd(x, r, g, dy):
    xh = x * r
    dxh = dy * g
    dx = r * (dxh - xh * jnp.mean(dxh * xh, axis=-1, keepdims=True))
    return dx, jnp.sum(dy * xh, axis=0, keepdims=True)


def _tril(w):
    return jnp.where(_row(w.shape) >= _lane(w.shape), w, 0.0)


def _swap64(x):
    return pltpu.roll(x, HEAD_DIM, 1)


def _group_sum64(x, lo):
    s_lo = jnp.sum(jnp.where(lo, x, 0.0), axis=-1, keepdims=True)
    s_hi = jnp.sum(jnp.where(lo, 0.0, x), axis=-1, keepdims=True)
    return jnp.where(lo, s_lo, s_hi)


def _partner(x):
    n = x.shape[-1]
    first = (_lane(x.shape) % HEAD_DIM) < HEAD_DIM // 2
    return jnp.where(first, pltpu.roll(x, n - HEAD_DIM // 2, 1), pltpu.roll(x, HEAD_DIM // 2, 1))


def _rope(y, cs, sn):
    return y * cs + _partner(y) * sn


def _rope_bwd(d, cs, sn):
    return d * cs + _partner(d * sn)


def _qk_norm_rope(x, g, cs, sn):
    lo = _lane(x.shape) < HEAD_DIM
    r = lax.rsqrt(_group_sum64(x * x, lo) * (1.0 / HEAD_DIM) + EPS)
    xh = x * r
    return _rope(xh * g, cs, sn), xh, r


def _qk_norm_rope_bwd(xh, r, g, cs, sn, d):
    lo = _lane(xh.shape) < HEAD_DIM
    dy = _rope_bwd(d, cs, sn)
    dxh = dy * g
    m = _group_sum64(dxh * xh, lo) * (1.0 / HEAD_DIM)
    return r * (dxh - xh * m), jnp.sum(dy * xh, axis=0, keepdims=True)


def _gmlp_head(blk, g, b):
    hi = _lane(blk.shape) >= HEAD_DIM
    mu = jnp.sum(jnp.where(hi, blk, 0.0), axis=-1, keepdims=True) * (1.0 / HEAD_DIM)
    xc = jnp.where(hi, blk - mu, 0.0)
    rstd = lax.rsqrt(jnp.sum(xc * xc, axis=-1, keepdims=True) * (1.0 / HEAD_DIM) + EPS)
    vhat = xc * rstd
    return vhat * g + b, vhat, rstd


def _rope_tables(pos_col, inv_row):
    T = pos_col.shape[0]
    tm = min(T, 1024)

    def body(p_ref, inv_ref, cs_ref, sn_ref):
        ang = p_ref[...].astype(F32) * inv_ref[...]
        s = jnp.sin(ang)
        cs_ref[...] = jnp.cos(ang)
        sn_ref[...] = jnp.where((_lane(ang.shape) % HEAD_DIM) < HEAD_DIM // 2, -s, s)

    blk = pl.BlockSpec((tm, LANES), lambda i: (i, 0))
    return pl.pallas_call(
        body, name="rope_tables", grid=(T // tm,),
        in_specs=[pl.BlockSpec((tm, 1), lambda i: (i, 0)), pl.BlockSpec((1, LANES), lambda i: (0, 0))],
        out_specs=[blk, blk], out_shape=[_sds((T, LANES))] * 2, compiler_params=_cp("parallel"))(pos_col, inv_row)


def _inproj_fwd(h, g, w):
    T = h.shape[0]
    tm = min(T, TOKEN_TILE)

    def body(h_ref, g_ref, w_ref, z_ref):
        x = h_ref[...]
        z_ref[...] = _mm(x * _rms_stat(x) * g_ref[...], w_ref[...])

    return pl.pallas_call(
        body, name="inproj_fwd", grid=(T // tm,),
        in_specs=[pl.BlockSpec((tm, D_MODEL), lambda i: (i, 0)), pl.BlockSpec((1, D_MODEL), lambda i: (0, 0)),
                  pl.BlockSpec((D_MODEL, IN_COLS), lambda i: (0, 0))],
        out_specs=pl.BlockSpec((tm, IN_COLS), lambda i: (i, 0)), out_shape=_sds((T, IN_COLS)),
        compiler_params=_cp("parallel"))(h, g, w)


def _inproj_bwd(gres, h, g, w, dza, dzq, dzk, dzv, dzc):
    T = h.shape[0]
    tm = min(T, TOKEN_TILE)
    nt = T // tm
    shard = IN_COLS // N_DEV

    def body(gr_ref, h_ref, g_ref, w_ref, a_ref, q_ref, k_ref, v_ref, c_ref, dh_ref, dw_ref, dg_ref, acc):
        i = pl.program_id(0)

        @pl.when(i == 0)
        def _():
            dg_ref[...] = jnp.zeros_like(dg_ref)
            acc[...] = jnp.zeros_like(acc)

        x = h_ref[...]
        r = _rms_stat(x)
        gg = g_ref[...]
        dz = jnp.concatenate([a_ref[...], q_ref[...], k_ref[...], v_ref[...], c_ref[...]], axis=1)
        dxn = _mm_nt(dz, w_ref[...])
        dx, dg = _rms_bwd(x, r, gg, dxn)
        dh_ref[...] = gr_ref[...] + dx
        dg_ref[...] += dg
        acc[...] += _mm_tn(x * r * gg, dz)

        @pl.when(i == nt - 1)
        def _():
            for d in range(N_DEV):
                dw_ref[d] = acc[:, d * shard:(d + 1) * shard].astype(WIRE)

    def rows(w_):
        return pl.BlockSpec((tm, w_), lambda i: (i, 0))

    row = pl.BlockSpec((1, D_MODEL), lambda i: (0, 0))
    return pl.pallas_call(
        body, name="inproj_bwd", grid=(nt,),
        in_specs=[rows(D_MODEL), rows(D_MODEL), row, pl.BlockSpec((D_MODEL, IN_COLS), lambda i: (0, 0)),
                  rows(IN_A), rows(B_WIDTH), rows(KV_WIDTH), rows(KV_WIDTH), rows(C_WIDTH)],
        out_specs=[rows(D_MODEL), pl.BlockSpec((N_DEV, D_MODEL, shard), lambda i: (0, 0, 0)), row],
        out_shape=[_sds((T, D_MODEL)), _sds((N_DEV, D_MODEL, shard), WIRE), _sds((1, D_MODEL))],
        scratch_shapes=[pltpu.VMEM((D_MODEL, IN_COLS), F32)],
        compiler_params=_cp("arbitrary"))(gres, h, g, w, dza, dzq, dzk, dzv, dzc)


def _gmlp_fwd(z, lng, lnb, ws, bsx):
    T = z.shape[0]
    tm = min(T, TOKEN_TILE)
    nc = tm // CHUNK

    def body(z_ref, g_ref, b_ref, w_ref, bs_ref, ya_ref):
        zg = _gelu(z_ref[...])
        lo = _lane((tm, LANES)) < HEAD_DIM
        prods = []
        for hd in range(A_HEADS):
            sl = slice(hd * LANES, (hd + 1) * LANES)
            blk = zg[:, sl]
            vn, _, _ = _gmlp_head(blk, g_ref[:, sl], b_ref[:, sl])
            wm = _tril(w_ref[hd])
            sv = jnp.concatenate([_mm(wm, vn[c * CHUNK:(c + 1) * CHUNK]) + bs_ref[hd] for c in range(nc)], axis=0)
            prods.append(blk * _swap64(sv))
        ya_ref[:, 0:LANES] = jnp.where(lo, prods[0], _swap64(prods[1]))
        ya_ref[:, LANES:2 * LANES] = jnp.where(lo, prods[2], _swap64(prods[3]))

    row = pl.BlockSpec((1, IN_A), lambda i: (0, 0))
    mat = pl.BlockSpec((A_HEADS, CHUNK, CHUNK), lambda i: (0, 0, 0))
    return pl.pallas_call(
        body, name="gmlp_fwd", grid=(T // tm,),
        in_specs=[pl.BlockSpec((tm, IN_A), lambda i: (i, 0)), row, row, mat, mat],
        out_specs=pl.BlockSpec((tm, A_WIDTH), lambda i: (i, 0)), out_shape=_sds((T, A_WIDTH)),
        compiler_params=_cp("parallel"))(z, lng, lnb, ws, bsx)


def _gmlp_bwd(z, dya, lng, lnb, ws, bsx):
    T = z.shape[0]
    tm = min(T, TOKEN_TILE)
    nc = tm // CHUNK

    def body(z_ref, dya_ref, g_ref, b_ref, w_ref, bs_ref, dza_ref, dw_ref, dbs_ref, dg_ref, db_ref):
        @pl.when(pl.program_id(0) == 0)
        def _():
            dw_ref[...] = jnp.zeros_like(dw_ref)
            dbs_ref[...] = jnp.zeros_like(dbs_ref)
            dg_ref[...] = jnp.zeros_like(dg_ref)
            db_ref[...] = jnp.zeros_like(db_ref)

        za = z_ref[...]
        zg = _gelu(za)
        gp = _gelu_grad(za)
        lo = _lane((tm, LANES)) < HEAD_DIM
        for hd in range(A_HEADS):
            sl = slice(hd * LANES, (hd + 1) * LANES)
            blk = zg[:, sl]
            g = g_ref[:, sl]
            vn, vhat, rstd = _gmlp_head(blk, g, b_ref[:, sl])
            wm = _tril(w_ref[hd])
            pair = dya_ref[:, (hd // 2) * LANES:(hd // 2 + 1) * LANES]
            dy = jnp.where(lo, pair if hd % 2 == 0 else _swap64(pair), 0.0)
            dsv = _swap64(dy * blk)
            svs, dvns = [], []
            dw = jnp.zeros((CHUNK, CHUNK), F32)
            dbs = jnp.zeros((CHUNK, 1), F32)
            for c in range(nc):
                cs = slice(c * CHUNK, (c + 1) * CHUNK)
                svs.append(_mm(wm, vn[cs]) + bs_ref[hd])
                dw = dw + _mm_nt(dsv[cs], vn[cs])
                dbs = dbs + jnp.sum(dsv[cs], axis=-1, keepdims=True)
                dvns.append(_mm_tn(wm, dsv[cs]))
            sv = jnp.concatenate(svs, axis=0)
            dvn = jnp.concatenate(dvns, axis=0)
            dw_ref[hd] += _tril(dw)
            dbs_ref[hd] += jnp.broadcast_to(dbs, (CHUNK, CHUNK))
            dg_ref[:, sl] += jnp.sum(dvn * vhat, axis=0, keepdims=True)
            db_ref[:, sl] += jnp.sum(dvn, axis=0, keepdims=True)
            du = dy * _swap64(sv)
            dvh = dvn * g
            m1 = jnp.sum(dvh, axis=-1, keepdims=True) * (1.0 / HEAD_DIM)
            m2 = jnp.sum(dvh * vhat, axis=-1, keepdims=True) * (1.0 / HEAD_DIM)
            dv = jnp.where(lo, 0.0, rstd * (dvh - m1 - vhat * m2))
            dza_ref[:, sl] = (du + dv) * gp[:, sl]

    row = pl.BlockSpec((1, IN_A), lambda i: (0, 0))
    mat = pl.BlockSpec((A_HEADS, CHUNK, CHUNK), lambda i: (0, 0, 0))
    return pl.pallas_call(
        body, name="gmlp_bwd", grid=(T // tm,),
        in_specs=[pl.BlockSpec((tm, IN_A), lambda i: (i, 0)), pl.BlockSpec((tm, A_WIDTH), lambda i: (i, 0)), row, row, mat, mat],
        out_specs=[pl.BlockSpec((tm, IN_A), lambda i: (i, 0)), mat, mat, row, row],
        out_shape=[_sds((T, IN_A)), _sds((A_HEADS, CHUNK, CHUNK)), _sds((A_HEADS, CHUNK, CHUNK)), _sds((1, IN_A)), _sds((1, IN_A))],
        compiler_params=_cp("arbitrary"))(z, dya, lng, lnb, ws, bsx)


def _attn_specs(T, tq, tile_of):
    nb = tq // WINDOW

    def prev(i):
        return jnp.maximum(tile_of(i) * nb - 1, 0)

    row = pl.BlockSpec((1, LANES), lambda i: (0, 0))
    return [
        pl.BlockSpec((tq, B_WIDTH), lambda i: (tile_of(i), Q_BLOCK)),
        pl.BlockSpec((tq, LANES), lambda i: (tile_of(i), K_BLOCK)),
        pl.BlockSpec((tq, LANES), lambda i: (tile_of(i), V_BLOCK)),
        pl.BlockSpec((WINDOW, LANES), lambda i: (prev(i), K_BLOCK)),
        pl.BlockSpec((WINDOW, LANES), lambda i: (prev(i), V_BLOCK)),
        pl.BlockSpec((tq, LANES), lambda i: (tile_of(i), 0)),
        pl.BlockSpec((tq, LANES), lambda i: (tile_of(i), 0)),
        pl.BlockSpec((WINDOW, LANES), lambda i: (prev(i), 0)),
        pl.BlockSpec((WINDOW, LANES), lambda i: (prev(i), 0)),
        row, row,
        pl.BlockSpec((8, LANES), lambda i: (0, 0)),
    ]


def _attn_bias(first):
    qi = lax.broadcasted_iota(jnp.int32, (WINDOW, 2 * WINDOW), 0)
    kj = lax.broadcasted_iota(jnp.int32, (WINDOW, 2 * WINDOW), 1)
    diff = qi + WINDOW - kj
    ok = (diff >= 0) & (diff < WINDOW) & ((kj >= WINDOW) | jnp.logical_not(first))
    return jnp.where(ok, 0.0, NEG)


def _dup_heads(x, lo):
    sw = _swap64(x)
    return jnp.where(lo, x, sw), jnp.where(lo, sw, x)


HEADS_PER_KV = 4


def _stack_heads(x0, x1, lo):
    return jnp.concatenate([jnp.where(lo, x0, 0.0), jnp.where(lo, 0.0, x0), jnp.where(lo, x1, 0.0), jnp.where(lo, 0.0, x1)], axis=0)


def _unstack_heads(x4, lo):
    return (jnp.where(lo, x4[0:WINDOW], x4[WINDOW:2 * WINDOW]), jnp.where(lo, x4[2 * WINDOW:3 * WINDOW], x4[3 * WINDOW:4 * WINDOW]))


def _sink_column(sk_ref, g):
    return jnp.concatenate([jnp.broadcast_to(sk_ref[a:a + 1, 0:1], (WINDOW, 1)) for a in range(HEADS_PER_KV * g, HEADS_PER_KV * (g + 1))], axis=0)


def _attn_probs(q4, kw, bias, sink):
    s = _mm_nt(q4, kw)
    s = (s.reshape(HEADS_PER_KV, WINDOW, 2 * WINDOW) + bias[None]).reshape(HEADS_PER_KV * WINDOW, 2 * WINDOW)
    m = jnp.maximum(jnp.max(s, axis=-1, keepdims=True), sink)
    p = jnp.exp(s - m)
    es = jnp.exp(sink - m)
    inv = 1.0 / (jnp.sum(p, axis=-1, keepdims=True) + es)
    return p * inv, es * inv


def _attn_fwd(z, cs, sn, qg, kg, sinks):
    T = z.shape[0]
    tq = min(T, 2 * TOKEN_TILE)
    nb = tq // WINDOW

    def body(q_ref, k_ref, v_ref, kp_ref, vp_ref, cs_ref, sn_ref, csp_ref, snp_ref, qg_ref, kg_ref, sk_ref, o_ref):
        i = pl.program_id(0)
        csq, snq = cs_ref[...], sn_ref[...]
        cs_all = jnp.concatenate([csp_ref[...], csq], axis=0)
        sn_all = jnp.concatenate([snp_ref[...], snq], axis=0)
        k_all = jnp.concatenate([kp_ref[...], k_ref[...]], axis=0)
        v_all = jnp.concatenate([vp_ref[...], v_ref[...]], axis=0)
        kr, _, _ = _qk_norm_rope(k_all, kg_ref[...], cs_all, sn_all)
        lo_all = _lane(k_all.shape) < HEAD_DIM
        kd = _dup_heads(kr, lo_all)
        vd = _dup_heads(v_all, lo_all)
        lo = _lane((WINDOW, LANES)) < HEAD_DIM
        qrs = [_qk_norm_rope(q_ref[:, pr * LANES:(pr + 1) * LANES], qg_ref[...], csq, snq)[0] * SCALE for pr in range(4)]
        biases = [_attn_bias(i * nb + b == 0) for b in range(nb)]
        for g in range(2):
            sink = _sink_column(sk_ref, g)
            for b in range(nb):
                bs = slice(b * WINDOW, (b + 1) * WINDOW)
                ws = slice(b * WINDOW, (b + 2) * WINDOW)
                pn, _ = _attn_probs(_stack_heads(qrs[2 * g][bs], qrs[2 * g + 1][bs], lo), kd[g][ws], biases[b], sink)
                o0, o1 = _unstack_heads(_mm(pn, vd[g][ws]), lo)
                o_ref[bs, 2 * g * LANES:(2 * g + 1) * LANES] = o0
                o_ref[bs, (2 * g + 1) * LANES:(2 * g + 2) * LANES] = o1

    return pl.pallas_call(
        body, name="attn_fwd", grid=(T // tq,),
        in_specs=_attn_specs(T, tq, lambda i: i),
        out_specs=pl.BlockSpec((tq, B_WIDTH), lambda i: (i, 0)), out_shape=_sds((T, B_WIDTH)),
        compiler_params=_cp("parallel"))(z, z, z, z, z, cs, sn, cs, sn, qg, kg, sinks)


def _attn_bwd(z, cs, sn, qg, kg, sinks, o, do):
    T = z.shape[0]
    tq = min(T, 2 * TOKEN_TILE)
    nb = tq // WINDOW
    nt = T // tq
    tk = tq + WINDOW

    def tile_of(i):
        return nt - 1 - i

    def body(q_ref, k_ref, v_ref, kp_ref, vp_ref, cs_ref, sn_ref, csp_ref, snp_ref, qg_ref, kg_ref, sk_ref, o_ref, do_ref,
             dq_ref, dk_ref, dv_ref, dqg_ref, dkg_ref, dsk_ref, acck, accv, ck, cv):
        i = pl.program_id(0)
        ti = nt - 1 - i

        @pl.when(i == 0)
        def _():
            dqg_ref[...] = jnp.zeros_like(dqg_ref)
            dkg_ref[...] = jnp.zeros_like(dkg_ref)
            dsk_ref[...] = jnp.zeros_like(dsk_ref)
            ck[...] = jnp.zeros_like(ck)
            cv[...] = jnp.zeros_like(cv)

        csq, snq = cs_ref[...], sn_ref[...]
        cs_all = jnp.concatenate([csp_ref[...], csq], axis=0)
        sn_all = jnp.concatenate([snp_ref[...], snq], axis=0)
        k_all = jnp.concatenate([kp_ref[...], k_ref[...]], axis=0)
        v_all = jnp.concatenate([vp_ref[...], v_ref[...]], axis=0)
        kr, kh, rk = _qk_norm_rope(k_all, kg_ref[...], cs_all, sn_all)
        lo_all = _lane(k_all.shape) < HEAD_DIM
        kd = _dup_heads(kr, lo_all)
        vd = _dup_heads(v_all, lo_all)
        lo = _lane((WINDOW, LANES)) < HEAD_DIM
        acck[...] = jnp.zeros_like(acck)
        accv[...] = jnp.zeros_like(accv)
        prep = [_qk_norm_rope(q_ref[:, pr * LANES:(pr + 1) * LANES], qg_ref[...], csq, snq) for pr in range(4)]
        biases = [_attn_bias(ti * nb + b == 0) for b in range(nb)]
        dqs = [[None] * nb for _ in range(4)]
        for g in range(2):
            sink = _sink_column(sk_ref, g)
            dsink = jnp.zeros((HEADS_PER_KV * WINDOW, 1), F32)
            for b in range(nb):
                bs = slice(b * WINDOW, (b + 1) * WINDOW)
                ws = slice(b * WINDOW, (b + 2) * WINDOW)
                kw, vw = kd[g][ws], vd[g][ws]
                q4 = _stack_heads(prep[2 * g][0][bs] * SCALE, prep[2 * g + 1][0][bs] * SCALE, lo)
                pn, psink = _attn_probs(q4, kw, biases[b], sink)
                o0, o1 = o_ref[bs, 2 * g * LANES:(2 * g + 1) * LANES], o_ref[bs, (2 * g + 1) * LANES:(2 * g + 2) * LANES]
                do4 = _stack_heads(do_ref[bs, 2 * g * LANES:(2 * g + 1) * LANES], do_ref[bs, (2 * g + 1) * LANES:(2 * g + 2) * LANES], lo)
                delta = jnp.sum(do4 * jnp.concatenate([o0, o0, o1, o1], axis=0), axis=-1, keepdims=True)
                ds = pn * (_mm_nt(do4, vw) - delta)
                dsink = dsink - psink * delta
                dqs[2 * g][b], dqs[2 * g + 1][b] = _unstack_heads(_mm(ds, kw) * SCALE, lo)
                acck[g, ws, :] += _mm_tn(ds, q4)
                accv[g, ws, :] += _mm_tn(pn, do4)
            for hh in range(HEADS_PER_KV):
                a = HEADS_PER_KV * g + hh
                dsk_ref[a:a + 1, :] += jnp.zeros((1, LANES), F32) + jnp.sum(dsink[hh * WINDOW:(hh + 1) * WINDOW])
        for pr in range(4):
            _, qh, rq = prep[pr]
            dx, dg = _qk_norm_rope_bwd(qh, rq, qg_ref[...], csq, snq, jnp.concatenate(dqs[pr], axis=0))
            dq_ref[:, pr * LANES:(pr + 1) * LANES] = dx
            dqg_ref[...] += dg

        def fold(acc):
            f0 = acc[0] + _swap64(acc[0])
            f1 = acc[1] + _swap64(acc[1])
            return jnp.where(lo_all, f0, f1)

        dk_all = fold(acck)
        dv_all = fold(accv)
        pad = jnp.zeros((tq - WINDOW, LANES), F32)
        dk_own = dk_all[WINDOW:] + (jnp.concatenate([pad, ck[...]], axis=0) if nb > 1 else ck[...])
        dv_own = dv_all[WINDOW:] + (jnp.concatenate([pad, cv[...]], axis=0) if nb > 1 else cv[...])
        ck[...] = dk_all[:WINDOW]
        cv[...] = dv_all[:WINDOW]
        dxk, dgk = _qk_norm_rope_bwd(kh[WINDOW:], rk[WINDOW:], kg_ref[...], csq, snq, dk_own)
        dk_ref[...] = dxk
        dkg_ref[...] += dgk
        dv_ref[...] = dv_own

    row = pl.BlockSpec((1, LANES), lambda i: (0, 0))
    return pl.pallas_call(
        body, name="attn_bwd", grid=(nt,),
        in_specs=_attn_specs(T, tq, tile_of) + [pl.BlockSpec((tq, B_WIDTH), lambda i: (tile_of(i), 0))] * 2,
        out_specs=[pl.BlockSpec((tq, B_WIDTH), lambda i: (tile_of(i), 0)), pl.BlockSpec((tq, LANES), lambda i: (tile_of(i), 0)),
                   pl.BlockSpec((tq, LANES), lambda i: (tile_of(i), 0)), row, row, pl.BlockSpec((8, LANES), lambda i: (0, 0))],
        out_shape=[_sds((T, B_WIDTH)), _sds((T, LANES)), _sds((T, LANES)), _sds((1, LANES)), _sds((1, LANES)), _sds((8, LANES))],
        scratch_shapes=[pltpu.VMEM((2, tk, LANES), F32), pltpu.VMEM((2, tk, LANES), F32),
                        pltpu.VMEM((WINDOW, LANES), F32), pltpu.VMEM((WINDOW, LANES), F32)],
        compiler_params=_cp("arbitrary"))(z, z, z, z, z, cs, sn, cs, sn, qg, kg, sinks, o, do)


def _bbar_t(are, aim, ldt, btr, bti):
    lbr, lbi = _lam_bar(are, aim, ldt)
    den = are * are + aim * aim
    nr = lbr - 1.0
    cr = (nr * are + lbi * aim) / den
    ci = (lbi * are - nr * aim) / den
    return cr * btr - ci * bti, cr * bti + ci * btr


def _lam_bar(are, aim, ldt):
    dt = jnp.exp(ldt)
    er = jnp.exp(are * dt)
    return er * jnp.cos(aim * dt), er * jnp.sin(aim * dt)


def _block_diag(x):
    t = jnp.concatenate([x] * C_GROUPS, axis=1)
    return jnp.where(_row(t.shape) // C_GROUP == _lane(t.shape) // C_STATE, t, 0.0)


def _block_diag_fold(m):
    rg = _row((C_WIDTH, C_STATE)) // C_GROUP
    acc = jnp.zeros((C_WIDTH, C_STATE), F32)
    for g in range(C_GROUPS):
        acc = acc + jnp.where(rg == g, m[:, g * C_STATE:(g + 1) * C_STATE], 0.0)
    return acc


def _ssm_prep(are, aim, ldt, are_x, aim_x, ldt_x, btr, bti, cre, cim):
    def body(are_r, aim_r, ldt_r, arex_r, aimx_r, ldtx_r, btr_r, bti_r, cre_r, cim_r, bbd_ref, cbd_ref, pwr_ref, pwi_ref):
        lr, li = _lam_bar(are_r[...], aim_r[...], ldt_r[...])
        cr, ci = lr, li
        for r in range(SCAN_SEG):
            pwr_ref[r:r + 1, :] = cr
            pwi_ref[r:r + 1, :] = ci
            cr, ci = cr * lr - ci * li, cr * li + ci * lr
        br, bi = _bbar_t(arex_r[...], aimx_r[...], ldtx_r[...], btr_r[...], bti_r[...])
        bbd_ref[...] = jnp.concatenate([_block_diag(br), _block_diag(bi)], axis=1).astype(MXU)
        cbd_ref[...] = jnp.concatenate([_block_diag(cre_r[...]), -_block_diag(cim_r[...])], axis=1).astype(MXU)

    return pl.pallas_call(
        body, name="ssm_prep",
        out_shape=[_sds((C_WIDTH, 2 * N_STATE), MXU), _sds((C_WIDTH, 2 * N_STATE), MXU), _sds((SCAN_SEG, N_STATE)), _sds((SCAN_SEG, N_STATE))],
        compiler_params=pltpu.CompilerParams(vmem_limit_bytes=VMEM_LIMIT))(are, aim, ldt, are_x, aim_x, ldt_x, btr, bti, cre, cim)


def _ssm_param_bwd(are, aim, ldt, are_x, aim_x, ldt_x, btr, bti, dlr, dli, dbbd, dcr, dci):
    def body(are_r, aim_r, ldt_r, arex_r, aimx_r, ldtx_r, btr_r, bti_r, dlr_r, dli_r, dbbd_r, dcr_r, dci_r,
             dare_ref, daim_ref, dldt_ref, dbtr_ref, dbti_ref, dcre_ref, dcim_ref):
        _, vjp_l = jax.vjp(_lam_bar, are_r[...], aim_r[...], ldt_r[...])
        da1, di1, dl1 = vjp_l((dlr_r[...], dli_r[...]))
        dbr = _block_diag_fold(dbbd_r[:, 0:N_STATE])
        dbi = _block_diag_fold(dbbd_r[:, N_STATE:2 * N_STATE])
        _, vjp_b = jax.vjp(_bbar_t, arex_r[...], aimx_r[...], ldtx_r[...], btr_r[...], bti_r[...])
        da2, di2, dl2, dbtr, dbti = vjp_b((dbr, dbi))

        def gsum(x):
            return x.reshape(C_GROUPS, C_GROUP, C_STATE).sum(axis=1)

        dare_ref[...] = da1 + gsum(da2)
        daim_ref[...] = di1 + gsum(di2)
        dldt_ref[...] = jnp.broadcast_to(jnp.sum(dl1 + gsum(dl2), axis=-1, keepdims=True), (C_GROUPS, LANES))
        dbtr_ref[...] = dbtr
        dbti_ref[...] = dbti
        dcre_ref[...] = _block_diag_fold(dcr_r[...])
        dcim_ref[...] = -_block_diag_fold(dci_r[...])

    g = _sds((C_GROUPS, C_STATE))
    x = _sds((C_WIDTH, C_STATE))
    return pl.pallas_call(
        body, name="ssm_param_bwd", out_shape=[g, g, _sds((C_GROUPS, LANES)), x, x, x, x],
        compiler_params=pltpu.CompilerParams(vmem_limit_bytes=VMEM_LIMIT))(are, aim, ldt, are_x, aim_x, ldt_x, btr, bti, dlr, dli, dbbd, dcr, dci)


SCAN_TILE = 512
SCAN_SEG = 8


def _scan_tables(pwr_ref, pwi_ref, conj, reverse):
    row = _row((SCAN_SEG, N_STATE))
    shifts = []
    for k in (1, 2, 4):
        keep = (row < SCAN_SEG - k) if reverse else (row >= k)
        ar = jnp.broadcast_to(pwr_ref[k - 1:k, :], (SCAN_SEG, N_STATE))
        ai = jnp.broadcast_to(pwi_ref[k - 1:k, :], (SCAN_SEG, N_STATE)) * conj
        shifts.append((SCAN_SEG - k if reverse else k, jnp.where(keep, ar, 0.0), jnp.where(keep, ai, 0.0)))
    if reverse:
        pr = jnp.concatenate([pwr_ref[SCAN_SEG - 1 - r:SCAN_SEG - r, :] for r in range(SCAN_SEG)], axis=0)
        pi = jnp.concatenate([pwi_ref[SCAN_SEG - 1 - r:SCAN_SEG - r, :] for r in range(SCAN_SEG)], axis=0) * conj
    else:
        pr, pi = pwr_ref[...], pwi_ref[...] * conj
    return shifts, (pr, pi)


def _tile_scan(xr_ref, xi_ref, pwr_ref, pwi_ref, sr, si, conj, reverse):
    shifts, (pr, pi) = _scan_tables(pwr_ref, pwi_ref, conj, reverse)
    groups = xr_ref.shape[0] // SCAN_SEG
    out_row = 0 if reverse else SCAN_SEG - 1

    def step(k, c):
        cr, ci = c
        g = groups - 1 - k if reverse else k
        rows = pl.ds(pl.multiple_of(g * SCAN_SEG, SCAN_SEG), SCAN_SEG)
        xr, xi = xr_ref[rows, :], xi_ref[rows, :]
        for amount, ar, ai in shifts:
            qr, qi = pltpu.roll(xr, amount, 0), pltpu.roll(xi, amount, 0)
            xr, xi = xr + ar * qr - ai * qi, xi + ar * qi + ai * qr
        xr, xi = xr + pr * cr - pi * ci, xi + pr * ci + pi * cr
        xr_ref[rows, :] = xr
        xi_ref[rows, :] = xi
        return xr[out_row:out_row + 1], xi[out_row:out_row + 1]

    cr, ci = lax.fori_loop(0, groups, step, (sr[...], si[...]), unroll=2)
    sr[...] = cr
    si[...] = ci


def _ssm_fwd(z, bbd, pwr, pwi, cbd, dsk, w1, w2):
    T = z.shape[0]
    tt = min(T, 2 * SCAN_TILE)

    def body(u_ref, bbd_ref, pwr_ref, pwi_ref, cbd_ref, d_ref, w1_ref, w2_ref, yc_ref, y_ref, xr_ref, xi_ref, sr, si):
        @pl.when(pl.program_id(0) == 0)
        def _():
            sr[...] = jnp.zeros_like(sr)
            si[...] = jnp.zeros_like(si)

        u = u_ref[...]
        bu = _mm(u, bbd_ref[...])
        xr_ref[...] = bu[:, 0:N_STATE]
        xi_ref[...] = bu[:, N_STATE:2 * N_STATE]
        _tile_scan(xr_ref, xi_ref, pwr_ref, pwi_ref, sr, si, 1.0, False)
        x = jnp.concatenate([xr_ref[...], xi_ref[...]], axis=1)
        y = _mm_nt(x, cbd_ref[...]) + d_ref[...] * u
        y_ref[...] = y
        y2 = _gelu(y)
        yc_ref[...] = _mm(y2, w1_ref[...]) * _sigmoid(_mm(y2, w2_ref[...]))

    big = pl.BlockSpec((C_WIDTH, 2 * N_STATE), lambda i: (0, 0))
    tab = pl.BlockSpec((SCAN_SEG, N_STATE), lambda i: (0, 0))
    wsp = pl.BlockSpec((C_WIDTH, C_WIDTH), lambda i: (0, 0))
    xs = pl.BlockSpec((tt, N_STATE), lambda i: (i, 0))
    return pl.pallas_call(
        body, name="ssm_fwd", grid=(T // tt,),
        in_specs=[pl.BlockSpec((tt, C_WIDTH), lambda i: (i, C_BLOCK)), big, tab, tab, big, pl.BlockSpec((1, C_WIDTH), lambda i: (0, 0)), wsp, wsp],
        out_specs=[pl.BlockSpec((tt, C_WIDTH), lambda i: (i, 0)), pl.BlockSpec((tt, C_WIDTH), lambda i: (i, 0)), xs, xs],
        out_shape=[_sds((T, C_WIDTH)), _sds((T, C_WIDTH)), _sds((T, N_STATE)), _sds((T, N_STATE))],
        scratch_shapes=[pltpu.VMEM((1, N_STATE), F32)] * 2,
        compiler_params=_cp("arbitrary"))(z, bbd, pwr, pwi, cbd, dsk, w1, w2)


def _ssm_bwd(dyc, z, y, xr, xi, bbd, pwr, pwi, cbd, dsk, w1, w2):
    T = z.shape[0]
    tt = min(T, SCAN_TILE)
    nt = T // tt

    def tile_of(i):
        return nt - 1 - i

    def body(dyc_ref, u_ref, y_ref, xr_ref, xi_ref, xpr_ref, xpi_ref, bbd_ref, pwr_ref, pwi_ref, cbd_ref, d_ref, w1_ref, w2_ref,
             du_ref, dw1_ref, dw2_ref, dcr_ref, dci_ref, dbbd_ref, dlr_ref, dli_ref, dd_ref, gr, gi, sr, si):
        i = pl.program_id(0)
        ti = nt - 1 - i

        @pl.when(i == 0)
        def _():
            sr[...] = jnp.zeros_like(sr)
            si[...] = jnp.zeros_like(si)
            for acc in (dw1_ref, dw2_ref, dcr_ref, dci_ref, dbbd_ref, dlr_ref, dli_ref, dd_ref):
                acc[...] = jnp.zeros_like(acc)

        u = u_ref[...]
        xr_t, xi_t = xr_ref[...], xi_ref[...]
        y = y_ref[...]
        y2 = _gelu(y)
        a1 = _mm(y2, w1_ref[...])
        sg = _sigmoid(_mm(y2, w2_ref[...]))
        dyc_t = dyc_ref[...]
        da1 = dyc_t * sg
        da2 = dyc_t * a1 * sg * (1.0 - sg)
        dy = (_mm_nt(da1, w1_ref[...]) + _mm_nt(da2, w2_ref[...])) * _gelu_grad(y)
        gx = _mm(dy, cbd_ref[...])
        gr[...] = gx[:, 0:N_STATE]
        gi[...] = gx[:, N_STATE:2 * N_STATE]
        _tile_scan(gr, gi, pwr_ref, pwi_ref, sr, si, -1.0, True)
        ar, ai = gr[...], gi[...]
        first_row = _row(ar.shape) == 0
        live = jnp.where(ti > 0, 1.0, 0.0)
        xsr = jnp.where(first_row, xpr_ref[7:8, :] * live, pltpu.roll(xr_t, 1, 0))
        xsi = jnp.where(first_row, xpi_ref[7:8, :] * live, pltpu.roll(xi_t, 1, 0))
        dlr_ref[...] += jnp.sum(ar * xsr + ai * xsi, axis=0, keepdims=True)
        dli_ref[...] += jnp.sum(ai * xsr - ar * xsi, axis=0, keepdims=True)
        dd_ref[...] += jnp.sum(dy * u, axis=0, keepdims=True)
        arai = jnp.concatenate([ar, ai], axis=1)
        du_ref[...] = _mm_nt(arai, bbd_ref[...]) + d_ref[...] * dy
        dw1_ref[...] += _mm_tn(y2, da1)
        dw2_ref[...] += _mm_tn(y2, da2)
        dcr_ref[...] += _mm_tn(dy, xr_t)
        dci_ref[...] += _mm_tn(dy, xi_t)
        dbbd_ref[...] += _mm_tn(u, arai)

    def prev(i):
        return jnp.maximum(tile_of(i) * (tt // 8) - 1, 0)

    big = pl.BlockSpec((C_WIDTH, 2 * N_STATE), lambda i: (0, 0))
    tab = pl.BlockSpec((SCAN_SEG, N_STATE), lambda i: (0, 0))
    srow = pl.BlockSpec((1, N_STATE), lambda i: (0, 0))
    wsp = pl.BlockSpec((C_WIDTH, C_WIDTH), lambda i: (0, 0))
    xs = pl.BlockSpec((tt, N_STATE), lambda i: (tile_of(i), 0))
    xp = pl.BlockSpec((8, N_STATE), lambda i: (prev(i), 0))
    cw = pl.BlockSpec((tt, C_WIDTH), lambda i: (tile_of(i), 0))
    drow = pl.BlockSpec((1, C_WIDTH), lambda i: (0, 0))
    return pl.pallas_call(
        body, name="ssm_bwd", grid=(nt,),
        in_specs=[cw, pl.BlockSpec((tt, C_WIDTH), lambda i: (tile_of(i), C_BLOCK)), cw, xs, xs, xp, xp, big, tab, tab, big, drow, wsp, wsp],
        out_specs=[cw, wsp, wsp, pl.BlockSpec((C_WIDTH, N_STATE), lambda i: (0, 0)), pl.BlockSpec((C_WIDTH, N_STATE), lambda i: (0, 0)), big,
                   srow, srow, drow],
        out_shape=[_sds((T, C_WIDTH)), _sds((C_WIDTH, C_WIDTH)), _sds((C_WIDTH, C_WIDTH)), _sds((C_WIDTH, N_STATE)), _sds((C_WIDTH, N_STATE)),
                   _sds((C_WIDTH, 2 * N_STATE)), _sds((1, N_STATE)), _sds((1, N_STATE)), _sds((1, C_WIDTH))],
        scratch_shapes=[pltpu.VMEM((tt, N_STATE), F32)] * 2 + [pltpu.VMEM((1, N_STATE), F32)] * 2,
        compiler_params=_cp("arbitrary"))(dyc, z, y, xr, xi, xr, xi, bbd, pwr, pwi, cbd, dsk, w1, w2)


_GROUPS = ((0, A_WIDTH), (A_WIDTH, A_WIDTH + B_WIDTH), (A_WIDTH + B_WIDTH, D_MODEL))


def _merge_fwd(h, ya, yb, yc, g, w):
    T = h.shape[0]
    tm = min(T, TOKEN_TILE)

    def body(h_ref, a_ref, b_ref, c_ref, g_ref, w_ref, o_ref):
        yn = jnp.concatenate([y * _rms_stat(y) for y in (a_ref[...], b_ref[...], c_ref[...])], axis=1) * g_ref[...]
        o_ref[...] = h_ref[...] + _mm(yn, w_ref[...])

    def rows(w_):
        return pl.BlockSpec((tm, w_), lambda i: (i, 0))

    return pl.pallas_call(
        body, name="merge_fwd", grid=(T // tm,),
        in_specs=[rows(D_MODEL), rows(A_WIDTH), rows(B_WIDTH), rows(C_WIDTH), pl.BlockSpec((1, D_MODEL), lambda i: (0, 0)),
                  pl.BlockSpec((D_MODEL, D_MODEL), lambda i: (0, 0))],
        out_specs=rows(D_MODEL), out_shape=_sds((T, D_MODEL)), compiler_params=_cp("parallel"))(h, ya, yb, yc, g, w)


def _merge_bwd(gres, ya, yb, yc, g, w):
    T = gres.shape[0]
    tm = min(T, 2 * TOKEN_TILE)
    nt = T // tm

    def body(gr_ref, a_ref, b_ref, c_ref, g_ref, w_ref, da_ref, db_ref, dc_ref, dw_ref, dg_ref, acc):
        i = pl.program_id(0)

        @pl.when(i == 0)
        def _():
            dg_ref[...] = jnp.zeros_like(dg_ref)
            acc[...] = jnp.zeros_like(acc)

        gr = gr_ref[...]
        dyn = _mm_nt(gr, w_ref[...])
        yns, dgs = [], []
        for (c0, c1), y_ref, d_ref in zip(_GROUPS, (a_ref, b_ref, c_ref), (da_ref, db_ref, dc_ref)):
            y = y_ref[...]
            r = _rms_stat(y)
            gg = g_ref[:, c0:c1]
            dx, dg = _rms_bwd(y, r, gg, dyn[:, c0:c1])
            d_ref[...] = dx
            dgs.append(dg)
            yns.append(y * r * gg)
        dg_ref[...] += jnp.concatenate(dgs, axis=1)
        acc[...] += _mm_tn(jnp.concatenate(yns, axis=1), gr)

        @pl.when(i == nt - 1)
        def _():
            dw_ref[...] = acc[...].astype(WIRE)

    def rows(w_):
        return pl.BlockSpec((tm, w_), lambda i: (i, 0))

    row = pl.BlockSpec((1, D_MODEL), lambda i: (0, 0))
    full = pl.BlockSpec((D_MODEL, D_MODEL), lambda i: (0, 0))
    return pl.pallas_call(
        body, name="merge_bwd", grid=(nt,),
        in_specs=[rows(D_MODEL), rows(A_WIDTH), rows(B_WIDTH), rows(C_WIDTH), row, full],
        out_specs=[rows(A_WIDTH), rows(B_WIDTH), rows(C_WIDTH), full, row],
        out_shape=[_sds((T, A_WIDTH)), _sds((T, B_WIDTH)), _sds((T, C_WIDTH)), _sds((D_MODEL, D_MODEL), WIRE), _sds((1, D_MODEL))],
        scratch_shapes=[pltpu.VMEM((D_MODEL, D_MODEL), F32)],
        compiler_params=_cp("arbitrary"))(gres, ya, yb, yc, g, w)


FF_BLOCK = D_FF // N_DEV


def _load_weights(w1_hbm, w2_hbm, w1, w2, sem):
    @pl.when(pl.program_id(0) == 0)
    def _():
        copies = [pltpu.make_async_copy(w1_hbm.at[j], w1.at[:, pl.ds(j * FF_BLOCK, FF_BLOCK)], sem.at[j]) for j in range(N_DEV)]
        copies.append(pltpu.make_async_copy(w2_hbm, w2, sem.at[N_DEV]))
        for cp in copies:
            cp.start()
        for cp in copies:
            cp.wait()


def _mlp_weight_scratch():
    return [pltpu.VMEM((D_MODEL, D_FF), MXU), pltpu.VMEM((D_FF, D_MODEL), MXU), pltpu.SemaphoreType.DMA((N_DEV + 1,))]


def _mlp_fwd(h, g, w1, w2):
    T = h.shape[0]
    tm = min(T, TOKEN_TILE)

    def body(h_ref, g_ref, w1_hbm, w2_hbm, o_ref, r_ref, w1_v, w2_v, sem):
        _load_weights(w1_hbm, w2_hbm, w1_v, w2_v, sem)
        x = h_ref[...]
        a = jnp.maximum(_mm(x * _rms_stat(x) * g_ref[...], w1_v[...]), 0.0)
        r = (a * a).astype(MXU)
        r_ref[...] = r
        o_ref[...] = x + _mm(r, w2_v[...])

    rows = pl.BlockSpec((tm, D_MODEL), lambda i: (i, 0))
    hbm = pl.BlockSpec(memory_space=pl.ANY)
    return pl.pallas_call(
        body, name="mlp_fwd", grid=(T // tm,),
        in_specs=[rows, pl.BlockSpec((1, D_MODEL), lambda i: (0, 0)), hbm, hbm],
        out_specs=[rows, pl.BlockSpec((tm, D_FF), lambda i: (i, 0))],
        out_shape=[_sds((T, D_MODEL)), _sds((T, D_FF), MXU)],
        scratch_shapes=_mlp_weight_scratch(), compiler_params=_cp("arbitrary"))(h, g, w1, w2)


def _mlp_bwd(gres, h, g, r, w1, w2):
    T = h.shape[0]
    tm = min(T, 256)

    def body(gr_ref, h_ref, g_ref, r_ref, w1_hbm, w2_hbm, dh_ref, hn_ref, da_ref, dg_ref, w1_v, w2_v, sem):
        _load_weights(w1_hbm, w2_hbm, w1_v, w2_v, sem)

        @pl.when(pl.program_id(0) == 0)
        def _():
            dg_ref[...] = jnp.zeros_like(dg_ref)

        gr = gr_ref[...]
        da = (_mm_nt(gr, w2_v[...]) * (2.0 * jnp.sqrt(r_ref[...].astype(F32)))).astype(MXU)
        da_ref[...] = da
        x = h_ref[...]
        rs = _rms_stat(x)
        gg = g_ref[...]
        dx, dg = _rms_bwd(x, rs, gg, _mm_nt(da, w1_v[...]))
        dh_ref[...] = gr + dx
        dg_ref[...] += dg
        hn_ref[...] = (x * rs * gg).astype(MXU)

    rows = pl.BlockSpec((tm, D_MODEL), lambda i: (i, 0))
    row = pl.BlockSpec((1, D_MODEL), lambda i: (0, 0))
    wide = pl.BlockSpec((tm, D_FF), lambda i: (i, 0))
    hbm = pl.BlockSpec(memory_space=pl.ANY)
    return pl.pallas_call(
        body, name="mlp_bwd", grid=(T // tm,),
        in_specs=[rows, rows, row, wide, hbm, hbm], out_specs=[rows, rows, wide, row],
        out_shape=[_sds((T, D_MODEL)), _sds((T, D_MODEL), MXU), _sds((T, D_FF), MXU), _sds((1, D_MODEL))],
        scratch_shapes=_mlp_weight_scratch(), compiler_params=_cp("arbitrary"))(gres, h, g, r, w1, w2)


def _ple_fwd(h, p, l, g, wg, wp):
    T = h.shape[0]
    tm = min(T, TOKEN_TILE)

    def body(h_ref, p_ref, g_ref, wg_ref, wp_ref, o_ref, gp_ref, e_ref):
        x = h_ref[...]
        gp = _mm(x * _rms_stat(x) * g_ref[...], wg_ref[...])
        e = _mm(p_ref[...], wp_ref[...])
        o_ref[...] = x + _sigmoid(gp) * e
        gp_ref[...] = gp.astype(MXU)
        e_ref[...] = e.astype(MXU)

    rows = pl.BlockSpec((tm, D_MODEL), lambda i: (i, 0))
    return pl.pallas_call(
        body, name="ple_fwd", grid=(T // tm,),
        in_specs=[rows, pl.BlockSpec((None, tm, PLE_DIM), lambda i: (l, i, 0)), pl.BlockSpec((1, D_MODEL), lambda i: (0, 0)),
                  pl.BlockSpec((D_MODEL, D_MODEL), lambda i: (0, 0)), pl.BlockSpec((PLE_DIM, D_MODEL), lambda i: (0, 0))],
        out_specs=[rows, rows, rows], out_shape=[_sds((T, D_MODEL)), _sds((T, D_MODEL), MXU), _sds((T, D_MODEL), MXU)],
        compiler_params=_cp("parallel"))(h, p, g, wg, wp)


def _ple_bwd(gres, h, p, l, g, gp, e, wg):
    T = h.shape[0]
    tm = min(T, TOKEN_TILE)
    nt = T // tm
    shard = D_MODEL // N_DEV

    def body(gr_ref, h_ref, p_ref, g_ref, gp_ref, e_ref, wg_ref, dh_ref, dwg_ref, dwp_ref, dg_ref, accg, accp):
        i = pl.program_id(0)

        @pl.when(i == 0)
        def _():
            dg_ref[...] = jnp.zeros_like(dg_ref)
            accg[...] = jnp.zeros_like(accg)
            accp[...] = jnp.zeros_like(accp)

        x = h_ref[...]
        r = _rms_stat(x)
        gg = g_ref[...]
        hn = x * r * gg
        gate = _sigmoid(gp_ref[...].astype(F32))
        pe = p_ref[...]
        e = e_ref[...].astype(F32)
        gr = gr_ref[...]
        dgp = gr * e * gate * (1.0 - gate)
        dx, dg = _rms_bwd(x, r, gg, _mm_nt(dgp, wg_ref[...]))
        dh_ref[...] = gr + dx
        dg_ref[...] += dg
        accg[...] += _mm_tn(hn, dgp)
        accp[...] += _mm_tn(pe, gr * gate)

        @pl.when(i == nt - 1)
        def _():
            dwg_ref[...] = accg[...].astype(WIRE)
            for d in range(N_DEV):
                dwp_ref[d] = accp[:, d * shard:(d + 1) * shard].astype(WIRE)

    rows = pl.BlockSpec((tm, D_MODEL), lambda i: (i, 0))
    row = pl.BlockSpec((1, D_MODEL), lambda i: (0, 0))
    full = pl.BlockSpec((D_MODEL, D_MODEL), lambda i: (0, 0))
    return pl.pallas_call(
        body, name="ple_bwd", grid=(nt,),
        in_specs=[rows, rows, pl.BlockSpec((None, tm, PLE_DIM), lambda i: (l, i, 0)), row, rows, rows, full],
        out_specs=[rows, full, pl.BlockSpec((N_DEV, PLE_DIM, shard), lambda i: (0, 0, 0)), row],
        out_shape=[_sds((T, D_MODEL)), _sds((D_MODEL, D_MODEL), WIRE), _sds((N_DEV, PLE_DIM, shard), WIRE), _sds((1, D_MODEL))],
        scratch_shapes=[pltpu.VMEM((D_MODEL, D_MODEL), F32), pltpu.VMEM((PLE_DIM, D_MODEL), F32)],
        compiler_params=_cp("arbitrary"))(gres, h, p, g, gp, e, wg)


def _loss_head(h, target):
    T = h.shape[0]
    tm = min(T, 1024)

    def body(h_ref, t_ref, dh_ref, l_ref):
        @pl.when(pl.program_id(0) == 0)
        def _():
            l_ref[...] = jnp.zeros_like(l_ref)

        e = h_ref[...] - t_ref[...]
        dh_ref[...] = e * (1.0 / D_MODEL)
        l_ref[...] += jnp.zeros_like(l_ref) + 0.5 * jnp.sum(jnp.mean(e * e, axis=-1, keepdims=True))

    rows = pl.BlockSpec((tm, D_MODEL), lambda i: (i, 0))
    return pl.pallas_call(
        body, name="loss_head", grid=(T // tm,), in_specs=[rows, rows],
        out_specs=[rows, pl.BlockSpec((8, LANES), lambda i: (0, 0))], out_shape=[_sds((T, D_MODEL)), _sds((8, LANES))],
        compiler_params=_cp("arbitrary"))(h, target)


TN_ROWS = 2048

def _tn(a, b, m, n, *, bm, bn, n_major=False, dtype=F32, name="tn"):
    T = a.shape[0]
    tk = min(T, TN_ROWS)
    nk = T // tk
    assert m % bm == 0 and n % bn == 0 and (not n_major or bm == m)

    def body(a_ref, b_ref, o_ref, acc):
        k = pl.program_id(2)

        @pl.when(k == 0)
        def _():
            acc[...] = jnp.zeros_like(acc)

        acc[...] += _mm_tn(a_ref[...], b_ref[...])

        @pl.when(k == nk - 1)
        def _():
            o_ref[...] = acc[...].astype(dtype)

    if n_major:
        out_spec = pl.BlockSpec((None, bm, bn), lambda i, j, k: (j, 0, 0))
        out_shape = _sds((n // bn, m, bn), dtype)
    else:
        out_spec = pl.BlockSpec((bm, bn), lambda i, j, k: (i, j))
        out_shape = _sds((m, n), dtype)
    return pl.pallas_call(
        body, name=name, grid=(m // bm, n // bn, nk),
        in_specs=[pl.BlockSpec((tk, bm), lambda i, j, k: (k, i)), pl.BlockSpec((tk, bn), lambda i, j, k: (k, j))],
        out_specs=out_spec, out_shape=out_shape, scratch_shapes=[pltpu.VMEM((bm, bn), F32)],
        compiler_params=_cp("parallel", "parallel", "arbitrary"))(a, b)


def _row_tile(R, C):
    for cand in (512, 256, 128, 64, 32, 16, 8):
        if R % cand == 0 and cand * C * 4 <= 2 ** 20:
            return cand
    return R


def _sum_slots(land):
    S, R, C = land.shape
    tr = _row_tile(R, C)

    def body(l_ref, o_ref):
        acc = l_ref[0].astype(F32)
        for s in range(1, S):
            acc = acc + l_ref[s].astype(F32)
        o_ref[...] = acc

    return pl.pallas_call(
        body, name="sum_slots", grid=(R // tr,), in_specs=[pl.BlockSpec((S, tr, C), lambda i: (0, i, 0))],
        out_specs=pl.BlockSpec((tr, C), lambda i: (i, 0)), out_shape=_sds((R, C)), compiler_params=_cp("parallel"))(land)


def _sum_adamw(land, w, m, v):
    R, C = w.shape
    S = land.shape[0]
    tr = _row_tile(R, C)

    def body(l_ref, w_ref, m_ref, v_ref, g_ref, d_ref, nm_ref, nv_ref):
        gg = l_ref[0].astype(F32)
        for s in range(1, S):
            gg = gg + l_ref[s].astype(F32)
        g_ref[...] = gg
        nm = ADAM_B1 * m_ref[...] + (1.0 - ADAM_B1) * gg
        nv = ADAM_B2 * v_ref[...] + (1.0 - ADAM_B2) * (gg * gg)
        m_hat = nm / (1.0 - ADAM_B1 ** ADAM_STEP)
        v_hat = nv / (1.0 - ADAM_B2 ** ADAM_STEP)
        d_ref[...] = -ADAM_LR * (m_hat / (jnp.sqrt(v_hat) + ADAM_EPS) + ADAM_WD * w_ref[...])
        nm_ref[...] = nm
        nv_ref[...] = nv

    blk = pl.BlockSpec((tr, C), lambda i: (i, 0))
    return pl.pallas_call(
        body, name="sum_adamw", grid=(R // tr,), in_specs=[pl.BlockSpec((S, tr, C), lambda i: (0, i, 0))] + [blk] * 3,
        out_specs=[blk] * 4, out_shape=[_sds((R, C))] * 4, compiler_params=_cp("parallel"))(land, w, m, v)


_HBM = pl.BlockSpec(memory_space=pltpu.HBM)
_SEM = pl.BlockSpec(memory_space=pltpu.SEMAPHORE)
_EFFECT = pltpu.SideEffectType.DATAFLOW_SIDE_EFFECTING


def _peers():
    x, y, c = lax.axis_index("x"), lax.axis_index("y"), lax.axis_index("c")
    out = []
    for k in range(1, N_DEV):
        px, py, pc = x ^ (k >> 2), y ^ ((k >> 1) & 1), c ^ (k & 1)
        out.append((k, (px, py, pc), 4 * px + 2 * py + pc))
    return 4 * x + 2 * y + c, out


def _route(mode, layer):
    if mode == "all":
        return (lambda ref, peer: ref), (lambda ref, sender: ref.at[sender])
    return (lambda ref, peer: ref.at[peer]), (lambda ref, sender: ref.at[sender, layer])


def _split_start(srcs, lands, modes, layer, after, name):
    n = len(srcs)
    routes = [_route(m, layer) for m in modes]

    def body(*refs):
        src, land = refs[:n], refs[n:2 * n]
        send, recv, token = refs[2 * n + 1], refs[2 * n + 2], refs[-1]
        me, peers = _peers()
        for k, dev, peer in peers:
            for t, (src_of, dst_of) in enumerate(routes):
                pltpu.make_async_remote_copy(src_ref=src_of(src[t], peer), dst_ref=dst_of(land[t], me), send_sem=send.at[t * N_DEV + k],
                                             recv_sem=recv.at[t * N_DEV + k], device_id=dev, device_id_type=MESH).start()
        token[...] = jnp.zeros_like(token)

    bufs = list(srcs) + list(lands)
    outs = pl.pallas_call(
        body, name=name,
        out_shape=(pltpu.SemaphoreType.DMA((n * N_DEV,)), pltpu.SemaphoreType.DMA((n * N_DEV,)),
                   *[pltpu.HBM(a.shape, a.dtype) for a in bufs], _sds((8, LANES))),
        in_specs=[_HBM] * (2 * n) + [pl.BlockSpec(memory_space=pl.ANY)],
        out_specs=(_SEM, _SEM, *[_HBM] * (2 * n), pl.BlockSpec(memory_space=pltpu.VMEM)),
        input_output_aliases={i: 2 + i for i in range(2 * n)},
        compiler_params=pltpu.CompilerParams(has_side_effects=_EFFECT),
    )(*[pltpu.with_memory_space_constraint(a, pltpu.HBM) for a in bufs], after)
    return outs[0], outs[1], list(outs[2:2 + n]), list(outs[2 + n:2 + 2 * n]), outs[-1]


def _split_wait(send, recv, srcs, lands, modes, layer, after, name):
    n = len(srcs)
    routes = [_route(m, layer) for m in modes]

    def body(*refs):
        src, land = refs[:n], refs[n:2 * n]
        send_r, recv_r = refs[2 * n], refs[2 * n + 1]
        _, peers = _peers()
        for k, dev, peer in peers:
            for t, (src_of, dst_of) in enumerate(routes):
                cp = pltpu.make_async_remote_copy(src_ref=src_of(src[t], peer), dst_ref=dst_of(land[t], peer), send_sem=send_r.at[t * N_DEV + k],
                                                  recv_sem=recv_r.at[t * N_DEV + k], device_id=dev, device_id_type=MESH)
                cp.wait_send()
                cp.wait_recv()

    bufs = list(srcs) + list(lands)
    outs = pl.pallas_call(
        body, name=name, out_shape=tuple(pltpu.HBM(a.shape, a.dtype) for a in bufs),
        in_specs=[_HBM] * (2 * n) + [_SEM, _SEM, pl.BlockSpec(memory_space=pl.ANY)], out_specs=[_HBM] * (2 * n),
        input_output_aliases={i: i for i in range(2 * n)},
        compiler_params=pltpu.CompilerParams(has_side_effects=_EFFECT),
    )(*bufs, send, recv, after)
    return list(outs[:n]), list(outs[n:])


SHARDED = ("w_in", "glu_w1", "glu_w2", "w_out", "w_ff1", "w_ff2", "w_ple_gate", "w_ple_proj")
SMALL = ("attn_norm_g", "gmlp_ln_g", "gmlp_ln_b", "gmlp_ws", "gmlp_bs", "q_norm_g", "k_norm_g", "sinks", "ssm_a_re", "ssm_a_im",
         "ssm_log_dt", "ssm_b_re", "ssm_b_im", "ssm_c_re", "ssm_c_im", "ssm_d", "mix_out_g", "mlp_norm_g", "ple_norm_g")
WEIGHTS = ("attn_norm_g", "w_in", "gmlp_ln_g", "gmlp_ln_b", "gmlp_ws", "gmlp_bs", "q_norm_g", "k_norm_g", "sinks", "ssm_a_re", "ssm_a_im",
           "ssm_log_dt", "ssm_b_re", "ssm_b_im", "ssm_c_re", "ssm_c_im", "ssm_d", "glu_w1", "glu_w2", "mix_out_g", "w_out", "mlp_norm_g",
           "w_ff1", "w_ff2", "ple_norm_g", "w_ple_gate", "w_ple_proj")
FLAT_COLS = 1024


PACK_TILE_ROWS = 8
PACK_ROWS_MULTIPLE = PACK_TILE_ROWS * N_DEV


def _packed_rows(shape):
    return -(-math.prod(shape) // (PACK_TILE_ROWS * FLAT_COLS)) * PACK_TILE_ROWS


def _pack(arrs, dtype):
    blocks = []
    for a in arrs:
        flat = a.astype(dtype).reshape(-1)
        pad = _packed_rows(a.shape) * FLAT_COLS - flat.shape[0]
        if pad:
            flat = jnp.concatenate([flat, jnp.zeros((pad,), dtype)])
        blocks.append(flat.reshape(-1, FLAT_COLS))
    rows = sum(b.shape[0] for b in blocks)
    if rows % PACK_ROWS_MULTIPLE:
        blocks.append(jnp.zeros((PACK_ROWS_MULTIPLE - rows % PACK_ROWS_MULTIPLE, FLAT_COLS), dtype))
    return jnp.concatenate(blocks, axis=0)


def _unpack(flat, shapes):
    out, r = [], 0
    for s in shapes:
        nr = _packed_rows(s)
        out.append(flat[r:r + nr].reshape(-1)[:math.prod(s)].reshape(s))
        r += nr
    return out


def _from_col_major(s):
    n, rows, cs = s.shape
    return s.transpose(1, 0, 2).reshape(rows, n * cs)


EARLY = ("w_in", "glu_w1", "glu_w2")
LATE = ("w_out", "w_ff1", "w_ff2", "w_ple_gate", "w_ple_proj")
GRADS_MID = ("w_ff1", "w_ff2", "w_ple_gate", "w_ple_proj")
GRADS_END = ("w_in", "glu_w1", "glu_w2", "w_out")


def _layer_fwd(h, p, l, cs, sn, W, late_weights, S, sp):
    z = _inproj_fwd(h, S["attn_norm_g"], W["w_in"])
    ya = _gmlp_fwd(z, S["lng"], S["lnb"], S["gmlp_ws"], S["bsx"])
    yb = _attn_fwd(z, cs, sn, S["qg"], S["kg"], S["sinks"])
    yc, y, xr, xi = _ssm_fwd(z, sp["bbd"], sp["pwr"], sp["pwi"], sp["cbd"], S["ssm_d"], W["glu_w1"], W["glu_w2"])
    late, token = late_weights(ya[0:8, 0:LANES] + yb[0:8, 0:LANES] + yc[0:8, 0:LANES])
    W = {**W, **late}
    h1 = _merge_fwd(h, ya, yb, yc, _behind(S["mix_out_g"], token), W["w_out"])
    h2, r = _mlp_fwd(h1, S["mlp_norm_g"], W["w_ff1"], W["w_ff2"])
    h3, gp, e = _ple_fwd(h2, p, l, S["ple_norm_g"], W["w_ple_gate"], W["w_ple_proj"])
    return h3, dict(h=h, z=z, ya=ya, yb=yb, yc=yc, y=y, xr=xr, xi=xi, h1=h1, r=r, h2=h2, gp=gp, e=e), W


def _layer_bwd(g3, p, l, cs, sn, W, S, sp, A, raw, mid_bwd):
    G = {}
    g2, dwg, G["w_ple_proj"], G["ple_norm_g"] = _ple_bwd(g3, A["h2"], p, l, S["ple_norm_g"], A["gp"], A["e"], W["w_ple_gate"])
    G["w_ple_gate"] = dwg.reshape(N_DEV, -1, D_MODEL)
    g1, hn, da, G["mlp_norm_g"] = _mlp_bwd(g2, A["h1"], S["mlp_norm_g"], A["r"], W["w_ff1"], W["w_ff2"])
    G["w_ff1"] = _tn(hn, da, D_MODEL, D_FF, bm=D_MODEL, bn=FF_BLOCK, n_major=True, dtype=WIRE, name="tn_ff1")
    G["w_ff2"] = _tn(A["r"], g2, D_FF, D_MODEL, bm=D_MODEL, bn=D_MODEL, dtype=WIRE, name="tn_ff2").reshape(N_DEV, -1, D_MODEL)
    token = mid_bwd(g1, G)
    dya, dyb, dyc, dwo, G["mix_out_g"] = _merge_bwd(g1, A["ya"], A["yb"], A["yc"], _behind(S["mix_out_g"], token), W["w_out"])
    G["w_out"] = dwo.reshape(N_DEV, -1, D_MODEL)
    dzc, dw1, dw2, dcr, dci, dbbd, dlr, dli, dd = _ssm_bwd(dyc, A["z"], A["y"], A["xr"], A["xi"], sp["bbd"], sp["pwr"], sp["pwi"], sp["cbd"],
                                                          S["ssm_d"], W["glu_w1"], W["glu_w2"])
    G["glu_w1"] = dw1.astype(WIRE).reshape(N_DEV, -1, C_WIDTH)
    G["glu_w2"] = dw2.astype(WIRE).reshape(N_DEV, -1, C_WIDTH)
    dare, daim, dldt, dbtr, dbti, dcre, dcim = _ssm_param_bwd(
        raw["are"], raw["aim"], raw["ldt"], raw["are_x"], raw["aim_x"], raw["ldt_x"], raw["btr"], raw["bti"],
        dlr.reshape(C_GROUPS, C_STATE), dli.reshape(C_GROUPS, C_STATE), dbbd, dcr, dci)
    G["ssm_a_re"], G["ssm_a_im"], G["ssm_log_dt"] = dare, daim, dldt[:, 0]
    G["ssm_b_re"] = dbtr.reshape(C_GROUPS, C_GROUP, C_STATE).transpose(0, 2, 1)
    G["ssm_b_im"] = dbti.reshape(C_GROUPS, C_GROUP, C_STATE).transpose(0, 2, 1)
    G["ssm_c_re"] = dcre.reshape(C_GROUPS, C_GROUP, C_STATE)
    G["ssm_c_im"] = dcim.reshape(C_GROUPS, C_GROUP, C_STATE)
    G["ssm_d"] = dd.reshape(C_GROUPS, C_GROUP)
    dzq, dzk, dzv, dqg, dkg, dsk = _attn_bwd(A["z"], cs, sn, S["qg"], S["kg"], S["sinks"], A["yb"], dyb)
    G["q_norm_g"] = dqg[0, :HEAD_DIM] + dqg[0, HEAD_DIM:]
    G["k_norm_g"] = dkg[0, :HEAD_DIM] + dkg[0, HEAD_DIM:]
    G["sinks"] = dsk[:, 0]
    dza, dws, dbs, dlng, dlnb = _gmlp_bwd(A["z"], dya, S["lng"], S["lnb"], S["gmlp_ws"], S["bsx"])
    G["gmlp_ws"] = dws
    G["gmlp_bs"] = dbs[:, :, 0]
    G["gmlp_ln_g"] = dlng.reshape(A_HEADS, 2, HEAD_DIM)[:, 1]
    G["gmlp_ln_b"] = dlnb.reshape(A_HEADS, 2, HEAD_DIM)[:, 1]
    g0, G["w_in"], G["attn_norm_g"] = _inproj_bwd(g1, A["h"], S["attn_norm_g"], W["w_in"], dza, dzq, dzk, dzv, dzc)
    return g0, G


def _small_layouts(P, l):
    def row(a):
        return a.reshape(1, -1)

    zeros = jnp.zeros((A_HEADS, HEAD_DIM), F32)
    S = dict(
        attn_norm_g=row(P["attn_norm_g"][l]), mix_out_g=row(P["mix_out_g"][l]), mlp_norm_g=row(P["mlp_norm_g"][l]),
        ple_norm_g=row(P["ple_norm_g"][l]),
        lng=jnp.stack([zeros, P["gmlp_ln_g"][l]], axis=1).reshape(1, IN_A),
        lnb=jnp.stack([zeros, P["gmlp_ln_b"][l]], axis=1).reshape(1, IN_A),
        gmlp_ws=P["gmlp_ws"][l],
        bsx=jnp.broadcast_to(P["gmlp_bs"][l][:, :, None], (A_HEADS, CHUNK, CHUNK)),
        qg=jnp.tile(P["q_norm_g"][l], 2).reshape(1, LANES), kg=jnp.tile(P["k_norm_g"][l], 2).reshape(1, LANES),
        sinks=jnp.broadcast_to(P["sinks"][l][:, None], (8, LANES)),
        ssm_d=row(P["ssm_d"][l]),
    )
    are, aim = P["ssm_a_re"][l], P["ssm_a_im"][l]
    ldt = jnp.broadcast_to(P["ssm_log_dt"][l][:, None], (C_GROUPS, C_STATE))
    raw = dict(
        are=are, aim=aim, ldt=ldt,
        are_x=jnp.repeat(are, C_GROUP, axis=0), aim_x=jnp.repeat(aim, C_GROUP, axis=0), ldt_x=jnp.repeat(ldt, C_GROUP, axis=0),
        btr=P["ssm_b_re"][l].transpose(0, 2, 1).reshape(C_WIDTH, C_STATE), bti=P["ssm_b_im"][l].transpose(0, 2, 1).reshape(C_WIDTH, C_STATE),
        cre=P["ssm_c_re"][l].reshape(C_WIDTH, C_STATE), cim=P["ssm_c_im"][l].reshape(C_WIDTH, C_STATE),
    )
    return S, raw


def _ssm_prep_layer(raw):
    bbd, cbd, pwr, pwi = _ssm_prep(raw["are"].reshape(1, N_STATE), raw["aim"].reshape(1, N_STATE), raw["ldt"].reshape(1, N_STATE),
                                   raw["are_x"], raw["aim_x"], raw["ldt_x"], raw["btr"], raw["bti"], raw["cre"], raw["cim"])
    return dict(bbd=bbd, cbd=cbd, pwr=pwr, pwi=pwi)


def _behind(row, token):
    return row if token is None else row + token[0:1, 0:1]


def _local_step(x, p, positions, target, P, weights_of, mid_bwd, after_bwd):
    inv = 1.0 / (ROPE_THETA ** (jnp.arange(0, HEAD_DIM, 2, dtype=F32) / HEAD_DIM))
    cs, sn = _rope_tables(positions.reshape(-1, 1), jnp.tile(inv, 4).reshape(1, LANES))
    h = x
    acts, smalls, weights = [], [], []
    for l in range(DEPTH):
        W, late_weights, token = weights_of(l, h)
        S, raw = _small_layouts(P, l)
        sp = _ssm_prep_layer(raw)
        h, A, W = _layer_fwd(h, p, l, cs, sn, W, late_weights, {**S, "attn_norm_g": _behind(S["attn_norm_g"], token)}, sp)
        acts.append(A)
        smalls.append((S, raw, sp))
        weights.append(W)
    g, lsum = _loss_head(h, target)
    grads = [None] * DEPTH
    token = None
    for l in reversed(range(DEPTH)):
        S, raw, sp = smalls[l]
        g, grads[l] = _layer_bwd(g, p, l, cs, sn, weights[l], {**S, "ple_norm_g": _behind(S["ple_norm_g"], token)}, sp, acts[l], raw,
                                 functools.partial(mid_bwd, l))
        token = after_bwd(l, g, grads[l])
    return lsum[0, 0], g, grads


def _layer_weights(g):
    layout = dict(
        w_in=_from_col_major, glu_w1=lambda a: a.reshape(C_WIDTH, C_WIDTH), glu_w2=lambda a: a.reshape(C_WIDTH, C_WIDTH),
        w_out=lambda a: a.reshape(D_MODEL, D_MODEL), w_ff1=lambda a: a, w_ff2=lambda a: a.reshape(D_FF, D_MODEL),
        w_ple_gate=lambda a: a.reshape(D_MODEL, D_MODEL), w_ple_proj=_from_col_major)
    return {n: layout[n](a) for n, a in g.items()}


def kernel(x, p, positions, attn_norm_g, w_in, gmlp_ln_g, gmlp_ln_b, gmlp_ws, gmlp_bs, q_norm_g, k_norm_g, sinks, ssm_a_re, ssm_a_im, ssm_log_dt, ssm_b_re, ssm_b_im, ssm_c_re, ssm_c_im, ssm_d, glu_w1, glu_w2, mix_out_g, w_out, mlp_norm_g, w_ff1, w_ff2, ple_norm_g, w_ple_gate, w_ple_proj, loss_target, m_attn_norm_g, m_w_in, m_gmlp_ln_g, m_gmlp_ln_b, m_gmlp_ws, m_gmlp_bs, m_q_norm_g, m_k_norm_g, m_sinks, m_ssm_a_re, m_ssm_a_im, m_ssm_log_dt, m_ssm_b_re, m_ssm_b_im, m_ssm_c_re, m_ssm_c_im, m_ssm_d, m_glu_w1, m_glu_w2, m_mix_out_g, m_w_out, m_mlp_norm_g, m_w_ff1, m_w_ff2, m_ple_norm_g, m_w_ple_gate, m_w_ple_proj, v_attn_norm_g, v_w_in, v_gmlp_ln_g, v_gmlp_ln_b, v_gmlp_ws, v_gmlp_bs, v_q_norm_g, v_k_norm_g, v_sinks, v_ssm_a_re, v_ssm_a_im, v_ssm_log_dt, v_ssm_b_re, v_ssm_b_im, v_ssm_c_re, v_ssm_c_im, v_ssm_d, v_glu_w1, v_glu_w2, v_mix_out_g, v_w_out, v_mlp_norm_g, v_w_ff1, v_w_ff2, v_ple_norm_g, v_w_ple_gate, v_w_ple_proj):
    env = dict(locals())
    P = {n: env[n] for n in WEIGHTS}
    M = {n: env["m_" + n] for n in WEIGHTS}
    V = {n: env["v_" + n] for n in WEIGHTS}
    return _step(x, p, positions, loss_target, P, M, V)


def _step(x, p, positions, loss_target, P, M, V):
    small_shapes = [P[n].shape for n in SMALL]
    me = 4 * lax.axis_index("x") + 2 * lax.axis_index("y") + lax.axis_index("c")
    nothing = jnp.zeros((8, LANES), F32)

    def put(land, own, lead):
        return lax.dynamic_update_slice(land, own.reshape((1,) * len(lead) + own.shape), tuple(lead) + (0,) * own.ndim)

    def gather_start(l, names, after, tag):
        shards = [P[n][l].astype(WIRE) for n in names]
        lands = [lax.empty((N_DEV,) + s.shape, WIRE) for s in shards]
        send, recv, shards, lands, token = _split_start(shards, lands, ["all"] * len(names), 0, after, f"gather_start_{l}{tag}")
        return dict(names=names, send=send, recv=recv, shards=shards, lands=lands, token=token, name=f"gather_wait_{l}{tag}")

    def gather_wait(f, after):
        shards, lands = _split_wait(f["send"], f["recv"], f["shards"], f["lands"], ["all"] * len(f["names"]), 0, after, f["name"])
        return dict(zip(f["names"], [put(ld, sh, (me,)) for sh, ld in zip(shards, lands)]))

    first = gather_start(0, EARLY, nothing, "a")
    flying = {0: (first, gather_start(0, LATE, first["token"], "b"))}

    def weights_of(l, h):
        fa, fb = flying.pop(l)
        got = gather_wait(fa, h)
        if fb is None:
            token = None
            if l + 1 < DEPTH:
                flying[l + 1] = (gather_start(l + 1, SHARDED, got["w_in"], ""), None)
                token = flying[l + 1][0]["token"]
            W = _layer_weights(got)
            return {n: W[n] for n in EARLY}, (lambda after: ({n: W[n] for n in LATE}, None)), token

        def late_weights(after):
            late = gather_wait(fb, after)
            flying[l + 1] = (gather_start(l + 1, SHARDED, late["w_out"], ""), None)
            return _layer_weights(late), flying[l + 1][0]["token"]

        return _layer_weights(got), late_weights, fb["token"]

    grad_lands = {n: lax.empty((N_DEV,) + P[n].shape, WIRE) for n in SHARDED}
    sent = []

    def scatter_start(l, names, parts, lands, tag):
        send, recv, parts, lands, token = _split_start(parts, lands, ["own"] * len(parts), l, nothing, f"scatter_start_{l}{tag}")
        sent.append(dict(l=l, names=names, send=send, recv=recv, parts=parts, lands=lands, name=f"scatter_wait_{l}{tag}"))
        return token

    def scatter_wait(after):
        f = sent.pop(0)
        parts, lands = _split_wait(f["send"], f["recv"], f["parts"], f["lands"], ["own"] * len(f["parts"]), f["l"], after, f["name"])
        lands = [put(ld, lax.dynamic_index_in_dim(part, me, 0, keepdims=False), (me, f["l"])) for part, ld in zip(parts, lands)]
        return dict(zip(f["names"], lands))

    def mid_bwd(l, g1, G):
        if l > 0:
            return None
        grad_lands.update(scatter_wait(g1))
        return scatter_start(0, GRADS_MID, [G[n] for n in GRADS_MID], [grad_lands[n] for n in GRADS_MID], "a")

    small_land = []

    def after_bwd(l, g, G):
        if l > 0:
            if sent:
                grad_lands.update(scatter_wait(g))
            return scatter_start(l, SHARDED, [G[n] for n in SHARDED], [grad_lands[n] for n in SHARDED], "")
        sflat = _pack([jnp.stack([grads_of[k][n] for k in range(DEPTH)]) for n in SMALL], F32)
        sparts = sflat.reshape(N_DEV, -1, FLAT_COLS)
        small_land.append(sflat.shape)
        return scatter_start(0, GRADS_END + ("small",), [G[n] for n in GRADS_END] + [sparts],
                             [grad_lands[n] for n in GRADS_END] + [lax.empty((N_DEV, 1) + sparts.shape[1:], F32)], "b")

    grads_of = {}

    def after_bwd_recording(l, g, G):
        grads_of[l] = G
        grads_of["token"] = after_bwd(l, g, G)
        return grads_of["token"]

    lsum, gx, grads = _local_step(x[0], p[:, 0], positions[0], loss_target[0], P, weights_of, mid_bwd, after_bwd_recording)
    G, delta, new_m, new_v = {}, {}, {}, {}

    def update(names):
        for n in names:
            shp = P[n].shape
            res = _sum_adamw(grad_lands[n].reshape(N_DEV, -1, shp[-1]), *(a.reshape(-1, shp[-1]) for a in (P[n], M[n], V[n])))
            G[n], delta[n], new_m[n], new_v[n] = (a.reshape(shp) for a in res)

    grad_lands.update(scatter_wait(grads_of["token"]))
    update(GRADS_MID)
    last = scatter_wait(sum(new_v[n].reshape(-1, LANES)[:8] for n in GRADS_MID))
    small_parts = last.pop("small")
    grad_lands.update(last)
    mine = _sum_slots(small_parts[:, 0])
    send, recv, srcs, lands, _ = _split_start([mine], [lax.empty((N_DEV,) + mine.shape, F32)], ["all"], 0, nothing, "gather_small_start")
    update(GRADS_END)
    srcs, lands = _split_wait(send, recv, srcs, lands, ["all"], 0, sum(new_v[n].reshape(-1, LANES)[:8] for n in GRADS_END), "gather_small_wait")
    small_sum = put(lands[0], srcs[0], (me,)).reshape((1,) + small_land[0])
    res = _sum_adamw(small_sum, _pack([P[n] for n in SMALL], F32), _pack([M[n] for n in SMALL], F32), _pack([V[n] for n in SMALL], F32))
    for dst, flat in zip((G, delta, new_m, new_v), res):
        dst.update(zip(SMALL, _unpack(flat, small_shapes)))
    loss = lax.psum(lsum, ("x", "y", "c"))
    return (loss, gx[None], *[G[n] for n in WEIGHTS], *[delta[n] for n in WEIGHTS], *[new_m[n] for n in WEIGHTS], *[new_v[n] for n in WEIGHTS])
```

```python
import functools
import math

import jax
import jax.numpy as jnp
from jax import lax
from jax.experimental import pallas as pl
from jax.experimental.pallas import tpu as pltpu

F32 = jnp.float32
MXU = jnp.bfloat16
WIRE = jnp.bfloat16

D_MODEL = 1024
DEPTH = 4
HEAD_DIM = 64
A_WIDTH = 256
A_HEADS = 4
CHUNK = 128
B_WIDTH = 512
WINDOW = 128
C_WIDTH = 256
C_GROUP = 16
C_GROUPS = 16
C_STATE = 64
N_STATE = C_GROUPS * C_STATE
IN_A = 2 * A_WIDTH
KV_WIDTH = 2 * HEAD_DIM
IN_COLS = IN_A + B_WIDTH + 2 * KV_WIDTH + C_WIDTH
Q_BLOCK = IN_A // B_WIDTH
K_BLOCK = (IN_A + B_WIDTH) // KV_WIDTH
V_BLOCK = K_BLOCK + 1
C_BLOCK = (IN_A + B_WIDTH + 2 * KV_WIDTH) // C_WIDTH
TOKEN_TILE = 512
D_FF = 4096
PLE_DIM = 256
EPS = 1e-6
ROPE_THETA = 10000.0
SCALE = HEAD_DIM ** -0.5
NEG = -1e30
N_DEV = 8

ADAM_LR = 0.001
ADAM_B1 = 0.9
ADAM_B2 = 0.999
ADAM_EPS = 1e-08
ADAM_WD = 0.01
ADAM_STEP = 10

V7X_VMEM_BYTES = 64 * 2 ** 20
VMEM_LIMIT = V7X_VMEM_BYTES - 8 * 2 ** 20
LANES = 128

MESH = pl.DeviceIdType.MESH


def _cp(*sem):
    return pltpu.CompilerParams(dimension_semantics=sem, vmem_limit_bytes=VMEM_LIMIT)


def _sds(shape, dtype=F32):
    return jax.ShapeDtypeStruct(shape, dtype)


def _mm(a, b):
    return jnp.dot(a.astype(MXU), b.astype(MXU), preferred_element_type=F32)


def _mm_nt(a, b):
    return lax.dot_general(a.astype(MXU), b.astype(MXU), (((1,), (1,)), ((), ())), preferred_element_type=F32)


def _mm_tn(a, b):
    return lax.dot_general(a.astype(MXU), b.astype(MXU), (((0,), (0,)), ((), ())), preferred_element_type=F32)


def _lane(shape):
    return lax.broadcasted_iota(jnp.int32, shape, len(shape) - 1)


def _row(shape):
    return lax.broadcasted_iota(jnp.int32, shape, 0)


_GELU_C = math.sqrt(2.0 / math.pi)


def _gelu(x):
    return 0.5 * x * (1.0 + jnp.tanh(_GELU_C * (x + 0.044715 * (x * x * x))))


def _gelu_grad(x):
    t = jnp.tanh(_GELU_C * (x + 0.044715 * (x * x * x)))
    return 0.5 * (1.0 + t) + 0.5 * x * (1.0 - t * t) * (_GELU_C * (1.0 + 3.0 * 0.044715 * (x * x)))


def _sigmoid(x):
    return 1.0 / (1.0 + jnp.exp(-x))


def _rms_stat(x):
    return lax.rsqrt(jnp.mean(x * x, axis=-1, keepdims=True) + EPS)


def _rms_bwd(x, r, g, dy):
    xh = x * r
    dxh = dy * g
    dx = r * (dxh - xh * jnp.mean(dxh * xh, axis=-1, keepdims=True))
    return dx, jnp.sum(dy * xh, axis=0, keepdims=True)


def _tril(w):
    return jnp.where(_row(w.shape) >= _lane(w.shape), w, 0.0)


def _swap64(x):
    return pltpu.roll(x, HEAD_DIM, 1)


def _group_sum64(x, lo):
    s_lo = jnp.sum(jnp.where(lo, x, 0.0), axis=-1, keepdims=True)
    s_hi = jnp.sum(jnp.where(lo, 0.0, x), axis=-1, keepdims=True)
    return jnp.where(lo, s_lo, s_hi)


def _partner(x):
    n = x.shape[-1]
    first = (_lane(x.shape) % HEAD_DIM) < HEAD_DIM // 2
    return jnp.where(first, pltpu.roll(x, n - HEAD_DIM // 2, 1), pltpu.roll(x, HEAD_DIM // 2, 1))


def _rope(y, cs, sn):
    return y * cs + _partner(y) * sn


def _rope_bwd(d, cs, sn):
    return d * cs + _partner(d * sn)


def _qk_norm_rope(x, g, cs, sn):
    lo = _lane(x.shape) < HEAD_DIM
    r = lax.rsqrt(_group_sum64(x * x, lo) * (1.0 / HEAD_DIM) + EPS)
    xh = x * r
    return _rope(xh * g, cs, sn), xh, r


def _qk_norm_rope_bwd(xh, r, g, cs, sn, d):
    lo = _lane(xh.shape) < HEAD_DIM
    dy = _rope_bwd(d, cs, sn)
    dxh = dy * g
    m = _group_sum64(dxh * xh, lo) * (1.0 / HEAD_DIM)
    return r * (dxh - xh * m), jnp.sum(dy * xh, axis=0, keepdims=True)


def _gmlp_head(blk, g, b):
    hi = _lane(blk.shape) >= HEAD_DIM
    mu = jnp.sum(jnp.where(hi, blk, 0.0), axis=-1, keepdims=True) * (1.0 / HEAD_DIM)
    xc = jnp.where(hi, blk - mu, 0.0)
    rstd = lax.rsqrt(jnp.sum(xc * xc, axis=-1, keepdims=True) * (1.0 / HEAD_DIM) + EPS)
    vhat = xc * rstd
    return vhat * g + b, vhat, rstd


def _rope_tables(pos_col, inv_row):
    T = pos_col.shape[0]
    tm = min(T, 1024)

    def body(p_ref, inv_ref, cs_ref, sn_ref):
        ang = p_ref[...].astype(F32) * inv_ref[...]
        s = jnp.sin(ang)
        cs_ref[...] = jnp.cos(ang)
        sn_ref[...] = jnp.where((_lane(ang.shape) % HEAD_DIM) < HEAD_DIM // 2, -s, s)

    blk = pl.BlockSpec((tm, LANES), lambda i: (i, 0))
    return pl.pallas_call(
        body, name="rope_tables", grid=(T // tm,),
        in_specs=[pl.BlockSpec((tm, 1), lambda i: (i, 0)), pl.BlockSpec((1, LANES), lambda i: (0, 0))],
        out_specs=[blk, blk], out_shape=[_sds((T, LANES))] * 2, compiler_params=_cp("parallel"))(pos_col, inv_row)


def _inproj_fwd(h, g, w):
    T = h.shape[0]
    tm = min(T, TOKEN_TILE)

    def body(h_ref, g_ref, w_ref, z_ref):
        x = h_ref[...]
        z_ref[...] = _mm(x * _rms_stat(x) * g_ref[...], w_ref[...])

    return pl.pallas_call(
        body, name="inproj_fwd", grid=(T // tm,),
        in_specs=[pl.BlockSpec((tm, D_MODEL), lambda i: (i, 0)), pl.BlockSpec((1, D_MODEL), lambda i: (0, 0)),
                  pl.BlockSpec((D_MODEL, IN_COLS), lambda i: (0, 0))],
        out_specs=pl.BlockSpec((tm, IN_COLS), lambda i: (i, 0)), out_shape=_sds((T, IN_COLS)),
        compiler_params=_cp("parallel"))(h, g, w)


def _inproj_bwd(gres, h, g, w, dza, dzq, dzk, dzv, dzc):
    T = h.shape[0]
    tm = min(T, TOKEN_TILE)
    nt = T // tm
    shard = IN_COLS // N_DEV

    def body(gr_ref, h_ref, g_ref, w_ref, a_ref, q_ref, k_ref, v_ref, c_ref, dh_ref, dw_ref, dg_ref, acc):
        i = pl.program_id(0)

        @pl.when(i == 0)
        def _():
            dg_ref[...] = jnp.zeros_like(dg_ref)
            acc[...] = jnp.zeros_like(acc)

        x = h_ref[...]
        r = _rms_stat(x)
        gg = g_ref[...]
        dz = jnp.concatenate([a_ref[...], q_ref[...], k_ref[...], v_ref[...], c_ref[...]], axis=1)
        dxn = _mm_nt(dz, w_ref[...])
        dx, dg = _rms_bwd(x, r, gg, dxn)
        dh_ref[...] = gr_ref[...] + dx
        dg_ref[...] += dg
        acc[...] += _mm_tn(x * r * gg, dz)

        @pl.when(i == nt - 1)
        def _():
            for d in range(N_DEV):
                dw_ref[d] = acc[:, d * shard:(d + 1) * shard].astype(WIRE)

    def rows(w_):
        return pl.BlockSpec((tm, w_), lambda i: (i, 0))

    row = pl.BlockSpec((1, D_MODEL), lambda i: (0, 0))
    return pl.pallas_call(
        body, name="inproj_bwd", grid=(nt,),
        in_specs=[rows(D_MODEL), rows(D_MODEL), row, pl.BlockSpec((D_MODEL, IN_COLS), lambda i: (0, 0)),
                  rows(IN_A), rows(B_WIDTH), rows(KV_WIDTH), rows(KV_WIDTH), rows(C_WIDTH)],
        out_specs=[rows(D_MODEL), pl.BlockSpec((N_DEV, D_MODEL, shard), lambda i: (0, 0, 0)), row],
        out_shape=[_sds((T, D_MODEL)), _sds((N_DEV, D_MODEL, shard), WIRE), _sds((1, D_MODEL))],
        scratch_shapes=[pltpu.VMEM((D_MODEL, IN_COLS), F32)],
        compiler_params=_cp("arbitrary"))(gres, h, g, w, dza, dzq, dzk, dzv, dzc)


def _gmlp_fwd(z, lng, lnb, ws, bsx):
    T = z.shape[0]
    tm = min(T, TOKEN_TILE)
    nc = tm // CHUNK

    def body(z_ref, g_ref, b_ref, w_ref, bs_ref, ya_ref):
        zg = _gelu(z_ref[...])
        lo = _lane((tm, LANES)) < HEAD_DIM
        prods = []
        for hd in range(A_HEADS):
            sl = slice(hd * LANES, (hd + 1) * LANES)
            blk = zg[:, sl]
            vn, _, _ = _gmlp_head(blk, g_ref[:, sl], b_ref[:, sl])
            wm = _tril(w_ref[hd])
            sv = jnp.concatenate([_mm(wm, vn[c * CHUNK:(c + 1) * CHUNK]) + bs_ref[hd] for c in range(nc)], axis=0)
            prods.append(blk * _swap64(sv))
        ya_ref[:, 0:LANES] = jnp.where(lo, prods[0], _swap64(prods[1]))
        ya_ref[:, LANES:2 * LANES] = jnp.where(lo, prods[2], _swap64(prods[3]))

    row = pl.BlockSpec((1, IN_A), lambda i: (0, 0))
    mat = pl.BlockSpec((A_HEADS, CHUNK, CHUNK), lambda i: (0, 0, 0))
    return pl.pallas_call(
        body, name="gmlp_fwd", grid=(T // tm,),
        in_specs=[pl.BlockSpec((tm, IN_A), lambda i: (i, 0)), row, row, mat, mat],
        out_specs=pl.BlockSpec((tm, A_WIDTH), lambda i: (i, 0)), out_shape=_sds((T, A_WIDTH)),
        compiler_params=_cp("parallel"))(z, lng, lnb, ws, bsx)


def _gmlp_bwd(z, dya, lng, lnb, ws, bsx):
    T = z.shape[0]
    tm = min(T, TOKEN_TILE)
    nc = tm // CHUNK

    def body(z_ref, dya_ref, g_ref, b_ref, w_ref, bs_ref, dza_ref, dw_ref, dbs_ref, dg_ref, db_ref):
        @pl.when(pl.program_id(0) == 0)
        def _():
            dw_ref[...] = jnp.zeros_like(dw_ref)
            dbs_ref[...] = jnp.zeros_like(dbs_ref)
            dg_ref[...] = jnp.zeros_like(dg_ref)
            db_ref[...] = jnp.zeros_like(db_ref)

        za = z_ref[...]
        zg = _gelu(za)
        gp = _gelu_grad(za)
        lo = _lane((tm, LANES)) < HEAD_DIM
        for hd in range(A_HEADS):
            sl = slice(hd * LANES, (hd + 1) * LANES)
            blk = zg[:, sl]
            g = g_ref[:, sl]
            vn, vhat, rstd = _gmlp_head(blk, g, b_ref[:, sl])
            wm = _tril(w_ref[hd])
            pair = dya_ref[:, (hd // 2) * LANES:(hd // 2 + 1) * LANES]
            dy = jnp.where(lo, pair if hd % 2 == 0 else _swap64(pair), 0.0)
            dsv = _swap64(dy * blk)
            svs, dvns = [], []
            dw = jnp.zeros((CHUNK, CHUNK), F32)
            dbs = jnp.zeros((CHUNK, 1), F32)
            for c in range(nc):
                cs = slice(c * CHUNK, (c + 1) * CHUNK)
                svs.append(_mm(wm, vn[cs]) + bs_ref[hd])
                dw = dw + _mm_nt(dsv[cs], vn[cs])
                dbs = dbs + jnp.sum(dsv[cs], axis=-1, keepdims=True)
                dvns.append(_mm_tn(wm, dsv[cs]))
            sv = jnp.concatenate(svs, axis=0)
            dvn = jnp.concatenate(dvns, axis=0)
            dw_ref[hd] += _tril(dw)
            dbs_ref[hd] += jnp.broadcast_to(dbs, (CHUNK, CHUNK))
            dg_ref[:, sl] += jnp.sum(dvn * vhat, axis=0, keepdims=True)
            db_ref[:, sl] += jnp.sum(dvn, axis=0, keepdims=True)
            du = dy * _swap64(sv)
            dvh = dvn * g
            m1 = jnp.sum(dvh, axis=-1, keepdims=True) * (1.0 / HEAD_DIM)
            m2 = jnp.sum(dvh * vhat, axis=-1, keepdims=True) * (1.0 / HEAD_DIM)
            dv = jnp.where(lo, 0.0, rstd * (dvh - m1 - vhat * m2))
            dza_ref[:, sl] = (du + dv) * gp[:, sl]

    row = pl.BlockSpec((1, IN_A), lambda i: (0, 0))
    mat = pl.BlockSpec((A_HEADS, CHUNK, CHUNK), lambda i: (0, 0, 0))
    return pl.pallas_call(
        body, name="gmlp_bwd", grid=(T // tm,),
        in_specs=[pl.BlockSpec((tm, IN_A), lambda i: (i, 0)), pl.BlockSpec((tm, A_WIDTH), lambda i: (i, 0)), row, row, mat, mat],
        out_specs=[pl.BlockSpec((tm, IN_A), lambda i: (i, 0)), mat, mat, row, row],
        out_shape=[_sds((T, IN_A)), _sds((A_HEADS, CHUNK, CHUNK)), _sds((A_HEADS, CHUNK, CHUNK)), _sds((1, IN_A)), _sds((1, IN_A))],
        compiler_params=_cp("arbitrary"))(z, dya, lng, lnb, ws, bsx)


def _attn_specs(T, tq, tile_of):
    nb = tq // WINDOW

    def prev(i):
        return jnp.maximum(tile_of(i) * nb - 1, 0)

    row = pl.BlockSpec((1, LANES), lambda i: (0, 0))
    return [
        pl.BlockSpec((tq, B_WIDTH), lambda i: (tile_of(i), Q_BLOCK)),
        pl.BlockSpec((tq, LANES), lambda i: (tile_of(i), K_BLOCK)),
        pl.BlockSpec((tq, LANES), lambda i: (tile_of(i), V_BLOCK)),
        pl.BlockSpec((WINDOW, LANES), lambda i: (prev(i), K_BLOCK)),
        pl.BlockSpec((WINDOW, LANES), lambda i: (prev(i), V_BLOCK)),
        pl.BlockSpec((tq, LANES), lambda i: (tile_of(i), 0)),
        pl.BlockSpec((tq, LANES), lambda i: (tile_of(i), 0)),
        pl.BlockSpec((WINDOW, LANES), lambda i: (prev(i), 0)),
        pl.BlockSpec((WINDOW, LANES), lambda i: (prev(i), 0)),
        row, row,
        pl.BlockSpec((8, LANES), lambda i: (0, 0)),
    ]


def _attn_bias(first):
    qi = lax.broadcasted_iota(jnp.int32, (WINDOW, 2 * WINDOW), 0)
    kj = lax.broadcasted_iota(jnp.int32, (WINDOW, 2 * WINDOW), 1)
    diff = qi + WINDOW - kj
    ok = (diff >= 0) & (diff < WINDOW) & ((kj >= WINDOW) | jnp.logical_not(first))
    return jnp.where(ok, 0.0, NEG)


def _dup_heads(x, lo):
    sw = _swap64(x)
    return jnp.where(lo, x, sw), jnp.where(lo, sw, x)


HEADS_PER_KV = 4


def _stack_heads(x0, x1, lo):
    return jnp.concatenate([jnp.where(lo, x0, 0.0), jnp.where(lo, 0.0, x0), jnp.where(lo, x1, 0.0), jnp.where(lo, 0.0, x1)], axis=0)


def _unstack_heads(x4, lo):
    return (jnp.where(lo, x4[0:WINDOW], x4[WINDOW:2 * WINDOW]), jnp.where(lo, x4[2 * WINDOW:3 * WINDOW], x4[3 * WINDOW:4 * WINDOW]))


def _sink_column(sk_ref, g):
    return jnp.concatenate([jnp.broadcast_to(sk_ref[a:a + 1, 0:1], (WINDOW, 1)) for a in range(HEADS_PER_KV * g, HEADS_PER_KV * (g + 1))], axis=0)


def _attn_probs(q4, kw, bias, sink):
    s = _mm_nt(q4, kw)
    s = (s.reshape(HEADS_PER_KV, WINDOW, 2 * WINDOW) + bias[None]).reshape(HEADS_PER_KV * WINDOW, 2 * WINDOW)
    m = jnp.maximum(jnp.max(s, axis=-1, keepdims=True), sink)
    p = jnp.exp(s - m)
    es = jnp.exp(sink - m)
    inv = 1.0 / (jnp.sum(p, axis=-1, keepdims=True) + es)
    return p * inv, es * inv


def _attn_fwd(z, cs, sn, qg, kg, sinks):
    T = z.shape[0]
    tq = min(T, 2 * TOKEN_TILE)
    nb = tq // WINDOW

    def body(q_ref, k_ref, v_ref, kp_ref, vp_ref, cs_ref, sn_ref, csp_ref, snp_ref, qg_ref, kg_ref, sk_ref, o_ref):
        i = pl.program_id(0)
        csq, snq = cs_ref[...], sn_ref[...]
        cs_all = jnp.concatenate([csp_ref[...], csq], axis=0)
        sn_all = jnp.concatenate([snp_ref[...], snq], axis=0)
        k_all = jnp.concatenate([kp_ref[...], k_ref[...]], axis=0)
        v_all = jnp.concatenate([vp_ref[...], v_ref[...]], axis=0)
        kr, _, _ = _qk_norm_rope(k_all, kg_ref[...], cs_all, sn_all)
        lo_all = _lane(k_all.shape) < HEAD_DIM
        kd = _dup_heads(kr, lo_all)
        vd = _dup_heads(v_all, lo_all)
        lo = _lane((WINDOW, LANES)) < HEAD_DIM
        qrs = [_qk_norm_rope(q_ref[:, pr * LANES:(pr + 1) * LANES], qg_ref[...], csq, snq)[0] * SCALE for pr in range(4)]
        biases = [_attn_bias(i * nb + b == 0) for b in range(nb)]
        for g in range(2):
            sink = _sink_column(sk_ref, g)
            for b in range(nb):
                bs = slice(b * WINDOW, (b + 1) * WINDOW)
                ws = slice(b * WINDOW, (b + 2) * WINDOW)
                pn, _ = _attn_probs(_stack_heads(qrs[2 * g][bs], qrs[2 * g + 1][bs], lo), kd[g][ws], biases[b], sink)
                o0, o1 = _unstack_heads(_mm(pn, vd[g][ws]), lo)
                o_ref[bs, 2 * g * LANES:(2 * g + 1) * LANES] = o0
                o_ref[bs, (2 * g + 1) * LANES:(2 * g + 2) * LANES] = o1

    return pl.pallas_call(
        body, name="attn_fwd", grid=(T // tq,),
        in_specs=_attn_specs(T, tq, lambda i: i),
        out_specs=pl.BlockSpec((tq, B_WIDTH), lambda i: (i, 0)), out_shape=_sds((T, B_WIDTH)),
        compiler_params=_cp("parallel"))(z, z, z, z, z, cs, sn, cs, sn, qg, kg, sinks)


def _attn_bwd(z, cs, sn, qg, kg, sinks, o, do):
    T = z.shape[0]
    tq = min(T, 2 * TOKEN_TILE)
    nb = tq // WINDOW
    nt = T // tq
    tk = tq + WINDOW

    def tile_of(i):
        return nt - 1 - i

    def body(q_ref, k_ref, v_ref, kp_ref, vp_ref, cs_ref, sn_ref, csp_ref, snp_ref, qg_ref, kg_ref, sk_ref, o_ref, do_ref,
             dq_ref, dk_ref, dv_ref, dqg_ref, dkg_ref, dsk_ref, acck, accv, ck, cv):
        i = pl.program_id(0)
        ti = nt - 1 - i

        @pl.when(i == 0)
        def _():
            dqg_ref[...] = jnp.zeros_like(dqg_ref)
            dkg_ref[...] = jnp.zeros_like(dkg_ref)
            dsk_ref[...] = jnp.zeros_like(dsk_ref)
            ck[...] = jnp.zeros_like(ck)
            cv[...] = jnp.zeros_like(cv)

        csq, snq = cs_ref[...], sn_ref[...]
        cs_all = jnp.concatenate([csp_ref[...], csq], axis=0)
        sn_all = jnp.concatenate([snp_ref[...], snq], axis=0)
        k_all = jnp.concatenate([kp_ref[...], k_ref[...]], axis=0)
        v_all = jnp.concatenate([vp_ref[...], v_ref[...]], axis=0)
        kr, kh, rk = _qk_norm_rope(k_all, kg_ref[...], cs_all, sn_all)
        lo_all = _lane(k_all.shape) < HEAD_DIM
        kd = _dup_heads(kr, lo_all)
        vd = _dup_heads(v_all, lo_all)
        lo = _lane((WINDOW, LANES)) < HEAD_DIM
        acck[...] = jnp.zeros_like(acck)
        accv[...] = jnp.zeros_like(accv)
        prep = [_qk_norm_rope(q_ref[:, pr * LANES:(pr + 1) * LANES], qg_ref[...], csq, snq) for pr in range(4)]
        biases = [_attn_bias(ti * nb + b == 0) for b in range(nb)]
        dqs = [[None] * nb for _ in range(4)]
        for g in range(2):
            sink = _sink_column(sk_ref, g)
            dsink = jnp.zeros((HEADS_PER_KV * WINDOW, 1), F32)
            for b in range(nb):
                bs = slice(b * WINDOW, (b + 1) * WINDOW)
                ws = slice(b * WINDOW, (b + 2) * WINDOW)
                kw, vw = kd[g][ws], vd[g][ws]
                q4 = _stack_heads(prep[2 * g][0][bs] * SCALE, prep[2 * g + 1][0][bs] * SCALE, lo)
                pn, psink = _attn_probs(q4, kw, biases[b], sink)
                o0, o1 = o_ref[bs, 2 * g * LANES:(2 * g + 1) * LANES], o_ref[bs, (2 * g + 1) * LANES:(2 * g + 2) * LANES]
                do4 = _stack_heads(do_ref[bs, 2 * g * LANES:(2 * g + 1) * LANES], do_ref[bs, (2 * g + 1) * LANES:(2 * g + 2) * LANES], lo)
                delta = jnp.sum(do4 * jnp.concatenate([o0, o0, o1, o1], axis=0), axis=-1, keepdims=True)
                ds = pn * (_mm_nt(do4, vw) - delta)
                dsink = dsink - psink * delta
                dqs[2 * g][b], dqs[2 * g + 1][b] = _unstack_heads(_mm(ds, kw) * SCALE, lo)
                acck[g, ws, :] += _mm_tn(ds, q4)
                accv[g, ws, :] += _mm_tn(pn, do4)
            for hh in range(HEADS_PER_KV):
                a = HEADS_PER_KV * g + hh
                dsk_ref[a:a + 1, :] += jnp.zeros((1, LANES), F32) + jnp.sum(dsink[hh * WINDOW:(hh + 1) * WINDOW])
        for pr in range(4):
            _, qh, rq = prep[pr]
            dx, dg = _qk_norm_rope_bwd(qh, rq, qg_ref[...], csq, snq, jnp.concatenate(dqs[pr], axis=0))
            dq_ref[:, pr * LANES:(pr + 1) * LANES] = dx
            dqg_ref[...] += dg

        def fold(acc):
            f0 = acc[0] + _swap64(acc[0])
            f1 = acc[1] + _swap64(acc[1])
            return jnp.where(lo_all, f0, f1)

        dk_all = fold(acck)
        dv_all = fold(accv)
        pad = jnp.zeros((tq - WINDOW, LANES), F32)
        dk_own = dk_all[WINDOW:] + (jnp.concatenate([pad, ck[...]], axis=0) if nb > 1 else ck[...])
        dv_own = dv_all[WINDOW:] + (jnp.concatenate([pad, cv[...]], axis=0) if nb > 1 else cv[...])
        ck[...] = dk_all[:WINDOW]
        cv[...] = dv_all[:WINDOW]
        dxk, dgk = _qk_norm_rope_bwd(kh[WINDOW:], rk[WINDOW:], kg_ref[...], csq, snq, dk_own)
        dk_ref[...] = dxk
        dkg_ref[...] += dgk
        dv_ref[...] = dv_own

    row = pl.BlockSpec((1, LANES), lambda i: (0, 0))
    return pl.pallas_call(
        body, name="attn_bwd", grid=(nt,),
        in_specs=_attn_specs(T, tq, tile_of) + [pl.BlockSpec((tq, B_WIDTH), lambda i: (tile_of(i), 0))] * 2,
        out_specs=[pl.BlockSpec((tq, B_WIDTH), lambda i: (tile_of(i), 0)), pl.BlockSpec((tq, LANES), lambda i: (tile_of(i), 0)),
                   pl.BlockSpec((tq, LANES), lambda i: (tile_of(i), 0)), row, row, pl.BlockSpec((8, LANES), lambda i: (0, 0))],
        out_shape=[_sds((T, B_WIDTH)), _sds((T, LANES)), _sds((T, LANES)), _sds((1, LANES)), _sds((1, LANES)), _sds((8, LANES))],
        scratch_shapes=[pltpu.VMEM((2, tk, LANES), F32), pltpu.VMEM((2, tk, LANES), F32),
                        pltpu.VMEM((WINDOW, LANES), F32), pltpu.VMEM((WINDOW, LANES), F32)],
        compiler_params=_cp("arbitrary"))(z, z, z, z, z, cs, sn, cs, sn, qg, kg, sinks, o, do)


def _bbar_t(are, aim, ldt, btr, bti):
    lbr, lbi = _lam_bar(are, aim, ldt)
    den = are * are + aim * aim
    nr = lbr - 1.0
    cr = (nr * are + lbi * aim) / den
    ci = (lbi * are - nr * aim) / den
    return cr * btr - ci * bti, cr * bti + ci * btr


def _lam_bar(are, aim, ldt):
    dt = jnp.exp(ldt)
    er = jnp.exp(are * dt)
    return er * jnp.cos(aim * dt), er * jnp.sin(aim * dt)


def _block_diag(x):
    t = jnp.concatenate([x] * C_GROUPS, axis=1)
    return jnp.where(_row(t.shape) // C_GROUP == _lane(t.shape) // C_STATE, t, 0.0)


def _block_diag_fold(m):
    rg = _row((C_WIDTH, C_STATE)) // C_GROUP
    acc = jnp.zeros((C_WIDTH, C_STATE), F32)
    for g in range(C_GROUPS):
        acc = acc + jnp.where(rg == g, m[:, g * C_STATE:(g + 1) * C_STATE], 0.0)
    return acc


def _ssm_prep(are, aim, ldt, are_x, aim_x, ldt_x, btr, bti, cre, cim):
    def body(are_r, aim_r, ldt_r, arex_r, aimx_r, ldtx_r, btr_r, bti_r, cre_r, cim_r, bbd_ref, cbd_ref, pwr_ref, pwi_ref):
        lr, li = _lam_bar(are_r[...], aim_r[...], ldt_r[...])
        cr, ci = lr, li
        for r in range(SCAN_SEG):
            pwr_ref[r:r + 1, :] = cr
            pwi_ref[r:r + 1, :] = ci
            cr, ci = cr * lr - ci * li, cr * li + ci * lr
        br, bi = _bbar_t(arex_r[...], aimx_r[...], ldtx_r[...], btr_r[...], bti_r[...])
        bbd_ref[...] = jnp.concatenate([_block_diag(br), _block_diag(bi)], axis=1).astype(MXU)
        cbd_ref[...] = jnp.concatenate([_block_diag(cre_r[...]), -_block_diag(cim_r[...])], axis=1).astype(MXU)

    return pl.pallas_call(
        body, name="ssm_prep",
        out_shape=[_sds((C_WIDTH, 2 * N_STATE), MXU), _sds((C_WIDTH, 2 * N_STATE), MXU), _sds((SCAN_SEG, N_STATE)), _sds((SCAN_SEG, N_STATE))],
        compiler_params=pltpu.CompilerParams(vmem_limit_bytes=VMEM_LIMIT))(are, aim, ldt, are_x, aim_x, ldt_x, btr, bti, cre, cim)


def _ssm_param_bwd(are, aim, ldt, are_x, aim_x, ldt_x, btr, bti, dlr, dli, dbbd, dcr, dci):
    def body(are_r, aim_r, ldt_r, arex_r, aimx_r, ldtx_r, btr_r, bti_r, dlr_r, dli_r, dbbd_r, dcr_r, dci_r,
             dare_ref, daim_ref, dldt_ref, dbtr_ref, dbti_ref, dcre_ref, dcim_ref):
        _, vjp_l = jax.vjp(_lam_bar, are_r[...], aim_r[...], ldt_r[...])
        da1, di1, dl1 = vjp_l((dlr_r[...], dli_r[...]))
        dbr = _block_diag_fold(dbbd_r[:, 0:N_STATE])
        dbi = _block_diag_fold(dbbd_r[:, N_STATE:2 * N_STATE])
        _, vjp_b = jax.vjp(_bbar_t, arex_r[...], aimx_r[...], ldtx_r[...], btr_r[...], bti_r[...])
        da2, di2, dl2, dbtr, dbti = vjp_b((dbr, dbi))

        def gsum(x):
            return x.reshape(C_GROUPS, C_GROUP, C_STATE).sum(axis=1)

        dare_ref[...] = da1 + gsum(da2)
        daim_ref[...] = di1 + gsum(di2)
        dldt_ref[...] = jnp.broadcast_to(jnp.sum(dl1 + gsum(dl2), axis=-1, keepdims=True), (C_GROUPS, LANES))
        dbtr_ref[...] = dbtr
        dbti_ref[...] = dbti
        dcre_ref[...] = _block_diag_fold(dcr_r[...])
        dcim_ref[...] = -_block_diag_fold(dci_r[...])

    g = _sds((C_GROUPS, C_STATE))
    x = _sds((C_WIDTH, C_STATE))
    return pl.pallas_call(
        body, name="ssm_param_bwd", out_shape=[g, g, _sds((C_GROUPS, LANES)), x, x, x, x],
        compiler_params=pltpu.CompilerParams(vmem_limit_bytes=VMEM_LIMIT))(are, aim, ldt, are_x, aim_x, ldt_x, btr, bti, dlr, dli, dbbd, dcr, dci)


SCAN_TILE = 512
SCAN_SEG = 8


def _scan_tables(pwr_ref, pwi_ref, conj, reverse):
    row = _row((SCAN_SEG, N_STATE))
    shifts = []
    for k in (1, 2, 4):
        keep = (row < SCAN_SEG - k) if reverse else (row >= k)
        ar = jnp.broadcast_to(pwr_ref[k - 1:k, :], (SCAN_SEG, N_STATE))
        ai = jnp.broadcast_to(pwi_ref[k - 1:k, :], (SCAN_SEG, N_STATE)) * conj
        shifts.append((SCAN_SEG - k if reverse else k, jnp.where(keep, ar, 0.0), jnp.where(keep, ai, 0.0)))
    if reverse:
        pr = jnp.concatenate([pwr_ref[SCAN_SEG - 1 - r:SCAN_SEG - r, :] for r in range(SCAN_SEG)], axis=0)
        pi = jnp.concatenate([pwi_ref[SCAN_SEG - 1 - r:SCAN_SEG - r, :] for r in range(SCAN_SEG)], axis=0) * conj
    else:
        pr, pi = pwr_ref[...], pwi_ref[...] * conj
    return shifts, (pr, pi)


def _tile_scan(xr_ref, xi_ref, pwr_ref, pwi_ref, sr, si, conj, reverse):
    shifts, (pr, pi) = _scan_tables(pwr_ref, pwi_ref, conj, reverse)
    groups = xr_ref.shape[0] // SCAN_SEG
    out_row = 0 if reverse else SCAN_SEG - 1

    def step(k, c):
        cr, ci = c
        g = groups - 1 - k if reverse else k
        rows = pl.ds(pl.multiple_of(g * SCAN_SEG, SCAN_SEG), SCAN_SEG)
        xr, xi = xr_ref[rows, :], xi_ref[rows, :]
        for amount, ar, ai in shifts:
            qr, qi = pltpu.roll(xr, amount, 0), pltpu.roll(xi, amount, 0)
            xr, xi = xr + ar * qr - ai * qi, xi + ar * qi + ai * qr
        xr, xi = xr + pr * cr - pi * ci, xi + pr * ci + pi * cr
        xr_ref[rows, :] = xr
        xi_ref[rows, :] = xi
        return xr[out_row:out_row + 1], xi[out_row:out_row + 1]

    cr, ci = lax.fori_loop(0, groups, step, (sr[...], si[...]), unroll=2)
    sr[...] = cr
    si[...] = ci


def _ssm_fwd(z, bbd, pwr, pwi, cbd, dsk, w1, w2):
    T = z.shape[0]
    tt = min(T, 2 * SCAN_TILE)

    def body(u_ref, bbd_ref, pwr_ref, pwi_ref, cbd_ref, d_ref, w1_ref, w2_ref, yc_ref, y_ref, xr_ref, xi_ref, sr, si):
        @pl.when(pl.program_id(0) == 0)
        def _():
            sr[...] = jnp.zeros_like(sr)
            si[...] = jnp.zeros_like(si)

        u = u_ref[...]
        bu = _mm(u, bbd_ref[...])
        xr_ref[...] = bu[:, 0:N_STATE]
        xi_ref[...] = bu[:, N_STATE:2 * N_STATE]
        _tile_scan(xr_ref, xi_ref, pwr_ref, pwi_ref, sr, si, 1.0, False)
        x = jnp.concatenate([xr_ref[...], xi_ref[...]], axis=1)
        y = _mm_nt(x, cbd_ref[...]) + d_ref[...] * u
        y_ref[...] = y
        y2 = _gelu(y)
        yc_ref[...] = _mm(y2, w1_ref[...]) * _sigmoid(_mm(y2, w2_ref[...]))

    big = pl.BlockSpec((C_WIDTH, 2 * N_STATE), lambda i: (0, 0))
    tab = pl.BlockSpec((SCAN_SEG, N_STATE), lambda i: (0, 0))
    wsp = pl.BlockSpec((C_WIDTH, C_WIDTH), lambda i: (0, 0))
    xs = pl.BlockSpec((tt, N_STATE), lambda i: (i, 0))
    return pl.pallas_call(
        body, name="ssm_fwd", grid=(T // tt,),
        in_specs=[pl.BlockSpec((tt, C_WIDTH), lambda i: (i, C_BLOCK)), big, tab, tab, big, pl.BlockSpec((1, C_WIDTH), lambda i: (0, 0)), wsp, wsp],
        out_specs=[pl.BlockSpec((tt, C_WIDTH), lambda i: (i, 0)), pl.BlockSpec((tt, C_WIDTH), lambda i: (i, 0)), xs, xs],
        out_shape=[_sds((T, C_WIDTH)), _sds((T, C_WIDTH)), _sds((T, N_STATE)), _sds((T, N_STATE))],
        scratch_shapes=[pltpu.VMEM((1, N_STATE), F32)] * 2,
        compiler_params=_cp("arbitrary"))(z, bbd, pwr, pwi, cbd, dsk, w1, w2)


def _ssm_bwd(dyc, z, y, xr, xi, bbd, pwr, pwi, cbd, dsk, w1, w2):
    T = z.shape[0]
    tt = min(T, SCAN_TILE)
    nt = T // tt

    def tile_of(i):
        return nt - 1 - i

    def body(dyc_ref, u_ref, y_ref, xr_ref, xi_ref, xpr_ref, xpi_ref, bbd_ref, pwr_ref, pwi_ref, cbd_ref, d_ref, w1_ref, w2_ref,
             du_ref, dw1_ref, dw2_ref, dcr_ref, dci_ref, dbbd_ref, dlr_ref, dli_ref, dd_ref, gr, gi, sr, si):
        i = pl.program_id(0)
        ti = nt - 1 - i

        @pl.when(i == 0)
        def _():
            sr[...] = jnp.zeros_like(sr)
            si[...] = jnp.zeros_like(si)
            for acc in (dw1_ref, dw2_ref, dcr_ref, dci_ref, dbbd_ref, dlr_ref, dli_ref, dd_ref):
                acc[...] = jnp.zeros_like(acc)

        u = u_ref[...]
        xr_t, xi_t = xr_ref[...], xi_ref[...]
        y = y_ref[...]
        y2 = _gelu(y)
        a1 = _mm(y2, w1_ref[...])
        sg = _sigmoid(_mm(y2, w2_ref[...]))
        dyc_t = dyc_ref[...]
        da1 = dyc_t * sg
        da2 = dyc_t * a1 * sg * (1.0 - sg)
        dy = (_mm_nt(da1, w1_ref[...]) + _mm_nt(da2, w2_ref[...])) * _gelu_grad(y)
        gx = _mm(dy, cbd_ref[...])
        gr[...] = gx[:, 0:N_STATE]
        gi[...] = gx[:, N_STATE:2 * N_STATE]
        _tile_scan(gr, gi, pwr_ref, pwi_ref, sr, si, -1.0, True)
        ar, ai = gr[...], gi[...]
        first_row = _row(ar.shape) == 0
        live = jnp.where(ti > 0, 1.0, 0.0)
        xsr = jnp.where(first_row, xpr_ref[7:8, :] * live, pltpu.roll(xr_t, 1, 0))
        xsi = jnp.where(first_row, xpi_ref[7:8, :] * live, pltpu.roll(xi_t, 1, 0))
        dlr_ref[...] += jnp.sum(ar * xsr + ai * xsi, axis=0, keepdims=True)
        dli_ref[...] += jnp.sum(ai * xsr - ar * xsi, axis=0, keepdims=True)
        dd_ref[...] += jnp.sum(dy * u, axis=0, keepdims=True)
        arai = jnp.concatenate([ar, ai], axis=1)
        du_ref[...] = _mm_nt(arai, bbd_ref[...]) + d_ref[...] * dy
        dw1_ref[...] += _mm_tn(y2, da1)
        dw2_ref[...] += _mm_tn(y2, da2)
        dcr_ref[...] += _mm_tn(dy, xr_t)
        dci_ref[...] += _mm_tn(dy, xi_t)
        dbbd_ref[...] += _mm_tn(u, arai)

    def prev(i):
        return jnp.maximum(tile_of(i) * (tt // 8) - 1, 0)

    big = pl.BlockSpec((C_WIDTH, 2 * N_STATE), lambda i: (0, 0))
    tab = pl.BlockSpec((SCAN_SEG, N_STATE), lambda i: (0, 0))
    srow = pl.BlockSpec((1, N_STATE), lambda i: (0, 0))
    wsp = pl.BlockSpec((C_WIDTH, C_WIDTH), lambda i: (0, 0))
    xs = pl.BlockSpec((tt, N_STATE), lambda i: (tile_of(i), 0))
    xp = pl.BlockSpec((8, N_STATE), lambda i: (prev(i), 0))
    cw = pl.BlockSpec((tt, C_WIDTH), lambda i: (tile_of(i), 0))
    drow = pl.BlockSpec((1, C_WIDTH), lambda i: (0, 0))
    return pl.pallas_call(
        body, name="ssm_bwd", grid=(nt,),
        in_specs=[cw, pl.BlockSpec((tt, C_WIDTH), lambda i: (tile_of(i), C_BLOCK)), cw, xs, xs, xp, xp, big, tab, tab, big, drow, wsp, wsp],
        out_specs=[cw, wsp, wsp, pl.BlockSpec((C_WIDTH, N_STATE), lambda i: (0, 0)), pl.BlockSpec((C_WIDTH, N_STATE), lambda i: (0, 0)), big,
                   srow, srow, drow],
        out_shape=[_sds((T, C_WIDTH)), _sds((C_WIDTH, C_WIDTH)), _sds((C_WIDTH, C_WIDTH)), _sds((C_WIDTH, N_STATE)), _sds((C_WIDTH, N_STATE)),
                   _sds((C_WIDTH, 2 * N_STATE)), _sds((1, N_STATE)), _sds((1, N_STATE)), _sds((1, C_WIDTH))],
        scratch_shapes=[pltpu.VMEM((tt, N_STATE), F32)] * 2 + [pltpu.VMEM((1, N_STATE), F32)] * 2,
        compiler_params=_cp("arbitrary"))(dyc, z, y, xr, xi, xr, xi, bbd, pwr, pwi, cbd, dsk, w1, w2)


_GROUPS = ((0, A_WIDTH), (A_WIDTH, A_WIDTH + B_WIDTH), (A_WIDTH + B_WIDTH, D_MODEL))


def _merge_fwd(h, ya, yb, yc, g, w):
    T = h.shape[0]
    tm = min(T, TOKEN_TILE)

    def body(h_ref, a_ref, b_ref, c_ref, g_ref, w_ref, o_ref):
        yn = jnp.concatenate([y * _rms_stat(y) for y in (a_ref[...], b_ref[...], c_ref[...])], axis=1) * g_ref[...]
        o_ref[...] = h_ref[...] + _mm(yn, w_ref[...])

    def rows(w_):
        return pl.BlockSpec((tm, w_), lambda i: (i, 0))

    return pl.pallas_call(
        body, name="merge_fwd", grid=(T // tm,),
        in_specs=[rows(D_MODEL), rows(A_WIDTH), rows(B_WIDTH), rows(C_WIDTH), pl.BlockSpec((1, D_MODEL), lambda i: (0, 0)),
                  pl.BlockSpec((D_MODEL, D_MODEL), lambda i: (0, 0))],
        out_specs=rows(D_MODEL), out_shape=_sds((T, D_MODEL)), compiler_params=_cp("parallel"))(h, ya, yb, yc, g, w)


def _merge_bwd(gres, ya, yb, yc, g, w):
    T = gres.shape[0]
    tm = min(T, TOKEN_TILE)
    nt = T // tm

    def body(gr_ref, a_ref, b_ref, c_ref, g_ref, w_ref, da_ref, db_ref, dc_ref, dw_ref, dg_ref, acc):
        i = pl.program_id(0)

        @pl.when(i == 0)
        def _():
            dg_ref[...] = jnp.zeros_like(dg_ref)
            acc[...] = jnp.zeros_like(acc)

        gr = gr_ref[...]
        dyn = _mm_nt(gr, w_ref[...])
        yns, dgs = [], []
        for (c0, c1), y_ref, d_ref in zip(_GROUPS, (a_ref, b_ref, c_ref), (da_ref, db_ref, dc_ref)):
            y = y_ref[...]
            r = _rms_stat(y)
            gg = g_ref[:, c0:c1]
            dx, dg = _rms_bwd(y, r, gg, dyn[:, c0:c1])
            d_ref[...] = dx
            dgs.append(dg)
            yns.append(y * r * gg)
        dg_ref[...] += jnp.concatenate(dgs, axis=1)
        acc[...] += _mm_tn(jnp.concatenate(yns, axis=1), gr)

        @pl.when(i == nt - 1)
        def _():
            dw_ref[...] = acc[...].astype(WIRE)

    def rows(w_):
        return pl.BlockSpec((tm, w_), lambda i: (i, 0))

    row = pl.BlockSpec((1, D_MODEL), lambda i: (0, 0))
    full = pl.BlockSpec((D_MODEL, D_MODEL), lambda i: (0, 0))
    return pl.pallas_call(
        body, name="merge_bwd", grid=(nt,),
        in_specs=[rows(D_MODEL), rows(A_WIDTH), rows(B_WIDTH), rows(C_WIDTH), row, full],
        out_specs=[rows(A_WIDTH), rows(B_WIDTH), rows(C_WIDTH), full, row],
        out_shape=[_sds((T, A_WIDTH)), _sds((T, B_WIDTH)), _sds((T, C_WIDTH)), _sds((D_MODEL, D_MODEL), WIRE), _sds((1, D_MODEL))],
        scratch_shapes=[pltpu.VMEM((D_MODEL, D_MODEL), F32)],
        compiler_params=_cp("arbitrary"))(gres, ya, yb, yc, g, w)


FF_BLOCK = D_FF // N_DEV


def _load_weights(w1_hbm, w2_hbm, w1, w2, sem):
    @pl.when(pl.program_id(0) == 0)
    def _():
        copies = [pltpu.make_async_copy(w1_hbm.at[j], w1.at[:, pl.ds(j * FF_BLOCK, FF_BLOCK)], sem.at[j]) for j in range(N_DEV)]
        copies.append(pltpu.make_async_copy(w2_hbm, w2, sem.at[N_DEV]))
        for cp in copies:
            cp.start()
        for cp in copies:
            cp.wait()


def _mlp_weight_scratch():
    return [pltpu.VMEM((D_MODEL, D_FF), MXU), pltpu.VMEM((D_FF, D_MODEL), MXU), pltpu.SemaphoreType.DMA((N_DEV + 1,))]


def _mlp_fwd(h, g, w1, w2):
    T = h.shape[0]
    tm = min(T, TOKEN_TILE)

    def body(h_ref, g_ref, w1_hbm, w2_hbm, o_ref, r_ref, w1_v, w2_v, sem):
        _load_weights(w1_hbm, w2_hbm, w1_v, w2_v, sem)
        x = h_ref[...]
        a = jnp.maximum(_mm(x * _rms_stat(x) * g_ref[...], w1_v[...]), 0.0)
        r = (a * a).astype(MXU)
        r_ref[...] = r
        o_ref[...] = x + _mm(r, w2_v[...])

    rows = pl.BlockSpec((tm, D_MODEL), lambda i: (i, 0))
    hbm = pl.BlockSpec(memory_space=pl.ANY)
    return pl.pallas_call(
        body, name="mlp_fwd", grid=(T // tm,),
        in_specs=[rows, pl.BlockSpec((1, D_MODEL), lambda i: (0, 0)), hbm, hbm],
        out_specs=[rows, pl.BlockSpec((tm, D_FF), lambda i: (i, 0))],
        out_shape=[_sds((T, D_MODEL)), _sds((T, D_FF), MXU)],
        scratch_shapes=_mlp_weight_scratch(), compiler_params=_cp("arbitrary"))(h, g, w1, w2)


def _mlp_bwd(gres, h, g, r, w1, w2):
    T = h.shape[0]
    tm = min(T, 256)

    def body(gr_ref, h_ref, g_ref, r_ref, w1_hbm, w2_hbm, dh_ref, hn_ref, da_ref, dg_ref, w1_v, w2_v, sem):
        _load_weights(w1_hbm, w2_hbm, w1_v, w2_v, sem)

        @pl.when(pl.program_id(0) == 0)
        def _():
            dg_ref[...] = jnp.zeros_like(dg_ref)

        gr = gr_ref[...]
        da = (_mm_nt(gr, w2_v[...]) * (2.0 * jnp.sqrt(r_ref[...].astype(F32)))).astype(MXU)
        da_ref[...] = da
        x = h_ref[...]
        rs = _rms_stat(x)
        gg = g_ref[...]
        dx, dg = _rms_bwd(x, rs, gg, _mm_nt(da, w1_v[...]))
        dh_ref[...] = gr + dx
        dg_ref[...] += dg
        hn_ref[...] = (x * rs * gg).astype(MXU)

    rows = pl.BlockSpec((tm, D_MODEL), lambda i: (i, 0))
    row = pl.BlockSpec((1, D_MODEL), lambda i: (0, 0))
    wide = pl.BlockSpec((tm, D_FF), lambda i: (i, 0))
    hbm = pl.BlockSpec(memory_space=pl.ANY)
    return pl.pallas_call(
        body, name="mlp_bwd", grid=(T // tm,),
        in_specs=[rows, rows, row, wide, hbm, hbm], out_specs=[rows, rows, wide, row],
        out_shape=[_sds((T, D_MODEL)), _sds((T, D_MODEL), MXU), _sds((T, D_FF), MXU), _sds((1, D_MODEL))],
        scratch_shapes=_mlp_weight_scratch(), compiler_params=_cp("arbitrary"))(gres, h, g, r, w1, w2)


def _ple_fwd(h, p, l, g, wg, wp):
    T = h.shape[0]
    tm = min(T, TOKEN_TILE)

    def body(h_ref, p_ref, g_ref, wg_ref, wp_ref, o_ref, gp_ref, e_ref):
        x = h_ref[...]
        gp = _mm(x * _rms_stat(x) * g_ref[...], wg_ref[...])
        e = _mm(p_ref[...], wp_ref[...])
        o_ref[...] = x + _sigmoid(gp) * e
        gp_ref[...] = gp.astype(MXU)
        e_ref[...] = e.astype(MXU)

    rows = pl.BlockSpec((tm, D_MODEL), lambda i: (i, 0))
    return pl.pallas_call(
        body, name="ple_fwd", grid=(T // tm,),
        in_specs=[rows, pl.BlockSpec((None, tm, PLE_DIM), lambda i: (l, i, 0)), pl.BlockSpec((1, D_MODEL), lambda i: (0, 0)),
                  pl.BlockSpec((D_MODEL, D_MODEL), lambda i: (0, 0)), pl.BlockSpec((PLE_DIM, D_MODEL), lambda i: (0, 0))],
        out_specs=[rows, rows, rows], out_shape=[_sds((T, D_MODEL)), _sds((T, D_MODEL), MXU), _sds((T, D_MODEL), MXU)],
        compiler_params=_cp("parallel"))(h, p, g, wg, wp)


def _ple_bwd(gres, h, p, l, g, gp, e, wg):
    T = h.shape[0]
    tm = min(T, TOKEN_TILE)
    nt = T // tm
    shard = D_MODEL // N_DEV

    def body(gr_ref, h_ref, p_ref, g_ref, gp_ref, e_ref, wg_ref, dh_ref, dhm_ref, dwg_ref, dwp_ref, dg_ref, accg, accp):
        i = pl.program_id(0)

        @pl.when(i == 0)
        def _():
            dg_ref[...] = jnp.zeros_like(dg_ref)
            accg[...] = jnp.zeros_like(accg)
            accp[...] = jnp.zeros_like(accp)

        x = h_ref[...]
        r = _rms_stat(x)
        gg = g_ref[...]
        hn = x * r * gg
        gate = _sigmoid(gp_ref[...].astype(F32))
        pe = p_ref[...]
        e = e_ref[...].astype(F32)
        gr = gr_ref[...]
        dgp = gr * e * gate * (1.0 - gate)
        dx, dg = _rms_bwd(x, r, gg, _mm_nt(dgp, wg_ref[...]))
        dh = gr + dx
        dh_ref[...] = dh
        dhm_ref[...] = dh.astype(MXU)
        dg_ref[...] += dg
        accg[...] += _mm_tn(hn, dgp)
        accp[...] += _mm_tn(pe, gr * gate)

        @pl.when(i == nt - 1)
        def _():
            dwg_ref[...] = accg[...].astype(WIRE)
            for d in range(N_DEV):
                dwp_ref[d] = accp[:, d * shard:(d + 1) * shard].astype(WIRE)

    rows = pl.BlockSpec((tm, D_MODEL), lambda i: (i, 0))
    row = pl.BlockSpec((1, D_MODEL), lambda i: (0, 0))
    full = pl.BlockSpec((D_MODEL, D_MODEL), lambda i: (0, 0))
    return pl.pallas_call(
        body, name="ple_bwd", grid=(nt,),
        in_specs=[rows, rows, pl.BlockSpec((None, tm, PLE_DIM), lambda i: (l, i, 0)), row, rows, rows, full],
        out_specs=[rows, rows, full, pl.BlockSpec((N_DEV, PLE_DIM, shard), lambda i: (0, 0, 0)), row],
        out_shape=[_sds((T, D_MODEL)), _sds((T, D_MODEL), MXU), _sds((D_MODEL, D_MODEL), WIRE), _sds((N_DEV, PLE_DIM, shard), WIRE), _sds((1, D_MODEL))],
        scratch_shapes=[pltpu.VMEM((D_MODEL, D_MODEL), F32), pltpu.VMEM((PLE_DIM, D_MODEL), F32)],
        compiler_params=_cp("arbitrary"))(gres, h, p, g, gp, e, wg)


def _loss_head(h, target):
    T = h.shape[0]
    tm = min(T, 1024)

    def body(h_ref, t_ref, dh_ref, l_ref):
        @pl.when(pl.program_id(0) == 0)
        def _():
            l_ref[...] = jnp.zeros_like(l_ref)

        e = h_ref[...] - t_ref[...]
        dh_ref[...] = e * (1.0 / D_MODEL)
        l_ref[...] += jnp.zeros_like(l_ref) + 0.5 * jnp.sum(jnp.mean(e * e, axis=-1, keepdims=True))

    rows = pl.BlockSpec((tm, D_MODEL), lambda i: (i, 0))
    return pl.pallas_call(
        body, name="loss_head", grid=(T // tm,), in_specs=[rows, rows],
        out_specs=[rows, pl.BlockSpec((8, LANES), lambda i: (0, 0))], out_shape=[_sds((T, D_MODEL)), _sds((8, LANES))],
        compiler_params=_cp("arbitrary"))(h, target)


TN_ROWS = 2048

def _tn(a, b, m, n, *, bm, bn, n_major=False, dtype=F32, name="tn"):
    T = a.shape[0]
    tk = min(T, TN_ROWS)
    nk = T // tk
    assert m % bm == 0 and n % bn == 0 and (not n_major or bm == m)

    def body(a_ref, b_ref, o_ref, acc):
        k = pl.program_id(2)

        @pl.when(k == 0)
        def _():
            acc[...] = jnp.zeros_like(acc)

        acc[...] += _mm_tn(a_ref[...], b_ref[...])

        @pl.when(k == nk - 1)
        def _():
            o_ref[...] = acc[...].astype(dtype)

    if n_major:
        out_spec = pl.BlockSpec((None, bm, bn), lambda i, j, k: (j, 0, 0))
        out_shape = _sds((n // bn, m, bn), dtype)
    else:
        out_spec = pl.BlockSpec((bm, bn), lambda i, j, k: (i, j))
        out_shape = _sds((m, n), dtype)
    return pl.pallas_call(
        body, name=name, grid=(m // bm, n // bn, nk),
        in_specs=[pl.BlockSpec((tk, bm), lambda i, j, k: (k, i)), pl.BlockSpec((tk, bn), lambda i, j, k: (k, j))],
        out_specs=out_spec, out_shape=out_shape, scratch_shapes=[pltpu.VMEM((bm, bn), F32)],
        compiler_params=_cp("parallel", "parallel", "arbitrary"))(a, b)


def _row_tile(R, C):
    for cand in (512, 256, 128, 64, 32, 16, 8):
        if R % cand == 0 and cand * C * 4 <= 2 ** 20:
            return cand
    return R


def _sum_slots(land):
    S, R, C = land.shape
    tr = _row_tile(R, C)

    def body(l_ref, o_ref):
        acc = l_ref[0].astype(F32)
        for s in range(1, S):
            acc = acc + l_ref[s].astype(F32)
        o_ref[...] = acc

    return pl.pallas_call(
        body, name="sum_slots", grid=(R // tr,), in_specs=[pl.BlockSpec((S, tr, C), lambda i: (0, i, 0))],
        out_specs=pl.BlockSpec((tr, C), lambda i: (i, 0)), out_shape=_sds((R, C)), compiler_params=_cp("parallel"))(land)


def _sum_adamw(land, w, m, v):
    R, C = w.shape
    S = land.shape[0]
    tr = _row_tile(R, C)

    def body(l_ref, w_ref, m_ref, v_ref, g_ref, d_ref, nm_ref, nv_ref):
        gg = l_ref[0].astype(F32)
        for s in range(1, S):
            gg = gg + l_ref[s].astype(F32)
        g_ref[...] = gg
        nm = ADAM_B1 * m_ref[...] + (1.0 - ADAM_B1) * gg
        nv = ADAM_B2 * v_ref[...] + (1.0 - ADAM_B2) * (gg * gg)
        m_hat = nm / (1.0 - ADAM_B1 ** ADAM_STEP)
        v_hat = nv / (1.0 - ADAM_B2 ** ADAM_STEP)
        d_ref[...] = -ADAM_LR * (m_hat / (jnp.sqrt(v_hat) + ADAM_EPS) + ADAM_WD * w_ref[...])
        nm_ref[...] = nm
        nv_ref[...] = nv

    blk = pl.BlockSpec((tr, C), lambda i: (i, 0))
    return pl.pallas_call(
        body, name="sum_adamw", grid=(R // tr,), in_specs=[pl.BlockSpec((S, tr, C), lambda i: (0, i, 0))] + [blk] * 3,
        out_specs=[blk] * 4, out_shape=[_sds((R, C))] * 4, compiler_params=_cp("parallel"))(land, w, m, v)


_HBM = pl.BlockSpec(memory_space=pltpu.HBM)
_SEM = pl.BlockSpec(memory_space=pltpu.SEMAPHORE)
_EFFECT = pltpu.SideEffectType.DATAFLOW_SIDE_EFFECTING


def _peers():
    x, y, c = lax.axis_index("x"), lax.axis_index("y"), lax.axis_index("c")
    out = []
    for k in range(1, N_DEV):
        px, py, pc = x ^ (k >> 2), y ^ ((k >> 1) & 1), c ^ (k & 1)
        out.append((k, (px, py, pc), 4 * px + 2 * py + pc))
    return 4 * x + 2 * y + c, out


def _route(mode, layer):
    if mode == "all":
        return (lambda ref, peer: ref), (lambda ref, sender: ref.at[sender])
    return (lambda ref, peer: ref.at[peer]), (lambda ref, sender: ref.at[sender, layer])


def _split_start(srcs, lands, modes, layer, after, name):
    n = len(srcs)
    routes = [_route(m, layer) for m in modes]

    def body(*refs):
        src, land = refs[:n], refs[n:2 * n]
        send, recv, token = refs[2 * n + 1], refs[2 * n + 2], refs[-1]
        me, peers = _peers()
        for k, dev, peer in peers:
            for t, (src_of, dst_of) in enumerate(routes):
                pltpu.make_async_remote_copy(src_ref=src_of(src[t], peer), dst_ref=dst_of(land[t], me), send_sem=send.at[t * N_DEV + k],
                                             recv_sem=recv.at[t * N_DEV + k], device_id=dev, device_id_type=MESH).start()
        token[...] = jnp.zeros_like(token)

    bufs = list(srcs) + list(lands)
    outs = pl.pallas_call(
        body, name=name,
        out_shape=(pltpu.SemaphoreType.DMA((n * N_DEV,)), pltpu.SemaphoreType.DMA((n * N_DEV,)),
                   *[pltpu.HBM(a.shape, a.dtype) for a in bufs], _sds((8, LANES))),
        in_specs=[_HBM] * (2 * n) + [pl.BlockSpec(memory_space=pl.ANY)],
        out_specs=(_SEM, _SEM, *[_HBM] * (2 * n), pl.BlockSpec(memory_space=pltpu.VMEM)),
        input_output_aliases={i: 2 + i for i in range(2 * n)},
        compiler_params=pltpu.CompilerParams(has_side_effects=_EFFECT),
    )(*[pltpu.with_memory_space_constraint(a, pltpu.HBM) for a in bufs], after)
    return outs[0], outs[1], list(outs[2:2 + n]), list(outs[2 + n:2 + 2 * n]), outs[-1]


def _split_wait(send, recv, srcs, lands, modes, layer, after, name):
    n = len(srcs)
    routes = [_route(m, layer) for m in modes]

    def body(*refs):
        src, land = refs[:n], refs[n:2 * n]
        send_r, recv_r = refs[2 * n], refs[2 * n + 1]
        _, peers = _peers()
        for k, dev, peer in peers:
            for t, (src_of, dst_of) in enumerate(routes):
                cp = pltpu.make_async_remote_copy(src_ref=src_of(src[t], peer), dst_ref=dst_of(land[t], peer), send_sem=send_r.at[t * N_DEV + k],
                                                  recv_sem=recv_r.at[t * N_DEV + k], device_id=dev, device_id_type=MESH)
                cp.wait_send()
                cp.wait_recv()

    bufs = list(srcs) + list(lands)
    outs = pl.pallas_call(
        body, name=name, out_shape=tuple(pltpu.HBM(a.shape, a.dtype) for a in bufs),
        in_specs=[_HBM] * (2 * n) + [_SEM, _SEM, pl.BlockSpec(memory_space=pl.ANY)], out_specs=[_HBM] * (2 * n),
        input_output_aliases={i: i for i in range(2 * n)},
        compiler_params=pltpu.CompilerParams(has_side_effects=_EFFECT),
    )(*bufs, send, recv, after)
    return list(outs[:n]), list(outs[n:])


SHARDED = ("w_in", "glu_w1", "glu_w2", "w_out", "w_ff1", "w_ff2", "w_ple_gate", "w_ple_proj")
SMALL = ("attn_norm_g", "gmlp_ln_g", "gmlp_ln_b", "gmlp_ws", "gmlp_bs", "q_norm_g", "k_norm_g", "sinks", "ssm_a_re", "ssm_a_im",
         "ssm_log_dt", "ssm_b_re", "ssm_b_im", "ssm_c_re", "ssm_c_im", "ssm_d", "mix_out_g", "mlp_norm_g", "ple_norm_g")
WEIGHTS = ("attn_norm_g", "w_in", "gmlp_ln_g", "gmlp_ln_b", "gmlp_ws", "gmlp_bs", "q_norm_g", "k_norm_g", "sinks", "ssm_a_re", "ssm_a_im",
           "ssm_log_dt", "ssm_b_re", "ssm_b_im", "ssm_c_re", "ssm_c_im", "ssm_d", "glu_w1", "glu_w2", "mix_out_g", "w_out", "mlp_norm_g",
           "w_ff1", "w_ff2", "ple_norm_g", "w_ple_gate", "w_ple_proj")
FLAT_COLS = 1024


PACK_TILE_ROWS = 8
PACK_ROWS_MULTIPLE = PACK_TILE_ROWS * N_DEV


def _packed_rows(shape):
    return -(-math.prod(shape) // (PACK_TILE_ROWS * FLAT_COLS)) * PACK_TILE_ROWS


def _pack(arrs, dtype):
    blocks = []
    for a in arrs:
        flat = a.astype(dtype).reshape(-1)
        pad = _packed_rows(a.shape) * FLAT_COLS - flat.shape[0]
        if pad:
            flat = jnp.concatenate([flat, jnp.zeros((pad,), dtype)])
        blocks.append(flat.reshape(-1, FLAT_COLS))
    rows = sum(b.shape[0] for b in blocks)
    if rows % PACK_ROWS_MULTIPLE:
        blocks.append(jnp.zeros((PACK_ROWS_MULTIPLE - rows % PACK_ROWS_MULTIPLE, FLAT_COLS), dtype))
    return jnp.concatenate(blocks, axis=0)


def _unpack(flat, shapes):
    out, r = [], 0
    for s in shapes:
        nr = _packed_rows(s)
        out.append(flat[r:r + nr].reshape(-1)[:math.prod(s)].reshape(s))
        r += nr
    return out


def _from_col_major(s):
    n, rows, cs = s.shape
    return s.transpose(1, 0, 2).reshape(rows, n * cs)


EARLY = ("w_in", "glu_w1", "glu_w2")
LATE = ("w_out", "w_ff1", "w_ff2", "w_ple_gate", "w_ple_proj")
GRADS_MID = ("w_ff1", "w_ff2", "w_ple_gate", "w_ple_proj")
GRADS_END = ("w_in", "glu_w1", "glu_w2", "w_out")


def _layer_fwd(h, p, l, cs, sn, W, late_weights, S, sp):
    z = _inproj_fwd(h, S["attn_norm_g"], W["w_in"])
    ya = _gmlp_fwd(z, S["lng"], S["lnb"], S["gmlp_ws"], S["bsx"])
    yb = _attn_fwd(z, cs, sn, S["qg"], S["kg"], S["sinks"])
    yc, y, xr, xi = _ssm_fwd(z, sp["bbd"], sp["pwr"], sp["pwi"], sp["cbd"], S["ssm_d"], W["glu_w1"], W["glu_w2"])
    late, token = late_weights(ya[0:8, 0:LANES] + yb[0:8, 0:LANES] + yc[0:8, 0:LANES])
    W = {**W, **late}
    h1 = _merge_fwd(h, ya, yb, yc, _behind(S["mix_out_g"], token), W["w_out"])
    h2, r = _mlp_fwd(h1, S["mlp_norm_g"], W["w_ff1"], W["w_ff2"])
    h3, gp, e = _ple_fwd(h2, p, l, S["ple_norm_g"], W["w_ple_gate"], W["w_ple_proj"])
    return h3, dict(h=h, z=z, ya=ya, yb=yb, yc=yc, y=y, xr=xr, xi=xi, h1=h1, r=r, h2=h2, gp=gp, e=e), W


def _layer_bwd(g3, p, l, cs, sn, W, S, sp, A, raw, mid_bwd):
    G = {}
    g2, g2m, dwg, G["w_ple_proj"], G["ple_norm_g"] = _ple_bwd(g3, A["h2"], p, l, S["ple_norm_g"], A["gp"], A["e"], W["w_ple_gate"])
    G["w_ple_gate"] = dwg.reshape(N_DEV, -1, D_MODEL)
    g1, hn, da, G["mlp_norm_g"] = _mlp_bwd(g2, A["h1"], S["mlp_norm_g"], A["r"], W["w_ff1"], W["w_ff2"])
    G["w_ff1"] = _tn(hn, da, D_MODEL, D_FF, bm=D_MODEL, bn=FF_BLOCK, n_major=True, dtype=WIRE, name="tn_ff1")
    G["w_ff2"] = _tn(A["r"], g2m, D_FF, D_MODEL, bm=D_MODEL, bn=D_MODEL, dtype=WIRE, name="tn_ff2").reshape(N_DEV, -1, D_MODEL)
    token = mid_bwd(g1, G)
    dya, dyb, dyc, dwo, G["mix_out_g"] = _merge_bwd(g1, A["ya"], A["yb"], A["yc"], _behind(S["mix_out_g"], token), W["w_out"])
    G["w_out"] = dwo.reshape(N_DEV, -1, D_MODEL)
    dzc, dw1, dw2, dcr, dci, dbbd, dlr, dli, dd = _ssm_bwd(dyc, A["z"], A["y"], A["xr"], A["xi"], sp["bbd"], sp["pwr"], sp["pwi"], sp["cbd"],
                                                          S["ssm_d"], W["glu_w1"], W["glu_w2"])
    G["glu_w1"] = dw1.astype(WIRE).reshape(N_DEV, -1, C_WIDTH)
    G["glu_w2"] = dw2.astype(WIRE).reshape(N_DEV, -1, C_WIDTH)
    dare, daim, dldt, dbtr, dbti, dcre, dcim = _ssm_param_bwd(
        raw["are"], raw["aim"], raw["ldt"], raw["are_x"], raw["aim_x"], raw["ldt_x"], raw["btr"], raw["bti"],
        dlr.reshape(C_GROUPS, C_STATE), dli.reshape(C_GROUPS, C_STATE), dbbd, dcr, dci)
    G["ssm_a_re"], G["ssm_a_im"], G["ssm_log_dt"] = dare, daim, dldt[:, 0]
    G["ssm_b_re"] = dbtr.reshape(C_GROUPS, C_GROUP, C_STATE).transpose(0, 2, 1)
    G["ssm_b_im"] = dbti.reshape(C_GROUPS, C_GROUP, C_STATE).transpose(0, 2, 1)
    G["ssm_c_re"] = dcre.reshape(C_GROUPS, C_GROUP, C_STATE)
    G["ssm_c_im"] = dcim.reshape(C_GROUPS, C_GROUP, C_STATE)
    G["ssm_d"] = dd.reshape(C_GROUPS, C_GROUP)
    dzq, dzk, dzv, dqg, dkg, dsk = _attn_bwd(A["z"], cs, sn, S["qg"], S["kg"], S["sinks"], A["yb"], dyb)
    G["q_norm_g"] = dqg[0, :HEAD_DIM] + dqg[0, HEAD_DIM:]
    G["k_norm_g"] = dkg[0, :HEAD_DIM] + dkg[0, HEAD_DIM:]
    G["sinks"] = dsk[:, 0]
    dza, dws, dbs, dlng, dlnb = _gmlp_bwd(A["z"], dya, S["lng"], S["lnb"], S["gmlp_ws"], S["bsx"])
    G["gmlp_ws"] = dws
    G["gmlp_bs"] = dbs[:, :, 0]
    G["gmlp_ln_g"] = dlng.reshape(A_HEADS, 2, HEAD_DIM)[:, 1]
    G["gmlp_ln_b"] = dlnb.reshape(A_HEADS, 2, HEAD_DIM)[:, 1]
    g0, G["w_in"], G["attn_norm_g"] = _inproj_bwd(g1, A["h"], S["attn_norm_g"], W["w_in"], dza, dzq, dzk, dzv, dzc)
    return g0, G


def _small_layouts(P, l):
    def row(a):
        return a.reshape(1, -1)

    zeros = jnp.zeros((A_HEADS, HEAD_DIM), F32)
    S = dict(
        attn_norm_g=row(P["attn_norm_g"][l]), mix_out_g=row(P["mix_out_g"][l]), mlp_norm_g=row(P["mlp_norm_g"][l]),
        ple_norm_g=row(P["ple_norm_g"][l]),
        lng=jnp.stack([zeros, P["gmlp_ln_g"][l]], axis=1).reshape(1, IN_A),
        lnb=jnp.stack([zeros, P["gmlp_ln_b"][l]], axis=1).reshape(1, IN_A),
        gmlp_ws=P["gmlp_ws"][l],
        bsx=jnp.broadcast_to(P["gmlp_bs"][l][:, :, None], (A_HEADS, CHUNK, CHUNK)),
        qg=jnp.tile(P["q_norm_g"][l], 2).reshape(1, LANES), kg=jnp.tile(P["k_norm_g"][l], 2).reshape(1, LANES),
        sinks=jnp.broadcast_to(P["sinks"][l][:, None], (8, LANES)),
        ssm_d=row(P["ssm_d"][l]),
    )
    are, aim = P["ssm_a_re"][l], P["ssm_a_im"][l]
    ldt = jnp.broadcast_to(P["ssm_log_dt"][l][:, None], (C_GROUPS, C_STATE))
    raw = dict(
        are=are, aim=aim, ldt=ldt,
        are_x=jnp.repeat(are, C_GROUP, axis=0), aim_x=jnp.repeat(aim, C_GROUP, axis=0), ldt_x=jnp.repeat(ldt, C_GROUP, axis=0),
        btr=P["ssm_b_re"][l].transpose(0, 2, 1).reshape(C_WIDTH, C_STATE), bti=P["ssm_b_im"][l].transpose(0, 2, 1).reshape(C_WIDTH, C_STATE),
        cre=P["ssm_c_re"][l].reshape(C_WIDTH, C_STATE), cim=P["ssm_c_im"][l].reshape(C_WIDTH, C_STATE),
    )
    return S, raw


def _ssm_prep_layer(raw):
    bbd, cbd, pwr, pwi = _ssm_prep(raw["are"].reshape(1, N_STATE), raw["aim"].reshape(1, N_STATE), raw["ldt"].reshape(1, N_STATE),
                                   raw["are_x"], raw["aim_x"], raw["ldt_x"], raw["btr"], raw["bti"], raw["cre"], raw["cim"])
    return dict(bbd=bbd, cbd=cbd, pwr=pwr, pwi=pwi)


def _behind(row, token):
    return row if token is None else row + token[0:1, 0:1]


def _local_step(x, p, positions, target, P, weights_of, mid_bwd, after_bwd):
    inv = 1.0 / (ROPE_THETA ** (jnp.arange(0, HEAD_DIM, 2, dtype=F32) / HEAD_DIM))
    cs, sn = _rope_tables(positions.reshape(-1, 1), jnp.tile(inv, 4).reshape(1, LANES))
    h = x
    acts, smalls, weights = [], [], []
    for l in range(DEPTH):
        W, late_weights, token = weights_of(l, h)
        S, raw = _small_layouts(P, l)
        sp = _ssm_prep_layer(raw)
        h, A, W = _layer_fwd(h, p, l, cs, sn, W, late_weights, {**S, "attn_norm_g": _behind(S["attn_norm_g"], token)}, sp)
        acts.append(A)
        smalls.append((S, raw, sp))
        weights.append(W)
    g, lsum = _loss_head(h, target)
    grads = [None] * DEPTH
    token = None
    for l in reversed(range(DEPTH)):
        S, raw, sp = smalls[l]
        g, grads[l] = _layer_bwd(g, p, l, cs, sn, weights[l], {**S, "ple_norm_g": _behind(S["ple_norm_g"], token)}, sp, acts[l], raw,
                                 functools.partial(mid_bwd, l))
        token = after_bwd(l, g, grads[l])
    return lsum[0, 0], g, grads


def _layer_weights(g):
    layout = dict(
        w_in=_from_col_major, glu_w1=lambda a: a.reshape(C_WIDTH, C_WIDTH), glu_w2=lambda a: a.reshape(C_WIDTH, C_WIDTH),
        w_out=lambda a: a.reshape(D_MODEL, D_MODEL), w_ff1=lambda a: a, w_ff2=lambda a: a.reshape(D_FF, D_MODEL),
        w_ple_gate=lambda a: a.reshape(D_MODEL, D_MODEL), w_ple_proj=_from_col_major)
    return {n: layout[n](a) for n, a in g.items()}


def kernel(x, p, positions, attn_norm_g, w_in, gmlp_ln_g, gmlp_ln_b, gmlp_ws, gmlp_bs, q_norm_g, k_norm_g, sinks, ssm_a_re, ssm_a_im, ssm_log_dt, ssm_b_re, ssm_b_im, ssm_c_re, ssm_c_im, ssm_d, glu_w1, glu_w2, mix_out_g, w_out, mlp_norm_g, w_ff1, w_ff2, ple_norm_g, w_ple_gate, w_ple_proj, loss_target, m_attn_norm_g, m_w_in, m_gmlp_ln_g, m_gmlp_ln_b, m_gmlp_ws, m_gmlp_bs, m_q_norm_g, m_k_norm_g, m_sinks, m_ssm_a_re, m_ssm_a_im, m_ssm_log_dt, m_ssm_b_re, m_ssm_b_im, m_ssm_c_re, m_ssm_c_im, m_ssm_d, m_glu_w1, m_glu_w2, m_mix_out_g, m_w_out, m_mlp_norm_g, m_w_ff1, m_w_ff2, m_ple_norm_g, m_w_ple_gate, m_w_ple_proj, v_attn_norm_g, v_w_in, v_gmlp_ln_g, v_gmlp_ln_b, v_gmlp_ws, v_gmlp_bs, v_q_norm_g, v_k_norm_g, v_sinks, v_ssm_a_re, v_ssm_a_im, v_ssm_log_dt, v_ssm_b_re, v_ssm_b_im, v_ssm_c_re, v_ssm_c_im, v_ssm_d, v_glu_w1, v_glu_w2, v_mix_out_g, v_w_out, v_mlp_norm_g, v_w_ff1, v_w_ff2, v_ple_norm_g, v_w_ple_gate, v_w_ple_proj):
    env = dict(locals())
    P = {n: env[n] for n in WEIGHTS}
    M = {n: env["m_" + n] for n in WEIGHTS}
    V = {n: env["v_" + n] for n in WEIGHTS}
    return _step(x, p, positions, loss_target, P, M, V)


def _step(x, p, positions, loss_target, P, M, V):
    small_shapes = [P[n].shape for n in SMALL]
    me = 4 * lax.axis_index("x") + 2 * lax.axis_index("y") + lax.axis_index("c")
    nothing = jnp.zeros((8, LANES), F32)

    def put(land, own, lead):
        return lax.dynamic_update_slice(land, own.reshape((1,) * len(lead) + own.shape), tuple(lead) + (0,) * own.ndim)

    def gather_start(l, names, after, tag):
        shards = [P[n][l].astype(WIRE) for n in names]
        lands = [lax.empty((N_DEV,) + s.shape, WIRE) for s in shards]
        send, recv, shards, lands, token = _split_start(shards, lands, ["all"] * len(names), 0, after, f"gather_start_{l}{tag}")
        return dict(names=names, send=send, recv=recv, shards=shards, lands=lands, token=token, name=f"gather_wait_{l}{tag}")

    def gather_wait(f, after):
        shards, lands = _split_wait(f["send"], f["recv"], f["shards"], f["lands"], ["all"] * len(f["names"]), 0, after, f["name"])
        return dict(zip(f["names"], [put(ld, sh, (me,)) for sh, ld in zip(shards, lands)]))

    first = gather_start(0, EARLY, nothing, "a")
    flying = {0: (first, gather_start(0, LATE, first["token"], "b"))}

    def weights_of(l, h):
        fa, fb = flying.pop(l)
        got = gather_wait(fa, h)
        if fb is None:
            token = None
            if l + 1 < DEPTH:
                flying[l + 1] = (gather_start(l + 1, SHARDED, got["w_in"], ""), None)
                token = flying[l + 1][0]["token"]
            W = _layer_weights(got)
            return {n: W[n] for n in EARLY}, (lambda after: ({n: W[n] for n in LATE}, None)), token

        def late_weights(after):
            late = gather_wait(fb, after)
            flying[l + 1] = (gather_start(l + 1, SHARDED, late["w_out"], ""), None)
            return _layer_weights(late), flying[l + 1][0]["token"]

        return _layer_weights(got), late_weights, fb["token"]

    grad_lands = {n: lax.empty((N_DEV,) + P[n].shape, WIRE) for n in SHARDED}
    sent = []

    def scatter_start(l, names, parts, lands, tag):
        send, recv, parts, lands, token = _split_start(parts, lands, ["own"] * len(parts), l, nothing, f"scatter_start_{l}{tag}")
        sent.append(dict(l=l, names=names, send=send, recv=recv, parts=parts, lands=lands, name=f"scatter_wait_{l}{tag}"))
        return token

    def scatter_wait(after):
        f = sent.pop(0)
        parts, lands = _split_wait(f["send"], f["recv"], f["parts"], f["lands"], ["own"] * len(f["parts"]), f["l"], after, f["name"])
        lands = [put(ld, lax.dynamic_index_in_dim(part, me, 0, keepdims=False), (me, f["l"])) for part, ld in zip(parts, lands)]
        return dict(zip(f["names"], lands))

    def mid_bwd(l, g1, G):
        if l > 0:
            return None
        grad_lands.update(scatter_wait(g1))
        return scatter_start(0, GRADS_MID, [G[n] for n in GRADS_MID], [grad_lands[n] for n in GRADS_MID], "a")

    small_land = []

    def after_bwd(l, g, G):
        if l > 0:
            if sent:
                grad_lands.update(scatter_wait(g))
            return scatter_start(l, SHARDED, [G[n] for n in SHARDED], [grad_lands[n] for n in SHARDED], "")
        sflat = _pack([jnp.stack([grads_of[k][n] for k in range(DEPTH)]) for n in SMALL], F32)
        sparts = sflat.reshape(N_DEV, -1, FLAT_COLS)
        small_land.append(sflat.shape)
        return scatter_start(0, GRADS_END + ("small",), [G[n] for n in GRADS_END] + [sparts],
                             [grad_lands[n] for n in GRADS_END] + [lax.empty((N_DEV, 1) + sparts.shape[1:], F32)], "b")

    grads_of = {}

    def after_bwd_recording(l, g, G):
        grads_of[l] = G
        grads_of["token"] = after_bwd(l, g, G)
        return grads_of["token"]

    lsum, gx, grads = _local_step(x[0], p[:, 0], positions[0], loss_target[0], P, weights_of, mid_bwd, after_bwd_recording)
    G, delta, new_m, new_v = {}, {}, {}, {}

    def update(names):
        for n in names:
            shp = P[n].shape
            res = _sum_adamw(grad_lands[n].reshape(N_DEV, -1, shp[-1]), *(a.reshape(-1, shp[-1]) for a in (P[n], M[n], V[n])))
            G[n], delta[n], new_m[n], new_v[n] = (a.reshape(shp) for a in res)

    grad_lands.update(scatter_wait(grads_of["token"]))
    update(GRADS_MID)
    last = scatter_wait(sum(new_v[n].reshape(-1, LANES)[:8] for n in GRADS_MID))
    small_parts = last.pop("small")
    grad_lands.update(last)
    mine = _sum_slots(small_parts[:, 0])
    send, recv, srcs, lands, _ = _split_start([mine], [lax.empty((N_DEV,) + mine.shape, F32)], ["all"], 0, nothing, "gather_small_start")
    update(GRADS_END)
    srcs, lands = _split_wait(send, recv, srcs, lands, ["all"], 0, sum(new_v[n].reshape(-1, LANES)[:8] for n in GRADS_END), "gather_small_wait")
    small_sum = put(lands[0], srcs[0], (me,)).reshape((1,) + small_land[0])
    res = _sum_adamw(small_sum, _pack([P[n] for n in SMALL], F32), _pack([M[n] for n in SMALL], F32), _pack([V[n] for n in SMALL], F32))
    for dst, flat in zip((G, delta, new_m, new_v), res):
        dst.update(zip(SMALL, _unpack(flat, small_shapes)))
    loss = lax.psum(lsum, ("x", "y", "c"))
    return (loss, gx[None], *[G[n] for n in WEIGHTS], *[delta[n] for n in WEIGHTS], *[new_m[n] for n in WEIGHTS], *[new_v[n] for n in WEIGHTS])
```

```python
import functools
import math

import jax
import jax.numpy as jnp
from jax import lax
from jax.experimental import pallas as pl
from jax.experimental.pallas import tpu as pltpu

F32 = jnp.float32
MXU = jnp.bfloat16
WIRE = jnp.bfloat16

D_MODEL = 1024
DEPTH = 4
HEAD_DIM = 64
A_WIDTH = 256
A_HEADS = 4
CHUNK = 128
B_WIDTH = 512
WINDOW = 128
C_WIDTH = 256
C_GROUP = 16
C_GROUPS = 16
C_STATE = 64
N_STATE = C_GROUPS * C_STATE
IN_A = 2 * A_WIDTH
KV_WIDTH = 2 * HEAD_DIM
IN_COLS = IN_A + B_WIDTH + 2 * KV_WIDTH + C_WIDTH
Q_BLOCK = IN_A // B_WIDTH
K_BLOCK = (IN_A + B_WIDTH) // KV_WIDTH
V_BLOCK = K_BLOCK + 1
C_BLOCK = (IN_A + B_WIDTH + 2 * KV_WIDTH) // C_WIDTH
TOKEN_TILE = 512
D_FF = 4096
PLE_DIM = 256
EPS = 1e-6
ROPE_THETA = 10000.0
SCALE = HEAD_DIM ** -0.5
NEG = -1e30
N_DEV = 8

ADAM_LR = 0.001
ADAM_B1 = 0.9
ADAM_B2 = 0.999
ADAM_EPS = 1e-08
ADAM_WD = 0.01
ADAM_STEP = 10

V7X_VMEM_BYTES = 64 * 2 ** 20
VMEM_LIMIT = V7X_VMEM_BYTES - 8 * 2 ** 20
LANES = 128

MESH = pl.DeviceIdType.MESH


def _cp(*sem):
    return pltpu.CompilerParams(dimension_semantics=sem, vmem_limit_bytes=VMEM_LIMIT)


def _sds(shape, dtype=F32):
    return jax.ShapeDtypeStruct(shape, dtype)


def _mm(a, b):
    return jnp.dot(a.astype(MXU), b.astype(MXU), preferred_element_type=F32)


def _mm_nt(a, b):
    return lax.dot_general(a.astype(MXU), b.astype(MXU), (((1,), (1,)), ((), ())), preferred_element_type=F32)


def _mm_tn(a, b):
    return lax.dot_general(a.astype(MXU), b.astype(MXU), (((0,), (0,)), ((), ())), preferred_element_type=F32)


def _lane(shape):
    return lax.broadcasted_iota(jnp.int32, shape, len(shape) - 1)


def _row(shape):
    return lax.broadcasted_iota(jnp.int32, shape, 0)


_GELU_C = math.sqrt(2.0 / math.pi)


def _gelu(x):
    return 0.5 * x * (1.0 + jnp.tanh(_GELU_C * (x + 0.044715 * (x * x * x))))


def _gelu_grad(x):
    t = jnp.tanh(_GELU_C * (x + 0.044715 * (x * x * x)))
    return 0.5 * (1.0 + t) + 0.5 * x * (1.0 - t * t) * (_GELU_C * (1.0 + 3.0 * 0.044715 * (x * x)))


def _sigmoid(x):
    return 1.0 / (1.0 + jnp.exp(-x))


def _rms_stat(x):
    return lax.rsqrt(jnp.mean(x * x, axis=-1, keepdims=True) + EPS)


def _rms_bwd(x, r, g, dy):
    xh = x * r
    dxh = dy * g
    dx = r * (dxh - xh * jnp.mean(dxh * xh, axis=-1, keepdims=True))
    return dx, jnp.sum(dy * xh, axis=0, keepdims=True)


def _tril(w):
    return jnp.where(_row(w.shape) >= _lane(w.shape), w, 0.0)


def _swap64(x):
    return pltpu.roll(x, HEAD_DIM, 1)


def _group_sum64(x, lo):
    s_lo = jnp.sum(jnp.where(lo, x, 0.0), axis=-1, keepdims=True)
    s_hi = jnp.sum(jnp.where(lo, 0.0, x), axis=-1, keepdims=True)
    return jnp.where(lo, s_lo, s_hi)


def _partner(x):
    n = x.shape[-1]
    first = (_lane(x.shape) % HEAD_DIM) < HEAD_DIM // 2
    return jnp.where(first, pltpu.roll(x, n - HEAD_DIM // 2, 1), pltpu.roll(x, HEAD_DIM // 2, 1))


def _rope(y, cs, sn):
    return y * cs + _partner(y) * sn


def _rope_bwd(d, cs, sn):
    return d * cs + _partner(d * sn)


def _qk_norm_rope(x, g, cs, sn):
    lo = _lane(x.shape) < HEAD_DIM
    r = lax.rsqrt(_group_sum64(x * x, lo) * (1.0 / HEAD_DIM) + EPS)
    xh = x * r
    return _rope(xh * g, cs, sn), xh, r


def _qk_norm_rope_bwd(xh, r, g, cs, sn, d):
    lo = _lane(xh.shape) < HEAD_DIM
    dy = _rope_bwd(d, cs, sn)
    dxh = dy * g
    m = _group_sum64(dxh * xh, lo) * (1.0 / HEAD_DIM)
    return r * (dxh - xh * m), jnp.sum(dy * xh, axis=0, keepdims=True)


def _gmlp_head(blk, g, b):
    hi = _lane(blk.shape) >= HEAD_DIM
    mu = jnp.sum(jnp.where(hi, blk, 0.0), axis=-1, keepdims=True) * (1.0 / HEAD_DIM)
    xc = jnp.where(hi, blk - mu, 0.0)
    rstd = lax.rsqrt(jnp.sum(xc * xc, axis=-1, keepdims=True) * (1.0 / HEAD_DIM) + EPS)
    vhat = xc * rstd
    return vhat * g + b, vhat, rstd


def _rope_tables(pos_col, inv_row):
    T = pos_col.shape[0]
    tm = min(T, 1024)

    def body(p_ref, inv_ref, cs_ref, sn_ref):
        ang = p_ref[...].astype(F32) * inv_ref[...]
        s = jnp.sin(ang)
        cs_ref[...] = jnp.cos(ang)
        sn_ref[...] = jnp.where((_lane(ang.shape) % HEAD_DIM) < HEAD_DIM // 2, -s, s)

    blk = pl.BlockSpec((tm, LANES), lambda i: (i, 0))
    return pl.pallas_call(
        body, name="rope_tables", grid=(T // tm,),
        in_specs=[pl.BlockSpec((tm, 1), lambda i: (i, 0)), pl.BlockSpec((1, LANES), lambda i: (0, 0))],
        out_specs=[blk, blk], out_shape=[_sds((T, LANES))] * 2, compiler_params=_cp("parallel"))(pos_col, inv_row)


def _inproj_fwd(h, g, w):
    T = h.shape[0]
    tm = min(T, TOKEN_TILE)

    def body(h_ref, g_ref, w_ref, z_ref):
        x = h_ref[...]
        z_ref[...] = _mm(x * _rms_stat(x) * g_ref[...], w_ref[...])

    return pl.pallas_call(
        body, name="inproj_fwd", grid=(T // tm,),
        in_specs=[pl.BlockSpec((tm, D_MODEL), lambda i: (i, 0)), pl.BlockSpec((1, D_MODEL), lambda i: (0, 0)),
                  pl.BlockSpec((D_MODEL, IN_COLS), lambda i: (0, 0))],
        out_specs=pl.BlockSpec((tm, IN_COLS), lambda i: (i, 0)), out_shape=_sds((T, IN_COLS)),
        compiler_params=_cp("parallel"))(h, g, w)


def _inproj_bwd(gres, h, g, w, dza, dzq, dzk, dzv, dzc):
    T = h.shape[0]
    tm = min(T, TOKEN_TILE)
    nt = T // tm
    shard = IN_COLS // N_DEV

    def body(gr_ref, h_ref, g_ref, w_ref, a_ref, q_ref, k_ref, v_ref, c_ref, dh_ref, dw_ref, dg_ref, acc):
        i = pl.program_id(0)

        @pl.when(i == 0)
        def _():
            dg_ref[...] = jnp.zeros_like(dg_ref)
            acc[...] = jnp.zeros_like(acc)

        x = h_ref[...]
        r = _rms_stat(x)
        gg = g_ref[...]
        dz = jnp.concatenate([a_ref[...], q_ref[...], k_ref[...], v_ref[...], c_ref[...]], axis=1)
        dxn = _mm_nt(dz, w_ref[...])
        dx, dg = _rms_bwd(x, r, gg, dxn)
        dh_ref[...] = gr_ref[...] + dx
        dg_ref[...] += dg
        acc[...] += _mm_tn(x * r * gg, dz)

        @pl.when(i == nt - 1)
        def _():
            for d in range(N_DEV):
                dw_ref[d] = acc[:, d * shard:(d + 1) * shard].astype(WIRE)

    def rows(w_):
        return pl.BlockSpec((tm, w_), lambda i: (i, 0))

    row = pl.BlockSpec((1, D_MODEL), lambda i: (0, 0))
    return pl.pallas_call(
        body, name="inproj_bwd", grid=(nt,),
        in_specs=[rows(D_MODEL), rows(D_MODEL), row, pl.BlockSpec((D_MODEL, IN_COLS), lambda i: (0, 0)),
                  rows(IN_A), rows(B_WIDTH), rows(KV_WIDTH), rows(KV_WIDTH), rows(C_WIDTH)],
        out_specs=[rows(D_MODEL), pl.BlockSpec((N_DEV, D_MODEL, shard), lambda i: (0, 0, 0)), row],
        out_shape=[_sds((T, D_MODEL)), _sds((N_DEV, D_MODEL, shard), WIRE), _sds((1, D_MODEL))],
        scratch_shapes=[pltpu.VMEM((D_MODEL, IN_COLS), F32)],
        compiler_params=_cp("arbitrary"))(gres, h, g, w, dza, dzq, dzk, dzv, dzc)


def _gmlp_fwd(z, lng, lnb, ws, bsx):
    T = z.shape[0]
    tm = min(T, TOKEN_TILE)
    nc = tm // CHUNK

    def body(z_ref, g_ref, b_ref, w_ref, bs_ref, ya_ref):
        zg = _gelu(z_ref[...])
        lo = _lane((tm, LANES)) < HEAD_DIM
        prods = []
        for hd in range(A_HEADS):
            sl = slice(hd * LANES, (hd + 1) * LANES)
            blk = zg[:, sl]
            vn, _, _ = _gmlp_head(blk, g_ref[:, sl], b_ref[:, sl])
            wm = _tril(w_ref[hd])
            sv = jnp.concatenate([_mm(wm, vn[c * CHUNK:(c + 1) * CHUNK]) + bs_ref[hd] for c in range(nc)], axis=0)
            prods.append(blk * _swap64(sv))
        ya_ref[:, 0:LANES] = jnp.where(lo, prods[0], _swap64(prods[1]))
        ya_ref[:, LANES:2 * LANES] = jnp.where(lo, prods[2], _swap64(prods[3]))

    row = pl.BlockSpec((1, IN_A), lambda i: (0, 0))
    mat = pl.BlockSpec((A_HEADS, CHUNK, CHUNK), lambda i: (0, 0, 0))
    return pl.pallas_call(
        body, name="gmlp_fwd", grid=(T // tm,),
        in_specs=[pl.BlockSpec((tm, IN_A), lambda i: (i, 0)), row, row, mat, mat],
        out_specs=pl.BlockSpec((tm, A_WIDTH), lambda i: (i, 0)), out_shape=_sds((T, A_WIDTH)),
        compiler_params=_cp("parallel"))(z, lng, lnb, ws, bsx)


def _gmlp_bwd(z, dya, lng, lnb, ws, bsx):
    T = z.shape[0]
    tm = min(T, TOKEN_TILE)
    nc = tm // CHUNK

    def body(z_ref, dya_ref, g_ref, b_ref, w_ref, bs_ref, dza_ref, dw_ref, dbs_ref, dg_ref, db_ref):
        @pl.when(pl.program_id(0) == 0)
        def _():
            dw_ref[...] = jnp.zeros_like(dw_ref)
            dbs_ref[...] = jnp.zeros_like(dbs_ref)
            dg_ref[...] = jnp.zeros_like(dg_ref)
            db_ref[...] = jnp.zeros_like(db_ref)

        za = z_ref[...]
        zg = _gelu(za)
        gp = _gelu_grad(za)
        lo = _lane((tm, LANES)) < HEAD_DIM
        for hd in range(A_HEADS):
            sl = slice(hd * LANES, (hd + 1) * LANES)
            blk = zg[:, sl]
            g = g_ref[:, sl]
            vn, vhat, rstd = _gmlp_head(blk, g, b_ref[:, sl])
            wm = _tril(w_ref[hd])
            pair = dya_ref[:, (hd // 2) * LANES:(hd // 2 + 1) * LANES]
            dy = jnp.where(lo, pair if hd % 2 == 0 else _swap64(pair), 0.0)
            dsv = _swap64(dy * blk)
            svs, dvns = [], []
            dw = jnp.zeros((CHUNK, CHUNK), F32)
            dbs = jnp.zeros((CHUNK, 1), F32)
            for c in range(nc):
                cs = slice(c * CHUNK, (c + 1) * CHUNK)
                svs.append(_mm(wm, vn[cs]) + bs_ref[hd])
                dw = dw + _mm_nt(dsv[cs], vn[cs])
                dbs = dbs + jnp.sum(dsv[cs], axis=-1, keepdims=True)
                dvns.append(_mm_tn(wm, dsv[cs]))
            sv = jnp.concatenate(svs, axis=0)
            dvn = jnp.concatenate(dvns, axis=0)
            dw_ref[hd] += _tril(dw)
            dbs_ref[hd] += jnp.broadcast_to(dbs, (CHUNK, CHUNK))
            dg_ref[:, sl] += jnp.sum(dvn * vhat, axis=0, keepdims=True)
            db_ref[:, sl] += jnp.sum(dvn, axis=0, keepdims=True)
            du = dy * _swap64(sv)
            dvh = dvn * g
            m1 = jnp.sum(dvh, axis=-1, keepdims=True) * (1.0 / HEAD_DIM)
            m2 = jnp.sum(dvh * vhat, axis=-1, keepdims=True) * (1.0 / HEAD_DIM)
            dv = jnp.where(lo, 0.0, rstd * (dvh - m1 - vhat * m2))
            dza_ref[:, sl] = (du + dv) * gp[:, sl]

    row = pl.BlockSpec((1, IN_A), lambda i: (0, 0))
    mat = pl.BlockSpec((A_HEADS, CHUNK, CHUNK), lambda i: (0, 0, 0))
    return pl.pallas_call(
        body, name="gmlp_bwd", grid=(T // tm,),
        in_specs=[pl.BlockSpec((tm, IN_A), lambda i: (i, 0)), pl.BlockSpec((tm, A_WIDTH), lambda i: (i, 0)), row, row, mat, mat],
        out_specs=[pl.BlockSpec((tm, IN_A), lambda i: (i, 0)), mat, mat, row, row],
        out_shape=[_sds((T, IN_A)), _sds((A_HEADS, CHUNK, CHUNK)), _sds((A_HEADS, CHUNK, CHUNK)), _sds((1, IN_A)), _sds((1, IN_A))],
        compiler_params=_cp("arbitrary"))(z, dya, lng, lnb, ws, bsx)


def _attn_specs(T, tq, tile_of):
    nb = tq // WINDOW

    def prev(i):
        return jnp.maximum(tile_of(i) * nb - 1, 0)

    row = pl.BlockSpec((1, LANES), lambda i: (0, 0))
    return [
        pl.BlockSpec((tq, B_WIDTH), lambda i: (tile_of(i), Q_BLOCK)),
        pl.BlockSpec((tq, LANES), lambda i: (tile_of(i), K_BLOCK)),
        pl.BlockSpec((tq, LANES), lambda i: (tile_of(i), V_BLOCK)),
        pl.BlockSpec((WINDOW, LANES), lambda i: (prev(i), K_BLOCK)),
        pl.BlockSpec((WINDOW, LANES), lambda i: (prev(i), V_BLOCK)),
        pl.BlockSpec((tq, LANES), lambda i: (tile_of(i), 0)),
        pl.BlockSpec((tq, LANES), lambda i: (tile_of(i), 0)),
        pl.BlockSpec((WINDOW, LANES), lambda i: (prev(i), 0)),
        pl.BlockSpec((WINDOW, LANES), lambda i: (prev(i), 0)),
        row, row,
        pl.BlockSpec((8, LANES), lambda i: (0, 0)),
    ]


def _attn_bias(first):
    qi = lax.broadcasted_iota(jnp.int32, (WINDOW, 2 * WINDOW), 0)
    kj = lax.broadcasted_iota(jnp.int32, (WINDOW, 2 * WINDOW), 1)
    diff = qi + WINDOW - kj
    ok = (diff >= 0) & (diff < WINDOW) & ((kj >= WINDOW) | jnp.logical_not(first))
    return jnp.where(ok, 0.0, NEG)


def _dup_heads(x, lo):
    sw = _swap64(x)
    return jnp.where(lo, x, sw), jnp.where(lo, sw, x)


HEADS_PER_KV = 4


def _stack_heads(x0, x1, lo):
    return jnp.concatenate([jnp.where(lo, x0, 0.0), jnp.where(lo, 0.0, x0), jnp.where(lo, x1, 0.0), jnp.where(lo, 0.0, x1)], axis=0)


def _unstack_heads(x4, lo):
    return (jnp.where(lo, x4[0:WINDOW], x4[WINDOW:2 * WINDOW]), jnp.where(lo, x4[2 * WINDOW:3 * WINDOW], x4[3 * WINDOW:4 * WINDOW]))


def _sink_column(sk_ref, g):
    return jnp.concatenate([jnp.broadcast_to(sk_ref[a:a + 1, 0:1], (WINDOW, 1)) for a in range(HEADS_PER_KV * g, HEADS_PER_KV * (g + 1))], axis=0)


def _attn_probs(q4, kw, bias, sink):
    s = _mm_nt(q4, kw)
    s = (s.reshape(HEADS_PER_KV, WINDOW, 2 * WINDOW) + bias[None]).reshape(HEADS_PER_KV * WINDOW, 2 * WINDOW)
    m = jnp.maximum(jnp.max(s, axis=-1, keepdims=True), sink)
    p = jnp.exp(s - m)
    es = jnp.exp(sink - m)
    inv = 1.0 / (jnp.sum(p, axis=-1, keepdims=True) + es)
    return p * inv, es * inv


def _attn_fwd(z, cs, sn, qg, kg, sinks):
    T = z.shape[0]
    tq = min(T, 2 * TOKEN_TILE)
    nb = tq // WINDOW

    def body(q_ref, k_ref, v_ref, kp_ref, vp_ref, cs_ref, sn_ref, csp_ref, snp_ref, qg_ref, kg_ref, sk_ref, o_ref):
        i = pl.program_id(0)
        csq, snq = cs_ref[...], sn_ref[...]
        cs_all = jnp.concatenate([csp_ref[...], csq], axis=0)
        sn_all = jnp.concatenate([snp_ref[...], snq], axis=0)
        k_all = jnp.concatenate([kp_ref[...], k_ref[...]], axis=0)
        v_all = jnp.concatenate([vp_ref[...], v_ref[...]], axis=0)
        kr, _, _ = _qk_norm_rope(k_all, kg_ref[...], cs_all, sn_all)
        lo_all = _lane(k_all.shape) < HEAD_DIM
        kd = _dup_heads(kr, lo_all)
        vd = _dup_heads(v_all, lo_all)
        lo = _lane((WINDOW, LANES)) < HEAD_DIM
        qrs = [_qk_norm_rope(q_ref[:, pr * LANES:(pr + 1) * LANES], qg_ref[...], csq, snq)[0] * SCALE for pr in range(4)]
        biases = [_attn_bias(i * nb + b == 0) for b in range(nb)]
        for g in range(2):
            sink = _sink_column(sk_ref, g)
            for b in range(nb):
                bs = slice(b * WINDOW, (b + 1) * WINDOW)
                ws = slice(b * WINDOW, (b + 2) * WINDOW)
                pn, _ = _attn_probs(_stack_heads(qrs[2 * g][bs], qrs[2 * g + 1][bs], lo), kd[g][ws], biases[b], sink)
                o0, o1 = _unstack_heads(_mm(pn, vd[g][ws]), lo)
                o_ref[bs, 2 * g * LANES:(2 * g + 1) * LANES] = o0
                o_ref[bs, (2 * g + 1) * LANES:(2 * g + 2) * LANES] = o1

    return pl.pallas_call(
        body, name="attn_fwd", grid=(T // tq,),
        in_specs=_attn_specs(T, tq, lambda i: i),
        out_specs=pl.BlockSpec((tq, B_WIDTH), lambda i: (i, 0)), out_shape=_sds((T, B_WIDTH)),
        compiler_params=_cp("parallel"))(z, z, z, z, z, cs, sn, cs, sn, qg, kg, sinks)


def _attn_bwd(z, cs, sn, qg, kg, sinks, o, do):
    T = z.shape[0]
    tq = min(T, 2 * TOKEN_TILE)
    nb = tq // WINDOW
    nt = T // tq
    tk = tq + WINDOW

    def tile_of(i):
        return nt - 1 - i

    def body(q_ref, k_ref, v_ref, kp_ref, vp_ref, cs_ref, sn_ref, csp_ref, snp_ref, qg_ref, kg_ref, sk_ref, o_ref, do_ref,
             dq_ref, dk_ref, dv_ref, dqg_ref, dkg_ref, dsk_ref, acck, accv, ck, cv):
        i = pl.program_id(0)
        ti = nt - 1 - i

        @pl.when(i == 0)
        def _():
            dqg_ref[...] = jnp.zeros_like(dqg_ref)
            dkg_ref[...] = jnp.zeros_like(dkg_ref)
            dsk_ref[...] = jnp.zeros_like(dsk_ref)
            ck[...] = jnp.zeros_like(ck)
            cv[...] = jnp.zeros_like(cv)

        csq, snq = cs_ref[...], sn_ref[...]
        cs_all = jnp.concatenate([csp_ref[...], csq], axis=0)
        sn_all = jnp.concatenate([snp_ref[...], snq], axis=0)
        k_all = jnp.concatenate([kp_ref[...], k_ref[...]], axis=0)
        v_all = jnp.concatenate([vp_ref[...], v_ref[...]], axis=0)
        kr, kh, rk = _qk_norm_rope(k_all, kg_ref[...], cs_all, sn_all)
        lo_all = _lane(k_all.shape) < HEAD_DIM
        kd = _dup_heads(kr, lo_all)
        vd = _dup_heads(v_all, lo_all)
        lo = _lane((WINDOW, LANES)) < HEAD_DIM
        acck[...] = jnp.zeros_like(acck)
        accv[...] = jnp.zeros_like(accv)
        prep = [_qk_norm_rope(q_ref[:, pr * LANES:(pr + 1) * LANES], qg_ref[...], csq, snq) for pr in range(4)]
        biases = [_attn_bias(ti * nb + b == 0) for b in range(nb)]
        dqs = [[None] * nb for _ in range(4)]
        for g in range(2):
            sink = _sink_column(sk_ref, g)
            dsink = jnp.zeros((HEADS_PER_KV * WINDOW, 1), F32)
            for b in range(nb):
                bs = slice(b * WINDOW, (b + 1) * WINDOW)
                ws = slice(b * WINDOW, (b + 2) * WINDOW)
                kw, vw = kd[g][ws], vd[g][ws]
                q4 = _stack_heads(prep[2 * g][0][bs] * SCALE, prep[2 * g + 1][0][bs] * SCALE, lo)
                pn, psink = _attn_probs(q4, kw, biases[b], sink)
                o0, o1 = o_ref[bs, 2 * g * LANES:(2 * g + 1) * LANES], o_ref[bs, (2 * g + 1) * LANES:(2 * g + 2) * LANES]
                do4 = _stack_heads(do_ref[bs, 2 * g * LANES:(2 * g + 1) * LANES], do_ref[bs, (2 * g + 1) * LANES:(2 * g + 2) * LANES], lo)
                delta = jnp.sum(do4 * jnp.concatenate([o0, o0, o1, o1], axis=0), axis=-1, keepdims=True)
                ds = pn * (_mm_nt(do4, vw) - delta)
                dsink = dsink - psink * delta
                dqs[2 * g][b], dqs[2 * g + 1][b] = _unstack_heads(_mm(ds, kw) * SCALE, lo)
                acck[g, ws, :] += _mm_tn(ds, q4)
                accv[g, ws, :] += _mm_tn(pn, do4)
            for hh in range(HEADS_PER_KV):
                a = HEADS_PER_KV * g + hh
                dsk_ref[a:a + 1, :] += jnp.zeros((1, LANES), F32) + jnp.sum(dsink[hh * WINDOW:(hh + 1) * WINDOW])
        for pr in range(4):
            _, qh, rq = prep[pr]
            dx, dg = _qk_norm_rope_bwd(qh, rq, qg_ref[...], csq, snq, jnp.concatenate(dqs[pr], axis=0))
            dq_ref[:, pr * LANES:(pr + 1) * LANES] = dx
            dqg_ref[...] += dg

        def fold(acc):
            f0 = acc[0] + _swap64(acc[0])
            f1 = acc[1] + _swap64(acc[1])
            return jnp.where(lo_all, f0, f1)

        dk_all = fold(acck)
        dv_all = fold(accv)
        pad = jnp.zeros((tq - WINDOW, LANES), F32)
        dk_own = dk_all[WINDOW:] + (jnp.concatenate([pad, ck[...]], axis=0) if nb > 1 else ck[...])
        dv_own = dv_all[WINDOW:] + (jnp.concatenate([pad, cv[...]], axis=0) if nb > 1 else cv[...])
        ck[...] = dk_all[:WINDOW]
        cv[...] = dv_all[:WINDOW]
        dxk, dgk = _qk_norm_rope_bwd(kh[WINDOW:], rk[WINDOW:], kg_ref[...], csq, snq, dk_own)
        dk_ref[...] = dxk
        dkg_ref[...] += dgk
        dv_ref[...] = dv_own

    row = pl.BlockSpec((1, LANES), lambda i: (0, 0))
    return pl.pallas_call(
        body, name="attn_bwd", grid=(nt,),
        in_specs=_attn_specs(T, tq, tile_of) + [pl.BlockSpec((tq, B_WIDTH), lambda i: (tile_of(i), 0))] * 2,
        out_specs=[pl.BlockSpec((tq, B_WIDTH), lambda i: (tile_of(i), 0)), pl.BlockSpec((tq, LANES), lambda i: (tile_of(i), 0)),
                   pl.BlockSpec((tq, LANES), lambda i: (tile_of(i), 0)), row, row, pl.BlockSpec((8, LANES), lambda i: (0, 0))],
        out_shape=[_sds((T, B_WIDTH)), _sds((T, LANES)), _sds((T, LANES)), _sds((1, LANES)), _sds((1, LANES)), _sds((8, LANES))],
        scratch_shapes=[pltpu.VMEM((2, tk, LANES), F32), pltpu.VMEM((2, tk, LANES), F32),
                        pltpu.VMEM((WINDOW, LANES), F32), pltpu.VMEM((WINDOW, LANES), F32)],
        compiler_params=_cp("arbitrary"))(z, z, z, z, z, cs, sn, cs, sn, qg, kg, sinks, o, do)


def _bbar_t(are, aim, ldt, btr, bti):
    lbr, lbi = _lam_bar(are, aim, ldt)
    den = are * are + aim * aim
    nr = lbr - 1.0
    cr = (nr * are + lbi * aim) / den
    ci = (lbi * are - nr * aim) / den
    return cr * btr - ci * bti, cr * bti + ci * btr


def _lam_bar(are, aim, ldt):
    dt = jnp.exp(ldt)
    er = jnp.exp(are * dt)
    return er * jnp.cos(aim * dt), er * jnp.sin(aim * dt)


def _block_diag(x):
    t = jnp.concatenate([x] * C_GROUPS, axis=1)
    return jnp.where(_row(t.shape) // C_GROUP == _lane(t.shape) // C_STATE, t, 0.0)


def _block_diag_fold(m):
    rg = _row((C_WIDTH, C_STATE)) // C_GROUP
    acc = jnp.zeros((C_WIDTH, C_STATE), F32)
    for g in range(C_GROUPS):
        acc = acc + jnp.where(rg == g, m[:, g * C_STATE:(g + 1) * C_STATE], 0.0)
    return acc


def _ssm_prep(are, aim, ldt, are_x, aim_x, ldt_x, btr, bti, cre, cim):
    def body(are_r, aim_r, ldt_r, arex_r, aimx_r, ldtx_r, btr_r, bti_r, cre_r, cim_r, bbd_ref, cbd_ref, pwr_ref, pwi_ref):
        lr, li = _lam_bar(are_r[...], aim_r[...], ldt_r[...])
        cr, ci = lr, li
        for r in range(SCAN_SEG):
            pwr_ref[r:r + 1, :] = cr
            pwi_ref[r:r + 1, :] = ci
            cr, ci = cr * lr - ci * li, cr * li + ci * lr
        br, bi = _bbar_t(arex_r[...], aimx_r[...], ldtx_r[...], btr_r[...], bti_r[...])
        bbd_ref[...] = jnp.concatenate([_block_diag(br), _block_diag(bi)], axis=1).astype(MXU)
        cbd_ref[...] = jnp.concatenate([_block_diag(cre_r[...]), -_block_diag(cim_r[...])], axis=1).astype(MXU)

    return pl.pallas_call(
        body, name="ssm_prep",
        out_shape=[_sds((C_WIDTH, 2 * N_STATE), MXU), _sds((C_WIDTH, 2 * N_STATE), MXU), _sds((SCAN_SEG, N_STATE)), _sds((SCAN_SEG, N_STATE))],
        compiler_params=pltpu.CompilerParams(vmem_limit_bytes=VMEM_LIMIT))(are, aim, ldt, are_x, aim_x, ldt_x, btr, bti, cre, cim)


def _ssm_param_bwd(are, aim, ldt, are_x, aim_x, ldt_x, btr, bti, dlr, dli, dbbd, dcr, dci):
    def body(are_r, aim_r, ldt_r, arex_r, aimx_r, ldtx_r, btr_r, bti_r, dlr_r, dli_r, dbbd_r, dcr_r, dci_r,
             dare_ref, daim_ref, dldt_ref, dbtr_ref, dbti_ref, dcre_ref, dcim_ref):
        _, vjp_l = jax.vjp(_lam_bar, are_r[...], aim_r[...], ldt_r[...])
        da1, di1, dl1 = vjp_l((dlr_r[...], dli_r[...]))
        dbr = _block_diag_fold(dbbd_r[:, 0:N_STATE])
        dbi = _block_diag_fold(dbbd_r[:, N_STATE:2 * N_STATE])
        _, vjp_b = jax.vjp(_bbar_t, arex_r[...], aimx_r[...], ldtx_r[...], btr_r[...], bti_r[...])
        da2, di2, dl2, dbtr, dbti = vjp_b((dbr, dbi))

        def gsum(x):
            return x.reshape(C_GROUPS, C_GROUP, C_STATE).sum(axis=1)

        dare_ref[...] = da1 + gsum(da2)
        daim_ref[...] = di1 + gsum(di2)
        dldt_ref[...] = jnp.broadcast_to(jnp.sum(dl1 + gsum(dl2), axis=-1, keepdims=True), (C_GROUPS, LANES))
        dbtr_ref[...] = dbtr
        dbti_ref[...] = dbti
        dcre_ref[...] = _block_diag_fold(dcr_r[...])
        dcim_ref[...] = -_block_diag_fold(dci_r[...])

    g = _sds((C_GROUPS, C_STATE))
    x = _sds((C_WIDTH, C_STATE))
    return pl.pallas_call(
        body, name="ssm_param_bwd", out_shape=[g, g, _sds((C_GROUPS, LANES)), x, x, x, x],
        compiler_params=pltpu.CompilerParams(vmem_limit_bytes=VMEM_LIMIT))(are, aim, ldt, are_x, aim_x, ldt_x, btr, bti, dlr, dli, dbbd, dcr, dci)


SCAN_TILE = 512
SCAN_SEG = 8


def _scan_tables(pwr_ref, pwi_ref, conj, reverse):
    row = _row((SCAN_SEG, N_STATE))
    shifts = []
    for k in (1, 2, 4):
        keep = (row < SCAN_SEG - k) if reverse else (row >= k)
        ar = jnp.broadcast_to(pwr_ref[k - 1:k, :], (SCAN_SEG, N_STATE))
        ai = jnp.broadcast_to(pwi_ref[k - 1:k, :], (SCAN_SEG, N_STATE)) * conj
        shifts.append((SCAN_SEG - k if reverse else k, jnp.where(keep, ar, 0.0), jnp.where(keep, ai, 0.0)))
    if reverse:
        pr = jnp.concatenate([pwr_ref[SCAN_SEG - 1 - r:SCAN_SEG - r, :] for r in range(SCAN_SEG)], axis=0)
        pi = jnp.concatenate([pwi_ref[SCAN_SEG - 1 - r:SCAN_SEG - r, :] for r in range(SCAN_SEG)], axis=0) * conj
    else:
        pr, pi = pwr_ref[...], pwi_ref[...] * conj
    return shifts, (pr, pi)


def _tile_scan(xr_ref, xi_ref, pwr_ref, pwi_ref, sr, si, conj, reverse):
    shifts, (pr, pi) = _scan_tables(pwr_ref, pwi_ref, conj, reverse)
    groups = xr_ref.shape[0] // SCAN_SEG
    out_row = 0 if reverse else SCAN_SEG - 1

    def step(k, c):
        cr, ci = c
        g = groups - 1 - k if reverse else k
        rows = pl.ds(pl.multiple_of(g * SCAN_SEG, SCAN_SEG), SCAN_SEG)
        xr, xi = xr_ref[rows, :], xi_ref[rows, :]
        for amount, ar, ai in shifts:
            qr, qi = pltpu.roll(xr, amount, 0), pltpu.roll(xi, amount, 0)
            xr, xi = xr + ar * qr - ai * qi, xi + ar * qi + ai * qr
        xr, xi = xr + pr * cr - pi * ci, xi + pr * ci + pi * cr
        xr_ref[rows, :] = xr
        xi_ref[rows, :] = xi
        return xr[out_row:out_row + 1], xi[out_row:out_row + 1]

    cr, ci = lax.fori_loop(0, groups, step, (sr[...], si[...]), unroll=2)
    sr[...] = cr
    si[...] = ci


def _ssm_fwd(z, bbd, pwr, pwi, cbd, dsk, w1, w2):
    T = z.shape[0]
    tt = min(T, 2 * SCAN_TILE)

    def body(u_ref, bbd_ref, pwr_ref, pwi_ref, cbd_ref, d_ref, w1_ref, w2_ref, yc_ref, y_ref, xr_ref, xi_ref, sr, si):
        @pl.when(pl.program_id(0) == 0)
        def _():
            sr[...] = jnp.zeros_like(sr)
            si[...] = jnp.zeros_like(si)

        u = u_ref[...]
        bu = _mm(u, bbd_ref[...])
        xr_ref[...] = bu[:, 0:N_STATE]
        xi_ref[...] = bu[:, N_STATE:2 * N_STATE]
        _tile_scan(xr_ref, xi_ref, pwr_ref, pwi_ref, sr, si, 1.0, False)
        x = jnp.concatenate([xr_ref[...], xi_ref[...]], axis=1)
        y = _mm_nt(x, cbd_ref[...]) + d_ref[...] * u
        y_ref[...] = y
        y2 = _gelu(y)
        yc_ref[...] = _mm(y2, w1_ref[...]) * _sigmoid(_mm(y2, w2_ref[...]))

    big = pl.BlockSpec((C_WIDTH, 2 * N_STATE), lambda i: (0, 0))
    tab = pl.BlockSpec((SCAN_SEG, N_STATE), lambda i: (0, 0))
    wsp = pl.BlockSpec((C_WIDTH, C_WIDTH), lambda i: (0, 0))
    xs = pl.BlockSpec((tt, N_STATE), lambda i: (i, 0))
    return pl.pallas_call(
        body, name="ssm_fwd", grid=(T // tt,),
        in_specs=[pl.BlockSpec((tt, C_WIDTH), lambda i: (i, C_BLOCK)), big, tab, tab, big, pl.BlockSpec((1, C_WIDTH), lambda i: (0, 0)), wsp, wsp],
        out_specs=[pl.BlockSpec((tt, C_WIDTH), lambda i: (i, 0)), pl.BlockSpec((tt, C_WIDTH), lambda i: (i, 0)), xs, xs],
        out_shape=[_sds((T, C_WIDTH)), _sds((T, C_WIDTH)), _sds((T, N_STATE)), _sds((T, N_STATE))],
        scratch_shapes=[pltpu.VMEM((1, N_STATE), F32)] * 2,
        compiler_params=_cp("arbitrary"))(z, bbd, pwr, pwi, cbd, dsk, w1, w2)


def _ssm_bwd(dyc, z, y, xr, xi, bbd, pwr, pwi, cbd, dsk, w1, w2):
    T = z.shape[0]
    tt = min(T, SCAN_TILE)
    nt = T // tt

    def tile_of(i):
        return nt - 1 - i

    def body(dyc_ref, u_ref, y_ref, xr_ref, xi_ref, xpr_ref, xpi_ref, bbd_ref, pwr_ref, pwi_ref, cbd_ref, d_ref, w1_ref, w2_ref,
             du_ref, dw1_ref, dw2_ref, dcr_ref, dci_ref, dbbd_ref, dlr_ref, dli_ref, dd_ref, gr, gi, sr, si):
        i = pl.program_id(0)
        ti = nt - 1 - i

        @pl.when(i == 0)
        def _():
            sr[...] = jnp.zeros_like(sr)
            si[...] = jnp.zeros_like(si)
            for acc in (dw1_ref, dw2_ref, dcr_ref, dci_ref, dbbd_ref, dlr_ref, dli_ref, dd_ref):
                acc[...] = jnp.zeros_like(acc)

        u = u_ref[...]
        xr_t, xi_t = xr_ref[...], xi_ref[...]
        y = y_ref[...]
        y2 = _gelu(y)
        a1 = _mm(y2, w1_ref[...])
        sg = _sigmoid(_mm(y2, w2_ref[...]))
        dyc_t = dyc_ref[...]
        da1 = dyc_t * sg
        da2 = dyc_t * a1 * sg * (1.0 - sg)
        dy = (_mm_nt(da1, w1_ref[...]) + _mm_nt(da2, w2_ref[...])) * _gelu_grad(y)
        gx = _mm(dy, cbd_ref[...])
        gr[...] = gx[:, 0:N_STATE]
        gi[...] = gx[:, N_STATE:2 * N_STATE]
        _tile_scan(gr, gi, pwr_ref, pwi_ref, sr, si, -1.0, True)
        ar, ai = gr[...], gi[...]
        first_row = _row(ar.shape) == 0
        live = jnp.where(ti > 0, 1.0, 0.0)
        xsr = jnp.where(first_row, xpr_ref[7:8, :] * live, pltpu.roll(xr_t, 1, 0))
        xsi = jnp.where(first_row, xpi_ref[7:8, :] * live, pltpu.roll(xi_t, 1, 0))
        dlr_ref[...] += jnp.sum(ar * xsr + ai * xsi, axis=0, keepdims=True)
        dli_ref[...] += jnp.sum(ai * xsr - ar * xsi, axis=0, keepdims=True)
        dd_ref[...] += jnp.sum(dy * u, axis=0, keepdims=True)
        arai = jnp.concatenate([ar, ai], axis=1)
        du_ref[...] = _mm_nt(arai, bbd_ref[...]) + d_ref[...] * dy
        dw1_ref[...] += _mm_tn(y2, da1)
        dw2_ref[...] += _mm_tn(y2, da2)
        dcr_ref[...] += _mm_tn(dy, xr_t)
        dci_ref[...] += _mm_tn(dy, xi_t)
        dbbd_ref[...] += _mm_tn(u, arai)

    def prev(i):
        return jnp.maximum(tile_of(i) * (tt // 8) - 1, 0)

    big = pl.BlockSpec((C_WIDTH, 2 * N_STATE), lambda i: (0, 0))
    tab = pl.BlockSpec((SCAN_SEG, N_STATE), lambda i: (0, 0))
    srow = pl.BlockSpec((1, N_STATE), lambda i: (0, 0))
    wsp = pl.BlockSpec((C_WIDTH, C_WIDTH), lambda i: (0, 0))
    xs = pl.BlockSpec((tt, N_STATE), lambda i: (tile_of(i), 0))
    xp = pl.BlockSpec((8, N_STATE), lambda i: (prev(i), 0))
    cw = pl.BlockSpec((tt, C_WIDTH), lambda i: (tile_of(i), 0))
    drow = pl.BlockSpec((1, C_WIDTH), lambda i: (0, 0))
    return pl.pallas_call(
        body, name="ssm_bwd", grid=(nt,),
        in_specs=[cw, pl.BlockSpec((tt, C_WIDTH), lambda i: (tile_of(i), C_BLOCK)), cw, xs, xs, xp, xp, big, tab, tab, big, drow, wsp, wsp],
        out_specs=[cw, wsp, wsp, pl.BlockSpec((C_WIDTH, N_STATE), lambda i: (0, 0)), pl.BlockSpec((C_WIDTH, N_STATE), lambda i: (0, 0)), big,
                   srow, srow, drow],
        out_shape=[_sds((T, C_WIDTH)), _sds((C_WIDTH, C_WIDTH)), _sds((C_WIDTH, C_WIDTH)), _sds((C_WIDTH, N_STATE)), _sds((C_WIDTH, N_STATE)),
                   _sds((C_WIDTH, 2 * N_STATE)), _sds((1, N_STATE)), _sds((1, N_STATE)), _sds((1, C_WIDTH))],
        scratch_shapes=[pltpu.VMEM((tt, N_STATE), F32)] * 2 + [pltpu.VMEM((1, N_STATE), F32)] * 2,
        compiler_params=_cp("arbitrary"))(dyc, z, y, xr, xi, xr, xi, bbd, pwr, pwi, cbd, dsk, w1, w2)


_GROUPS = ((0, A_WIDTH), (A_WIDTH, A_WIDTH + B_WIDTH), (A_WIDTH + B_WIDTH, D_MODEL))


def _merge_fwd(h, ya, yb, yc, g, w):
    T = h.shape[0]
    tm = min(T, TOKEN_TILE)

    def body(h_ref, a_ref, b_ref, c_ref, g_ref, w_ref, o_ref):
        yn = jnp.concatenate([y * _rms_stat(y) for y in (a_ref[...], b_ref[...], c_ref[...])], axis=1) * g_ref[...]
        o_ref[...] = h_ref[...] + _mm(yn, w_ref[...])

    def rows(w_):
        return pl.BlockSpec((tm, w_), lambda i: (i, 0))

    return pl.pallas_call(
        body, name="merge_fwd", grid=(T // tm,),
        in_specs=[rows(D_MODEL), rows(A_WIDTH), rows(B_WIDTH), rows(C_WIDTH), pl.BlockSpec((1, D_MODEL), lambda i: (0, 0)),
                  pl.BlockSpec((D_MODEL, D_MODEL), lambda i: (0, 0))],
        out_specs=rows(D_MODEL), out_shape=_sds((T, D_MODEL)), compiler_params=_cp("parallel"))(h, ya, yb, yc, g, w)


def _merge_bwd(gres, ya, yb, yc, g, w):
    T = gres.shape[0]
    tm = min(T, TOKEN_TILE)
    nt = T // tm

    def body(gr_ref, a_ref, b_ref, c_ref, g_ref, w_ref, da_ref, db_ref, dc_ref, dw_ref, dg_ref, acc):
        i = pl.program_id(0)

        @pl.when(i == 0)
        def _():
            dg_ref[...] = jnp.zeros_like(dg_ref)
            acc[...] = jnp.zeros_like(acc)

        gr = gr_ref[...]
        dyn = _mm_nt(gr, w_ref[...])
        yns, dgs = [], []
        for (c0, c1), y_ref, d_ref in zip(_GROUPS, (a_ref, b_ref, c_ref), (da_ref, db_ref, dc_ref)):
            y = y_ref[...]
            r = _rms_stat(y)
            gg = g_ref[:, c0:c1]
            dx, dg = _rms_bwd(y, r, gg, dyn[:, c0:c1])
            d_ref[...] = dx
            dgs.append(dg)
            yns.append(y * r * gg)
        dg_ref[...] += jnp.concatenate(dgs, axis=1)
        acc[...] += _mm_tn(jnp.concatenate(yns, axis=1), gr)

        @pl.when(i == nt - 1)
        def _():
            dw_ref[...] = acc[...].astype(WIRE)

    def rows(w_):
        return pl.BlockSpec((tm, w_), lambda i: (i, 0))

    row = pl.BlockSpec((1, D_MODEL), lambda i: (0, 0))
    full = pl.BlockSpec((D_MODEL, D_MODEL), lambda i: (0, 0))
    return pl.pallas_call(
        body, name="merge_bwd", grid=(nt,),
        in_specs=[rows(D_MODEL), rows(A_WIDTH), rows(B_WIDTH), rows(C_WIDTH), row, full],
        out_specs=[rows(A_WIDTH), rows(B_WIDTH), rows(C_WIDTH), full, row],
        out_shape=[_sds((T, A_WIDTH)), _sds((T, B_WIDTH)), _sds((T, C_WIDTH)), _sds((D_MODEL, D_MODEL), WIRE), _sds((1, D_MODEL))],
        scratch_shapes=[pltpu.VMEM((D_MODEL, D_MODEL), F32)],
        compiler_params=_cp("arbitrary"))(gres, ya, yb, yc, g, w)


FF_BLOCK = D_FF // N_DEV


def _load_weights(w1_hbm, w2_hbm, w1, w2, sem):
    @pl.when(pl.program_id(0) == 0)
    def _():
        copies = [pltpu.make_async_copy(w1_hbm.at[j], w1.at[:, pl.ds(j * FF_BLOCK, FF_BLOCK)], sem.at[j]) for j in range(N_DEV)]
        copies.append(pltpu.make_async_copy(w2_hbm, w2, sem.at[N_DEV]))
        for cp in copies:
            cp.start()
        for cp in copies:
            cp.wait()


MLP_CHUNK = 1024


def _mlp_weight_scratch():
    return [pltpu.VMEM((D_MODEL, D_FF), MXU), pltpu.VMEM((D_FF, D_MODEL), MXU), pltpu.SemaphoreType.DMA((N_DEV + 1,))]


def _mlp_fwd(h, g, w1, w2):
    T = h.shape[0]
    tm = min(T, TOKEN_TILE)

    def body(h_ref, g_ref, w1_hbm, w2_hbm, o_ref, r_ref, w1_v, w2_v, sem):
        _load_weights(w1_hbm, w2_hbm, w1_v, w2_v, sem)
        x = h_ref[...]
        a = jnp.maximum(_mm(x * _rms_stat(x) * g_ref[...], w1_v[...]), 0.0)
        r = (a * a).astype(MXU)
        r_ref[...] = r
        o_ref[...] = x + _mm(r, w2_v[...])

    rows = pl.BlockSpec((tm, D_MODEL), lambda i: (i, 0))
    hbm = pl.BlockSpec(memory_space=pl.ANY)
    return pl.pallas_call(
        body, name="mlp_fwd", grid=(T // tm,),
        in_specs=[rows, pl.BlockSpec((1, D_MODEL), lambda i: (0, 0)), hbm, hbm],
        out_specs=[rows, pl.BlockSpec((tm, D_FF), lambda i: (i, 0))],
        out_shape=[_sds((T, D_MODEL)), _sds((T, D_FF), MXU)],
        scratch_shapes=_mlp_weight_scratch(), compiler_params=_cp("arbitrary"))(h, g, w1, w2)


def _mlp_bwd(gres, h, g, r, w1, w2):
    T = h.shape[0]
    tm = min(T, TOKEN_TILE)

    def body(gr_ref, h_ref, g_ref, r_ref, w1_hbm, w2_hbm, dh_ref, hn_ref, da_ref, dg_ref, w1_v, w2_v, sem):
        _load_weights(w1_hbm, w2_hbm, w1_v, w2_v, sem)

        @pl.when(pl.program_id(0) == 0)
        def _():
            dg_ref[...] = jnp.zeros_like(dg_ref)

        gr = gr_ref[...]
        grm = gr.astype(MXU)
        dhn = jnp.zeros((tm, D_MODEL), F32)
        for c in range(D_FF // MLP_CHUNK):
            cs = slice(c * MLP_CHUNK, (c + 1) * MLP_CHUNK)
            da = (_mm_nt(grm, w2_v[cs, :]) * (2.0 * jnp.sqrt(r_ref[:, cs].astype(F32)))).astype(MXU)
            da_ref[:, cs] = da
            dhn = dhn + _mm_nt(da, w1_v[:, cs])
        x = h_ref[...]
        rs = _rms_stat(x)
        gg = g_ref[...]
        dx, dg = _rms_bwd(x, rs, gg, dhn)
        dh_ref[...] = gr + dx
        dg_ref[...] += dg
        hn_ref[...] = (x * rs * gg).astype(MXU)

    rows = pl.BlockSpec((tm, D_MODEL), lambda i: (i, 0))
    row = pl.BlockSpec((1, D_MODEL), lambda i: (0, 0))
    wide = pl.BlockSpec((tm, D_FF), lambda i: (i, 0))
    hbm = pl.BlockSpec(memory_space=pl.ANY)
    return pl.pallas_call(
        body, name="mlp_bwd", grid=(T // tm,),
        in_specs=[rows, rows, row, wide, hbm, hbm], out_specs=[rows, rows, wide, row],
        out_shape=[_sds((T, D_MODEL)), _sds((T, D_MODEL), MXU), _sds((T, D_FF), MXU), _sds((1, D_MODEL))],
        scratch_shapes=_mlp_weight_scratch(), compiler_params=_cp("arbitrary"))(gres, h, g, r, w1, w2)


def _ple_fwd(h, p, l, g, wg, wp):
    T = h.shape[0]
    tm = min(T, TOKEN_TILE)

    def body(h_ref, p_ref, g_ref, wg_ref, wp_ref, o_ref, gp_ref, e_ref):
        x = h_ref[...]
        gp = _mm(x * _rms_stat(x) * g_ref[...], wg_ref[...])
        e = _mm(p_ref[...], wp_ref[...])
        o_ref[...] = x + _sigmoid(gp) * e
        gp_ref[...] = gp.astype(MXU)
        e_ref[...] = e.astype(MXU)

    rows = pl.BlockSpec((tm, D_MODEL), lambda i: (i, 0))
    return pl.pallas_call(
        body, name="ple_fwd", grid=(T // tm,),
        in_specs=[rows, pl.BlockSpec((None, tm, PLE_DIM), lambda i: (l, i, 0)), pl.BlockSpec((1, D_MODEL), lambda i: (0, 0)),
                  pl.BlockSpec((D_MODEL, D_MODEL), lambda i: (0, 0)), pl.BlockSpec((PLE_DIM, D_MODEL), lambda i: (0, 0))],
        out_specs=[rows, rows, rows], out_shape=[_sds((T, D_MODEL)), _sds((T, D_MODEL), MXU), _sds((T, D_MODEL), MXU)],
        compiler_params=_cp("parallel"))(h, p, g, wg, wp)


def _ple_bwd(gres, h, p, l, g, gp, e, wg):
    T = h.shape[0]
    tm = min(T, TOKEN_TILE)
    nt = T // tm
    shard = D_MODEL // N_DEV

    def body(gr_ref, h_ref, p_ref, g_ref, gp_ref, e_ref, wg_ref, dh_ref, dwg_ref, dwp_ref, dg_ref, accg, accp):
        i = pl.program_id(0)

        @pl.when(i == 0)
        def _():
            dg_ref[...] = jnp.zeros_like(dg_ref)
            accg[...] = jnp.zeros_like(accg)
            accp[...] = jnp.zeros_like(accp)

        x = h_ref[...]
        r = _rms_stat(x)
        gg = g_ref[...]
        hn = x * r * gg
        gate = _sigmoid(gp_ref[...].astype(F32))
        pe = p_ref[...]
        e = e_ref[...].astype(F32)
        gr = gr_ref[...]
        dgp = gr * e * gate * (1.0 - gate)
        dx, dg = _rms_bwd(x, r, gg, _mm_nt(dgp, wg_ref[...]))
        dh_ref[...] = gr + dx
        dg_ref[...] += dg
        accg[...] += _mm_tn(hn, dgp)
        accp[...] += _mm_tn(pe, gr * gate)

        @pl.when(i == nt - 1)
        def _():
            dwg_ref[...] = accg[...].astype(WIRE)
            for d in range(N_DEV):
                dwp_ref[d] = accp[:, d * shard:(d + 1) * shard].astype(WIRE)

    rows = pl.BlockSpec((tm, D_MODEL), lambda i: (i, 0))
    row = pl.BlockSpec((1, D_MODEL), lambda i: (0, 0))
    full = pl.BlockSpec((D_MODEL, D_MODEL), lambda i: (0, 0))
    return pl.pallas_call(
        body, name="ple_bwd", grid=(nt,),
        in_specs=[rows, rows, pl.BlockSpec((None, tm, PLE_DIM), lambda i: (l, i, 0)), row, rows, rows, full],
        out_specs=[rows, full, pl.BlockSpec((N_DEV, PLE_DIM, shard), lambda i: (0, 0, 0)), row],
        out_shape=[_sds((T, D_MODEL)), _sds((D_MODEL, D_MODEL), WIRE), _sds((N_DEV, PLE_DIM, shard), WIRE), _sds((1, D_MODEL))],
        scratch_shapes=[pltpu.VMEM((D_MODEL, D_MODEL), F32), pltpu.VMEM((PLE_DIM, D_MODEL), F32)],
        compiler_params=_cp("arbitrary"))(gres, h, p, g, gp, e, wg)


def _loss_head(h, target):
    T = h.shape[0]
    tm = min(T, 1024)

    def body(h_ref, t_ref, dh_ref, l_ref):
        @pl.when(pl.program_id(0) == 0)
        def _():
            l_ref[...] = jnp.zeros_like(l_ref)

        e = h_ref[...] - t_ref[...]
        dh_ref[...] = e * (1.0 / D_MODEL)
        l_ref[...] += jnp.zeros_like(l_ref) + 0.5 * jnp.sum(jnp.mean(e * e, axis=-1, keepdims=True))

    rows = pl.BlockSpec((tm, D_MODEL), lambda i: (i, 0))
    return pl.pallas_call(
        body, name="loss_head", grid=(T // tm,), in_specs=[rows, rows],
        out_specs=[rows, pl.BlockSpec((8, LANES), lambda i: (0, 0))], out_shape=[_sds((T, D_MODEL)), _sds((8, LANES))],
        compiler_params=_cp("arbitrary"))(h, target)


TN_ROWS = 2048

def _tn(a, b, m, n, *, bm, bn, n_major=False, dtype=F32, name="tn"):
    T = a.shape[0]
    tk = min(T, TN_ROWS)
    nk = T // tk
    assert m % bm == 0 and n % bn == 0 and (not n_major or bm == m)

    def body(a_ref, b_ref, o_ref, acc):
        k = pl.program_id(2)

        @pl.when(k == 0)
        def _():
            acc[...] = jnp.zeros_like(acc)

        acc[...] += _mm_tn(a_ref[...], b_ref[...])

        @pl.when(k == nk - 1)
        def _():
            o_ref[...] = acc[...].astype(dtype)

    if n_major:
        out_spec = pl.BlockSpec((None, bm, bn), lambda i, j, k: (j, 0, 0))
        out_shape = _sds((n // bn, m, bn), dtype)
    else:
        out_spec = pl.BlockSpec((bm, bn), lambda i, j, k: (i, j))
        out_shape = _sds((m, n), dtype)
    return pl.pallas_call(
        body, name=name, grid=(m // bm, n // bn, nk),
        in_specs=[pl.BlockSpec((tk, bm), lambda i, j, k: (k, i)), pl.BlockSpec((tk, bn), lambda i, j, k: (k, j))],
        out_specs=out_spec, out_shape=out_shape, scratch_shapes=[pltpu.VMEM((bm, bn), F32)],
        compiler_params=_cp("parallel", "parallel", "arbitrary"))(a, b)


def _row_tile(R, C):
    for cand in (512, 256, 128, 64, 32, 16, 8):
        if R % cand == 0 and cand * C * 4 <= 2 ** 20:
            return cand
    return R


def _sum_slots(land):
    S, R, C = land.shape
    tr = _row_tile(R, C)

    def body(l_ref, o_ref):
        acc = l_ref[0].astype(F32)
        for s in range(1, S):
            acc = acc + l_ref[s].astype(F32)
        o_ref[...] = acc

    return pl.pallas_call(
        body, name="sum_slots", grid=(R // tr,), in_specs=[pl.BlockSpec((S, tr, C), lambda i: (0, i, 0))],
        out_specs=pl.BlockSpec((tr, C), lambda i: (i, 0)), out_shape=_sds((R, C)), compiler_params=_cp("parallel"))(land)


def _sum_adamw(land, w, m, v):
    R, C = w.shape
    S = land.shape[0]
    tr = _row_tile(R, C)

    def body(l_ref, w_ref, m_ref, v_ref, g_ref, d_ref, nm_ref, nv_ref):
        gg = l_ref[0].astype(F32)
        for s in range(1, S):
            gg = gg + l_ref[s].astype(F32)
        g_ref[...] = gg
        nm = ADAM_B1 * m_ref[...] + (1.0 - ADAM_B1) * gg
        nv = ADAM_B2 * v_ref[...] + (1.0 - ADAM_B2) * (gg * gg)
        m_hat = nm / (1.0 - ADAM_B1 ** ADAM_STEP)
        v_hat = nv / (1.0 - ADAM_B2 ** ADAM_STEP)
        d_ref[...] = -ADAM_LR * (m_hat / (jnp.sqrt(v_hat) + ADAM_EPS) + ADAM_WD * w_ref[...])
        nm_ref[...] = nm
        nv_ref[...] = nv

    blk = pl.BlockSpec((tr, C), lambda i: (i, 0))
    return pl.pallas_call(
        body, name="sum_adamw", grid=(R // tr,), in_specs=[pl.BlockSpec((S, tr, C), lambda i: (0, i, 0))] + [blk] * 3,
        out_specs=[blk] * 4, out_shape=[_sds((R, C))] * 4, compiler_params=_cp("parallel"))(land, w, m, v)


_HBM = pl.BlockSpec(memory_space=pltpu.HBM)
_SEM = pl.BlockSpec(memory_space=pltpu.SEMAPHORE)
_EFFECT = pltpu.SideEffectType.DATAFLOW_SIDE_EFFECTING


def _peers():
    x, y, c = lax.axis_index("x"), lax.axis_index("y"), lax.axis_index("c")
    out = []
    for k in range(1, N_DEV):
        px, py, pc = x ^ (k >> 2), y ^ ((k >> 1) & 1), c ^ (k & 1)
        out.append((k, (px, py, pc), 4 * px + 2 * py + pc))
    return 4 * x + 2 * y + c, out


def _route(mode, layer):
    if mode == "all":
        return (lambda ref, peer: ref), (lambda ref, sender: ref.at[sender])
    return (lambda ref, peer: ref.at[peer]), (lambda ref, sender: ref.at[sender, layer])


def _split_start(srcs, lands, modes, layer, after, name):
    n = len(srcs)
    routes = [_route(m, layer) for m in modes]

    def body(*refs):
        src, land = refs[:n], refs[n:2 * n]
        send, recv, token = refs[2 * n + 1], refs[2 * n + 2], refs[-1]
        me, peers = _peers()
        for k, dev, peer in peers:
            for t, (src_of, dst_of) in enumerate(routes):
                pltpu.make_async_remote_copy(src_ref=src_of(src[t], peer), dst_ref=dst_of(land[t], me), send_sem=send.at[t * N_DEV + k],
                                             recv_sem=recv.at[t * N_DEV + k], device_id=dev, device_id_type=MESH).start()
        token[...] = jnp.zeros_like(token)

    bufs = list(srcs) + list(lands)
    outs = pl.pallas_call(
        body, name=name,
        out_shape=(pltpu.SemaphoreType.DMA((n * N_DEV,)), pltpu.SemaphoreType.DMA((n * N_DEV,)),
                   *[pltpu.HBM(a.shape, a.dtype) for a in bufs], _sds((8, LANES))),
        in_specs=[_HBM] * (2 * n) + [pl.BlockSpec(memory_space=pl.ANY)],
        out_specs=(_SEM, _SEM, *[_HBM] * (2 * n), pl.BlockSpec(memory_space=pltpu.VMEM)),
        input_output_aliases={i: 2 + i for i in range(2 * n)},
        compiler_params=pltpu.CompilerParams(has_side_effects=_EFFECT),
    )(*[pltpu.with_memory_space_constraint(a, pltpu.HBM) for a in bufs], after)
    return outs[0], outs[1], list(outs[2:2 + n]), list(outs[2 + n:2 + 2 * n]), outs[-1]


def _split_wait(send, recv, srcs, lands, modes, layer, after, name):
    n = len(srcs)
    routes = [_route(m, layer) for m in modes]

    def body(*refs):
        src, land = refs[:n], refs[n:2 * n]
        send_r, recv_r = refs[2 * n], refs[2 * n + 1]
        _, peers = _peers()
        for k, dev, peer in peers:
            for t, (src_of, dst_of) in enumerate(routes):
                cp = pltpu.make_async_remote_copy(src_ref=src_of(src[t], peer), dst_ref=dst_of(land[t], peer), send_sem=send_r.at[t * N_DEV + k],
                                                  recv_sem=recv_r.at[t * N_DEV + k], device_id=dev, device_id_type=MESH)
                cp.wait_send()
                cp.wait_recv()

    bufs = list(srcs) + list(lands)
    outs = pl.pallas_call(
        body, name=name, out_shape=tuple(pltpu.HBM(a.shape, a.dtype) for a in bufs),
        in_specs=[_HBM] * (2 * n) + [_SEM, _SEM, pl.BlockSpec(memory_space=pl.ANY)], out_specs=[_HBM] * (2 * n),
        input_output_aliases={i: i for i in range(2 * n)},
        compiler_params=pltpu.CompilerParams(has_side_effects=_EFFECT),
    )(*bufs, send, recv, after)
    return list(outs[:n]), list(outs[n:])


SHARDED = ("w_in", "glu_w1", "glu_w2", "w_out", "w_ff1", "w_ff2", "w_ple_gate", "w_ple_proj")
SMALL = ("attn_norm_g", "gmlp_ln_g", "gmlp_ln_b", "gmlp_ws", "gmlp_bs", "q_norm_g", "k_norm_g", "sinks", "ssm_a_re", "ssm_a_im",
         "ssm_log_dt", "ssm_b_re", "ssm_b_im", "ssm_c_re", "ssm_c_im", "ssm_d", "mix_out_g", "mlp_norm_g", "ple_norm_g")
WEIGHTS = ("attn_norm_g", "w_in", "gmlp_ln_g", "gmlp_ln_b", "gmlp_ws", "gmlp_bs", "q_norm_g", "k_norm_g", "sinks", "ssm_a_re", "ssm_a_im",
           "ssm_log_dt", "ssm_b_re", "ssm_b_im", "ssm_c_re", "ssm_c_im", "ssm_d", "glu_w1", "glu_w2", "mix_out_g", "w_out", "mlp_norm_g",
           "w_ff1", "w_ff2", "ple_norm_g", "w_ple_gate", "w_ple_proj")
FLAT_COLS = 1024


PACK_TILE_ROWS = 8
PACK_ROWS_MULTIPLE = PACK_TILE_ROWS * N_DEV


def _packed_rows(shape):
    return -(-math.prod(shape) // (PACK_TILE_ROWS * FLAT_COLS)) * PACK_TILE_ROWS


def _pack(arrs, dtype):
    blocks = []
    for a in arrs:
        flat = a.astype(dtype).reshape(-1)
        pad = _packed_rows(a.shape) * FLAT_COLS - flat.shape[0]
        if pad:
            flat = jnp.concatenate([flat, jnp.zeros((pad,), dtype)])
        blocks.append(flat.reshape(-1, FLAT_COLS))
    rows = sum(b.shape[0] for b in blocks)
    if rows % PACK_ROWS_MULTIPLE:
        blocks.append(jnp.zeros((PACK_ROWS_MULTIPLE - rows % PACK_ROWS_MULTIPLE, FLAT_COLS), dtype))
    return jnp.concatenate(blocks, axis=0)


def _unpack(flat, shapes):
    out, r = [], 0
    for s in shapes:
        nr = _packed_rows(s)
        out.append(flat[r:r + nr].reshape(-1)[:math.prod(s)].reshape(s))
        r += nr
    return out


def _from_col_major(s):
    n, rows, cs = s.shape
    return s.transpose(1, 0, 2).reshape(rows, n * cs)


EARLY = ("w_in", "glu_w1", "glu_w2")
LATE = ("w_out", "w_ff1", "w_ff2", "w_ple_gate", "w_ple_proj")
GRADS_MID = ("w_ff1", "w_ff2", "w_ple_gate", "w_ple_proj")
GRADS_END = ("w_in", "glu_w1", "glu_w2", "w_out")


def _layer_fwd(h, p, l, cs, sn, W, late_weights, S, sp):
    z = _inproj_fwd(h, S["attn_norm_g"], W["w_in"])
    ya = _gmlp_fwd(z, S["lng"], S["lnb"], S["gmlp_ws"], S["bsx"])
    yb = _attn_fwd(z, cs, sn, S["qg"], S["kg"], S["sinks"])
    yc, y, xr, xi = _ssm_fwd(z, sp["bbd"], sp["pwr"], sp["pwi"], sp["cbd"], S["ssm_d"], W["glu_w1"], W["glu_w2"])
    late, token = late_weights(ya[0:8, 0:LANES] + yb[0:8, 0:LANES] + yc[0:8, 0:LANES])
    W = {**W, **late}
    h1 = _merge_fwd(h, ya, yb, yc, _behind(S["mix_out_g"], token), W["w_out"])
    h2, r = _mlp_fwd(h1, S["mlp_norm_g"], W["w_ff1"], W["w_ff2"])
    h3, gp, e = _ple_fwd(h2, p, l, S["ple_norm_g"], W["w_ple_gate"], W["w_ple_proj"])
    return h3, dict(h=h, z=z, ya=ya, yb=yb, yc=yc, y=y, xr=xr, xi=xi, h1=h1, r=r, h2=h2, gp=gp, e=e), W


def _layer_bwd(g3, p, l, cs, sn, W, S, sp, A, raw, mid_bwd):
    G = {}
    g2, dwg, G["w_ple_proj"], G["ple_norm_g"] = _ple_bwd(g3, A["h2"], p, l, S["ple_norm_g"], A["gp"], A["e"], W["w_ple_gate"])
    G["w_ple_gate"] = dwg.reshape(N_DEV, -1, D_MODEL)
    g1, hn, da, G["mlp_norm_g"] = _mlp_bwd(g2, A["h1"], S["mlp_norm_g"], A["r"], W["w_ff1"], W["w_ff2"])
    G["w_ff1"] = _tn(hn, da, D_MODEL, D_FF, bm=D_MODEL, bn=FF_BLOCK, n_major=True, dtype=WIRE, name="tn_ff1")
    G["w_ff2"] = _tn(A["r"], g2, D_FF, D_MODEL, bm=D_MODEL, bn=D_MODEL, dtype=WIRE, name="tn_ff2").reshape(N_DEV, -1, D_MODEL)
    token = mid_bwd(g1, G)
    dya, dyb, dyc, dwo, G["mix_out_g"] = _merge_bwd(g1, A["ya"], A["yb"], A["yc"], _behind(S["mix_out_g"], token), W["w_out"])
    G["w_out"] = dwo.reshape(N_DEV, -1, D_MODEL)
    dzc, dw1, dw2, dcr, dci, dbbd, dlr, dli, dd = _ssm_bwd(dyc, A["z"], A["y"], A["xr"], A["xi"], sp["bbd"], sp["pwr"], sp["pwi"], sp["cbd"],
                                                          S["ssm_d"], W["glu_w1"], W["glu_w2"])
    G["glu_w1"] = dw1.astype(WIRE).reshape(N_DEV, -1, C_WIDTH)
    G["glu_w2"] = dw2.astype(WIRE).reshape(N_DEV, -1, C_WIDTH)
    dare, daim, dldt, dbtr, dbti, dcre, dcim = _ssm_param_bwd(
        raw["are"], raw["aim"], raw["ldt"], raw["are_x"], raw["aim_x"], raw["ldt_x"], raw["btr"], raw["bti"],
        dlr.reshape(C_GROUPS, C_STATE), dli.reshape(C_GROUPS, C_STATE), dbbd, dcr, dci)
    G["ssm_a_re"], G["ssm_a_im"], G["ssm_log_dt"] = dare, daim, dldt[:, 0]
    G["ssm_b_re"] = dbtr.reshape(C_GROUPS, C_GROUP, C_STATE).transpose(0, 2, 1)
    G["ssm_b_im"] = dbti.reshape(C_GROUPS, C_GROUP, C_STATE).transpose(0, 2, 1)
    G["ssm_c_re"] = dcre.reshape(C_GROUPS, C_GROUP, C_STATE)
    G["ssm_c_im"] = dcim.reshape(C_GROUPS, C_GROUP, C_STATE)
    G["ssm_d"] = dd.reshape(C_GROUPS, C_GROUP)
    dzq, dzk, dzv, dqg, dkg, dsk = _attn_bwd(A["z"], cs, sn, S["qg"], S["kg"], S["sinks"], A["yb"], dyb)
    G["q_norm_g"] = dqg[0, :HEAD_DIM] + dqg[0, HEAD_DIM:]
    G["k_norm_g"] = dkg[0, :HEAD_DIM] + dkg[0, HEAD_DIM:]
    G["sinks"] = dsk[:, 0]
    dza, dws, dbs, dlng, dlnb = _gmlp_bwd(A["z"], dya, S["lng"], S["lnb"], S["gmlp_ws"], S["bsx"])
    G["gmlp_ws"] = dws
    G["gmlp_bs"] = dbs[:, :, 0]
    G["gmlp_ln_g"] = dlng.reshape(A_HEADS, 2, HEAD_DIM)[:, 1]
    G["gmlp_ln_b"] = dlnb.reshape(A_HEADS, 2, HEAD_DIM)[:, 1]
    g0, G["w_in"], G["attn_norm_g"] = _inproj_bwd(g1, A["h"], S["attn_norm_g"], W["w_in"], dza, dzq, dzk, dzv, dzc)
    return g0, G


def _small_layouts(P, l):
    def row(a):
        return a.reshape(1, -1)

    zeros = jnp.zeros((A_HEADS, HEAD_DIM), F32)
    S = dict(
        attn_norm_g=row(P["attn_norm_g"][l]), mix_out_g=row(P["mix_out_g"][l]), mlp_norm_g=row(P["mlp_norm_g"][l]),
        ple_norm_g=row(P["ple_norm_g"][l]),
        lng=jnp.stack([zeros, P["gmlp_ln_g"][l]], axis=1).reshape(1, IN_A),
        lnb=jnp.stack([zeros, P["gmlp_ln_b"][l]], axis=1).reshape(1, IN_A),
        gmlp_ws=P["gmlp_ws"][l],
        bsx=jnp.broadcast_to(P["gmlp_bs"][l][:, :, None], (A_HEADS, CHUNK, CHUNK)),
        qg=jnp.tile(P["q_norm_g"][l], 2).reshape(1, LANES), kg=jnp.tile(P["k_norm_g"][l], 2).reshape(1, LANES),
        sinks=jnp.broadcast_to(P["sinks"][l][:, None], (8, LANES)),
        ssm_d=row(P["ssm_d"][l]),
    )
    are, aim = P["ssm_a_re"][l], P["ssm_a_im"][l]
    ldt = jnp.broadcast_to(P["ssm_log_dt"][l][:, None], (C_GROUPS, C_STATE))
    raw = dict(
        are=are, aim=aim, ldt=ldt,
        are_x=jnp.repeat(are, C_GROUP, axis=0), aim_x=jnp.repeat(aim, C_GROUP, axis=0), ldt_x=jnp.repeat(ldt, C_GROUP, axis=0),
        btr=P["ssm_b_re"][l].transpose(0, 2, 1).reshape(C_WIDTH, C_STATE), bti=P["ssm_b_im"][l].transpose(0, 2, 1).reshape(C_WIDTH, C_STATE),
        cre=P["ssm_c_re"][l].reshape(C_WIDTH, C_STATE), cim=P["ssm_c_im"][l].reshape(C_WIDTH, C_STATE),
    )
    return S, raw


def _ssm_prep_layer(raw):
    bbd, cbd, pwr, pwi = _ssm_prep(raw["are"].reshape(1, N_STATE), raw["aim"].reshape(1, N_STATE), raw["ldt"].reshape(1, N_STATE),
                                   raw["are_x"], raw["aim_x"], raw["ldt_x"], raw["btr"], raw["bti"], raw["cre"], raw["cim"])
    return dict(bbd=bbd, cbd=cbd, pwr=pwr, pwi=pwi)


def _behind(row, token):
    return row if token is None else row + token[0:1, 0:1]


def _local_step(x, p, positions, target, P, weights_of, mid_bwd, after_bwd):
    inv = 1.0 / (ROPE_THETA ** (jnp.arange(0, HEAD_DIM, 2, dtype=F32) / HEAD_DIM))
    cs, sn = _rope_tables(positions.reshape(-1, 1), jnp.tile(inv, 4).reshape(1, LANES))
    h = x
    acts, smalls, weights = [], [], []
    for l in range(DEPTH):
        W, late_weights, token = weights_of(l, h)
        S, raw = _small_layouts(P, l)
        sp = _ssm_prep_layer(raw)
        h, A, W = _layer_fwd(h, p, l, cs, sn, W, late_weights, {**S, "attn_norm_g": _behind(S["attn_norm_g"], token)}, sp)
        acts.append(A)
        smalls.append((S, raw, sp))
        weights.append(W)
    g, lsum = _loss_head(h, target)
    grads = [None] * DEPTH
    token = None
    for l in reversed(range(DEPTH)):
        S, raw, sp = smalls[l]
        g, grads[l] = _layer_bwd(g, p, l, cs, sn, weights[l], {**S, "ple_norm_g": _behind(S["ple_norm_g"], token)}, sp, acts[l], raw,
                                 functools.partial(mid_bwd, l))
        token = after_bwd(l, g, grads[l])
    return lsum[0, 0], g, grads


def _layer_weights(g):
    layout = dict(
        w_in=_from_col_major, glu_w1=lambda a: a.reshape(C_WIDTH, C_WIDTH), glu_w2=lambda a: a.reshape(C_WIDTH, C_WIDTH),
        w_out=lambda a: a.reshape(D_MODEL, D_MODEL), w_ff1=lambda a: a, w_ff2=lambda a: a.reshape(D_FF, D_MODEL),
        w_ple_gate=lambda a: a.reshape(D_MODEL, D_MODEL), w_ple_proj=_from_col_major)
    return {n: layout[n](a) for n, a in g.items()}


def kernel(x, p, positions, attn_norm_g, w_in, gmlp_ln_g, gmlp_ln_b, gmlp_ws, gmlp_bs, q_norm_g, k_norm_g, sinks, ssm_a_re, ssm_a_im, ssm_log_dt, ssm_b_re, ssm_b_im, ssm_c_re, ssm_c_im, ssm_d, glu_w1, glu_w2, mix_out_g, w_out, mlp_norm_g, w_ff1, w_ff2, ple_norm_g, w_ple_gate, w_ple_proj, loss_target, m_attn_norm_g, m_w_in, m_gmlp_ln_g, m_gmlp_ln_b, m_gmlp_ws, m_gmlp_bs, m_q_norm_g, m_k_norm_g, m_sinks, m_ssm_a_re, m_ssm_a_im, m_ssm_log_dt, m_ssm_b_re, m_ssm_b_im, m_ssm_c_re, m_ssm_c_im, m_ssm_d, m_glu_w1, m_glu_w2, m_mix_out_g, m_w_out, m_mlp_norm_g, m_w_ff1, m_w_ff2, m_ple_norm_g, m_w_ple_gate, m_w_ple_proj, v_attn_norm_g, v_w_in, v_gmlp_ln_g, v_gmlp_ln_b, v_gmlp_ws, v_gmlp_bs, v_q_norm_g, v_k_norm_g, v_sinks, v_ssm_a_re, v_ssm_a_im, v_ssm_log_dt, v_ssm_b_re, v_ssm_b_im, v_ssm_c_re, v_ssm_c_im, v_ssm_d, v_glu_w1, v_glu_w2, v_mix_out_g, v_w_out, v_mlp_norm_g, v_w_ff1, v_w_ff2, v_ple_norm_g, v_w_ple_gate, v_w_ple_proj):
    env = dict(locals())
    P = {n: env[n] for n in WEIGHTS}
    M = {n: env["m_" + n] for n in WEIGHTS}
    V = {n: env["v_" + n] for n in WEIGHTS}
    return _step(x, p, positions, loss_target, P, M, V)


def _step(x, p, positions, loss_target, P, M, V):
    small_shapes = [P[n].shape for n in SMALL]
    me = 4 * lax.axis_index("x") + 2 * lax.axis_index("y") + lax.axis_index("c")
    nothing = jnp.zeros((8, LANES), F32)

    def put(land, own, lead):
        return lax.dynamic_update_slice(land, own.reshape((1,) * len(lead) + own.shape), tuple(lead) + (0,) * own.ndim)

    def gather_start(l, names, after, tag):
        shards = [P[n][l].astype(WIRE) for n in names]
        lands = [lax.empty((N_DEV,) + s.shape, WIRE) for s in shards]
        send, recv, shards, lands, token = _split_start(shards, lands, ["all"] * len(names), 0, after, f"gather_start_{l}{tag}")
        return dict(names=names, send=send, recv=recv, shards=shards, lands=lands, token=token, name=f"gather_wait_{l}{tag}")

    def gather_wait(f, after):
        shards, lands = _split_wait(f["send"], f["recv"], f["shards"], f["lands"], ["all"] * len(f["names"]), 0, after, f["name"])
        return dict(zip(f["names"], [put(ld, sh, (me,)) for sh, ld in zip(shards, lands)]))

    first = gather_start(0, EARLY, nothing, "a")
    flying = {0: (first, gather_start(0, LATE, first["token"], "b"))}

    def weights_of(l, h):
        fa, fb = flying.pop(l)
        got = gather_wait(fa, h)
        if fb is None:
            token = None
            if l + 1 < DEPTH:
                flying[l + 1] = (gather_start(l + 1, SHARDED, got["w_in"], ""), None)
                token = flying[l + 1][0]["token"]
            W = _layer_weights(got)
            return {n: W[n] for n in EARLY}, (lambda after: ({n: W[n] for n in LATE}, None)), token

        def late_weights(after):
            late = gather_wait(fb, after)
            flying[l + 1] = (gather_start(l + 1, SHARDED, late["w_out"], ""), None)
            return _layer_weights(late), flying[l + 1][0]["token"]

        return _layer_weights(got), late_weights, fb["token"]

    grad_lands = {n: lax.empty((N_DEV,) + P[n].shape, WIRE) for n in SHARDED}
    sent = []

    def scatter_start(l, names, parts, lands, tag):
        send, recv, parts, lands, token = _split_start(parts, lands, ["own"] * len(parts), l, nothing, f"scatter_start_{l}{tag}")
        sent.append(dict(l=l, names=names, send=send, recv=recv, parts=parts, lands=lands, name=f"scatter_wait_{l}{tag}"))
        return token

    def scatter_wait(after):
        f = sent.pop(0)
        parts, lands = _split_wait(f["send"], f["recv"], f["parts"], f["lands"], ["own"] * len(f["parts"]), f["l"], after, f["name"])
        lands = [put(ld, lax.dynamic_index_in_dim(part, me, 0, keepdims=False), (me, f["l"])) for part, ld in zip(parts, lands)]
        return dict(zip(f["names"], lands))

    def mid_bwd(l, g1, G):
        if l > 0:
            return None
        grad_lands.update(scatter_wait(g1))
        return scatter_start(0, GRADS_MID, [G[n] for n in GRADS_MID], [grad_lands[n] for n in GRADS_MID], "a")

    small_land = []

    def after_bwd(l, g, G):
        if l > 0:
            if sent:
                grad_lands.update(scatter_wait(g))
            return scatter_start(l, SHARDED, [G[n] for n in SHARDED], [grad_lands[n] for n in SHARDED], "")
        sflat = _pack([jnp.stack([grads_of[k][n] for k in range(DEPTH)]) for n in SMALL], F32)
        sparts = sflat.reshape(N_DEV, -1, FLAT_COLS)
        small_land.append(sflat.shape)
        return scatter_start(0, GRADS_END + ("small",), [G[n] for n in GRADS_END] + [sparts],
                             [grad_lands[n] for n in GRADS_END] + [lax.empty((N_DEV, 1) + sparts.shape[1:], F32)], "b")

    grads_of = {}

    def after_bwd_recording(l, g, G):
        grads_of[l] = G
        grads_of["token"] = after_bwd(l, g, G)
        return grads_of["token"]

    lsum, gx, grads = _local_step(x[0], p[:, 0], positions[0], loss_target[0], P, weights_of, mid_bwd, after_bwd_recording)
    G, delta, new_m, new_v = {}, {}, {}, {}

    def update(names):
        for n in names:
            shp = P[n].shape
            res = _sum_adamw(grad_lands[n].reshape(N_DEV, -1, shp[-1]), *(a.reshape(-1, shp[-1]) for a in (P[n], M[n], V[n])))
            G[n], delta[n], new_m[n], new_v[n] = (a.reshape(shp) for a in res)

    grad_lands.update(scatter_wait(grads_of["token"]))
    update(GRADS_MID)
    last = scatter_wait(sum(new_v[n].reshape(-1, LANES)[:8] for n in GRADS_MID))
    small_parts = last.pop("small")
    grad_lands.update(last)
    mine = _sum_slots(small_parts[:, 0])
    send, recv, srcs, lands, _ = _split_start([mine], [lax.empty((N_DEV,) + mine.shape, F32)], ["all"], 0, nothing, "gather_small_start")
    update(GRADS_END)
    srcs, lands = _split_wait(send, recv, srcs, lands, ["all"], 0, sum(new_v[n].reshape(-1, LANES)[:8] for n in GRADS_END), "gather_small_wait")
    small_sum = put(lands[0], srcs[0], (me,)).reshape((1,) + small_land[0])
    res = _sum_adamw(small_sum, _pack([P[n] for n in SMALL], F32), _pack([M[n] for n in SMALL], F32), _pack([V[n] for n in SMALL], F32))
    for dst, flat in zip((G, delta, new_m, new_v), res):
        dst.update(zip(SMALL, _unpack(flat, small_shapes)))
    loss = lax.psum(lsum, ("x", "y", "c"))
    return (loss, gx[None], *[G[n] for n in WEIGHTS], *[delta[n] for n in WEIGHTS], *[new_m[n] for n in WEIGHTS], *[new_v[n] for n in WEIGHTS])
```
